```python
import math
import jax, jax.numpy as jnp
from jax import lax
import numpy as np

D_MODEL = 1024
BATCH = 8
SEQ = 4096
DEPTH = 1

HEAD_DIM = 64
N_HEADS = D_MODEL // HEAD_DIM
N_KV_HEADS = N_HEADS // 4
ATTN_WIDTH = N_HEADS * HEAD_DIM
KV_WIDTH = N_KV_HEADS * HEAD_DIM
WINDOW = 128
BLOCK = 128
ROT_DIM = HEAD_DIM // 4
ROPE_THETA = 500000.0
RNN_WIDTH = D_MODEL
RNN_BLOCK_WIDTH = 256
RNN_N_BLOCKS = RNN_WIDTH // RNN_BLOCK_WIDTH
LRU_C = 8.0
CONV_WIDTH = 4
NORM_EPS = 1e-6
SPLIT_SIZES = (ATTN_WIDTH, KV_WIDTH, KV_WIDTH, ATTN_WIDTH,
               RNN_WIDTH, RNN_WIDTH,
               D_MODEL, D_MODEL)
IN_WIDTH = sum(SPLIT_SIZES)
SPLIT_POINTS = tuple(int(v) for v in np.cumsum(SPLIT_SIZES)[:-1])

kernel_name = "hybrid_swa_sink_rglru_gated_block"


def rmsnorm(x, g):
    xf = x.astype(jnp.float32)
    r = lax.rsqrt(jnp.mean(xf * xf, axis=-1, keepdims=True) + NORM_EPS)
    return (xf * r).astype(x.dtype) * g


def partial_rope(t, pos):
    half = ROT_DIM // 2
    inv_freq = ROPE_THETA ** (-jnp.arange(0, ROT_DIM, 2, dtype=jnp.float32) / ROT_DIM)
    ang = pos[..., None].astype(jnp.float32) * inv_freq
    cos = jnp.cos(ang)[:, :, None, :]
    sin = jnp.sin(ang)[:, :, None, :]
    rot = t[..., :ROT_DIM].astype(jnp.float32)
    x1, x2 = rot[..., :half], rot[..., half:]
    rotated = jnp.concatenate([x1 * cos - x2 * sin, x2 * cos + x1 * sin], axis=-1)
    return jnp.concatenate([rotated.astype(t.dtype), t[..., ROT_DIM:]], axis=-1)


def sliding_window_attention_with_sinks(q, k, v, sinks):
    B, S, H, hd = q.shape
    nb = S // BLOCK
    G = H // N_KV_HEADS
    qb = q.reshape(B, nb, BLOCK, N_KV_HEADS, G, hd).astype(jnp.float32)
    pad = ((0, 0), (BLOCK, 0), (0, 0), (0, 0))
    kp = jnp.pad(k, pad).reshape(B, nb + 1, BLOCK, N_KV_HEADS, hd)
    vp = jnp.pad(v, pad).reshape(B, nb + 1, BLOCK, N_KV_HEADS, hd)
    kb = jnp.concatenate([kp[:, :-1], kp[:, 1:]], axis=2).astype(jnp.float32)
    vb = jnp.concatenate([vp[:, :-1], vp[:, 1:]], axis=2).astype(jnp.float32)
    s = jnp.einsum('bnqkgd,bnskd->bnkgqs', qb, kb) * (1.0 / math.sqrt(hd))
    qi = jnp.arange(BLOCK)[:, None]
    kj = jnp.arange(2 * BLOCK)[None, :]
    diff = qi + BLOCK - kj
    band = (diff >= 0) & (diff < WINDOW)
    kpos = jnp.arange(nb)[:, None] * BLOCK - BLOCK + jnp.arange(2 * BLOCK)[None, :]
    mask = band[None] & (kpos >= 0)[:, None, :]
    s = jnp.where(mask[None, :, None, None], s, -1e30)
    sink = sinks.astype(jnp.float32).reshape(N_KV_HEADS, G)[None, None, :, :, None, None]
    m = jnp.maximum(jnp.max(s, axis=-1, keepdims=True), sink)
    p = jnp.exp(s - m)
    denom = jnp.sum(p, axis=-1, keepdims=True) + jnp.exp(sink - m)
    o = jnp.einsum('bnkgqs,bnskd->bnqkgd', p / denom, vb)
    return o.reshape(B, S, H * hd).astype(q.dtype)


def causal_depthwise_conv(x, w, b):
    S = x.shape[1]
    xp = jnp.pad(x, ((0, 0), (CONV_WIDTH - 1, 0), (0, 0)))
    y = sum(xp[:, k:k + S] * w[k] for k in range(CONV_WIDTH))
    return y + b


def rg_lru(xr, pos, wa, ba, wx, bx, lam):
    B, S, D = xr.shape
    xb = xr.reshape(B, S, RNN_N_BLOCKS, RNN_BLOCK_WIDTH)
    r = jax.nn.sigmoid(jnp.einsum('bshi,hij->bshj', xb, wa).reshape(B, S, D) + ba)
    i = jax.nn.sigmoid(jnp.einsum('bshi,hij->bshj', xb, wx).reshape(B, S, D) + bx)
    log_a = -LRU_C * r.astype(jnp.float32) * jax.nn.softplus(-lam.astype(jnp.float32))
    a = jnp.exp(log_a)
    mult = jnp.sqrt(-jnp.expm1(2.0 * log_a))
    reset = (pos == 0)[..., None]
    mult = jnp.where(reset, 1.0, mult)
    a = jnp.where(reset, 0.0, a)
    b = mult * (i * xr).astype(jnp.float32)

    def combine(lhs, rhs):
        a1, b1 = lhs
        a2, b2 = rhs
        return a1 * a2, a2 * b1 + b2

    _, h = lax.associative_scan(combine, (a, b), axis=1)
    return h.astype(xr.dtype)


def _fwd_setup_inputs(seed: int = 0) -> dict:
    key = jax.random.key(seed)
    ks = jax.random.split(key, 20)
    f32 = jnp.float32
    nrm = lambda k, shape, scale: jax.random.normal(k, shape, f32) * scale
    s = jax.nn.sigmoid(jnp.zeros(()))
    del s
    a_c = jax.random.uniform(ks[12], (DEPTH, RNN_WIDTH), f32, 0.9, 0.999)
    a_base = a_c ** (1.0 / LRU_C)
    lam = jnp.log(a_base) - jnp.log1p(-a_base)
    return {
        "x": nrm(ks[0], (BATCH, SEQ, D_MODEL), 1.0),
        "c": nrm(ks[1], (BATCH, D_MODEL), 1.0),
        "positions": jnp.broadcast_to(jnp.arange(SEQ, dtype=jnp.int32), (BATCH, SEQ)),
        "w_ada": nrm(ks[2], (DEPTH, D_MODEL, 3 * D_MODEL), 0.1 * D_MODEL ** -0.5),
        "b_ada": nrm(ks[3], (DEPTH, 3 * D_MODEL), 0.01),
        "norm_g": 1.0 + nrm(ks[4], (DEPTH, D_MODEL), 0.02),
        "w_in": nrm(ks[5], (DEPTH, D_MODEL, IN_WIDTH), D_MODEL ** -0.5),
        "attn_sinks": nrm(ks[6], (DEPTH, N_HEADS), 0.5),
        "conv_w": nrm(ks[7], (DEPTH, CONV_WIDTH, RNN_WIDTH), CONV_WIDTH ** -0.5),
        "conv_b": nrm(ks[8], (DEPTH, RNN_WIDTH), 0.01),
        "rg_wa": nrm(ks[9], (DEPTH, RNN_N_BLOCKS, RNN_BLOCK_WIDTH, RNN_BLOCK_WIDTH), RNN_BLOCK_WIDTH ** -0.5),
        "rg_ba": nrm(ks[10], (DEPTH, RNN_WIDTH), 0.01),
        "rg_wx": nrm(ks[11], (DEPTH, RNN_N_BLOCKS, RNN_BLOCK_WIDTH, RNN_BLOCK_WIDTH), RNN_BLOCK_WIDTH ** -0.5),
        "rg_bx": nrm(ks[13], (DEPTH, RNN_WIDTH), 0.01),
        "rg_lambda": lam,
        "w_attn_proj": nrm(ks[14], (DEPTH, ATTN_WIDTH, D_MODEL), ATTN_WIDTH ** -0.5),
        "w_rnn_proj": nrm(ks[15], (DEPTH, RNN_WIDTH, D_MODEL), RNN_WIDTH ** -0.5),
        "w_out": nrm(ks[16], (DEPTH, D_MODEL, D_MODEL), D_MODEL ** -0.5),
        "final_g": 1.0 + nrm(ks[17], (D_MODEL,), 0.02),
    }


def _fwd_reference(x, c, positions, w_ada, b_ada, norm_g, w_in, attn_sinks, conv_w, conv_b,
              rg_wa, rg_ba, rg_wx, rg_bx, rg_lambda, w_attn_proj, w_rnn_proj, w_out, final_g):
    B, S, _ = x.shape
    for l in range(DEPTH):
        mod = c @ w_ada[l] + b_ada[l]
        shift, scale, gate = jnp.split(mod, 3, axis=-1)
        h = rmsnorm(x, norm_g[l]) * (1.0 + scale[:, None, :]) + shift[:, None, :]
        proj = h @ w_in[l]
        q, k, v, g_attn, xr, g_rnn, m_attn, m_rnn = jnp.split(proj, SPLIT_POINTS, axis=-1)
        q = partial_rope(q.reshape(B, S, N_HEADS, HEAD_DIM), positions)
        k = partial_rope(k.reshape(B, S, N_KV_HEADS, HEAD_DIM), positions)
        v = v.reshape(B, S, N_KV_HEADS, HEAD_DIM)
        y_attn = sliding_window_attention_with_sinks(q, k, v, attn_sinks[l]) * jax.nn.silu(g_attn)
        xr = causal_depthwise_conv(xr, conv_w[l], conv_b[l])
        y_rnn = rg_lru(xr, positions, rg_wa[l], rg_ba[l], rg_wx[l], rg_bx[l], rg_lambda[l]) * jax.nn.silu(g_rnn)
        merged = (jax.nn.sigmoid(m_attn) * (y_attn @ w_attn_proj[l])
                  + jax.nn.sigmoid(m_rnn) * (y_rnn @ w_rnn_proj[l]))
        x = x + gate[:, None, :] * (merged @ w_out[l])
    return rmsnorm(x, final_g)


import jax as _jax
import jax.numpy as _jnp

TWIN_FORMAT = 'train_step'
FWD_PARAMS = ['x', 'c', 'positions', 'w_ada', 'b_ada', 'norm_g', 'w_in', 'attn_sinks', 'conv_w', 'conv_b', 'rg_wa', 'rg_ba', 'rg_wx', 'rg_bx', 'rg_lambda', 'w_attn_proj', 'w_rnn_proj', 'w_out', 'final_g']
TWIN_WEIGHTS = ['w_ada', 'b_ada', 'norm_g', 'w_in', 'attn_sinks', 'conv_w', 'conv_b', 'rg_wa', 'rg_ba', 'rg_wx', 'rg_bx', 'rg_lambda', 'w_attn_proj', 'w_rnn_proj', 'w_out', 'final_g']
TWIN_DIFF_INPUT = 'x'
TWIN_INPUTS = ['x', 'c', 'positions', 'w_ada', 'b_ada', 'norm_g', 'w_in', 'attn_sinks', 'conv_w', 'conv_b', 'rg_wa', 'rg_ba', 'rg_wx', 'rg_bx', 'rg_lambda', 'w_attn_proj', 'w_rnn_proj', 'w_out', 'final_g', 'loss_target', 'm_w_ada', 'm_b_ada', 'm_norm_g', 'm_w_in', 'm_attn_sinks', 'm_conv_w', 'm_conv_b', 'm_rg_wa', 'm_rg_ba', 'm_rg_wx', 'm_rg_bx', 'm_rg_lambda', 'm_w_attn_proj', 'm_w_rnn_proj', 'm_w_out', 'm_final_g', 'v_w_ada', 'v_b_ada', 'v_norm_g', 'v_w_in', 'v_attn_sinks', 'v_conv_w', 'v_conv_b', 'v_rg_wa', 'v_rg_ba', 'v_rg_wx', 'v_rg_bx', 'v_rg_lambda', 'v_w_attn_proj', 'v_w_rnn_proj', 'v_w_out', 'v_final_g']
TWIN_OUTPUTS = ['loss', 'grad_x', 'grad_w_ada', 'grad_b_ada', 'grad_norm_g', 'grad_w_in', 'grad_attn_sinks', 'grad_conv_w', 'grad_conv_b', 'grad_rg_wa', 'grad_rg_ba', 'grad_rg_wx', 'grad_rg_bx', 'grad_rg_lambda', 'grad_w_attn_proj', 'grad_w_rnn_proj', 'grad_w_out', 'grad_final_g', 'delta_w_ada', 'delta_b_ada', 'delta_norm_g', 'delta_w_in', 'delta_attn_sinks', 'delta_conv_w', 'delta_conv_b', 'delta_rg_wa', 'delta_rg_ba', 'delta_rg_wx', 'delta_rg_bx', 'delta_rg_lambda', 'delta_w_attn_proj', 'delta_w_rnn_proj', 'delta_w_out', 'delta_final_g', 'new_m_w_ada', 'new_m_b_ada', 'new_m_norm_g', 'new_m_w_in', 'new_m_attn_sinks', 'new_m_conv_w', 'new_m_conv_b', 'new_m_rg_wa', 'new_m_rg_ba', 'new_m_rg_wx', 'new_m_rg_bx', 'new_m_rg_lambda', 'new_m_w_attn_proj', 'new_m_w_rnn_proj', 'new_m_w_out', 'new_m_final_g', 'new_v_w_ada', 'new_v_b_ada', 'new_v_norm_g', 'new_v_w_in', 'new_v_attn_sinks', 'new_v_conv_w', 'new_v_conv_b', 'new_v_rg_wa', 'new_v_rg_ba', 'new_v_rg_wx', 'new_v_rg_bx', 'new_v_rg_lambda', 'new_v_w_attn_proj', 'new_v_w_rnn_proj', 'new_v_w_out', 'new_v_final_g']
TWIN_LEAF_KINDS = {'loss': 'loss', 'grad_x': 'grad_x', 'grad_w_ada': 'grad_w', 'grad_b_ada': 'grad_w', 'grad_norm_g': 'grad_w', 'grad_w_in': 'grad_w', 'grad_attn_sinks': 'grad_w', 'grad_conv_w': 'grad_w', 'grad_conv_b': 'grad_w', 'grad_rg_wa': 'grad_w', 'grad_rg_ba': 'grad_w', 'grad_rg_wx': 'grad_w', 'grad_rg_bx': 'grad_w', 'grad_rg_lambda': 'grad_w', 'grad_w_attn_proj': 'grad_w', 'grad_w_rnn_proj': 'grad_w', 'grad_w_out': 'grad_w', 'grad_final_g': 'grad_w', 'delta_w_ada': 'delta_w', 'delta_b_ada': 'delta_w', 'delta_norm_g': 'delta_w', 'delta_w_in': 'delta_w', 'delta_attn_sinks': 'delta_w', 'delta_conv_w': 'delta_w', 'delta_conv_b': 'delta_w', 'delta_rg_wa': 'delta_w', 'delta_rg_ba': 'delta_w', 'delta_rg_wx': 'delta_w', 'delta_rg_bx': 'delta_w', 'delta_rg_lambda': 'delta_w', 'delta_w_attn_proj': 'delta_w', 'delta_w_rnn_proj': 'delta_w', 'delta_w_out': 'delta_w', 'delta_final_g': 'delta_w', 'new_m_w_ada': 'new_m', 'new_m_b_ada': 'new_m', 'new_m_norm_g': 'new_m', 'new_m_w_in': 'new_m', 'new_m_attn_sinks': 'new_m', 'new_m_conv_w': 'new_m', 'new_m_conv_b': 'new_m', 'new_m_rg_wa': 'new_m', 'new_m_rg_ba': 'new_m', 'new_m_rg_wx': 'new_m', 'new_m_rg_bx': 'new_m', 'new_m_rg_lambda': 'new_m', 'new_m_w_attn_proj': 'new_m', 'new_m_w_rnn_proj': 'new_m', 'new_m_w_out': 'new_m', 'new_m_final_g': 'new_m', 'new_v_w_ada': 'new_v', 'new_v_b_ada': 'new_v', 'new_v_norm_g': 'new_v', 'new_v_w_in': 'new_v', 'new_v_attn_sinks': 'new_v', 'new_v_conv_w': 'new_v', 'new_v_conv_b': 'new_v', 'new_v_rg_wa': 'new_v', 'new_v_rg_ba': 'new_v', 'new_v_rg_wx': 'new_v', 'new_v_rg_bx': 'new_v', 'new_v_rg_lambda': 'new_v', 'new_v_w_attn_proj': 'new_v', 'new_v_w_rnn_proj': 'new_v', 'new_v_w_out': 'new_v', 'new_v_final_g': 'new_v'}


def _forward(args):
    return _fwd_reference(*[args[k] for k in FWD_PARAMS])


def _output_shape():
    out = _jax.eval_shape(lambda: _forward(_fwd_setup_inputs(0)))
    return out.shape, out.dtype

N_MICROBATCH = 1
ADAM_LR = 0.001
ADAM_B1 = 0.9
ADAM_B2 = 0.999
ADAM_EPS = 1e-08
ADAM_WD = 0.01
ADAM_STEP = 10
PER_EXAMPLE_BATCH_AXIS = {'x': 0, 'c': 0, 'positions': 0, 'loss_target': 0}
SHARED_INPUTS = []
_WEIGHT_DTYPES = {'w_ada': _jnp.float32, 'b_ada': _jnp.float32, 'norm_g': _jnp.float32, 'w_in': _jnp.float32, 'attn_sinks': _jnp.float32, 'conv_w': _jnp.float32, 'conv_b': _jnp.float32, 'rg_wa': _jnp.float32, 'rg_ba': _jnp.float32, 'rg_wx': _jnp.float32, 'rg_bx': _jnp.float32, 'rg_lambda': _jnp.float32, 'w_attn_proj': _jnp.float32, 'w_rnn_proj': _jnp.float32, 'w_out': _jnp.float32, 'final_g': _jnp.float32}
MOMENT_SCALE = {'w_ada': 4.762791e-02, 'b_ada': 5.312507e-02, 'norm_g': 7.581715e-03, 'w_in': 3.638008e-03, 'attn_sinks': 1.090039e-03, 'conv_w': 6.207240e-03, 'conv_b': 5.133221e-02, 'rg_wa': 1.218696e-03, 'rg_ba': 1.409639e-03, 'rg_wx': 2.202445e-03, 'rg_bx': 2.757729e-03, 'rg_lambda': 3.297831e-03, 'w_attn_proj': 1.385643e-03, 'w_rnn_proj': 6.550301e-03, 'w_out': 6.425179e-03, 'final_g': 3.200269e+01}


def _to_microbatches(a, axis):
    t = _jnp.moveaxis(a, axis, 0)
    t = t.reshape((N_MICROBATCH, t.shape[0] // N_MICROBATCH) + t.shape[1:])
    return _jnp.moveaxis(t, 1, axis + 1)


def setup_inputs(seed: int = 0) -> dict:
    inp = _fwd_setup_inputs(seed)
    key = _jax.random.fold_in(_jax.random.key(seed), 7919)
    shape, _ = _output_shape()
    out = dict(inp)
    out["loss_target"] = _jax.random.normal(_jax.random.fold_in(key, 0), shape, _jnp.float32)
    for i, name in enumerate(TWIN_WEIGHTS):
        w = inp[name].astype(_jnp.float32)
        if MOMENT_SCALE is None:
            s = _jnp.sqrt(_jnp.mean(_jnp.square(w)) + 1e-30)
        else:
            s = MOMENT_SCALE[name]
        km, kv = _jax.random.split(_jax.random.fold_in(key, i + 1))
        out[name] = w
        out["m_" + name] = s * _jax.random.normal(km, w.shape, _jnp.float32)
        out["v_" + name] = (s * s) * _jax.random.uniform(kv, w.shape, _jnp.float32, 0.5, 1.5)
    if N_MICROBATCH > 1:
        for name, axis in PER_EXAMPLE_BATCH_AXIS.items():
            out[name] = _to_microbatches(out[name], axis)
    return {'x': out['x'], 'c': out['c'], 'positions': out['positions'], 'w_ada': out['w_ada'], 'b_ada': out['b_ada'], 'norm_g': out['norm_g'], 'w_in': out['w_in'], 'attn_sinks': out['attn_sinks'], 'conv_w': out['conv_w'], 'conv_b': out['conv_b'], 'rg_wa': out['rg_wa'], 'rg_ba': out['rg_ba'], 'rg_wx': out['rg_wx'], 'rg_bx': out['rg_bx'], 'rg_lambda': out['rg_lambda'], 'w_attn_proj': out['w_attn_proj'], 'w_rnn_proj': out['w_rnn_proj'], 'w_out': out['w_out'], 'final_g': out['final_g'], 'loss_target': out['loss_target'], 'm_w_ada': out['m_w_ada'], 'm_b_ada': out['m_b_ada'], 'm_norm_g': out['m_norm_g'], 'm_w_in': out['m_w_in'], 'm_attn_sinks': out['m_attn_sinks'], 'm_conv_w': out['m_conv_w'], 'm_conv_b': out['m_conv_b'], 'm_rg_wa': out['m_rg_wa'], 'm_rg_ba': out['m_rg_ba'], 'm_rg_wx': out['m_rg_wx'], 'm_rg_bx': out['m_rg_bx'], 'm_rg_lambda': out['m_rg_lambda'], 'm_w_attn_proj': out['m_w_attn_proj'], 'm_w_rnn_proj': out['m_w_rnn_proj'], 'm_w_out': out['m_w_out'], 'm_final_g': out['m_final_g'], 'v_w_ada': out['v_w_ada'], 'v_b_ada': out['v_b_ada'], 'v_norm_g': out['v_norm_g'], 'v_w_in': out['v_w_in'], 'v_attn_sinks': out['v_attn_sinks'], 'v_conv_w': out['v_conv_w'], 'v_conv_b': out['v_conv_b'], 'v_rg_wa': out['v_rg_wa'], 'v_rg_ba': out['v_rg_ba'], 'v_rg_wx': out['v_rg_wx'], 'v_rg_bx': out['v_rg_bx'], 'v_rg_lambda': out['v_rg_lambda'], 'v_w_attn_proj': out['v_w_attn_proj'], 'v_w_rnn_proj': out['v_w_rnn_proj'], 'v_w_out': out['v_w_out'], 'v_final_g': out['v_final_g']}


def _loss(weights, diff, rest, loss_target):
    with _jax.named_scope("forward"):
        args = {**rest, TWIN_DIFF_INPUT: diff, **{k: w.astype(_WEIGHT_DTYPES[k]) for k, w in weights.items()}}
        y = _forward(args)
    with _jax.named_scope("loss_head"):
        err = _jnp.square(y.astype(_jnp.float32) - loss_target)
        return 0.5 * _jnp.sum(_jnp.mean(err, axis=-1)) if err.ndim else 0.5 * err


def _adamw(w, g, m, v):
    m = ADAM_B1 * m + (1.0 - ADAM_B1) * g
    v = ADAM_B2 * v + (1.0 - ADAM_B2) * _jnp.square(g)
    m_hat = m / (1.0 - ADAM_B1 ** ADAM_STEP)
    v_hat = v / (1.0 - ADAM_B2 ** ADAM_STEP)
    delta = -ADAM_LR * (m_hat / (_jnp.sqrt(v_hat) + ADAM_EPS) + ADAM_WD * w)
    return delta, m, v


def reference(x, c, positions, w_ada, b_ada, norm_g, w_in, attn_sinks, conv_w, conv_b, rg_wa, rg_ba, rg_wx, rg_bx, rg_lambda, w_attn_proj, w_rnn_proj, w_out, final_g, loss_target, m_w_ada, m_b_ada, m_norm_g, m_w_in, m_attn_sinks, m_conv_w, m_conv_b, m_rg_wa, m_rg_ba, m_rg_wx, m_rg_bx, m_rg_lambda, m_w_attn_proj, m_w_rnn_proj, m_w_out, m_final_g, v_w_ada, v_b_ada, v_norm_g, v_w_in, v_attn_sinks, v_conv_w, v_conv_b, v_rg_wa, v_rg_ba, v_rg_wx, v_rg_bx, v_rg_lambda, v_w_attn_proj, v_w_rnn_proj, v_w_out, v_final_g):
    given = dict(x=x, c=c, positions=positions, w_ada=w_ada, b_ada=b_ada, norm_g=norm_g, w_in=w_in, attn_sinks=attn_sinks, conv_w=conv_w, conv_b=conv_b, rg_wa=rg_wa, rg_ba=rg_ba, rg_wx=rg_wx, rg_bx=rg_bx, rg_lambda=rg_lambda, w_attn_proj=w_attn_proj, w_rnn_proj=w_rnn_proj, w_out=w_out, final_g=final_g, loss_target=loss_target, m_w_ada=m_w_ada, m_b_ada=m_b_ada, m_norm_g=m_norm_g, m_w_in=m_w_in, m_attn_sinks=m_attn_sinks, m_conv_w=m_conv_w, m_conv_b=m_conv_b, m_rg_wa=m_rg_wa, m_rg_ba=m_rg_ba, m_rg_wx=m_rg_wx, m_rg_bx=m_rg_bx, m_rg_lambda=m_rg_lambda, m_w_attn_proj=m_w_attn_proj, m_w_rnn_proj=m_w_rnn_proj, m_w_out=m_w_out, m_final_g=m_final_g, v_w_ada=v_w_ada, v_b_ada=v_b_ada, v_norm_g=v_norm_g, v_w_in=v_w_in, v_attn_sinks=v_attn_sinks, v_conv_w=v_conv_w, v_conv_b=v_conv_b, v_rg_wa=v_rg_wa, v_rg_ba=v_rg_ba, v_rg_wx=v_rg_wx, v_rg_bx=v_rg_bx, v_rg_lambda=v_rg_lambda, v_w_attn_proj=v_w_attn_proj, v_w_rnn_proj=v_w_rnn_proj, v_w_out=v_w_out, v_final_g=v_final_g)
    weights = {n: given[n] for n in TWIN_WEIGHTS}
    shared = {n: given[n] for n in SHARED_INPUTS}
    per_example = {n: given[n] for n in ['x', 'c', 'positions']}
    grad_fn = _jax.value_and_grad(_loss, argnums=(0, 1))

    def one_microbatch(ex, loss_target):
        ex = dict(ex)
        diff = ex.pop(TWIN_DIFF_INPUT)
        return grad_fn(weights, diff, {**shared, **ex}, loss_target)

    if N_MICROBATCH == 1:
        loss, (grad_w, grad_x) = one_microbatch(per_example, given["loss_target"])
    else:
        def body(carry, xs):
            loss_sum, grad_sum = carry
            l_k, (gw_k, gx_k) = one_microbatch(xs[0], xs[1])
            with _jax.named_scope("update"):
                return (loss_sum + l_k, _jax.tree.map(_jnp.add, grad_sum, gw_k)), gx_k

        init = (_jnp.zeros((), _jnp.float32), _jax.tree.map(_jnp.zeros_like, weights))
        (loss, grad_w), grad_x = _jax.lax.scan(body, init, (per_example, given["loss_target"]))
    with _jax.named_scope("update"):
        delta_w, new_m, new_v = {}, {}, {}
        for n in TWIN_WEIGHTS:
            delta_w[n], new_m[n], new_v[n] = _adamw(weights[n], grad_w[n], given["m_" + n], given["v_" + n])
    return (loss, grad_x, *[grad_w[n] for n in TWIN_WEIGHTS], *[delta_w[n] for n in TWIN_WEIGHTS],
            *[new_m[n] for n in TWIN_WEIGHTS], *[new_v[n] for n in TWIN_WEIGHTS])
```

```python
import numpy as np
import jax
import jax.numpy as jnp
from jax import lax
from jax.experimental import pallas as pl
from jax.experimental.pallas import tpu as pltpu

F32 = jnp.float32
BF16 = jnp.bfloat16

D_MODEL = 1024
N_HEADS = 16
N_KV = 4
HEAD_DIM = 64
GROUP = N_HEADS // N_KV
BLOCK = 128
KV_W = N_KV * HEAD_DIM
ROT_HALF = 8
ROPE_THETA = 500000.0
RNN_BLOCKS = 4
RNN_BW = 256
CONV_W = 4
LRU_C = 8.0
NORM_EPS = 1e-6
IN_W = 6656
CB = 512
N_CB = IN_W // CB
CB_Q, CB_KV, CB_GA, CB_XR, CB_GR, CB_MA, CB_MR = 0, 2, 3, 5, 7, 9, 11
N_CHIPS = 4
N_DEV = 8
SHARD_IN = IN_W // N_CHIPS
SHARD_ROWS = D_MODEL // N_CHIPS
SHARD_RG = RNN_BW // N_CHIPS
ADA_W = 3 * D_MODEL
SHARD_ADA = ADA_W // N_CHIPS
SMALL_ROWS = 16

ADAM_LR = 0.001
ADAM_B1 = 0.9
ADAM_B2 = 0.999
ADAM_EPS = 1e-08
ADAM_WD = 0.01
ADAM_STEP = 10

VMEM_LIMIT_V7X = 52 * 1024 * 1024
MESH = pl.DeviceIdType.MESH
ANY = pl.BlockSpec(memory_space=pl.ANY)
VMEM_SPEC = pl.BlockSpec(memory_space=pltpu.VMEM)


def _cp(*sem):
    return pltpu.CompilerParams(dimension_semantics=sem if sem else None, vmem_limit_bytes=VMEM_LIMIT_V7X)


def _dot(a, b):
    return jnp.dot(a, b, preferred_element_type=F32)


def _dot_nt(a, b):
    return lax.dot_general(a, b, (((1,), (1,)), ((), ())), preferred_element_type=F32)


def _dot_tn(a, b):
    return lax.dot_general(a, b, (((0,), (0,)), ((), ())), preferred_element_type=F32)


def _sigmoid(z):
    return 1.0 / (1.0 + jnp.exp(-z))


def _neg_expm1(z):
    series = -(z * (1.0 + z * (0.5 + z * (1.0 / 6.0 + z * (1.0 / 24.0 + z * (1.0 / 120.0))))))
    return jnp.where(z > -0.05, series, 1.0 - jnp.exp(z))


def _softplus(z):
    u = jnp.exp(-jnp.abs(z))
    log1p_u = jnp.where(u < 1e-3, u * (1.0 - u * (0.5 - u * (1.0 / 3.0))), jnp.log(1.0 + u))
    return jnp.maximum(z, 0.0) + log1p_u


def _rms(xf):
    return lax.rsqrt(jnp.mean(xf * xf, axis=-1, keepdims=True) + NORM_EPS)


def _me():
    return lax.axis_index("x"), lax.axis_index("y"), lax.axis_index("c")


def _peer(mask):
    x, y, c = _me()
    fx, fy, fc = (mask >> 2) & 1, (mask >> 1) & 1, mask & 1
    return (x ^ fx if fx else x, y ^ fy if fy else y, c ^ fc if fc else c)


def _chip_of(pos):
    return pos[0] * 2 + pos[1]


CHIP_MASKS = (4, 2, 6)
ALL_MASKS = (1, 2, 3, 4, 5, 6, 7)


def _gather_weights(c_row, w_ada_s, b_w_in, b_wap, b_wrp, b_wo, b_rwa, b_rwx, conv_w_s):
    def body(c_ref, wada_ref, win_s, wap_s, wrp_s, wo_s, rwa_s, rwx_s, cw_s,
             win_f, wap_f, wrp_f, wo_f, rwa_f, rwx_f, cw_f, call_ref, mod_ref,
             wsend, wrecv, lsem, csend, crecv, msend, mrecv):
        me = _me()
        my_chip = _chip_of(me)
        my_dev = my_chip * 2 + me[2]
        srcs = (win_s, wap_s, wrp_s, wo_s, rwa_s, rwx_s, cw_s)
        fulls = (win_f, wap_f, wrp_f, wo_f, rwa_f, rwx_f, cw_f)

        def slot(full, idx, chip):
            if idx == 0:
                return full.at[:, pl.ds(pl.multiple_of(chip * SHARD_IN, 128), SHARD_IN)]
            if idx in (1, 2, 3):
                return full.at[pl.ds(pl.multiple_of(chip * SHARD_ROWS, SHARD_ROWS), SHARD_ROWS), :]
            if idx in (4, 5):
                return full.at[:, pl.ds(pl.multiple_of(chip * SHARD_RG, SHARD_RG), SHARD_RG), :]
            return full.at[chip]

        def wcopy(idx, k, to):
            return pltpu.make_async_remote_copy(
                src_ref=srcs[idx], dst_ref=slot(fulls[idx], idx, my_chip),
                send_sem=wsend.at[idx, k], recv_sem=wrecv.at[idx, k], device_id=to, device_id_type=MESH)

        def wrecv_wait(idx, k, frm):
            pltpu.make_async_remote_copy(
                src_ref=srcs[idx], dst_ref=slot(fulls[idx], idx, _chip_of(frm)),
                send_sem=wsend.at[idx, k], recv_sem=wrecv.at[idx, k], device_id=frm, device_id_type=MESH).wait_recv()

        sends = []
        for idx in range(7):
            for k, mask in enumerate(CHIP_MASKS):
                cp = wcopy(idx, k, _peer(mask))
                cp.start()
                sends.append(cp)
        local = [pltpu.make_async_copy(srcs[idx], slot(fulls[idx], idx, my_chip), lsem.at[idx]) for idx in range(7)]
        for cp in local:
            cp.start()

        call_ref[my_dev] = c_ref[0]
        csends = []
        for k, mask in enumerate(ALL_MASKS):
            cp = pltpu.make_async_remote_copy(
                src_ref=c_ref.at[0], dst_ref=call_ref.at[my_dev],
                send_sem=csend.at[k], recv_sem=crecv.at[k], device_id=_peer(mask), device_id_type=MESH)
            cp.start()
            csends.append(cp)
        for k, mask in enumerate(ALL_MASKS):
            frm = _peer(mask)
            pltpu.make_async_remote_copy(
                src_ref=c_ref.at[0], dst_ref=call_ref.at[_chip_of(frm) * 2 + frm[2]],
                send_sem=csend.at[k], recv_sem=crecv.at[k], device_id=frm, device_id_type=MESH).wait_recv()
        for cp in csends:
            cp.wait_send()

        c_all = call_ref[...].reshape(N_DEV, D_MODEL).astype(BF16)
        mod_ref[my_chip] = _dot(c_all, wada_ref[...].astype(BF16))
        msends = []
        for k, mask in enumerate(CHIP_MASKS):
            cp = pltpu.make_async_remote_copy(
                src_ref=mod_ref.at[my_chip], dst_ref=mod_ref.at[my_chip],
                send_sem=msend.at[k], recv_sem=mrecv.at[k], device_id=_peer(mask), device_id_type=MESH)
            cp.start()
            msends.append(cp)
        for k, mask in enumerate(CHIP_MASKS):
            frm = _peer(mask)
            pltpu.make_async_remote_copy(
                src_ref=mod_ref.at[my_chip], dst_ref=mod_ref.at[_chip_of(frm)],
                send_sem=msend.at[k], recv_sem=mrecv.at[k], device_id=frm, device_id_type=MESH).wait_recv()
        for cp in msends:
            cp.wait_send()

        for idx in range(7):
            for k, mask in enumerate(CHIP_MASKS):
                wrecv_wait(idx, k, _peer(mask))
        for cp in sends:
            cp.wait_send()
        for cp in local:
            cp.wait()

    out_shape = (
        jax.ShapeDtypeStruct((D_MODEL, IN_W), BF16),
        jax.ShapeDtypeStruct((D_MODEL, D_MODEL), BF16),
        jax.ShapeDtypeStruct((D_MODEL, D_MODEL), BF16),
        jax.ShapeDtypeStruct((D_MODEL, D_MODEL), BF16),
        jax.ShapeDtypeStruct((RNN_BLOCKS, RNN_BW, RNN_BW), BF16),
        jax.ShapeDtypeStruct((RNN_BLOCKS, RNN_BW, RNN_BW), BF16),
        jax.ShapeDtypeStruct((N_CHIPS, CONV_W, D_MODEL // N_CHIPS), F32),
        jax.ShapeDtypeStruct((N_DEV, 1, D_MODEL), F32),
        jax.ShapeDtypeStruct((N_CHIPS, N_DEV, SHARD_ADA), F32),
    )
    return pl.pallas_call(
        body, out_shape=out_shape, name="gather_weights",
        in_specs=[VMEM_SPEC, VMEM_SPEC] + [ANY] * 7,
        out_specs=tuple([ANY] * 7 + [VMEM_SPEC, VMEM_SPEC]),
        scratch_shapes=[
            pltpu.SemaphoreType.DMA((7, 3)), pltpu.SemaphoreType.DMA((7, 3)), pltpu.SemaphoreType.DMA((7,)),
            pltpu.SemaphoreType.DMA((7,)), pltpu.SemaphoreType.DMA((7,)),
            pltpu.SemaphoreType.DMA((3,)), pltpu.SemaphoreType.DMA((3,)),
        ],
        compiler_params=pltpu.CompilerParams(vmem_limit_bytes=VMEM_LIMIT_V7X),
    )(c_row, w_ada_s, b_w_in, b_wap, b_wrp, b_wo, b_rwa, b_rwx, conv_w_s)


def _exchange_partials(g_in, g_ap, g_rp, g_o, g_wa, g_wx, small):
    def body(gin, gap, grp_, go, gwa, gwx, small_ref, lin, lap, lrp, lo, lwa, lwx, small_all,
             gsend, grecv, lsem, ssend, srecv):
        me = _me()
        my_chip = _chip_of(me)
        my_dev = my_chip * 2 + me[2]
        srcs = (gin, gap, grp_, go, gwa, gwx)
        lands = (lin, lap, lrp, lo, lwa, lwx)

        def shard(idx, chip):
            if idx == 0:
                return srcs[idx].at[:, pl.ds(pl.multiple_of(chip * SHARD_IN, 128), SHARD_IN)]
            if idx in (1, 2, 3):
                return srcs[idx].at[pl.ds(pl.multiple_of(chip * SHARD_ROWS, SHARD_ROWS), SHARD_ROWS), :]
            return srcs[idx].at[:, pl.ds(pl.multiple_of(chip * SHARD_RG, SHARD_RG), SHARD_RG), :]

        sends = []
        for idx in range(6):
            for k, mask in enumerate(CHIP_MASKS):
                to = _peer(mask)
                cp = pltpu.make_async_remote_copy(
                    src_ref=shard(idx, _chip_of(to)), dst_ref=lands[idx].at[k],
                    send_sem=gsend.at[idx, k], recv_sem=grecv.at[idx, k], device_id=to, device_id_type=MESH)
                cp.start()
                sends.append(cp)
        local = [pltpu.make_async_copy(shard(idx, my_chip), lands[idx].at[3], lsem.at[idx]) for idx in range(6)]
        for cp in local:
            cp.start()

        small_all[my_dev] = small_ref[...]
        ssends = []
        for k, mask in enumerate(ALL_MASKS):
            cp = pltpu.make_async_remote_copy(
                src_ref=small_ref, dst_ref=small_all.at[my_dev],
                send_sem=ssend.at[k], recv_sem=srecv.at[k], device_id=_peer(mask), device_id_type=MESH)
            cp.start()
            ssends.append(cp)
        for k, mask in enumerate(ALL_MASKS):
            frm = _peer(mask)
            pltpu.make_async_remote_copy(
                src_ref=small_ref, dst_ref=small_all.at[_chip_of(frm) * 2 + frm[2]],
                send_sem=ssend.at[k], recv_sem=srecv.at[k], device_id=frm, device_id_type=MESH).wait_recv()
        for cp in ssends:
            cp.wait_send()

        for idx in range(6):
            for k, mask in enumerate(CHIP_MASKS):
                frm = _peer(mask)
                pltpu.make_async_remote_copy(
                    src_ref=shard(idx, my_chip), dst_ref=lands[idx].at[k],
                    send_sem=gsend.at[idx, k], recv_sem=grecv.at[idx, k], device_id=frm, device_id_type=MESH).wait_recv()
        for cp in sends:
            cp.wait_send()
        for cp in local:
            cp.wait()

    out_shape = (
        jax.ShapeDtypeStruct((4, D_MODEL, SHARD_IN), F32),
        jax.ShapeDtypeStruct((4, SHARD_ROWS, D_MODEL), F32),
        jax.ShapeDtypeStruct((4, SHARD_ROWS, D_MODEL), F32),
        jax.ShapeDtypeStruct((4, SHARD_ROWS, D_MODEL), F32),
        jax.ShapeDtypeStruct((4, RNN_BLOCKS, SHARD_RG, RNN_BW), F32),
        jax.ShapeDtypeStruct((4, RNN_BLOCKS, SHARD_RG, RNN_BW), F32),
        jax.ShapeDtypeStruct((N_DEV, SMALL_ROWS, D_MODEL), F32),
    )
    return pl.pallas_call(
        body, out_shape=out_shape, name="exchange_partials",
        in_specs=[ANY] * 6 + [VMEM_SPEC],
        out_specs=tuple([ANY] * 6 + [VMEM_SPEC]),
        scratch_shapes=[
            pltpu.SemaphoreType.DMA((6, 3)), pltpu.SemaphoreType.DMA((6, 3)), pltpu.SemaphoreType.DMA((6,)),
            pltpu.SemaphoreType.DMA((7,)), pltpu.SemaphoreType.DMA((7,)),
        ],
        compiler_params=pltpu.CompilerParams(vmem_limit_bytes=VMEM_LIMIT_V7X),
    )(g_in, g_ap, g_rp, g_o, g_wa, g_wx, small)


def _swap_with_sibling(parts):
    n = len(parts)

    def body(*refs):
        ins, outs, ssem, rsem = refs[:n], refs[n:2 * n], refs[2 * n], refs[2 * n + 1]
        sib = _peer(1)
        cps = [pltpu.make_async_remote_copy(src_ref=ins[i], dst_ref=outs[i], send_sem=ssem.at[i], recv_sem=rsem.at[i],
                                            device_id=sib, device_id_type=MESH) for i in range(n)]
        for cp in cps:
            cp.start()
        for cp in cps:
            cp.wait()

    return pl.pallas_call(
        body, out_shape=tuple(jax.ShapeDtypeStruct(p.shape, p.dtype) for p in parts), name="swap_with_sibling",
        in_specs=[ANY] * n, out_specs=tuple([ANY] * n),
        scratch_shapes=[pltpu.SemaphoreType.DMA((n,)), pltpu.SemaphoreType.DMA((n,))],
    )(*parts)


def _rope_tables(pos_col):
    T = pos_col.shape[0]
    tm = min(T, 512)
    inv = np.float32(ROPE_THETA) ** (-(np.arange(0, 2 * ROT_HALF, 2, dtype=np.float32)) / np.float32(2 * ROT_HALF))
    lane = np.arange(128) % HEAD_DIM
    freq = np.where(lane < 2 * ROT_HALF, inv[lane % ROT_HALF], 0.0).astype(np.float32)[None, :]

    def body(pos_ref, f_ref, c_ref, sa_ref, sb_ref):
        ang = pos_ref[...].astype(F32) * f_ref[...]
        c, s = jnp.cos(ang), jnp.sin(ang)
        m = lax.broadcasted_iota(jnp.int32, ang.shape, 1) & (HEAD_DIM - 1)
        c_ref[...] = jnp.where(m < 2 * ROT_HALF, c, 1.0)
        sa_ref[...] = jnp.where(m < ROT_HALF, -s, 0.0)
        sb_ref[...] = jnp.where((m >= ROT_HALF) & (m < 2 * ROT_HALF), s, 0.0)

    tab = jax.ShapeDtypeStruct((T, 128), F32)
    return pl.pallas_call(
        body, out_shape=(tab, tab, tab), grid=(T // tm,), name="rope_tables",
        in_specs=[pl.BlockSpec((tm, 1), lambda i: (i, 0)), pl.BlockSpec((1, 128), lambda i: (0, 0))],
        out_specs=tuple(pl.BlockSpec((tm, 128), lambda i: (i, 0)) for _ in range(3)),
        compiler_params=_cp("parallel"),
    )(pos_col, jnp.asarray(freq))


def _wide(tab, width):
    return jnp.concatenate([tab] * (width // 128), axis=1)


def _rope(t, c, sa, sb):
    w = t.shape[-1]
    return t * c + pltpu.roll(t, w - ROT_HALF, 1) * sa + pltpu.roll(t, ROT_HALF, 1) * sb


def _unrope(d, c, sa, sb):
    w = d.shape[-1]
    return d * c + pltpu.roll(d * sa, ROT_HALF, 1) + pltpu.roll(d * sb, w - ROT_HALF, 1)


def _prenorm(x, mod_row, norm_g):
    T = x.shape[0]
    tm = min(T, 512)

    def body(x_ref, mod_ref, g_ref, h_ref):
        xf = x_ref[...]
        shift, scale = mod_ref[:, 0:D_MODEL], mod_ref[:, D_MODEL:2 * D_MODEL]
        h = (xf * _rms(xf)) * g_ref[...] * (1.0 + scale) + shift
        h_ref[...] = h.astype(BF16)

    return pl.pallas_call(
        body, out_shape=jax.ShapeDtypeStruct((T, D_MODEL), BF16), grid=(T // tm,), name="prenorm",
        in_specs=[pl.BlockSpec((tm, D_MODEL), lambda i: (i, 0)), pl.BlockSpec((1, ADA_W), lambda i: (0, 0)),
                  pl.BlockSpec((1, D_MODEL), lambda i: (0, 0))],
        out_specs=pl.BlockSpec((tm, D_MODEL), lambda i: (i, 0)),
        compiler_params=_cp("parallel"),
    )(x, mod_row, norm_g)


def _in_projection(h, w_in):
    T = h.shape[0]
    tm, tn = min(T, 512), SHARD_IN

    def body(h_ref, w_ref, o_ref):
        o_ref[...] = _dot(h_ref[...], w_ref[...])

    return pl.pallas_call(
        body, out_shape=jax.ShapeDtypeStruct((T, IN_W), F32), grid=(IN_W // tn, T // tm), name="in_projection",
        in_specs=[pl.BlockSpec((tm, D_MODEL), lambda j, i: (i, 0)), pl.BlockSpec((D_MODEL, tn), lambda j, i: (0, j))],
        out_specs=pl.BlockSpec((tm, tn), lambda j, i: (i, j)),
        compiler_params=_cp("parallel", "parallel"),
    )(h, w_in)


def _attn_mask(n):
    qi = lax.broadcasted_iota(jnp.int32, (GROUP * BLOCK, 2 * BLOCK), 0) & (BLOCK - 1)
    kj = lax.broadcasted_iota(jnp.int32, (GROUP * BLOCK, 2 * BLOCK), 1)
    diff = qi + BLOCK - kj
    return (diff >= 0) & (diff < BLOCK) & ((kj >= BLOCK) | (n > 0))


def _sink_col(sink_ref, kh):
    rowg = lax.broadcasted_iota(jnp.int32, (GROUP * BLOCK, 1), 0) // BLOCK
    col = jnp.full((GROUP * BLOCK, 1), sink_ref[0, GROUP * kh], F32)
    for g in range(1, GROUP):
        col = jnp.where(rowg == g, sink_ref[0, GROUP * kh + g], col)
    return col


def _attn_probs(qr, kr_prev, kr_cur, v_prev, v_cur, kh, sink_col, mask):
    heads = [qr[:, HEAD_DIM * (GROUP * kh + g): HEAD_DIM * (GROUP * kh + g + 1)] for g in range(GROUP)]
    qs = jnp.concatenate(heads, axis=0).astype(BF16)
    lo, hi = HEAD_DIM * kh, HEAD_DIM * (kh + 1)
    kk = jnp.concatenate([kr_prev[:, lo:hi], kr_cur[:, lo:hi]], axis=0).astype(BF16)
    vv = jnp.concatenate([v_prev[:, lo:hi], v_cur[:, lo:hi]], axis=0).astype(BF16)
    s = _dot_nt(qs, kk) * (1.0 / 8.0)
    s = jnp.where(mask, s, -1e30)
    m = jnp.maximum(jnp.max(s, axis=-1, keepdims=True), sink_col)
    p = jnp.exp(s - m)
    p_sink = jnp.exp(sink_col - m)
    denom = jnp.sum(p, axis=-1, keepdims=True) + p_sink
    return qs, kk, vv, p / denom, p_sink / denom


def _unstack_heads(parts):
    cols = []
    for kh in range(N_KV):
        for g in range(GROUP):
            cols.append(parts[kh][g * BLOCK:(g + 1) * BLOCK, :])
    return jnp.concatenate(cols, axis=1)


def _attn_forward(proj, tabs, sinks):
    T = proj.shape[0]
    nb = T // BLOCK

    def body(q_ref, kvc_ref, kvp_ref, g0_ref, g1_ref, cc, sac, sbc, cp_, sap, sbp, sink_ref, y_ref):
        n = pl.program_id(0)
        tc = (_wide(cc[...], D_MODEL), _wide(sac[...], D_MODEL), _wide(sbc[...], D_MODEL))
        tcur = tuple(t[:, :KV_W] for t in tc)
        tprev = (_wide(cp_[...], KV_W), _wide(sap[...], KV_W), _wide(sbp[...], KV_W))
        qr = _rope(q_ref[...], *tc)
        kr_cur = _rope(kvc_ref[:, 0:KV_W], *tcur)
        kr_prev = _rope(kvp_ref[:, 0:KV_W], *tprev)
        v_cur, v_prev = kvc_ref[:, KV_W:2 * KV_W], kvp_ref[:, KV_W:2 * KV_W]
        mask = _attn_mask(n)
        outs = []
        for kh in range(N_KV):
            _, _, vv, pn, _ = _attn_probs(qr, kr_prev, kr_cur, v_prev, v_cur, kh, _sink_col(sink_ref, kh), mask)
            outs.append(_dot(pn.astype(BF16), vv))
        o = _unstack_heads(outs)
        g = jnp.concatenate([g0_ref[...], g1_ref[...]], axis=1)
        y_ref[...] = (o * (g * _sigmoid(g))).astype(BF16)

    def blk(w, cb):
        return pl.BlockSpec((BLOCK, w), lambda n, cb=cb: (n, cb))

    prev = lambda w, cb: pl.BlockSpec((BLOCK, w), lambda n, cb=cb: (jnp.maximum(n - 1, 0), cb))
    return pl.pallas_call(
        body, out_shape=jax.ShapeDtypeStruct((T, D_MODEL), BF16), grid=(nb,), name="attn_forward",
        in_specs=[blk(D_MODEL, 0), blk(CB, CB_KV), prev(CB, CB_KV), blk(CB, CB_GA), blk(CB, CB_GA + 1),
                  blk(128, 0), blk(128, 0), blk(128, 0), prev(128, 0), prev(128, 0), prev(128, 0),
                  pl.BlockSpec(memory_space=pltpu.SMEM)],
        out_specs=pl.BlockSpec((BLOCK, D_MODEL), lambda n: (n, 0)),
        compiler_params=_cp("parallel"),
    )(proj, proj, proj, proj, proj, *tabs, *tabs, sinks)


def _scan_rows8():
    return lax.broadcasted_iota(jnp.int32, (8, D_MODEL), 0)


def _scan_forward(a_ref, b_ref, h_ref, carry, rows):
    row = _scan_rows8()

    def group(i, carry):
        off = pl.multiple_of(i * 8, 8)
        a, b = a_ref[pl.ds(off, 8), :], b_ref[pl.ds(off, 8), :]
        for d in (1, 2, 4):
            ok = row >= d
            b = jnp.where(ok, a * pltpu.roll(b, d, 0) + b, b)
            a = jnp.where(ok, a * pltpu.roll(a, d, 0), a)
        h = a * carry + b
        h_ref[pl.ds(off, 8), :] = h
        return h[7:8, :]

    return lax.fori_loop(0, rows // 8, group, carry)


def _scan_backward(a_ref, g_ref, lam_ref, carry, rows):
    row = _scan_rows8()

    def group(i, carry):
        off = pl.multiple_of((rows // 8 - 1 - i) * 8, 8)
        a, g = a_ref[pl.ds(off, 8), :], g_ref[pl.ds(off, 8), :]
        b = a * g
        for d in (1, 2, 4):
            ok = row < 8 - d
            b = jnp.where(ok, a * pltpu.roll(b, 8 - d, 0) + b, b)
            a = jnp.where(ok, a * pltpu.roll(a, 8 - d, 0), a)
        mu = a * carry + b
        mu_below = jnp.where(row == 7, carry, pltpu.roll(mu, 7, 0))
        lam_ref[pl.ds(off, 8), :] = g + mu_below
        return mu[0:1, :]

    return lax.fori_loop(0, rows // 8, group, carry)


def _rnn_recompute(xbuf, xr, tail, cw, cb, wa_ref, wx_ref, ba, bx, sp, reset):
    rows = xr.shape[0]
    xbuf[0:8, :] = tail
    xbuf[8:rows + 8, :] = xr
    xs = [xbuf[pl.ds(8 - (CONV_W - 1 - k), rows), :] for k in range(CONV_W - 1)] + [xr]
    xc = xs[0] * cw[0:1, :]
    for k in range(1, CONV_W):
        xc = xc + xs[k] * cw[k:k + 1, :]
    xc = xc + cb
    xcb = xc.astype(BF16)
    za = jnp.concatenate([_dot(xcb[:, RNN_BW * j:RNN_BW * (j + 1)], wa_ref[j]) for j in range(RNN_BLOCKS)], axis=1) + ba
    zx = jnp.concatenate([_dot(xcb[:, RNN_BW * j:RNN_BW * (j + 1)], wx_ref[j]) for j in range(RNN_BLOCKS)], axis=1) + bx
    r, i = _sigmoid(za), _sigmoid(zx)
    log_a = -LRU_C * r * sp
    a_raw = jnp.exp(log_a)
    mult_raw = jnp.sqrt(_neg_expm1(2.0 * log_a))
    a = jnp.where(reset, 0.0, a_raw)
    mult = jnp.where(reset, 1.0, mult_raw)
    return xs, xc, xcb, r, i, a_raw, mult_raw, a, mult


def _rnn_forward(proj, pos_col, conv_w, conv_b, rwa, rwx, ba, bx, lam):
    T = proj.shape[0]
    tr = min(T, 256)

    def body(x0, x1, g0, g1, pos_ref, cw_ref, cb_ref, wa_ref, wx_ref, ba_ref, bx_ref, lam_ref,
             y_ref, h_ref, xbuf, abuf, bbuf, tail, carry):
        t = pl.program_id(0)

        @pl.when(t == 0)
        def _():
            tail[...] = jnp.zeros_like(tail)
            carry[...] = jnp.zeros_like(carry)

        xr = jnp.concatenate([x0[...], x1[...]], axis=1)
        sp = _softplus(-lam_ref[...])
        reset = pos_ref[...] == 0
        _, xc, _, _, i, _, _, a, mult = _rnn_recompute(
            xbuf, xr, tail[...], cw_ref[...], cb_ref[...], wa_ref, wx_ref, ba_ref[...], bx_ref[...], sp, reset)
        abuf[...] = a
        bbuf[...] = mult * (i * xc)
        last = _scan_forward(abuf, bbuf, h_ref, carry[0:1, :], tr)
        carry[...] = jnp.broadcast_to(last, carry.shape)
        tail[...] = xr[tr - 8:tr, :]
        g = jnp.concatenate([g0[...], g1[...]], axis=1)
        y_ref[...] = (h_ref[...] * (g * _sigmoid(g))).astype(BF16)

    blk = lambda cb: pl.BlockSpec((tr, CB), lambda t, cb=cb: (t, cb))
    row = lambda w: pl.BlockSpec((1, w), lambda t: (0, 0))
    full3 = pl.BlockSpec((RNN_BLOCKS, RNN_BW, RNN_BW), lambda t: (0, 0, 0))
    return pl.pallas_call(
        body, out_shape=(jax.ShapeDtypeStruct((T, D_MODEL), BF16), jax.ShapeDtypeStruct((T, D_MODEL), F32)),
        grid=(T // tr,), name="rnn_forward",
        in_specs=[blk(CB_XR), blk(CB_XR + 1), blk(CB_GR), blk(CB_GR + 1), pl.BlockSpec((tr, 1), lambda t: (t, 0)),
                  pl.BlockSpec((CONV_W, D_MODEL), lambda t: (0, 0)), row(D_MODEL), full3, full3,
                  row(D_MODEL), row(D_MODEL), row(D_MODEL)],
        out_specs=(pl.BlockSpec((tr, D_MODEL), lambda t: (t, 0)), pl.BlockSpec((tr, D_MODEL), lambda t: (t, 0))),
        scratch_shapes=[pltpu.VMEM((tr + 8, D_MODEL), F32), pltpu.VMEM((tr, D_MODEL), F32), pltpu.VMEM((tr, D_MODEL), F32),
                        pltpu.VMEM((8, D_MODEL), F32), pltpu.VMEM((8, D_MODEL), F32)],
        compiler_params=_cp("arbitrary"),
    )(proj, proj, proj, proj, pos_col, conv_w, conv_b, rwa, rwx, ba, bx, lam)


def _merge_and_head(x, target, y_attn, y_rnn, proj, wap, wrp, wo, mod_row, final_g):
    T = x.shape[0]
    tm = min(T, 256)

    def body(x_ref, t_ref, ya_ref, yr_ref, ma0, ma1, mr0, mr1, wap_ref, wrp_ref, wo_ref, mod_ref, fg_ref,
             dx2_ref, mg_ref, do_ref, dpa_ref, dpr_ref, dya_ref, dyr_ref, dc_ref, dfg_ref, dgate_ref, loss_ref):
        i = pl.program_id(0)
        gate = mod_ref[:, 2 * D_MODEL:3 * D_MODEL]
        ya, yr = ya_ref[...], yr_ref[...]
        pa, pr = _dot(ya, wap_ref[...]), _dot(yr, wrp_ref[...])
        sa = _sigmoid(jnp.concatenate([ma0[...], ma1[...]], axis=1))
        sr = _sigmoid(jnp.concatenate([mr0[...], mr1[...]], axis=1))
        merged = sa * pa + sr * pr
        mb = merged.astype(BF16)
        o = _dot(mb, wo_ref[...])
        x2 = x_ref[...] + gate * o
        r2 = _rms(x2)
        xn2 = x2 * r2
        fg = fg_ref[...]
        err = xn2 * fg - t_ref[...]
        loss_t = 0.5 * jnp.sum(jnp.sum(err * err, axis=-1, keepdims=True) * (1.0 / D_MODEL), axis=0, keepdims=True)
        dy = err * (1.0 / D_MODEL)
        dfg_t = jnp.sum(dy * xn2, axis=0, keepdims=True)
        dxn = dy * fg
        dx2 = r2 * (dxn - xn2 * jnp.mean(dxn * xn2, axis=-1, keepdims=True))
        dgate_t = jnp.sum(dx2 * o, axis=0, keepdims=True)
        dob = (dx2 * gate).astype(BF16)
        dmerged = _dot_nt(dob, wo_ref[...])
        dpa = (dmerged * sa).astype(BF16)
        dpr = (dmerged * sr).astype(BF16)
        dx2_ref[...] = dx2
        mg_ref[...] = mb
        do_ref[...] = dob
        dpa_ref[...] = dpa
        dpr_ref[...] = dpr
        dya_ref[...] = _dot_nt(dpa, wap_ref[...])
        dyr_ref[...] = _dot_nt(dpr, wrp_ref[...])
        dc_ref[:, 0:D_MODEL] = (dmerged * pa * sa * (1.0 - sa)).astype(BF16)
        dc_ref[:, D_MODEL:2 * D_MODEL] = (dmerged * pr * sr * (1.0 - sr)).astype(BF16)

        @pl.when(i == 0)
        def _():
            dfg_ref[...] = jnp.zeros_like(dfg_ref)
            dgate_ref[...] = jnp.zeros_like(dgate_ref)
            loss_ref[...] = jnp.zeros_like(loss_ref)

        dfg_ref[...] += dfg_t
        dgate_ref[...] += dgate_t
        loss_ref[...] += jnp.broadcast_to(loss_t, loss_ref.shape)

    tok = lambda w: pl.BlockSpec((tm, w), lambda i: (i, 0))
    blk = lambda cb: pl.BlockSpec((tm, CB), lambda i, cb=cb: (i, cb))
    wfull = pl.BlockSpec((D_MODEL, D_MODEL), lambda i: (0, 0))
    row = lambda w: pl.BlockSpec((1, w), lambda i: (0, 0))
    out_shape = (
        jax.ShapeDtypeStruct((T, D_MODEL), F32), jax.ShapeDtypeStruct((T, D_MODEL), BF16),
        jax.ShapeDtypeStruct((T, D_MODEL), BF16), jax.ShapeDtypeStruct((T, D_MODEL), BF16),
        jax.ShapeDtypeStruct((T, D_MODEL), BF16), jax.ShapeDtypeStruct((T, D_MODEL), F32),
        jax.ShapeDtypeStruct((T, D_MODEL), F32), jax.ShapeDtypeStruct((T, 2 * D_MODEL), BF16),
        jax.ShapeDtypeStruct((1, D_MODEL), F32), jax.ShapeDtypeStruct((1, D_MODEL), F32),
        jax.ShapeDtypeStruct((1, 128), F32),
    )
    return pl.pallas_call(
        body, out_shape=out_shape, grid=(T // tm,), name="merge_and_head",
        in_specs=[tok(D_MODEL), tok(D_MODEL), tok(D_MODEL), tok(D_MODEL), blk(CB_MA), blk(CB_MA + 1), blk(CB_MR),
                  blk(CB_MR + 1), wfull, wfull, wfull, row(ADA_W), row(D_MODEL)],
        out_specs=(tok(D_MODEL),) * 7 + (tok(2 * D_MODEL), row(D_MODEL), row(D_MODEL), row(128)),
        compiler_params=_cp("arbitrary"),
    )(x, target, y_attn, y_rnn, proj, proj, proj, proj, wap, wrp, wo, mod_row, final_g)


def _attn_backward(proj, d_y, tabs, sinks):
    T = proj.shape[0]
    nb = T // BLOCK

    def body(q_ref, kvc_ref, kvp_ref, g0_ref, g1_ref, dy_ref, cc, sac, sbc, cp_, sap, sbp, sink_ref,
             dq_ref, dkv_ref, dg_ref, dsink_ref, carry):
        n = pl.program_id(0)

        @pl.when(n == 0)
        def _():
            carry[...] = jnp.zeros_like(carry)
            dsink_ref[...] = jnp.zeros_like(dsink_ref)

        @pl.when(n < nb)
        def _():
            tc = (_wide(cc[...], D_MODEL), _wide(sac[...], D_MODEL), _wide(sbc[...], D_MODEL))
            tcur = tuple(t[:, :KV_W] for t in tc)
            tprev = (_wide(cp_[...], KV_W), _wide(sap[...], KV_W), _wide(sbp[...], KV_W))
            qr = _rope(q_ref[...], *tc)
            kr_cur = _rope(kvc_ref[:, 0:KV_W], *tcur)
            kr_prev = _rope(kvp_ref[:, 0:KV_W], *tprev)
            v_cur, v_prev = kvc_ref[:, KV_W:2 * KV_W], kvp_ref[:, KV_W:2 * KV_W]
            g = jnp.concatenate([g0_ref[...], g1_ref[...]], axis=1)
            sg = _sigmoid(g)
            dy = dy_ref[...]
            d_o = dy * (g * sg)
            mask = _attn_mask(n)
            lane = lax.broadcasted_iota(jnp.int32, (1, 128), 1)
            rowg = lax.broadcasted_iota(jnp.int32, (GROUP * BLOCK, 1), 0) // BLOCK
            o_parts, dq_parts, dk_parts, dv_parts = [], [], [], []
            dsink = jnp.zeros((1, 128), F32)
            for kh in range(N_KV):
                qs, kk, vv, pn, pn_sink = _attn_probs(qr, kr_prev, kr_cur, v_prev, v_cur, kh, _sink_col(sink_ref, kh), mask)
                pnb = pn.astype(BF16)
                o_parts.append(_dot(pnb, vv))
                dos = jnp.concatenate(
                    [d_o[:, HEAD_DIM * (GROUP * kh + gq): HEAD_DIM * (GROUP * kh + gq + 1)] for gq in range(GROUP)],
                    axis=0).astype(BF16)
                dpn = _dot_nt(dos, vv)
                delta = jnp.sum(pn * dpn, axis=-1, keepdims=True)
                dsb = (pn * (dpn - delta) * (1.0 / 8.0)).astype(BF16)
                dq_parts.append(_dot(dsb, kk))
                dk_parts.append(_dot_tn(dsb, qs))
                dv_parts.append(_dot_tn(pnb, dos))
                ds_rows = pn_sink * delta
                for gq in range(GROUP):
                    val = -jnp.sum(jnp.where(rowg == gq, ds_rows, 0.0), axis=0, keepdims=True)
                    dsink = dsink + jnp.where(lane == GROUP * kh + gq, val, 0.0)
            o = _unstack_heads(o_parts)
            dg_ref[...] = (dy * o * (sg * (1.0 + g * (1.0 - sg)))).astype(BF16)
            dq_ref[...] = _unrope(_unstack_heads(dq_parts), *tc).astype(BF16)
            dk_prev = _unrope(jnp.concatenate([p[0:BLOCK, :] for p in dk_parts], axis=1), *tprev)
            dk_cur = _unrope(jnp.concatenate([p[BLOCK:2 * BLOCK, :] for p in dk_parts], axis=1), *tcur)
            dv_prev = jnp.concatenate([p[0:BLOCK, :] for p in dv_parts], axis=1)
            dv_cur = jnp.concatenate([p[BLOCK:2 * BLOCK, :] for p in dv_parts], axis=1)
            dkv_ref[...] = (carry[...] + jnp.concatenate([dk_prev, dv_prev], axis=1)).astype(BF16)
            carry[...] = jnp.concatenate([dk_cur, dv_cur], axis=1)
            dsink_ref[...] += dsink

        @pl.when(n == nb)
        def _():
            dkv_ref[...] = carry[...].astype(BF16)

    cur = lambda w, cb: pl.BlockSpec((BLOCK, w), lambda n, cb=cb: (jnp.minimum(n, nb - 1), cb))
    prev = lambda w, cb: pl.BlockSpec((BLOCK, w), lambda n, cb=cb: (jnp.maximum(jnp.minimum(n, nb - 1) - 1, 0), cb))
    out_shape = (jax.ShapeDtypeStruct((T, D_MODEL), BF16), jax.ShapeDtypeStruct((T, 2 * KV_W), BF16),
                 jax.ShapeDtypeStruct((T, D_MODEL), BF16), jax.ShapeDtypeStruct((1, 128), F32))
    return pl.pallas_call(
        body, out_shape=out_shape, grid=(nb + 1,), name="attn_backward",
        in_specs=[cur(D_MODEL, 0), cur(CB, CB_KV), prev(CB, CB_KV), cur(CB, CB_GA), cur(CB, CB_GA + 1), cur(D_MODEL, 0),
                  cur(128, 0), cur(128, 0), cur(128, 0), prev(128, 0), prev(128, 0), prev(128, 0),
                  pl.BlockSpec(memory_space=pltpu.SMEM)],
        out_specs=(cur(D_MODEL, 0), pl.BlockSpec((BLOCK, 2 * KV_W), lambda n: (jnp.maximum(n - 1, 0), 0)),
                   cur(D_MODEL, 0), pl.BlockSpec((1, 128), lambda n: (0, 0))),
        scratch_shapes=[pltpu.VMEM((BLOCK, 2 * KV_W), F32)],
        compiler_params=_cp("arbitrary"),
    )(proj, proj, proj, proj, proj, d_y, *tabs, *tabs, sinks)


def _rnn_backward(proj, pos_col, h_rnn, d_y, conv_w, conv_b, rwa, rwx, ba, bx, lam):
    T = proj.shape[0]
    tr = min(T, 256)
    nt = T // tr
    hb = tr // 8

    def body(x0, x1, xh0, xh1, g0, g1, pos_ref, h_ref, hh_ref, dy_ref, cw_ref, cb_ref, wa_ref, wx_ref, ba_ref, bx_ref,
             lam_ref, db_ref, dcw_ref, dcb_ref, dwa_ref, dwx_ref, dba_ref, dbx_ref, dlam_ref,
             xbuf, hbuf, dbuf, abuf, gbuf, lbuf, mu_carry, dxc_head):
        step = pl.program_id(0)
        first_tile = step == nt - 1

        @pl.when(step == 0)
        def _():
            mu_carry[...] = jnp.zeros_like(mu_carry)
            dxc_head[...] = jnp.zeros_like(dxc_head)
            for ref in (dcw_ref, dcb_ref, dwa_ref, dwx_ref, dba_ref, dbx_ref, dlam_ref):
                ref[...] = jnp.zeros_like(ref)

        xr = jnp.concatenate([x0[...], x1[...]], axis=1)
        tail = jnp.where(first_tile, 0.0, jnp.concatenate([xh0[...], xh1[...]], axis=1))
        lam_v = lam_ref[...]
        sp = _softplus(-lam_v)
        reset = pos_ref[...] == 0
        cw = cw_ref[...]
        xs, xc, xcb, r, i, a_raw, mult_raw, a, mult = _rnn_recompute(
            xbuf, xr, tail, cw, cb_ref[...], wa_ref, wx_ref, ba_ref[...], bx_ref[...], sp, reset)
        g = jnp.concatenate([g0[...], g1[...]], axis=1)
        sg = _sigmoid(g)
        dy = dy_ref[...]
        h = h_ref[...]
        d_g = dy * h * (sg * (1.0 + g * (1.0 - sg)))
        abuf[...] = a
        gbuf[...] = dy * (g * sg)
        top = _scan_backward(abuf, gbuf, lbuf, mu_carry[0:1, :], tr)
        mu_carry[...] = jnp.broadcast_to(top, mu_carry.shape)
        lam_t = lbuf[...]
        hbuf[0:8, :] = jnp.where(first_tile, 0.0, hh_ref[...])
        hbuf[8:tr + 8, :] = h
        h_prev = hbuf[pl.ds(7, tr), :]
        live = jnp.logical_not(reset)
        d_a = jnp.where(live, lam_t * h_prev, 0.0)
        d_mult = jnp.where(live, lam_t * (i * xc), 0.0)
        d_ixc = lam_t * mult
        d_i = d_ixc * xc
        d_xc = d_ixc * i
        d_log_a = d_a * a_raw - d_mult * (a_raw * a_raw / mult_raw)
        d_log_a = jnp.where(live, d_log_a, 0.0)
        d_za = d_log_a * (-LRU_C * sp) * (r * (1.0 - r))
        d_zx = d_i * (i * (1.0 - i))
        dlam_ref[...] += jnp.sum(d_log_a * r, axis=0, keepdims=True) * (LRU_C * _sigmoid(-lam_v))
        dba_ref[...] += jnp.sum(d_za, axis=0, keepdims=True)
        dbx_ref[...] += jnp.sum(d_zx, axis=0, keepdims=True)
        dzab, dzxb = d_za.astype(BF16), d_zx.astype(BF16)
        back = []
        for j in range(RNN_BLOCKS):
            sl = slice(RNN_BW * j, RNN_BW * (j + 1))
            dwa_ref[j] += _dot_tn(xcb[:, sl], dzab[:, sl])
            dwx_ref[j] += _dot_tn(xcb[:, sl], dzxb[:, sl])
            back.append(_dot_nt(dzab[:, sl], wa_ref[j]) + _dot_nt(dzxb[:, sl], wx_ref[j]))
        d_xc = d_xc + jnp.concatenate(back, axis=1)
        dcb_ref[...] += jnp.sum(d_xc, axis=0, keepdims=True)
        for k in range(CONV_W):
            dcw_ref[k:k + 1, :] += jnp.sum(d_xc * xs[k], axis=0, keepdims=True)
        dbuf[0:tr, :] = d_xc
        dbuf[tr:tr + 8, :] = dxc_head[...]
        d_xr = d_xc * cw[CONV_W - 1:CONV_W, :]
        for k in range(CONV_W - 1):
            d_xr = d_xr + dbuf[pl.ds(CONV_W - 1 - k, tr), :] * cw[k:k + 1, :]
        dxc_head[...] = d_xc[0:8, :]
        db_ref[:, 0:D_MODEL] = d_xr.astype(BF16)
        db_ref[:, D_MODEL:2 * D_MODEL] = d_g.astype(BF16)

    rev = lambda s: nt - 1 - s
    blk = lambda cb: pl.BlockSpec((tr, CB), lambda s, cb=cb: (rev(s), cb))
    halo = lambda w, cb: pl.BlockSpec((8, w), lambda s, cb=cb: (jnp.maximum(rev(s) * hb - 1, 0), cb))
    tok = lambda w: pl.BlockSpec((tr, w), lambda s: (rev(s), 0))
    row = lambda w: pl.BlockSpec((1, w), lambda s: (0, 0))
    full3 = pl.BlockSpec((RNN_BLOCKS, RNN_BW, RNN_BW), lambda s: (0, 0, 0))
    cwspec = pl.BlockSpec((CONV_W, D_MODEL), lambda s: (0, 0))
    vec = jax.ShapeDtypeStruct((1, D_MODEL), F32)
    gate_w = jax.ShapeDtypeStruct((RNN_BLOCKS, RNN_BW, RNN_BW), F32)
    out_shape = (jax.ShapeDtypeStruct((T, 2 * D_MODEL), BF16), jax.ShapeDtypeStruct((CONV_W, D_MODEL), F32), vec,
                 gate_w, gate_w, vec, vec, vec)
    big = lambda: pltpu.VMEM((tr, D_MODEL), F32)
    ext = lambda: pltpu.VMEM((tr + 8, D_MODEL), F32)
    return pl.pallas_call(
        body, out_shape=out_shape, grid=(nt,), name="rnn_backward",
        in_specs=[blk(CB_XR), blk(CB_XR + 1), halo(CB, CB_XR), halo(CB, CB_XR + 1), blk(CB_GR), blk(CB_GR + 1),
                  pl.BlockSpec((tr, 1), lambda s: (rev(s), 0)), tok(D_MODEL), halo(D_MODEL, 0), tok(D_MODEL),
                  cwspec, row(D_MODEL), full3, full3, row(D_MODEL), row(D_MODEL), row(D_MODEL)],
        out_specs=(tok(2 * D_MODEL), cwspec, row(D_MODEL), full3, full3, row(D_MODEL), row(D_MODEL), row(D_MODEL)),
        scratch_shapes=[ext(), ext(), ext(), big(), big(), big(), pltpu.VMEM((8, D_MODEL), F32), pltpu.VMEM((8, D_MODEL), F32)],
        compiler_params=_cp("arbitrary"),
    )(proj, proj, proj, proj, proj, proj, pos_col, h_rnn, h_rnn, d_y, conv_w, conv_b, rwa, rwx, ba, bx, lam)


def _input_backward(pieces, w_in, x, dx2, mod_row, norm_g):
    T = x.shape[0]
    tm = min(T, 512)
    n = len(pieces)

    def body(*refs):
        d_refs = refs[:n]
        w_ref, x_ref, dx2_ref, mod_ref, g_ref, gx_ref, dshift_ref, dscale_ref, dg_ref, acc = refs[n:]
        i, k = pl.program_id(0), pl.program_id(1)

        @pl.when(k == 0)
        def _():
            acc[...] = jnp.zeros_like(acc)

        for d_ref, (_, start, count) in zip(d_refs, pieces):
            @pl.when((k >= start) & (k < start + count))
            def _(d_ref=d_ref):
                acc[...] += _dot_nt(d_ref[...], w_ref[...])

        @pl.when((i == 0) & (k == 0))
        def _():
            dshift_ref[...] = jnp.zeros_like(dshift_ref)
            dscale_ref[...] = jnp.zeros_like(dscale_ref)
            dg_ref[...] = jnp.zeros_like(dg_ref)

        @pl.when(k == N_CB - 1)
        def _():
            dh = acc[...]
            xf = x_ref[...]
            r1 = _rms(xf)
            xn = xf * r1
            gn = g_ref[...]
            s1 = 1.0 + mod_ref[:, D_MODEL:2 * D_MODEL]
            dshift_ref[...] += jnp.sum(dh, axis=0, keepdims=True)
            dscale_ref[...] += jnp.sum(dh * (xn * gn), axis=0, keepdims=True)
            dg_ref[...] += jnp.sum(dh * s1 * xn, axis=0, keepdims=True)
            dxn = dh * s1 * gn
            gx_ref[...] = dx2_ref[...] + r1 * (dxn - xn * jnp.mean(dxn * xn, axis=-1, keepdims=True))

    def piece_spec(start, count):
        return pl.BlockSpec((tm, CB), lambda i, k: (i, jnp.clip(k - start, 0, count - 1)))

    tok = pl.BlockSpec((tm, D_MODEL), lambda i, k: (i, 0))
    row = lambda w: pl.BlockSpec((1, w), lambda i, k: (0, 0))
    vec = jax.ShapeDtypeStruct((1, D_MODEL), F32)
    return pl.pallas_call(
        body, out_shape=(jax.ShapeDtypeStruct((T, D_MODEL), F32), vec, vec, vec), grid=(T // tm, N_CB), name="input_backward",
        in_specs=[piece_spec(s, c) for _, s, c in pieces]
        + [pl.BlockSpec((D_MODEL, CB), lambda i, k: (0, k)), tok, tok, row(ADA_W), row(D_MODEL)],
        out_specs=(tok, row(D_MODEL), row(D_MODEL), row(D_MODEL)),
        scratch_shapes=[pltpu.VMEM((tm, D_MODEL), F32)],
        compiler_params=_cp("arbitrary", "arbitrary"),
    )(*[p[0] for p in pieces], w_in, x, dx2, mod_row, norm_g)


def _weight_grad(a, b, tag, into=None, col_block=0, total_cols=None):
    T, M = a.shape
    N = b.shape[1]
    tk = min(T, 512)
    tn = CB
    total_cols = N if total_cols is None else total_cols

    def body(*refs):
        a_ref, b_ref, o_ref = refs[0], refs[1], refs[-1]
        k = pl.program_id(1)

        @pl.when(k == 0)
        def _():
            o_ref[...] = jnp.zeros_like(o_ref)

        o_ref[...] += _dot_tn(a_ref[...], b_ref[...])

    in_specs = [pl.BlockSpec((tk, M), lambda j, k: (k, 0)), pl.BlockSpec((tk, tn), lambda j, k: (k, j))]
    args = [a, b]
    aliases = {}
    if into is not None:
        in_specs.append(ANY)
        args.append(into)
        aliases = {2: 0}
    return pl.pallas_call(
        body, out_shape=jax.ShapeDtypeStruct((M, total_cols), F32), grid=(N // tn, T // tk), name=f"weight_grad_{tag}",
        in_specs=in_specs, out_specs=pl.BlockSpec((M, tn), lambda j, k: (0, col_block + j)),
        input_output_aliases=aliases, compiler_params=_cp("parallel", "arbitrary"),
    )(*args)


def _adamw(w, g, m, v):
    m = ADAM_B1 * m + (1.0 - ADAM_B1) * g
    v = ADAM_B2 * v + (1.0 - ADAM_B2) * (g * g)
    m_hat = m / (1.0 - ADAM_B1 ** ADAM_STEP)
    v_hat = v / (1.0 - ADAM_B2 ** ADAM_STEP)
    delta = -ADAM_LR * (m_hat / (jnp.sqrt(v_hat) + ADAM_EPS) + ADAM_WD * w)
    return delta, m, v


def _sum_landed(land, tag):
    _, R, C = land.shape
    tr = min(R, 256)

    def body(l_ref, s_ref):
        s_ref[...] = ((l_ref[3] + l_ref[0]) + l_ref[1]) + l_ref[2]

    return pl.pallas_call(
        body, out_shape=jax.ShapeDtypeStruct((R, C), F32), grid=(R // tr,), name=f"sum_landed_{tag}",
        in_specs=[pl.BlockSpec((4, tr, C), lambda i: (0, i, 0))], out_specs=pl.BlockSpec((tr, C), lambda i: (i, 0)),
        compiler_params=_cp("parallel"),
    )(land)


def _adamw_shard(s_mine, s_sib, w, m, v, tag):
    R, C = w.shape
    tr = min(R, 256)

    def body(a_ref, b_ref, w_ref, m_ref, v_ref, g_ref, d_ref, nm_ref, nv_ref):
        g = a_ref[...] + b_ref[...]
        d, nm, nv = _adamw(w_ref[...], g, m_ref[...], v_ref[...])
        g_ref[...] = g
        d_ref[...] = d
        nm_ref[...] = nm
        nv_ref[...] = nv

    spec = pl.BlockSpec((tr, C), lambda i: (i, 0))
    sds = jax.ShapeDtypeStruct((R, C), F32)
    return pl.pallas_call(
        body, out_shape=(sds,) * 4, grid=(R // tr,), name=f"adamw_{tag}",
        in_specs=[spec] * 5, out_specs=(spec,) * 4, compiler_params=_cp("parallel"),
    )(s_mine, s_sib, w, m, v)


def _adamw_w_ada(c_t, dmod_cols, w, m, v):
    R, C = w.shape

    def body(ct_ref, dm_ref, w_ref, m_ref, v_ref, g_ref, d_ref, nm_ref, nv_ref):
        g = _dot(ct_ref[...].astype(BF16), dm_ref[...].astype(BF16))
        d, nm, nv = _adamw(w_ref[...], g, m_ref[...], v_ref[...])
        g_ref[...] = g
        d_ref[...] = d
        nm_ref[...] = nm
        nv_ref[...] = nv

    tr = 256
    spec = pl.BlockSpec((tr, C), lambda i: (i, 0))
    sds = jax.ShapeDtypeStruct((R, C), F32)
    return pl.pallas_call(
        body, out_shape=(sds,) * 4, grid=(R // tr,), name="adamw_w_ada",
        in_specs=[pl.BlockSpec((tr, 128), lambda i: (i, 0)), pl.BlockSpec((128, C), lambda i: (0, 0))] + [spec] * 3,
        out_specs=(spec,) * 4, compiler_params=_cp("parallel"),
    )(c_t, dmod_cols, w, m, v)


def _adamw_small(small_all, ws, ms, vs):
    def body(s_ref, w_ref, m_ref, v_ref, g_ref, d_ref, nm_ref, nv_ref):
        g = s_ref[0]
        for b in range(1, N_DEV):
            g = g + s_ref[b]
        d, nm, nv = _adamw(w_ref[...], g, m_ref[...], v_ref[...])
        g_ref[...] = g
        d_ref[...] = d
        nm_ref[...] = nm
        nv_ref[...] = nv

    sds = jax.ShapeDtypeStruct((SMALL_ROWS, D_MODEL), F32)
    return pl.pallas_call(
        body, out_shape=(sds,) * 4, name="adamw_small", in_specs=[VMEM_SPEC] * 4, out_specs=(VMEM_SPEC,) * 4,
        compiler_params=pltpu.CompilerParams(vmem_limit_bytes=VMEM_LIMIT_V7X),
    )(small_all, ws, ms, vs)


ROW_MOD, ROW_NORM_G, ROW_CONV_B, ROW_BA, ROW_BX, ROW_LAM, ROW_FINAL_G, ROW_SINKS, ROW_CONV_W = 0, 3, 4, 5, 6, 7, 8, 9, 10


def _pack_small(b_ada, norm_g, conv_b, ba, bx, lam, final_g, sinks, conv_w_full):
    rows = [b_ada.reshape(3, D_MODEL), norm_g, conv_b, ba, bx, lam, final_g.reshape(1, D_MODEL),
            jnp.pad(sinks.reshape(1, -1), ((0, 0), (0, D_MODEL - sinks.size))), conv_w_full,
            jnp.zeros((SMALL_ROWS - 14, D_MODEL), F32)]
    return jnp.concatenate([r.astype(F32) for r in rows], axis=0)


def kernel(x, c, positions, w_ada, b_ada, norm_g, w_in, attn_sinks, conv_w, conv_b, rg_wa, rg_ba, rg_wx, rg_bx, rg_lambda, w_attn_proj, w_rnn_proj, w_out, final_g, loss_target, m_w_ada, m_b_ada, m_norm_g, m_w_in, m_attn_sinks, m_conv_w, m_conv_b, m_rg_wa, m_rg_ba, m_rg_wx, m_rg_bx, m_rg_lambda, m_w_attn_proj, m_w_rnn_proj, m_w_out, m_final_g, v_w_ada, v_b_ada, v_norm_g, v_w_in, v_attn_sinks, v_conv_w, v_conv_b, v_rg_wa, v_rg_ba, v_rg_wx, v_rg_bx, v_rg_lambda, v_w_attn_proj, v_w_rnn_proj, v_w_out, v_final_g):
    T = x.shape[1]
    my_chip = lax.axis_index("x") * 2 + lax.axis_index("y")
    my_dev = my_chip * 2 + lax.axis_index("c")
    x2d, tgt = x[0], loss_target[0]
    pos_col = positions.reshape(T, 1)

    gathered = _gather_weights(
        c.reshape(1, 1, D_MODEL), w_ada[0], w_in[0].astype(BF16), w_attn_proj[0].astype(BF16), w_rnn_proj[0].astype(BF16),
        w_out[0].astype(BF16), rg_wa[0].astype(BF16), rg_wx[0].astype(BF16), conv_w[0])
    w_in_f, wap_f, wrp_f, wo_f, rwa_f, rwx_f, cw_chips, c_all, mod_chips = gathered
    conv_w_f = jnp.transpose(cw_chips, (1, 0, 2)).reshape(CONV_W, D_MODEL)
    mod_all = jnp.transpose(mod_chips, (1, 0, 2)).reshape(N_DEV, ADA_W) + b_ada
    mod_row = lax.dynamic_slice_in_dim(mod_all, my_dev, 1, axis=0)

    tabs = _rope_tables(pos_col)
    h = _prenorm(x2d, mod_row, norm_g)
    proj = _in_projection(h, w_in_f)
    y_attn = _attn_forward(proj, tabs, attn_sinks)
    y_rnn, h_rnn = _rnn_forward(proj, pos_col, conv_w_f, conv_b, rwa_f, rwx_f, rg_ba, rg_bx, rg_lambda)
    (dx2, merged, d_o, d_pa, d_pr, d_ya, d_yr, d_c, d_final_g, d_gate, loss_vec) = _merge_and_head(
        x2d, tgt, y_attn, y_rnn, proj, wap_f, wrp_f, wo_f, mod_row, final_g.reshape(1, D_MODEL))

    d_q, d_kv, d_ga, d_sinks = _attn_backward(proj, d_ya, tabs, attn_sinks)
    d_b, d_conv_w, d_conv_b, d_rwa, d_rwx, d_ba, d_bx, d_lam = _rnn_backward(
        proj, pos_col, h_rnn, d_yr, conv_w_f, conv_b, rwa_f, rwx_f, rg_ba, rg_bx, rg_lambda)
    pieces = [(d_q, CB_Q, 2), (d_kv, CB_KV, 1), (d_ga, CB_GA, 2), (d_b, CB_XR, 4), (d_c, CB_MA, 4)]
    grad_x, d_shift, d_scale, d_norm_g = _input_backward(pieces, w_in_f, x2d, dx2, mod_row, norm_g)
    g_in = None
    for arr, start, _ in pieces:
        g_in = _weight_grad(h, arr, f"w_in_{start}", into=g_in, col_block=start, total_cols=IN_W)
    g_ap = _weight_grad(y_attn, d_pa, "w_attn_proj")
    g_rp = _weight_grad(y_rnn, d_pr, "w_rnn_proj")
    g_o = _weight_grad(merged, d_o, "w_out")

    d_mod = jnp.concatenate([d_shift, d_scale, d_gate], axis=1)
    small = _pack_small(d_mod, d_norm_g, d_conv_b, d_ba, d_bx, d_lam, d_final_g, d_sinks[:, :N_HEADS], d_conv_w)
    lands = _exchange_partials(g_in, g_ap, g_rp, g_o, d_rwa, d_rwx, small)
    small_all = lands[6]
    shapes2d = [(D_MODEL, SHARD_IN), (SHARD_ROWS, D_MODEL), (SHARD_ROWS, D_MODEL), (SHARD_ROWS, D_MODEL),
                (RNN_BLOCKS * SHARD_RG, RNN_BW), (RNN_BLOCKS * SHARD_RG, RNN_BW)]
    tags = ["w_in", "w_attn_proj", "w_rnn_proj", "w_out", "rg_wa", "rg_wx"]
    sums = [_sum_landed(lands[i].reshape((4,) + shapes2d[i]), tags[i]) for i in range(6)]
    sib = _swap_with_sibling(sums)
    big_w = [w_in, w_attn_proj, w_rnn_proj, w_out, rg_wa, rg_wx]
    big_m = [m_w_in, m_w_attn_proj, m_w_rnn_proj, m_w_out, m_rg_wa, m_rg_wx]
    big_v = [v_w_in, v_w_attn_proj, v_w_rnn_proj, v_w_out, v_rg_wa, v_rg_wx]
    res = {}
    for i, tag in enumerate(tags):
        outs = _adamw_shard(sums[i], sib[i], big_w[i].reshape(shapes2d[i]), big_m[i].reshape(shapes2d[i]),
                            big_v[i].reshape(shapes2d[i]), tag)
        res[tag] = [o.reshape(big_w[i].shape) for o in outs]

    dmod_all = small_all[:, ROW_MOD:ROW_MOD + 3, :].reshape(N_DEV, ADA_W)
    dmod_cols = lax.dynamic_slice_in_dim(dmod_all, my_chip * SHARD_ADA, SHARD_ADA, axis=1)
    c_t = jnp.pad(jnp.transpose(c_all.reshape(N_DEV, D_MODEL)), ((0, 0), (0, 128 - N_DEV)))
    dmod_cols = jnp.pad(dmod_cols, ((0, 128 - N_DEV), (0, 0)))
    res["w_ada"] = [o.reshape(w_ada.shape) for o in _adamw_w_ada(c_t, dmod_cols, w_ada[0], m_w_ada[0], v_w_ada[0])]

    def full_conv(a):
        return lax.dynamic_update_slice_in_dim(jnp.zeros((CONV_W, D_MODEL), F32), a[0], my_chip * (D_MODEL // N_CHIPS), axis=1)

    packed = [_pack_small(p[0], p[1], p[2], p[3], p[4], p[5], p[6], p[7], full_conv(p[8])) for p in (
        (b_ada, norm_g, conv_b, rg_ba, rg_bx, rg_lambda, final_g, attn_sinks, conv_w),
        (m_b_ada, m_norm_g, m_conv_b, m_rg_ba, m_rg_bx, m_rg_lambda, m_final_g, m_attn_sinks, m_conv_w),
        (v_b_ada, v_norm_g, v_conv_b, v_rg_ba, v_rg_bx, v_rg_lambda, v_final_g, v_attn_sinks, v_conv_w))]
    small_out = _adamw_small(small_all, *packed)

    def unpack(slab):
        cw = lax.dynamic_slice_in_dim(slab[ROW_CONV_W:ROW_CONV_W + CONV_W], my_chip * (D_MODEL // N_CHIPS),
                                      D_MODEL // N_CHIPS, axis=1)
        return {
            "b_ada": slab[ROW_MOD:ROW_MOD + 3].reshape(1, ADA_W), "norm_g": slab[ROW_NORM_G:ROW_NORM_G + 1],
            "conv_b": slab[ROW_CONV_B:ROW_CONV_B + 1], "rg_ba": slab[ROW_BA:ROW_BA + 1], "rg_bx": slab[ROW_BX:ROW_BX + 1],
            "rg_lambda": slab[ROW_LAM:ROW_LAM + 1], "final_g": slab[ROW_FINAL_G], "attn_sinks": slab[ROW_SINKS:ROW_SINKS + 1, :N_HEADS],
            "conv_w": cw[None],
        }

    small_res = [unpack(s) for s in small_out]
    order = ["w_ada", "b_ada", "norm_g", "w_in", "attn_sinks", "conv_w", "conv_b", "rg_wa", "rg_ba", "rg_wx", "rg_bx",
             "rg_lambda", "w_attn_proj", "w_rnn_proj", "w_out", "final_g"]
    loss = lax.psum(loss_vec[0, 0], ("x", "y", "c"))
    outs = [loss, grad_x[None]]
    for kind in range(4):
        for name in order:
            outs.append(res[name][kind] if name in res else small_res[kind][name])
    return tuple(outs)
```

```python
import numpy as np
import jax
import jax.numpy as jnp
from jax import lax
from jax.experimental import pallas as pl
from jax.experimental.pallas import tpu as pltpu

F32 = jnp.float32
BF16 = jnp.bfloat16

D_MODEL = 1024
N_HEADS = 16
N_KV = 4
HEAD_DIM = 64
GROUP = N_HEADS // N_KV
BLOCK = 128
KV_W = N_KV * HEAD_DIM
ROT_HALF = 8
ROPE_THETA = 500000.0
RNN_BLOCKS = 4
RNN_BW = 256
CONV_W = 4
LRU_C = 8.0
NORM_EPS = 1e-6
IN_W = 6656
CB = 512
N_CB = IN_W // CB
CB_Q, CB_KV, CB_GA, CB_XR, CB_GR, CB_MA, CB_MR = 0, 2, 3, 5, 7, 9, 11
N_CHIPS = 4
N_DEV = 8
SHARD_IN = IN_W // N_CHIPS
SHARD_ROWS = D_MODEL // N_CHIPS
SHARD_RG = RNN_BW // N_CHIPS
ADA_W = 3 * D_MODEL
SHARD_ADA = ADA_W // N_CHIPS
SMALL_ROWS = 16

ADAM_LR = 0.001
ADAM_B1 = 0.9
ADAM_B2 = 0.999
ADAM_EPS = 1e-08
ADAM_WD = 0.01
ADAM_STEP = 10

VMEM_LIMIT_V7X = 52 * 1024 * 1024
MESH = pl.DeviceIdType.MESH
ANY = pl.BlockSpec(memory_space=pl.ANY)
VMEM_SPEC = pl.BlockSpec(memory_space=pltpu.VMEM)


def _cp(*sem):
    return pltpu.CompilerParams(dimension_semantics=sem if sem else None, vmem_limit_bytes=VMEM_LIMIT_V7X)


def _dot(a, b):
    return jnp.dot(a, b, preferred_element_type=F32)


def _dot_nt(a, b):
    return lax.dot_general(a, b, (((1,), (1,)), ((), ())), preferred_element_type=F32)


def _dot_tn(a, b):
    return lax.dot_general(a, b, (((0,), (0,)), ((), ())), preferred_element_type=F32)


def _sigmoid(z):
    return 1.0 / (1.0 + jnp.exp(-z))


def _neg_expm1(z):
    series = -(z * (1.0 + z * (0.5 + z * (1.0 / 6.0 + z * (1.0 / 24.0 + z * (1.0 / 120.0))))))
    return jnp.where(z > -0.05, series, 1.0 - jnp.exp(z))


def _softplus(z):
    u = jnp.exp(-jnp.abs(z))
    log1p_u = jnp.where(u < 1e-3, u * (1.0 - u * (0.5 - u * (1.0 / 3.0))), jnp.log(1.0 + u))
    return jnp.maximum(z, 0.0) + log1p_u


def _rms(xf):
    return lax.rsqrt(jnp.mean(xf * xf, axis=-1, keepdims=True) + NORM_EPS)


def _me():
    return lax.axis_index("x"), lax.axis_index("y"), lax.axis_index("c")


def _peer(mask):
    x, y, c = _me()
    fx, fy, fc = (mask >> 2) & 1, (mask >> 1) & 1, mask & 1
    return (x ^ fx if fx else x, y ^ fy if fy else y, c ^ fc if fc else c)


def _chip_of(pos):
    return pos[0] * 2 + pos[1]


CHIP_MASKS = (4, 2, 6)
ALL_MASKS = (1, 2, 3, 4, 5, 6, 7)


def _gather_weights(c_row, w_ada_s, b_w_in, b_wap, b_wrp, b_wo, b_rwa, b_rwx, conv_w_s):
    def body(c_ref, wada_ref, win_s, wap_s, wrp_s, wo_s, rwa_s, rwx_s, cw_s,
             win_f, wap_f, wrp_f, wo_f, rwa_f, rwx_f, cw_f, call_ref, mod_ref,
             wsend, wrecv, lsem, csend, crecv, msend, mrecv, fsend, frecv):
        me = _me()
        my_chip = _chip_of(me)
        my_dev = my_chip * 2 + me[2]
        srcs = (win_s, wap_s, wrp_s, wo_s, rwa_s, rwx_s, cw_s)
        fulls = (win_f, wap_f, wrp_f, wo_f, rwa_f, rwx_f, cw_f)

        def slot(idx, chip, half=None):
            full = fulls[idx]
            if idx == 0:
                cols = pl.ds(pl.multiple_of(chip * SHARD_IN, 128), SHARD_IN)
                return full.at[:, :, cols] if half is None else full.at[half, :, cols]
            if idx in (1, 2, 3):
                return full.at[chip] if half is None else full.at[chip, half]
            if idx in (4, 5):
                return full.at[:, chip] if half is None else full.at[:, chip, half]
            return full.at[chip]

        def my_half(idx):
            if idx in (0, 1, 2, 3):
                return srcs[idx].at[me[2]]
            if idx in (4, 5):
                return srcs[idx].at[:, me[2]]
            return srcs[idx]

        def wcopy(idx, k, to):
            return pltpu.make_async_remote_copy(
                src_ref=my_half(idx), dst_ref=slot(idx, my_chip, None if idx == 6 else me[2]),
                send_sem=wsend.at[idx, k], recv_sem=wrecv.at[idx, k], device_id=to, device_id_type=MESH)

        def wrecv_wait(idx, k, frm):
            pltpu.make_async_remote_copy(
                src_ref=my_half(idx), dst_ref=slot(idx, _chip_of(frm), None if idx == 6 else me[2]),
                send_sem=wsend.at[idx, k], recv_sem=wrecv.at[idx, k], device_id=frm, device_id_type=MESH).wait_recv()

        def forward(idx, k, chip, half, to):
            return pltpu.make_async_remote_copy(
                src_ref=slot(idx, chip, half), dst_ref=slot(idx, chip, half),
                send_sem=fsend.at[idx, k], recv_sem=frecv.at[idx, k], device_id=to, device_id_type=MESH)

        sends = []
        for idx in range(7):
            for k, mask in enumerate(CHIP_MASKS):
                cp = wcopy(idx, k, _peer(mask))
                cp.start()
                sends.append(cp)
        local = [pltpu.make_async_copy(srcs[idx], slot(idx, my_chip), lsem.at[idx]) for idx in range(7)]
        for cp in local:
            cp.start()

        call_ref[my_dev] = c_ref[0]
        csends = []
        for k, mask in enumerate(ALL_MASKS):
            cp = pltpu.make_async_remote_copy(
                src_ref=c_ref.at[0], dst_ref=call_ref.at[my_dev],
                send_sem=csend.at[k], recv_sem=crecv.at[k], device_id=_peer(mask), device_id_type=MESH)
            cp.start()
            csends.append(cp)
        for k, mask in enumerate(ALL_MASKS):
            frm = _peer(mask)
            pltpu.make_async_remote_copy(
                src_ref=c_ref.at[0], dst_ref=call_ref.at[_chip_of(frm) * 2 + frm[2]],
                send_sem=csend.at[k], recv_sem=crecv.at[k], device_id=frm, device_id_type=MESH).wait_recv()
        for cp in csends:
            cp.wait_send()

        c_all = call_ref[...].reshape(N_DEV, D_MODEL).astype(BF16)
        mod_ref[my_chip] = _dot(c_all, wada_ref[...].astype(BF16))
        msends = []
        for k, mask in enumerate(CHIP_MASKS):
            cp = pltpu.make_async_remote_copy(
                src_ref=mod_ref.at[my_chip], dst_ref=mod_ref.at[my_chip],
                send_sem=msend.at[k], recv_sem=mrecv.at[k], device_id=_peer(mask), device_id_type=MESH)
            cp.start()
            msends.append(cp)
        for k, mask in enumerate(CHIP_MASKS):
            frm = _peer(mask)
            pltpu.make_async_remote_copy(
                src_ref=mod_ref.at[my_chip], dst_ref=mod_ref.at[_chip_of(frm)],
                send_sem=msend.at[k], recv_sem=mrecv.at[k], device_id=frm, device_id_type=MESH).wait_recv()
        for cp in msends:
            cp.wait_send()

        sib = _peer(1)
        forwards = []
        for idx in range(7):
            for k, mask in enumerate(CHIP_MASKS):
                frm = _peer(mask)
                wrecv_wait(idx, k, frm)
                if idx < 6:
                    cp = forward(idx, k, _chip_of(frm), me[2], sib)
                    cp.start()
                    forwards.append(cp)
        for idx in range(6):
            for k, mask in enumerate(CHIP_MASKS):
                forward(idx, k, _chip_of(_peer(mask)), 1 - me[2], sib).wait_recv()
        for cp in sends + forwards:
            cp.wait_send()
        for cp in local:
            cp.wait()

    out_shape = (
        jax.ShapeDtypeStruct((2, D_MODEL // 2, IN_W), BF16),
        jax.ShapeDtypeStruct((N_CHIPS, 2, SHARD_ROWS // 2, D_MODEL), BF16),
        jax.ShapeDtypeStruct((N_CHIPS, 2, SHARD_ROWS // 2, D_MODEL), BF16),
        jax.ShapeDtypeStruct((N_CHIPS, 2, SHARD_ROWS // 2, D_MODEL), BF16),
        jax.ShapeDtypeStruct((RNN_BLOCKS, N_CHIPS, 2, SHARD_RG // 2, RNN_BW), BF16),
        jax.ShapeDtypeStruct((RNN_BLOCKS, N_CHIPS, 2, SHARD_RG // 2, RNN_BW), BF16),
        jax.ShapeDtypeStruct((N_CHIPS, CONV_W, D_MODEL // N_CHIPS), F32),
        jax.ShapeDtypeStruct((N_DEV, 1, D_MODEL), F32),
        jax.ShapeDtypeStruct((N_CHIPS, N_DEV, SHARD_ADA), F32),
    )
    return pl.pallas_call(
        body, out_shape=out_shape, name="gather_weights",
        in_specs=[VMEM_SPEC, VMEM_SPEC] + [ANY] * 7,
        out_specs=tuple([ANY] * 7 + [VMEM_SPEC, VMEM_SPEC]),
        scratch_shapes=[
            pltpu.SemaphoreType.DMA((7, 3)), pltpu.SemaphoreType.DMA((7, 3)), pltpu.SemaphoreType.DMA((7,)),
            pltpu.SemaphoreType.DMA((7,)), pltpu.SemaphoreType.DMA((7,)),
            pltpu.SemaphoreType.DMA((3,)), pltpu.SemaphoreType.DMA((3,)),
            pltpu.SemaphoreType.DMA((6, 3)), pltpu.SemaphoreType.DMA((6, 3)),
        ],
        compiler_params=pltpu.CompilerParams(vmem_limit_bytes=VMEM_LIMIT_V7X),
    )(c_row, w_ada_s, b_w_in.reshape(2, D_MODEL // 2, SHARD_IN),
      b_wap.reshape(2, SHARD_ROWS // 2, D_MODEL), b_wrp.reshape(2, SHARD_ROWS // 2, D_MODEL),
      b_wo.reshape(2, SHARD_ROWS // 2, D_MODEL), b_rwa.reshape(RNN_BLOCKS, 2, SHARD_RG // 2, RNN_BW),
      b_rwx.reshape(RNN_BLOCKS, 2, SHARD_RG // 2, RNN_BW), conv_w_s)


def _exchange_partials(g_in, g_ap, g_rp, g_o, g_wa, g_wx, small):
    def body(gin, gap, grp_, go, gwa, gwx, small_ref, lin, lap, lrp, lo, lwa, lwx, small_all,
             gsend, grecv, lsem, ssend, srecv):
        me = _me()
        my_chip = _chip_of(me)
        my_dev = my_chip * 2 + me[2]
        srcs = (gin, gap, grp_, go, gwa, gwx)
        lands = (lin, lap, lrp, lo, lwa, lwx)

        def shard(idx, chip):
            if idx == 0:
                return srcs[idx].at[:, pl.ds(pl.multiple_of(chip * SHARD_IN, 128), SHARD_IN)]
            if idx in (1, 2, 3):
                return srcs[idx].at[chip]
            return srcs[idx].at[:, chip]

        sends = []
        for idx in range(6):
            for k, mask in enumerate(CHIP_MASKS):
                to = _peer(mask)
                cp = pltpu.make_async_remote_copy(
                    src_ref=shard(idx, _chip_of(to)), dst_ref=lands[idx].at[k],
                    send_sem=gsend.at[idx, k], recv_sem=grecv.at[idx, k], device_id=to, device_id_type=MESH)
                cp.start()
                sends.append(cp)
        local = [pltpu.make_async_copy(shard(idx, my_chip), lands[idx].at[3], lsem.at[idx]) for idx in range(6)]
        for cp in local:
            cp.start()

        small_all[my_dev] = small_ref[...]
        ssends = []
        for k, mask in enumerate(ALL_MASKS):
            cp = pltpu.make_async_remote_copy(
                src_ref=small_ref, dst_ref=small_all.at[my_dev],
                send_sem=ssend.at[k], recv_sem=srecv.at[k], device_id=_peer(mask), device_id_type=MESH)
            cp.start()
            ssends.append(cp)
        for k, mask in enumerate(ALL_MASKS):
            frm = _peer(mask)
            pltpu.make_async_remote_copy(
                src_ref=small_ref, dst_ref=small_all.at[_chip_of(frm) * 2 + frm[2]],
                send_sem=ssend.at[k], recv_sem=srecv.at[k], device_id=frm, device_id_type=MESH).wait_recv()
        for cp in ssends:
            cp.wait_send()

        for idx in range(6):
            for k, mask in enumerate(CHIP_MASKS):
                frm = _peer(mask)
                pltpu.make_async_remote_copy(
                    src_ref=shard(idx, my_chip), dst_ref=lands[idx].at[k],
                    send_sem=gsend.at[idx, k], recv_sem=grecv.at[idx, k], device_id=frm, device_id_type=MESH).wait_recv()
        for cp in sends:
            cp.wait_send()
        for cp in local:
            cp.wait()

    out_shape = (
        jax.ShapeDtypeStruct((4, D_MODEL // 2, SHARD_IN), F32),
        jax.ShapeDtypeStruct((4, SHARD_ROWS // 2, D_MODEL), F32),
        jax.ShapeDtypeStruct((4, SHARD_ROWS // 2, D_MODEL), F32),
        jax.ShapeDtypeStruct((4, SHARD_ROWS // 2, D_MODEL), F32),
        jax.ShapeDtypeStruct((4, RNN_BLOCKS, SHARD_RG // 2, RNN_BW), F32),
        jax.ShapeDtypeStruct((4, RNN_BLOCKS, SHARD_RG // 2, RNN_BW), F32),
        jax.ShapeDtypeStruct((N_DEV, SMALL_ROWS, D_MODEL), F32),
    )
    return pl.pallas_call(
        body, out_shape=out_shape, name="exchange_partials",
        in_specs=[ANY] * 6 + [VMEM_SPEC],
        out_specs=tuple([ANY] * 6 + [VMEM_SPEC]),
        scratch_shapes=[
            pltpu.SemaphoreType.DMA((6, 3)), pltpu.SemaphoreType.DMA((6, 3)), pltpu.SemaphoreType.DMA((6,)),
            pltpu.SemaphoreType.DMA((7,)), pltpu.SemaphoreType.DMA((7,)),
        ],
        compiler_params=pltpu.CompilerParams(vmem_limit_bytes=VMEM_LIMIT_V7X),
    )(g_in, g_ap, g_rp, g_o, g_wa, g_wx, small)


def _half_of(ref, axis, half):
    return ref.at[(slice(None),) * axis + (half,)]


def _swap_halves(parts, axes):
    n = len(parts)

    def body(*refs):
        ins, outs, ssem, rsem = refs[:n], refs[n:2 * n], refs[2 * n], refs[2 * n + 1]
        c = lax.axis_index("c")
        cps = [pltpu.make_async_remote_copy(src_ref=_half_of(ins[i], axes[i], 1 - c), dst_ref=outs[i], send_sem=ssem.at[i],
                                            recv_sem=rsem.at[i], device_id=_peer(1), device_id_type=MESH) for i in range(n)]
        for cp in cps:
            cp.start()
        for cp in cps:
            cp.wait()

    shapes = [p.shape[:a] + p.shape[a + 1:] for p, a in zip(parts, axes)]
    return pl.pallas_call(
        body, out_shape=tuple(jax.ShapeDtypeStruct(s, p.dtype) for s, p in zip(shapes, parts)), name="swap_halves",
        in_specs=[ANY] * n, out_specs=tuple([ANY] * n),
        scratch_shapes=[pltpu.SemaphoreType.DMA((n,)), pltpu.SemaphoreType.DMA((n,))],
    )(*parts)


def _presum(mine, sib, c_idx, tag):
    S, _, R, C = mine.shape
    tr = min(R, 256)
    tc = SHARD_IN if C % SHARD_IN == 0 else C

    def body(c_ref, m_ref, s_ref, o_ref):
        o_ref[...] = m_ref[:, 0] + s_ref[...]

    grid_spec = pltpu.PrefetchScalarGridSpec(
        num_scalar_prefetch=1, grid=(R // tr, C // tc),
        in_specs=[pl.BlockSpec((S, 1, tr, tc), lambda i, j, c_ref: (0, c_ref[0], i, j)),
                  pl.BlockSpec((S, tr, tc), lambda i, j, c_ref: (0, i, j))],
        out_specs=pl.BlockSpec((S, tr, tc), lambda i, j, c_ref: (0, i, j)))
    return pl.pallas_call(
        body, out_shape=jax.ShapeDtypeStruct((S, R, C), F32), grid_spec=grid_spec, name=f"presum_{tag}",
        compiler_params=_cp("parallel", "parallel"),
    )(c_idx, mine, sib)


def _assemble_with_sibling(parts, axes):
    n = len(parts)

    def body(*refs):
        ins, outs, ssem, rsem, lsem = refs[:n], refs[n:2 * n], refs[2 * n], refs[2 * n + 1], refs[2 * n + 2]
        c = lax.axis_index("c")
        cps = [pltpu.make_async_remote_copy(src_ref=ins[i], dst_ref=_half_of(outs[i], axes[i], c), send_sem=ssem.at[i],
                                            recv_sem=rsem.at[i], device_id=_peer(1), device_id_type=MESH) for i in range(n)]
        loc = [pltpu.make_async_copy(ins[i], _half_of(outs[i], axes[i], c), lsem.at[i]) for i in range(n)]
        for cp in cps + loc:
            cp.start()
        for i in range(n):
            pltpu.make_async_remote_copy(src_ref=ins[i], dst_ref=_half_of(outs[i], axes[i], 1 - c), send_sem=ssem.at[i],
                                         recv_sem=rsem.at[i], device_id=_peer(1), device_id_type=MESH).wait_recv()
        for cp in cps:
            cp.wait_send()
        for cp in loc:
            cp.wait()

    shapes = [p.shape[:a] + (2,) + p.shape[a:] for p, a in zip(parts, axes)]
    return pl.pallas_call(
        body, out_shape=tuple(jax.ShapeDtypeStruct(s, p.dtype) for s, p in zip(shapes, parts)), name="assemble_with_sibling",
        in_specs=[ANY] * n, out_specs=tuple([ANY] * n),
        scratch_shapes=[pltpu.SemaphoreType.DMA((n,)), pltpu.SemaphoreType.DMA((n,)), pltpu.SemaphoreType.DMA((n,))],
    )(*parts)


def _rope_tables(pos_col):
    T = pos_col.shape[0]
    tm = min(T, 512)
    inv = np.float32(ROPE_THETA) ** (-(np.arange(0, 2 * ROT_HALF, 2, dtype=np.float32)) / np.float32(2 * ROT_HALF))
    lane = np.arange(128) % HEAD_DIM
    freq = np.where(lane < 2 * ROT_HALF, inv[lane % ROT_HALF], 0.0).astype(np.float32)[None, :]

    def body(pos_ref, f_ref, c_ref, sa_ref, sb_ref):
        ang = pos_ref[...].astype(F32) * f_ref[...]
        c, s = jnp.cos(ang), jnp.sin(ang)
        m = lax.broadcasted_iota(jnp.int32, ang.shape, 1) & (HEAD_DIM - 1)
        c_ref[...] = jnp.where(m < 2 * ROT_HALF, c, 1.0)
        sa_ref[...] = jnp.where(m < ROT_HALF, -s, 0.0)
        sb_ref[...] = jnp.where((m >= ROT_HALF) & (m < 2 * ROT_HALF), s, 0.0)

    tab = jax.ShapeDtypeStruct((T, 128), F32)
    return pl.pallas_call(
        body, out_shape=(tab, tab, tab), grid=(T // tm,), name="rope_tables",
        in_specs=[pl.BlockSpec((tm, 1), lambda i: (i, 0)), pl.BlockSpec((1, 128), lambda i: (0, 0))],
        out_specs=tuple(pl.BlockSpec((tm, 128), lambda i: (i, 0)) for _ in range(3)),
        compiler_params=_cp("parallel"),
    )(pos_col, jnp.asarray(freq))


def _wide(tab, width):
    return jnp.concatenate([tab] * (width // 128), axis=1)


def _rope(t, c, sa, sb):
    w = t.shape[-1]
    return t * c + pltpu.roll(t, w - ROT_HALF, 1) * sa + pltpu.roll(t, ROT_HALF, 1) * sb


def _unrope(d, c, sa, sb):
    w = d.shape[-1]
    return d * c + pltpu.roll(d * sa, ROT_HALF, 1) + pltpu.roll(d * sb, w - ROT_HALF, 1)


def _prenorm(x, mod_row, norm_g):
    T = x.shape[0]
    tm = min(T, 512)

    def body(x_ref, mod_ref, g_ref, h_ref):
        xf = x_ref[...]
        shift, scale = mod_ref[:, 0:D_MODEL], mod_ref[:, D_MODEL:2 * D_MODEL]
        h = (xf * _rms(xf)) * g_ref[...] * (1.0 + scale) + shift
        h_ref[...] = h.astype(BF16)

    return pl.pallas_call(
        body, out_shape=jax.ShapeDtypeStruct((T, D_MODEL), BF16), grid=(T // tm,), name="prenorm",
        in_specs=[pl.BlockSpec((tm, D_MODEL), lambda i: (i, 0)), pl.BlockSpec((1, ADA_W), lambda i: (0, 0)),
                  pl.BlockSpec((1, D_MODEL), lambda i: (0, 0))],
        out_specs=pl.BlockSpec((tm, D_MODEL), lambda i: (i, 0)),
        compiler_params=_cp("parallel"),
    )(x, mod_row, norm_g)


def _in_projection(h, w_in):
    T = h.shape[0]
    tm, tn = min(T, 512), SHARD_IN

    def body(h_ref, w_ref, o_ref):
        o_ref[...] = _dot(h_ref[...], w_ref[...])

    return pl.pallas_call(
        body, out_shape=jax.ShapeDtypeStruct((T, IN_W), F32), grid=(IN_W // tn, T // tm), name="in_projection",
        in_specs=[pl.BlockSpec((tm, D_MODEL), lambda j, i: (i, 0)), pl.BlockSpec((D_MODEL, tn), lambda j, i: (0, j))],
        out_specs=pl.BlockSpec((tm, tn), lambda j, i: (i, j)),
        compiler_params=_cp("parallel", "parallel"),
    )(h, w_in)


def _attn_mask(n):
    qi = lax.broadcasted_iota(jnp.int32, (GROUP * BLOCK, 2 * BLOCK), 0) & (BLOCK - 1)
    kj = lax.broadcasted_iota(jnp.int32, (GROUP * BLOCK, 2 * BLOCK), 1)
    diff = qi + BLOCK - kj
    return (diff >= 0) & (diff < BLOCK) & ((kj >= BLOCK) | (n > 0))


def _sink_col(sink_ref, kh):
    rowg = lax.broadcasted_iota(jnp.int32, (GROUP * BLOCK, 1), 0) // BLOCK
    col = jnp.full((GROUP * BLOCK, 1), sink_ref[0, GROUP * kh], F32)
    for g in range(1, GROUP):
        col = jnp.where(rowg == g, sink_ref[0, GROUP * kh + g], col)
    return col


def _attn_probs(qr, kr_prev, kr_cur, v_prev, v_cur, kh, sink_col, mask):
    heads = [qr[:, HEAD_DIM * (GROUP * kh + g): HEAD_DIM * (GROUP * kh + g + 1)] for g in range(GROUP)]
    qs = jnp.concatenate(heads, axis=0).astype(BF16)
    lo, hi = HEAD_DIM * kh, HEAD_DIM * (kh + 1)
    kk = jnp.concatenate([kr_prev[:, lo:hi], kr_cur[:, lo:hi]], axis=0).astype(BF16)
    vv = jnp.concatenate([v_prev[:, lo:hi], v_cur[:, lo:hi]], axis=0).astype(BF16)
    s = _dot_nt(qs, kk) * (1.0 / 8.0)
    s = jnp.where(mask, s, -1e30)
    m = jnp.maximum(jnp.max(s, axis=-1, keepdims=True), sink_col)
    p = jnp.exp(s - m)
    p_sink = jnp.exp(sink_col - m)
    denom = jnp.sum(p, axis=-1, keepdims=True) + p_sink
    return qs, kk, vv, p / denom, p_sink / denom


def _unstack_heads(parts):
    cols = []
    for kh in range(N_KV):
        for g in range(GROUP):
            cols.append(parts[kh][g * BLOCK:(g + 1) * BLOCK, :])
    return jnp.concatenate(cols, axis=1)


def _attn_forward(proj, tabs, sinks):
    T = proj.shape[0]
    nb = T // BLOCK

    def body(q_ref, kvc_ref, kvp_ref, g0_ref, g1_ref, cc, sac, sbc, cp_, sap, sbp, sink_ref, y_ref):
        n = pl.program_id(0)
        tc = (_wide(cc[...], D_MODEL), _wide(sac[...], D_MODEL), _wide(sbc[...], D_MODEL))
        tcur = tuple(t[:, :KV_W] for t in tc)
        tprev = (_wide(cp_[...], KV_W), _wide(sap[...], KV_W), _wide(sbp[...], KV_W))
        qr = _rope(q_ref[...], *tc)
        kr_cur = _rope(kvc_ref[:, 0:KV_W], *tcur)
        kr_prev = _rope(kvp_ref[:, 0:KV_W], *tprev)
        v_cur, v_prev = kvc_ref[:, KV_W:2 * KV_W], kvp_ref[:, KV_W:2 * KV_W]
        mask = _attn_mask(n)
        outs = []
        for kh in range(N_KV):
            _, _, vv, pn, _ = _attn_probs(qr, kr_prev, kr_cur, v_prev, v_cur, kh, _sink_col(sink_ref, kh), mask)
            outs.append(_dot(pn.astype(BF16), vv))
        o = _unstack_heads(outs)
        g = jnp.concatenate([g0_ref[...], g1_ref[...]], axis=1)
        y_ref[...] = (o * (g * _sigmoid(g))).astype(BF16)

    def blk(w, cb):
        return pl.BlockSpec((BLOCK, w), lambda n, cb=cb: (n, cb))

    prev = lambda w, cb: pl.BlockSpec((BLOCK, w), lambda n, cb=cb: (jnp.maximum(n - 1, 0), cb))
    return pl.pallas_call(
        body, out_shape=jax.ShapeDtypeStruct((T, D_MODEL), BF16), grid=(nb,), name="attn_forward",
        in_specs=[blk(D_MODEL, 0), blk(CB, CB_KV), prev(CB, CB_KV), blk(CB, CB_GA), blk(CB, CB_GA + 1),
                  blk(128, 0), blk(128, 0), blk(128, 0), prev(128, 0), prev(128, 0), prev(128, 0),
                  pl.BlockSpec(memory_space=pltpu.SMEM)],
        out_specs=pl.BlockSpec((BLOCK, D_MODEL), lambda n: (n, 0)),
        compiler_params=_cp("parallel"),
    )(proj, proj, proj, proj, proj, *tabs, *tabs, sinks)


def _scan_rows8():
    return lax.broadcasted_iota(jnp.int32, (8, D_MODEL), 0)


def _scan_forward(a_ref, b_ref, h_ref, carry, rows):
    row = _scan_rows8()

    def group(i, carry):
        off = pl.multiple_of(i * 8, 8)
        a, b = a_ref[pl.ds(off, 8), :], b_ref[pl.ds(off, 8), :]
        for d in (1, 2, 4):
            ok = row >= d
            b = jnp.where(ok, a * pltpu.roll(b, d, 0) + b, b)
            a = jnp.where(ok, a * pltpu.roll(a, d, 0), a)
        h = a * carry + b
        h_ref[pl.ds(off, 8), :] = h
        return h[7:8, :]

    return lax.fori_loop(0, rows // 8, group, carry)


def _scan_backward(a_ref, g_ref, lam_ref, carry, rows):
    row = _scan_rows8()

    def group(i, carry):
        off = pl.multiple_of((rows // 8 - 1 - i) * 8, 8)
        a, g = a_ref[pl.ds(off, 8), :], g_ref[pl.ds(off, 8), :]
        b = a * g
        for d in (1, 2, 4):
            ok = row < 8 - d
            b = jnp.where(ok, a * pltpu.roll(b, 8 - d, 0) + b, b)
            a = jnp.where(ok, a * pltpu.roll(a, 8 - d, 0), a)
        mu = a * carry + b
        mu_below = jnp.where(row == 7, carry, pltpu.roll(mu, 7, 0))
        lam_ref[pl.ds(off, 8), :] = g + mu_below
        return mu[0:1, :]

    return lax.fori_loop(0, rows // 8, group, carry)


def _rnn_recompute(xbuf, xr, tail, cw, cb, wa_ref, wx_ref, ba, bx, sp, reset):
    rows = xr.shape[0]
    xbuf[0:8, :] = tail
    xbuf[8:rows + 8, :] = xr
    xs = [xbuf[pl.ds(8 - (CONV_W - 1 - k), rows), :] for k in range(CONV_W - 1)] + [xr]
    xc = xs[0] * cw[0:1, :]
    for k in range(1, CONV_W):
        xc = xc + xs[k] * cw[k:k + 1, :]
    xc = xc + cb
    xcb = xc.astype(BF16)
    za = jnp.concatenate([_dot(xcb[:, RNN_BW * j:RNN_BW * (j + 1)], wa_ref[j]) for j in range(RNN_BLOCKS)], axis=1) + ba
    zx = jnp.concatenate([_dot(xcb[:, RNN_BW * j:RNN_BW * (j + 1)], wx_ref[j]) for j in range(RNN_BLOCKS)], axis=1) + bx
    r, i = _sigmoid(za), _sigmoid(zx)
    log_a = -LRU_C * r * sp
    a_raw = jnp.exp(log_a)
    mult_raw = jnp.sqrt(_neg_expm1(2.0 * log_a))
    a = jnp.where(reset, 0.0, a_raw)
    mult = jnp.where(reset, 1.0, mult_raw)
    return xs, xc, xcb, r, i, a_raw, mult_raw, a, mult


def _rnn_forward(proj, pos_col, conv_w, conv_b, rwa, rwx, ba, bx, lam):
    T = proj.shape[0]
    tr = min(T, 256)

    def body(x0, x1, g0, g1, pos_ref, cw_ref, cb_ref, wa_ref, wx_ref, ba_ref, bx_ref, lam_ref,
             y_ref, h_ref, xbuf, abuf, bbuf, tail, carry):
        t = pl.program_id(0)

        @pl.when(t == 0)
        def _():
            tail[...] = jnp.zeros_like(tail)
            carry[...] = jnp.zeros_like(carry)

        xr = jnp.concatenate([x0[...], x1[...]], axis=1)
        sp = _softplus(-lam_ref[...])
        reset = pos_ref[...] == 0
        _, xc, _, _, i, _, _, a, mult = _rnn_recompute(
            xbuf, xr, tail[...], cw_ref[...], cb_ref[...], wa_ref, wx_ref, ba_ref[...], bx_ref[...], sp, reset)
        abuf[...] = a
        bbuf[...] = mult * (i * xc)
        last = _scan_forward(abuf, bbuf, h_ref, carry[0:1, :], tr)
        carry[...] = jnp.broadcast_to(last, carry.shape)
        tail[...] = xr[tr - 8:tr, :]
        g = jnp.concatenate([g0[...], g1[...]], axis=1)
        y_ref[...] = (h_ref[...] * (g * _sigmoid(g))).astype(BF16)

    blk = lambda cb: pl.BlockSpec((tr, CB), lambda t, cb=cb: (t, cb))
    row = lambda w: pl.BlockSpec((1, w), lambda t: (0, 0))
    full3 = pl.BlockSpec((RNN_BLOCKS, RNN_BW, RNN_BW), lambda t: (0, 0, 0))
    return pl.pallas_call(
        body, out_shape=(jax.ShapeDtypeStruct((T, D_MODEL), BF16), jax.ShapeDtypeStruct((T, D_MODEL), F32)),
        grid=(T // tr,), name="rnn_forward",
        in_specs=[blk(CB_XR), blk(CB_XR + 1), blk(CB_GR), blk(CB_GR + 1), pl.BlockSpec((tr, 1), lambda t: (t, 0)),
                  pl.BlockSpec((CONV_W, D_MODEL), lambda t: (0, 0)), row(D_MODEL), full3, full3,
                  row(D_MODEL), row(D_MODEL), row(D_MODEL)],
        out_specs=(pl.BlockSpec((tr, D_MODEL), lambda t: (t, 0)), pl.BlockSpec((tr, D_MODEL), lambda t: (t, 0))),
        scratch_shapes=[pltpu.VMEM((tr + 8, D_MODEL), F32), pltpu.VMEM((tr, D_MODEL), F32), pltpu.VMEM((tr, D_MODEL), F32),
                        pltpu.VMEM((8, D_MODEL), F32), pltpu.VMEM((8, D_MODEL), F32)],
        compiler_params=_cp("arbitrary"),
    )(proj, proj, proj, proj, pos_col, conv_w, conv_b, rwa, rwx, ba, bx, lam)


def _merge_and_head(x, target, y_attn, y_rnn, proj, wap, wrp, wo, mod_row, final_g):
    T = x.shape[0]
    tm = min(T, 256)

    def body(x_ref, t_ref, ya_ref, yr_ref, ma0, ma1, mr0, mr1, wap_ref, wrp_ref, wo_ref, mod_ref, fg_ref,
             dx2_ref, mg_ref, do_ref, dpa_ref, dpr_ref, dya_ref, dyr_ref, dc_ref, dfg_ref, dgate_ref, loss_ref):
        i = pl.program_id(0)
        gate = mod_ref[:, 2 * D_MODEL:3 * D_MODEL]
        ya, yr = ya_ref[...], yr_ref[...]
        pa, pr = _dot(ya, wap_ref[...]), _dot(yr, wrp_ref[...])
        sa = _sigmoid(jnp.concatenate([ma0[...], ma1[...]], axis=1))
        sr = _sigmoid(jnp.concatenate([mr0[...], mr1[...]], axis=1))
        merged = sa * pa + sr * pr
        mb = merged.astype(BF16)
        o = _dot(mb, wo_ref[...])
        x2 = x_ref[...] + gate * o
        r2 = _rms(x2)
        xn2 = x2 * r2
        fg = fg_ref[...]
        err = xn2 * fg - t_ref[...]
        loss_t = 0.5 * jnp.sum(jnp.sum(err * err, axis=-1, keepdims=True) * (1.0 / D_MODEL), axis=0, keepdims=True)
        dy = err * (1.0 / D_MODEL)
        dfg_t = jnp.sum(dy * xn2, axis=0, keepdims=True)
        dxn = dy * fg
        dx2 = r2 * (dxn - xn2 * jnp.mean(dxn * xn2, axis=-1, keepdims=True))
        dgate_t = jnp.sum(dx2 * o, axis=0, keepdims=True)
        dob = (dx2 * gate).astype(BF16)
        dmerged = _dot_nt(dob, wo_ref[...])
        dpa = (dmerged * sa).astype(BF16)
        dpr = (dmerged * sr).astype(BF16)
        dx2_ref[...] = dx2
        mg_ref[...] = mb
        do_ref[...] = dob
        dpa_ref[...] = dpa
        dpr_ref[...] = dpr
        dya_ref[...] = _dot_nt(dpa, wap_ref[...])
        dyr_ref[...] = _dot_nt(dpr, wrp_ref[...])
        dc_ref[:, 0:D_MODEL] = (dmerged * pa * sa * (1.0 - sa)).astype(BF16)
        dc_ref[:, D_MODEL:2 * D_MODEL] = (dmerged * pr * sr * (1.0 - sr)).astype(BF16)

        @pl.when(i == 0)
        def _():
            dfg_ref[...] = jnp.zeros_like(dfg_ref)
            dgate_ref[...] = jnp.zeros_like(dgate_ref)
            loss_ref[...] = jnp.zeros_like(loss_ref)

        dfg_ref[...] += dfg_t
        dgate_ref[...] += dgate_t
        loss_ref[...] += jnp.broadcast_to(loss_t, loss_ref.shape)

    tok = lambda w: pl.BlockSpec((tm, w), lambda i: (i, 0))
    blk = lambda cb: pl.BlockSpec((tm, CB), lambda i, cb=cb: (i, cb))
    wfull = pl.BlockSpec((D_MODEL, D_MODEL), lambda i: (0, 0))
    row = lambda w: pl.BlockSpec((1, w), lambda i: (0, 0))
    out_shape = (
        jax.ShapeDtypeStruct((T, D_MODEL), F32), jax.ShapeDtypeStruct((T, D_MODEL), BF16),
        jax.ShapeDtypeStruct((T, D_MODEL), BF16), jax.ShapeDtypeStruct((T, D_MODEL), BF16),
        jax.ShapeDtypeStruct((T, D_MODEL), BF16), jax.ShapeDtypeStruct((T, D_MODEL), F32),
        jax.ShapeDtypeStruct((T, D_MODEL), F32), jax.ShapeDtypeStruct((T, 2 * D_MODEL), BF16),
        jax.ShapeDtypeStruct((1, D_MODEL), F32), jax.ShapeDtypeStruct((1, D_MODEL), F32),
        jax.ShapeDtypeStruct((1, 128), F32),
    )
    return pl.pallas_call(
        body, out_shape=out_shape, grid=(T // tm,), name="merge_and_head",
        in_specs=[tok(D_MODEL), tok(D_MODEL), tok(D_MODEL), tok(D_MODEL), blk(CB_MA), blk(CB_MA + 1), blk(CB_MR),
                  blk(CB_MR + 1), wfull, wfull, wfull, row(ADA_W), row(D_MODEL)],
        out_specs=(tok(D_MODEL),) * 7 + (tok(2 * D_MODEL), row(D_MODEL), row(D_MODEL), row(128)),
        compiler_params=_cp("arbitrary"),
    )(x, target, y_attn, y_rnn, proj, proj, proj, proj, wap, wrp, wo, mod_row, final_g)


def _attn_backward(proj, d_y, tabs, sinks):
    T = proj.shape[0]
    nb = T // BLOCK

    def body(q_ref, kvc_ref, kvp_ref, g0_ref, g1_ref, dy_ref, cc, sac, sbc, cp_, sap, sbp, sink_ref,
             dq_ref, dkv_ref, dg_ref, dsink_ref, carry):
        n = pl.program_id(0)

        @pl.when(n == 0)
        def _():
            carry[...] = jnp.zeros_like(carry)
            dsink_ref[...] = jnp.zeros_like(dsink_ref)

        @pl.when(n < nb)
        def _():
            tc = (_wide(cc[...], D_MODEL), _wide(sac[...], D_MODEL), _wide(sbc[...], D_MODEL))
            tcur = tuple(t[:, :KV_W] for t in tc)
            tprev = (_wide(cp_[...], KV_W), _wide(sap[...], KV_W), _wide(sbp[...], KV_W))
            qr = _rope(q_ref[...], *tc)
            kr_cur = _rope(kvc_ref[:, 0:KV_W], *tcur)
            kr_prev = _rope(kvp_ref[:, 0:KV_W], *tprev)
            v_cur, v_prev = kvc_ref[:, KV_W:2 * KV_W], kvp_ref[:, KV_W:2 * KV_W]
            g = jnp.concatenate([g0_ref[...], g1_ref[...]], axis=1)
            sg = _sigmoid(g)
            dy = dy_ref[...]
            d_o = dy * (g * sg)
            mask = _attn_mask(n)
            lane = lax.broadcasted_iota(jnp.int32, (1, 128), 1)
            rowg = lax.broadcasted_iota(jnp.int32, (GROUP * BLOCK, 1), 0) // BLOCK
            o_parts, dq_parts, dk_parts, dv_parts = [], [], [], []
            dsink = jnp.zeros((1, 128), F32)
            for kh in range(N_KV):
                qs, kk, vv, pn, pn_sink = _attn_probs(qr, kr_prev, kr_cur, v_prev, v_cur, kh, _sink_col(sink_ref, kh), mask)
                pnb = pn.astype(BF16)
                o_parts.append(_dot(pnb, vv))
                dos = jnp.concatenate(
                    [d_o[:, HEAD_DIM * (GROUP * kh + gq): HEAD_DIM * (GROUP * kh + gq + 1)] for gq in range(GROUP)],
                    axis=0).astype(BF16)
                dpn = _dot_nt(dos, vv)
                delta = jnp.sum(pn * dpn, axis=-1, keepdims=True)
                dsb = (pn * (dpn - delta) * (1.0 / 8.0)).astype(BF16)
                dq_parts.append(_dot(dsb, kk))
                dk_parts.append(_dot_tn(dsb, qs))
                dv_parts.append(_dot_tn(pnb, dos))
                ds_rows = pn_sink * delta
                for gq in range(GROUP):
                    val = -jnp.sum(jnp.where(rowg == gq, ds_rows, 0.0), axis=0, keepdims=True)
                    dsink = dsink + jnp.where(lane == GROUP * kh + gq, val, 0.0)
            o = _unstack_heads(o_parts)
            dg_ref[...] = (dy * o * (sg * (1.0 + g * (1.0 - sg)))).astype(BF16)
            dq_ref[...] = _unrope(_unstack_heads(dq_parts), *tc).astype(BF16)
            dk_prev = _unrope(jnp.concatenate([p[0:BLOCK, :] for p in dk_parts], axis=1), *tprev)
            dk_cur = _unrope(jnp.concatenate([p[BLOCK:2 * BLOCK, :] for p in dk_parts], axis=1), *tcur)
            dv_prev = jnp.concatenate([p[0:BLOCK, :] for p in dv_parts], axis=1)
            dv_cur = jnp.concatenate([p[BLOCK:2 * BLOCK, :] for p in dv_parts], axis=1)
            dkv_ref[...] = (carry[...] + jnp.concatenate([dk_prev, dv_prev], axis=1)).astype(BF16)
            carry[...] = jnp.concatenate([dk_cur, dv_cur], axis=1)
            dsink_ref[...] += dsink

        @pl.when(n == nb)
        def _():
            dkv_ref[...] = carry[...].astype(BF16)

    cur = lambda w, cb: pl.BlockSpec((BLOCK, w), lambda n, cb=cb: (jnp.minimum(n, nb - 1), cb))
    prev = lambda w, cb: pl.BlockSpec((BLOCK, w), lambda n, cb=cb: (jnp.maximum(jnp.minimum(n, nb - 1) - 1, 0), cb))
    out_shape = (jax.ShapeDtypeStruct((T, D_MODEL), BF16), jax.ShapeDtypeStruct((T, 2 * KV_W), BF16),
                 jax.ShapeDtypeStruct((T, D_MODEL), BF16), jax.ShapeDtypeStruct((1, 128), F32))
    return pl.pallas_call(
        body, out_shape=out_shape, grid=(nb + 1,), name="attn_backward",
        in_specs=[cur(D_MODEL, 0), cur(CB, CB_KV), prev(CB, CB_KV), cur(CB, CB_GA), cur(CB, CB_GA + 1), cur(D_MODEL, 0),
                  cur(128, 0), cur(128, 0), cur(128, 0), prev(128, 0), prev(128, 0), prev(128, 0),
                  pl.BlockSpec(memory_space=pltpu.SMEM)],
        out_specs=(cur(D_MODEL, 0), pl.BlockSpec((BLOCK, 2 * KV_W), lambda n: (jnp.maximum(n - 1, 0), 0)),
                   cur(D_MODEL, 0), pl.BlockSpec((1, 128), lambda n: (0, 0))),
        scratch_shapes=[pltpu.VMEM((BLOCK, 2 * KV_W), F32)],
        compiler_params=_cp("arbitrary"),
    )(proj, proj, proj, proj, proj, d_y, *tabs, *tabs, sinks)


def _rnn_backward(proj, pos_col, h_rnn, d_y, conv_w, conv_b, rwa, rwx, ba, bx, lam):
    T = proj.shape[0]
    tr = min(T, 256)
    nt = T // tr
    hb = tr // 8

    def body(x0, x1, xh0, xh1, g0, g1, pos_ref, h_ref, hh_ref, dy_ref, cw_ref, cb_ref, wa_ref, wx_ref, ba_ref, bx_ref,
             lam_ref, db_ref, dcw_ref, dcb_ref, dwa_ref, dwx_ref, dba_ref, dbx_ref, dlam_ref,
             xbuf, hbuf, dbuf, abuf, gbuf, lbuf, mu_carry, dxc_head):
        step = pl.program_id(0)
        first_tile = step == nt - 1

        @pl.when(step == 0)
        def _():
            mu_carry[...] = jnp.zeros_like(mu_carry)
            dxc_head[...] = jnp.zeros_like(dxc_head)
            for ref in (dcw_ref, dcb_ref, dwa_ref, dwx_ref, dba_ref, dbx_ref, dlam_ref):
                ref[...] = jnp.zeros_like(ref)

        xr = jnp.concatenate([x0[...], x1[...]], axis=1)
        tail = jnp.where(first_tile, 0.0, jnp.concatenate([xh0[...], xh1[...]], axis=1))
        lam_v = lam_ref[...]
        sp = _softplus(-lam_v)
        reset = pos_ref[...] == 0
        cw = cw_ref[...]
        xs, xc, xcb, r, i, a_raw, mult_raw, a, mult = _rnn_recompute(
            xbuf, xr, tail, cw, cb_ref[...], wa_ref, wx_ref, ba_ref[...], bx_ref[...], sp, reset)
        g = jnp.concatenate([g0[...], g1[...]], axis=1)
        sg = _sigmoid(g)
        dy = dy_ref[...]
        h = h_ref[...]
        d_g = dy * h * (sg * (1.0 + g * (1.0 - sg)))
        abuf[...] = a
        gbuf[...] = dy * (g * sg)
        top = _scan_backward(abuf, gbuf, lbuf, mu_carry[0:1, :], tr)
        mu_carry[...] = jnp.broadcast_to(top, mu_carry.shape)
        lam_t = lbuf[...]
        hbuf[0:8, :] = jnp.where(first_tile, 0.0, hh_ref[...])
        hbuf[8:tr + 8, :] = h
        h_prev = hbuf[pl.ds(7, tr), :]
        live = jnp.logical_not(reset)
        d_a = jnp.where(live, lam_t * h_prev, 0.0)
        d_mult = jnp.where(live, lam_t * (i * xc), 0.0)
        d_ixc = lam_t * mult
        d_i = d_ixc * xc
        d_xc = d_ixc * i
        d_log_a = d_a * a_raw - d_mult * (a_raw * a_raw / mult_raw)
        d_log_a = jnp.where(live, d_log_a, 0.0)
        d_za = d_log_a * (-LRU_C * sp) * (r * (1.0 - r))
        d_zx = d_i * (i * (1.0 - i))
        dlam_ref[...] += jnp.sum(d_log_a * r, axis=0, keepdims=True) * (LRU_C * _sigmoid(-lam_v))
        dba_ref[...] += jnp.sum(d_za, axis=0, keepdims=True)
        dbx_ref[...] += jnp.sum(d_zx, axis=0, keepdims=True)
        dzab, dzxb = d_za.astype(BF16), d_zx.astype(BF16)
        back = []
        for j in range(RNN_BLOCKS):
            sl = slice(RNN_BW * j, RNN_BW * (j + 1))
            dwa_ref[j] += _dot_tn(xcb[:, sl], dzab[:, sl])
            dwx_ref[j] += _dot_tn(xcb[:, sl], dzxb[:, sl])
            back.append(_dot_nt(dzab[:, sl], wa_ref[j]) + _dot_nt(dzxb[:, sl], wx_ref[j]))
        d_xc = d_xc + jnp.concatenate(back, axis=1)
        dcb_ref[...] += jnp.sum(d_xc, axis=0, keepdims=True)
        for k in range(CONV_W):
            dcw_ref[k:k + 1, :] += jnp.sum(d_xc * xs[k], axis=0, keepdims=True)
        dbuf[0:tr, :] = d_xc
        dbuf[tr:tr + 8, :] = dxc_head[...]
        d_xr = d_xc * cw[CONV_W - 1:CONV_W, :]
        for k in range(CONV_W - 1):
            d_xr = d_xr + dbuf[pl.ds(CONV_W - 1 - k, tr), :] * cw[k:k + 1, :]
        dxc_head[...] = d_xc[0:8, :]
        db_ref[:, 0:D_MODEL] = d_xr.astype(BF16)
        db_ref[:, D_MODEL:2 * D_MODEL] = d_g.astype(BF16)

    rev = lambda s: nt - 1 - s
    blk = lambda cb: pl.BlockSpec((tr, CB), lambda s, cb=cb: (rev(s), cb))
    halo = lambda w, cb: pl.BlockSpec((8, w), lambda s, cb=cb: (jnp.maximum(rev(s) * hb - 1, 0), cb))
    tok = lambda w: pl.BlockSpec((tr, w), lambda s: (rev(s), 0))
    row = lambda w: pl.BlockSpec((1, w), lambda s: (0, 0))
    full3 = pl.BlockSpec((RNN_BLOCKS, RNN_BW, RNN_BW), lambda s: (0, 0, 0))
    cwspec = pl.BlockSpec((CONV_W, D_MODEL), lambda s: (0, 0))
    vec = jax.ShapeDtypeStruct((1, D_MODEL), F32)
    gate_w = jax.ShapeDtypeStruct((RNN_BLOCKS, RNN_BW, RNN_BW), F32)
    out_shape = (jax.ShapeDtypeStruct((T, 2 * D_MODEL), BF16), jax.ShapeDtypeStruct((CONV_W, D_MODEL), F32), vec,
                 gate_w, gate_w, vec, vec, vec)
    big = lambda: pltpu.VMEM((tr, D_MODEL), F32)
    ext = lambda: pltpu.VMEM((tr + 8, D_MODEL), F32)
    return pl.pallas_call(
        body, out_shape=out_shape, grid=(nt,), name="rnn_backward",
        in_specs=[blk(CB_XR), blk(CB_XR + 1), halo(CB, CB_XR), halo(CB, CB_XR + 1), blk(CB_GR), blk(CB_GR + 1),
                  pl.BlockSpec((tr, 1), lambda s: (rev(s), 0)), tok(D_MODEL), halo(D_MODEL, 0), tok(D_MODEL),
                  cwspec, row(D_MODEL), full3, full3, row(D_MODEL), row(D_MODEL), row(D_MODEL)],
        out_specs=(tok(2 * D_MODEL), cwspec, row(D_MODEL), full3, full3, row(D_MODEL), row(D_MODEL), row(D_MODEL)),
        scratch_shapes=[ext(), ext(), ext(), big(), big(), big(), pltpu.VMEM((8, D_MODEL), F32), pltpu.VMEM((8, D_MODEL), F32)],
        compiler_params=_cp("arbitrary"),
    )(proj, proj, proj, proj, proj, proj, pos_col, h_rnn, h_rnn, d_y, conv_w, conv_b, rwa, rwx, ba, bx, lam)


def _input_backward(pieces, w_in, x, dx2, mod_row, norm_g):
    T = x.shape[0]
    tm = min(T, 512)
    n = len(pieces)

    def body(*refs):
        d_refs = refs[:n]
        w_ref, x_ref, dx2_ref, mod_ref, g_ref, gx_ref, dshift_ref, dscale_ref, dg_ref, acc = refs[n:]
        i, k = pl.program_id(0), pl.program_id(1)

        @pl.when(k == 0)
        def _():
            acc[...] = jnp.zeros_like(acc)

        for d_ref, (_, start, count) in zip(d_refs, pieces):
            @pl.when((k >= start) & (k < start + count))
            def _(d_ref=d_ref):
                acc[...] += _dot_nt(d_ref[...], w_ref[...])

        @pl.when((i == 0) & (k == 0))
        def _():
            dshift_ref[...] = jnp.zeros_like(dshift_ref)
            dscale_ref[...] = jnp.zeros_like(dscale_ref)
            dg_ref[...] = jnp.zeros_like(dg_ref)

        @pl.when(k == N_CB - 1)
        def _():
            dh = acc[...]
            xf = x_ref[...]
            r1 = _rms(xf)
            xn = xf * r1
            gn = g_ref[...]
            s1 = 1.0 + mod_ref[:, D_MODEL:2 * D_MODEL]
            dshift_ref[...] += jnp.sum(dh, axis=0, keepdims=True)
            dscale_ref[...] += jnp.sum(dh * (xn * gn), axis=0, keepdims=True)
            dg_ref[...] += jnp.sum(dh * s1 * xn, axis=0, keepdims=True)
            dxn = dh * s1 * gn
            gx_ref[...] = dx2_ref[...] + r1 * (dxn - xn * jnp.mean(dxn * xn, axis=-1, keepdims=True))

    def piece_spec(start, count):
        return pl.BlockSpec((tm, CB), lambda i, k: (i, jnp.clip(k - start, 0, count - 1)))

    tok = pl.BlockSpec((tm, D_MODEL), lambda i, k: (i, 0))
    row = lambda w: pl.BlockSpec((1, w), lambda i, k: (0, 0))
    vec = jax.ShapeDtypeStruct((1, D_MODEL), F32)
    return pl.pallas_call(
        body, out_shape=(jax.ShapeDtypeStruct((T, D_MODEL), F32), vec, vec, vec), grid=(T // tm, N_CB), name="input_backward",
        in_specs=[piece_spec(s, c) for _, s, c in pieces]
        + [pl.BlockSpec((D_MODEL, CB), lambda i, k: (0, k)), tok, tok, row(ADA_W), row(D_MODEL)],
        out_specs=(tok, row(D_MODEL), row(D_MODEL), row(D_MODEL)),
        scratch_shapes=[pltpu.VMEM((tm, D_MODEL), F32)],
        compiler_params=_cp("arbitrary", "arbitrary"),
    )(*[p[0] for p in pieces], w_in, x, dx2, mod_row, norm_g)


def _weight_grad(a, b, tag, into=None, col_block=0, total_cols=None):
    T, M = a.shape
    N = b.shape[1]
    tk = min(T, 512)
    tn = CB
    total_cols = N if total_cols is None else total_cols

    def body(*refs):
        a_ref, b_ref, o_ref = refs[0], refs[1], refs[-1]
        k = pl.program_id(1)

        @pl.when(k == 0)
        def _():
            o_ref[...] = jnp.zeros_like(o_ref)

        o_ref[...] += _dot_tn(a_ref[...], b_ref[...])

    in_specs = [pl.BlockSpec((tk, M), lambda j, k: (k, 0)), pl.BlockSpec((tk, tn), lambda j, k: (k, j))]
    args = [a, b]
    aliases = {}
    if into is not None:
        in_specs.append(ANY)
        args.append(into)
        aliases = {2: 0}
    return pl.pallas_call(
        body, out_shape=jax.ShapeDtypeStruct((M, total_cols), F32), grid=(N // tn, T // tk), name=f"weight_grad_{tag}",
        in_specs=in_specs, out_specs=pl.BlockSpec((M, tn), lambda j, k: (0, col_block + j)),
        input_output_aliases=aliases, compiler_params=_cp("parallel", "arbitrary"),
    )(*args)


def _adamw(w, g, m, v):
    m = ADAM_B1 * m + (1.0 - ADAM_B1) * g
    v = ADAM_B2 * v + (1.0 - ADAM_B2) * (g * g)
    m_hat = m / (1.0 - ADAM_B1 ** ADAM_STEP)
    v_hat = v / (1.0 - ADAM_B2 ** ADAM_STEP)
    delta = -ADAM_LR * (m_hat / (jnp.sqrt(v_hat) + ADAM_EPS) + ADAM_WD * w)
    return delta, m, v


def _sum_landed(land, tag):
    _, R, C = land.shape
    tr = min(R, 256)

    def body(l_ref, s_ref):
        s_ref[...] = ((l_ref[3] + l_ref[0]) + l_ref[1]) + l_ref[2]

    return pl.pallas_call(
        body, out_shape=jax.ShapeDtypeStruct((R, C), F32), grid=(R // tr,), name=f"sum_landed_{tag}",
        in_specs=[pl.BlockSpec((4, tr, C), lambda i: (0, i, 0))], out_specs=pl.BlockSpec((tr, C), lambda i: (i, 0)),
        compiler_params=_cp("parallel"),
    )(land)


def _adamw_shard(g, w, m, v, tag):
    R, C = w.shape
    tr = min(R, 256)

    def body(g_ref, w_ref, m_ref, v_ref, d_ref, nm_ref, nv_ref):
        d, nm, nv = _adamw(w_ref[...], g_ref[...], m_ref[...], v_ref[...])
        d_ref[...] = d
        nm_ref[...] = nm
        nv_ref[...] = nv

    spec = pl.BlockSpec((tr, C), lambda i: (i, 0))
    sds = jax.ShapeDtypeStruct((R, C), F32)
    return pl.pallas_call(
        body, out_shape=(sds,) * 3, grid=(R // tr,), name=f"adamw_{tag}",
        in_specs=[spec] * 4, out_specs=(spec,) * 3, compiler_params=_cp("parallel"),
    )(g, w, m, v)


def _adamw_w_ada(c_t, dmod_cols, w, m, v):
    R, C = w.shape

    def body(ct_ref, dm_ref, w_ref, m_ref, v_ref, g_ref, d_ref, nm_ref, nv_ref):
        g = _dot(ct_ref[...].astype(BF16), dm_ref[...].astype(BF16))
        d, nm, nv = _adamw(w_ref[...], g, m_ref[...], v_ref[...])
        g_ref[...] = g
        d_ref[...] = d
        nm_ref[...] = nm
        nv_ref[...] = nv

    tr = 256
    spec = pl.BlockSpec((tr, C), lambda i: (i, 0))
    sds = jax.ShapeDtypeStruct((R, C), F32)
    return pl.pallas_call(
        body, out_shape=(sds,) * 4, grid=(R // tr,), name="adamw_w_ada",
        in_specs=[pl.BlockSpec((tr, 128), lambda i: (i, 0)), pl.BlockSpec((128, C), lambda i: (0, 0))] + [spec] * 3,
        out_specs=(spec,) * 4, compiler_params=_cp("parallel"),
    )(c_t, dmod_cols, w, m, v)


def _adamw_small(small_all, ws, ms, vs):
    def body(s_ref, w_ref, m_ref, v_ref, g_ref, d_ref, nm_ref, nv_ref):
        g = s_ref[0]
        for b in range(1, N_DEV):
            g = g + s_ref[b]
        d, nm, nv = _adamw(w_ref[...], g, m_ref[...], v_ref[...])
        g_ref[...] = g
        d_ref[...] = d
        nm_ref[...] = nm
        nv_ref[...] = nv

    sds = jax.ShapeDtypeStruct((SMALL_ROWS, D_MODEL), F32)
    return pl.pallas_call(
        body, out_shape=(sds,) * 4, name="adamw_small", in_specs=[VMEM_SPEC] * 4, out_specs=(VMEM_SPEC,) * 4,
        compiler_params=pltpu.CompilerParams(vmem_limit_bytes=VMEM_LIMIT_V7X),
    )(small_all, ws, ms, vs)


ROW_MOD, ROW_NORM_G, ROW_CONV_B, ROW_BA, ROW_BX, ROW_LAM, ROW_FINAL_G, ROW_SINKS, ROW_CONV_W = 0, 3, 4, 5, 6, 7, 8, 9, 10


def _pack_small(b_ada, norm_g, conv_b, ba, bx, lam, final_g, sinks, conv_w_full):
    rows = [b_ada.reshape(3, D_MODEL), norm_g, conv_b, ba, bx, lam, final_g.reshape(1, D_MODEL),
            jnp.pad(sinks.reshape(1, -1), ((0, 0), (0, D_MODEL - sinks.size))), conv_w_full,
            jnp.zeros((SMALL_ROWS - 14, D_MODEL), F32)]
    return jnp.concatenate([r.astype(F32) for r in rows], axis=0)


def kernel(x, c, positions, w_ada, b_ada, norm_g, w_in, attn_sinks, conv_w, conv_b, rg_wa, rg_ba, rg_wx, rg_bx, rg_lambda, w_attn_proj, w_rnn_proj, w_out, final_g, loss_target, m_w_ada, m_b_ada, m_norm_g, m_w_in, m_attn_sinks, m_conv_w, m_conv_b, m_rg_wa, m_rg_ba, m_rg_wx, m_rg_bx, m_rg_lambda, m_w_attn_proj, m_w_rnn_proj, m_w_out, m_final_g, v_w_ada, v_b_ada, v_norm_g, v_w_in, v_attn_sinks, v_conv_w, v_conv_b, v_rg_wa, v_rg_ba, v_rg_wx, v_rg_bx, v_rg_lambda, v_w_attn_proj, v_w_rnn_proj, v_w_out, v_final_g):
    T = x.shape[1]
    my_chip = lax.axis_index("x") * 2 + lax.axis_index("y")
    my_dev = my_chip * 2 + lax.axis_index("c")
    x2d, tgt = x[0], loss_target[0]
    pos_col = positions.reshape(T, 1)

    gathered = _gather_weights(
        c.reshape(1, 1, D_MODEL), w_ada[0], w_in[0].astype(BF16), w_attn_proj[0].astype(BF16), w_rnn_proj[0].astype(BF16),
        w_out[0].astype(BF16), rg_wa[0].astype(BF16), rg_wx[0].astype(BF16), conv_w[0])
    w_in_f = gathered[0].reshape(D_MODEL, IN_W)
    wap_f, wrp_f, wo_f = (g.reshape(D_MODEL, D_MODEL) for g in gathered[1:4])
    rwa_f, rwx_f = (g.reshape(RNN_BLOCKS, RNN_BW, RNN_BW) for g in gathered[4:6])
    cw_chips, c_all, mod_chips = gathered[6:]
    conv_w_f = jnp.transpose(cw_chips, (1, 0, 2)).reshape(CONV_W, D_MODEL)
    mod_all = jnp.transpose(mod_chips, (1, 0, 2)).reshape(N_DEV, ADA_W) + b_ada
    mod_row = lax.dynamic_slice_in_dim(mod_all, my_dev, 1, axis=0)

    tabs = _rope_tables(pos_col)
    h = _prenorm(x2d, mod_row, norm_g)
    proj = _in_projection(h, w_in_f)
    y_attn = _attn_forward(proj, tabs, attn_sinks)
    y_rnn, h_rnn = _rnn_forward(proj, pos_col, conv_w_f, conv_b, rwa_f, rwx_f, rg_ba, rg_bx, rg_lambda)
    (dx2, merged, d_o, d_pa, d_pr, d_ya, d_yr, d_c, d_final_g, d_gate, loss_vec) = _merge_and_head(
        x2d, tgt, y_attn, y_rnn, proj, wap_f, wrp_f, wo_f, mod_row, final_g.reshape(1, D_MODEL))

    d_q, d_kv, d_ga, d_sinks = _attn_backward(proj, d_ya, tabs, attn_sinks)
    d_b, d_conv_w, d_conv_b, d_rwa, d_rwx, d_ba, d_bx, d_lam = _rnn_backward(
        proj, pos_col, h_rnn, d_yr, conv_w_f, conv_b, rwa_f, rwx_f, rg_ba, rg_bx, rg_lambda)
    pieces = [(d_q, CB_Q, 2), (d_kv, CB_KV, 1), (d_ga, CB_GA, 2), (d_b, CB_XR, 4), (d_c, CB_MA, 4)]
    grad_x, d_shift, d_scale, d_norm_g = _input_backward(pieces, w_in_f, x2d, dx2, mod_row, norm_g)
    g_in = None
    for arr, start, _ in pieces:
        g_in = _weight_grad(h, arr, f"w_in_{start}", into=g_in, col_block=start, total_cols=IN_W)
    g_ap = _weight_grad(y_attn, d_pa, "w_attn_proj")
    g_rp = _weight_grad(y_rnn, d_pr, "w_rnn_proj")
    g_o = _weight_grad(merged, d_o, "w_out")

    d_mod = jnp.concatenate([d_shift, d_scale, d_gate], axis=1)
    small = _pack_small(d_mod, d_norm_g, d_conv_b, d_ba, d_bx, d_lam, d_final_g, d_sinks[:, :N_HEADS], d_conv_w)
    tags = ["w_in", "w_attn_proj", "w_rnn_proj", "w_out", "rg_wa", "rg_wx"]
    sq = (N_CHIPS, 2, SHARD_ROWS // 2, D_MODEL)
    rg = (RNN_BLOCKS, N_CHIPS, 2, SHARD_RG // 2, RNN_BW)
    views = [g_in.reshape(2, D_MODEL // 2, IN_W), g_ap.reshape(sq), g_rp.reshape(sq), g_o.reshape(sq),
             d_rwa.reshape(rg), d_rwx.reshape(rg)]
    from_sib = _swap_halves(views, [0, 1, 1, 1, 2, 2])
    c_idx = lax.axis_index("c").reshape(1).astype(jnp.int32)
    flat4 = [(1, 2, D_MODEL // 2, IN_W), sq, sq, sq, (RNN_BLOCKS * N_CHIPS, 2, SHARD_RG // 2, RNN_BW)]
    flat4.append(flat4[4])
    chip_sums = [_presum(views[i].reshape(flat4[i]), from_sib[i].reshape(flat4[i][:1] + flat4[i][2:]), c_idx, tags[i])
                 for i in range(6)]
    chip_sums = [chip_sums[0].reshape(D_MODEL // 2, IN_W)] + chip_sums[1:4] + [
        s.reshape(RNN_BLOCKS, N_CHIPS, SHARD_RG // 2, RNN_BW) for s in chip_sums[4:6]]
    lands = _exchange_partials(*chip_sums, small)
    small_all = lands[6]
    half2d = [(D_MODEL // 2, SHARD_IN)] + [(SHARD_ROWS // 2, D_MODEL)] * 3 + [(RNN_BLOCKS * SHARD_RG // 2, RNN_BW)] * 2
    halves = [_sum_landed(lands[i].reshape((4,) + half2d[i]), tags[i]) for i in range(6)]
    halves = halves[:4] + [s.reshape(RNN_BLOCKS, SHARD_RG // 2, RNN_BW) for s in halves[4:6]]
    grads = _assemble_with_sibling(halves, [0, 0, 0, 0, 1, 1])
    shapes2d = [(D_MODEL, SHARD_IN), (SHARD_ROWS, D_MODEL), (SHARD_ROWS, D_MODEL), (SHARD_ROWS, D_MODEL),
                (RNN_BLOCKS * SHARD_RG, RNN_BW), (RNN_BLOCKS * SHARD_RG, RNN_BW)]
    big_w = [w_in, w_attn_proj, w_rnn_proj, w_out, rg_wa, rg_wx]
    big_m = [m_w_in, m_w_attn_proj, m_w_rnn_proj, m_w_out, m_rg_wa, m_rg_wx]
    big_v = [v_w_in, v_w_attn_proj, v_w_rnn_proj, v_w_out, v_rg_wa, v_rg_wx]
    res = {}
    for i, tag in enumerate(tags):
        g = grads[i].reshape(shapes2d[i])
        outs = _adamw_shard(g, big_w[i].reshape(shapes2d[i]), big_m[i].reshape(shapes2d[i]), big_v[i].reshape(shapes2d[i]), tag)
        res[tag] = [o.reshape(big_w[i].shape) for o in (g,) + tuple(outs)]

    dmod_all = small_all[:, ROW_MOD:ROW_MOD + 3, :].reshape(N_DEV, ADA_W)
    dmod_cols = lax.dynamic_slice_in_dim(dmod_all, my_chip * SHARD_ADA, SHARD_ADA, axis=1)
    c_t = jnp.pad(jnp.transpose(c_all.reshape(N_DEV, D_MODEL)), ((0, 0), (0, 128 - N_DEV)))
    dmod_cols = jnp.pad(dmod_cols, ((0, 128 - N_DEV), (0, 0)))
    res["w_ada"] = [o.reshape(w_ada.shape) for o in _adamw_w_ada(c_t, dmod_cols, w_ada[0], m_w_ada[0], v_w_ada[0])]

    def full_conv(a):
        return lax.dynamic_update_slice_in_dim(jnp.zeros((CONV_W, D_MODEL), F32), a[0], my_chip * (D_MODEL // N_CHIPS), axis=1)

    packed = [_pack_small(p[0], p[1], p[2], p[3], p[4], p[5], p[6], p[7], full_conv(p[8])) for p in (
        (b_ada, norm_g, conv_b, rg_ba, rg_bx, rg_lambda, final_g, attn_sinks, conv_w),
        (m_b_ada, m_norm_g, m_conv_b, m_rg_ba, m_rg_bx, m_rg_lambda, m_final_g, m_attn_sinks, m_conv_w),
        (v_b_ada, v_norm_g, v_conv_b, v_rg_ba, v_rg_bx, v_rg_lambda, v_final_g, v_attn_sinks, v_conv_w))]
    small_out = _adamw_small(small_all, *packed)

    def unpack(slab):
        cw = lax.dynamic_slice_in_dim(slab[ROW_CONV_W:ROW_CONV_W + CONV_W], my_chip * (D_MODEL // N_CHIPS),
                                      D_MODEL // N_CHIPS, axis=1)
        return {
            "b_ada": slab[ROW_MOD:ROW_MOD + 3].reshape(1, ADA_W), "norm_g": slab[ROW_NORM_G:ROW_NORM_G + 1],
            "conv_b": slab[ROW_CONV_B:ROW_CONV_B + 1], "rg_ba": slab[ROW_BA:ROW_BA + 1], "rg_bx": slab[ROW_BX:ROW_BX + 1],
            "rg_lambda": slab[ROW_LAM:ROW_LAM + 1], "final_g": slab[ROW_FINAL_G], "attn_sinks": slab[ROW_SINKS:ROW_SINKS + 1, :N_HEADS],
            "conv_w": cw[None],
        }

    small_res = [unpack(s) for s in small_out]
    order = ["w_ada", "b_ada", "norm_g", "w_in", "attn_sinks", "conv_w", "conv_b", "rg_wa", "rg_ba", "rg_wx", "rg_bx",
             "rg_lambda", "w_attn_proj", "w_rnn_proj", "w_out", "final_g"]
    loss = lax.psum(loss_vec[0, 0], ("x", "y", "c"))
    outs = [loss, grad_x[None]]
    for kind in range(4):
        for name in order:
            outs.append(res[name][kind] if name in res else small_res[kind][name])
    return tuple(outs)
```

```python
import numpy as np
import jax
import jax.numpy as jnp
from jax import lax
from jax.experimental import pallas as pl
from jax.experimental.pallas import tpu as pltpu

F32 = jnp.float32
BF16 = jnp.bfloat16

D_MODEL = 1024
N_HEADS = 16
N_KV = 4
HEAD_DIM = 64
GROUP = N_HEADS // N_KV
BLOCK = 128
KV_W = N_KV * HEAD_DIM
ROT_HALF = 8
ROPE_THETA = 500000.0
RNN_BLOCKS = 4
RNN_BW = 256
CONV_W = 4
LRU_C = 8.0
NORM_EPS = 1e-6
IN_W = 6656
CB = 512
N_CB = IN_W // CB
CB_Q, CB_KV, CB_GA, CB_XR, CB_GR, CB_MA, CB_MR = 0, 2, 3, 5, 7, 9, 11
N_CHIPS = 4
N_DEV = 8
SHARD_IN = IN_W // N_CHIPS
SHARD_ROWS = D_MODEL // N_CHIPS
SHARD_RG = RNN_BW // N_CHIPS
ADA_W = 3 * D_MODEL
SHARD_ADA = ADA_W // N_CHIPS
SMALL_ROWS = 16

ADAM_LR = 0.001
ADAM_B1 = 0.9
ADAM_B2 = 0.999
ADAM_EPS = 1e-08
ADAM_WD = 0.01
ADAM_STEP = 10

VMEM_LIMIT_V7X = 52 * 1024 * 1024
MESH = pl.DeviceIdType.MESH
ANY = pl.BlockSpec(memory_space=pl.ANY)
VMEM_SPEC = pl.BlockSpec(memory_space=pltpu.VMEM)


def _cp(*sem):
    return pltpu.CompilerParams(dimension_semantics=sem if sem else None, vmem_limit_bytes=VMEM_LIMIT_V7X)


def _dot(a, b):
    return jnp.dot(a, b, preferred_element_type=F32)


def _dot_nt(a, b):
    return lax.dot_general(a, b, (((1,), (1,)), ((), ())), preferred_element_type=F32)


def _dot_tn(a, b):
    return lax.dot_general(a, b, (((0,), (0,)), ((), ())), preferred_element_type=F32)


def _sigmoid(z):
    return 1.0 / (1.0 + jnp.exp(-z))


def _neg_expm1(z):
    series = -(z * (1.0 + z * (0.5 + z * (1.0 / 6.0 + z * (1.0 / 24.0 + z * (1.0 / 120.0))))))
    return jnp.where(z > -0.05, series, 1.0 - jnp.exp(z))


def _softplus(z):
    u = jnp.exp(-jnp.abs(z))
    log1p_u = jnp.where(u < 1e-3, u * (1.0 - u * (0.5 - u * (1.0 / 3.0))), jnp.log(1.0 + u))
    return jnp.maximum(z, 0.0) + log1p_u


def _rms(xf):
    return lax.rsqrt(jnp.mean(xf * xf, axis=-1, keepdims=True) + NORM_EPS)


def _me():
    return lax.axis_index("x"), lax.axis_index("y"), lax.axis_index("c")


def _peer(mask):
    x, y, c = _me()
    fx, fy, fc = (mask >> 2) & 1, (mask >> 1) & 1, mask & 1
    return (x ^ fx if fx else x, y ^ fy if fy else y, c ^ fc if fc else c)


def _chip_of(pos):
    return pos[0] * 2 + pos[1]


CHIP_MASKS = (4, 2, 6)
ALL_MASKS = (1, 2, 3, 4, 5, 6, 7)


def _gather_weights(c_row, w_ada_s, b_w_in, b_wap, b_wrp, b_wo, b_rwa, b_rwx, conv_w_s):
    def body(c_ref, wada_ref, win_s, wap_s, wrp_s, wo_s, rwa_s, rwx_s, cw_s,
             win_f, wap_f, wrp_f, wo_f, rwa_f, rwx_f, cw_f, call_ref, mod_ref,
             wsend, wrecv, lsem, csend, crecv, msend, mrecv, fsend, frecv):
        me = _me()
        my_chip = _chip_of(me)
        my_dev = my_chip * 2 + me[2]
        fulls = (win_f, wap_f, wrp_f, wo_f, rwa_f, rwx_f, cw_f)

        def slot(idx, chip, half=None):
            full = fulls[idx]
            if idx == 0:
                cols = pl.ds(pl.multiple_of(chip * SHARD_IN, 128), SHARD_IN)
                return full.at[:, :, cols] if half is None else full.at[half, :, cols]
            if idx in (1, 2, 3):
                return full.at[chip] if half is None else full.at[chip, half]
            if idx in (4, 5):
                return full.at[:, chip] if half is None else full.at[:, chip, half]
            return full.at[chip]

        def my_half(idx):
            return cw_s if idx == 6 else slot(idx, my_chip, me[2])

        def wcopy(idx, k, to):
            return pltpu.make_async_remote_copy(
                src_ref=my_half(idx), dst_ref=slot(idx, my_chip, None if idx == 6 else me[2]),
                send_sem=wsend.at[idx, k], recv_sem=wrecv.at[idx, k], device_id=to, device_id_type=MESH)

        def wrecv_wait(idx, k, frm):
            pltpu.make_async_remote_copy(
                src_ref=my_half(idx), dst_ref=slot(idx, _chip_of(frm), None if idx == 6 else me[2]),
                send_sem=wsend.at[idx, k], recv_sem=wrecv.at[idx, k], device_id=frm, device_id_type=MESH).wait_recv()

        def forward(idx, k, chip, half, to):
            return pltpu.make_async_remote_copy(
                src_ref=slot(idx, chip, half), dst_ref=slot(idx, chip, half),
                send_sem=fsend.at[idx, k], recv_sem=frecv.at[idx, k], device_id=to, device_id_type=MESH)

        sends = []
        for idx in range(7):
            for k, mask in enumerate(CHIP_MASKS):
                cp = wcopy(idx, k, _peer(mask))
                cp.start()
                sends.append(cp)
        local = [pltpu.make_async_copy(cw_s, slot(6, my_chip), lsem.at[0])]
        for cp in local:
            cp.start()

        call_ref[my_dev] = c_ref[0]
        csends = []
        for k, mask in enumerate(ALL_MASKS):
            cp = pltpu.make_async_remote_copy(
                src_ref=c_ref.at[0], dst_ref=call_ref.at[my_dev],
                send_sem=csend.at[k], recv_sem=crecv.at[k], device_id=_peer(mask), device_id_type=MESH)
            cp.start()
            csends.append(cp)
        for k, mask in enumerate(ALL_MASKS):
            frm = _peer(mask)
            pltpu.make_async_remote_copy(
                src_ref=c_ref.at[0], dst_ref=call_ref.at[_chip_of(frm) * 2 + frm[2]],
                send_sem=csend.at[k], recv_sem=crecv.at[k], device_id=frm, device_id_type=MESH).wait_recv()
        for cp in csends:
            cp.wait_send()

        c_all = call_ref[...].reshape(N_DEV, D_MODEL).astype(BF16)
        mod_ref[my_chip] = _dot(c_all, wada_ref[...].astype(BF16))
        msends = []
        for k, mask in enumerate(CHIP_MASKS):
            cp = pltpu.make_async_remote_copy(
                src_ref=mod_ref.at[my_chip], dst_ref=mod_ref.at[my_chip],
                send_sem=msend.at[k], recv_sem=mrecv.at[k], device_id=_peer(mask), device_id_type=MESH)
            cp.start()
            msends.append(cp)
        for k, mask in enumerate(CHIP_MASKS):
            frm = _peer(mask)
            pltpu.make_async_remote_copy(
                src_ref=mod_ref.at[my_chip], dst_ref=mod_ref.at[_chip_of(frm)],
                send_sem=msend.at[k], recv_sem=mrecv.at[k], device_id=frm, device_id_type=MESH).wait_recv()
        for cp in msends:
            cp.wait_send()

        sib = _peer(1)
        forwards = []
        for idx in range(7):
            for k, mask in enumerate(CHIP_MASKS):
                frm = _peer(mask)
                wrecv_wait(idx, k, frm)
                if idx < 6:
                    cp = forward(idx, k, _chip_of(frm), me[2], sib)
                    cp.start()
                    forwards.append(cp)
        for idx in range(6):
            for k, mask in enumerate(CHIP_MASKS):
                forward(idx, k, _chip_of(_peer(mask)), 1 - me[2], sib).wait_recv()
        for cp in sends + forwards:
            cp.wait_send()
        for cp in local:
            cp.wait()

    out_shape = (
        jax.ShapeDtypeStruct((2, D_MODEL // 2, IN_W), BF16),
        jax.ShapeDtypeStruct((N_CHIPS, 2, SHARD_ROWS // 2, D_MODEL), BF16),
        jax.ShapeDtypeStruct((N_CHIPS, 2, SHARD_ROWS // 2, D_MODEL), BF16),
        jax.ShapeDtypeStruct((N_CHIPS, 2, SHARD_ROWS // 2, D_MODEL), BF16),
        jax.ShapeDtypeStruct((RNN_BLOCKS, N_CHIPS, 2, SHARD_RG // 2, RNN_BW), BF16),
        jax.ShapeDtypeStruct((RNN_BLOCKS, N_CHIPS, 2, SHARD_RG // 2, RNN_BW), BF16),
        jax.ShapeDtypeStruct((N_CHIPS, CONV_W, D_MODEL // N_CHIPS), F32),
        jax.ShapeDtypeStruct((N_DEV, 1, D_MODEL), F32),
        jax.ShapeDtypeStruct((N_CHIPS, N_DEV, SHARD_ADA), F32),
    )
    return pl.pallas_call(
        body, out_shape=out_shape, name="gather_weights",
        in_specs=[VMEM_SPEC, VMEM_SPEC] + [ANY] * 7,
        out_specs=tuple([ANY] * 7 + [VMEM_SPEC, VMEM_SPEC]),
        scratch_shapes=[
            pltpu.SemaphoreType.DMA((7, 3)), pltpu.SemaphoreType.DMA((7, 3)), pltpu.SemaphoreType.DMA((7,)),
            pltpu.SemaphoreType.DMA((7,)), pltpu.SemaphoreType.DMA((7,)),
            pltpu.SemaphoreType.DMA((3,)), pltpu.SemaphoreType.DMA((3,)),
            pltpu.SemaphoreType.DMA((6, 3)), pltpu.SemaphoreType.DMA((6, 3)),
        ],
        input_output_aliases={2: 0, 3: 1, 4: 2, 5: 3, 6: 4, 7: 5},
        compiler_params=pltpu.CompilerParams(vmem_limit_bytes=VMEM_LIMIT_V7X),
    )(c_row, w_ada_s, b_w_in.reshape(out_shape[0].shape), b_wap.reshape(out_shape[1].shape),
      b_wrp.reshape(out_shape[2].shape), b_wo.reshape(out_shape[3].shape), b_rwa.reshape(out_shape[4].shape),
      b_rwx.reshape(out_shape[5].shape), conv_w_s)


def _cast_place(shard, chip_idx, full_shape, block, index_map, tag):
    def body(chip_ref, s_ref, o_ref):
        o_ref[...] = s_ref[...].astype(BF16)

    grid_spec = pltpu.PrefetchScalarGridSpec(
        num_scalar_prefetch=1, grid=(1,),
        in_specs=[pl.BlockSpec(shard.shape, lambda i, chip_ref: (0,) * shard.ndim)],
        out_specs=pl.BlockSpec(block, lambda i, chip_ref: index_map(chip_ref[0])))
    return pl.pallas_call(
        body, out_shape=jax.ShapeDtypeStruct(full_shape, BF16), grid_spec=grid_spec, name=f"cast_place_{tag}",
        compiler_params=_cp("arbitrary"),
    )(chip_idx, shard)


def _exchange_partials(g_in, g_ap, g_rp, g_o, g_wa, g_wx, small):
    def body(gin, gap, grp_, go, gwa, gwx, small_ref, lin, lap, lrp, lo, lwa, lwx, small_all,
             gsend, grecv, ssend, srecv):
        me = _me()
        my_chip = _chip_of(me)
        my_dev = my_chip * 2 + me[2]
        srcs = (gin, gap, grp_, go, gwa, gwx)
        lands = (lin, lap, lrp, lo, lwa, lwx)

        def shard(idx, chip):
            if idx == 0:
                return srcs[idx].at[:, pl.ds(pl.multiple_of(chip * SHARD_IN, 128), SHARD_IN)]
            if idx in (1, 2, 3):
                return srcs[idx].at[chip]
            return srcs[idx].at[:, chip]

        sends = []
        for idx in range(6):
            for k, mask in enumerate(CHIP_MASKS):
                to = _peer(mask)
                cp = pltpu.make_async_remote_copy(
                    src_ref=shard(idx, _chip_of(to)), dst_ref=lands[idx].at[k],
                    send_sem=gsend.at[idx, k], recv_sem=grecv.at[idx, k], device_id=to, device_id_type=MESH)
                cp.start()
                sends.append(cp)
        small_all[my_dev] = small_ref[...]
        ssends = []
        for k, mask in enumerate(ALL_MASKS):
            cp = pltpu.make_async_remote_copy(
                src_ref=small_ref, dst_ref=small_all.at[my_dev],
                send_sem=ssend.at[k], recv_sem=srecv.at[k], device_id=_peer(mask), device_id_type=MESH)
            cp.start()
            ssends.append(cp)
        for k, mask in enumerate(ALL_MASKS):
            frm = _peer(mask)
            pltpu.make_async_remote_copy(
                src_ref=small_ref, dst_ref=small_all.at[_chip_of(frm) * 2 + frm[2]],
                send_sem=ssend.at[k], recv_sem=srecv.at[k], device_id=frm, device_id_type=MESH).wait_recv()
        for cp in ssends:
            cp.wait_send()

        for idx in range(6):
            for k, mask in enumerate(CHIP_MASKS):
                frm = _peer(mask)
                pltpu.make_async_remote_copy(
                    src_ref=shard(idx, my_chip), dst_ref=lands[idx].at[k],
                    send_sem=gsend.at[idx, k], recv_sem=grecv.at[idx, k], device_id=frm, device_id_type=MESH).wait_recv()
        for cp in sends:
            cp.wait_send()

    out_shape = (
        jax.ShapeDtypeStruct((3, D_MODEL // 2, SHARD_IN), F32),
        jax.ShapeDtypeStruct((3, SHARD_ROWS // 2, D_MODEL), F32),
        jax.ShapeDtypeStruct((3, SHARD_ROWS // 2, D_MODEL), F32),
        jax.ShapeDtypeStruct((3, SHARD_ROWS // 2, D_MODEL), F32),
        jax.ShapeDtypeStruct((3, RNN_BLOCKS, SHARD_RG // 2, RNN_BW), F32),
        jax.ShapeDtypeStruct((3, RNN_BLOCKS, SHARD_RG // 2, RNN_BW), F32),
        jax.ShapeDtypeStruct((N_DEV, SMALL_ROWS, D_MODEL), F32),
    )
    return pl.pallas_call(
        body, out_shape=out_shape, name="exchange_partials",
        in_specs=[ANY] * 6 + [VMEM_SPEC],
        out_specs=tuple([ANY] * 6 + [VMEM_SPEC]),
        scratch_shapes=[
            pltpu.SemaphoreType.DMA((6, 3)), pltpu.SemaphoreType.DMA((6, 3)),
            pltpu.SemaphoreType.DMA((7,)), pltpu.SemaphoreType.DMA((7,)),
        ],
        compiler_params=pltpu.CompilerParams(vmem_limit_bytes=VMEM_LIMIT_V7X),
    )(g_in, g_ap, g_rp, g_o, g_wa, g_wx, small)


def _half_of(ref, axis, half):
    return ref.at[(slice(None),) * axis + (half,)]


def _swap_halves(parts, axes):
    n = len(parts)

    def body(*refs):
        ins, outs, ssem, rsem = refs[:n], refs[n:2 * n], refs[2 * n], refs[2 * n + 1]
        c = lax.axis_index("c")
        cps = [pltpu.make_async_remote_copy(src_ref=_half_of(ins[i], axes[i], 1 - c), dst_ref=outs[i], send_sem=ssem.at[i],
                                            recv_sem=rsem.at[i], device_id=_peer(1), device_id_type=MESH) for i in range(n)]
        for cp in cps:
            cp.start()
        for cp in cps:
            cp.wait()

    shapes = [p.shape[:a] + p.shape[a + 1:] for p, a in zip(parts, axes)]
    return pl.pallas_call(
        body, out_shape=tuple(jax.ShapeDtypeStruct(s, p.dtype) for s, p in zip(shapes, parts)), name="swap_halves",
        in_specs=[ANY] * n, out_specs=tuple([ANY] * n),
        scratch_shapes=[pltpu.SemaphoreType.DMA((n,)), pltpu.SemaphoreType.DMA((n,))],
    )(*parts)


def _presum(mine, sib, c_idx, tag):
    S, _, R, C = mine.shape
    tr = min(R, 256)
    tc = SHARD_IN if C % SHARD_IN == 0 else C

    def body(c_ref, m_ref, s_ref, o_ref):
        o_ref[...] = m_ref[:, 0] + s_ref[...]

    grid_spec = pltpu.PrefetchScalarGridSpec(
        num_scalar_prefetch=1, grid=(R // tr, C // tc),
        in_specs=[pl.BlockSpec((S, 1, tr, tc), lambda i, j, c_ref: (0, c_ref[0], i, j)),
                  pl.BlockSpec((S, tr, tc), lambda i, j, c_ref: (0, i, j))],
        out_specs=pl.BlockSpec((S, tr, tc), lambda i, j, c_ref: (0, i, j)))
    return pl.pallas_call(
        body, out_shape=jax.ShapeDtypeStruct((S, R, C), F32), grid_spec=grid_spec, name=f"presum_{tag}",
        compiler_params=_cp("parallel", "parallel"),
    )(c_idx, mine, sib)


def _assemble_with_sibling(parts, axes):
    n = len(parts)

    def body(*refs):
        outs, ssem, rsem = refs[n:2 * n], refs[2 * n], refs[2 * n + 1]
        c = lax.axis_index("c")
        cps = [pltpu.make_async_remote_copy(
            src_ref=_half_of(outs[i], axes[i], c), dst_ref=_half_of(outs[i], axes[i], c), send_sem=ssem.at[i],
            recv_sem=rsem.at[i], device_id=_peer(1), device_id_type=MESH) for i in range(n)]
        for cp in cps:
            cp.start()
        for i in range(n):
            pltpu.make_async_remote_copy(
                src_ref=_half_of(outs[i], axes[i], c), dst_ref=_half_of(outs[i], axes[i], 1 - c), send_sem=ssem.at[i],
                recv_sem=rsem.at[i], device_id=_peer(1), device_id_type=MESH).wait_recv()
        for cp in cps:
            cp.wait_send()

    return pl.pallas_call(
        body, out_shape=tuple(jax.ShapeDtypeStruct(p.shape, p.dtype) for p in parts), name="assemble_with_sibling",
        in_specs=[ANY] * n, out_specs=tuple([ANY] * n), input_output_aliases={i: i for i in range(n)},
        scratch_shapes=[pltpu.SemaphoreType.DMA((n,)), pltpu.SemaphoreType.DMA((n,))],
    )(*parts)


def _rope_tables(pos_col):
    T = pos_col.shape[0]
    tm = min(T, 512)
    inv = np.float32(ROPE_THETA) ** (-(np.arange(0, 2 * ROT_HALF, 2, dtype=np.float32)) / np.float32(2 * ROT_HALF))
    lane = np.arange(128) % HEAD_DIM
    freq = np.where(lane < 2 * ROT_HALF, inv[lane % ROT_HALF], 0.0).astype(np.float32)[None, :]

    def body(pos_ref, f_ref, c_ref, sa_ref, sb_ref):
        ang = pos_ref[...].astype(F32) * f_ref[...]
        c, s = jnp.cos(ang), jnp.sin(ang)
        m = lax.broadcasted_iota(jnp.int32, ang.shape, 1) & (HEAD_DIM - 1)
        c_ref[...] = jnp.where(m < 2 * ROT_HALF, c, 1.0)
        sa_ref[...] = jnp.where(m < ROT_HALF, -s, 0.0)
        sb_ref[...] = jnp.where((m >= ROT_HALF) & (m < 2 * ROT_HALF), s, 0.0)

    tab = jax.ShapeDtypeStruct((T, 128), F32)
    return pl.pallas_call(
        body, out_shape=(tab, tab, tab), grid=(T // tm,), name="rope_tables",
        in_specs=[pl.BlockSpec((tm, 1), lambda i: (i, 0)), pl.BlockSpec((1, 128), lambda i: (0, 0))],
        out_specs=tuple(pl.BlockSpec((tm, 128), lambda i: (i, 0)) for _ in range(3)),
        compiler_params=_cp("parallel"),
    )(pos_col, jnp.asarray(freq))


def _wide(tab, width):
    return jnp.concatenate([tab] * (width // 128), axis=1)


def _rope(t, c, sa, sb):
    w = t.shape[-1]
    return t * c + pltpu.roll(t, w - ROT_HALF, 1) * sa + pltpu.roll(t, ROT_HALF, 1) * sb


def _unrope(d, c, sa, sb):
    w = d.shape[-1]
    return d * c + pltpu.roll(d * sa, ROT_HALF, 1) + pltpu.roll(d * sb, w - ROT_HALF, 1)


def _prenorm(x, mod_row, norm_g):
    T = x.shape[0]
    tm = min(T, 512)

    def body(x_ref, mod_ref, g_ref, h_ref):
        xf = x_ref[...]
        shift, scale = mod_ref[:, 0:D_MODEL], mod_ref[:, D_MODEL:2 * D_MODEL]
        h = (xf * _rms(xf)) * g_ref[...] * (1.0 + scale) + shift
        h_ref[...] = h.astype(BF16)

    return pl.pallas_call(
        body, out_shape=jax.ShapeDtypeStruct((T, D_MODEL), BF16), grid=(T // tm,), name="prenorm",
        in_specs=[pl.BlockSpec((tm, D_MODEL), lambda i: (i, 0)), pl.BlockSpec((1, ADA_W), lambda i: (0, 0)),
                  pl.BlockSpec((1, D_MODEL), lambda i: (0, 0))],
        out_specs=pl.BlockSpec((tm, D_MODEL), lambda i: (i, 0)),
        compiler_params=_cp("parallel"),
    )(x, mod_row, norm_g)


def _in_projection(h, w_in):
    T = h.shape[0]
    tm, tn = min(T, 512), SHARD_IN

    def body(h_ref, w_ref, o_ref):
        o_ref[...] = _dot(h_ref[...], w_ref[...])

    return pl.pallas_call(
        body, out_shape=jax.ShapeDtypeStruct((T, IN_W), F32), grid=(IN_W // tn, T // tm), name="in_projection",
        in_specs=[pl.BlockSpec((tm, D_MODEL), lambda j, i: (i, 0)), pl.BlockSpec((D_MODEL, tn), lambda j, i: (0, j))],
        out_specs=pl.BlockSpec((tm, tn), lambda j, i: (i, j)),
        compiler_params=_cp("parallel", "parallel"),
    )(h, w_in)


def _attn_mask(n):
    qi = lax.broadcasted_iota(jnp.int32, (GROUP * BLOCK, 2 * BLOCK), 0) & (BLOCK - 1)
    kj = lax.broadcasted_iota(jnp.int32, (GROUP * BLOCK, 2 * BLOCK), 1)
    diff = qi + BLOCK - kj
    return (diff >= 0) & (diff < BLOCK) & ((kj >= BLOCK) | (n > 0))


def _sink_col(sink_ref, kh):
    rowg = lax.broadcasted_iota(jnp.int32, (GROUP * BLOCK, 1), 0) // BLOCK
    col = jnp.full((GROUP * BLOCK, 1), sink_ref[0, GROUP * kh], F32)
    for g in range(1, GROUP):
        col = jnp.where(rowg == g, sink_ref[0, GROUP * kh + g], col)
    return col


def _attn_probs(qr, kr_prev, kr_cur, v_prev, v_cur, kh, sink_col, mask):
    heads = [qr[:, HEAD_DIM * (GROUP * kh + g): HEAD_DIM * (GROUP * kh + g + 1)] for g in range(GROUP)]
    qs = jnp.concatenate(heads, axis=0).astype(BF16)
    lo, hi = HEAD_DIM * kh, HEAD_DIM * (kh + 1)
    kk = jnp.concatenate([kr_prev[:, lo:hi], kr_cur[:, lo:hi]], axis=0).astype(BF16)
    vv = jnp.concatenate([v_prev[:, lo:hi], v_cur[:, lo:hi]], axis=0).astype(BF16)
    s = _dot_nt(qs, kk) * (1.0 / 8.0)
    s = jnp.where(mask, s, -1e30)
    m = jnp.maximum(jnp.max(s, axis=-1, keepdims=True), sink_col)
    p = jnp.exp(s - m)
    p_sink = jnp.exp(sink_col - m)
    denom = jnp.sum(p, axis=-1, keepdims=True) + p_sink
    return qs, kk, vv, p / denom, p_sink / denom


def _unstack_heads(parts):
    cols = []
    for kh in range(N_KV):
        for g in range(GROUP):
            cols.append(parts[kh][g * BLOCK:(g + 1) * BLOCK, :])
    return jnp.concatenate(cols, axis=1)


def _attn_forward(proj, tabs, sinks):
    T = proj.shape[0]
    nb = T // BLOCK

    def body(q_ref, kvc_ref, kvp_ref, g0_ref, g1_ref, cc, sac, sbc, cp_, sap, sbp, sink_ref, y_ref):
        n = pl.program_id(0)
        tc = (_wide(cc[...], D_MODEL), _wide(sac[...], D_MODEL), _wide(sbc[...], D_MODEL))
        tcur = tuple(t[:, :KV_W] for t in tc)
        tprev = (_wide(cp_[...], KV_W), _wide(sap[...], KV_W), _wide(sbp[...], KV_W))
        qr = _rope(q_ref[...], *tc)
        kr_cur = _rope(kvc_ref[:, 0:KV_W], *tcur)
        kr_prev = _rope(kvp_ref[:, 0:KV_W], *tprev)
        v_cur, v_prev = kvc_ref[:, KV_W:2 * KV_W], kvp_ref[:, KV_W:2 * KV_W]
        mask = _attn_mask(n)
        outs = []
        for kh in range(N_KV):
            _, _, vv, pn, _ = _attn_probs(qr, kr_prev, kr_cur, v_prev, v_cur, kh, _sink_col(sink_ref, kh), mask)
            outs.append(_dot(pn.astype(BF16), vv))
        o = _unstack_heads(outs)
        g = jnp.concatenate([g0_ref[...], g1_ref[...]], axis=1)
        y_ref[...] = (o * (g * _sigmoid(g))).astype(BF16)

    def blk(w, cb):
        return pl.BlockSpec((BLOCK, w), lambda n, cb=cb: (n, cb))

    prev = lambda w, cb: pl.BlockSpec((BLOCK, w), lambda n, cb=cb: (jnp.maximum(n - 1, 0), cb))
    return pl.pallas_call(
        body, out_shape=jax.ShapeDtypeStruct((T, D_MODEL), BF16), grid=(nb,), name="attn_forward",
        in_specs=[blk(D_MODEL, 0), blk(CB, CB_KV), prev(CB, CB_KV), blk(CB, CB_GA), blk(CB, CB_GA + 1),
                  blk(128, 0), blk(128, 0), blk(128, 0), prev(128, 0), prev(128, 0), prev(128, 0),
                  pl.BlockSpec(memory_space=pltpu.SMEM)],
        out_specs=pl.BlockSpec((BLOCK, D_MODEL), lambda n: (n, 0)),
        compiler_params=_cp("parallel"),
    )(proj, proj, proj, proj, proj, *tabs, *tabs, sinks)


def _scan_rows8():
    return lax.broadcasted_iota(jnp.int32, (8, D_MODEL), 0)


def _scan_forward(a_ref, b_ref, h_ref, carry, rows):
    row = _scan_rows8()

    def group(i, carry):
        off = pl.multiple_of(i * 8, 8)
        a, b = a_ref[pl.ds(off, 8), :], b_ref[pl.ds(off, 8), :]
        for d in (1, 2, 4):
            ok = row >= d
            b = jnp.where(ok, a * pltpu.roll(b, d, 0) + b, b)
            a = jnp.where(ok, a * pltpu.roll(a, d, 0), a)
        h = a * carry + b
        h_ref[pl.ds(off, 8), :] = h
        return h[7:8, :]

    return lax.fori_loop(0, rows // 8, group, carry)


def _scan_backward(a_ref, g_ref, lam_ref, carry, rows):
    row = _scan_rows8()

    def group(i, carry):
        off = pl.multiple_of((rows // 8 - 1 - i) * 8, 8)
        a, g = a_ref[pl.ds(off, 8), :], g_ref[pl.ds(off, 8), :]
        b = a * g
        for d in (1, 2, 4):
            ok = row < 8 - d
            b = jnp.where(ok, a * pltpu.roll(b, 8 - d, 0) + b, b)
            a = jnp.where(ok, a * pltpu.roll(a, 8 - d, 0), a)
        mu = a * carry + b
        mu_below = jnp.where(row == 7, carry, pltpu.roll(mu, 7, 0))
        lam_ref[pl.ds(off, 8), :] = g + mu_below
        return mu[0:1, :]

    return lax.fori_loop(0, rows // 8, group, carry)


def _rnn_recompute(xbuf, xr, tail, cw, cb, wa_ref, wx_ref, ba, bx, sp, reset):
    rows = xr.shape[0]
    xbuf[0:8, :] = tail
    xbuf[8:rows + 8, :] = xr
    xs = [xbuf[pl.ds(8 - (CONV_W - 1 - k), rows), :] for k in range(CONV_W - 1)] + [xr]
    xc = xs[0] * cw[0:1, :]
    for k in range(1, CONV_W):
        xc = xc + xs[k] * cw[k:k + 1, :]
    xc = xc + cb
    xcb = xc.astype(BF16)
    za = jnp.concatenate([_dot(xcb[:, RNN_BW * j:RNN_BW * (j + 1)], wa_ref[j]) for j in range(RNN_BLOCKS)], axis=1) + ba
    zx = jnp.concatenate([_dot(xcb[:, RNN_BW * j:RNN_BW * (j + 1)], wx_ref[j]) for j in range(RNN_BLOCKS)], axis=1) + bx
    r, i = _sigmoid(za), _sigmoid(zx)
    log_a = -LRU_C * r * sp
    a_raw = jnp.exp(log_a)
    mult_raw = jnp.sqrt(_neg_expm1(2.0 * log_a))
    a = jnp.where(reset, 0.0, a_raw)
    mult = jnp.where(reset, 1.0, mult_raw)
    return xs, xc, xcb, r, i, a_raw, mult_raw, a, mult


def _rnn_forward(proj, pos_col, conv_w, conv_b, rwa, rwx, ba, bx, lam):
    T = proj.shape[0]
    tr = min(T, 256)

    def body(x0, x1, g0, g1, pos_ref, cw_ref, cb_ref, wa_ref, wx_ref, ba_ref, bx_ref, lam_ref,
             y_ref, h_ref, xbuf, abuf, bbuf, tail, carry):
        t = pl.program_id(0)

        @pl.when(t == 0)
        def _():
            tail[...] = jnp.zeros_like(tail)
            carry[...] = jnp.zeros_like(carry)

        xr = jnp.concatenate([x0[...], x1[...]], axis=1)
        sp = _softplus(-lam_ref[...])
        reset = pos_ref[...] == 0
        _, xc, _, _, i, _, _, a, mult = _rnn_recompute(
            xbuf, xr, tail[...], cw_ref[...], cb_ref[...], wa_ref, wx_ref, ba_ref[...], bx_ref[...], sp, reset)
        abuf[...] = a
        bbuf[...] = mult * (i * xc)
        last = _scan_forward(abuf, bbuf, h_ref, carry[0:1, :], tr)
        carry[...] = jnp.broadcast_to(last, carry.shape)
        tail[...] = xr[tr - 8:tr, :]
        g = jnp.concatenate([g0[...], g1[...]], axis=1)
        y_ref[...] = (h_ref[...] * (g * _sigmoid(g))).astype(BF16)

    blk = lambda cb: pl.BlockSpec((tr, CB), lambda t, cb=cb: (t, cb))
    row = lambda w: pl.BlockSpec((1, w), lambda t: (0, 0))
    full3 = pl.BlockSpec((RNN_BLOCKS, RNN_BW, RNN_BW), lambda t: (0, 0, 0))
    return pl.pallas_call(
        body, out_shape=(jax.ShapeDtypeStruct((T, D_MODEL), BF16), jax.ShapeDtypeStruct((T, D_MODEL), F32)),
        grid=(T // tr,), name="rnn_forward",
        in_specs=[blk(CB_XR), blk(CB_XR + 1), blk(CB_GR), blk(CB_GR + 1), pl.BlockSpec((tr, 1), lambda t: (t, 0)),
                  pl.BlockSpec((CONV_W, D_MODEL), lambda t: (0, 0)), row(D_MODEL), full3, full3,
                  row(D_MODEL), row(D_MODEL), row(D_MODEL)],
        out_specs=(pl.BlockSpec((tr, D_MODEL), lambda t: (t, 0)), pl.BlockSpec((tr, D_MODEL), lambda t: (t, 0))),
        scratch_shapes=[pltpu.VMEM((tr + 8, D_MODEL), F32), pltpu.VMEM((tr, D_MODEL), F32), pltpu.VMEM((tr, D_MODEL), F32),
                        pltpu.VMEM((8, D_MODEL), F32), pltpu.VMEM((8, D_MODEL), F32)],
        compiler_params=_cp("arbitrary"),
    )(proj, proj, proj, proj, pos_col, conv_w, conv_b, rwa, rwx, ba, bx, lam)


def _merge_and_head(x, target, y_attn, y_rnn, proj, wap, wrp, wo, mod_row, final_g):
    T = x.shape[0]
    tm = min(T, 256)

    def body(x_ref, t_ref, ya_ref, yr_ref, ma0, ma1, mr0, mr1, wap_ref, wrp_ref, wo_ref, mod_ref, fg_ref,
             dx2_ref, mg_ref, do_ref, dpa_ref, dpr_ref, dya_ref, dyr_ref, dc_ref, dfg_ref, dgate_ref, loss_ref):
        i = pl.program_id(0)
        gate = mod_ref[:, 2 * D_MODEL:3 * D_MODEL]
        ya, yr = ya_ref[...], yr_ref[...]
        pa, pr = _dot(ya, wap_ref[...]), _dot(yr, wrp_ref[...])
        sa = _sigmoid(jnp.concatenate([ma0[...], ma1[...]], axis=1))
        sr = _sigmoid(jnp.concatenate([mr0[...], mr1[...]], axis=1))
        merged = sa * pa + sr * pr
        mb = merged.astype(BF16)
        o = _dot(mb, wo_ref[...])
        x2 = x_ref[...] + gate * o
        r2 = _rms(x2)
        xn2 = x2 * r2
        fg = fg_ref[...]
        err = xn2 * fg - t_ref[...]
        loss_t = 0.5 * jnp.sum(jnp.sum(err * err, axis=-1, keepdims=True) * (1.0 / D_MODEL), axis=0, keepdims=True)
        dy = err * (1.0 / D_MODEL)
        dfg_t = jnp.sum(dy * xn2, axis=0, keepdims=True)
        dxn = dy * fg
        dx2 = r2 * (dxn - xn2 * jnp.mean(dxn * xn2, axis=-1, keepdims=True))
        dgate_t = jnp.sum(dx2 * o, axis=0, keepdims=True)
        dob = (dx2 * gate).astype(BF16)
        dmerged = _dot_nt(dob, wo_ref[...])
        dpa = (dmerged * sa).astype(BF16)
        dpr = (dmerged * sr).astype(BF16)
        dx2_ref[...] = dx2
        mg_ref[...] = mb
        do_ref[...] = dob
        dpa_ref[...] = dpa
        dpr_ref[...] = dpr
        dya_ref[...] = _dot_nt(dpa, wap_ref[...])
        dyr_ref[...] = _dot_nt(dpr, wrp_ref[...])
        dc_ref[:, 0:D_MODEL] = (dmerged * pa * sa * (1.0 - sa)).astype(BF16)
        dc_ref[:, D_MODEL:2 * D_MODEL] = (dmerged * pr * sr * (1.0 - sr)).astype(BF16)

        @pl.when(i == 0)
        def _():
            dfg_ref[...] = jnp.zeros_like(dfg_ref)
            dgate_ref[...] = jnp.zeros_like(dgate_ref)
            loss_ref[...] = jnp.zeros_like(loss_ref)

        dfg_ref[...] += dfg_t
        dgate_ref[...] += dgate_t
        loss_ref[...] += jnp.broadcast_to(loss_t, loss_ref.shape)

    tok = lambda w: pl.BlockSpec((tm, w), lambda i: (i, 0))
    blk = lambda cb: pl.BlockSpec((tm, CB), lambda i, cb=cb: (i, cb))
    wfull = pl.BlockSpec((D_MODEL, D_MODEL), lambda i: (0, 0))
    row = lambda w: pl.BlockSpec((1, w), lambda i: (0, 0))
    out_shape = (
        jax.ShapeDtypeStruct((T, D_MODEL), F32), jax.ShapeDtypeStruct((T, D_MODEL), BF16),
        jax.ShapeDtypeStruct((T, D_MODEL), BF16), jax.ShapeDtypeStruct((T, D_MODEL), BF16),
        jax.ShapeDtypeStruct((T, D_MODEL), BF16), jax.ShapeDtypeStruct((T, D_MODEL), F32),
        jax.ShapeDtypeStruct((T, D_MODEL), F32), jax.ShapeDtypeStruct((T, 2 * D_MODEL), BF16),
        jax.ShapeDtypeStruct((1, D_MODEL), F32), jax.ShapeDtypeStruct((1, D_MODEL), F32),
        jax.ShapeDtypeStruct((1, 128), F32),
    )
    return pl.pallas_call(
        body, out_shape=out_shape, grid=(T // tm,), name="merge_and_head",
        in_specs=[tok(D_MODEL), tok(D_MODEL), tok(D_MODEL), tok(D_MODEL), blk(CB_MA), blk(CB_MA + 1), blk(CB_MR),
                  blk(CB_MR + 1), wfull, wfull, wfull, row(ADA_W), row(D_MODEL)],
        out_specs=(tok(D_MODEL),) * 7 + (tok(2 * D_MODEL), row(D_MODEL), row(D_MODEL), row(128)),
        compiler_params=_cp("arbitrary"),
    )(x, target, y_attn, y_rnn, proj, proj, proj, proj, wap, wrp, wo, mod_row, final_g)


def _attn_backward(proj, d_y, tabs, sinks):
    T = proj.shape[0]
    nb = T // BLOCK

    def body(q_ref, kvc_ref, kvp_ref, g0_ref, g1_ref, dy_ref, cc, sac, sbc, cp_, sap, sbp, sink_ref,
             dq_ref, dkv_ref, dg_ref, dsink_ref, carry):
        n = pl.program_id(0)

        @pl.when(n == 0)
        def _():
            carry[...] = jnp.zeros_like(carry)
            dsink_ref[...] = jnp.zeros_like(dsink_ref)

        @pl.when(n < nb)
        def _():
            tc = (_wide(cc[...], D_MODEL), _wide(sac[...], D_MODEL), _wide(sbc[...], D_MODEL))
            tcur = tuple(t[:, :KV_W] for t in tc)
            tprev = (_wide(cp_[...], KV_W), _wide(sap[...], KV_W), _wide(sbp[...], KV_W))
            qr = _rope(q_ref[...], *tc)
            kr_cur = _rope(kvc_ref[:, 0:KV_W], *tcur)
            kr_prev = _rope(kvp_ref[:, 0:KV_W], *tprev)
            v_cur, v_prev = kvc_ref[:, KV_W:2 * KV_W], kvp_ref[:, KV_W:2 * KV_W]
            g = jnp.concatenate([g0_ref[...], g1_ref[...]], axis=1)
            sg = _sigmoid(g)
            dy = dy_ref[...]
            d_o = dy * (g * sg)
            mask = _attn_mask(n)
            lane = lax.broadcasted_iota(jnp.int32, (1, 128), 1)
            rowg = lax.broadcasted_iota(jnp.int32, (GROUP * BLOCK, 1), 0) // BLOCK
            o_parts, dq_parts, dk_parts, dv_parts = [], [], [], []
            dsink = jnp.zeros((1, 128), F32)
            for kh in range(N_KV):
                qs, kk, vv, pn, pn_sink = _attn_probs(qr, kr_prev, kr_cur, v_prev, v_cur, kh, _sink_col(sink_ref, kh), mask)
                pnb = pn.astype(BF16)
                o_parts.append(_dot(pnb, vv))
                dos = jnp.concatenate(
                    [d_o[:, HEAD_DIM * (GROUP * kh + gq): HEAD_DIM * (GROUP * kh + gq + 1)] for gq in range(GROUP)],
                    axis=0).astype(BF16)
                dpn = _dot_nt(dos, vv)
                delta = jnp.sum(pn * dpn, axis=-1, keepdims=True)
                dsb = (pn * (dpn - delta) * (1.0 / 8.0)).astype(BF16)
                dq_parts.append(_dot(dsb, kk))
                dk_parts.append(_dot_tn(dsb, qs))
                dv_parts.append(_dot_tn(pnb, dos))
                ds_rows = pn_sink * delta
                for gq in range(GROUP):
                    val = -jnp.sum(jnp.where(rowg == gq, ds_rows, 0.0), axis=0, keepdims=True)
                    dsink = dsink + jnp.where(lane == GROUP * kh + gq, val, 0.0)
            o = _unstack_heads(o_parts)
            dg_ref[...] = (dy * o * (sg * (1.0 + g * (1.0 - sg)))).astype(BF16)
            dq_ref[...] = _unrope(_unstack_heads(dq_parts), *tc).astype(BF16)
            dk_prev = _unrope(jnp.concatenate([p[0:BLOCK, :] for p in dk_parts], axis=1), *tprev)
            dk_cur = _unrope(jnp.concatenate([p[BLOCK:2 * BLOCK, :] for p in dk_parts], axis=1), *tcur)
            dv_prev = jnp.concatenate([p[0:BLOCK, :] for p in dv_parts], axis=1)
            dv_cur = jnp.concatenate([p[BLOCK:2 * BLOCK, :] for p in dv_parts], axis=1)
            dkv_ref[...] = (carry[...] + jnp.concatenate([dk_prev, dv_prev], axis=1)).astype(BF16)
            carry[...] = jnp.concatenate([dk_cur, dv_cur], axis=1)
            dsink_ref[...] += dsink

        @pl.when(n == nb)
        def _():
            dkv_ref[...] = carry[...].astype(BF16)

    cur = lambda w, cb: pl.BlockSpec((BLOCK, w), lambda n, cb=cb: (jnp.minimum(n, nb - 1), cb))
    prev = lambda w, cb: pl.BlockSpec((BLOCK, w), lambda n, cb=cb: (jnp.maximum(jnp.minimum(n, nb - 1) - 1, 0), cb))
    out_shape = (jax.ShapeDtypeStruct((T, D_MODEL), BF16), jax.ShapeDtypeStruct((T, 2 * KV_W), BF16),
                 jax.ShapeDtypeStruct((T, D_MODEL), BF16), jax.ShapeDtypeStruct((1, 128), F32))
    return pl.pallas_call(
        body, out_shape=out_shape, grid=(nb + 1,), name="attn_backward",
        in_specs=[cur(D_MODEL, 0), cur(CB, CB_KV), prev(CB, CB_KV), cur(CB, CB_GA), cur(CB, CB_GA + 1), cur(D_MODEL, 0),
                  cur(128, 0), cur(128, 0), cur(128, 0), prev(128, 0), prev(128, 0), prev(128, 0),
                  pl.BlockSpec(memory_space=pltpu.SMEM)],
        out_specs=(cur(D_MODEL, 0), pl.BlockSpec((BLOCK, 2 * KV_W), lambda n: (jnp.maximum(n - 1, 0), 0)),
                   cur(D_MODEL, 0), pl.BlockSpec((1, 128), lambda n: (0, 0))),
        scratch_shapes=[pltpu.VMEM((BLOCK, 2 * KV_W), F32)],
        compiler_params=_cp("arbitrary"),
    )(proj, proj, proj, proj, proj, d_y, *tabs, *tabs, sinks)


def _rnn_backward(proj, pos_col, h_rnn, d_y, conv_w, conv_b, rwa, rwx, ba, bx, lam):
    T = proj.shape[0]
    tr = min(T, 256)
    nt = T // tr
    hb = tr // 8

    def body(x0, x1, xh0, xh1, g0, g1, pos_ref, h_ref, hh_ref, dy_ref, cw_ref, cb_ref, wa_ref, wx_ref, ba_ref, bx_ref,
             lam_ref, db_ref, dcw_ref, dcb_ref, dwa_ref, dwx_ref, dba_ref, dbx_ref, dlam_ref,
             xbuf, hbuf, dbuf, abuf, gbuf, lbuf, mu_carry, dxc_head):
        step = pl.program_id(0)
        first_tile = step == nt - 1

        @pl.when(step == 0)
        def _():
            mu_carry[...] = jnp.zeros_like(mu_carry)
            dxc_head[...] = jnp.zeros_like(dxc_head)
            for ref in (dcw_ref, dcb_ref, dwa_ref, dwx_ref, dba_ref, dbx_ref, dlam_ref):
                ref[...] = jnp.zeros_like(ref)

        xr = jnp.concatenate([x0[...], x1[...]], axis=1)
        tail = jnp.where(first_tile, 0.0, jnp.concatenate([xh0[...], xh1[...]], axis=1))
        lam_v = lam_ref[...]
        sp = _softplus(-lam_v)
        reset = pos_ref[...] == 0
        cw = cw_ref[...]
        xs, xc, xcb, r, i, a_raw, mult_raw, a, mult = _rnn_recompute(
            xbuf, xr, tail, cw, cb_ref[...], wa_ref, wx_ref, ba_ref[...], bx_ref[...], sp, reset)
        g = jnp.concatenate([g0[...], g1[...]], axis=1)
        sg = _sigmoid(g)
        dy = dy_ref[...]
        h = h_ref[...]
        d_g = dy * h * (sg * (1.0 + g * (1.0 - sg)))
        abuf[...] = a
        gbuf[...] = dy * (g * sg)
        top = _scan_backward(abuf, gbuf, lbuf, mu_carry[0:1, :], tr)
        mu_carry[...] = jnp.broadcast_to(top, mu_carry.shape)
        lam_t = lbuf[...]
        hbuf[0:8, :] = jnp.where(first_tile, 0.0, hh_ref[...])
        hbuf[8:tr + 8, :] = h
        h_prev = hbuf[pl.ds(7, tr), :]
        live = jnp.logical_not(reset)
        d_a = jnp.where(live, lam_t * h_prev, 0.0)
        d_mult = jnp.where(live, lam_t * (i * xc), 0.0)
        d_ixc = lam_t * mult
        d_i = d_ixc * xc
        d_xc = d_ixc * i
        d_log_a = d_a * a_raw - d_mult * (a_raw * a_raw / mult_raw)
        d_log_a = jnp.where(live, d_log_a, 0.0)
        d_za = d_log_a * (-LRU_C * sp) * (r * (1.0 - r))
        d_zx = d_i * (i * (1.0 - i))
        dlam_ref[...] += jnp.sum(d_log_a * r, axis=0, keepdims=True) * (LRU_C * _sigmoid(-lam_v))
        dba_ref[...] += jnp.sum(d_za, axis=0, keepdims=True)
        dbx_ref[...] += jnp.sum(d_zx, axis=0, keepdims=True)
        dzab, dzxb = d_za.astype(BF16), d_zx.astype(BF16)
        back = []
        for j in range(RNN_BLOCKS):
            sl = slice(RNN_BW * j, RNN_BW * (j + 1))
            dwa_ref[j] += _dot_tn(xcb[:, sl], dzab[:, sl])
            dwx_ref[j] += _dot_tn(xcb[:, sl], dzxb[:, sl])
            back.append(_dot_nt(dzab[:, sl], wa_ref[j]) + _dot_nt(dzxb[:, sl], wx_ref[j]))
        d_xc = d_xc + jnp.concatenate(back, axis=1)
        dcb_ref[...] += jnp.sum(d_xc, axis=0, keepdims=True)
        for k in range(CONV_W):
            dcw_ref[k:k + 1, :] += jnp.sum(d_xc * xs[k], axis=0, keepdims=True)
        dbuf[0:tr, :] = d_xc
        dbuf[tr:tr + 8, :] = dxc_head[...]
        d_xr = d_xc * cw[CONV_W - 1:CONV_W, :]
        for k in range(CONV_W - 1):
            d_xr = d_xr + dbuf[pl.ds(CONV_W - 1 - k, tr), :] * cw[k:k + 1, :]
        dxc_head[...] = d_xc[0:8, :]
        db_ref[:, 0:D_MODEL] = d_xr.astype(BF16)
        db_ref[:, D_MODEL:2 * D_MODEL] = d_g.astype(BF16)

    rev = lambda s: nt - 1 - s
    blk = lambda cb: pl.BlockSpec((tr, CB), lambda s, cb=cb: (rev(s), cb))
    halo = lambda w, cb: pl.BlockSpec((8, w), lambda s, cb=cb: (jnp.maximum(rev(s) * hb - 1, 0), cb))
    tok = lambda w: pl.BlockSpec((tr, w), lambda s: (rev(s), 0))
    row = lambda w: pl.BlockSpec((1, w), lambda s: (0, 0))
    full3 = pl.BlockSpec((RNN_BLOCKS, RNN_BW, RNN_BW), lambda s: (0, 0, 0))
    cwspec = pl.BlockSpec((CONV_W, D_MODEL), lambda s: (0, 0))
    vec = jax.ShapeDtypeStruct((1, D_MODEL), F32)
    gate_w = jax.ShapeDtypeStruct((RNN_BLOCKS, RNN_BW, RNN_BW), F32)
    out_shape = (jax.ShapeDtypeStruct((T, 2 * D_MODEL), BF16), jax.ShapeDtypeStruct((CONV_W, D_MODEL), F32), vec,
                 gate_w, gate_w, vec, vec, vec)
    big = lambda: pltpu.VMEM((tr, D_MODEL), F32)
    ext = lambda: pltpu.VMEM((tr + 8, D_MODEL), F32)
    return pl.pallas_call(
        body, out_shape=out_shape, grid=(nt,), name="rnn_backward",
        in_specs=[blk(CB_XR), blk(CB_XR + 1), halo(CB, CB_XR), halo(CB, CB_XR + 1), blk(CB_GR), blk(CB_GR + 1),
                  pl.BlockSpec((tr, 1), lambda s: (rev(s), 0)), tok(D_MODEL), halo(D_MODEL, 0), tok(D_MODEL),
                  cwspec, row(D_MODEL), full3, full3, row(D_MODEL), row(D_MODEL), row(D_MODEL)],
        out_specs=(tok(2 * D_MODEL), cwspec, row(D_MODEL), full3, full3, row(D_MODEL), row(D_MODEL), row(D_MODEL)),
        scratch_shapes=[ext(), ext(), ext(), big(), big(), big(), pltpu.VMEM((8, D_MODEL), F32), pltpu.VMEM((8, D_MODEL), F32)],
        compiler_params=_cp("arbitrary"),
    )(proj, proj, proj, proj, proj, proj, pos_col, h_rnn, h_rnn, d_y, conv_w, conv_b, rwa, rwx, ba, bx, lam)


def _input_backward(pieces, w_in, x, dx2, mod_row, norm_g):
    T = x.shape[0]
    tm = min(T, 512)
    n = len(pieces)

    def body(*refs):
        d_refs = refs[:n]
        w_ref, x_ref, dx2_ref, mod_ref, g_ref, gx_ref, dshift_ref, dscale_ref, dg_ref, acc = refs[n:]
        i, k = pl.program_id(0), pl.program_id(1)

        @pl.when(k == 0)
        def _():
            acc[...] = jnp.zeros_like(acc)

        for d_ref, (_, start, count) in zip(d_refs, pieces):
            @pl.when((k >= start) & (k < start + count))
            def _(d_ref=d_ref):
                acc[...] += _dot_nt(d_ref[...], w_ref[...])

        @pl.when((i == 0) & (k == 0))
        def _():
            dshift_ref[...] = jnp.zeros_like(dshift_ref)
            dscale_ref[...] = jnp.zeros_like(dscale_ref)
            dg_ref[...] = jnp.zeros_like(dg_ref)

        @pl.when(k == N_CB - 1)
        def _():
            dh = acc[...]
            xf = x_ref[...]
            r1 = _rms(xf)
            xn = xf * r1
            gn = g_ref[...]
            s1 = 1.0 + mod_ref[:, D_MODEL:2 * D_MODEL]
            dshift_ref[...] += jnp.sum(dh, axis=0, keepdims=True)
            dscale_ref[...] += jnp.sum(dh * (xn * gn), axis=0, keepdims=True)
            dg_ref[...] += jnp.sum(dh * s1 * xn, axis=0, keepdims=True)
            dxn = dh * s1 * gn
            gx_ref[...] = dx2_ref[...] + r1 * (dxn - xn * jnp.mean(dxn * xn, axis=-1, keepdims=True))

    def piece_spec(start, count):
        return pl.BlockSpec((tm, CB), lambda i, k: (i, jnp.clip(k - start, 0, count - 1)))

    tok = pl.BlockSpec((tm, D_MODEL), lambda i, k: (i, 0))
    row = lambda w: pl.BlockSpec((1, w), lambda i, k: (0, 0))
    vec = jax.ShapeDtypeStruct((1, D_MODEL), F32)
    return pl.pallas_call(
        body, out_shape=(jax.ShapeDtypeStruct((T, D_MODEL), F32), vec, vec, vec), grid=(T // tm, N_CB), name="input_backward",
        in_specs=[piece_spec(s, c) for _, s, c in pieces]
        + [pl.BlockSpec((D_MODEL, CB), lambda i, k: (0, k)), tok, tok, row(ADA_W), row(D_MODEL)],
        out_specs=(tok, row(D_MODEL), row(D_MODEL), row(D_MODEL)),
        scratch_shapes=[pltpu.VMEM((tm, D_MODEL), F32)],
        compiler_params=_cp("arbitrary", "arbitrary"),
    )(*[p[0] for p in pieces], w_in, x, dx2, mod_row, norm_g)


def _weight_grad(a, b, tag, into=None, col_block=0, total_cols=None):
    T, M = a.shape
    N = b.shape[1]
    tk = min(T, 512)
    tn = CB
    total_cols = N if total_cols is None else total_cols

    def body(*refs):
        a_ref, b_ref, o_ref = refs[0], refs[1], refs[-1]
        k = pl.program_id(1)

        @pl.when(k == 0)
        def _():
            o_ref[...] = jnp.zeros_like(o_ref)

        o_ref[...] += _dot_tn(a_ref[...], b_ref[...])

    in_specs = [pl.BlockSpec((tk, M), lambda j, k: (k, 0)), pl.BlockSpec((tk, tn), lambda j, k: (k, j))]
    args = [a, b]
    aliases = {}
    if into is not None:
        in_specs.append(ANY)
        args.append(into)
        aliases = {2: 0}
    return pl.pallas_call(
        body, out_shape=jax.ShapeDtypeStruct((M, total_cols), F32), grid=(N // tn, T // tk), name=f"weight_grad_{tag}",
        in_specs=in_specs, out_specs=pl.BlockSpec((M, tn), lambda j, k: (0, col_block + j)),
        input_output_aliases=aliases, compiler_params=_cp("parallel", "arbitrary"),
    )(*args)


def _adamw(w, g, m, v):
    m = ADAM_B1 * m + (1.0 - ADAM_B1) * g
    v = ADAM_B2 * v + (1.0 - ADAM_B2) * (g * g)
    m_hat = m / (1.0 - ADAM_B1 ** ADAM_STEP)
    v_hat = v / (1.0 - ADAM_B2 ** ADAM_STEP)
    delta = -ADAM_LR * (m_hat / (jnp.sqrt(v_hat) + ADAM_EPS) + ADAM_WD * w)
    return delta, m, v


def _sum_landed(kind, own, land, where, tag):
    if kind == "in":
        R, C = land.shape[1:]
        tr = 256
        grid = (R // tr,)
        own_spec = pl.BlockSpec((tr, C), lambda i, w: (i, w[0]))
        land_spec = pl.BlockSpec((3, tr, C), lambda i, w: (0, i, 0))
        out_spec = pl.BlockSpec((1, tr, C), lambda i, w: (w[1], i, 0))
        out_shape = (2, R, C)
        pick = lambda ref: ref[...]
    elif kind == "sq":
        R, C = land.shape[1:]
        grid = (1,)
        own_spec = pl.BlockSpec((1, R, C), lambda i, w: (w[0], 0, 0))
        land_spec = pl.BlockSpec((3, R, C), lambda i, w: (0, 0, 0))
        out_spec = pl.BlockSpec((1, R, C), lambda i, w: (w[1], 0, 0))
        out_shape = (2, R, C)
        pick = lambda ref: ref[0]
    else:
        B, R, C = land.shape[1:]
        grid = (1,)
        own_spec = pl.BlockSpec((B, 1, R, C), lambda i, w: (0, w[0], 0, 0))
        land_spec = pl.BlockSpec((3, B, R, C), lambda i, w: (0, 0, 0, 0))
        out_spec = pl.BlockSpec((B, 1, R, C), lambda i, w: (0, w[1], 0, 0))
        out_shape = (B, 2, R, C)
        pick = lambda ref: ref[:, 0]

    def body(w_ref, own_ref, l_ref, o_ref):
        total = ((pick(own_ref) + l_ref[0]) + l_ref[1]) + l_ref[2]
        if kind == "in":
            o_ref[0] = total
        elif kind == "sq":
            o_ref[0] = total
        else:
            o_ref[:, 0] = total

    grid_spec = pltpu.PrefetchScalarGridSpec(num_scalar_prefetch=1, grid=grid, in_specs=[own_spec, land_spec], out_specs=out_spec)
    return pl.pallas_call(
        body, out_shape=jax.ShapeDtypeStruct(out_shape, F32), grid_spec=grid_spec, name=f"sum_landed_{tag}",
        compiler_params=_cp("parallel"),
    )(where, own, land)


def _adamw_shard(g, w, m, v, tag):
    R, C = w.shape
    tr = min(R, 256)

    def body(g_ref, w_ref, m_ref, v_ref, d_ref, nm_ref, nv_ref):
        d, nm, nv = _adamw(w_ref[...], g_ref[...], m_ref[...], v_ref[...])
        d_ref[...] = d
        nm_ref[...] = nm
        nv_ref[...] = nv

    spec = pl.BlockSpec((tr, C), lambda i: (i, 0))
    sds = jax.ShapeDtypeStruct((R, C), F32)
    return pl.pallas_call(
        body, out_shape=(sds,) * 3, grid=(R // tr,), name=f"adamw_{tag}",
        in_specs=[spec] * 4, out_specs=(spec,) * 3, compiler_params=_cp("parallel"),
    )(g, w, m, v)


def _adamw_w_ada(c_t, dmod_cols, w, m, v):
    R, C = w.shape

    def body(ct_ref, dm_ref, w_ref, m_ref, v_ref, g_ref, d_ref, nm_ref, nv_ref):
        g = _dot(ct_ref[...].astype(BF16), dm_ref[...].astype(BF16))
        d, nm, nv = _adamw(w_ref[...], g, m_ref[...], v_ref[...])
        g_ref[...] = g
        d_ref[...] = d
        nm_ref[...] = nm
        nv_ref[...] = nv

    tr = 256
    spec = pl.BlockSpec((tr, C), lambda i: (i, 0))
    sds = jax.ShapeDtypeStruct((R, C), F32)
    return pl.pallas_call(
        body, out_shape=(sds,) * 4, grid=(R // tr,), name="adamw_w_ada",
        in_specs=[pl.BlockSpec((tr, 128), lambda i: (i, 0)), pl.BlockSpec((128, C), lambda i: (0, 0))] + [spec] * 3,
        out_specs=(spec,) * 4, compiler_params=_cp("parallel"),
    )(c_t, dmod_cols, w, m, v)


def _adamw_small(small_all, ws, ms, vs):
    def body(s_ref, w_ref, m_ref, v_ref, g_ref, d_ref, nm_ref, nv_ref):
        g = s_ref[0]
        for b in range(1, N_DEV):
            g = g + s_ref[b]
        d, nm, nv = _adamw(w_ref[...], g, m_ref[...], v_ref[...])
        g_ref[...] = g
        d_ref[...] = d
        nm_ref[...] = nm
        nv_ref[...] = nv

    sds = jax.ShapeDtypeStruct((SMALL_ROWS, D_MODEL), F32)
    return pl.pallas_call(
        body, out_shape=(sds,) * 4, name="adamw_small", in_specs=[VMEM_SPEC] * 4, out_specs=(VMEM_SPEC,) * 4,
        compiler_params=pltpu.CompilerParams(vmem_limit_bytes=VMEM_LIMIT_V7X),
    )(small_all, ws, ms, vs)


ROW_MOD, ROW_NORM_G, ROW_CONV_B, ROW_BA, ROW_BX, ROW_LAM, ROW_FINAL_G, ROW_SINKS, ROW_CONV_W = 0, 3, 4, 5, 6, 7, 8, 9, 10


def _pack_small(b_ada, norm_g, conv_b, ba, bx, lam, final_g, sinks, conv_w_full):
    rows = [b_ada.reshape(3, D_MODEL), norm_g, conv_b, ba, bx, lam, final_g.reshape(1, D_MODEL),
            jnp.pad(sinks.reshape(1, -1), ((0, 0), (0, D_MODEL - sinks.size))), conv_w_full,
            jnp.zeros((SMALL_ROWS - 14, D_MODEL), F32)]
    return jnp.concatenate([r.astype(F32) for r in rows], axis=0)


def kernel(x, c, positions, w_ada, b_ada, norm_g, w_in, attn_sinks, conv_w, conv_b, rg_wa, rg_ba, rg_wx, rg_bx, rg_lambda, w_attn_proj, w_rnn_proj, w_out, final_g, loss_target, m_w_ada, m_b_ada, m_norm_g, m_w_in, m_attn_sinks, m_conv_w, m_conv_b, m_rg_wa, m_rg_ba, m_rg_wx, m_rg_bx, m_rg_lambda, m_w_attn_proj, m_w_rnn_proj, m_w_out, m_final_g, v_w_ada, v_b_ada, v_norm_g, v_w_in, v_attn_sinks, v_conv_w, v_conv_b, v_rg_wa, v_rg_ba, v_rg_wx, v_rg_bx, v_rg_lambda, v_w_attn_proj, v_w_rnn_proj, v_w_out, v_final_g):
    T = x.shape[1]
    my_chip = lax.axis_index("x") * 2 + lax.axis_index("y")
    my_dev = my_chip * 2 + lax.axis_index("c")
    x2d, tgt = x[0], loss_target[0]
    pos_col = positions.reshape(T, 1)

    chip_idx = my_chip.reshape(1).astype(jnp.int32)
    c_idx = lax.axis_index("c").reshape(1).astype(jnp.int32)
    sq_place = ((D_MODEL, D_MODEL), (SHARD_ROWS, D_MODEL), lambda chip: (chip, 0))
    rg_place = ((RNN_BLOCKS, RNN_BW, RNN_BW), (RNN_BLOCKS, SHARD_RG, RNN_BW), lambda chip: (0, chip, 0))
    gathered = _gather_weights(
        c.reshape(1, 1, D_MODEL), w_ada[0],
        _cast_place(w_in[0], chip_idx, (D_MODEL, IN_W), (D_MODEL, SHARD_IN), lambda chip: (0, chip), "w_in"),
        _cast_place(w_attn_proj[0], chip_idx, *sq_place, "w_attn_proj"),
        _cast_place(w_rnn_proj[0], chip_idx, *sq_place, "w_rnn_proj"),
        _cast_place(w_out[0], chip_idx, *sq_place, "w_out"),
        _cast_place(rg_wa[0], chip_idx, *rg_place, "rg_wa"),
        _cast_place(rg_wx[0], chip_idx, *rg_place, "rg_wx"),
        conv_w[0])
    w_in_f = gathered[0].reshape(D_MODEL, IN_W)
    wap_f, wrp_f, wo_f = (g.reshape(D_MODEL, D_MODEL) for g in gathered[1:4])
    rwa_f, rwx_f = (g.reshape(RNN_BLOCKS, RNN_BW, RNN_BW) for g in gathered[4:6])
    cw_chips, c_all, mod_chips = gathered[6:]
    conv_w_f = jnp.transpose(cw_chips, (1, 0, 2)).reshape(CONV_W, D_MODEL)
    mod_all = jnp.transpose(mod_chips, (1, 0, 2)).reshape(N_DEV, ADA_W) + b_ada
    mod_row = lax.dynamic_slice_in_dim(mod_all, my_dev, 1, axis=0)

    tabs = _rope_tables(pos_col)
    h = _prenorm(x2d, mod_row, norm_g)
    proj = _in_projection(h, w_in_f)
    y_attn = _attn_forward(proj, tabs, attn_sinks)
    y_rnn, h_rnn = _rnn_forward(proj, pos_col, conv_w_f, conv_b, rwa_f, rwx_f, rg_ba, rg_bx, rg_lambda)
    (dx2, merged, d_o, d_pa, d_pr, d_ya, d_yr, d_c, d_final_g, d_gate, loss_vec) = _merge_and_head(
        x2d, tgt, y_attn, y_rnn, proj, wap_f, wrp_f, wo_f, mod_row, final_g.reshape(1, D_MODEL))

    d_q, d_kv, d_ga, d_sinks = _attn_backward(proj, d_ya, tabs, attn_sinks)
    d_b, d_conv_w, d_conv_b, d_rwa, d_rwx, d_ba, d_bx, d_lam = _rnn_backward(
        proj, pos_col, h_rnn, d_yr, conv_w_f, conv_b, rwa_f, rwx_f, rg_ba, rg_bx, rg_lambda)
    pieces = [(d_q, CB_Q, 2), (d_kv, CB_KV, 1), (d_ga, CB_GA, 2), (d_b, CB_XR, 4), (d_c, CB_MA, 4)]
    grad_x, d_shift, d_scale, d_norm_g = _input_backward(pieces, w_in_f, x2d, dx2, mod_row, norm_g)
    g_in = None
    for arr, start, _ in pieces:
        g_in = _weight_grad(h, arr, f"w_in_{start}", into=g_in, col_block=start, total_cols=IN_W)
    g_ap = _weight_grad(y_attn, d_pa, "w_attn_proj")
    g_rp = _weight_grad(y_rnn, d_pr, "w_rnn_proj")
    g_o = _weight_grad(merged, d_o, "w_out")

    d_mod = jnp.concatenate([d_shift, d_scale, d_gate], axis=1)
    small = _pack_small(d_mod, d_norm_g, d_conv_b, d_ba, d_bx, d_lam, d_final_g, d_sinks[:, :N_HEADS], d_conv_w)
    tags = ["w_in", "w_attn_proj", "w_rnn_proj", "w_out", "rg_wa", "rg_wx"]
    sq = (N_CHIPS, 2, SHARD_ROWS // 2, D_MODEL)
    rg = (RNN_BLOCKS, N_CHIPS, 2, SHARD_RG // 2, RNN_BW)
    views = [g_in.reshape(2, D_MODEL // 2, IN_W), g_ap.reshape(sq), g_rp.reshape(sq), g_o.reshape(sq),
             d_rwa.reshape(rg), d_rwx.reshape(rg)]
    from_sib = _swap_halves(views, [0, 1, 1, 1, 2, 2])
    flat4 =[(1, 2, D_MODEL // 2, IN_W), sq, sq, sq, (RNN_BLOCKS * N_CHIPS, 2, SHARD_RG // 2, RNN_BW)]
    flat4.append(flat4[4])
    chip_sums = [_presum(views[i].reshape(flat4[i]), from_sib[i].reshape(flat4[i][:1] + flat4[i][2:]), c_idx, tags[i])
                 for i in range(6)]
    chip_sums = [chip_sums[0].reshape(D_MODEL // 2, IN_W)] + chip_sums[1:4] + [
        s.reshape(RNN_BLOCKS, N_CHIPS, SHARD_RG // 2, RNN_BW) for s in chip_sums[4:6]]
    lands = _exchange_partials(*chip_sums, small)
    small_all = lands[6]
    where = jnp.concatenate([chip_idx, c_idx])
    kinds = ["in", "sq", "sq", "sq", "rg", "rg"]
    halves = [_sum_landed(kinds[i], chip_sums[i], lands[i], where, tags[i]) for i in range(6)]
    grads = _assemble_with_sibling(halves, [0, 0, 0, 0, 1, 1])
    shapes2d = [(D_MODEL, SHARD_IN), (SHARD_ROWS, D_MODEL), (SHARD_ROWS, D_MODEL), (SHARD_ROWS, D_MODEL),
                (RNN_BLOCKS * SHARD_RG, RNN_BW), (RNN_BLOCKS * SHARD_RG, RNN_BW)]
    big_w = [w_in, w_attn_proj, w_rnn_proj, w_out, rg_wa, rg_wx]
    big_m = [m_w_in, m_w_attn_proj, m_w_rnn_proj, m_w_out, m_rg_wa, m_rg_wx]
    big_v = [v_w_in, v_w_attn_proj, v_w_rnn_proj, v_w_out, v_rg_wa, v_rg_wx]
    res = {}
    for i, tag in enumerate(tags):
        g = grads[i].reshape(shapes2d[i])
        outs = _adamw_shard(g, big_w[i].reshape(shapes2d[i]), big_m[i].reshape(shapes2d[i]), big_v[i].reshape(shapes2d[i]), tag)
        res[tag] = [o.reshape(big_w[i].shape) for o in (g,) + tuple(outs)]

    dmod_all = small_all[:, ROW_MOD:ROW_MOD + 3, :].reshape(N_DEV, ADA_W)
    dmod_cols = lax.dynamic_slice_in_dim(dmod_all, my_chip * SHARD_ADA, SHARD_ADA, axis=1)
    c_t = jnp.pad(jnp.transpose(c_all.reshape(N_DEV, D_MODEL)), ((0, 0), (0, 128 - N_DEV)))
    dmod_cols = jnp.pad(dmod_cols, ((0, 128 - N_DEV), (0, 0)))
    res["w_ada"] = [o.reshape(w_ada.shape) for o in _adamw_w_ada(c_t, dmod_cols, w_ada[0], m_w_ada[0], v_w_ada[0])]

    def full_conv(a):
        return lax.dynamic_update_slice_in_dim(jnp.zeros((CONV_W, D_MODEL), F32), a[0], my_chip * (D_MODEL // N_CHIPS), axis=1)

    packed = [_pack_small(p[0], p[1], p[2], p[3], p[4], p[5], p[6], p[7], full_conv(p[8])) for p in (
        (b_ada, norm_g, conv_b, rg_ba, rg_bx, rg_lambda, final_g, attn_sinks, conv_w),
        (m_b_ada, m_norm_g, m_conv_b, m_rg_ba, m_rg_bx, m_rg_lambda, m_final_g, m_attn_sinks, m_conv_w),
        (v_b_ada, v_norm_g, v_conv_b, v_rg_ba, v_rg_bx, v_rg_lambda, v_final_g, v_attn_sinks, v_conv_w))]
    small_out = _adamw_small(small_all, *packed)

    def unpack(slab):
        cw = lax.dynamic_slice_in_dim(slab[ROW_CONV_W:ROW_CONV_W + CONV_W], my_chip * (D_MODEL // N_CHIPS),
                                      D_MODEL // N_CHIPS, axis=1)
        return {
            "b_ada": slab[ROW_MOD:ROW_MOD + 3].reshape(1, ADA_W), "norm_g": slab[ROW_NORM_G:ROW_NORM_G + 1],
            "conv_b": slab[ROW_CONV_B:ROW_CONV_B + 1], "rg_ba": slab[ROW_BA:ROW_BA + 1], "rg_bx": slab[ROW_BX:ROW_BX + 1],
            "rg_lambda": slab[ROW_LAM:ROW_LAM + 1], "final_g": slab[ROW_FINAL_G], "attn_sinks": slab[ROW_SINKS:ROW_SINKS + 1, :N_HEADS],
            "conv_w": cw[None],
        }

    small_res = [unpack(s) for s in small_out]
    order = ["w_ada", "b_ada", "norm_g", "w_in", "attn_sinks", "conv_w", "conv_b", "rg_wa", "rg_ba", "rg_wx", "rg_bx",
             "rg_lambda", "w_attn_proj", "w_rnn_proj", "w_out", "final_g"]
    loss = lax.psum(loss_vec[0, 0], ("x", "y", "c"))
    outs = [loss, grad_x[None]]
    for kind in range(4):
        for name in order:
            outs.append(res[name][kind] if name in res else small_res[kind][name])
    return tuple(outs)
```

```python
import numpy as np
import jax
import jax.numpy as jnp
from jax import lax
from jax.experimental import pallas as pl
from jax.experimental.pallas import tpu as pltpu

F32 = jnp.float32
BF16 = jnp.bfloat16

D_MODEL = 1024
N_HEADS = 16
N_KV = 4
HEAD_DIM = 64
GROUP = N_HEADS // N_KV
BLOCK = 128
KV_W = N_KV * HEAD_DIM
ROT_HALF = 8
ROPE_THETA = 500000.0
RNN_BLOCKS = 4
RNN_BW = 256
CONV_W = 4
LRU_C = 8.0
NORM_EPS = 1e-6
IN_W = 6656
CB = 512
N_CB = IN_W // CB
CB_Q, CB_KV, CB_GA, CB_XR, CB_GR, CB_MA, CB_MR = 0, 2, 3, 5, 7, 9, 11
N_CHIPS = 4
N_DEV = 8
SHARD_IN = IN_W // N_CHIPS
SHARD_ROWS = D_MODEL // N_CHIPS
SHARD_RG = RNN_BW // N_CHIPS
ADA_W = 3 * D_MODEL
SHARD_ADA = ADA_W // N_CHIPS
SMALL_ROWS = 16

ADAM_LR = 0.001
ADAM_B1 = 0.9
ADAM_B2 = 0.999
ADAM_EPS = 1e-08
ADAM_WD = 0.01
ADAM_STEP = 10

VMEM_LIMIT_V7X = 52 * 1024 * 1024
MESH = pl.DeviceIdType.MESH
ANY = pl.BlockSpec(memory_space=pl.ANY)
VMEM_SPEC = pl.BlockSpec(memory_space=pltpu.VMEM)


def _cp(*sem):
    return pltpu.CompilerParams(dimension_semantics=sem if sem else None, vmem_limit_bytes=VMEM_LIMIT_V7X)


def _dot(a, b):
    return jnp.dot(a, b, preferred_element_type=F32)


def _dot_nt(a, b):
    return lax.dot_general(a, b, (((1,), (1,)), ((), ())), preferred_element_type=F32)


def _dot_tn(a, b):
    return lax.dot_general(a, b, (((0,), (0,)), ((), ())), preferred_element_type=F32)


def _sigmoid(z):
    return 1.0 / (1.0 + jnp.exp(-z))


def _neg_expm1(z):
    series = -(z * (1.0 + z * (0.5 + z * (1.0 / 6.0 + z * (1.0 / 24.0 + z * (1.0 / 120.0))))))
    return jnp.where(z > -0.05, series, 1.0 - jnp.exp(z))


def _softplus(z):
    u = jnp.exp(-jnp.abs(z))
    log1p_u = jnp.where(u < 1e-3, u * (1.0 - u * (0.5 - u * (1.0 / 3.0))), jnp.log(1.0 + u))
    return jnp.maximum(z, 0.0) + log1p_u


def _rms(xf):
    return lax.rsqrt(jnp.mean(xf * xf, axis=-1, keepdims=True) + NORM_EPS)


def _me():
    return lax.axis_index("x"), lax.axis_index("y"), lax.axis_index("c")


def _peer(mask):
    x, y, c = _me()
    fx, fy, fc = (mask >> 2) & 1, (mask >> 1) & 1, mask & 1
    return (x ^ fx if fx else x, y ^ fy if fy else y, c ^ fc if fc else c)


def _chip_of(pos):
    return pos[0] * 2 + pos[1]


CHIP_MASKS = (4, 2, 6)
ALL_MASKS = (1, 2, 3, 4, 5, 6, 7)


def _gather_weights(c_row, w_ada_s, b_w_in, b_wap, b_wrp, b_wo, b_rwa, b_rwx, conv_w_s):
    def body(c_ref, wada_ref, win_s, wap_s, wrp_s, wo_s, rwa_s, rwx_s, cw_s,
             win_f, wap_f, wrp_f, wo_f, rwa_f, rwx_f, cw_f, call_ref, mod_ref,
             wsend, wrecv, lsem, csend, crecv, msend, mrecv, fsend, frecv):
        me = _me()
        my_chip = _chip_of(me)
        my_dev = my_chip * 2 + me[2]
        fulls = (win_f, wap_f, wrp_f, wo_f, rwa_f, rwx_f, cw_f)

        def slot(idx, chip, half=None):
            full = fulls[idx]
            if idx == 0:
                cols = pl.ds(pl.multiple_of(chip * SHARD_IN, 128), SHARD_IN)
                return full.at[:, :, cols] if half is None else full.at[half, :, cols]
            if idx in (1, 2, 3):
                return full.at[chip] if half is None else full.at[chip, half]
            if idx in (4, 5):
                return full.at[:, chip] if half is None else full.at[:, chip, half]
            return full.at[chip]

        def my_half(idx):
            return cw_s if idx == 6 else slot(idx, my_chip, me[2])

        def wcopy(idx, k, to):
            return pltpu.make_async_remote_copy(
                src_ref=my_half(idx), dst_ref=slot(idx, my_chip, None if idx == 6 else me[2]),
                send_sem=wsend.at[idx, k], recv_sem=wrecv.at[idx, k], device_id=to, device_id_type=MESH)

        def wrecv_wait(idx, k, frm):
            pltpu.make_async_remote_copy(
                src_ref=my_half(idx), dst_ref=slot(idx, _chip_of(frm), None if idx == 6 else me[2]),
                send_sem=wsend.at[idx, k], recv_sem=wrecv.at[idx, k], device_id=frm, device_id_type=MESH).wait_recv()

        def forward(idx, k, chip, half, to):
            return pltpu.make_async_remote_copy(
                src_ref=slot(idx, chip, half), dst_ref=slot(idx, chip, half),
                send_sem=fsend.at[idx, k], recv_sem=frecv.at[idx, k], device_id=to, device_id_type=MESH)

        sends = []
        for idx in range(7):
            for k, mask in enumerate(CHIP_MASKS):
                cp = wcopy(idx, k, _peer(mask))
                cp.start()
                sends.append(cp)
        local = [pltpu.make_async_copy(cw_s, slot(6, my_chip), lsem.at[0])]
        for cp in local:
            cp.start()

        call_ref[my_dev] = c_ref[0]
        csends = []
        for k, mask in enumerate(ALL_MASKS):
            cp = pltpu.make_async_remote_copy(
                src_ref=c_ref.at[0], dst_ref=call_ref.at[my_dev],
                send_sem=csend.at[k], recv_sem=crecv.at[k], device_id=_peer(mask), device_id_type=MESH)
            cp.start()
            csends.append(cp)
        for k, mask in enumerate(ALL_MASKS):
            frm = _peer(mask)
            pltpu.make_async_remote_copy(
                src_ref=c_ref.at[0], dst_ref=call_ref.at[_chip_of(frm) * 2 + frm[2]],
                send_sem=csend.at[k], recv_sem=crecv.at[k], device_id=frm, device_id_type=MESH).wait_recv()
        for cp in csends:
            cp.wait_send()

        c_all = call_ref[...].reshape(N_DEV, D_MODEL).astype(BF16)
        mod_ref[my_chip] = _dot(c_all, wada_ref[...].astype(BF16))
        msends = []
        for k, mask in enumerate(CHIP_MASKS):
            cp = pltpu.make_async_remote_copy(
                src_ref=mod_ref.at[my_chip], dst_ref=mod_ref.at[my_chip],
                send_sem=msend.at[k], recv_sem=mrecv.at[k], device_id=_peer(mask), device_id_type=MESH)
            cp.start()
            msends.append(cp)
        for k, mask in enumerate(CHIP_MASKS):
            frm = _peer(mask)
            pltpu.make_async_remote_copy(
                src_ref=mod_ref.at[my_chip], dst_ref=mod_ref.at[_chip_of(frm)],
                send_sem=msend.at[k], recv_sem=mrecv.at[k], device_id=frm, device_id_type=MESH).wait_recv()
        for cp in msends:
            cp.wait_send()

        sib = _peer(1)
        forwards = []
        for idx in range(7):
            for k, mask in enumerate(CHIP_MASKS):
                frm = _peer(mask)
                wrecv_wait(idx, k, frm)
                if idx < 6:
                    cp = forward(idx, k, _chip_of(frm), me[2], sib)
                    cp.start()
                    forwards.append(cp)
        for idx in range(6):
            for k, mask in enumerate(CHIP_MASKS):
                forward(idx, k, _chip_of(_peer(mask)), 1 - me[2], sib).wait_recv()
        for cp in sends + forwards:
            cp.wait_send()
        for cp in local:
            cp.wait()

    out_shape = (
        jax.ShapeDtypeStruct((2, D_MODEL // 2, IN_W), BF16),
        jax.ShapeDtypeStruct((N_CHIPS, 2, SHARD_ROWS // 2, D_MODEL), BF16),
        jax.ShapeDtypeStruct((N_CHIPS, 2, SHARD_ROWS // 2, D_MODEL), BF16),
        jax.ShapeDtypeStruct((N_CHIPS, 2, SHARD_ROWS // 2, D_MODEL), BF16),
        jax.ShapeDtypeStruct((RNN_BLOCKS, N_CHIPS, 2, SHARD_RG // 2, RNN_BW), BF16),
        jax.ShapeDtypeStruct((RNN_BLOCKS, N_CHIPS, 2, SHARD_RG // 2, RNN_BW), BF16),
        jax.ShapeDtypeStruct((N_CHIPS, CONV_W, D_MODEL // N_CHIPS), F32),
        jax.ShapeDtypeStruct((N_DEV, 1, D_MODEL), F32),
        jax.ShapeDtypeStruct((N_CHIPS, N_DEV, SHARD_ADA), F32),
    )
    return pl.pallas_call(
        body, out_shape=out_shape, name="gather_weights",
        in_specs=[VMEM_SPEC, VMEM_SPEC] + [ANY] * 7,
        out_specs=tuple([ANY] * 7 + [VMEM_SPEC, VMEM_SPEC]),
        scratch_shapes=[
            pltpu.SemaphoreType.DMA((7, 3)), pltpu.SemaphoreType.DMA((7, 3)), pltpu.SemaphoreType.DMA((7,)),
            pltpu.SemaphoreType.DMA((7,)), pltpu.SemaphoreType.DMA((7,)),
            pltpu.SemaphoreType.DMA((3,)), pltpu.SemaphoreType.DMA((3,)),
            pltpu.SemaphoreType.DMA((6, 3)), pltpu.SemaphoreType.DMA((6, 3)),
        ],
        input_output_aliases={2: 0, 3: 1, 4: 2, 5: 3, 6: 4, 7: 5},
        compiler_params=pltpu.CompilerParams(vmem_limit_bytes=VMEM_LIMIT_V7X),
    )(c_row, w_ada_s, b_w_in.reshape(out_shape[0].shape), b_wap.reshape(out_shape[1].shape),
      b_wrp.reshape(out_shape[2].shape), b_wo.reshape(out_shape[3].shape), b_rwa.reshape(out_shape[4].shape),
      b_rwx.reshape(out_shape[5].shape), conv_w_s)


def _cast_place(shard, chip_idx, full_shape, block, index_map, tag):
    def body(chip_ref, s_ref, o_ref):
        o_ref[...] = s_ref[...].astype(BF16)

    grid_spec = pltpu.PrefetchScalarGridSpec(
        num_scalar_prefetch=1, grid=(1,),
        in_specs=[pl.BlockSpec(shard.shape, lambda i, chip_ref: (0,) * shard.ndim)],
        out_specs=pl.BlockSpec(block, lambda i, chip_ref: index_map(chip_ref[0])))
    return pl.pallas_call(
        body, out_shape=jax.ShapeDtypeStruct(full_shape, BF16), grid_spec=grid_spec, name=f"cast_place_{tag}",
        compiler_params=_cp("arbitrary"),
    )(chip_idx, shard)


HBM_SPEC = pl.BlockSpec(memory_space=pltpu.HBM)
SEM_SPEC = pl.BlockSpec(memory_space=pltpu.SEMAPHORE)


def _shard_of(ref, kind, chip):
    if kind == "in":
        return ref.at[:, pl.ds(pl.multiple_of(chip * SHARD_IN, 128), SHARD_IN)]
    return ref.at[chip] if kind == "sq" else ref.at[:, chip]


def _land_shape(src, kind):
    if kind == "in":
        return (3, src.shape[0], SHARD_IN)
    return (3,) + src.shape[1:] if kind == "sq" else (3, src.shape[0]) + src.shape[2:]


def _exchange_start(srcs, kinds, tag):
    n = len(srcs)
    lands = [pltpu.with_memory_space_constraint(lax.empty(_land_shape(s, k), F32), pltpu.HBM) for s, k in zip(srcs, kinds)]

    def body(*refs):
        src_refs, land_refs = refs[:n], refs[n:2 * n]
        ssems, rsems = refs[2 * n:3 * n], refs[3 * n:4 * n]
        token = refs[6 * n]
        for i in range(n):
            for k, mask in enumerate(CHIP_MASKS):
                to = _peer(mask)
                pltpu.make_async_remote_copy(
                    src_ref=_shard_of(src_refs[i], kinds[i], _chip_of(to)), dst_ref=land_refs[i].at[k],
                    send_sem=ssems[i], recv_sem=rsems[i], device_id=to, device_id_type=MESH).start()
        token[...] = jnp.zeros_like(token)

    sem = pltpu.SemaphoreType.DMA(())
    out_shape = ((sem,) * (2 * n) + tuple(pltpu.HBM(s.shape, s.dtype) for s in srcs)
                 + tuple(pltpu.HBM(l.shape, l.dtype) for l in lands) + (jax.ShapeDtypeStruct((8, 128), F32),))
    outs = pl.pallas_call(
        body, out_shape=out_shape, name=f"exchange_start_{tag}",
        in_specs=[HBM_SPEC] * (2 * n), out_specs=tuple([SEM_SPEC] * (2 * n) + [HBM_SPEC] * (2 * n) + [VMEM_SPEC]),
        input_output_aliases={i: 2 * n + i for i in range(2 * n)},
        compiler_params=pltpu.CompilerParams(has_side_effects=pltpu.SideEffectType.DATAFLOW_SIDE_EFFECTING),
    )(*[pltpu.with_memory_space_constraint(s, pltpu.HBM) for s in srcs], *lands)
    return outs[:n], outs[n:2 * n], outs[2 * n:3 * n], outs[3 * n:4 * n], outs[4 * n]


def _exchange_wait(ssems, rsems, srcs, lands, after, tag):
    n = len(srcs)

    def body(*refs):
        land_refs = refs[n:2 * n]
        ssem_refs, rsem_refs = refs[2 * n:3 * n], refs[3 * n:4 * n]
        for i in range(n):
            all_three = pltpu.make_async_remote_copy(
                src_ref=land_refs[i], dst_ref=land_refs[i], send_sem=ssem_refs[i], recv_sem=rsem_refs[i],
                device_id=_me(), device_id_type=MESH)
            all_three.wait_send()
            all_three.wait_recv()

    outs = pl.pallas_call(
        body, out_shape=tuple(pltpu.HBM(a.shape, a.dtype) for a in list(srcs) + list(lands)), name=f"exchange_wait_{tag}",
        in_specs=[HBM_SPEC] * (2 * n) + [SEM_SPEC] * (2 * n) + [ANY], out_specs=tuple([HBM_SPEC] * (2 * n)),
        input_output_aliases={i: i for i in range(2 * n)},
        compiler_params=pltpu.CompilerParams(has_side_effects=pltpu.SideEffectType.DATAFLOW_SIDE_EFFECTING),
    )(*srcs, *lands, *ssems, *rsems, after)
    return outs[:n], outs[n:]


def _gather_small(small):
    def body(small_ref, small_all, ssend, srecv):
        me = _me()
        my_dev = _chip_of(me) * 2 + me[2]
        small_all[my_dev] = small_ref[...]
        ssends = []
        for k, mask in enumerate(ALL_MASKS):
            cp = pltpu.make_async_remote_copy(
                src_ref=small_ref, dst_ref=small_all.at[my_dev],
                send_sem=ssend.at[k], recv_sem=srecv.at[k], device_id=_peer(mask), device_id_type=MESH)
            cp.start()
            ssends.append(cp)
        for k, mask in enumerate(ALL_MASKS):
            frm = _peer(mask)
            pltpu.make_async_remote_copy(
                src_ref=small_ref, dst_ref=small_all.at[_chip_of(frm) * 2 + frm[2]],
                send_sem=ssend.at[k], recv_sem=srecv.at[k], device_id=frm, device_id_type=MESH).wait_recv()
        for cp in ssends:
            cp.wait_send()

    return pl.pallas_call(
        body, out_shape=jax.ShapeDtypeStruct((N_DEV, SMALL_ROWS, D_MODEL), F32), name="gather_small",
        in_specs=[VMEM_SPEC], out_specs=VMEM_SPEC,
        scratch_shapes=[pltpu.SemaphoreType.DMA((7,)), pltpu.SemaphoreType.DMA((7,))],
    )(small)


def _half_of(ref, axis, half):
    return ref.at[(slice(None),) * axis + (half,)]


def _swap_halves(parts, axes):
    n = len(parts)

    def body(*refs):
        ins, outs, ssem, rsem = refs[:n], refs[n:2 * n], refs[2 * n], refs[2 * n + 1]
        c = lax.axis_index("c")
        cps = [pltpu.make_async_remote_copy(src_ref=_half_of(ins[i], axes[i], 1 - c), dst_ref=outs[i], send_sem=ssem.at[i],
                                            recv_sem=rsem.at[i], device_id=_peer(1), device_id_type=MESH) for i in range(n)]
        for cp in cps:
            cp.start()
        for cp in cps:
            cp.wait()

    shapes = [p.shape[:a] + p.shape[a + 1:] for p, a in zip(parts, axes)]
    return pl.pallas_call(
        body, out_shape=tuple(jax.ShapeDtypeStruct(s, p.dtype) for s, p in zip(shapes, parts)), name="swap_halves",
        in_specs=[ANY] * n, out_specs=tuple([ANY] * n),
        scratch_shapes=[pltpu.SemaphoreType.DMA((n,)), pltpu.SemaphoreType.DMA((n,))],
    )(*parts)


def _presum(mine, sib, c_idx, tag):
    S, _, R, C = mine.shape
    tr = min(R, 256)
    tc = SHARD_IN if C % SHARD_IN == 0 else C

    def body(c_ref, m_ref, s_ref, o_ref):
        o_ref[...] = m_ref[:, 0] + s_ref[...]

    grid_spec = pltpu.PrefetchScalarGridSpec(
        num_scalar_prefetch=1, grid=(R // tr, C // tc),
        in_specs=[pl.BlockSpec((S, 1, tr, tc), lambda i, j, c_ref: (0, c_ref[0], i, j)),
                  pl.BlockSpec((S, tr, tc), lambda i, j, c_ref: (0, i, j))],
        out_specs=pl.BlockSpec((S, tr, tc), lambda i, j, c_ref: (0, i, j)))
    return pl.pallas_call(
        body, out_shape=jax.ShapeDtypeStruct((S, R, C), F32), grid_spec=grid_spec, name=f"presum_{tag}",
        compiler_params=_cp("parallel", "parallel"),
    )(c_idx, mine, sib)


def _assemble_with_sibling(parts, axes):
    n = len(parts)

    def body(*refs):
        outs, ssem, rsem = refs[n:2 * n], refs[2 * n], refs[2 * n + 1]
        c = lax.axis_index("c")
        cps = [pltpu.make_async_remote_copy(
            src_ref=_half_of(outs[i], axes[i], c), dst_ref=_half_of(outs[i], axes[i], c), send_sem=ssem.at[i],
            recv_sem=rsem.at[i], device_id=_peer(1), device_id_type=MESH) for i in range(n)]
        for cp in cps:
            cp.start()
        for i in range(n):
            pltpu.make_async_remote_copy(
                src_ref=_half_of(outs[i], axes[i], c), dst_ref=_half_of(outs[i], axes[i], 1 - c), send_sem=ssem.at[i],
                recv_sem=rsem.at[i], device_id=_peer(1), device_id_type=MESH).wait_recv()
        for cp in cps:
            cp.wait_send()

    return pl.pallas_call(
        body, out_shape=tuple(jax.ShapeDtypeStruct(p.shape, p.dtype) for p in parts), name="assemble_with_sibling",
        in_specs=[ANY] * n, out_specs=tuple([ANY] * n), input_output_aliases={i: i for i in range(n)},
        scratch_shapes=[pltpu.SemaphoreType.DMA((n,)), pltpu.SemaphoreType.DMA((n,))],
    )(*parts)


def _rope_tables(pos_col):
    T = pos_col.shape[0]
    tm = min(T, 512)
    inv = np.float32(ROPE_THETA) ** (-(np.arange(0, 2 * ROT_HALF, 2, dtype=np.float32)) / np.float32(2 * ROT_HALF))
    lane = np.arange(128) % HEAD_DIM
    freq = np.where(lane < 2 * ROT_HALF, inv[lane % ROT_HALF], 0.0).astype(np.float32)[None, :]

    def body(pos_ref, f_ref, c_ref, sa_ref, sb_ref):
        ang = pos_ref[...].astype(F32) * f_ref[...]
        c, s = jnp.cos(ang), jnp.sin(ang)
        m = lax.broadcasted_iota(jnp.int32, ang.shape, 1) & (HEAD_DIM - 1)
        c_ref[...] = jnp.where(m < 2 * ROT_HALF, c, 1.0)
        sa_ref[...] = jnp.where(m < ROT_HALF, -s, 0.0)
        sb_ref[...] = jnp.where((m >= ROT_HALF) & (m < 2 * ROT_HALF), s, 0.0)

    tab = jax.ShapeDtypeStruct((T, 128), F32)
    return pl.pallas_call(
        body, out_shape=(tab, tab, tab), grid=(T // tm,), name="rope_tables",
        in_specs=[pl.BlockSpec((tm, 1), lambda i: (i, 0)), pl.BlockSpec((1, 128), lambda i: (0, 0))],
        out_specs=tuple(pl.BlockSpec((tm, 128), lambda i: (i, 0)) for _ in range(3)),
        compiler_params=_cp("parallel"),
    )(pos_col, jnp.asarray(freq))


def _wide(tab, width):
    return jnp.concatenate([tab] * (width // 128), axis=1)


def _rope(t, c, sa, sb):
    w = t.shape[-1]
    return t * c + pltpu.roll(t, w - ROT_HALF, 1) * sa + pltpu.roll(t, ROT_HALF, 1) * sb


def _unrope(d, c, sa, sb):
    w = d.shape[-1]
    return d * c + pltpu.roll(d * sa, ROT_HALF, 1) + pltpu.roll(d * sb, w - ROT_HALF, 1)


def _prenorm(x, mod_row, norm_g):
    T = x.shape[0]
    tm = min(T, 512)

    def body(x_ref, mod_ref, g_ref, h_ref):
        xf = x_ref[...]
        shift, scale = mod_ref[:, 0:D_MODEL], mod_ref[:, D_MODEL:2 * D_MODEL]
        h = (xf * _rms(xf)) * g_ref[...] * (1.0 + scale) + shift
        h_ref[...] = h.astype(BF16)

    return pl.pallas_call(
        body, out_shape=jax.ShapeDtypeStruct((T, D_MODEL), BF16), grid=(T // tm,), name="prenorm",
        in_specs=[pl.BlockSpec((tm, D_MODEL), lambda i: (i, 0)), pl.BlockSpec((1, ADA_W), lambda i: (0, 0)),
                  pl.BlockSpec((1, D_MODEL), lambda i: (0, 0))],
        out_specs=pl.BlockSpec((tm, D_MODEL), lambda i: (i, 0)),
        compiler_params=_cp("parallel"),
    )(x, mod_row, norm_g)


def _in_projection(h, w_in):
    T = h.shape[0]
    tm, tn = min(T, 512), SHARD_IN

    def body(h_ref, w_ref, o_ref):
        o_ref[...] = _dot(h_ref[...], w_ref[...])

    return pl.pallas_call(
        body, out_shape=jax.ShapeDtypeStruct((T, IN_W), F32), grid=(IN_W // tn, T // tm), name="in_projection",
        in_specs=[pl.BlockSpec((tm, D_MODEL), lambda j, i: (i, 0)), pl.BlockSpec((D_MODEL, tn), lambda j, i: (0, j))],
        out_specs=pl.BlockSpec((tm, tn), lambda j, i: (i, j)),
        compiler_params=_cp("parallel", "parallel"),
    )(h, w_in)


def _attn_mask(n):
    qi = lax.broadcasted_iota(jnp.int32, (GROUP * BLOCK, 2 * BLOCK), 0) & (BLOCK - 1)
    kj = lax.broadcasted_iota(jnp.int32, (GROUP * BLOCK, 2 * BLOCK), 1)
    diff = qi + BLOCK - kj
    return (diff >= 0) & (diff < BLOCK) & ((kj >= BLOCK) | (n > 0))


def _sink_col(sink_ref, kh):
    rowg = lax.broadcasted_iota(jnp.int32, (GROUP * BLOCK, 1), 0) // BLOCK
    col = jnp.full((GROUP * BLOCK, 1), sink_ref[0, GROUP * kh], F32)
    for g in range(1, GROUP):
        col = jnp.where(rowg == g, sink_ref[0, GROUP * kh + g], col)
    return col


def _attn_probs(qr, kr_prev, kr_cur, v_prev, v_cur, kh, sink_col, mask):
    heads = [qr[:, HEAD_DIM * (GROUP * kh + g): HEAD_DIM * (GROUP * kh + g + 1)] for g in range(GROUP)]
    qs = jnp.concatenate(heads, axis=0).astype(BF16)
    lo, hi = HEAD_DIM * kh, HEAD_DIM * (kh + 1)
    kk = jnp.concatenate([kr_prev[:, lo:hi], kr_cur[:, lo:hi]], axis=0).astype(BF16)
    vv = jnp.concatenate([v_prev[:, lo:hi], v_cur[:, lo:hi]], axis=0).astype(BF16)
    s = _dot_nt(qs, kk) * (1.0 / 8.0)
    s = jnp.where(mask, s, -1e30)
    m = jnp.maximum(jnp.max(s, axis=-1, keepdims=True), sink_col)
    p = jnp.exp(s - m)
    p_sink = jnp.exp(sink_col - m)
    denom = jnp.sum(p, axis=-1, keepdims=True) + p_sink
    return qs, kk, vv, p / denom, p_sink / denom


def _unstack_heads(parts):
    cols = []
    for kh in range(N_KV):
        for g in range(GROUP):
            cols.append(parts[kh][g * BLOCK:(g + 1) * BLOCK, :])
    return jnp.concatenate(cols, axis=1)


def _attn_forward(proj, tabs, sinks):
    T = proj.shape[0]
    nb = T // BLOCK

    def body(q_ref, kvc_ref, kvp_ref, g0_ref, g1_ref, cc, sac, sbc, cp_, sap, sbp, sink_ref, y_ref):
        n = pl.program_id(0)
        tc = (_wide(cc[...], D_MODEL), _wide(sac[...], D_MODEL), _wide(sbc[...], D_MODEL))
        tcur = tuple(t[:, :KV_W] for t in tc)
        tprev = (_wide(cp_[...], KV_W), _wide(sap[...], KV_W), _wide(sbp[...], KV_W))
        qr = _rope(q_ref[...], *tc)
        kr_cur = _rope(kvc_ref[:, 0:KV_W], *tcur)
        kr_prev = _rope(kvp_ref[:, 0:KV_W], *tprev)
        v_cur, v_prev = kvc_ref[:, KV_W:2 * KV_W], kvp_ref[:, KV_W:2 * KV_W]
        mask = _attn_mask(n)
        outs = []
        for kh in range(N_KV):
            _, _, vv, pn, _ = _attn_probs(qr, kr_prev, kr_cur, v_prev, v_cur, kh, _sink_col(sink_ref, kh), mask)
            outs.append(_dot(pn.astype(BF16), vv))
        o = _unstack_heads(outs)
        g = jnp.concatenate([g0_ref[...], g1_ref[...]], axis=1)
        y_ref[...] = (o * (g * _sigmoid(g))).astype(BF16)

    def blk(w, cb):
        return pl.BlockSpec((BLOCK, w), lambda n, cb=cb: (n, cb))

    prev = lambda w, cb: pl.BlockSpec((BLOCK, w), lambda n, cb=cb: (jnp.maximum(n - 1, 0), cb))
    return pl.pallas_call(
        body, out_shape=jax.ShapeDtypeStruct((T, D_MODEL), BF16), grid=(nb,), name="attn_forward",
        in_specs=[blk(D_MODEL, 0), blk(CB, CB_KV), prev(CB, CB_KV), blk(CB, CB_GA), blk(CB, CB_GA + 1),
                  blk(128, 0), blk(128, 0), blk(128, 0), prev(128, 0), prev(128, 0), prev(128, 0),
                  pl.BlockSpec(memory_space=pltpu.SMEM)],
        out_specs=pl.BlockSpec((BLOCK, D_MODEL), lambda n: (n, 0)),
        compiler_params=_cp("parallel"),
    )(proj, proj, proj, proj, proj, *tabs, *tabs, sinks)


def _scan_rows8():
    return lax.broadcasted_iota(jnp.int32, (8, D_MODEL), 0)


def _scan_forward(a_ref, b_ref, h_ref, carry, rows):
    row = _scan_rows8()

    def group(i, carry):
        off = pl.multiple_of(i * 8, 8)
        a, b = a_ref[pl.ds(off, 8), :], b_ref[pl.ds(off, 8), :]
        for d in (1, 2, 4):
            ok = row >= d
            b = jnp.where(ok, a * pltpu.roll(b, d, 0) + b, b)
            a = jnp.where(ok, a * pltpu.roll(a, d, 0), a)
        h = a * carry + b
        h_ref[pl.ds(off, 8), :] = h
        return h[7:8, :]

    return lax.fori_loop(0, rows // 8, group, carry)


def _scan_backward(a_ref, g_ref, lam_ref, carry, rows):
    row = _scan_rows8()

    def group(i, carry):
        off = pl.multiple_of((rows // 8 - 1 - i) * 8, 8)
        a, g = a_ref[pl.ds(off, 8), :], g_ref[pl.ds(off, 8), :]
        b = a * g
        for d in (1, 2, 4):
            ok = row < 8 - d
            b = jnp.where(ok, a * pltpu.roll(b, 8 - d, 0) + b, b)
            a = jnp.where(ok, a * pltpu.roll(a, 8 - d, 0), a)
        mu = a * carry + b
        mu_below = jnp.where(row == 7, carry, pltpu.roll(mu, 7, 0))
        lam_ref[pl.ds(off, 8), :] = g + mu_below
        return mu[0:1, :]

    return lax.fori_loop(0, rows // 8, group, carry)


def _rnn_recompute(xbuf, xr, tail, cw, cb, wa_ref, wx_ref, ba, bx, sp, reset):
    rows = xr.shape[0]
    xbuf[0:8, :] = tail
    xbuf[8:rows + 8, :] = xr
    xs = [xbuf[pl.ds(8 - (CONV_W - 1 - k), rows), :] for k in range(CONV_W - 1)] + [xr]
    xc = xs[0] * cw[0:1, :]
    for k in range(1, CONV_W):
        xc = xc + xs[k] * cw[k:k + 1, :]
    xc = xc + cb
    xcb = xc.astype(BF16)
    za = jnp.concatenate([_dot(xcb[:, RNN_BW * j:RNN_BW * (j + 1)], wa_ref[j]) for j in range(RNN_BLOCKS)], axis=1) + ba
    zx = jnp.concatenate([_dot(xcb[:, RNN_BW * j:RNN_BW * (j + 1)], wx_ref[j]) for j in range(RNN_BLOCKS)], axis=1) + bx
    r, i = _sigmoid(za), _sigmoid(zx)
    log_a = -LRU_C * r * sp
    a_raw = jnp.exp(log_a)
    mult_raw = jnp.sqrt(_neg_expm1(2.0 * log_a))
    a = jnp.where(reset, 0.0, a_raw)
    mult = jnp.where(reset, 1.0, mult_raw)
    return xs, xc, xcb, r, i, a_raw, mult_raw, a, mult


def _rnn_forward(proj, pos_col, conv_w, conv_b, rwa, rwx, ba, bx, lam):
    T = proj.shape[0]
    tr = min(T, 256)

    def body(x0, x1, g0, g1, pos_ref, cw_ref, cb_ref, wa_ref, wx_ref, ba_ref, bx_ref, lam_ref,
             y_ref, h_ref, xbuf, abuf, bbuf, tail, carry):
        t = pl.program_id(0)

        @pl.when(t == 0)
        def _():
            tail[...] = jnp.zeros_like(tail)
            carry[...] = jnp.zeros_like(carry)

        xr = jnp.concatenate([x0[...], x1[...]], axis=1)
        sp = _softplus(-lam_ref[...])
        reset = pos_ref[...] == 0
        _, xc, _, _, i, _, _, a, mult = _rnn_recompute(
            xbuf, xr, tail[...], cw_ref[...], cb_ref[...], wa_ref, wx_ref, ba_ref[...], bx_ref[...], sp, reset)
        abuf[...] = a
        bbuf[...] = mult * (i * xc)
        last = _scan_forward(abuf, bbuf, h_ref, carry[0:1, :], tr)
        carry[...] = jnp.broadcast_to(last, carry.shape)
        tail[...] = xr[tr - 8:tr, :]
        g = jnp.concatenate([g0[...], g1[...]], axis=1)
        y_ref[...] = (h_ref[...] * (g * _sigmoid(g))).astype(BF16)

    blk = lambda cb: pl.BlockSpec((tr, CB), lambda t, cb=cb: (t, cb))
    row = lambda w: pl.BlockSpec((1, w), lambda t: (0, 0))
    full3 = pl.BlockSpec((RNN_BLOCKS, RNN_BW, RNN_BW), lambda t: (0, 0, 0))
    return pl.pallas_call(
        body, out_shape=(jax.ShapeDtypeStruct((T, D_MODEL), BF16), jax.ShapeDtypeStruct((T, D_MODEL), F32)),
        grid=(T // tr,), name="rnn_forward",
        in_specs=[blk(CB_XR), blk(CB_XR + 1), blk(CB_GR), blk(CB_GR + 1), pl.BlockSpec((tr, 1), lambda t: (t, 0)),
                  pl.BlockSpec((CONV_W, D_MODEL), lambda t: (0, 0)), row(D_MODEL), full3, full3,
                  row(D_MODEL), row(D_MODEL), row(D_MODEL)],
        out_specs=(pl.BlockSpec((tr, D_MODEL), lambda t: (t, 0)), pl.BlockSpec((tr, D_MODEL), lambda t: (t, 0))),
        scratch_shapes=[pltpu.VMEM((tr + 8, D_MODEL), F32), pltpu.VMEM((tr, D_MODEL), F32), pltpu.VMEM((tr, D_MODEL), F32),
                        pltpu.VMEM((8, D_MODEL), F32), pltpu.VMEM((8, D_MODEL), F32)],
        compiler_params=_cp("arbitrary"),
    )(proj, proj, proj, proj, pos_col, conv_w, conv_b, rwa, rwx, ba, bx, lam)


def _merge_and_head(x, target, y_attn, y_rnn, proj, wap, wrp, wo, mod_row, final_g):
    T = x.shape[0]
    tm = min(T, 256)

    def body(x_ref, t_ref, ya_ref, yr_ref, ma0, ma1, mr0, mr1, wap_ref, wrp_ref, wo_ref, mod_ref, fg_ref,
             dx2_ref, mg_ref, do_ref, dpa_ref, dpr_ref, dya_ref, dyr_ref, dc_ref, dfg_ref, dgate_ref, loss_ref):
        i = pl.program_id(0)
        gate = mod_ref[:, 2 * D_MODEL:3 * D_MODEL]
        ya, yr = ya_ref[...], yr_ref[...]
        pa, pr = _dot(ya, wap_ref[...]), _dot(yr, wrp_ref[...])
        sa = _sigmoid(jnp.concatenate([ma0[...], ma1[...]], axis=1))
        sr = _sigmoid(jnp.concatenate([mr0[...], mr1[...]], axis=1))
        merged = sa * pa + sr * pr
        mb = merged.astype(BF16)
        o = _dot(mb, wo_ref[...])
        x2 = x_ref[...] + gate * o
        r2 = _rms(x2)
        xn2 = x2 * r2
        fg = fg_ref[...]
        err = xn2 * fg - t_ref[...]
        loss_t = 0.5 * jnp.sum(jnp.sum(err * err, axis=-1, keepdims=True) * (1.0 / D_MODEL), axis=0, keepdims=True)
        dy = err * (1.0 / D_MODEL)
        dfg_t = jnp.sum(dy * xn2, axis=0, keepdims=True)
        dxn = dy * fg
        dx2 = r2 * (dxn - xn2 * jnp.mean(dxn * xn2, axis=-1, keepdims=True))
        dgate_t = jnp.sum(dx2 * o, axis=0, keepdims=True)
        dob = (dx2 * gate).astype(BF16)
        dmerged = _dot_nt(dob, wo_ref[...])
        dpa = (dmerged * sa).astype(BF16)
        dpr = (dmerged * sr).astype(BF16)
        dx2_ref[...] = dx2
        mg_ref[...] = mb
        do_ref[...] = dob
        dpa_ref[...] = dpa
        dpr_ref[...] = dpr
        dya_ref[...] = _dot_nt(dpa, wap_ref[...])
        dyr_ref[...] = _dot_nt(dpr, wrp_ref[...])
        dc_ref[:, 0:D_MODEL] = (dmerged * pa * sa * (1.0 - sa)).astype(BF16)
        dc_ref[:, D_MODEL:2 * D_MODEL] = (dmerged * pr * sr * (1.0 - sr)).astype(BF16)

        @pl.when(i == 0)
        def _():
            dfg_ref[...] = jnp.zeros_like(dfg_ref)
            dgate_ref[...] = jnp.zeros_like(dgate_ref)
            loss_ref[...] = jnp.zeros_like(loss_ref)

        dfg_ref[...] += dfg_t
        dgate_ref[...] += dgate_t
        loss_ref[...] += jnp.broadcast_to(loss_t, loss_ref.shape)

    tok = lambda w: pl.BlockSpec((tm, w), lambda i: (i, 0))
    blk = lambda cb: pl.BlockSpec((tm, CB), lambda i, cb=cb: (i, cb))
    wfull = pl.BlockSpec((D_MODEL, D_MODEL), lambda i: (0, 0))
    row = lambda w: pl.BlockSpec((1, w), lambda i: (0, 0))
    out_shape = (
        jax.ShapeDtypeStruct((T, D_MODEL), F32), jax.ShapeDtypeStruct((T, D_MODEL), BF16),
        jax.ShapeDtypeStruct((T, D_MODEL), BF16), jax.ShapeDtypeStruct((T, D_MODEL), BF16),
        jax.ShapeDtypeStruct((T, D_MODEL), BF16), jax.ShapeDtypeStruct((T, D_MODEL), F32),
        jax.ShapeDtypeStruct((T, D_MODEL), F32), jax.ShapeDtypeStruct((T, 2 * D_MODEL), BF16),
        jax.ShapeDtypeStruct((1, D_MODEL), F32), jax.ShapeDtypeStruct((1, D_MODEL), F32),
        jax.ShapeDtypeStruct((1, 128), F32),
    )
    return pl.pallas_call(
        body, out_shape=out_shape, grid=(T // tm,), name="merge_and_head",
        in_specs=[tok(D_MODEL), tok(D_MODEL), tok(D_MODEL), tok(D_MODEL), blk(CB_MA), blk(CB_MA + 1), blk(CB_MR),
                  blk(CB_MR + 1), wfull, wfull, wfull, row(ADA_W), row(D_MODEL)],
        out_specs=(tok(D_MODEL),) * 7 + (tok(2 * D_MODEL), row(D_MODEL), row(D_MODEL), row(128)),
        compiler_params=_cp("arbitrary"),
    )(x, target, y_attn, y_rnn, proj, proj, proj, proj, wap, wrp, wo, mod_row, final_g)


def _attn_backward(proj, d_y, tabs, sinks):
    T = proj.shape[0]
    nb = T // BLOCK

    def body(q_ref, kvc_ref, kvp_ref, g0_ref, g1_ref, dy_ref, cc, sac, sbc, cp_, sap, sbp, sink_ref,
             dq_ref, dkv_ref, dg_ref, dsink_ref, carry):
        n = pl.program_id(0)

        @pl.when(n == 0)
        def _():
            carry[...] = jnp.zeros_like(carry)
            dsink_ref[...] = jnp.zeros_like(dsink_ref)

        @pl.when(n < nb)
        def _():
            tc = (_wide(cc[...], D_MODEL), _wide(sac[...], D_MODEL), _wide(sbc[...], D_MODEL))
            tcur = tuple(t[:, :KV_W] for t in tc)
            tprev = (_wide(cp_[...], KV_W), _wide(sap[...], KV_W), _wide(sbp[...], KV_W))
            qr = _rope(q_ref[...], *tc)
            kr_cur = _rope(kvc_ref[:, 0:KV_W], *tcur)
            kr_prev = _rope(kvp_ref[:, 0:KV_W], *tprev)
            v_cur, v_prev = kvc_ref[:, KV_W:2 * KV_W], kvp_ref[:, KV_W:2 * KV_W]
            g = jnp.concatenate([g0_ref[...], g1_ref[...]], axis=1)
            sg = _sigmoid(g)
            dy = dy_ref[...]
            d_o = dy * (g * sg)
            mask = _attn_mask(n)
            lane = lax.broadcasted_iota(jnp.int32, (1, 128), 1)
            rowg = lax.broadcasted_iota(jnp.int32, (GROUP * BLOCK, 1), 0) // BLOCK
            o_parts, dq_parts, dk_parts, dv_parts = [], [], [], []
            dsink = jnp.zeros((1, 128), F32)
            for kh in range(N_KV):
                qs, kk, vv, pn, pn_sink = _attn_probs(qr, kr_prev, kr_cur, v_prev, v_cur, kh, _sink_col(sink_ref, kh), mask)
                pnb = pn.astype(BF16)
                o_parts.append(_dot(pnb, vv))
                dos = jnp.concatenate(
                    [d_o[:, HEAD_DIM * (GROUP * kh + gq): HEAD_DIM * (GROUP * kh + gq + 1)] for gq in range(GROUP)],
                    axis=0).astype(BF16)
                dpn = _dot_nt(dos, vv)
                delta = jnp.sum(pn * dpn, axis=-1, keepdims=True)
                dsb = (pn * (dpn - delta) * (1.0 / 8.0)).astype(BF16)
                dq_parts.append(_dot(dsb, kk))
                dk_parts.append(_dot_tn(dsb, qs))
                dv_parts.append(_dot_tn(pnb, dos))
                ds_rows = pn_sink * delta
                for gq in range(GROUP):
                    val = -jnp.sum(jnp.where(rowg == gq, ds_rows, 0.0), axis=0, keepdims=True)
                    dsink = dsink + jnp.where(lane == GROUP * kh + gq, val, 0.0)
            o = _unstack_heads(o_parts)
            dg_ref[...] = (dy * o * (sg * (1.0 + g * (1.0 - sg)))).astype(BF16)
            dq_ref[...] = _unrope(_unstack_heads(dq_parts), *tc).astype(BF16)
            dk_prev = _unrope(jnp.concatenate([p[0:BLOCK, :] for p in dk_parts], axis=1), *tprev)
            dk_cur = _unrope(jnp.concatenate([p[BLOCK:2 * BLOCK, :] for p in dk_parts], axis=1), *tcur)
            dv_prev = jnp.concatenate([p[0:BLOCK, :] for p in dv_parts], axis=1)
            dv_cur = jnp.concatenate([p[BLOCK:2 * BLOCK, :] for p in dv_parts], axis=1)
            dkv_ref[...] = (carry[...] + jnp.concatenate([dk_prev, dv_prev], axis=1)).astype(BF16)
            carry[...] = jnp.concatenate([dk_cur, dv_cur], axis=1)
            dsink_ref[...] += dsink

        @pl.when(n == nb)
        def _():
            dkv_ref[...] = carry[...].astype(BF16)

    cur = lambda w, cb: pl.BlockSpec((BLOCK, w), lambda n, cb=cb: (jnp.minimum(n, nb - 1), cb))
    prev = lambda w, cb: pl.BlockSpec((BLOCK, w), lambda n, cb=cb: (jnp.maximum(jnp.minimum(n, nb - 1) - 1, 0), cb))
    out_shape = (jax.ShapeDtypeStruct((T, D_MODEL), BF16), jax.ShapeDtypeStruct((T, 2 * KV_W), BF16),
                 jax.ShapeDtypeStruct((T, D_MODEL), BF16), jax.ShapeDtypeStruct((1, 128), F32))
    return pl.pallas_call(
        body, out_shape=out_shape, grid=(nb + 1,), name="attn_backward",
        in_specs=[cur(D_MODEL, 0), cur(CB, CB_KV), prev(CB, CB_KV), cur(CB, CB_GA), cur(CB, CB_GA + 1), cur(D_MODEL, 0),
                  cur(128, 0), cur(128, 0), cur(128, 0), prev(128, 0), prev(128, 0), prev(128, 0),
                  pl.BlockSpec(memory_space=pltpu.SMEM)],
        out_specs=(cur(D_MODEL, 0), pl.BlockSpec((BLOCK, 2 * KV_W), lambda n: (jnp.maximum(n - 1, 0), 0)),
                   cur(D_MODEL, 0), pl.BlockSpec((1, 128), lambda n: (0, 0))),
        scratch_shapes=[pltpu.VMEM((BLOCK, 2 * KV_W), F32)],
        compiler_params=_cp("arbitrary"),
    )(proj, proj, proj, proj, proj, d_y, *tabs, *tabs, sinks)


def _rnn_backward(proj, pos_col, h_rnn, d_y, conv_w, conv_b, rwa, rwx, ba, bx, lam):
    T = proj.shape[0]
    tr = min(T, 256)
    nt = T // tr
    hb = tr // 8

    def body(x0, x1, xh0, xh1, g0, g1, pos_ref, h_ref, hh_ref, dy_ref, cw_ref, cb_ref, wa_ref, wx_ref, ba_ref, bx_ref,
             lam_ref, db_ref, dcw_ref, dcb_ref, dwa_ref, dwx_ref, dba_ref, dbx_ref, dlam_ref,
             xbuf, hbuf, dbuf, abuf, gbuf, lbuf, mu_carry, dxc_head):
        step = pl.program_id(0)
        first_tile = step == nt - 1

        @pl.when(step == 0)
        def _():
            mu_carry[...] = jnp.zeros_like(mu_carry)
            dxc_head[...] = jnp.zeros_like(dxc_head)
            for ref in (dcw_ref, dcb_ref, dwa_ref, dwx_ref, dba_ref, dbx_ref, dlam_ref):
                ref[...] = jnp.zeros_like(ref)

        xr = jnp.concatenate([x0[...], x1[...]], axis=1)
        tail = jnp.where(first_tile, 0.0, jnp.concatenate([xh0[...], xh1[...]], axis=1))
        lam_v = lam_ref[...]
        sp = _softplus(-lam_v)
        reset = pos_ref[...] == 0
        cw = cw_ref[...]
        xs, xc, xcb, r, i, a_raw, mult_raw, a, mult = _rnn_recompute(
            xbuf, xr, tail, cw, cb_ref[...], wa_ref, wx_ref, ba_ref[...], bx_ref[...], sp, reset)
        g = jnp.concatenate([g0[...], g1[...]], axis=1)
        sg = _sigmoid(g)
        dy = dy_ref[...]
        h = h_ref[...]
        d_g = dy * h * (sg * (1.0 + g * (1.0 - sg)))
        abuf[...] = a
        gbuf[...] = dy * (g * sg)
        top = _scan_backward(abuf, gbuf, lbuf, mu_carry[0:1, :], tr)
        mu_carry[...] = jnp.broadcast_to(top, mu_carry.shape)
        lam_t = lbuf[...]
        hbuf[0:8, :] = jnp.where(first_tile, 0.0, hh_ref[...])
        hbuf[8:tr + 8, :] = h
        h_prev = hbuf[pl.ds(7, tr), :]
        live = jnp.logical_not(reset)
        d_a = jnp.where(live, lam_t * h_prev, 0.0)
        d_mult = jnp.where(live, lam_t * (i * xc), 0.0)
        d_ixc = lam_t * mult
        d_i = d_ixc * xc
        d_xc = d_ixc * i
        d_log_a = d_a * a_raw - d_mult * (a_raw * a_raw / mult_raw)
        d_log_a = jnp.where(live, d_log_a, 0.0)
        d_za = d_log_a * (-LRU_C * sp) * (r * (1.0 - r))
        d_zx = d_i * (i * (1.0 - i))
        dlam_ref[...] += jnp.sum(d_log_a * r, axis=0, keepdims=True) * (LRU_C * _sigmoid(-lam_v))
        dba_ref[...] += jnp.sum(d_za, axis=0, keepdims=True)
        dbx_ref[...] += jnp.sum(d_zx, axis=0, keepdims=True)
        dzab, dzxb = d_za.astype(BF16), d_zx.astype(BF16)
        back = []
        for j in range(RNN_BLOCKS):
            sl = slice(RNN_BW * j, RNN_BW * (j + 1))
            dwa_ref[j] += _dot_tn(xcb[:, sl], dzab[:, sl])
            dwx_ref[j] += _dot_tn(xcb[:, sl], dzxb[:, sl])
            back.append(_dot_nt(dzab[:, sl], wa_ref[j]) + _dot_nt(dzxb[:, sl], wx_ref[j]))
        d_xc = d_xc + jnp.concatenate(back, axis=1)
        dcb_ref[...] += jnp.sum(d_xc, axis=0, keepdims=True)
        for k in range(CONV_W):
            dcw_ref[k:k + 1, :] += jnp.sum(d_xc * xs[k], axis=0, keepdims=True)
        dbuf[0:tr, :] = d_xc
        dbuf[tr:tr + 8, :] = dxc_head[...]
        d_xr = d_xc * cw[CONV_W - 1:CONV_W, :]
        for k in range(CONV_W - 1):
            d_xr = d_xr + dbuf[pl.ds(CONV_W - 1 - k, tr), :] * cw[k:k + 1, :]
        dxc_head[...] = d_xc[0:8, :]
        db_ref[:, 0:D_MODEL] = d_xr.astype(BF16)
        db_ref[:, D_MODEL:2 * D_MODEL] = d_g.astype(BF16)

    rev = lambda s: nt - 1 - s
    blk = lambda cb: pl.BlockSpec((tr, CB), lambda s, cb=cb: (rev(s), cb))
    halo = lambda w, cb: pl.BlockSpec((8, w), lambda s, cb=cb: (jnp.maximum(rev(s) * hb - 1, 0), cb))
    tok = lambda w: pl.BlockSpec((tr, w), lambda s: (rev(s), 0))
    row = lambda w: pl.BlockSpec((1, w), lambda s: (0, 0))
    full3 = pl.BlockSpec((RNN_BLOCKS, RNN_BW, RNN_BW), lambda s: (0, 0, 0))
    cwspec = pl.BlockSpec((CONV_W, D_MODEL), lambda s: (0, 0))
    vec = jax.ShapeDtypeStruct((1, D_MODEL), F32)
    gate_w = jax.ShapeDtypeStruct((RNN_BLOCKS, RNN_BW, RNN_BW), F32)
    out_shape = (jax.ShapeDtypeStruct((T, 2 * D_MODEL), BF16), jax.ShapeDtypeStruct((CONV_W, D_MODEL), F32), vec,
                 gate_w, gate_w, vec, vec, vec)
    big = lambda: pltpu.VMEM((tr, D_MODEL), F32)
    ext = lambda: pltpu.VMEM((tr + 8, D_MODEL), F32)
    return pl.pallas_call(
        body, out_shape=out_shape, grid=(nt,), name="rnn_backward",
        in_specs=[blk(CB_XR), blk(CB_XR + 1), halo(CB, CB_XR), halo(CB, CB_XR + 1), blk(CB_GR), blk(CB_GR + 1),
                  pl.BlockSpec((tr, 1), lambda s: (rev(s), 0)), tok(D_MODEL), halo(D_MODEL, 0), tok(D_MODEL),
                  cwspec, row(D_MODEL), full3, full3, row(D_MODEL), row(D_MODEL), row(D_MODEL)],
        out_specs=(tok(2 * D_MODEL), cwspec, row(D_MODEL), full3, full3, row(D_MODEL), row(D_MODEL), row(D_MODEL)),
        scratch_shapes=[ext(), ext(), ext(), big(), big(), big(), pltpu.VMEM((8, D_MODEL), F32), pltpu.VMEM((8, D_MODEL), F32)],
        compiler_params=_cp("arbitrary"),
    )(proj, proj, proj, proj, proj, proj, pos_col, h_rnn, h_rnn, d_y, conv_w, conv_b, rwa, rwx, ba, bx, lam)


def _input_backward(pieces, w_in, x, dx2, mod_row, norm_g):
    T = x.shape[0]
    tm = min(T, 512)
    n = len(pieces)

    def body(*refs):
        d_refs = refs[:n]
        w_ref, x_ref, dx2_ref, mod_ref, g_ref, gx_ref, dshift_ref, dscale_ref, dg_ref, acc = refs[n:]
        i, k = pl.program_id(0), pl.program_id(1)

        @pl.when(k == 0)
        def _():
            acc[...] = jnp.zeros_like(acc)

        for d_ref, (_, start, count) in zip(d_refs, pieces):
            @pl.when((k >= start) & (k < start + count))
            def _(d_ref=d_ref):
                acc[...] += _dot_nt(d_ref[...], w_ref[...])

        @pl.when((i == 0) & (k == 0))
        def _():
            dshift_ref[...] = jnp.zeros_like(dshift_ref)
            dscale_ref[...] = jnp.zeros_like(dscale_ref)
            dg_ref[...] = jnp.zeros_like(dg_ref)

        @pl.when(k == N_CB - 1)
        def _():
            dh = acc[...]
            xf = x_ref[...]
            r1 = _rms(xf)
            xn = xf * r1
            gn = g_ref[...]
            s1 = 1.0 + mod_ref[:, D_MODEL:2 * D_MODEL]
            dshift_ref[...] += jnp.sum(dh, axis=0, keepdims=True)
            dscale_ref[...] += jnp.sum(dh * (xn * gn), axis=0, keepdims=True)
            dg_ref[...] += jnp.sum(dh * s1 * xn, axis=0, keepdims=True)
            dxn = dh * s1 * gn
            gx_ref[...] = dx2_ref[...] + r1 * (dxn - xn * jnp.mean(dxn * xn, axis=-1, keepdims=True))

    def piece_spec(start, count):
        return pl.BlockSpec((tm, CB), lambda i, k: (i, jnp.clip(k - start, 0, count - 1)))

    tok = pl.BlockSpec((tm, D_MODEL), lambda i, k: (i, 0))
    row = lambda w: pl.BlockSpec((1, w), lambda i, k: (0, 0))
    vec = jax.ShapeDtypeStruct((1, D_MODEL), F32)
    return pl.pallas_call(
        body, out_shape=(jax.ShapeDtypeStruct((T, D_MODEL), F32), vec, vec, vec), grid=(T // tm, N_CB), name="input_backward",
        in_specs=[piece_spec(s, c) for _, s, c in pieces]
        + [pl.BlockSpec((D_MODEL, CB), lambda i, k: (0, k)), tok, tok, row(ADA_W), row(D_MODEL)],
        out_specs=(tok, row(D_MODEL), row(D_MODEL), row(D_MODEL)),
        scratch_shapes=[pltpu.VMEM((tm, D_MODEL), F32)],
        compiler_params=_cp("arbitrary", "arbitrary"),
    )(*[p[0] for p in pieces], w_in, x, dx2, mod_row, norm_g)


def _weight_grad(a, b, tag, into=None, col_block=0, total_cols=None):
    T, M = a.shape
    N = b.shape[1]
    tk = min(T, 512)
    tn = CB
    total_cols = N if total_cols is None else total_cols

    def body(*refs):
        a_ref, b_ref, o_ref = refs[0], refs[1], refs[-1]
        k = pl.program_id(1)

        @pl.when(k == 0)
        def _():
            o_ref[...] = jnp.zeros_like(o_ref)

        o_ref[...] += _dot_tn(a_ref[...], b_ref[...])

    in_specs = [pl.BlockSpec((tk, M), lambda j, k: (k, 0)), pl.BlockSpec((tk, tn), lambda j, k: (k, j))]
    args = [a, b]
    aliases = {}
    if into is not None:
        in_specs.append(ANY)
        args.append(into)
        aliases = {2: 0}
    return pl.pallas_call(
        body, out_shape=jax.ShapeDtypeStruct((M, total_cols), F32), grid=(N // tn, T // tk), name=f"weight_grad_{tag}",
        in_specs=in_specs, out_specs=pl.BlockSpec((M, tn), lambda j, k: (0, col_block + j)),
        input_output_aliases=aliases, compiler_params=_cp("parallel", "arbitrary"),
    )(*args)


def _adamw(w, g, m, v):
    m = ADAM_B1 * m + (1.0 - ADAM_B1) * g
    v = ADAM_B2 * v + (1.0 - ADAM_B2) * (g * g)
    m_hat = m / (1.0 - ADAM_B1 ** ADAM_STEP)
    v_hat = v / (1.0 - ADAM_B2 ** ADAM_STEP)
    delta = -ADAM_LR * (m_hat / (jnp.sqrt(v_hat) + ADAM_EPS) + ADAM_WD * w)
    return delta, m, v


def _sum_landed(kind, own, land, where, tag):
    if kind == "in":
        R, C = land.shape[1:]
        tr = 256
        grid = (R // tr,)
        own_spec = pl.BlockSpec((tr, C), lambda i, w: (i, w[0]))
        land_spec = pl.BlockSpec((3, tr, C), lambda i, w: (0, i, 0))
        out_spec = pl.BlockSpec((1, tr, C), lambda i, w: (w[1], i, 0))
        out_shape = (2, R, C)
        pick = lambda ref: ref[...]
    elif kind == "sq":
        R, C = land.shape[1:]
        grid = (1,)
        own_spec = pl.BlockSpec((1, R, C), lambda i, w: (w[0], 0, 0))
        land_spec = pl.BlockSpec((3, R, C), lambda i, w: (0, 0, 0))
        out_spec = pl.BlockSpec((1, R, C), lambda i, w: (w[1], 0, 0))
        out_shape = (2, R, C)
        pick = lambda ref: ref[0]
    else:
        B, R, C = land.shape[1:]
        grid = (1,)
        own_spec = pl.BlockSpec((B, 1, R, C), lambda i, w: (0, w[0], 0, 0))
        land_spec = pl.BlockSpec((3, B, R, C), lambda i, w: (0, 0, 0, 0))
        out_spec = pl.BlockSpec((B, 1, R, C), lambda i, w: (0, w[1], 0, 0))
        out_shape = (B, 2, R, C)
        pick = lambda ref: ref[:, 0]

    def body(w_ref, own_ref, l_ref, o_ref):
        total = ((pick(own_ref) + l_ref[0]) + l_ref[1]) + l_ref[2]
        if kind == "in":
            o_ref[0] = total
        elif kind == "sq":
            o_ref[0] = total
        else:
            o_ref[:, 0] = total

    grid_spec = pltpu.PrefetchScalarGridSpec(num_scalar_prefetch=1, grid=grid, in_specs=[own_spec, land_spec], out_specs=out_spec)
    return pl.pallas_call(
        body, out_shape=jax.ShapeDtypeStruct(out_shape, F32), grid_spec=grid_spec, name=f"sum_landed_{tag}",
        compiler_params=_cp("parallel"),
    )(where, own, land)


def _adamw_shard(g, w, m, v, tag):
    R, C = w.shape
    tr = min(R, 256)

    def body(g_ref, w_ref, m_ref, v_ref, d_ref, nm_ref, nv_ref):
        d, nm, nv = _adamw(w_ref[...], g_ref[...], m_ref[...], v_ref[...])
        d_ref[...] = d
        nm_ref[...] = nm
        nv_ref[...] = nv

    spec = pl.BlockSpec((tr, C), lambda i: (i, 0))
    sds = jax.ShapeDtypeStruct((R, C), F32)
    return pl.pallas_call(
        body, out_shape=(sds,) * 3, grid=(R // tr,), name=f"adamw_{tag}",
        in_specs=[spec] * 4, out_specs=(spec,) * 3, compiler_params=_cp("parallel"),
    )(g, w, m, v)


def _adamw_w_ada(c_t, dmod_cols, w, m, v):
    R, C = w.shape

    def body(ct_ref, dm_ref, w_ref, m_ref, v_ref, g_ref, d_ref, nm_ref, nv_ref):
        g = _dot(ct_ref[...].astype(BF16), dm_ref[...].astype(BF16))
        d, nm, nv = _adamw(w_ref[...], g, m_ref[...], v_ref[...])
        g_ref[...] = g
        d_ref[...] = d
        nm_ref[...] = nm
        nv_ref[...] = nv

    tr = 256
    spec = pl.BlockSpec((tr, C), lambda i: (i, 0))
    sds = jax.ShapeDtypeStruct((R, C), F32)
    return pl.pallas_call(
        body, out_shape=(sds,) * 4, grid=(R // tr,), name="adamw_w_ada",
        in_specs=[pl.BlockSpec((tr, 128), lambda i: (i, 0)), pl.BlockSpec((128, C), lambda i: (0, 0))] + [spec] * 3,
        out_specs=(spec,) * 4, compiler_params=_cp("parallel"),
    )(c_t, dmod_cols, w, m, v)


def _adamw_small(small_all, ws, ms, vs):
    def body(s_ref, w_ref, m_ref, v_ref, g_ref, d_ref, nm_ref, nv_ref):
        g = s_ref[0]
        for b in range(1, N_DEV):
            g = g + s_ref[b]
        d, nm, nv = _adamw(w_ref[...], g, m_ref[...], v_ref[...])
        g_ref[...] = g
        d_ref[...] = d
        nm_ref[...] = nm
        nv_ref[...] = nv

    sds = jax.ShapeDtypeStruct((SMALL_ROWS, D_MODEL), F32)
    return pl.pallas_call(
        body, out_shape=(sds,) * 4, name="adamw_small", in_specs=[VMEM_SPEC] * 4, out_specs=(VMEM_SPEC,) * 4,
        compiler_params=pltpu.CompilerParams(vmem_limit_bytes=VMEM_LIMIT_V7X),
    )(small_all, ws, ms, vs)


ROW_MOD, ROW_NORM_G, ROW_CONV_B, ROW_BA, ROW_BX, ROW_LAM, ROW_FINAL_G, ROW_SINKS, ROW_CONV_W = 0, 3, 4, 5, 6, 7, 8, 9, 10


def _pack_small(b_ada, norm_g, conv_b, ba, bx, lam, final_g, sinks, conv_w_full):
    rows = [b_ada.reshape(3, D_MODEL), norm_g, conv_b, ba, bx, lam, final_g.reshape(1, D_MODEL),
            jnp.pad(sinks.reshape(1, -1), ((0, 0), (0, D_MODEL - sinks.size))), conv_w_full,
            jnp.zeros((SMALL_ROWS - 14, D_MODEL), F32)]
    return jnp.concatenate([r.astype(F32) for r in rows], axis=0)


def kernel(x, c, positions, w_ada, b_ada, norm_g, w_in, attn_sinks, conv_w, conv_b, rg_wa, rg_ba, rg_wx, rg_bx, rg_lambda, w_attn_proj, w_rnn_proj, w_out, final_g, loss_target, m_w_ada, m_b_ada, m_norm_g, m_w_in, m_attn_sinks, m_conv_w, m_conv_b, m_rg_wa, m_rg_ba, m_rg_wx, m_rg_bx, m_rg_lambda, m_w_attn_proj, m_w_rnn_proj, m_w_out, m_final_g, v_w_ada, v_b_ada, v_norm_g, v_w_in, v_attn_sinks, v_conv_w, v_conv_b, v_rg_wa, v_rg_ba, v_rg_wx, v_rg_bx, v_rg_lambda, v_w_attn_proj, v_w_rnn_proj, v_w_out, v_final_g):
    T = x.shape[1]
    my_chip = lax.axis_index("x") * 2 + lax.axis_index("y")
    my_dev = my_chip * 2 + lax.axis_index("c")
    x2d, tgt = x[0], loss_target[0]
    pos_col = positions.reshape(T, 1)

    chip_idx = my_chip.reshape(1).astype(jnp.int32)
    c_idx = lax.axis_index("c").reshape(1).astype(jnp.int32)
    sq_place = ((D_MODEL, D_MODEL), (SHARD_ROWS, D_MODEL), lambda chip: (chip, 0))
    rg_place = ((RNN_BLOCKS, RNN_BW, RNN_BW), (RNN_BLOCKS, SHARD_RG, RNN_BW), lambda chip: (0, chip, 0))
    gathered = _gather_weights(
        c.reshape(1, 1, D_MODEL), w_ada[0],
        _cast_place(w_in[0], chip_idx, (D_MODEL, IN_W), (D_MODEL, SHARD_IN), lambda chip: (0, chip), "w_in"),
        _cast_place(w_attn_proj[0], chip_idx, *sq_place, "w_attn_proj"),
        _cast_place(w_rnn_proj[0], chip_idx, *sq_place, "w_rnn_proj"),
        _cast_place(w_out[0], chip_idx, *sq_place, "w_out"),
        _cast_place(rg_wa[0], chip_idx, *rg_place, "rg_wa"),
        _cast_place(rg_wx[0], chip_idx, *rg_place, "rg_wx"),
        conv_w[0])
    w_in_f = gathered[0].reshape(D_MODEL, IN_W)
    wap_f, wrp_f, wo_f = (g.reshape(D_MODEL, D_MODEL) for g in gathered[1:4])
    rwa_f, rwx_f = (g.reshape(RNN_BLOCKS, RNN_BW, RNN_BW) for g in gathered[4:6])
    cw_chips, c_all, mod_chips = gathered[6:]
    conv_w_f = jnp.transpose(cw_chips, (1, 0, 2)).reshape(CONV_W, D_MODEL)
    mod_all = jnp.transpose(mod_chips, (1, 0, 2)).reshape(N_DEV, ADA_W) + b_ada
    mod_row = lax.dynamic_slice_in_dim(mod_all, my_dev, 1, axis=0)

    tabs = _rope_tables(pos_col)
    h = _prenorm(x2d, mod_row, norm_g)
    proj = _in_projection(h, w_in_f)
    y_attn = _attn_forward(proj, tabs, attn_sinks)
    y_rnn, h_rnn = _rnn_forward(proj, pos_col, conv_w_f, conv_b, rwa_f, rwx_f, rg_ba, rg_bx, rg_lambda)
    (dx2, merged, d_o, d_pa, d_pr, d_ya, d_yr, d_c, d_final_g, d_gate, loss_vec) = _merge_and_head(
        x2d, tgt, y_attn, y_rnn, proj, wap_f, wrp_f, wo_f, mod_row, final_g.reshape(1, D_MODEL))

    sq = (N_CHIPS, 2, SHARD_ROWS // 2, D_MODEL)
    rg = (RNN_BLOCKS, N_CHIPS, 2, SHARD_RG // 2, RNN_BW)
    rg_flat = (RNN_BLOCKS * N_CHIPS, 2, SHARD_RG // 2, RNN_BW)

    def chip_sum_and_start(views, axes, flat, unflat, tags_, kinds_, group):
        from_sib = _swap_halves(views, axes)
        sums = [_presum(v.reshape(f), s.reshape(f[:1] + f[2:]), c_idx, t).reshape(u)
                for v, s, f, u, t in zip(views, from_sib, flat, unflat, tags_)]
        return _exchange_start(sums, kinds_, group)

    g_ap = _weight_grad(y_attn, d_pa, "w_attn_proj")
    g_rp = _weight_grad(y_rnn, d_pr, "w_rnn_proj")
    g_o = _weight_grad(merged, d_o, "w_out")
    sq_half = (N_CHIPS, SHARD_ROWS // 2, D_MODEL)
    started1 = chip_sum_and_start([g_ap.reshape(sq), g_rp.reshape(sq), g_o.reshape(sq)], [1, 1, 1], [sq] * 3, [sq_half] * 3,
                                  ["w_attn_proj", "w_rnn_proj", "w_out"], ["sq"] * 3, "proj")
    d_q, d_kv, d_ga, d_sinks = _attn_backward(proj, d_ya, tabs, attn_sinks + started1[4][0, 0])
    d_b, d_conv_w, d_conv_b, d_rwa, d_rwx, d_ba, d_bx, d_lam = _rnn_backward(
        proj, pos_col, h_rnn, d_yr, conv_w_f, conv_b, rwa_f, rwx_f, rg_ba, rg_bx, rg_lambda)
    pieces = [(d_q, CB_Q, 2), (d_kv, CB_KV, 1), (d_ga, CB_GA, 2), (d_b, CB_XR, 4), (d_c, CB_MA, 4)]
    g_in = None
    for arr, start, _ in pieces:
        g_in = _weight_grad(h, arr, f"w_in_{start}", into=g_in, col_block=start, total_cols=IN_W)
    started2 = chip_sum_and_start(
        [g_in.reshape(2, D_MODEL // 2, IN_W), d_rwa.reshape(rg), d_rwx.reshape(rg)], [0, 2, 2],
        [(1, 2, D_MODEL // 2, IN_W), rg_flat, rg_flat],
        [(D_MODEL // 2, IN_W), (RNN_BLOCKS, N_CHIPS, SHARD_RG // 2, RNN_BW), (RNN_BLOCKS, N_CHIPS, SHARD_RG // 2, RNN_BW)],
        ["w_in", "rg_wa", "rg_wx"], ["in", "rg", "rg"], "in")
    grad_x, d_shift, d_scale, d_norm_g = _input_backward(pieces, w_in_f, x2d, dx2, mod_row + started2[4][0, 0], norm_g)

    d_mod = jnp.concatenate([d_shift, d_scale, d_gate], axis=1)
    small = _pack_small(d_mod, d_norm_g, d_conv_b, d_ba, d_bx, d_lam, d_final_g, d_sinks[:, :N_HEADS], d_conv_w)
    small_all = _gather_small(small)
    sums1, lands1 = _exchange_wait(*started1[:4], grad_x, "proj")
    sums2, lands2 = _exchange_wait(*started2[:4], grad_x, "in")
    tags = ["w_in", "w_attn_proj", "w_rnn_proj", "w_out", "rg_wa", "rg_wx"]
    chip_sums = [sums2[0]] + list(sums1) + list(sums2[1:])
    lands = [lands2[0]] + list(lands1) + list(lands2[1:])
    where = jnp.concatenate([chip_idx, c_idx])
    kinds = ["in", "sq", "sq", "sq", "rg", "rg"]
    halves = [_sum_landed(kinds[i], chip_sums[i], lands[i], where, tags[i]) for i in range(6)]
    grads = _assemble_with_sibling(halves, [0, 0, 0, 0, 1, 1])
    shapes2d = [(D_MODEL, SHARD_IN), (SHARD_ROWS, D_MODEL), (SHARD_ROWS, D_MODEL), (SHARD_ROWS, D_MODEL),
                (RNN_BLOCKS * SHARD_RG, RNN_BW), (RNN_BLOCKS * SHARD_RG, RNN_BW)]
    big_w = [w_in, w_attn_proj, w_rnn_proj, w_out, rg_wa, rg_wx]
    big_m = [m_w_in, m_w_attn_proj, m_w_rnn_proj, m_w_out, m_rg_wa, m_rg_wx]
    big_v = [v_w_in, v_w_attn_proj, v_w_rnn_proj, v_w_out, v_rg_wa, v_rg_wx]
    res = {}
    for i, tag in enumerate(tags):
        g = grads[i].reshape(shapes2d[i])
        outs = _adamw_shard(g, big_w[i].reshape(shapes2d[i]), big_m[i].reshape(shapes2d[i]), big_v[i].reshape(shapes2d[i]), tag)
        res[tag] = [o.reshape(big_w[i].shape) for o in (g,) + tuple(outs)]

    dmod_all = small_all[:, ROW_MOD:ROW_MOD + 3, :].reshape(N_DEV, ADA_W)
    dmod_cols = lax.dynamic_slice_in_dim(dmod_all, my_chip * SHARD_ADA, SHARD_ADA, axis=1)
    c_t = jnp.pad(jnp.transpose(c_all.reshape(N_DEV, D_MODEL)), ((0, 0), (0, 128 - N_DEV)))
    dmod_cols = jnp.pad(dmod_cols, ((0, 128 - N_DEV), (0, 0)))
    res["w_ada"] = [o.reshape(w_ada.shape) for o in _adamw_w_ada(c_t, dmod_cols, w_ada[0], m_w_ada[0], v_w_ada[0])]

    def full_conv(a):
        return lax.dynamic_update_slice_in_dim(jnp.zeros((CONV_W, D_MODEL), F32), a[0], my_chip * (D_MODEL // N_CHIPS), axis=1)

    packed = [_pack_small(p[0], p[1], p[2], p[3], p[4], p[5], p[6], p[7], full_conv(p[8])) for p in (
        (b_ada, norm_g, conv_b, rg_ba, rg_bx, rg_lambda, final_g, attn_sinks, conv_w),
        (m_b_ada, m_norm_g, m_conv_b, m_rg_ba, m_rg_bx, m_rg_lambda, m_final_g, m_attn_sinks, m_conv_w),
        (v_b_ada, v_norm_g, v_conv_b, v_rg_ba, v_rg_bx, v_rg_lambda, v_final_g, v_attn_sinks, v_conv_w))]
    small_out = _adamw_small(small_all, *packed)

    def unpack(slab):
        cw = lax.dynamic_slice_in_dim(slab[ROW_CONV_W:ROW_CONV_W + CONV_W], my_chip * (D_MODEL // N_CHIPS),
                                      D_MODEL // N_CHIPS, axis=1)
        return {
            "b_ada": slab[ROW_MOD:ROW_MOD + 3].reshape(1, ADA_W), "norm_g": slab[ROW_NORM_G:ROW_NORM_G + 1],
            "conv_b": slab[ROW_CONV_B:ROW_CONV_B + 1], "rg_ba": slab[ROW_BA:ROW_BA + 1], "rg_bx": slab[ROW_BX:ROW_BX + 1],
            "rg_lambda": slab[ROW_LAM:ROW_LAM + 1], "final_g": slab[ROW_FINAL_G], "attn_sinks": slab[ROW_SINKS:ROW_SINKS + 1, :N_HEADS],
            "conv_w": cw[None],
        }

    small_res = [unpack(s) for s in small_out]
    order = ["w_ada", "b_ada", "norm_g", "w_in", "attn_sinks", "conv_w", "conv_b", "rg_wa", "rg_ba", "rg_wx", "rg_bx",
             "rg_lambda", "w_attn_proj", "w_rnn_proj", "w_out", "final_g"]
    loss = lax.psum(loss_vec[0, 0], ("x", "y", "c"))
    outs = [loss, grad_x[None]]
    for kind in range(4):
        for name in order:
            outs.append(res[name][kind] if name in res else small_res[kind][name])
    return tuple(outs)
```

```python
import numpy as np
import jax
import jax.numpy as jnp
from jax import lax
from jax.experimental import pallas as pl
from jax.experimental.pallas import tpu as pltpu

F32 = jnp.float32
BF16 = jnp.bfloat16

D_MODEL = 1024
N_HEADS = 16
N_KV = 4
HEAD_DIM = 64
GROUP = N_HEADS // N_KV
BLOCK = 128
KV_W = N_KV * HEAD_DIM
ROT_HALF = 8
ROPE_THETA = 500000.0
RNN_BLOCKS = 4
RNN_BW = 256
CONV_W = 4
LRU_C = 8.0
NORM_EPS = 1e-6
IN_W = 6656
CB = 512
N_CB = IN_W // CB
CB_Q, CB_KV, CB_GA, CB_XR, CB_GR, CB_MA, CB_MR = 0, 2, 3, 5, 7, 9, 11
N_CHIPS = 4
N_DEV = 8
SHARD_IN = IN_W // N_CHIPS
SHARD_ROWS = D_MODEL // N_CHIPS
SHARD_RG = RNN_BW // N_CHIPS
ADA_W = 3 * D_MODEL
SHARD_ADA = ADA_W // N_CHIPS
SMALL_ROWS = 16

ADAM_LR = 0.001
ADAM_B1 = 0.9
ADAM_B2 = 0.999
ADAM_EPS = 1e-08
ADAM_WD = 0.01
ADAM_STEP = 10

VMEM_LIMIT_V7X = 52 * 1024 * 1024
MESH = pl.DeviceIdType.MESH
ANY = pl.BlockSpec(memory_space=pl.ANY)
VMEM_SPEC = pl.BlockSpec(memory_space=pltpu.VMEM)


def _cp(*sem):
    return pltpu.CompilerParams(dimension_semantics=sem if sem else None, vmem_limit_bytes=VMEM_LIMIT_V7X)


def _dot(a, b):
    return jnp.dot(a, b, preferred_element_type=F32)


def _dot_nt(a, b):
    return lax.dot_general(a, b, (((1,), (1,)), ((), ())), preferred_element_type=F32)


def _dot_tn(a, b):
    return lax.dot_general(a, b, (((0,), (0,)), ((), ())), preferred_element_type=F32)


def _sigmoid(z):
    return 1.0 / (1.0 + jnp.exp(-z))


def _neg_expm1(z):
    series = -(z * (1.0 + z * (0.5 + z * (1.0 / 6.0 + z * (1.0 / 24.0 + z * (1.0 / 120.0))))))
    return jnp.where(z > -0.05, series, 1.0 - jnp.exp(z))


def _softplus(z):
    u = jnp.exp(-jnp.abs(z))
    log1p_u = jnp.where(u < 1e-3, u * (1.0 - u * (0.5 - u * (1.0 / 3.0))), jnp.log(1.0 + u))
    return jnp.maximum(z, 0.0) + log1p_u


def _rms(xf):
    return lax.rsqrt(jnp.mean(xf * xf, axis=-1, keepdims=True) + NORM_EPS)


def _me():
    return lax.axis_index("x"), lax.axis_index("y"), lax.axis_index("c")


def _peer(mask):
    x, y, c = _me()
    fx, fy, fc = (mask >> 2) & 1, (mask >> 1) & 1, mask & 1
    return (x ^ fx if fx else x, y ^ fy if fy else y, c ^ fc if fc else c)


def _chip_of(pos):
    return pos[0] * 2 + pos[1]


CHIP_MASKS = (4, 2, 6)
ALL_MASKS = (1, 2, 3, 4, 5, 6, 7)


def _gather_weights(c_row, w_ada_s, b_w_in, b_wap, b_wrp, b_wo, b_rwa, b_rwx, conv_w_s):
    def body(c_ref, wada_ref, win_s, wap_s, wrp_s, wo_s, rwa_s, rwx_s, cw_s,
             win_f, wap_f, wrp_f, wo_f, rwa_f, rwx_f, cw_f, call_ref, mod_ref,
             wsend, wrecv, lsem, csend, crecv, msend, mrecv, fsend, frecv):
        me = _me()
        my_chip = _chip_of(me)
        my_dev = my_chip * 2 + me[2]
        fulls = (win_f, wap_f, wrp_f, wo_f, rwa_f, rwx_f, cw_f)

        def slot(idx, chip, half=None):
            full = fulls[idx]
            if idx == 0:
                cols = pl.ds(pl.multiple_of(chip * SHARD_IN, 128), SHARD_IN)
                return full.at[:, :, cols] if half is None else full.at[half, :, cols]
            if idx in (1, 2, 3):
                return full.at[chip] if half is None else full.at[chip, half]
            if idx in (4, 5):
                return full.at[:, chip] if half is None else full.at[:, chip, half]
            return full.at[chip]

        def my_half(idx):
            return cw_s if idx == 6 else slot(idx, my_chip, me[2])

        def wcopy(idx, k, to):
            return pltpu.make_async_remote_copy(
                src_ref=my_half(idx), dst_ref=slot(idx, my_chip, None if idx == 6 else me[2]),
                send_sem=wsend.at[idx, k], recv_sem=wrecv.at[idx, k], device_id=to, device_id_type=MESH)

        def wrecv_wait(idx, k, frm):
            pltpu.make_async_remote_copy(
                src_ref=my_half(idx), dst_ref=slot(idx, _chip_of(frm), None if idx == 6 else me[2]),
                send_sem=wsend.at[idx, k], recv_sem=wrecv.at[idx, k], device_id=frm, device_id_type=MESH).wait_recv()

        def forward(idx, k, chip, half, to):
            return pltpu.make_async_remote_copy(
                src_ref=slot(idx, chip, half), dst_ref=slot(idx, chip, half),
                send_sem=fsend.at[idx, k], recv_sem=frecv.at[idx, k], device_id=to, device_id_type=MESH)

        sends = []
        for idx in range(7):
            for k, mask in enumerate(CHIP_MASKS):
                cp = wcopy(idx, k, _peer(mask))
                cp.start()
                sends.append(cp)
        local = [pltpu.make_async_copy(cw_s, slot(6, my_chip), lsem.at[0])]
        for cp in local:
            cp.start()

        call_ref[my_dev] = c_ref[0]
        csends = []
        for k, mask in enumerate(ALL_MASKS):
            cp = pltpu.make_async_remote_copy(
                src_ref=c_ref.at[0], dst_ref=call_ref.at[my_dev],
                send_sem=csend.at[k], recv_sem=crecv.at[k], device_id=_peer(mask), device_id_type=MESH)
            cp.start()
            csends.append(cp)
        for k, mask in enumerate(ALL_MASKS):
            frm = _peer(mask)
            pltpu.make_async_remote_copy(
                src_ref=c_ref.at[0], dst_ref=call_ref.at[_chip_of(frm) * 2 + frm[2]],
                send_sem=csend.at[k], recv_sem=crecv.at[k], device_id=frm, device_id_type=MESH).wait_recv()
        for cp in csends:
            cp.wait_send()

        c_all = call_ref[...].reshape(N_DEV, D_MODEL).astype(BF16)
        mod_ref[my_chip] = _dot(c_all, wada_ref[...].astype(BF16))
        msends = []
        for k, mask in enumerate(CHIP_MASKS):
            cp = pltpu.make_async_remote_copy(
                src_ref=mod_ref.at[my_chip], dst_ref=mod_ref.at[my_chip],
                send_sem=msend.at[k], recv_sem=mrecv.at[k], device_id=_peer(mask), device_id_type=MESH)
            cp.start()
            msends.append(cp)
        for k, mask in enumerate(CHIP_MASKS):
            frm = _peer(mask)
            pltpu.make_async_remote_copy(
                src_ref=mod_ref.at[my_chip], dst_ref=mod_ref.at[_chip_of(frm)],
                send_sem=msend.at[k], recv_sem=mrecv.at[k], device_id=frm, device_id_type=MESH).wait_recv()
        for cp in msends:
            cp.wait_send()

        sib = _peer(1)
        forwards = []
        for idx in range(7):
            for k, mask in enumerate(CHIP_MASKS):
                frm = _peer(mask)
                wrecv_wait(idx, k, frm)
                if idx < 6:
                    cp = forward(idx, k, _chip_of(frm), me[2], sib)
                    cp.start()
                    forwards.append(cp)
        for idx in range(6):
            for k, mask in enumerate(CHIP_MASKS):
                forward(idx, k, _chip_of(_peer(mask)), 1 - me[2], sib).wait_recv()
        for cp in sends + forwards:
            cp.wait_send()
        for cp in local:
            cp.wait()

    out_shape = (
        jax.ShapeDtypeStruct((2, D_MODEL // 2, IN_W), BF16),
        jax.ShapeDtypeStruct((N_CHIPS, 2, SHARD_ROWS // 2, D_MODEL), BF16),
        jax.ShapeDtypeStruct((N_CHIPS, 2, SHARD_ROWS // 2, D_MODEL), BF16),
        jax.ShapeDtypeStruct((N_CHIPS, 2, SHARD_ROWS // 2, D_MODEL), BF16),
        jax.ShapeDtypeStruct((RNN_BLOCKS, N_CHIPS, 2, SHARD_RG // 2, RNN_BW), BF16),
        jax.ShapeDtypeStruct((RNN_BLOCKS, N_CHIPS, 2, SHARD_RG // 2, RNN_BW), BF16),
        jax.ShapeDtypeStruct((N_CHIPS, CONV_W, D_MODEL // N_CHIPS), F32),
        jax.ShapeDtypeStruct((N_DEV, 1, D_MODEL), F32),
        jax.ShapeDtypeStruct((N_CHIPS, N_DEV, SHARD_ADA), F32),
    )
    return pl.pallas_call(
        body, out_shape=out_shape, name="gather_weights",
        in_specs=[VMEM_SPEC, VMEM_SPEC] + [ANY] * 7,
        out_specs=tuple([ANY] * 7 + [VMEM_SPEC, VMEM_SPEC]),
        scratch_shapes=[
            pltpu.SemaphoreType.DMA((7, 3)), pltpu.SemaphoreType.DMA((7, 3)), pltpu.SemaphoreType.DMA((7,)),
            pltpu.SemaphoreType.DMA((7,)), pltpu.SemaphoreType.DMA((7,)),
            pltpu.SemaphoreType.DMA((3,)), pltpu.SemaphoreType.DMA((3,)),
            pltpu.SemaphoreType.DMA((6, 3)), pltpu.SemaphoreType.DMA((6, 3)),
        ],
        input_output_aliases={2: 0, 3: 1, 4: 2, 5: 3, 6: 4, 7: 5},
        compiler_params=pltpu.CompilerParams(vmem_limit_bytes=VMEM_LIMIT_V7X),
    )(c_row, w_ada_s, b_w_in.reshape(out_shape[0].shape), b_wap.reshape(out_shape[1].shape),
      b_wrp.reshape(out_shape[2].shape), b_wo.reshape(out_shape[3].shape), b_rwa.reshape(out_shape[4].shape),
      b_rwx.reshape(out_shape[5].shape), conv_w_s)


def _cast_place(shard, chip_idx, full_shape, block, index_map, tag):
    def body(chip_ref, s_ref, o_ref):
        o_ref[...] = s_ref[...].astype(BF16)

    grid_spec = pltpu.PrefetchScalarGridSpec(
        num_scalar_prefetch=1, grid=(1,),
        in_specs=[pl.BlockSpec(shard.shape, lambda i, chip_ref: (0,) * shard.ndim)],
        out_specs=pl.BlockSpec(block, lambda i, chip_ref: index_map(chip_ref[0])))
    return pl.pallas_call(
        body, out_shape=jax.ShapeDtypeStruct(full_shape, BF16), grid_spec=grid_spec, name=f"cast_place_{tag}",
        compiler_params=_cp("arbitrary"),
    )(chip_idx, shard)


HBM_SPEC = pl.BlockSpec(memory_space=pltpu.HBM)
SEM_SPEC = pl.BlockSpec(memory_space=pltpu.SEMAPHORE)


def _shard_of(ref, kind, chip):
    if kind == "in":
        return ref.at[:, pl.ds(pl.multiple_of(chip * SHARD_IN, 128), SHARD_IN)]
    return ref.at[chip] if kind == "sq" else ref.at[:, chip]


def _land_shape(src, kind):
    if kind == "in":
        return (3, src.shape[0], SHARD_IN)
    return (3,) + src.shape[1:] if kind == "sq" else (3, src.shape[0]) + src.shape[2:]


def _exchange_start(srcs, kinds, tag):
    n = len(srcs)
    lands = [pltpu.with_memory_space_constraint(lax.empty(_land_shape(s, k), F32), pltpu.HBM) for s, k in zip(srcs, kinds)]

    def body(*refs):
        src_refs, land_refs = refs[:n], refs[n:2 * n]
        ssems, rsems = refs[2 * n:3 * n], refs[3 * n:4 * n]
        token = refs[6 * n]
        for i in range(n):
            for k, mask in enumerate(CHIP_MASKS):
                to = _peer(mask)
                pltpu.make_async_remote_copy(
                    src_ref=_shard_of(src_refs[i], kinds[i], _chip_of(to)), dst_ref=land_refs[i].at[k],
                    send_sem=ssems[i], recv_sem=rsems[i], device_id=to, device_id_type=MESH).start()
        token[...] = jnp.zeros_like(token)

    sem = pltpu.SemaphoreType.DMA(())
    out_shape = ((sem,) * (2 * n) + tuple(pltpu.HBM(s.shape, s.dtype) for s in srcs)
                 + tuple(pltpu.HBM(l.shape, l.dtype) for l in lands) + (jax.ShapeDtypeStruct((8, 128), F32),))
    outs = pl.pallas_call(
        body, out_shape=out_shape, name=f"exchange_start_{tag}",
        in_specs=[HBM_SPEC] * (2 * n), out_specs=tuple([SEM_SPEC] * (2 * n) + [HBM_SPEC] * (2 * n) + [VMEM_SPEC]),
        input_output_aliases={i: 2 * n + i for i in range(2 * n)},
        compiler_params=pltpu.CompilerParams(has_side_effects=pltpu.SideEffectType.DATAFLOW_SIDE_EFFECTING),
    )(*[pltpu.with_memory_space_constraint(s, pltpu.HBM) for s in srcs], *lands)
    return outs[:n], outs[n:2 * n], outs[2 * n:3 * n], outs[3 * n:4 * n], outs[4 * n]


def _exchange_wait(ssems, rsems, srcs, lands, after, tag):
    n = len(srcs)

    def body(*refs):
        land_refs = refs[n:2 * n]
        ssem_refs, rsem_refs = refs[2 * n:3 * n], refs[3 * n:4 * n]
        for i in range(n):
            all_three = pltpu.make_async_remote_copy(
                src_ref=land_refs[i], dst_ref=land_refs[i], send_sem=ssem_refs[i], recv_sem=rsem_refs[i],
                device_id=_me(), device_id_type=MESH)
            all_three.wait_send()
            all_three.wait_recv()

    outs = pl.pallas_call(
        body, out_shape=tuple(pltpu.HBM(a.shape, a.dtype) for a in list(srcs) + list(lands)), name=f"exchange_wait_{tag}",
        in_specs=[HBM_SPEC] * (2 * n) + [SEM_SPEC] * (2 * n) + [ANY], out_specs=tuple([HBM_SPEC] * (2 * n)),
        input_output_aliases={i: i for i in range(2 * n)},
        compiler_params=pltpu.CompilerParams(has_side_effects=pltpu.SideEffectType.DATAFLOW_SIDE_EFFECTING),
    )(*srcs, *lands, *ssems, *rsems, after)
    return outs[:n], outs[n:]


def _gather_small(small):
    def body(small_ref, small_all, ssend, srecv):
        me = _me()
        my_dev = _chip_of(me) * 2 + me[2]
        small_all[my_dev] = small_ref[...]
        ssends = []
        for k, mask in enumerate(ALL_MASKS):
            cp = pltpu.make_async_remote_copy(
                src_ref=small_ref, dst_ref=small_all.at[my_dev],
                send_sem=ssend.at[k], recv_sem=srecv.at[k], device_id=_peer(mask), device_id_type=MESH)
            cp.start()
            ssends.append(cp)
        for k, mask in enumerate(ALL_MASKS):
            frm = _peer(mask)
            pltpu.make_async_remote_copy(
                src_ref=small_ref, dst_ref=small_all.at[_chip_of(frm) * 2 + frm[2]],
                send_sem=ssend.at[k], recv_sem=srecv.at[k], device_id=frm, device_id_type=MESH).wait_recv()
        for cp in ssends:
            cp.wait_send()

    return pl.pallas_call(
        body, out_shape=jax.ShapeDtypeStruct((N_DEV, SMALL_ROWS, D_MODEL), F32), name="gather_small",
        in_specs=[VMEM_SPEC], out_specs=VMEM_SPEC,
        scratch_shapes=[pltpu.SemaphoreType.DMA((7,)), pltpu.SemaphoreType.DMA((7,))],
    )(small)


def _half_of(ref, axis, half):
    return ref.at[(slice(None),) * axis + (half,)]


def _swap_halves(parts, axes):
    n = len(parts)

    def body(*refs):
        ins, outs, ssem, rsem = refs[:n], refs[n:2 * n], refs[2 * n], refs[2 * n + 1]
        c = lax.axis_index("c")
        cps = [pltpu.make_async_remote_copy(src_ref=_half_of(ins[i], axes[i], 1 - c), dst_ref=outs[i], send_sem=ssem.at[i],
                                            recv_sem=rsem.at[i], device_id=_peer(1), device_id_type=MESH) for i in range(n)]
        for cp in cps:
            cp.start()
        for cp in cps:
            cp.wait()

    shapes = [p.shape[:a] + p.shape[a + 1:] for p, a in zip(parts, axes)]
    return pl.pallas_call(
        body, out_shape=tuple(jax.ShapeDtypeStruct(s, p.dtype) for s, p in zip(shapes, parts)), name="swap_halves",
        in_specs=[ANY] * n, out_specs=tuple([ANY] * n),
        scratch_shapes=[pltpu.SemaphoreType.DMA((n,)), pltpu.SemaphoreType.DMA((n,))],
    )(*parts)


def _presum(mine, sib, c_idx, tag):
    S, _, R, C = mine.shape
    tr = min(R, 256)
    tc = SHARD_IN if C % SHARD_IN == 0 else C

    def body(c_ref, m_ref, s_ref, o_ref):
        o_ref[...] = m_ref[:, 0] + s_ref[...]

    grid_spec = pltpu.PrefetchScalarGridSpec(
        num_scalar_prefetch=1, grid=(R // tr, C // tc),
        in_specs=[pl.BlockSpec((S, 1, tr, tc), lambda i, j, c_ref: (0, c_ref[0], i, j)),
                  pl.BlockSpec((S, tr, tc), lambda i, j, c_ref: (0, i, j))],
        out_specs=pl.BlockSpec((S, tr, tc), lambda i, j, c_ref: (0, i, j)))
    return pl.pallas_call(
        body, out_shape=jax.ShapeDtypeStruct((S, R, C), F32), grid_spec=grid_spec, name=f"presum_{tag}",
        compiler_params=_cp("parallel", "parallel"),
    )(c_idx, mine, sib)


def _assemble_with_sibling(parts, axes):
    n = len(parts)

    def body(*refs):
        outs, ssem, rsem = refs[n:2 * n], refs[2 * n], refs[2 * n + 1]
        c = lax.axis_index("c")
        cps = [pltpu.make_async_remote_copy(
            src_ref=_half_of(outs[i], axes[i], c), dst_ref=_half_of(outs[i], axes[i], c), send_sem=ssem.at[i],
            recv_sem=rsem.at[i], device_id=_peer(1), device_id_type=MESH) for i in range(n)]
        for cp in cps:
            cp.start()
        for i in range(n):
            pltpu.make_async_remote_copy(
                src_ref=_half_of(outs[i], axes[i], c), dst_ref=_half_of(outs[i], axes[i], 1 - c), send_sem=ssem.at[i],
                recv_sem=rsem.at[i], device_id=_peer(1), device_id_type=MESH).wait_recv()
        for cp in cps:
            cp.wait_send()

    return pl.pallas_call(
        body, out_shape=tuple(jax.ShapeDtypeStruct(p.shape, p.dtype) for p in parts), name="assemble_with_sibling",
        in_specs=[ANY] * n, out_specs=tuple([ANY] * n), input_output_aliases={i: i for i in range(n)},
        scratch_shapes=[pltpu.SemaphoreType.DMA((n,)), pltpu.SemaphoreType.DMA((n,))],
    )(*parts)


def _rope_tables(pos_col):
    T = pos_col.shape[0]
    tm = min(T, 512)
    inv = np.float32(ROPE_THETA) ** (-(np.arange(0, 2 * ROT_HALF, 2, dtype=np.float32)) / np.float32(2 * ROT_HALF))
    lane = np.arange(128) % HEAD_DIM
    freq = np.where(lane < 2 * ROT_HALF, inv[lane % ROT_HALF], 0.0).astype(np.float32)[None, :]

    def body(pos_ref, f_ref, c_ref, sa_ref, sb_ref):
        ang = pos_ref[...].astype(F32) * f_ref[...]
        c, s = jnp.cos(ang), jnp.sin(ang)
        m = lax.broadcasted_iota(jnp.int32, ang.shape, 1) & (HEAD_DIM - 1)
        c_ref[...] = jnp.where(m < 2 * ROT_HALF, c, 1.0)
        sa_ref[...] = jnp.where(m < ROT_HALF, -s, 0.0)
        sb_ref[...] = jnp.where((m >= ROT_HALF) & (m < 2 * ROT_HALF), s, 0.0)

    tab = jax.ShapeDtypeStruct((T, 128), F32)
    return pl.pallas_call(
        body, out_shape=(tab, tab, tab), grid=(T // tm,), name="rope_tables",
        in_specs=[pl.BlockSpec((tm, 1), lambda i: (i, 0)), pl.BlockSpec((1, 128), lambda i: (0, 0))],
        out_specs=tuple(pl.BlockSpec((tm, 128), lambda i: (i, 0)) for _ in range(3)),
        compiler_params=_cp("parallel"),
    )(pos_col, jnp.asarray(freq))


def _wide(tab, width):
    return jnp.concatenate([tab] * (width // 128), axis=1)


def _rope(t, c, sa, sb):
    w = t.shape[-1]
    return t * c + pltpu.roll(t, w - ROT_HALF, 1) * sa + pltpu.roll(t, ROT_HALF, 1) * sb


def _unrope(d, c, sa, sb):
    w = d.shape[-1]
    return d * c + pltpu.roll(d * sa, ROT_HALF, 1) + pltpu.roll(d * sb, w - ROT_HALF, 1)


def _prenorm(x, mod_row, norm_g):
    T = x.shape[0]
    tm = min(T, 512)

    def body(x_ref, mod_ref, g_ref, h_ref):
        xf = x_ref[...]
        shift, scale = mod_ref[:, 0:D_MODEL], mod_ref[:, D_MODEL:2 * D_MODEL]
        h = (xf * _rms(xf)) * g_ref[...] * (1.0 + scale) + shift
        h_ref[...] = h.astype(BF16)

    return pl.pallas_call(
        body, out_shape=jax.ShapeDtypeStruct((T, D_MODEL), BF16), grid=(T // tm,), name="prenorm",
        in_specs=[pl.BlockSpec((tm, D_MODEL), lambda i: (i, 0)), pl.BlockSpec((1, ADA_W), lambda i: (0, 0)),
                  pl.BlockSpec((1, D_MODEL), lambda i: (0, 0))],
        out_specs=pl.BlockSpec((tm, D_MODEL), lambda i: (i, 0)),
        compiler_params=_cp("parallel"),
    )(x, mod_row, norm_g)


def _in_projection(h, w_in):
    T = h.shape[0]
    tm, tn = min(T, 512), SHARD_IN

    def body(h_ref, w_ref, o_ref):
        o_ref[...] = _dot(h_ref[...], w_ref[...])

    return pl.pallas_call(
        body, out_shape=jax.ShapeDtypeStruct((T, IN_W), F32), grid=(IN_W // tn, T // tm), name="in_projection",
        in_specs=[pl.BlockSpec((tm, D_MODEL), lambda j, i: (i, 0)), pl.BlockSpec((D_MODEL, tn), lambda j, i: (0, j))],
        out_specs=pl.BlockSpec((tm, tn), lambda j, i: (i, j)),
        compiler_params=_cp("parallel", "parallel"),
    )(h, w_in)


def _attn_mask(n):
    qi = lax.broadcasted_iota(jnp.int32, (GROUP * BLOCK, 2 * BLOCK), 0) & (BLOCK - 1)
    kj = lax.broadcasted_iota(jnp.int32, (GROUP * BLOCK, 2 * BLOCK), 1)
    diff = qi + BLOCK - kj
    return (diff >= 0) & (diff < BLOCK) & ((kj >= BLOCK) | (n > 0))


def _sink_col(sink_ref, kh):
    rowg = lax.broadcasted_iota(jnp.int32, (GROUP * BLOCK, 1), 0) // BLOCK
    col = jnp.full((GROUP * BLOCK, 1), sink_ref[0, GROUP * kh], F32)
    for g in range(1, GROUP):
        col = jnp.where(rowg == g, sink_ref[0, GROUP * kh + g], col)
    return col


def _attn_probs(qr, kr_prev, kr_cur, v_prev, v_cur, kh, sink_col, mask):
    heads = [qr[:, HEAD_DIM * (GROUP * kh + g): HEAD_DIM * (GROUP * kh + g + 1)] for g in range(GROUP)]
    qs = jnp.concatenate(heads, axis=0).astype(BF16)
    lo, hi = HEAD_DIM * kh, HEAD_DIM * (kh + 1)
    kk = jnp.concatenate([kr_prev[:, lo:hi], kr_cur[:, lo:hi]], axis=0).astype(BF16)
    vv = jnp.concatenate([v_prev[:, lo:hi], v_cur[:, lo:hi]], axis=0).astype(BF16)
    s = _dot_nt(qs, kk) * (1.0 / 8.0)
    s = jnp.where(mask, s, -1e30)
    m = jnp.maximum(jnp.max(s, axis=-1, keepdims=True), sink_col)
    p = jnp.exp(s - m)
    p_sink = jnp.exp(sink_col - m)
    denom = jnp.sum(p, axis=-1, keepdims=True) + p_sink
    return qs, kk, vv, p / denom, p_sink / denom


def _unstack_heads(parts):
    cols = []
    for kh in range(N_KV):
        for g in range(GROUP):
            cols.append(parts[kh][g * BLOCK:(g + 1) * BLOCK, :])
    return jnp.concatenate(cols, axis=1)


def _attn_forward(proj, tabs, sinks):
    T = proj.shape[0]
    nb = T // BLOCK

    def body(q_ref, kvc_ref, kvp_ref, g0_ref, g1_ref, cc, sac, sbc, cp_, sap, sbp, sink_ref, y_ref):
        n = pl.program_id(0)
        tc = (_wide(cc[...], D_MODEL), _wide(sac[...], D_MODEL), _wide(sbc[...], D_MODEL))
        tcur = tuple(t[:, :KV_W] for t in tc)
        tprev = (_wide(cp_[...], KV_W), _wide(sap[...], KV_W), _wide(sbp[...], KV_W))
        qr = _rope(q_ref[...], *tc)
        kr_cur = _rope(kvc_ref[:, 0:KV_W], *tcur)
        kr_prev = _rope(kvp_ref[:, 0:KV_W], *tprev)
        v_cur, v_prev = kvc_ref[:, KV_W:2 * KV_W], kvp_ref[:, KV_W:2 * KV_W]
        mask = _attn_mask(n)
        outs = []
        for kh in range(N_KV):
            _, _, vv, pn, _ = _attn_probs(qr, kr_prev, kr_cur, v_prev, v_cur, kh, _sink_col(sink_ref, kh), mask)
            outs.append(_dot(pn.astype(BF16), vv))
        o = _unstack_heads(outs)
        g = jnp.concatenate([g0_ref[...], g1_ref[...]], axis=1)
        y_ref[...] = (o * (g * _sigmoid(g))).astype(BF16)

    def blk(w, cb):
        return pl.BlockSpec((BLOCK, w), lambda n, cb=cb: (n, cb))

    prev = lambda w, cb: pl.BlockSpec((BLOCK, w), lambda n, cb=cb: (jnp.maximum(n - 1, 0), cb))
    return pl.pallas_call(
        body, out_shape=jax.ShapeDtypeStruct((T, D_MODEL), BF16), grid=(nb,), name="attn_forward",
        in_specs=[blk(D_MODEL, 0), blk(CB, CB_KV), prev(CB, CB_KV), blk(CB, CB_GA), blk(CB, CB_GA + 1),
                  blk(128, 0), blk(128, 0), blk(128, 0), prev(128, 0), prev(128, 0), prev(128, 0),
                  pl.BlockSpec(memory_space=pltpu.SMEM)],
        out_specs=pl.BlockSpec((BLOCK, D_MODEL), lambda n: (n, 0)),
        compiler_params=_cp("parallel"),
    )(proj, proj, proj, proj, proj, *tabs, *tabs, sinks)


def _scan_rows8():
    return lax.broadcasted_iota(jnp.int32, (8, D_MODEL), 0)


def _scan_forward(a_ref, b_ref, h_ref, carry, rows):
    row = _scan_rows8()

    def group(i, carry):
        off = pl.multiple_of(i * 8, 8)
        a, b = a_ref[pl.ds(off, 8), :], b_ref[pl.ds(off, 8), :]
        for d in (1, 2, 4):
            ok = row >= d
            b = jnp.where(ok, a * pltpu.roll(b, d, 0) + b, b)
            a = jnp.where(ok, a * pltpu.roll(a, d, 0), a)
        h = a * carry + b
        h_ref[pl.ds(off, 8), :] = h
        return h[7:8, :]

    return lax.fori_loop(0, rows // 8, group, carry)


def _scan_backward(a_ref, g_ref, lam_ref, carry, rows):
    row = _scan_rows8()

    def group(i, carry):
        off = pl.multiple_of((rows // 8 - 1 - i) * 8, 8)
        a, g = a_ref[pl.ds(off, 8), :], g_ref[pl.ds(off, 8), :]
        b = a * g
        for d in (1, 2, 4):
            ok = row < 8 - d
            b = jnp.where(ok, a * pltpu.roll(b, 8 - d, 0) + b, b)
            a = jnp.where(ok, a * pltpu.roll(a, 8 - d, 0), a)
        mu = a * carry + b
        mu_below = jnp.where(row == 7, carry, pltpu.roll(mu, 7, 0))
        lam_ref[pl.ds(off, 8), :] = g + mu_below
        return mu[0:1, :]

    return lax.fori_loop(0, rows // 8, group, carry)


def _rnn_recompute(xbuf, xr, tail, cw, cb, wa_ref, wx_ref, ba, bx, sp, reset):
    rows = xr.shape[0]
    xbuf[0:8, :] = tail
    xbuf[8:rows + 8, :] = xr
    xs = [xbuf[pl.ds(8 - (CONV_W - 1 - k), rows), :] for k in range(CONV_W - 1)] + [xr]
    xc = xs[0] * cw[0:1, :]
    for k in range(1, CONV_W):
        xc = xc + xs[k] * cw[k:k + 1, :]
    xc = xc + cb
    xcb = xc.astype(BF16)
    za = jnp.concatenate([_dot(xcb[:, RNN_BW * j:RNN_BW * (j + 1)], wa_ref[j]) for j in range(RNN_BLOCKS)], axis=1) + ba
    zx = jnp.concatenate([_dot(xcb[:, RNN_BW * j:RNN_BW * (j + 1)], wx_ref[j]) for j in range(RNN_BLOCKS)], axis=1) + bx
    r, i = _sigmoid(za), _sigmoid(zx)
    log_a = -LRU_C * r * sp
    a_raw = jnp.exp(log_a)
    mult_raw = jnp.sqrt(_neg_expm1(2.0 * log_a))
    a = jnp.where(reset, 0.0, a_raw)
    mult = jnp.where(reset, 1.0, mult_raw)
    return xs, xc, xcb, r, i, a_raw, mult_raw, a, mult


def _rnn_forward(proj, pos_col, conv_w, conv_b, rwa, rwx, ba, bx, lam):
    T = proj.shape[0]
    tr = min(T, 256)

    def body(x0, x1, g0, g1, pos_ref, cw_ref, cb_ref, wa_ref, wx_ref, ba_ref, bx_ref, lam_ref,
             y_ref, h_ref, xbuf, abuf, bbuf, tail, carry):
        t = pl.program_id(0)

        @pl.when(t == 0)
        def _():
            tail[...] = jnp.zeros_like(tail)
            carry[...] = jnp.zeros_like(carry)

        xr = jnp.concatenate([x0[...], x1[...]], axis=1)
        sp = _softplus(-lam_ref[...])
        reset = pos_ref[...] == 0
        _, xc, _, _, i, _, _, a, mult = _rnn_recompute(
            xbuf, xr, tail[...], cw_ref[...], cb_ref[...], wa_ref, wx_ref, ba_ref[...], bx_ref[...], sp, reset)
        abuf[...] = a
        bbuf[...] = mult * (i * xc)
        last = _scan_forward(abuf, bbuf, h_ref, carry[0:1, :], tr)
        carry[...] = jnp.broadcast_to(last, carry.shape)
        tail[...] = xr[tr - 8:tr, :]
        g = jnp.concatenate([g0[...], g1[...]], axis=1)
        y_ref[...] = (h_ref[...] * (g * _sigmoid(g))).astype(BF16)

    blk = lambda cb: pl.BlockSpec((tr, CB), lambda t, cb=cb: (t, cb))
    row = lambda w: pl.BlockSpec((1, w), lambda t: (0, 0))
    full3 = pl.BlockSpec((RNN_BLOCKS, RNN_BW, RNN_BW), lambda t: (0, 0, 0))
    return pl.pallas_call(
        body, out_shape=(jax.ShapeDtypeStruct((T, D_MODEL), BF16), jax.ShapeDtypeStruct((T, D_MODEL), F32)),
        grid=(T // tr,), name="rnn_forward",
        in_specs=[blk(CB_XR), blk(CB_XR + 1), blk(CB_GR), blk(CB_GR + 1), pl.BlockSpec((tr, 1), lambda t: (t, 0)),
                  pl.BlockSpec((CONV_W, D_MODEL), lambda t: (0, 0)), row(D_MODEL), full3, full3,
                  row(D_MODEL), row(D_MODEL), row(D_MODEL)],
        out_specs=(pl.BlockSpec((tr, D_MODEL), lambda t: (t, 0)), pl.BlockSpec((tr, D_MODEL), lambda t: (t, 0))),
        scratch_shapes=[pltpu.VMEM((tr + 8, D_MODEL), F32), pltpu.VMEM((tr, D_MODEL), F32), pltpu.VMEM((tr, D_MODEL), F32),
                        pltpu.VMEM((8, D_MODEL), F32), pltpu.VMEM((8, D_MODEL), F32)],
        compiler_params=_cp("arbitrary"),
    )(proj, proj, proj, proj, pos_col, conv_w, conv_b, rwa, rwx, ba, bx, lam)


def _merge_and_head(x, target, y_attn, y_rnn, proj, wap, wrp, wo, mod_row, final_g):
    T = x.shape[0]
    tm = min(T, 256)

    def body(x_ref, t_ref, ya_ref, yr_ref, ma0, ma1, mr0, mr1, wap_ref, wrp_ref, wo_ref, mod_ref, fg_ref,
             dx2_ref, mg_ref, do_ref, dpa_ref, dpr_ref, dya_ref, dyr_ref, dc_ref, dfg_ref, dgate_ref, loss_ref):
        i = pl.program_id(0)
        gate = mod_ref[:, 2 * D_MODEL:3 * D_MODEL]
        ya, yr = ya_ref[...], yr_ref[...]
        pa, pr = _dot(ya, wap_ref[...]), _dot(yr, wrp_ref[...])
        sa = _sigmoid(jnp.concatenate([ma0[...], ma1[...]], axis=1))
        sr = _sigmoid(jnp.concatenate([mr0[...], mr1[...]], axis=1))
        merged = sa * pa + sr * pr
        mb = merged.astype(BF16)
        o = _dot(mb, wo_ref[...])
        x2 = x_ref[...] + gate * o
        r2 = _rms(x2)
        xn2 = x2 * r2
        fg = fg_ref[...]
        err = xn2 * fg - t_ref[...]
        loss_t = 0.5 * jnp.sum(jnp.sum(err * err, axis=-1, keepdims=True) * (1.0 / D_MODEL), axis=0, keepdims=True)
        dy = err * (1.0 / D_MODEL)
        dfg_t = jnp.sum(dy * xn2, axis=0, keepdims=True)
        dxn = dy * fg
        dx2 = r2 * (dxn - xn2 * jnp.mean(dxn * xn2, axis=-1, keepdims=True))
        dgate_t = jnp.sum(dx2 * o, axis=0, keepdims=True)
        dob = (dx2 * gate).astype(BF16)
        dmerged = _dot_nt(dob, wo_ref[...])
        dpa = (dmerged * sa).astype(BF16)
        dpr = (dmerged * sr).astype(BF16)
        dx2_ref[...] = dx2
        mg_ref[...] = mb
        do_ref[...] = dob
        dpa_ref[...] = dpa
        dpr_ref[...] = dpr
        dya_ref[...] = _dot_nt(dpa, wap_ref[...])
        dyr_ref[...] = _dot_nt(dpr, wrp_ref[...])
        dc_ref[:, 0:D_MODEL] = (dmerged * pa * sa * (1.0 - sa)).astype(BF16)
        dc_ref[:, D_MODEL:2 * D_MODEL] = (dmerged * pr * sr * (1.0 - sr)).astype(BF16)

        @pl.when(i == 0)
        def _():
            dfg_ref[...] = jnp.zeros_like(dfg_ref)
            dgate_ref[...] = jnp.zeros_like(dgate_ref)
            loss_ref[...] = jnp.zeros_like(loss_ref)

        dfg_ref[...] += dfg_t
        dgate_ref[...] += dgate_t
        loss_ref[...] += jnp.broadcast_to(loss_t, loss_ref.shape)

    tok = lambda w: pl.BlockSpec((tm, w), lambda i: (i, 0))
    blk = lambda cb: pl.BlockSpec((tm, CB), lambda i, cb=cb: (i, cb))
    wfull = pl.BlockSpec((D_MODEL, D_MODEL), lambda i: (0, 0))
    row = lambda w: pl.BlockSpec((1, w), lambda i: (0, 0))
    out_shape = (
        jax.ShapeDtypeStruct((T, D_MODEL), F32), jax.ShapeDtypeStruct((T, D_MODEL), BF16),
        jax.ShapeDtypeStruct((T, D_MODEL), BF16), jax.ShapeDtypeStruct((T, D_MODEL), BF16),
        jax.ShapeDtypeStruct((T, D_MODEL), BF16), jax.ShapeDtypeStruct((T, D_MODEL), F32),
        jax.ShapeDtypeStruct((T, D_MODEL), F32), jax.ShapeDtypeStruct((T, 2 * D_MODEL), BF16),
        jax.ShapeDtypeStruct((1, D_MODEL), F32), jax.ShapeDtypeStruct((1, D_MODEL), F32),
        jax.ShapeDtypeStruct((1, 128), F32),
    )
    return pl.pallas_call(
        body, out_shape=out_shape, grid=(T // tm,), name="merge_and_head",
        in_specs=[tok(D_MODEL), tok(D_MODEL), tok(D_MODEL), tok(D_MODEL), blk(CB_MA), blk(CB_MA + 1), blk(CB_MR),
                  blk(CB_MR + 1), wfull, wfull, wfull, row(ADA_W), row(D_MODEL)],
        out_specs=(tok(D_MODEL),) * 7 + (tok(2 * D_MODEL), row(D_MODEL), row(D_MODEL), row(128)),
        compiler_params=_cp("arbitrary"),
    )(x, target, y_attn, y_rnn, proj, proj, proj, proj, wap, wrp, wo, mod_row, final_g)


def _attn_backward(proj, d_y, tabs, sinks):
    T = proj.shape[0]
    nb = T // BLOCK

    def body(q_ref, kvc_ref, kvp_ref, g0_ref, g1_ref, dy_ref, cc, sac, sbc, cp_, sap, sbp, sink_ref,
             dq_ref, dkv_ref, dg_ref, dsink_ref, carry):
        n = pl.program_id(0)

        @pl.when(n == 0)
        def _():
            carry[...] = jnp.zeros_like(carry)
            dsink_ref[...] = jnp.zeros_like(dsink_ref)

        @pl.when(n < nb)
        def _():
            tc = (_wide(cc[...], D_MODEL), _wide(sac[...], D_MODEL), _wide(sbc[...], D_MODEL))
            tcur = tuple(t[:, :KV_W] for t in tc)
            tprev = (_wide(cp_[...], KV_W), _wide(sap[...], KV_W), _wide(sbp[...], KV_W))
            qr = _rope(q_ref[...], *tc)
            kr_cur = _rope(kvc_ref[:, 0:KV_W], *tcur)
            kr_prev = _rope(kvp_ref[:, 0:KV_W], *tprev)
            v_cur, v_prev = kvc_ref[:, KV_W:2 * KV_W], kvp_ref[:, KV_W:2 * KV_W]
            g = jnp.concatenate([g0_ref[...], g1_ref[...]], axis=1)
            sg = _sigmoid(g)
            dy = dy_ref[...]
            d_o = dy * (g * sg)
            mask = _attn_mask(n)
            lane = lax.broadcasted_iota(jnp.int32, (1, 128), 1)
            rowg = lax.broadcasted_iota(jnp.int32, (GROUP * BLOCK, 1), 0) // BLOCK
            o_parts, dq_parts, dk_parts, dv_parts = [], [], [], []
            dsink = jnp.zeros((1, 128), F32)
            for kh in range(N_KV):
                qs, kk, vv, pn, pn_sink = _attn_probs(qr, kr_prev, kr_cur, v_prev, v_cur, kh, _sink_col(sink_ref, kh), mask)
                pnb = pn.astype(BF16)
                o_parts.append(_dot(pnb, vv))
                dos = jnp.concatenate(
                    [d_o[:, HEAD_DIM * (GROUP * kh + gq): HEAD_DIM * (GROUP * kh + gq + 1)] for gq in range(GROUP)],
                    axis=0).astype(BF16)
                dpn = _dot_nt(dos, vv)
                delta = jnp.sum(pn * dpn, axis=-1, keepdims=True)
                dsb = (pn * (dpn - delta) * (1.0 / 8.0)).astype(BF16)
                dq_parts.append(_dot(dsb, kk))
                dk_parts.append(_dot_tn(dsb, qs))
                dv_parts.append(_dot_tn(pnb, dos))
                ds_rows = pn_sink * delta
                for gq in range(GROUP):
                    val = -jnp.sum(jnp.where(rowg == gq, ds_rows, 0.0), axis=0, keepdims=True)
                    dsink = dsink + jnp.where(lane == GROUP * kh + gq, val, 0.0)
            o = _unstack_heads(o_parts)
            dg_ref[...] = (dy * o * (sg * (1.0 + g * (1.0 - sg)))).astype(BF16)
            dq_ref[...] = _unrope(_unstack_heads(dq_parts), *tc).astype(BF16)
            dk_prev = _unrope(jnp.concatenate([p[0:BLOCK, :] for p in dk_parts], axis=1), *tprev)
            dk_cur = _unrope(jnp.concatenate([p[BLOCK:2 * BLOCK, :] for p in dk_parts], axis=1), *tcur)
            dv_prev = jnp.concatenate([p[0:BLOCK, :] for p in dv_parts], axis=1)
            dv_cur = jnp.concatenate([p[BLOCK:2 * BLOCK, :] for p in dv_parts], axis=1)
            dkv_ref[...] = (carry[...] + jnp.concatenate([dk_prev, dv_prev], axis=1)).astype(BF16)
            carry[...] = jnp.concatenate([dk_cur, dv_cur], axis=1)
            dsink_ref[...] += dsink

        @pl.when(n == nb)
        def _():
            dkv_ref[...] = carry[...].astype(BF16)

    cur = lambda w, cb: pl.BlockSpec((BLOCK, w), lambda n, cb=cb: (jnp.minimum(n, nb - 1), cb))
    prev = lambda w, cb: pl.BlockSpec((BLOCK, w), lambda n, cb=cb: (jnp.maximum(jnp.minimum(n, nb - 1) - 1, 0), cb))
    out_shape = (jax.ShapeDtypeStruct((T, D_MODEL), BF16), jax.ShapeDtypeStruct((T, 2 * KV_W), BF16),
                 jax.ShapeDtypeStruct((T, D_MODEL), BF16), jax.ShapeDtypeStruct((1, 128), F32))
    return pl.pallas_call(
        body, out_shape=out_shape, grid=(nb + 1,), name="attn_backward",
        in_specs=[cur(D_MODEL, 0), cur(CB, CB_KV), prev(CB, CB_KV), cur(CB, CB_GA), cur(CB, CB_GA + 1), cur(D_MODEL, 0),
                  cur(128, 0), cur(128, 0), cur(128, 0), prev(128, 0), prev(128, 0), prev(128, 0),
                  pl.BlockSpec(memory_space=pltpu.SMEM)],
        out_specs=(cur(D_MODEL, 0), pl.BlockSpec((BLOCK, 2 * KV_W), lambda n: (jnp.maximum(n - 1, 0), 0)),
                   cur(D_MODEL, 0), pl.BlockSpec((1, 128), lambda n: (0, 0))),
        scratch_shapes=[pltpu.VMEM((BLOCK, 2 * KV_W), F32)],
        compiler_params=_cp("arbitrary"),
    )(proj, proj, proj, proj, proj, d_y, *tabs, *tabs, sinks)


def _rnn_backward(proj, pos_col, h_rnn, d_y, conv_w, conv_b, rwa, rwx, ba, bx, lam):
    T = proj.shape[0]
    tr = min(T, 256)
    nt = T // tr
    hb = tr // 8

    def body(x0, x1, xh0, xh1, g0, g1, pos_ref, h_ref, hh_ref, dy_ref, cw_ref, cb_ref, wa_ref, wx_ref, ba_ref, bx_ref,
             lam_ref, db_ref, dcw_ref, dcb_ref, dwa_ref, dwx_ref, dba_ref, dbx_ref, dlam_ref,
             xbuf, hbuf, dbuf, abuf, gbuf, lbuf, mu_carry, dxc_head):
        step = pl.program_id(0)
        first_tile = step == nt - 1

        @pl.when(step == 0)
        def _():
            mu_carry[...] = jnp.zeros_like(mu_carry)
            dxc_head[...] = jnp.zeros_like(dxc_head)
            for ref in (dcw_ref, dcb_ref, dwa_ref, dwx_ref, dba_ref, dbx_ref, dlam_ref):
                ref[...] = jnp.zeros_like(ref)

        xr = jnp.concatenate([x0[...], x1[...]], axis=1)
        tail = jnp.where(first_tile, 0.0, jnp.concatenate([xh0[...], xh1[...]], axis=1))
        lam_v = lam_ref[...]
        sp = _softplus(-lam_v)
        reset = pos_ref[...] == 0
        cw = cw_ref[...]
        xs, xc, xcb, r, i, a_raw, mult_raw, a, mult = _rnn_recompute(
            xbuf, xr, tail, cw, cb_ref[...], wa_ref, wx_ref, ba_ref[...], bx_ref[...], sp, reset)
        g = jnp.concatenate([g0[...], g1[...]], axis=1)
        sg = _sigmoid(g)
        dy = dy_ref[...]
        h = h_ref[...]
        d_g = dy * h * (sg * (1.0 + g * (1.0 - sg)))
        abuf[...] = a
        gbuf[...] = dy * (g * sg)
        top = _scan_backward(abuf, gbuf, lbuf, mu_carry[0:1, :], tr)
        mu_carry[...] = jnp.broadcast_to(top, mu_carry.shape)
        lam_t = lbuf[...]
        hbuf[0:8, :] = jnp.where(first_tile, 0.0, hh_ref[...])
        hbuf[8:tr + 8, :] = h
        h_prev = hbuf[pl.ds(7, tr), :]
        live = jnp.logical_not(reset)
        d_a = jnp.where(live, lam_t * h_prev, 0.0)
        d_mult = jnp.where(live, lam_t * (i * xc), 0.0)
        d_ixc = lam_t * mult
        d_i = d_ixc * xc
        d_xc = d_ixc * i
        d_log_a = d_a * a_raw - d_mult * (a_raw * a_raw / mult_raw)
        d_log_a = jnp.where(live, d_log_a, 0.0)
        d_za = d_log_a * (-LRU_C * sp) * (r * (1.0 - r))
        d_zx = d_i * (i * (1.0 - i))
        dlam_ref[...] += jnp.sum(d_log_a * r, axis=0, keepdims=True) * (LRU_C * _sigmoid(-lam_v))
        dba_ref[...] += jnp.sum(d_za, axis=0, keepdims=True)
        dbx_ref[...] += jnp.sum(d_zx, axis=0, keepdims=True)
        dzab, dzxb = d_za.astype(BF16), d_zx.astype(BF16)
        back = []
        for j in range(RNN_BLOCKS):
            sl = slice(RNN_BW * j, RNN_BW * (j + 1))
            dwa_ref[j] += _dot_tn(xcb[:, sl], dzab[:, sl])
            dwx_ref[j] += _dot_tn(xcb[:, sl], dzxb[:, sl])
            back.append(_dot_nt(dzab[:, sl], wa_ref[j]) + _dot_nt(dzxb[:, sl], wx_ref[j]))
        d_xc = d_xc + jnp.concatenate(back, axis=1)
        dcb_ref[...] += jnp.sum(d_xc, axis=0, keepdims=True)
        for k in range(CONV_W):
            dcw_ref[k:k + 1, :] += jnp.sum(d_xc * xs[k], axis=0, keepdims=True)
        dbuf[0:tr, :] = d_xc
        dbuf[tr:tr + 8, :] = dxc_head[...]
        d_xr = d_xc * cw[CONV_W - 1:CONV_W, :]
        for k in range(CONV_W - 1):
            d_xr = d_xr + dbuf[pl.ds(CONV_W - 1 - k, tr), :] * cw[k:k + 1, :]
        dxc_head[...] = d_xc[0:8, :]
        db_ref[:, 0:D_MODEL] = d_xr.astype(BF16)
        db_ref[:, D_MODEL:2 * D_MODEL] = d_g.astype(BF16)

    rev = lambda s: nt - 1 - s
    blk = lambda cb: pl.BlockSpec((tr, CB), lambda s, cb=cb: (rev(s), cb))
    halo = lambda w, cb: pl.BlockSpec((8, w), lambda s, cb=cb: (jnp.maximum(rev(s) * hb - 1, 0), cb))
    tok = lambda w: pl.BlockSpec((tr, w), lambda s: (rev(s), 0))
    row = lambda w: pl.BlockSpec((1, w), lambda s: (0, 0))
    full3 = pl.BlockSpec((RNN_BLOCKS, RNN_BW, RNN_BW), lambda s: (0, 0, 0))
    cwspec = pl.BlockSpec((CONV_W, D_MODEL), lambda s: (0, 0))
    vec = jax.ShapeDtypeStruct((1, D_MODEL), F32)
    gate_w = jax.ShapeDtypeStruct((RNN_BLOCKS, RNN_BW, RNN_BW), F32)
    out_shape = (jax.ShapeDtypeStruct((T, 2 * D_MODEL), BF16), jax.ShapeDtypeStruct((CONV_W, D_MODEL), F32), vec,
                 gate_w, gate_w, vec, vec, vec)
    big = lambda: pltpu.VMEM((tr, D_MODEL), F32)
    ext = lambda: pltpu.VMEM((tr + 8, D_MODEL), F32)
    return pl.pallas_call(
        body, out_shape=out_shape, grid=(nt,), name="rnn_backward",
        in_specs=[blk(CB_XR), blk(CB_XR + 1), halo(CB, CB_XR), halo(CB, CB_XR + 1), blk(CB_GR), blk(CB_GR + 1),
                  pl.BlockSpec((tr, 1), lambda s: (rev(s), 0)), tok(D_MODEL), halo(D_MODEL, 0), tok(D_MODEL),
                  cwspec, row(D_MODEL), full3, full3, row(D_MODEL), row(D_MODEL), row(D_MODEL)],
        out_specs=(tok(2 * D_MODEL), cwspec, row(D_MODEL), full3, full3, row(D_MODEL), row(D_MODEL), row(D_MODEL)),
        scratch_shapes=[ext(), ext(), ext(), big(), big(), big(), pltpu.VMEM((8, D_MODEL), F32), pltpu.VMEM((8, D_MODEL), F32)],
        compiler_params=_cp("arbitrary"),
    )(proj, proj, proj, proj, proj, proj, pos_col, h_rnn, h_rnn, d_y, conv_w, conv_b, rwa, rwx, ba, bx, lam)


def _input_backward(pieces, w_in, x, dx2, mod_row, norm_g):
    T = x.shape[0]
    tm = min(T, 256)
    n = len(pieces)

    def body(*refs):
        d_refs = refs[:n]
        w_ref, x_ref, dx2_ref, mod_ref, g_ref, gx_ref, dshift_ref, dscale_ref, dg_ref = refs[n:]
        i = pl.program_id(0)
        dh = None
        for d_ref, (_, start, count) in zip(d_refs, pieces):
            part = _dot_nt(d_ref[...], w_ref[:, start * CB:(start + count) * CB])
            dh = part if dh is None else dh + part

        @pl.when(i == 0)
        def _():
            dshift_ref[...] = jnp.zeros_like(dshift_ref)
            dscale_ref[...] = jnp.zeros_like(dscale_ref)
            dg_ref[...] = jnp.zeros_like(dg_ref)

        xf = x_ref[...]
        r1 = _rms(xf)
        xn = xf * r1
        gn = g_ref[...]
        s1 = 1.0 + mod_ref[:, D_MODEL:2 * D_MODEL]
        dshift_ref[...] += jnp.sum(dh, axis=0, keepdims=True)
        dscale_ref[...] += jnp.sum(dh * (xn * gn), axis=0, keepdims=True)
        dg_ref[...] += jnp.sum(dh * s1 * xn, axis=0, keepdims=True)
        dxn = dh * s1 * gn
        gx_ref[...] = dx2_ref[...] + r1 * (dxn - xn * jnp.mean(dxn * xn, axis=-1, keepdims=True))

    tok = lambda w: pl.BlockSpec((tm, w), lambda i: (i, 0))
    row = lambda w: pl.BlockSpec((1, w), lambda i: (0, 0))
    vec = jax.ShapeDtypeStruct((1, D_MODEL), F32)
    return pl.pallas_call(
        body, out_shape=(jax.ShapeDtypeStruct((T, D_MODEL), F32), vec, vec, vec), grid=(T // tm,), name="input_backward",
        in_specs=[tok(c * CB) for _, _, c in pieces]
        + [pl.BlockSpec((D_MODEL, IN_W), lambda i: (0, 0), pipeline_mode=pl.Buffered(1)), tok(D_MODEL), tok(D_MODEL),
           row(ADA_W), row(D_MODEL)],
        out_specs=(tok(D_MODEL), row(D_MODEL), row(D_MODEL), row(D_MODEL)),
        compiler_params=_cp("arbitrary"),
    )(*[p[0] for p in pieces], w_in, x, dx2, mod_row, norm_g)


def _weight_grad(a, b, tag, into=None, col_block=0, total_cols=None):
    T, M = a.shape
    N = b.shape[1]
    tn = CB
    total_cols = N if total_cols is None else total_cols

    def body(*refs):
        a_ref, b_ref, o_ref = refs[0], refs[1], refs[-1]
        o_ref[...] = _dot_tn(a_ref[...], b_ref[...])

    in_specs = [pl.BlockSpec((T, M), lambda j: (0, 0), pipeline_mode=pl.Buffered(1)), pl.BlockSpec((T, tn), lambda j: (0, j))]
    args = [a, b]
    aliases = {}
    if into is not None:
        in_specs.append(ANY)
        args.append(into)
        aliases = {2: 0}
    return pl.pallas_call(
        body, out_shape=jax.ShapeDtypeStruct((M, total_cols), F32), grid=(N // tn,), name=f"weight_grad_{tag}",
        in_specs=in_specs, out_specs=pl.BlockSpec((M, tn), lambda j: (0, col_block + j)),
        input_output_aliases=aliases, compiler_params=_cp("parallel"),
    )(*args)


def _adamw(w, g, m, v):
    m = ADAM_B1 * m + (1.0 - ADAM_B1) * g
    v = ADAM_B2 * v + (1.0 - ADAM_B2) * (g * g)
    m_hat = m / (1.0 - ADAM_B1 ** ADAM_STEP)
    v_hat = v / (1.0 - ADAM_B2 ** ADAM_STEP)
    delta = -ADAM_LR * (m_hat / (jnp.sqrt(v_hat) + ADAM_EPS) + ADAM_WD * w)
    return delta, m, v


def _sum_landed(kind, own, land, where, tag):
    if kind == "in":
        R, C = land.shape[1:]
        tr = 256
        grid = (R // tr,)
        own_spec = pl.BlockSpec((tr, C), lambda i, w: (i, w[0]))
        land_spec = pl.BlockSpec((3, tr, C), lambda i, w: (0, i, 0))
        out_spec = pl.BlockSpec((1, tr, C), lambda i, w: (w[1], i, 0))
        out_shape = (2, R, C)
        pick = lambda ref: ref[...]
    elif kind == "sq":
        R, C = land.shape[1:]
        grid = (1,)
        own_spec = pl.BlockSpec((1, R, C), lambda i, w: (w[0], 0, 0))
        land_spec = pl.BlockSpec((3, R, C), lambda i, w: (0, 0, 0))
        out_spec = pl.BlockSpec((1, R, C), lambda i, w: (w[1], 0, 0))
        out_shape = (2, R, C)
        pick = lambda ref: ref[0]
    else:
        B, R, C = land.shape[1:]
        grid = (1,)
        own_spec = pl.BlockSpec((B, 1, R, C), lambda i, w: (0, w[0], 0, 0))
        land_spec = pl.BlockSpec((3, B, R, C), lambda i, w: (0, 0, 0, 0))
        out_spec = pl.BlockSpec((B, 1, R, C), lambda i, w: (0, w[1], 0, 0))
        out_shape = (B, 2, R, C)
        pick = lambda ref: ref[:, 0]

    def body(w_ref, own_ref, l_ref, o_ref):
        total = ((pick(own_ref) + l_ref[0]) + l_ref[1]) + l_ref[2]
        if kind == "in":
            o_ref[0] = total
        elif kind == "sq":
            o_ref[0] = total
        else:
            o_ref[:, 0] = total

    grid_spec = pltpu.PrefetchScalarGridSpec(num_scalar_prefetch=1, grid=grid, in_specs=[own_spec, land_spec], out_specs=out_spec)
    return pl.pallas_call(
        body, out_shape=jax.ShapeDtypeStruct(out_shape, F32), grid_spec=grid_spec, name=f"sum_landed_{tag}",
        compiler_params=_cp("parallel"),
    )(where, own, land)


def _adamw_shard(g, w, m, v, tag):
    R, C = w.shape
    tr = min(R, 256)

    def body(g_ref, w_ref, m_ref, v_ref, d_ref, nm_ref, nv_ref):
        d, nm, nv = _adamw(w_ref[...], g_ref[...], m_ref[...], v_ref[...])
        d_ref[...] = d
        nm_ref[...] = nm
        nv_ref[...] = nv

    spec = pl.BlockSpec((tr, C), lambda i: (i, 0))
    sds = jax.ShapeDtypeStruct((R, C), F32)
    return pl.pallas_call(
        body, out_shape=(sds,) * 3, grid=(R // tr,), name=f"adamw_{tag}",
        in_specs=[spec] * 4, out_specs=(spec,) * 3, compiler_params=_cp("parallel"),
    )(g, w, m, v)


def _adamw_w_ada(c_t, dmod_cols, w, m, v):
    R, C = w.shape

    def body(ct_ref, dm_ref, w_ref, m_ref, v_ref, g_ref, d_ref, nm_ref, nv_ref):
        g = _dot(ct_ref[...].astype(BF16), dm_ref[...].astype(BF16))
        d, nm, nv = _adamw(w_ref[...], g, m_ref[...], v_ref[...])
        g_ref[...] = g
        d_ref[...] = d
        nm_ref[...] = nm
        nv_ref[...] = nv

    tr = 256
    spec = pl.BlockSpec((tr, C), lambda i: (i, 0))
    sds = jax.ShapeDtypeStruct((R, C), F32)
    return pl.pallas_call(
        body, out_shape=(sds,) * 4, grid=(R // tr,), name="adamw_w_ada",
        in_specs=[pl.BlockSpec((tr, 128), lambda i: (i, 0)), pl.BlockSpec((128, C), lambda i: (0, 0))] + [spec] * 3,
        out_specs=(spec,) * 4, compiler_params=_cp("parallel"),
    )(c_t, dmod_cols, w, m, v)


def _adamw_small(small_all, ws, ms, vs):
    def body(s_ref, w_ref, m_ref, v_ref, g_ref, d_ref, nm_ref, nv_ref):
        g = s_ref[0]
        for b in range(1, N_DEV):
            g = g + s_ref[b]
        d, nm, nv = _adamw(w_ref[...], g, m_ref[...], v_ref[...])
        g_ref[...] = g
        d_ref[...] = d
        nm_ref[...] = nm
        nv_ref[...] = nv

    sds = jax.ShapeDtypeStruct((SMALL_ROWS, D_MODEL), F32)
    return pl.pallas_call(
        body, out_shape=(sds,) * 4, name="adamw_small", in_specs=[VMEM_SPEC] * 4, out_specs=(VMEM_SPEC,) * 4,
        compiler_params=pltpu.CompilerParams(vmem_limit_bytes=VMEM_LIMIT_V7X),
    )(small_all, ws, ms, vs)


ROW_MOD, ROW_NORM_G, ROW_CONV_B, ROW_BA, ROW_BX, ROW_LAM, ROW_FINAL_G, ROW_SINKS, ROW_CONV_W = 0, 3, 4, 5, 6, 7, 8, 9, 10


def _pack_small(b_ada, norm_g, conv_b, ba, bx, lam, final_g, sinks, conv_w_full):
    rows = [b_ada.reshape(3, D_MODEL), norm_g, conv_b, ba, bx, lam, final_g.reshape(1, D_MODEL),
            jnp.pad(sinks.reshape(1, -1), ((0, 0), (0, D_MODEL - sinks.size))), conv_w_full,
            jnp.zeros((SMALL_ROWS - 14, D_MODEL), F32)]
    return jnp.concatenate([r.astype(F32) for r in rows], axis=0)


def kernel(x, c, positions, w_ada, b_ada, norm_g, w_in, attn_sinks, conv_w, conv_b, rg_wa, rg_ba, rg_wx, rg_bx, rg_lambda, w_attn_proj, w_rnn_proj, w_out, final_g, loss_target, m_w_ada, m_b_ada, m_norm_g, m_w_in, m_attn_sinks, m_conv_w, m_conv_b, m_rg_wa, m_rg_ba, m_rg_wx, m_rg_bx, m_rg_lambda, m_w_attn_proj, m_w_rnn_proj, m_w_out, m_final_g, v_w_ada, v_b_ada, v_norm_g, v_w_in, v_attn_sinks, v_conv_w, v_conv_b, v_rg_wa, v_rg_ba, v_rg_wx, v_rg_bx, v_rg_lambda, v_w_attn_proj, v_w_rnn_proj, v_w_out, v_final_g):
    T = x.shape[1]
    my_chip = lax.axis_index("x") * 2 + lax.axis_index("y")
    my_dev = my_chip * 2 + lax.axis_index("c")
    x2d, tgt = x[0], loss_target[0]
    pos_col = positions.reshape(T, 1)

    chip_idx = my_chip.reshape(1).astype(jnp.int32)
    c_idx = lax.axis_index("c").reshape(1).astype(jnp.int32)
    sq_place = ((D_MODEL, D_MODEL), (SHARD_ROWS, D_MODEL), lambda chip: (chip, 0))
    rg_place = ((RNN_BLOCKS, RNN_BW, RNN_BW), (RNN_BLOCKS, SHARD_RG, RNN_BW), lambda chip: (0, chip, 0))
    gathered = _gather_weights(
        c.reshape(1, 1, D_MODEL), w_ada[0],
        _cast_place(w_in[0], chip_idx, (D_MODEL, IN_W), (D_MODEL, SHARD_IN), lambda chip: (0, chip), "w_in"),
        _cast_place(w_attn_proj[0], chip_idx, *sq_place, "w_attn_proj"),
        _cast_place(w_rnn_proj[0], chip_idx, *sq_place, "w_rnn_proj"),
        _cast_place(w_out[0], chip_idx, *sq_place, "w_out"),
        _cast_place(rg_wa[0], chip_idx, *rg_place, "rg_wa"),
        _cast_place(rg_wx[0], chip_idx, *rg_place, "rg_wx"),
        conv_w[0])
    w_in_f = gathered[0].reshape(D_MODEL, IN_W)
    wap_f, wrp_f, wo_f = (g.reshape(D_MODEL, D_MODEL) for g in gathered[1:4])
    rwa_f, rwx_f = (g.reshape(RNN_BLOCKS, RNN_BW, RNN_BW) for g in gathered[4:6])
    cw_chips, c_all, mod_chips = gathered[6:]
    conv_w_f = jnp.transpose(cw_chips, (1, 0, 2)).reshape(CONV_W, D_MODEL)
    mod_all = jnp.transpose(mod_chips, (1, 0, 2)).reshape(N_DEV, ADA_W) + b_ada
    mod_row = lax.dynamic_slice_in_dim(mod_all, my_dev, 1, axis=0)

    tabs = _rope_tables(pos_col)
    h = _prenorm(x2d, mod_row, norm_g)
    proj = _in_projection(h, w_in_f)
    y_attn = _attn_forward(proj, tabs, attn_sinks)
    y_rnn, h_rnn = _rnn_forward(proj, pos_col, conv_w_f, conv_b, rwa_f, rwx_f, rg_ba, rg_bx, rg_lambda)
    (dx2, merged, d_o, d_pa, d_pr, d_ya, d_yr, d_c, d_final_g, d_gate, loss_vec) = _merge_and_head(
        x2d, tgt, y_attn, y_rnn, proj, wap_f, wrp_f, wo_f, mod_row, final_g.reshape(1, D_MODEL))

    sq = (N_CHIPS, 2, SHARD_ROWS // 2, D_MODEL)
    rg = (RNN_BLOCKS, N_CHIPS, 2, SHARD_RG // 2, RNN_BW)
    rg_flat = (RNN_BLOCKS * N_CHIPS, 2, SHARD_RG // 2, RNN_BW)

    def chip_sum_and_start(views, axes, flat, unflat, tags_, kinds_, group):
        from_sib = _swap_halves(views, axes)
        sums = [_presum(v.reshape(f), s.reshape(f[:1] + f[2:]), c_idx, t).reshape(u)
                for v, s, f, u, t in zip(views, from_sib, flat, unflat, tags_)]
        return _exchange_start(sums, kinds_, group)

    g_ap = _weight_grad(y_attn, d_pa, "w_attn_proj")
    g_rp = _weight_grad(y_rnn, d_pr, "w_rnn_proj")
    g_o = _weight_grad(merged, d_o, "w_out")
    sq_half = (N_CHIPS, SHARD_ROWS // 2, D_MODEL)
    started1 = chip_sum_and_start([g_ap.reshape(sq), g_rp.reshape(sq), g_o.reshape(sq)], [1, 1, 1], [sq] * 3, [sq_half] * 3,
                                  ["w_attn_proj", "w_rnn_proj", "w_out"], ["sq"] * 3, "proj")
    d_q, d_kv, d_ga, d_sinks = _attn_backward(proj, d_ya, tabs, attn_sinks + started1[4][0, 0])
    d_b, d_conv_w, d_conv_b, d_rwa, d_rwx, d_ba, d_bx, d_lam = _rnn_backward(
        proj, pos_col, h_rnn, d_yr, conv_w_f, conv_b, rwa_f, rwx_f, rg_ba, rg_bx, rg_lambda)
    pieces = [(d_q, CB_Q, 2), (d_kv, CB_KV, 1), (d_ga, CB_GA, 2), (d_b, CB_XR, 4), (d_c, CB_MA, 4)]
    g_in = None
    for arr, start, _ in pieces:
        g_in = _weight_grad(h, arr, f"w_in_{start}", into=g_in, col_block=start, total_cols=IN_W)
    started2 = chip_sum_and_start(
        [g_in.reshape(2, D_MODEL // 2, IN_W), d_rwa.reshape(rg), d_rwx.reshape(rg)], [0, 2, 2],
        [(1, 2, D_MODEL // 2, IN_W), rg_flat, rg_flat],
        [(D_MODEL // 2, IN_W), (RNN_BLOCKS, N_CHIPS, SHARD_RG // 2, RNN_BW), (RNN_BLOCKS, N_CHIPS, SHARD_RG // 2, RNN_BW)],
        ["w_in", "rg_wa", "rg_wx"], ["in", "rg", "rg"], "in")
    grad_x, d_shift, d_scale, d_norm_g = _input_backward(pieces, w_in_f, x2d, dx2, mod_row + started2[4][0, 0], norm_g)

    d_mod = jnp.concatenate([d_shift, d_scale, d_gate], axis=1)
    small = _pack_small(d_mod, d_norm_g, d_conv_b, d_ba, d_bx, d_lam, d_final_g, d_sinks[:, :N_HEADS], d_conv_w)
    small_all = _gather_small(small)
    sums1, lands1 = _exchange_wait(*started1[:4], grad_x, "proj")
    sums2, lands2 = _exchange_wait(*started2[:4], grad_x, "in")
    tags = ["w_in", "w_attn_proj", "w_rnn_proj", "w_out", "rg_wa", "rg_wx"]
    chip_sums = [sums2[0]] + list(sums1) + list(sums2[1:])
    lands = [lands2[0]] + list(lands1) + list(lands2[1:])
    where = jnp.concatenate([chip_idx, c_idx])
    kinds = ["in", "sq", "sq", "sq", "rg", "rg"]
    halves = [_sum_landed(kinds[i], chip_sums[i], lands[i], where, tags[i]) for i in range(6)]
    grads = _assemble_with_sibling(halves, [0, 0, 0, 0, 1, 1])
    shapes2d = [(D_MODEL, SHARD_IN), (SHARD_ROWS, D_MODEL), (SHARD_ROWS, D_MODEL), (SHARD_ROWS, D_MODEL),
                (RNN_BLOCKS * SHARD_RG, RNN_BW), (RNN_BLOCKS * SHARD_RG, RNN_BW)]
    big_w = [w_in, w_attn_proj, w_rnn_proj, w_out, rg_wa, rg_wx]
    big_m = [m_w_in, m_w_attn_proj, m_w_rnn_proj, m_w_out, m_rg_wa, m_rg_wx]
    big_v = [v_w_in, v_w_attn_proj, v_w_rnn_proj, v_w_out, v_rg_wa, v_rg_wx]
    res = {}
    for i, tag in enumerate(tags):
        g = grads[i].reshape(shapes2d[i])
        outs = _adamw_shard(g, big_w[i].reshape(shapes2d[i]), big_m[i].reshape(shapes2d[i]), big_v[i].reshape(shapes2d[i]), tag)
        res[tag] = [o.reshape(big_w[i].shape) for o in (g,) + tuple(outs)]

    dmod_all = small_all[:, ROW_MOD:ROW_MOD + 3, :].reshape(N_DEV, ADA_W)
    dmod_cols = lax.dynamic_slice_in_dim(dmod_all, my_chip * SHARD_ADA, SHARD_ADA, axis=1)
    c_t = jnp.pad(jnp.transpose(c_all.reshape(N_DEV, D_MODEL)), ((0, 0), (0, 128 - N_DEV)))
    dmod_cols = jnp.pad(dmod_cols, ((0, 128 - N_DEV), (0, 0)))
    res["w_ada"] = [o.reshape(w_ada.shape) for o in _adamw_w_ada(c_t, dmod_cols, w_ada[0], m_w_ada[0], v_w_ada[0])]

    def full_conv(a):
        return lax.dynamic_update_slice_in_dim(jnp.zeros((CONV_W, D_MODEL), F32), a[0], my_chip * (D_MODEL // N_CHIPS), axis=1)

    packed = [_pack_small(p[0], p[1], p[2], p[3], p[4], p[5], p[6], p[7], full_conv(p[8])) for p in (
        (b_ada, norm_g, conv_b, rg_ba, rg_bx, rg_lambda, final_g, attn_sinks, conv_w),
        (m_b_ada, m_norm_g, m_conv_b, m_rg_ba, m_rg_bx, m_rg_lambda, m_final_g, m_attn_sinks, m_conv_w),
        (v_b_ada, v_norm_g, v_conv_b, v_rg_ba, v_rg_bx, v_rg_lambda, v_final_g, v_attn_sinks, v_conv_w))]
    small_out = _adamw_small(small_all, *packed)

    def unpack(slab):
        cw = lax.dynamic_slice_in_dim(slab[ROW_CONV_W:ROW_CONV_W + CONV_W], my_chip * (D_MODEL // N_CHIPS),
                                      D_MODEL // N_CHIPS, axis=1)
        return {
            "b_ada": slab[ROW_MOD:ROW_MOD + 3].reshape(1, ADA_W), "norm_g": slab[ROW_NORM_G:ROW_NORM_G + 1],
            "conv_b": slab[ROW_CONV_B:ROW_CONV_B + 1], "rg_ba": slab[ROW_BA:ROW_BA + 1], "rg_bx": slab[ROW_BX:ROW_BX + 1],
            "rg_lambda": slab[ROW_LAM:ROW_LAM + 1], "final_g": slab[ROW_FINAL_G], "attn_sinks": slab[ROW_SINKS:ROW_SINKS + 1, :N_HEADS],
            "conv_w": cw[None],
        }

    small_res = [unpack(s) for s in small_out]
    order = ["w_ada", "b_ada", "norm_g", "w_in", "attn_sinks", "conv_w", "conv_b", "rg_wa", "rg_ba", "rg_wx", "rg_bx",
             "rg_lambda", "w_attn_proj", "w_rnn_proj", "w_out", "final_g"]
    loss = lax.psum(loss_vec[0, 0], ("x", "y", "c"))
    outs = [loss, grad_x[None]]
    for kind in range(4):
        for name in order:
            outs.append(res[name][kind] if name in res else small_res[kind][name])
    return tuple(outs)
```

```python
import numpy as np
import jax
import jax.numpy as jnp
from jax import lax
from jax.experimental import pallas as pl
from jax.experimental.pallas import tpu as pltpu

F32 = jnp.float32
BF16 = jnp.bfloat16

D_MODEL = 1024
N_HEADS = 16
N_KV = 4
HEAD_DIM = 64
GROUP = N_HEADS // N_KV
BLOCK = 128
KV_W = N_KV * HEAD_DIM
ROT_HALF = 8
ROPE_THETA = 500000.0
RNN_BLOCKS = 4
RNN_BW = 256
CONV_W = 4
LRU_C = 8.0
NORM_EPS = 1e-6
IN_W = 6656
CB = 512
N_CB = IN_W // CB
CB_Q, CB_KV, CB_GA, CB_XR, CB_GR, CB_MA, CB_MR = 0, 2, 3, 5, 7, 9, 11
N_CHIPS = 4
N_DEV = 8
SHARD_IN = IN_W // N_CHIPS
SHARD_ROWS = D_MODEL // N_CHIPS
SHARD_RG = RNN_BW // N_CHIPS
ADA_W = 3 * D_MODEL
SHARD_ADA = ADA_W // N_CHIPS
SMALL_ROWS = 16

ADAM_LR = 0.001
ADAM_B1 = 0.9
ADAM_B2 = 0.999
ADAM_EPS = 1e-08
ADAM_WD = 0.01
ADAM_STEP = 10

VMEM_LIMIT_V7X = 52 * 1024 * 1024
MESH = pl.DeviceIdType.MESH
ANY = pl.BlockSpec(memory_space=pl.ANY)
VMEM_SPEC = pl.BlockSpec(memory_space=pltpu.VMEM)


def _cp(*sem):
    return pltpu.CompilerParams(dimension_semantics=sem if sem else None, vmem_limit_bytes=VMEM_LIMIT_V7X)


def _dot(a, b):
    return jnp.dot(a, b, preferred_element_type=F32)


def _dot_nt(a, b):
    return lax.dot_general(a, b, (((1,), (1,)), ((), ())), preferred_element_type=F32)


def _dot_tn(a, b):
    return lax.dot_general(a, b, (((0,), (0,)), ((), ())), preferred_element_type=F32)


def _sigmoid(z):
    return 1.0 / (1.0 + jnp.exp(-z))


def _neg_expm1(z):
    series = -(z * (1.0 + z * (0.5 + z * (1.0 / 6.0 + z * (1.0 / 24.0 + z * (1.0 / 120.0))))))
    return jnp.where(z > -0.05, series, 1.0 - jnp.exp(z))


def _softplus(z):
    u = jnp.exp(-jnp.abs(z))
    log1p_u = jnp.where(u < 1e-3, u * (1.0 - u * (0.5 - u * (1.0 / 3.0))), jnp.log(1.0 + u))
    return jnp.maximum(z, 0.0) + log1p_u


def _rms(xf):
    return lax.rsqrt(jnp.mean(xf * xf, axis=-1, keepdims=True) + NORM_EPS)


def _me():
    return lax.axis_index("x"), lax.axis_index("y"), lax.axis_index("c")


def _peer(mask):
    x, y, c = _me()
    fx, fy, fc = (mask >> 2) & 1, (mask >> 1) & 1, mask & 1
    return (x ^ fx if fx else x, y ^ fy if fy else y, c ^ fc if fc else c)


def _chip_of(pos):
    return pos[0] * 2 + pos[1]


CHIP_MASKS = (4, 2, 6)
ALL_MASKS = (1, 2, 3, 4, 5, 6, 7)


def _gather_weights(c_row, w_ada_s, b_w_in, b_wap, b_wrp, b_wo, b_rwa, b_rwx, conv_w_s):
    def body(c_ref, wada_ref, win_s, wap_s, wrp_s, wo_s, rwa_s, rwx_s, cw_s,
             win_f, wap_f, wrp_f, wo_f, rwa_f, rwx_f, cw_f, call_ref, mod_ref,
             wsend, wrecv, lsem, csend, crecv, msend, mrecv, fsend, frecv):
        me = _me()
        my_chip = _chip_of(me)
        my_dev = my_chip * 2 + me[2]
        fulls = (win_f, wap_f, wrp_f, wo_f, rwa_f, rwx_f, cw_f)

        def slot(idx, chip, half=None):
            full = fulls[idx]
            if idx == 0:
                cols = pl.ds(pl.multiple_of(chip * SHARD_IN, 128), SHARD_IN)
                return full.at[:, :, cols] if half is None else full.at[half, :, cols]
            if idx in (1, 2, 3):
                return full.at[chip] if half is None else full.at[chip, half]
            if idx in (4, 5):
                return full.at[:, chip] if half is None else full.at[:, chip, half]
            return full.at[chip]

        def my_half(idx):
            return cw_s if idx == 6 else slot(idx, my_chip, me[2])

        def wcopy(idx, k, to):
            return pltpu.make_async_remote_copy(
                src_ref=my_half(idx), dst_ref=slot(idx, my_chip, None if idx == 6 else me[2]),
                send_sem=wsend.at[idx, k], recv_sem=wrecv.at[idx, k], device_id=to, device_id_type=MESH)

        def wrecv_wait(idx, k, frm):
            pltpu.make_async_remote_copy(
                src_ref=my_half(idx), dst_ref=slot(idx, _chip_of(frm), None if idx == 6 else me[2]),
                send_sem=wsend.at[idx, k], recv_sem=wrecv.at[idx, k], device_id=frm, device_id_type=MESH).wait_recv()

        def forward(idx, k, chip, half, to):
            return pltpu.make_async_remote_copy(
                src_ref=slot(idx, chip, half), dst_ref=slot(idx, chip, half),
                send_sem=fsend.at[idx, k], recv_sem=frecv.at[idx, k], device_id=to, device_id_type=MESH)

        sends = []
        for idx in range(7):
            for k, mask in enumerate(CHIP_MASKS):
                cp = wcopy(idx, k, _peer(mask))
                cp.start()
                sends.append(cp)
        local = [pltpu.make_async_copy(cw_s, slot(6, my_chip), lsem.at[0])]
        for cp in local:
            cp.start()

        call_ref[my_dev] = c_ref[0]
        csends = []
        for k, mask in enumerate(ALL_MASKS):
            cp = pltpu.make_async_remote_copy(
                src_ref=c_ref.at[0], dst_ref=call_ref.at[my_dev],
                send_sem=csend.at[k], recv_sem=crecv.at[k], device_id=_peer(mask), device_id_type=MESH)
            cp.start()
            csends.append(cp)
        for k, mask in enumerate(ALL_MASKS):
            frm = _peer(mask)
            pltpu.make_async_remote_copy(
                src_ref=c_ref.at[0], dst_ref=call_ref.at[_chip_of(frm) * 2 + frm[2]],
                send_sem=csend.at[k], recv_sem=crecv.at[k], device_id=frm, device_id_type=MESH).wait_recv()
        for cp in csends:
            cp.wait_send()

        c_all = call_ref[...].reshape(N_DEV, D_MODEL).astype(BF16)
        mod_ref[my_chip] = _dot(c_all, wada_ref[...].astype(BF16))
        msends = []
        for k, mask in enumerate(CHIP_MASKS):
            cp = pltpu.make_async_remote_copy(
                src_ref=mod_ref.at[my_chip], dst_ref=mod_ref.at[my_chip],
                send_sem=msend.at[k], recv_sem=mrecv.at[k], device_id=_peer(mask), device_id_type=MESH)
            cp.start()
            msends.append(cp)
        for k, mask in enumerate(CHIP_MASKS):
            frm = _peer(mask)
            pltpu.make_async_remote_copy(
                src_ref=mod_ref.at[my_chip], dst_ref=mod_ref.at[_chip_of(frm)],
                send_sem=msend.at[k], recv_sem=mrecv.at[k], device_id=frm, device_id_type=MESH).wait_recv()
        for cp in msends:
            cp.wait_send()

        sib = _peer(1)
        forwards = []
        for idx in range(7):
            for k, mask in enumerate(CHIP_MASKS):
                frm = _peer(mask)
                wrecv_wait(idx, k, frm)
                if idx < 6:
                    cp = forward(idx, k, _chip_of(frm), me[2], sib)
                    cp.start()
                    forwards.append(cp)
        for idx in range(6):
            for k, mask in enumerate(CHIP_MASKS):
                forward(idx, k, _chip_of(_peer(mask)), 1 - me[2], sib).wait_recv()
        for cp in sends + forwards:
            cp.wait_send()
        for cp in local:
            cp.wait()

    out_shape = (
        jax.ShapeDtypeStruct((2, D_MODEL // 2, IN_W), BF16),
        jax.ShapeDtypeStruct((N_CHIPS, 2, SHARD_ROWS // 2, D_MODEL), BF16),
        jax.ShapeDtypeStruct((N_CHIPS, 2, SHARD_ROWS // 2, D_MODEL), BF16),
        jax.ShapeDtypeStruct((N_CHIPS, 2, SHARD_ROWS // 2, D_MODEL), BF16),
        jax.ShapeDtypeStruct((RNN_BLOCKS, N_CHIPS, 2, SHARD_RG // 2, RNN_BW), BF16),
        jax.ShapeDtypeStruct((RNN_BLOCKS, N_CHIPS, 2, SHARD_RG // 2, RNN_BW), BF16),
        jax.ShapeDtypeStruct((N_CHIPS, CONV_W, D_MODEL // N_CHIPS), F32),
        jax.ShapeDtypeStruct((N_DEV, 1, D_MODEL), F32),
        jax.ShapeDtypeStruct((N_CHIPS, N_DEV, SHARD_ADA), F32),
    )
    return pl.pallas_call(
        body, out_shape=out_shape, name="gather_weights",
        in_specs=[VMEM_SPEC, VMEM_SPEC] + [ANY] * 7,
        out_specs=tuple([ANY] * 7 + [VMEM_SPEC, VMEM_SPEC]),
        scratch_shapes=[
            pltpu.SemaphoreType.DMA((7, 3)), pltpu.SemaphoreType.DMA((7, 3)), pltpu.SemaphoreType.DMA((7,)),
            pltpu.SemaphoreType.DMA((7,)), pltpu.SemaphoreType.DMA((7,)),
            pltpu.SemaphoreType.DMA((3,)), pltpu.SemaphoreType.DMA((3,)),
            pltpu.SemaphoreType.DMA((6, 3)), pltpu.SemaphoreType.DMA((6, 3)),
        ],
        input_output_aliases={2: 0, 3: 1, 4: 2, 5: 3, 6: 4, 7: 5},
        compiler_params=pltpu.CompilerParams(vmem_limit_bytes=VMEM_LIMIT_V7X),
    )(c_row, w_ada_s, b_w_in.reshape(out_shape[0].shape), b_wap.reshape(out_shape[1].shape),
      b_wrp.reshape(out_shape[2].shape), b_wo.reshape(out_shape[3].shape), b_rwa.reshape(out_shape[4].shape),
      b_rwx.reshape(out_shape[5].shape), conv_w_s)


def _cast_place(shard, chip_idx, full_shape, block, index_map, tag):
    def body(chip_ref, s_ref, o_ref):
        o_ref[...] = s_ref[...].astype(BF16)

    grid_spec = pltpu.PrefetchScalarGridSpec(
        num_scalar_prefetch=1, grid=(1,),
        in_specs=[pl.BlockSpec(shard.shape, lambda i, chip_ref: (0,) * shard.ndim)],
        out_specs=pl.BlockSpec(block, lambda i, chip_ref: index_map(chip_ref[0])))
    return pl.pallas_call(
        body, out_shape=jax.ShapeDtypeStruct(full_shape, BF16), grid_spec=grid_spec, name=f"cast_place_{tag}",
        compiler_params=_cp("arbitrary"),
    )(chip_idx, shard)


HBM_SPEC = pl.BlockSpec(memory_space=pltpu.HBM)
SEM_SPEC = pl.BlockSpec(memory_space=pltpu.SEMAPHORE)


def _shard_of(ref, kind, chip):
    if kind == "in":
        return ref.at[:, pl.ds(pl.multiple_of(chip * SHARD_IN, 128), SHARD_IN)]
    return ref.at[chip] if kind == "sq" else ref.at[:, chip]


def _land_shape(src, kind):
    if kind == "in":
        return (3, src.shape[0], SHARD_IN)
    return (3,) + src.shape[1:] if kind == "sq" else (3, src.shape[0]) + src.shape[2:]


def _exchange_start(srcs, kinds, tag):
    n = len(srcs)
    lands = [pltpu.with_memory_space_constraint(lax.empty(_land_shape(s, k), s.dtype), pltpu.HBM) for s, k in zip(srcs, kinds)]

    def body(*refs):
        src_refs, land_refs = refs[:n], refs[n:2 * n]
        ssems, rsems = refs[2 * n:3 * n], refs[3 * n:4 * n]
        token = refs[6 * n]
        for i in range(n):
            for k, mask in enumerate(CHIP_MASKS):
                to = _peer(mask)
                pltpu.make_async_remote_copy(
                    src_ref=_shard_of(src_refs[i], kinds[i], _chip_of(to)), dst_ref=land_refs[i].at[k],
                    send_sem=ssems[i], recv_sem=rsems[i], device_id=to, device_id_type=MESH).start()
        token[...] = jnp.zeros_like(token)

    sem = pltpu.SemaphoreType.DMA(())
    out_shape = ((sem,) * (2 * n) + tuple(pltpu.HBM(s.shape, s.dtype) for s in srcs)
                 + tuple(pltpu.HBM(l.shape, l.dtype) for l in lands) + (jax.ShapeDtypeStruct((8, 128), F32),))
    outs = pl.pallas_call(
        body, out_shape=out_shape, name=f"exchange_start_{tag}",
        in_specs=[HBM_SPEC] * (2 * n), out_specs=tuple([SEM_SPEC] * (2 * n) + [HBM_SPEC] * (2 * n) + [VMEM_SPEC]),
        input_output_aliases={i: 2 * n + i for i in range(2 * n)},
        compiler_params=pltpu.CompilerParams(has_side_effects=pltpu.SideEffectType.DATAFLOW_SIDE_EFFECTING),
    )(*[pltpu.with_memory_space_constraint(s, pltpu.HBM) for s in srcs], *lands)
    return outs[:n], outs[n:2 * n], outs[2 * n:3 * n], outs[3 * n:4 * n], outs[4 * n]


def _exchange_wait(ssems, rsems, srcs, lands, after, tag):
    n = len(srcs)

    def body(*refs):
        land_refs = refs[n:2 * n]
        ssem_refs, rsem_refs = refs[2 * n:3 * n], refs[3 * n:4 * n]
        for i in range(n):
            all_three = pltpu.make_async_remote_copy(
                src_ref=land_refs[i], dst_ref=land_refs[i], send_sem=ssem_refs[i], recv_sem=rsem_refs[i],
                device_id=_me(), device_id_type=MESH)
            all_three.wait_send()
            all_three.wait_recv()

    outs = pl.pallas_call(
        body, out_shape=tuple(pltpu.HBM(a.shape, a.dtype) for a in list(srcs) + list(lands)), name=f"exchange_wait_{tag}",
        in_specs=[HBM_SPEC] * (2 * n) + [SEM_SPEC] * (2 * n) + [ANY], out_specs=tuple([HBM_SPEC] * (2 * n)),
        input_output_aliases={i: i for i in range(2 * n)},
        compiler_params=pltpu.CompilerParams(has_side_effects=pltpu.SideEffectType.DATAFLOW_SIDE_EFFECTING),
    )(*srcs, *lands, *ssems, *rsems, after)
    return outs[:n], outs[n:]


def _gather_small(small):
    def body(small_ref, small_all, ssend, srecv):
        me = _me()
        my_dev = _chip_of(me) * 2 + me[2]
        small_all[my_dev] = small_ref[...]
        ssends = []
        for k, mask in enumerate(ALL_MASKS):
            cp = pltpu.make_async_remote_copy(
                src_ref=small_ref, dst_ref=small_all.at[my_dev],
                send_sem=ssend.at[k], recv_sem=srecv.at[k], device_id=_peer(mask), device_id_type=MESH)
            cp.start()
            ssends.append(cp)
        for k, mask in enumerate(ALL_MASKS):
            frm = _peer(mask)
            pltpu.make_async_remote_copy(
                src_ref=small_ref, dst_ref=small_all.at[_chip_of(frm) * 2 + frm[2]],
                send_sem=ssend.at[k], recv_sem=srecv.at[k], device_id=frm, device_id_type=MESH).wait_recv()
        for cp in ssends:
            cp.wait_send()

    return pl.pallas_call(
        body, out_shape=jax.ShapeDtypeStruct((N_DEV, SMALL_ROWS, D_MODEL), F32), name="gather_small",
        in_specs=[VMEM_SPEC], out_specs=VMEM_SPEC,
        scratch_shapes=[pltpu.SemaphoreType.DMA((7,)), pltpu.SemaphoreType.DMA((7,))],
    )(small)


def _half_of(ref, axis, half):
    return ref.at[(slice(None),) * axis + (half,)]


def _swap_halves(parts, axes):
    n = len(parts)

    def body(*refs):
        ins, outs, ssem, rsem = refs[:n], refs[n:2 * n], refs[2 * n], refs[2 * n + 1]
        c = lax.axis_index("c")
        cps = [pltpu.make_async_remote_copy(src_ref=_half_of(ins[i], axes[i], 1 - c), dst_ref=outs[i], send_sem=ssem.at[i],
                                            recv_sem=rsem.at[i], device_id=_peer(1), device_id_type=MESH) for i in range(n)]
        for cp in cps:
            cp.start()
        for cp in cps:
            cp.wait()

    shapes = [p.shape[:a] + p.shape[a + 1:] for p, a in zip(parts, axes)]
    return pl.pallas_call(
        body, out_shape=tuple(jax.ShapeDtypeStruct(s, p.dtype) for s, p in zip(shapes, parts)), name="swap_halves",
        in_specs=[ANY] * n, out_specs=tuple([ANY] * n),
        scratch_shapes=[pltpu.SemaphoreType.DMA((n,)), pltpu.SemaphoreType.DMA((n,))],
    )(*parts)


def _presum(mine, sib, c_idx, tag):
    S, _, R, C = mine.shape
    tr = min(R, 256)
    tc = SHARD_IN if C % SHARD_IN == 0 else C

    def body(c_ref, m_ref, s_ref, o_ref, ob_ref):
        total = m_ref[:, 0] + s_ref[...]
        o_ref[...] = total
        ob_ref[...] = total.astype(BF16)

    out_spec = pl.BlockSpec((S, tr, tc), lambda i, j, c_ref: (0, i, j))
    grid_spec = pltpu.PrefetchScalarGridSpec(
        num_scalar_prefetch=1, grid=(R // tr, C // tc),
        in_specs=[pl.BlockSpec((S, 1, tr, tc), lambda i, j, c_ref: (0, c_ref[0], i, j)),
                  pl.BlockSpec((S, tr, tc), lambda i, j, c_ref: (0, i, j))],
        out_specs=(out_spec, out_spec))
    return pl.pallas_call(
        body, out_shape=(jax.ShapeDtypeStruct((S, R, C), F32), jax.ShapeDtypeStruct((S, R, C), BF16)),
        grid_spec=grid_spec, name=f"presum_{tag}", compiler_params=_cp("parallel", "parallel"),
    )(c_idx, mine, sib)


def _assemble_with_sibling(parts, axes):
    n = len(parts)

    def body(*refs):
        outs, ssem, rsem = refs[n:2 * n], refs[2 * n], refs[2 * n + 1]
        c = lax.axis_index("c")
        cps = [pltpu.make_async_remote_copy(
            src_ref=_half_of(outs[i], axes[i], c), dst_ref=_half_of(outs[i], axes[i], c), send_sem=ssem.at[i],
            recv_sem=rsem.at[i], device_id=_peer(1), device_id_type=MESH) for i in range(n)]
        for cp in cps:
            cp.start()
        for i in range(n):
            pltpu.make_async_remote_copy(
                src_ref=_half_of(outs[i], axes[i], c), dst_ref=_half_of(outs[i], axes[i], 1 - c), send_sem=ssem.at[i],
                recv_sem=rsem.at[i], device_id=_peer(1), device_id_type=MESH).wait_recv()
        for cp in cps:
            cp.wait_send()

    return pl.pallas_call(
        body, out_shape=tuple(jax.ShapeDtypeStruct(p.shape, p.dtype) for p in parts), name="assemble_with_sibling",
        in_specs=[ANY] * n, out_specs=tuple([ANY] * n), input_output_aliases={i: i for i in range(n)},
        scratch_shapes=[pltpu.SemaphoreType.DMA((n,)), pltpu.SemaphoreType.DMA((n,))],
    )(*parts)


def _rope_tables(pos_col):
    T = pos_col.shape[0]
    tm = min(T, 512)
    inv = np.float32(ROPE_THETA) ** (-(np.arange(0, 2 * ROT_HALF, 2, dtype=np.float32)) / np.float32(2 * ROT_HALF))
    lane = np.arange(128) % HEAD_DIM
    freq = np.where(lane < 2 * ROT_HALF, inv[lane % ROT_HALF], 0.0).astype(np.float32)[None, :]

    def body(pos_ref, f_ref, c_ref, sa_ref, sb_ref):
        ang = pos_ref[...].astype(F32) * f_ref[...]
        c, s = jnp.cos(ang), jnp.sin(ang)
        m = lax.broadcasted_iota(jnp.int32, ang.shape, 1) & (HEAD_DIM - 1)
        c_ref[...] = jnp.where(m < 2 * ROT_HALF, c, 1.0)
        sa_ref[...] = jnp.where(m < ROT_HALF, -s, 0.0)
        sb_ref[...] = jnp.where((m >= ROT_HALF) & (m < 2 * ROT_HALF), s, 0.0)

    tab = jax.ShapeDtypeStruct((T, 128), F32)
    return pl.pallas_call(
        body, out_shape=(tab, tab, tab), grid=(T // tm,), name="rope_tables",
        in_specs=[pl.BlockSpec((tm, 1), lambda i: (i, 0)), pl.BlockSpec((1, 128), lambda i: (0, 0))],
        out_specs=tuple(pl.BlockSpec((tm, 128), lambda i: (i, 0)) for _ in range(3)),
        compiler_params=_cp("parallel"),
    )(pos_col, jnp.asarray(freq))


def _wide(tab, width):
    return jnp.concatenate([tab] * (width // 128), axis=1)


def _rope(t, c, sa, sb):
    w = t.shape[-1]
    return t * c + pltpu.roll(t, w - ROT_HALF, 1) * sa + pltpu.roll(t, ROT_HALF, 1) * sb


def _unrope(d, c, sa, sb):
    w = d.shape[-1]
    return d * c + pltpu.roll(d * sa, ROT_HALF, 1) + pltpu.roll(d * sb, w - ROT_HALF, 1)


def _prenorm(x, mod_row, norm_g):
    T = x.shape[0]
    tm = min(T, 512)

    def body(x_ref, mod_ref, g_ref, h_ref):
        xf = x_ref[...]
        shift, scale = mod_ref[:, 0:D_MODEL], mod_ref[:, D_MODEL:2 * D_MODEL]
        h = (xf * _rms(xf)) * g_ref[...] * (1.0 + scale) + shift
        h_ref[...] = h.astype(BF16)

    return pl.pallas_call(
        body, out_shape=jax.ShapeDtypeStruct((T, D_MODEL), BF16), grid=(T // tm,), name="prenorm",
        in_specs=[pl.BlockSpec((tm, D_MODEL), lambda i: (i, 0)), pl.BlockSpec((1, ADA_W), lambda i: (0, 0)),
                  pl.BlockSpec((1, D_MODEL), lambda i: (0, 0))],
        out_specs=pl.BlockSpec((tm, D_MODEL), lambda i: (i, 0)),
        compiler_params=_cp("parallel"),
    )(x, mod_row, norm_g)


def _in_projection(h, w_in):
    T = h.shape[0]
    tm, tn = min(T, 512), SHARD_IN

    def body(h_ref, w_ref, o_ref):
        o_ref[...] = _dot(h_ref[...], w_ref[...])

    return pl.pallas_call(
        body, out_shape=jax.ShapeDtypeStruct((T, IN_W), F32), grid=(IN_W // tn, T // tm), name="in_projection",
        in_specs=[pl.BlockSpec((tm, D_MODEL), lambda j, i: (i, 0)), pl.BlockSpec((D_MODEL, tn), lambda j, i: (0, j))],
        out_specs=pl.BlockSpec((tm, tn), lambda j, i: (i, j)),
        compiler_params=_cp("parallel", "parallel"),
    )(h, w_in)


def _attn_mask(n):
    qi = lax.broadcasted_iota(jnp.int32, (GROUP * BLOCK, 2 * BLOCK), 0) & (BLOCK - 1)
    kj = lax.broadcasted_iota(jnp.int32, (GROUP * BLOCK, 2 * BLOCK), 1)
    diff = qi + BLOCK - kj
    return (diff >= 0) & (diff < BLOCK) & ((kj >= BLOCK) | (n > 0))


def _sink_col(sink_ref, kh):
    rowg = lax.broadcasted_iota(jnp.int32, (GROUP * BLOCK, 1), 0) // BLOCK
    col = jnp.full((GROUP * BLOCK, 1), sink_ref[0, GROUP * kh], F32)
    for g in range(1, GROUP):
        col = jnp.where(rowg == g, sink_ref[0, GROUP * kh + g], col)
    return col


def _attn_probs(qr, kr_prev, kr_cur, v_prev, v_cur, kh, sink_col, mask):
    heads = [qr[:, HEAD_DIM * (GROUP * kh + g): HEAD_DIM * (GROUP * kh + g + 1)] for g in range(GROUP)]
    qs = jnp.concatenate(heads, axis=0).astype(BF16)
    lo, hi = HEAD_DIM * kh, HEAD_DIM * (kh + 1)
    kk = jnp.concatenate([kr_prev[:, lo:hi], kr_cur[:, lo:hi]], axis=0).astype(BF16)
    vv = jnp.concatenate([v_prev[:, lo:hi], v_cur[:, lo:hi]], axis=0).astype(BF16)
    s = _dot_nt(qs, kk) * (1.0 / 8.0)
    s = jnp.where(mask, s, -1e30)
    m = jnp.maximum(jnp.max(s, axis=-1, keepdims=True), sink_col)
    p = jnp.exp(s - m)
    p_sink = jnp.exp(sink_col - m)
    denom = jnp.sum(p, axis=-1, keepdims=True) + p_sink
    return qs, kk, vv, p / denom, p_sink / denom


def _unstack_heads(parts):
    cols = []
    for kh in range(N_KV):
        for g in range(GROUP):
            cols.append(parts[kh][g * BLOCK:(g + 1) * BLOCK, :])
    return jnp.concatenate(cols, axis=1)


def _attn_forward(proj, tabs, sinks):
    T = proj.shape[0]
    nb = T // BLOCK

    def body(q_ref, kvc_ref, kvp_ref, g0_ref, g1_ref, cc, sac, sbc, cp_, sap, sbp, sink_ref, y_ref):
        n = pl.program_id(0)
        tc = (_wide(cc[...], D_MODEL), _wide(sac[...], D_MODEL), _wide(sbc[...], D_MODEL))
        tcur = tuple(t[:, :KV_W] for t in tc)
        tprev = (_wide(cp_[...], KV_W), _wide(sap[...], KV_W), _wide(sbp[...], KV_W))
        qr = _rope(q_ref[...], *tc)
        kr_cur = _rope(kvc_ref[:, 0:KV_W], *tcur)
        kr_prev = _rope(kvp_ref[:, 0:KV_W], *tprev)
        v_cur, v_prev = kvc_ref[:, KV_W:2 * KV_W], kvp_ref[:, KV_W:2 * KV_W]
        mask = _attn_mask(n)
        outs = []
        for kh in range(N_KV):
            _, _, vv, pn, _ = _attn_probs(qr, kr_prev, kr_cur, v_prev, v_cur, kh, _sink_col(sink_ref, kh), mask)
            outs.append(_dot(pn.astype(BF16), vv))
        o = _unstack_heads(outs)
        g = jnp.concatenate([g0_ref[...], g1_ref[...]], axis=1)
        y_ref[...] = (o * (g * _sigmoid(g))).astype(BF16)

    def blk(w, cb):
        return pl.BlockSpec((BLOCK, w), lambda n, cb=cb: (n, cb))

    prev = lambda w, cb: pl.BlockSpec((BLOCK, w), lambda n, cb=cb: (jnp.maximum(n - 1, 0), cb))
    return pl.pallas_call(
        body, out_shape=jax.ShapeDtypeStruct((T, D_MODEL), BF16), grid=(nb,), name="attn_forward",
        in_specs=[blk(D_MODEL, 0), blk(CB, CB_KV), prev(CB, CB_KV), blk(CB, CB_GA), blk(CB, CB_GA + 1),
                  blk(128, 0), blk(128, 0), blk(128, 0), prev(128, 0), prev(128, 0), prev(128, 0),
                  pl.BlockSpec(memory_space=pltpu.SMEM)],
        out_specs=pl.BlockSpec((BLOCK, D_MODEL), lambda n: (n, 0)),
        compiler_params=_cp("parallel"),
    )(proj, proj, proj, proj, proj, *tabs, *tabs, sinks)


def _scan_rows8():
    return lax.broadcasted_iota(jnp.int32, (8, D_MODEL), 0)


def _scan_forward(a_ref, b_ref, h_ref, carry, rows):
    row = _scan_rows8()

    def group(i, carry):
        off = pl.multiple_of(i * 8, 8)
        a, b = a_ref[pl.ds(off, 8), :], b_ref[pl.ds(off, 8), :]
        for d in (1, 2, 4):
            ok = row >= d
            b = jnp.where(ok, a * pltpu.roll(b, d, 0) + b, b)
            a = jnp.where(ok, a * pltpu.roll(a, d, 0), a)
        h = a * carry + b
        h_ref[pl.ds(off, 8), :] = h
        return h[7:8, :]

    return lax.fori_loop(0, rows // 8, group, carry)


def _scan_backward(a_ref, g_ref, lam_ref, carry, rows):
    row = _scan_rows8()

    def group(i, carry):
        off = pl.multiple_of((rows // 8 - 1 - i) * 8, 8)
        a, g = a_ref[pl.ds(off, 8), :], g_ref[pl.ds(off, 8), :]
        b = a * g
        for d in (1, 2, 4):
            ok = row < 8 - d
            b = jnp.where(ok, a * pltpu.roll(b, 8 - d, 0) + b, b)
            a = jnp.where(ok, a * pltpu.roll(a, 8 - d, 0), a)
        mu = a * carry + b
        mu_below = jnp.where(row == 7, carry, pltpu.roll(mu, 7, 0))
        lam_ref[pl.ds(off, 8), :] = g + mu_below
        return mu[0:1, :]

    return lax.fori_loop(0, rows // 8, group, carry)


def _rnn_recompute(xbuf, xr, tail, cw, cb, wa_ref, wx_ref, ba, bx, sp, reset):
    rows = xr.shape[0]
    xbuf[0:8, :] = tail
    xbuf[8:rows + 8, :] = xr
    xs = [xbuf[pl.ds(8 - (CONV_W - 1 - k), rows), :] for k in range(CONV_W - 1)] + [xr]
    xc = xs[0] * cw[0:1, :]
    for k in range(1, CONV_W):
        xc = xc + xs[k] * cw[k:k + 1, :]
    xc = xc + cb
    xcb = xc.astype(BF16)
    za = jnp.concatenate([_dot(xcb[:, RNN_BW * j:RNN_BW * (j + 1)], wa_ref[j]) for j in range(RNN_BLOCKS)], axis=1) + ba
    zx = jnp.concatenate([_dot(xcb[:, RNN_BW * j:RNN_BW * (j + 1)], wx_ref[j]) for j in range(RNN_BLOCKS)], axis=1) + bx
    r, i = _sigmoid(za), _sigmoid(zx)
    log_a = -LRU_C * r * sp
    a_raw = jnp.exp(log_a)
    mult_raw = jnp.sqrt(_neg_expm1(2.0 * log_a))
    a = jnp.where(reset, 0.0, a_raw)
    mult = jnp.where(reset, 1.0, mult_raw)
    return xs, xc, xcb, r, i, a_raw, mult_raw, a, mult


def _rnn_forward(proj, pos_col, conv_w, conv_b, rwa, rwx, ba, bx, lam):
    T = proj.shape[0]
    tr = min(T, 256)

    def body(x0, x1, g0, g1, pos_ref, cw_ref, cb_ref, wa_ref, wx_ref, ba_ref, bx_ref, lam_ref,
             y_ref, h_ref, xbuf, abuf, bbuf, tail, carry):
        t = pl.program_id(0)

        @pl.when(t == 0)
        def _():
            tail[...] = jnp.zeros_like(tail)
            carry[...] = jnp.zeros_like(carry)

        xr = jnp.concatenate([x0[...], x1[...]], axis=1)
        sp = _softplus(-lam_ref[...])
        reset = pos_ref[...] == 0
        _, xc, _, _, i, _, _, a, mult = _rnn_recompute(
            xbuf, xr, tail[...], cw_ref[...], cb_ref[...], wa_ref, wx_ref, ba_ref[...], bx_ref[...], sp, reset)
        abuf[...] = a
        bbuf[...] = mult * (i * xc)
        last = _scan_forward(abuf, bbuf, h_ref, carry[0:1, :], tr)
        carry[...] = jnp.broadcast_to(last, carry.shape)
        tail[...] = xr[tr - 8:tr, :]
        g = jnp.concatenate([g0[...], g1[...]], axis=1)
        y_ref[...] = (h_ref[...] * (g * _sigmoid(g))).astype(BF16)

    blk = lambda cb: pl.BlockSpec((tr, CB), lambda t, cb=cb: (t, cb))
    row = lambda w: pl.BlockSpec((1, w), lambda t: (0, 0))
    full3 = pl.BlockSpec((RNN_BLOCKS, RNN_BW, RNN_BW), lambda t: (0, 0, 0))
    return pl.pallas_call(
        body, out_shape=(jax.ShapeDtypeStruct((T, D_MODEL), BF16), jax.ShapeDtypeStruct((T, D_MODEL), F32)),
        grid=(T // tr,), name="rnn_forward",
        in_specs=[blk(CB_XR), blk(CB_XR + 1), blk(CB_GR), blk(CB_GR + 1), pl.BlockSpec((tr, 1), lambda t: (t, 0)),
                  pl.BlockSpec((CONV_W, D_MODEL), lambda t: (0, 0)), row(D_MODEL), full3, full3,
                  row(D_MODEL), row(D_MODEL), row(D_MODEL)],
        out_specs=(pl.BlockSpec((tr, D_MODEL), lambda t: (t, 0)), pl.BlockSpec((tr, D_MODEL), lambda t: (t, 0))),
        scratch_shapes=[pltpu.VMEM((tr + 8, D_MODEL), F32), pltpu.VMEM((tr, D_MODEL), F32), pltpu.VMEM((tr, D_MODEL), F32),
                        pltpu.VMEM((8, D_MODEL), F32), pltpu.VMEM((8, D_MODEL), F32)],
        compiler_params=_cp("arbitrary"),
    )(proj, proj, proj, proj, pos_col, conv_w, conv_b, rwa, rwx, ba, bx, lam)


def _merge_and_head(x, target, y_attn, y_rnn, proj, wap, wrp, wo, mod_row, final_g):
    T = x.shape[0]
    tm = min(T, 256)

    def body(x_ref, t_ref, ya_ref, yr_ref, ma0, ma1, mr0, mr1, wap_ref, wrp_ref, wo_ref, mod_ref, fg_ref,
             dx2_ref, mg_ref, do_ref, dpa_ref, dpr_ref, dya_ref, dyr_ref, dc_ref, dfg_ref, dgate_ref, loss_ref):
        i = pl.program_id(0)
        gate = mod_ref[:, 2 * D_MODEL:3 * D_MODEL]
        ya, yr = ya_ref[...], yr_ref[...]
        pa, pr = _dot(ya, wap_ref[...]), _dot(yr, wrp_ref[...])
        sa = _sigmoid(jnp.concatenate([ma0[...], ma1[...]], axis=1))
        sr = _sigmoid(jnp.concatenate([mr0[...], mr1[...]], axis=1))
        merged = sa * pa + sr * pr
        mb = merged.astype(BF16)
        o = _dot(mb, wo_ref[...])
        x2 = x_ref[...] + gate * o
        r2 = _rms(x2)
        xn2 = x2 * r2
        fg = fg_ref[...]
        err = xn2 * fg - t_ref[...]
        loss_t = 0.5 * jnp.sum(jnp.sum(err * err, axis=-1, keepdims=True) * (1.0 / D_MODEL), axis=0, keepdims=True)
        dy = err * (1.0 / D_MODEL)
        dfg_t = jnp.sum(dy * xn2, axis=0, keepdims=True)
        dxn = dy * fg
        dx2 = r2 * (dxn - xn2 * jnp.mean(dxn * xn2, axis=-1, keepdims=True))
        dgate_t = jnp.sum(dx2 * o, axis=0, keepdims=True)
        dob = (dx2 * gate).astype(BF16)
        dmerged = _dot_nt(dob, wo_ref[...])
        dpa = (dmerged * sa).astype(BF16)
        dpr = (dmerged * sr).astype(BF16)
        dx2_ref[...] = dx2
        mg_ref[...] = mb
        do_ref[...] = dob
        dpa_ref[...] = dpa
        dpr_ref[...] = dpr
        dya_ref[...] = _dot_nt(dpa, wap_ref[...])
        dyr_ref[...] = _dot_nt(dpr, wrp_ref[...])
        dc_ref[:, 0:D_MODEL] = (dmerged * pa * sa * (1.0 - sa)).astype(BF16)
        dc_ref[:, D_MODEL:2 * D_MODEL] = (dmerged * pr * sr * (1.0 - sr)).astype(BF16)

        @pl.when(i == 0)
        def _():
            dfg_ref[...] = jnp.zeros_like(dfg_ref)
            dgate_ref[...] = jnp.zeros_like(dgate_ref)
            loss_ref[...] = jnp.zeros_like(loss_ref)

        dfg_ref[...] += dfg_t
        dgate_ref[...] += dgate_t
        loss_ref[...] += jnp.broadcast_to(loss_t, loss_ref.shape)

    tok = lambda w: pl.BlockSpec((tm, w), lambda i: (i, 0))
    blk = lambda cb: pl.BlockSpec((tm, CB), lambda i, cb=cb: (i, cb))
    wfull = pl.BlockSpec((D_MODEL, D_MODEL), lambda i: (0, 0))
    row = lambda w: pl.BlockSpec((1, w), lambda i: (0, 0))
    out_shape = (
        jax.ShapeDtypeStruct((T, D_MODEL), F32), jax.ShapeDtypeStruct((T, D_MODEL), BF16),
        jax.ShapeDtypeStruct((T, D_MODEL), BF16), jax.ShapeDtypeStruct((T, D_MODEL), BF16),
        jax.ShapeDtypeStruct((T, D_MODEL), BF16), jax.ShapeDtypeStruct((T, D_MODEL), F32),
        jax.ShapeDtypeStruct((T, D_MODEL), F32), jax.ShapeDtypeStruct((T, 2 * D_MODEL), BF16),
        jax.ShapeDtypeStruct((1, D_MODEL), F32), jax.ShapeDtypeStruct((1, D_MODEL), F32),
        jax.ShapeDtypeStruct((1, 128), F32),
    )
    return pl.pallas_call(
        body, out_shape=out_shape, grid=(T // tm,), name="merge_and_head",
        in_specs=[tok(D_MODEL), tok(D_MODEL), tok(D_MODEL), tok(D_MODEL), blk(CB_MA), blk(CB_MA + 1), blk(CB_MR),
                  blk(CB_MR + 1), wfull, wfull, wfull, row(ADA_W), row(D_MODEL)],
        out_specs=(tok(D_MODEL),) * 7 + (tok(2 * D_MODEL), row(D_MODEL), row(D_MODEL), row(128)),
        compiler_params=_cp("arbitrary"),
    )(x, target, y_attn, y_rnn, proj, proj, proj, proj, wap, wrp, wo, mod_row, final_g)


def _attn_backward(proj, d_y, tabs, sinks):
    T = proj.shape[0]
    nb = T // BLOCK

    def body(q_ref, kvc_ref, kvp_ref, g0_ref, g1_ref, dy_ref, cc, sac, sbc, cp_, sap, sbp, sink_ref,
             dq_ref, dkv_ref, dg_ref, dsink_ref, carry):
        n = pl.program_id(0)

        @pl.when(n == 0)
        def _():
            carry[...] = jnp.zeros_like(carry)
            dsink_ref[...] = jnp.zeros_like(dsink_ref)

        @pl.when(n < nb)
        def _():
            tc = (_wide(cc[...], D_MODEL), _wide(sac[...], D_MODEL), _wide(sbc[...], D_MODEL))
            tcur = tuple(t[:, :KV_W] for t in tc)
            tprev = (_wide(cp_[...], KV_W), _wide(sap[...], KV_W), _wide(sbp[...], KV_W))
            qr = _rope(q_ref[...], *tc)
            kr_cur = _rope(kvc_ref[:, 0:KV_W], *tcur)
            kr_prev = _rope(kvp_ref[:, 0:KV_W], *tprev)
            v_cur, v_prev = kvc_ref[:, KV_W:2 * KV_W], kvp_ref[:, KV_W:2 * KV_W]
            g = jnp.concatenate([g0_ref[...], g1_ref[...]], axis=1)
            sg = _sigmoid(g)
            dy = dy_ref[...]
            d_o = dy * (g * sg)
            mask = _attn_mask(n)
            lane = lax.broadcasted_iota(jnp.int32, (1, 128), 1)
            rowg = lax.broadcasted_iota(jnp.int32, (GROUP * BLOCK, 1), 0) // BLOCK
            o_parts, dq_parts, dk_parts, dv_parts = [], [], [], []
            dsink = jnp.zeros((1, 128), F32)
            for kh in range(N_KV):
                qs, kk, vv, pn, pn_sink = _attn_probs(qr, kr_prev, kr_cur, v_prev, v_cur, kh, _sink_col(sink_ref, kh), mask)
                pnb = pn.astype(BF16)
                o_parts.append(_dot(pnb, vv))
                dos = jnp.concatenate(
                    [d_o[:, HEAD_DIM * (GROUP * kh + gq): HEAD_DIM * (GROUP * kh + gq + 1)] for gq in range(GROUP)],
                    axis=0).astype(BF16)
                dpn = _dot_nt(dos, vv)
                delta = jnp.sum(pn * dpn, axis=-1, keepdims=True)
                dsb = (pn * (dpn - delta) * (1.0 / 8.0)).astype(BF16)
                dq_parts.append(_dot(dsb, kk))
                dk_parts.append(_dot_tn(dsb, qs))
                dv_parts.append(_dot_tn(pnb, dos))
                ds_rows = pn_sink * delta
                for gq in range(GROUP):
                    val = -jnp.sum(jnp.where(rowg == gq, ds_rows, 0.0), axis=0, keepdims=True)
                    dsink = dsink + jnp.where(lane == GROUP * kh + gq, val, 0.0)
            o = _unstack_heads(o_parts)
            dg_ref[...] = (dy * o * (sg * (1.0 + g * (1.0 - sg)))).astype(BF16)
            dq_ref[...] = _unrope(_unstack_heads(dq_parts), *tc).astype(BF16)
            dk_prev = _unrope(jnp.concatenate([p[0:BLOCK, :] for p in dk_parts], axis=1), *tprev)
            dk_cur = _unrope(jnp.concatenate([p[BLOCK:2 * BLOCK, :] for p in dk_parts], axis=1), *tcur)
            dv_prev = jnp.concatenate([p[0:BLOCK, :] for p in dv_parts], axis=1)
            dv_cur = jnp.concatenate([p[BLOCK:2 * BLOCK, :] for p in dv_parts], axis=1)
            dkv_ref[...] = (carry[...] + jnp.concatenate([dk_prev, dv_prev], axis=1)).astype(BF16)
            carry[...] = jnp.concatenate([dk_cur, dv_cur], axis=1)
            dsink_ref[...] += dsink

        @pl.when(n == nb)
        def _():
            dkv_ref[...] = carry[...].astype(BF16)

    cur = lambda w, cb: pl.BlockSpec((BLOCK, w), lambda n, cb=cb: (jnp.minimum(n, nb - 1), cb))
    prev = lambda w, cb: pl.BlockSpec((BLOCK, w), lambda n, cb=cb: (jnp.maximum(jnp.minimum(n, nb - 1) - 1, 0), cb))
    out_shape = (jax.ShapeDtypeStruct((T, D_MODEL), BF16), jax.ShapeDtypeStruct((T, 2 * KV_W), BF16),
                 jax.ShapeDtypeStruct((T, D_MODEL), BF16), jax.ShapeDtypeStruct((1, 128), F32))
    return pl.pallas_call(
        body, out_shape=out_shape, grid=(nb + 1,), name="attn_backward",
        in_specs=[cur(D_MODEL, 0), cur(CB, CB_KV), prev(CB, CB_KV), cur(CB, CB_GA), cur(CB, CB_GA + 1), cur(D_MODEL, 0),
                  cur(128, 0), cur(128, 0), cur(128, 0), prev(128, 0), prev(128, 0), prev(128, 0),
                  pl.BlockSpec(memory_space=pltpu.SMEM)],
        out_specs=(cur(D_MODEL, 0), pl.BlockSpec((BLOCK, 2 * KV_W), lambda n: (jnp.maximum(n - 1, 0), 0)),
                   cur(D_MODEL, 0), pl.BlockSpec((1, 128), lambda n: (0, 0))),
        scratch_shapes=[pltpu.VMEM((BLOCK, 2 * KV_W), F32)],
        compiler_params=_cp("arbitrary"),
    )(proj, proj, proj, proj, proj, d_y, *tabs, *tabs, sinks)


def _rnn_backward(proj, pos_col, h_rnn, d_y, conv_w, conv_b, rwa, rwx, ba, bx, lam):
    T = proj.shape[0]
    tr = min(T, 256)
    nt = T // tr
    hb = tr // 8

    def body(x0, x1, xh0, xh1, g0, g1, pos_ref, h_ref, hh_ref, dy_ref, cw_ref, cb_ref, wa_ref, wx_ref, ba_ref, bx_ref,
             lam_ref, db_ref, dcw_ref, dcb_ref, dwa_ref, dwx_ref, dba_ref, dbx_ref, dlam_ref,
             xbuf, hbuf, dbuf, abuf, gbuf, lbuf, mu_carry, dxc_head):
        step = pl.program_id(0)
        first_tile = step == nt - 1

        @pl.when(step == 0)
        def _():
            mu_carry[...] = jnp.zeros_like(mu_carry)
            dxc_head[...] = jnp.zeros_like(dxc_head)
            for ref in (dcw_ref, dcb_ref, dwa_ref, dwx_ref, dba_ref, dbx_ref, dlam_ref):
                ref[...] = jnp.zeros_like(ref)

        xr = jnp.concatenate([x0[...], x1[...]], axis=1)
        tail = jnp.where(first_tile, 0.0, jnp.concatenate([xh0[...], xh1[...]], axis=1))
        lam_v = lam_ref[...]
        sp = _softplus(-lam_v)
        reset = pos_ref[...] == 0
        cw = cw_ref[...]
        xs, xc, xcb, r, i, a_raw, mult_raw, a, mult = _rnn_recompute(
            xbuf, xr, tail, cw, cb_ref[...], wa_ref, wx_ref, ba_ref[...], bx_ref[...], sp, reset)
        g = jnp.concatenate([g0[...], g1[...]], axis=1)
        sg = _sigmoid(g)
        dy = dy_ref[...]
        h = h_ref[...]
        d_g = dy * h * (sg * (1.0 + g * (1.0 - sg)))
        abuf[...] = a
        gbuf[...] = dy * (g * sg)
        top = _scan_backward(abuf, gbuf, lbuf, mu_carry[0:1, :], tr)
        mu_carry[...] = jnp.broadcast_to(top, mu_carry.shape)
        lam_t = lbuf[...]
        hbuf[0:8, :] = jnp.where(first_tile, 0.0, hh_ref[...])
        hbuf[8:tr + 8, :] = h
        h_prev = hbuf[pl.ds(7, tr), :]
        live = jnp.logical_not(reset)
        d_a = jnp.where(live, lam_t * h_prev, 0.0)
        d_mult = jnp.where(live, lam_t * (i * xc), 0.0)
        d_ixc = lam_t * mult
        d_i = d_ixc * xc
        d_xc = d_ixc * i
        d_log_a = d_a * a_raw - d_mult * (a_raw * a_raw / mult_raw)
        d_log_a = jnp.where(live, d_log_a, 0.0)
        d_za = d_log_a * (-LRU_C * sp) * (r * (1.0 - r))
        d_zx = d_i * (i * (1.0 - i))
        dlam_ref[...] += jnp.sum(d_log_a * r, axis=0, keepdims=True) * (LRU_C * _sigmoid(-lam_v))
        dba_ref[...] += jnp.sum(d_za, axis=0, keepdims=True)
        dbx_ref[...] += jnp.sum(d_zx, axis=0, keepdims=True)
        dzab, dzxb = d_za.astype(BF16), d_zx.astype(BF16)
        back = []
        for j in range(RNN_BLOCKS):
            sl = slice(RNN_BW * j, RNN_BW * (j + 1))
            dwa_ref[j] += _dot_tn(xcb[:, sl], dzab[:, sl])
            dwx_ref[j] += _dot_tn(xcb[:, sl], dzxb[:, sl])
            back.append(_dot_nt(dzab[:, sl], wa_ref[j]) + _dot_nt(dzxb[:, sl], wx_ref[j]))
        d_xc = d_xc + jnp.concatenate(back, axis=1)
        dcb_ref[...] += jnp.sum(d_xc, axis=0, keepdims=True)
        for k in range(CONV_W):
            dcw_ref[k:k + 1, :] += jnp.sum(d_xc * xs[k], axis=0, keepdims=True)
        dbuf[0:tr, :] = d_xc
        dbuf[tr:tr + 8, :] = dxc_head[...]
        d_xr = d_xc * cw[CONV_W - 1:CONV_W, :]
        for k in range(CONV_W - 1):
            d_xr = d_xr + dbuf[pl.ds(CONV_W - 1 - k, tr), :] * cw[k:k + 1, :]
        dxc_head[...] = d_xc[0:8, :]
        db_ref[:, 0:D_MODEL] = d_xr.astype(BF16)
        db_ref[:, D_MODEL:2 * D_MODEL] = d_g.astype(BF16)

    rev = lambda s: nt - 1 - s
    blk = lambda cb: pl.BlockSpec((tr, CB), lambda s, cb=cb: (rev(s), cb))
    halo = lambda w, cb: pl.BlockSpec((8, w), lambda s, cb=cb: (jnp.maximum(rev(s) * hb - 1, 0), cb))
    tok = lambda w: pl.BlockSpec((tr, w), lambda s: (rev(s), 0))
    row = lambda w: pl.BlockSpec((1, w), lambda s: (0, 0))
    full3 = pl.BlockSpec((RNN_BLOCKS, RNN_BW, RNN_BW), lambda s: (0, 0, 0))
    cwspec = pl.BlockSpec((CONV_W, D_MODEL), lambda s: (0, 0))
    vec = jax.ShapeDtypeStruct((1, D_MODEL), F32)
    gate_w = jax.ShapeDtypeStruct((RNN_BLOCKS, RNN_BW, RNN_BW), F32)
    out_shape = (jax.ShapeDtypeStruct((T, 2 * D_MODEL), BF16), jax.ShapeDtypeStruct((CONV_W, D_MODEL), F32), vec,
                 gate_w, gate_w, vec, vec, vec)
    big = lambda: pltpu.VMEM((tr, D_MODEL), F32)
    ext = lambda: pltpu.VMEM((tr + 8, D_MODEL), F32)
    return pl.pallas_call(
        body, out_shape=out_shape, grid=(nt,), name="rnn_backward",
        in_specs=[blk(CB_XR), blk(CB_XR + 1), halo(CB, CB_XR), halo(CB, CB_XR + 1), blk(CB_GR), blk(CB_GR + 1),
                  pl.BlockSpec((tr, 1), lambda s: (rev(s), 0)), tok(D_MODEL), halo(D_MODEL, 0), tok(D_MODEL),
                  cwspec, row(D_MODEL), full3, full3, row(D_MODEL), row(D_MODEL), row(D_MODEL)],
        out_specs=(tok(2 * D_MODEL), cwspec, row(D_MODEL), full3, full3, row(D_MODEL), row(D_MODEL), row(D_MODEL)),
        scratch_shapes=[ext(), ext(), ext(), big(), big(), big(), pltpu.VMEM((8, D_MODEL), F32), pltpu.VMEM((8, D_MODEL), F32)],
        compiler_params=_cp("arbitrary"),
    )(proj, proj, proj, proj, proj, proj, pos_col, h_rnn, h_rnn, d_y, conv_w, conv_b, rwa, rwx, ba, bx, lam)


def _input_backward(pieces, w_in, x, dx2, mod_row, norm_g):
    T = x.shape[0]
    tm = min(T, 256)
    n = len(pieces)

    def body(*refs):
        d_refs = refs[:n]
        w_ref, x_ref, dx2_ref, mod_ref, g_ref, gx_ref, dshift_ref, dscale_ref, dg_ref = refs[n:]
        i = pl.program_id(0)
        dh = None
        for d_ref, (_, start, count) in zip(d_refs, pieces):
            part = _dot_nt(d_ref[...], w_ref[:, start * CB:(start + count) * CB])
            dh = part if dh is None else dh + part

        @pl.when(i == 0)
        def _():
            dshift_ref[...] = jnp.zeros_like(dshift_ref)
            dscale_ref[...] = jnp.zeros_like(dscale_ref)
            dg_ref[...] = jnp.zeros_like(dg_ref)

        xf = x_ref[...]
        r1 = _rms(xf)
        xn = xf * r1
        gn = g_ref[...]
        s1 = 1.0 + mod_ref[:, D_MODEL:2 * D_MODEL]
        dshift_ref[...] += jnp.sum(dh, axis=0, keepdims=True)
        dscale_ref[...] += jnp.sum(dh * (xn * gn), axis=0, keepdims=True)
        dg_ref[...] += jnp.sum(dh * s1 * xn, axis=0, keepdims=True)
        dxn = dh * s1 * gn
        gx_ref[...] = dx2_ref[...] + r1 * (dxn - xn * jnp.mean(dxn * xn, axis=-1, keepdims=True))

    tok = lambda w: pl.BlockSpec((tm, w), lambda i: (i, 0))
    row = lambda w: pl.BlockSpec((1, w), lambda i: (0, 0))
    vec = jax.ShapeDtypeStruct((1, D_MODEL), F32)
    return pl.pallas_call(
        body, out_shape=(jax.ShapeDtypeStruct((T, D_MODEL), F32), vec, vec, vec), grid=(T // tm,), name="input_backward",
        in_specs=[tok(c * CB) for _, _, c in pieces]
        + [pl.BlockSpec((D_MODEL, IN_W), lambda i: (0, 0), pipeline_mode=pl.Buffered(1)), tok(D_MODEL), tok(D_MODEL),
           row(ADA_W), row(D_MODEL)],
        out_specs=(tok(D_MODEL), row(D_MODEL), row(D_MODEL), row(D_MODEL)),
        compiler_params=_cp("arbitrary"),
    )(*[p[0] for p in pieces], w_in, x, dx2, mod_row, norm_g)


def _weight_grad(a, pieces, tag):
    T, M = a.shape
    n_blocks = sum(count for _, _, count in pieces)
    n = len(pieces)

    def body(*refs):
        a_ref, b_refs, o_ref = refs[0], refs[1:1 + n], refs[-1]
        j = pl.program_id(0)
        for b_ref, (_, start, count) in zip(b_refs, pieces):
            @pl.when((j >= start) & (j < start + count))
            def _(b_ref=b_ref):
                o_ref[...] = _dot_tn(a_ref[...], b_ref[...])

    def piece_spec(start, count):
        return pl.BlockSpec((T, CB), lambda j: (0, jnp.clip(j - start, 0, count - 1)))

    return pl.pallas_call(
        body, out_shape=jax.ShapeDtypeStruct((M, n_blocks * CB), F32), grid=(n_blocks,), name=f"weight_grad_{tag}",
        in_specs=[pl.BlockSpec((T, M), lambda j: (0, 0), pipeline_mode=pl.Buffered(1))] + [piece_spec(s, c) for _, s, c in pieces],
        out_specs=pl.BlockSpec((M, CB), lambda j: (0, j)), compiler_params=_cp("arbitrary"),
    )(a, *[p[0] for p in pieces])


def _adamw(w, g, m, v):
    m = ADAM_B1 * m + (1.0 - ADAM_B1) * g
    v = ADAM_B2 * v + (1.0 - ADAM_B2) * (g * g)
    m_hat = m / (1.0 - ADAM_B1 ** ADAM_STEP)
    v_hat = v / (1.0 - ADAM_B2 ** ADAM_STEP)
    delta = -ADAM_LR * (m_hat / (jnp.sqrt(v_hat) + ADAM_EPS) + ADAM_WD * w)
    return delta, m, v


def _sum_landed(kind, own, land, where, tag):
    if kind == "in":
        R, C = land.shape[1:]
        tr = 256
        grid = (R // tr,)
        own_spec = pl.BlockSpec((tr, C), lambda i, w: (i, w[0]))
        land_spec = pl.BlockSpec((3, tr, C), lambda i, w: (0, i, 0))
        out_spec = pl.BlockSpec((1, tr, C), lambda i, w: (w[1], i, 0))
        out_shape = (2, R, C)
        pick = lambda ref: ref[...]
    elif kind == "sq":
        R, C = land.shape[1:]
        grid = (1,)
        own_spec = pl.BlockSpec((1, R, C), lambda i, w: (w[0], 0, 0))
        land_spec = pl.BlockSpec((3, R, C), lambda i, w: (0, 0, 0))
        out_spec = pl.BlockSpec((1, R, C), lambda i, w: (w[1], 0, 0))
        out_shape = (2, R, C)
        pick = lambda ref: ref[0]
    else:
        B, R, C = land.shape[1:]
        grid = (1,)
        own_spec = pl.BlockSpec((B, 1, R, C), lambda i, w: (0, w[0], 0, 0))
        land_spec = pl.BlockSpec((3, B, R, C), lambda i, w: (0, 0, 0, 0))
        out_spec = pl.BlockSpec((B, 1, R, C), lambda i, w: (0, w[1], 0, 0))
        out_shape = (B, 2, R, C)
        pick = lambda ref: ref[:, 0]

    def body(w_ref, own_ref, l_ref, o_ref):
        total = ((pick(own_ref) + l_ref[0].astype(F32)) + l_ref[1].astype(F32)) + l_ref[2].astype(F32)
        if kind == "in":
            o_ref[0] = total
        elif kind == "sq":
            o_ref[0] = total
        else:
            o_ref[:, 0] = total

    grid_spec = pltpu.PrefetchScalarGridSpec(num_scalar_prefetch=1, grid=grid, in_specs=[own_spec, land_spec], out_specs=out_spec)
    return pl.pallas_call(
        body, out_shape=jax.ShapeDtypeStruct(out_shape, F32), grid_spec=grid_spec, name=f"sum_landed_{tag}",
        compiler_params=_cp("parallel"),
    )(where, own, land)


def _adamw_shard(g, w, m, v, tag):
    R, C = w.shape
    tr = min(R, 256)

    def body(g_ref, w_ref, m_ref, v_ref, d_ref, nm_ref, nv_ref):
        d, nm, nv = _adamw(w_ref[...], g_ref[...], m_ref[...], v_ref[...])
        d_ref[...] = d
        nm_ref[...] = nm
        nv_ref[...] = nv

    spec = pl.BlockSpec((tr, C), lambda i: (i, 0))
    sds = jax.ShapeDtypeStruct((R, C), F32)
    return pl.pallas_call(
        body, out_shape=(sds,) * 3, grid=(R // tr,), name=f"adamw_{tag}",
        in_specs=[spec] * 4, out_specs=(spec,) * 3, compiler_params=_cp("parallel"),
    )(g, w, m, v)


def _adamw_w_ada(c_t, dmod_cols, w, m, v):
    R, C = w.shape

    def body(ct_ref, dm_ref, w_ref, m_ref, v_ref, g_ref, d_ref, nm_ref, nv_ref):
        g = _dot(ct_ref[...].astype(BF16), dm_ref[...].astype(BF16))
        d, nm, nv = _adamw(w_ref[...], g, m_ref[...], v_ref[...])
        g_ref[...] = g
        d_ref[...] = d
        nm_ref[...] = nm
        nv_ref[...] = nv

    tr = 256
    spec = pl.BlockSpec((tr, C), lambda i: (i, 0))
    sds = jax.ShapeDtypeStruct((R, C), F32)
    return pl.pallas_call(
        body, out_shape=(sds,) * 4, grid=(R // tr,), name="adamw_w_ada",
        in_specs=[pl.BlockSpec((tr, 128), lambda i: (i, 0)), pl.BlockSpec((128, C), lambda i: (0, 0))] + [spec] * 3,
        out_specs=(spec,) * 4, compiler_params=_cp("parallel"),
    )(c_t, dmod_cols, w, m, v)


def _adamw_small(small_all, ws, ms, vs):
    def body(s_ref, w_ref, m_ref, v_ref, g_ref, d_ref, nm_ref, nv_ref):
        g = s_ref[0]
        for b in range(1, N_DEV):
            g = g + s_ref[b]
        d, nm, nv = _adamw(w_ref[...], g, m_ref[...], v_ref[...])
        g_ref[...] = g
        d_ref[...] = d
        nm_ref[...] = nm
        nv_ref[...] = nv

    sds = jax.ShapeDtypeStruct((SMALL_ROWS, D_MODEL), F32)
    return pl.pallas_call(
        body, out_shape=(sds,) * 4, name="adamw_small", in_specs=[VMEM_SPEC] * 4, out_specs=(VMEM_SPEC,) * 4,
        compiler_params=pltpu.CompilerParams(vmem_limit_bytes=VMEM_LIMIT_V7X),
    )(small_all, ws, ms, vs)


ROW_MOD, ROW_NORM_G, ROW_CONV_B, ROW_BA, ROW_BX, ROW_LAM, ROW_FINAL_G, ROW_SINKS, ROW_CONV_W = 0, 3, 4, 5, 6, 7, 8, 9, 10


def _pack_small(b_ada, norm_g, conv_b, ba, bx, lam, final_g, sinks, conv_w_full):
    rows = [b_ada.reshape(3, D_MODEL), norm_g, conv_b, ba, bx, lam, final_g.reshape(1, D_MODEL),
            jnp.pad(sinks.reshape(1, -1), ((0, 0), (0, D_MODEL - sinks.size))), conv_w_full,
            jnp.zeros((SMALL_ROWS - 14, D_MODEL), F32)]
    return jnp.concatenate([r.astype(F32) for r in rows], axis=0)


def kernel(x, c, positions, w_ada, b_ada, norm_g, w_in, attn_sinks, conv_w, conv_b, rg_wa, rg_ba, rg_wx, rg_bx, rg_lambda, w_attn_proj, w_rnn_proj, w_out, final_g, loss_target, m_w_ada, m_b_ada, m_norm_g, m_w_in, m_attn_sinks, m_conv_w, m_conv_b, m_rg_wa, m_rg_ba, m_rg_wx, m_rg_bx, m_rg_lambda, m_w_attn_proj, m_w_rnn_proj, m_w_out, m_final_g, v_w_ada, v_b_ada, v_norm_g, v_w_in, v_attn_sinks, v_conv_w, v_conv_b, v_rg_wa, v_rg_ba, v_rg_wx, v_rg_bx, v_rg_lambda, v_w_attn_proj, v_w_rnn_proj, v_w_out, v_final_g):
    T = x.shape[1]
    my_chip = lax.axis_index("x") * 2 + lax.axis_index("y")
    my_dev = my_chip * 2 + lax.axis_index("c")
    x2d, tgt = x[0], loss_target[0]
    pos_col = positions.reshape(T, 1)

    chip_idx = my_chip.reshape(1).astype(jnp.int32)
    c_idx = lax.axis_index("c").reshape(1).astype(jnp.int32)
    sq_place = ((D_MODEL, D_MODEL), (SHARD_ROWS, D_MODEL), lambda chip: (chip, 0))
    rg_place = ((RNN_BLOCKS, RNN_BW, RNN_BW), (RNN_BLOCKS, SHARD_RG, RNN_BW), lambda chip: (0, chip, 0))
    gathered = _gather_weights(
        c.reshape(1, 1, D_MODEL), w_ada[0],
        _cast_place(w_in[0], chip_idx, (D_MODEL, IN_W), (D_MODEL, SHARD_IN), lambda chip: (0, chip), "w_in"),
        _cast_place(w_attn_proj[0], chip_idx, *sq_place, "w_attn_proj"),
        _cast_place(w_rnn_proj[0], chip_idx, *sq_place, "w_rnn_proj"),
        _cast_place(w_out[0], chip_idx, *sq_place, "w_out"),
        _cast_place(rg_wa[0], chip_idx, *rg_place, "rg_wa"),
        _cast_place(rg_wx[0], chip_idx, *rg_place, "rg_wx"),
        conv_w[0])
    w_in_f = gathered[0].reshape(D_MODEL, IN_W)
    wap_f, wrp_f, wo_f = (g.reshape(D_MODEL, D_MODEL) for g in gathered[1:4])
    rwa_f, rwx_f = (g.reshape(RNN_BLOCKS, RNN_BW, RNN_BW) for g in gathered[4:6])
    cw_chips, c_all, mod_chips = gathered[6:]
    conv_w_f = jnp.transpose(cw_chips, (1, 0, 2)).reshape(CONV_W, D_MODEL)
    mod_all = jnp.transpose(mod_chips, (1, 0, 2)).reshape(N_DEV, ADA_W) + b_ada
    mod_row = lax.dynamic_slice_in_dim(mod_all, my_dev, 1, axis=0)

    tabs = _rope_tables(pos_col)
    h = _prenorm(x2d, mod_row, norm_g)
    proj = _in_projection(h, w_in_f)
    y_attn = _attn_forward(proj, tabs, attn_sinks)
    y_rnn, h_rnn = _rnn_forward(proj, pos_col, conv_w_f, conv_b, rwa_f, rwx_f, rg_ba, rg_bx, rg_lambda)
    (dx2, merged, d_o, d_pa, d_pr, d_ya, d_yr, d_c, d_final_g, d_gate, loss_vec) = _merge_and_head(
        x2d, tgt, y_attn, y_rnn, proj, wap_f, wrp_f, wo_f, mod_row, final_g.reshape(1, D_MODEL))

    sq = (N_CHIPS, 2, SHARD_ROWS // 2, D_MODEL)
    rg = (RNN_BLOCKS, N_CHIPS, 2, SHARD_RG // 2, RNN_BW)
    rg_flat = (RNN_BLOCKS * N_CHIPS, 2, SHARD_RG // 2, RNN_BW)

    def chip_sum_and_start(views, axes, flat, unflat, tags_, kinds_, group):
        from_sib = _swap_halves(views, axes)
        sums = [_presum(v.reshape(f), s.reshape(f[:1] + f[2:]), c_idx, t) for v, s, f, t in zip(views, from_sib, flat, tags_)]
        exact = [s[0].reshape(u) for s, u in zip(sums, unflat)]
        rounded = [s[1].reshape(u) for s, u in zip(sums, unflat)]
        return _exchange_start(rounded, kinds_, group), exact

    g_ap = _weight_grad(y_attn, [(d_pa, 0, 2)], "w_attn_proj")
    g_rp = _weight_grad(y_rnn, [(d_pr, 0, 2)], "w_rnn_proj")
    g_o = _weight_grad(merged, [(d_o, 0, 2)], "w_out")
    sq_half = (N_CHIPS, SHARD_ROWS // 2, D_MODEL)
    started1, own1 = chip_sum_and_start([g_ap.reshape(sq), g_rp.reshape(sq), g_o.reshape(sq)], [1, 1, 1], [sq] * 3, [sq_half] * 3,
                                  ["w_attn_proj", "w_rnn_proj", "w_out"], ["sq"] * 3, "proj")
    d_q, d_kv, d_ga, d_sinks = _attn_backward(proj, d_ya, tabs, attn_sinks + started1[4][0, 0])
    d_b, d_conv_w, d_conv_b, d_rwa, d_rwx, d_ba, d_bx, d_lam = _rnn_backward(
        proj, pos_col, h_rnn, d_yr, conv_w_f, conv_b, rwa_f, rwx_f, rg_ba, rg_bx, rg_lambda)
    pieces = [(d_q, CB_Q, 2), (d_kv, CB_KV, 1), (d_ga, CB_GA, 2), (d_b, CB_XR, 4), (d_c, CB_MA, 4)]
    g_in = _weight_grad(h, pieces, "w_in")
    started2, own2 = chip_sum_and_start(
        [g_in.reshape(2, D_MODEL // 2, IN_W), d_rwa.reshape(rg), d_rwx.reshape(rg)], [0, 2, 2],
        [(1, 2, D_MODEL // 2, IN_W), rg_flat, rg_flat],
        [(D_MODEL // 2, IN_W), (RNN_BLOCKS, N_CHIPS, SHARD_RG // 2, RNN_BW), (RNN_BLOCKS, N_CHIPS, SHARD_RG // 2, RNN_BW)],
        ["w_in", "rg_wa", "rg_wx"], ["in", "rg", "rg"], "in")
    grad_x, d_shift, d_scale, d_norm_g = _input_backward(pieces, w_in_f, x2d, dx2, mod_row + started2[4][0, 0], norm_g)

    d_mod = jnp.concatenate([d_shift, d_scale, d_gate], axis=1)
    small = _pack_small(d_mod, d_norm_g, d_conv_b, d_ba, d_bx, d_lam, d_final_g, d_sinks[:, :N_HEADS], d_conv_w)
    small_all = _gather_small(small)
    _, lands1 = _exchange_wait(*started1[:4], grad_x, "proj")
    _, lands2 = _exchange_wait(*started2[:4], grad_x, "in")
    tags = ["w_in", "w_attn_proj", "w_rnn_proj", "w_out", "rg_wa", "rg_wx"]
    chip_sums = [own2[0]] + list(own1) + list(own2[1:])
    lands = [lands2[0]] + list(lands1) + list(lands2[1:])
    where = jnp.concatenate([chip_idx, c_idx])
    kinds = ["in", "sq", "sq", "sq", "rg", "rg"]
    halves = [_sum_landed(kinds[i], chip_sums[i], lands[i], where, tags[i]) for i in range(6)]
    grads = _assemble_with_sibling(halves, [0, 0, 0, 0, 1, 1])
    shapes2d = [(D_MODEL, SHARD_IN), (SHARD_ROWS, D_MODEL), (SHARD_ROWS, D_MODEL), (SHARD_ROWS, D_MODEL),
                (RNN_BLOCKS * SHARD_RG, RNN_BW), (RNN_BLOCKS * SHARD_RG, RNN_BW)]
    big_w = [w_in, w_attn_proj, w_rnn_proj, w_out, rg_wa, rg_wx]
    big_m = [m_w_in, m_w_attn_proj, m_w_rnn_proj, m_w_out, m_rg_wa, m_rg_wx]
    big_v = [v_w_in, v_w_attn_proj, v_w_rnn_proj, v_w_out, v_rg_wa, v_rg_wx]
    res = {}
    for i, tag in enumerate(tags):
        g = grads[i].reshape(shapes2d[i])
        outs = _adamw_shard(g, big_w[i].reshape(shapes2d[i]), big_m[i].reshape(shapes2d[i]), big_v[i].reshape(shapes2d[i]), tag)
        res[tag] = [o.reshape(big_w[i].shape) for o in (g,) + tuple(outs)]

    dmod_all = small_all[:, ROW_MOD:ROW_MOD + 3, :].reshape(N_DEV, ADA_W)
    dmod_cols = lax.dynamic_slice_in_dim(dmod_all, my_chip * SHARD_ADA, SHARD_ADA, axis=1)
    c_t = jnp.pad(jnp.transpose(c_all.reshape(N_DEV, D_MODEL)), ((0, 0), (0, 128 - N_DEV)))
    dmod_cols = jnp.pad(dmod_cols, ((0, 128 - N_DEV), (0, 0)))
    res["w_ada"] = [o.reshape(w_ada.shape) for o in _adamw_w_ada(c_t, dmod_cols, w_ada[0], m_w_ada[0], v_w_ada[0])]

    def full_conv(a):
        return lax.dynamic_update_slice_in_dim(jnp.zeros((CONV_W, D_MODEL), F32), a[0], my_chip * (D_MODEL // N_CHIPS), axis=1)

    packed = [_pack_small(p[0], p[1], p[2], p[3], p[4], p[5], p[6], p[7], full_conv(p[8])) for p in (
        (b_ada, norm_g, conv_b, rg_ba, rg_bx, rg_lambda, final_g, attn_sinks, conv_w),
        (m_b_ada, m_norm_g, m_conv_b, m_rg_ba, m_rg_bx, m_rg_lambda, m_final_g, m_attn_sinks, m_conv_w),
        (v_b_ada, v_norm_g, v_conv_b, v_rg_ba, v_rg_bx, v_rg_lambda, v_final_g, v_attn_sinks, v_conv_w))]
    small_out = _adamw_small(small_all, *packed)

    def unpack(slab):
        cw = lax.dynamic_slice_in_dim(slab[ROW_CONV_W:ROW_CONV_W + CONV_W], my_chip * (D_MODEL // N_CHIPS),
                                      D_MODEL // N_CHIPS, axis=1)
        return {
            "b_ada": slab[ROW_MOD:ROW_MOD + 3].reshape(1, ADA_W), "norm_g": slab[ROW_NORM_G:ROW_NORM_G + 1],
            "conv_b": slab[ROW_CONV_B:ROW_CONV_B + 1], "rg_ba": slab[ROW_BA:ROW_BA + 1], "rg_bx": slab[ROW_BX:ROW_BX + 1],
            "rg_lambda": slab[ROW_LAM:ROW_LAM + 1], "final_g": slab[ROW_FINAL_G], "attn_sinks": slab[ROW_SINKS:ROW_SINKS + 1, :N_HEADS],
            "conv_w": cw[None],
        }

    small_res = [unpack(s) for s in small_out]
    order = ["w_ada", "b_ada", "norm_g", "w_in", "attn_sinks", "conv_w", "conv_b", "rg_wa", "rg_ba", "rg_wx", "rg_bx",
             "rg_lambda", "w_attn_proj", "w_rnn_proj", "w_out", "final_g"]
    loss = lax.psum(loss_vec[0, 0], ("x", "y", "c"))
    outs = [loss, grad_x[None]]
    for kind in range(4):
        for name in order:
            outs.append(res[name][kind] if name in res else small_res[kind][name])
    return tuple(outs)
```

```python
import numpy as np
import jax
import jax.numpy as jnp
from jax import lax
from jax.experimental import pallas as pl
from jax.experimental.pallas import tpu as pltpu

F32 = jnp.float32
BF16 = jnp.bfloat16

D_MODEL = 1024
N_HEADS = 16
N_KV = 4
HEAD_DIM = 64
GROUP = N_HEADS // N_KV
BLOCK = 128
KV_W = N_KV * HEAD_DIM
ROT_HALF = 8
ROPE_THETA = 500000.0
ATTN_SCALE = 0.125
RNN_BLOCKS = 4
RNN_BW = 256
CONV_W = 4
LRU_C = 8.0
NORM_EPS = 1e-6
IN_W = 6656
CB = 512
N_CB = IN_W // CB
CB_Q, CB_KV, CB_GA, CB_XR, CB_GR, CB_MA, CB_MR = 0, 2, 3, 5, 7, 9, 11
N_CHIPS = 4
N_DEV = 8
SHARD_IN = IN_W // N_CHIPS
SHARD_ROWS = D_MODEL // N_CHIPS
SHARD_RG = RNN_BW // N_CHIPS
ADA_W = 3 * D_MODEL
SHARD_ADA = ADA_W // N_CHIPS
SMALL_ROWS = 16

ADAM_LR = 0.001
ADAM_B1 = 0.9
ADAM_B2 = 0.999
ADAM_EPS = 1e-08
ADAM_WD = 0.01
ADAM_STEP = 10

VMEM_LIMIT_V7X = 52 * 1024 * 1024
MESH = pl.DeviceIdType.MESH
ANY = pl.BlockSpec(memory_space=pl.ANY)
VMEM_SPEC = pl.BlockSpec(memory_space=pltpu.VMEM)


def _cp(*sem):
    return pltpu.CompilerParams(dimension_semantics=sem if sem else None, vmem_limit_bytes=VMEM_LIMIT_V7X)


def _dot(a, b):
    return jnp.dot(a, b, preferred_element_type=F32)


def _dot_nt(a, b):
    return lax.dot_general(a, b, (((1,), (1,)), ((), ())), preferred_element_type=F32)


def _dot_tn(a, b):
    return lax.dot_general(a, b, (((0,), (0,)), ((), ())), preferred_element_type=F32)


def _sigmoid(z):
    return 1.0 / (1.0 + jnp.exp(-z))


def _neg_expm1(z):
    series = -(z * (1.0 + z * (0.5 + z * (1.0 / 6.0 + z * (1.0 / 24.0 + z * (1.0 / 120.0))))))
    return jnp.where(z > -0.05, series, 1.0 - jnp.exp(z))


def _softplus(z):
    u = jnp.exp(-jnp.abs(z))
    log1p_u = jnp.where(u < 1e-3, u * (1.0 - u * (0.5 - u * (1.0 / 3.0))), jnp.log(1.0 + u))
    return jnp.maximum(z, 0.0) + log1p_u


def _rms(xf):
    return lax.rsqrt(jnp.mean(xf * xf, axis=-1, keepdims=True) + NORM_EPS)


def _me():
    return lax.axis_index("x"), lax.axis_index("y"), lax.axis_index("c")


def _peer(mask):
    x, y, c = _me()
    fx, fy, fc = (mask >> 2) & 1, (mask >> 1) & 1, mask & 1
    return (x ^ fx if fx else x, y ^ fy if fy else y, c ^ fc if fc else c)


def _chip_of(pos):
    return pos[0] * 2 + pos[1]


CHIP_MASKS = (4, 2, 6)
ALL_MASKS = (1, 2, 3, 4, 5, 6, 7)


def _gather_weights(c_row, w_ada_s, b_w_in, b_wap, b_wrp, b_wo, b_rwa, b_rwx, conv_w_s):
    def body(c_ref, wada_ref, win_s, wap_s, wrp_s, wo_s, rwa_s, rwx_s, cw_s,
             win_f, wap_f, wrp_f, wo_f, rwa_f, rwx_f, cw_f, call_ref, mod_ref,
             wsend, wrecv, lsem, csend, crecv, msend, mrecv, fsend, frecv):
        me = _me()
        my_chip = _chip_of(me)
        my_dev = my_chip * 2 + me[2]
        fulls = (win_f, wap_f, wrp_f, wo_f, rwa_f, rwx_f, cw_f)

        def slot(idx, chip, half=None):
            full = fulls[idx]
            if idx == 0:
                cols = pl.ds(pl.multiple_of(chip * SHARD_IN, 128), SHARD_IN)
                return full.at[:, :, cols] if half is None else full.at[half, :, cols]
            if idx in (1, 2, 3):
                return full.at[chip] if half is None else full.at[chip, half]
            if idx in (4, 5):
                return full.at[:, chip] if half is None else full.at[:, chip, half]
            return full.at[chip]

        def my_half(idx):
            return cw_s if idx == 6 else slot(idx, my_chip, me[2])

        def wcopy(idx, k, to):
            return pltpu.make_async_remote_copy(
                src_ref=my_half(idx), dst_ref=slot(idx, my_chip, None if idx == 6 else me[2]),
                send_sem=wsend.at[idx, k], recv_sem=wrecv.at[idx, k], device_id=to, device_id_type=MESH)

        def wrecv_wait(idx, k, frm):
            pltpu.make_async_remote_copy(
                src_ref=my_half(idx), dst_ref=slot(idx, _chip_of(frm), None if idx == 6 else me[2]),
                send_sem=wsend.at[idx, k], recv_sem=wrecv.at[idx, k], device_id=frm, device_id_type=MESH).wait_recv()

        def forward(idx, k, chip, half, to):
            return pltpu.make_async_remote_copy(
                src_ref=slot(idx, chip, half), dst_ref=slot(idx, chip, half),
                send_sem=fsend.at[idx, k], recv_sem=frecv.at[idx, k], device_id=to, device_id_type=MESH)

        sends = []
        for idx in range(7):
            for k, mask in enumerate(CHIP_MASKS):
                cp = wcopy(idx, k, _peer(mask))
                cp.start()
                sends.append(cp)
        local = [pltpu.make_async_copy(cw_s, slot(6, my_chip), lsem.at[0])]
        for cp in local:
            cp.start()

        call_ref[my_dev] = c_ref[0]
        csends = []
        for k, mask in enumerate(ALL_MASKS):
            cp = pltpu.make_async_remote_copy(
                src_ref=c_ref.at[0], dst_ref=call_ref.at[my_dev],
                send_sem=csend.at[k], recv_sem=crecv.at[k], device_id=_peer(mask), device_id_type=MESH)
            cp.start()
            csends.append(cp)
        for k, mask in enumerate(ALL_MASKS):
            frm = _peer(mask)
            pltpu.make_async_remote_copy(
                src_ref=c_ref.at[0], dst_ref=call_ref.at[_chip_of(frm) * 2 + frm[2]],
                send_sem=csend.at[k], recv_sem=crecv.at[k], device_id=frm, device_id_type=MESH).wait_recv()
        for cp in csends:
            cp.wait_send()

        c_all = call_ref[...].reshape(N_DEV, D_MODEL).astype(BF16)
        mod_ref[my_chip] = _dot(c_all, wada_ref[...].astype(BF16))
        msends = []
        for k, mask in enumerate(CHIP_MASKS):
            cp = pltpu.make_async_remote_copy(
                src_ref=mod_ref.at[my_chip], dst_ref=mod_ref.at[my_chip],
                send_sem=msend.at[k], recv_sem=mrecv.at[k], device_id=_peer(mask), device_id_type=MESH)
            cp.start()
            msends.append(cp)
        for k, mask in enumerate(CHIP_MASKS):
            frm = _peer(mask)
            pltpu.make_async_remote_copy(
                src_ref=mod_ref.at[my_chip], dst_ref=mod_ref.at[_chip_of(frm)],
                send_sem=msend.at[k], recv_sem=mrecv.at[k], device_id=frm, device_id_type=MESH).wait_recv()
        for cp in msends:
            cp.wait_send()

        sib = _peer(1)
        forwards = []
        for idx in range(7):
            for k, mask in enumerate(CHIP_MASKS):
                frm = _peer(mask)
                wrecv_wait(idx, k, frm)
                if idx < 6:
                    cp = forward(idx, k, _chip_of(frm), me[2], sib)
                    cp.start()
                    forwards.append(cp)
        for idx in range(6):
            for k, mask in enumerate(CHIP_MASKS):
                forward(idx, k, _chip_of(_peer(mask)), 1 - me[2], sib).wait_recv()
        for cp in sends + forwards:
            cp.wait_send()
        for cp in local:
            cp.wait()

    out_shape = (
        jax.ShapeDtypeStruct((2, D_MODEL // 2, IN_W), BF16),
        jax.ShapeDtypeStruct((N_CHIPS, 2, SHARD_ROWS // 2, D_MODEL), BF16),
        jax.ShapeDtypeStruct((N_CHIPS, 2, SHARD_ROWS // 2, D_MODEL), BF16),
        jax.ShapeDtypeStruct((N_CHIPS, 2, SHARD_ROWS // 2, D_MODEL), BF16),
        jax.ShapeDtypeStruct((RNN_BLOCKS, N_CHIPS, 2, SHARD_RG // 2, RNN_BW), BF16),
        jax.ShapeDtypeStruct((RNN_BLOCKS, N_CHIPS, 2, SHARD_RG // 2, RNN_BW), BF16),
        jax.ShapeDtypeStruct((N_CHIPS, CONV_W, D_MODEL // N_CHIPS), F32),
        jax.ShapeDtypeStruct((N_DEV, 1, D_MODEL), F32),
        jax.ShapeDtypeStruct((N_CHIPS, N_DEV, SHARD_ADA), F32),
    )
    return pl.pallas_call(
        body, out_shape=out_shape, name="gather_weights",
        in_specs=[VMEM_SPEC, VMEM_SPEC] + [ANY] * 7,
        out_specs=tuple([ANY] * 7 + [VMEM_SPEC, VMEM_SPEC]),
        scratch_shapes=[
            pltpu.SemaphoreType.DMA((7, 3)), pltpu.SemaphoreType.DMA((7, 3)), pltpu.SemaphoreType.DMA((7,)),
            pltpu.SemaphoreType.DMA((7,)), pltpu.SemaphoreType.DMA((7,)),
            pltpu.SemaphoreType.DMA((3,)), pltpu.SemaphoreType.DMA((3,)),
            pltpu.SemaphoreType.DMA((6, 3)), pltpu.SemaphoreType.DMA((6, 3)),
        ],
        input_output_aliases={2: 0, 3: 1, 4: 2, 5: 3, 6: 4, 7: 5},
        compiler_params=pltpu.CompilerParams(vmem_limit_bytes=VMEM_LIMIT_V7X),
    )(c_row, w_ada_s, b_w_in.reshape(out_shape[0].shape), b_wap.reshape(out_shape[1].shape),
      b_wrp.reshape(out_shape[2].shape), b_wo.reshape(out_shape[3].shape), b_rwa.reshape(out_shape[4].shape),
      b_rwx.reshape(out_shape[5].shape), conv_w_s)


def _cast_place(shard, chip_idx, full_shape, block, index_map, tag):
    def body(chip_ref, s_ref, o_ref):
        o_ref[...] = s_ref[...].astype(BF16)

    grid_spec = pltpu.PrefetchScalarGridSpec(
        num_scalar_prefetch=1, grid=(1,),
        in_specs=[pl.BlockSpec(shard.shape, lambda i, chip_ref: (0,) * shard.ndim)],
        out_specs=pl.BlockSpec(block, lambda i, chip_ref: index_map(chip_ref[0])))
    return pl.pallas_call(
        body, out_shape=jax.ShapeDtypeStruct(full_shape, BF16), grid_spec=grid_spec, name=f"cast_place_{tag}",
        compiler_params=_cp("arbitrary"),
    )(chip_idx, shard)


HBM_SPEC = pl.BlockSpec(memory_space=pltpu.HBM)
SEM_SPEC = pl.BlockSpec(memory_space=pltpu.SEMAPHORE)


def _shard_of(ref, kind, chip):
    if kind == "in":
        return ref.at[:, pl.ds(pl.multiple_of(chip * SHARD_IN, 128), SHARD_IN)]
    return ref.at[chip] if kind == "sq" else ref.at[:, chip]


def _land_shape(src, kind):
    if kind == "in":
        return (3, src.shape[0], SHARD_IN)
    return (3,) + src.shape[1:] if kind == "sq" else (3, src.shape[0]) + src.shape[2:]


def _exchange_start(srcs, kinds, tag):
    n = len(srcs)
    lands = [pltpu.with_memory_space_constraint(lax.empty(_land_shape(s, k), s.dtype), pltpu.HBM) for s, k in zip(srcs, kinds)]

    def body(*refs):
        src_refs, land_refs = refs[:n], refs[n:2 * n]
        ssems, rsems = refs[2 * n:3 * n], refs[3 * n:4 * n]
        token = refs[6 * n]
        for i in range(n):
            for k, mask in enumerate(CHIP_MASKS):
                to = _peer(mask)
                pltpu.make_async_remote_copy(
                    src_ref=_shard_of(src_refs[i], kinds[i], _chip_of(to)), dst_ref=land_refs[i].at[k],
                    send_sem=ssems[i], recv_sem=rsems[i], device_id=to, device_id_type=MESH).start()
        token[...] = jnp.zeros_like(token)

    sem = pltpu.SemaphoreType.DMA(())
    out_shape = ((sem,) * (2 * n) + tuple(pltpu.HBM(s.shape, s.dtype) for s in srcs)
                 + tuple(pltpu.HBM(l.shape, l.dtype) for l in lands) + (jax.ShapeDtypeStruct((8, 128), F32),))
    outs = pl.pallas_call(
        body, out_shape=out_shape, name=f"exchange_start_{tag}",
        in_specs=[HBM_SPEC] * (2 * n), out_specs=tuple([SEM_SPEC] * (2 * n) + [HBM_SPEC] * (2 * n) + [VMEM_SPEC]),
        input_output_aliases={i: 2 * n + i for i in range(2 * n)},
        compiler_params=pltpu.CompilerParams(has_side_effects=pltpu.SideEffectType.DATAFLOW_SIDE_EFFECTING),
    )(*[pltpu.with_memory_space_constraint(s, pltpu.HBM) for s in srcs], *lands)
    return outs[:n], outs[n:2 * n], outs[2 * n:3 * n], outs[3 * n:4 * n], outs[4 * n]


def _exchange_wait(ssems, rsems, srcs, lands, after, tag):
    n = len(srcs)

    def body(*refs):
        land_refs = refs[n:2 * n]
        ssem_refs, rsem_refs = refs[2 * n:3 * n], refs[3 * n:4 * n]
        for i in range(n):
            all_three = pltpu.make_async_remote_copy(
                src_ref=land_refs[i], dst_ref=land_refs[i], send_sem=ssem_refs[i], recv_sem=rsem_refs[i],
                device_id=_me(), device_id_type=MESH)
            all_three.wait_send()
            all_three.wait_recv()

    outs = pl.pallas_call(
        body, out_shape=tuple(pltpu.HBM(a.shape, a.dtype) for a in list(srcs) + list(lands)), name=f"exchange_wait_{tag}",
        in_specs=[HBM_SPEC] * (2 * n) + [SEM_SPEC] * (2 * n) + [ANY], out_specs=tuple([HBM_SPEC] * (2 * n)),
        input_output_aliases={i: i for i in range(2 * n)},
        compiler_params=pltpu.CompilerParams(has_side_effects=pltpu.SideEffectType.DATAFLOW_SIDE_EFFECTING),
    )(*srcs, *lands, *ssems, *rsems, after)
    return outs[:n], outs[n:]


def _gather_small(small):
    def body(small_ref, small_all, ssend, srecv):
        me = _me()
        my_dev = _chip_of(me) * 2 + me[2]
        small_all[my_dev] = small_ref[...]
        ssends = []
        for k, mask in enumerate(ALL_MASKS):
            cp = pltpu.make_async_remote_copy(
                src_ref=small_ref, dst_ref=small_all.at[my_dev],
                send_sem=ssend.at[k], recv_sem=srecv.at[k], device_id=_peer(mask), device_id_type=MESH)
            cp.start()
            ssends.append(cp)
        for k, mask in enumerate(ALL_MASKS):
            frm = _peer(mask)
            pltpu.make_async_remote_copy(
                src_ref=small_ref, dst_ref=small_all.at[_chip_of(frm) * 2 + frm[2]],
                send_sem=ssend.at[k], recv_sem=srecv.at[k], device_id=frm, device_id_type=MESH).wait_recv()
        for cp in ssends:
            cp.wait_send()

    return pl.pallas_call(
        body, out_shape=jax.ShapeDtypeStruct((N_DEV, SMALL_ROWS, D_MODEL), F32), name="gather_small",
        in_specs=[VMEM_SPEC], out_specs=VMEM_SPEC,
        scratch_shapes=[pltpu.SemaphoreType.DMA((7,)), pltpu.SemaphoreType.DMA((7,))],
    )(small)


def _half_of(ref, axis, half):
    return ref.at[(slice(None),) * axis + (half,)]


def _swap_halves(parts, axes):
    n = len(parts)

    def body(*refs):
        ins, outs, ssem, rsem = refs[:n], refs[n:2 * n], refs[2 * n], refs[2 * n + 1]
        c = lax.axis_index("c")
        cps = [pltpu.make_async_remote_copy(src_ref=_half_of(ins[i], axes[i], 1 - c), dst_ref=outs[i], send_sem=ssem.at[i],
                                            recv_sem=rsem.at[i], device_id=_peer(1), device_id_type=MESH) for i in range(n)]
        for cp in cps:
            cp.start()
        for cp in cps:
            cp.wait()

    shapes = [p.shape[:a] + p.shape[a + 1:] for p, a in zip(parts, axes)]
    return pl.pallas_call(
        body, out_shape=tuple(jax.ShapeDtypeStruct(s, p.dtype) for s, p in zip(shapes, parts)), name="swap_halves",
        in_specs=[ANY] * n, out_specs=tuple([ANY] * n),
        scratch_shapes=[pltpu.SemaphoreType.DMA((n,)), pltpu.SemaphoreType.DMA((n,))],
    )(*parts)


def _presum(mine, sib, c_idx, tag):
    S, _, R, C = mine.shape
    tr = min(R, 256)
    tc = SHARD_IN if C % SHARD_IN == 0 else C

    def body(c_ref, m_ref, s_ref, o_ref, ob_ref):
        total = m_ref[:, 0] + s_ref[...]
        o_ref[...] = total
        ob_ref[...] = total.astype(BF16)

    out_spec = pl.BlockSpec((S, tr, tc), lambda i, j, c_ref: (0, i, j))
    grid_spec = pltpu.PrefetchScalarGridSpec(
        num_scalar_prefetch=1, grid=(R // tr, C // tc),
        in_specs=[pl.BlockSpec((S, 1, tr, tc), lambda i, j, c_ref: (0, c_ref[0], i, j)),
                  pl.BlockSpec((S, tr, tc), lambda i, j, c_ref: (0, i, j))],
        out_specs=(out_spec, out_spec))
    return pl.pallas_call(
        body, out_shape=(jax.ShapeDtypeStruct((S, R, C), F32), jax.ShapeDtypeStruct((S, R, C), BF16)),
        grid_spec=grid_spec, name=f"presum_{tag}", compiler_params=_cp("parallel", "parallel"),
    )(c_idx, mine, sib)


def _assemble_with_sibling(parts, axes):
    n = len(parts)

    def body(*refs):
        outs, ssem, rsem = refs[n:2 * n], refs[2 * n], refs[2 * n + 1]
        c = lax.axis_index("c")
        cps = [pltpu.make_async_remote_copy(
            src_ref=_half_of(outs[i], axes[i], c), dst_ref=_half_of(outs[i], axes[i], c), send_sem=ssem.at[i],
            recv_sem=rsem.at[i], device_id=_peer(1), device_id_type=MESH) for i in range(n)]
        for cp in cps:
            cp.start()
        for i in range(n):
            pltpu.make_async_remote_copy(
                src_ref=_half_of(outs[i], axes[i], c), dst_ref=_half_of(outs[i], axes[i], 1 - c), send_sem=ssem.at[i],
                recv_sem=rsem.at[i], device_id=_peer(1), device_id_type=MESH).wait_recv()
        for cp in cps:
            cp.wait_send()

    return pl.pallas_call(
        body, out_shape=tuple(jax.ShapeDtypeStruct(p.shape, p.dtype) for p in parts), name="assemble_with_sibling",
        in_specs=[ANY] * n, out_specs=tuple([ANY] * n), input_output_aliases={i: i for i in range(n)},
        scratch_shapes=[pltpu.SemaphoreType.DMA((n,)), pltpu.SemaphoreType.DMA((n,))],
    )(*parts)


def _rope_tables(pos_col):
    T = pos_col.shape[0]
    tm = min(T, 512)
    inv = np.float32(ROPE_THETA) ** (-(np.arange(0, 2 * ROT_HALF, 2, dtype=np.float32)) / np.float32(2 * ROT_HALF))
    lane = np.arange(128) % HEAD_DIM
    freq = np.where(lane < 2 * ROT_HALF, inv[lane % ROT_HALF], 0.0).astype(np.float32)[None, :]

    def body(pos_ref, f_ref, c_ref, sa_ref, sb_ref):
        ang = pos_ref[...].astype(F32) * f_ref[...]
        c, s = jnp.cos(ang), jnp.sin(ang)
        m = lax.broadcasted_iota(jnp.int32, ang.shape, 1) & (HEAD_DIM - 1)
        c_ref[...] = jnp.where(m < 2 * ROT_HALF, c, 1.0)
        sa_ref[...] = jnp.where(m < ROT_HALF, -s, 0.0)
        sb_ref[...] = jnp.where((m >= ROT_HALF) & (m < 2 * ROT_HALF), s, 0.0)

    tab = jax.ShapeDtypeStruct((T, 128), F32)
    return pl.pallas_call(
        body, out_shape=(tab, tab, tab), grid=(T // tm,), name="rope_tables",
        in_specs=[pl.BlockSpec((tm, 1), lambda i: (i, 0)), pl.BlockSpec((1, 128), lambda i: (0, 0))],
        out_specs=tuple(pl.BlockSpec((tm, 128), lambda i: (i, 0)) for _ in range(3)),
        compiler_params=_cp("parallel"),
    )(pos_col, jnp.asarray(freq))


def _wide(tab, width):
    del width
    return tab


def _columns(t):
    return [t[:, i:i + 128] for i in range(0, t.shape[-1], 128)]


def _rope(t, c, sa, sb):
    return jnp.concatenate(
        [x * c + pltpu.roll(x, 128 - ROT_HALF, 1) * sa + pltpu.roll(x, ROT_HALF, 1) * sb for x in _columns(t)], axis=1)


def _unrope(d, c, sa, sb):
    return jnp.concatenate(
        [x * c + pltpu.roll(x * sa, ROT_HALF, 1) + pltpu.roll(x * sb, 128 - ROT_HALF, 1) for x in _columns(d)], axis=1)


def _prenorm(x, mod_row, norm_g):
    T = x.shape[0]
    tm = min(T, 512)

    def body(x_ref, mod_ref, g_ref, h_ref):
        xf = x_ref[...]
        shift, scale = mod_ref[:, 0:D_MODEL], mod_ref[:, D_MODEL:2 * D_MODEL]
        h = (xf * _rms(xf)) * g_ref[...] * (1.0 + scale) + shift
        h_ref[...] = h.astype(BF16)

    return pl.pallas_call(
        body, out_shape=jax.ShapeDtypeStruct((T, D_MODEL), BF16), grid=(T // tm,), name="prenorm",
        in_specs=[pl.BlockSpec((tm, D_MODEL), lambda i: (i, 0)), pl.BlockSpec((1, ADA_W), lambda i: (0, 0)),
                  pl.BlockSpec((1, D_MODEL), lambda i: (0, 0))],
        out_specs=pl.BlockSpec((tm, D_MODEL), lambda i: (i, 0)),
        compiler_params=_cp("parallel"),
    )(x, mod_row, norm_g)


def _in_projection(h, w_in):
    T = h.shape[0]
    tm, tn = min(T, 512), SHARD_IN

    def body(h_ref, w_ref, o_ref):
        o_ref[...] = _dot(h_ref[...], w_ref[...])

    return pl.pallas_call(
        body, out_shape=jax.ShapeDtypeStruct((T, IN_W), F32), grid=(IN_W // tn, T // tm), name="in_projection",
        in_specs=[pl.BlockSpec((tm, D_MODEL), lambda j, i: (i, 0)), pl.BlockSpec((D_MODEL, tn), lambda j, i: (0, j))],
        out_specs=pl.BlockSpec((tm, tn), lambda j, i: (i, j)),
        compiler_params=_cp("parallel", "parallel"),
    )(h, w_in)


def _attn_mask(n):
    qi = lax.broadcasted_iota(jnp.int32, (GROUP * BLOCK, 2 * BLOCK), 0) & (BLOCK - 1)
    kj = lax.broadcasted_iota(jnp.int32, (GROUP * BLOCK, 2 * BLOCK), 1)
    diff = qi + BLOCK - kj
    return (diff >= 0) & (diff < BLOCK) & ((kj >= BLOCK) | (n > 0))


ROW_GROUP_HEAD = (0, 2, 1, 3)


def _sink_col(sink_ref, kh):
    rowg = lax.broadcasted_iota(jnp.int32, (GROUP * BLOCK, 1), 0) // BLOCK
    col = jnp.full((GROUP * BLOCK, 1), sink_ref[0, GROUP * kh + ROW_GROUP_HEAD[0]], F32)
    for g in range(1, GROUP):
        col = jnp.where(rowg == g, sink_ref[0, GROUP * kh + ROW_GROUP_HEAD[g]], col)
    return col


def _low_lanes(shape):
    return lax.broadcasted_iota(jnp.int32, shape, 1) < HEAD_DIM


def _kv_pair_operand(prev, cur, kh):
    c = 128 * (kh // 2)
    col = jnp.concatenate([prev[:, c:c + 128], cur[:, c:c + 128]], axis=0)
    if kh % 2 == 0:
        lo = jnp.where(_low_lanes(col.shape), col, 0.0)
        hi = pltpu.roll(lo, HEAD_DIM, 1)
    else:
        hi = jnp.where(_low_lanes(col.shape), 0.0, col)
        lo = pltpu.roll(hi, HEAD_DIM, 1)
    return jnp.concatenate([lo, hi], axis=0).astype(BF16)


def _pair_rows(x, kh):
    c = 2 * 128 * kh
    return jnp.concatenate([x[:, c:c + 128], x[:, c + 128:c + 256]], axis=0)


def _restack(big):
    return jnp.concatenate([big[:, 0:2 * BLOCK], big[:, 2 * BLOCK:4 * BLOCK]], axis=0)


def _unrestack(stacked):
    return jnp.concatenate([stacked[0:2 * BLOCK], stacked[2 * BLOCK:4 * BLOCK]], axis=1)


def _fold_pair(x2, kh):
    low = _low_lanes((2 * BLOCK, 128))
    mixed = jnp.where(low, x2[0:2 * BLOCK], x2[2 * BLOCK:4 * BLOCK])
    total = mixed + pltpu.roll(mixed, HEAD_DIM, 1)
    return jnp.where(low, total, 0.0) if kh % 2 == 0 else jnp.where(low, 0.0, total)


def _attn_scores(qr, k2, kh):
    q2 = _pair_rows(qr, kh).astype(BF16)
    return q2, _restack(_dot_nt(q2, k2))


def _attn_softmax(s, sink_col, mask):
    s = jnp.where(mask, s, -1e30)
    m = jnp.maximum(jnp.max(s, axis=-1, keepdims=True), sink_col)
    p = jnp.exp(s - m)
    p_sink = jnp.exp(sink_col - m)
    denom = jnp.sum(p, axis=-1, keepdims=True) + p_sink
    return p / denom, p_sink / denom


def _attn_forward(proj, tabs, sinks):
    T = proj.shape[0]
    nb = T // BLOCK

    def body(q_ref, kvc_ref, kvp_ref, g0_ref, g1_ref, cc, sac, sbc, cp_, sap, sbp, sink_ref, y_ref):
        n = pl.program_id(0)
        tc = (_wide(cc[...], D_MODEL), _wide(sac[...], D_MODEL), _wide(sbc[...], D_MODEL))
        tcur = tuple(t[:, :KV_W] for t in tc)
        tprev = (_wide(cp_[...], KV_W), _wide(sap[...], KV_W), _wide(sbp[...], KV_W))
        qr = _rope(q_ref[...], *tc) * ATTN_SCALE
        kr_cur = _rope(kvc_ref[:, 0:KV_W], *tcur)
        kr_prev = _rope(kvp_ref[:, 0:KV_W], *tprev)
        v_cur, v_prev = kvc_ref[:, KV_W:2 * KV_W], kvp_ref[:, KV_W:2 * KV_W]
        mask = _attn_mask(n)
        outs = []
        k2s = [_kv_pair_operand(kr_prev, kr_cur, kh) for kh in range(N_KV)]
        v2s = [_kv_pair_operand(v_prev, v_cur, kh) for kh in range(N_KV)]
        scores = [_attn_scores(qr, k2s[kh], kh) for kh in range(N_KV)]
        for kh in range(N_KV):
            pn, _ = _attn_softmax(scores[kh][1], _sink_col(sink_ref, kh), mask)
            o_big = _dot(_unrestack(pn.astype(BF16)), v2s[kh])
            outs += [o_big[0:BLOCK], o_big[BLOCK:2 * BLOCK]]
        o = jnp.concatenate(outs, axis=1)
        g = jnp.concatenate([g0_ref[...], g1_ref[...]], axis=1)
        y_ref[...] = (o * (g * _sigmoid(g))).astype(BF16)

    def blk(w, cb):
        return pl.BlockSpec((BLOCK, w), lambda n, cb=cb: (n, cb))

    prev = lambda w, cb: pl.BlockSpec((BLOCK, w), lambda n, cb=cb: (jnp.maximum(n - 1, 0), cb))
    return pl.pallas_call(
        body, out_shape=jax.ShapeDtypeStruct((T, D_MODEL), BF16), grid=(nb,), name="attn_forward",
        in_specs=[blk(D_MODEL, 0), blk(CB, CB_KV), prev(CB, CB_KV), blk(CB, CB_GA), blk(CB, CB_GA + 1),
                  blk(128, 0), blk(128, 0), blk(128, 0), prev(128, 0), prev(128, 0), prev(128, 0),
                  pl.BlockSpec(memory_space=pltpu.SMEM)],
        out_specs=pl.BlockSpec((BLOCK, D_MODEL), lambda n: (n, 0)),
        compiler_params=_cp("parallel"),
    )(proj, proj, proj, proj, proj, *tabs, *tabs, sinks)


def _scan_rows8():
    return lax.broadcasted_iota(jnp.int32, (8, D_MODEL), 0)


def _scan_forward(a_ref, b_ref, h_ref, carry, rows):
    row = _scan_rows8()

    def group(i, carry):
        off = pl.multiple_of(i * 8, 8)
        a, b = a_ref[pl.ds(off, 8), :], b_ref[pl.ds(off, 8), :]
        for d in (1, 2, 4):
            ok = row >= d
            b = jnp.where(ok, a * pltpu.roll(b, d, 0) + b, b)
            a = jnp.where(ok, a * pltpu.roll(a, d, 0), a)
        h = a * carry + b
        h_ref[pl.ds(off, 8), :] = h
        return h[7:8, :]

    return lax.fori_loop(0, rows // 8, group, carry)


def _scan_backward(a_ref, g_ref, lam_ref, carry, rows):
    row = _scan_rows8()

    def group(i, carry):
        off = pl.multiple_of((rows // 8 - 1 - i) * 8, 8)
        a, g = a_ref[pl.ds(off, 8), :], g_ref[pl.ds(off, 8), :]
        b = a * g
        for d in (1, 2, 4):
            ok = row < 8 - d
            b = jnp.where(ok, a * pltpu.roll(b, 8 - d, 0) + b, b)
            a = jnp.where(ok, a * pltpu.roll(a, 8 - d, 0), a)
        mu = a * carry + b
        mu_below = jnp.where(row == 7, carry, pltpu.roll(mu, 7, 0))
        lam_ref[pl.ds(off, 8), :] = g + mu_below
        return mu[0:1, :]

    return lax.fori_loop(0, rows // 8, group, carry)


def _rnn_recompute(xbuf, xr, tail, cw, cb, wa_ref, wx_ref, ba, bx, sp, reset):
    rows = xr.shape[0]
    xbuf[0:8, :] = tail
    xbuf[8:rows + 8, :] = xr
    xs = [xbuf[pl.ds(8 - (CONV_W - 1 - k), rows), :] for k in range(CONV_W - 1)] + [xr]
    xc = xs[0] * cw[0:1, :]
    for k in range(1, CONV_W):
        xc = xc + xs[k] * cw[k:k + 1, :]
    xc = xc + cb
    xcb = xc.astype(BF16)
    za = jnp.concatenate([_dot(xcb[:, RNN_BW * j:RNN_BW * (j + 1)], wa_ref[j]) for j in range(RNN_BLOCKS)], axis=1) + ba
    zx = jnp.concatenate([_dot(xcb[:, RNN_BW * j:RNN_BW * (j + 1)], wx_ref[j]) for j in range(RNN_BLOCKS)], axis=1) + bx
    r, i = _sigmoid(za), _sigmoid(zx)
    log_a = -LRU_C * r * sp
    a_raw = jnp.exp(log_a)
    mult_raw = jnp.sqrt(_neg_expm1(2.0 * log_a))
    a = jnp.where(reset, 0.0, a_raw)
    mult = jnp.where(reset, 1.0, mult_raw)
    return xs, xc, xcb, r, i, a_raw, mult_raw, a, mult


def _rnn_forward(proj, pos_col, conv_w, conv_b, rwa, rwx, ba, bx, lam):
    T = proj.shape[0]
    tr = min(T, 256)

    def body(x0, x1, g0, g1, pos_ref, cw_ref, cb_ref, wa_ref, wx_ref, ba_ref, bx_ref, lam_ref,
             y_ref, h_ref, xbuf, abuf, bbuf, tail, carry):
        t = pl.program_id(0)

        @pl.when(t == 0)
        def _():
            tail[...] = jnp.zeros_like(tail)
            carry[...] = jnp.zeros_like(carry)

        xr = jnp.concatenate([x0[...], x1[...]], axis=1)
        sp = _softplus(-lam_ref[...])
        reset = pos_ref[...] == 0
        _, xc, _, _, i, _, _, a, mult = _rnn_recompute(
            xbuf, xr, tail[...], cw_ref[...], cb_ref[...], wa_ref, wx_ref, ba_ref[...], bx_ref[...], sp, reset)
        abuf[...] = a
        bbuf[...] = mult * (i * xc)
        last = _scan_forward(abuf, bbuf, h_ref, carry[0:1, :], tr)
        carry[...] = jnp.broadcast_to(last, carry.shape)
        tail[...] = xr[tr - 8:tr, :]
        g = jnp.concatenate([g0[...], g1[...]], axis=1)
        y_ref[...] = (h_ref[...] * (g * _sigmoid(g))).astype(BF16)

    blk = lambda cb: pl.BlockSpec((tr, CB), lambda t, cb=cb: (t, cb))
    row = lambda w: pl.BlockSpec((1, w), lambda t: (0, 0))
    full3 = pl.BlockSpec((RNN_BLOCKS, RNN_BW, RNN_BW), lambda t: (0, 0, 0))
    return pl.pallas_call(
        body, out_shape=(jax.ShapeDtypeStruct((T, D_MODEL), BF16), jax.ShapeDtypeStruct((T, D_MODEL), F32)),
        grid=(T // tr,), name="rnn_forward",
        in_specs=[blk(CB_XR), blk(CB_XR + 1), blk(CB_GR), blk(CB_GR + 1), pl.BlockSpec((tr, 1), lambda t: (t, 0)),
                  pl.BlockSpec((CONV_W, D_MODEL), lambda t: (0, 0)), row(D_MODEL), full3, full3,
                  row(D_MODEL), row(D_MODEL), row(D_MODEL)],
        out_specs=(pl.BlockSpec((tr, D_MODEL), lambda t: (t, 0)), pl.BlockSpec((tr, D_MODEL), lambda t: (t, 0))),
        scratch_shapes=[pltpu.VMEM((tr + 8, D_MODEL), F32), pltpu.VMEM((tr, D_MODEL), F32), pltpu.VMEM((tr, D_MODEL), F32),
                        pltpu.VMEM((8, D_MODEL), F32), pltpu.VMEM((8, D_MODEL), F32)],
        compiler_params=_cp("arbitrary"),
    )(proj, proj, proj, proj, pos_col, conv_w, conv_b, rwa, rwx, ba, bx, lam)


def _merge_and_head(x, target, y_attn, y_rnn, proj, wap, wrp, wo, mod_row, final_g):
    T = x.shape[0]
    tm = min(T, 256)

    def body(x_ref, t_ref, ya_ref, yr_ref, ma0, ma1, mr0, mr1, wap_ref, wrp_ref, wo_ref, mod_ref, fg_ref,
             dx2_ref, mg_ref, do_ref, dpa_ref, dpr_ref, dya_ref, dyr_ref, dc_ref, dfg_ref, dgate_ref, loss_ref):
        i = pl.program_id(0)
        gate = mod_ref[:, 2 * D_MODEL:3 * D_MODEL]
        ya, yr = ya_ref[...], yr_ref[...]
        pa, pr = _dot(ya, wap_ref[...]), _dot(yr, wrp_ref[...])
        sa = _sigmoid(jnp.concatenate([ma0[...], ma1[...]], axis=1))
        sr = _sigmoid(jnp.concatenate([mr0[...], mr1[...]], axis=1))
        merged = sa * pa + sr * pr
        mb = merged.astype(BF16)
        o = _dot(mb, wo_ref[...])
        x2 = x_ref[...] + gate * o
        r2 = _rms(x2)
        xn2 = x2 * r2
        fg = fg_ref[...]
        err = xn2 * fg - t_ref[...]
        loss_t = 0.5 * jnp.sum(jnp.sum(err * err, axis=-1, keepdims=True) * (1.0 / D_MODEL), axis=0, keepdims=True)
        dy = err * (1.0 / D_MODEL)
        dfg_t = jnp.sum(dy * xn2, axis=0, keepdims=True)
        dxn = dy * fg
        dx2 = r2 * (dxn - xn2 * jnp.mean(dxn * xn2, axis=-1, keepdims=True))
        dgate_t = jnp.sum(dx2 * o, axis=0, keepdims=True)
        dob = (dx2 * gate).astype(BF16)
        dmerged = _dot_nt(dob, wo_ref[...])
        dpa = (dmerged * sa).astype(BF16)
        dpr = (dmerged * sr).astype(BF16)
        dx2_ref[...] = dx2
        mg_ref[...] = mb
        do_ref[...] = dob
        dpa_ref[...] = dpa
        dpr_ref[...] = dpr
        dya_ref[...] = _dot_nt(dpa, wap_ref[...])
        dyr_ref[...] = _dot_nt(dpr, wrp_ref[...])
        dc_ref[:, 0:D_MODEL] = (dmerged * pa * sa * (1.0 - sa)).astype(BF16)
        dc_ref[:, D_MODEL:2 * D_MODEL] = (dmerged * pr * sr * (1.0 - sr)).astype(BF16)

        @pl.when(i == 0)
        def _():
            dfg_ref[...] = jnp.zeros_like(dfg_ref)
            dgate_ref[...] = jnp.zeros_like(dgate_ref)
            loss_ref[...] = jnp.zeros_like(loss_ref)

        dfg_ref[...] += dfg_t
        dgate_ref[...] += dgate_t
        loss_ref[...] += jnp.broadcast_to(loss_t, loss_ref.shape)

    tok = lambda w: pl.BlockSpec((tm, w), lambda i: (i, 0))
    blk = lambda cb: pl.BlockSpec((tm, CB), lambda i, cb=cb: (i, cb))
    wfull = pl.BlockSpec((D_MODEL, D_MODEL), lambda i: (0, 0))
    row = lambda w: pl.BlockSpec((1, w), lambda i: (0, 0))
    out_shape = (
        jax.ShapeDtypeStruct((T, D_MODEL), F32), jax.ShapeDtypeStruct((T, D_MODEL), BF16),
        jax.ShapeDtypeStruct((T, D_MODEL), BF16), jax.ShapeDtypeStruct((T, D_MODEL), BF16),
        jax.ShapeDtypeStruct((T, D_MODEL), BF16), jax.ShapeDtypeStruct((T, D_MODEL), F32),
        jax.ShapeDtypeStruct((T, D_MODEL), F32), jax.ShapeDtypeStruct((T, 2 * D_MODEL), BF16),
        jax.ShapeDtypeStruct((1, D_MODEL), F32), jax.ShapeDtypeStruct((1, D_MODEL), F32),
        jax.ShapeDtypeStruct((1, 128), F32),
    )
    return pl.pallas_call(
        body, out_shape=out_shape, grid=(T // tm,), name="merge_and_head",
        in_specs=[tok(D_MODEL), tok(D_MODEL), tok(D_MODEL), tok(D_MODEL), blk(CB_MA), blk(CB_MA + 1), blk(CB_MR),
                  blk(CB_MR + 1), wfull, wfull, wfull, row(ADA_W), row(D_MODEL)],
        out_specs=(tok(D_MODEL),) * 7 + (tok(2 * D_MODEL), row(D_MODEL), row(D_MODEL), row(128)),
        compiler_params=_cp("arbitrary"),
    )(x, target, y_attn, y_rnn, proj, proj, proj, proj, wap, wrp, wo, mod_row, final_g)


def _attn_backward(proj, d_y, tabs, sinks):
    T = proj.shape[0]
    nb = T // BLOCK

    def body(q_ref, kvc_ref, kvp_ref, g0_ref, g1_ref, dy_ref, cc, sac, sbc, cp_, sap, sbp, sink_ref,
             dq_ref, dkv_ref, dg_ref, dsink_ref, carry):
        n = pl.program_id(0)

        @pl.when(n == 0)
        def _():
            carry[...] = jnp.zeros_like(carry)
            dsink_ref[...] = jnp.zeros_like(dsink_ref)

        @pl.when(n < nb)
        def _():
            tc = (_wide(cc[...], D_MODEL), _wide(sac[...], D_MODEL), _wide(sbc[...], D_MODEL))
            tcur = tuple(t[:, :KV_W] for t in tc)
            tprev = (_wide(cp_[...], KV_W), _wide(sap[...], KV_W), _wide(sbp[...], KV_W))
            qr = _rope(q_ref[...], *tc) * ATTN_SCALE
            kr_cur = _rope(kvc_ref[:, 0:KV_W], *tcur)
            kr_prev = _rope(kvp_ref[:, 0:KV_W], *tprev)
            v_cur, v_prev = kvc_ref[:, KV_W:2 * KV_W], kvp_ref[:, KV_W:2 * KV_W]
            g = jnp.concatenate([g0_ref[...], g1_ref[...]], axis=1)
            sg = _sigmoid(g)
            dy = dy_ref[...]
            d_o = dy * (g * sg)
            mask = _attn_mask(n)
            lane = lax.broadcasted_iota(jnp.int32, (1, 128), 1)
            rowg = lax.broadcasted_iota(jnp.int32, (GROUP * BLOCK, 1), 0) // BLOCK
            o_parts, dq_parts = [], []
            dk_cols, dv_cols = [None, None], [None, None]
            dsink = jnp.zeros((1, 128), F32)
            k2s = [_kv_pair_operand(kr_prev, kr_cur, kh) for kh in range(N_KV)]
            v2s = [_kv_pair_operand(v_prev, v_cur, kh) for kh in range(N_KV)]
            scores = [_attn_scores(qr, k2s[kh], kh) for kh in range(N_KV)]
            do2s = [_pair_rows(d_o, kh).astype(BF16) for kh in range(N_KV)]
            dpns = [_restack(_dot_nt(do2s[kh], v2s[kh])) for kh in range(N_KV)]
            probs = [_attn_softmax(scores[kh][1], _sink_col(sink_ref, kh), mask) for kh in range(N_KV)]
            p_bigs = [_unrestack(probs[kh][0].astype(BF16)) for kh in range(N_KV)]
            o_bigs = [_dot(p_bigs[kh], v2s[kh]) for kh in range(N_KV)]
            dv2s = [_dot_tn(p_bigs[kh], do2s[kh]) for kh in range(N_KV)]
            deltas = [jnp.sum(probs[kh][0] * dpns[kh], axis=-1, keepdims=True) for kh in range(N_KV)]
            ds_bigs = [_unrestack((probs[kh][0] * (dpns[kh] - deltas[kh])).astype(BF16)) for kh in range(N_KV)]
            dq2s = [_dot(ds_bigs[kh], k2s[kh]) for kh in range(N_KV)]
            dk2s = [_dot_tn(ds_bigs[kh], scores[kh][0]) for kh in range(N_KV)]
            for kh in range(N_KV):
                o_parts += [o_bigs[kh][0:BLOCK], o_bigs[kh][BLOCK:2 * BLOCK]]
                dq_parts += [dq2s[kh][0:BLOCK], dq2s[kh][BLOCK:2 * BLOCK]]
                dk_c, dv_c = _fold_pair(dk2s[kh], kh), _fold_pair(dv2s[kh], kh)
                c = kh // 2
                dk_cols[c] = dk_c if dk_cols[c] is None else dk_cols[c] + dk_c
                dv_cols[c] = dv_c if dv_cols[c] is None else dv_cols[c] + dv_c
                ds_rows = probs[kh][1] * deltas[kh]
                for gq in range(GROUP):
                    val = -jnp.sum(jnp.where(rowg == gq, ds_rows, 0.0), axis=0, keepdims=True)
                    dsink = dsink + jnp.where(lane == GROUP * kh + ROW_GROUP_HEAD[gq], val, 0.0)
            o = jnp.concatenate(o_parts, axis=1)
            dg_ref[...] = (dy * o * (sg * (1.0 + g * (1.0 - sg)))).astype(BF16)
            dq_ref[...] = (_unrope(jnp.concatenate(dq_parts, axis=1), *tc) * ATTN_SCALE).astype(BF16)
            dk_all, dv_all = jnp.concatenate(dk_cols, axis=1), jnp.concatenate(dv_cols, axis=1)
            dk_prev = _unrope(dk_all[0:BLOCK], *tprev)
            dk_cur = _unrope(dk_all[BLOCK:2 * BLOCK], *tcur)
            dv_prev, dv_cur = dv_all[0:BLOCK], dv_all[BLOCK:2 * BLOCK]
            dkv_ref[...] = (carry[...] + jnp.concatenate([dk_prev, dv_prev], axis=1)).astype(BF16)
            carry[...] = jnp.concatenate([dk_cur, dv_cur], axis=1)
            dsink_ref[...] += dsink

        @pl.when(n == nb)
        def _():
            dkv_ref[...] = carry[...].astype(BF16)

    cur = lambda w, cb: pl.BlockSpec((BLOCK, w), lambda n, cb=cb: (jnp.minimum(n, nb - 1), cb))
    prev = lambda w, cb: pl.BlockSpec((BLOCK, w), lambda n, cb=cb: (jnp.maximum(jnp.minimum(n, nb - 1) - 1, 0), cb))
    out_shape = (jax.ShapeDtypeStruct((T, D_MODEL), BF16), jax.ShapeDtypeStruct((T, 2 * KV_W), BF16),
                 jax.ShapeDtypeStruct((T, D_MODEL), BF16), jax.ShapeDtypeStruct((1, 128), F32))
    return pl.pallas_call(
        body, out_shape=out_shape, grid=(nb + 1,), name="attn_backward",
        in_specs=[cur(D_MODEL, 0), cur(CB, CB_KV), prev(CB, CB_KV), cur(CB, CB_GA), cur(CB, CB_GA + 1), cur(D_MODEL, 0),
                  cur(128, 0), cur(128, 0), cur(128, 0), prev(128, 0), prev(128, 0), prev(128, 0),
                  pl.BlockSpec(memory_space=pltpu.SMEM)],
        out_specs=(cur(D_MODEL, 0), pl.BlockSpec((BLOCK, 2 * KV_W), lambda n: (jnp.maximum(n - 1, 0), 0)),
                   cur(D_MODEL, 0), pl.BlockSpec((1, 128), lambda n: (0, 0))),
        scratch_shapes=[pltpu.VMEM((BLOCK, 2 * KV_W), F32)],
        compiler_params=_cp("arbitrary"),
    )(proj, proj, proj, proj, proj, d_y, *tabs, *tabs, sinks)


def _rnn_backward(proj, pos_col, h_rnn, d_y, conv_w, conv_b, rwa, rwx, ba, bx, lam):
    T = proj.shape[0]
    tr = min(T, 256)
    nt = T // tr
    hb = tr // 8

    def body(x0, x1, xh0, xh1, g0, g1, pos_ref, h_ref, hh_ref, dy_ref, cw_ref, cb_ref, wa_ref, wx_ref, ba_ref, bx_ref,
             lam_ref, db_ref, dcw_ref, dcb_ref, dwa_ref, dwx_ref, dba_ref, dbx_ref, dlam_ref,
             xbuf, hbuf, dbuf, abuf, gbuf, lbuf, mu_carry, dxc_head):
        step = pl.program_id(0)
        first_tile = step == nt - 1

        @pl.when(step == 0)
        def _():
            mu_carry[...] = jnp.zeros_like(mu_carry)
            dxc_head[...] = jnp.zeros_like(dxc_head)
            for ref in (dcw_ref, dcb_ref, dwa_ref, dwx_ref, dba_ref, dbx_ref, dlam_ref):
                ref[...] = jnp.zeros_like(ref)

        xr = jnp.concatenate([x0[...], x1[...]], axis=1)
        tail = jnp.where(first_tile, 0.0, jnp.concatenate([xh0[...], xh1[...]], axis=1))
        lam_v = lam_ref[...]
        sp = _softplus(-lam_v)
        reset = pos_ref[...] == 0
        cw = cw_ref[...]
        xs, xc, xcb, r, i, a_raw, mult_raw, a, mult = _rnn_recompute(
            xbuf, xr, tail, cw, cb_ref[...], wa_ref, wx_ref, ba_ref[...], bx_ref[...], sp, reset)
        g = jnp.concatenate([g0[...], g1[...]], axis=1)
        sg = _sigmoid(g)
        dy = dy_ref[...]
        h = h_ref[...]
        d_g = dy * h * (sg * (1.0 + g * (1.0 - sg)))
        abuf[...] = a
        gbuf[...] = dy * (g * sg)
        top = _scan_backward(abuf, gbuf, lbuf, mu_carry[0:1, :], tr)
        mu_carry[...] = jnp.broadcast_to(top, mu_carry.shape)
        lam_t = lbuf[...]
        hbuf[0:8, :] = jnp.where(first_tile, 0.0, hh_ref[...])
        hbuf[8:tr + 8, :] = h
        h_prev = hbuf[pl.ds(7, tr), :]
        live = jnp.logical_not(reset)
        d_a = jnp.where(live, lam_t * h_prev, 0.0)
        d_mult = jnp.where(live, lam_t * (i * xc), 0.0)
        d_ixc = lam_t * mult
        d_i = d_ixc * xc
        d_xc = d_ixc * i
        d_log_a = d_a * a_raw - d_mult * (a_raw * a_raw / mult_raw)
        d_log_a = jnp.where(live, d_log_a, 0.0)
        d_za = d_log_a * (-LRU_C * sp) * (r * (1.0 - r))
        d_zx = d_i * (i * (1.0 - i))
        dlam_ref[...] += jnp.sum(d_log_a * r, axis=0, keepdims=True) * (LRU_C * _sigmoid(-lam_v))
        dba_ref[...] += jnp.sum(d_za, axis=0, keepdims=True)
        dbx_ref[...] += jnp.sum(d_zx, axis=0, keepdims=True)
        dzab, dzxb = d_za.astype(BF16), d_zx.astype(BF16)
        back = []
        for j in range(RNN_BLOCKS):
            sl = slice(RNN_BW * j, RNN_BW * (j + 1))
            dwa_ref[j] += _dot_tn(xcb[:, sl], dzab[:, sl])
            dwx_ref[j] += _dot_tn(xcb[:, sl], dzxb[:, sl])
            back.append(_dot_nt(dzab[:, sl], wa_ref[j]) + _dot_nt(dzxb[:, sl], wx_ref[j]))
        d_xc = d_xc + jnp.concatenate(back, axis=1)
        dcb_ref[...] += jnp.sum(d_xc, axis=0, keepdims=True)
        for k in range(CONV_W):
            dcw_ref[k:k + 1, :] += jnp.sum(d_xc * xs[k], axis=0, keepdims=True)
        dbuf[0:tr, :] = d_xc
        dbuf[tr:tr + 8, :] = dxc_head[...]
        d_xr = d_xc * cw[CONV_W - 1:CONV_W, :]
        for k in range(CONV_W - 1):
            d_xr = d_xr + dbuf[pl.ds(CONV_W - 1 - k, tr), :] * cw[k:k + 1, :]
        dxc_head[...] = d_xc[0:8, :]
        db_ref[:, 0:D_MODEL] = d_xr.astype(BF16)
        db_ref[:, D_MODEL:2 * D_MODEL] = d_g.astype(BF16)

    rev = lambda s: nt - 1 - s
    blk = lambda cb: pl.BlockSpec((tr, CB), lambda s, cb=cb: (rev(s), cb))
    halo = lambda w, cb: pl.BlockSpec((8, w), lambda s, cb=cb: (jnp.maximum(rev(s) * hb - 1, 0), cb))
    tok = lambda w: pl.BlockSpec((tr, w), lambda s: (rev(s), 0))
    row = lambda w: pl.BlockSpec((1, w), lambda s: (0, 0))
    full3 = pl.BlockSpec((RNN_BLOCKS, RNN_BW, RNN_BW), lambda s: (0, 0, 0))
    cwspec = pl.BlockSpec((CONV_W, D_MODEL), lambda s: (0, 0))
    vec = jax.ShapeDtypeStruct((1, D_MODEL), F32)
    gate_w = jax.ShapeDtypeStruct((RNN_BLOCKS, RNN_BW, RNN_BW), F32)
    out_shape = (jax.ShapeDtypeStruct((T, 2 * D_MODEL), BF16), jax.ShapeDtypeStruct((CONV_W, D_MODEL), F32), vec,
                 gate_w, gate_w, vec, vec, vec)
    big = lambda: pltpu.VMEM((tr, D_MODEL), F32)
    ext = lambda: pltpu.VMEM((tr + 8, D_MODEL), F32)
    return pl.pallas_call(
        body, out_shape=out_shape, grid=(nt,), name="rnn_backward",
        in_specs=[blk(CB_XR), blk(CB_XR + 1), halo(CB, CB_XR), halo(CB, CB_XR + 1), blk(CB_GR), blk(CB_GR + 1),
                  pl.BlockSpec((tr, 1), lambda s: (rev(s), 0)), tok(D_MODEL), halo(D_MODEL, 0), tok(D_MODEL),
                  cwspec, row(D_MODEL), full3, full3, row(D_MODEL), row(D_MODEL), row(D_MODEL)],
        out_specs=(tok(2 * D_MODEL), cwspec, row(D_MODEL), full3, full3, row(D_MODEL), row(D_MODEL), row(D_MODEL)),
        scratch_shapes=[ext(), ext(), ext(), big(), big(), big(), pltpu.VMEM((8, D_MODEL), F32), pltpu.VMEM((8, D_MODEL), F32)],
        compiler_params=_cp("arbitrary"),
    )(proj, proj, proj, proj, proj, proj, pos_col, h_rnn, h_rnn, d_y, conv_w, conv_b, rwa, rwx, ba, bx, lam)


def _input_backward(pieces, w_in, x, dx2, mod_row, norm_g):
    T = x.shape[0]
    tm = min(T, 256)
    n = len(pieces)

    def body(*refs):
        d_refs = refs[:n]
        w_ref, x_ref, dx2_ref, mod_ref, g_ref, gx_ref, dshift_ref, dscale_ref, dg_ref = refs[n:]
        i = pl.program_id(0)
        dh = None
        for d_ref, (_, start, count) in zip(d_refs, pieces):
            part = _dot_nt(d_ref[...], w_ref[:, start * CB:(start + count) * CB])
            dh = part if dh is None else dh + part

        @pl.when(i == 0)
        def _():
            dshift_ref[...] = jnp.zeros_like(dshift_ref)
            dscale_ref[...] = jnp.zeros_like(dscale_ref)
            dg_ref[...] = jnp.zeros_like(dg_ref)

        xf = x_ref[...]
        r1 = _rms(xf)
        xn = xf * r1
        gn = g_ref[...]
        s1 = 1.0 + mod_ref[:, D_MODEL:2 * D_MODEL]
        dshift_ref[...] += jnp.sum(dh, axis=0, keepdims=True)
        dscale_ref[...] += jnp.sum(dh * (xn * gn), axis=0, keepdims=True)
        dg_ref[...] += jnp.sum(dh * s1 * xn, axis=0, keepdims=True)
        dxn = dh * s1 * gn
        gx_ref[...] = dx2_ref[...] + r1 * (dxn - xn * jnp.mean(dxn * xn, axis=-1, keepdims=True))

    tok = lambda w: pl.BlockSpec((tm, w), lambda i: (i, 0))
    row = lambda w: pl.BlockSpec((1, w), lambda i: (0, 0))
    vec = jax.ShapeDtypeStruct((1, D_MODEL), F32)
    return pl.pallas_call(
        body, out_shape=(jax.ShapeDtypeStruct((T, D_MODEL), F32), vec, vec, vec), grid=(T // tm,), name="input_backward",
        in_specs=[tok(c * CB) for _, _, c in pieces]
        + [pl.BlockSpec((D_MODEL, IN_W), lambda i: (0, 0), pipeline_mode=pl.Buffered(1)), tok(D_MODEL), tok(D_MODEL),
           row(ADA_W), row(D_MODEL)],
        out_specs=(tok(D_MODEL), row(D_MODEL), row(D_MODEL), row(D_MODEL)),
        compiler_params=_cp("arbitrary"),
    )(*[p[0] for p in pieces], w_in, x, dx2, mod_row, norm_g)


def _weight_grad(a, pieces, tag):
    T, M = a.shape
    n_blocks = sum(count for _, _, count in pieces)
    n = len(pieces)

    def body(*refs):
        a_ref, b_refs, o_ref = refs[0], refs[1:1 + n], refs[-1]
        j = pl.program_id(0)
        for b_ref, (_, start, count) in zip(b_refs, pieces):
            @pl.when((j >= start) & (j < start + count))
            def _(b_ref=b_ref):
                o_ref[...] = _dot_tn(a_ref[...], b_ref[...])

    def piece_spec(start, count):
        return pl.BlockSpec((T, CB), lambda j: (0, jnp.clip(j - start, 0, count - 1)))

    return pl.pallas_call(
        body, out_shape=jax.ShapeDtypeStruct((M, n_blocks * CB), F32), grid=(n_blocks,), name=f"weight_grad_{tag}",
        in_specs=[pl.BlockSpec((T, M), lambda j: (0, 0), pipeline_mode=pl.Buffered(1))] + [piece_spec(s, c) for _, s, c in pieces],
        out_specs=pl.BlockSpec((M, CB), lambda j: (0, j)), compiler_params=_cp("arbitrary"),
    )(a, *[p[0] for p in pieces])


def _adamw(w, g, m, v):
    m = ADAM_B1 * m + (1.0 - ADAM_B1) * g
    v = ADAM_B2 * v + (1.0 - ADAM_B2) * (g * g)
    m_hat = m / (1.0 - ADAM_B1 ** ADAM_STEP)
    v_hat = v / (1.0 - ADAM_B2 ** ADAM_STEP)
    delta = -ADAM_LR * (m_hat / (jnp.sqrt(v_hat) + ADAM_EPS) + ADAM_WD * w)
    return delta, m, v


def _sum_landed(kind, own, land, where, tag):
    if kind == "in":
        R, C = land.shape[1:]
        tr = 256
        grid = (R // tr,)
        own_spec = pl.BlockSpec((tr, C), lambda i, w: (i, w[0]))
        land_spec = pl.BlockSpec((3, tr, C), lambda i, w: (0, i, 0))
        out_spec = pl.BlockSpec((1, tr, C), lambda i, w: (w[1], i, 0))
        out_shape = (2, R, C)
        pick = lambda ref: ref[...]
    elif kind == "sq":
        R, C = land.shape[1:]
        grid = (1,)
        own_spec = pl.BlockSpec((1, R, C), lambda i, w: (w[0], 0, 0))
        land_spec = pl.BlockSpec((3, R, C), lambda i, w: (0, 0, 0))
        out_spec = pl.BlockSpec((1, R, C), lambda i, w: (w[1], 0, 0))
        out_shape = (2, R, C)
        pick = lambda ref: ref[0]
    else:
        B, R, C = land.shape[1:]
        grid = (1,)
        own_spec = pl.BlockSpec((B, 1, R, C), lambda i, w: (0, w[0], 0, 0))
        land_spec = pl.BlockSpec((3, B, R, C), lambda i, w: (0, 0, 0, 0))
        out_spec = pl.BlockSpec((B, 1, R, C), lambda i, w: (0, w[1], 0, 0))
        out_shape = (B, 2, R, C)
        pick = lambda ref: ref[:, 0]

    def body(w_ref, own_ref, l_ref, o_ref):
        total = ((pick(own_ref) + l_ref[0].astype(F32)) + l_ref[1].astype(F32)) + l_ref[2].astype(F32)
        if kind == "in":
            o_ref[0] = total
        elif kind == "sq":
            o_ref[0] = total
        else:
            o_ref[:, 0] = total

    grid_spec = pltpu.PrefetchScalarGridSpec(num_scalar_prefetch=1, grid=grid, in_specs=[own_spec, land_spec], out_specs=out_spec)
    return pl.pallas_call(
        body, out_shape=jax.ShapeDtypeStruct(out_shape, F32), grid_spec=grid_spec, name=f"sum_landed_{tag}",
        compiler_params=_cp("parallel"),
    )(where, own, land)


def _adamw_shard(g, w, m, v, tag):
    R, C = w.shape
    tr = min(R, 256)

    def body(g_ref, w_ref, m_ref, v_ref, d_ref, nm_ref, nv_ref):
        d, nm, nv = _adamw(w_ref[...], g_ref[...], m_ref[...], v_ref[...])
        d_ref[...] = d
        nm_ref[...] = nm
        nv_ref[...] = nv

    spec = pl.BlockSpec((tr, C), lambda i: (i, 0))
    sds = jax.ShapeDtypeStruct((R, C), F32)
    return pl.pallas_call(
        body, out_shape=(sds,) * 3, grid=(R // tr,), name=f"adamw_{tag}",
        in_specs=[spec] * 4, out_specs=(spec,) * 3, compiler_params=_cp("parallel"),
    )(g, w, m, v)


def _adamw_w_ada(c_t, dmod_cols, w, m, v):
    R, C = w.shape

    def body(ct_ref, dm_ref, w_ref, m_ref, v_ref, g_ref, d_ref, nm_ref, nv_ref):
        g = _dot(ct_ref[...].astype(BF16), dm_ref[...].astype(BF16))
        d, nm, nv = _adamw(w_ref[...], g, m_ref[...], v_ref[...])
        g_ref[...] = g
        d_ref[...] = d
        nm_ref[...] = nm
        nv_ref[...] = nv

    tr = 256
    spec = pl.BlockSpec((tr, C), lambda i: (i, 0))
    sds = jax.ShapeDtypeStruct((R, C), F32)
    return pl.pallas_call(
        body, out_shape=(sds,) * 4, grid=(R // tr,), name="adamw_w_ada",
        in_specs=[pl.BlockSpec((tr, 128), lambda i: (i, 0)), pl.BlockSpec((128, C), lambda i: (0, 0))] + [spec] * 3,
        out_specs=(spec,) * 4, compiler_params=_cp("parallel"),
    )(c_t, dmod_cols, w, m, v)


def _adamw_small(small_all, ws, ms, vs):
    def body(s_ref, w_ref, m_ref, v_ref, g_ref, d_ref, nm_ref, nv_ref):
        g = s_ref[0]
        for b in range(1, N_DEV):
            g = g + s_ref[b]
        d, nm, nv = _adamw(w_ref[...], g, m_ref[...], v_ref[...])
        g_ref[...] = g
        d_ref[...] = d
        nm_ref[...] = nm
        nv_ref[...] = nv

    sds = jax.ShapeDtypeStruct((SMALL_ROWS, D_MODEL), F32)
    return pl.pallas_call(
        body, out_shape=(sds,) * 4, name="adamw_small", in_specs=[VMEM_SPEC] * 4, out_specs=(VMEM_SPEC,) * 4,
        compiler_params=pltpu.CompilerParams(vmem_limit_bytes=VMEM_LIMIT_V7X),
    )(small_all, ws, ms, vs)


ROW_MOD, ROW_NORM_G, ROW_CONV_B, ROW_BA, ROW_BX, ROW_LAM, ROW_FINAL_G, ROW_SINKS, ROW_CONV_W = 0, 3, 4, 5, 6, 7, 8, 9, 10


def _pack_small(b_ada, norm_g, conv_b, ba, bx, lam, final_g, sinks, conv_w_full):
    rows = [b_ada.reshape(3, D_MODEL), norm_g, conv_b, ba, bx, lam, final_g.reshape(1, D_MODEL),
            jnp.pad(sinks.reshape(1, -1), ((0, 0), (0, D_MODEL - sinks.size))), conv_w_full,
            jnp.zeros((SMALL_ROWS - 14, D_MODEL), F32)]
    return jnp.concatenate([r.astype(F32) for r in rows], axis=0)


def kernel(x, c, positions, w_ada, b_ada, norm_g, w_in, attn_sinks, conv_w, conv_b, rg_wa, rg_ba, rg_wx, rg_bx, rg_lambda, w_attn_proj, w_rnn_proj, w_out, final_g, loss_target, m_w_ada, m_b_ada, m_norm_g, m_w_in, m_attn_sinks, m_conv_w, m_conv_b, m_rg_wa, m_rg_ba, m_rg_wx, m_rg_bx, m_rg_lambda, m_w_attn_proj, m_w_rnn_proj, m_w_out, m_final_g, v_w_ada, v_b_ada, v_norm_g, v_w_in, v_attn_sinks, v_conv_w, v_conv_b, v_rg_wa, v_rg_ba, v_rg_wx, v_rg_bx, v_rg_lambda, v_w_attn_proj, v_w_rnn_proj, v_w_out, v_final_g):
    T = x.shape[1]
    my_chip = lax.axis_index("x") * 2 + lax.axis_index("y")
    my_dev = my_chip * 2 + lax.axis_index("c")
    x2d, tgt = x[0], loss_target[0]
    pos_col = positions.reshape(T, 1)

    chip_idx = my_chip.reshape(1).astype(jnp.int32)
    c_idx = lax.axis_index("c").reshape(1).astype(jnp.int32)
    sq_place = ((D_MODEL, D_MODEL), (SHARD_ROWS, D_MODEL), lambda chip: (chip, 0))
    rg_place = ((RNN_BLOCKS, RNN_BW, RNN_BW), (RNN_BLOCKS, SHARD_RG, RNN_BW), lambda chip: (0, chip, 0))
    gathered = _gather_weights(
        c.reshape(1, 1, D_MODEL), w_ada[0],
        _cast_place(w_in[0], chip_idx, (D_MODEL, IN_W), (D_MODEL, SHARD_IN), lambda chip: (0, chip), "w_in"),
        _cast_place(w_attn_proj[0], chip_idx, *sq_place, "w_attn_proj"),
        _cast_place(w_rnn_proj[0], chip_idx, *sq_place, "w_rnn_proj"),
        _cast_place(w_out[0], chip_idx, *sq_place, "w_out"),
        _cast_place(rg_wa[0], chip_idx, *rg_place, "rg_wa"),
        _cast_place(rg_wx[0], chip_idx, *rg_place, "rg_wx"),
        conv_w[0])
    w_in_f = gathered[0].reshape(D_MODEL, IN_W)
    wap_f, wrp_f, wo_f = (g.reshape(D_MODEL, D_MODEL) for g in gathered[1:4])
    rwa_f, rwx_f = (g.reshape(RNN_BLOCKS, RNN_BW, RNN_BW) for g in gathered[4:6])
    cw_chips, c_all, mod_chips = gathered[6:]
    conv_w_f = jnp.transpose(cw_chips, (1, 0, 2)).reshape(CONV_W, D_MODEL)
    mod_all = jnp.transpose(mod_chips, (1, 0, 2)).reshape(N_DEV, ADA_W) + b_ada
    mod_row = lax.dynamic_slice_in_dim(mod_all, my_dev, 1, axis=0)

    tabs = _rope_tables(pos_col)
    h = _prenorm(x2d, mod_row, norm_g)
    proj = _in_projection(h, w_in_f)
    y_attn = _attn_forward(proj, tabs, attn_sinks)
    y_rnn, h_rnn = _rnn_forward(proj, pos_col, conv_w_f, conv_b, rwa_f, rwx_f, rg_ba, rg_bx, rg_lambda)
    (dx2, merged, d_o, d_pa, d_pr, d_ya, d_yr, d_c, d_final_g, d_gate, loss_vec) = _merge_and_head(
        x2d, tgt, y_attn, y_rnn, proj, wap_f, wrp_f, wo_f, mod_row, final_g.reshape(1, D_MODEL))

    sq = (N_CHIPS, 2, SHARD_ROWS // 2, D_MODEL)
    rg = (RNN_BLOCKS, N_CHIPS, 2, SHARD_RG // 2, RNN_BW)
    rg_flat = (RNN_BLOCKS * N_CHIPS, 2, SHARD_RG // 2, RNN_BW)

    def chip_sum_and_start(views, axes, flat, unflat, tags_, kinds_, group):
        from_sib = _swap_halves(views, axes)
        sums = [_presum(v.reshape(f), s.reshape(f[:1] + f[2:]), c_idx, t) for v, s, f, t in zip(views, from_sib, flat, tags_)]
        exact = [s[0].reshape(u) for s, u in zip(sums, unflat)]
        rounded = [s[1].reshape(u) for s, u in zip(sums, unflat)]
        return _exchange_start(rounded, kinds_, group), exact

    g_ap = _weight_grad(y_attn, [(d_pa, 0, 2)], "w_attn_proj")
    g_rp = _weight_grad(y_rnn, [(d_pr, 0, 2)], "w_rnn_proj")
    g_o = _weight_grad(merged, [(d_o, 0, 2)], "w_out")
    sq_half = (N_CHIPS, SHARD_ROWS // 2, D_MODEL)
    started1, own1 = chip_sum_and_start([g_ap.reshape(sq), g_rp.reshape(sq), g_o.reshape(sq)], [1, 1, 1], [sq] * 3, [sq_half] * 3,
                                  ["w_attn_proj", "w_rnn_proj", "w_out"], ["sq"] * 3, "proj")
    d_q, d_kv, d_ga, d_sinks = _attn_backward(proj, d_ya, tabs, attn_sinks + started1[4][0, 0])
    d_b, d_conv_w, d_conv_b, d_rwa, d_rwx, d_ba, d_bx, d_lam = _rnn_backward(
        proj, pos_col, h_rnn, d_yr, conv_w_f, conv_b, rwa_f, rwx_f, rg_ba, rg_bx, rg_lambda)
    pieces = [(d_q, CB_Q, 2), (d_kv, CB_KV, 1), (d_ga, CB_GA, 2), (d_b, CB_XR, 4), (d_c, CB_MA, 4)]
    g_in = _weight_grad(h, pieces, "w_in")
    started2, own2 = chip_sum_and_start(
        [g_in.reshape(2, D_MODEL // 2, IN_W), d_rwa.reshape(rg), d_rwx.reshape(rg)], [0, 2, 2],
        [(1, 2, D_MODEL // 2, IN_W), rg_flat, rg_flat],
        [(D_MODEL // 2, IN_W), (RNN_BLOCKS, N_CHIPS, SHARD_RG // 2, RNN_BW), (RNN_BLOCKS, N_CHIPS, SHARD_RG // 2, RNN_BW)],
        ["w_in", "rg_wa", "rg_wx"], ["in", "rg", "rg"], "in")
    grad_x, d_shift, d_scale, d_norm_g = _input_backward(pieces, w_in_f, x2d, dx2, mod_row + started2[4][0, 0], norm_g)

    d_mod = jnp.concatenate([d_shift, d_scale, d_gate], axis=1)
    small = _pack_small(d_mod, d_norm_g, d_conv_b, d_ba, d_bx, d_lam, d_final_g, d_sinks[:, :N_HEADS], d_conv_w)
    small_all = _gather_small(small)
    _, lands1 = _exchange_wait(*started1[:4], grad_x, "proj")
    _, lands2 = _exchange_wait(*started2[:4], grad_x, "in")
    tags = ["w_in", "w_attn_proj", "w_rnn_proj", "w_out", "rg_wa", "rg_wx"]
    chip_sums = [own2[0]] + list(own1) + list(own2[1:])
    lands = [lands2[0]] + list(lands1) + list(lands2[1:])
    where = jnp.concatenate([chip_idx, c_idx])
    kinds = ["in", "sq", "sq", "sq", "rg", "rg"]
    halves = [_sum_landed(kinds[i], chip_sums[i], lands[i], where, tags[i]) for i in range(6)]
    grads = _assemble_with_sibling(halves, [0, 0, 0, 0, 1, 1])
    shapes2d = [(D_MODEL, SHARD_IN), (SHARD_ROWS, D_MODEL), (SHARD_ROWS, D_MODEL), (SHARD_ROWS, D_MODEL),
                (RNN_BLOCKS * SHARD_RG, RNN_BW), (RNN_BLOCKS * SHARD_RG, RNN_BW)]
    big_w = [w_in, w_attn_proj, w_rnn_proj, w_out, rg_wa, rg_wx]
    big_m = [m_w_in, m_w_attn_proj, m_w_rnn_proj, m_w_out, m_rg_wa, m_rg_wx]
    big_v = [v_w_in, v_w_attn_proj, v_w_rnn_proj, v_w_out, v_rg_wa, v_rg_wx]
    res = {}
    for i, tag in enumerate(tags):
        g = grads[i].reshape(shapes2d[i])
        outs = _adamw_shard(g, big_w[i].reshape(shapes2d[i]), big_m[i].reshape(shapes2d[i]), big_v[i].reshape(shapes2d[i]), tag)
        res[tag] = [o.reshape(big_w[i].shape) for o in (g,) + tuple(outs)]

    dmod_all = small_all[:, ROW_MOD:ROW_MOD + 3, :].reshape(N_DEV, ADA_W)
    dmod_cols = lax.dynamic_slice_in_dim(dmod_all, my_chip * SHARD_ADA, SHARD_ADA, axis=1)
    c_t = jnp.pad(jnp.transpose(c_all.reshape(N_DEV, D_MODEL)), ((0, 0), (0, 128 - N_DEV)))
    dmod_cols = jnp.pad(dmod_cols, ((0, 128 - N_DEV), (0, 0)))
    res["w_ada"] = [o.reshape(w_ada.shape) for o in _adamw_w_ada(c_t, dmod_cols, w_ada[0], m_w_ada[0], v_w_ada[0])]

    def full_conv(a):
        return lax.dynamic_update_slice_in_dim(jnp.zeros((CONV_W, D_MODEL), F32), a[0], my_chip * (D_MODEL // N_CHIPS), axis=1)

    packed = [_pack_small(p[0], p[1], p[2], p[3], p[4], p[5], p[6], p[7], full_conv(p[8])) for p in (
        (b_ada, norm_g, conv_b, rg_ba, rg_bx, rg_lambda, final_g, attn_sinks, conv_w),
        (m_b_ada, m_norm_g, m_conv_b, m_rg_ba, m_rg_bx, m_rg_lambda, m_final_g, m_attn_sinks, m_conv_w),
        (v_b_ada, v_norm_g, v_conv_b, v_rg_ba, v_rg_bx, v_rg_lambda, v_final_g, v_attn_sinks, v_conv_w))]
    small_out = _adamw_small(small_all, *packed)

    def unpack(slab):
        cw = lax.dynamic_slice_in_dim(slab[ROW_CONV_W:ROW_CONV_W + CONV_W], my_chip * (D_MODEL // N_CHIPS),
                                      D_MODEL // N_CHIPS, axis=1)
        return {
            "b_ada": slab[ROW_MOD:ROW_MOD + 3].reshape(1, ADA_W), "norm_g": slab[ROW_NORM_G:ROW_NORM_G + 1],
            "conv_b": slab[ROW_CONV_B:ROW_CONV_B + 1], "rg_ba": slab[ROW_BA:ROW_BA + 1], "rg_bx": slab[ROW_BX:ROW_BX + 1],
            "rg_lambda": slab[ROW_LAM:ROW_LAM + 1], "final_g": slab[ROW_FINAL_G], "attn_sinks": slab[ROW_SINKS:ROW_SINKS + 1, :N_HEADS],
            "conv_w": cw[None],
        }

    small_res = [unpack(s) for s in small_out]
    order = ["w_ada", "b_ada", "norm_g", "w_in", "attn_sinks", "conv_w", "conv_b", "rg_wa", "rg_ba", "rg_wx", "rg_bx",
             "rg_lambda", "w_attn_proj", "w_rnn_proj", "w_out", "final_g"]
    loss = lax.psum(loss_vec[0, 0], ("x", "y", "c"))
    outs = [loss, grad_x[None]]
    for kind in range(4):
        for name in order:
            outs.append(res[name][kind] if name in res else small_res[kind][name])
    return tuple(outs)
```

```python
import numpy as np
import jax
import jax.numpy as jnp
from jax import lax
from jax.experimental import pallas as pl
from jax.experimental.pallas import tpu as pltpu

F32 = jnp.float32
BF16 = jnp.bfloat16

D_MODEL = 1024
N_HEADS = 16
N_KV = 4
HEAD_DIM = 64
GROUP = N_HEADS // N_KV
BLOCK = 128
KV_W = N_KV * HEAD_DIM
ROT_HALF = 8
ROPE_THETA = 500000.0
ATTN_SCALE = 0.125
RNN_BLOCKS = 4
RNN_BW = 256
CONV_W = 4
LRU_C = 8.0
NORM_EPS = 1e-6
IN_W = 6656
CB = 512
N_CB = IN_W // CB
CB_Q, CB_KV, CB_GA, CB_XR, CB_GR, CB_MA, CB_MR = 0, 2, 3, 5, 7, 9, 11
N_CHIPS = 4
N_DEV = 8
SHARD_IN = IN_W // N_CHIPS
SHARD_ROWS = D_MODEL // N_CHIPS
SHARD_RG = RNN_BW // N_CHIPS
ADA_W = 3 * D_MODEL
SHARD_ADA = ADA_W // N_CHIPS
SMALL_ROWS = 16

ADAM_LR = 0.001
ADAM_B1 = 0.9
ADAM_B2 = 0.999
ADAM_EPS = 1e-08
ADAM_WD = 0.01
ADAM_STEP = 10

VMEM_LIMIT_V7X = 52 * 1024 * 1024
MESH = pl.DeviceIdType.MESH
ANY = pl.BlockSpec(memory_space=pl.ANY)
VMEM_SPEC = pl.BlockSpec(memory_space=pltpu.VMEM)


def _cp(*sem):
    return pltpu.CompilerParams(dimension_semantics=sem if sem else None, vmem_limit_bytes=VMEM_LIMIT_V7X)


def _dot(a, b):
    return jnp.dot(a, b, preferred_element_type=F32)


def _dot_nt(a, b):
    return lax.dot_general(a, b, (((1,), (1,)), ((), ())), preferred_element_type=F32)


def _dot_tn(a, b):
    return lax.dot_general(a, b, (((0,), (0,)), ((), ())), preferred_element_type=F32)


def _sigmoid(z):
    return 1.0 / (1.0 + jnp.exp(-z))


def _neg_expm1(z):
    series = -(z * (1.0 + z * (0.5 + z * (1.0 / 6.0 + z * (1.0 / 24.0 + z * (1.0 / 120.0))))))
    return jnp.where(z > -0.05, series, 1.0 - jnp.exp(z))


def _softplus(z):
    u = jnp.exp(-jnp.abs(z))
    log1p_u = jnp.where(u < 1e-3, u * (1.0 - u * (0.5 - u * (1.0 / 3.0))), jnp.log(1.0 + u))
    return jnp.maximum(z, 0.0) + log1p_u


def _rms(xf):
    return lax.rsqrt(jnp.mean(xf * xf, axis=-1, keepdims=True) + NORM_EPS)


def _me():
    return lax.axis_index("x"), lax.axis_index("y"), lax.axis_index("c")


def _peer(mask):
    x, y, c = _me()
    fx, fy, fc = (mask >> 2) & 1, (mask >> 1) & 1, mask & 1
    return (x ^ fx if fx else x, y ^ fy if fy else y, c ^ fc if fc else c)


def _chip_of(pos):
    return pos[0] * 2 + pos[1]


CHIP_MASKS = (4, 2, 6)
ALL_MASKS = (1, 2, 3, 4, 5, 6, 7)


HBM_SPEC = pl.BlockSpec(memory_space=pltpu.HBM)
SEM_SPEC = pl.BlockSpec(memory_space=pltpu.SEMAPHORE)
SPLIT_COPY = pltpu.CompilerParams(has_side_effects=pltpu.SideEffectType.DATAFLOW_SIDE_EFFECTING)
N_BIG = 6
FULL_SHAPES = (
    (2, D_MODEL // 2, IN_W),
    (N_CHIPS, 2, SHARD_ROWS // 2, D_MODEL), (N_CHIPS, 2, SHARD_ROWS // 2, D_MODEL), (N_CHIPS, 2, SHARD_ROWS // 2, D_MODEL),
    (RNN_BLOCKS, N_CHIPS, 2, SHARD_RG // 2, RNN_BW), (RNN_BLOCKS, N_CHIPS, 2, SHARD_RG // 2, RNN_BW),
)


def _slot(full, idx, chip, half):
    if idx == 0:
        return full.at[half, :, pl.ds(pl.multiple_of(chip * SHARD_IN, 128), SHARD_IN)]
    return full.at[chip, half] if idx in (1, 2, 3) else full.at[:, chip, half]


def _three_halves(full, idx):
    return full.at[pl.ds(0, 3), 0] if idx in (1, 2, 3) else full.at[:, pl.ds(0, 3), 0]


def _gather_start(fulls):
    def body(*refs):
        full_refs = refs[:N_BIG]
        ssems, rsems = refs[N_BIG:N_BIG + 4], refs[N_BIG + 4:N_BIG + 8]
        token = refs[2 * N_BIG + 8]
        me = _me()
        my_chip = _chip_of(me)
        for idx in range(N_BIG):
            for k, mask in enumerate(CHIP_MASKS):
                pair = k if idx == 0 else 3
                mine = _slot(full_refs[idx], idx, my_chip, me[2])
                pltpu.make_async_remote_copy(src_ref=mine, dst_ref=mine, send_sem=ssems[pair], recv_sem=rsems[pair],
                                             device_id=_peer(mask), device_id_type=MESH).start()
        token[...] = jnp.zeros_like(token)

    sem = pltpu.SemaphoreType.DMA(())
    out_shape = (sem,) * 8 + tuple(pltpu.HBM(f.shape, f.dtype) for f in fulls) + (jax.ShapeDtypeStruct((8, 128), F32),)
    outs = pl.pallas_call(
        body, out_shape=out_shape, name="gather_start",
        in_specs=[HBM_SPEC] * N_BIG, out_specs=tuple([SEM_SPEC] * 8 + [HBM_SPEC] * N_BIG + [VMEM_SPEC]),
        input_output_aliases={i: 8 + i for i in range(N_BIG)}, compiler_params=SPLIT_COPY,
    )(*[pltpu.with_memory_space_constraint(f, pltpu.HBM) for f in fulls])
    return outs[0:4], outs[4:8], outs[8:8 + N_BIG], outs[8 + N_BIG]


def _gather_wait(ssem, rsem, arrays, idxs, after, tag):
    n = len(arrays)

    def body(*refs):
        full_refs, ssem_ref, rsem_ref = refs[:n], refs[n], refs[n + 1]
        me = _me()
        for full, idx in zip(full_refs, idxs):
            region = _slot(full, 0, _chip_of(me), me[2]) if idx == 0 else _three_halves(full, idx)
            arrived = pltpu.make_async_remote_copy(
                src_ref=region, dst_ref=region, send_sem=ssem_ref, recv_sem=rsem_ref, device_id=me, device_id_type=MESH)
            arrived.wait_send()
            arrived.wait_recv()

    outs = pl.pallas_call(
        body, out_shape=tuple(pltpu.HBM(a.shape, a.dtype) for a in arrays), name=f"gather_wait_{tag}",
        in_specs=[HBM_SPEC] * n + [SEM_SPEC, SEM_SPEC, ANY], out_specs=tuple([HBM_SPEC] * n),
        input_output_aliases={i: i for i in range(n)}, compiler_params=SPLIT_COPY,
    )(*arrays, ssem, rsem, after)
    return list(outs)


def _forward_halves(arrays, items, tag):
    n, m = len(arrays), len(items)

    def body(*refs):
        outs, ssem, rsem = refs[n:2 * n], refs[2 * n], refs[2 * n + 1]
        me = _me()
        sib = _peer(1)
        cps = []
        for j, (pos, idx, k) in enumerate(items):
            chip = _chip_of(_peer(CHIP_MASKS[k]))
            cp = pltpu.make_async_remote_copy(
                src_ref=_slot(outs[pos], idx, chip, me[2]), dst_ref=_slot(outs[pos], idx, chip, me[2]),
                send_sem=ssem.at[j], recv_sem=rsem.at[j], device_id=sib, device_id_type=MESH)
            cp.start()
            cps.append(cp)
        for j, (pos, idx, k) in enumerate(items):
            chip = _chip_of(_peer(CHIP_MASKS[k]))
            pltpu.make_async_remote_copy(
                src_ref=_slot(outs[pos], idx, chip, me[2]), dst_ref=_slot(outs[pos], idx, chip, 1 - me[2]),
                send_sem=ssem.at[j], recv_sem=rsem.at[j], device_id=sib, device_id_type=MESH).wait_recv()
        for cp in cps:
            cp.wait_send()

    outs = pl.pallas_call(
        body, out_shape=tuple(jax.ShapeDtypeStruct(a.shape, a.dtype) for a in arrays), name=f"forward_halves_{tag}",
        in_specs=[ANY] * n, out_specs=tuple([ANY] * n), input_output_aliases={i: i for i in range(n)},
        scratch_shapes=[pltpu.SemaphoreType.DMA((m,)), pltpu.SemaphoreType.DMA((m,))],
    )(*arrays)
    return list(outs)


def _gather_mod(c_row, w_ada_s, conv_w_s):
    def body(c_ref, wada_ref, cw_s, cw_f, call_ref, mod_ref, wsend, wrecv, lsem, csend, crecv, msend, mrecv):
        me = _me()
        my_chip = _chip_of(me)
        my_dev = my_chip * 2 + me[2]
        sends = []
        for k, mask in enumerate(CHIP_MASKS):
            cp = pltpu.make_async_remote_copy(src_ref=cw_s, dst_ref=cw_f.at[my_chip], send_sem=wsend.at[k], recv_sem=wrecv.at[k],
                                              device_id=_peer(mask), device_id_type=MESH)
            cp.start()
            sends.append(cp)
        local = [pltpu.make_async_copy(cw_s, cw_f.at[my_chip], lsem.at[0])]
        for cp in local:
            cp.start()

        call_ref[my_dev] = c_ref[0]
        csends = []
        for k, mask in enumerate(ALL_MASKS):
            cp = pltpu.make_async_remote_copy(
                src_ref=c_ref.at[0], dst_ref=call_ref.at[my_dev],
                send_sem=csend.at[k], recv_sem=crecv.at[k], device_id=_peer(mask), device_id_type=MESH)
            cp.start()
            csends.append(cp)
        for k, mask in enumerate(ALL_MASKS):
            frm = _peer(mask)
            pltpu.make_async_remote_copy(
                src_ref=c_ref.at[0], dst_ref=call_ref.at[_chip_of(frm) * 2 + frm[2]],
                send_sem=csend.at[k], recv_sem=crecv.at[k], device_id=frm, device_id_type=MESH).wait_recv()
        for cp in csends:
            cp.wait_send()

        c_all = call_ref[...].reshape(N_DEV, D_MODEL).astype(BF16)
        mod_ref[my_chip] = _dot(c_all, wada_ref[...].astype(BF16))
        msends = []
        for k, mask in enumerate(CHIP_MASKS):
            cp = pltpu.make_async_remote_copy(
                src_ref=mod_ref.at[my_chip], dst_ref=mod_ref.at[my_chip],
                send_sem=msend.at[k], recv_sem=mrecv.at[k], device_id=_peer(mask), device_id_type=MESH)
            cp.start()
            msends.append(cp)
        for k, mask in enumerate(CHIP_MASKS):
            frm = _peer(mask)
            pltpu.make_async_remote_copy(
                src_ref=mod_ref.at[my_chip], dst_ref=mod_ref.at[_chip_of(frm)],
                send_sem=msend.at[k], recv_sem=mrecv.at[k], device_id=frm, device_id_type=MESH).wait_recv()
        for cp in msends:
            cp.wait_send()

        for k, mask in enumerate(CHIP_MASKS):
            frm = _peer(mask)
            pltpu.make_async_remote_copy(src_ref=cw_s, dst_ref=cw_f.at[_chip_of(frm)], send_sem=wsend.at[k], recv_sem=wrecv.at[k],
                                         device_id=frm, device_id_type=MESH).wait_recv()
        for cp in sends:
            cp.wait_send()
        for cp in local:
            cp.wait()

    out_shape = (
        jax.ShapeDtypeStruct((N_CHIPS, CONV_W, D_MODEL // N_CHIPS), F32),
        jax.ShapeDtypeStruct((N_DEV, 1, D_MODEL), F32),
        jax.ShapeDtypeStruct((N_CHIPS, N_DEV, SHARD_ADA), F32),
    )
    return pl.pallas_call(
        body, out_shape=out_shape, name="gather_mod",
        in_specs=[VMEM_SPEC, VMEM_SPEC, ANY], out_specs=(ANY, VMEM_SPEC, VMEM_SPEC),
        scratch_shapes=[
            pltpu.SemaphoreType.DMA((3,)), pltpu.SemaphoreType.DMA((3,)), pltpu.SemaphoreType.DMA((1,)),
            pltpu.SemaphoreType.DMA((7,)), pltpu.SemaphoreType.DMA((7,)),
            pltpu.SemaphoreType.DMA((3,)), pltpu.SemaphoreType.DMA((3,)),
        ],
        compiler_params=pltpu.CompilerParams(vmem_limit_bytes=VMEM_LIMIT_V7X),
    )(c_row, w_ada_s, conv_w_s)


def _cast_place(shard, chip_idx, full_shape, block, index_map, tag):
    def body(chip_ref, s_ref, o_ref):
        o_ref[...] = s_ref[...].astype(BF16)

    grid_spec = pltpu.PrefetchScalarGridSpec(
        num_scalar_prefetch=1, grid=(1,),
        in_specs=[pl.BlockSpec(shard.shape, lambda i, chip_ref: (0,) * shard.ndim)],
        out_specs=pl.BlockSpec(block, lambda i, chip_ref: index_map(chip_ref[0])))
    return pl.pallas_call(
        body, out_shape=jax.ShapeDtypeStruct(full_shape, BF16), grid_spec=grid_spec, name=f"cast_place_{tag}",
        compiler_params=_cp("arbitrary"),
    )(chip_idx, shard)


def _shard_of(ref, kind, chip):
    if kind == "in":
        return ref.at[:, pl.ds(pl.multiple_of(chip * SHARD_IN, 128), SHARD_IN)]
    return ref.at[chip] if kind == "sq" else ref.at[:, chip]


def _land_shape(src, kind):
    if kind == "in":
        return (3, src.shape[0], SHARD_IN)
    return (3,) + src.shape[1:] if kind == "sq" else (3, src.shape[0]) + src.shape[2:]


def _exchange_start(srcs, kinds, tag):
    n = len(srcs)
    lands = [pltpu.with_memory_space_constraint(lax.empty(_land_shape(s, k), s.dtype), pltpu.HBM) for s, k in zip(srcs, kinds)]

    def body(*refs):
        src_refs, land_refs = refs[:n], refs[n:2 * n]
        ssems, rsems = refs[2 * n:3 * n], refs[3 * n:4 * n]
        token = refs[6 * n]
        for i in range(n):
            for k, mask in enumerate(CHIP_MASKS):
                to = _peer(mask)
                pltpu.make_async_remote_copy(
                    src_ref=_shard_of(src_refs[i], kinds[i], _chip_of(to)), dst_ref=land_refs[i].at[k],
                    send_sem=ssems[i], recv_sem=rsems[i], device_id=to, device_id_type=MESH).start()
        token[...] = jnp.zeros_like(token)

    sem = pltpu.SemaphoreType.DMA(())
    out_shape = ((sem,) * (2 * n) + tuple(pltpu.HBM(s.shape, s.dtype) for s in srcs)
                 + tuple(pltpu.HBM(l.shape, l.dtype) for l in lands) + (jax.ShapeDtypeStruct((8, 128), F32),))
    outs = pl.pallas_call(
        body, out_shape=out_shape, name=f"exchange_start_{tag}",
        in_specs=[HBM_SPEC] * (2 * n), out_specs=tuple([SEM_SPEC] * (2 * n) + [HBM_SPEC] * (2 * n) + [VMEM_SPEC]),
        input_output_aliases={i: 2 * n + i for i in range(2 * n)},
        compiler_params=pltpu.CompilerParams(has_side_effects=pltpu.SideEffectType.DATAFLOW_SIDE_EFFECTING),
    )(*[pltpu.with_memory_space_constraint(s, pltpu.HBM) for s in srcs], *lands)
    return outs[:n], outs[n:2 * n], outs[2 * n:3 * n], outs[3 * n:4 * n], outs[4 * n]


def _exchange_wait(ssems, rsems, srcs, lands, after, tag):
    n = len(srcs)

    def body(*refs):
        land_refs = refs[n:2 * n]
        ssem_refs, rsem_refs = refs[2 * n:3 * n], refs[3 * n:4 * n]
        for i in range(n):
            all_three = pltpu.make_async_remote_copy(
                src_ref=land_refs[i], dst_ref=land_refs[i], send_sem=ssem_refs[i], recv_sem=rsem_refs[i],
                device_id=_me(), device_id_type=MESH)
            all_three.wait_send()
            all_three.wait_recv()

    outs = pl.pallas_call(
        body, out_shape=tuple(pltpu.HBM(a.shape, a.dtype) for a in list(srcs) + list(lands)), name=f"exchange_wait_{tag}",
        in_specs=[HBM_SPEC] * (2 * n) + [SEM_SPEC] * (2 * n) + [ANY], out_specs=tuple([HBM_SPEC] * (2 * n)),
        input_output_aliases={i: i for i in range(2 * n)},
        compiler_params=pltpu.CompilerParams(has_side_effects=pltpu.SideEffectType.DATAFLOW_SIDE_EFFECTING),
    )(*srcs, *lands, *ssems, *rsems, after)
    return outs[:n], outs[n:]


def _gather_small(small):
    def body(small_ref, small_all, ssend, srecv):
        me = _me()
        my_dev = _chip_of(me) * 2 + me[2]
        small_all[my_dev] = small_ref[...]
        ssends = []
        for k, mask in enumerate(ALL_MASKS):
            cp = pltpu.make_async_remote_copy(
                src_ref=small_ref, dst_ref=small_all.at[my_dev],
                send_sem=ssend.at[k], recv_sem=srecv.at[k], device_id=_peer(mask), device_id_type=MESH)
            cp.start()
            ssends.append(cp)
        for k, mask in enumerate(ALL_MASKS):
            frm = _peer(mask)
            pltpu.make_async_remote_copy(
                src_ref=small_ref, dst_ref=small_all.at[_chip_of(frm) * 2 + frm[2]],
                send_sem=ssend.at[k], recv_sem=srecv.at[k], device_id=frm, device_id_type=MESH).wait_recv()
        for cp in ssends:
            cp.wait_send()

    return pl.pallas_call(
        body, out_shape=jax.ShapeDtypeStruct((N_DEV, SMALL_ROWS, D_MODEL), F32), name="gather_small",
        in_specs=[VMEM_SPEC], out_specs=VMEM_SPEC,
        scratch_shapes=[pltpu.SemaphoreType.DMA((7,)), pltpu.SemaphoreType.DMA((7,))],
    )(small)


def _half_of(ref, axis, half):
    return ref.at[(slice(None),) * axis + (half,)]


def _swap_halves(parts, axes):
    n = len(parts)

    def body(*refs):
        ins, outs, ssem, rsem = refs[:n], refs[n:2 * n], refs[2 * n], refs[2 * n + 1]
        c = lax.axis_index("c")
        cps = [pltpu.make_async_remote_copy(src_ref=_half_of(ins[i], axes[i], 1 - c), dst_ref=outs[i], send_sem=ssem.at[i],
                                            recv_sem=rsem.at[i], device_id=_peer(1), device_id_type=MESH) for i in range(n)]
        for cp in cps:
            cp.start()
        for cp in cps:
            cp.wait()

    shapes = [p.shape[:a] + p.shape[a + 1:] for p, a in zip(parts, axes)]
    return pl.pallas_call(
        body, out_shape=tuple(jax.ShapeDtypeStruct(s, p.dtype) for s, p in zip(shapes, parts)), name="swap_halves",
        in_specs=[ANY] * n, out_specs=tuple([ANY] * n),
        scratch_shapes=[pltpu.SemaphoreType.DMA((n,)), pltpu.SemaphoreType.DMA((n,))],
    )(*parts)


def _presum(mine, sib, c_idx, tag):
    S, _, R, C = mine.shape
    tr = min(R, 256)
    tc = SHARD_IN if C % SHARD_IN == 0 else C

    def body(c_ref, m_ref, s_ref, o_ref, ob_ref):
        total = m_ref[:, 0] + s_ref[...]
        o_ref[...] = total
        ob_ref[...] = total.astype(BF16)

    out_spec = pl.BlockSpec((S, tr, tc), lambda i, j, c_ref: (0, i, j))
    grid_spec = pltpu.PrefetchScalarGridSpec(
        num_scalar_prefetch=1, grid=(R // tr, C // tc),
        in_specs=[pl.BlockSpec((S, 1, tr, tc), lambda i, j, c_ref: (0, c_ref[0], i, j)),
                  pl.BlockSpec((S, tr, tc), lambda i, j, c_ref: (0, i, j))],
        out_specs=(out_spec, out_spec))
    return pl.pallas_call(
        body, out_shape=(jax.ShapeDtypeStruct((S, R, C), F32), jax.ShapeDtypeStruct((S, R, C), BF16)),
        grid_spec=grid_spec, name=f"presum_{tag}", compiler_params=_cp("parallel", "parallel"),
    )(c_idx, mine, sib)


def _assemble_with_sibling(parts, axes):
    n = len(parts)

    def body(*refs):
        outs, ssem, rsem = refs[n:2 * n], refs[2 * n], refs[2 * n + 1]
        c = lax.axis_index("c")
        cps = [pltpu.make_async_remote_copy(
            src_ref=_half_of(outs[i], axes[i], c), dst_ref=_half_of(outs[i], axes[i], c), send_sem=ssem.at[i],
            recv_sem=rsem.at[i], device_id=_peer(1), device_id_type=MESH) for i in range(n)]
        for cp in cps:
            cp.start()
        for i in range(n):
            pltpu.make_async_remote_copy(
                src_ref=_half_of(outs[i], axes[i], c), dst_ref=_half_of(outs[i], axes[i], 1 - c), send_sem=ssem.at[i],
                recv_sem=rsem.at[i], device_id=_peer(1), device_id_type=MESH).wait_recv()
        for cp in cps:
            cp.wait_send()

    return pl.pallas_call(
        body, out_shape=tuple(jax.ShapeDtypeStruct(p.shape, p.dtype) for p in parts), name="assemble_with_sibling",
        in_specs=[ANY] * n, out_specs=tuple([ANY] * n), input_output_aliases={i: i for i in range(n)},
        scratch_shapes=[pltpu.SemaphoreType.DMA((n,)), pltpu.SemaphoreType.DMA((n,))],
    )(*parts)


def _rope_tables(pos_col):
    T = pos_col.shape[0]
    tm = min(T, 512)
    inv = np.float32(ROPE_THETA) ** (-(np.arange(0, 2 * ROT_HALF, 2, dtype=np.float32)) / np.float32(2 * ROT_HALF))
    lane = np.arange(128) % HEAD_DIM
    freq = np.where(lane < 2 * ROT_HALF, inv[lane % ROT_HALF], 0.0).astype(np.float32)[None, :]

    def body(pos_ref, f_ref, c_ref, sa_ref, sb_ref):
        ang = pos_ref[...].astype(F32) * f_ref[...]
        c, s = jnp.cos(ang), jnp.sin(ang)
        m = lax.broadcasted_iota(jnp.int32, ang.shape, 1) & (HEAD_DIM - 1)
        c_ref[...] = jnp.where(m < 2 * ROT_HALF, c, 1.0)
        sa_ref[...] = jnp.where(m < ROT_HALF, -s, 0.0)
        sb_ref[...] = jnp.where((m >= ROT_HALF) & (m < 2 * ROT_HALF), s, 0.0)

    tab = jax.ShapeDtypeStruct((T, 128), F32)
    return pl.pallas_call(
        body, out_shape=(tab, tab, tab), grid=(T // tm,), name="rope_tables",
        in_specs=[pl.BlockSpec((tm, 1), lambda i: (i, 0)), pl.BlockSpec((1, 128), lambda i: (0, 0))],
        out_specs=tuple(pl.BlockSpec((tm, 128), lambda i: (i, 0)) for _ in range(3)),
        compiler_params=_cp("parallel"),
    )(pos_col, jnp.asarray(freq))


def _wide(tab, width):
    del width
    return tab


def _columns(t):
    return [t[:, i:i + 128] for i in range(0, t.shape[-1], 128)]


def _rope(t, c, sa, sb):
    return jnp.concatenate(
        [x * c + pltpu.roll(x, 128 - ROT_HALF, 1) * sa + pltpu.roll(x, ROT_HALF, 1) * sb for x in _columns(t)], axis=1)


def _unrope(d, c, sa, sb):
    return jnp.concatenate(
        [x * c + pltpu.roll(x * sa, ROT_HALF, 1) + pltpu.roll(x * sb, 128 - ROT_HALF, 1) for x in _columns(d)], axis=1)


def _prenorm(x, mod_row, norm_g):
    T = x.shape[0]
    tm = min(T, 512)

    def body(x_ref, mod_ref, g_ref, h_ref):
        xf = x_ref[...]
        shift, scale = mod_ref[:, 0:D_MODEL], mod_ref[:, D_MODEL:2 * D_MODEL]
        h = (xf * _rms(xf)) * g_ref[...] * (1.0 + scale) + shift
        h_ref[...] = h.astype(BF16)

    return pl.pallas_call(
        body, out_shape=jax.ShapeDtypeStruct((T, D_MODEL), BF16), grid=(T // tm,), name="prenorm",
        in_specs=[pl.BlockSpec((tm, D_MODEL), lambda i: (i, 0)), pl.BlockSpec((1, ADA_W), lambda i: (0, 0)),
                  pl.BlockSpec((1, D_MODEL), lambda i: (0, 0))],
        out_specs=pl.BlockSpec((tm, D_MODEL), lambda i: (i, 0)),
        compiler_params=_cp("parallel"),
    )(x, mod_row, norm_g)


def _in_projection(h, w_in, chip, into, tag):
    T = h.shape[0]
    tm, tn = min(T, 512), SHARD_IN

    def body(chip_ref, h_ref, w_ref, *rest):
        rest[-1][...] = _dot(h_ref[...], w_ref[...])

    in_specs = [pl.BlockSpec((tm, D_MODEL), lambda i, c: (i, 0)),
                pl.BlockSpec((D_MODEL, tn), lambda i, c: (0, c[0]), pipeline_mode=pl.Buffered(1))]
    args = [chip, h, w_in]
    aliases = {}
    if into is not None:
        in_specs.append(ANY)
        args.append(into)
        aliases = {3: 0}
    grid_spec = pltpu.PrefetchScalarGridSpec(num_scalar_prefetch=1, grid=(T // tm,), in_specs=in_specs,
                                             out_specs=pl.BlockSpec((tm, tn), lambda i, c: (i, c[0])))
    return pl.pallas_call(
        body, out_shape=jax.ShapeDtypeStruct((T, IN_W), F32), grid_spec=grid_spec, name=f"in_projection_{tag}",
        input_output_aliases=aliases, compiler_params=_cp("parallel"),
    )(*args)


def _attn_mask(n):
    qi = lax.broadcasted_iota(jnp.int32, (GROUP * BLOCK, 2 * BLOCK), 0) & (BLOCK - 1)
    kj = lax.broadcasted_iota(jnp.int32, (GROUP * BLOCK, 2 * BLOCK), 1)
    diff = qi + BLOCK - kj
    return (diff >= 0) & (diff < BLOCK) & ((kj >= BLOCK) | (n > 0))


ROW_GROUP_HEAD = (0, 2, 1, 3)


def _sink_col(sink_ref, kh):
    rowg = lax.broadcasted_iota(jnp.int32, (GROUP * BLOCK, 1), 0) // BLOCK
    col = jnp.full((GROUP * BLOCK, 1), sink_ref[0, GROUP * kh + ROW_GROUP_HEAD[0]], F32)
    for g in range(1, GROUP):
        col = jnp.where(rowg == g, sink_ref[0, GROUP * kh + ROW_GROUP_HEAD[g]], col)
    return col


def _low_lanes(shape):
    return lax.broadcasted_iota(jnp.int32, shape, 1) < HEAD_DIM


def _kv_pair_operand(prev, cur, kh):
    c = 128 * (kh // 2)
    col = jnp.concatenate([prev[:, c:c + 128], cur[:, c:c + 128]], axis=0)
    if kh % 2 == 0:
        lo = jnp.where(_low_lanes(col.shape), col, 0.0)
        hi = pltpu.roll(lo, HEAD_DIM, 1)
    else:
        hi = jnp.where(_low_lanes(col.shape), 0.0, col)
        lo = pltpu.roll(hi, HEAD_DIM, 1)
    return jnp.concatenate([lo, hi], axis=0).astype(BF16)


def _pair_rows(x, kh):
    c = 2 * 128 * kh
    return jnp.concatenate([x[:, c:c + 128], x[:, c + 128:c + 256]], axis=0)


def _restack(big):
    return jnp.concatenate([big[:, 0:2 * BLOCK], big[:, 2 * BLOCK:4 * BLOCK]], axis=0)


def _unrestack(stacked):
    return jnp.concatenate([stacked[0:2 * BLOCK], stacked[2 * BLOCK:4 * BLOCK]], axis=1)


def _fold_pair(x2, kh):
    low = _low_lanes((2 * BLOCK, 128))
    mixed = jnp.where(low, x2[0:2 * BLOCK], x2[2 * BLOCK:4 * BLOCK])
    total = mixed + pltpu.roll(mixed, HEAD_DIM, 1)
    return jnp.where(low, total, 0.0) if kh % 2 == 0 else jnp.where(low, 0.0, total)


def _attn_scores(qr, k2, kh):
    q2 = _pair_rows(qr, kh).astype(BF16)
    return q2, _restack(_dot_nt(q2, k2))


def _attn_softmax(s, sink_col, mask):
    s = jnp.where(mask, s, -1e30)
    m = jnp.maximum(jnp.max(s, axis=-1, keepdims=True), sink_col)
    p = jnp.exp(s - m)
    p_sink = jnp.exp(sink_col - m)
    denom = jnp.sum(p, axis=-1, keepdims=True) + p_sink
    return p / denom, p_sink / denom


def _attn_forward(proj, tabs, sinks):
    T = proj.shape[0]
    nb = T // BLOCK

    def body(q_ref, kvc_ref, kvp_ref, g0_ref, g1_ref, cc, sac, sbc, cp_, sap, sbp, sink_ref, y_ref):
        n = pl.program_id(0)
        tc = (_wide(cc[...], D_MODEL), _wide(sac[...], D_MODEL), _wide(sbc[...], D_MODEL))
        tcur = tuple(t[:, :KV_W] for t in tc)
        tprev = (_wide(cp_[...], KV_W), _wide(sap[...], KV_W), _wide(sbp[...], KV_W))
        qr = _rope(q_ref[...], *tc) * ATTN_SCALE
        kr_cur = _rope(kvc_ref[:, 0:KV_W], *tcur)
        kr_prev = _rope(kvp_ref[:, 0:KV_W], *tprev)
        v_cur, v_prev = kvc_ref[:, KV_W:2 * KV_W], kvp_ref[:, KV_W:2 * KV_W]
        mask = _attn_mask(n)
        outs = []
        k2s = [_kv_pair_operand(kr_prev, kr_cur, kh) for kh in range(N_KV)]
        v2s = [_kv_pair_operand(v_prev, v_cur, kh) for kh in range(N_KV)]
        scores = [_attn_scores(qr, k2s[kh], kh) for kh in range(N_KV)]
        for kh in range(N_KV):
            pn, _ = _attn_softmax(scores[kh][1], _sink_col(sink_ref, kh), mask)
            o_big = _dot(_unrestack(pn.astype(BF16)), v2s[kh])
            outs += [o_big[0:BLOCK], o_big[BLOCK:2 * BLOCK]]
        o = jnp.concatenate(outs, axis=1)
        g = jnp.concatenate([g0_ref[...], g1_ref[...]], axis=1)
        y_ref[...] = (o * (g * _sigmoid(g))).astype(BF16)

    def blk(w, cb):
        return pl.BlockSpec((BLOCK, w), lambda n, cb=cb: (n, cb))

    prev = lambda w, cb: pl.BlockSpec((BLOCK, w), lambda n, cb=cb: (jnp.maximum(n - 1, 0), cb))
    return pl.pallas_call(
        body, out_shape=jax.ShapeDtypeStruct((T, D_MODEL), BF16), grid=(nb,), name="attn_forward",
        in_specs=[blk(D_MODEL, 0), blk(CB, CB_KV), prev(CB, CB_KV), blk(CB, CB_GA), blk(CB, CB_GA + 1),
                  blk(128, 0), blk(128, 0), blk(128, 0), prev(128, 0), prev(128, 0), prev(128, 0),
                  pl.BlockSpec(memory_space=pltpu.SMEM)],
        out_specs=pl.BlockSpec((BLOCK, D_MODEL), lambda n: (n, 0)),
        compiler_params=_cp("parallel"),
    )(proj, proj, proj, proj, proj, *tabs, *tabs, sinks)


def _scan_rows8():
    return lax.broadcasted_iota(jnp.int32, (8, D_MODEL), 0)


def _scan_forward(a_ref, b_ref, h_ref, carry, rows):
    row = _scan_rows8()

    def group(i, carry):
        off = pl.multiple_of(i * 8, 8)
        a, b = a_ref[pl.ds(off, 8), :], b_ref[pl.ds(off, 8), :]
        for d in (1, 2, 4):
            ok = row >= d
            b = jnp.where(ok, a * pltpu.roll(b, d, 0) + b, b)
            a = jnp.where(ok, a * pltpu.roll(a, d, 0), a)
        h = a * carry + b
        h_ref[pl.ds(off, 8), :] = h
        return h[7:8, :]

    return lax.fori_loop(0, rows // 8, group, carry)


def _scan_backward(a_ref, g_ref, lam_ref, carry, rows):
    row = _scan_rows8()

    def group(i, carry):
        off = pl.multiple_of((rows // 8 - 1 - i) * 8, 8)
        a, g = a_ref[pl.ds(off, 8), :], g_ref[pl.ds(off, 8), :]
        b = a * g
        for d in (1, 2, 4):
            ok = row < 8 - d
            b = jnp.where(ok, a * pltpu.roll(b, 8 - d, 0) + b, b)
            a = jnp.where(ok, a * pltpu.roll(a, 8 - d, 0), a)
        mu = a * carry + b
        mu_below = jnp.where(row == 7, carry, pltpu.roll(mu, 7, 0))
        lam_ref[pl.ds(off, 8), :] = g + mu_below
        return mu[0:1, :]

    return lax.fori_loop(0, rows // 8, group, carry)


def _rnn_recompute(xbuf, xr, tail, cw, cb, wa_ref, wx_ref, ba, bx, sp, reset):
    rows = xr.shape[0]
    xbuf[0:8, :] = tail
    xbuf[8:rows + 8, :] = xr
    xs = [xbuf[pl.ds(8 - (CONV_W - 1 - k), rows), :] for k in range(CONV_W - 1)] + [xr]
    xc = xs[0] * cw[0:1, :]
    for k in range(1, CONV_W):
        xc = xc + xs[k] * cw[k:k + 1, :]
    xc = xc + cb
    xcb = xc.astype(BF16)
    za = jnp.concatenate([_dot(xcb[:, RNN_BW * j:RNN_BW * (j + 1)], wa_ref[j]) for j in range(RNN_BLOCKS)], axis=1) + ba
    zx = jnp.concatenate([_dot(xcb[:, RNN_BW * j:RNN_BW * (j + 1)], wx_ref[j]) for j in range(RNN_BLOCKS)], axis=1) + bx
    r, i = _sigmoid(za), _sigmoid(zx)
    log_a = -LRU_C * r * sp
    a_raw = jnp.exp(log_a)
    mult_raw = jnp.sqrt(_neg_expm1(2.0 * log_a))
    a = jnp.where(reset, 0.0, a_raw)
    mult = jnp.where(reset, 1.0, mult_raw)
    return xs, xc, xcb, r, i, a_raw, mult_raw, a, mult


def _rnn_forward(proj, pos_col, conv_w, conv_b, rwa, rwx, ba, bx, lam):
    T = proj.shape[0]
    tr = min(T, 256)

    def body(x0, x1, g0, g1, pos_ref, cw_ref, cb_ref, wa_ref, wx_ref, ba_ref, bx_ref, lam_ref,
             y_ref, h_ref, xbuf, abuf, bbuf, tail, carry):
        t = pl.program_id(0)

        @pl.when(t == 0)
        def _():
            tail[...] = jnp.zeros_like(tail)
            carry[...] = jnp.zeros_like(carry)

        xr = jnp.concatenate([x0[...], x1[...]], axis=1)
        sp = _softplus(-lam_ref[...])
        reset = pos_ref[...] == 0
        _, xc, _, _, i, _, _, a, mult = _rnn_recompute(
            xbuf, xr, tail[...], cw_ref[...], cb_ref[...], wa_ref, wx_ref, ba_ref[...], bx_ref[...], sp, reset)
        abuf[...] = a
        bbuf[...] = mult * (i * xc)
        last = _scan_forward(abuf, bbuf, h_ref, carry[0:1, :], tr)
        carry[...] = jnp.broadcast_to(last, carry.shape)
        tail[...] = xr[tr - 8:tr, :]
        g = jnp.concatenate([g0[...], g1[...]], axis=1)
        y_ref[...] = (h_ref[...] * (g * _sigmoid(g))).astype(BF16)

    blk = lambda cb: pl.BlockSpec((tr, CB), lambda t, cb=cb: (t, cb))
    row = lambda w: pl.BlockSpec((1, w), lambda t: (0, 0))
    full3 = pl.BlockSpec((RNN_BLOCKS, RNN_BW, RNN_BW), lambda t: (0, 0, 0))
    return pl.pallas_call(
        body, out_shape=(jax.ShapeDtypeStruct((T, D_MODEL), BF16), jax.ShapeDtypeStruct((T, D_MODEL), F32)),
        grid=(T // tr,), name="rnn_forward",
        in_specs=[blk(CB_XR), blk(CB_XR + 1), blk(CB_GR), blk(CB_GR + 1), pl.BlockSpec((tr, 1), lambda t: (t, 0)),
                  pl.BlockSpec((CONV_W, D_MODEL), lambda t: (0, 0)), row(D_MODEL), full3, full3,
                  row(D_MODEL), row(D_MODEL), row(D_MODEL)],
        out_specs=(pl.BlockSpec((tr, D_MODEL), lambda t: (t, 0)), pl.BlockSpec((tr, D_MODEL), lambda t: (t, 0))),
        scratch_shapes=[pltpu.VMEM((tr + 8, D_MODEL), F32), pltpu.VMEM((tr, D_MODEL), F32), pltpu.VMEM((tr, D_MODEL), F32),
                        pltpu.VMEM((8, D_MODEL), F32), pltpu.VMEM((8, D_MODEL), F32)],
        compiler_params=_cp("arbitrary"),
    )(proj, proj, proj, proj, pos_col, conv_w, conv_b, rwa, rwx, ba, bx, lam)


def _merge_and_head(x, target, y_attn, y_rnn, proj, wap, wrp, wo, mod_row, final_g):
    T = x.shape[0]
    tm = min(T, 256)

    def body(x_ref, t_ref, ya_ref, yr_ref, ma0, ma1, mr0, mr1, wap_ref, wrp_ref, wo_ref, mod_ref, fg_ref,
             dx2_ref, mg_ref, do_ref, dpa_ref, dpr_ref, dya_ref, dyr_ref, dc_ref, dfg_ref, dgate_ref, loss_ref):
        i = pl.program_id(0)
        gate = mod_ref[:, 2 * D_MODEL:3 * D_MODEL]
        ya, yr = ya_ref[...], yr_ref[...]
        pa, pr = _dot(ya, wap_ref[...]), _dot(yr, wrp_ref[...])
        sa = _sigmoid(jnp.concatenate([ma0[...], ma1[...]], axis=1))
        sr = _sigmoid(jnp.concatenate([mr0[...], mr1[...]], axis=1))
        merged = sa * pa + sr * pr
        mb = merged.astype(BF16)
        o = _dot(mb, wo_ref[...])
        x2 = x_ref[...] + gate * o
        r2 = _rms(x2)
        xn2 = x2 * r2
        fg = fg_ref[...]
        err = xn2 * fg - t_ref[...]
        loss_t = 0.5 * jnp.sum(jnp.sum(err * err, axis=-1, keepdims=True) * (1.0 / D_MODEL), axis=0, keepdims=True)
        dy = err * (1.0 / D_MODEL)
        dfg_t = jnp.sum(dy * xn2, axis=0, keepdims=True)
        dxn = dy * fg
        dx2 = r2 * (dxn - xn2 * jnp.mean(dxn * xn2, axis=-1, keepdims=True))
        dgate_t = jnp.sum(dx2 * o, axis=0, keepdims=True)
        dob = (dx2 * gate).astype(BF16)
        dmerged = _dot_nt(dob, wo_ref[...])
        dpa = (dmerged * sa).astype(BF16)
        dpr = (dmerged * sr).astype(BF16)
        dx2_ref[...] = dx2
        mg_ref[...] = mb
        do_ref[...] = dob
        dpa_ref[...] = dpa
        dpr_ref[...] = dpr
        dya_ref[...] = _dot_nt(dpa, wap_ref[...])
        dyr_ref[...] = _dot_nt(dpr, wrp_ref[...])
        dc_ref[:, 0:D_MODEL] = (dmerged * pa * sa * (1.0 - sa)).astype(BF16)
        dc_ref[:, D_MODEL:2 * D_MODEL] = (dmerged * pr * sr * (1.0 - sr)).astype(BF16)

        @pl.when(i == 0)
        def _():
            dfg_ref[...] = jnp.zeros_like(dfg_ref)
            dgate_ref[...] = jnp.zeros_like(dgate_ref)
            loss_ref[...] = jnp.zeros_like(loss_ref)

        dfg_ref[...] += dfg_t
        dgate_ref[...] += dgate_t
        loss_ref[...] += jnp.broadcast_to(loss_t, loss_ref.shape)

    tok = lambda w: pl.BlockSpec((tm, w), lambda i: (i, 0))
    blk = lambda cb: pl.BlockSpec((tm, CB), lambda i, cb=cb: (i, cb))
    wfull = pl.BlockSpec((D_MODEL, D_MODEL), lambda i: (0, 0))
    row = lambda w: pl.BlockSpec((1, w), lambda i: (0, 0))
    out_shape = (
        jax.ShapeDtypeStruct((T, D_MODEL), F32), jax.ShapeDtypeStruct((T, D_MODEL), BF16),
        jax.ShapeDtypeStruct((T, D_MODEL), BF16), jax.ShapeDtypeStruct((T, D_MODEL), BF16),
        jax.ShapeDtypeStruct((T, D_MODEL), BF16), jax.ShapeDtypeStruct((T, D_MODEL), F32),
        jax.ShapeDtypeStruct((T, D_MODEL), F32), jax.ShapeDtypeStruct((T, 2 * D_MODEL), BF16),
        jax.ShapeDtypeStruct((1, D_MODEL), F32), jax.ShapeDtypeStruct((1, D_MODEL), F32),
        jax.ShapeDtypeStruct((1, 128), F32),
    )
    return pl.pallas_call(
        body, out_shape=out_shape, grid=(T // tm,), name="merge_and_head",
        in_specs=[tok(D_MODEL), tok(D_MODEL), tok(D_MODEL), tok(D_MODEL), blk(CB_MA), blk(CB_MA + 1), blk(CB_MR),
                  blk(CB_MR + 1), wfull, wfull, wfull, row(ADA_W), row(D_MODEL)],
        out_specs=(tok(D_MODEL),) * 7 + (tok(2 * D_MODEL), row(D_MODEL), row(D_MODEL), row(128)),
        compiler_params=_cp("arbitrary"),
    )(x, target, y_attn, y_rnn, proj, proj, proj, proj, wap, wrp, wo, mod_row, final_g)


def _attn_backward(proj, d_y, tabs, sinks):
    T = proj.shape[0]
    nb = T // BLOCK

    def body(q_ref, kvc_ref, kvp_ref, g0_ref, g1_ref, dy_ref, cc, sac, sbc, cp_, sap, sbp, sink_ref,
             dq_ref, dkv_ref, dg_ref, dsink_ref, carry):
        n = pl.program_id(0)

        @pl.when(n == 0)
        def _():
            carry[...] = jnp.zeros_like(carry)
            dsink_ref[...] = jnp.zeros_like(dsink_ref)

        @pl.when(n < nb)
        def _():
            tc = (_wide(cc[...], D_MODEL), _wide(sac[...], D_MODEL), _wide(sbc[...], D_MODEL))
            tcur = tuple(t[:, :KV_W] for t in tc)
            tprev = (_wide(cp_[...], KV_W), _wide(sap[...], KV_W), _wide(sbp[...], KV_W))
            qr = _rope(q_ref[...], *tc) * ATTN_SCALE
            kr_cur = _rope(kvc_ref[:, 0:KV_W], *tcur)
            kr_prev = _rope(kvp_ref[:, 0:KV_W], *tprev)
            v_cur, v_prev = kvc_ref[:, KV_W:2 * KV_W], kvp_ref[:, KV_W:2 * KV_W]
            g = jnp.concatenate([g0_ref[...], g1_ref[...]], axis=1)
            sg = _sigmoid(g)
            dy = dy_ref[...]
            d_o = dy * (g * sg)
            mask = _attn_mask(n)
            lane = lax.broadcasted_iota(jnp.int32, (1, 128), 1)
            rowg = lax.broadcasted_iota(jnp.int32, (GROUP * BLOCK, 1), 0) // BLOCK
            o_parts, dq_parts = [], []
            dk_cols, dv_cols = [None, None], [None, None]
            dsink = jnp.zeros((1, 128), F32)
            k2s = [_kv_pair_operand(kr_prev, kr_cur, kh) for kh in range(N_KV)]
            v2s = [_kv_pair_operand(v_prev, v_cur, kh) for kh in range(N_KV)]
            scores = [_attn_scores(qr, k2s[kh], kh) for kh in range(N_KV)]
            do2s = [_pair_rows(d_o, kh).astype(BF16) for kh in range(N_KV)]
            dpns = [_restack(_dot_nt(do2s[kh], v2s[kh])) for kh in range(N_KV)]
            probs = [_attn_softmax(scores[kh][1], _sink_col(sink_ref, kh), mask) for kh in range(N_KV)]
            p_bigs = [_unrestack(probs[kh][0].astype(BF16)) for kh in range(N_KV)]
            o_bigs = [_dot(p_bigs[kh], v2s[kh]) for kh in range(N_KV)]
            dv2s = [_dot_tn(p_bigs[kh], do2s[kh]) for kh in range(N_KV)]
            deltas = [jnp.sum(probs[kh][0] * dpns[kh], axis=-1, keepdims=True) for kh in range(N_KV)]
            ds_bigs = [_unrestack((probs[kh][0] * (dpns[kh] - deltas[kh])).astype(BF16)) for kh in range(N_KV)]
            dq2s = [_dot(ds_bigs[kh], k2s[kh]) for kh in range(N_KV)]
            dk2s = [_dot_tn(ds_bigs[kh], scores[kh][0]) for kh in range(N_KV)]
            for kh in range(N_KV):
                o_parts += [o_bigs[kh][0:BLOCK], o_bigs[kh][BLOCK:2 * BLOCK]]
                dq_parts += [dq2s[kh][0:BLOCK], dq2s[kh][BLOCK:2 * BLOCK]]
                dk_c, dv_c = _fold_pair(dk2s[kh], kh), _fold_pair(dv2s[kh], kh)
                c = kh // 2
                dk_cols[c] = dk_c if dk_cols[c] is None else dk_cols[c] + dk_c
                dv_cols[c] = dv_c if dv_cols[c] is None else dv_cols[c] + dv_c
                ds_rows = probs[kh][1] * deltas[kh]
                for gq in range(GROUP):
                    val = -jnp.sum(jnp.where(rowg == gq, ds_rows, 0.0), axis=0, keepdims=True)
                    dsink = dsink + jnp.where(lane == GROUP * kh + ROW_GROUP_HEAD[gq], val, 0.0)
            o = jnp.concatenate(o_parts, axis=1)
            dg_ref[...] = (dy * o * (sg * (1.0 + g * (1.0 - sg)))).astype(BF16)
            dq_ref[...] = (_unrope(jnp.concatenate(dq_parts, axis=1), *tc) * ATTN_SCALE).astype(BF16)
            dk_all, dv_all = jnp.concatenate(dk_cols, axis=1), jnp.concatenate(dv_cols, axis=1)
            dk_prev = _unrope(dk_all[0:BLOCK], *tprev)
            dk_cur = _unrope(dk_all[BLOCK:2 * BLOCK], *tcur)
            dv_prev, dv_cur = dv_all[0:BLOCK], dv_all[BLOCK:2 * BLOCK]
            dkv_ref[...] = (carry[...] + jnp.concatenate([dk_prev, dv_prev], axis=1)).astype(BF16)
            carry[...] = jnp.concatenate([dk_cur, dv_cur], axis=1)
            dsink_ref[...] += dsink

        @pl.when(n == nb)
        def _():
            dkv_ref[...] = carry[...].astype(BF16)

    cur = lambda w, cb: pl.BlockSpec((BLOCK, w), lambda n, cb=cb: (jnp.minimum(n, nb - 1), cb))
    prev = lambda w, cb: pl.BlockSpec((BLOCK, w), lambda n, cb=cb: (jnp.maximum(jnp.minimum(n, nb - 1) - 1, 0), cb))
    out_shape = (jax.ShapeDtypeStruct((T, D_MODEL), BF16), jax.ShapeDtypeStruct((T, 2 * KV_W), BF16),
                 jax.ShapeDtypeStruct((T, D_MODEL), BF16), jax.ShapeDtypeStruct((1, 128), F32))
    return pl.pallas_call(
        body, out_shape=out_shape, grid=(nb + 1,), name="attn_backward",
        in_specs=[cur(D_MODEL, 0), cur(CB, CB_KV), prev(CB, CB_KV), cur(CB, CB_GA), cur(CB, CB_GA + 1), cur(D_MODEL, 0),
                  cur(128, 0), cur(128, 0), cur(128, 0), prev(128, 0), prev(128, 0), prev(128, 0),
                  pl.BlockSpec(memory_space=pltpu.SMEM)],
        out_specs=(cur(D_MODEL, 0), pl.BlockSpec((BLOCK, 2 * KV_W), lambda n: (jnp.maximum(n - 1, 0), 0)),
                   cur(D_MODEL, 0), pl.BlockSpec((1, 128), lambda n: (0, 0))),
        scratch_shapes=[pltpu.VMEM((BLOCK, 2 * KV_W), F32)],
        compiler_params=_cp("arbitrary"),
    )(proj, proj, proj, proj, proj, d_y, *tabs, *tabs, sinks)


def _rnn_backward(proj, pos_col, h_rnn, d_y, conv_w, conv_b, rwa, rwx, ba, bx, lam):
    T = proj.shape[0]
    tr = min(T, 256)
    nt = T // tr
    hb = tr // 8

    def body(x0, x1, xh0, xh1, g0, g1, pos_ref, h_ref, hh_ref, dy_ref, cw_ref, cb_ref, wa_ref, wx_ref, ba_ref, bx_ref,
             lam_ref, db_ref, dcw_ref, dcb_ref, dwa_ref, dwx_ref, dba_ref, dbx_ref, dlam_ref,
             xbuf, hbuf, dbuf, abuf, gbuf, lbuf, mu_carry, dxc_head):
        step = pl.program_id(0)
        first_tile = step == nt - 1

        @pl.when(step == 0)
        def _():
            mu_carry[...] = jnp.zeros_like(mu_carry)
            dxc_head[...] = jnp.zeros_like(dxc_head)
            for ref in (dcw_ref, dcb_ref, dwa_ref, dwx_ref, dba_ref, dbx_ref, dlam_ref):
                ref[...] = jnp.zeros_like(ref)

        xr = jnp.concatenate([x0[...], x1[...]], axis=1)
        tail = jnp.where(first_tile, 0.0, jnp.concatenate([xh0[...], xh1[...]], axis=1))
        lam_v = lam_ref[...]
        sp = _softplus(-lam_v)
        reset = pos_ref[...] == 0
        cw = cw_ref[...]
        xs, xc, xcb, r, i, a_raw, mult_raw, a, mult = _rnn_recompute(
            xbuf, xr, tail, cw, cb_ref[...], wa_ref, wx_ref, ba_ref[...], bx_ref[...], sp, reset)
        g = jnp.concatenate([g0[...], g1[...]], axis=1)
        sg = _sigmoid(g)
        dy = dy_ref[...]
        h = h_ref[...]
        d_g = dy * h * (sg * (1.0 + g * (1.0 - sg)))
        abuf[...] = a
        gbuf[...] = dy * (g * sg)
        top = _scan_backward(abuf, gbuf, lbuf, mu_carry[0:1, :], tr)
        mu_carry[...] = jnp.broadcast_to(top, mu_carry.shape)
        lam_t = lbuf[...]
        hbuf[0:8, :] = jnp.where(first_tile, 0.0, hh_ref[...])
        hbuf[8:tr + 8, :] = h
        h_prev = hbuf[pl.ds(7, tr), :]
        live = jnp.logical_not(reset)
        d_a = jnp.where(live, lam_t * h_prev, 0.0)
        d_mult = jnp.where(live, lam_t * (i * xc), 0.0)
        d_ixc = lam_t * mult
        d_i = d_ixc * xc
        d_xc = d_ixc * i
        d_log_a = d_a * a_raw - d_mult * (a_raw * a_raw / mult_raw)
        d_log_a = jnp.where(live, d_log_a, 0.0)
        d_za = d_log_a * (-LRU_C * sp) * (r * (1.0 - r))
        d_zx = d_i * (i * (1.0 - i))
        dlam_ref[...] += jnp.sum(d_log_a * r, axis=0, keepdims=True) * (LRU_C * _sigmoid(-lam_v))
        dba_ref[...] += jnp.sum(d_za, axis=0, keepdims=True)
        dbx_ref[...] += jnp.sum(d_zx, axis=0, keepdims=True)
        dzab, dzxb = d_za.astype(BF16), d_zx.astype(BF16)
        back = []
        for j in range(RNN_BLOCKS):
            sl = slice(RNN_BW * j, RNN_BW * (j + 1))
            dwa_ref[j] += _dot_tn(xcb[:, sl], dzab[:, sl])
            dwx_ref[j] += _dot_tn(xcb[:, sl], dzxb[:, sl])
            back.append(_dot_nt(dzab[:, sl], wa_ref[j]) + _dot_nt(dzxb[:, sl], wx_ref[j]))
        d_xc = d_xc + jnp.concatenate(back, axis=1)
        dcb_ref[...] += jnp.sum(d_xc, axis=0, keepdims=True)
        for k in range(CONV_W):
            dcw_ref[k:k + 1, :] += jnp.sum(d_xc * xs[k], axis=0, keepdims=True)
        dbuf[0:tr, :] = d_xc
        dbuf[tr:tr + 8, :] = dxc_head[...]
        d_xr = d_xc * cw[CONV_W - 1:CONV_W, :]
        for k in range(CONV_W - 1):
            d_xr = d_xr + dbuf[pl.ds(CONV_W - 1 - k, tr), :] * cw[k:k + 1, :]
        dxc_head[...] = d_xc[0:8, :]
        db_ref[:, 0:D_MODEL] = d_xr.astype(BF16)
        db_ref[:, D_MODEL:2 * D_MODEL] = d_g.astype(BF16)

    rev = lambda s: nt - 1 - s
    blk = lambda cb: pl.BlockSpec((tr, CB), lambda s, cb=cb: (rev(s), cb))
    halo = lambda w, cb: pl.BlockSpec((8, w), lambda s, cb=cb: (jnp.maximum(rev(s) * hb - 1, 0), cb))
    tok = lambda w: pl.BlockSpec((tr, w), lambda s: (rev(s), 0))
    row = lambda w: pl.BlockSpec((1, w), lambda s: (0, 0))
    full3 = pl.BlockSpec((RNN_BLOCKS, RNN_BW, RNN_BW), lambda s: (0, 0, 0))
    cwspec = pl.BlockSpec((CONV_W, D_MODEL), lambda s: (0, 0))
    vec = jax.ShapeDtypeStruct((1, D_MODEL), F32)
    gate_w = jax.ShapeDtypeStruct((RNN_BLOCKS, RNN_BW, RNN_BW), F32)
    out_shape = (jax.ShapeDtypeStruct((T, 2 * D_MODEL), BF16), jax.ShapeDtypeStruct((CONV_W, D_MODEL), F32), vec,
                 gate_w, gate_w, vec, vec, vec)
    big = lambda: pltpu.VMEM((tr, D_MODEL), F32)
    ext = lambda: pltpu.VMEM((tr + 8, D_MODEL), F32)
    return pl.pallas_call(
        body, out_shape=out_shape, grid=(nt,), name="rnn_backward",
        in_specs=[blk(CB_XR), blk(CB_XR + 1), halo(CB, CB_XR), halo(CB, CB_XR + 1), blk(CB_GR), blk(CB_GR + 1),
                  pl.BlockSpec((tr, 1), lambda s: (rev(s), 0)), tok(D_MODEL), halo(D_MODEL, 0), tok(D_MODEL),
                  cwspec, row(D_MODEL), full3, full3, row(D_MODEL), row(D_MODEL), row(D_MODEL)],
        out_specs=(tok(2 * D_MODEL), cwspec, row(D_MODEL), full3, full3, row(D_MODEL), row(D_MODEL), row(D_MODEL)),
        scratch_shapes=[ext(), ext(), ext(), big(), big(), big(), pltpu.VMEM((8, D_MODEL), F32), pltpu.VMEM((8, D_MODEL), F32)],
        compiler_params=_cp("arbitrary"),
    )(proj, proj, proj, proj, proj, proj, pos_col, h_rnn, h_rnn, d_y, conv_w, conv_b, rwa, rwx, ba, bx, lam)


def _input_backward(pieces, w_in, x, dx2, mod_row, norm_g):
    T = x.shape[0]
    tm = min(T, 256)
    n = len(pieces)

    def body(*refs):
        d_refs = refs[:n]
        w_ref, x_ref, dx2_ref, mod_ref, g_ref, gx_ref, dshift_ref, dscale_ref, dg_ref = refs[n:]
        i = pl.program_id(0)
        dh = None
        for d_ref, (_, start, count) in zip(d_refs, pieces):
            part = _dot_nt(d_ref[...], w_ref[:, start * CB:(start + count) * CB])
            dh = part if dh is None else dh + part

        @pl.when(i == 0)
        def _():
            dshift_ref[...] = jnp.zeros_like(dshift_ref)
            dscale_ref[...] = jnp.zeros_like(dscale_ref)
            dg_ref[...] = jnp.zeros_like(dg_ref)

        xf = x_ref[...]
        r1 = _rms(xf)
        xn = xf * r1
        gn = g_ref[...]
        s1 = 1.0 + mod_ref[:, D_MODEL:2 * D_MODEL]
        dshift_ref[...] += jnp.sum(dh, axis=0, keepdims=True)
        dscale_ref[...] += jnp.sum(dh * (xn * gn), axis=0, keepdims=True)
        dg_ref[...] += jnp.sum(dh * s1 * xn, axis=0, keepdims=True)
        dxn = dh * s1 * gn
        gx_ref[...] = dx2_ref[...] + r1 * (dxn - xn * jnp.mean(dxn * xn, axis=-1, keepdims=True))

    tok = lambda w: pl.BlockSpec((tm, w), lambda i: (i, 0))
    row = lambda w: pl.BlockSpec((1, w), lambda i: (0, 0))
    vec = jax.ShapeDtypeStruct((1, D_MODEL), F32)
    return pl.pallas_call(
        body, out_shape=(jax.ShapeDtypeStruct((T, D_MODEL), F32), vec, vec, vec), grid=(T // tm,), name="input_backward",
        in_specs=[tok(c * CB) for _, _, c in pieces]
        + [pl.BlockSpec((D_MODEL, IN_W), lambda i: (0, 0), pipeline_mode=pl.Buffered(1)), tok(D_MODEL), tok(D_MODEL),
           row(ADA_W), row(D_MODEL)],
        out_specs=(tok(D_MODEL), row(D_MODEL), row(D_MODEL), row(D_MODEL)),
        compiler_params=_cp("arbitrary"),
    )(*[p[0] for p in pieces], w_in, x, dx2, mod_row, norm_g)


def _weight_grad(a, pieces, tag):
    T, M = a.shape
    n_blocks = sum(count for _, _, count in pieces)
    n = len(pieces)

    def body(*refs):
        a_ref, b_refs, o_ref = refs[0], refs[1:1 + n], refs[-1]
        j = pl.program_id(0)
        for b_ref, (_, start, count) in zip(b_refs, pieces):
            @pl.when((j >= start) & (j < start + count))
            def _(b_ref=b_ref):
                o_ref[...] = _dot_tn(a_ref[...], b_ref[...])

    def piece_spec(start, count):
        return pl.BlockSpec((T, CB), lambda j: (0, jnp.clip(j - start, 0, count - 1)))

    return pl.pallas_call(
        body, out_shape=jax.ShapeDtypeStruct((M, n_blocks * CB), F32), grid=(n_blocks,), name=f"weight_grad_{tag}",
        in_specs=[pl.BlockSpec((T, M), lambda j: (0, 0), pipeline_mode=pl.Buffered(1))] + [piece_spec(s, c) for _, s, c in pieces],
        out_specs=pl.BlockSpec((M, CB), lambda j: (0, j)), compiler_params=_cp("arbitrary"),
    )(a, *[p[0] for p in pieces])


def _adamw(w, g, m, v):
    m = ADAM_B1 * m + (1.0 - ADAM_B1) * g
    v = ADAM_B2 * v + (1.0 - ADAM_B2) * (g * g)
    m_hat = m / (1.0 - ADAM_B1 ** ADAM_STEP)
    v_hat = v / (1.0 - ADAM_B2 ** ADAM_STEP)
    delta = -ADAM_LR * (m_hat / (jnp.sqrt(v_hat) + ADAM_EPS) + ADAM_WD * w)
    return delta, m, v


def _sum_landed(kind, own, land, where, tag):
    if kind == "in":
        R, C = land.shape[1:]
        tr = 256
        grid = (R // tr,)
        own_spec = pl.BlockSpec((tr, C), lambda i, w: (i, w[0]))
        land_spec = pl.BlockSpec((3, tr, C), lambda i, w: (0, i, 0))
        out_spec = pl.BlockSpec((1, tr, C), lambda i, w: (w[1], i, 0))
        out_shape = (2, R, C)
        pick = lambda ref: ref[...]
    elif kind == "sq":
        R, C = land.shape[1:]
        grid = (1,)
        own_spec = pl.BlockSpec((1, R, C), lambda i, w: (w[0], 0, 0))
        land_spec = pl.BlockSpec((3, R, C), lambda i, w: (0, 0, 0))
        out_spec = pl.BlockSpec((1, R, C), lambda i, w: (w[1], 0, 0))
        out_shape = (2, R, C)
        pick = lambda ref: ref[0]
    else:
        B, R, C = land.shape[1:]
        grid = (1,)
        own_spec = pl.BlockSpec((B, 1, R, C), lambda i, w: (0, w[0], 0, 0))
        land_spec = pl.BlockSpec((3, B, R, C), lambda i, w: (0, 0, 0, 0))
        out_spec = pl.BlockSpec((B, 1, R, C), lambda i, w: (0, w[1], 0, 0))
        out_shape = (B, 2, R, C)
        pick = lambda ref: ref[:, 0]

    def body(w_ref, own_ref, l_ref, o_ref):
        total = ((pick(own_ref) + l_ref[0].astype(F32)) + l_ref[1].astype(F32)) + l_ref[2].astype(F32)
        if kind == "in":
            o_ref[0] = total
        elif kind == "sq":
            o_ref[0] = total
        else:
            o_ref[:, 0] = total

    grid_spec = pltpu.PrefetchScalarGridSpec(num_scalar_prefetch=1, grid=grid, in_specs=[own_spec, land_spec], out_specs=out_spec)
    return pl.pallas_call(
        body, out_shape=jax.ShapeDtypeStruct(out_shape, F32), grid_spec=grid_spec, name=f"sum_landed_{tag}",
        compiler_params=_cp("parallel"),
    )(where, own, land)


def _adamw_shard(g, w, m, v, tag):
    R, C = w.shape
    tr = min(R, 256)

    def body(g_ref, w_ref, m_ref, v_ref, d_ref, nm_ref, nv_ref):
        d, nm, nv = _adamw(w_ref[...], g_ref[...], m_ref[...], v_ref[...])
        d_ref[...] = d
        nm_ref[...] = nm
        nv_ref[...] = nv

    spec = pl.BlockSpec((tr, C), lambda i: (i, 0))
    sds = jax.ShapeDtypeStruct((R, C), F32)
    return pl.pallas_call(
        body, out_shape=(sds,) * 3, grid=(R // tr,), name=f"adamw_{tag}",
        in_specs=[spec] * 4, out_specs=(spec,) * 3, compiler_params=_cp("parallel"),
    )(g, w, m, v)


def _adamw_w_ada(c_t, dmod_cols, w, m, v):
    R, C = w.shape

    def body(ct_ref, dm_ref, w_ref, m_ref, v_ref, g_ref, d_ref, nm_ref, nv_ref):
        g = _dot(ct_ref[...].astype(BF16), dm_ref[...].astype(BF16))
        d, nm, nv = _adamw(w_ref[...], g, m_ref[...], v_ref[...])
        g_ref[...] = g
        d_ref[...] = d
        nm_ref[...] = nm
        nv_ref[...] = nv

    tr = 256
    spec = pl.BlockSpec((tr, C), lambda i: (i, 0))
    sds = jax.ShapeDtypeStruct((R, C), F32)
    return pl.pallas_call(
        body, out_shape=(sds,) * 4, grid=(R // tr,), name="adamw_w_ada",
        in_specs=[pl.BlockSpec((tr, 128), lambda i: (i, 0)), pl.BlockSpec((128, C), lambda i: (0, 0))] + [spec] * 3,
        out_specs=(spec,) * 4, compiler_params=_cp("parallel"),
    )(c_t, dmod_cols, w, m, v)


def _adamw_small(small_all, ws, ms, vs):
    def body(s_ref, w_ref, m_ref, v_ref, g_ref, d_ref, nm_ref, nv_ref):
        g = s_ref[0]
        for b in range(1, N_DEV):
            g = g + s_ref[b]
        d, nm, nv = _adamw(w_ref[...], g, m_ref[...], v_ref[...])
        g_ref[...] = g
        d_ref[...] = d
        nm_ref[...] = nm
        nv_ref[...] = nv

    sds = jax.ShapeDtypeStruct((SMALL_ROWS, D_MODEL), F32)
    return pl.pallas_call(
        body, out_shape=(sds,) * 4, name="adamw_small", in_specs=[VMEM_SPEC] * 4, out_specs=(VMEM_SPEC,) * 4,
        compiler_params=pltpu.CompilerParams(vmem_limit_bytes=VMEM_LIMIT_V7X),
    )(small_all, ws, ms, vs)


ROW_MOD, ROW_NORM_G, ROW_CONV_B, ROW_BA, ROW_BX, ROW_LAM, ROW_FINAL_G, ROW_SINKS, ROW_CONV_W = 0, 3, 4, 5, 6, 7, 8, 9, 10


def _pack_small(b_ada, norm_g, conv_b, ba, bx, lam, final_g, sinks, conv_w_full):
    rows = [b_ada.reshape(3, D_MODEL), norm_g, conv_b, ba, bx, lam, final_g.reshape(1, D_MODEL),
            jnp.pad(sinks.reshape(1, -1), ((0, 0), (0, D_MODEL - sinks.size))), conv_w_full,
            jnp.zeros((SMALL_ROWS - 14, D_MODEL), F32)]
    return jnp.concatenate([r.astype(F32) for r in rows], axis=0)


def kernel(x, c, positions, w_ada, b_ada, norm_g, w_in, attn_sinks, conv_w, conv_b, rg_wa, rg_ba, rg_wx, rg_bx, rg_lambda, w_attn_proj, w_rnn_proj, w_out, final_g, loss_target, m_w_ada, m_b_ada, m_norm_g, m_w_in, m_attn_sinks, m_conv_w, m_conv_b, m_rg_wa, m_rg_ba, m_rg_wx, m_rg_bx, m_rg_lambda, m_w_attn_proj, m_w_rnn_proj, m_w_out, m_final_g, v_w_ada, v_b_ada, v_norm_g, v_w_in, v_attn_sinks, v_conv_w, v_conv_b, v_rg_wa, v_rg_ba, v_rg_wx, v_rg_bx, v_rg_lambda, v_w_attn_proj, v_w_rnn_proj, v_w_out, v_final_g):
    T = x.shape[1]
    my_chip = lax.axis_index("x") * 2 + lax.axis_index("y")
    my_dev = my_chip * 2 + lax.axis_index("c")
    x2d, tgt = x[0], loss_target[0]
    pos_col = positions.reshape(T, 1)

    chip_idx = my_chip.reshape(1).astype(jnp.int32)
    c_idx = lax.axis_index("c").reshape(1).astype(jnp.int32)
    sq_place = ((D_MODEL, D_MODEL), (SHARD_ROWS, D_MODEL), lambda chip: (chip, 0))
    rg_place = ((RNN_BLOCKS, RNN_BW, RNN_BW), (RNN_BLOCKS, SHARD_RG, RNN_BW), lambda chip: (0, chip, 0))
    placed = [
        _cast_place(w_in[0], chip_idx, (D_MODEL, IN_W), (D_MODEL, SHARD_IN), lambda chip: (0, chip), "w_in"),
        _cast_place(w_attn_proj[0], chip_idx, *sq_place, "w_attn_proj"),
        _cast_place(w_rnn_proj[0], chip_idx, *sq_place, "w_rnn_proj"),
        _cast_place(w_out[0], chip_idx, *sq_place, "w_out"),
        _cast_place(rg_wa[0], chip_idx, *rg_place, "rg_wa"),
        _cast_place(rg_wx[0], chip_idx, *rg_place, "rg_wx"),
    ]
    g_ssems, g_rsems, fulls, g_token = _gather_start([p.reshape(s) for p, s in zip(placed, FULL_SHAPES)])
    cw_chips, c_all, mod_chips = _gather_mod(c.reshape(1, 1, D_MODEL) + g_token[0, 0], w_ada[0], conv_w[0])
    conv_w_f = jnp.transpose(cw_chips, (1, 0, 2)).reshape(CONV_W, D_MODEL)
    mod_all = jnp.transpose(mod_chips, (1, 0, 2)).reshape(N_DEV, ADA_W) + b_ada
    mod_row = lax.dynamic_slice_in_dim(mod_all, my_dev, 1, axis=0)

    tabs = _rope_tables(pos_col)
    h = _prenorm(x2d, mod_row, norm_g)
    w_in_v = fulls[0]
    proj = _in_projection(h, w_in_v.reshape(D_MODEL, IN_W), chip_idx, None, "own")
    for k, mask in enumerate(CHIP_MASKS):
        w_in_v = _gather_wait(g_ssems[k], g_rsems[k], [w_in_v], [0], proj, f"w_in_{k}")[0]
        w_in_v = _forward_halves([w_in_v], [(0, 0, k)], f"w_in_{k}")[0]
        from_chip = (chip_idx ^ (mask >> 1)).astype(jnp.int32)
        proj = _in_projection(h, w_in_v.reshape(D_MODEL, IN_W), from_chip, proj, f"from_{k}")
    w_in_f = w_in_v.reshape(D_MODEL, IN_W)
    rest = _gather_wait(g_ssems[3], g_rsems[3], list(fulls[1:]), [1, 2, 3, 4, 5], proj, "rest")
    rest = _forward_halves(rest, [(idx - 1, idx, k) for idx in range(1, N_BIG) for k in range(3)], "rest")
    wap_f, wrp_f, wo_f = (g.reshape(D_MODEL, D_MODEL) for g in rest[0:3])
    rwa_f, rwx_f = (g.reshape(RNN_BLOCKS, RNN_BW, RNN_BW) for g in rest[3:5])
    y_attn = _attn_forward(proj, tabs, attn_sinks)
    y_rnn, h_rnn = _rnn_forward(proj, pos_col, conv_w_f, conv_b, rwa_f, rwx_f, rg_ba, rg_bx, rg_lambda)
    (dx2, merged, d_o, d_pa, d_pr, d_ya, d_yr, d_c, d_final_g, d_gate, loss_vec) = _merge_and_head(
        x2d, tgt, y_attn, y_rnn, proj, wap_f, wrp_f, wo_f, mod_row, final_g.reshape(1, D_MODEL))

    sq = (N_CHIPS, 2, SHARD_ROWS // 2, D_MODEL)
    rg = (RNN_BLOCKS, N_CHIPS, 2, SHARD_RG // 2, RNN_BW)
    rg_flat = (RNN_BLOCKS * N_CHIPS, 2, SHARD_RG // 2, RNN_BW)

    def chip_sum_and_start(views, axes, flat, unflat, tags_, kinds_, group):
        from_sib = _swap_halves(views, axes)
        sums = [_presum(v.reshape(f), s.reshape(f[:1] + f[2:]), c_idx, t) for v, s, f, t in zip(views, from_sib, flat, tags_)]
        exact = [s[0].reshape(u) for s, u in zip(sums, unflat)]
        rounded = [s[1].reshape(u) for s, u in zip(sums, unflat)]
        return _exchange_start(rounded, kinds_, group), exact

    g_ap = _weight_grad(y_attn, [(d_pa, 0, 2)], "w_attn_proj")
    g_rp = _weight_grad(y_rnn, [(d_pr, 0, 2)], "w_rnn_proj")
    g_o = _weight_grad(merged, [(d_o, 0, 2)], "w_out")
    sq_half = (N_CHIPS, SHARD_ROWS // 2, D_MODEL)
    started1, own1 = chip_sum_and_start([g_ap.reshape(sq), g_rp.reshape(sq), g_o.reshape(sq)], [1, 1, 1], [sq] * 3, [sq_half] * 3,
                                  ["w_attn_proj", "w_rnn_proj", "w_out"], ["sq"] * 3, "proj")
    d_q, d_kv, d_ga, d_sinks = _attn_backward(proj, d_ya, tabs, attn_sinks + started1[4][0, 0])
    d_b, d_conv_w, d_conv_b, d_rwa, d_rwx, d_ba, d_bx, d_lam = _rnn_backward(
        proj, pos_col, h_rnn, d_yr, conv_w_f, conv_b, rwa_f, rwx_f, rg_ba, rg_bx, rg_lambda)
    pieces = [(d_q, CB_Q, 2), (d_kv, CB_KV, 1), (d_ga, CB_GA, 2), (d_b, CB_XR, 4), (d_c, CB_MA, 4)]
    g_in = _weight_grad(h, pieces, "w_in")
    started2, own2 = chip_sum_and_start(
        [g_in.reshape(2, D_MODEL // 2, IN_W), d_rwa.reshape(rg), d_rwx.reshape(rg)], [0, 2, 2],
        [(1, 2, D_MODEL // 2, IN_W), rg_flat, rg_flat],
        [(D_MODEL // 2, IN_W), (RNN_BLOCKS, N_CHIPS, SHARD_RG // 2, RNN_BW), (RNN_BLOCKS, N_CHIPS, SHARD_RG // 2, RNN_BW)],
        ["w_in", "rg_wa", "rg_wx"], ["in", "rg", "rg"], "in")
    grad_x, d_shift, d_scale, d_norm_g = _input_backward(pieces, w_in_f, x2d, dx2, mod_row + started2[4][0, 0], norm_g)

    d_mod = jnp.concatenate([d_shift, d_scale, d_gate], axis=1)
    small = _pack_small(d_mod, d_norm_g, d_conv_b, d_ba, d_bx, d_lam, d_final_g, d_sinks[:, :N_HEADS], d_conv_w)
    small_all = _gather_small(small)
    _, lands1 = _exchange_wait(*started1[:4], grad_x, "proj")
    _, lands2 = _exchange_wait(*started2[:4], grad_x, "in")
    tags = ["w_in", "w_attn_proj", "w_rnn_proj", "w_out", "rg_wa", "rg_wx"]
    chip_sums = [own2[0]] + list(own1) + list(own2[1:])
    lands = [lands2[0]] + list(lands1) + list(lands2[1:])
    where = jnp.concatenate([chip_idx, c_idx])
    kinds = ["in", "sq", "sq", "sq", "rg", "rg"]
    halves = [_sum_landed(kinds[i], chip_sums[i], lands[i], where, tags[i]) for i in range(6)]
    grads = _assemble_with_sibling(halves, [0, 0, 0, 0, 1, 1])
    shapes2d = [(D_MODEL, SHARD_IN), (SHARD_ROWS, D_MODEL), (SHARD_ROWS, D_MODEL), (SHARD_ROWS, D_MODEL),
                (RNN_BLOCKS * SHARD_RG, RNN_BW), (RNN_BLOCKS * SHARD_RG, RNN_BW)]
    big_w = [w_in, w_attn_proj, w_rnn_proj, w_out, rg_wa, rg_wx]
    big_m = [m_w_in, m_w_attn_proj, m_w_rnn_proj, m_w_out, m_rg_wa, m_rg_wx]
    big_v = [v_w_in, v_w_attn_proj, v_w_rnn_proj, v_w_out, v_rg_wa, v_rg_wx]
    res = {}
    for i, tag in enumerate(tags):
        g = grads[i].reshape(shapes2d[i])
        outs = _adamw_shard(g, big_w[i].reshape(shapes2d[i]), big_m[i].reshape(shapes2d[i]), big_v[i].reshape(shapes2d[i]), tag)
        res[tag] = [o.reshape(big_w[i].shape) for o in (g,) + tuple(outs)]

    dmod_all = small_all[:, ROW_MOD:ROW_MOD + 3, :].reshape(N_DEV, ADA_W)
    dmod_cols = lax.dynamic_slice_in_dim(dmod_all, my_chip * SHARD_ADA, SHARD_ADA, axis=1)
    c_t = jnp.pad(jnp.transpose(c_all.reshape(N_DEV, D_MODEL)), ((0, 0), (0, 128 - N_DEV)))
    dmod_cols = jnp.pad(dmod_cols, ((0, 128 - N_DEV), (0, 0)))
    res["w_ada"] = [o.reshape(w_ada.shape) for o in _adamw_w_ada(c_t, dmod_cols, w_ada[0], m_w_ada[0], v_w_ada[0])]

    def full_conv(a):
        return lax.dynamic_update_slice_in_dim(jnp.zeros((CONV_W, D_MODEL), F32), a[0], my_chip * (D_MODEL // N_CHIPS), axis=1)

    packed = [_pack_small(p[0], p[1], p[2], p[3], p[4], p[5], p[6], p[7], full_conv(p[8])) for p in (
        (b_ada, norm_g, conv_b, rg_ba, rg_bx, rg_lambda, final_g, attn_sinks, conv_w),
        (m_b_ada, m_norm_g, m_conv_b, m_rg_ba, m_rg_bx, m_rg_lambda, m_final_g, m_attn_sinks, m_conv_w),
        (v_b_ada, v_norm_g, v_conv_b, v_rg_ba, v_rg_bx, v_rg_lambda, v_final_g, v_attn_sinks, v_conv_w))]
    small_out = _adamw_small(small_all, *packed)

    def unpack(slab):
        cw = lax.dynamic_slice_in_dim(slab[ROW_CONV_W:ROW_CONV_W + CONV_W], my_chip * (D_MODEL // N_CHIPS),
                                      D_MODEL // N_CHIPS, axis=1)
        return {
            "b_ada": slab[ROW_MOD:ROW_MOD + 3].reshape(1, ADA_W), "norm_g": slab[ROW_NORM_G:ROW_NORM_G + 1],
            "conv_b": slab[ROW_CONV_B:ROW_CONV_B + 1], "rg_ba": slab[ROW_BA:ROW_BA + 1], "rg_bx": slab[ROW_BX:ROW_BX + 1],
            "rg_lambda": slab[ROW_LAM:ROW_LAM + 1], "final_g": slab[ROW_FINAL_G], "attn_sinks": slab[ROW_SINKS:ROW_SINKS + 1, :N_HEADS],
            "conv_w": cw[None],
        }

    small_res = [unpack(s) for s in small_out]
    order = ["w_ada", "b_ada", "norm_g", "w_in", "attn_sinks", "conv_w", "conv_b", "rg_wa", "rg_ba", "rg_wx", "rg_bx",
             "rg_lambda", "w_attn_proj", "w_rnn_proj", "w_out", "final_g"]
    loss = lax.psum(loss_vec[0, 0], ("x", "y", "c"))
    outs = [loss, grad_x[None]]
    for kind in range(4):
        for name in order:
            outs.append(res[name][kind] if name in res else small_res[kind][name])
    return tuple(outs)
```

```python
import numpy as np
import jax
import jax.numpy as jnp
from jax import lax
from jax.experimental import pallas as pl
from jax.experimental.pallas import tpu as pltpu

F32 = jnp.float32
BF16 = jnp.bfloat16

D_MODEL = 1024
N_HEADS = 16
N_KV = 4
HEAD_DIM = 64
GROUP = N_HEADS // N_KV
BLOCK = 128
KV_W = N_KV * HEAD_DIM
ROT_HALF = 8
ROPE_THETA = 500000.0
ATTN_SCALE = 0.125
RNN_BLOCKS = 4
RNN_BW = 256
CONV_W = 4
LRU_C = 8.0
NORM_EPS = 1e-6
IN_W = 6656
CB = 512
N_CB = IN_W // CB
CB_Q, CB_KV, CB_GA, CB_XR, CB_GR, CB_MA, CB_MR = 0, 2, 3, 5, 7, 9, 11
N_CHIPS = 4
N_DEV = 8
SHARD_IN = IN_W // N_CHIPS
SHARD_ROWS = D_MODEL // N_CHIPS
SHARD_RG = RNN_BW // N_CHIPS
ADA_W = 3 * D_MODEL
SHARD_ADA = ADA_W // N_CHIPS
SMALL_ROWS = 16

ADAM_LR = 0.001
ADAM_B1 = 0.9
ADAM_B2 = 0.999
ADAM_EPS = 1e-08
ADAM_WD = 0.01
ADAM_STEP = 10

VMEM_LIMIT_V7X = 52 * 1024 * 1024
MESH = pl.DeviceIdType.MESH
ANY = pl.BlockSpec(memory_space=pl.ANY)
VMEM_SPEC = pl.BlockSpec(memory_space=pltpu.VMEM)


def _cp(*sem):
    return pltpu.CompilerParams(dimension_semantics=sem if sem else None, vmem_limit_bytes=VMEM_LIMIT_V7X)


def _dot(a, b):
    return jnp.dot(a, b, preferred_element_type=F32)


def _dot_nt(a, b):
    return lax.dot_general(a, b, (((1,), (1,)), ((), ())), preferred_element_type=F32)


def _dot_tn(a, b):
    return lax.dot_general(a, b, (((0,), (0,)), ((), ())), preferred_element_type=F32)


def _sigmoid(z):
    return 1.0 / (1.0 + jnp.exp(-z))


def _neg_expm1(z):
    series = -(z * (1.0 + z * (0.5 + z * (1.0 / 6.0 + z * (1.0 / 24.0 + z * (1.0 / 120.0))))))
    return jnp.where(z > -0.05, series, 1.0 - jnp.exp(z))


def _softplus(z):
    u = jnp.exp(-jnp.abs(z))
    log1p_u = jnp.where(u < 1e-3, u * (1.0 - u * (0.5 - u * (1.0 / 3.0))), jnp.log(1.0 + u))
    return jnp.maximum(z, 0.0) + log1p_u


def _rms(xf):
    return lax.rsqrt(jnp.mean(xf * xf, axis=-1, keepdims=True) + NORM_EPS)


def _me():
    return lax.axis_index("x"), lax.axis_index("y"), lax.axis_index("c")


def _peer(mask):
    x, y, c = _me()
    fx, fy, fc = (mask >> 2) & 1, (mask >> 1) & 1, mask & 1
    return (x ^ fx if fx else x, y ^ fy if fy else y, c ^ fc if fc else c)


def _chip_of(pos):
    return pos[0] * 2 + pos[1]


CHIP_MASKS = (4, 2, 6)
ALL_MASKS = (1, 2, 3, 4, 5, 6, 7)


HBM_SPEC = pl.BlockSpec(memory_space=pltpu.HBM)
SEM_SPEC = pl.BlockSpec(memory_space=pltpu.SEMAPHORE)
SPLIT_COPY = pltpu.CompilerParams(has_side_effects=pltpu.SideEffectType.DATAFLOW_SIDE_EFFECTING)
N_BIG = 6
FULL_SHAPES = (
    (2, D_MODEL // 2, IN_W),
    (N_CHIPS, 2, SHARD_ROWS // 2, D_MODEL), (N_CHIPS, 2, SHARD_ROWS // 2, D_MODEL), (N_CHIPS, 2, SHARD_ROWS // 2, D_MODEL),
    (RNN_BLOCKS, N_CHIPS, 2, SHARD_RG // 2, RNN_BW), (RNN_BLOCKS, N_CHIPS, 2, SHARD_RG // 2, RNN_BW),
)


def _slot(full, idx, chip, half):
    if idx == 0:
        return full.at[half, :, pl.ds(pl.multiple_of(chip * SHARD_IN, 128), SHARD_IN)]
    return full.at[chip, half] if idx in (1, 2, 3) else full.at[:, chip, half]


def _three_halves(full, idx):
    return full.at[pl.ds(0, 3), 0] if idx in (1, 2, 3) else full.at[:, pl.ds(0, 3), 0]


def _gather_start(fulls, after):
    def body(*refs):
        full_refs = refs[:N_BIG]
        ssems, rsems = refs[N_BIG + 1:N_BIG + 5], refs[N_BIG + 5:N_BIG + 9]
        token = refs[2 * N_BIG + 9]
        me = _me()
        my_chip = _chip_of(me)
        for idx in range(N_BIG):
            for k, mask in enumerate(CHIP_MASKS):
                pair = k if idx == 0 else 3
                mine = _slot(full_refs[idx], idx, my_chip, me[2])
                pltpu.make_async_remote_copy(src_ref=mine, dst_ref=mine, send_sem=ssems[pair], recv_sem=rsems[pair],
                                             device_id=_peer(mask), device_id_type=MESH).start()
        token[...] = jnp.zeros_like(token)

    sem = pltpu.SemaphoreType.DMA(())
    out_shape = (sem,) * 8 + tuple(pltpu.HBM(f.shape, f.dtype) for f in fulls) + (jax.ShapeDtypeStruct((8, 128), F32),)
    outs = pl.pallas_call(
        body, out_shape=out_shape, name="gather_start",
        in_specs=[HBM_SPEC] * N_BIG + [ANY], out_specs=tuple([SEM_SPEC] * 8 + [HBM_SPEC] * N_BIG + [VMEM_SPEC]),
        input_output_aliases={i: 8 + i for i in range(N_BIG)}, compiler_params=SPLIT_COPY,
    )(*[pltpu.with_memory_space_constraint(f, pltpu.HBM) for f in fulls], after)
    return outs[0:4], outs[4:8], outs[8:8 + N_BIG], outs[8 + N_BIG]


def _gather_wait(ssem, rsem, arrays, idxs, after, tag):
    n = len(arrays)

    def body(*refs):
        full_refs, ssem_ref, rsem_ref = refs[:n], refs[n], refs[n + 1]
        me = _me()
        for full, idx in zip(full_refs, idxs):
            region = _slot(full, 0, _chip_of(me), me[2]) if idx == 0 else _three_halves(full, idx)
            arrived = pltpu.make_async_remote_copy(
                src_ref=region, dst_ref=region, send_sem=ssem_ref, recv_sem=rsem_ref, device_id=me, device_id_type=MESH)
            arrived.wait_send()
            arrived.wait_recv()

    outs = pl.pallas_call(
        body, out_shape=tuple(pltpu.HBM(a.shape, a.dtype) for a in arrays), name=f"gather_wait_{tag}",
        in_specs=[HBM_SPEC] * n + [SEM_SPEC, SEM_SPEC, ANY], out_specs=tuple([HBM_SPEC] * n),
        input_output_aliases={i: i for i in range(n)}, compiler_params=SPLIT_COPY,
    )(*arrays, ssem, rsem, after)
    return list(outs)


def _forward_halves(arrays, items, tag):
    n, m = len(arrays), len(items)

    def body(*refs):
        outs, ssem, rsem = refs[n:2 * n], refs[2 * n], refs[2 * n + 1]
        me = _me()
        sib = _peer(1)
        cps = []
        for j, (pos, idx, k) in enumerate(items):
            chip = _chip_of(_peer(CHIP_MASKS[k]))
            cp = pltpu.make_async_remote_copy(
                src_ref=_slot(outs[pos], idx, chip, me[2]), dst_ref=_slot(outs[pos], idx, chip, me[2]),
                send_sem=ssem.at[j], recv_sem=rsem.at[j], device_id=sib, device_id_type=MESH)
            cp.start()
            cps.append(cp)
        for j, (pos, idx, k) in enumerate(items):
            chip = _chip_of(_peer(CHIP_MASKS[k]))
            pltpu.make_async_remote_copy(
                src_ref=_slot(outs[pos], idx, chip, me[2]), dst_ref=_slot(outs[pos], idx, chip, 1 - me[2]),
                send_sem=ssem.at[j], recv_sem=rsem.at[j], device_id=sib, device_id_type=MESH).wait_recv()
        for cp in cps:
            cp.wait_send()

    outs = pl.pallas_call(
        body, out_shape=tuple(jax.ShapeDtypeStruct(a.shape, a.dtype) for a in arrays), name=f"forward_halves_{tag}",
        in_specs=[ANY] * n, out_specs=tuple([ANY] * n), input_output_aliases={i: i for i in range(n)},
        scratch_shapes=[pltpu.SemaphoreType.DMA((m,)), pltpu.SemaphoreType.DMA((m,))],
    )(*arrays)
    return list(outs)


def _gather_mod(c_row, w_ada_s, conv_w_s):
    def body(c_ref, wada_ref, cw_s, cw_f, call_ref, mod_ref, wsend, wrecv, lsem, csend, crecv, msend, mrecv):
        me = _me()
        my_chip = _chip_of(me)
        my_dev = my_chip * 2 + me[2]
        sends = []
        for k, mask in enumerate(CHIP_MASKS):
            cp = pltpu.make_async_remote_copy(src_ref=cw_s, dst_ref=cw_f.at[my_chip], send_sem=wsend.at[k], recv_sem=wrecv.at[k],
                                              device_id=_peer(mask), device_id_type=MESH)
            cp.start()
            sends.append(cp)
        local = [pltpu.make_async_copy(cw_s, cw_f.at[my_chip], lsem.at[0])]
        for cp in local:
            cp.start()

        call_ref[my_dev] = c_ref[0]
        csends = []
        for k, mask in enumerate(ALL_MASKS):
            cp = pltpu.make_async_remote_copy(
                src_ref=c_ref.at[0], dst_ref=call_ref.at[my_dev],
                send_sem=csend.at[k], recv_sem=crecv.at[k], device_id=_peer(mask), device_id_type=MESH)
            cp.start()
            csends.append(cp)
        for k, mask in enumerate(ALL_MASKS):
            frm = _peer(mask)
            pltpu.make_async_remote_copy(
                src_ref=c_ref.at[0], dst_ref=call_ref.at[_chip_of(frm) * 2 + frm[2]],
                send_sem=csend.at[k], recv_sem=crecv.at[k], device_id=frm, device_id_type=MESH).wait_recv()
        for cp in csends:
            cp.wait_send()

        c_all = call_ref[...].reshape(N_DEV, D_MODEL).astype(BF16)
        mod_ref[my_chip] = _dot(c_all, wada_ref[...].astype(BF16))
        msends = []
        for k, mask in enumerate(CHIP_MASKS):
            cp = pltpu.make_async_remote_copy(
                src_ref=mod_ref.at[my_chip], dst_ref=mod_ref.at[my_chip],
                send_sem=msend.at[k], recv_sem=mrecv.at[k], device_id=_peer(mask), device_id_type=MESH)
            cp.start()
            msends.append(cp)
        for k, mask in enumerate(CHIP_MASKS):
            frm = _peer(mask)
            pltpu.make_async_remote_copy(
                src_ref=mod_ref.at[my_chip], dst_ref=mod_ref.at[_chip_of(frm)],
                send_sem=msend.at[k], recv_sem=mrecv.at[k], device_id=frm, device_id_type=MESH).wait_recv()
        for cp in msends:
            cp.wait_send()

        for k, mask in enumerate(CHIP_MASKS):
            frm = _peer(mask)
            pltpu.make_async_remote_copy(src_ref=cw_s, dst_ref=cw_f.at[_chip_of(frm)], send_sem=wsend.at[k], recv_sem=wrecv.at[k],
                                         device_id=frm, device_id_type=MESH).wait_recv()
        for cp in sends:
            cp.wait_send()
        for cp in local:
            cp.wait()

    out_shape = (
        jax.ShapeDtypeStruct((N_CHIPS, CONV_W, D_MODEL // N_CHIPS), F32),
        jax.ShapeDtypeStruct((N_DEV, 1, D_MODEL), F32),
        jax.ShapeDtypeStruct((N_CHIPS, N_DEV, SHARD_ADA), F32),
    )
    return pl.pallas_call(
        body, out_shape=out_shape, name="gather_mod",
        in_specs=[VMEM_SPEC, VMEM_SPEC, ANY], out_specs=(ANY, VMEM_SPEC, VMEM_SPEC),
        scratch_shapes=[
            pltpu.SemaphoreType.DMA((3,)), pltpu.SemaphoreType.DMA((3,)), pltpu.SemaphoreType.DMA((1,)),
            pltpu.SemaphoreType.DMA((7,)), pltpu.SemaphoreType.DMA((7,)),
            pltpu.SemaphoreType.DMA((3,)), pltpu.SemaphoreType.DMA((3,)),
        ],
        compiler_params=pltpu.CompilerParams(vmem_limit_bytes=VMEM_LIMIT_V7X),
    )(c_row, w_ada_s, conv_w_s)


def _cast_place(shard, chip_idx, full_shape, block, index_map, tag):
    def body(chip_ref, s_ref, o_ref):
        o_ref[...] = s_ref[...].astype(BF16)

    grid_spec = pltpu.PrefetchScalarGridSpec(
        num_scalar_prefetch=1, grid=(1,),
        in_specs=[pl.BlockSpec(shard.shape, lambda i, chip_ref: (0,) * shard.ndim)],
        out_specs=pl.BlockSpec(block, lambda i, chip_ref: index_map(chip_ref[0])))
    return pl.pallas_call(
        body, out_shape=jax.ShapeDtypeStruct(full_shape, BF16), grid_spec=grid_spec, name=f"cast_place_{tag}",
        compiler_params=_cp("arbitrary"),
    )(chip_idx, shard)


def _shard_of(ref, kind, chip):
    if kind == "in":
        return ref.at[:, pl.ds(pl.multiple_of(chip * SHARD_IN, 128), SHARD_IN)]
    return ref.at[chip] if kind == "sq" else ref.at[:, chip]


def _land_shape(src, kind):
    if kind == "in":
        return (3, src.shape[0], SHARD_IN)
    return (3,) + src.shape[1:] if kind == "sq" else (3, src.shape[0]) + src.shape[2:]


def _exchange_start(srcs, kinds, tag):
    n = len(srcs)
    lands = [pltpu.with_memory_space_constraint(lax.empty(_land_shape(s, k), s.dtype), pltpu.HBM) for s, k in zip(srcs, kinds)]

    def body(*refs):
        src_refs, land_refs = refs[:n], refs[n:2 * n]
        ssems, rsems = refs[2 * n:3 * n], refs[3 * n:4 * n]
        token = refs[6 * n]
        for i in range(n):
            for k, mask in enumerate(CHIP_MASKS):
                to = _peer(mask)
                pltpu.make_async_remote_copy(
                    src_ref=_shard_of(src_refs[i], kinds[i], _chip_of(to)), dst_ref=land_refs[i].at[k],
                    send_sem=ssems[i], recv_sem=rsems[i], device_id=to, device_id_type=MESH).start()
        token[...] = jnp.zeros_like(token)

    sem = pltpu.SemaphoreType.DMA(())
    out_shape = ((sem,) * (2 * n) + tuple(pltpu.HBM(s.shape, s.dtype) for s in srcs)
                 + tuple(pltpu.HBM(l.shape, l.dtype) for l in lands) + (jax.ShapeDtypeStruct((8, 128), F32),))
    outs = pl.pallas_call(
        body, out_shape=out_shape, name=f"exchange_start_{tag}",
        in_specs=[HBM_SPEC] * (2 * n), out_specs=tuple([SEM_SPEC] * (2 * n) + [HBM_SPEC] * (2 * n) + [VMEM_SPEC]),
        input_output_aliases={i: 2 * n + i for i in range(2 * n)},
        compiler_params=pltpu.CompilerParams(has_side_effects=pltpu.SideEffectType.DATAFLOW_SIDE_EFFECTING),
    )(*[pltpu.with_memory_space_constraint(s, pltpu.HBM) for s in srcs], *lands)
    return outs[:n], outs[n:2 * n], outs[2 * n:3 * n], outs[3 * n:4 * n], outs[4 * n]


def _exchange_wait(ssems, rsems, srcs, lands, after, tag):
    n = len(srcs)

    def body(*refs):
        land_refs = refs[n:2 * n]
        ssem_refs, rsem_refs = refs[2 * n:3 * n], refs[3 * n:4 * n]
        for i in range(n):
            all_three = pltpu.make_async_remote_copy(
                src_ref=land_refs[i], dst_ref=land_refs[i], send_sem=ssem_refs[i], recv_sem=rsem_refs[i],
                device_id=_me(), device_id_type=MESH)
            all_three.wait_send()
            all_three.wait_recv()

    outs = pl.pallas_call(
        body, out_shape=tuple(pltpu.HBM(a.shape, a.dtype) for a in list(srcs) + list(lands)), name=f"exchange_wait_{tag}",
        in_specs=[HBM_SPEC] * (2 * n) + [SEM_SPEC] * (2 * n) + [ANY], out_specs=tuple([HBM_SPEC] * (2 * n)),
        input_output_aliases={i: i for i in range(2 * n)},
        compiler_params=pltpu.CompilerParams(has_side_effects=pltpu.SideEffectType.DATAFLOW_SIDE_EFFECTING),
    )(*srcs, *lands, *ssems, *rsems, after)
    return outs[:n], outs[n:]


def _gather_small(small):
    def body(small_ref, small_all, ssend, srecv):
        me = _me()
        my_dev = _chip_of(me) * 2 + me[2]
        small_all[my_dev] = small_ref[...]
        ssends = []
        for k, mask in enumerate(ALL_MASKS):
            cp = pltpu.make_async_remote_copy(
                src_ref=small_ref, dst_ref=small_all.at[my_dev],
                send_sem=ssend.at[k], recv_sem=srecv.at[k], device_id=_peer(mask), device_id_type=MESH)
            cp.start()
            ssends.append(cp)
        for k, mask in enumerate(ALL_MASKS):
            frm = _peer(mask)
            pltpu.make_async_remote_copy(
                src_ref=small_ref, dst_ref=small_all.at[_chip_of(frm) * 2 + frm[2]],
                send_sem=ssend.at[k], recv_sem=srecv.at[k], device_id=frm, device_id_type=MESH).wait_recv()
        for cp in ssends:
            cp.wait_send()

    return pl.pallas_call(
        body, out_shape=jax.ShapeDtypeStruct((N_DEV, SMALL_ROWS, D_MODEL), F32), name="gather_small",
        in_specs=[VMEM_SPEC], out_specs=VMEM_SPEC,
        scratch_shapes=[pltpu.SemaphoreType.DMA((7,)), pltpu.SemaphoreType.DMA((7,))],
    )(small)


def _half_of(ref, axis, half):
    return ref.at[(slice(None),) * axis + (half,)]


def _swap_halves(parts, axes):
    n = len(parts)

    def body(*refs):
        ins, outs, ssem, rsem = refs[:n], refs[n:2 * n], refs[2 * n], refs[2 * n + 1]
        c = lax.axis_index("c")
        cps = [pltpu.make_async_remote_copy(src_ref=_half_of(ins[i], axes[i], 1 - c), dst_ref=outs[i], send_sem=ssem.at[i],
                                            recv_sem=rsem.at[i], device_id=_peer(1), device_id_type=MESH) for i in range(n)]
        for cp in cps:
            cp.start()
        for cp in cps:
            cp.wait()

    shapes = [p.shape[:a] + p.shape[a + 1:] for p, a in zip(parts, axes)]
    return pl.pallas_call(
        body, out_shape=tuple(jax.ShapeDtypeStruct(s, p.dtype) for s, p in zip(shapes, parts)), name="swap_halves",
        in_specs=[ANY] * n, out_specs=tuple([ANY] * n),
        scratch_shapes=[pltpu.SemaphoreType.DMA((n,)), pltpu.SemaphoreType.DMA((n,))],
    )(*parts)


def _presum(mine, sib, c_idx, tag):
    S, _, R, C = mine.shape
    tr = min(R, 256)
    tc = SHARD_IN if C % SHARD_IN == 0 else C

    def body(c_ref, m_ref, s_ref, o_ref, ob_ref):
        total = m_ref[:, 0] + s_ref[...]
        o_ref[...] = total
        ob_ref[...] = total.astype(BF16)

    out_spec = pl.BlockSpec((S, tr, tc), lambda i, j, c_ref: (0, i, j))
    grid_spec = pltpu.PrefetchScalarGridSpec(
        num_scalar_prefetch=1, grid=(R // tr, C // tc),
        in_specs=[pl.BlockSpec((S, 1, tr, tc), lambda i, j, c_ref: (0, c_ref[0], i, j)),
                  pl.BlockSpec((S, tr, tc), lambda i, j, c_ref: (0, i, j))],
        out_specs=(out_spec, out_spec))
    return pl.pallas_call(
        body, out_shape=(jax.ShapeDtypeStruct((S, R, C), F32), jax.ShapeDtypeStruct((S, R, C), BF16)),
        grid_spec=grid_spec, name=f"presum_{tag}", compiler_params=_cp("parallel", "parallel"),
    )(c_idx, mine, sib)


def _assemble_with_sibling(parts, axes):
    n = len(parts)

    def body(*refs):
        outs, ssem, rsem = refs[n:2 * n], refs[2 * n], refs[2 * n + 1]
        c = lax.axis_index("c")
        cps = [pltpu.make_async_remote_copy(
            src_ref=_half_of(outs[i], axes[i], c), dst_ref=_half_of(outs[i], axes[i], c), send_sem=ssem.at[i],
            recv_sem=rsem.at[i], device_id=_peer(1), device_id_type=MESH) for i in range(n)]
        for cp in cps:
            cp.start()
        for i in range(n):
            pltpu.make_async_remote_copy(
                src_ref=_half_of(outs[i], axes[i], c), dst_ref=_half_of(outs[i], axes[i], 1 - c), send_sem=ssem.at[i],
                recv_sem=rsem.at[i], device_id=_peer(1), device_id_type=MESH).wait_recv()
        for cp in cps:
            cp.wait_send()

    return pl.pallas_call(
        body, out_shape=tuple(jax.ShapeDtypeStruct(p.shape, p.dtype) for p in parts), name="assemble_with_sibling",
        in_specs=[ANY] * n, out_specs=tuple([ANY] * n), input_output_aliases={i: i for i in range(n)},
        scratch_shapes=[pltpu.SemaphoreType.DMA((n,)), pltpu.SemaphoreType.DMA((n,))],
    )(*parts)


def _rope_tables(pos_col):
    T = pos_col.shape[0]
    tm = min(T, 512)
    inv = np.float32(ROPE_THETA) ** (-(np.arange(0, 2 * ROT_HALF, 2, dtype=np.float32)) / np.float32(2 * ROT_HALF))
    lane = np.arange(128) % HEAD_DIM
    freq = np.where(lane < 2 * ROT_HALF, inv[lane % ROT_HALF], 0.0).astype(np.float32)[None, :]

    def body(pos_ref, f_ref, c_ref, sa_ref, sb_ref):
        ang = pos_ref[...].astype(F32) * f_ref[...]
        c, s = jnp.cos(ang), jnp.sin(ang)
        m = lax.broadcasted_iota(jnp.int32, ang.shape, 1) & (HEAD_DIM - 1)
        c_ref[...] = jnp.where(m < 2 * ROT_HALF, c, 1.0)
        sa_ref[...] = jnp.where(m < ROT_HALF, -s, 0.0)
        sb_ref[...] = jnp.where((m >= ROT_HALF) & (m < 2 * ROT_HALF), s, 0.0)

    tab = jax.ShapeDtypeStruct((T, 128), F32)
    return pl.pallas_call(
        body, out_shape=(tab, tab, tab), grid=(T // tm,), name="rope_tables",
        in_specs=[pl.BlockSpec((tm, 1), lambda i: (i, 0)), pl.BlockSpec((1, 128), lambda i: (0, 0))],
        out_specs=tuple(pl.BlockSpec((tm, 128), lambda i: (i, 0)) for _ in range(3)),
        compiler_params=_cp("parallel"),
    )(pos_col, jnp.asarray(freq))


def _wide(tab, width):
    del width
    return tab


def _columns(t):
    return [t[:, i:i + 128] for i in range(0, t.shape[-1], 128)]


def _rope(t, c, sa, sb):
    return jnp.concatenate(
        [x * c + pltpu.roll(x, 128 - ROT_HALF, 1) * sa + pltpu.roll(x, ROT_HALF, 1) * sb for x in _columns(t)], axis=1)


def _unrope(d, c, sa, sb):
    return jnp.concatenate(
        [x * c + pltpu.roll(x * sa, ROT_HALF, 1) + pltpu.roll(x * sb, 128 - ROT_HALF, 1) for x in _columns(d)], axis=1)


def _prenorm(x, mod_row, norm_g):
    T = x.shape[0]
    tm = min(T, 512)

    def body(x_ref, mod_ref, g_ref, h_ref):
        xf = x_ref[...]
        shift, scale = mod_ref[:, 0:D_MODEL], mod_ref[:, D_MODEL:2 * D_MODEL]
        h = (xf * _rms(xf)) * g_ref[...] * (1.0 + scale) + shift
        h_ref[...] = h.astype(BF16)

    return pl.pallas_call(
        body, out_shape=jax.ShapeDtypeStruct((T, D_MODEL), BF16), grid=(T // tm,), name="prenorm",
        in_specs=[pl.BlockSpec((tm, D_MODEL), lambda i: (i, 0)), pl.BlockSpec((1, ADA_W), lambda i: (0, 0)),
                  pl.BlockSpec((1, D_MODEL), lambda i: (0, 0))],
        out_specs=pl.BlockSpec((tm, D_MODEL), lambda i: (i, 0)),
        compiler_params=_cp("parallel"),
    )(x, mod_row, norm_g)


def _in_projection(h, w_in, chip, into, tag):
    T = h.shape[0]
    tm, tn = min(T, 512), SHARD_IN

    def body(chip_ref, h_ref, w_ref, *rest):
        rest[-1][...] = _dot(h_ref[...], w_ref[...])

    in_specs = [pl.BlockSpec((tm, D_MODEL), lambda i, c: (i, 0)),
                pl.BlockSpec((D_MODEL, tn), lambda i, c: (0, c[0]), pipeline_mode=pl.Buffered(1))]
    args = [chip, h, w_in]
    aliases = {}
    if into is not None:
        in_specs.append(ANY)
        args.append(into)
        aliases = {3: 0}
    grid_spec = pltpu.PrefetchScalarGridSpec(num_scalar_prefetch=1, grid=(T // tm,), in_specs=in_specs,
                                             out_specs=pl.BlockSpec((tm, tn), lambda i, c: (i, c[0])))
    return pl.pallas_call(
        body, out_shape=jax.ShapeDtypeStruct((T, IN_W), F32), grid_spec=grid_spec, name=f"in_projection_{tag}",
        input_output_aliases=aliases, compiler_params=_cp("parallel"),
    )(*args)


def _attn_mask(n):
    qi = lax.broadcasted_iota(jnp.int32, (GROUP * BLOCK, 2 * BLOCK), 0) & (BLOCK - 1)
    kj = lax.broadcasted_iota(jnp.int32, (GROUP * BLOCK, 2 * BLOCK), 1)
    diff = qi + BLOCK - kj
    return (diff >= 0) & (diff < BLOCK) & ((kj >= BLOCK) | (n > 0))


ROW_GROUP_HEAD = (0, 2, 1, 3)


def _sink_col(sink_ref, kh):
    rowg = lax.broadcasted_iota(jnp.int32, (GROUP * BLOCK, 1), 0) // BLOCK
    col = jnp.full((GROUP * BLOCK, 1), sink_ref[0, GROUP * kh + ROW_GROUP_HEAD[0]], F32)
    for g in range(1, GROUP):
        col = jnp.where(rowg == g, sink_ref[0, GROUP * kh + ROW_GROUP_HEAD[g]], col)
    return col


def _low_lanes(shape):
    return lax.broadcasted_iota(jnp.int32, shape, 1) < HEAD_DIM


def _kv_pair_operand(prev, cur, kh):
    c = 128 * (kh // 2)
    col = jnp.concatenate([prev[:, c:c + 128], cur[:, c:c + 128]], axis=0)
    if kh % 2 == 0:
        lo = jnp.where(_low_lanes(col.shape), col, 0.0)
        hi = pltpu.roll(lo, HEAD_DIM, 1)
    else:
        hi = jnp.where(_low_lanes(col.shape), 0.0, col)
        lo = pltpu.roll(hi, HEAD_DIM, 1)
    return jnp.concatenate([lo, hi], axis=0).astype(BF16)


def _pair_rows(x, kh):
    c = 2 * 128 * kh
    return jnp.concatenate([x[:, c:c + 128], x[:, c + 128:c + 256]], axis=0)


def _restack(big):
    return jnp.concatenate([big[:, 0:2 * BLOCK], big[:, 2 * BLOCK:4 * BLOCK]], axis=0)


def _unrestack(stacked):
    return jnp.concatenate([stacked[0:2 * BLOCK], stacked[2 * BLOCK:4 * BLOCK]], axis=1)


def _fold_pair(x2, kh):
    low = _low_lanes((2 * BLOCK, 128))
    mixed = jnp.where(low, x2[0:2 * BLOCK], x2[2 * BLOCK:4 * BLOCK])
    total = mixed + pltpu.roll(mixed, HEAD_DIM, 1)
    return jnp.where(low, total, 0.0) if kh % 2 == 0 else jnp.where(low, 0.0, total)


def _attn_scores(qr, k2, kh):
    q2 = _pair_rows(qr, kh).astype(BF16)
    return q2, _restack(_dot_nt(q2, k2))


def _attn_softmax(s, sink_col, mask):
    s = jnp.where(mask, s, -1e30)
    m = jnp.maximum(jnp.max(s, axis=-1, keepdims=True), sink_col)
    p = jnp.exp(s - m)
    p_sink = jnp.exp(sink_col - m)
    denom = jnp.sum(p, axis=-1, keepdims=True) + p_sink
    return p / denom, p_sink / denom


def _attn_forward(proj, tabs, sinks):
    T = proj.shape[0]
    nb = T // BLOCK

    def body(q_ref, kvc_ref, kvp_ref, g0_ref, g1_ref, cc, sac, sbc, cp_, sap, sbp, sink_ref, y_ref):
        n = pl.program_id(0)
        tc = (_wide(cc[...], D_MODEL), _wide(sac[...], D_MODEL), _wide(sbc[...], D_MODEL))
        tcur = tuple(t[:, :KV_W] for t in tc)
        tprev = (_wide(cp_[...], KV_W), _wide(sap[...], KV_W), _wide(sbp[...], KV_W))
        qr = _rope(q_ref[...], *tc) * ATTN_SCALE
        kr_cur = _rope(kvc_ref[:, 0:KV_W], *tcur)
        kr_prev = _rope(kvp_ref[:, 0:KV_W], *tprev)
        v_cur, v_prev = kvc_ref[:, KV_W:2 * KV_W], kvp_ref[:, KV_W:2 * KV_W]
        mask = _attn_mask(n)
        outs = []
        k2s = [_kv_pair_operand(kr_prev, kr_cur, kh) for kh in range(N_KV)]
        v2s = [_kv_pair_operand(v_prev, v_cur, kh) for kh in range(N_KV)]
        scores = [_attn_scores(qr, k2s[kh], kh) for kh in range(N_KV)]
        for kh in range(N_KV):
            pn, _ = _attn_softmax(scores[kh][1], _sink_col(sink_ref, kh), mask)
            o_big = _dot(_unrestack(pn.astype(BF16)), v2s[kh])
            outs += [o_big[0:BLOCK], o_big[BLOCK:2 * BLOCK]]
        o = jnp.concatenate(outs, axis=1)
        g = jnp.concatenate([g0_ref[...], g1_ref[...]], axis=1)
        y_ref[...] = (o * (g * _sigmoid(g))).astype(BF16)

    def blk(w, cb):
        return pl.BlockSpec((BLOCK, w), lambda n, cb=cb: (n, cb))

    prev = lambda w, cb: pl.BlockSpec((BLOCK, w), lambda n, cb=cb: (jnp.maximum(n - 1, 0), cb))
    return pl.pallas_call(
        body, out_shape=jax.ShapeDtypeStruct((T, D_MODEL), BF16), grid=(nb,), name="attn_forward",
        in_specs=[blk(D_MODEL, 0), blk(CB, CB_KV), prev(CB, CB_KV), blk(CB, CB_GA), blk(CB, CB_GA + 1),
                  blk(128, 0), blk(128, 0), blk(128, 0), prev(128, 0), prev(128, 0), prev(128, 0),
                  pl.BlockSpec(memory_space=pltpu.SMEM)],
        out_specs=pl.BlockSpec((BLOCK, D_MODEL), lambda n: (n, 0)),
        compiler_params=_cp("parallel"),
    )(proj, proj, proj, proj, proj, *tabs, *tabs, sinks)


def _scan_rows8():
    return lax.broadcasted_iota(jnp.int32, (8, D_MODEL), 0)


def _scan_forward(a_ref, b_ref, h_ref, carry, rows):
    row = _scan_rows8()

    def group(i, carry):
        off = pl.multiple_of(i * 8, 8)
        a, b = a_ref[pl.ds(off, 8), :], b_ref[pl.ds(off, 8), :]
        for d in (1, 2, 4):
            ok = row >= d
            b = jnp.where(ok, a * pltpu.roll(b, d, 0) + b, b)
            a = jnp.where(ok, a * pltpu.roll(a, d, 0), a)
        h = a * carry + b
        h_ref[pl.ds(off, 8), :] = h
        return h[7:8, :]

    return lax.fori_loop(0, rows // 8, group, carry)


def _scan_backward(a_ref, g_ref, lam_ref, carry, rows):
    row = _scan_rows8()

    def group(i, carry):
        off = pl.multiple_of((rows // 8 - 1 - i) * 8, 8)
        a, g = a_ref[pl.ds(off, 8), :], g_ref[pl.ds(off, 8), :]
        b = a * g
        for d in (1, 2, 4):
            ok = row < 8 - d
            b = jnp.where(ok, a * pltpu.roll(b, 8 - d, 0) + b, b)
            a = jnp.where(ok, a * pltpu.roll(a, 8 - d, 0), a)
        mu = a * carry + b
        mu_below = jnp.where(row == 7, carry, pltpu.roll(mu, 7, 0))
        lam_ref[pl.ds(off, 8), :] = g + mu_below
        return mu[0:1, :]

    return lax.fori_loop(0, rows // 8, group, carry)


def _rnn_recompute(xbuf, xr, tail, cw, cb, wa_ref, wx_ref, ba, bx, sp, reset):
    rows = xr.shape[0]
    xbuf[0:8, :] = tail
    xbuf[8:rows + 8, :] = xr
    xs = [xbuf[pl.ds(8 - (CONV_W - 1 - k), rows), :] for k in range(CONV_W - 1)] + [xr]
    xc = xs[0] * cw[0:1, :]
    for k in range(1, CONV_W):
        xc = xc + xs[k] * cw[k:k + 1, :]
    xc = xc + cb
    xcb = xc.astype(BF16)
    za = jnp.concatenate([_dot(xcb[:, RNN_BW * j:RNN_BW * (j + 1)], wa_ref[j]) for j in range(RNN_BLOCKS)], axis=1) + ba
    zx = jnp.concatenate([_dot(xcb[:, RNN_BW * j:RNN_BW * (j + 1)], wx_ref[j]) for j in range(RNN_BLOCKS)], axis=1) + bx
    r, i = _sigmoid(za), _sigmoid(zx)
    log_a = -LRU_C * r * sp
    a_raw = jnp.exp(log_a)
    mult_raw = jnp.sqrt(_neg_expm1(2.0 * log_a))
    a = jnp.where(reset, 0.0, a_raw)
    mult = jnp.where(reset, 1.0, mult_raw)
    return xs, xc, xcb, r, i, a_raw, mult_raw, a, mult


def _rnn_forward(proj, pos_col, conv_w, conv_b, rwa, rwx, ba, bx, lam):
    T = proj.shape[0]
    tr = min(T, 256)

    def body(x0, x1, g0, g1, pos_ref, cw_ref, cb_ref, wa_ref, wx_ref, ba_ref, bx_ref, lam_ref,
             y_ref, h_ref, xbuf, abuf, bbuf, tail, carry):
        t = pl.program_id(0)

        @pl.when(t == 0)
        def _():
            tail[...] = jnp.zeros_like(tail)
            carry[...] = jnp.zeros_like(carry)

        xr = jnp.concatenate([x0[...], x1[...]], axis=1)
        sp = _softplus(-lam_ref[...])
        reset = pos_ref[...] == 0
        _, xc, _, _, i, _, _, a, mult = _rnn_recompute(
            xbuf, xr, tail[...], cw_ref[...], cb_ref[...], wa_ref, wx_ref, ba_ref[...], bx_ref[...], sp, reset)
        abuf[...] = a
        bbuf[...] = mult * (i * xc)
        last = _scan_forward(abuf, bbuf, h_ref, carry[0:1, :], tr)
        carry[...] = jnp.broadcast_to(last, carry.shape)
        tail[...] = xr[tr - 8:tr, :]
        g = jnp.concatenate([g0[...], g1[...]], axis=1)
        y_ref[...] = (h_ref[...] * (g * _sigmoid(g))).astype(BF16)

    blk = lambda cb: pl.BlockSpec((tr, CB), lambda t, cb=cb: (t, cb))
    row = lambda w: pl.BlockSpec((1, w), lambda t: (0, 0))
    full3 = pl.BlockSpec((RNN_BLOCKS, RNN_BW, RNN_BW), lambda t: (0, 0, 0))
    return pl.pallas_call(
        body, out_shape=(jax.ShapeDtypeStruct((T, D_MODEL), BF16), jax.ShapeDtypeStruct((T, D_MODEL), F32)),
        grid=(T // tr,), name="rnn_forward",
        in_specs=[blk(CB_XR), blk(CB_XR + 1), blk(CB_GR), blk(CB_GR + 1), pl.BlockSpec((tr, 1), lambda t: (t, 0)),
                  pl.BlockSpec((CONV_W, D_MODEL), lambda t: (0, 0)), row(D_MODEL), full3, full3,
                  row(D_MODEL), row(D_MODEL), row(D_MODEL)],
        out_specs=(pl.BlockSpec((tr, D_MODEL), lambda t: (t, 0)), pl.BlockSpec((tr, D_MODEL), lambda t: (t, 0))),
        scratch_shapes=[pltpu.VMEM((tr + 8, D_MODEL), F32), pltpu.VMEM((tr, D_MODEL), F32), pltpu.VMEM((tr, D_MODEL), F32),
                        pltpu.VMEM((8, D_MODEL), F32), pltpu.VMEM((8, D_MODEL), F32)],
        compiler_params=_cp("arbitrary"),
    )(proj, proj, proj, proj, pos_col, conv_w, conv_b, rwa, rwx, ba, bx, lam)


def _merge_and_head(x, target, y_attn, y_rnn, proj, wap, wrp, wo, mod_row, final_g):
    T = x.shape[0]
    tm = min(T, 256)

    def body(x_ref, t_ref, ya_ref, yr_ref, ma0, ma1, mr0, mr1, wap_ref, wrp_ref, wo_ref, mod_ref, fg_ref,
             dx2_ref, mg_ref, do_ref, dpa_ref, dpr_ref, dya_ref, dyr_ref, dc_ref, dfg_ref, dgate_ref, loss_ref):
        i = pl.program_id(0)
        gate = mod_ref[:, 2 * D_MODEL:3 * D_MODEL]
        ya, yr = ya_ref[...], yr_ref[...]
        pa, pr = _dot(ya, wap_ref[...]), _dot(yr, wrp_ref[...])
        sa = _sigmoid(jnp.concatenate([ma0[...], ma1[...]], axis=1))
        sr = _sigmoid(jnp.concatenate([mr0[...], mr1[...]], axis=1))
        merged = sa * pa + sr * pr
        mb = merged.astype(BF16)
        o = _dot(mb, wo_ref[...])
        x2 = x_ref[...] + gate * o
        r2 = _rms(x2)
        xn2 = x2 * r2
        fg = fg_ref[...]
        err = xn2 * fg - t_ref[...]
        loss_t = 0.5 * jnp.sum(jnp.sum(err * err, axis=-1, keepdims=True) * (1.0 / D_MODEL), axis=0, keepdims=True)
        dy = err * (1.0 / D_MODEL)
        dfg_t = jnp.sum(dy * xn2, axis=0, keepdims=True)
        dxn = dy * fg
        dx2 = r2 * (dxn - xn2 * jnp.mean(dxn * xn2, axis=-1, keepdims=True))
        dgate_t = jnp.sum(dx2 * o, axis=0, keepdims=True)
        dob = (dx2 * gate).astype(BF16)
        dmerged = _dot_nt(dob, wo_ref[...])
        dpa = (dmerged * sa).astype(BF16)
        dpr = (dmerged * sr).astype(BF16)
        dx2_ref[...] = dx2
        mg_ref[...] = mb
        do_ref[...] = dob
        dpa_ref[...] = dpa
        dpr_ref[...] = dpr
        dya_ref[...] = _dot_nt(dpa, wap_ref[...])
        dyr_ref[...] = _dot_nt(dpr, wrp_ref[...])
        dc_ref[:, 0:D_MODEL] = (dmerged * pa * sa * (1.0 - sa)).astype(BF16)
        dc_ref[:, D_MODEL:2 * D_MODEL] = (dmerged * pr * sr * (1.0 - sr)).astype(BF16)

        @pl.when(i == 0)
        def _():
            dfg_ref[...] = jnp.zeros_like(dfg_ref)
            dgate_ref[...] = jnp.zeros_like(dgate_ref)
            loss_ref[...] = jnp.zeros_like(loss_ref)

        dfg_ref[...] += dfg_t
        dgate_ref[...] += dgate_t
        loss_ref[...] += jnp.broadcast_to(loss_t, loss_ref.shape)

    tok = lambda w: pl.BlockSpec((tm, w), lambda i: (i, 0))
    blk = lambda cb: pl.BlockSpec((tm, CB), lambda i, cb=cb: (i, cb))
    wfull = pl.BlockSpec((D_MODEL, D_MODEL), lambda i: (0, 0))
    row = lambda w: pl.BlockSpec((1, w), lambda i: (0, 0))
    out_shape = (
        jax.ShapeDtypeStruct((T, D_MODEL), F32), jax.ShapeDtypeStruct((T, D_MODEL), BF16),
        jax.ShapeDtypeStruct((T, D_MODEL), BF16), jax.ShapeDtypeStruct((T, D_MODEL), BF16),
        jax.ShapeDtypeStruct((T, D_MODEL), BF16), jax.ShapeDtypeStruct((T, D_MODEL), F32),
        jax.ShapeDtypeStruct((T, D_MODEL), F32), jax.ShapeDtypeStruct((T, 2 * D_MODEL), BF16),
        jax.ShapeDtypeStruct((1, D_MODEL), F32), jax.ShapeDtypeStruct((1, D_MODEL), F32),
        jax.ShapeDtypeStruct((1, 128), F32),
    )
    return pl.pallas_call(
        body, out_shape=out_shape, grid=(T // tm,), name="merge_and_head",
        in_specs=[tok(D_MODEL), tok(D_MODEL), tok(D_MODEL), tok(D_MODEL), blk(CB_MA), blk(CB_MA + 1), blk(CB_MR),
                  blk(CB_MR + 1), wfull, wfull, wfull, row(ADA_W), row(D_MODEL)],
        out_specs=(tok(D_MODEL),) * 7 + (tok(2 * D_MODEL), row(D_MODEL), row(D_MODEL), row(128)),
        compiler_params=_cp("arbitrary"),
    )(x, target, y_attn, y_rnn, proj, proj, proj, proj, wap, wrp, wo, mod_row, final_g)


def _attn_backward(proj, d_y, tabs, sinks):
    T = proj.shape[0]
    nb = T // BLOCK

    def body(q_ref, kvc_ref, kvp_ref, g0_ref, g1_ref, dy_ref, cc, sac, sbc, cp_, sap, sbp, sink_ref,
             dq_ref, dkv_ref, dg_ref, dsink_ref, carry):
        n = pl.program_id(0)

        @pl.when(n == 0)
        def _():
            carry[...] = jnp.zeros_like(carry)
            dsink_ref[...] = jnp.zeros_like(dsink_ref)

        @pl.when(n < nb)
        def _():
            tc = (_wide(cc[...], D_MODEL), _wide(sac[...], D_MODEL), _wide(sbc[...], D_MODEL))
            tcur = tuple(t[:, :KV_W] for t in tc)
            tprev = (_wide(cp_[...], KV_W), _wide(sap[...], KV_W), _wide(sbp[...], KV_W))
            qr = _rope(q_ref[...], *tc) * ATTN_SCALE
            kr_cur = _rope(kvc_ref[:, 0:KV_W], *tcur)
            kr_prev = _rope(kvp_ref[:, 0:KV_W], *tprev)
            v_cur, v_prev = kvc_ref[:, KV_W:2 * KV_W], kvp_ref[:, KV_W:2 * KV_W]
            g = jnp.concatenate([g0_ref[...], g1_ref[...]], axis=1)
            sg = _sigmoid(g)
            dy = dy_ref[...]
            d_o = dy * (g * sg)
            mask = _attn_mask(n)
            lane = lax.broadcasted_iota(jnp.int32, (1, 128), 1)
            rowg = lax.broadcasted_iota(jnp.int32, (GROUP * BLOCK, 1), 0) // BLOCK
            o_parts, dq_parts = [], []
            dk_cols, dv_cols = [None, None], [None, None]
            dsink = jnp.zeros((1, 128), F32)
            k2s = [_kv_pair_operand(kr_prev, kr_cur, kh) for kh in range(N_KV)]
            v2s = [_kv_pair_operand(v_prev, v_cur, kh) for kh in range(N_KV)]
            scores = [_attn_scores(qr, k2s[kh], kh) for kh in range(N_KV)]
            do2s = [_pair_rows(d_o, kh).astype(BF16) for kh in range(N_KV)]
            dpns = [_restack(_dot_nt(do2s[kh], v2s[kh])) for kh in range(N_KV)]
            probs = [_attn_softmax(scores[kh][1], _sink_col(sink_ref, kh), mask) for kh in range(N_KV)]
            p_bigs = [_unrestack(probs[kh][0].astype(BF16)) for kh in range(N_KV)]
            o_bigs = [_dot(p_bigs[kh], v2s[kh]) for kh in range(N_KV)]
            dv2s = [_dot_tn(p_bigs[kh], do2s[kh]) for kh in range(N_KV)]
            deltas = [jnp.sum(probs[kh][0] * dpns[kh], axis=-1, keepdims=True) for kh in range(N_KV)]
            ds_bigs = [_unrestack((probs[kh][0] * (dpns[kh] - deltas[kh])).astype(BF16)) for kh in range(N_KV)]
            dq2s = [_dot(ds_bigs[kh], k2s[kh]) for kh in range(N_KV)]
            dk2s = [_dot_tn(ds_bigs[kh], scores[kh][0]) for kh in range(N_KV)]
            for kh in range(N_KV):
                o_parts += [o_bigs[kh][0:BLOCK], o_bigs[kh][BLOCK:2 * BLOCK]]
                dq_parts += [dq2s[kh][0:BLOCK], dq2s[kh][BLOCK:2 * BLOCK]]
                dk_c, dv_c = _fold_pair(dk2s[kh], kh), _fold_pair(dv2s[kh], kh)
                c = kh // 2
                dk_cols[c] = dk_c if dk_cols[c] is None else dk_cols[c] + dk_c
                dv_cols[c] = dv_c if dv_cols[c] is None else dv_cols[c] + dv_c
                ds_rows = probs[kh][1] * deltas[kh]
                for gq in range(GROUP):
                    val = -jnp.sum(jnp.where(rowg == gq, ds_rows, 0.0), axis=0, keepdims=True)
                    dsink = dsink + jnp.where(lane == GROUP * kh + ROW_GROUP_HEAD[gq], val, 0.0)
            o = jnp.concatenate(o_parts, axis=1)
            dg_ref[...] = (dy * o * (sg * (1.0 + g * (1.0 - sg)))).astype(BF16)
            dq_ref[...] = (_unrope(jnp.concatenate(dq_parts, axis=1), *tc) * ATTN_SCALE).astype(BF16)
            dk_all, dv_all = jnp.concatenate(dk_cols, axis=1), jnp.concatenate(dv_cols, axis=1)
            dk_prev = _unrope(dk_all[0:BLOCK], *tprev)
            dk_cur = _unrope(dk_all[BLOCK:2 * BLOCK], *tcur)
            dv_prev, dv_cur = dv_all[0:BLOCK], dv_all[BLOCK:2 * BLOCK]
            dkv_ref[...] = (carry[...] + jnp.concatenate([dk_prev, dv_prev], axis=1)).astype(BF16)
            carry[...] = jnp.concatenate([dk_cur, dv_cur], axis=1)
            dsink_ref[...] += dsink

        @pl.when(n == nb)
        def _():
            dkv_ref[...] = carry[...].astype(BF16)

    cur = lambda w, cb: pl.BlockSpec((BLOCK, w), lambda n, cb=cb: (jnp.minimum(n, nb - 1), cb))
    prev = lambda w, cb: pl.BlockSpec((BLOCK, w), lambda n, cb=cb: (jnp.maximum(jnp.minimum(n, nb - 1) - 1, 0), cb))
    out_shape = (jax.ShapeDtypeStruct((T, D_MODEL), BF16), jax.ShapeDtypeStruct((T, 2 * KV_W), BF16),
                 jax.ShapeDtypeStruct((T, D_MODEL), BF16), jax.ShapeDtypeStruct((1, 128), F32))
    return pl.pallas_call(
        body, out_shape=out_shape, grid=(nb + 1,), name="attn_backward",
        in_specs=[cur(D_MODEL, 0), cur(CB, CB_KV), prev(CB, CB_KV), cur(CB, CB_GA), cur(CB, CB_GA + 1), cur(D_MODEL, 0),
                  cur(128, 0), cur(128, 0), cur(128, 0), prev(128, 0), prev(128, 0), prev(128, 0),
                  pl.BlockSpec(memory_space=pltpu.SMEM)],
        out_specs=(cur(D_MODEL, 0), pl.BlockSpec((BLOCK, 2 * KV_W), lambda n: (jnp.maximum(n - 1, 0), 0)),
                   cur(D_MODEL, 0), pl.BlockSpec((1, 128), lambda n: (0, 0))),
        scratch_shapes=[pltpu.VMEM((BLOCK, 2 * KV_W), F32)],
        compiler_params=_cp("arbitrary"),
    )(proj, proj, proj, proj, proj, d_y, *tabs, *tabs, sinks)


def _rnn_backward(proj, pos_col, h_rnn, d_y, conv_w, conv_b, rwa, rwx, ba, bx, lam):
    T = proj.shape[0]
    tr = min(T, 256)
    nt = T // tr
    hb = tr // 8

    def body(x0, x1, xh0, xh1, g0, g1, pos_ref, h_ref, hh_ref, dy_ref, cw_ref, cb_ref, wa_ref, wx_ref, ba_ref, bx_ref,
             lam_ref, db_ref, dcw_ref, dcb_ref, dwa_ref, dwx_ref, dba_ref, dbx_ref, dlam_ref,
             xbuf, hbuf, dbuf, abuf, gbuf, lbuf, mu_carry, dxc_head):
        step = pl.program_id(0)
        first_tile = step == nt - 1

        @pl.when(step == 0)
        def _():
            mu_carry[...] = jnp.zeros_like(mu_carry)
            dxc_head[...] = jnp.zeros_like(dxc_head)
            for ref in (dcw_ref, dcb_ref, dwa_ref, dwx_ref, dba_ref, dbx_ref, dlam_ref):
                ref[...] = jnp.zeros_like(ref)

        xr = jnp.concatenate([x0[...], x1[...]], axis=1)
        tail = jnp.where(first_tile, 0.0, jnp.concatenate([xh0[...], xh1[...]], axis=1))
        lam_v = lam_ref[...]
        sp = _softplus(-lam_v)
        reset = pos_ref[...] == 0
        cw = cw_ref[...]
        xs, xc, xcb, r, i, a_raw, mult_raw, a, mult = _rnn_recompute(
            xbuf, xr, tail, cw, cb_ref[...], wa_ref, wx_ref, ba_ref[...], bx_ref[...], sp, reset)
        g = jnp.concatenate([g0[...], g1[...]], axis=1)
        sg = _sigmoid(g)
        dy = dy_ref[...]
        h = h_ref[...]
        d_g = dy * h * (sg * (1.0 + g * (1.0 - sg)))
        abuf[...] = a
        gbuf[...] = dy * (g * sg)
        top = _scan_backward(abuf, gbuf, lbuf, mu_carry[0:1, :], tr)
        mu_carry[...] = jnp.broadcast_to(top, mu_carry.shape)
        lam_t = lbuf[...]
        hbuf[0:8, :] = jnp.where(first_tile, 0.0, hh_ref[...])
        hbuf[8:tr + 8, :] = h
        h_prev = hbuf[pl.ds(7, tr), :]
        live = jnp.logical_not(reset)
        d_a = jnp.where(live, lam_t * h_prev, 0.0)
        d_mult = jnp.where(live, lam_t * (i * xc), 0.0)
        d_ixc = lam_t * mult
        d_i = d_ixc * xc
        d_xc = d_ixc * i
        d_log_a = d_a * a_raw - d_mult * (a_raw * a_raw / mult_raw)
        d_log_a = jnp.where(live, d_log_a, 0.0)
        d_za = d_log_a * (-LRU_C * sp) * (r * (1.0 - r))
        d_zx = d_i * (i * (1.0 - i))
        dlam_ref[...] += jnp.sum(d_log_a * r, axis=0, keepdims=True) * (LRU_C * _sigmoid(-lam_v))
        dba_ref[...] += jnp.sum(d_za, axis=0, keepdims=True)
        dbx_ref[...] += jnp.sum(d_zx, axis=0, keepdims=True)
        dzab, dzxb = d_za.astype(BF16), d_zx.astype(BF16)
        back = []
        for j in range(RNN_BLOCKS):
            sl = slice(RNN_BW * j, RNN_BW * (j + 1))
            dwa_ref[j] += _dot_tn(xcb[:, sl], dzab[:, sl])
            dwx_ref[j] += _dot_tn(xcb[:, sl], dzxb[:, sl])
            back.append(_dot_nt(dzab[:, sl], wa_ref[j]) + _dot_nt(dzxb[:, sl], wx_ref[j]))
        d_xc = d_xc + jnp.concatenate(back, axis=1)
        dcb_ref[...] += jnp.sum(d_xc, axis=0, keepdims=True)
        for k in range(CONV_W):
            dcw_ref[k:k + 1, :] += jnp.sum(d_xc * xs[k], axis=0, keepdims=True)
        dbuf[0:tr, :] = d_xc
        dbuf[tr:tr + 8, :] = dxc_head[...]
        d_xr = d_xc * cw[CONV_W - 1:CONV_W, :]
        for k in range(CONV_W - 1):
            d_xr = d_xr + dbuf[pl.ds(CONV_W - 1 - k, tr), :] * cw[k:k + 1, :]
        dxc_head[...] = d_xc[0:8, :]
        db_ref[:, 0:D_MODEL] = d_xr.astype(BF16)
        db_ref[:, D_MODEL:2 * D_MODEL] = d_g.astype(BF16)

    rev = lambda s: nt - 1 - s
    blk = lambda cb: pl.BlockSpec((tr, CB), lambda s, cb=cb: (rev(s), cb))
    halo = lambda w, cb: pl.BlockSpec((8, w), lambda s, cb=cb: (jnp.maximum(rev(s) * hb - 1, 0), cb))
    tok = lambda w: pl.BlockSpec((tr, w), lambda s: (rev(s), 0))
    row = lambda w: pl.BlockSpec((1, w), lambda s: (0, 0))
    full3 = pl.BlockSpec((RNN_BLOCKS, RNN_BW, RNN_BW), lambda s: (0, 0, 0))
    cwspec = pl.BlockSpec((CONV_W, D_MODEL), lambda s: (0, 0))
    vec = jax.ShapeDtypeStruct((1, D_MODEL), F32)
    gate_w = jax.ShapeDtypeStruct((RNN_BLOCKS, RNN_BW, RNN_BW), F32)
    out_shape = (jax.ShapeDtypeStruct((T, 2 * D_MODEL), BF16), jax.ShapeDtypeStruct((CONV_W, D_MODEL), F32), vec,
                 gate_w, gate_w, vec, vec, vec)
    big = lambda: pltpu.VMEM((tr, D_MODEL), F32)
    ext = lambda: pltpu.VMEM((tr + 8, D_MODEL), F32)
    return pl.pallas_call(
        body, out_shape=out_shape, grid=(nt,), name="rnn_backward",
        in_specs=[blk(CB_XR), blk(CB_XR + 1), halo(CB, CB_XR), halo(CB, CB_XR + 1), blk(CB_GR), blk(CB_GR + 1),
                  pl.BlockSpec((tr, 1), lambda s: (rev(s), 0)), tok(D_MODEL), halo(D_MODEL, 0), tok(D_MODEL),
                  cwspec, row(D_MODEL), full3, full3, row(D_MODEL), row(D_MODEL), row(D_MODEL)],
        out_specs=(tok(2 * D_MODEL), cwspec, row(D_MODEL), full3, full3, row(D_MODEL), row(D_MODEL), row(D_MODEL)),
        scratch_shapes=[ext(), ext(), ext(), big(), big(), big(), pltpu.VMEM((8, D_MODEL), F32), pltpu.VMEM((8, D_MODEL), F32)],
        compiler_params=_cp("arbitrary"),
    )(proj, proj, proj, proj, proj, proj, pos_col, h_rnn, h_rnn, d_y, conv_w, conv_b, rwa, rwx, ba, bx, lam)


def _input_backward(pieces, w_in, x, dx2, mod_row, norm_g):
    T = x.shape[0]
    tm = min(T, 256)
    n = len(pieces)

    def body(*refs):
        d_refs = refs[:n]
        w_ref, x_ref, dx2_ref, mod_ref, g_ref, gx_ref, dshift_ref, dscale_ref, dg_ref = refs[n:]
        i = pl.program_id(0)
        dh = None
        for d_ref, (_, start, count) in zip(d_refs, pieces):
            part = _dot_nt(d_ref[...], w_ref[:, start * CB:(start + count) * CB])
            dh = part if dh is None else dh + part

        @pl.when(i == 0)
        def _():
            dshift_ref[...] = jnp.zeros_like(dshift_ref)
            dscale_ref[...] = jnp.zeros_like(dscale_ref)
            dg_ref[...] = jnp.zeros_like(dg_ref)

        xf = x_ref[...]
        r1 = _rms(xf)
        xn = xf * r1
        gn = g_ref[...]
        s1 = 1.0 + mod_ref[:, D_MODEL:2 * D_MODEL]
        dshift_ref[...] += jnp.sum(dh, axis=0, keepdims=True)
        dscale_ref[...] += jnp.sum(dh * (xn * gn), axis=0, keepdims=True)
        dg_ref[...] += jnp.sum(dh * s1 * xn, axis=0, keepdims=True)
        dxn = dh * s1 * gn
        gx_ref[...] = dx2_ref[...] + r1 * (dxn - xn * jnp.mean(dxn * xn, axis=-1, keepdims=True))

    tok = lambda w: pl.BlockSpec((tm, w), lambda i: (i, 0))
    row = lambda w: pl.BlockSpec((1, w), lambda i: (0, 0))
    vec = jax.ShapeDtypeStruct((1, D_MODEL), F32)
    return pl.pallas_call(
        body, out_shape=(jax.ShapeDtypeStruct((T, D_MODEL), F32), vec, vec, vec), grid=(T // tm,), name="input_backward",
        in_specs=[tok(c * CB) for _, _, c in pieces]
        + [pl.BlockSpec((D_MODEL, IN_W), lambda i: (0, 0), pipeline_mode=pl.Buffered(1)), tok(D_MODEL), tok(D_MODEL),
           row(ADA_W), row(D_MODEL)],
        out_specs=(tok(D_MODEL), row(D_MODEL), row(D_MODEL), row(D_MODEL)),
        compiler_params=_cp("arbitrary"),
    )(*[p[0] for p in pieces], w_in, x, dx2, mod_row, norm_g)


def _weight_grad(a, pieces, tag):
    T, M = a.shape
    n_blocks = sum(count for _, _, count in pieces)
    n = len(pieces)

    def body(*refs):
        a_ref, b_refs, o_ref = refs[0], refs[1:1 + n], refs[-1]
        j = pl.program_id(0)
        for b_ref, (_, start, count) in zip(b_refs, pieces):
            @pl.when((j >= start) & (j < start + count))
            def _(b_ref=b_ref):
                o_ref[...] = _dot_tn(a_ref[...], b_ref[...])

    def piece_spec(start, count):
        return pl.BlockSpec((T, CB), lambda j: (0, jnp.clip(j - start, 0, count - 1)))

    return pl.pallas_call(
        body, out_shape=jax.ShapeDtypeStruct((M, n_blocks * CB), F32), grid=(n_blocks,), name=f"weight_grad_{tag}",
        in_specs=[pl.BlockSpec((T, M), lambda j: (0, 0), pipeline_mode=pl.Buffered(1))] + [piece_spec(s, c) for _, s, c in pieces],
        out_specs=pl.BlockSpec((M, CB), lambda j: (0, j)), compiler_params=_cp("arbitrary"),
    )(a, *[p[0] for p in pieces])


def _adamw(w, g, m, v):
    m = ADAM_B1 * m + (1.0 - ADAM_B1) * g
    v = ADAM_B2 * v + (1.0 - ADAM_B2) * (g * g)
    m_hat = m / (1.0 - ADAM_B1 ** ADAM_STEP)
    v_hat = v / (1.0 - ADAM_B2 ** ADAM_STEP)
    delta = -ADAM_LR * (m_hat / (jnp.sqrt(v_hat) + ADAM_EPS) + ADAM_WD * w)
    return delta, m, v


def _sum_landed(kind, own, land, where, tag):
    if kind == "in":
        R, C = land.shape[1:]
        tr = 256
        grid = (R // tr,)
        own_spec = pl.BlockSpec((tr, C), lambda i, w: (i, w[0]))
        land_spec = pl.BlockSpec((3, tr, C), lambda i, w: (0, i, 0))
        out_spec = pl.BlockSpec((1, tr, C), lambda i, w: (w[1], i, 0))
        out_shape = (2, R, C)
        pick = lambda ref: ref[...]
    elif kind == "sq":
        R, C = land.shape[1:]
        grid = (1,)
        own_spec = pl.BlockSpec((1, R, C), lambda i, w: (w[0], 0, 0))
        land_spec = pl.BlockSpec((3, R, C), lambda i, w: (0, 0, 0))
        out_spec = pl.BlockSpec((1, R, C), lambda i, w: (w[1], 0, 0))
        out_shape = (2, R, C)
        pick = lambda ref: ref[0]
    else:
        B, R, C = land.shape[1:]
        grid = (1,)
        own_spec = pl.BlockSpec((B, 1, R, C), lambda i, w: (0, w[0], 0, 0))
        land_spec = pl.BlockSpec((3, B, R, C), lambda i, w: (0, 0, 0, 0))
        out_spec = pl.BlockSpec((B, 1, R, C), lambda i, w: (0, w[1], 0, 0))
        out_shape = (B, 2, R, C)
        pick = lambda ref: ref[:, 0]

    def body(w_ref, own_ref, l_ref, o_ref):
        total = ((pick(own_ref) + l_ref[0].astype(F32)) + l_ref[1].astype(F32)) + l_ref[2].astype(F32)
        if kind == "in":
            o_ref[0] = total
        elif kind == "sq":
            o_ref[0] = total
        else:
            o_ref[:, 0] = total

    grid_spec = pltpu.PrefetchScalarGridSpec(num_scalar_prefetch=1, grid=grid, in_specs=[own_spec, land_spec], out_specs=out_spec)
    return pl.pallas_call(
        body, out_shape=jax.ShapeDtypeStruct(out_shape, F32), grid_spec=grid_spec, name=f"sum_landed_{tag}",
        compiler_params=_cp("parallel"),
    )(where, own, land)


def _adamw_shard(g, w, m, v, tag):
    R, C = w.shape
    tr = min(R, 256)

    def body(g_ref, w_ref, m_ref, v_ref, d_ref, nm_ref, nv_ref):
        d, nm, nv = _adamw(w_ref[...], g_ref[...], m_ref[...], v_ref[...])
        d_ref[...] = d
        nm_ref[...] = nm
        nv_ref[...] = nv

    spec = pl.BlockSpec((tr, C), lambda i: (i, 0))
    sds = jax.ShapeDtypeStruct((R, C), F32)
    return pl.pallas_call(
        body, out_shape=(sds,) * 3, grid=(R // tr,), name=f"adamw_{tag}",
        in_specs=[spec] * 4, out_specs=(spec,) * 3, compiler_params=_cp("parallel"),
    )(g, w, m, v)


def _adamw_w_ada(c_t, dmod_cols, w, m, v):
    R, C = w.shape

    def body(ct_ref, dm_ref, w_ref, m_ref, v_ref, g_ref, d_ref, nm_ref, nv_ref):
        g = _dot(ct_ref[...].astype(BF16), dm_ref[...].astype(BF16))
        d, nm, nv = _adamw(w_ref[...], g, m_ref[...], v_ref[...])
        g_ref[...] = g
        d_ref[...] = d
        nm_ref[...] = nm
        nv_ref[...] = nv

    tr = 256
    spec = pl.BlockSpec((tr, C), lambda i: (i, 0))
    sds = jax.ShapeDtypeStruct((R, C), F32)
    return pl.pallas_call(
        body, out_shape=(sds,) * 4, grid=(R // tr,), name="adamw_w_ada",
        in_specs=[pl.BlockSpec((tr, 128), lambda i: (i, 0)), pl.BlockSpec((128, C), lambda i: (0, 0))] + [spec] * 3,
        out_specs=(spec,) * 4, compiler_params=_cp("parallel"),
    )(c_t, dmod_cols, w, m, v)


def _adamw_small(small_all, ws, ms, vs):
    def body(s_ref, w_ref, m_ref, v_ref, g_ref, d_ref, nm_ref, nv_ref):
        g = s_ref[0]
        for b in range(1, N_DEV):
            g = g + s_ref[b]
        d, nm, nv = _adamw(w_ref[...], g, m_ref[...], v_ref[...])
        g_ref[...] = g
        d_ref[...] = d
        nm_ref[...] = nm
        nv_ref[...] = nv

    sds = jax.ShapeDtypeStruct((SMALL_ROWS, D_MODEL), F32)
    return pl.pallas_call(
        body, out_shape=(sds,) * 4, name="adamw_small", in_specs=[VMEM_SPEC] * 4, out_specs=(VMEM_SPEC,) * 4,
        compiler_params=pltpu.CompilerParams(vmem_limit_bytes=VMEM_LIMIT_V7X),
    )(small_all, ws, ms, vs)


ROW_MOD, ROW_NORM_G, ROW_CONV_B, ROW_BA, ROW_BX, ROW_LAM, ROW_FINAL_G, ROW_SINKS, ROW_CONV_W = 0, 3, 4, 5, 6, 7, 8, 9, 10


def _pack_small(b_ada, norm_g, conv_b, ba, bx, lam, final_g, sinks, conv_w_full):
    rows = [b_ada.reshape(3, D_MODEL), norm_g, conv_b, ba, bx, lam, final_g.reshape(1, D_MODEL),
            jnp.pad(sinks.reshape(1, -1), ((0, 0), (0, D_MODEL - sinks.size))), conv_w_full,
            jnp.zeros((SMALL_ROWS - 14, D_MODEL), F32)]
    return jnp.concatenate([r.astype(F32) for r in rows], axis=0)


def kernel(x, c, positions, w_ada, b_ada, norm_g, w_in, attn_sinks, conv_w, conv_b, rg_wa, rg_ba, rg_wx, rg_bx, rg_lambda, w_attn_proj, w_rnn_proj, w_out, final_g, loss_target, m_w_ada, m_b_ada, m_norm_g, m_w_in, m_attn_sinks, m_conv_w, m_conv_b, m_rg_wa, m_rg_ba, m_rg_wx, m_rg_bx, m_rg_lambda, m_w_attn_proj, m_w_rnn_proj, m_w_out, m_final_g, v_w_ada, v_b_ada, v_norm_g, v_w_in, v_attn_sinks, v_conv_w, v_conv_b, v_rg_wa, v_rg_ba, v_rg_wx, v_rg_bx, v_rg_lambda, v_w_attn_proj, v_w_rnn_proj, v_w_out, v_final_g):
    T = x.shape[1]
    my_chip = lax.axis_index("x") * 2 + lax.axis_index("y")
    my_dev = my_chip * 2 + lax.axis_index("c")
    x2d, tgt = x[0], loss_target[0]
    pos_col = positions.reshape(T, 1)

    chip_idx = my_chip.reshape(1).astype(jnp.int32)
    c_idx = lax.axis_index("c").reshape(1).astype(jnp.int32)
    sq_place = ((D_MODEL, D_MODEL), (SHARD_ROWS, D_MODEL), lambda chip: (chip, 0))
    rg_place = ((RNN_BLOCKS, RNN_BW, RNN_BW), (RNN_BLOCKS, SHARD_RG, RNN_BW), lambda chip: (0, chip, 0))
    placed = [
        _cast_place(w_in[0], chip_idx, (D_MODEL, IN_W), (D_MODEL, SHARD_IN), lambda chip: (0, chip), "w_in"),
        _cast_place(w_attn_proj[0], chip_idx, *sq_place, "w_attn_proj"),
        _cast_place(w_rnn_proj[0], chip_idx, *sq_place, "w_rnn_proj"),
        _cast_place(w_out[0], chip_idx, *sq_place, "w_out"),
        _cast_place(rg_wa[0], chip_idx, *rg_place, "rg_wa"),
        _cast_place(rg_wx[0], chip_idx, *rg_place, "rg_wx"),
    ]
    cw_chips, c_all, mod_chips = _gather_mod(c.reshape(1, 1, D_MODEL), w_ada[0], conv_w[0])
    g_ssems, g_rsems, fulls, g_token = _gather_start([p.reshape(s) for p, s in zip(placed, FULL_SHAPES)], mod_chips)
    conv_w_f = jnp.transpose(cw_chips, (1, 0, 2)).reshape(CONV_W, D_MODEL)
    mod_all = jnp.transpose(mod_chips, (1, 0, 2)).reshape(N_DEV, ADA_W) + b_ada
    mod_row = lax.dynamic_slice_in_dim(mod_all, my_dev, 1, axis=0) + g_token[0:1, 0:1]

    tabs = _rope_tables(pos_col)
    h = _prenorm(x2d, mod_row, norm_g)
    w_in_v = fulls[0]
    proj = _in_projection(h, w_in_v.reshape(D_MODEL, IN_W), chip_idx, None, "own")
    for k, mask in enumerate(CHIP_MASKS):
        w_in_v = _gather_wait(g_ssems[k], g_rsems[k], [w_in_v], [0], proj, f"w_in_{k}")[0]
        w_in_v = _forward_halves([w_in_v], [(0, 0, k)], f"w_in_{k}")[0]
        from_chip = (chip_idx ^ (mask >> 1)).astype(jnp.int32)
        proj = _in_projection(h, w_in_v.reshape(D_MODEL, IN_W), from_chip, proj, f"from_{k}")
    w_in_f = w_in_v.reshape(D_MODEL, IN_W)
    rest = _gather_wait(g_ssems[3], g_rsems[3], list(fulls[1:]), [1, 2, 3, 4, 5], proj, "rest")
    rest = _forward_halves(rest, [(idx - 1, idx, k) for idx in range(1, N_BIG) for k in range(3)], "rest")
    wap_f, wrp_f, wo_f = (g.reshape(D_MODEL, D_MODEL) for g in rest[0:3])
    rwa_f, rwx_f = (g.reshape(RNN_BLOCKS, RNN_BW, RNN_BW) for g in rest[3:5])
    y_attn = _attn_forward(proj, tabs, attn_sinks)
    y_rnn, h_rnn = _rnn_forward(proj, pos_col, conv_w_f, conv_b, rwa_f, rwx_f, rg_ba, rg_bx, rg_lambda)
    (dx2, merged, d_o, d_pa, d_pr, d_ya, d_yr, d_c, d_final_g, d_gate, loss_vec) = _merge_and_head(
        x2d, tgt, y_attn, y_rnn, proj, wap_f, wrp_f, wo_f, mod_row, final_g.reshape(1, D_MODEL))

    sq = (N_CHIPS, 2, SHARD_ROWS // 2, D_MODEL)
    rg = (RNN_BLOCKS, N_CHIPS, 2, SHARD_RG // 2, RNN_BW)
    rg_flat = (RNN_BLOCKS * N_CHIPS, 2, SHARD_RG // 2, RNN_BW)

    def chip_sum_and_start(views, axes, flat, unflat, tags_, kinds_, group):
        from_sib = _swap_halves(views, axes)
        sums = [_presum(v.reshape(f), s.reshape(f[:1] + f[2:]), c_idx, t) for v, s, f, t in zip(views, from_sib, flat, tags_)]
        exact = [s[0].reshape(u) for s, u in zip(sums, unflat)]
        rounded = [s[1].reshape(u) for s, u in zip(sums, unflat)]
        return _exchange_start(rounded, kinds_, group), exact

    g_ap = _weight_grad(y_attn, [(d_pa, 0, 2)], "w_attn_proj")
    g_rp = _weight_grad(y_rnn, [(d_pr, 0, 2)], "w_rnn_proj")
    g_o = _weight_grad(merged, [(d_o, 0, 2)], "w_out")
    sq_half = (N_CHIPS, SHARD_ROWS // 2, D_MODEL)
    started1, own1 = chip_sum_and_start([g_ap.reshape(sq), g_rp.reshape(sq), g_o.reshape(sq)], [1, 1, 1], [sq] * 3, [sq_half] * 3,
                                  ["w_attn_proj", "w_rnn_proj", "w_out"], ["sq"] * 3, "proj")
    d_q, d_kv, d_ga, d_sinks = _attn_backward(proj, d_ya, tabs, attn_sinks + started1[4][0, 0])
    d_b, d_conv_w, d_conv_b, d_rwa, d_rwx, d_ba, d_bx, d_lam = _rnn_backward(
        proj, pos_col, h_rnn, d_yr, conv_w_f, conv_b, rwa_f, rwx_f, rg_ba, rg_bx, rg_lambda)
    pieces = [(d_q, CB_Q, 2), (d_kv, CB_KV, 1), (d_ga, CB_GA, 2), (d_b, CB_XR, 4), (d_c, CB_MA, 4)]
    g_in = _weight_grad(h, pieces, "w_in")
    started2, own2 = chip_sum_and_start(
        [g_in.reshape(2, D_MODEL // 2, IN_W), d_rwa.reshape(rg), d_rwx.reshape(rg)], [0, 2, 2],
        [(1, 2, D_MODEL // 2, IN_W), rg_flat, rg_flat],
        [(D_MODEL // 2, IN_W), (RNN_BLOCKS, N_CHIPS, SHARD_RG // 2, RNN_BW), (RNN_BLOCKS, N_CHIPS, SHARD_RG // 2, RNN_BW)],
        ["w_in", "rg_wa", "rg_wx"], ["in", "rg", "rg"], "in")
    grad_x, d_shift, d_scale, d_norm_g = _input_backward(pieces, w_in_f, x2d, dx2, mod_row + started2[4][0, 0], norm_g)

    d_mod = jnp.concatenate([d_shift, d_scale, d_gate], axis=1)
    small = _pack_small(d_mod, d_norm_g, d_conv_b, d_ba, d_bx, d_lam, d_final_g, d_sinks[:, :N_HEADS], d_conv_w)
    small_all = _gather_small(small)
    _, lands1 = _exchange_wait(*started1[:4], grad_x, "proj")
    _, lands2 = _exchange_wait(*started2[:4], grad_x, "in")
    tags = ["w_in", "w_attn_proj", "w_rnn_proj", "w_out", "rg_wa", "rg_wx"]
    chip_sums = [own2[0]] + list(own1) + list(own2[1:])
    lands = [lands2[0]] + list(lands1) + list(lands2[1:])
    where = jnp.concatenate([chip_idx, c_idx])
    kinds = ["in", "sq", "sq", "sq", "rg", "rg"]
    halves = [_sum_landed(kinds[i], chip_sums[i], lands[i], where, tags[i]) for i in range(6)]
    grads = _assemble_with_sibling(halves, [0, 0, 0, 0, 1, 1])
    shapes2d = [(D_MODEL, SHARD_IN), (SHARD_ROWS, D_MODEL), (SHARD_ROWS, D_MODEL), (SHARD_ROWS, D_MODEL),
                (RNN_BLOCKS * SHARD_RG, RNN_BW), (RNN_BLOCKS * SHARD_RG, RNN_BW)]
    big_w = [w_in, w_attn_proj, w_rnn_proj, w_out, rg_wa, rg_wx]
    big_m = [m_w_in, m_w_attn_proj, m_w_rnn_proj, m_w_out, m_rg_wa, m_rg_wx]
    big_v = [v_w_in, v_w_attn_proj, v_w_rnn_proj, v_w_out, v_rg_wa, v_rg_wx]
    res = {}
    for i, tag in enumerate(tags):
        g = grads[i].reshape(shapes2d[i])
        outs = _adamw_shard(g, big_w[i].reshape(shapes2d[i]), big_m[i].reshape(shapes2d[i]), big_v[i].reshape(shapes2d[i]), tag)
        res[tag] = [o.reshape(big_w[i].shape) for o in (g,) + tuple(outs)]

    dmod_all = small_all[:, ROW_MOD:ROW_MOD + 3, :].reshape(N_DEV, ADA_W)
    dmod_cols = lax.dynamic_slice_in_dim(dmod_all, my_chip * SHARD_ADA, SHARD_ADA, axis=1)
    c_t = jnp.pad(jnp.transpose(c_all.reshape(N_DEV, D_MODEL)), ((0, 0), (0, 128 - N_DEV)))
    dmod_cols = jnp.pad(dmod_cols, ((0, 128 - N_DEV), (0, 0)))
    res["w_ada"] = [o.reshape(w_ada.shape) for o in _adamw_w_ada(c_t, dmod_cols, w_ada[0], m_w_ada[0], v_w_ada[0])]

    def full_conv(a):
        return lax.dynamic_update_slice_in_dim(jnp.zeros((CONV_W, D_MODEL), F32), a[0], my_chip * (D_MODEL // N_CHIPS), axis=1)

    packed = [_pack_small(p[0], p[1], p[2], p[3], p[4], p[5], p[6], p[7], full_conv(p[8])) for p in (
        (b_ada, norm_g, conv_b, rg_ba, rg_bx, rg_lambda, final_g, attn_sinks, conv_w),
        (m_b_ada, m_norm_g, m_conv_b, m_rg_ba, m_rg_bx, m_rg_lambda, m_final_g, m_attn_sinks, m_conv_w),
        (v_b_ada, v_norm_g, v_conv_b, v_rg_ba, v_rg_bx, v_rg_lambda, v_final_g, v_attn_sinks, v_conv_w))]
    small_out = _adamw_small(small_all, *packed)

    def unpack(slab):
        cw = lax.dynamic_slice_in_dim(slab[ROW_CONV_W:ROW_CONV_W + CONV_W], my_chip * (D_MODEL // N_CHIPS),
                                      D_MODEL // N_CHIPS, axis=1)
        return {
            "b_ada": slab[ROW_MOD:ROW_MOD + 3].reshape(1, ADA_W), "norm_g": slab[ROW_NORM_G:ROW_NORM_G + 1],
            "conv_b": slab[ROW_CONV_B:ROW_CONV_B + 1], "rg_ba": slab[ROW_BA:ROW_BA + 1], "rg_bx": slab[ROW_BX:ROW_BX + 1],
            "rg_lambda": slab[ROW_LAM:ROW_LAM + 1], "final_g": slab[ROW_FINAL_G], "attn_sinks": slab[ROW_SINKS:ROW_SINKS + 1, :N_HEADS],
            "conv_w": cw[None],
        }

    small_res = [unpack(s) for s in small_out]
    order = ["w_ada", "b_ada", "norm_g", "w_in", "attn_sinks", "conv_w", "conv_b", "rg_wa", "rg_ba", "rg_wx", "rg_bx",
             "rg_lambda", "w_attn_proj", "w_rnn_proj", "w_out", "final_g"]
    loss = lax.psum(loss_vec[0, 0], ("x", "y", "c"))
    outs = [loss, grad_x[None]]
    for kind in range(4):
        for name in order:
            outs.append(res[name][kind] if name in res else small_res[kind][name])
    return tuple(outs)
```

```python
import numpy as np
import jax
import jax.numpy as jnp
from jax import lax
from jax.experimental import pallas as pl
from jax.experimental.pallas import tpu as pltpu

F32 = jnp.float32
BF16 = jnp.bfloat16

D_MODEL = 1024
N_HEADS = 16
N_KV = 4
HEAD_DIM = 64
GROUP = N_HEADS // N_KV
BLOCK = 128
KV_W = N_KV * HEAD_DIM
ROT_HALF = 8
ROPE_THETA = 500000.0
ATTN_SCALE = 0.125
RNN_BLOCKS = 4
RNN_BW = 256
CONV_W = 4
LRU_C = 8.0
NORM_EPS = 1e-6
IN_W = 6656
CB = 512
N_CB = IN_W // CB
CB_Q, CB_KV, CB_GA, CB_XR, CB_GR, CB_MA, CB_MR = 0, 2, 3, 5, 7, 9, 11
N_CHIPS = 4
N_DEV = 8
SHARD_IN = IN_W // N_CHIPS
SHARD_ROWS = D_MODEL // N_CHIPS
SHARD_RG = RNN_BW // N_CHIPS
ADA_W = 3 * D_MODEL
SHARD_ADA = ADA_W // N_CHIPS
SMALL_ROWS = 16

ADAM_LR = 0.001
ADAM_B1 = 0.9
ADAM_B2 = 0.999
ADAM_EPS = 1e-08
ADAM_WD = 0.01
ADAM_STEP = 10

VMEM_LIMIT_V7X = 52 * 1024 * 1024
MESH = pl.DeviceIdType.MESH
ANY = pl.BlockSpec(memory_space=pl.ANY)
VMEM_SPEC = pl.BlockSpec(memory_space=pltpu.VMEM)


def _cp(*sem):
    return pltpu.CompilerParams(dimension_semantics=sem if sem else None, vmem_limit_bytes=VMEM_LIMIT_V7X)


def _dot(a, b):
    return jnp.dot(a, b, preferred_element_type=F32)


def _dot_nt(a, b):
    return lax.dot_general(a, b, (((1,), (1,)), ((), ())), preferred_element_type=F32)


def _dot_tn(a, b):
    return lax.dot_general(a, b, (((0,), (0,)), ((), ())), preferred_element_type=F32)


def _sigmoid(z):
    return 1.0 / (1.0 + jnp.exp(-z))


def _softplus(z):
    u = jnp.exp(-jnp.abs(z))
    log1p_u = jnp.where(u < 1e-3, u * (1.0 - u * (0.5 - u * (1.0 / 3.0))), jnp.log(1.0 + u))
    return jnp.maximum(z, 0.0) + log1p_u


def _rms(xf):
    return lax.rsqrt(jnp.mean(xf * xf, axis=-1, keepdims=True) + NORM_EPS)


def _me():
    return lax.axis_index("x"), lax.axis_index("y"), lax.axis_index("c")


def _peer(mask):
    x, y, c = _me()
    fx, fy, fc = (mask >> 2) & 1, (mask >> 1) & 1, mask & 1
    return (x ^ fx if fx else x, y ^ fy if fy else y, c ^ fc if fc else c)


def _chip_of(pos):
    return pos[0] * 2 + pos[1]


CHIP_MASKS = (4, 2, 6)
ALL_MASKS = (1, 2, 3, 4, 5, 6, 7)


HBM_SPEC = pl.BlockSpec(memory_space=pltpu.HBM)
SEM_SPEC = pl.BlockSpec(memory_space=pltpu.SEMAPHORE)
SPLIT_COPY = pltpu.CompilerParams(has_side_effects=pltpu.SideEffectType.DATAFLOW_SIDE_EFFECTING)
N_BIG = 6
FULL_SHAPES = (
    (2, D_MODEL // 2, IN_W),
    (N_CHIPS, 2, SHARD_ROWS // 2, D_MODEL), (N_CHIPS, 2, SHARD_ROWS // 2, D_MODEL), (N_CHIPS, 2, SHARD_ROWS // 2, D_MODEL),
    (RNN_BLOCKS, N_CHIPS, 2, SHARD_RG // 2, RNN_BW), (RNN_BLOCKS, N_CHIPS, 2, SHARD_RG // 2, RNN_BW),
)


def _slot(full, idx, chip, half):
    if idx == 0:
        return full.at[half, :, pl.ds(pl.multiple_of(chip * SHARD_IN, 128), SHARD_IN)]
    return full.at[chip, half] if idx in (1, 2, 3) else full.at[:, chip, half]


def _three_halves(full, idx):
    return full.at[pl.ds(0, 3), 0] if idx in (1, 2, 3) else full.at[:, pl.ds(0, 3), 0]


def _gather_start(fulls, after):
    def body(*refs):
        full_refs = refs[:N_BIG]
        ssems, rsems = refs[N_BIG + 1:N_BIG + 5], refs[N_BIG + 5:N_BIG + 9]
        token = refs[2 * N_BIG + 9]
        me = _me()
        my_chip = _chip_of(me)
        for idx in range(N_BIG):
            for k, mask in enumerate(CHIP_MASKS):
                pair = k if idx == 0 else 3
                mine = _slot(full_refs[idx], idx, my_chip, me[2])
                pltpu.make_async_remote_copy(src_ref=mine, dst_ref=mine, send_sem=ssems[pair], recv_sem=rsems[pair],
                                             device_id=_peer(mask), device_id_type=MESH).start()
        token[...] = jnp.zeros_like(token)

    sem = pltpu.SemaphoreType.DMA(())
    out_shape = (sem,) * 8 + tuple(pltpu.HBM(f.shape, f.dtype) for f in fulls) + (jax.ShapeDtypeStruct((8, 128), F32),)
    outs = pl.pallas_call(
        body, out_shape=out_shape, name="gather_start",
        in_specs=[HBM_SPEC] * N_BIG + [ANY], out_specs=tuple([SEM_SPEC] * 8 + [HBM_SPEC] * N_BIG + [VMEM_SPEC]),
        input_output_aliases={i: 8 + i for i in range(N_BIG)}, compiler_params=SPLIT_COPY,
    )(*[pltpu.with_memory_space_constraint(f, pltpu.HBM) for f in fulls], after)
    return outs[0:4], outs[4:8], outs[8:8 + N_BIG], outs[8 + N_BIG]


def _gather_wait(ssem, rsem, arrays, idxs, after, tag):
    n = len(arrays)

    def body(*refs):
        full_refs, ssem_ref, rsem_ref = refs[:n], refs[n], refs[n + 1]
        me = _me()
        for full, idx in zip(full_refs, idxs):
            region = _slot(full, 0, _chip_of(me), me[2]) if idx == 0 else _three_halves(full, idx)
            arrived = pltpu.make_async_remote_copy(
                src_ref=region, dst_ref=region, send_sem=ssem_ref, recv_sem=rsem_ref, device_id=me, device_id_type=MESH)
            arrived.wait_send()
            arrived.wait_recv()

    outs = pl.pallas_call(
        body, out_shape=tuple(pltpu.HBM(a.shape, a.dtype) for a in arrays), name=f"gather_wait_{tag}",
        in_specs=[HBM_SPEC] * n + [SEM_SPEC, SEM_SPEC, ANY], out_specs=tuple([HBM_SPEC] * n),
        input_output_aliases={i: i for i in range(n)}, compiler_params=SPLIT_COPY,
    )(*arrays, ssem, rsem, after)
    return list(outs)


def _forward_halves(arrays, items, tag):
    n, m = len(arrays), len(items)

    def body(*refs):
        outs, ssem, rsem = refs[n:2 * n], refs[2 * n], refs[2 * n + 1]
        me = _me()
        sib = _peer(1)
        cps = []
        for j, (pos, idx, k) in enumerate(items):
            chip = _chip_of(_peer(CHIP_MASKS[k]))
            cp = pltpu.make_async_remote_copy(
                src_ref=_slot(outs[pos], idx, chip, me[2]), dst_ref=_slot(outs[pos], idx, chip, me[2]),
                send_sem=ssem.at[j], recv_sem=rsem.at[j], device_id=sib, device_id_type=MESH)
            cp.start()
            cps.append(cp)
        for j, (pos, idx, k) in enumerate(items):
            chip = _chip_of(_peer(CHIP_MASKS[k]))
            pltpu.make_async_remote_copy(
                src_ref=_slot(outs[pos], idx, chip, me[2]), dst_ref=_slot(outs[pos], idx, chip, 1 - me[2]),
                send_sem=ssem.at[j], recv_sem=rsem.at[j], device_id=sib, device_id_type=MESH).wait_recv()
        for cp in cps:
            cp.wait_send()

    outs = pl.pallas_call(
        body, out_shape=tuple(jax.ShapeDtypeStruct(a.shape, a.dtype) for a in arrays), name=f"forward_halves_{tag}",
        in_specs=[ANY] * n, out_specs=tuple([ANY] * n), input_output_aliases={i: i for i in range(n)},
        scratch_shapes=[pltpu.SemaphoreType.DMA((m,)), pltpu.SemaphoreType.DMA((m,))],
    )(*arrays)
    return list(outs)


def _gather_mod(c_row, w_ada_s, conv_w_s):
    def body(c_ref, wada_ref, cw_s, cw_f, call_ref, mod_ref, wsend, wrecv, lsem, csend, crecv, msend, mrecv):
        me = _me()
        my_chip = _chip_of(me)
        my_dev = my_chip * 2 + me[2]
        sends = []
        for k, mask in enumerate(CHIP_MASKS):
            cp = pltpu.make_async_remote_copy(src_ref=cw_s, dst_ref=cw_f.at[my_chip], send_sem=wsend.at[k], recv_sem=wrecv.at[k],
                                              device_id=_peer(mask), device_id_type=MESH)
            cp.start()
            sends.append(cp)
        local = [pltpu.make_async_copy(cw_s, cw_f.at[my_chip], lsem.at[0])]
        for cp in local:
            cp.start()

        call_ref[my_dev] = c_ref[0]
        csends = []
        for k, mask in enumerate(ALL_MASKS):
            cp = pltpu.make_async_remote_copy(
                src_ref=c_ref.at[0], dst_ref=call_ref.at[my_dev],
                send_sem=csend.at[k], recv_sem=crecv.at[k], device_id=_peer(mask), device_id_type=MESH)
            cp.start()
            csends.append(cp)
        for k, mask in enumerate(ALL_MASKS):
            frm = _peer(mask)
            pltpu.make_async_remote_copy(
                src_ref=c_ref.at[0], dst_ref=call_ref.at[_chip_of(frm) * 2 + frm[2]],
                send_sem=csend.at[k], recv_sem=crecv.at[k], device_id=frm, device_id_type=MESH).wait_recv()
        for cp in csends:
            cp.wait_send()

        c_all = call_ref[...].reshape(N_DEV, D_MODEL).astype(BF16)
        mod_ref[my_chip] = _dot(c_all, wada_ref[...].astype(BF16))
        msends = []
        for k, mask in enumerate(CHIP_MASKS):
            cp = pltpu.make_async_remote_copy(
                src_ref=mod_ref.at[my_chip], dst_ref=mod_ref.at[my_chip],
                send_sem=msend.at[k], recv_sem=mrecv.at[k], device_id=_peer(mask), device_id_type=MESH)
            cp.start()
            msends.append(cp)
        for k, mask in enumerate(CHIP_MASKS):
            frm = _peer(mask)
            pltpu.make_async_remote_copy(
                src_ref=mod_ref.at[my_chip], dst_ref=mod_ref.at[_chip_of(frm)],
                send_sem=msend.at[k], recv_sem=mrecv.at[k], device_id=frm, device_id_type=MESH).wait_recv()
        for cp in msends:
            cp.wait_send()

        for k, mask in enumerate(CHIP_MASKS):
            frm = _peer(mask)
            pltpu.make_async_remote_copy(src_ref=cw_s, dst_ref=cw_f.at[_chip_of(frm)], send_sem=wsend.at[k], recv_sem=wrecv.at[k],
                                         device_id=frm, device_id_type=MESH).wait_recv()
        for cp in sends:
            cp.wait_send()
        for cp in local:
            cp.wait()

    out_shape = (
        jax.ShapeDtypeStruct((N_CHIPS, CONV_W, D_MODEL // N_CHIPS), F32),
        jax.ShapeDtypeStruct((N_DEV, 1, D_MODEL), F32),
        jax.ShapeDtypeStruct((N_CHIPS, N_DEV, SHARD_ADA), F32),
    )
    return pl.pallas_call(
        body, out_shape=out_shape, name="gather_mod",
        in_specs=[VMEM_SPEC, VMEM_SPEC, ANY], out_specs=(ANY, VMEM_SPEC, VMEM_SPEC),
        scratch_shapes=[
            pltpu.SemaphoreType.DMA((3,)), pltpu.SemaphoreType.DMA((3,)), pltpu.SemaphoreType.DMA((1,)),
            pltpu.SemaphoreType.DMA((7,)), pltpu.SemaphoreType.DMA((7,)),
            pltpu.SemaphoreType.DMA((3,)), pltpu.SemaphoreType.DMA((3,)),
        ],
        compiler_params=pltpu.CompilerParams(vmem_limit_bytes=VMEM_LIMIT_V7X),
    )(c_row, w_ada_s, conv_w_s)


def _cast_place(shard, chip_idx, full_shape, block, index_map, tag):
    def body(chip_ref, s_ref, o_ref):
        o_ref[...] = s_ref[...].astype(BF16)

    grid_spec = pltpu.PrefetchScalarGridSpec(
        num_scalar_prefetch=1, grid=(1,),
        in_specs=[pl.BlockSpec(shard.shape, lambda i, chip_ref: (0,) * shard.ndim)],
        out_specs=pl.BlockSpec(block, lambda i, chip_ref: index_map(chip_ref[0])))
    return pl.pallas_call(
        body, out_shape=jax.ShapeDtypeStruct(full_shape, BF16), grid_spec=grid_spec, name=f"cast_place_{tag}",
        compiler_params=_cp("arbitrary"),
    )(chip_idx, shard)


def _shard_of(ref, kind, chip):
    if kind == "in":
        return ref.at[:, pl.ds(pl.multiple_of(chip * SHARD_IN, 128), SHARD_IN)]
    return ref.at[chip] if kind == "sq" else ref.at[:, chip]


def _land_shape(src, kind):
    if kind == "in":
        return (3, src.shape[0], SHARD_IN)
    return (3,) + src.shape[1:] if kind == "sq" else (3, src.shape[0]) + src.shape[2:]


def _exchange_start(srcs, kinds, tag):
    n = len(srcs)
    lands = [pltpu.with_memory_space_constraint(lax.empty(_land_shape(s, k), s.dtype), pltpu.HBM) for s, k in zip(srcs, kinds)]

    def body(*refs):
        src_refs, land_refs = refs[:n], refs[n:2 * n]
        ssems, rsems = refs[2 * n:3 * n], refs[3 * n:4 * n]
        token = refs[6 * n]
        for i in range(n):
            for k, mask in enumerate(CHIP_MASKS):
                to = _peer(mask)
                pltpu.make_async_remote_copy(
                    src_ref=_shard_of(src_refs[i], kinds[i], _chip_of(to)), dst_ref=land_refs[i].at[k],
                    send_sem=ssems[i], recv_sem=rsems[i], device_id=to, device_id_type=MESH).start()
        token[...] = jnp.zeros_like(token)

    sem = pltpu.SemaphoreType.DMA(())
    out_shape = ((sem,) * (2 * n) + tuple(pltpu.HBM(s.shape, s.dtype) for s in srcs)
                 + tuple(pltpu.HBM(l.shape, l.dtype) for l in lands) + (jax.ShapeDtypeStruct((8, 128), F32),))
    outs = pl.pallas_call(
        body, out_shape=out_shape, name=f"exchange_start_{tag}",
        in_specs=[HBM_SPEC] * (2 * n), out_specs=tuple([SEM_SPEC] * (2 * n) + [HBM_SPEC] * (2 * n) + [VMEM_SPEC]),
        input_output_aliases={i: 2 * n + i for i in range(2 * n)},
        compiler_params=pltpu.CompilerParams(has_side_effects=pltpu.SideEffectType.DATAFLOW_SIDE_EFFECTING),
    )(*[pltpu.with_memory_space_constraint(s, pltpu.HBM) for s in srcs], *lands)
    return outs[:n], outs[n:2 * n], outs[2 * n:3 * n], outs[3 * n:4 * n], outs[4 * n]


def _exchange_wait(ssems, rsems, srcs, lands, after, tag):
    n = len(srcs)

    def body(*refs):
        land_refs = refs[n:2 * n]
        ssem_refs, rsem_refs = refs[2 * n:3 * n], refs[3 * n:4 * n]
        for i in range(n):
            all_three = pltpu.make_async_remote_copy(
                src_ref=land_refs[i], dst_ref=land_refs[i], send_sem=ssem_refs[i], recv_sem=rsem_refs[i],
                device_id=_me(), device_id_type=MESH)
            all_three.wait_send()
            all_three.wait_recv()

    outs = pl.pallas_call(
        body, out_shape=tuple(pltpu.HBM(a.shape, a.dtype) for a in list(srcs) + list(lands)), name=f"exchange_wait_{tag}",
        in_specs=[HBM_SPEC] * (2 * n) + [SEM_SPEC] * (2 * n) + [ANY], out_specs=tuple([HBM_SPEC] * (2 * n)),
        input_output_aliases={i: i for i in range(2 * n)},
        compiler_params=pltpu.CompilerParams(has_side_effects=pltpu.SideEffectType.DATAFLOW_SIDE_EFFECTING),
    )(*srcs, *lands, *ssems, *rsems, after)
    return outs[:n], outs[n:]


def _gather_small(small):
    def body(small_ref, small_all, ssend, srecv):
        me = _me()
        my_dev = _chip_of(me) * 2 + me[2]
        small_all[my_dev] = small_ref[...]
        ssends = []
        for k, mask in enumerate(ALL_MASKS):
            cp = pltpu.make_async_remote_copy(
                src_ref=small_ref, dst_ref=small_all.at[my_dev],
                send_sem=ssend.at[k], recv_sem=srecv.at[k], device_id=_peer(mask), device_id_type=MESH)
            cp.start()
            ssends.append(cp)
        for k, mask in enumerate(ALL_MASKS):
            frm = _peer(mask)
            pltpu.make_async_remote_copy(
                src_ref=small_ref, dst_ref=small_all.at[_chip_of(frm) * 2 + frm[2]],
                send_sem=ssend.at[k], recv_sem=srecv.at[k], device_id=frm, device_id_type=MESH).wait_recv()
        for cp in ssends:
            cp.wait_send()

    return pl.pallas_call(
        body, out_shape=jax.ShapeDtypeStruct((N_DEV, SMALL_ROWS, D_MODEL), F32), name="gather_small",
        in_specs=[VMEM_SPEC], out_specs=VMEM_SPEC,
        scratch_shapes=[pltpu.SemaphoreType.DMA((7,)), pltpu.SemaphoreType.DMA((7,))],
    )(small)


def _half_of(ref, axis, half):
    return ref.at[(slice(None),) * axis + (half,)]


def _swap_halves(parts, axes):
    n = len(parts)

    def body(*refs):
        ins, outs, ssem, rsem = refs[:n], refs[n:2 * n], refs[2 * n], refs[2 * n + 1]
        c = lax.axis_index("c")
        cps = [pltpu.make_async_remote_copy(src_ref=_half_of(ins[i], axes[i], 1 - c), dst_ref=outs[i], send_sem=ssem.at[i],
                                            recv_sem=rsem.at[i], device_id=_peer(1), device_id_type=MESH) for i in range(n)]
        for cp in cps:
            cp.start()
        for cp in cps:
            cp.wait()

    shapes = [p.shape[:a] + p.shape[a + 1:] for p, a in zip(parts, axes)]
    return pl.pallas_call(
        body, out_shape=tuple(jax.ShapeDtypeStruct(s, p.dtype) for s, p in zip(shapes, parts)), name="swap_halves",
        in_specs=[ANY] * n, out_specs=tuple([ANY] * n),
        scratch_shapes=[pltpu.SemaphoreType.DMA((n,)), pltpu.SemaphoreType.DMA((n,))],
    )(*parts)


def _presum(mine, sib, c_idx, tag):
    S, _, R, C = mine.shape
    tr = min(R, 256)
    tc = SHARD_IN if C % SHARD_IN == 0 else C

    def body(c_ref, m_ref, s_ref, o_ref, ob_ref):
        total = m_ref[:, 0] + s_ref[...]
        o_ref[...] = total
        ob_ref[...] = total.astype(BF16)

    out_spec = pl.BlockSpec((S, tr, tc), lambda i, j, c_ref: (0, i, j))
    grid_spec = pltpu.PrefetchScalarGridSpec(
        num_scalar_prefetch=1, grid=(R // tr, C // tc),
        in_specs=[pl.BlockSpec((S, 1, tr, tc), lambda i, j, c_ref: (0, c_ref[0], i, j)),
                  pl.BlockSpec((S, tr, tc), lambda i, j, c_ref: (0, i, j))],
        out_specs=(out_spec, out_spec))
    return pl.pallas_call(
        body, out_shape=(jax.ShapeDtypeStruct((S, R, C), F32), jax.ShapeDtypeStruct((S, R, C), BF16)),
        grid_spec=grid_spec, name=f"presum_{tag}", compiler_params=_cp("parallel", "parallel"),
    )(c_idx, mine, sib)


def _assemble_with_sibling(parts, axes):
    n = len(parts)

    def body(*refs):
        outs, ssem, rsem = refs[n:2 * n], refs[2 * n], refs[2 * n + 1]
        c = lax.axis_index("c")
        cps = [pltpu.make_async_remote_copy(
            src_ref=_half_of(outs[i], axes[i], c), dst_ref=_half_of(outs[i], axes[i], c), send_sem=ssem.at[i],
            recv_sem=rsem.at[i], device_id=_peer(1), device_id_type=MESH) for i in range(n)]
        for cp in cps:
            cp.start()
        for i in range(n):
            pltpu.make_async_remote_copy(
                src_ref=_half_of(outs[i], axes[i], c), dst_ref=_half_of(outs[i], axes[i], 1 - c), send_sem=ssem.at[i],
                recv_sem=rsem.at[i], device_id=_peer(1), device_id_type=MESH).wait_recv()
        for cp in cps:
            cp.wait_send()

    return pl.pallas_call(
        body, out_shape=tuple(jax.ShapeDtypeStruct(p.shape, p.dtype) for p in parts), name="assemble_with_sibling",
        in_specs=[ANY] * n, out_specs=tuple([ANY] * n), input_output_aliases={i: i for i in range(n)},
        scratch_shapes=[pltpu.SemaphoreType.DMA((n,)), pltpu.SemaphoreType.DMA((n,))],
    )(*parts)


def _rope_tables(pos_col):
    T = pos_col.shape[0]
    tm = min(T, 512)
    inv = np.float32(ROPE_THETA) ** (-(np.arange(0, 2 * ROT_HALF, 2, dtype=np.float32)) / np.float32(2 * ROT_HALF))
    lane = np.arange(128) % HEAD_DIM
    freq = np.where(lane < 2 * ROT_HALF, inv[lane % ROT_HALF], 0.0).astype(np.float32)[None, :]

    def body(pos_ref, f_ref, c_ref, sa_ref, sb_ref):
        ang = pos_ref[...].astype(F32) * f_ref[...]
        c, s = jnp.cos(ang), jnp.sin(ang)
        m = lax.broadcasted_iota(jnp.int32, ang.shape, 1) & (HEAD_DIM - 1)
        c_ref[...] = jnp.where(m < 2 * ROT_HALF, c, 1.0)
        sa_ref[...] = jnp.where(m < ROT_HALF, -s, 0.0)
        sb_ref[...] = jnp.where((m >= ROT_HALF) & (m < 2 * ROT_HALF), s, 0.0)

    tab = jax.ShapeDtypeStruct((T, 128), F32)
    return pl.pallas_call(
        body, out_shape=(tab, tab, tab), grid=(T // tm,), name="rope_tables",
        in_specs=[pl.BlockSpec((tm, 1), lambda i: (i, 0)), pl.BlockSpec((1, 128), lambda i: (0, 0))],
        out_specs=tuple(pl.BlockSpec((tm, 128), lambda i: (i, 0)) for _ in range(3)),
        compiler_params=_cp("parallel"),
    )(pos_col, jnp.asarray(freq))


def _wide(tab, width):
    del width
    return tab


def _columns(t):
    return [t[:, i:i + 128] for i in range(0, t.shape[-1], 128)]


def _rope(t, c, sa, sb):
    return jnp.concatenate(
        [x * c + pltpu.roll(x, 128 - ROT_HALF, 1) * sa + pltpu.roll(x, ROT_HALF, 1) * sb for x in _columns(t)], axis=1)


def _unrope(d, c, sa, sb):
    return jnp.concatenate(
        [x * c + pltpu.roll(x * sa, ROT_HALF, 1) + pltpu.roll(x * sb, 128 - ROT_HALF, 1) for x in _columns(d)], axis=1)


def _prenorm(x, mod_row, norm_g):
    T = x.shape[0]
    tm = min(T, 512)

    def body(x_ref, mod_ref, g_ref, h_ref):
        xf = x_ref[...]
        shift, scale = mod_ref[:, 0:D_MODEL], mod_ref[:, D_MODEL:2 * D_MODEL]
        h = (xf * _rms(xf)) * g_ref[...] * (1.0 + scale) + shift
        h_ref[...] = h.astype(BF16)

    return pl.pallas_call(
        body, out_shape=jax.ShapeDtypeStruct((T, D_MODEL), BF16), grid=(T // tm,), name="prenorm",
        in_specs=[pl.BlockSpec((tm, D_MODEL), lambda i: (i, 0)), pl.BlockSpec((1, ADA_W), lambda i: (0, 0)),
                  pl.BlockSpec((1, D_MODEL), lambda i: (0, 0))],
        out_specs=pl.BlockSpec((tm, D_MODEL), lambda i: (i, 0)),
        compiler_params=_cp("parallel"),
    )(x, mod_row, norm_g)


def _in_projection(h, w_in, chip, into, tag):
    T = h.shape[0]
    tm, tn = min(T, 512), SHARD_IN

    def body(chip_ref, h_ref, w_ref, *rest):
        rest[-1][...] = _dot(h_ref[...], w_ref[...])

    in_specs = [pl.BlockSpec((tm, D_MODEL), lambda i, c: (i, 0)),
                pl.BlockSpec((D_MODEL, tn), lambda i, c: (0, c[0]), pipeline_mode=pl.Buffered(1))]
    args = [chip, h, w_in]
    aliases = {}
    if into is not None:
        in_specs.append(ANY)
        args.append(into)
        aliases = {3: 0}
    grid_spec = pltpu.PrefetchScalarGridSpec(num_scalar_prefetch=1, grid=(T // tm,), in_specs=in_specs,
                                             out_specs=pl.BlockSpec((tm, tn), lambda i, c: (i, c[0])))
    return pl.pallas_call(
        body, out_shape=jax.ShapeDtypeStruct((T, IN_W), F32), grid_spec=grid_spec, name=f"in_projection_{tag}",
        input_output_aliases=aliases, compiler_params=_cp("parallel"),
    )(*args)


def _attn_mask(n):
    qi = lax.broadcasted_iota(jnp.int32, (GROUP * BLOCK, 2 * BLOCK), 0) & (BLOCK - 1)
    kj = lax.broadcasted_iota(jnp.int32, (GROUP * BLOCK, 2 * BLOCK), 1)
    diff = qi + BLOCK - kj
    return (diff >= 0) & (diff < BLOCK) & ((kj >= BLOCK) | (n > 0))


ROW_GROUP_HEAD = (0, 2, 1, 3)


def _sink_col(sink_ref, kh):
    rowg = lax.broadcasted_iota(jnp.int32, (GROUP * BLOCK, 1), 0) // BLOCK
    col = jnp.full((GROUP * BLOCK, 1), sink_ref[0, GROUP * kh + ROW_GROUP_HEAD[0]], F32)
    for g in range(1, GROUP):
        col = jnp.where(rowg == g, sink_ref[0, GROUP * kh + ROW_GROUP_HEAD[g]], col)
    return col


def _low_lanes(shape):
    return lax.broadcasted_iota(jnp.int32, shape, 1) < HEAD_DIM


def _kv_pair_operand(prev, cur, kh):
    c = 128 * (kh // 2)
    col = jnp.concatenate([prev[:, c:c + 128], cur[:, c:c + 128]], axis=0)
    if kh % 2 == 0:
        lo = jnp.where(_low_lanes(col.shape), col, 0.0)
        hi = pltpu.roll(lo, HEAD_DIM, 1)
    else:
        hi = jnp.where(_low_lanes(col.shape), 0.0, col)
        lo = pltpu.roll(hi, HEAD_DIM, 1)
    return jnp.concatenate([lo, hi], axis=0).astype(BF16)


def _pair_rows(x, kh):
    c = 2 * 128 * kh
    return jnp.concatenate([x[:, c:c + 128], x[:, c + 128:c + 256]], axis=0)


def _restack(big):
    return jnp.concatenate([big[:, 0:2 * BLOCK], big[:, 2 * BLOCK:4 * BLOCK]], axis=0)


def _unrestack(stacked):
    return jnp.concatenate([stacked[0:2 * BLOCK], stacked[2 * BLOCK:4 * BLOCK]], axis=1)


def _fold_pair(x2, kh):
    low = _low_lanes((2 * BLOCK, 128))
    mixed = jnp.where(low, x2[0:2 * BLOCK], x2[2 * BLOCK:4 * BLOCK])
    total = mixed + pltpu.roll(mixed, HEAD_DIM, 1)
    return jnp.where(low, total, 0.0) if kh % 2 == 0 else jnp.where(low, 0.0, total)


def _attn_scores(qr, k2, kh):
    q2 = _pair_rows(qr, kh).astype(BF16)
    return q2, _restack(_dot_nt(q2, k2))


def _attn_softmax(s, sink_col, mask):
    s = jnp.where(mask, s, -1e30)
    m = jnp.maximum(jnp.max(s, axis=-1, keepdims=True), sink_col)
    p = jnp.exp(s - m)
    p_sink = jnp.exp(sink_col - m)
    denom = jnp.sum(p, axis=-1, keepdims=True) + p_sink
    return p / denom, p_sink / denom


def _attn_forward(proj, tabs, sinks):
    T = proj.shape[0]
    nb = T // BLOCK

    def body(q_ref, kvc_ref, kvp_ref, g0_ref, g1_ref, cc, sac, sbc, cp_, sap, sbp, sink_ref, y_ref):
        n = pl.program_id(0)
        tc = (_wide(cc[...], D_MODEL), _wide(sac[...], D_MODEL), _wide(sbc[...], D_MODEL))
        tcur = tuple(t[:, :KV_W] for t in tc)
        tprev = (_wide(cp_[...], KV_W), _wide(sap[...], KV_W), _wide(sbp[...], KV_W))
        qr = _rope(q_ref[...], *tc) * ATTN_SCALE
        kr_cur = _rope(kvc_ref[:, 0:KV_W], *tcur)
        kr_prev = _rope(kvp_ref[:, 0:KV_W], *tprev)
        v_cur, v_prev = kvc_ref[:, KV_W:2 * KV_W], kvp_ref[:, KV_W:2 * KV_W]
        mask = _attn_mask(n)
        outs = []
        k2s = [_kv_pair_operand(kr_prev, kr_cur, kh) for kh in range(N_KV)]
        v2s = [_kv_pair_operand(v_prev, v_cur, kh) for kh in range(N_KV)]
        scores = [_attn_scores(qr, k2s[kh], kh) for kh in range(N_KV)]
        for kh in range(N_KV):
            pn, _ = _attn_softmax(scores[kh][1], _sink_col(sink_ref, kh), mask)
            o_big = _dot(_unrestack(pn.astype(BF16)), v2s[kh])
            outs += [o_big[0:BLOCK], o_big[BLOCK:2 * BLOCK]]
        o = jnp.concatenate(outs, axis=1)
        g = jnp.concatenate([g0_ref[...], g1_ref[...]], axis=1)
        y_ref[...] = (o * (g * _sigmoid(g))).astype(BF16)

    def blk(w, cb):
        return pl.BlockSpec((BLOCK, w), lambda n, cb=cb: (n, cb))

    prev = lambda w, cb: pl.BlockSpec((BLOCK, w), lambda n, cb=cb: (jnp.maximum(n - 1, 0), cb))
    return pl.pallas_call(
        body, out_shape=jax.ShapeDtypeStruct((T, D_MODEL), BF16), grid=(nb,), name="attn_forward",
        in_specs=[blk(D_MODEL, 0), blk(CB, CB_KV), prev(CB, CB_KV), blk(CB, CB_GA), blk(CB, CB_GA + 1),
                  blk(128, 0), blk(128, 0), blk(128, 0), prev(128, 0), prev(128, 0), prev(128, 0),
                  pl.BlockSpec(memory_space=pltpu.SMEM)],
        out_specs=pl.BlockSpec((BLOCK, D_MODEL), lambda n: (n, 0)),
        compiler_params=_cp("parallel"),
    )(proj, proj, proj, proj, proj, *tabs, *tabs, sinks)


def _scan_rows8():
    return lax.broadcasted_iota(jnp.int32, (8, D_MODEL), 0)


def _scan_forward(a_ref, b_ref, h_ref, carry, rows):
    row = _scan_rows8()

    def group(i, carry):
        off = pl.multiple_of(i * 8, 8)
        a, b = a_ref[pl.ds(off, 8), :], b_ref[pl.ds(off, 8), :]
        for d in (1, 2, 4):
            ok = row >= d
            b = jnp.where(ok, a * pltpu.roll(b, d, 0) + b, b)
            a = jnp.where(ok, a * pltpu.roll(a, d, 0), a)
        h = a * carry + b
        h_ref[pl.ds(off, 8), :] = h
        return h[7:8, :]

    return lax.fori_loop(0, rows // 8, group, carry)


def _scan_backward(a_ref, g_ref, lam_ref, carry, rows):
    row = _scan_rows8()

    def group(i, carry):
        off = pl.multiple_of((rows // 8 - 1 - i) * 8, 8)
        a, g = a_ref[pl.ds(off, 8), :], g_ref[pl.ds(off, 8), :]
        b = a * g
        for d in (1, 2, 4):
            ok = row < 8 - d
            b = jnp.where(ok, a * pltpu.roll(b, 8 - d, 0) + b, b)
            a = jnp.where(ok, a * pltpu.roll(a, 8 - d, 0), a)
        mu = a * carry + b
        mu_below = jnp.where(row == 7, carry, pltpu.roll(mu, 7, 0))
        lam_ref[pl.ds(off, 8), :] = g + mu_below
        return mu[0:1, :]

    return lax.fori_loop(0, rows // 8, group, carry)


def _conv_taps(xbuf, xr, tail):
    rows = xr.shape[0]
    xbuf[0:8, :] = tail
    xbuf[8:rows + 8, :] = xr
    return [xbuf[pl.ds(8 - (CONV_W - 1 - k), rows), :] for k in range(CONV_W - 1)] + [xr]


def _rnn_gates(xbuf, xr, tail, cw, cb, wa_ref, wx_ref, ba, bx, sp, reset):
    xs = _conv_taps(xbuf, xr, tail)
    xc = xs[0] * cw[0:1, :]
    for k in range(1, CONV_W):
        xc = xc + xs[k] * cw[k:k + 1, :]
    xc = xc + cb
    xcb = xc.astype(BF16)
    za = jnp.concatenate([_dot(xcb[:, RNN_BW * j:RNN_BW * (j + 1)], wa_ref[j]) for j in range(RNN_BLOCKS)], axis=1) + ba
    zx = jnp.concatenate([_dot(xcb[:, RNN_BW * j:RNN_BW * (j + 1)], wx_ref[j]) for j in range(RNN_BLOCKS)], axis=1) + bx
    r, i = _sigmoid(za), _sigmoid(zx)
    neg_log_a = LRU_C * r * sp
    a_raw = jnp.exp(-neg_log_a)
    mult_raw = jnp.sqrt(jnp.tanh(neg_log_a) * (1.0 + a_raw * a_raw))
    a = jnp.where(reset, 0.0, a_raw)
    mult = jnp.where(reset, 1.0, mult_raw)
    return xc, r, i, a, mult


def _rnn_forward(proj, pos_col, conv_w, conv_b, rwa, rwx, ba, bx, lam):
    T = proj.shape[0]
    tr = min(T, 256)

    def body(x0, x1, g0, g1, pos_ref, cw_ref, cb_ref, wa_ref, wx_ref, ba_ref, bx_ref, lam_ref,
             y_ref, h_ref, xc_ref, r_ref, i_ref, a_ref, mult_ref, xbuf, bbuf, tail, carry):
        t = pl.program_id(0)

        @pl.when(t == 0)
        def _():
            tail[...] = jnp.zeros_like(tail)
            carry[...] = jnp.zeros_like(carry)

        xr = jnp.concatenate([x0[...], x1[...]], axis=1)
        sp = _softplus(-lam_ref[...])
        reset = pos_ref[...] == 0
        xc, r, i, a, mult = _rnn_gates(
            xbuf, xr, tail[...], cw_ref[...], cb_ref[...], wa_ref, wx_ref, ba_ref[...], bx_ref[...], sp, reset)
        xc_ref[...] = xc
        r_ref[...] = r
        i_ref[...] = i
        a_ref[...] = a
        mult_ref[...] = mult
        bbuf[...] = mult * (i * xc)
        last = _scan_forward(a_ref, bbuf, h_ref, carry[0:1, :], tr)
        carry[...] = jnp.broadcast_to(last, carry.shape)
        tail[...] = xr[tr - 8:tr, :]
        g = jnp.concatenate([g0[...], g1[...]], axis=1)
        y_ref[...] = (h_ref[...] * (g * _sigmoid(g))).astype(BF16)

    blk = lambda cb: pl.BlockSpec((tr, CB), lambda t, cb=cb: (t, cb))
    row = lambda w: pl.BlockSpec((1, w), lambda t: (0, 0))
    full3 = pl.BlockSpec((RNN_BLOCKS, RNN_BW, RNN_BW), lambda t: (0, 0, 0))
    tok = pl.BlockSpec((tr, D_MODEL), lambda t: (t, 0))
    act = jax.ShapeDtypeStruct((T, D_MODEL), F32)
    return pl.pallas_call(
        body, out_shape=(jax.ShapeDtypeStruct((T, D_MODEL), BF16),) + (act,) * 6,
        grid=(T // tr,), name="rnn_forward",
        in_specs=[blk(CB_XR), blk(CB_XR + 1), blk(CB_GR), blk(CB_GR + 1), pl.BlockSpec((tr, 1), lambda t: (t, 0)),
                  pl.BlockSpec((CONV_W, D_MODEL), lambda t: (0, 0)), row(D_MODEL), full3, full3,
                  row(D_MODEL), row(D_MODEL), row(D_MODEL)],
        out_specs=(tok,) * 7,
        scratch_shapes=[pltpu.VMEM((tr + 8, D_MODEL), F32), pltpu.VMEM((tr, D_MODEL), F32),
                        pltpu.VMEM((8, D_MODEL), F32), pltpu.VMEM((8, D_MODEL), F32)],
        compiler_params=_cp("arbitrary"),
    )(proj, proj, proj, proj, pos_col, conv_w, conv_b, rwa, rwx, ba, bx, lam)


def _merge_and_head(x, target, y_attn, y_rnn, proj, wap, wrp, wo, mod_row, final_g):
    T = x.shape[0]
    tm = min(T, 256)

    def body(x_ref, t_ref, ya_ref, yr_ref, ma0, ma1, mr0, mr1, wap_ref, wrp_ref, wo_ref, mod_ref, fg_ref,
             dx2_ref, mg_ref, do_ref, dpa_ref, dpr_ref, dya_ref, dyr_ref, dc_ref, dfg_ref, dgate_ref, loss_ref):
        i = pl.program_id(0)
        gate = mod_ref[:, 2 * D_MODEL:3 * D_MODEL]
        ya, yr = ya_ref[...], yr_ref[...]
        pa, pr = _dot(ya, wap_ref[...]), _dot(yr, wrp_ref[...])
        sa = _sigmoid(jnp.concatenate([ma0[...], ma1[...]], axis=1))
        sr = _sigmoid(jnp.concatenate([mr0[...], mr1[...]], axis=1))
        merged = sa * pa + sr * pr
        mb = merged.astype(BF16)
        o = _dot(mb, wo_ref[...])
        x2 = x_ref[...] + gate * o
        r2 = _rms(x2)
        xn2 = x2 * r2
        fg = fg_ref[...]
        err = xn2 * fg - t_ref[...]
        loss_t = 0.5 * jnp.sum(jnp.sum(err * err, axis=-1, keepdims=True) * (1.0 / D_MODEL), axis=0, keepdims=True)
        dy = err * (1.0 / D_MODEL)
        dfg_t = jnp.sum(dy * xn2, axis=0, keepdims=True)
        dxn = dy * fg
        dx2 = r2 * (dxn - xn2 * jnp.mean(dxn * xn2, axis=-1, keepdims=True))
        dgate_t = jnp.sum(dx2 * o, axis=0, keepdims=True)
        dob = (dx2 * gate).astype(BF16)
        dmerged = _dot_nt(dob, wo_ref[...])
        dpa = (dmerged * sa).astype(BF16)
        dpr = (dmerged * sr).astype(BF16)
        dx2_ref[...] = dx2
        mg_ref[...] = mb
        do_ref[...] = dob
        dpa_ref[...] = dpa
        dpr_ref[...] = dpr
        dya_ref[...] = _dot_nt(dpa, wap_ref[...])
        dyr_ref[...] = _dot_nt(dpr, wrp_ref[...])
        dc_ref[:, 0:D_MODEL] = (dmerged * pa * sa * (1.0 - sa)).astype(BF16)
        dc_ref[:, D_MODEL:2 * D_MODEL] = (dmerged * pr * sr * (1.0 - sr)).astype(BF16)

        @pl.when(i == 0)
        def _():
            dfg_ref[...] = jnp.zeros_like(dfg_ref)
            dgate_ref[...] = jnp.zeros_like(dgate_ref)
            loss_ref[...] = jnp.zeros_like(loss_ref)

        dfg_ref[...] += dfg_t
        dgate_ref[...] += dgate_t
        loss_ref[...] += jnp.broadcast_to(loss_t, loss_ref.shape)

    tok = lambda w: pl.BlockSpec((tm, w), lambda i: (i, 0))
    blk = lambda cb: pl.BlockSpec((tm, CB), lambda i, cb=cb: (i, cb))
    wfull = pl.BlockSpec((D_MODEL, D_MODEL), lambda i: (0, 0))
    row = lambda w: pl.BlockSpec((1, w), lambda i: (0, 0))
    out_shape = (
        jax.ShapeDtypeStruct((T, D_MODEL), F32), jax.ShapeDtypeStruct((T, D_MODEL), BF16),
        jax.ShapeDtypeStruct((T, D_MODEL), BF16), jax.ShapeDtypeStruct((T, D_MODEL), BF16),
        jax.ShapeDtypeStruct((T, D_MODEL), BF16), jax.ShapeDtypeStruct((T, D_MODEL), F32),
        jax.ShapeDtypeStruct((T, D_MODEL), F32), jax.ShapeDtypeStruct((T, 2 * D_MODEL), BF16),
        jax.ShapeDtypeStruct((1, D_MODEL), F32), jax.ShapeDtypeStruct((1, D_MODEL), F32),
        jax.ShapeDtypeStruct((1, 128), F32),
    )
    return pl.pallas_call(
        body, out_shape=out_shape, grid=(T // tm,), name="merge_and_head",
        in_specs=[tok(D_MODEL), tok(D_MODEL), tok(D_MODEL), tok(D_MODEL), blk(CB_MA), blk(CB_MA + 1), blk(CB_MR),
                  blk(CB_MR + 1), wfull, wfull, wfull, row(ADA_W), row(D_MODEL)],
        out_specs=(tok(D_MODEL),) * 7 + (tok(2 * D_MODEL), row(D_MODEL), row(D_MODEL), row(128)),
        compiler_params=_cp("arbitrary"),
    )(x, target, y_attn, y_rnn, proj, proj, proj, proj, wap, wrp, wo, mod_row, final_g)


def _attn_backward(proj, d_y, tabs, sinks):
    T = proj.shape[0]
    nb = T // BLOCK

    def body(q_ref, kvc_ref, kvp_ref, g0_ref, g1_ref, dy_ref, cc, sac, sbc, cp_, sap, sbp, sink_ref,
             dq_ref, dkv_ref, dg_ref, dsink_ref, carry):
        n = pl.program_id(0)

        @pl.when(n == 0)
        def _():
            carry[...] = jnp.zeros_like(carry)
            dsink_ref[...] = jnp.zeros_like(dsink_ref)

        @pl.when(n < nb)
        def _():
            tc = (_wide(cc[...], D_MODEL), _wide(sac[...], D_MODEL), _wide(sbc[...], D_MODEL))
            tcur = tuple(t[:, :KV_W] for t in tc)
            tprev = (_wide(cp_[...], KV_W), _wide(sap[...], KV_W), _wide(sbp[...], KV_W))
            qr = _rope(q_ref[...], *tc) * ATTN_SCALE
            kr_cur = _rope(kvc_ref[:, 0:KV_W], *tcur)
            kr_prev = _rope(kvp_ref[:, 0:KV_W], *tprev)
            v_cur, v_prev = kvc_ref[:, KV_W:2 * KV_W], kvp_ref[:, KV_W:2 * KV_W]
            g = jnp.concatenate([g0_ref[...], g1_ref[...]], axis=1)
            sg = _sigmoid(g)
            dy = dy_ref[...]
            d_o = dy * (g * sg)
            mask = _attn_mask(n)
            lane = lax.broadcasted_iota(jnp.int32, (1, 128), 1)
            rowg = lax.broadcasted_iota(jnp.int32, (GROUP * BLOCK, 1), 0) // BLOCK
            o_parts, dq_parts = [], []
            dk_cols, dv_cols = [None, None], [None, None]
            dsink = jnp.zeros((1, 128), F32)
            k2s = [_kv_pair_operand(kr_prev, kr_cur, kh) for kh in range(N_KV)]
            v2s = [_kv_pair_operand(v_prev, v_cur, kh) for kh in range(N_KV)]
            scores = [_attn_scores(qr, k2s[kh], kh) for kh in range(N_KV)]
            do2s = [_pair_rows(d_o, kh).astype(BF16) for kh in range(N_KV)]
            dpns = [_restack(_dot_nt(do2s[kh], v2s[kh])) for kh in range(N_KV)]
            probs = [_attn_softmax(scores[kh][1], _sink_col(sink_ref, kh), mask) for kh in range(N_KV)]
            p_bigs = [_unrestack(probs[kh][0].astype(BF16)) for kh in range(N_KV)]
            o_bigs = [_dot(p_bigs[kh], v2s[kh]) for kh in range(N_KV)]
            dv2s = [_dot_tn(p_bigs[kh], do2s[kh]) for kh in range(N_KV)]
            deltas = [jnp.sum(probs[kh][0] * dpns[kh], axis=-1, keepdims=True) for kh in range(N_KV)]
            ds_bigs = [_unrestack((probs[kh][0] * (dpns[kh] - deltas[kh])).astype(BF16)) for kh in range(N_KV)]
            dq2s = [_dot(ds_bigs[kh], k2s[kh]) for kh in range(N_KV)]
            dk2s = [_dot_tn(ds_bigs[kh], scores[kh][0]) for kh in range(N_KV)]
            for kh in range(N_KV):
                o_parts += [o_bigs[kh][0:BLOCK], o_bigs[kh][BLOCK:2 * BLOCK]]
                dq_parts += [dq2s[kh][0:BLOCK], dq2s[kh][BLOCK:2 * BLOCK]]
                dk_c, dv_c = _fold_pair(dk2s[kh], kh), _fold_pair(dv2s[kh], kh)
                c = kh // 2
                dk_cols[c] = dk_c if dk_cols[c] is None else dk_cols[c] + dk_c
                dv_cols[c] = dv_c if dv_cols[c] is None else dv_cols[c] + dv_c
                ds_rows = probs[kh][1] * deltas[kh]
                for gq in range(GROUP):
                    val = -jnp.sum(jnp.where(rowg == gq, ds_rows, 0.0), axis=0, keepdims=True)
                    dsink = dsink + jnp.where(lane == GROUP * kh + ROW_GROUP_HEAD[gq], val, 0.0)
            o = jnp.concatenate(o_parts, axis=1)
            dg_ref[...] = (dy * o * (sg * (1.0 + g * (1.0 - sg)))).astype(BF16)
            dq_ref[...] = (_unrope(jnp.concatenate(dq_parts, axis=1), *tc) * ATTN_SCALE).astype(BF16)
            dk_all, dv_all = jnp.concatenate(dk_cols, axis=1), jnp.concatenate(dv_cols, axis=1)
            dk_prev = _unrope(dk_all[0:BLOCK], *tprev)
            dk_cur = _unrope(dk_all[BLOCK:2 * BLOCK], *tcur)
            dv_prev, dv_cur = dv_all[0:BLOCK], dv_all[BLOCK:2 * BLOCK]
            dkv_ref[...] = (carry[...] + jnp.concatenate([dk_prev, dv_prev], axis=1)).astype(BF16)
            carry[...] = jnp.concatenate([dk_cur, dv_cur], axis=1)
            dsink_ref[...] += dsink

        @pl.when(n == nb)
        def _():
            dkv_ref[...] = carry[...].astype(BF16)

    cur = lambda w, cb: pl.BlockSpec((BLOCK, w), lambda n, cb=cb: (jnp.minimum(n, nb - 1), cb))
    prev = lambda w, cb: pl.BlockSpec((BLOCK, w), lambda n, cb=cb: (jnp.maximum(jnp.minimum(n, nb - 1) - 1, 0), cb))
    out_shape = (jax.ShapeDtypeStruct((T, D_MODEL), BF16), jax.ShapeDtypeStruct((T, 2 * KV_W), BF16),
                 jax.ShapeDtypeStruct((T, D_MODEL), BF16), jax.ShapeDtypeStruct((1, 128), F32))
    return pl.pallas_call(
        body, out_shape=out_shape, grid=(nb + 1,), name="attn_backward",
        in_specs=[cur(D_MODEL, 0), cur(CB, CB_KV), prev(CB, CB_KV), cur(CB, CB_GA), cur(CB, CB_GA + 1), cur(D_MODEL, 0),
                  cur(128, 0), cur(128, 0), cur(128, 0), prev(128, 0), prev(128, 0), prev(128, 0),
                  pl.BlockSpec(memory_space=pltpu.SMEM)],
        out_specs=(cur(D_MODEL, 0), pl.BlockSpec((BLOCK, 2 * KV_W), lambda n: (jnp.maximum(n - 1, 0), 0)),
                   cur(D_MODEL, 0), pl.BlockSpec((1, 128), lambda n: (0, 0))),
        scratch_shapes=[pltpu.VMEM((BLOCK, 2 * KV_W), F32)],
        compiler_params=_cp("arbitrary"),
    )(proj, proj, proj, proj, proj, d_y, *tabs, *tabs, sinks)


def _rnn_backward(proj, pos_col, h_rnn, saved, d_y, conv_w, rwa, rwx, lam):
    T = proj.shape[0]
    tr = min(T, 256)
    nt = T // tr
    hb = tr // 8

    def body(x0, x1, xh0, xh1, g0, g1, pos_ref, h_ref, hh_ref, xc_ref, r_ref, i_ref, a_ref, mult_ref, dy_ref,
             cw_ref, wa_ref, wx_ref, lam_ref, db_ref, dcw_ref, dcb_ref, dwa_ref, dwx_ref, dba_ref, dbx_ref, dlam_ref,
             xbuf, hbuf, dbuf, gbuf, lbuf, mu_carry, dxc_head):
        step = pl.program_id(0)
        first_tile = step == nt - 1

        @pl.when(step == 0)
        def _():
            mu_carry[...] = jnp.zeros_like(mu_carry)
            dxc_head[...] = jnp.zeros_like(dxc_head)
            for ref in (dcw_ref, dcb_ref, dwa_ref, dwx_ref, dba_ref, dbx_ref, dlam_ref):
                ref[...] = jnp.zeros_like(ref)

        xr = jnp.concatenate([x0[...], x1[...]], axis=1)
        tail = jnp.where(first_tile, 0.0, jnp.concatenate([xh0[...], xh1[...]], axis=1))
        lam_v = lam_ref[...]
        sp = _softplus(-lam_v)
        reset = pos_ref[...] == 0
        cw = cw_ref[...]
        xs = _conv_taps(xbuf, xr, tail)
        xc, r, i, a, mult = xc_ref[...], r_ref[...], i_ref[...], a_ref[...], mult_ref[...]
        xcb = xc.astype(BF16)
        g = jnp.concatenate([g0[...], g1[...]], axis=1)
        sg = _sigmoid(g)
        dy = dy_ref[...]
        h = h_ref[...]
        d_g = dy * h * (sg * (1.0 + g * (1.0 - sg)))
        gbuf[...] = dy * (g * sg)
        top = _scan_backward(a_ref, gbuf, lbuf, mu_carry[0:1, :], tr)
        mu_carry[...] = jnp.broadcast_to(top, mu_carry.shape)
        lam_t = lbuf[...]
        hbuf[0:8, :] = jnp.where(first_tile, 0.0, hh_ref[...])
        hbuf[8:tr + 8, :] = h
        h_prev = hbuf[pl.ds(7, tr), :]
        live = jnp.logical_not(reset)
        d_a = jnp.where(live, lam_t * h_prev, 0.0)
        d_mult = jnp.where(live, lam_t * (i * xc), 0.0)
        d_ixc = lam_t * mult
        d_i = d_ixc * xc
        d_xc = d_ixc * i
        d_log_a = d_a * a - d_mult * (a * a / mult)
        d_za = d_log_a * (-LRU_C * sp) * (r * (1.0 - r))
        d_zx = d_i * (i * (1.0 - i))
        dlam_ref[...] += jnp.sum(d_log_a * r, axis=0, keepdims=True) * (LRU_C * _sigmoid(-lam_v))
        dba_ref[...] += jnp.sum(d_za, axis=0, keepdims=True)
        dbx_ref[...] += jnp.sum(d_zx, axis=0, keepdims=True)
        dzab, dzxb = d_za.astype(BF16), d_zx.astype(BF16)
        back = []
        for j in range(RNN_BLOCKS):
            sl = slice(RNN_BW * j, RNN_BW * (j + 1))
            dwa_ref[j] += _dot_tn(xcb[:, sl], dzab[:, sl])
            dwx_ref[j] += _dot_tn(xcb[:, sl], dzxb[:, sl])
            back.append(_dot_nt(dzab[:, sl], wa_ref[j]) + _dot_nt(dzxb[:, sl], wx_ref[j]))
        d_xc = d_xc + jnp.concatenate(back, axis=1)
        dcb_ref[...] += jnp.sum(d_xc, axis=0, keepdims=True)
        for k in range(CONV_W):
            dcw_ref[k:k + 1, :] += jnp.sum(d_xc * xs[k], axis=0, keepdims=True)
        dbuf[0:tr, :] = d_xc
        dbuf[tr:tr + 8, :] = dxc_head[...]
        d_xr = d_xc * cw[CONV_W - 1:CONV_W, :]
        for k in range(CONV_W - 1):
            d_xr = d_xr + dbuf[pl.ds(CONV_W - 1 - k, tr), :] * cw[k:k + 1, :]
        dxc_head[...] = d_xc[0:8, :]
        db_ref[:, 0:D_MODEL] = d_xr.astype(BF16)
        db_ref[:, D_MODEL:2 * D_MODEL] = d_g.astype(BF16)

    rev = lambda s: nt - 1 - s
    blk = lambda cb: pl.BlockSpec((tr, CB), lambda s, cb=cb: (rev(s), cb))
    halo = lambda w, cb: pl.BlockSpec((8, w), lambda s, cb=cb: (jnp.maximum(rev(s) * hb - 1, 0), cb))
    tok = lambda w: pl.BlockSpec((tr, w), lambda s: (rev(s), 0))
    row = lambda w: pl.BlockSpec((1, w), lambda s: (0, 0))
    full3 = pl.BlockSpec((RNN_BLOCKS, RNN_BW, RNN_BW), lambda s: (0, 0, 0))
    cwspec = pl.BlockSpec((CONV_W, D_MODEL), lambda s: (0, 0))
    vec = jax.ShapeDtypeStruct((1, D_MODEL), F32)
    gate_w = jax.ShapeDtypeStruct((RNN_BLOCKS, RNN_BW, RNN_BW), F32)
    out_shape = (jax.ShapeDtypeStruct((T, 2 * D_MODEL), BF16), jax.ShapeDtypeStruct((CONV_W, D_MODEL), F32), vec,
                 gate_w, gate_w, vec, vec, vec)
    big = lambda: pltpu.VMEM((tr, D_MODEL), F32)
    ext = lambda: pltpu.VMEM((tr + 8, D_MODEL), F32)
    return pl.pallas_call(
        body, out_shape=out_shape, grid=(nt,), name="rnn_backward",
        in_specs=[blk(CB_XR), blk(CB_XR + 1), halo(CB, CB_XR), halo(CB, CB_XR + 1), blk(CB_GR), blk(CB_GR + 1),
                  pl.BlockSpec((tr, 1), lambda s: (rev(s), 0)), tok(D_MODEL), halo(D_MODEL, 0)] + [tok(D_MODEL)] * 6
        + [cwspec, full3, full3, row(D_MODEL)],
        out_specs=(tok(2 * D_MODEL), cwspec, row(D_MODEL), full3, full3, row(D_MODEL), row(D_MODEL), row(D_MODEL)),
        scratch_shapes=[ext(), ext(), ext(), big(), big(), pltpu.VMEM((8, D_MODEL), F32), pltpu.VMEM((8, D_MODEL), F32)],
        compiler_params=_cp("arbitrary"),
    )(proj, proj, proj, proj, proj, proj, pos_col, h_rnn, h_rnn, *saved, d_y, conv_w, rwa, rwx, lam)


def _input_backward(pieces, w_in, x, dx2, mod_row, norm_g):
    T = x.shape[0]
    tm = min(T, 256)
    n = len(pieces)

    def body(*refs):
        d_refs = refs[:n]
        w_ref, x_ref, dx2_ref, mod_ref, g_ref, gx_ref, dshift_ref, dscale_ref, dg_ref = refs[n:]
        i = pl.program_id(0)
        dh = None
        for d_ref, (_, start, count) in zip(d_refs, pieces):
            part = _dot_nt(d_ref[...], w_ref[:, start * CB:(start + count) * CB])
            dh = part if dh is None else dh + part

        @pl.when(i == 0)
        def _():
            dshift_ref[...] = jnp.zeros_like(dshift_ref)
            dscale_ref[...] = jnp.zeros_like(dscale_ref)
            dg_ref[...] = jnp.zeros_like(dg_ref)

        xf = x_ref[...]
        r1 = _rms(xf)
        xn = xf * r1
        gn = g_ref[...]
        s1 = 1.0 + mod_ref[:, D_MODEL:2 * D_MODEL]
        dshift_ref[...] += jnp.sum(dh, axis=0, keepdims=True)
        dscale_ref[...] += jnp.sum(dh * (xn * gn), axis=0, keepdims=True)
        dg_ref[...] += jnp.sum(dh * s1 * xn, axis=0, keepdims=True)
        dxn = dh * s1 * gn
        gx_ref[...] = dx2_ref[...] + r1 * (dxn - xn * jnp.mean(dxn * xn, axis=-1, keepdims=True))

    tok = lambda w: pl.BlockSpec((tm, w), lambda i: (i, 0))
    row = lambda w: pl.BlockSpec((1, w), lambda i: (0, 0))
    vec = jax.ShapeDtypeStruct((1, D_MODEL), F32)
    return pl.pallas_call(
        body, out_shape=(jax.ShapeDtypeStruct((T, D_MODEL), F32), vec, vec, vec), grid=(T // tm,), name="input_backward",
        in_specs=[tok(c * CB) for _, _, c in pieces]
        + [pl.BlockSpec((D_MODEL, IN_W), lambda i: (0, 0), pipeline_mode=pl.Buffered(1)), tok(D_MODEL), tok(D_MODEL),
           row(ADA_W), row(D_MODEL)],
        out_specs=(tok(D_MODEL), row(D_MODEL), row(D_MODEL), row(D_MODEL)),
        compiler_params=_cp("arbitrary"),
    )(*[p[0] for p in pieces], w_in, x, dx2, mod_row, norm_g)


def _weight_grad(a, pieces, tag):
    T, M = a.shape
    n_blocks = sum(count for _, _, count in pieces)
    n = len(pieces)

    def body(*refs):
        a_ref, b_refs, o_ref = refs[0], refs[1:1 + n], refs[-1]
        j = pl.program_id(0)
        for b_ref, (_, start, count) in zip(b_refs, pieces):
            @pl.when((j >= start) & (j < start + count))
            def _(b_ref=b_ref):
                o_ref[...] = _dot_tn(a_ref[...], b_ref[...])

    def piece_spec(start, count):
        return pl.BlockSpec((T, CB), lambda j: (0, jnp.clip(j - start, 0, count - 1)))

    return pl.pallas_call(
        body, out_shape=jax.ShapeDtypeStruct((M, n_blocks * CB), F32), grid=(n_blocks,), name=f"weight_grad_{tag}",
        in_specs=[pl.BlockSpec((T, M), lambda j: (0, 0), pipeline_mode=pl.Buffered(1))] + [piece_spec(s, c) for _, s, c in pieces],
        out_specs=pl.BlockSpec((M, CB), lambda j: (0, j)), compiler_params=_cp("arbitrary"),
    )(a, *[p[0] for p in pieces])


def _adamw(w, g, m, v):
    m = ADAM_B1 * m + (1.0 - ADAM_B1) * g
    v = ADAM_B2 * v + (1.0 - ADAM_B2) * (g * g)
    m_hat = m / (1.0 - ADAM_B1 ** ADAM_STEP)
    v_hat = v / (1.0 - ADAM_B2 ** ADAM_STEP)
    delta = -ADAM_LR * (m_hat / (jnp.sqrt(v_hat) + ADAM_EPS) + ADAM_WD * w)
    return delta, m, v


def _sum_landed(kind, own, land, where, tag):
    if kind == "in":
        R, C = land.shape[1:]
        tr = 256
        grid = (R // tr,)
        own_spec = pl.BlockSpec((tr, C), lambda i, w: (i, w[0]))
        land_spec = pl.BlockSpec((3, tr, C), lambda i, w: (0, i, 0))
        out_spec = pl.BlockSpec((1, tr, C), lambda i, w: (w[1], i, 0))
        out_shape = (2, R, C)
        pick = lambda ref: ref[...]
    elif kind == "sq":
        R, C = land.shape[1:]
        grid = (1,)
        own_spec = pl.BlockSpec((1, R, C), lambda i, w: (w[0], 0, 0))
        land_spec = pl.BlockSpec((3, R, C), lambda i, w: (0, 0, 0))
        out_spec = pl.BlockSpec((1, R, C), lambda i, w: (w[1], 0, 0))
        out_shape = (2, R, C)
        pick = lambda ref: ref[0]
    else:
        B, R, C = land.shape[1:]
        grid = (1,)
        own_spec = pl.BlockSpec((B, 1, R, C), lambda i, w: (0, w[0], 0, 0))
        land_spec = pl.BlockSpec((3, B, R, C), lambda i, w: (0, 0, 0, 0))
        out_spec = pl.BlockSpec((B, 1, R, C), lambda i, w: (0, w[1], 0, 0))
        out_shape = (B, 2, R, C)
        pick = lambda ref: ref[:, 0]

    def body(w_ref, own_ref, l_ref, o_ref):
        total = ((pick(own_ref) + l_ref[0].astype(F32)) + l_ref[1].astype(F32)) + l_ref[2].astype(F32)
        if kind == "in":
            o_ref[0] = total
        elif kind == "sq":
            o_ref[0] = total
        else:
            o_ref[:, 0] = total

    grid_spec = pltpu.PrefetchScalarGridSpec(num_scalar_prefetch=1, grid=grid, in_specs=[own_spec, land_spec], out_specs=out_spec)
    return pl.pallas_call(
        body, out_shape=jax.ShapeDtypeStruct(out_shape, F32), grid_spec=grid_spec, name=f"sum_landed_{tag}",
        compiler_params=_cp("parallel"),
    )(where, own, land)


def _adamw_shard(g, w, m, v, tag):
    R, C = w.shape
    tr = min(R, 256)

    def body(g_ref, w_ref, m_ref, v_ref, d_ref, nm_ref, nv_ref):
        d, nm, nv = _adamw(w_ref[...], g_ref[...], m_ref[...], v_ref[...])
        d_ref[...] = d
        nm_ref[...] = nm
        nv_ref[...] = nv

    spec = pl.BlockSpec((tr, C), lambda i: (i, 0))
    sds = jax.ShapeDtypeStruct((R, C), F32)
    return pl.pallas_call(
        body, out_shape=(sds,) * 3, grid=(R // tr,), name=f"adamw_{tag}",
        in_specs=[spec] * 4, out_specs=(spec,) * 3, compiler_params=_cp("parallel"),
    )(g, w, m, v)


def _adamw_w_ada(c_t, dmod_cols, w, m, v):
    R, C = w.shape

    def body(ct_ref, dm_ref, w_ref, m_ref, v_ref, g_ref, d_ref, nm_ref, nv_ref):
        g = _dot(ct_ref[...].astype(BF16), dm_ref[...].astype(BF16))
        d, nm, nv = _adamw(w_ref[...], g, m_ref[...], v_ref[...])
        g_ref[...] = g
        d_ref[...] = d
        nm_ref[...] = nm
        nv_ref[...] = nv

    tr = 256
    spec = pl.BlockSpec((tr, C), lambda i: (i, 0))
    sds = jax.ShapeDtypeStruct((R, C), F32)
    return pl.pallas_call(
        body, out_shape=(sds,) * 4, grid=(R // tr,), name="adamw_w_ada",
        in_specs=[pl.BlockSpec((tr, 128), lambda i: (i, 0)), pl.BlockSpec((128, C), lambda i: (0, 0))] + [spec] * 3,
        out_specs=(spec,) * 4, compiler_params=_cp("parallel"),
    )(c_t, dmod_cols, w, m, v)


def _adamw_small(small_all, ws, ms, vs):
    def body(s_ref, w_ref, m_ref, v_ref, g_ref, d_ref, nm_ref, nv_ref):
        g = s_ref[0]
        for b in range(1, N_DEV):
            g = g + s_ref[b]
        d, nm, nv = _adamw(w_ref[...], g, m_ref[...], v_ref[...])
        g_ref[...] = g
        d_ref[...] = d
        nm_ref[...] = nm
        nv_ref[...] = nv

    sds = jax.ShapeDtypeStruct((SMALL_ROWS, D_MODEL), F32)
    return pl.pallas_call(
        body, out_shape=(sds,) * 4, name="adamw_small", in_specs=[VMEM_SPEC] * 4, out_specs=(VMEM_SPEC,) * 4,
        compiler_params=pltpu.CompilerParams(vmem_limit_bytes=VMEM_LIMIT_V7X),
    )(small_all, ws, ms, vs)


ROW_MOD, ROW_NORM_G, ROW_CONV_B, ROW_BA, ROW_BX, ROW_LAM, ROW_FINAL_G, ROW_SINKS, ROW_CONV_W, ROW_LOSS = 0, 3, 4, 5, 6, 7, 8, 9, 10, 14


def _pack_small(b_ada, norm_g, conv_b, ba, bx, lam, final_g, sinks, conv_w_full, loss_row=None):
    lane_pad = lambda a: jnp.pad(a.reshape(1, -1), ((0, 0), (0, D_MODEL - a.size)))
    rows = [b_ada.reshape(3, D_MODEL), norm_g, conv_b, ba, bx, lam, final_g.reshape(1, D_MODEL), lane_pad(sinks), conv_w_full,
            jnp.zeros((1, D_MODEL), F32) if loss_row is None else lane_pad(loss_row),
            jnp.zeros((SMALL_ROWS - ROW_LOSS - 1, D_MODEL), F32)]
    return jnp.concatenate([r.astype(F32) for r in rows], axis=0)


def kernel(x, c, positions, w_ada, b_ada, norm_g, w_in, attn_sinks, conv_w, conv_b, rg_wa, rg_ba, rg_wx, rg_bx, rg_lambda, w_attn_proj, w_rnn_proj, w_out, final_g, loss_target, m_w_ada, m_b_ada, m_norm_g, m_w_in, m_attn_sinks, m_conv_w, m_conv_b, m_rg_wa, m_rg_ba, m_rg_wx, m_rg_bx, m_rg_lambda, m_w_attn_proj, m_w_rnn_proj, m_w_out, m_final_g, v_w_ada, v_b_ada, v_norm_g, v_w_in, v_attn_sinks, v_conv_w, v_conv_b, v_rg_wa, v_rg_ba, v_rg_wx, v_rg_bx, v_rg_lambda, v_w_attn_proj, v_w_rnn_proj, v_w_out, v_final_g):
    T = x.shape[1]
    my_chip = lax.axis_index("x") * 2 + lax.axis_index("y")
    my_dev = my_chip * 2 + lax.axis_index("c")
    x2d, tgt = x[0], loss_target[0]
    pos_col = positions.reshape(T, 1)

    chip_idx = my_chip.reshape(1).astype(jnp.int32)
    c_idx = lax.axis_index("c").reshape(1).astype(jnp.int32)
    sq_place = ((D_MODEL, D_MODEL), (SHARD_ROWS, D_MODEL), lambda chip: (chip, 0))
    rg_place = ((RNN_BLOCKS, RNN_BW, RNN_BW), (RNN_BLOCKS, SHARD_RG, RNN_BW), lambda chip: (0, chip, 0))
    placed = [
        _cast_place(w_in[0], chip_idx, (D_MODEL, IN_W), (D_MODEL, SHARD_IN), lambda chip: (0, chip), "w_in"),
        _cast_place(w_attn_proj[0], chip_idx, *sq_place, "w_attn_proj"),
        _cast_place(w_rnn_proj[0], chip_idx, *sq_place, "w_rnn_proj"),
        _cast_place(w_out[0], chip_idx, *sq_place, "w_out"),
        _cast_place(rg_wa[0], chip_idx, *rg_place, "rg_wa"),
        _cast_place(rg_wx[0], chip_idx, *rg_place, "rg_wx"),
    ]
    cw_chips, c_all, mod_chips = _gather_mod(c.reshape(1, 1, D_MODEL), w_ada[0], conv_w[0])
    g_ssems, g_rsems, fulls, g_token = _gather_start([p.reshape(s) for p, s in zip(placed, FULL_SHAPES)], mod_chips)
    conv_w_f = jnp.transpose(cw_chips, (1, 0, 2)).reshape(CONV_W, D_MODEL)
    mod_all = jnp.transpose(mod_chips, (1, 0, 2)).reshape(N_DEV, ADA_W) + b_ada
    mod_row = lax.dynamic_slice_in_dim(mod_all, my_dev, 1, axis=0) + g_token[0:1, 0:1]

    tabs = _rope_tables(pos_col)
    h = _prenorm(x2d, mod_row, norm_g)
    w_in_v = fulls[0]
    proj = _in_projection(h, w_in_v.reshape(D_MODEL, IN_W), chip_idx, None, "own")
    for k, mask in enumerate(CHIP_MASKS):
        w_in_v = _gather_wait(g_ssems[k], g_rsems[k], [w_in_v], [0], proj, f"w_in_{k}")[0]
        w_in_v = _forward_halves([w_in_v], [(0, 0, k)], f"w_in_{k}")[0]
        from_chip = (chip_idx ^ (mask >> 1)).astype(jnp.int32)
        proj = _in_projection(h, w_in_v.reshape(D_MODEL, IN_W), from_chip, proj, f"from_{k}")
    w_in_f = w_in_v.reshape(D_MODEL, IN_W)
    rest = _gather_wait(g_ssems[3], g_rsems[3], list(fulls[1:]), [1, 2, 3, 4, 5], proj, "rest")
    rest = _forward_halves(rest, [(idx - 1, idx, k) for idx in range(1, N_BIG) for k in range(3)], "rest")
    wap_f, wrp_f, wo_f = (g.reshape(D_MODEL, D_MODEL) for g in rest[0:3])
    rwa_f, rwx_f = (g.reshape(RNN_BLOCKS, RNN_BW, RNN_BW) for g in rest[3:5])
    y_attn = _attn_forward(proj, tabs, attn_sinks)
    y_rnn, h_rnn, *rnn_saved = _rnn_forward(proj, pos_col, conv_w_f, conv_b, rwa_f, rwx_f, rg_ba, rg_bx, rg_lambda)
    (dx2, merged, d_o, d_pa, d_pr, d_ya, d_yr, d_c, d_final_g, d_gate, loss_vec) = _merge_and_head(
        x2d, tgt, y_attn, y_rnn, proj, wap_f, wrp_f, wo_f, mod_row, final_g.reshape(1, D_MODEL))

    sq = (N_CHIPS, 2, SHARD_ROWS // 2, D_MODEL)
    rg = (RNN_BLOCKS, N_CHIPS, 2, SHARD_RG // 2, RNN_BW)
    rg_flat = (RNN_BLOCKS * N_CHIPS, 2, SHARD_RG // 2, RNN_BW)

    def chip_sum_and_start(views, axes, flat, unflat, tags_, kinds_, group):
        from_sib = _swap_halves(views, axes)
        sums = [_presum(v.reshape(f), s.reshape(f[:1] + f[2:]), c_idx, t) for v, s, f, t in zip(views, from_sib, flat, tags_)]
        exact = [s[0].reshape(u) for s, u in zip(sums, unflat)]
        rounded = [s[1].reshape(u) for s, u in zip(sums, unflat)]
        return _exchange_start(rounded, kinds_, group), exact

    g_ap = _weight_grad(y_attn, [(d_pa, 0, 2)], "w_attn_proj")
    g_rp = _weight_grad(y_rnn, [(d_pr, 0, 2)], "w_rnn_proj")
    g_o = _weight_grad(merged, [(d_o, 0, 2)], "w_out")
    sq_half = (N_CHIPS, SHARD_ROWS // 2, D_MODEL)
    started1, own1 = chip_sum_and_start([g_ap.reshape(sq), g_rp.reshape(sq), g_o.reshape(sq)], [1, 1, 1], [sq] * 3, [sq_half] * 3,
                                  ["w_attn_proj", "w_rnn_proj", "w_out"], ["sq"] * 3, "proj")
    d_q, d_kv, d_ga, d_sinks = _attn_backward(proj, d_ya, tabs, attn_sinks + started1[4][0, 0])
    d_b, d_conv_w, d_conv_b, d_rwa, d_rwx, d_ba, d_bx, d_lam = _rnn_backward(
        proj, pos_col, h_rnn, rnn_saved, d_yr, conv_w_f, rwa_f, rwx_f, rg_lambda)
    pieces = [(d_q, CB_Q, 2), (d_kv, CB_KV, 1), (d_ga, CB_GA, 2), (d_b, CB_XR, 4), (d_c, CB_MA, 4)]
    g_in = _weight_grad(h, pieces, "w_in")
    started2, own2 = chip_sum_and_start(
        [g_in.reshape(2, D_MODEL // 2, IN_W), d_rwa.reshape(rg), d_rwx.reshape(rg)], [0, 2, 2],
        [(1, 2, D_MODEL // 2, IN_W), rg_flat, rg_flat],
        [(D_MODEL // 2, IN_W), (RNN_BLOCKS, N_CHIPS, SHARD_RG // 2, RNN_BW), (RNN_BLOCKS, N_CHIPS, SHARD_RG // 2, RNN_BW)],
        ["w_in", "rg_wa", "rg_wx"], ["in", "rg", "rg"], "in")
    grad_x, d_shift, d_scale, d_norm_g = _input_backward(pieces, w_in_f, x2d, dx2, mod_row + started2[4][0, 0], norm_g)

    d_mod = jnp.concatenate([d_shift, d_scale, d_gate], axis=1)
    small = _pack_small(d_mod, d_norm_g, d_conv_b, d_ba, d_bx, d_lam, d_final_g, d_sinks[:, :N_HEADS], d_conv_w, loss_vec)
    small_all = _gather_small(small)
    _, lands1 = _exchange_wait(*started1[:4], grad_x, "proj")
    _, lands2 = _exchange_wait(*started2[:4], grad_x, "in")
    tags = ["w_in", "w_attn_proj", "w_rnn_proj", "w_out", "rg_wa", "rg_wx"]
    chip_sums = [own2[0]] + list(own1) + list(own2[1:])
    lands = [lands2[0]] + list(lands1) + list(lands2[1:])
    where = jnp.concatenate([chip_idx, c_idx])
    kinds = ["in", "sq", "sq", "sq", "rg", "rg"]
    halves = [_sum_landed(kinds[i], chip_sums[i], lands[i], where, tags[i]) for i in range(6)]
    grads = _assemble_with_sibling(halves, [0, 0, 0, 0, 1, 1])
    shapes2d = [(D_MODEL, SHARD_IN), (SHARD_ROWS, D_MODEL), (SHARD_ROWS, D_MODEL), (SHARD_ROWS, D_MODEL),
                (RNN_BLOCKS * SHARD_RG, RNN_BW), (RNN_BLOCKS * SHARD_RG, RNN_BW)]
    big_w = [w_in, w_attn_proj, w_rnn_proj, w_out, rg_wa, rg_wx]
    big_m = [m_w_in, m_w_attn_proj, m_w_rnn_proj, m_w_out, m_rg_wa, m_rg_wx]
    big_v = [v_w_in, v_w_attn_proj, v_w_rnn_proj, v_w_out, v_rg_wa, v_rg_wx]
    res = {}
    for i, tag in enumerate(tags):
        g = grads[i].reshape(shapes2d[i])
        outs = _adamw_shard(g, big_w[i].reshape(shapes2d[i]), big_m[i].reshape(shapes2d[i]), big_v[i].reshape(shapes2d[i]), tag)
        res[tag] = [o.reshape(big_w[i].shape) for o in (g,) + tuple(outs)]

    dmod_all = small_all[:, ROW_MOD:ROW_MOD + 3, :].reshape(N_DEV, ADA_W)
    dmod_cols = lax.dynamic_slice_in_dim(dmod_all, my_chip * SHARD_ADA, SHARD_ADA, axis=1)
    c_t = jnp.pad(jnp.transpose(c_all.reshape(N_DEV, D_MODEL)), ((0, 0), (0, 128 - N_DEV)))
    dmod_cols = jnp.pad(dmod_cols, ((0, 128 - N_DEV), (0, 0)))
    res["w_ada"] = [o.reshape(w_ada.shape) for o in _adamw_w_ada(c_t, dmod_cols, w_ada[0], m_w_ada[0], v_w_ada[0])]

    def full_conv(a):
        return lax.dynamic_update_slice_in_dim(jnp.zeros((CONV_W, D_MODEL), F32), a[0], my_chip * (D_MODEL // N_CHIPS), axis=1)

    packed = [_pack_small(p[0], p[1], p[2], p[3], p[4], p[5], p[6], p[7], full_conv(p[8])) for p in (
        (b_ada, norm_g, conv_b, rg_ba, rg_bx, rg_lambda, final_g, attn_sinks, conv_w),
        (m_b_ada, m_norm_g, m_conv_b, m_rg_ba, m_rg_bx, m_rg_lambda, m_final_g, m_attn_sinks, m_conv_w),
        (v_b_ada, v_norm_g, v_conv_b, v_rg_ba, v_rg_bx, v_rg_lambda, v_final_g, v_attn_sinks, v_conv_w))]
    small_out = _adamw_small(small_all, *packed)

    def unpack(slab):
        cw = lax.dynamic_slice_in_dim(slab[ROW_CONV_W:ROW_CONV_W + CONV_W], my_chip * (D_MODEL // N_CHIPS),
                                      D_MODEL // N_CHIPS, axis=1)
        return {
            "b_ada": slab[ROW_MOD:ROW_MOD + 3].reshape(1, ADA_W), "norm_g": slab[ROW_NORM_G:ROW_NORM_G + 1],
            "conv_b": slab[ROW_CONV_B:ROW_CONV_B + 1], "rg_ba": slab[ROW_BA:ROW_BA + 1], "rg_bx": slab[ROW_BX:ROW_BX + 1],
            "rg_lambda": slab[ROW_LAM:ROW_LAM + 1], "final_g": slab[ROW_FINAL_G], "attn_sinks": slab[ROW_SINKS:ROW_SINKS + 1, :N_HEADS],
            "conv_w": cw[None],
        }

    small_res = [unpack(s) for s in small_out]
    order = ["w_ada", "b_ada", "norm_g", "w_in", "attn_sinks", "conv_w", "conv_b", "rg_wa", "rg_ba", "rg_wx", "rg_bx",
             "rg_lambda", "w_attn_proj", "w_rnn_proj", "w_out", "final_g"]
    loss = small_out[0][ROW_LOSS, 0]
    outs = [loss, grad_x[None]]
    for kind in range(4):
        for name in order:
            outs.append(res[name][kind] if name in res else small_res[kind][name])
    return tuple(outs)
```

```python
import numpy as np
import jax
import jax.numpy as jnp
from jax import lax
from jax.experimental import pallas as pl
from jax.experimental.pallas import tpu as pltpu

F32 = jnp.float32
BF16 = jnp.bfloat16

D_MODEL = 1024
N_HEADS = 16
N_KV = 4
HEAD_DIM = 64
GROUP = N_HEADS // N_KV
BLOCK = 128
KV_W = N_KV * HEAD_DIM
ROT_HALF = 8
ROPE_THETA = 500000.0
ATTN_SCALE = 0.125
RNN_BLOCKS = 4
RNN_BW = 256
CONV_W = 4
LRU_C = 8.0
NORM_EPS = 1e-6
IN_W = 6656
CB = 512
N_CB = IN_W // CB
CB_Q, CB_KV, CB_GA, CB_XR, CB_GR, CB_MA, CB_MR = 0, 2, 3, 5, 7, 9, 11
V_COL_BLOCK = 5
N_CHIPS = 4
N_DEV = 8
SHARD_IN = IN_W // N_CHIPS
SHARD_ROWS = D_MODEL // N_CHIPS
SHARD_RG = RNN_BW // N_CHIPS
ADA_W = 3 * D_MODEL
SHARD_ADA = ADA_W // N_CHIPS
SMALL_ROWS = 16

ADAM_LR = 0.001
ADAM_B1 = 0.9
ADAM_B2 = 0.999
ADAM_EPS = 1e-08
ADAM_WD = 0.01
ADAM_STEP = 10

VMEM_LIMIT_V7X = 52 * 1024 * 1024
MESH = pl.DeviceIdType.MESH
ANY = pl.BlockSpec(memory_space=pl.ANY)
VMEM_SPEC = pl.BlockSpec(memory_space=pltpu.VMEM)


def _cp(*sem):
    return pltpu.CompilerParams(dimension_semantics=sem if sem else None, vmem_limit_bytes=VMEM_LIMIT_V7X)


def _dot(a, b):
    return jnp.dot(a, b, preferred_element_type=F32)


def _dot_nt(a, b):
    return lax.dot_general(a, b, (((1,), (1,)), ((), ())), preferred_element_type=F32)


def _dot_tn(a, b):
    return lax.dot_general(a, b, (((0,), (0,)), ((), ())), preferred_element_type=F32)


def _sigmoid(z):
    return 1.0 / (1.0 + jnp.exp(-z))


def _softplus(z):
    u = jnp.exp(-jnp.abs(z))
    log1p_u = jnp.where(u < 1e-3, u * (1.0 - u * (0.5 - u * (1.0 / 3.0))), jnp.log(1.0 + u))
    return jnp.maximum(z, 0.0) + log1p_u


def _rms(xf):
    return lax.rsqrt(jnp.mean(xf * xf, axis=-1, keepdims=True) + NORM_EPS)


def _me():
    return lax.axis_index("x"), lax.axis_index("y"), lax.axis_index("c")


def _peer(mask):
    x, y, c = _me()
    fx, fy, fc = (mask >> 2) & 1, (mask >> 1) & 1, mask & 1
    return (x ^ fx if fx else x, y ^ fy if fy else y, c ^ fc if fc else c)


def _chip_of(pos):
    return pos[0] * 2 + pos[1]


CHIP_MASKS = (4, 2, 6)
ALL_MASKS = (1, 2, 3, 4, 5, 6, 7)


HBM_SPEC = pl.BlockSpec(memory_space=pltpu.HBM)
SEM_SPEC = pl.BlockSpec(memory_space=pltpu.SEMAPHORE)
SPLIT_COPY = pltpu.CompilerParams(has_side_effects=pltpu.SideEffectType.DATAFLOW_SIDE_EFFECTING)
N_BIG = 6
FULL_SHAPES = (
    (2, D_MODEL // 2, IN_W),
    (N_CHIPS, 2, SHARD_ROWS // 2, D_MODEL), (N_CHIPS, 2, SHARD_ROWS // 2, D_MODEL), (N_CHIPS, 2, SHARD_ROWS // 2, D_MODEL),
    (RNN_BLOCKS, N_CHIPS, 2, SHARD_RG // 2, RNN_BW), (RNN_BLOCKS, N_CHIPS, 2, SHARD_RG // 2, RNN_BW),
)


def _slot(full, idx, chip, half):
    if idx == 0:
        return full.at[half, :, pl.ds(pl.multiple_of(chip * SHARD_IN, 128), SHARD_IN)]
    return full.at[chip, half] if idx in (1, 2, 3) else full.at[:, chip, half]


def _three_halves(full, idx):
    return full.at[pl.ds(0, 3), 0] if idx in (1, 2, 3) else full.at[:, pl.ds(0, 3), 0]


def _gather_start(fulls, after):
    def body(*refs):
        full_refs = refs[:N_BIG]
        ssems, rsems = refs[N_BIG + 1:N_BIG + 5], refs[N_BIG + 5:N_BIG + 9]
        token = refs[2 * N_BIG + 9]
        me = _me()
        my_chip = _chip_of(me)
        for idx in range(N_BIG):
            for k, mask in enumerate(CHIP_MASKS):
                pair = k if idx == 0 else 3
                mine = _slot(full_refs[idx], idx, my_chip, me[2])
                pltpu.make_async_remote_copy(src_ref=mine, dst_ref=mine, send_sem=ssems[pair], recv_sem=rsems[pair],
                                             device_id=_peer(mask), device_id_type=MESH).start()
        token[...] = jnp.zeros_like(token)

    sem = pltpu.SemaphoreType.DMA(())
    out_shape = (sem,) * 8 + tuple(pltpu.HBM(f.shape, f.dtype) for f in fulls) + (jax.ShapeDtypeStruct((8, 128), F32),)
    outs = pl.pallas_call(
        body, out_shape=out_shape, name="gather_start",
        in_specs=[HBM_SPEC] * N_BIG + [ANY], out_specs=tuple([SEM_SPEC] * 8 + [HBM_SPEC] * N_BIG + [VMEM_SPEC]),
        input_output_aliases={i: 8 + i for i in range(N_BIG)}, compiler_params=SPLIT_COPY,
    )(*[pltpu.with_memory_space_constraint(f, pltpu.HBM) for f in fulls], after)
    return outs[0:4], outs[4:8], outs[8:8 + N_BIG], outs[8 + N_BIG]


def _gather_wait(ssem, rsem, arrays, idxs, after, tag):
    n = len(arrays)

    def body(*refs):
        full_refs, ssem_ref, rsem_ref = refs[:n], refs[n], refs[n + 1]
        me = _me()
        for full, idx in zip(full_refs, idxs):
            region = _slot(full, 0, _chip_of(me), me[2]) if idx == 0 else _three_halves(full, idx)
            arrived = pltpu.make_async_remote_copy(
                src_ref=region, dst_ref=region, send_sem=ssem_ref, recv_sem=rsem_ref, device_id=me, device_id_type=MESH)
            arrived.wait_send()
            arrived.wait_recv()

    outs = pl.pallas_call(
        body, out_shape=tuple(pltpu.HBM(a.shape, a.dtype) for a in arrays), name=f"gather_wait_{tag}",
        in_specs=[HBM_SPEC] * n + [SEM_SPEC, SEM_SPEC, ANY], out_specs=tuple([HBM_SPEC] * n),
        input_output_aliases={i: i for i in range(n)}, compiler_params=SPLIT_COPY,
    )(*arrays, ssem, rsem, after)
    return list(outs)


def _forward_halves(arrays, items, tag):
    n, m = len(arrays), len(items)

    def body(*refs):
        outs, ssem, rsem = refs[n:2 * n], refs[2 * n], refs[2 * n + 1]
        me = _me()
        sib = _peer(1)
        cps = []
        for j, (pos, idx, k) in enumerate(items):
            chip = _chip_of(_peer(CHIP_MASKS[k]))
            cp = pltpu.make_async_remote_copy(
                src_ref=_slot(outs[pos], idx, chip, me[2]), dst_ref=_slot(outs[pos], idx, chip, me[2]),
                send_sem=ssem.at[j], recv_sem=rsem.at[j], device_id=sib, device_id_type=MESH)
            cp.start()
            cps.append(cp)
        for j, (pos, idx, k) in enumerate(items):
            chip = _chip_of(_peer(CHIP_MASKS[k]))
            pltpu.make_async_remote_copy(
                src_ref=_slot(outs[pos], idx, chip, me[2]), dst_ref=_slot(outs[pos], idx, chip, 1 - me[2]),
                send_sem=ssem.at[j], recv_sem=rsem.at[j], device_id=sib, device_id_type=MESH).wait_recv()
        for cp in cps:
            cp.wait_send()

    outs = pl.pallas_call(
        body, out_shape=tuple(jax.ShapeDtypeStruct(a.shape, a.dtype) for a in arrays), name=f"forward_halves_{tag}",
        in_specs=[ANY] * n, out_specs=tuple([ANY] * n), input_output_aliases={i: i for i in range(n)},
        scratch_shapes=[pltpu.SemaphoreType.DMA((m,)), pltpu.SemaphoreType.DMA((m,))],
    )(*arrays)
    return list(outs)


def _gather_mod(c_row, w_ada_s, conv_w_s):
    def body(c_ref, wada_ref, cw_s, cw_f, call_ref, mod_ref, wsend, wrecv, lsem, csend, crecv, msend, mrecv):
        me = _me()
        my_chip = _chip_of(me)
        my_dev = my_chip * 2 + me[2]
        sends = []
        for k, mask in enumerate(CHIP_MASKS):
            cp = pltpu.make_async_remote_copy(src_ref=cw_s, dst_ref=cw_f.at[my_chip], send_sem=wsend.at[k], recv_sem=wrecv.at[k],
                                              device_id=_peer(mask), device_id_type=MESH)
            cp.start()
            sends.append(cp)
        local = [pltpu.make_async_copy(cw_s, cw_f.at[my_chip], lsem.at[0])]
        for cp in local:
            cp.start()

        call_ref[my_dev] = c_ref[0]
        csends = []
        for k, mask in enumerate(ALL_MASKS):
            cp = pltpu.make_async_remote_copy(
                src_ref=c_ref.at[0], dst_ref=call_ref.at[my_dev],
                send_sem=csend.at[k], recv_sem=crecv.at[k], device_id=_peer(mask), device_id_type=MESH)
            cp.start()
            csends.append(cp)
        for k, mask in enumerate(ALL_MASKS):
            frm = _peer(mask)
            pltpu.make_async_remote_copy(
                src_ref=c_ref.at[0], dst_ref=call_ref.at[_chip_of(frm) * 2 + frm[2]],
                send_sem=csend.at[k], recv_sem=crecv.at[k], device_id=frm, device_id_type=MESH).wait_recv()
        for cp in csends:
            cp.wait_send()

        c_all = call_ref[...].reshape(N_DEV, D_MODEL).astype(BF16)
        mod_ref[my_chip] = _dot(c_all, wada_ref[...].astype(BF16))
        msends = []
        for k, mask in enumerate(CHIP_MASKS):
            cp = pltpu.make_async_remote_copy(
                src_ref=mod_ref.at[my_chip], dst_ref=mod_ref.at[my_chip],
                send_sem=msend.at[k], recv_sem=mrecv.at[k], device_id=_peer(mask), device_id_type=MESH)
            cp.start()
            msends.append(cp)
        for k, mask in enumerate(CHIP_MASKS):
            frm = _peer(mask)
            pltpu.make_async_remote_copy(
                src_ref=mod_ref.at[my_chip], dst_ref=mod_ref.at[_chip_of(frm)],
                send_sem=msend.at[k], recv_sem=mrecv.at[k], device_id=frm, device_id_type=MESH).wait_recv()
        for cp in msends:
            cp.wait_send()

        for k, mask in enumerate(CHIP_MASKS):
            frm = _peer(mask)
            pltpu.make_async_remote_copy(src_ref=cw_s, dst_ref=cw_f.at[_chip_of(frm)], send_sem=wsend.at[k], recv_sem=wrecv.at[k],
                                         device_id=frm, device_id_type=MESH).wait_recv()
        for cp in sends:
            cp.wait_send()
        for cp in local:
            cp.wait()

    out_shape = (
        jax.ShapeDtypeStruct((N_CHIPS, CONV_W, D_MODEL // N_CHIPS), F32),
        jax.ShapeDtypeStruct((N_DEV, 1, D_MODEL), F32),
        jax.ShapeDtypeStruct((N_CHIPS, N_DEV, SHARD_ADA), F32),
    )
    return pl.pallas_call(
        body, out_shape=out_shape, name="gather_mod",
        in_specs=[VMEM_SPEC, VMEM_SPEC, ANY], out_specs=(ANY, VMEM_SPEC, VMEM_SPEC),
        scratch_shapes=[
            pltpu.SemaphoreType.DMA((3,)), pltpu.SemaphoreType.DMA((3,)), pltpu.SemaphoreType.DMA((1,)),
            pltpu.SemaphoreType.DMA((7,)), pltpu.SemaphoreType.DMA((7,)),
            pltpu.SemaphoreType.DMA((3,)), pltpu.SemaphoreType.DMA((3,)),
        ],
        compiler_params=pltpu.CompilerParams(vmem_limit_bytes=VMEM_LIMIT_V7X),
    )(c_row, w_ada_s, conv_w_s)


def _cast_place(shard, chip_idx, full_shape, block, index_map, tag):
    def body(chip_ref, s_ref, o_ref):
        o_ref[...] = s_ref[...].astype(BF16)

    grid_spec = pltpu.PrefetchScalarGridSpec(
        num_scalar_prefetch=1, grid=(1,),
        in_specs=[pl.BlockSpec(shard.shape, lambda i, chip_ref: (0,) * shard.ndim)],
        out_specs=pl.BlockSpec(block, lambda i, chip_ref: index_map(chip_ref[0])))
    return pl.pallas_call(
        body, out_shape=jax.ShapeDtypeStruct(full_shape, BF16), grid_spec=grid_spec, name=f"cast_place_{tag}",
        compiler_params=_cp("arbitrary"),
    )(chip_idx, shard)


def _shard_of(ref, kind, chip):
    if kind == "in":
        return ref.at[:, pl.ds(pl.multiple_of(chip * SHARD_IN, 128), SHARD_IN)]
    return ref.at[chip] if kind == "sq" else ref.at[:, chip]


def _land_shape(src, kind):
    if kind == "in":
        return (3, src.shape[0], SHARD_IN)
    return (3,) + src.shape[1:] if kind == "sq" else (3, src.shape[0]) + src.shape[2:]


def _exchange_start(srcs, kinds, tag):
    n = len(srcs)
    lands = [pltpu.with_memory_space_constraint(lax.empty(_land_shape(s, k), s.dtype), pltpu.HBM) for s, k in zip(srcs, kinds)]

    def body(*refs):
        src_refs, land_refs = refs[:n], refs[n:2 * n]
        ssems, rsems = refs[2 * n:3 * n], refs[3 * n:4 * n]
        token = refs[6 * n]
        for i in range(n):
            for k, mask in enumerate(CHIP_MASKS):
                to = _peer(mask)
                pltpu.make_async_remote_copy(
                    src_ref=_shard_of(src_refs[i], kinds[i], _chip_of(to)), dst_ref=land_refs[i].at[k],
                    send_sem=ssems[i], recv_sem=rsems[i], device_id=to, device_id_type=MESH).start()
        token[...] = jnp.zeros_like(token)

    sem = pltpu.SemaphoreType.DMA(())
    out_shape = ((sem,) * (2 * n) + tuple(pltpu.HBM(s.shape, s.dtype) for s in srcs)
                 + tuple(pltpu.HBM(l.shape, l.dtype) for l in lands) + (jax.ShapeDtypeStruct((8, 128), F32),))
    outs = pl.pallas_call(
        body, out_shape=out_shape, name=f"exchange_start_{tag}",
        in_specs=[HBM_SPEC] * (2 * n), out_specs=tuple([SEM_SPEC] * (2 * n) + [HBM_SPEC] * (2 * n) + [VMEM_SPEC]),
        input_output_aliases={i: 2 * n + i for i in range(2 * n)},
        compiler_params=pltpu.CompilerParams(has_side_effects=pltpu.SideEffectType.DATAFLOW_SIDE_EFFECTING),
    )(*[pltpu.with_memory_space_constraint(s, pltpu.HBM) for s in srcs], *lands)
    return outs[:n], outs[n:2 * n], outs[2 * n:3 * n], outs[3 * n:4 * n], outs[4 * n]


def _exchange_wait(ssems, rsems, srcs, lands, after, tag):
    n = len(srcs)

    def body(*refs):
        land_refs = refs[n:2 * n]
        ssem_refs, rsem_refs = refs[2 * n:3 * n], refs[3 * n:4 * n]
        for i in range(n):
            all_three = pltpu.make_async_remote_copy(
                src_ref=land_refs[i], dst_ref=land_refs[i], send_sem=ssem_refs[i], recv_sem=rsem_refs[i],
                device_id=_me(), device_id_type=MESH)
            all_three.wait_send()
            all_three.wait_recv()

    outs = pl.pallas_call(
        body, out_shape=tuple(pltpu.HBM(a.shape, a.dtype) for a in list(srcs) + list(lands)), name=f"exchange_wait_{tag}",
        in_specs=[HBM_SPEC] * (2 * n) + [SEM_SPEC] * (2 * n) + [ANY], out_specs=tuple([HBM_SPEC] * (2 * n)),
        input_output_aliases={i: i for i in range(2 * n)},
        compiler_params=pltpu.CompilerParams(has_side_effects=pltpu.SideEffectType.DATAFLOW_SIDE_EFFECTING),
    )(*srcs, *lands, *ssems, *rsems, after)
    return outs[:n], outs[n:]


def _gather_small(small):
    def body(small_ref, small_all, ssend, srecv):
        me = _me()
        my_dev = _chip_of(me) * 2 + me[2]
        small_all[my_dev] = small_ref[...]
        ssends = []
        for k, mask in enumerate(ALL_MASKS):
            cp = pltpu.make_async_remote_copy(
                src_ref=small_ref, dst_ref=small_all.at[my_dev],
                send_sem=ssend.at[k], recv_sem=srecv.at[k], device_id=_peer(mask), device_id_type=MESH)
            cp.start()
            ssends.append(cp)
        for k, mask in enumerate(ALL_MASKS):
            frm = _peer(mask)
            pltpu.make_async_remote_copy(
                src_ref=small_ref, dst_ref=small_all.at[_chip_of(frm) * 2 + frm[2]],
                send_sem=ssend.at[k], recv_sem=srecv.at[k], device_id=frm, device_id_type=MESH).wait_recv()
        for cp in ssends:
            cp.wait_send()

    return pl.pallas_call(
        body, out_shape=jax.ShapeDtypeStruct((N_DEV, SMALL_ROWS, D_MODEL), F32), name="gather_small",
        in_specs=[VMEM_SPEC], out_specs=VMEM_SPEC,
        scratch_shapes=[pltpu.SemaphoreType.DMA((7,)), pltpu.SemaphoreType.DMA((7,))],
    )(small)


def _half_of(ref, axis, half):
    return ref.at[(slice(None),) * axis + (half,)]


def _swap_halves(parts, axes):
    n = len(parts)

    def body(*refs):
        ins, outs, ssem, rsem = refs[:n], refs[n:2 * n], refs[2 * n], refs[2 * n + 1]
        c = lax.axis_index("c")
        cps = [pltpu.make_async_remote_copy(src_ref=_half_of(ins[i], axes[i], 1 - c), dst_ref=outs[i], send_sem=ssem.at[i],
                                            recv_sem=rsem.at[i], device_id=_peer(1), device_id_type=MESH) for i in range(n)]
        for cp in cps:
            cp.start()
        for cp in cps:
            cp.wait()

    shapes = [p.shape[:a] + p.shape[a + 1:] for p, a in zip(parts, axes)]
    return pl.pallas_call(
        body, out_shape=tuple(jax.ShapeDtypeStruct(s, p.dtype) for s, p in zip(shapes, parts)), name="swap_halves",
        in_specs=[ANY] * n, out_specs=tuple([ANY] * n),
        scratch_shapes=[pltpu.SemaphoreType.DMA((n,)), pltpu.SemaphoreType.DMA((n,))],
    )(*parts)


def _presum(mine, sib, c_idx, tag):
    S, _, R, C = mine.shape
    tr = min(R, 256)
    tc = SHARD_IN if C % SHARD_IN == 0 else C

    def body(c_ref, m_ref, s_ref, o_ref, ob_ref):
        total = m_ref[:, 0] + s_ref[...]
        o_ref[...] = total
        ob_ref[...] = total.astype(BF16)

    out_spec = pl.BlockSpec((S, tr, tc), lambda i, j, c_ref: (0, i, j))
    grid_spec = pltpu.PrefetchScalarGridSpec(
        num_scalar_prefetch=1, grid=(R // tr, C // tc),
        in_specs=[pl.BlockSpec((S, 1, tr, tc), lambda i, j, c_ref: (0, c_ref[0], i, j)),
                  pl.BlockSpec((S, tr, tc), lambda i, j, c_ref: (0, i, j))],
        out_specs=(out_spec, out_spec))
    return pl.pallas_call(
        body, out_shape=(jax.ShapeDtypeStruct((S, R, C), F32), jax.ShapeDtypeStruct((S, R, C), BF16)),
        grid_spec=grid_spec, name=f"presum_{tag}", compiler_params=_cp("parallel", "parallel"),
    )(c_idx, mine, sib)


def _assemble_with_sibling(parts, axes):
    n = len(parts)

    def body(*refs):
        outs, ssem, rsem = refs[n:2 * n], refs[2 * n], refs[2 * n + 1]
        c = lax.axis_index("c")
        cps = [pltpu.make_async_remote_copy(
            src_ref=_half_of(outs[i], axes[i], c), dst_ref=_half_of(outs[i], axes[i], c), send_sem=ssem.at[i],
            recv_sem=rsem.at[i], device_id=_peer(1), device_id_type=MESH) for i in range(n)]
        for cp in cps:
            cp.start()
        for i in range(n):
            pltpu.make_async_remote_copy(
                src_ref=_half_of(outs[i], axes[i], c), dst_ref=_half_of(outs[i], axes[i], 1 - c), send_sem=ssem.at[i],
                recv_sem=rsem.at[i], device_id=_peer(1), device_id_type=MESH).wait_recv()
        for cp in cps:
            cp.wait_send()

    return pl.pallas_call(
        body, out_shape=tuple(jax.ShapeDtypeStruct(p.shape, p.dtype) for p in parts), name="assemble_with_sibling",
        in_specs=[ANY] * n, out_specs=tuple([ANY] * n), input_output_aliases={i: i for i in range(n)},
        scratch_shapes=[pltpu.SemaphoreType.DMA((n,)), pltpu.SemaphoreType.DMA((n,))],
    )(*parts)


def _rope_tables(pos_col):
    T = pos_col.shape[0]
    tm = min(T, 512)
    inv = np.float32(ROPE_THETA) ** (-(np.arange(0, 2 * ROT_HALF, 2, dtype=np.float32)) / np.float32(2 * ROT_HALF))
    lane = np.arange(128) % HEAD_DIM
    freq = np.where(lane < 2 * ROT_HALF, inv[lane % ROT_HALF], 0.0).astype(np.float32)[None, :]

    def body(pos_ref, f_ref, c_ref, sa_ref, sb_ref):
        ang = pos_ref[...].astype(F32) * f_ref[...]
        c, s = jnp.cos(ang), jnp.sin(ang)
        m = lax.broadcasted_iota(jnp.int32, ang.shape, 1) & (HEAD_DIM - 1)
        c_ref[...] = jnp.where(m < 2 * ROT_HALF, c, 1.0)
        sa_ref[...] = jnp.where(m < ROT_HALF, -s, 0.0)
        sb_ref[...] = jnp.where((m >= ROT_HALF) & (m < 2 * ROT_HALF), s, 0.0)

    tab = jax.ShapeDtypeStruct((T, 128), F32)
    return pl.pallas_call(
        body, out_shape=(tab, tab, tab), grid=(T // tm,), name="rope_tables",
        in_specs=[pl.BlockSpec((tm, 1), lambda i: (i, 0)), pl.BlockSpec((1, 128), lambda i: (0, 0))],
        out_specs=tuple(pl.BlockSpec((tm, 128), lambda i: (i, 0)) for _ in range(3)),
        compiler_params=_cp("parallel"),
    )(pos_col, jnp.asarray(freq))


def _wide(tab, width):
    del width
    return tab


def _columns(t):
    return [t[:, i:i + 128] for i in range(0, t.shape[-1], 128)]


def _rope(t, c, sa, sb):
    return jnp.concatenate(
        [x * c + pltpu.roll(x, 128 - ROT_HALF, 1) * sa + pltpu.roll(x, ROT_HALF, 1) * sb for x in _columns(t)], axis=1)


def _unrope(d, c, sa, sb):
    return jnp.concatenate(
        [x * c + pltpu.roll(x * sa, ROT_HALF, 1) + pltpu.roll(x * sb, 128 - ROT_HALF, 1) for x in _columns(d)], axis=1)


def _prenorm(x, mod_row, norm_g):
    T = x.shape[0]
    tm = min(T, 512)

    def body(x_ref, mod_ref, g_ref, h_ref):
        xf = x_ref[...]
        shift, scale = mod_ref[:, 0:D_MODEL], mod_ref[:, D_MODEL:2 * D_MODEL]
        h = (xf * _rms(xf)) * g_ref[...] * (1.0 + scale) + shift
        h_ref[...] = h.astype(BF16)

    return pl.pallas_call(
        body, out_shape=jax.ShapeDtypeStruct((T, D_MODEL), BF16), grid=(T // tm,), name="prenorm",
        in_specs=[pl.BlockSpec((tm, D_MODEL), lambda i: (i, 0)), pl.BlockSpec((1, ADA_W), lambda i: (0, 0)),
                  pl.BlockSpec((1, D_MODEL), lambda i: (0, 0))],
        out_specs=pl.BlockSpec((tm, D_MODEL), lambda i: (i, 0)),
        compiler_params=_cp("parallel"),
    )(x, mod_row, norm_g)


def _in_projection(h, w_in, chip, into, tag):
    T = h.shape[0]
    tm, tn = min(T, 512), SHARD_IN

    def body(chip_ref, h_ref, w_ref, *rest):
        rest[-1][...] = _dot(h_ref[...], w_ref[...])

    in_specs = [pl.BlockSpec((tm, D_MODEL), lambda i, c: (i, 0)),
                pl.BlockSpec((D_MODEL, tn), lambda i, c: (0, c[0]), pipeline_mode=pl.Buffered(1))]
    args = [chip, h, w_in]
    aliases = {}
    if into is not None:
        in_specs.append(ANY)
        args.append(into)
        aliases = {3: 0}
    grid_spec = pltpu.PrefetchScalarGridSpec(num_scalar_prefetch=1, grid=(T // tm,), in_specs=in_specs,
                                             out_specs=pl.BlockSpec((tm, tn), lambda i, c: (i, c[0])))
    return pl.pallas_call(
        body, out_shape=jax.ShapeDtypeStruct((T, IN_W), F32), grid_spec=grid_spec, name=f"in_projection_{tag}",
        input_output_aliases=aliases, compiler_params=_cp("parallel"),
    )(*args)


def _attn_mask(n):
    qi = lax.broadcasted_iota(jnp.int32, (GROUP * BLOCK, 2 * BLOCK), 0) & (BLOCK - 1)
    kj = lax.broadcasted_iota(jnp.int32, (GROUP * BLOCK, 2 * BLOCK), 1)
    diff = qi + BLOCK - kj
    return (diff >= 0) & (diff < BLOCK) & ((kj >= BLOCK) | (n > 0))


ROW_GROUP_HEAD = (0, 2, 1, 3)


def _sink_col(sink_ref, kh):
    rowg = lax.broadcasted_iota(jnp.int32, (GROUP * BLOCK, 1), 0) // BLOCK
    col = jnp.full((GROUP * BLOCK, 1), sink_ref[0, GROUP * kh + ROW_GROUP_HEAD[0]], F32)
    for g in range(1, GROUP):
        col = jnp.where(rowg == g, sink_ref[0, GROUP * kh + ROW_GROUP_HEAD[g]], col)
    return col


def _low_lanes(shape):
    return lax.broadcasted_iota(jnp.int32, shape, 1) < HEAD_DIM


def _kv_pair_operand(prev, cur, kh):
    c = 128 * (kh // 2)
    col = jnp.concatenate([prev[:, c:c + 128], cur[:, c:c + 128]], axis=0).astype(F32)
    if kh % 2 == 0:
        lo = jnp.where(_low_lanes(col.shape), col, 0.0)
        hi = pltpu.roll(lo, HEAD_DIM, 1)
    else:
        hi = jnp.where(_low_lanes(col.shape), 0.0, col)
        lo = pltpu.roll(hi, HEAD_DIM, 1)
    return jnp.concatenate([lo, hi], axis=0).astype(BF16)


def _pair_rows(x, kh):
    c = 2 * 128 * kh
    return jnp.concatenate([x[:, c:c + 128], x[:, c + 128:c + 256]], axis=0)


def _restack(big):
    return jnp.concatenate([big[:, 0:2 * BLOCK], big[:, 2 * BLOCK:4 * BLOCK]], axis=0)


def _unrestack(stacked):
    return jnp.concatenate([stacked[0:2 * BLOCK], stacked[2 * BLOCK:4 * BLOCK]], axis=1)


def _fold_pair(x2, kh):
    low = _low_lanes((2 * BLOCK, 128))
    mixed = jnp.where(low, x2[0:2 * BLOCK], x2[2 * BLOCK:4 * BLOCK])
    total = mixed + pltpu.roll(mixed, HEAD_DIM, 1)
    return jnp.where(low, total, 0.0) if kh % 2 == 0 else jnp.where(low, 0.0, total)


def _attn_scores(qr, k2, kh):
    q2 = _pair_rows(qr, kh).astype(BF16)
    return q2, _restack(_dot_nt(q2, k2))


def _attn_softmax(s, sink_col, mask):
    s = jnp.where(mask, s, -1e30)
    m = jnp.maximum(jnp.max(s, axis=-1, keepdims=True), sink_col)
    p = jnp.exp(s - m)
    p_sink = jnp.exp(sink_col - m)
    denom = jnp.sum(p, axis=-1, keepdims=True) + p_sink
    return p / denom, p_sink / denom


def _attn_forward(proj, tabs, sinks):
    T = proj.shape[0]
    nb = T // BLOCK

    def body(q_ref, kvc_ref, kvp_ref, g0_ref, g1_ref, cc, sac, sbc, cp_, sap, sbp, sink_ref, y_ref, qrb_ref, krb_ref):
        n = pl.program_id(0)
        tc = (_wide(cc[...], D_MODEL), _wide(sac[...], D_MODEL), _wide(sbc[...], D_MODEL))
        tcur = tuple(t[:, :KV_W] for t in tc)
        tprev = (_wide(cp_[...], KV_W), _wide(sap[...], KV_W), _wide(sbp[...], KV_W))
        qr = _rope(q_ref[...], *tc) * ATTN_SCALE
        kr_cur = _rope(kvc_ref[:, 0:KV_W], *tcur)
        kr_prev = _rope(kvp_ref[:, 0:KV_W], *tprev)
        qrb_ref[...] = qr.astype(BF16)
        krb_ref[...] = kr_cur.astype(BF16)
        v_cur, v_prev = kvc_ref[:, KV_W:2 * KV_W], kvp_ref[:, KV_W:2 * KV_W]
        mask = _attn_mask(n)
        outs = []
        k2s = [_kv_pair_operand(kr_prev, kr_cur, kh) for kh in range(N_KV)]
        v2s = [_kv_pair_operand(v_prev, v_cur, kh) for kh in range(N_KV)]
        scores = [_attn_scores(qr, k2s[kh], kh) for kh in range(N_KV)]
        for kh in range(N_KV):
            pn, _ = _attn_softmax(scores[kh][1], _sink_col(sink_ref, kh), mask)
            o_big = _dot(_unrestack(pn.astype(BF16)), v2s[kh])
            outs += [o_big[0:BLOCK], o_big[BLOCK:2 * BLOCK]]
        o = jnp.concatenate(outs, axis=1)
        g = jnp.concatenate([g0_ref[...], g1_ref[...]], axis=1)
        y_ref[...] = (o * (g * _sigmoid(g))).astype(BF16)

    def blk(w, cb):
        return pl.BlockSpec((BLOCK, w), lambda n, cb=cb: (n, cb))

    prev = lambda w, cb: pl.BlockSpec((BLOCK, w), lambda n, cb=cb: (jnp.maximum(n - 1, 0), cb))
    return pl.pallas_call(
        body, grid=(nb,), name="attn_forward",
        out_shape=(jax.ShapeDtypeStruct((T, D_MODEL), BF16), jax.ShapeDtypeStruct((T, D_MODEL), BF16),
                   jax.ShapeDtypeStruct((T, KV_W), BF16)),
        in_specs=[blk(D_MODEL, 0), blk(CB, CB_KV), prev(CB, CB_KV), blk(CB, CB_GA), blk(CB, CB_GA + 1),
                  blk(128, 0), blk(128, 0), blk(128, 0), prev(128, 0), prev(128, 0), prev(128, 0),
                  pl.BlockSpec(memory_space=pltpu.SMEM)],
        out_specs=(blk(D_MODEL, 0), blk(D_MODEL, 0), blk(KV_W, 0)),
        compiler_params=_cp("parallel"),
    )(proj, proj, proj, proj, proj, *tabs, *tabs, sinks)


def _scan_rows8():
    return lax.broadcasted_iota(jnp.int32, (8, D_MODEL), 0)


def _scan_forward(a_ref, b_ref, h_ref, carry, rows):
    row = _scan_rows8()

    def group(i, carry):
        off = pl.multiple_of(i * 8, 8)
        a, b = a_ref[pl.ds(off, 8), :], b_ref[pl.ds(off, 8), :]
        for d in (1, 2, 4):
            ok = row >= d
            b = jnp.where(ok, a * pltpu.roll(b, d, 0) + b, b)
            a = jnp.where(ok, a * pltpu.roll(a, d, 0), a)
        h = a * carry + b
        h_ref[pl.ds(off, 8), :] = h
        return h[7:8, :]

    return lax.fori_loop(0, rows // 8, group, carry)


def _scan_backward(a_ref, g_ref, lam_ref, carry, rows):
    row = _scan_rows8()

    def group(i, carry):
        off = pl.multiple_of((rows // 8 - 1 - i) * 8, 8)
        a, g = a_ref[pl.ds(off, 8), :], g_ref[pl.ds(off, 8), :]
        b = a * g
        for d in (1, 2, 4):
            ok = row < 8 - d
            b = jnp.where(ok, a * pltpu.roll(b, 8 - d, 0) + b, b)
            a = jnp.where(ok, a * pltpu.roll(a, 8 - d, 0), a)
        mu = a * carry + b
        mu_below = jnp.where(row == 7, carry, pltpu.roll(mu, 7, 0))
        lam_ref[pl.ds(off, 8), :] = g + mu_below
        return mu[0:1, :]

    return lax.fori_loop(0, rows // 8, group, carry)


def _conv_taps(xbuf, xr, tail):
    rows = xr.shape[0]
    xbuf[0:8, :] = tail
    xbuf[8:rows + 8, :] = xr
    return [xbuf[pl.ds(8 - (CONV_W - 1 - k), rows), :] for k in range(CONV_W - 1)] + [xr]


def _rnn_gates(xbuf, xr, tail, cw, cb, wa_ref, wx_ref, ba, bx, sp, reset):
    xs = _conv_taps(xbuf, xr, tail)
    xc = xs[0] * cw[0:1, :]
    for k in range(1, CONV_W):
        xc = xc + xs[k] * cw[k:k + 1, :]
    xc = xc + cb
    xcb = xc.astype(BF16)
    za = jnp.concatenate([_dot(xcb[:, RNN_BW * j:RNN_BW * (j + 1)], wa_ref[j]) for j in range(RNN_BLOCKS)], axis=1) + ba
    zx = jnp.concatenate([_dot(xcb[:, RNN_BW * j:RNN_BW * (j + 1)], wx_ref[j]) for j in range(RNN_BLOCKS)], axis=1) + bx
    r, i = _sigmoid(za), _sigmoid(zx)
    neg_log_a = LRU_C * r * sp
    a_raw = jnp.exp(-neg_log_a)
    mult_raw = jnp.sqrt(jnp.tanh(neg_log_a) * (1.0 + a_raw * a_raw))
    a = jnp.where(reset, 0.0, a_raw)
    mult = jnp.where(reset, 1.0, mult_raw)
    return xc, r, i, a, mult


def _rnn_forward(proj, pos_col, conv_w, conv_b, rwa, rwx, ba, bx, lam):
    T = proj.shape[0]
    tr = min(T, 256)

    def body(x0, x1, g0, g1, pos_ref, cw_ref, cb_ref, wa_ref, wx_ref, ba_ref, bx_ref, lam_ref,
             y_ref, h_ref, xc_ref, r_ref, i_ref, a_ref, mult_ref, xbuf, bbuf, tail, carry):
        t = pl.program_id(0)

        @pl.when(t == 0)
        def _():
            tail[...] = jnp.zeros_like(tail)
            carry[...] = jnp.zeros_like(carry)

        xr = jnp.concatenate([x0[...], x1[...]], axis=1)
        sp = _softplus(-lam_ref[...])
        reset = pos_ref[...] == 0
        xc, r, i, a, mult = _rnn_gates(
            xbuf, xr, tail[...], cw_ref[...], cb_ref[...], wa_ref, wx_ref, ba_ref[...], bx_ref[...], sp, reset)
        xc_ref[...] = xc
        r_ref[...] = r
        i_ref[...] = i
        a_ref[...] = a
        mult_ref[...] = mult
        bbuf[...] = mult * (i * xc)
        last = _scan_forward(a_ref, bbuf, h_ref, carry[0:1, :], tr)
        carry[...] = jnp.broadcast_to(last, carry.shape)
        tail[...] = xr[tr - 8:tr, :]
        g = jnp.concatenate([g0[...], g1[...]], axis=1)
        y_ref[...] = (h_ref[...] * (g * _sigmoid(g))).astype(BF16)

    blk = lambda cb: pl.BlockSpec((tr, CB), lambda t, cb=cb: (t, cb))
    row = lambda w: pl.BlockSpec((1, w), lambda t: (0, 0))
    full3 = pl.BlockSpec((RNN_BLOCKS, RNN_BW, RNN_BW), lambda t: (0, 0, 0))
    tok = pl.BlockSpec((tr, D_MODEL), lambda t: (t, 0))
    act = jax.ShapeDtypeStruct((T, D_MODEL), F32)
    return pl.pallas_call(
        body, out_shape=(jax.ShapeDtypeStruct((T, D_MODEL), BF16),) + (act,) * 6,
        grid=(T // tr,), name="rnn_forward",
        in_specs=[blk(CB_XR), blk(CB_XR + 1), blk(CB_GR), blk(CB_GR + 1), pl.BlockSpec((tr, 1), lambda t: (t, 0)),
                  pl.BlockSpec((CONV_W, D_MODEL), lambda t: (0, 0)), row(D_MODEL), full3, full3,
                  row(D_MODEL), row(D_MODEL), row(D_MODEL)],
        out_specs=(tok,) * 7,
        scratch_shapes=[pltpu.VMEM((tr + 8, D_MODEL), F32), pltpu.VMEM((tr, D_MODEL), F32),
                        pltpu.VMEM((8, D_MODEL), F32), pltpu.VMEM((8, D_MODEL), F32)],
        compiler_params=_cp("arbitrary"),
    )(proj, proj, proj, proj, pos_col, conv_w, conv_b, rwa, rwx, ba, bx, lam)


def _merge_and_head(x, target, y_attn, y_rnn, proj, wap, wrp, wo, mod_row, final_g):
    T = x.shape[0]
    tm = min(T, 256)

    def body(x_ref, t_ref, ya_ref, yr_ref, ma0, ma1, mr0, mr1, wap_ref, wrp_ref, wo_ref, mod_ref, fg_ref,
             dx2_ref, mg_ref, do_ref, dpa_ref, dpr_ref, dya_ref, dyr_ref, dc_ref, dfg_ref, dgate_ref, loss_ref):
        i = pl.program_id(0)
        gate = mod_ref[:, 2 * D_MODEL:3 * D_MODEL]
        ya, yr = ya_ref[...], yr_ref[...]
        pa, pr = _dot(ya, wap_ref[...]), _dot(yr, wrp_ref[...])
        sa = _sigmoid(jnp.concatenate([ma0[...], ma1[...]], axis=1))
        sr = _sigmoid(jnp.concatenate([mr0[...], mr1[...]], axis=1))
        merged = sa * pa + sr * pr
        mb = merged.astype(BF16)
        o = _dot(mb, wo_ref[...])
        x2 = x_ref[...] + gate * o
        r2 = _rms(x2)
        xn2 = x2 * r2
        fg = fg_ref[...]
        err = xn2 * fg - t_ref[...]
        loss_t = 0.5 * jnp.sum(jnp.sum(err * err, axis=-1, keepdims=True) * (1.0 / D_MODEL), axis=0, keepdims=True)
        dy = err * (1.0 / D_MODEL)
        dfg_t = jnp.sum(dy * xn2, axis=0, keepdims=True)
        dxn = dy * fg
        dx2 = r2 * (dxn - xn2 * jnp.mean(dxn * xn2, axis=-1, keepdims=True))
        dgate_t = jnp.sum(dx2 * o, axis=0, keepdims=True)
        dob = (dx2 * gate).astype(BF16)
        dmerged = _dot_nt(dob, wo_ref[...])
        dpa = (dmerged * sa).astype(BF16)
        dpr = (dmerged * sr).astype(BF16)
        dx2_ref[...] = dx2
        mg_ref[...] = mb
        do_ref[...] = dob
        dpa_ref[...] = dpa
        dpr_ref[...] = dpr
        dya_ref[...] = _dot_nt(dpa, wap_ref[...])
        dyr_ref[...] = _dot_nt(dpr, wrp_ref[...])
        dc_ref[:, 0:D_MODEL] = (dmerged * pa * sa * (1.0 - sa)).astype(BF16)
        dc_ref[:, D_MODEL:2 * D_MODEL] = (dmerged * pr * sr * (1.0 - sr)).astype(BF16)

        @pl.when(i == 0)
        def _():
            dfg_ref[...] = jnp.zeros_like(dfg_ref)
            dgate_ref[...] = jnp.zeros_like(dgate_ref)
            loss_ref[...] = jnp.zeros_like(loss_ref)

        dfg_ref[...] += dfg_t
        dgate_ref[...] += dgate_t
        loss_ref[...] += jnp.broadcast_to(loss_t, loss_ref.shape)

    tok = lambda w: pl.BlockSpec((tm, w), lambda i: (i, 0))
    blk = lambda cb: pl.BlockSpec((tm, CB), lambda i, cb=cb: (i, cb))
    wfull = pl.BlockSpec((D_MODEL, D_MODEL), lambda i: (0, 0))
    row = lambda w: pl.BlockSpec((1, w), lambda i: (0, 0))
    out_shape = (
        jax.ShapeDtypeStruct((T, D_MODEL), F32), jax.ShapeDtypeStruct((T, D_MODEL), BF16),
        jax.ShapeDtypeStruct((T, D_MODEL), BF16), jax.ShapeDtypeStruct((T, D_MODEL), BF16),
        jax.ShapeDtypeStruct((T, D_MODEL), BF16), jax.ShapeDtypeStruct((T, D_MODEL), F32),
        jax.ShapeDtypeStruct((T, D_MODEL), F32), jax.ShapeDtypeStruct((T, 2 * D_MODEL), BF16),
        jax.ShapeDtypeStruct((1, D_MODEL), F32), jax.ShapeDtypeStruct((1, D_MODEL), F32),
        jax.ShapeDtypeStruct((1, 128), F32),
    )
    return pl.pallas_call(
        body, out_shape=out_shape, grid=(T // tm,), name="merge_and_head",
        in_specs=[tok(D_MODEL), tok(D_MODEL), tok(D_MODEL), tok(D_MODEL), blk(CB_MA), blk(CB_MA + 1), blk(CB_MR),
                  blk(CB_MR + 1), wfull, wfull, wfull, row(ADA_W), row(D_MODEL)],
        out_specs=(tok(D_MODEL),) * 7 + (tok(2 * D_MODEL), row(D_MODEL), row(D_MODEL), row(128)),
        compiler_params=_cp("arbitrary"),
    )(x, target, y_attn, y_rnn, proj, proj, proj, proj, wap, wrp, wo, mod_row, final_g)


def _attn_backward(proj, qr_b, kr_b, d_y, tabs, sinks):
    T = proj.shape[0]
    nb = T // BLOCK

    def body(qrb_ref, krc_ref, krp_ref, vc_ref, vp_ref, g0_ref, g1_ref, dy_ref, cc, sac, sbc, cp_, sap, sbp, sink_ref,
             dq_ref, dkv_ref, dg_ref, dsink_ref, carry):
        n = pl.program_id(0)

        @pl.when(n == 0)
        def _():
            carry[...] = jnp.zeros_like(carry)
            dsink_ref[...] = jnp.zeros_like(dsink_ref)

        @pl.when(n < nb)
        def _():
            tc = (_wide(cc[...], D_MODEL), _wide(sac[...], D_MODEL), _wide(sbc[...], D_MODEL))
            tcur = tuple(t[:, :KV_W] for t in tc)
            tprev = (_wide(cp_[...], KV_W), _wide(sap[...], KV_W), _wide(sbp[...], KV_W))
            qr, kr_cur, kr_prev = qrb_ref[...], krc_ref[...], krp_ref[...]
            v_cur, v_prev = vc_ref[...], vp_ref[...]
            g = jnp.concatenate([g0_ref[...], g1_ref[...]], axis=1)
            sg = _sigmoid(g)
            dy = dy_ref[...]
            d_o = dy * (g * sg)
            mask = _attn_mask(n)
            lane = lax.broadcasted_iota(jnp.int32, (1, 128), 1)
            rowg = lax.broadcasted_iota(jnp.int32, (GROUP * BLOCK, 1), 0) // BLOCK
            o_parts, dq_parts = [], []
            dk_cols, dv_cols = [None, None], [None, None]
            dsink = jnp.zeros((1, 128), F32)
            for heads in ((0, 1, 2, 3),):
                k2s = {kh: _kv_pair_operand(kr_prev, kr_cur, kh) for kh in heads}
                v2s = {kh: _kv_pair_operand(v_prev, v_cur, kh) for kh in heads}
                scores = {kh: _attn_scores(qr, k2s[kh], kh) for kh in heads}
                do2s = {kh: _pair_rows(d_o, kh).astype(BF16) for kh in heads}
                dpns = {kh: _restack(_dot_nt(do2s[kh], v2s[kh])) for kh in heads}
                probs = {kh: _attn_softmax(scores[kh][1], _sink_col(sink_ref, kh), mask) for kh in heads}
                p_bigs = {kh: _unrestack(probs[kh][0].astype(BF16)) for kh in heads}
                o_bigs = {kh: _dot(p_bigs[kh], v2s[kh]) for kh in heads}
                dv2s = {kh: _dot_tn(p_bigs[kh], do2s[kh]) for kh in heads}
                deltas = {kh: jnp.sum(probs[kh][0] * dpns[kh], axis=-1, keepdims=True) for kh in heads}
                ds_bigs = {kh: _unrestack((probs[kh][0] * (dpns[kh] - deltas[kh])).astype(BF16)) for kh in heads}
                dq2s = {kh: _dot(ds_bigs[kh], k2s[kh]) for kh in heads}
                dk2s = {kh: _dot_tn(ds_bigs[kh], scores[kh][0]) for kh in heads}
                for kh in heads:
                    o_parts += [o_bigs[kh][0:BLOCK], o_bigs[kh][BLOCK:2 * BLOCK]]
                    dq_parts += [dq2s[kh][0:BLOCK], dq2s[kh][BLOCK:2 * BLOCK]]
                    dk_c, dv_c = _fold_pair(dk2s[kh], kh), _fold_pair(dv2s[kh], kh)
                    c = kh // 2
                    dk_cols[c] = dk_c if dk_cols[c] is None else dk_cols[c] + dk_c
                    dv_cols[c] = dv_c if dv_cols[c] is None else dv_cols[c] + dv_c
                    ds_rows = probs[kh][1] * deltas[kh]
                    for gq in range(GROUP):
                        val = -jnp.sum(jnp.where(rowg == gq, ds_rows, 0.0), axis=0, keepdims=True)
                        dsink = dsink + jnp.where(lane == GROUP * kh + ROW_GROUP_HEAD[gq], val, 0.0)
            o = jnp.concatenate(o_parts, axis=1)
            dg_ref[...] = (dy * o * (sg * (1.0 + g * (1.0 - sg)))).astype(BF16)
            dq_ref[...] = (_unrope(jnp.concatenate(dq_parts, axis=1), *tc) * ATTN_SCALE).astype(BF16)
            dk_all, dv_all = jnp.concatenate(dk_cols, axis=1), jnp.concatenate(dv_cols, axis=1)
            dk_prev = _unrope(dk_all[0:BLOCK], *tprev)
            dk_cur = _unrope(dk_all[BLOCK:2 * BLOCK], *tcur)
            dv_prev, dv_cur = dv_all[0:BLOCK], dv_all[BLOCK:2 * BLOCK]
            dkv_ref[...] = (carry[...] + jnp.concatenate([dk_prev, dv_prev], axis=1)).astype(BF16)
            carry[...] = jnp.concatenate([dk_cur, dv_cur], axis=1)
            dsink_ref[...] += dsink

        @pl.when(n == nb)
        def _():
            dkv_ref[...] = carry[...].astype(BF16)

    cur = lambda w, cb: pl.BlockSpec((BLOCK, w), lambda n, cb=cb: (jnp.minimum(n, nb - 1), cb))
    prev = lambda w, cb: pl.BlockSpec((BLOCK, w), lambda n, cb=cb: (jnp.maximum(jnp.minimum(n, nb - 1) - 1, 0), cb))
    out_shape = (jax.ShapeDtypeStruct((T, D_MODEL), BF16), jax.ShapeDtypeStruct((T, 2 * KV_W), BF16),
                 jax.ShapeDtypeStruct((T, D_MODEL), BF16), jax.ShapeDtypeStruct((1, 128), F32))
    return pl.pallas_call(
        body, out_shape=out_shape, grid=(nb + 1,), name="attn_backward",
        in_specs=[cur(D_MODEL, 0), cur(KV_W, 0), prev(KV_W, 0), cur(KV_W, V_COL_BLOCK), prev(KV_W, V_COL_BLOCK),
                  cur(CB, CB_GA), cur(CB, CB_GA + 1), cur(D_MODEL, 0),
                  cur(128, 0), cur(128, 0), cur(128, 0), prev(128, 0), prev(128, 0), prev(128, 0),
                  pl.BlockSpec(memory_space=pltpu.SMEM)],
        out_specs=(cur(D_MODEL, 0), pl.BlockSpec((BLOCK, 2 * KV_W), lambda n: (jnp.maximum(n - 1, 0), 0)),
                   cur(D_MODEL, 0), pl.BlockSpec((1, 128), lambda n: (0, 0))),
        scratch_shapes=[pltpu.VMEM((BLOCK, 2 * KV_W), F32)],
        compiler_params=_cp("arbitrary"),
    )(qr_b, kr_b, kr_b, proj, proj, proj, proj, d_y, *tabs, *tabs, sinks)


def _rnn_backward(proj, pos_col, h_rnn, saved, d_y, conv_w, rwa, rwx, lam):
    T = proj.shape[0]
    tr = min(T, 256)
    nt = T // tr
    hb = tr // 8

    def body(x0, x1, xh0, xh1, g0, g1, pos_ref, h_ref, hh_ref, xc_ref, r_ref, i_ref, a_ref, mult_ref, dy_ref,
             cw_ref, wa_ref, wx_ref, lam_ref, db_ref, dcw_ref, dcb_ref, dwa_ref, dwx_ref, dba_ref, dbx_ref, dlam_ref,
             xbuf, hbuf, dbuf, gbuf, lbuf, mu_carry, dxc_head):
        step = pl.program_id(0)
        first_tile = step == nt - 1

        @pl.when(step == 0)
        def _():
            mu_carry[...] = jnp.zeros_like(mu_carry)
            dxc_head[...] = jnp.zeros_like(dxc_head)
            for ref in (dcw_ref, dcb_ref, dwa_ref, dwx_ref, dba_ref, dbx_ref, dlam_ref):
                ref[...] = jnp.zeros_like(ref)

        xr = jnp.concatenate([x0[...], x1[...]], axis=1)
        tail = jnp.where(first_tile, 0.0, jnp.concatenate([xh0[...], xh1[...]], axis=1))
        lam_v = lam_ref[...]
        sp = _softplus(-lam_v)
        reset = pos_ref[...] == 0
        cw = cw_ref[...]
        xs = _conv_taps(xbuf, xr, tail)
        xc, r, i, a, mult = xc_ref[...], r_ref[...], i_ref[...], a_ref[...], mult_ref[...]
        xcb = xc.astype(BF16)
        g = jnp.concatenate([g0[...], g1[...]], axis=1)
        sg = _sigmoid(g)
        dy = dy_ref[...]
        h = h_ref[...]
        d_g = dy * h * (sg * (1.0 + g * (1.0 - sg)))
        gbuf[...] = dy * (g * sg)
        top = _scan_backward(a_ref, gbuf, lbuf, mu_carry[0:1, :], tr)
        mu_carry[...] = jnp.broadcast_to(top, mu_carry.shape)
        lam_t = lbuf[...]
        hbuf[0:8, :] = jnp.where(first_tile, 0.0, hh_ref[...])
        hbuf[8:tr + 8, :] = h
        h_prev = hbuf[pl.ds(7, tr), :]
        live = jnp.logical_not(reset)
        d_a = jnp.where(live, lam_t * h_prev, 0.0)
        d_mult = jnp.where(live, lam_t * (i * xc), 0.0)
        d_ixc = lam_t * mult
        d_i = d_ixc * xc
        d_xc = d_ixc * i
        d_log_a = d_a * a - d_mult * (a * a / mult)
        d_za = d_log_a * (-LRU_C * sp) * (r * (1.0 - r))
        d_zx = d_i * (i * (1.0 - i))
        dlam_ref[...] += jnp.sum(d_log_a * r, axis=0, keepdims=True) * (LRU_C * _sigmoid(-lam_v))
        dba_ref[...] += jnp.sum(d_za, axis=0, keepdims=True)
        dbx_ref[...] += jnp.sum(d_zx, axis=0, keepdims=True)
        dzab, dzxb = d_za.astype(BF16), d_zx.astype(BF16)
        back = []
        for j in range(RNN_BLOCKS):
            sl = slice(RNN_BW * j, RNN_BW * (j + 1))
            dwa_ref[j] += _dot_tn(xcb[:, sl], dzab[:, sl])
            dwx_ref[j] += _dot_tn(xcb[:, sl], dzxb[:, sl])
            back.append(_dot_nt(dzab[:, sl], wa_ref[j]) + _dot_nt(dzxb[:, sl], wx_ref[j]))
        d_xc = d_xc + jnp.concatenate(back, axis=1)
        dcb_ref[...] += jnp.sum(d_xc, axis=0, keepdims=True)
        for k in range(CONV_W):
            dcw_ref[k:k + 1, :] += jnp.sum(d_xc * xs[k], axis=0, keepdims=True)
        dbuf[0:tr, :] = d_xc
        dbuf[tr:tr + 8, :] = dxc_head[...]
        d_xr = d_xc * cw[CONV_W - 1:CONV_W, :]
        for k in range(CONV_W - 1):
            d_xr = d_xr + dbuf[pl.ds(CONV_W - 1 - k, tr), :] * cw[k:k + 1, :]
        dxc_head[...] = d_xc[0:8, :]
        db_ref[:, 0:D_MODEL] = d_xr.astype(BF16)
        db_ref[:, D_MODEL:2 * D_MODEL] = d_g.astype(BF16)

    rev = lambda s: nt - 1 - s
    blk = lambda cb: pl.BlockSpec((tr, CB), lambda s, cb=cb: (rev(s), cb))
    halo = lambda w, cb: pl.BlockSpec((8, w), lambda s, cb=cb: (jnp.maximum(rev(s) * hb - 1, 0), cb))
    tok = lambda w: pl.BlockSpec((tr, w), lambda s: (rev(s), 0))
    row = lambda w: pl.BlockSpec((1, w), lambda s: (0, 0))
    full3 = pl.BlockSpec((RNN_BLOCKS, RNN_BW, RNN_BW), lambda s: (0, 0, 0))
    cwspec = pl.BlockSpec((CONV_W, D_MODEL), lambda s: (0, 0))
    vec = jax.ShapeDtypeStruct((1, D_MODEL), F32)
    gate_w = jax.ShapeDtypeStruct((RNN_BLOCKS, RNN_BW, RNN_BW), F32)
    out_shape = (jax.ShapeDtypeStruct((T, 2 * D_MODEL), BF16), jax.ShapeDtypeStruct((CONV_W, D_MODEL), F32), vec,
                 gate_w, gate_w, vec, vec, vec)
    big = lambda: pltpu.VMEM((tr, D_MODEL), F32)
    ext = lambda: pltpu.VMEM((tr + 8, D_MODEL), F32)
    return pl.pallas_call(
        body, out_shape=out_shape, grid=(nt,), name="rnn_backward",
        in_specs=[blk(CB_XR), blk(CB_XR + 1), halo(CB, CB_XR), halo(CB, CB_XR + 1), blk(CB_GR), blk(CB_GR + 1),
                  pl.BlockSpec((tr, 1), lambda s: (rev(s), 0)), tok(D_MODEL), halo(D_MODEL, 0)] + [tok(D_MODEL)] * 6
        + [cwspec, full3, full3, row(D_MODEL)],
        out_specs=(tok(2 * D_MODEL), cwspec, row(D_MODEL), full3, full3, row(D_MODEL), row(D_MODEL), row(D_MODEL)),
        scratch_shapes=[ext(), ext(), ext(), big(), big(), pltpu.VMEM((8, D_MODEL), F32), pltpu.VMEM((8, D_MODEL), F32)],
        compiler_params=_cp("arbitrary"),
    )(proj, proj, proj, proj, proj, proj, pos_col, h_rnn, h_rnn, *saved, d_y, conv_w, rwa, rwx, lam)


def _input_backward(pieces, w_in, x, dx2, mod_row, norm_g):
    T = x.shape[0]
    tm = min(T, 512)
    n = len(pieces)

    def body(*refs):
        d_refs = refs[:n]
        w_ref, x_ref, dx2_ref, mod_ref, g_ref, gx_ref, dshift_ref, dscale_ref, dg_ref = refs[n:]
        i = pl.program_id(0)
        dh = None
        for d_ref, (_, start, count) in zip(d_refs, pieces):
            part = _dot_nt(d_ref[...], w_ref[:, start * CB:(start + count) * CB])
            dh = part if dh is None else dh + part

        @pl.when(i == 0)
        def _():
            dshift_ref[...] = jnp.zeros_like(dshift_ref)
            dscale_ref[...] = jnp.zeros_like(dscale_ref)
            dg_ref[...] = jnp.zeros_like(dg_ref)

        xf = x_ref[...]
        r1 = _rms(xf)
        xn = xf * r1
        gn = g_ref[...]
        s1 = 1.0 + mod_ref[:, D_MODEL:2 * D_MODEL]
        dshift_ref[...] += jnp.sum(dh, axis=0, keepdims=True)
        dscale_ref[...] += jnp.sum(dh * (xn * gn), axis=0, keepdims=True)
        dg_ref[...] += jnp.sum(dh * s1 * xn, axis=0, keepdims=True)
        dxn = dh * s1 * gn
        gx_ref[...] = dx2_ref[...] + r1 * (dxn - xn * jnp.mean(dxn * xn, axis=-1, keepdims=True))

    tok = lambda w: pl.BlockSpec((tm, w), lambda i: (i, 0))
    row = lambda w: pl.BlockSpec((1, w), lambda i: (0, 0))
    vec = jax.ShapeDtypeStruct((1, D_MODEL), F32)
    return pl.pallas_call(
        body, out_shape=(jax.ShapeDtypeStruct((T, D_MODEL), F32), vec, vec, vec), grid=(T // tm,), name="input_backward",
        in_specs=[tok(c * CB) for _, _, c in pieces]
        + [pl.BlockSpec((D_MODEL, IN_W), lambda i: (0, 0), pipeline_mode=pl.Buffered(1)), tok(D_MODEL), tok(D_MODEL),
           row(ADA_W), row(D_MODEL)],
        out_specs=(tok(D_MODEL), row(D_MODEL), row(D_MODEL), row(D_MODEL)),
        compiler_params=_cp("arbitrary"),
    )(*[p[0] for p in pieces], w_in, x, dx2, mod_row, norm_g)


def _weight_grad(a, pieces, tag):
    T, M = a.shape
    n_blocks = sum(count for _, _, count in pieces)
    n = len(pieces)

    def body(*refs):
        a_ref, b_refs, o_ref = refs[0], refs[1:1 + n], refs[-1]
        j = pl.program_id(0)
        for b_ref, (_, start, count) in zip(b_refs, pieces):
            @pl.when((j >= start) & (j < start + count))
            def _(b_ref=b_ref):
                o_ref[...] = _dot_tn(a_ref[...], b_ref[...])

    def piece_spec(start, count):
        return pl.BlockSpec((T, CB), lambda j: (0, jnp.clip(j - start, 0, count - 1)))

    return pl.pallas_call(
        body, out_shape=jax.ShapeDtypeStruct((M, n_blocks * CB), F32), grid=(n_blocks,), name=f"weight_grad_{tag}",
        in_specs=[pl.BlockSpec((T, M), lambda j: (0, 0), pipeline_mode=pl.Buffered(1))] + [piece_spec(s, c) for _, s, c in pieces],
        out_specs=pl.BlockSpec((M, CB), lambda j: (0, j)), compiler_params=_cp("arbitrary"),
    )(a, *[p[0] for p in pieces])


def _adamw(w, g, m, v):
    m = ADAM_B1 * m + (1.0 - ADAM_B1) * g
    v = ADAM_B2 * v + (1.0 - ADAM_B2) * (g * g)
    m_hat = m / (1.0 - ADAM_B1 ** ADAM_STEP)
    v_hat = v / (1.0 - ADAM_B2 ** ADAM_STEP)
    delta = -ADAM_LR * (m_hat / (jnp.sqrt(v_hat) + ADAM_EPS) + ADAM_WD * w)
    return delta, m, v


def _sum_landed(kind, own, land, where, tag):
    if kind == "in":
        R, C = land.shape[1:]
        tr = 256
        grid = (R // tr,)
        own_spec = pl.BlockSpec((tr, C), lambda i, w: (i, w[0]))
        land_spec = pl.BlockSpec((3, tr, C), lambda i, w: (0, i, 0))
        out_spec = pl.BlockSpec((1, tr, C), lambda i, w: (w[1], i, 0))
        out_shape = (2, R, C)
        pick = lambda ref: ref[...]
    elif kind == "sq":
        R, C = land.shape[1:]
        grid = (1,)
        own_spec = pl.BlockSpec((1, R, C), lambda i, w: (w[0], 0, 0))
        land_spec = pl.BlockSpec((3, R, C), lambda i, w: (0, 0, 0))
        out_spec = pl.BlockSpec((1, R, C), lambda i, w: (w[1], 0, 0))
        out_shape = (2, R, C)
        pick = lambda ref: ref[0]
    else:
        B, R, C = land.shape[1:]
        grid = (1,)
        own_spec = pl.BlockSpec((B, 1, R, C), lambda i, w: (0, w[0], 0, 0))
        land_spec = pl.BlockSpec((3, B, R, C), lambda i, w: (0, 0, 0, 0))
        out_spec = pl.BlockSpec((B, 1, R, C), lambda i, w: (0, w[1], 0, 0))
        out_shape = (B, 2, R, C)
        pick = lambda ref: ref[:, 0]

    def body(w_ref, own_ref, l_ref, o_ref):
        total = ((pick(own_ref) + l_ref[0].astype(F32)) + l_ref[1].astype(F32)) + l_ref[2].astype(F32)
        if kind == "in":
            o_ref[0] = total
        elif kind == "sq":
            o_ref[0] = total
        else:
            o_ref[:, 0] = total

    grid_spec = pltpu.PrefetchScalarGridSpec(num_scalar_prefetch=1, grid=grid, in_specs=[own_spec, land_spec], out_specs=out_spec)
    return pl.pallas_call(
        body, out_shape=jax.ShapeDtypeStruct(out_shape, F32), grid_spec=grid_spec, name=f"sum_landed_{tag}",
        compiler_params=_cp("parallel"),
    )(where, own, land)


def _adamw_shard(g, w, m, v, tag):
    R, C = w.shape
    tr = min(R, 256)

    def body(g_ref, w_ref, m_ref, v_ref, d_ref, nm_ref, nv_ref):
        d, nm, nv = _adamw(w_ref[...], g_ref[...], m_ref[...], v_ref[...])
        d_ref[...] = d
        nm_ref[...] = nm
        nv_ref[...] = nv

    spec = pl.BlockSpec((tr, C), lambda i: (i, 0))
    sds = jax.ShapeDtypeStruct((R, C), F32)
    return pl.pallas_call(
        body, out_shape=(sds,) * 3, grid=(R // tr,), name=f"adamw_{tag}",
        in_specs=[spec] * 4, out_specs=(spec,) * 3, compiler_params=_cp("parallel"),
    )(g, w, m, v)


def _adamw_w_ada(c_t, dmod_cols, w, m, v):
    R, C = w.shape

    def body(ct_ref, dm_ref, w_ref, m_ref, v_ref, g_ref, d_ref, nm_ref, nv_ref):
        g = _dot(ct_ref[...].astype(BF16), dm_ref[...].astype(BF16))
        d, nm, nv = _adamw(w_ref[...], g, m_ref[...], v_ref[...])
        g_ref[...] = g
        d_ref[...] = d
        nm_ref[...] = nm
        nv_ref[...] = nv

    tr = 256
    spec = pl.BlockSpec((tr, C), lambda i: (i, 0))
    sds = jax.ShapeDtypeStruct((R, C), F32)
    return pl.pallas_call(
        body, out_shape=(sds,) * 4, grid=(R // tr,), name="adamw_w_ada",
        in_specs=[pl.BlockSpec((tr, 128), lambda i: (i, 0)), pl.BlockSpec((128, C), lambda i: (0, 0))] + [spec] * 3,
        out_specs=(spec,) * 4, compiler_params=_cp("parallel"),
    )(c_t, dmod_cols, w, m, v)


def _adamw_small(small_all, ws, ms, vs):
    def body(s_ref, w_ref, m_ref, v_ref, g_ref, d_ref, nm_ref, nv_ref):
        g = s_ref[0]
        for b in range(1, N_DEV):
            g = g + s_ref[b]
        d, nm, nv = _adamw(w_ref[...], g, m_ref[...], v_ref[...])
        g_ref[...] = g
        d_ref[...] = d
        nm_ref[...] = nm
        nv_ref[...] = nv

    sds = jax.ShapeDtypeStruct((SMALL_ROWS, D_MODEL), F32)
    return pl.pallas_call(
        body, out_shape=(sds,) * 4, name="adamw_small", in_specs=[VMEM_SPEC] * 4, out_specs=(VMEM_SPEC,) * 4,
        compiler_params=pltpu.CompilerParams(vmem_limit_bytes=VMEM_LIMIT_V7X),
    )(small_all, ws, ms, vs)


ROW_MOD, ROW_NORM_G, ROW_CONV_B, ROW_BA, ROW_BX, ROW_LAM, ROW_FINAL_G, ROW_SINKS, ROW_CONV_W, ROW_LOSS = 0, 3, 4, 5, 6, 7, 8, 9, 10, 14


def _pack_small(b_ada, norm_g, conv_b, ba, bx, lam, final_g, sinks, conv_w_full, loss_row=None):
    lane_pad = lambda a: jnp.pad(a.reshape(1, -1), ((0, 0), (0, D_MODEL - a.size)))
    rows = [b_ada.reshape(3, D_MODEL), norm_g, conv_b, ba, bx, lam, final_g.reshape(1, D_MODEL), lane_pad(sinks), conv_w_full,
            jnp.zeros((1, D_MODEL), F32) if loss_row is None else lane_pad(loss_row),
            jnp.zeros((SMALL_ROWS - ROW_LOSS - 1, D_MODEL), F32)]
    return jnp.concatenate([r.astype(F32) for r in rows], axis=0)


def kernel(x, c, positions, w_ada, b_ada, norm_g, w_in, attn_sinks, conv_w, conv_b, rg_wa, rg_ba, rg_wx, rg_bx, rg_lambda, w_attn_proj, w_rnn_proj, w_out, final_g, loss_target, m_w_ada, m_b_ada, m_norm_g, m_w_in, m_attn_sinks, m_conv_w, m_conv_b, m_rg_wa, m_rg_ba, m_rg_wx, m_rg_bx, m_rg_lambda, m_w_attn_proj, m_w_rnn_proj, m_w_out, m_final_g, v_w_ada, v_b_ada, v_norm_g, v_w_in, v_attn_sinks, v_conv_w, v_conv_b, v_rg_wa, v_rg_ba, v_rg_wx, v_rg_bx, v_rg_lambda, v_w_attn_proj, v_w_rnn_proj, v_w_out, v_final_g):
    T = x.shape[1]
    my_chip = lax.axis_index("x") * 2 + lax.axis_index("y")
    my_dev = my_chip * 2 + lax.axis_index("c")
    x2d, tgt = x[0], loss_target[0]
    pos_col = positions.reshape(T, 1)

    chip_idx = my_chip.reshape(1).astype(jnp.int32)
    c_idx = lax.axis_index("c").reshape(1).astype(jnp.int32)
    sq_place = ((D_MODEL, D_MODEL), (SHARD_ROWS, D_MODEL), lambda chip: (chip, 0))
    rg_place = ((RNN_BLOCKS, RNN_BW, RNN_BW), (RNN_BLOCKS, SHARD_RG, RNN_BW), lambda chip: (0, chip, 0))
    placed = [
        _cast_place(w_in[0], chip_idx, (D_MODEL, IN_W), (D_MODEL, SHARD_IN), lambda chip: (0, chip), "w_in"),
        _cast_place(w_attn_proj[0], chip_idx, *sq_place, "w_attn_proj"),
        _cast_place(w_rnn_proj[0], chip_idx, *sq_place, "w_rnn_proj"),
        _cast_place(w_out[0], chip_idx, *sq_place, "w_out"),
        _cast_place(rg_wa[0], chip_idx, *rg_place, "rg_wa"),
        _cast_place(rg_wx[0], chip_idx, *rg_place, "rg_wx"),
    ]
    cw_chips, c_all, mod_chips = _gather_mod(c.reshape(1, 1, D_MODEL), w_ada[0], conv_w[0])
    g_ssems, g_rsems, fulls, g_token = _gather_start([p.reshape(s) for p, s in zip(placed, FULL_SHAPES)], mod_chips)
    conv_w_f = jnp.transpose(cw_chips, (1, 0, 2)).reshape(CONV_W, D_MODEL)
    mod_all = jnp.transpose(mod_chips, (1, 0, 2)).reshape(N_DEV, ADA_W) + b_ada
    mod_row = lax.dynamic_slice_in_dim(mod_all, my_dev, 1, axis=0) + g_token[0:1, 0:1]

    tabs = _rope_tables(pos_col)
    h = _prenorm(x2d, mod_row, norm_g)
    w_in_v = fulls[0]
    proj = _in_projection(h, w_in_v.reshape(D_MODEL, IN_W), chip_idx, None, "own")
    for k, mask in enumerate(CHIP_MASKS):
        w_in_v = _gather_wait(g_ssems[k], g_rsems[k], [w_in_v], [0], proj, f"w_in_{k}")[0]
        w_in_v = _forward_halves([w_in_v], [(0, 0, k)], f"w_in_{k}")[0]
        from_chip = (chip_idx ^ (mask >> 1)).astype(jnp.int32)
        proj = _in_projection(h, w_in_v.reshape(D_MODEL, IN_W), from_chip, proj, f"from_{k}")
    w_in_f = w_in_v.reshape(D_MODEL, IN_W)
    rest = _gather_wait(g_ssems[3], g_rsems[3], list(fulls[1:]), [1, 2, 3, 4, 5], proj, "rest")
    rest = _forward_halves(rest, [(idx - 1, idx, k) for idx in range(1, N_BIG) for k in range(3)], "rest")
    wap_f, wrp_f, wo_f = (g.reshape(D_MODEL, D_MODEL) for g in rest[0:3])
    rwa_f, rwx_f = (g.reshape(RNN_BLOCKS, RNN_BW, RNN_BW) for g in rest[3:5])
    y_attn, qr_b, kr_b = _attn_forward(proj, tabs, attn_sinks)
    y_rnn, h_rnn, *rnn_saved = _rnn_forward(proj, pos_col, conv_w_f, conv_b, rwa_f, rwx_f, rg_ba, rg_bx, rg_lambda)
    (dx2, merged, d_o, d_pa, d_pr, d_ya, d_yr, d_c, d_final_g, d_gate, loss_vec) = _merge_and_head(
        x2d, tgt, y_attn, y_rnn, proj, wap_f, wrp_f, wo_f, mod_row, final_g.reshape(1, D_MODEL))

    sq = (N_CHIPS, 2, SHARD_ROWS // 2, D_MODEL)
    rg = (RNN_BLOCKS, N_CHIPS, 2, SHARD_RG // 2, RNN_BW)
    rg_flat = (RNN_BLOCKS * N_CHIPS, 2, SHARD_RG // 2, RNN_BW)

    def chip_sum_and_start(views, axes, flat, unflat, tags_, kinds_, group):
        from_sib = _swap_halves(views, axes)
        sums = [_presum(v.reshape(f), s.reshape(f[:1] + f[2:]), c_idx, t) for v, s, f, t in zip(views, from_sib, flat, tags_)]
        exact = [s[0].reshape(u) for s, u in zip(sums, unflat)]
        rounded = [s[1].reshape(u) for s, u in zip(sums, unflat)]
        return _exchange_start(rounded, kinds_, group), exact

    g_ap = _weight_grad(y_attn, [(d_pa, 0, 2)], "w_attn_proj")
    g_rp = _weight_grad(y_rnn, [(d_pr, 0, 2)], "w_rnn_proj")
    g_o = _weight_grad(merged, [(d_o, 0, 2)], "w_out")
    sq_half = (N_CHIPS, SHARD_ROWS // 2, D_MODEL)
    started1, own1 = chip_sum_and_start([g_ap.reshape(sq), g_rp.reshape(sq), g_o.reshape(sq)], [1, 1, 1], [sq] * 3, [sq_half] * 3,
                                  ["w_attn_proj", "w_rnn_proj", "w_out"], ["sq"] * 3, "proj")
    d_q, d_kv, d_ga, d_sinks = _attn_backward(proj, qr_b, kr_b, d_ya, tabs, attn_sinks + started1[4][0, 0])
    d_b, d_conv_w, d_conv_b, d_rwa, d_rwx, d_ba, d_bx, d_lam = _rnn_backward(
        proj, pos_col, h_rnn, rnn_saved, d_yr, conv_w_f, rwa_f, rwx_f, rg_lambda)
    pieces = [(d_q, CB_Q, 2), (d_kv, CB_KV, 1), (d_ga, CB_GA, 2), (d_b, CB_XR, 4), (d_c, CB_MA, 4)]
    g_in = _weight_grad(h, pieces, "w_in")
    started2, own2 = chip_sum_and_start(
        [g_in.reshape(2, D_MODEL // 2, IN_W), d_rwa.reshape(rg), d_rwx.reshape(rg)], [0, 2, 2],
        [(1, 2, D_MODEL // 2, IN_W), rg_flat, rg_flat],
        [(D_MODEL // 2, IN_W), (RNN_BLOCKS, N_CHIPS, SHARD_RG // 2, RNN_BW), (RNN_BLOCKS, N_CHIPS, SHARD_RG // 2, RNN_BW)],
        ["w_in", "rg_wa", "rg_wx"], ["in", "rg", "rg"], "in")
    grad_x, d_shift, d_scale, d_norm_g = _input_backward(pieces, w_in_f, x2d, dx2, mod_row + started2[4][0, 0], norm_g)

    d_mod = jnp.concatenate([d_shift, d_scale, d_gate], axis=1)
    small = _pack_small(d_mod, d_norm_g, d_conv_b, d_ba, d_bx, d_lam, d_final_g, d_sinks[:, :N_HEADS], d_conv_w, loss_vec)
    small_all = _gather_small(small)
    _, lands1 = _exchange_wait(*started1[:4], grad_x, "proj")
    _, lands2 = _exchange_wait(*started2[:4], grad_x, "in")
    tags = ["w_in", "w_attn_proj", "w_rnn_proj", "w_out", "rg_wa", "rg_wx"]
    chip_sums = [own2[0]] + list(own1) + list(own2[1:])
    lands = [lands2[0]] + list(lands1) + list(lands2[1:])
    where = jnp.concatenate([chip_idx, c_idx])
    kinds = ["in", "sq", "sq", "sq", "rg", "rg"]
    halves = [_sum_landed(kinds[i], chip_sums[i], lands[i], where, tags[i]) for i in range(6)]
    grads = _assemble_with_sibling(halves, [0, 0, 0, 0, 1, 1])
    shapes2d = [(D_MODEL, SHARD_IN), (SHARD_ROWS, D_MODEL), (SHARD_ROWS, D_MODEL), (SHARD_ROWS, D_MODEL),
                (RNN_BLOCKS * SHARD_RG, RNN_BW), (RNN_BLOCKS * SHARD_RG, RNN_BW)]
    big_w = [w_in, w_attn_proj, w_rnn_proj, w_out, rg_wa, rg_wx]
    big_m = [m_w_in, m_w_attn_proj, m_w_rnn_proj, m_w_out, m_rg_wa, m_rg_wx]
    big_v = [v_w_in, v_w_attn_proj, v_w_rnn_proj, v_w_out, v_rg_wa, v_rg_wx]
    res = {}
    for i, tag in enumerate(tags):
        g = grads[i].reshape(shapes2d[i])
        outs = _adamw_shard(g, big_w[i].reshape(shapes2d[i]), big_m[i].reshape(shapes2d[i]), big_v[i].reshape(shapes2d[i]), tag)
        res[tag] = [o.reshape(big_w[i].shape) for o in (g,) + tuple(outs)]

    dmod_all = small_all[:, ROW_MOD:ROW_MOD + 3, :].reshape(N_DEV, ADA_W)
    dmod_cols = lax.dynamic_slice_in_dim(dmod_all, my_chip * SHARD_ADA, SHARD_ADA, axis=1)
    c_t = jnp.pad(jnp.transpose(c_all.reshape(N_DEV, D_MODEL)), ((0, 0), (0, 128 - N_DEV)))
    dmod_cols = jnp.pad(dmod_cols, ((0, 128 - N_DEV), (0, 0)))
    res["w_ada"] = [o.reshape(w_ada.shape) for o in _adamw_w_ada(c_t, dmod_cols, w_ada[0], m_w_ada[0], v_w_ada[0])]

    def full_conv(a):
        return lax.dynamic_update_slice_in_dim(jnp.zeros((CONV_W, D_MODEL), F32), a[0], my_chip * (D_MODEL // N_CHIPS), axis=1)

    packed = [_pack_small(p[0], p[1], p[2], p[3], p[4], p[5], p[6], p[7], full_conv(p[8])) for p in (
        (b_ada, norm_g, conv_b, rg_ba, rg_bx, rg_lambda, final_g, attn_sinks, conv_w),
        (m_b_ada, m_norm_g, m_conv_b, m_rg_ba, m_rg_bx, m_rg_lambda, m_final_g, m_attn_sinks, m_conv_w),
        (v_b_ada, v_norm_g, v_conv_b, v_rg_ba, v_rg_bx, v_rg_lambda, v_final_g, v_attn_sinks, v_conv_w))]
    small_out = _adamw_small(small_all, *packed)

    def unpack(slab):
        cw = lax.dynamic_slice_in_dim(slab[ROW_CONV_W:ROW_CONV_W + CONV_W], my_chip * (D_MODEL // N_CHIPS),
                                      D_MODEL // N_CHIPS, axis=1)
        return {
            "b_ada": slab[ROW_MOD:ROW_MOD + 3].reshape(1, ADA_W), "norm_g": slab[ROW_NORM_G:ROW_NORM_G + 1],
            "conv_b": slab[ROW_CONV_B:ROW_CONV_B + 1], "rg_ba": slab[ROW_BA:ROW_BA + 1], "rg_bx": slab[ROW_BX:ROW_BX + 1],
            "rg_lambda": slab[ROW_LAM:ROW_LAM + 1], "final_g": slab[ROW_FINAL_G], "attn_sinks": slab[ROW_SINKS:ROW_SINKS + 1, :N_HEADS],
            "conv_w": cw[None],
        }

    small_res = [unpack(s) for s in small_out]
    order = ["w_ada", "b_ada", "norm_g", "w_in", "attn_sinks", "conv_w", "conv_b", "rg_wa", "rg_ba", "rg_wx", "rg_bx",
             "rg_lambda", "w_attn_proj", "w_rnn_proj", "w_out", "final_g"]
    loss = small_out[0][ROW_LOSS, 0]
    outs = [loss, grad_x[None]]
    for kind in range(4):
        for name in order:
            outs.append(res[name][kind] if name in res else small_res[kind][name])
    return tuple(outs)
```

```python
import numpy as np
import jax
import jax.numpy as jnp
from jax import lax
from jax.experimental import pallas as pl
from jax.experimental.pallas import tpu as pltpu

F32 = jnp.float32
BF16 = jnp.bfloat16

D_MODEL = 1024
N_HEADS = 16
N_KV = 4
HEAD_DIM = 64
GROUP = N_HEADS // N_KV
BLOCK = 128
KV_W = N_KV * HEAD_DIM
ROT_HALF = 8
ROPE_THETA = 500000.0
ATTN_SCALE = 0.125
RNN_BLOCKS = 4
RNN_BW = 256
CONV_W = 4
LRU_C = 8.0
NORM_EPS = 1e-6
IN_W = 6656
CB = 512
N_CB = IN_W // CB
CB_Q, CB_KV, CB_GA, CB_XR, CB_GR, CB_MA, CB_MR = 0, 2, 3, 5, 7, 9, 11
V_COL_BLOCK = 5
N_CHIPS = 4
N_DEV = 8
SHARD_IN = IN_W // N_CHIPS
SHARD_ROWS = D_MODEL // N_CHIPS
SHARD_RG = RNN_BW // N_CHIPS
ADA_W = 3 * D_MODEL
SHARD_ADA = ADA_W // N_CHIPS
SMALL_ROWS = 16

ADAM_LR = 0.001
ADAM_B1 = 0.9
ADAM_B2 = 0.999
ADAM_EPS = 1e-08
ADAM_WD = 0.01
ADAM_STEP = 10

VMEM_LIMIT_V7X = 52 * 1024 * 1024
MESH = pl.DeviceIdType.MESH
ANY = pl.BlockSpec(memory_space=pl.ANY)
VMEM_SPEC = pl.BlockSpec(memory_space=pltpu.VMEM)


def _cp(*sem):
    return pltpu.CompilerParams(dimension_semantics=sem if sem else None, vmem_limit_bytes=VMEM_LIMIT_V7X)


def _dot(a, b):
    return jnp.dot(a, b, preferred_element_type=F32)


def _dot_nt(a, b):
    return lax.dot_general(a, b, (((1,), (1,)), ((), ())), preferred_element_type=F32)


def _dot_tn(a, b):
    return lax.dot_general(a, b, (((0,), (0,)), ((), ())), preferred_element_type=F32)


def _sigmoid(z):
    return 1.0 / (1.0 + jnp.exp(-z))


def _softplus(z):
    u = jnp.exp(-jnp.abs(z))
    log1p_u = jnp.where(u < 1e-3, u * (1.0 - u * (0.5 - u * (1.0 / 3.0))), jnp.log(1.0 + u))
    return jnp.maximum(z, 0.0) + log1p_u


def _rms(xf):
    return lax.rsqrt(jnp.mean(xf * xf, axis=-1, keepdims=True) + NORM_EPS)


def _me():
    return lax.axis_index("x"), lax.axis_index("y"), lax.axis_index("c")


def _peer(mask):
    x, y, c = _me()
    fx, fy, fc = (mask >> 2) & 1, (mask >> 1) & 1, mask & 1
    return (x ^ fx if fx else x, y ^ fy if fy else y, c ^ fc if fc else c)


def _chip_of(pos):
    return pos[0] * 2 + pos[1]


CHIP_MASKS = (4, 2, 6)
ALL_MASKS = (1, 2, 3, 4, 5, 6, 7)


HBM_SPEC = pl.BlockSpec(memory_space=pltpu.HBM)
SEM_SPEC = pl.BlockSpec(memory_space=pltpu.SEMAPHORE)
SPLIT_COPY = pltpu.CompilerParams(has_side_effects=pltpu.SideEffectType.DATAFLOW_SIDE_EFFECTING)
N_BIG = 6
FULL_SHAPES = (
    (2, D_MODEL // 2, IN_W),
    (N_CHIPS, 2, SHARD_ROWS // 2, D_MODEL), (N_CHIPS, 2, SHARD_ROWS // 2, D_MODEL), (N_CHIPS, 2, SHARD_ROWS // 2, D_MODEL),
    (RNN_BLOCKS, N_CHIPS, 2, SHARD_RG // 2, RNN_BW), (RNN_BLOCKS, N_CHIPS, 2, SHARD_RG // 2, RNN_BW),
)


def _slot(full, idx, chip, half):
    if idx == 0:
        return full.at[half, :, pl.ds(pl.multiple_of(chip * SHARD_IN, 128), SHARD_IN)]
    return full.at[chip, half] if idx in (1, 2, 3) else full.at[:, chip, half]


def _three_halves(full, idx):
    return full.at[pl.ds(0, 3), 0] if idx in (1, 2, 3) else full.at[:, pl.ds(0, 3), 0]


def _gather_start(fulls, after):
    def body(*refs):
        full_refs = refs[:N_BIG]
        ssems, rsems = refs[N_BIG + 1:N_BIG + 5], refs[N_BIG + 5:N_BIG + 9]
        token = refs[2 * N_BIG + 9]
        me = _me()
        my_chip = _chip_of(me)
        for idx in range(N_BIG):
            for k, mask in enumerate(CHIP_MASKS):
                pair = k if idx == 0 else 3
                mine = _slot(full_refs[idx], idx, my_chip, me[2])
                pltpu.make_async_remote_copy(src_ref=mine, dst_ref=mine, send_sem=ssems[pair], recv_sem=rsems[pair],
                                             device_id=_peer(mask), device_id_type=MESH).start()
        token[...] = jnp.zeros_like(token)

    sem = pltpu.SemaphoreType.DMA(())
    out_shape = (sem,) * 8 + tuple(pltpu.HBM(f.shape, f.dtype) for f in fulls) + (jax.ShapeDtypeStruct((8, 128), F32),)
    outs = pl.pallas_call(
        body, out_shape=out_shape, name="gather_start",
        in_specs=[HBM_SPEC] * N_BIG + [ANY], out_specs=tuple([SEM_SPEC] * 8 + [HBM_SPEC] * N_BIG + [VMEM_SPEC]),
        input_output_aliases={i: 8 + i for i in range(N_BIG)}, compiler_params=SPLIT_COPY,
    )(*[pltpu.with_memory_space_constraint(f, pltpu.HBM) for f in fulls], after)
    return outs[0:4], outs[4:8], outs[8:8 + N_BIG], outs[8 + N_BIG]


def _gather_wait(ssem, rsem, arrays, idxs, after, tag):
    n = len(arrays)

    def body(*refs):
        full_refs, ssem_ref, rsem_ref = refs[:n], refs[n], refs[n + 1]
        me = _me()
        for full, idx in zip(full_refs, idxs):
            region = _slot(full, 0, _chip_of(me), me[2]) if idx == 0 else _three_halves(full, idx)
            arrived = pltpu.make_async_remote_copy(
                src_ref=region, dst_ref=region, send_sem=ssem_ref, recv_sem=rsem_ref, device_id=me, device_id_type=MESH)
            arrived.wait_send()
            arrived.wait_recv()

    outs = pl.pallas_call(
        body, out_shape=tuple(pltpu.HBM(a.shape, a.dtype) for a in arrays), name=f"gather_wait_{tag}",
        in_specs=[HBM_SPEC] * n + [SEM_SPEC, SEM_SPEC, ANY], out_specs=tuple([HBM_SPEC] * n),
        input_output_aliases={i: i for i in range(n)}, compiler_params=SPLIT_COPY,
    )(*arrays, ssem, rsem, after)
    return list(outs)


def _forward_halves(arrays, items, tag):
    n, m = len(arrays), len(items)

    def body(*refs):
        outs, ssem, rsem = refs[n:2 * n], refs[2 * n], refs[2 * n + 1]
        me = _me()
        sib = _peer(1)
        cps = []
        for j, (pos, idx, k) in enumerate(items):
            chip = _chip_of(_peer(CHIP_MASKS[k]))
            cp = pltpu.make_async_remote_copy(
                src_ref=_slot(outs[pos], idx, chip, me[2]), dst_ref=_slot(outs[pos], idx, chip, me[2]),
                send_sem=ssem.at[j], recv_sem=rsem.at[j], device_id=sib, device_id_type=MESH)
            cp.start()
            cps.append(cp)
        for j, (pos, idx, k) in enumerate(items):
            chip = _chip_of(_peer(CHIP_MASKS[k]))
            pltpu.make_async_remote_copy(
                src_ref=_slot(outs[pos], idx, chip, me[2]), dst_ref=_slot(outs[pos], idx, chip, 1 - me[2]),
                send_sem=ssem.at[j], recv_sem=rsem.at[j], device_id=sib, device_id_type=MESH).wait_recv()
        for cp in cps:
            cp.wait_send()

    outs = pl.pallas_call(
        body, out_shape=tuple(jax.ShapeDtypeStruct(a.shape, a.dtype) for a in arrays), name=f"forward_halves_{tag}",
        in_specs=[ANY] * n, out_specs=tuple([ANY] * n), input_output_aliases={i: i for i in range(n)},
        scratch_shapes=[pltpu.SemaphoreType.DMA((m,)), pltpu.SemaphoreType.DMA((m,))],
    )(*arrays)
    return list(outs)


def _gather_mod(c_row, w_ada_s, conv_w_s):
    def body(c_ref, wada_ref, cw_s, cw_f, call_ref, mod_ref, wsend, wrecv, lsem, csend, crecv, msend, mrecv):
        me = _me()
        my_chip = _chip_of(me)
        my_dev = my_chip * 2 + me[2]
        sends = []
        for k, mask in enumerate(CHIP_MASKS):
            cp = pltpu.make_async_remote_copy(src_ref=cw_s, dst_ref=cw_f.at[my_chip], send_sem=wsend.at[k], recv_sem=wrecv.at[k],
                                              device_id=_peer(mask), device_id_type=MESH)
            cp.start()
            sends.append(cp)
        local = [pltpu.make_async_copy(cw_s, cw_f.at[my_chip], lsem.at[0])]
        for cp in local:
            cp.start()

        call_ref[my_dev] = c_ref[0]
        csends = []
        for k, mask in enumerate(ALL_MASKS):
            cp = pltpu.make_async_remote_copy(
                src_ref=c_ref.at[0], dst_ref=call_ref.at[my_dev],
                send_sem=csend.at[k], recv_sem=crecv.at[k], device_id=_peer(mask), device_id_type=MESH)
            cp.start()
            csends.append(cp)
        for k, mask in enumerate(ALL_MASKS):
            frm = _peer(mask)
            pltpu.make_async_remote_copy(
                src_ref=c_ref.at[0], dst_ref=call_ref.at[_chip_of(frm) * 2 + frm[2]],
                send_sem=csend.at[k], recv_sem=crecv.at[k], device_id=frm, device_id_type=MESH).wait_recv()
        for cp in csends:
            cp.wait_send()

        c_all = call_ref[...].reshape(N_DEV, D_MODEL).astype(BF16)
        mod_ref[my_chip] = _dot(c_all, wada_ref[...].astype(BF16))
        msends = []
        for k, mask in enumerate(CHIP_MASKS):
            cp = pltpu.make_async_remote_copy(
                src_ref=mod_ref.at[my_chip], dst_ref=mod_ref.at[my_chip],
                send_sem=msend.at[k], recv_sem=mrecv.at[k], device_id=_peer(mask), device_id_type=MESH)
            cp.start()
            msends.append(cp)
        for k, mask in enumerate(CHIP_MASKS):
            frm = _peer(mask)
            pltpu.make_async_remote_copy(
                src_ref=mod_ref.at[my_chip], dst_ref=mod_ref.at[_chip_of(frm)],
                send_sem=msend.at[k], recv_sem=mrecv.at[k], device_id=frm, device_id_type=MESH).wait_recv()
        for cp in msends:
            cp.wait_send()

        for k, mask in enumerate(CHIP_MASKS):
            frm = _peer(mask)
            pltpu.make_async_remote_copy(src_ref=cw_s, dst_ref=cw_f.at[_chip_of(frm)], send_sem=wsend.at[k], recv_sem=wrecv.at[k],
                                         device_id=frm, device_id_type=MESH).wait_recv()
        for cp in sends:
            cp.wait_send()
        for cp in local:
            cp.wait()

    out_shape = (
        jax.ShapeDtypeStruct((N_CHIPS, CONV_W, D_MODEL // N_CHIPS), F32),
        jax.ShapeDtypeStruct((N_DEV, 1, D_MODEL), F32),
        jax.ShapeDtypeStruct((N_CHIPS, N_DEV, SHARD_ADA), F32),
    )
    return pl.pallas_call(
        body, out_shape=out_shape, name="gather_mod",
        in_specs=[VMEM_SPEC, VMEM_SPEC, ANY], out_specs=(ANY, VMEM_SPEC, VMEM_SPEC),
        scratch_shapes=[
            pltpu.SemaphoreType.DMA((3,)), pltpu.SemaphoreType.DMA((3,)), pltpu.SemaphoreType.DMA((1,)),
            pltpu.SemaphoreType.DMA((7,)), pltpu.SemaphoreType.DMA((7,)),
            pltpu.SemaphoreType.DMA((3,)), pltpu.SemaphoreType.DMA((3,)),
        ],
        compiler_params=pltpu.CompilerParams(vmem_limit_bytes=VMEM_LIMIT_V7X),
    )(c_row, w_ada_s, conv_w_s)


def _cast_place(shard, chip_idx, full_shape, block, index_map, tag):
    def body(chip_ref, s_ref, o_ref):
        o_ref[...] = s_ref[...].astype(BF16)

    grid_spec = pltpu.PrefetchScalarGridSpec(
        num_scalar_prefetch=1, grid=(1,),
        in_specs=[pl.BlockSpec(shard.shape, lambda i, chip_ref: (0,) * shard.ndim)],
        out_specs=pl.BlockSpec(block, lambda i, chip_ref: index_map(chip_ref[0])))
    return pl.pallas_call(
        body, out_shape=jax.ShapeDtypeStruct(full_shape, BF16), grid_spec=grid_spec, name=f"cast_place_{tag}",
        compiler_params=_cp("arbitrary"),
    )(chip_idx, shard)


def _shard_of(ref, kind, chip):
    if kind == "in":
        return ref.at[:, pl.ds(pl.multiple_of(chip * SHARD_IN, 128), SHARD_IN)]
    return ref.at[chip] if kind == "sq" else ref.at[:, chip]


def _land_shape(src, kind):
    if kind == "in":
        return (3, src.shape[0], SHARD_IN)
    return (3,) + src.shape[1:] if kind == "sq" else (3, src.shape[0]) + src.shape[2:]


def _exchange_start(srcs, kinds, tag):
    n = len(srcs)
    lands = [pltpu.with_memory_space_constraint(lax.empty(_land_shape(s, k), s.dtype), pltpu.HBM) for s, k in zip(srcs, kinds)]

    def body(*refs):
        src_refs, land_refs = refs[:n], refs[n:2 * n]
        ssems, rsems = refs[2 * n:3 * n], refs[3 * n:4 * n]
        token = refs[6 * n]
        for i in range(n):
            for k, mask in enumerate(CHIP_MASKS):
                to = _peer(mask)
                pltpu.make_async_remote_copy(
                    src_ref=_shard_of(src_refs[i], kinds[i], _chip_of(to)), dst_ref=land_refs[i].at[k],
                    send_sem=ssems[i], recv_sem=rsems[i], device_id=to, device_id_type=MESH).start()
        token[...] = jnp.zeros_like(token)

    sem = pltpu.SemaphoreType.DMA(())
    out_shape = ((sem,) * (2 * n) + tuple(pltpu.HBM(s.shape, s.dtype) for s in srcs)
                 + tuple(pltpu.HBM(l.shape, l.dtype) for l in lands) + (jax.ShapeDtypeStruct((8, 128), F32),))
    outs = pl.pallas_call(
        body, out_shape=out_shape, name=f"exchange_start_{tag}",
        in_specs=[HBM_SPEC] * (2 * n), out_specs=tuple([SEM_SPEC] * (2 * n) + [HBM_SPEC] * (2 * n) + [VMEM_SPEC]),
        input_output_aliases={i: 2 * n + i for i in range(2 * n)},
        compiler_params=pltpu.CompilerParams(has_side_effects=pltpu.SideEffectType.DATAFLOW_SIDE_EFFECTING),
    )(*[pltpu.with_memory_space_constraint(s, pltpu.HBM) for s in srcs], *lands)
    return outs[:n], outs[n:2 * n], outs[2 * n:3 * n], outs[3 * n:4 * n], outs[4 * n]


def _exchange_wait(ssems, rsems, srcs, lands, after, tag):
    n = len(srcs)

    def body(*refs):
        land_refs = refs[n:2 * n]
        ssem_refs, rsem_refs = refs[2 * n:3 * n], refs[3 * n:4 * n]
        for i in range(n):
            all_three = pltpu.make_async_remote_copy(
                src_ref=land_refs[i], dst_ref=land_refs[i], send_sem=ssem_refs[i], recv_sem=rsem_refs[i],
                device_id=_me(), device_id_type=MESH)
            all_three.wait_send()
            all_three.wait_recv()

    outs = pl.pallas_call(
        body, out_shape=tuple(pltpu.HBM(a.shape, a.dtype) for a in list(srcs) + list(lands)), name=f"exchange_wait_{tag}",
        in_specs=[HBM_SPEC] * (2 * n) + [SEM_SPEC] * (2 * n) + [ANY], out_specs=tuple([HBM_SPEC] * (2 * n)),
        input_output_aliases={i: i for i in range(2 * n)},
        compiler_params=pltpu.CompilerParams(has_side_effects=pltpu.SideEffectType.DATAFLOW_SIDE_EFFECTING),
    )(*srcs, *lands, *ssems, *rsems, after)
    return outs[:n], outs[n:]


def _gather_small(small):
    def body(small_ref, small_all, ssend, srecv):
        me = _me()
        my_dev = _chip_of(me) * 2 + me[2]
        small_all[my_dev] = small_ref[...]
        ssends = []
        for k, mask in enumerate(ALL_MASKS):
            cp = pltpu.make_async_remote_copy(
                src_ref=small_ref, dst_ref=small_all.at[my_dev],
                send_sem=ssend.at[k], recv_sem=srecv.at[k], device_id=_peer(mask), device_id_type=MESH)
            cp.start()
            ssends.append(cp)
        for k, mask in enumerate(ALL_MASKS):
            frm = _peer(mask)
            pltpu.make_async_remote_copy(
                src_ref=small_ref, dst_ref=small_all.at[_chip_of(frm) * 2 + frm[2]],
                send_sem=ssend.at[k], recv_sem=srecv.at[k], device_id=frm, device_id_type=MESH).wait_recv()
        for cp in ssends:
            cp.wait_send()

    return pl.pallas_call(
        body, out_shape=jax.ShapeDtypeStruct((N_DEV, SMALL_ROWS, D_MODEL), F32), name="gather_small",
        in_specs=[VMEM_SPEC], out_specs=VMEM_SPEC,
        scratch_shapes=[pltpu.SemaphoreType.DMA((7,)), pltpu.SemaphoreType.DMA((7,))],
    )(small)


def _half_of(ref, axis, half):
    return ref.at[(slice(None),) * axis + (half,)]


def _swap_halves(parts, axes):
    n = len(parts)

    def body(*refs):
        ins, outs, ssem, rsem = refs[:n], refs[n:2 * n], refs[2 * n], refs[2 * n + 1]
        c = lax.axis_index("c")
        cps = [pltpu.make_async_remote_copy(src_ref=_half_of(ins[i], axes[i], 1 - c), dst_ref=outs[i], send_sem=ssem.at[i],
                                            recv_sem=rsem.at[i], device_id=_peer(1), device_id_type=MESH) for i in range(n)]
        for cp in cps:
            cp.start()
        for cp in cps:
            cp.wait()

    shapes = [p.shape[:a] + p.shape[a + 1:] for p, a in zip(parts, axes)]
    return pl.pallas_call(
        body, out_shape=tuple(jax.ShapeDtypeStruct(s, p.dtype) for s, p in zip(shapes, parts)), name="swap_halves",
        in_specs=[ANY] * n, out_specs=tuple([ANY] * n),
        scratch_shapes=[pltpu.SemaphoreType.DMA((n,)), pltpu.SemaphoreType.DMA((n,))],
    )(*parts)


def _presum(mine, sib, c_idx, tag):
    S, _, R, C = mine.shape
    tr = min(R, 256)
    tc = SHARD_IN if C % SHARD_IN == 0 else C

    def body(c_ref, m_ref, s_ref, o_ref, ob_ref):
        total = m_ref[:, 0] + s_ref[...]
        o_ref[...] = total
        ob_ref[...] = total.astype(BF16)

    out_spec = pl.BlockSpec((S, tr, tc), lambda i, j, c_ref: (0, i, j))
    grid_spec = pltpu.PrefetchScalarGridSpec(
        num_scalar_prefetch=1, grid=(R // tr, C // tc),
        in_specs=[pl.BlockSpec((S, 1, tr, tc), lambda i, j, c_ref: (0, c_ref[0], i, j)),
                  pl.BlockSpec((S, tr, tc), lambda i, j, c_ref: (0, i, j))],
        out_specs=(out_spec, out_spec))
    return pl.pallas_call(
        body, out_shape=(jax.ShapeDtypeStruct((S, R, C), F32), jax.ShapeDtypeStruct((S, R, C), BF16)),
        grid_spec=grid_spec, name=f"presum_{tag}", compiler_params=_cp("parallel", "parallel"),
    )(c_idx, mine, sib)


def _assemble_with_sibling(parts, axes):
    n = len(parts)

    def body(*refs):
        outs, ssem, rsem = refs[n:2 * n], refs[2 * n], refs[2 * n + 1]
        c = lax.axis_index("c")
        cps = [pltpu.make_async_remote_copy(
            src_ref=_half_of(outs[i], axes[i], c), dst_ref=_half_of(outs[i], axes[i], c), send_sem=ssem.at[i],
            recv_sem=rsem.at[i], device_id=_peer(1), device_id_type=MESH) for i in range(n)]
        for cp in cps:
            cp.start()
        for i in range(n):
            pltpu.make_async_remote_copy(
                src_ref=_half_of(outs[i], axes[i], c), dst_ref=_half_of(outs[i], axes[i], 1 - c), send_sem=ssem.at[i],
                recv_sem=rsem.at[i], device_id=_peer(1), device_id_type=MESH).wait_recv()
        for cp in cps:
            cp.wait_send()

    return pl.pallas_call(
        body, out_shape=tuple(jax.ShapeDtypeStruct(p.shape, p.dtype) for p in parts), name="assemble_with_sibling",
        in_specs=[ANY] * n, out_specs=tuple([ANY] * n), input_output_aliases={i: i for i in range(n)},
        scratch_shapes=[pltpu.SemaphoreType.DMA((n,)), pltpu.SemaphoreType.DMA((n,))],
    )(*parts)


def _rope_tables(pos_col):
    T = pos_col.shape[0]
    tm = min(T, 512)
    inv = np.float32(ROPE_THETA) ** (-(np.arange(0, 2 * ROT_HALF, 2, dtype=np.float32)) / np.float32(2 * ROT_HALF))
    lane = np.arange(128) % HEAD_DIM
    freq = np.where(lane < 2 * ROT_HALF, inv[lane % ROT_HALF], 0.0).astype(np.float32)[None, :]

    def body(pos_ref, f_ref, c_ref, sa_ref, sb_ref):
        ang = pos_ref[...].astype(F32) * f_ref[...]
        c, s = jnp.cos(ang), jnp.sin(ang)
        m = lax.broadcasted_iota(jnp.int32, ang.shape, 1) & (HEAD_DIM - 1)
        c_ref[...] = jnp.where(m < 2 * ROT_HALF, c, 1.0)
        sa_ref[...] = jnp.where(m < ROT_HALF, -s, 0.0)
        sb_ref[...] = jnp.where((m >= ROT_HALF) & (m < 2 * ROT_HALF), s, 0.0)

    tab = jax.ShapeDtypeStruct((T, 128), F32)
    return pl.pallas_call(
        body, out_shape=(tab, tab, tab), grid=(T // tm,), name="rope_tables",
        in_specs=[pl.BlockSpec((tm, 1), lambda i: (i, 0)), pl.BlockSpec((1, 128), lambda i: (0, 0))],
        out_specs=tuple(pl.BlockSpec((tm, 128), lambda i: (i, 0)) for _ in range(3)),
        compiler_params=_cp("parallel"),
    )(pos_col, jnp.asarray(freq))


def _wide(tab, width):
    del width
    return tab


def _columns(t):
    return [t[:, i:i + 128] for i in range(0, t.shape[-1], 128)]


def _rope(t, c, sa, sb):
    return jnp.concatenate(
        [x * c + pltpu.roll(x, 128 - ROT_HALF, 1) * sa + pltpu.roll(x, ROT_HALF, 1) * sb for x in _columns(t)], axis=1)


def _unrope(d, c, sa, sb):
    return jnp.concatenate(
        [x * c + pltpu.roll(x * sa, ROT_HALF, 1) + pltpu.roll(x * sb, 128 - ROT_HALF, 1) for x in _columns(d)], axis=1)


def _prenorm(x, mod_row, norm_g):
    T = x.shape[0]
    tm = min(T, 512)

    def body(x_ref, mod_ref, g_ref, h_ref, ht_ref):
        xf = x_ref[...]
        shift, scale = mod_ref[:, 0:D_MODEL], mod_ref[:, D_MODEL:2 * D_MODEL]
        h = (xf * _rms(xf)) * g_ref[...] * (1.0 + scale) + shift
        h_ref[...] = h.astype(BF16)
        ht_ref[...] = h.T.astype(BF16)

    return pl.pallas_call(
        body, out_shape=(jax.ShapeDtypeStruct((T, D_MODEL), BF16), jax.ShapeDtypeStruct((D_MODEL, T), BF16)),
        grid=(T // tm,), name="prenorm",
        in_specs=[pl.BlockSpec((tm, D_MODEL), lambda i: (i, 0)), pl.BlockSpec((1, ADA_W), lambda i: (0, 0)),
                  pl.BlockSpec((1, D_MODEL), lambda i: (0, 0))],
        out_specs=(pl.BlockSpec((tm, D_MODEL), lambda i: (i, 0)), pl.BlockSpec((D_MODEL, tm), lambda i: (0, i))),
        compiler_params=_cp("parallel"),
    )(x, mod_row, norm_g)


def _in_projection(h, w_in, chip, into, tag):
    T = h.shape[0]
    tm, tn = min(T, 512), SHARD_IN

    def body(chip_ref, h_ref, w_ref, *rest):
        rest[-1][...] = _dot(h_ref[...], w_ref[...])

    in_specs = [pl.BlockSpec((tm, D_MODEL), lambda i, c: (i, 0)),
                pl.BlockSpec((D_MODEL, tn), lambda i, c: (0, c[0]), pipeline_mode=pl.Buffered(1))]
    args = [chip, h, w_in]
    aliases = {}
    if into is not None:
        in_specs.append(ANY)
        args.append(into)
        aliases = {3: 0}
    grid_spec = pltpu.PrefetchScalarGridSpec(num_scalar_prefetch=1, grid=(T // tm,), in_specs=in_specs,
                                             out_specs=pl.BlockSpec((tm, tn), lambda i, c: (i, c[0])))
    return pl.pallas_call(
        body, out_shape=jax.ShapeDtypeStruct((T, IN_W), F32), grid_spec=grid_spec, name=f"in_projection_{tag}",
        input_output_aliases=aliases, compiler_params=_cp("parallel"),
    )(*args)


def _attn_mask(n):
    qi = lax.broadcasted_iota(jnp.int32, (GROUP * BLOCK, BLOCK), 0) & (BLOCK - 1)
    j = lax.broadcasted_iota(jnp.int32, (GROUP * BLOCK, BLOCK), 1)
    own = j <= qi
    return own, jnp.logical_not(own) & (n == 0)


def _fold(x, own):
    return jnp.where(own, x[:, BLOCK:2 * BLOCK], x[:, 0:BLOCK])


def _unfold(xf, own):
    zero = jnp.zeros_like(xf)
    return jnp.concatenate([jnp.where(own, zero, xf), jnp.where(own, xf, zero)], axis=1)


ROW_GROUP_HEAD = (0, 2, 1, 3)


def _sink_col(sink_ref, kh):
    rowg = lax.broadcasted_iota(jnp.int32, (GROUP * BLOCK, 1), 0) // BLOCK
    col = jnp.full((GROUP * BLOCK, 1), sink_ref[0, GROUP * kh + ROW_GROUP_HEAD[0]], F32)
    for g in range(1, GROUP):
        col = jnp.where(rowg == g, sink_ref[0, GROUP * kh + ROW_GROUP_HEAD[g]], col)
    return col


def _low_lanes(shape):
    return lax.broadcasted_iota(jnp.int32, shape, 1) < HEAD_DIM


def _kv_pair_operand(prev, cur, kh):
    c = 128 * (kh // 2)
    col = jnp.concatenate([prev[:, c:c + 128], cur[:, c:c + 128]], axis=0).astype(F32)
    if kh % 2 == 0:
        lo = jnp.where(_low_lanes(col.shape), col, 0.0)
        hi = pltpu.roll(lo, HEAD_DIM, 1)
    else:
        hi = jnp.where(_low_lanes(col.shape), 0.0, col)
        lo = pltpu.roll(hi, HEAD_DIM, 1)
    return jnp.concatenate([lo, hi], axis=0).astype(BF16)


def _pair_rows(x, kh):
    c = 2 * 128 * kh
    return jnp.concatenate([x[:, c:c + 128], x[:, c + 128:c + 256]], axis=0)


def _restack(big):
    return jnp.concatenate([big[:, 0:2 * BLOCK], big[:, 2 * BLOCK:4 * BLOCK]], axis=0)


def _unrestack(stacked):
    return jnp.concatenate([stacked[0:2 * BLOCK], stacked[2 * BLOCK:4 * BLOCK]], axis=1)


def _fold_pair(x2, kh):
    low = _low_lanes((2 * BLOCK, 128))
    mixed = jnp.where(low, x2[0:2 * BLOCK], x2[2 * BLOCK:4 * BLOCK])
    total = mixed + pltpu.roll(mixed, HEAD_DIM, 1)
    return jnp.where(low, total, 0.0) if kh % 2 == 0 else jnp.where(low, 0.0, total)


def _attn_scores(qr, k2, kh):
    q2 = _pair_rows(qr, kh).astype(BF16)
    return q2, _restack(_dot_nt(q2, k2))


def _attn_softmax(s, sink_col, mask):
    own, no_key = mask
    s = jnp.where(no_key, -1e30, _fold(s, own))
    m = jnp.maximum(jnp.max(s, axis=-1, keepdims=True), sink_col)
    p = jnp.exp(s - m)
    p_sink = jnp.exp(sink_col - m)
    denom = jnp.sum(p, axis=-1, keepdims=True) + p_sink
    return p / denom, p_sink / denom


def _attn_forward(proj, tabs, sinks):
    T = proj.shape[0]
    nb = T // BLOCK

    def body(q_ref, kvc_ref, kvp_ref, g0_ref, g1_ref, cc, sac, sbc, cp_, sap, sbp, sink_ref, y_ref, qrb_ref, krb_ref):
        n = pl.program_id(0)
        tc = (_wide(cc[...], D_MODEL), _wide(sac[...], D_MODEL), _wide(sbc[...], D_MODEL))
        tcur = tuple(t[:, :KV_W] for t in tc)
        tprev = (_wide(cp_[...], KV_W), _wide(sap[...], KV_W), _wide(sbp[...], KV_W))
        qr = _rope(q_ref[...], *tc) * ATTN_SCALE
        kr_cur = _rope(kvc_ref[:, 0:KV_W], *tcur)
        kr_prev = _rope(kvp_ref[:, 0:KV_W], *tprev)
        qrb_ref[...] = qr.astype(BF16)
        krb_ref[...] = kr_cur.astype(BF16)
        v_cur, v_prev = kvc_ref[:, KV_W:2 * KV_W], kvp_ref[:, KV_W:2 * KV_W]
        mask = _attn_mask(n)
        outs = []
        k2s = [_kv_pair_operand(kr_prev, kr_cur, kh) for kh in range(N_KV)]
        v2s = [_kv_pair_operand(v_prev, v_cur, kh) for kh in range(N_KV)]
        scores = [_attn_scores(qr, k2s[kh], kh) for kh in range(N_KV)]
        for kh in range(N_KV):
            pn, _ = _attn_softmax(scores[kh][1], _sink_col(sink_ref, kh), mask)
            o_big = _dot(_unrestack(_unfold(pn.astype(BF16), mask[0])), v2s[kh])
            outs += [o_big[0:BLOCK], o_big[BLOCK:2 * BLOCK]]
        o = jnp.concatenate(outs, axis=1)
        g = jnp.concatenate([g0_ref[...], g1_ref[...]], axis=1)
        y_ref[...] = (o * (g * _sigmoid(g))).astype(BF16)

    def blk(w, cb):
        return pl.BlockSpec((BLOCK, w), lambda n, cb=cb: (n, cb))

    prev = lambda w, cb: pl.BlockSpec((BLOCK, w), lambda n, cb=cb: (jnp.maximum(n - 1, 0), cb))
    return pl.pallas_call(
        body, grid=(nb,), name="attn_forward",
        out_shape=(jax.ShapeDtypeStruct((T, D_MODEL), BF16), jax.ShapeDtypeStruct((T, D_MODEL), BF16),
                   jax.ShapeDtypeStruct((T, KV_W), BF16)),
        in_specs=[blk(D_MODEL, 0), blk(CB, CB_KV), prev(CB, CB_KV), blk(CB, CB_GA), blk(CB, CB_GA + 1),
                  blk(128, 0), blk(128, 0), blk(128, 0), prev(128, 0), prev(128, 0), prev(128, 0),
                  pl.BlockSpec(memory_space=pltpu.SMEM)],
        out_specs=(blk(D_MODEL, 0), blk(D_MODEL, 0), blk(KV_W, 0)),
        compiler_params=_cp("parallel"),
    )(proj, proj, proj, proj, proj, *tabs, *tabs, sinks)


def _scan_rows8():
    return lax.broadcasted_iota(jnp.int32, (8, D_MODEL), 0)


def _scan_forward(a_ref, b_ref, h_ref, carry, rows):
    row = _scan_rows8()

    def group(i, carry):
        off = pl.multiple_of(i * 8, 8)
        a, b = a_ref[pl.ds(off, 8), :], b_ref[pl.ds(off, 8), :]
        for d in (1, 2, 4):
            ok = row >= d
            b = jnp.where(ok, a * pltpu.roll(b, d, 0) + b, b)
            a = jnp.where(ok, a * pltpu.roll(a, d, 0), a)
        h = a * carry + b
        h_ref[pl.ds(off, 8), :] = h
        return h[7:8, :]

    return lax.fori_loop(0, rows // 8, group, carry)


def _scan_backward(a_ref, g_ref, lam_ref, carry, rows):
    row = _scan_rows8()

    def group(i, carry):
        off = pl.multiple_of((rows // 8 - 1 - i) * 8, 8)
        a, g = a_ref[pl.ds(off, 8), :], g_ref[pl.ds(off, 8), :]
        b = a * g
        for d in (1, 2, 4):
            ok = row < 8 - d
            b = jnp.where(ok, a * pltpu.roll(b, 8 - d, 0) + b, b)
            a = jnp.where(ok, a * pltpu.roll(a, 8 - d, 0), a)
        mu = a * carry + b
        mu_below = jnp.where(row == 7, carry, pltpu.roll(mu, 7, 0))
        lam_ref[pl.ds(off, 8), :] = g + mu_below
        return mu[0:1, :]

    return lax.fori_loop(0, rows // 8, group, carry)


def _conv_taps(xbuf, xr, tail):
    rows = xr.shape[0]
    xbuf[0:8, :] = tail
    xbuf[8:rows + 8, :] = xr
    return [xbuf[pl.ds(8 - (CONV_W - 1 - k), rows), :] for k in range(CONV_W - 1)] + [xr]


def _rnn_gates(xbuf, xr, tail, cw, cb, wa_ref, wx_ref, ba, bx, sp, reset):
    xs = _conv_taps(xbuf, xr, tail)
    xc = xs[0] * cw[0:1, :]
    for k in range(1, CONV_W):
        xc = xc + xs[k] * cw[k:k + 1, :]
    xc = xc + cb
    xcb = xc.astype(BF16)
    za = jnp.concatenate([_dot(xcb[:, RNN_BW * j:RNN_BW * (j + 1)], wa_ref[j]) for j in range(RNN_BLOCKS)], axis=1) + ba
    zx = jnp.concatenate([_dot(xcb[:, RNN_BW * j:RNN_BW * (j + 1)], wx_ref[j]) for j in range(RNN_BLOCKS)], axis=1) + bx
    r, i = _sigmoid(za), _sigmoid(zx)
    neg_log_a = LRU_C * r * sp
    a_raw = jnp.exp(-neg_log_a)
    mult_raw = jnp.sqrt(jnp.tanh(neg_log_a) * (1.0 + a_raw * a_raw))
    a = jnp.where(reset, 0.0, a_raw)
    mult = jnp.where(reset, 1.0, mult_raw)
    return xc, r, i, a, mult


def _rnn_forward(proj, pos_col, conv_w, conv_b, rwa, rwx, ba, bx, lam):
    T = proj.shape[0]
    tr = min(T, 256)

    def body(x0, x1, g0, g1, pos_ref, cw_ref, cb_ref, wa_ref, wx_ref, ba_ref, bx_ref, lam_ref,
             y_ref, h_ref, xc_ref, r_ref, i_ref, a_ref, mult_ref, xbuf, bbuf, tail, carry):
        t = pl.program_id(0)

        @pl.when(t == 0)
        def _():
            tail[...] = jnp.zeros_like(tail)
            carry[...] = jnp.zeros_like(carry)

        xr = jnp.concatenate([x0[...], x1[...]], axis=1)
        sp = _softplus(-lam_ref[...])
        reset = pos_ref[...] == 0
        xc, r, i, a, mult = _rnn_gates(
            xbuf, xr, tail[...], cw_ref[...], cb_ref[...], wa_ref, wx_ref, ba_ref[...], bx_ref[...], sp, reset)
        xc_ref[...] = xc
        r_ref[...] = r
        i_ref[...] = i
        a_ref[...] = a
        mult_ref[...] = mult
        bbuf[...] = mult * (i * xc)
        last = _scan_forward(a_ref, bbuf, h_ref, carry[0:1, :], tr)
        carry[...] = jnp.broadcast_to(last, carry.shape)
        tail[...] = xr[tr - 8:tr, :]
        g = jnp.concatenate([g0[...], g1[...]], axis=1)
        y_ref[...] = (h_ref[...] * (g * _sigmoid(g))).astype(BF16)

    blk = lambda cb: pl.BlockSpec((tr, CB), lambda t, cb=cb: (t, cb))
    row = lambda w: pl.BlockSpec((1, w), lambda t: (0, 0))
    full3 = pl.BlockSpec((RNN_BLOCKS, RNN_BW, RNN_BW), lambda t: (0, 0, 0))
    tok = pl.BlockSpec((tr, D_MODEL), lambda t: (t, 0))
    act = jax.ShapeDtypeStruct((T, D_MODEL), F32)
    return pl.pallas_call(
        body, out_shape=(jax.ShapeDtypeStruct((T, D_MODEL), BF16),) + (act,) * 6,
        grid=(T // tr,), name="rnn_forward",
        in_specs=[blk(CB_XR), blk(CB_XR + 1), blk(CB_GR), blk(CB_GR + 1), pl.BlockSpec((tr, 1), lambda t: (t, 0)),
                  pl.BlockSpec((CONV_W, D_MODEL), lambda t: (0, 0)), row(D_MODEL), full3, full3,
                  row(D_MODEL), row(D_MODEL), row(D_MODEL)],
        out_specs=(tok,) * 7,
        scratch_shapes=[pltpu.VMEM((tr + 8, D_MODEL), F32), pltpu.VMEM((tr, D_MODEL), F32),
                        pltpu.VMEM((8, D_MODEL), F32), pltpu.VMEM((8, D_MODEL), F32)],
        compiler_params=_cp("arbitrary"),
    )(proj, proj, proj, proj, pos_col, conv_w, conv_b, rwa, rwx, ba, bx, lam)


ROW_PARTS = 1


def _merge_and_head(x, target, y_attn, y_rnn, proj, wap, wrp, wo, mod_row, final_g):
    T = x.shape[0]
    tm = min(T, 256)

    def body(x_ref, t_ref, ya_ref, yr_ref, ma0, ma1, mr0, mr1, wap_ref, wrp_ref, wo_ref, mod_ref, fg_ref,
             dx2_ref, mg_ref, do_ref, dpa_ref, dpr_ref, dya_ref, dyr_ref, dc_ref, dfg_ref, dgate_ref, loss_ref):
        i = pl.program_id(0)
        gate = mod_ref[:, 2 * D_MODEL:3 * D_MODEL]
        fg = fg_ref[...]
        parts = [slice(p * (tm // ROW_PARTS), (p + 1) * (tm // ROW_PARTS)) for p in range(ROW_PARTS)]
        each = range(ROW_PARTS)
        pa = [_dot(ya_ref[r, :], wap_ref[...]) for r in parts]
        pr = [_dot(yr_ref[r, :], wrp_ref[...]) for r in parts]
        sa = [_sigmoid(jnp.concatenate([ma0[r, :], ma1[r, :]], axis=1)) for r in parts]
        sr = [_sigmoid(jnp.concatenate([mr0[r, :], mr1[r, :]], axis=1)) for r in parts]
        mb = [(sa[p] * pa[p] + sr[p] * pr[p]).astype(BF16) for p in each]
        o = [_dot(mb[p], wo_ref[...]) for p in each]
        x2 = [x_ref[r, :] + gate * o[p] for p, r in enumerate(parts)]
        r2 = [_rms(v) for v in x2]
        xn2 = [x2[p] * r2[p] for p in each]
        err = [xn2[p] * fg - t_ref[r, :] for p, r in enumerate(parts)]
        dy = [e * (1.0 / D_MODEL) for e in err]
        dxn = [d * fg for d in dy]
        dx2 = [r2[p] * (dxn[p] - xn2[p] * jnp.mean(dxn[p] * xn2[p], axis=-1, keepdims=True)) for p in each]
        dob = [(dx2[p] * gate).astype(BF16) for p in each]
        dmerged = [_dot_nt(d, wo_ref[...]) for d in dob]
        dpa = [(dmerged[p] * sa[p]).astype(BF16) for p in each]
        dpr = [(dmerged[p] * sr[p]).astype(BF16) for p in each]
        dya = [_dot_nt(d, wap_ref[...]) for d in dpa]
        dyr = [_dot_nt(d, wrp_ref[...]) for d in dpr]
        loss_t, dfg_t, dgate_t = 0.0, 0.0, 0.0
        for p, r in enumerate(parts):
            dx2_ref[r, :] = dx2[p]
            mg_ref[r, :] = mb[p]
            do_ref[r, :] = dob[p]
            dpa_ref[r, :] = dpa[p]
            dpr_ref[r, :] = dpr[p]
            dya_ref[r, :] = dya[p]
            dyr_ref[r, :] = dyr[p]
            dc_ref[r, 0:D_MODEL] = (dmerged[p] * pa[p] * sa[p] * (1.0 - sa[p])).astype(BF16)
            dc_ref[r, D_MODEL:2 * D_MODEL] = (dmerged[p] * pr[p] * sr[p] * (1.0 - sr[p])).astype(BF16)
            loss_t = loss_t + 0.5 * jnp.sum(
                jnp.sum(err[p] * err[p], axis=-1, keepdims=True) * (1.0 / D_MODEL), axis=0, keepdims=True)
            dfg_t = dfg_t + jnp.sum(dy[p] * xn2[p], axis=0, keepdims=True)
            dgate_t = dgate_t + jnp.sum(dx2[p] * o[p], axis=0, keepdims=True)

        @pl.when(i == 0)
        def _():
            dfg_ref[...] = jnp.zeros_like(dfg_ref)
            dgate_ref[...] = jnp.zeros_like(dgate_ref)
            loss_ref[...] = jnp.zeros_like(loss_ref)

        dfg_ref[...] += dfg_t
        dgate_ref[...] += dgate_t
        loss_ref[...] += jnp.broadcast_to(loss_t, loss_ref.shape)

    tok = lambda w: pl.BlockSpec((tm, w), lambda i: (i, 0))
    blk = lambda cb: pl.BlockSpec((tm, CB), lambda i, cb=cb: (i, cb))
    wfull = pl.BlockSpec((D_MODEL, D_MODEL), lambda i: (0, 0), pipeline_mode=pl.Buffered(1))
    row = lambda w: pl.BlockSpec((1, w), lambda i: (0, 0))
    out_shape = (
        jax.ShapeDtypeStruct((T, D_MODEL), F32), jax.ShapeDtypeStruct((T, D_MODEL), BF16),
        jax.ShapeDtypeStruct((T, D_MODEL), BF16), jax.ShapeDtypeStruct((T, D_MODEL), BF16),
        jax.ShapeDtypeStruct((T, D_MODEL), BF16), jax.ShapeDtypeStruct((T, D_MODEL), F32),
        jax.ShapeDtypeStruct((T, D_MODEL), F32), jax.ShapeDtypeStruct((T, 2 * D_MODEL), BF16),
        jax.ShapeDtypeStruct((1, D_MODEL), F32), jax.ShapeDtypeStruct((1, D_MODEL), F32),
        jax.ShapeDtypeStruct((1, 128), F32),
    )
    return pl.pallas_call(
        body, out_shape=out_shape, grid=(T // tm,), name="merge_and_head",
        in_specs=[tok(D_MODEL), tok(D_MODEL), tok(D_MODEL), tok(D_MODEL), blk(CB_MA), blk(CB_MA + 1), blk(CB_MR),
                  blk(CB_MR + 1), wfull, wfull, wfull, row(ADA_W), row(D_MODEL)],
        out_specs=(tok(D_MODEL),) * 7 + (tok(2 * D_MODEL), row(D_MODEL), row(D_MODEL), row(128)),
        compiler_params=_cp("arbitrary"),
    )(x, target, y_attn, y_rnn, proj, proj, proj, proj, wap, wrp, wo, mod_row, final_g)


def _attn_backward(proj, qr_b, kr_b, d_y, tabs, sinks):
    T = proj.shape[0]
    nb = T // BLOCK

    def body(qrb_ref, krc_ref, krp_ref, vc_ref, vp_ref, g0_ref, g1_ref, dy_ref, cc, sac, sbc, cp_, sap, sbp, sink_ref,
             dq_ref, dkv_ref, dg_ref, dsink_ref, carry):
        n = pl.program_id(0)

        @pl.when(n == 0)
        def _():
            carry[...] = jnp.zeros_like(carry)
            dsink_ref[...] = jnp.zeros_like(dsink_ref)

        @pl.when(n < nb)
        def _():
            tc = (_wide(cc[...], D_MODEL), _wide(sac[...], D_MODEL), _wide(sbc[...], D_MODEL))
            tcur = tuple(t[:, :KV_W] for t in tc)
            tprev = (_wide(cp_[...], KV_W), _wide(sap[...], KV_W), _wide(sbp[...], KV_W))
            qr, kr_cur, kr_prev = qrb_ref[...], krc_ref[...], krp_ref[...]
            v_cur, v_prev = vc_ref[...], vp_ref[...]
            g = jnp.concatenate([g0_ref[...], g1_ref[...]], axis=1)
            sg = _sigmoid(g)
            dy = dy_ref[...]
            d_o = dy * (g * sg)
            mask = _attn_mask(n)
            lane = lax.broadcasted_iota(jnp.int32, (1, 128), 1)
            rowg = lax.broadcasted_iota(jnp.int32, (GROUP * BLOCK, 1), 0) // BLOCK
            o_parts, dq_parts = [], []
            dk_cols, dv_cols = [None, None], [None, None]
            dsink = jnp.zeros((1, 128), F32)
            for heads in ((0, 1, 2, 3),):
                k2s = {kh: _kv_pair_operand(kr_prev, kr_cur, kh) for kh in heads}
                v2s = {kh: _kv_pair_operand(v_prev, v_cur, kh) for kh in heads}
                scores = {kh: _attn_scores(qr, k2s[kh], kh) for kh in heads}
                do2s = {kh: _pair_rows(d_o, kh).astype(BF16) for kh in heads}
                dpns = {kh: _fold(_restack(_dot_nt(do2s[kh], v2s[kh])), mask[0]) for kh in heads}
                probs = {kh: _attn_softmax(scores[kh][1], _sink_col(sink_ref, kh), mask) for kh in heads}
                p_bigs = {kh: _unrestack(_unfold(probs[kh][0].astype(BF16), mask[0])) for kh in heads}
                o_bigs = {kh: _dot(p_bigs[kh], v2s[kh]) for kh in heads}
                dv2s = {kh: _dot_tn(p_bigs[kh], do2s[kh]) for kh in heads}
                deltas = {kh: jnp.sum(probs[kh][0] * dpns[kh], axis=-1, keepdims=True) for kh in heads}
                ds_bigs = {kh: _unrestack(_unfold((probs[kh][0] * (dpns[kh] - deltas[kh])).astype(BF16), mask[0]))
                           for kh in heads}
                dq2s = {kh: _dot(ds_bigs[kh], k2s[kh]) for kh in heads}
                dk2s = {kh: _dot_tn(ds_bigs[kh], scores[kh][0]) for kh in heads}
                for kh in heads:
                    o_parts += [o_bigs[kh][0:BLOCK], o_bigs[kh][BLOCK:2 * BLOCK]]
                    dq_parts += [dq2s[kh][0:BLOCK], dq2s[kh][BLOCK:2 * BLOCK]]
                    dk_c, dv_c = _fold_pair(dk2s[kh], kh), _fold_pair(dv2s[kh], kh)
                    c = kh // 2
                    dk_cols[c] = dk_c if dk_cols[c] is None else dk_cols[c] + dk_c
                    dv_cols[c] = dv_c if dv_cols[c] is None else dv_cols[c] + dv_c
                    ds_rows = probs[kh][1] * deltas[kh]
                    for gq in range(GROUP):
                        val = -jnp.sum(jnp.where(rowg == gq, ds_rows, 0.0), axis=0, keepdims=True)
                        dsink = dsink + jnp.where(lane == GROUP * kh + ROW_GROUP_HEAD[gq], val, 0.0)
            o = jnp.concatenate(o_parts, axis=1)
            dg_ref[...] = (dy * o * (sg * (1.0 + g * (1.0 - sg)))).astype(BF16)
            dq_ref[...] = (_unrope(jnp.concatenate(dq_parts, axis=1), *tc) * ATTN_SCALE).astype(BF16)
            dk_all, dv_all = jnp.concatenate(dk_cols, axis=1), jnp.concatenate(dv_cols, axis=1)
            dk_prev = _unrope(dk_all[0:BLOCK], *tprev)
            dk_cur = _unrope(dk_all[BLOCK:2 * BLOCK], *tcur)
            dv_prev, dv_cur = dv_all[0:BLOCK], dv_all[BLOCK:2 * BLOCK]
            dkv_ref[...] = (carry[...] + jnp.concatenate([dk_prev, dv_prev], axis=1)).astype(BF16)
            carry[...] = jnp.concatenate([dk_cur, dv_cur], axis=1)
            dsink_ref[...] += dsink

        @pl.when(n == nb)
        def _():
            dkv_ref[...] = carry[...].astype(BF16)

    cur = lambda w, cb: pl.BlockSpec((BLOCK, w), lambda n, cb=cb: (jnp.minimum(n, nb - 1), cb))
    prev = lambda w, cb: pl.BlockSpec((BLOCK, w), lambda n, cb=cb: (jnp.maximum(jnp.minimum(n, nb - 1) - 1, 0), cb))
    out_shape = (jax.ShapeDtypeStruct((T, D_MODEL), BF16), jax.ShapeDtypeStruct((T, 2 * KV_W), BF16),
                 jax.ShapeDtypeStruct((T, D_MODEL), BF16), jax.ShapeDtypeStruct((1, 128), F32))
    return pl.pallas_call(
        body, out_shape=out_shape, grid=(nb + 1,), name="attn_backward",
        in_specs=[cur(D_MODEL, 0), cur(KV_W, 0), prev(KV_W, 0), cur(KV_W, V_COL_BLOCK), prev(KV_W, V_COL_BLOCK),
                  cur(CB, CB_GA), cur(CB, CB_GA + 1), cur(D_MODEL, 0),
                  cur(128, 0), cur(128, 0), cur(128, 0), prev(128, 0), prev(128, 0), prev(128, 0),
                  pl.BlockSpec(memory_space=pltpu.SMEM)],
        out_specs=(cur(D_MODEL, 0), pl.BlockSpec((BLOCK, 2 * KV_W), lambda n: (jnp.maximum(n - 1, 0), 0)),
                   cur(D_MODEL, 0), pl.BlockSpec((1, 128), lambda n: (0, 0))),
        scratch_shapes=[pltpu.VMEM((BLOCK, 2 * KV_W), F32)],
        compiler_params=_cp("arbitrary"),
    )(qr_b, kr_b, kr_b, proj, proj, proj, proj, d_y, *tabs, *tabs, sinks)


def _rnn_backward(proj, pos_col, h_rnn, saved, d_y, conv_w, rwa, rwx, lam):
    T = proj.shape[0]
    tr = min(T, 256)
    nt = T // tr
    hb = tr // 8

    def body(x0, x1, xh0, xh1, g0, g1, pos_ref, h_ref, hh_ref, xc_ref, r_ref, i_ref, a_ref, mult_ref, dy_ref,
             cw_ref, wa_ref, wx_ref, lam_ref, db_ref, dcw_ref, dcb_ref, dwa_ref, dwx_ref, dba_ref, dbx_ref, dlam_ref,
             xbuf, hbuf, dbuf, gbuf, lbuf, mu_carry, dxc_head):
        step = pl.program_id(0)
        first_tile = step == nt - 1

        @pl.when(step == 0)
        def _():
            mu_carry[...] = jnp.zeros_like(mu_carry)
            dxc_head[...] = jnp.zeros_like(dxc_head)
            for ref in (dcw_ref, dcb_ref, dwa_ref, dwx_ref, dba_ref, dbx_ref, dlam_ref):
                ref[...] = jnp.zeros_like(ref)

        xr = jnp.concatenate([x0[...], x1[...]], axis=1)
        tail = jnp.where(first_tile, 0.0, jnp.concatenate([xh0[...], xh1[...]], axis=1))
        lam_v = lam_ref[...]
        sp = _softplus(-lam_v)
        reset = pos_ref[...] == 0
        cw = cw_ref[...]
        xs = _conv_taps(xbuf, xr, tail)
        xc, r, i, a, mult = xc_ref[...], r_ref[...], i_ref[...], a_ref[...], mult_ref[...]
        xcb = xc.astype(BF16)
        g = jnp.concatenate([g0[...], g1[...]], axis=1)
        sg = _sigmoid(g)
        dy = dy_ref[...]
        h = h_ref[...]
        d_g = dy * h * (sg * (1.0 + g * (1.0 - sg)))
        gbuf[...] = dy * (g * sg)
        top = _scan_backward(a_ref, gbuf, lbuf, mu_carry[0:1, :], tr)
        mu_carry[...] = jnp.broadcast_to(top, mu_carry.shape)
        lam_t = lbuf[...]
        hbuf[0:8, :] = jnp.where(first_tile, 0.0, hh_ref[...])
        hbuf[8:tr + 8, :] = h
        h_prev = hbuf[pl.ds(7, tr), :]
        live = jnp.logical_not(reset)
        d_a = jnp.where(live, lam_t * h_prev, 0.0)
        d_mult = jnp.where(live, lam_t * (i * xc), 0.0)
        d_ixc = lam_t * mult
        d_i = d_ixc * xc
        d_xc = d_ixc * i
        d_log_a = d_a * a - d_mult * (a * a / mult)
        d_za = d_log_a * (-LRU_C * sp) * (r * (1.0 - r))
        d_zx = d_i * (i * (1.0 - i))
        dlam_ref[...] += jnp.sum(d_log_a * r, axis=0, keepdims=True) * (LRU_C * _sigmoid(-lam_v))
        dba_ref[...] += jnp.sum(d_za, axis=0, keepdims=True)
        dbx_ref[...] += jnp.sum(d_zx, axis=0, keepdims=True)
        dzab, dzxb = d_za.astype(BF16), d_zx.astype(BF16)
        back = []
        for j in range(RNN_BLOCKS):
            sl = slice(RNN_BW * j, RNN_BW * (j + 1))
            dwa_ref[j] += _dot_tn(xcb[:, sl], dzab[:, sl])
            dwx_ref[j] += _dot_tn(xcb[:, sl], dzxb[:, sl])
            back.append(_dot_nt(dzab[:, sl], wa_ref[j]) + _dot_nt(dzxb[:, sl], wx_ref[j]))
        d_xc = d_xc + jnp.concatenate(back, axis=1)
        dcb_ref[...] += jnp.sum(d_xc, axis=0, keepdims=True)
        for k in range(CONV_W):
            dcw_ref[k:k + 1, :] += jnp.sum(d_xc * xs[k], axis=0, keepdims=True)
        dbuf[0:tr, :] = d_xc
        dbuf[tr:tr + 8, :] = dxc_head[...]
        d_xr = d_xc * cw[CONV_W - 1:CONV_W, :]
        for k in range(CONV_W - 1):
            d_xr = d_xr + dbuf[pl.ds(CONV_W - 1 - k, tr), :] * cw[k:k + 1, :]
        dxc_head[...] = d_xc[0:8, :]
        db_ref[:, 0:D_MODEL] = d_xr.astype(BF16)
        db_ref[:, D_MODEL:2 * D_MODEL] = d_g.astype(BF16)

    rev = lambda s: nt - 1 - s
    blk = lambda cb: pl.BlockSpec((tr, CB), lambda s, cb=cb: (rev(s), cb))
    halo = lambda w, cb: pl.BlockSpec((8, w), lambda s, cb=cb: (jnp.maximum(rev(s) * hb - 1, 0), cb))
    tok = lambda w: pl.BlockSpec((tr, w), lambda s: (rev(s), 0))
    row = lambda w: pl.BlockSpec((1, w), lambda s: (0, 0))
    full3 = pl.BlockSpec((RNN_BLOCKS, RNN_BW, RNN_BW), lambda s: (0, 0, 0))
    cwspec = pl.BlockSpec((CONV_W, D_MODEL), lambda s: (0, 0))
    vec = jax.ShapeDtypeStruct((1, D_MODEL), F32)
    gate_w = jax.ShapeDtypeStruct((RNN_BLOCKS, RNN_BW, RNN_BW), F32)
    out_shape = (jax.ShapeDtypeStruct((T, 2 * D_MODEL), BF16), jax.ShapeDtypeStruct((CONV_W, D_MODEL), F32), vec,
                 gate_w, gate_w, vec, vec, vec)
    big = lambda: pltpu.VMEM((tr, D_MODEL), F32)
    ext = lambda: pltpu.VMEM((tr + 8, D_MODEL), F32)
    return pl.pallas_call(
        body, out_shape=out_shape, grid=(nt,), name="rnn_backward",
        in_specs=[blk(CB_XR), blk(CB_XR + 1), halo(CB, CB_XR), halo(CB, CB_XR + 1), blk(CB_GR), blk(CB_GR + 1),
                  pl.BlockSpec((tr, 1), lambda s: (rev(s), 0)), tok(D_MODEL), halo(D_MODEL, 0)] + [tok(D_MODEL)] * 6
        + [cwspec, full3, full3, row(D_MODEL)],
        out_specs=(tok(2 * D_MODEL), cwspec, row(D_MODEL), full3, full3, row(D_MODEL), row(D_MODEL), row(D_MODEL)),
        scratch_shapes=[ext(), ext(), ext(), big(), big(), pltpu.VMEM((8, D_MODEL), F32), pltpu.VMEM((8, D_MODEL), F32)],
        compiler_params=_cp("arbitrary"),
    )(proj, proj, proj, proj, proj, proj, pos_col, h_rnn, h_rnn, *saved, d_y, conv_w, rwa, rwx, lam)


def _input_backward(pieces, w_in, x, dx2, mod_row, norm_g):
    T = x.shape[0]
    tm = min(T, 512)
    n = len(pieces)

    def body(*refs):
        d_refs = refs[:n]
        w_ref, x_ref, dx2_ref, mod_ref, g_ref, gx_ref, dshift_ref, dscale_ref, dg_ref = refs[n:]
        i = pl.program_id(0)
        dh = None
        for d_ref, (_, start, count) in zip(d_refs, pieces):
            part = _dot_nt(d_ref[...], w_ref[:, start * CB:(start + count) * CB])
            dh = part if dh is None else dh + part

        @pl.when(i == 0)
        def _():
            dshift_ref[...] = jnp.zeros_like(dshift_ref)
            dscale_ref[...] = jnp.zeros_like(dscale_ref)
            dg_ref[...] = jnp.zeros_like(dg_ref)

        xf = x_ref[...]
        r1 = _rms(xf)
        xn = xf * r1
        gn = g_ref[...]
        s1 = 1.0 + mod_ref[:, D_MODEL:2 * D_MODEL]
        dshift_ref[...] += jnp.sum(dh, axis=0, keepdims=True)
        dscale_ref[...] += jnp.sum(dh * (xn * gn), axis=0, keepdims=True)
        dg_ref[...] += jnp.sum(dh * s1 * xn, axis=0, keepdims=True)
        dxn = dh * s1 * gn
        gx_ref[...] = dx2_ref[...] + r1 * (dxn - xn * jnp.mean(dxn * xn, axis=-1, keepdims=True))

    tok = lambda w: pl.BlockSpec((tm, w), lambda i: (i, 0))
    row = lambda w: pl.BlockSpec((1, w), lambda i: (0, 0))
    vec = jax.ShapeDtypeStruct((1, D_MODEL), F32)
    return pl.pallas_call(
        body, out_shape=(jax.ShapeDtypeStruct((T, D_MODEL), F32), vec, vec, vec), grid=(T // tm,), name="input_backward",
        in_specs=[tok(c * CB) for _, _, c in pieces]
        + [pl.BlockSpec((D_MODEL, IN_W), lambda i: (0, 0), pipeline_mode=pl.Buffered(1)), tok(D_MODEL), tok(D_MODEL),
           row(ADA_W), row(D_MODEL)],
        out_specs=(tok(D_MODEL), row(D_MODEL), row(D_MODEL), row(D_MODEL)),
        compiler_params=_cp("arbitrary"),
    )(*[p[0] for p in pieces], w_in, x, dx2, mod_row, norm_g)


def _weight_grad(a, pieces, tag, a_is_transposed=False):
    M, T = a.shape if a_is_transposed else a.shape[::-1]
    n_blocks = sum(count for _, _, count in pieces)
    n = len(pieces)
    contract = _dot if a_is_transposed else _dot_tn

    def body(*refs):
        a_ref, b_refs, o_ref = refs[0], refs[1:1 + n], refs[-1]
        j = pl.program_id(0)
        for b_ref, (_, start, count) in zip(b_refs, pieces):
            @pl.when((j >= start) & (j < start + count))
            def _(b_ref=b_ref):
                o_ref[...] = contract(a_ref[...], b_ref[...])

    def piece_spec(start, count):
        return pl.BlockSpec((T, CB), lambda j: (0, jnp.clip(j - start, 0, count - 1)))

    return pl.pallas_call(
        body, out_shape=jax.ShapeDtypeStruct((M, n_blocks * CB), F32), grid=(n_blocks,), name=f"weight_grad_{tag}",
        in_specs=[pl.BlockSpec(a.shape, lambda j: (0, 0), pipeline_mode=pl.Buffered(1))] + [piece_spec(s, c) for _, s, c in pieces],
        out_specs=pl.BlockSpec((M, CB), lambda j: (0, j)), compiler_params=_cp("arbitrary"),
    )(a, *[p[0] for p in pieces])


def _adamw(w, g, m, v):
    m = ADAM_B1 * m + (1.0 - ADAM_B1) * g
    v = ADAM_B2 * v + (1.0 - ADAM_B2) * (g * g)
    m_hat = m / (1.0 - ADAM_B1 ** ADAM_STEP)
    v_hat = v / (1.0 - ADAM_B2 ** ADAM_STEP)
    delta = -ADAM_LR * (m_hat / (jnp.sqrt(v_hat) + ADAM_EPS) + ADAM_WD * w)
    return delta, m, v


def _sum_landed(kind, own, land, where, tag):
    if kind == "in":
        R, C = land.shape[1:]
        tr = 256
        grid = (R // tr,)
        own_spec = pl.BlockSpec((tr, C), lambda i, w: (i, w[0]))
        land_spec = pl.BlockSpec((3, tr, C), lambda i, w: (0, i, 0))
        out_spec = pl.BlockSpec((1, tr, C), lambda i, w: (w[1], i, 0))
        out_shape = (2, R, C)
        pick = lambda ref: ref[...]
    elif kind == "sq":
        R, C = land.shape[1:]
        grid = (1,)
        own_spec = pl.BlockSpec((1, R, C), lambda i, w: (w[0], 0, 0))
        land_spec = pl.BlockSpec((3, R, C), lambda i, w: (0, 0, 0))
        out_spec = pl.BlockSpec((1, R, C), lambda i, w: (w[1], 0, 0))
        out_shape = (2, R, C)
        pick = lambda ref: ref[0]
    else:
        B, R, C = land.shape[1:]
        grid = (1,)
        own_spec = pl.BlockSpec((B, 1, R, C), lambda i, w: (0, w[0], 0, 0))
        land_spec = pl.BlockSpec((3, B, R, C), lambda i, w: (0, 0, 0, 0))
        out_spec = pl.BlockSpec((B, 1, R, C), lambda i, w: (0, w[1], 0, 0))
        out_shape = (B, 2, R, C)
        pick = lambda ref: ref[:, 0]

    def body(w_ref, own_ref, l_ref, o_ref):
        total = ((pick(own_ref) + l_ref[0].astype(F32)) + l_ref[1].astype(F32)) + l_ref[2].astype(F32)
        if kind == "in":
            o_ref[0] = total
        elif kind == "sq":
            o_ref[0] = total
        else:
            o_ref[:, 0] = total

    grid_spec = pltpu.PrefetchScalarGridSpec(num_scalar_prefetch=1, grid=grid, in_specs=[own_spec, land_spec], out_specs=out_spec)
    return pl.pallas_call(
        body, out_shape=jax.ShapeDtypeStruct(out_shape, F32), grid_spec=grid_spec, name=f"sum_landed_{tag}",
        compiler_params=_cp("parallel"),
    )(where, own, land)


def _adamw_shard(g, w, m, v, tag):
    R, C = w.shape
    tr = min(R, 256)

    def body(g_ref, w_ref, m_ref, v_ref, d_ref, nm_ref, nv_ref):
        d, nm, nv = _adamw(w_ref[...], g_ref[...], m_ref[...], v_ref[...])
        d_ref[...] = d
        nm_ref[...] = nm
        nv_ref[...] = nv

    spec = pl.BlockSpec((tr, C), lambda i: (i, 0))
    sds = jax.ShapeDtypeStruct((R, C), F32)
    return pl.pallas_call(
        body, out_shape=(sds,) * 3, grid=(R // tr,), name=f"adamw_{tag}",
        in_specs=[spec] * 4, out_specs=(spec,) * 3, compiler_params=_cp("parallel"),
    )(g, w, m, v)


def _adamw_w_ada(c_t, dmod_cols, w, m, v):
    R, C = w.shape

    def body(ct_ref, dm_ref, w_ref, m_ref, v_ref, g_ref, d_ref, nm_ref, nv_ref):
        g = _dot(ct_ref[...].astype(BF16), dm_ref[...].astype(BF16))
        d, nm, nv = _adamw(w_ref[...], g, m_ref[...], v_ref[...])
        g_ref[...] = g
        d_ref[...] = d
        nm_ref[...] = nm
        nv_ref[...] = nv

    tr = 256
    spec = pl.BlockSpec((tr, C), lambda i: (i, 0))
    sds = jax.ShapeDtypeStruct((R, C), F32)
    return pl.pallas_call(
        body, out_shape=(sds,) * 4, grid=(R // tr,), name="adamw_w_ada",
        in_specs=[pl.BlockSpec((tr, 128), lambda i: (i, 0)), pl.BlockSpec((128, C), lambda i: (0, 0))] + [spec] * 3,
        out_specs=(spec,) * 4, compiler_params=_cp("parallel"),
    )(c_t, dmod_cols, w, m, v)


def _adamw_small(small_all, ws, ms, vs):
    def body(s_ref, w_ref, m_ref, v_ref, g_ref, d_ref, nm_ref, nv_ref):
        g = s_ref[0]
        for b in range(1, N_DEV):
            g = g + s_ref[b]
        d, nm, nv = _adamw(w_ref[...], g, m_ref[...], v_ref[...])
        g_ref[...] = g
        d_ref[...] = d
        nm_ref[...] = nm
        nv_ref[...] = nv

    sds = jax.ShapeDtypeStruct((SMALL_ROWS, D_MODEL), F32)
    return pl.pallas_call(
        body, out_shape=(sds,) * 4, name="adamw_small", in_specs=[VMEM_SPEC] * 4, out_specs=(VMEM_SPEC,) * 4,
        compiler_params=pltpu.CompilerParams(vmem_limit_bytes=VMEM_LIMIT_V7X),
    )(small_all, ws, ms, vs)


ROW_MOD, ROW_NORM_G, ROW_CONV_B, ROW_BA, ROW_BX, ROW_LAM, ROW_FINAL_G, ROW_SINKS, ROW_CONV_W, ROW_LOSS = 0, 3, 4, 5, 6, 7, 8, 9, 10, 14


def _pack_small(b_ada, norm_g, conv_b, ba, bx, lam, final_g, sinks, conv_w_full, loss_row=None):
    lane_pad = lambda a: jnp.pad(a.reshape(1, -1), ((0, 0), (0, D_MODEL - a.size)))
    rows = [b_ada.reshape(3, D_MODEL), norm_g, conv_b, ba, bx, lam, final_g.reshape(1, D_MODEL), lane_pad(sinks), conv_w_full,
            jnp.zeros((1, D_MODEL), F32) if loss_row is None else lane_pad(loss_row),
            jnp.zeros((SMALL_ROWS - ROW_LOSS - 1, D_MODEL), F32)]
    return jnp.concatenate([r.astype(F32) for r in rows], axis=0)


def kernel(x, c, positions, w_ada, b_ada, norm_g, w_in, attn_sinks, conv_w, conv_b, rg_wa, rg_ba, rg_wx, rg_bx, rg_lambda, w_attn_proj, w_rnn_proj, w_out, final_g, loss_target, m_w_ada, m_b_ada, m_norm_g, m_w_in, m_attn_sinks, m_conv_w, m_conv_b, m_rg_wa, m_rg_ba, m_rg_wx, m_rg_bx, m_rg_lambda, m_w_attn_proj, m_w_rnn_proj, m_w_out, m_final_g, v_w_ada, v_b_ada, v_norm_g, v_w_in, v_attn_sinks, v_conv_w, v_conv_b, v_rg_wa, v_rg_ba, v_rg_wx, v_rg_bx, v_rg_lambda, v_w_attn_proj, v_w_rnn_proj, v_w_out, v_final_g):
    T = x.shape[1]
    my_chip = lax.axis_index("x") * 2 + lax.axis_index("y")
    my_dev = my_chip * 2 + lax.axis_index("c")
    x2d, tgt = x[0], loss_target[0]
    pos_col = positions.reshape(T, 1)

    chip_idx = my_chip.reshape(1).astype(jnp.int32)
    c_idx = lax.axis_index("c").reshape(1).astype(jnp.int32)
    sq_place = ((D_MODEL, D_MODEL), (SHARD_ROWS, D_MODEL), lambda chip: (chip, 0))
    rg_place = ((RNN_BLOCKS, RNN_BW, RNN_BW), (RNN_BLOCKS, SHARD_RG, RNN_BW), lambda chip: (0, chip, 0))
    placed = [
        _cast_place(w_in[0], chip_idx, (D_MODEL, IN_W), (D_MODEL, SHARD_IN), lambda chip: (0, chip), "w_in"),
        _cast_place(w_attn_proj[0], chip_idx, *sq_place, "w_attn_proj"),
        _cast_place(w_rnn_proj[0], chip_idx, *sq_place, "w_rnn_proj"),
        _cast_place(w_out[0], chip_idx, *sq_place, "w_out"),
        _cast_place(rg_wa[0], chip_idx, *rg_place, "rg_wa"),
        _cast_place(rg_wx[0], chip_idx, *rg_place, "rg_wx"),
    ]
    cw_chips, c_all, mod_chips = _gather_mod(c.reshape(1, 1, D_MODEL), w_ada[0], conv_w[0])
    g_ssems, g_rsems, fulls, g_token = _gather_start([p.reshape(s) for p, s in zip(placed, FULL_SHAPES)], mod_chips)
    conv_w_f = jnp.transpose(cw_chips, (1, 0, 2)).reshape(CONV_W, D_MODEL)
    mod_all = jnp.transpose(mod_chips, (1, 0, 2)).reshape(N_DEV, ADA_W) + b_ada
    mod_row = lax.dynamic_slice_in_dim(mod_all, my_dev, 1, axis=0) + g_token[0:1, 0:1]

    tabs = _rope_tables(pos_col)
    h, h_t = _prenorm(x2d, mod_row, norm_g)
    w_in_v = fulls[0]
    proj = _in_projection(h, w_in_v.reshape(D_MODEL, IN_W), chip_idx, None, "own")
    for k, mask in enumerate(CHIP_MASKS):
        w_in_v = _gather_wait(g_ssems[k], g_rsems[k], [w_in_v], [0], proj, f"w_in_{k}")[0]
        w_in_v = _forward_halves([w_in_v], [(0, 0, k)], f"w_in_{k}")[0]
        from_chip = (chip_idx ^ (mask >> 1)).astype(jnp.int32)
        proj = _in_projection(h, w_in_v.reshape(D_MODEL, IN_W), from_chip, proj, f"from_{k}")
    w_in_f = w_in_v.reshape(D_MODEL, IN_W)
    rest = _gather_wait(g_ssems[3], g_rsems[3], list(fulls[1:]), [1, 2, 3, 4, 5], proj, "rest")
    rest = _forward_halves(rest, [(idx - 1, idx, k) for idx in range(1, N_BIG) for k in range(3)], "rest")
    wap_f, wrp_f, wo_f = (g.reshape(D_MODEL, D_MODEL) for g in rest[0:3])
    rwa_f, rwx_f = (g.reshape(RNN_BLOCKS, RNN_BW, RNN_BW) for g in rest[3:5])
    y_attn, qr_b, kr_b = _attn_forward(proj, tabs, attn_sinks)
    y_rnn, h_rnn, *rnn_saved = _rnn_forward(proj, pos_col, conv_w_f, conv_b, rwa_f, rwx_f, rg_ba, rg_bx, rg_lambda)
    (dx2, merged, d_o, d_pa, d_pr, d_ya, d_yr, d_c, d_final_g, d_gate, loss_vec) = _merge_and_head(
        x2d, tgt, y_attn, y_rnn, proj, wap_f, wrp_f, wo_f, mod_row, final_g.reshape(1, D_MODEL))

    sq = (N_CHIPS, 2, SHARD_ROWS // 2, D_MODEL)
    rg = (RNN_BLOCKS, N_CHIPS, 2, SHARD_RG // 2, RNN_BW)
    rg_flat = (RNN_BLOCKS * N_CHIPS, 2, SHARD_RG // 2, RNN_BW)

    def chip_sum_and_start(views, axes, flat, unflat, tags_, kinds_, group):
        from_sib = _swap_halves(views, axes)
        sums = [_presum(v.reshape(f), s.reshape(f[:1] + f[2:]), c_idx, t) for v, s, f, t in zip(views, from_sib, flat, tags_)]
        exact = [s[0].reshape(u) for s, u in zip(sums, unflat)]
        rounded = [s[1].reshape(u) for s, u in zip(sums, unflat)]
        return _exchange_start(rounded, kinds_, group), exact

    g_ap = _weight_grad(y_attn, [(d_pa, 0, 2)], "w_attn_proj")
    g_rp = _weight_grad(y_rnn, [(d_pr, 0, 2)], "w_rnn_proj")
    g_o = _weight_grad(merged, [(d_o, 0, 2)], "w_out")
    sq_half = (N_CHIPS, SHARD_ROWS // 2, D_MODEL)
    started1, own1 = chip_sum_and_start([g_ap.reshape(sq), g_rp.reshape(sq), g_o.reshape(sq)], [1, 1, 1], [sq] * 3, [sq_half] * 3,
                                  ["w_attn_proj", "w_rnn_proj", "w_out"], ["sq"] * 3, "proj")
    d_q, d_kv, d_ga, d_sinks = _attn_backward(proj, qr_b, kr_b, d_ya, tabs, attn_sinks + started1[4][0, 0])
    d_b, d_conv_w, d_conv_b, d_rwa, d_rwx, d_ba, d_bx, d_lam = _rnn_backward(
        proj, pos_col, h_rnn, rnn_saved, d_yr, conv_w_f, rwa_f, rwx_f, rg_lambda)
    pieces = [(d_q, CB_Q, 2), (d_kv, CB_KV, 1), (d_ga, CB_GA, 2), (d_b, CB_XR, 4), (d_c, CB_MA, 4)]
    g_in = _weight_grad(h_t, pieces, "w_in", a_is_transposed=True)
    started2, own2 = chip_sum_and_start(
        [g_in.reshape(2, D_MODEL // 2, IN_W), d_rwa.reshape(rg), d_rwx.reshape(rg)], [0, 2, 2],
        [(1, 2, D_MODEL // 2, IN_W), rg_flat, rg_flat],
        [(D_MODEL // 2, IN_W), (RNN_BLOCKS, N_CHIPS, SHARD_RG // 2, RNN_BW), (RNN_BLOCKS, N_CHIPS, SHARD_RG // 2, RNN_BW)],
        ["w_in", "rg_wa", "rg_wx"], ["in", "rg", "rg"], "in")
    grad_x, d_shift, d_scale, d_norm_g = _input_backward(pieces, w_in_f, x2d, dx2, mod_row + started2[4][0, 0], norm_g)

    d_mod = jnp.concatenate([d_shift, d_scale, d_gate], axis=1)
    small = _pack_small(d_mod, d_norm_g, d_conv_b, d_ba, d_bx, d_lam, d_final_g, d_sinks[:, :N_HEADS], d_conv_w, loss_vec)
    small_all = _gather_small(small)
    _, lands1 = _exchange_wait(*started1[:4], grad_x, "proj")
    _, lands2 = _exchange_wait(*started2[:4], grad_x, "in")
    tags = ["w_in", "w_attn_proj", "w_rnn_proj", "w_out", "rg_wa", "rg_wx"]
    chip_sums = [own2[0]] + list(own1) + list(own2[1:])
    lands = [lands2[0]] + list(lands1) + list(lands2[1:])
    where = jnp.concatenate([chip_idx, c_idx])
    kinds = ["in", "sq", "sq", "sq", "rg", "rg"]
    halves = [_sum_landed(kinds[i], chip_sums[i], lands[i], where, tags[i]) for i in range(6)]
    grads = _assemble_with_sibling(halves, [0, 0, 0, 0, 1, 1])
    shapes2d = [(D_MODEL, SHARD_IN), (SHARD_ROWS, D_MODEL), (SHARD_ROWS, D_MODEL), (SHARD_ROWS, D_MODEL),
                (RNN_BLOCKS * SHARD_RG, RNN_BW), (RNN_BLOCKS * SHARD_RG, RNN_BW)]
    big_w = [w_in, w_attn_proj, w_rnn_proj, w_out, rg_wa, rg_wx]
    big_m = [m_w_in, m_w_attn_proj, m_w_rnn_proj, m_w_out, m_rg_wa, m_rg_wx]
    big_v = [v_w_in, v_w_attn_proj, v_w_rnn_proj, v_w_out, v_rg_wa, v_rg_wx]
    res = {}
    for i, tag in enumerate(tags):
        g = grads[i].reshape(shapes2d[i])
        outs = _adamw_shard(g, big_w[i].reshape(shapes2d[i]), big_m[i].reshape(shapes2d[i]), big_v[i].reshape(shapes2d[i]), tag)
        res[tag] = [o.reshape(big_w[i].shape) for o in (g,) + tuple(outs)]

    dmod_all = small_all[:, ROW_MOD:ROW_MOD + 3, :].reshape(N_DEV, ADA_W)
    dmod_cols = lax.dynamic_slice_in_dim(dmod_all, my_chip * SHARD_ADA, SHARD_ADA, axis=1)
    c_t = jnp.pad(jnp.transpose(c_all.reshape(N_DEV, D_MODEL)), ((0, 0), (0, 128 - N_DEV)))
    dmod_cols = jnp.pad(dmod_cols, ((0, 128 - N_DEV), (0, 0)))
    res["w_ada"] = [o.reshape(w_ada.shape) for o in _adamw_w_ada(c_t, dmod_cols, w_ada[0], m_w_ada[0], v_w_ada[0])]

    def full_conv(a):
        return lax.dynamic_update_slice_in_dim(jnp.zeros((CONV_W, D_MODEL), F32), a[0], my_chip * (D_MODEL // N_CHIPS), axis=1)

    packed = [_pack_small(p[0], p[1], p[2], p[3], p[4], p[5], p[6], p[7], full_conv(p[8])) for p in (
        (b_ada, norm_g, conv_b, rg_ba, rg_bx, rg_lambda, final_g, attn_sinks, conv_w),
        (m_b_ada, m_norm_g, m_conv_b, m_rg_ba, m_rg_bx, m_rg_lambda, m_final_g, m_attn_sinks, m_conv_w),
        (v_b_ada, v_norm_g, v_conv_b, v_rg_ba, v_rg_bx, v_rg_lambda, v_final_g, v_attn_sinks, v_conv_w))]
    small_out = _adamw_small(small_all, *packed)

    def unpack(slab):
        cw = lax.dynamic_slice_in_dim(slab[ROW_CONV_W:ROW_CONV_W + CONV_W], my_chip * (D_MODEL // N_CHIPS),
                                      D_MODEL // N_CHIPS, axis=1)
        return {
            "b_ada": slab[ROW_MOD:ROW_MOD + 3].reshape(1, ADA_W), "norm_g": slab[ROW_NORM_G:ROW_NORM_G + 1],
            "conv_b": slab[ROW_CONV_B:ROW_CONV_B + 1], "rg_ba": slab[ROW_BA:ROW_BA + 1], "rg_bx": slab[ROW_BX:ROW_BX + 1],
            "rg_lambda": slab[ROW_LAM:ROW_LAM + 1], "final_g": slab[ROW_FINAL_G], "attn_sinks": slab[ROW_SINKS:ROW_SINKS + 1, :N_HEADS],
            "conv_w": cw[None],
        }

    small_res = [unpack(s) for s in small_out]
    order = ["w_ada", "b_ada", "norm_g", "w_in", "attn_sinks", "conv_w", "conv_b", "rg_wa", "rg_ba", "rg_wx", "rg_bx",
             "rg_lambda", "w_attn_proj", "w_rnn_proj", "w_out", "final_g"]
    loss = small_out[0][ROW_LOSS, 0]
    outs = [loss, grad_x[None]]
    for kind in range(4):
        for name in order:
            outs.append(res[name][kind] if name in res else small_res[kind][name])
    return tuple(outs)
```

```python
import numpy as np
import jax
import jax.numpy as jnp
from jax import lax
from jax.experimental import pallas as pl
from jax.experimental.pallas import tpu as pltpu

F32 = jnp.float32
BF16 = jnp.bfloat16

D_MODEL = 1024
N_HEADS = 16
N_KV = 4
HEAD_DIM = 64
GROUP = N_HEADS // N_KV
BLOCK = 128
KV_W = N_KV * HEAD_DIM
ROT_HALF = 8
ROPE_THETA = 500000.0
ATTN_SCALE = 0.125
RNN_BLOCKS = 4
RNN_BW = 256
CONV_W = 4
LRU_C = 8.0
NORM_EPS = 1e-6
IN_W = 6656
CB = 512
N_CB = IN_W // CB
CB_Q, CB_KV, CB_GA, CB_XR, CB_GR, CB_MA, CB_MR = 0, 2, 3, 5, 7, 9, 11
V_COL_BLOCK = 5
N_CHIPS = 4
N_DEV = 8
SHARD_IN = IN_W // N_CHIPS
SHARD_ROWS = D_MODEL // N_CHIPS
SHARD_RG = RNN_BW // N_CHIPS
ADA_W = 3 * D_MODEL
SHARD_ADA = ADA_W // N_CHIPS
SMALL_ROWS = 16

ADAM_LR = 0.001
ADAM_B1 = 0.9
ADAM_B2 = 0.999
ADAM_EPS = 1e-08
ADAM_WD = 0.01
ADAM_STEP = 10

VMEM_LIMIT_V7X = 52 * 1024 * 1024
MESH = pl.DeviceIdType.MESH
ANY = pl.BlockSpec(memory_space=pl.ANY)
VMEM_SPEC = pl.BlockSpec(memory_space=pltpu.VMEM)


def _cp(*sem):
    return pltpu.CompilerParams(dimension_semantics=sem if sem else None, vmem_limit_bytes=VMEM_LIMIT_V7X)


def _dot(a, b):
    return jnp.dot(a, b, preferred_element_type=F32)


def _dot_nt(a, b):
    return lax.dot_general(a, b, (((1,), (1,)), ((), ())), preferred_element_type=F32)


def _dot_tn(a, b):
    return lax.dot_general(a, b, (((0,), (0,)), ((), ())), preferred_element_type=F32)


def _sigmoid(z):
    return 1.0 / (1.0 + jnp.exp(-z))


def _softplus(z):
    u = jnp.exp(-jnp.abs(z))
    log1p_u = jnp.where(u < 1e-3, u * (1.0 - u * (0.5 - u * (1.0 / 3.0))), jnp.log(1.0 + u))
    return jnp.maximum(z, 0.0) + log1p_u


def _rms(xf):
    return lax.rsqrt(jnp.mean(xf * xf, axis=-1, keepdims=True) + NORM_EPS)


def _me():
    return lax.axis_index("x"), lax.axis_index("y"), lax.axis_index("c")


def _peer(mask):
    x, y, c = _me()
    fx, fy, fc = (mask >> 2) & 1, (mask >> 1) & 1, mask & 1
    return (x ^ fx if fx else x, y ^ fy if fy else y, c ^ fc if fc else c)


def _chip_of(pos):
    return pos[0] * 2 + pos[1]


SIBLING_COLLECTIVE_ID = 0
SIBLING_ONLY = pltpu.CompilerParams(collective_id=SIBLING_COLLECTIVE_ID)


def _sibling_handshake():
    barrier = pltpu.get_barrier_semaphore()
    pl.semaphore_signal(barrier, inc=1, device_id=_peer(1), device_id_type=MESH)
    pl.semaphore_wait(barrier, 1)


CHIP_MASKS = (4, 2, 6)
ALL_MASKS = (1, 2, 3, 4, 5, 6, 7)


HBM_SPEC = pl.BlockSpec(memory_space=pltpu.HBM)
SEM_SPEC = pl.BlockSpec(memory_space=pltpu.SEMAPHORE)
SPLIT_COPY = pltpu.CompilerParams(has_side_effects=pltpu.SideEffectType.DATAFLOW_SIDE_EFFECTING)
N_BIG = 6
FULL_SHAPES = (
    (2, D_MODEL // 2, IN_W),
    (N_CHIPS, 2, SHARD_ROWS // 2, D_MODEL), (N_CHIPS, 2, SHARD_ROWS // 2, D_MODEL), (N_CHIPS, 2, SHARD_ROWS // 2, D_MODEL),
    (RNN_BLOCKS, N_CHIPS, 2, SHARD_RG // 2, RNN_BW), (RNN_BLOCKS, N_CHIPS, 2, SHARD_RG // 2, RNN_BW),
)


def _slot(full, idx, chip, half):
    if idx == 0:
        return full.at[half, :, pl.ds(pl.multiple_of(chip * SHARD_IN, 128), SHARD_IN)]
    return full.at[chip, half] if idx in (1, 2, 3) else full.at[:, chip, half]


def _three_halves(full, idx):
    return full.at[pl.ds(0, 3), 0] if idx in (1, 2, 3) else full.at[:, pl.ds(0, 3), 0]


def _gather_start(fulls, after):
    def body(*refs):
        full_refs = refs[:N_BIG]
        ssems, rsems = refs[N_BIG + 1:N_BIG + 5], refs[N_BIG + 5:N_BIG + 9]
        token = refs[2 * N_BIG + 9]
        me = _me()
        my_chip = _chip_of(me)
        for idx in range(N_BIG):
            for k, mask in enumerate(CHIP_MASKS):
                pair = k if idx == 0 else 3
                mine = _slot(full_refs[idx], idx, my_chip, me[2])
                pltpu.make_async_remote_copy(src_ref=mine, dst_ref=mine, send_sem=ssems[pair], recv_sem=rsems[pair],
                                             device_id=_peer(mask), device_id_type=MESH).start()
        token[...] = jnp.zeros_like(token)

    sem = pltpu.SemaphoreType.DMA(())
    out_shape = (sem,) * 8 + tuple(pltpu.HBM(f.shape, f.dtype) for f in fulls) + (jax.ShapeDtypeStruct((8, 128), F32),)
    outs = pl.pallas_call(
        body, out_shape=out_shape, name="gather_start",
        in_specs=[HBM_SPEC] * N_BIG + [ANY], out_specs=tuple([SEM_SPEC] * 8 + [HBM_SPEC] * N_BIG + [VMEM_SPEC]),
        input_output_aliases={i: 8 + i for i in range(N_BIG)}, compiler_params=SPLIT_COPY,
    )(*[pltpu.with_memory_space_constraint(f, pltpu.HBM) for f in fulls], after)
    return outs[0:4], outs[4:8], outs[8:8 + N_BIG], outs[8 + N_BIG]


def _gather_wait(ssem, rsem, arrays, idxs, after, tag):
    n = len(arrays)

    def body(*refs):
        full_refs, ssem_ref, rsem_ref = refs[:n], refs[n], refs[n + 1]
        me = _me()
        for full, idx in zip(full_refs, idxs):
            region = _slot(full, 0, _chip_of(me), me[2]) if idx == 0 else _three_halves(full, idx)
            arrived = pltpu.make_async_remote_copy(
                src_ref=region, dst_ref=region, send_sem=ssem_ref, recv_sem=rsem_ref, device_id=me, device_id_type=MESH)
            arrived.wait_send()
            arrived.wait_recv()

    outs = pl.pallas_call(
        body, out_shape=tuple(pltpu.HBM(a.shape, a.dtype) for a in arrays), name=f"gather_wait_{tag}",
        in_specs=[HBM_SPEC] * n + [SEM_SPEC, SEM_SPEC, ANY], out_specs=tuple([HBM_SPEC] * n),
        input_output_aliases={i: i for i in range(n)}, compiler_params=SPLIT_COPY,
    )(*arrays, ssem, rsem, after)
    return list(outs)


def _forward_halves(arrays, items, tag):
    n, m = len(arrays), len(items)

    def body(*refs):
        outs, ssem, rsem = refs[n:2 * n], refs[2 * n], refs[2 * n + 1]
        me = _me()
        sib = _peer(1)
        _sibling_handshake()
        cps = []
        for j, (pos, idx, k) in enumerate(items):
            chip = _chip_of(_peer(CHIP_MASKS[k]))
            cp = pltpu.make_async_remote_copy(
                src_ref=_slot(outs[pos], idx, chip, me[2]), dst_ref=_slot(outs[pos], idx, chip, me[2]),
                send_sem=ssem.at[j], recv_sem=rsem.at[j], device_id=sib, device_id_type=MESH)
            cp.start()
            cps.append(cp)
        for j, (pos, idx, k) in enumerate(items):
            chip = _chip_of(_peer(CHIP_MASKS[k]))
            pltpu.make_async_remote_copy(
                src_ref=_slot(outs[pos], idx, chip, me[2]), dst_ref=_slot(outs[pos], idx, chip, 1 - me[2]),
                send_sem=ssem.at[j], recv_sem=rsem.at[j], device_id=sib, device_id_type=MESH).wait_recv()
        for cp in cps:
            cp.wait_send()

    outs = pl.pallas_call(
        body, out_shape=tuple(jax.ShapeDtypeStruct(a.shape, a.dtype) for a in arrays), name=f"forward_halves_{tag}",
        in_specs=[ANY] * n, out_specs=tuple([ANY] * n), input_output_aliases={i: i for i in range(n)},
        scratch_shapes=[pltpu.SemaphoreType.DMA((m,)), pltpu.SemaphoreType.DMA((m,))], compiler_params=SIBLING_ONLY,
    )(*arrays)
    return list(outs)


def _gather_mod(c_row, w_ada_s, conv_w_s):
    def body(c_ref, wada_ref, cw_s, cw_f, call_ref, mod_ref, wsend, wrecv, lsem, csend, crecv, msend, mrecv):
        me = _me()
        my_chip = _chip_of(me)
        my_dev = my_chip * 2 + me[2]
        sends = []
        for k, mask in enumerate(CHIP_MASKS):
            cp = pltpu.make_async_remote_copy(src_ref=cw_s, dst_ref=cw_f.at[my_chip], send_sem=wsend.at[k], recv_sem=wrecv.at[k],
                                              device_id=_peer(mask), device_id_type=MESH)
            cp.start()
            sends.append(cp)
        local = [pltpu.make_async_copy(cw_s, cw_f.at[my_chip], lsem.at[0])]
        for cp in local:
            cp.start()

        call_ref[my_dev] = c_ref[0]
        csends = []
        for k, mask in enumerate(ALL_MASKS):
            cp = pltpu.make_async_remote_copy(
                src_ref=c_ref.at[0], dst_ref=call_ref.at[my_dev],
                send_sem=csend.at[k], recv_sem=crecv.at[k], device_id=_peer(mask), device_id_type=MESH)
            cp.start()
            csends.append(cp)
        for k, mask in enumerate(ALL_MASKS):
            frm = _peer(mask)
            pltpu.make_async_remote_copy(
                src_ref=c_ref.at[0], dst_ref=call_ref.at[_chip_of(frm) * 2 + frm[2]],
                send_sem=csend.at[k], recv_sem=crecv.at[k], device_id=frm, device_id_type=MESH).wait_recv()
        for cp in csends:
            cp.wait_send()

        c_all = call_ref[...].reshape(N_DEV, D_MODEL).astype(BF16)
        mod_ref[my_chip] = _dot(c_all, wada_ref[...].astype(BF16))
        msends = []
        for k, mask in enumerate(CHIP_MASKS):
            cp = pltpu.make_async_remote_copy(
                src_ref=mod_ref.at[my_chip], dst_ref=mod_ref.at[my_chip],
                send_sem=msend.at[k], recv_sem=mrecv.at[k], device_id=_peer(mask), device_id_type=MESH)
            cp.start()
            msends.append(cp)
        for k, mask in enumerate(CHIP_MASKS):
            frm = _peer(mask)
            pltpu.make_async_remote_copy(
                src_ref=mod_ref.at[my_chip], dst_ref=mod_ref.at[_chip_of(frm)],
                send_sem=msend.at[k], recv_sem=mrecv.at[k], device_id=frm, device_id_type=MESH).wait_recv()
        for cp in msends:
            cp.wait_send()

        for k, mask in enumerate(CHIP_MASKS):
            frm = _peer(mask)
            pltpu.make_async_remote_copy(src_ref=cw_s, dst_ref=cw_f.at[_chip_of(frm)], send_sem=wsend.at[k], recv_sem=wrecv.at[k],
                                         device_id=frm, device_id_type=MESH).wait_recv()
        for cp in sends:
            cp.wait_send()
        for cp in local:
            cp.wait()

    out_shape = (
        jax.ShapeDtypeStruct((N_CHIPS, CONV_W, D_MODEL // N_CHIPS), F32),
        jax.ShapeDtypeStruct((N_DEV, 1, D_MODEL), F32),
        jax.ShapeDtypeStruct((N_CHIPS, N_DEV, SHARD_ADA), F32),
    )
    return pl.pallas_call(
        body, out_shape=out_shape, name="gather_mod",
        in_specs=[VMEM_SPEC, VMEM_SPEC, ANY], out_specs=(ANY, VMEM_SPEC, VMEM_SPEC),
        scratch_shapes=[
            pltpu.SemaphoreType.DMA((3,)), pltpu.SemaphoreType.DMA((3,)), pltpu.SemaphoreType.DMA((1,)),
            pltpu.SemaphoreType.DMA((7,)), pltpu.SemaphoreType.DMA((7,)),
            pltpu.SemaphoreType.DMA((3,)), pltpu.SemaphoreType.DMA((3,)),
        ],
        compiler_params=pltpu.CompilerParams(vmem_limit_bytes=VMEM_LIMIT_V7X),
    )(c_row, w_ada_s, conv_w_s)


def _cast_place(shard, chip_idx, full_shape, block, index_map, tag):
    def body(chip_ref, s_ref, o_ref):
        o_ref[...] = s_ref[...].astype(BF16)

    grid_spec = pltpu.PrefetchScalarGridSpec(
        num_scalar_prefetch=1, grid=(1,),
        in_specs=[pl.BlockSpec(shard.shape, lambda i, chip_ref: (0,) * shard.ndim)],
        out_specs=pl.BlockSpec(block, lambda i, chip_ref: index_map(chip_ref[0])))
    return pl.pallas_call(
        body, out_shape=jax.ShapeDtypeStruct(full_shape, BF16), grid_spec=grid_spec, name=f"cast_place_{tag}",
        compiler_params=_cp("arbitrary"),
    )(chip_idx, shard)


def _shard_of(ref, kind, chip):
    if kind == "in":
        return ref.at[:, pl.ds(pl.multiple_of(chip * SHARD_IN, 128), SHARD_IN)]
    return ref.at[chip] if kind == "sq" else ref.at[:, chip]


def _land_shape(src, kind):
    if kind == "in":
        return (3, src.shape[0], SHARD_IN)
    return (3,) + src.shape[1:] if kind == "sq" else (3, src.shape[0]) + src.shape[2:]


def _exchange_start(srcs, kinds, tag):
    n = len(srcs)
    lands = [pltpu.with_memory_space_constraint(lax.empty(_land_shape(s, k), s.dtype), pltpu.HBM) for s, k in zip(srcs, kinds)]

    def body(*refs):
        src_refs, land_refs = refs[:n], refs[n:2 * n]
        ssems, rsems = refs[2 * n:3 * n], refs[3 * n:4 * n]
        token = refs[6 * n]
        for i in range(n):
            for k, mask in enumerate(CHIP_MASKS):
                to = _peer(mask)
                pltpu.make_async_remote_copy(
                    src_ref=_shard_of(src_refs[i], kinds[i], _chip_of(to)), dst_ref=land_refs[i].at[k],
                    send_sem=ssems[i], recv_sem=rsems[i], device_id=to, device_id_type=MESH).start()
        token[...] = jnp.zeros_like(token)

    sem = pltpu.SemaphoreType.DMA(())
    out_shape = ((sem,) * (2 * n) + tuple(pltpu.HBM(s.shape, s.dtype) for s in srcs)
                 + tuple(pltpu.HBM(l.shape, l.dtype) for l in lands) + (jax.ShapeDtypeStruct((8, 128), F32),))
    outs = pl.pallas_call(
        body, out_shape=out_shape, name=f"exchange_start_{tag}",
        in_specs=[HBM_SPEC] * (2 * n), out_specs=tuple([SEM_SPEC] * (2 * n) + [HBM_SPEC] * (2 * n) + [VMEM_SPEC]),
        input_output_aliases={i: 2 * n + i for i in range(2 * n)},
        compiler_params=pltpu.CompilerParams(has_side_effects=pltpu.SideEffectType.DATAFLOW_SIDE_EFFECTING),
    )(*[pltpu.with_memory_space_constraint(s, pltpu.HBM) for s in srcs], *lands)
    return outs[:n], outs[n:2 * n], outs[2 * n:3 * n], outs[3 * n:4 * n], outs[4 * n]


def _exchange_wait(ssems, rsems, srcs, lands, after, tag):
    n = len(srcs)

    def body(*refs):
        land_refs = refs[n:2 * n]
        ssem_refs, rsem_refs = refs[2 * n:3 * n], refs[3 * n:4 * n]
        for i in range(n):
            all_three = pltpu.make_async_remote_copy(
                src_ref=land_refs[i], dst_ref=land_refs[i], send_sem=ssem_refs[i], recv_sem=rsem_refs[i],
                device_id=_me(), device_id_type=MESH)
            all_three.wait_send()
            all_three.wait_recv()

    outs = pl.pallas_call(
        body, out_shape=tuple(pltpu.HBM(a.shape, a.dtype) for a in list(srcs) + list(lands)), name=f"exchange_wait_{tag}",
        in_specs=[HBM_SPEC] * (2 * n) + [SEM_SPEC] * (2 * n) + [ANY], out_specs=tuple([HBM_SPEC] * (2 * n)),
        input_output_aliases={i: i for i in range(2 * n)},
        compiler_params=pltpu.CompilerParams(has_side_effects=pltpu.SideEffectType.DATAFLOW_SIDE_EFFECTING),
    )(*srcs, *lands, *ssems, *rsems, after)
    return outs[:n], outs[n:]


def _gather_small(small):
    def body(small_ref, small_all, ssend, srecv):
        me = _me()
        my_dev = _chip_of(me) * 2 + me[2]
        small_all[my_dev] = small_ref[...]
        ssends = []
        for k, mask in enumerate(ALL_MASKS):
            cp = pltpu.make_async_remote_copy(
                src_ref=small_ref, dst_ref=small_all.at[my_dev],
                send_sem=ssend.at[k], recv_sem=srecv.at[k], device_id=_peer(mask), device_id_type=MESH)
            cp.start()
            ssends.append(cp)
        for k, mask in enumerate(ALL_MASKS):
            frm = _peer(mask)
            pltpu.make_async_remote_copy(
                src_ref=small_ref, dst_ref=small_all.at[_chip_of(frm) * 2 + frm[2]],
                send_sem=ssend.at[k], recv_sem=srecv.at[k], device_id=frm, device_id_type=MESH).wait_recv()
        for cp in ssends:
            cp.wait_send()

    return pl.pallas_call(
        body, out_shape=jax.ShapeDtypeStruct((N_DEV, SMALL_ROWS, D_MODEL), F32), name="gather_small",
        in_specs=[VMEM_SPEC], out_specs=VMEM_SPEC,
        scratch_shapes=[pltpu.SemaphoreType.DMA((7,)), pltpu.SemaphoreType.DMA((7,))],
    )(small)


def _half_of(ref, axis, half):
    return ref.at[(slice(None),) * axis + (half,)]


def _swap_halves(parts, axes):
    n = len(parts)

    def body(*refs):
        ins, outs, ssem, rsem = refs[:n], refs[n:2 * n], refs[2 * n], refs[2 * n + 1]
        c = lax.axis_index("c")
        _sibling_handshake()
        cps = [pltpu.make_async_remote_copy(src_ref=_half_of(ins[i], axes[i], 1 - c), dst_ref=outs[i], send_sem=ssem.at[i],
                                            recv_sem=rsem.at[i], device_id=_peer(1), device_id_type=MESH) for i in range(n)]
        for cp in cps:
            cp.start()
        for cp in cps:
            cp.wait()

    shapes = [p.shape[:a] + p.shape[a + 1:] for p, a in zip(parts, axes)]
    return pl.pallas_call(
        body, out_shape=tuple(jax.ShapeDtypeStruct(s, p.dtype) for s, p in zip(shapes, parts)), name="swap_halves",
        in_specs=[ANY] * n, out_specs=tuple([ANY] * n),
        scratch_shapes=[pltpu.SemaphoreType.DMA((n,)), pltpu.SemaphoreType.DMA((n,))], compiler_params=SIBLING_ONLY,
    )(*parts)


def _presum(mine, sib, c_idx, tag):
    S, _, R, C = mine.shape
    tr = min(R, 256)
    tc = SHARD_IN if C % SHARD_IN == 0 else C

    def body(c_ref, m_ref, s_ref, o_ref, ob_ref):
        total = m_ref[:, 0] + s_ref[...]
        o_ref[...] = total
        ob_ref[...] = total.astype(BF16)

    out_spec = pl.BlockSpec((S, tr, tc), lambda i, j, c_ref: (0, i, j))
    grid_spec = pltpu.PrefetchScalarGridSpec(
        num_scalar_prefetch=1, grid=(R // tr, C // tc),
        in_specs=[pl.BlockSpec((S, 1, tr, tc), lambda i, j, c_ref: (0, c_ref[0], i, j)),
                  pl.BlockSpec((S, tr, tc), lambda i, j, c_ref: (0, i, j))],
        out_specs=(out_spec, out_spec))
    return pl.pallas_call(
        body, out_shape=(jax.ShapeDtypeStruct((S, R, C), F32), jax.ShapeDtypeStruct((S, R, C), BF16)),
        grid_spec=grid_spec, name=f"presum_{tag}", compiler_params=_cp("parallel", "parallel"),
    )(c_idx, mine, sib)


def _assemble_with_sibling(parts, axes):
    n = len(parts)

    def body(*refs):
        outs, ssem, rsem = refs[n:2 * n], refs[2 * n], refs[2 * n + 1]
        c = lax.axis_index("c")
        _sibling_handshake()
        cps = [pltpu.make_async_remote_copy(
            src_ref=_half_of(outs[i], axes[i], c), dst_ref=_half_of(outs[i], axes[i], c), send_sem=ssem.at[i],
            recv_sem=rsem.at[i], device_id=_peer(1), device_id_type=MESH) for i in range(n)]
        for cp in cps:
            cp.start()
        for i in range(n):
            pltpu.make_async_remote_copy(
                src_ref=_half_of(outs[i], axes[i], c), dst_ref=_half_of(outs[i], axes[i], 1 - c), send_sem=ssem.at[i],
                recv_sem=rsem.at[i], device_id=_peer(1), device_id_type=MESH).wait_recv()
        for cp in cps:
            cp.wait_send()

    return pl.pallas_call(
        body, out_shape=tuple(jax.ShapeDtypeStruct(p.shape, p.dtype) for p in parts), name="assemble_with_sibling",
        in_specs=[ANY] * n, out_specs=tuple([ANY] * n), input_output_aliases={i: i for i in range(n)},
        scratch_shapes=[pltpu.SemaphoreType.DMA((n,)), pltpu.SemaphoreType.DMA((n,))], compiler_params=SIBLING_ONLY,
    )(*parts)


def _rope_lane_frequencies():
    inv = np.float32(ROPE_THETA) ** (-(np.arange(0, 2 * ROT_HALF, 2, dtype=np.float32)) / np.float32(2 * ROT_HALF))
    lane = np.arange(128) % HEAD_DIM
    return jnp.asarray(np.where(lane < 2 * ROT_HALF, inv[lane % ROT_HALF], 0.0).astype(np.float32)[None, :])


def _rope_tables(pos, freq):
    ang = pos.astype(F32) * freq
    c, s = jnp.cos(ang), jnp.sin(ang)
    m = lax.broadcasted_iota(jnp.int32, ang.shape, 1) & (HEAD_DIM - 1)
    return (jnp.where(m < 2 * ROT_HALF, c, 1.0), jnp.where(m < ROT_HALF, -s, 0.0),
            jnp.where((m >= ROT_HALF) & (m < 2 * ROT_HALF), s, 0.0))


def _columns(t):
    return [t[:, i:i + 128] for i in range(0, t.shape[-1], 128)]


def _rope(t, c, sa, sb):
    return jnp.concatenate(
        [x * c + pltpu.roll(x, 128 - ROT_HALF, 1) * sa + pltpu.roll(x, ROT_HALF, 1) * sb for x in _columns(t)], axis=1)


def _unrope(d, c, sa, sb):
    return jnp.concatenate(
        [x * c + pltpu.roll(x * sa, ROT_HALF, 1) + pltpu.roll(x * sb, 128 - ROT_HALF, 1) for x in _columns(d)], axis=1)


def _prenorm(x, mod_row, norm_g, pos_col):
    T = x.shape[0]
    tm = min(T, 512)

    def body(x_ref, mod_ref, g_ref, pos_ref, f_ref, h_ref, ht_ref, c_ref, sa_ref, sb_ref):
        xf = x_ref[...]
        shift, scale = mod_ref[:, 0:D_MODEL], mod_ref[:, D_MODEL:2 * D_MODEL]
        h = (xf * _rms(xf)) * g_ref[...] * (1.0 + scale) + shift
        h_ref[...] = h.astype(BF16)
        ht_ref[...] = h.T.astype(BF16)
        c_ref[...], sa_ref[...], sb_ref[...] = _rope_tables(pos_ref[...], f_ref[...])

    tab = jax.ShapeDtypeStruct((T, 128), F32)
    tok = lambda w: pl.BlockSpec((tm, w), lambda i: (i, 0))
    row = lambda w: pl.BlockSpec((1, w), lambda i: (0, 0))
    outs = pl.pallas_call(
        body, out_shape=(jax.ShapeDtypeStruct((T, D_MODEL), BF16), jax.ShapeDtypeStruct((D_MODEL, T), BF16), tab, tab, tab),
        grid=(T // tm,), name="prenorm",
        in_specs=[tok(D_MODEL), row(ADA_W), row(D_MODEL), tok(1), row(128)],
        out_specs=(tok(D_MODEL), pl.BlockSpec((D_MODEL, tm), lambda i: (0, i)), tok(128), tok(128), tok(128)),
        compiler_params=_cp("parallel"),
    )(x, mod_row, norm_g, pos_col, _rope_lane_frequencies())
    return outs[0], outs[1], tuple(outs[2:])


def _in_projection(h, w_in, chip, into, tag):
    T = h.shape[0]
    tm, tn = min(T, 512), SHARD_IN

    def body(chip_ref, h_ref, w_ref, *rest):
        rest[-1][...] = _dot(h_ref[...], w_ref[...])

    in_specs = [pl.BlockSpec((tm, D_MODEL), lambda i, c: (i, 0)),
                pl.BlockSpec((D_MODEL, tn), lambda i, c: (0, c[0]), pipeline_mode=pl.Buffered(1))]
    args = [chip, h, w_in]
    aliases = {}
    if into is not None:
        in_specs.append(ANY)
        args.append(into)
        aliases = {3: 0}
    grid_spec = pltpu.PrefetchScalarGridSpec(num_scalar_prefetch=1, grid=(T // tm,), in_specs=in_specs,
                                             out_specs=pl.BlockSpec((tm, tn), lambda i, c: (i, c[0])))
    return pl.pallas_call(
        body, out_shape=jax.ShapeDtypeStruct((T, IN_W), F32), grid_spec=grid_spec, name=f"in_projection_{tag}",
        input_output_aliases=aliases, compiler_params=_cp("parallel"),
    )(*args)


def _attn_mask(n):
    qi = lax.broadcasted_iota(jnp.int32, (GROUP * BLOCK, BLOCK), 0) & (BLOCK - 1)
    j = lax.broadcasted_iota(jnp.int32, (GROUP * BLOCK, BLOCK), 1)
    own = j <= qi
    return own, jnp.logical_not(own) & (n == 0)


def _fold(x, own):
    return jnp.where(own, x[:, BLOCK:2 * BLOCK], x[:, 0:BLOCK])


def _unfold(xf, own):
    zero = jnp.zeros_like(xf)
    return jnp.concatenate([jnp.where(own, zero, xf), jnp.where(own, xf, zero)], axis=1)


ROW_GROUP_HEAD = (0, 2, 1, 3)


def _sink_col(sink_ref, kh):
    rowg = lax.broadcasted_iota(jnp.int32, (GROUP * BLOCK, 1), 0) // BLOCK
    col = jnp.full((GROUP * BLOCK, 1), sink_ref[0, GROUP * kh + ROW_GROUP_HEAD[0]], F32)
    for g in range(1, GROUP):
        col = jnp.where(rowg == g, sink_ref[0, GROUP * kh + ROW_GROUP_HEAD[g]], col)
    return col


def _low_lanes(shape):
    return lax.broadcasted_iota(jnp.int32, shape, 1) < HEAD_DIM


def _kv_pair_operand(prev, cur, kh):
    c = 128 * (kh // 2)
    col = jnp.concatenate([prev[:, c:c + 128], cur[:, c:c + 128]], axis=0).astype(F32)
    if kh % 2 == 0:
        lo = jnp.where(_low_lanes(col.shape), col, 0.0)
        hi = pltpu.roll(lo, HEAD_DIM, 1)
    else:
        hi = jnp.where(_low_lanes(col.shape), 0.0, col)
        lo = pltpu.roll(hi, HEAD_DIM, 1)
    return jnp.concatenate([lo, hi], axis=0).astype(BF16)


def _pair_rows(x, kh):
    c = 2 * 128 * kh
    return jnp.concatenate([x[:, c:c + 128], x[:, c + 128:c + 256]], axis=0)


def _restack(big):
    return jnp.concatenate([big[:, 0:2 * BLOCK], big[:, 2 * BLOCK:4 * BLOCK]], axis=0)


def _unrestack(stacked):
    return jnp.concatenate([stacked[0:2 * BLOCK], stacked[2 * BLOCK:4 * BLOCK]], axis=1)


def _fold_pair(x2, kh):
    low = _low_lanes((2 * BLOCK, 128))
    mixed = jnp.where(low, x2[0:2 * BLOCK], x2[2 * BLOCK:4 * BLOCK])
    total = mixed + pltpu.roll(mixed, HEAD_DIM, 1)
    return jnp.where(low, total, 0.0) if kh % 2 == 0 else jnp.where(low, 0.0, total)


def _attn_scores(qr, k2, kh):
    q2 = _pair_rows(qr, kh).astype(BF16)
    return q2, _restack(_dot_nt(q2, k2))


def _attn_softmax(s, sink_col, mask):
    own, no_key = mask
    s = jnp.where(no_key, -1e30, _fold(s, own))
    m = jnp.maximum(jnp.max(s, axis=-1, keepdims=True), sink_col)
    p = jnp.exp(s - m)
    p_sink = jnp.exp(sink_col - m)
    denom = jnp.sum(p, axis=-1, keepdims=True) + p_sink
    return p / denom, p_sink / denom


def _attn_forward(proj, tabs, sinks):
    T = proj.shape[0]
    nb = T // BLOCK

    def body(q_ref, kvc_ref, kvp_ref, g0_ref, g1_ref, cc, sac, sbc, cp_, sap, sbp, sink_ref, y_ref, qrb_ref, krb_ref):
        n = pl.program_id(0)
        tc = tcur = (cc[...], sac[...], sbc[...])
        tprev = (cp_[...], sap[...], sbp[...])
        qr = _rope(q_ref[...], *tc) * ATTN_SCALE
        kr_cur = _rope(kvc_ref[:, 0:KV_W], *tcur)
        kr_prev = _rope(kvp_ref[:, 0:KV_W], *tprev)
        qrb_ref[...] = qr.astype(BF16)
        krb_ref[...] = kr_cur.astype(BF16)
        v_cur, v_prev = kvc_ref[:, KV_W:2 * KV_W], kvp_ref[:, KV_W:2 * KV_W]
        mask = _attn_mask(n)
        outs = []
        k2s = [_kv_pair_operand(kr_prev, kr_cur, kh) for kh in range(N_KV)]
        v2s = [_kv_pair_operand(v_prev, v_cur, kh) for kh in range(N_KV)]
        scores = [_attn_scores(qr, k2s[kh], kh) for kh in range(N_KV)]
        for kh in range(N_KV):
            pn, _ = _attn_softmax(scores[kh][1], _sink_col(sink_ref, kh), mask)
            o_big = _dot(_unrestack(_unfold(pn.astype(BF16), mask[0])), v2s[kh])
            outs += [o_big[0:BLOCK], o_big[BLOCK:2 * BLOCK]]
        o = jnp.concatenate(outs, axis=1)
        g = jnp.concatenate([g0_ref[...], g1_ref[...]], axis=1)
        y_ref[...] = (o * (g * _sigmoid(g))).astype(BF16)

    def blk(w, cb):
        return pl.BlockSpec((BLOCK, w), lambda n, cb=cb: (n, cb))

    prev = lambda w, cb: pl.BlockSpec((BLOCK, w), lambda n, cb=cb: (jnp.maximum(n - 1, 0), cb))
    return pl.pallas_call(
        body, grid=(nb,), name="attn_forward",
        out_shape=(jax.ShapeDtypeStruct((T, D_MODEL), BF16), jax.ShapeDtypeStruct((T, D_MODEL), BF16),
                   jax.ShapeDtypeStruct((T, KV_W), BF16)),
        in_specs=[blk(D_MODEL, 0), blk(CB, CB_KV), prev(CB, CB_KV), blk(CB, CB_GA), blk(CB, CB_GA + 1),
                  blk(128, 0), blk(128, 0), blk(128, 0), prev(128, 0), prev(128, 0), prev(128, 0),
                  pl.BlockSpec(memory_space=pltpu.SMEM)],
        out_specs=(blk(D_MODEL, 0), blk(D_MODEL, 0), blk(KV_W, 0)),
        compiler_params=_cp("parallel"),
    )(proj, proj, proj, proj, proj, *tabs, *tabs, sinks)


def _scan_rows8():
    return lax.broadcasted_iota(jnp.int32, (8, D_MODEL), 0)


def _scan_forward(a_ref, b_ref, h_ref, carry, rows):
    row = _scan_rows8()

    def group(i, carry):
        off = pl.multiple_of(i * 8, 8)
        a, b = a_ref[pl.ds(off, 8), :], b_ref[pl.ds(off, 8), :]
        for d in (1, 2, 4):
            ok = row >= d
            b = jnp.where(ok, a * pltpu.roll(b, d, 0) + b, b)
            a = jnp.where(ok, a * pltpu.roll(a, d, 0), a)
        h = a * carry + b
        h_ref[pl.ds(off, 8), :] = h
        return h[7:8, :]

    return lax.fori_loop(0, rows // 8, group, carry)


def _scan_backward(a_ref, g_ref, lam_ref, carry, rows):
    row = _scan_rows8()

    def group(i, carry):
        off = pl.multiple_of((rows // 8 - 1 - i) * 8, 8)
        a, g = a_ref[pl.ds(off, 8), :], g_ref[pl.ds(off, 8), :]
        b = a * g
        for d in (1, 2, 4):
            ok = row < 8 - d
            b = jnp.where(ok, a * pltpu.roll(b, 8 - d, 0) + b, b)
            a = jnp.where(ok, a * pltpu.roll(a, 8 - d, 0), a)
        mu = a * carry + b
        mu_below = jnp.where(row == 7, carry, pltpu.roll(mu, 7, 0))
        lam_ref[pl.ds(off, 8), :] = g + mu_below
        return mu[0:1, :]

    return lax.fori_loop(0, rows // 8, group, carry)


def _conv_taps(xbuf, xr, tail):
    rows = xr.shape[0]
    xbuf[0:8, :] = tail
    xbuf[8:rows + 8, :] = xr
    return [xbuf[pl.ds(8 - (CONV_W - 1 - k), rows), :] for k in range(CONV_W - 1)] + [xr]


def _rnn_gates(xbuf, xr, tail, cw, cb, wa_ref, wx_ref, ba, bx, sp, reset):
    xs = _conv_taps(xbuf, xr, tail)
    xc = xs[0] * cw[0:1, :]
    for k in range(1, CONV_W):
        xc = xc + xs[k] * cw[k:k + 1, :]
    xc = xc + cb
    xcb = xc.astype(BF16)
    za = jnp.concatenate([_dot(xcb[:, RNN_BW * j:RNN_BW * (j + 1)], wa_ref[j]) for j in range(RNN_BLOCKS)], axis=1) + ba
    zx = jnp.concatenate([_dot(xcb[:, RNN_BW * j:RNN_BW * (j + 1)], wx_ref[j]) for j in range(RNN_BLOCKS)], axis=1) + bx
    r, i = _sigmoid(za), _sigmoid(zx)
    neg_log_a = LRU_C * r * sp
    a_raw = jnp.exp(-neg_log_a)
    mult_raw = jnp.sqrt(jnp.tanh(neg_log_a) * (1.0 + a_raw * a_raw))
    a = jnp.where(reset, 0.0, a_raw)
    mult = jnp.where(reset, 1.0, mult_raw)
    return xc, r, i, a, mult


def _rnn_forward(proj, pos_col, conv_w, conv_b, rwa, rwx, ba, bx, lam):
    T = proj.shape[0]
    tr = min(T, 256)

    def body(x0, x1, g0, g1, pos_ref, cw_ref, cb_ref, wa_ref, wx_ref, ba_ref, bx_ref, lam_ref,
             y_ref, h_ref, xc_ref, r_ref, i_ref, a_ref, mult_ref, xbuf, bbuf, tail, carry):
        t = pl.program_id(0)

        @pl.when(t == 0)
        def _():
            tail[...] = jnp.zeros_like(tail)
            carry[...] = jnp.zeros_like(carry)

        xr = jnp.concatenate([x0[...], x1[...]], axis=1)
        sp = _softplus(-lam_ref[...])
        reset = pos_ref[...] == 0
        xc, r, i, a, mult = _rnn_gates(
            xbuf, xr, tail[...], cw_ref[...], cb_ref[...], wa_ref, wx_ref, ba_ref[...], bx_ref[...], sp, reset)
        xc_ref[...] = xc
        r_ref[...] = r
        i_ref[...] = i
        a_ref[...] = a
        mult_ref[...] = mult
        bbuf[...] = mult * (i * xc)
        last = _scan_forward(a_ref, bbuf, h_ref, carry[0:1, :], tr)
        carry[...] = jnp.broadcast_to(last, carry.shape)
        tail[...] = xr[tr - 8:tr, :]
        g = jnp.concatenate([g0[...], g1[...]], axis=1)
        y_ref[...] = (h_ref[...] * (g * _sigmoid(g))).astype(BF16)

    blk = lambda cb: pl.BlockSpec((tr, CB), lambda t, cb=cb: (t, cb))
    row = lambda w: pl.BlockSpec((1, w), lambda t: (0, 0))
    full3 = pl.BlockSpec((RNN_BLOCKS, RNN_BW, RNN_BW), lambda t: (0, 0, 0))
    tok = pl.BlockSpec((tr, D_MODEL), lambda t: (t, 0))
    act = jax.ShapeDtypeStruct((T, D_MODEL), F32)
    return pl.pallas_call(
        body, out_shape=(jax.ShapeDtypeStruct((T, D_MODEL), BF16),) + (act,) * 6,
        grid=(T // tr,), name="rnn_forward",
        in_specs=[blk(CB_XR), blk(CB_XR + 1), blk(CB_GR), blk(CB_GR + 1), pl.BlockSpec((tr, 1), lambda t: (t, 0)),
                  pl.BlockSpec((CONV_W, D_MODEL), lambda t: (0, 0)), row(D_MODEL), full3, full3,
                  row(D_MODEL), row(D_MODEL), row(D_MODEL)],
        out_specs=(tok,) * 7,
        scratch_shapes=[pltpu.VMEM((tr + 8, D_MODEL), F32), pltpu.VMEM((tr, D_MODEL), F32),
                        pltpu.VMEM((8, D_MODEL), F32), pltpu.VMEM((8, D_MODEL), F32)],
        compiler_params=_cp("arbitrary"),
    )(proj, proj, proj, proj, pos_col, conv_w, conv_b, rwa, rwx, ba, bx, lam)


ROW_PARTS = 1


def _merge_and_head(x, target, y_attn, y_rnn, proj, wap, wrp, wo, mod_row, final_g):
    T = x.shape[0]
    tm = min(T, 256)

    def body(x_ref, t_ref, ya_ref, yr_ref, ma0, ma1, mr0, mr1, wap_ref, wrp_ref, wo_ref, mod_ref, fg_ref,
             dx2_ref, mg_ref, do_ref, dpa_ref, dpr_ref, dya_ref, dyr_ref, dc_ref, dfg_ref, dgate_ref, loss_ref):
        i = pl.program_id(0)
        gate = mod_ref[:, 2 * D_MODEL:3 * D_MODEL]
        fg = fg_ref[...]
        parts = [slice(p * (tm // ROW_PARTS), (p + 1) * (tm // ROW_PARTS)) for p in range(ROW_PARTS)]
        each = range(ROW_PARTS)
        pa = [_dot(ya_ref[r, :], wap_ref[...]) for r in parts]
        pr = [_dot(yr_ref[r, :], wrp_ref[...]) for r in parts]
        sa = [_sigmoid(jnp.concatenate([ma0[r, :], ma1[r, :]], axis=1)) for r in parts]
        sr = [_sigmoid(jnp.concatenate([mr0[r, :], mr1[r, :]], axis=1)) for r in parts]
        mb = [(sa[p] * pa[p] + sr[p] * pr[p]).astype(BF16) for p in each]
        o = [_dot(mb[p], wo_ref[...]) for p in each]
        x2 = [x_ref[r, :] + gate * o[p] for p, r in enumerate(parts)]
        r2 = [_rms(v) for v in x2]
        xn2 = [x2[p] * r2[p] for p in each]
        err = [xn2[p] * fg - t_ref[r, :] for p, r in enumerate(parts)]
        dy = [e * (1.0 / D_MODEL) for e in err]
        dxn = [d * fg for d in dy]
        dx2 = [r2[p] * (dxn[p] - xn2[p] * jnp.mean(dxn[p] * xn2[p], axis=-1, keepdims=True)) for p in each]
        dob = [(dx2[p] * gate).astype(BF16) for p in each]
        dmerged = [_dot_nt(d, wo_ref[...]) for d in dob]
        dpa = [(dmerged[p] * sa[p]).astype(BF16) for p in each]
        dpr = [(dmerged[p] * sr[p]).astype(BF16) for p in each]
        dya = [_dot_nt(d, wap_ref[...]) for d in dpa]
        dyr = [_dot_nt(d, wrp_ref[...]) for d in dpr]
        loss_t, dfg_t, dgate_t = 0.0, 0.0, 0.0
        for p, r in enumerate(parts):
            dx2_ref[r, :] = dx2[p]
            mg_ref[r, :] = mb[p]
            do_ref[r, :] = dob[p]
            dpa_ref[r, :] = dpa[p]
            dpr_ref[r, :] = dpr[p]
            dya_ref[r, :] = dya[p]
            dyr_ref[r, :] = dyr[p]
            dc_ref[r, 0:D_MODEL] = (dmerged[p] * pa[p] * sa[p] * (1.0 - sa[p])).astype(BF16)
            dc_ref[r, D_MODEL:2 * D_MODEL] = (dmerged[p] * pr[p] * sr[p] * (1.0 - sr[p])).astype(BF16)
            loss_t = loss_t + 0.5 * jnp.sum(
                jnp.sum(err[p] * err[p], axis=-1, keepdims=True) * (1.0 / D_MODEL), axis=0, keepdims=True)
            dfg_t = dfg_t + jnp.sum(dy[p] * xn2[p], axis=0, keepdims=True)
            dgate_t = dgate_t + jnp.sum(dx2[p] * o[p], axis=0, keepdims=True)

        @pl.when(i == 0)
        def _():
            dfg_ref[...] = jnp.zeros_like(dfg_ref)
            dgate_ref[...] = jnp.zeros_like(dgate_ref)
            loss_ref[...] = jnp.zeros_like(loss_ref)

        dfg_ref[...] += dfg_t
        dgate_ref[...] += dgate_t
        loss_ref[...] += jnp.broadcast_to(loss_t, loss_ref.shape)

    tok = lambda w: pl.BlockSpec((tm, w), lambda i: (i, 0))
    blk = lambda cb: pl.BlockSpec((tm, CB), lambda i, cb=cb: (i, cb))
    wfull = pl.BlockSpec((D_MODEL, D_MODEL), lambda i: (0, 0), pipeline_mode=pl.Buffered(1))
    row = lambda w: pl.BlockSpec((1, w), lambda i: (0, 0))
    out_shape = (
        jax.ShapeDtypeStruct((T, D_MODEL), F32), jax.ShapeDtypeStruct((T, D_MODEL), BF16),
        jax.ShapeDtypeStruct((T, D_MODEL), BF16), jax.ShapeDtypeStruct((T, D_MODEL), BF16),
        jax.ShapeDtypeStruct((T, D_MODEL), BF16), jax.ShapeDtypeStruct((T, D_MODEL), F32),
        jax.ShapeDtypeStruct((T, D_MODEL), F32), jax.ShapeDtypeStruct((T, 2 * D_MODEL), BF16),
        jax.ShapeDtypeStruct((1, D_MODEL), F32), jax.ShapeDtypeStruct((1, D_MODEL), F32),
        jax.ShapeDtypeStruct((1, 128), F32),
    )
    return pl.pallas_call(
        body, out_shape=out_shape, grid=(T // tm,), name="merge_and_head",
        in_specs=[tok(D_MODEL), tok(D_MODEL), tok(D_MODEL), tok(D_MODEL), blk(CB_MA), blk(CB_MA + 1), blk(CB_MR),
                  blk(CB_MR + 1), wfull, wfull, wfull, row(ADA_W), row(D_MODEL)],
        out_specs=(tok(D_MODEL),) * 7 + (tok(2 * D_MODEL), row(D_MODEL), row(D_MODEL), row(128)),
        compiler_params=_cp("arbitrary"),
    )(x, target, y_attn, y_rnn, proj, proj, proj, proj, wap, wrp, wo, mod_row, final_g)


def _attn_backward(proj, qr_b, kr_b, d_y, tabs, sinks):
    T = proj.shape[0]
    nb = T // BLOCK

    def body(qrb_ref, krc_ref, krp_ref, vc_ref, vp_ref, g0_ref, g1_ref, dy_ref, cc, sac, sbc, cp_, sap, sbp, sink_ref,
             dq_ref, dkv_ref, dg_ref, dsink_ref, carry):
        n = pl.program_id(0)

        @pl.when(n == 0)
        def _():
            carry[...] = jnp.zeros_like(carry)
            dsink_ref[...] = jnp.zeros_like(dsink_ref)

        @pl.when(n < nb)
        def _():
            tc = tcur = (cc[...], sac[...], sbc[...])
            tprev = (cp_[...], sap[...], sbp[...])
            qr, kr_cur, kr_prev = qrb_ref[...], krc_ref[...], krp_ref[...]
            v_cur, v_prev = vc_ref[...], vp_ref[...]
            g = jnp.concatenate([g0_ref[...], g1_ref[...]], axis=1)
            sg = _sigmoid(g)
            dy = dy_ref[...]
            d_o = dy * (g * sg)
            mask = _attn_mask(n)
            lane = lax.broadcasted_iota(jnp.int32, (1, 128), 1)
            rowg = lax.broadcasted_iota(jnp.int32, (GROUP * BLOCK, 1), 0) // BLOCK
            o_parts, dq_parts = [], []
            dk_cols, dv_cols = [None, None], [None, None]
            dsink = jnp.zeros((1, 128), F32)
            for heads in ((0, 1, 2, 3),):
                k2s = {kh: _kv_pair_operand(kr_prev, kr_cur, kh) for kh in heads}
                v2s = {kh: _kv_pair_operand(v_prev, v_cur, kh) for kh in heads}
                scores = {kh: _attn_scores(qr, k2s[kh], kh) for kh in heads}
                do2s = {kh: _pair_rows(d_o, kh).astype(BF16) for kh in heads}
                dpns = {kh: _fold(_restack(_dot_nt(do2s[kh], v2s[kh])), mask[0]) for kh in heads}
                probs = {kh: _attn_softmax(scores[kh][1], _sink_col(sink_ref, kh), mask) for kh in heads}
                p_bigs = {kh: _unrestack(_unfold(probs[kh][0].astype(BF16), mask[0])) for kh in heads}
                o_bigs = {kh: _dot(p_bigs[kh], v2s[kh]) for kh in heads}
                dv2s = {kh: _dot_tn(p_bigs[kh], do2s[kh]) for kh in heads}
                deltas = {kh: jnp.sum(probs[kh][0] * dpns[kh], axis=-1, keepdims=True) for kh in heads}
                ds_bigs = {kh: _unrestack(_unfold((probs[kh][0] * (dpns[kh] - deltas[kh])).astype(BF16), mask[0]))
                           for kh in heads}
                dq2s = {kh: _dot(ds_bigs[kh], k2s[kh]) for kh in heads}
                dk2s = {kh: _dot_tn(ds_bigs[kh], scores[kh][0]) for kh in heads}
                for kh in heads:
                    o_parts += [o_bigs[kh][0:BLOCK], o_bigs[kh][BLOCK:2 * BLOCK]]
                    dq_parts += [dq2s[kh][0:BLOCK], dq2s[kh][BLOCK:2 * BLOCK]]
                    dk_c, dv_c = _fold_pair(dk2s[kh], kh), _fold_pair(dv2s[kh], kh)
                    c = kh // 2
                    dk_cols[c] = dk_c if dk_cols[c] is None else dk_cols[c] + dk_c
                    dv_cols[c] = dv_c if dv_cols[c] is None else dv_cols[c] + dv_c
                    ds_rows = probs[kh][1] * deltas[kh]
                    for gq in range(GROUP):
                        val = -jnp.sum(jnp.where(rowg == gq, ds_rows, 0.0), axis=0, keepdims=True)
                        dsink = dsink + jnp.where(lane == GROUP * kh + ROW_GROUP_HEAD[gq], val, 0.0)
            o = jnp.concatenate(o_parts, axis=1)
            dg_ref[...] = (dy * o * (sg * (1.0 + g * (1.0 - sg)))).astype(BF16)
            dq_ref[...] = (_unrope(jnp.concatenate(dq_parts, axis=1), *tc) * ATTN_SCALE).astype(BF16)
            dk_all, dv_all = jnp.concatenate(dk_cols, axis=1), jnp.concatenate(dv_cols, axis=1)
            dk_prev = _unrope(dk_all[0:BLOCK], *tprev)
            dk_cur = _unrope(dk_all[BLOCK:2 * BLOCK], *tcur)
            dv_prev, dv_cur = dv_all[0:BLOCK], dv_all[BLOCK:2 * BLOCK]
            dkv_ref[...] = (carry[...] + jnp.concatenate([dk_prev, dv_prev], axis=1)).astype(BF16)
            carry[...] = jnp.concatenate([dk_cur, dv_cur], axis=1)
            dsink_ref[...] += dsink

        @pl.when(n == nb)
        def _():
            dkv_ref[...] = carry[...].astype(BF16)

    cur = lambda w, cb: pl.BlockSpec((BLOCK, w), lambda n, cb=cb: (jnp.minimum(n, nb - 1), cb))
    prev = lambda w, cb: pl.BlockSpec((BLOCK, w), lambda n, cb=cb: (jnp.maximum(jnp.minimum(n, nb - 1) - 1, 0), cb))
    out_shape = (jax.ShapeDtypeStruct((T, D_MODEL), BF16), jax.ShapeDtypeStruct((T, 2 * KV_W), BF16),
                 jax.ShapeDtypeStruct((T, D_MODEL), BF16), jax.ShapeDtypeStruct((1, 128), F32))
    return pl.pallas_call(
        body, out_shape=out_shape, grid=(nb + 1,), name="attn_backward",
        in_specs=[cur(D_MODEL, 0), cur(KV_W, 0), prev(KV_W, 0), cur(KV_W, V_COL_BLOCK), prev(KV_W, V_COL_BLOCK),
                  cur(CB, CB_GA), cur(CB, CB_GA + 1), cur(D_MODEL, 0),
                  cur(128, 0), cur(128, 0), cur(128, 0), prev(128, 0), prev(128, 0), prev(128, 0),
                  pl.BlockSpec(memory_space=pltpu.SMEM)],
        out_specs=(cur(D_MODEL, 0), pl.BlockSpec((BLOCK, 2 * KV_W), lambda n: (jnp.maximum(n - 1, 0), 0)),
                   cur(D_MODEL, 0), pl.BlockSpec((1, 128), lambda n: (0, 0))),
        scratch_shapes=[pltpu.VMEM((BLOCK, 2 * KV_W), F32)],
        compiler_params=_cp("arbitrary"),
    )(qr_b, kr_b, kr_b, proj, proj, proj, proj, d_y, *tabs, *tabs, sinks)


def _rnn_backward(proj, pos_col, h_rnn, saved, d_y, conv_w, rwa, rwx, lam):
    T = proj.shape[0]
    tr = min(T, 256)
    nt = T // tr
    hb = tr // 8

    def body(x0, x1, xh0, xh1, g0, g1, pos_ref, h_ref, hh_ref, xc_ref, r_ref, i_ref, a_ref, mult_ref, dy_ref,
             cw_ref, wa_ref, wx_ref, lam_ref, db_ref, dcw_ref, dcb_ref, dwa_ref, dwx_ref, dba_ref, dbx_ref, dlam_ref,
             xbuf, hbuf, dbuf, gbuf, lbuf, mu_carry, dxc_head):
        step = pl.program_id(0)
        first_tile = step == nt - 1

        @pl.when(step == 0)
        def _():
            mu_carry[...] = jnp.zeros_like(mu_carry)
            dxc_head[...] = jnp.zeros_like(dxc_head)
            for ref in (dcw_ref, dcb_ref, dwa_ref, dwx_ref, dba_ref, dbx_ref, dlam_ref):
                ref[...] = jnp.zeros_like(ref)

        xr = jnp.concatenate([x0[...], x1[...]], axis=1)
        tail = jnp.where(first_tile, 0.0, jnp.concatenate([xh0[...], xh1[...]], axis=1))
        lam_v = lam_ref[...]
        sp = _softplus(-lam_v)
        reset = pos_ref[...] == 0
        cw = cw_ref[...]
        xs = _conv_taps(xbuf, xr, tail)
        xc, r, i, a, mult = xc_ref[...], r_ref[...], i_ref[...], a_ref[...], mult_ref[...]
        xcb = xc.astype(BF16)
        g = jnp.concatenate([g0[...], g1[...]], axis=1)
        sg = _sigmoid(g)
        dy = dy_ref[...]
        h = h_ref[...]
        d_g = dy * h * (sg * (1.0 + g * (1.0 - sg)))
        gbuf[...] = dy * (g * sg)
        top = _scan_backward(a_ref, gbuf, lbuf, mu_carry[0:1, :], tr)
        mu_carry[...] = jnp.broadcast_to(top, mu_carry.shape)
        lam_t = lbuf[...]
        hbuf[0:8, :] = jnp.where(first_tile, 0.0, hh_ref[...])
        hbuf[8:tr + 8, :] = h
        h_prev = hbuf[pl.ds(7, tr), :]
        live = jnp.logical_not(reset)
        d_a = jnp.where(live, lam_t * h_prev, 0.0)
        d_mult = jnp.where(live, lam_t * (i * xc), 0.0)
        d_ixc = lam_t * mult
        d_i = d_ixc * xc
        d_xc = d_ixc * i
        d_log_a = d_a * a - d_mult * (a * a / mult)
        d_za = d_log_a * (-LRU_C * sp) * (r * (1.0 - r))
        d_zx = d_i * (i * (1.0 - i))
        dlam_ref[...] += jnp.sum(d_log_a * r, axis=0, keepdims=True) * (LRU_C * _sigmoid(-lam_v))
        dba_ref[...] += jnp.sum(d_za, axis=0, keepdims=True)
        dbx_ref[...] += jnp.sum(d_zx, axis=0, keepdims=True)
        dzab, dzxb = d_za.astype(BF16), d_zx.astype(BF16)
        back = []
        for j in range(RNN_BLOCKS):
            sl = slice(RNN_BW * j, RNN_BW * (j + 1))
            dwa_ref[j] += _dot_tn(xcb[:, sl], dzab[:, sl])
            dwx_ref[j] += _dot_tn(xcb[:, sl], dzxb[:, sl])
            back.append(_dot_nt(dzab[:, sl], wa_ref[j]) + _dot_nt(dzxb[:, sl], wx_ref[j]))
        d_xc = d_xc + jnp.concatenate(back, axis=1)
        dcb_ref[...] += jnp.sum(d_xc, axis=0, keepdims=True)
        for k in range(CONV_W):
            dcw_ref[k:k + 1, :] += jnp.sum(d_xc * xs[k], axis=0, keepdims=True)
        dbuf[0:tr, :] = d_xc
        dbuf[tr:tr + 8, :] = dxc_head[...]
        d_xr = d_xc * cw[CONV_W - 1:CONV_W, :]
        for k in range(CONV_W - 1):
            d_xr = d_xr + dbuf[pl.ds(CONV_W - 1 - k, tr), :] * cw[k:k + 1, :]
        dxc_head[...] = d_xc[0:8, :]
        db_ref[:, 0:D_MODEL] = d_xr.astype(BF16)
        db_ref[:, D_MODEL:2 * D_MODEL] = d_g.astype(BF16)

    rev = lambda s: nt - 1 - s
    blk = lambda cb: pl.BlockSpec((tr, CB), lambda s, cb=cb: (rev(s), cb))
    halo = lambda w, cb: pl.BlockSpec((8, w), lambda s, cb=cb: (jnp.maximum(rev(s) * hb - 1, 0), cb))
    tok = lambda w: pl.BlockSpec((tr, w), lambda s: (rev(s), 0))
    row = lambda w: pl.BlockSpec((1, w), lambda s: (0, 0))
    full3 = pl.BlockSpec((RNN_BLOCKS, RNN_BW, RNN_BW), lambda s: (0, 0, 0))
    cwspec = pl.BlockSpec((CONV_W, D_MODEL), lambda s: (0, 0))
    vec = jax.ShapeDtypeStruct((1, D_MODEL), F32)
    gate_w = jax.ShapeDtypeStruct((RNN_BLOCKS, RNN_BW, RNN_BW), F32)
    out_shape = (jax.ShapeDtypeStruct((T, 2 * D_MODEL), BF16), jax.ShapeDtypeStruct((CONV_W, D_MODEL), F32), vec,
                 gate_w, gate_w, vec, vec, vec)
    big = lambda: pltpu.VMEM((tr, D_MODEL), F32)
    ext = lambda: pltpu.VMEM((tr + 8, D_MODEL), F32)
    return pl.pallas_call(
        body, out_shape=out_shape, grid=(nt,), name="rnn_backward",
        in_specs=[blk(CB_XR), blk(CB_XR + 1), halo(CB, CB_XR), halo(CB, CB_XR + 1), blk(CB_GR), blk(CB_GR + 1),
                  pl.BlockSpec((tr, 1), lambda s: (rev(s), 0)), tok(D_MODEL), halo(D_MODEL, 0)] + [tok(D_MODEL)] * 6
        + [cwspec, full3, full3, row(D_MODEL)],
        out_specs=(tok(2 * D_MODEL), cwspec, row(D_MODEL), full3, full3, row(D_MODEL), row(D_MODEL), row(D_MODEL)),
        scratch_shapes=[ext(), ext(), ext(), big(), big(), pltpu.VMEM((8, D_MODEL), F32), pltpu.VMEM((8, D_MODEL), F32)],
        compiler_params=_cp("arbitrary"),
    )(proj, proj, proj, proj, proj, proj, pos_col, h_rnn, h_rnn, *saved, d_y, conv_w, rwa, rwx, lam)


def _input_backward(pieces, w_in, x, dx2, mod_row, norm_g):
    T = x.shape[0]
    tm = min(T, 512)
    n = len(pieces)

    def body(*refs):
        d_refs = refs[:n]
        w_ref, x_ref, dx2_ref, mod_ref, g_ref, gx_ref, dshift_ref, dscale_ref, dg_ref = refs[n:]
        i = pl.program_id(0)
        dh = None
        for d_ref, (_, start, count) in zip(d_refs, pieces):
            part = _dot_nt(d_ref[...], w_ref[:, start * CB:(start + count) * CB])
            dh = part if dh is None else dh + part

        @pl.when(i == 0)
        def _():
            dshift_ref[...] = jnp.zeros_like(dshift_ref)
            dscale_ref[...] = jnp.zeros_like(dscale_ref)
            dg_ref[...] = jnp.zeros_like(dg_ref)

        xf = x_ref[...]
        r1 = _rms(xf)
        xn = xf * r1
        gn = g_ref[...]
        s1 = 1.0 + mod_ref[:, D_MODEL:2 * D_MODEL]
        dshift_ref[...] += jnp.sum(dh, axis=0, keepdims=True)
        dscale_ref[...] += jnp.sum(dh * (xn * gn), axis=0, keepdims=True)
        dg_ref[...] += jnp.sum(dh * s1 * xn, axis=0, keepdims=True)
        dxn = dh * s1 * gn
        gx_ref[...] = dx2_ref[...] + r1 * (dxn - xn * jnp.mean(dxn * xn, axis=-1, keepdims=True))

    tok = lambda w: pl.BlockSpec((tm, w), lambda i: (i, 0))
    row = lambda w: pl.BlockSpec((1, w), lambda i: (0, 0))
    vec = jax.ShapeDtypeStruct((1, D_MODEL), F32)
    return pl.pallas_call(
        body, out_shape=(jax.ShapeDtypeStruct((T, D_MODEL), F32), vec, vec, vec), grid=(T // tm,), name="input_backward",
        in_specs=[tok(c * CB) for _, _, c in pieces]
        + [pl.BlockSpec((D_MODEL, IN_W), lambda i: (0, 0), pipeline_mode=pl.Buffered(1)), tok(D_MODEL), tok(D_MODEL),
           row(ADA_W), row(D_MODEL)],
        out_specs=(tok(D_MODEL), row(D_MODEL), row(D_MODEL), row(D_MODEL)),
        compiler_params=_cp("arbitrary"),
    )(*[p[0] for p in pieces], w_in, x, dx2, mod_row, norm_g)


def _weight_grad(a, pieces, tag, a_is_transposed=False):
    M, T = a.shape if a_is_transposed else a.shape[::-1]
    n_blocks = sum(count for _, _, count in pieces)
    n = len(pieces)
    contract = _dot if a_is_transposed else _dot_tn

    def body(*refs):
        a_ref, b_refs, o_ref = refs[0], refs[1:1 + n], refs[-1]
        j = pl.program_id(0)
        for b_ref, (_, start, count) in zip(b_refs, pieces):
            @pl.when((j >= start) & (j < start + count))
            def _(b_ref=b_ref):
                o_ref[...] = contract(a_ref[...], b_ref[...])

    def piece_spec(start, count):
        return pl.BlockSpec((T, CB), lambda j: (0, jnp.clip(j - start, 0, count - 1)))

    return pl.pallas_call(
        body, out_shape=jax.ShapeDtypeStruct((M, n_blocks * CB), F32), grid=(n_blocks,), name=f"weight_grad_{tag}",
        in_specs=[pl.BlockSpec(a.shape, lambda j: (0, 0), pipeline_mode=pl.Buffered(1))] + [piece_spec(s, c) for _, s, c in pieces],
        out_specs=pl.BlockSpec((M, CB), lambda j: (0, j)), compiler_params=_cp("arbitrary"),
    )(a, *[p[0] for p in pieces])


def _adamw(w, g, m, v):
    m = ADAM_B1 * m + (1.0 - ADAM_B1) * g
    v = ADAM_B2 * v + (1.0 - ADAM_B2) * (g * g)
    m_hat = m / (1.0 - ADAM_B1 ** ADAM_STEP)
    v_hat = v / (1.0 - ADAM_B2 ** ADAM_STEP)
    delta = -ADAM_LR * (m_hat / (jnp.sqrt(v_hat) + ADAM_EPS) + ADAM_WD * w)
    return delta, m, v


def _sum_landed(kind, own, land, where, tag):
    if kind == "in":
        R, C = land.shape[1:]
        tr = 256
        grid = (R // tr,)
        own_spec = pl.BlockSpec((tr, C), lambda i, w: (i, w[0]))
        land_spec = pl.BlockSpec((3, tr, C), lambda i, w: (0, i, 0))
        out_spec = pl.BlockSpec((1, tr, C), lambda i, w: (w[1], i, 0))
        out_shape = (2, R, C)
        pick = lambda ref: ref[...]
    elif kind == "sq":
        R, C = land.shape[1:]
        grid = (1,)
        own_spec = pl.BlockSpec((1, R, C), lambda i, w: (w[0], 0, 0))
        land_spec = pl.BlockSpec((3, R, C), lambda i, w: (0, 0, 0))
        out_spec = pl.BlockSpec((1, R, C), lambda i, w: (w[1], 0, 0))
        out_shape = (2, R, C)
        pick = lambda ref: ref[0]
    else:
        B, R, C = land.shape[1:]
        grid = (1,)
        own_spec = pl.BlockSpec((B, 1, R, C), lambda i, w: (0, w[0], 0, 0))
        land_spec = pl.BlockSpec((3, B, R, C), lambda i, w: (0, 0, 0, 0))
        out_spec = pl.BlockSpec((B, 1, R, C), lambda i, w: (0, w[1], 0, 0))
        out_shape = (B, 2, R, C)
        pick = lambda ref: ref[:, 0]

    def body(w_ref, own_ref, l_ref, o_ref):
        total = ((pick(own_ref) + l_ref[0].astype(F32)) + l_ref[1].astype(F32)) + l_ref[2].astype(F32)
        if kind == "in":
            o_ref[0] = total
        elif kind == "sq":
            o_ref[0] = total
        else:
            o_ref[:, 0] = total

    grid_spec = pltpu.PrefetchScalarGridSpec(num_scalar_prefetch=1, grid=grid, in_specs=[own_spec, land_spec], out_specs=out_spec)
    return pl.pallas_call(
        body, out_shape=jax.ShapeDtypeStruct(out_shape, F32), grid_spec=grid_spec, name=f"sum_landed_{tag}",
        compiler_params=_cp("parallel"),
    )(where, own, land)


def _adamw_shard(g, w, m, v, tag):
    R, C = w.shape
    tr = min(R, 256)

    def body(g_ref, w_ref, m_ref, v_ref, d_ref, nm_ref, nv_ref):
        d, nm, nv = _adamw(w_ref[...], g_ref[...], m_ref[...], v_ref[...])
        d_ref[...] = d
        nm_ref[...] = nm
        nv_ref[...] = nv

    spec = pl.BlockSpec((tr, C), lambda i: (i, 0))
    sds = jax.ShapeDtypeStruct((R, C), F32)
    return pl.pallas_call(
        body, out_shape=(sds,) * 3, grid=(R // tr,), name=f"adamw_{tag}",
        in_specs=[spec] * 4, out_specs=(spec,) * 3, compiler_params=_cp("parallel"),
    )(g, w, m, v)


def _adamw_w_ada(c_t, dmod_cols, w, m, v):
    R, C = w.shape

    def body(ct_ref, dm_ref, w_ref, m_ref, v_ref, g_ref, d_ref, nm_ref, nv_ref):
        g = _dot(ct_ref[...].astype(BF16), dm_ref[...].astype(BF16))
        d, nm, nv = _adamw(w_ref[...], g, m_ref[...], v_ref[...])
        g_ref[...] = g
        d_ref[...] = d
        nm_ref[...] = nm
        nv_ref[...] = nv

    tr = 256
    spec = pl.BlockSpec((tr, C), lambda i: (i, 0))
    sds = jax.ShapeDtypeStruct((R, C), F32)
    return pl.pallas_call(
        body, out_shape=(sds,) * 4, grid=(R // tr,), name="adamw_w_ada",
        in_specs=[pl.BlockSpec((tr, 128), lambda i: (i, 0)), pl.BlockSpec((128, C), lambda i: (0, 0))] + [spec] * 3,
        out_specs=(spec,) * 4, compiler_params=_cp("parallel"),
    )(c_t, dmod_cols, w, m, v)


def _adamw_small(small_all, ws, ms, vs):
    def body(s_ref, w_ref, m_ref, v_ref, g_ref, d_ref, nm_ref, nv_ref):
        g = s_ref[0]
        for b in range(1, N_DEV):
            g = g + s_ref[b]
        d, nm, nv = _adamw(w_ref[...], g, m_ref[...], v_ref[...])
        g_ref[...] = g
        d_ref[...] = d
        nm_ref[...] = nm
        nv_ref[...] = nv

    sds = jax.ShapeDtypeStruct((SMALL_ROWS, D_MODEL), F32)
    return pl.pallas_call(
        body, out_shape=(sds,) * 4, name="adamw_small", in_specs=[VMEM_SPEC] * 4, out_specs=(VMEM_SPEC,) * 4,
        compiler_params=pltpu.CompilerParams(vmem_limit_bytes=VMEM_LIMIT_V7X),
    )(small_all, ws, ms, vs)


ROW_MOD, ROW_NORM_G, ROW_CONV_B, ROW_BA, ROW_BX, ROW_LAM, ROW_FINAL_G, ROW_SINKS, ROW_CONV_W, ROW_LOSS = 0, 3, 4, 5, 6, 7, 8, 9, 10, 14


def _pack_small(b_ada, norm_g, conv_b, ba, bx, lam, final_g, sinks, conv_w_full, loss_row=None):
    lane_pad = lambda a: jnp.pad(a.reshape(1, -1), ((0, 0), (0, D_MODEL - a.size)))
    rows = [b_ada.reshape(3, D_MODEL), norm_g, conv_b, ba, bx, lam, final_g.reshape(1, D_MODEL), lane_pad(sinks), conv_w_full,
            jnp.zeros((1, D_MODEL), F32) if loss_row is None else lane_pad(loss_row),
            jnp.zeros((SMALL_ROWS - ROW_LOSS - 1, D_MODEL), F32)]
    return jnp.concatenate([r.astype(F32) for r in rows], axis=0)


def kernel(x, c, positions, w_ada, b_ada, norm_g, w_in, attn_sinks, conv_w, conv_b, rg_wa, rg_ba, rg_wx, rg_bx, rg_lambda, w_attn_proj, w_rnn_proj, w_out, final_g, loss_target, m_w_ada, m_b_ada, m_norm_g, m_w_in, m_attn_sinks, m_conv_w, m_conv_b, m_rg_wa, m_rg_ba, m_rg_wx, m_rg_bx, m_rg_lambda, m_w_attn_proj, m_w_rnn_proj, m_w_out, m_final_g, v_w_ada, v_b_ada, v_norm_g, v_w_in, v_attn_sinks, v_conv_w, v_conv_b, v_rg_wa, v_rg_ba, v_rg_wx, v_rg_bx, v_rg_lambda, v_w_attn_proj, v_w_rnn_proj, v_w_out, v_final_g):
    T = x.shape[1]
    my_chip = lax.axis_index("x") * 2 + lax.axis_index("y")
    my_dev = my_chip * 2 + lax.axis_index("c")
    x2d, tgt = x[0], loss_target[0]
    pos_col = positions.reshape(T, 1)

    chip_idx = my_chip.reshape(1).astype(jnp.int32)
    c_idx = lax.axis_index("c").reshape(1).astype(jnp.int32)
    sq_place = ((D_MODEL, D_MODEL), (SHARD_ROWS, D_MODEL), lambda chip: (chip, 0))
    rg_place = ((RNN_BLOCKS, RNN_BW, RNN_BW), (RNN_BLOCKS, SHARD_RG, RNN_BW), lambda chip: (0, chip, 0))
    placed = [
        _cast_place(w_in[0], chip_idx, (D_MODEL, IN_W), (D_MODEL, SHARD_IN), lambda chip: (0, chip), "w_in"),
        _cast_place(w_attn_proj[0], chip_idx, *sq_place, "w_attn_proj"),
        _cast_place(w_rnn_proj[0], chip_idx, *sq_place, "w_rnn_proj"),
        _cast_place(w_out[0], chip_idx, *sq_place, "w_out"),
        _cast_place(rg_wa[0], chip_idx, *rg_place, "rg_wa"),
        _cast_place(rg_wx[0], chip_idx, *rg_place, "rg_wx"),
    ]
    cw_chips, c_all, mod_chips = _gather_mod(c.reshape(1, 1, D_MODEL), w_ada[0], conv_w[0])
    g_ssems, g_rsems, fulls, g_token = _gather_start([p.reshape(s) for p, s in zip(placed, FULL_SHAPES)], mod_chips)
    conv_w_f = jnp.transpose(cw_chips, (1, 0, 2)).reshape(CONV_W, D_MODEL)
    mod_all = jnp.transpose(mod_chips, (1, 0, 2)).reshape(N_DEV, ADA_W) + b_ada
    mod_row = lax.dynamic_slice_in_dim(mod_all, my_dev, 1, axis=0) + g_token[0:1, 0:1]

    h, h_t, tabs = _prenorm(x2d, mod_row, norm_g, pos_col)
    w_in_v = fulls[0]
    proj = _in_projection(h, w_in_v.reshape(D_MODEL, IN_W), chip_idx, None, "own")
    for k, mask in enumerate(CHIP_MASKS):
        w_in_v = _gather_wait(g_ssems[k], g_rsems[k], [w_in_v], [0], proj, f"w_in_{k}")[0]
        w_in_v = _forward_halves([w_in_v], [(0, 0, k)], f"w_in_{k}")[0]
        from_chip = (chip_idx ^ (mask >> 1)).astype(jnp.int32)
        proj = _in_projection(h, w_in_v.reshape(D_MODEL, IN_W), from_chip, proj, f"from_{k}")
    w_in_f = w_in_v.reshape(D_MODEL, IN_W)
    rest = _gather_wait(g_ssems[3], g_rsems[3], list(fulls[1:]), [1, 2, 3, 4, 5], proj, "rest")
    rest = _forward_halves(rest, [(idx - 1, idx, k) for idx in range(1, N_BIG) for k in range(3)], "rest")
    wap_f, wrp_f, wo_f = (g.reshape(D_MODEL, D_MODEL) for g in rest[0:3])
    rwa_f, rwx_f = (g.reshape(RNN_BLOCKS, RNN_BW, RNN_BW) for g in rest[3:5])
    y_attn, qr_b, kr_b = _attn_forward(proj, tabs, attn_sinks)
    y_rnn, h_rnn, *rnn_saved = _rnn_forward(proj, pos_col, conv_w_f, conv_b, rwa_f, rwx_f, rg_ba, rg_bx, rg_lambda)
    (dx2, merged, d_o, d_pa, d_pr, d_ya, d_yr, d_c, d_final_g, d_gate, loss_vec) = _merge_and_head(
        x2d, tgt, y_attn, y_rnn, proj, wap_f, wrp_f, wo_f, mod_row, final_g.reshape(1, D_MODEL))

    sq = (N_CHIPS, 2, SHARD_ROWS // 2, D_MODEL)
    rg = (RNN_BLOCKS, N_CHIPS, 2, SHARD_RG // 2, RNN_BW)
    rg_flat = (RNN_BLOCKS * N_CHIPS, 2, SHARD_RG // 2, RNN_BW)

    def chip_sum_and_start(views, axes, flat, unflat, tags_, kinds_, group):
        from_sib = _swap_halves(views, axes)
        sums = [_presum(v.reshape(f), s.reshape(f[:1] + f[2:]), c_idx, t) for v, s, f, t in zip(views, from_sib, flat, tags_)]
        exact = [s[0].reshape(u) for s, u in zip(sums, unflat)]
        rounded = [s[1].reshape(u) for s, u in zip(sums, unflat)]
        return _exchange_start(rounded, kinds_, group), exact

    g_ap = _weight_grad(y_attn, [(d_pa, 0, 2)], "w_attn_proj")
    g_rp = _weight_grad(y_rnn, [(d_pr, 0, 2)], "w_rnn_proj")
    g_o = _weight_grad(merged, [(d_o, 0, 2)], "w_out")
    sq_half = (N_CHIPS, SHARD_ROWS // 2, D_MODEL)
    started1, own1 = chip_sum_and_start([g_ap.reshape(sq), g_rp.reshape(sq), g_o.reshape(sq)], [1, 1, 1], [sq] * 3, [sq_half] * 3,
                                  ["w_attn_proj", "w_rnn_proj", "w_out"], ["sq"] * 3, "proj")
    d_q, d_kv, d_ga, d_sinks = _attn_backward(proj, qr_b, kr_b, d_ya, tabs, attn_sinks + started1[4][0, 0])
    d_b, d_conv_w, d_conv_b, d_rwa, d_rwx, d_ba, d_bx, d_lam = _rnn_backward(
        proj, pos_col, h_rnn, rnn_saved, d_yr, conv_w_f, rwa_f, rwx_f, rg_lambda)
    pieces = [(d_q, CB_Q, 2), (d_kv, CB_KV, 1), (d_ga, CB_GA, 2), (d_b, CB_XR, 4), (d_c, CB_MA, 4)]
    g_in = _weight_grad(h_t, pieces, "w_in", a_is_transposed=True)
    started2, own2 = chip_sum_and_start(
        [g_in.reshape(2, D_MODEL // 2, IN_W), d_rwa.reshape(rg), d_rwx.reshape(rg)], [0, 2, 2],
        [(1, 2, D_MODEL // 2, IN_W), rg_flat, rg_flat],
        [(D_MODEL // 2, IN_W), (RNN_BLOCKS, N_CHIPS, SHARD_RG // 2, RNN_BW), (RNN_BLOCKS, N_CHIPS, SHARD_RG // 2, RNN_BW)],
        ["w_in", "rg_wa", "rg_wx"], ["in", "rg", "rg"], "in")
    grad_x, d_shift, d_scale, d_norm_g = _input_backward(pieces, w_in_f, x2d, dx2, mod_row + started2[4][0, 0], norm_g)

    d_mod = jnp.concatenate([d_shift, d_scale, d_gate], axis=1)
    small = _pack_small(d_mod, d_norm_g, d_conv_b, d_ba, d_bx, d_lam, d_final_g, d_sinks[:, :N_HEADS], d_conv_w, loss_vec)
    small_all = _gather_small(small)
    _, lands1 = _exchange_wait(*started1[:4], grad_x, "proj")
    _, lands2 = _exchange_wait(*started2[:4], grad_x, "in")
    tags = ["w_in", "w_attn_proj", "w_rnn_proj", "w_out", "rg_wa", "rg_wx"]
    chip_sums = [own2[0]] + list(own1) + list(own2[1:])
    lands = [lands2[0]] + list(lands1) + list(lands2[1:])
    where = jnp.concatenate([chip_idx, c_idx])
    kinds = ["in", "sq", "sq", "sq", "rg", "rg"]
    halves = [_sum_landed(kinds[i], chip_sums[i], lands[i], where, tags[i]) for i in range(6)]
    grads = _assemble_with_sibling(halves, [0, 0, 0, 0, 1, 1])
    shapes2d = [(D_MODEL, SHARD_IN), (SHARD_ROWS, D_MODEL), (SHARD_ROWS, D_MODEL), (SHARD_ROWS, D_MODEL),
                (RNN_BLOCKS * SHARD_RG, RNN_BW), (RNN_BLOCKS * SHARD_RG, RNN_BW)]
    big_w = [w_in, w_attn_proj, w_rnn_proj, w_out, rg_wa, rg_wx]
    big_m = [m_w_in, m_w_attn_proj, m_w_rnn_proj, m_w_out, m_rg_wa, m_rg_wx]
    big_v = [v_w_in, v_w_attn_proj, v_w_rnn_proj, v_w_out, v_rg_wa, v_rg_wx]
    res = {}
    for i, tag in enumerate(tags):
        g = grads[i].reshape(shapes2d[i])
        outs = _adamw_shard(g, big_w[i].reshape(shapes2d[i]), big_m[i].reshape(shapes2d[i]), big_v[i].reshape(shapes2d[i]), tag)
        res[tag] = [o.reshape(big_w[i].shape) for o in (g,) + tuple(outs)]

    dmod_all = small_all[:, ROW_MOD:ROW_MOD + 3, :].reshape(N_DEV, ADA_W)
    dmod_cols = lax.dynamic_slice_in_dim(dmod_all, my_chip * SHARD_ADA, SHARD_ADA, axis=1)
    c_t = jnp.pad(jnp.transpose(c_all.reshape(N_DEV, D_MODEL)), ((0, 0), (0, 128 - N_DEV)))
    dmod_cols = jnp.pad(dmod_cols, ((0, 128 - N_DEV), (0, 0)))
    res["w_ada"] = [o.reshape(w_ada.shape) for o in _adamw_w_ada(c_t, dmod_cols, w_ada[0], m_w_ada[0], v_w_ada[0])]

    def full_conv(a):
        return lax.dynamic_update_slice_in_dim(jnp.zeros((CONV_W, D_MODEL), F32), a[0], my_chip * (D_MODEL // N_CHIPS), axis=1)

    packed = [_pack_small(p[0], p[1], p[2], p[3], p[4], p[5], p[6], p[7], full_conv(p[8])) for p in (
        (b_ada, norm_g, conv_b, rg_ba, rg_bx, rg_lambda, final_g, attn_sinks, conv_w),
        (m_b_ada, m_norm_g, m_conv_b, m_rg_ba, m_rg_bx, m_rg_lambda, m_final_g, m_attn_sinks, m_conv_w),
        (v_b_ada, v_norm_g, v_conv_b, v_rg_ba, v_rg_bx, v_rg_lambda, v_final_g, v_attn_sinks, v_conv_w))]
    small_out = _adamw_small(small_all, *packed)

    def unpack(slab):
        cw = lax.dynamic_slice_in_dim(slab[ROW_CONV_W:ROW_CONV_W + CONV_W], my_chip * (D_MODEL // N_CHIPS),
                                      D_MODEL // N_CHIPS, axis=1)
        return {
            "b_ada": slab[ROW_MOD:ROW_MOD + 3].reshape(1, ADA_W), "norm_g": slab[ROW_NORM_G:ROW_NORM_G + 1],
            "conv_b": slab[ROW_CONV_B:ROW_CONV_B + 1], "rg_ba": slab[ROW_BA:ROW_BA + 1], "rg_bx": slab[ROW_BX:ROW_BX + 1],
            "rg_lambda": slab[ROW_LAM:ROW_LAM + 1], "final_g": slab[ROW_FINAL_G], "attn_sinks": slab[ROW_SINKS:ROW_SINKS + 1, :N_HEADS],
            "conv_w": cw[None],
        }

    small_res = [unpack(s) for s in small_out]
    order = ["w_ada", "b_ada", "norm_g", "w_in", "attn_sinks", "conv_w", "conv_b", "rg_wa", "rg_ba", "rg_wx", "rg_bx",
             "rg_lambda", "w_attn_proj", "w_rnn_proj", "w_out", "final_g"]
    loss = small_out[0][ROW_LOSS, 0]
    outs = [loss, grad_x[None]]
    for kind in range(4):
        for name in order:
            outs.append(res[name][kind] if name in res else small_res[kind][name])
    return tuple(outs)
```

```python
import numpy as np
import jax
import jax.numpy as jnp
from jax import lax
from jax.experimental import pallas as pl
from jax.experimental.pallas import tpu as pltpu

F32 = jnp.float32
BF16 = jnp.bfloat16

D_MODEL = 1024
N_HEADS = 16
N_KV = 4
HEAD_DIM = 64
GROUP = N_HEADS // N_KV
BLOCK = 128
KV_W = N_KV * HEAD_DIM
ROT_HALF = 8
ROPE_THETA = 500000.0
ATTN_SCALE = 0.125
RNN_BLOCKS = 4
RNN_BW = 256
CONV_W = 4
LRU_C = 8.0
NORM_EPS = 1e-6
IN_W = 6656
CB = 512
N_CB = IN_W // CB
CB_Q, CB_KV, CB_GA, CB_XR, CB_GR, CB_MA, CB_MR = 0, 2, 3, 5, 7, 9, 11
V_COL_BLOCK = 5
N_CHIPS = 4
N_DEV = 8
SHARD_IN = IN_W // N_CHIPS
SHARD_ROWS = D_MODEL // N_CHIPS
SHARD_RG = RNN_BW // N_CHIPS
ADA_W = 3 * D_MODEL
SHARD_ADA = ADA_W // N_CHIPS
SMALL_ROWS = 16

ADAM_LR = 0.001
ADAM_B1 = 0.9
ADAM_B2 = 0.999
ADAM_EPS = 1e-08
ADAM_WD = 0.01
ADAM_STEP = 10

VMEM_LIMIT_V7X = 52 * 1024 * 1024
MESH = pl.DeviceIdType.MESH
ANY = pl.BlockSpec(memory_space=pl.ANY)
VMEM_SPEC = pl.BlockSpec(memory_space=pltpu.VMEM)


def _cp(*sem):
    return pltpu.CompilerParams(dimension_semantics=sem if sem else None, vmem_limit_bytes=VMEM_LIMIT_V7X)


def _dot(a, b):
    return jnp.dot(a, b, preferred_element_type=F32)


def _dot_nt(a, b):
    return lax.dot_general(a, b, (((1,), (1,)), ((), ())), preferred_element_type=F32)


def _dot_tn(a, b):
    return lax.dot_general(a, b, (((0,), (0,)), ((), ())), preferred_element_type=F32)


def _sigmoid(z):
    return 1.0 / (1.0 + jnp.exp(-z))


def _softplus(z):
    u = jnp.exp(-jnp.abs(z))
    log1p_u = jnp.where(u < 1e-3, u * (1.0 - u * (0.5 - u * (1.0 / 3.0))), jnp.log(1.0 + u))
    return jnp.maximum(z, 0.0) + log1p_u


def _rms(xf):
    return lax.rsqrt(jnp.mean(xf * xf, axis=-1, keepdims=True) + NORM_EPS)


def _me():
    return lax.axis_index("x"), lax.axis_index("y"), lax.axis_index("c")


def _peer(mask):
    x, y, c = _me()
    fx, fy, fc = (mask >> 2) & 1, (mask >> 1) & 1, mask & 1
    return (x ^ fx if fx else x, y ^ fy if fy else y, c ^ fc if fc else c)


def _chip_of(pos):
    return pos[0] * 2 + pos[1]


SIBLING_COLLECTIVE_ID = 0
SIBLING_ONLY = pltpu.CompilerParams(collective_id=SIBLING_COLLECTIVE_ID)


def _sibling_handshake():
    barrier = pltpu.get_barrier_semaphore()
    pl.semaphore_signal(barrier, inc=1, device_id=_peer(1), device_id_type=MESH)
    pl.semaphore_wait(barrier, 1)


CHIP_MASKS = (4, 2, 6)
ALL_MASKS = (1, 2, 3, 4, 5, 6, 7)


HBM_SPEC = pl.BlockSpec(memory_space=pltpu.HBM)
SEM_SPEC = pl.BlockSpec(memory_space=pltpu.SEMAPHORE)
SPLIT_COPY = pltpu.CompilerParams(has_side_effects=pltpu.SideEffectType.DATAFLOW_SIDE_EFFECTING)
N_BIG = 6
FULL_SHAPES = (
    (2, D_MODEL // 2, IN_W),
    (N_CHIPS, 2, SHARD_ROWS // 2, D_MODEL), (N_CHIPS, 2, SHARD_ROWS // 2, D_MODEL), (N_CHIPS, 2, SHARD_ROWS // 2, D_MODEL),
    (RNN_BLOCKS, N_CHIPS, 2, SHARD_RG // 2, RNN_BW), (RNN_BLOCKS, N_CHIPS, 2, SHARD_RG // 2, RNN_BW),
)


def _slot(full, idx, chip, half):
    if idx == 0:
        return full.at[half, :, pl.ds(pl.multiple_of(chip * SHARD_IN, 128), SHARD_IN)]
    return full.at[chip, half] if idx in (1, 2, 3) else full.at[:, chip, half]


def _three_halves(full, idx):
    return full.at[pl.ds(0, 3), 0] if idx in (1, 2, 3) else full.at[:, pl.ds(0, 3), 0]


def _gather_start(fulls, after):
    def body(*refs):
        full_refs = refs[:N_BIG]
        ssems, rsems = refs[N_BIG + 1:N_BIG + 5], refs[N_BIG + 5:N_BIG + 9]
        token = refs[2 * N_BIG + 9]
        me = _me()
        my_chip = _chip_of(me)
        for idx in range(N_BIG):
            for k, mask in enumerate(CHIP_MASKS):
                pair = k if idx == 0 else 3
                mine = _slot(full_refs[idx], idx, my_chip, me[2])
                pltpu.make_async_remote_copy(src_ref=mine, dst_ref=mine, send_sem=ssems[pair], recv_sem=rsems[pair],
                                             device_id=_peer(mask), device_id_type=MESH).start()
        token[...] = jnp.zeros_like(token)

    sem = pltpu.SemaphoreType.DMA(())
    out_shape = (sem,) * 8 + tuple(pltpu.HBM(f.shape, f.dtype) for f in fulls) + (jax.ShapeDtypeStruct((8, 128), F32),)
    outs = pl.pallas_call(
        body, out_shape=out_shape, name="gather_start",
        in_specs=[HBM_SPEC] * N_BIG + [ANY], out_specs=tuple([SEM_SPEC] * 8 + [HBM_SPEC] * N_BIG + [VMEM_SPEC]),
        input_output_aliases={i: 8 + i for i in range(N_BIG)}, compiler_params=SPLIT_COPY,
    )(*[pltpu.with_memory_space_constraint(f, pltpu.HBM) for f in fulls], after)
    return outs[0:4], outs[4:8], outs[8:8 + N_BIG], outs[8 + N_BIG]


def _gather_wait(ssem, rsem, arrays, idxs, after, tag):
    n = len(arrays)

    def body(*refs):
        full_refs, ssem_ref, rsem_ref = refs[:n], refs[n], refs[n + 1]
        me = _me()
        for full, idx in zip(full_refs, idxs):
            region = _slot(full, 0, _chip_of(me), me[2]) if idx == 0 else _three_halves(full, idx)
            arrived = pltpu.make_async_remote_copy(
                src_ref=region, dst_ref=region, send_sem=ssem_ref, recv_sem=rsem_ref, device_id=me, device_id_type=MESH)
            arrived.wait_send()
            arrived.wait_recv()

    outs = pl.pallas_call(
        body, out_shape=tuple(pltpu.HBM(a.shape, a.dtype) for a in arrays), name=f"gather_wait_{tag}",
        in_specs=[HBM_SPEC] * n + [SEM_SPEC, SEM_SPEC, ANY], out_specs=tuple([HBM_SPEC] * n),
        input_output_aliases={i: i for i in range(n)}, compiler_params=SPLIT_COPY,
    )(*arrays, ssem, rsem, after)
    return list(outs)


def _forward_halves(arrays, items, tag):
    n, m = len(arrays), len(items)

    def body(*refs):
        outs, ssem, rsem = refs[n:2 * n], refs[2 * n], refs[2 * n + 1]
        me = _me()
        sib = _peer(1)
        _sibling_handshake()
        cps = []
        for j, (pos, idx, k) in enumerate(items):
            chip = _chip_of(_peer(CHIP_MASKS[k]))
            cp = pltpu.make_async_remote_copy(
                src_ref=_slot(outs[pos], idx, chip, me[2]), dst_ref=_slot(outs[pos], idx, chip, me[2]),
                send_sem=ssem.at[j], recv_sem=rsem.at[j], device_id=sib, device_id_type=MESH)
            cp.start()
            cps.append(cp)
        for j, (pos, idx, k) in enumerate(items):
            chip = _chip_of(_peer(CHIP_MASKS[k]))
            pltpu.make_async_remote_copy(
                src_ref=_slot(outs[pos], idx, chip, me[2]), dst_ref=_slot(outs[pos], idx, chip, 1 - me[2]),
                send_sem=ssem.at[j], recv_sem=rsem.at[j], device_id=sib, device_id_type=MESH).wait_recv()
        for cp in cps:
            cp.wait_send()

    outs = pl.pallas_call(
        body, out_shape=tuple(jax.ShapeDtypeStruct(a.shape, a.dtype) for a in arrays), name=f"forward_halves_{tag}",
        in_specs=[ANY] * n, out_specs=tuple([ANY] * n), input_output_aliases={i: i for i in range(n)},
        scratch_shapes=[pltpu.SemaphoreType.DMA((m,)), pltpu.SemaphoreType.DMA((m,))], compiler_params=SIBLING_ONLY,
    )(*arrays)
    return list(outs)


def _gather_mod(c_row, w_ada_s, conv_w_s):
    def body(c_ref, wada_ref, cw_s, cw_f, call_ref, mod_ref, wsend, wrecv, lsem, csend, crecv, msend, mrecv):
        me = _me()
        my_chip = _chip_of(me)
        my_dev = my_chip * 2 + me[2]
        sends = []
        for k, mask in enumerate(CHIP_MASKS):
            cp = pltpu.make_async_remote_copy(src_ref=cw_s, dst_ref=cw_f.at[my_chip], send_sem=wsend.at[k], recv_sem=wrecv.at[k],
                                              device_id=_peer(mask), device_id_type=MESH)
            cp.start()
            sends.append(cp)
        local = [pltpu.make_async_copy(cw_s, cw_f.at[my_chip], lsem.at[0])]
        for cp in local:
            cp.start()

        call_ref[my_dev] = c_ref[0]
        csends = []
        for k, mask in enumerate(ALL_MASKS):
            cp = pltpu.make_async_remote_copy(
                src_ref=c_ref.at[0], dst_ref=call_ref.at[my_dev],
                send_sem=csend.at[k], recv_sem=crecv.at[k], device_id=_peer(mask), device_id_type=MESH)
            cp.start()
            csends.append(cp)
        for k, mask in enumerate(ALL_MASKS):
            frm = _peer(mask)
            pltpu.make_async_remote_copy(
                src_ref=c_ref.at[0], dst_ref=call_ref.at[_chip_of(frm) * 2 + frm[2]],
                send_sem=csend.at[k], recv_sem=crecv.at[k], device_id=frm, device_id_type=MESH).wait_recv()
        for cp in csends:
            cp.wait_send()

        c_all = call_ref[...].reshape(N_DEV, D_MODEL).astype(BF16)
        mod_ref[my_chip] = _dot(c_all, wada_ref[...].astype(BF16))
        msends = []
        for k, mask in enumerate(CHIP_MASKS):
            cp = pltpu.make_async_remote_copy(
                src_ref=mod_ref.at[my_chip], dst_ref=mod_ref.at[my_chip],
                send_sem=msend.at[k], recv_sem=mrecv.at[k], device_id=_peer(mask), device_id_type=MESH)
            cp.start()
            msends.append(cp)
        for k, mask in enumerate(CHIP_MASKS):
            frm = _peer(mask)
            pltpu.make_async_remote_copy(
                src_ref=mod_ref.at[my_chip], dst_ref=mod_ref.at[_chip_of(frm)],
                send_sem=msend.at[k], recv_sem=mrecv.at[k], device_id=frm, device_id_type=MESH).wait_recv()
        for cp in msends:
            cp.wait_send()

        for k, mask in enumerate(CHIP_MASKS):
            frm = _peer(mask)
            pltpu.make_async_remote_copy(src_ref=cw_s, dst_ref=cw_f.at[_chip_of(frm)], send_sem=wsend.at[k], recv_sem=wrecv.at[k],
                                         device_id=frm, device_id_type=MESH).wait_recv()
        for cp in sends:
            cp.wait_send()
        for cp in local:
            cp.wait()

    out_shape = (
        jax.ShapeDtypeStruct((N_CHIPS, CONV_W, D_MODEL // N_CHIPS), F32),
        jax.ShapeDtypeStruct((N_DEV, 1, D_MODEL), F32),
        jax.ShapeDtypeStruct((N_CHIPS, N_DEV, SHARD_ADA), F32),
    )
    return pl.pallas_call(
        body, out_shape=out_shape, name="gather_mod",
        in_specs=[VMEM_SPEC, VMEM_SPEC, ANY], out_specs=(ANY, VMEM_SPEC, VMEM_SPEC),
        scratch_shapes=[
            pltpu.SemaphoreType.DMA((3,)), pltpu.SemaphoreType.DMA((3,)), pltpu.SemaphoreType.DMA((1,)),
            pltpu.SemaphoreType.DMA((7,)), pltpu.SemaphoreType.DMA((7,)),
            pltpu.SemaphoreType.DMA((3,)), pltpu.SemaphoreType.DMA((3,)),
        ],
        compiler_params=pltpu.CompilerParams(vmem_limit_bytes=VMEM_LIMIT_V7X),
    )(c_row, w_ada_s, conv_w_s)


def _cast_place(shards, chip_idx, places):
    n = len(shards)

    def body(chip_ref, *refs):
        for s_ref, o_ref in zip(refs[:n], refs[n:]):
            o_ref[...] = s_ref[...].astype(BF16)

    grid_spec = pltpu.PrefetchScalarGridSpec(
        num_scalar_prefetch=1, grid=(1,),
        in_specs=[pl.BlockSpec(s.shape, lambda i, chip_ref, nd=s.ndim: (0,) * nd) for s in shards],
        out_specs=tuple(pl.BlockSpec(block, lambda i, chip_ref, im=im: im(chip_ref[0])) for _, block, im in places))
    return pl.pallas_call(
        body, out_shape=tuple(jax.ShapeDtypeStruct(full, BF16) for full, _, _ in places), grid_spec=grid_spec,
        name="cast_place", compiler_params=_cp("arbitrary"),
    )(chip_idx, *shards)


def _shard_of(ref, kind, chip):
    if kind == "in":
        return ref.at[:, pl.ds(pl.multiple_of(chip * SHARD_IN, 128), SHARD_IN)]
    return ref.at[chip] if kind == "sq" else ref.at[:, chip]


def _land_shape(src, kind):
    if kind == "in":
        return (3, src.shape[0], SHARD_IN)
    return (3,) + src.shape[1:] if kind == "sq" else (3, src.shape[0]) + src.shape[2:]


def _exchange_start(srcs, kinds, tag):
    n = len(srcs)
    lands = [pltpu.with_memory_space_constraint(lax.empty(_land_shape(s, k), s.dtype), pltpu.HBM) for s, k in zip(srcs, kinds)]

    def body(*refs):
        src_refs, land_refs = refs[:n], refs[n:2 * n]
        ssems, rsems = refs[2 * n:3 * n], refs[3 * n:4 * n]
        token = refs[6 * n]
        for i in range(n):
            for k, mask in enumerate(CHIP_MASKS):
                to = _peer(mask)
                pltpu.make_async_remote_copy(
                    src_ref=_shard_of(src_refs[i], kinds[i], _chip_of(to)), dst_ref=land_refs[i].at[k],
                    send_sem=ssems[i], recv_sem=rsems[i], device_id=to, device_id_type=MESH).start()
        token[...] = jnp.zeros_like(token)

    sem = pltpu.SemaphoreType.DMA(())
    out_shape = ((sem,) * (2 * n) + tuple(pltpu.HBM(s.shape, s.dtype) for s in srcs)
                 + tuple(pltpu.HBM(l.shape, l.dtype) for l in lands) + (jax.ShapeDtypeStruct((8, 128), F32),))
    outs = pl.pallas_call(
        body, out_shape=out_shape, name=f"exchange_start_{tag}",
        in_specs=[HBM_SPEC] * (2 * n), out_specs=tuple([SEM_SPEC] * (2 * n) + [HBM_SPEC] * (2 * n) + [VMEM_SPEC]),
        input_output_aliases={i: 2 * n + i for i in range(2 * n)},
        compiler_params=pltpu.CompilerParams(has_side_effects=pltpu.SideEffectType.DATAFLOW_SIDE_EFFECTING),
    )(*[pltpu.with_memory_space_constraint(s, pltpu.HBM) for s in srcs], *lands)
    return outs[:n], outs[n:2 * n], outs[2 * n:3 * n], outs[3 * n:4 * n], outs[4 * n]


def _exchange_wait(ssems, rsems, srcs, lands, after, tag):
    n = len(srcs)

    def body(*refs):
        land_refs = refs[n:2 * n]
        ssem_refs, rsem_refs = refs[2 * n:3 * n], refs[3 * n:4 * n]
        for i in range(n):
            all_three = pltpu.make_async_remote_copy(
                src_ref=land_refs[i], dst_ref=land_refs[i], send_sem=ssem_refs[i], recv_sem=rsem_refs[i],
                device_id=_me(), device_id_type=MESH)
            all_three.wait_send()
            all_three.wait_recv()

    outs = pl.pallas_call(
        body, out_shape=tuple(pltpu.HBM(a.shape, a.dtype) for a in list(srcs) + list(lands)), name=f"exchange_wait_{tag}",
        in_specs=[HBM_SPEC] * (2 * n) + [SEM_SPEC] * (2 * n) + [ANY], out_specs=tuple([HBM_SPEC] * (2 * n)),
        input_output_aliases={i: i for i in range(2 * n)},
        compiler_params=pltpu.CompilerParams(has_side_effects=pltpu.SideEffectType.DATAFLOW_SIDE_EFFECTING),
    )(*srcs, *lands, *ssems, *rsems, after)
    return outs[:n], outs[n:]


def _gather_small(small):
    def body(small_ref, small_all, ssend, srecv):
        me = _me()
        my_dev = _chip_of(me) * 2 + me[2]
        small_all[my_dev] = small_ref[...]
        ssends = []
        for k, mask in enumerate(ALL_MASKS):
            cp = pltpu.make_async_remote_copy(
                src_ref=small_ref, dst_ref=small_all.at[my_dev],
                send_sem=ssend.at[k], recv_sem=srecv.at[k], device_id=_peer(mask), device_id_type=MESH)
            cp.start()
            ssends.append(cp)
        for k, mask in enumerate(ALL_MASKS):
            frm = _peer(mask)
            pltpu.make_async_remote_copy(
                src_ref=small_ref, dst_ref=small_all.at[_chip_of(frm) * 2 + frm[2]],
                send_sem=ssend.at[k], recv_sem=srecv.at[k], device_id=frm, device_id_type=MESH).wait_recv()
        for cp in ssends:
            cp.wait_send()

    return pl.pallas_call(
        body, out_shape=jax.ShapeDtypeStruct((N_DEV, SMALL_ROWS, D_MODEL), F32), name="gather_small",
        in_specs=[VMEM_SPEC], out_specs=VMEM_SPEC,
        scratch_shapes=[pltpu.SemaphoreType.DMA((7,)), pltpu.SemaphoreType.DMA((7,))],
    )(small)


def _half_of(ref, axis, half):
    return ref.at[(slice(None),) * axis + (half,)]


def _swap_halves(parts, axes):
    n = len(parts)

    def body(*refs):
        ins, outs, ssem, rsem = refs[:n], refs[n:2 * n], refs[2 * n], refs[2 * n + 1]
        c = lax.axis_index("c")
        _sibling_handshake()
        cps = [pltpu.make_async_remote_copy(src_ref=_half_of(ins[i], axes[i], 1 - c), dst_ref=outs[i], send_sem=ssem.at[i],
                                            recv_sem=rsem.at[i], device_id=_peer(1), device_id_type=MESH) for i in range(n)]
        for cp in cps:
            cp.start()
        for cp in cps:
            cp.wait()

    shapes = [p.shape[:a] + p.shape[a + 1:] for p, a in zip(parts, axes)]
    return pl.pallas_call(
        body, out_shape=tuple(jax.ShapeDtypeStruct(s, p.dtype) for s, p in zip(shapes, parts)), name="swap_halves",
        in_specs=[ANY] * n, out_specs=tuple([ANY] * n),
        scratch_shapes=[pltpu.SemaphoreType.DMA((n,)), pltpu.SemaphoreType.DMA((n,))], compiler_params=SIBLING_ONLY,
    )(*parts)


def _presum(mines, sibs, c_idx, tag):
    n = len(mines)
    S, _, R, C = mines[0].shape
    tr = min(R, 256)
    tc = SHARD_IN if C % SHARD_IN == 0 else (C // 2 if n > 1 and C % 256 == 0 else C)

    def body(c_ref, *refs):
        for k in range(n):
            total = refs[k][:, 0] + refs[n + k][...]
            refs[2 * n + k][...] = total
            refs[3 * n + k][...] = total.astype(BF16)

    out_spec = pl.BlockSpec((S, tr, tc), lambda i, j, c_ref: (0, i, j))
    grid_spec = pltpu.PrefetchScalarGridSpec(
        num_scalar_prefetch=1, grid=(R // tr, C // tc),
        in_specs=[pl.BlockSpec((S, 1, tr, tc), lambda i, j, c_ref: (0, c_ref[0], i, j))] * n + [out_spec] * n,
        out_specs=(out_spec,) * (2 * n))
    outs = pl.pallas_call(
        body, out_shape=(jax.ShapeDtypeStruct((S, R, C), F32),) * n + (jax.ShapeDtypeStruct((S, R, C), BF16),) * n,
        grid_spec=grid_spec, name=f"presum_{tag}", compiler_params=_cp("parallel", "parallel"),
    )(c_idx, *mines, *sibs)
    return list(outs[:n]), list(outs[n:])


def _assemble_with_sibling(parts, axes):
    n = len(parts)

    def body(*refs):
        outs, ssem, rsem = refs[n:2 * n], refs[2 * n], refs[2 * n + 1]
        c = lax.axis_index("c")
        _sibling_handshake()
        cps = [pltpu.make_async_remote_copy(
            src_ref=_half_of(outs[i], axes[i], c), dst_ref=_half_of(outs[i], axes[i], c), send_sem=ssem.at[i],
            recv_sem=rsem.at[i], device_id=_peer(1), device_id_type=MESH) for i in range(n)]
        for cp in cps:
            cp.start()
        for i in range(n):
            pltpu.make_async_remote_copy(
                src_ref=_half_of(outs[i], axes[i], c), dst_ref=_half_of(outs[i], axes[i], 1 - c), send_sem=ssem.at[i],
                recv_sem=rsem.at[i], device_id=_peer(1), device_id_type=MESH).wait_recv()
        for cp in cps:
            cp.wait_send()

    return pl.pallas_call(
        body, out_shape=tuple(jax.ShapeDtypeStruct(p.shape, p.dtype) for p in parts), name="assemble_with_sibling",
        in_specs=[ANY] * n, out_specs=tuple([ANY] * n), input_output_aliases={i: i for i in range(n)},
        scratch_shapes=[pltpu.SemaphoreType.DMA((n,)), pltpu.SemaphoreType.DMA((n,))], compiler_params=SIBLING_ONLY,
    )(*parts)


def _rope_lane_frequencies():
    inv = np.float32(ROPE_THETA) ** (-(np.arange(0, 2 * ROT_HALF, 2, dtype=np.float32)) / np.float32(2 * ROT_HALF))
    lane = np.arange(128) % HEAD_DIM
    return jnp.asarray(np.where(lane < 2 * ROT_HALF, inv[lane % ROT_HALF], 0.0).astype(np.float32)[None, :])


def _rope_tables(pos, freq):
    ang = pos.astype(F32) * freq
    c, s = jnp.cos(ang), jnp.sin(ang)
    m = lax.broadcasted_iota(jnp.int32, ang.shape, 1) & (HEAD_DIM - 1)
    return (jnp.where(m < 2 * ROT_HALF, c, 1.0), jnp.where(m < ROT_HALF, -s, 0.0),
            jnp.where((m >= ROT_HALF) & (m < 2 * ROT_HALF), s, 0.0))


def _columns(t):
    return [t[:, i:i + 128] for i in range(0, t.shape[-1], 128)]


def _rope(t, c, sa, sb):
    return jnp.concatenate(
        [x * c + pltpu.roll(x, 128 - ROT_HALF, 1) * sa + pltpu.roll(x, ROT_HALF, 1) * sb for x in _columns(t)], axis=1)


def _unrope(d, c, sa, sb):
    return jnp.concatenate(
        [x * c + pltpu.roll(x * sa, ROT_HALF, 1) + pltpu.roll(x * sb, 128 - ROT_HALF, 1) for x in _columns(d)], axis=1)


def _prenorm(x, mod_row, norm_g, pos_col):
    T = x.shape[0]
    tm = min(T, 512)

    def body(x_ref, mod_ref, g_ref, pos_ref, f_ref, h_ref, ht_ref, c_ref, sa_ref, sb_ref):
        xf = x_ref[...]
        shift, scale = mod_ref[:, 0:D_MODEL], mod_ref[:, D_MODEL:2 * D_MODEL]
        h = (xf * _rms(xf)) * g_ref[...] * (1.0 + scale) + shift
        h_ref[...] = h.astype(BF16)
        ht_ref[...] = h.T.astype(BF16)
        c_ref[...], sa_ref[...], sb_ref[...] = _rope_tables(pos_ref[...], f_ref[...])

    tab = jax.ShapeDtypeStruct((T, 128), F32)
    tok = lambda w: pl.BlockSpec((tm, w), lambda i: (i, 0))
    row = lambda w: pl.BlockSpec((1, w), lambda i: (0, 0))
    outs = pl.pallas_call(
        body, out_shape=(jax.ShapeDtypeStruct((T, D_MODEL), BF16), jax.ShapeDtypeStruct((D_MODEL, T), BF16), tab, tab, tab),
        grid=(T // tm,), name="prenorm",
        in_specs=[tok(D_MODEL), row(ADA_W), row(D_MODEL), tok(1), row(128)],
        out_specs=(tok(D_MODEL), pl.BlockSpec((D_MODEL, tm), lambda i: (0, i)), tok(128), tok(128), tok(128)),
        compiler_params=_cp("parallel"),
    )(x, mod_row, norm_g, pos_col, _rope_lane_frequencies())
    return outs[0], outs[1], tuple(outs[2:])


def _in_projection(h, w_in, chip, into, tag):
    T = h.shape[0]
    tm, tn = min(T, 512), SHARD_IN

    def body(chip_ref, h_ref, w_ref, *rest):
        rest[-1][...] = _dot(h_ref[...], w_ref[...])

    in_specs = [pl.BlockSpec((tm, D_MODEL), lambda i, c: (i, 0)),
                pl.BlockSpec((D_MODEL, tn), lambda i, c: (0, c[0]), pipeline_mode=pl.Buffered(1))]
    args = [chip, h, w_in]
    aliases = {}
    if into is not None:
        in_specs.append(ANY)
        args.append(into)
        aliases = {3: 0}
    grid_spec = pltpu.PrefetchScalarGridSpec(num_scalar_prefetch=1, grid=(T // tm,), in_specs=in_specs,
                                             out_specs=pl.BlockSpec((tm, tn), lambda i, c: (i, c[0])))
    return pl.pallas_call(
        body, out_shape=jax.ShapeDtypeStruct((T, IN_W), F32), grid_spec=grid_spec, name=f"in_projection_{tag}",
        input_output_aliases=aliases, compiler_params=_cp("parallel"),
    )(*args)


def _attn_mask(n):
    qi = lax.broadcasted_iota(jnp.int32, (GROUP * BLOCK, BLOCK), 0) & (BLOCK - 1)
    j = lax.broadcasted_iota(jnp.int32, (GROUP * BLOCK, BLOCK), 1)
    own = j <= qi
    return own, jnp.logical_not(own) & (n == 0)


def _fold(x, own):
    return jnp.where(own, x[:, BLOCK:2 * BLOCK], x[:, 0:BLOCK])


def _unfold(xf, own):
    zero = jnp.zeros_like(xf)
    return jnp.concatenate([jnp.where(own, zero, xf), jnp.where(own, xf, zero)], axis=1)


ROW_GROUP_HEAD = (0, 2, 1, 3)


def _sink_col(sink_ref, kh):
    rowg = lax.broadcasted_iota(jnp.int32, (GROUP * BLOCK, 1), 0) // BLOCK
    col = jnp.full((GROUP * BLOCK, 1), sink_ref[0, GROUP * kh + ROW_GROUP_HEAD[0]], F32)
    for g in range(1, GROUP):
        col = jnp.where(rowg == g, sink_ref[0, GROUP * kh + ROW_GROUP_HEAD[g]], col)
    return col


def _low_lanes(shape):
    return lax.broadcasted_iota(jnp.int32, shape, 1) < HEAD_DIM


def _kv_pair_operand(prev, cur, kh):
    c = 128 * (kh // 2)
    col = jnp.concatenate([prev[:, c:c + 128], cur[:, c:c + 128]], axis=0).astype(F32)
    if kh % 2 == 0:
        lo = jnp.where(_low_lanes(col.shape), col, 0.0)
        hi = pltpu.roll(lo, HEAD_DIM, 1)
    else:
        hi = jnp.where(_low_lanes(col.shape), 0.0, col)
        lo = pltpu.roll(hi, HEAD_DIM, 1)
    return jnp.concatenate([lo, hi], axis=0).astype(BF16)


def _pair_rows(x, kh):
    c = 2 * 128 * kh
    return jnp.concatenate([x[:, c:c + 128], x[:, c + 128:c + 256]], axis=0)


def _restack(big):
    return jnp.concatenate([big[:, 0:2 * BLOCK], big[:, 2 * BLOCK:4 * BLOCK]], axis=0)


def _unrestack(stacked):
    return jnp.concatenate([stacked[0:2 * BLOCK], stacked[2 * BLOCK:4 * BLOCK]], axis=1)


def _fold_pair(x2, kh):
    low = _low_lanes((2 * BLOCK, 128))
    mixed = jnp.where(low, x2[0:2 * BLOCK], x2[2 * BLOCK:4 * BLOCK])
    total = mixed + pltpu.roll(mixed, HEAD_DIM, 1)
    return jnp.where(low, total, 0.0) if kh % 2 == 0 else jnp.where(low, 0.0, total)


def _attn_scores(qr, k2, kh):
    q2 = _pair_rows(qr, kh).astype(BF16)
    return q2, _restack(_dot_nt(q2, k2))


def _attn_softmax(s, sink_col, mask):
    own, no_key = mask
    s = jnp.where(no_key, -1e30, _fold(s, own))
    m = jnp.maximum(jnp.max(s, axis=-1, keepdims=True), sink_col)
    p = jnp.exp(s - m)
    p_sink = jnp.exp(sink_col - m)
    denom = jnp.sum(p, axis=-1, keepdims=True) + p_sink
    return p / denom, p_sink / denom


def _attn_forward(proj, tabs, sinks):
    T = proj.shape[0]
    nb = T // BLOCK

    def body(q_ref, kvc_ref, kvp_ref, g0_ref, g1_ref, cc, sac, sbc, cp_, sap, sbp, sink_ref, y_ref, qrb_ref, krb_ref):
        n = pl.program_id(0)
        tc = tcur = (cc[...], sac[...], sbc[...])
        tprev = (cp_[...], sap[...], sbp[...])
        qr = _rope(q_ref[...], *tc) * ATTN_SCALE
        kr_cur = _rope(kvc_ref[:, 0:KV_W], *tcur)
        kr_prev = _rope(kvp_ref[:, 0:KV_W], *tprev)
        qrb_ref[...] = qr.astype(BF16)
        krb_ref[...] = kr_cur.astype(BF16)
        v_cur, v_prev = kvc_ref[:, KV_W:2 * KV_W], kvp_ref[:, KV_W:2 * KV_W]
        mask = _attn_mask(n)
        outs = []
        k2s = [_kv_pair_operand(kr_prev, kr_cur, kh) for kh in range(N_KV)]
        v2s = [_kv_pair_operand(v_prev, v_cur, kh) for kh in range(N_KV)]
        scores = [_attn_scores(qr, k2s[kh], kh) for kh in range(N_KV)]
        for kh in range(N_KV):
            pn, _ = _attn_softmax(scores[kh][1], _sink_col(sink_ref, kh), mask)
            o_big = _dot(_unrestack(_unfold(pn.astype(BF16), mask[0])), v2s[kh])
            outs += [o_big[0:BLOCK], o_big[BLOCK:2 * BLOCK]]
        o = jnp.concatenate(outs, axis=1)
        g = jnp.concatenate([g0_ref[...], g1_ref[...]], axis=1)
        y_ref[...] = (o * (g * _sigmoid(g))).astype(BF16)

    def blk(w, cb):
        return pl.BlockSpec((BLOCK, w), lambda n, cb=cb: (n, cb))

    prev = lambda w, cb: pl.BlockSpec((BLOCK, w), lambda n, cb=cb: (jnp.maximum(n - 1, 0), cb))
    return pl.pallas_call(
        body, grid=(nb,), name="attn_forward",
        out_shape=(jax.ShapeDtypeStruct((T, D_MODEL), BF16), jax.ShapeDtypeStruct((T, D_MODEL), BF16),
                   jax.ShapeDtypeStruct((T, KV_W), BF16)),
        in_specs=[blk(D_MODEL, 0), blk(CB, CB_KV), prev(CB, CB_KV), blk(CB, CB_GA), blk(CB, CB_GA + 1),
                  blk(128, 0), blk(128, 0), blk(128, 0), prev(128, 0), prev(128, 0), prev(128, 0),
                  pl.BlockSpec(memory_space=pltpu.SMEM)],
        out_specs=(blk(D_MODEL, 0), blk(D_MODEL, 0), blk(KV_W, 0)),
        compiler_params=_cp("parallel"),
    )(proj, proj, proj, proj, proj, *tabs, *tabs, sinks)


def _scan_rows8():
    return lax.broadcasted_iota(jnp.int32, (8, D_MODEL), 0)


def _scan_forward(a_ref, b_ref, h_ref, carry, rows):
    row = _scan_rows8()

    def group(i, carry):
        off = pl.multiple_of(i * 8, 8)
        a, b = a_ref[pl.ds(off, 8), :], b_ref[pl.ds(off, 8), :]
        for d in (1, 2, 4):
            ok = row >= d
            b = jnp.where(ok, a * pltpu.roll(b, d, 0) + b, b)
            a = jnp.where(ok, a * pltpu.roll(a, d, 0), a)
        h = a * carry + b
        h_ref[pl.ds(off, 8), :] = h
        return h[7:8, :]

    return lax.fori_loop(0, rows // 8, group, carry)


def _scan_backward(a_ref, g_ref, lam_ref, carry, rows):
    row = _scan_rows8()

    def group(i, carry):
        off = pl.multiple_of((rows // 8 - 1 - i) * 8, 8)
        a, g = a_ref[pl.ds(off, 8), :], g_ref[pl.ds(off, 8), :]
        b = a * g
        for d in (1, 2, 4):
            ok = row < 8 - d
            b = jnp.where(ok, a * pltpu.roll(b, 8 - d, 0) + b, b)
            a = jnp.where(ok, a * pltpu.roll(a, 8 - d, 0), a)
        mu = a * carry + b
        mu_below = jnp.where(row == 7, carry, pltpu.roll(mu, 7, 0))
        lam_ref[pl.ds(off, 8), :] = g + mu_below
        return mu[0:1, :]

    return lax.fori_loop(0, rows // 8, group, carry)


def _conv_taps(xbuf, xr, tail):
    rows = xr.shape[0]
    xbuf[0:8, :] = tail
    xbuf[8:rows + 8, :] = xr
    return [xbuf[pl.ds(8 - (CONV_W - 1 - k), rows), :] for k in range(CONV_W - 1)] + [xr]


def _rnn_gates(xbuf, xr, tail, cw, cb, wa_ref, wx_ref, ba, bx, sp, reset):
    xs = _conv_taps(xbuf, xr, tail)
    xc = xs[0] * cw[0:1, :]
    for k in range(1, CONV_W):
        xc = xc + xs[k] * cw[k:k + 1, :]
    xc = xc + cb
    xcb = xc.astype(BF16)
    za = jnp.concatenate([_dot(xcb[:, RNN_BW * j:RNN_BW * (j + 1)], wa_ref[j]) for j in range(RNN_BLOCKS)], axis=1) + ba
    zx = jnp.concatenate([_dot(xcb[:, RNN_BW * j:RNN_BW * (j + 1)], wx_ref[j]) for j in range(RNN_BLOCKS)], axis=1) + bx
    r, i = _sigmoid(za), _sigmoid(zx)
    neg_log_a = LRU_C * r * sp
    a_raw = jnp.exp(-neg_log_a)
    mult_raw = jnp.sqrt(jnp.tanh(neg_log_a) * (1.0 + a_raw * a_raw))
    a = jnp.where(reset, 0.0, a_raw)
    mult = jnp.where(reset, 1.0, mult_raw)
    return xc, r, i, a, mult


def _rnn_forward(proj, pos_col, conv_w, conv_b, rwa, rwx, ba, bx, lam):
    T = proj.shape[0]
    tr = min(T, 256)

    def body(x0, x1, g0, g1, pos_ref, cw_ref, cb_ref, wa_ref, wx_ref, ba_ref, bx_ref, lam_ref,
             y_ref, h_ref, xc_ref, r_ref, i_ref, a_ref, mult_ref, xbuf, bbuf, tail, carry):
        t = pl.program_id(0)

        @pl.when(t == 0)
        def _():
            tail[...] = jnp.zeros_like(tail)
            carry[...] = jnp.zeros_like(carry)

        xr = jnp.concatenate([x0[...], x1[...]], axis=1)
        sp = _softplus(-lam_ref[...])
        reset = pos_ref[...] == 0
        xc, r, i, a, mult = _rnn_gates(
            xbuf, xr, tail[...], cw_ref[...], cb_ref[...], wa_ref, wx_ref, ba_ref[...], bx_ref[...], sp, reset)
        xc_ref[...] = xc
        r_ref[...] = r
        i_ref[...] = i
        a_ref[...] = a
        mult_ref[...] = mult
        bbuf[...] = mult * (i * xc)
        last = _scan_forward(a_ref, bbuf, h_ref, carry[0:1, :], tr)
        carry[...] = jnp.broadcast_to(last, carry.shape)
        tail[...] = xr[tr - 8:tr, :]
        g = jnp.concatenate([g0[...], g1[...]], axis=1)
        y_ref[...] = (h_ref[...] * (g * _sigmoid(g))).astype(BF16)

    blk = lambda cb: pl.BlockSpec((tr, CB), lambda t, cb=cb: (t, cb))
    row = lambda w: pl.BlockSpec((1, w), lambda t: (0, 0))
    full3 = pl.BlockSpec((RNN_BLOCKS, RNN_BW, RNN_BW), lambda t: (0, 0, 0))
    tok = pl.BlockSpec((tr, D_MODEL), lambda t: (t, 0))
    act = jax.ShapeDtypeStruct((T, D_MODEL), F32)
    return pl.pallas_call(
        body, out_shape=(jax.ShapeDtypeStruct((T, D_MODEL), BF16),) + (act,) * 6,
        grid=(T // tr,), name="rnn_forward",
        in_specs=[blk(CB_XR), blk(CB_XR + 1), blk(CB_GR), blk(CB_GR + 1), pl.BlockSpec((tr, 1), lambda t: (t, 0)),
                  pl.BlockSpec((CONV_W, D_MODEL), lambda t: (0, 0)), row(D_MODEL), full3, full3,
                  row(D_MODEL), row(D_MODEL), row(D_MODEL)],
        out_specs=(tok,) * 7,
        scratch_shapes=[pltpu.VMEM((tr + 8, D_MODEL), F32), pltpu.VMEM((tr, D_MODEL), F32),
                        pltpu.VMEM((8, D_MODEL), F32), pltpu.VMEM((8, D_MODEL), F32)],
        compiler_params=_cp("arbitrary"),
    )(proj, proj, proj, proj, pos_col, conv_w, conv_b, rwa, rwx, ba, bx, lam)


ROW_PARTS = 1


def _merge_and_head(x, target, y_attn, y_rnn, proj, wap, wrp, wo, mod_row, final_g):
    T = x.shape[0]
    tm = min(T, 256)

    def body(x_ref, t_ref, ya_ref, yr_ref, ma0, ma1, mr0, mr1, wap_ref, wrp_ref, wo_ref, mod_ref, fg_ref,
             dx2_ref, mg_ref, do_ref, dpa_ref, dpr_ref, dya_ref, dyr_ref, dc_ref, dfg_ref, dgate_ref, loss_ref):
        i = pl.program_id(0)
        gate = mod_ref[:, 2 * D_MODEL:3 * D_MODEL]
        fg = fg_ref[...]
        parts = [slice(p * (tm // ROW_PARTS), (p + 1) * (tm // ROW_PARTS)) for p in range(ROW_PARTS)]
        each = range(ROW_PARTS)
        pa = [_dot(ya_ref[r, :], wap_ref[...]) for r in parts]
        pr = [_dot(yr_ref[r, :], wrp_ref[...]) for r in parts]
        sa = [_sigmoid(jnp.concatenate([ma0[r, :], ma1[r, :]], axis=1)) for r in parts]
        sr = [_sigmoid(jnp.concatenate([mr0[r, :], mr1[r, :]], axis=1)) for r in parts]
        mb = [(sa[p] * pa[p] + sr[p] * pr[p]).astype(BF16) for p in each]
        o = [_dot(mb[p], wo_ref[...]) for p in each]
        x2 = [x_ref[r, :] + gate * o[p] for p, r in enumerate(parts)]
        r2 = [_rms(v) for v in x2]
        xn2 = [x2[p] * r2[p] for p in each]
        err = [xn2[p] * fg - t_ref[r, :] for p, r in enumerate(parts)]
        dy = [e * (1.0 / D_MODEL) for e in err]
        dxn = [d * fg for d in dy]
        dx2 = [r2[p] * (dxn[p] - xn2[p] * jnp.mean(dxn[p] * xn2[p], axis=-1, keepdims=True)) for p in each]
        dob = [(dx2[p] * gate).astype(BF16) for p in each]
        dmerged = [_dot_nt(d, wo_ref[...]) for d in dob]
        dpa = [(dmerged[p] * sa[p]).astype(BF16) for p in each]
        dpr = [(dmerged[p] * sr[p]).astype(BF16) for p in each]
        dya = [_dot_nt(d, wap_ref[...]) for d in dpa]
        dyr = [_dot_nt(d, wrp_ref[...]) for d in dpr]
        loss_t, dfg_t, dgate_t = 0.0, 0.0, 0.0
        for p, r in enumerate(parts):
            dx2_ref[r, :] = dx2[p]
            mg_ref[r, :] = mb[p]
            do_ref[r, :] = dob[p]
            dpa_ref[r, :] = dpa[p]
            dpr_ref[r, :] = dpr[p]
            dya_ref[r, :] = dya[p]
            dyr_ref[r, :] = dyr[p]
            dc_ref[r, 0:D_MODEL] = (dmerged[p] * pa[p] * sa[p] * (1.0 - sa[p])).astype(BF16)
            dc_ref[r, D_MODEL:2 * D_MODEL] = (dmerged[p] * pr[p] * sr[p] * (1.0 - sr[p])).astype(BF16)
            loss_t = loss_t + 0.5 * jnp.sum(
                jnp.sum(err[p] * err[p], axis=-1, keepdims=True) * (1.0 / D_MODEL), axis=0, keepdims=True)
            dfg_t = dfg_t + jnp.sum(dy[p] * xn2[p], axis=0, keepdims=True)
            dgate_t = dgate_t + jnp.sum(dx2[p] * o[p], axis=0, keepdims=True)

        @pl.when(i == 0)
        def _():
            dfg_ref[...] = jnp.zeros_like(dfg_ref)
            dgate_ref[...] = jnp.zeros_like(dgate_ref)
            loss_ref[...] = jnp.zeros_like(loss_ref)

        dfg_ref[...] += dfg_t
        dgate_ref[...] += dgate_t
        loss_ref[...] += jnp.broadcast_to(loss_t, loss_ref.shape)

    tok = lambda w: pl.BlockSpec((tm, w), lambda i: (i, 0))
    blk = lambda cb: pl.BlockSpec((tm, CB), lambda i, cb=cb: (i, cb))
    wfull = pl.BlockSpec((D_MODEL, D_MODEL), lambda i: (0, 0), pipeline_mode=pl.Buffered(1))
    row = lambda w: pl.BlockSpec((1, w), lambda i: (0, 0))
    out_shape = (
        jax.ShapeDtypeStruct((T, D_MODEL), F32), jax.ShapeDtypeStruct((T, D_MODEL), BF16),
        jax.ShapeDtypeStruct((T, D_MODEL), BF16), jax.ShapeDtypeStruct((T, D_MODEL), BF16),
        jax.ShapeDtypeStruct((T, D_MODEL), BF16), jax.ShapeDtypeStruct((T, D_MODEL), F32),
        jax.ShapeDtypeStruct((T, D_MODEL), F32), jax.ShapeDtypeStruct((T, 2 * D_MODEL), BF16),
        jax.ShapeDtypeStruct((1, D_MODEL), F32), jax.ShapeDtypeStruct((1, D_MODEL), F32),
        jax.ShapeDtypeStruct((1, 128), F32),
    )
    return pl.pallas_call(
        body, out_shape=out_shape, grid=(T // tm,), name="merge_and_head",
        in_specs=[tok(D_MODEL), tok(D_MODEL), tok(D_MODEL), tok(D_MODEL), blk(CB_MA), blk(CB_MA + 1), blk(CB_MR),
                  blk(CB_MR + 1), wfull, wfull, wfull, row(ADA_W), row(D_MODEL)],
        out_specs=(tok(D_MODEL),) * 7 + (tok(2 * D_MODEL), row(D_MODEL), row(D_MODEL), row(128)),
        compiler_params=_cp("arbitrary"),
    )(x, target, y_attn, y_rnn, proj, proj, proj, proj, wap, wrp, wo, mod_row, final_g)


def _attn_backward(proj, qr_b, kr_b, d_y, tabs, sinks):
    T = proj.shape[0]
    nb = T // BLOCK

    def body(qrb_ref, krc_ref, krp_ref, vc_ref, vp_ref, g0_ref, g1_ref, dy_ref, cc, sac, sbc, cp_, sap, sbp, sink_ref,
             dq_ref, dkv_ref, dg_ref, dsink_ref, carry):
        n = pl.program_id(0)

        @pl.when(n == 0)
        def _():
            carry[...] = jnp.zeros_like(carry)
            dsink_ref[...] = jnp.zeros_like(dsink_ref)

        @pl.when(n < nb)
        def _():
            tc = tcur = (cc[...], sac[...], sbc[...])
            tprev = (cp_[...], sap[...], sbp[...])
            qr, kr_cur, kr_prev = qrb_ref[...], krc_ref[...], krp_ref[...]
            v_cur, v_prev = vc_ref[...], vp_ref[...]
            g = jnp.concatenate([g0_ref[...], g1_ref[...]], axis=1)
            sg = _sigmoid(g)
            dy = dy_ref[...]
            d_o = dy * (g * sg)
            mask = _attn_mask(n)
            lane = lax.broadcasted_iota(jnp.int32, (1, 128), 1)
            rowg = lax.broadcasted_iota(jnp.int32, (GROUP * BLOCK, 1), 0) // BLOCK
            o_parts, dq_parts = [], []
            dk_cols, dv_cols = [None, None], [None, None]
            dsink = jnp.zeros((1, 128), F32)
            heads = range(N_KV)
            k2s = [_kv_pair_operand(kr_prev, kr_cur, kh) for kh in heads]
            v2s = [_kv_pair_operand(v_prev, v_cur, kh) for kh in heads]
            scores = [_attn_scores(qr, k2s[kh], kh) for kh in heads]
            do2s = [_pair_rows(d_o, kh).astype(BF16) for kh in heads]
            dpns = [_fold(_restack(_dot_nt(do2s[kh], v2s[kh])), mask[0]) for kh in heads]
            probs = [_attn_softmax(scores[kh][1], _sink_col(sink_ref, kh), mask) for kh in heads]
            p_bigs = [_unrestack(_unfold(probs[kh][0].astype(BF16), mask[0])) for kh in heads]
            o_bigs = [_dot(p_bigs[kh], v2s[kh]) for kh in heads]
            dv2s = [_dot_tn(p_bigs[kh], do2s[kh]) for kh in heads]
            deltas = [jnp.sum(probs[kh][0] * dpns[kh], axis=-1, keepdims=True) for kh in heads]
            ds_bigs = [_unrestack(_unfold((probs[kh][0] * (dpns[kh] - deltas[kh])).astype(BF16), mask[0])) for kh in heads]
            dq2s = [_dot(ds_bigs[kh], k2s[kh]) for kh in heads]
            dk2s = [_dot_tn(ds_bigs[kh], scores[kh][0]) for kh in heads]
            for kh in heads:
                o_parts += [o_bigs[kh][0:BLOCK], o_bigs[kh][BLOCK:2 * BLOCK]]
                dq_parts += [dq2s[kh][0:BLOCK], dq2s[kh][BLOCK:2 * BLOCK]]
                dk_c, dv_c = _fold_pair(dk2s[kh], kh), _fold_pair(dv2s[kh], kh)
                c = kh // 2
                dk_cols[c] = dk_c if dk_cols[c] is None else dk_cols[c] + dk_c
                dv_cols[c] = dv_c if dv_cols[c] is None else dv_cols[c] + dv_c
                ds_rows = probs[kh][1] * deltas[kh]
                for gq in range(GROUP):
                    val = -jnp.sum(jnp.where(rowg == gq, ds_rows, 0.0), axis=0, keepdims=True)
                    dsink = dsink + jnp.where(lane == GROUP * kh + ROW_GROUP_HEAD[gq], val, 0.0)
            o = jnp.concatenate(o_parts, axis=1)
            dg_ref[...] = (dy * o * (sg * (1.0 + g * (1.0 - sg)))).astype(BF16)
            dq_ref[...] = (_unrope(jnp.concatenate(dq_parts, axis=1), *tc) * ATTN_SCALE).astype(BF16)
            dk_all, dv_all = jnp.concatenate(dk_cols, axis=1), jnp.concatenate(dv_cols, axis=1)
            dk_prev = _unrope(dk_all[0:BLOCK], *tprev)
            dk_cur = _unrope(dk_all[BLOCK:2 * BLOCK], *tcur)
            dv_prev, dv_cur = dv_all[0:BLOCK], dv_all[BLOCK:2 * BLOCK]
            dkv_ref[...] = (carry[...] + jnp.concatenate([dk_prev, dv_prev], axis=1)).astype(BF16)
            carry[...] = jnp.concatenate([dk_cur, dv_cur], axis=1)
            dsink_ref[...] += dsink

        @pl.when(n == nb)
        def _():
            dkv_ref[...] = carry[...].astype(BF16)

    cur = lambda w, cb: pl.BlockSpec((BLOCK, w), lambda n, cb=cb: (jnp.minimum(n, nb - 1), cb))
    prev = lambda w, cb: pl.BlockSpec((BLOCK, w), lambda n, cb=cb: (jnp.maximum(jnp.minimum(n, nb - 1) - 1, 0), cb))
    out_shape = (jax.ShapeDtypeStruct((T, D_MODEL), BF16), jax.ShapeDtypeStruct((T, 2 * KV_W), BF16),
                 jax.ShapeDtypeStruct((T, D_MODEL), BF16), jax.ShapeDtypeStruct((1, 128), F32))
    return pl.pallas_call(
        body, out_shape=out_shape, grid=(nb + 1,), name="attn_backward",
        in_specs=[cur(D_MODEL, 0), cur(KV_W, 0), prev(KV_W, 0), cur(KV_W, V_COL_BLOCK), prev(KV_W, V_COL_BLOCK),
                  cur(CB, CB_GA), cur(CB, CB_GA + 1), cur(D_MODEL, 0),
                  cur(128, 0), cur(128, 0), cur(128, 0), prev(128, 0), prev(128, 0), prev(128, 0),
                  pl.BlockSpec(memory_space=pltpu.SMEM)],
        out_specs=(cur(D_MODEL, 0), pl.BlockSpec((BLOCK, 2 * KV_W), lambda n: (jnp.maximum(n - 1, 0), 0)),
                   cur(D_MODEL, 0), pl.BlockSpec((1, 128), lambda n: (0, 0))),
        scratch_shapes=[pltpu.VMEM((BLOCK, 2 * KV_W), F32)],
        compiler_params=_cp("arbitrary"),
    )(qr_b, kr_b, kr_b, proj, proj, proj, proj, d_y, *tabs, *tabs, sinks)


def _rnn_backward(proj, pos_col, h_rnn, saved, d_y, conv_w, rwa, rwx, lam):
    T = proj.shape[0]
    tr = min(T, 256)
    nt = T // tr
    hb = tr // 8

    def body(x0, x1, xh0, xh1, g0, g1, pos_ref, h_ref, hh_ref, xc_ref, r_ref, i_ref, a_ref, mult_ref, dy_ref,
             cw_ref, wa_ref, wx_ref, lam_ref, db_ref, dcw_ref, dcb_ref, dwa_ref, dwx_ref, dba_ref, dbx_ref, dlam_ref,
             xbuf, hbuf, dbuf, gbuf, lbuf, mu_carry, dxc_head):
        step = pl.program_id(0)
        first_tile = step == nt - 1

        @pl.when(step == 0)
        def _():
            mu_carry[...] = jnp.zeros_like(mu_carry)
            dxc_head[...] = jnp.zeros_like(dxc_head)
            for ref in (dcw_ref, dcb_ref, dwa_ref, dwx_ref, dba_ref, dbx_ref, dlam_ref):
                ref[...] = jnp.zeros_like(ref)

        xr = jnp.concatenate([x0[...], x1[...]], axis=1)
        tail = jnp.where(first_tile, 0.0, jnp.concatenate([xh0[...], xh1[...]], axis=1))
        lam_v = lam_ref[...]
        sp = _softplus(-lam_v)
        reset = pos_ref[...] == 0
        cw = cw_ref[...]
        xs = _conv_taps(xbuf, xr, tail)
        xc, r, i, a, mult = xc_ref[...], r_ref[...], i_ref[...], a_ref[...], mult_ref[...]
        xcb = xc.astype(BF16)
        g = jnp.concatenate([g0[...], g1[...]], axis=1)
        sg = _sigmoid(g)
        dy = dy_ref[...]
        h = h_ref[...]
        db_ref[:, D_MODEL:2 * D_MODEL] = (dy * h * (sg * (1.0 + g * (1.0 - sg)))).astype(BF16)
        gbuf[...] = dy * (g * sg)
        top = _scan_backward(a_ref, gbuf, lbuf, mu_carry[0:1, :], tr)
        mu_carry[...] = jnp.broadcast_to(top, mu_carry.shape)
        lam_t = lbuf[...]
        hbuf[0:8, :] = jnp.where(first_tile, 0.0, hh_ref[...])
        hbuf[8:tr + 8, :] = h
        h_prev = hbuf[pl.ds(7, tr), :]
        live = jnp.logical_not(reset)
        d_a = jnp.where(live, lam_t * h_prev, 0.0)
        d_mult = jnp.where(live, lam_t * (i * xc), 0.0)
        d_ixc = lam_t * mult
        d_i = d_ixc * xc
        d_xc = d_ixc * i
        d_log_a = d_a * a - d_mult * (a * a / mult)
        d_za = d_log_a * (-LRU_C * sp) * (r * (1.0 - r))
        d_zx = d_i * (i * (1.0 - i))
        dlam_ref[...] += jnp.sum(d_log_a * r, axis=0, keepdims=True) * (LRU_C * _sigmoid(-lam_v))
        dba_ref[...] += jnp.sum(d_za, axis=0, keepdims=True)
        dbx_ref[...] += jnp.sum(d_zx, axis=0, keepdims=True)
        dzab, dzxb = d_za.astype(BF16), d_zx.astype(BF16)
        back = []
        for j in range(RNN_BLOCKS):
            sl = slice(RNN_BW * j, RNN_BW * (j + 1))
            dwa_ref[j] += _dot_tn(xcb[:, sl], dzab[:, sl])
            dwx_ref[j] += _dot_tn(xcb[:, sl], dzxb[:, sl])
            back.append(_dot_nt(dzab[:, sl], wa_ref[j]) + _dot_nt(dzxb[:, sl], wx_ref[j]))
        d_xc = d_xc + jnp.concatenate(back, axis=1)
        dcb_ref[...] += jnp.sum(d_xc, axis=0, keepdims=True)
        for k in range(CONV_W):
            dcw_ref[k:k + 1, :] += jnp.sum(d_xc * xs[k], axis=0, keepdims=True)
        dbuf[0:tr, :] = d_xc
        dbuf[tr:tr + 8, :] = dxc_head[...]
        d_xr = d_xc * cw[CONV_W - 1:CONV_W, :]
        for k in range(CONV_W - 1):
            d_xr = d_xr + dbuf[pl.ds(CONV_W - 1 - k, tr), :] * cw[k:k + 1, :]
        dxc_head[...] = d_xc[0:8, :]
        db_ref[:, 0:D_MODEL] = d_xr.astype(BF16)

    rev = lambda s: nt - 1 - s
    blk = lambda cb: pl.BlockSpec((tr, CB), lambda s, cb=cb: (rev(s), cb))
    halo = lambda w, cb: pl.BlockSpec((8, w), lambda s, cb=cb: (jnp.maximum(rev(s) * hb - 1, 0), cb))
    tok = lambda w: pl.BlockSpec((tr, w), lambda s: (rev(s), 0))
    row = lambda w: pl.BlockSpec((1, w), lambda s: (0, 0))
    full3 = pl.BlockSpec((RNN_BLOCKS, RNN_BW, RNN_BW), lambda s: (0, 0, 0))
    cwspec = pl.BlockSpec((CONV_W, D_MODEL), lambda s: (0, 0))
    vec = jax.ShapeDtypeStruct((1, D_MODEL), F32)
    gate_w = jax.ShapeDtypeStruct((RNN_BLOCKS, RNN_BW, RNN_BW), F32)
    out_shape = (jax.ShapeDtypeStruct((T, 2 * D_MODEL), BF16), jax.ShapeDtypeStruct((CONV_W, D_MODEL), F32), vec,
                 gate_w, gate_w, vec, vec, vec)
    big = lambda: pltpu.VMEM((tr, D_MODEL), F32)
    ext = lambda: pltpu.VMEM((tr + 8, D_MODEL), F32)
    return pl.pallas_call(
        body, out_shape=out_shape, grid=(nt,), name="rnn_backward",
        in_specs=[blk(CB_XR), blk(CB_XR + 1), halo(CB, CB_XR), halo(CB, CB_XR + 1), blk(CB_GR), blk(CB_GR + 1),
                  pl.BlockSpec((tr, 1), lambda s: (rev(s), 0)), tok(D_MODEL), halo(D_MODEL, 0)] + [tok(D_MODEL)] * 6
        + [cwspec, full3, full3, row(D_MODEL)],
        out_specs=(tok(2 * D_MODEL), cwspec, row(D_MODEL), full3, full3, row(D_MODEL), row(D_MODEL), row(D_MODEL)),
        scratch_shapes=[ext(), ext(), ext(), big(), big(), pltpu.VMEM((8, D_MODEL), F32), pltpu.VMEM((8, D_MODEL), F32)],
        compiler_params=_cp("arbitrary"),
    )(proj, proj, proj, proj, proj, proj, pos_col, h_rnn, h_rnn, *saved, d_y, conv_w, rwa, rwx, lam)


def _input_backward(pieces, w_in, x, dx2, mod_row, norm_g):
    T = x.shape[0]
    tm = min(T, 512)
    n = len(pieces)

    def body(*refs):
        d_refs = refs[:n]
        w_ref, x_ref, dx2_ref, mod_ref, g_ref, gx_ref, dshift_ref, dscale_ref, dg_ref = refs[n:]
        i = pl.program_id(0)
        dh = None
        for d_ref, (_, start, count) in zip(d_refs, pieces):
            part = _dot_nt(d_ref[...], w_ref[:, start * CB:(start + count) * CB])
            dh = part if dh is None else dh + part

        @pl.when(i == 0)
        def _():
            dshift_ref[...] = jnp.zeros_like(dshift_ref)
            dscale_ref[...] = jnp.zeros_like(dscale_ref)
            dg_ref[...] = jnp.zeros_like(dg_ref)

        xf = x_ref[...]
        r1 = _rms(xf)
        xn = xf * r1
        gn = g_ref[...]
        s1 = 1.0 + mod_ref[:, D_MODEL:2 * D_MODEL]
        dshift_ref[...] += jnp.sum(dh, axis=0, keepdims=True)
        dscale_ref[...] += jnp.sum(dh * (xn * gn), axis=0, keepdims=True)
        dg_ref[...] += jnp.sum(dh * s1 * xn, axis=0, keepdims=True)
        dxn = dh * s1 * gn
        gx_ref[...] = dx2_ref[...] + r1 * (dxn - xn * jnp.mean(dxn * xn, axis=-1, keepdims=True))

    tok = lambda w: pl.BlockSpec((tm, w), lambda i: (i, 0))
    row = lambda w: pl.BlockSpec((1, w), lambda i: (0, 0))
    vec = jax.ShapeDtypeStruct((1, D_MODEL), F32)
    return pl.pallas_call(
        body, out_shape=(jax.ShapeDtypeStruct((T, D_MODEL), F32), vec, vec, vec), grid=(T // tm,), name="input_backward",
        in_specs=[tok(c * CB) for _, _, c in pieces]
        + [pl.BlockSpec((D_MODEL, IN_W), lambda i: (0, 0), pipeline_mode=pl.Buffered(1)), tok(D_MODEL), tok(D_MODEL),
           row(ADA_W), row(D_MODEL)],
        out_specs=(tok(D_MODEL), row(D_MODEL), row(D_MODEL), row(D_MODEL)),
        compiler_params=_cp("arbitrary"),
    )(*[p[0] for p in pieces], w_in, x, dx2, mod_row, norm_g)


def _weight_grad(a, pieces, tag, a_is_transposed=False):
    M, T = a.shape if a_is_transposed else a.shape[::-1]
    n_blocks = sum(count for _, _, count in pieces)
    n = len(pieces)
    contract = _dot if a_is_transposed else _dot_tn

    def body(*refs):
        a_ref, b_refs, o_ref = refs[0], refs[1:1 + n], refs[-1]
        j = pl.program_id(0)
        for b_ref, (_, start, count) in zip(b_refs, pieces):
            @pl.when((j >= start) & (j < start + count))
            def _(b_ref=b_ref):
                o_ref[...] = contract(a_ref[...], b_ref[...])

    def piece_spec(start, count):
        return pl.BlockSpec((T, CB), lambda j: (0, jnp.clip(j - start, 0, count - 1)))

    return pl.pallas_call(
        body, out_shape=jax.ShapeDtypeStruct((M, n_blocks * CB), F32), grid=(n_blocks,), name=f"weight_grad_{tag}",
        in_specs=[pl.BlockSpec(a.shape, lambda j: (0, 0), pipeline_mode=pl.Buffered(1))] + [piece_spec(s, c) for _, s, c in pieces],
        out_specs=pl.BlockSpec((M, CB), lambda j: (0, j)), compiler_params=_cp("arbitrary"),
    )(a, *[p[0] for p in pieces])


def _adamw(w, g, m, v):
    m = ADAM_B1 * m + (1.0 - ADAM_B1) * g
    v = ADAM_B2 * v + (1.0 - ADAM_B2) * (g * g)
    m_hat = m / (1.0 - ADAM_B1 ** ADAM_STEP)
    v_hat = v / (1.0 - ADAM_B2 ** ADAM_STEP)
    delta = -ADAM_LR * (m_hat / (jnp.sqrt(v_hat) + ADAM_EPS) + ADAM_WD * w)
    return delta, m, v


def _sum_landed(kind, owns, lands, where, tag):
    n = len(owns)
    land = lands[0]
    if kind == "in":
        R, C = land.shape[1:]
        tr = 256
        grid = (R // tr,)
        own_spec = pl.BlockSpec((tr, C), lambda i, w: (i, w[0]))
        land_spec = pl.BlockSpec((3, tr, C), lambda i, w: (0, i, 0))
        out_spec = pl.BlockSpec((1, tr, C), lambda i, w: (w[1], i, 0))
        out_shape = (2, R, C)
        pick = lambda ref: ref[...]
    elif kind == "sq":
        R, C = land.shape[1:]
        grid = (1,)
        own_spec = pl.BlockSpec((1, R, C), lambda i, w: (w[0], 0, 0))
        land_spec = pl.BlockSpec((3, R, C), lambda i, w: (0, 0, 0))
        out_spec = pl.BlockSpec((1, R, C), lambda i, w: (w[1], 0, 0))
        out_shape = (2, R, C)
        pick = lambda ref: ref[0]
    else:
        B, R, C = land.shape[1:]
        grid = (1,)
        own_spec = pl.BlockSpec((B, 1, R, C), lambda i, w: (0, w[0], 0, 0))
        land_spec = pl.BlockSpec((3, B, R, C), lambda i, w: (0, 0, 0, 0))
        out_spec = pl.BlockSpec((B, 1, R, C), lambda i, w: (0, w[1], 0, 0))
        out_shape = (B, 2, R, C)
        pick = lambda ref: ref[:, 0]

    def body(w_ref, *refs):
        for k in range(n):
            own_ref, l_ref, o_ref = refs[k], refs[n + k], refs[2 * n + k]
            total = ((pick(own_ref) + l_ref[0].astype(F32)) + l_ref[1].astype(F32)) + l_ref[2].astype(F32)
            if kind == "rg":
                o_ref[:, 0] = total
            else:
                o_ref[0] = total

    grid_spec = pltpu.PrefetchScalarGridSpec(num_scalar_prefetch=1, grid=grid, in_specs=[own_spec] * n + [land_spec] * n,
                                             out_specs=(out_spec,) * n)
    return list(pl.pallas_call(
        body, out_shape=(jax.ShapeDtypeStruct(out_shape, F32),) * n, grid_spec=grid_spec, name=f"sum_landed_{tag}",
        compiler_params=_cp("parallel"),
    )(where, *owns, *lands))


def _adamw_shard(gs, ws, ms, vs, tag):
    n = len(ws)
    R, C = ws[0].shape
    tr = min(R, 256 if n == 1 else 64)

    def body(*refs):
        for k in range(n):
            g = refs[k][...]
            d, nm, nv = _adamw(refs[n + k][...], g, refs[2 * n + k][...], refs[3 * n + k][...])
            out = refs[4 * n + 4 * k:4 * n + 4 * k + 4]
            out[0][...] = g
            out[1][...] = d
            out[2][...] = nm
            out[3][...] = nv

    spec = pl.BlockSpec((tr, C), lambda i: (i, 0))
    sds = jax.ShapeDtypeStruct((R, C), F32)
    outs = pl.pallas_call(
        body, out_shape=(sds,) * (4 * n), grid=(R // tr,), name=f"adamw_{tag}",
        in_specs=[spec] * (4 * n), out_specs=(spec,) * (4 * n), compiler_params=_cp("parallel"),
    )(*gs, *ws, *ms, *vs)
    return [outs[4 * k:4 * k + 4] for k in range(n)]


def _adamw_w_ada(c_t, dmod_cols, w, m, v):
    R, C = w.shape

    def body(ct_ref, dm_ref, w_ref, m_ref, v_ref, g_ref, d_ref, nm_ref, nv_ref):
        g = _dot(ct_ref[...].astype(BF16), dm_ref[...].astype(BF16))
        d, nm, nv = _adamw(w_ref[...], g, m_ref[...], v_ref[...])
        g_ref[...] = g
        d_ref[...] = d
        nm_ref[...] = nm
        nv_ref[...] = nv

    tr = 256
    spec = pl.BlockSpec((tr, C), lambda i: (i, 0))
    sds = jax.ShapeDtypeStruct((R, C), F32)
    return pl.pallas_call(
        body, out_shape=(sds,) * 4, grid=(R // tr,), name="adamw_w_ada",
        in_specs=[pl.BlockSpec((tr, 128), lambda i: (i, 0)), pl.BlockSpec((128, C), lambda i: (0, 0))] + [spec] * 3,
        out_specs=(spec,) * 4, compiler_params=_cp("parallel"),
    )(c_t, dmod_cols, w, m, v)


def _adamw_small(small_all, ws, ms, vs):
    def body(s_ref, w_ref, m_ref, v_ref, g_ref, d_ref, nm_ref, nv_ref):
        g = s_ref[0]
        for b in range(1, N_DEV):
            g = g + s_ref[b]
        d, nm, nv = _adamw(w_ref[...], g, m_ref[...], v_ref[...])
        g_ref[...] = g
        d_ref[...] = d
        nm_ref[...] = nm
        nv_ref[...] = nv

    sds = jax.ShapeDtypeStruct((SMALL_ROWS, D_MODEL), F32)
    return pl.pallas_call(
        body, out_shape=(sds,) * 4, name="adamw_small", in_specs=[VMEM_SPEC] * 4, out_specs=(VMEM_SPEC,) * 4,
        compiler_params=pltpu.CompilerParams(vmem_limit_bytes=VMEM_LIMIT_V7X),
    )(small_all, ws, ms, vs)


ROW_MOD, ROW_NORM_G, ROW_CONV_B, ROW_BA, ROW_BX, ROW_LAM, ROW_FINAL_G, ROW_SINKS, ROW_CONV_W, ROW_LOSS = 0, 3, 4, 5, 6, 7, 8, 9, 10, 14


def _pack_small(b_ada, norm_g, conv_b, ba, bx, lam, final_g, sinks, conv_w_full, loss_row=None):
    lane_pad = lambda a: jnp.pad(a.reshape(1, -1), ((0, 0), (0, D_MODEL - a.size)))
    rows = [b_ada.reshape(3, D_MODEL), norm_g, conv_b, ba, bx, lam, final_g.reshape(1, D_MODEL), lane_pad(sinks), conv_w_full,
            jnp.zeros((1, D_MODEL), F32) if loss_row is None else lane_pad(loss_row),
            jnp.zeros((SMALL_ROWS - ROW_LOSS - 1, D_MODEL), F32)]
    return jnp.concatenate([r.astype(F32) for r in rows], axis=0)


def kernel(x, c, positions, w_ada, b_ada, norm_g, w_in, attn_sinks, conv_w, conv_b, rg_wa, rg_ba, rg_wx, rg_bx, rg_lambda, w_attn_proj, w_rnn_proj, w_out, final_g, loss_target, m_w_ada, m_b_ada, m_norm_g, m_w_in, m_attn_sinks, m_conv_w, m_conv_b, m_rg_wa, m_rg_ba, m_rg_wx, m_rg_bx, m_rg_lambda, m_w_attn_proj, m_w_rnn_proj, m_w_out, m_final_g, v_w_ada, v_b_ada, v_norm_g, v_w_in, v_attn_sinks, v_conv_w, v_conv_b, v_rg_wa, v_rg_ba, v_rg_wx, v_rg_bx, v_rg_lambda, v_w_attn_proj, v_w_rnn_proj, v_w_out, v_final_g):
    T = x.shape[1]
    my_chip = lax.axis_index("x") * 2 + lax.axis_index("y")
    my_dev = my_chip * 2 + lax.axis_index("c")
    x2d, tgt = x[0], loss_target[0]
    pos_col = positions.reshape(T, 1)

    chip_idx = my_chip.reshape(1).astype(jnp.int32)
    c_idx = lax.axis_index("c").reshape(1).astype(jnp.int32)
    sq_place = ((D_MODEL, D_MODEL), (SHARD_ROWS, D_MODEL), lambda chip: (chip, 0))
    rg_place = ((RNN_BLOCKS, RNN_BW, RNN_BW), (RNN_BLOCKS, SHARD_RG, RNN_BW), lambda chip: (0, chip, 0))
    in_place = ((D_MODEL, IN_W), (D_MODEL, SHARD_IN), lambda chip: (0, chip))
    placed = _cast_place([w_in[0], w_attn_proj[0], w_rnn_proj[0], w_out[0], rg_wa[0], rg_wx[0]], chip_idx,
                         [in_place, sq_place, sq_place, sq_place, rg_place, rg_place])
    cw_chips, c_all, mod_chips = _gather_mod(c.reshape(1, 1, D_MODEL), w_ada[0], conv_w[0])
    g_ssems, g_rsems, fulls, g_token = _gather_start([p.reshape(s) for p, s in zip(placed, FULL_SHAPES)], mod_chips)
    conv_w_f = jnp.transpose(cw_chips, (1, 0, 2)).reshape(CONV_W, D_MODEL)
    mod_all = jnp.transpose(mod_chips, (1, 0, 2)).reshape(N_DEV, ADA_W) + b_ada
    mod_row = lax.dynamic_slice_in_dim(mod_all, my_dev, 1, axis=0) + g_token[0:1, 0:1]

    h, h_t, tabs = _prenorm(x2d, mod_row, norm_g, pos_col)
    w_in_v = fulls[0]
    proj = _in_projection(h, w_in_v.reshape(D_MODEL, IN_W), chip_idx, None, "own")
    for k, mask in enumerate(CHIP_MASKS):
        w_in_v = _gather_wait(g_ssems[k], g_rsems[k], [w_in_v], [0], proj, f"w_in_{k}")[0]
        w_in_v = _forward_halves([w_in_v], [(0, 0, k)], f"w_in_{k}")[0]
        from_chip = (chip_idx ^ (mask >> 1)).astype(jnp.int32)
        proj = _in_projection(h, w_in_v.reshape(D_MODEL, IN_W), from_chip, proj, f"from_{k}")
    w_in_f = w_in_v.reshape(D_MODEL, IN_W)
    rest = _gather_wait(g_ssems[3], g_rsems[3], list(fulls[1:]), [1, 2, 3, 4, 5], proj, "rest")
    rest = _forward_halves(rest, [(idx - 1, idx, k) for idx in range(1, N_BIG) for k in range(3)], "rest")
    wap_f, wrp_f, wo_f = (g.reshape(D_MODEL, D_MODEL) for g in rest[0:3])
    rwa_f, rwx_f = (g.reshape(RNN_BLOCKS, RNN_BW, RNN_BW) for g in rest[3:5])
    y_attn, qr_b, kr_b = _attn_forward(proj, tabs, attn_sinks)
    y_rnn, h_rnn, *rnn_saved = _rnn_forward(proj, pos_col, conv_w_f, conv_b, rwa_f, rwx_f, rg_ba, rg_bx, rg_lambda)
    (dx2, merged, d_o, d_pa, d_pr, d_ya, d_yr, d_c, d_final_g, d_gate, loss_vec) = _merge_and_head(
        x2d, tgt, y_attn, y_rnn, proj, wap_f, wrp_f, wo_f, mod_row, final_g.reshape(1, D_MODEL))

    sq = (N_CHIPS, 2, SHARD_ROWS // 2, D_MODEL)
    rg = (RNN_BLOCKS, N_CHIPS, 2, SHARD_RG // 2, RNN_BW)
    rg_flat = (RNN_BLOCKS * N_CHIPS, 2, SHARD_RG // 2, RNN_BW)

    def chip_sum_and_start(views, axes, flat, unflat, tags_, kinds_, group):
        from_sib = _swap_halves(views, axes)
        exact, rounded = [None] * len(views), [None] * len(views)
        for shape in dict.fromkeys(flat):
            ids = [k for k, f in enumerate(flat) if f == shape]
            ex, ro = _presum([views[k].reshape(shape) for k in ids],
                             [from_sib[k].reshape(shape[:1] + shape[2:]) for k in ids], c_idx, tags_[ids[0]])
            for k, e, r in zip(ids, ex, ro):
                exact[k], rounded[k] = e.reshape(unflat[k]), r.reshape(unflat[k])
        return _exchange_start(rounded, kinds_, group), exact

    g_ap = _weight_grad(y_attn, [(d_pa, 0, 2)], "w_attn_proj")
    g_rp = _weight_grad(y_rnn, [(d_pr, 0, 2)], "w_rnn_proj")
    g_o = _weight_grad(merged, [(d_o, 0, 2)], "w_out")
    sq_half = (N_CHIPS, SHARD_ROWS // 2, D_MODEL)
    started1, own1 = chip_sum_and_start([g_ap.reshape(sq), g_rp.reshape(sq), g_o.reshape(sq)], [1, 1, 1], [sq] * 3, [sq_half] * 3,
                                  ["w_attn_proj", "w_rnn_proj", "w_out"], ["sq"] * 3, "proj")
    d_q, d_kv, d_ga, d_sinks = _attn_backward(proj, qr_b, kr_b, d_ya, tabs, attn_sinks + started1[4][0, 0])
    d_b, d_conv_w, d_conv_b, d_rwa, d_rwx, d_ba, d_bx, d_lam = _rnn_backward(
        proj, pos_col, h_rnn, rnn_saved, d_yr, conv_w_f, rwa_f, rwx_f, rg_lambda)
    pieces = [(d_q, CB_Q, 2), (d_kv, CB_KV, 1), (d_ga, CB_GA, 2), (d_b, CB_XR, 4), (d_c, CB_MA, 4)]
    g_in = _weight_grad(h_t, pieces, "w_in", a_is_transposed=True)
    started2, own2 = chip_sum_and_start(
        [g_in.reshape(2, D_MODEL // 2, IN_W), d_rwa.reshape(rg), d_rwx.reshape(rg)], [0, 2, 2],
        [(1, 2, D_MODEL // 2, IN_W), rg_flat, rg_flat],
        [(D_MODEL // 2, IN_W), (RNN_BLOCKS, N_CHIPS, SHARD_RG // 2, RNN_BW), (RNN_BLOCKS, N_CHIPS, SHARD_RG // 2, RNN_BW)],
        ["w_in", "rg_wa", "rg_wx"], ["in", "rg", "rg"], "in")
    grad_x, d_shift, d_scale, d_norm_g = _input_backward(pieces, w_in_f, x2d, dx2, mod_row + started2[4][0, 0], norm_g)

    d_mod = jnp.concatenate([d_shift, d_scale, d_gate], axis=1)
    small = _pack_small(d_mod, d_norm_g, d_conv_b, d_ba, d_bx, d_lam, d_final_g, d_sinks[:, :N_HEADS], d_conv_w, loss_vec)
    small_all = _gather_small(small)
    _, lands1 = _exchange_wait(*started1[:4], grad_x, "proj")
    _, lands2 = _exchange_wait(*started2[:4], grad_x, "in")
    tags = ["w_in", "w_attn_proj", "w_rnn_proj", "w_out", "rg_wa", "rg_wx"]
    chip_sums = [own2[0]] + list(own1) + list(own2[1:])
    lands = [lands2[0]] + list(lands1) + list(lands2[1:])
    where = jnp.concatenate([chip_idx, c_idx])
    kinds = ["in", "sq", "sq", "sq", "rg", "rg"]
    groups = [[0], [1, 2, 3], [4, 5]]
    halves = [None] * 6
    for ids in groups:
        for i, half in zip(ids, _sum_landed(kinds[ids[0]], [chip_sums[i] for i in ids], [lands[i] for i in ids], where,
                                            tags[ids[0]])):
            halves[i] = half
    grads = _assemble_with_sibling(halves, [0, 0, 0, 0, 1, 1])
    shapes2d = [(D_MODEL, SHARD_IN), (SHARD_ROWS, D_MODEL), (SHARD_ROWS, D_MODEL), (SHARD_ROWS, D_MODEL),
                (RNN_BLOCKS * SHARD_RG, RNN_BW), (RNN_BLOCKS * SHARD_RG, RNN_BW)]
    big_w = [w_in, w_attn_proj, w_rnn_proj, w_out, rg_wa, rg_wx]
    big_m = [m_w_in, m_w_attn_proj, m_w_rnn_proj, m_w_out, m_rg_wa, m_rg_wx]
    big_v = [v_w_in, v_w_attn_proj, v_w_rnn_proj, v_w_out, v_rg_wa, v_rg_wx]
    res = {}
    for ids in groups:
        flat2d = lambda arrs: [arrs[i].reshape(shapes2d[i]) for i in ids]
        outs = _adamw_shard(flat2d(grads), flat2d(big_w), flat2d(big_m), flat2d(big_v), tags[ids[0]])
        for i, four in zip(ids, outs):
            res[tags[i]] = [o.reshape(big_w[i].shape) for o in four]

    dmod_all = small_all[:, ROW_MOD:ROW_MOD + 3, :].reshape(N_DEV, ADA_W)
    dmod_cols = lax.dynamic_slice_in_dim(dmod_all, my_chip * SHARD_ADA, SHARD_ADA, axis=1)
    c_t = jnp.pad(jnp.transpose(c_all.reshape(N_DEV, D_MODEL)), ((0, 0), (0, 128 - N_DEV)))
    dmod_cols = jnp.pad(dmod_cols, ((0, 128 - N_DEV), (0, 0)))
    res["w_ada"] = [o.reshape(w_ada.shape) for o in _adamw_w_ada(c_t, dmod_cols, w_ada[0], m_w_ada[0], v_w_ada[0])]

    def full_conv(a):
        return lax.dynamic_update_slice_in_dim(jnp.zeros((CONV_W, D_MODEL), F32), a[0], my_chip * (D_MODEL // N_CHIPS), axis=1)

    packed = [_pack_small(p[0], p[1], p[2], p[3], p[4], p[5], p[6], p[7], full_conv(p[8])) for p in (
        (b_ada, norm_g, conv_b, rg_ba, rg_bx, rg_lambda, final_g, attn_sinks, conv_w),
        (m_b_ada, m_norm_g, m_conv_b, m_rg_ba, m_rg_bx, m_rg_lambda, m_final_g, m_attn_sinks, m_conv_w),
        (v_b_ada, v_norm_g, v_conv_b, v_rg_ba, v_rg_bx, v_rg_lambda, v_final_g, v_attn_sinks, v_conv_w))]
    small_out = _adamw_small(small_all, *packed)

    def unpack(slab):
        cw = lax.dynamic_slice_in_dim(slab[ROW_CONV_W:ROW_CONV_W + CONV_W], my_chip * (D_MODEL // N_CHIPS),
                                      D_MODEL // N_CHIPS, axis=1)
        return {
            "b_ada": slab[ROW_MOD:ROW_MOD + 3].reshape(1, ADA_W), "norm_g": slab[ROW_NORM_G:ROW_NORM_G + 1],
            "conv_b": slab[ROW_CONV_B:ROW_CONV_B + 1], "rg_ba": slab[ROW_BA:ROW_BA + 1], "rg_bx": slab[ROW_BX:ROW_BX + 1],
            "rg_lambda": slab[ROW_LAM:ROW_LAM + 1], "final_g": slab[ROW_FINAL_G], "attn_sinks": slab[ROW_SINKS:ROW_SINKS + 1, :N_HEADS],
            "conv_w": cw[None],
        }

    small_res = [unpack(s) for s in small_out]
    order = ["w_ada", "b_ada", "norm_g", "w_in", "attn_sinks", "conv_w", "conv_b", "rg_wa", "rg_ba", "rg_wx", "rg_bx",
             "rg_lambda", "w_attn_proj", "w_rnn_proj", "w_out", "final_g"]
    loss = small_out[0][ROW_LOSS, 0]
    outs = [loss, grad_x[None]]
    for kind in range(4):
        for name in order:
            outs.append(res[name][kind] if name in res else small_res[kind][name])
    return tuple(outs)
```

```python
import numpy as np
import jax
import jax.numpy as jnp
from jax import lax
from jax.experimental import pallas as pl
from jax.experimental.pallas import tpu as pltpu

F32 = jnp.float32
BF16 = jnp.bfloat16

D_MODEL = 1024
N_HEADS = 16
N_KV = 4
HEAD_DIM = 64
GROUP = N_HEADS // N_KV
BLOCK = 128
KV_W = N_KV * HEAD_DIM
ROT_HALF = 8
ROPE_THETA = 500000.0
ATTN_SCALE = 0.125
RNN_BLOCKS = 4
RNN_BW = 256
CONV_W = 4
LRU_C = 8.0
NORM_EPS = 1e-6
IN_W = 6656
CB = 512
N_CB = IN_W // CB
CB_Q, CB_KV, CB_GA, CB_XR, CB_GR, CB_MA, CB_MR = 0, 2, 3, 5, 7, 9, 11
V_COL_BLOCK = 5
N_CHIPS = 4
N_DEV = 8
SHARD_IN = IN_W // N_CHIPS
SHARD_ROWS = D_MODEL // N_CHIPS
SHARD_RG = RNN_BW // N_CHIPS
ADA_W = 3 * D_MODEL
SHARD_ADA = ADA_W // N_CHIPS
SMALL_ROWS = 16

ADAM_LR = 0.001
ADAM_B1 = 0.9
ADAM_B2 = 0.999
ADAM_EPS = 1e-08
ADAM_WD = 0.01
ADAM_STEP = 10

VMEM_LIMIT_V7X = 52 * 1024 * 1024
MESH = pl.DeviceIdType.MESH
ANY = pl.BlockSpec(memory_space=pl.ANY)
VMEM_SPEC = pl.BlockSpec(memory_space=pltpu.VMEM)


def _cp(*sem):
    return pltpu.CompilerParams(dimension_semantics=sem if sem else None, vmem_limit_bytes=VMEM_LIMIT_V7X)


def _dot(a, b):
    return jnp.dot(a, b, preferred_element_type=F32)


def _dot_nt(a, b):
    return lax.dot_general(a, b, (((1,), (1,)), ((), ())), preferred_element_type=F32)


def _dot_tn(a, b):
    return lax.dot_general(a, b, (((0,), (0,)), ((), ())), preferred_element_type=F32)


def _sigmoid(z):
    return 1.0 / (1.0 + jnp.exp(-z))


def _softplus(z):
    u = jnp.exp(-jnp.abs(z))
    log1p_u = jnp.where(u < 1e-3, u * (1.0 - u * (0.5 - u * (1.0 / 3.0))), jnp.log(1.0 + u))
    return jnp.maximum(z, 0.0) + log1p_u


def _rms(xf):
    return lax.rsqrt(jnp.mean(xf * xf, axis=-1, keepdims=True) + NORM_EPS)


def _me():
    return lax.axis_index("x"), lax.axis_index("y"), lax.axis_index("c")


def _peer(mask):
    x, y, c = _me()
    fx, fy, fc = (mask >> 2) & 1, (mask >> 1) & 1, mask & 1
    return (x ^ fx if fx else x, y ^ fy if fy else y, c ^ fc if fc else c)


def _chip_of(pos):
    return pos[0] * 2 + pos[1]


SIBLING_COLLECTIVE_ID = 0
SIBLING_ONLY = pltpu.CompilerParams(collective_id=SIBLING_COLLECTIVE_ID)


def _sibling_handshake():
    barrier = pltpu.get_barrier_semaphore()
    pl.semaphore_signal(barrier, inc=1, device_id=_peer(1), device_id_type=MESH)
    pl.semaphore_wait(barrier, 1)


CHIP_MASKS = (4, 2, 6)
ALL_MASKS = (1, 2, 3, 4, 5, 6, 7)


HBM_SPEC = pl.BlockSpec(memory_space=pltpu.HBM)
SEM_SPEC = pl.BlockSpec(memory_space=pltpu.SEMAPHORE)
SPLIT_COPY = pltpu.CompilerParams(has_side_effects=pltpu.SideEffectType.DATAFLOW_SIDE_EFFECTING)
N_BIG = 6
FULL_SHAPES = (
    (2, D_MODEL // 2, IN_W),
    (N_CHIPS, 2, SHARD_ROWS // 2, D_MODEL), (N_CHIPS, 2, SHARD_ROWS // 2, D_MODEL), (N_CHIPS, 2, SHARD_ROWS // 2, D_MODEL),
    (RNN_BLOCKS, N_CHIPS, 2, SHARD_RG // 2, RNN_BW), (RNN_BLOCKS, N_CHIPS, 2, SHARD_RG // 2, RNN_BW),
)


def _slot(full, idx, chip, half):
    if idx == 0:
        return full.at[half, :, pl.ds(pl.multiple_of(chip * SHARD_IN, 128), SHARD_IN)]
    return full.at[chip, half] if idx in (1, 2, 3) else full.at[:, chip, half]


def _three_halves(full, idx):
    return full.at[pl.ds(0, 3), 0] if idx in (1, 2, 3) else full.at[:, pl.ds(0, 3), 0]


def _gather_start(fulls, after):
    def body(*refs):
        full_refs = refs[:N_BIG]
        ssems, rsems = refs[N_BIG + 1:N_BIG + 5], refs[N_BIG + 5:N_BIG + 9]
        token = refs[2 * N_BIG + 9]
        me = _me()
        my_chip = _chip_of(me)
        for idx in range(N_BIG):
            for k, mask in enumerate(CHIP_MASKS):
                pair = k if idx == 0 else 3
                mine = _slot(full_refs[idx], idx, my_chip, me[2])
                pltpu.make_async_remote_copy(src_ref=mine, dst_ref=mine, send_sem=ssems[pair], recv_sem=rsems[pair],
                                             device_id=_peer(mask), device_id_type=MESH).start()
        token[...] = jnp.zeros_like(token)

    sem = pltpu.SemaphoreType.DMA(())
    out_shape = (sem,) * 8 + tuple(pltpu.HBM(f.shape, f.dtype) for f in fulls) + (jax.ShapeDtypeStruct((8, 128), F32),)
    outs = pl.pallas_call(
        body, out_shape=out_shape, name="gather_start",
        in_specs=[HBM_SPEC] * N_BIG + [ANY], out_specs=tuple([SEM_SPEC] * 8 + [HBM_SPEC] * N_BIG + [VMEM_SPEC]),
        input_output_aliases={i: 8 + i for i in range(N_BIG)}, compiler_params=SPLIT_COPY,
    )(*[pltpu.with_memory_space_constraint(f, pltpu.HBM) for f in fulls], after)
    return outs[0:4], outs[4:8], outs[8:8 + N_BIG], outs[8 + N_BIG]


def _gather_wait(ssem, rsem, arrays, idxs, after, tag):
    n = len(arrays)

    def body(*refs):
        full_refs, ssem_ref, rsem_ref = refs[:n], refs[n], refs[n + 1]
        me = _me()
        for full, idx in zip(full_refs, idxs):
            region = _slot(full, 0, _chip_of(me), me[2]) if idx == 0 else _three_halves(full, idx)
            arrived = pltpu.make_async_remote_copy(
                src_ref=region, dst_ref=region, send_sem=ssem_ref, recv_sem=rsem_ref, device_id=me, device_id_type=MESH)
            arrived.wait_send()
            arrived.wait_recv()

    outs = pl.pallas_call(
        body, out_shape=tuple(pltpu.HBM(a.shape, a.dtype) for a in arrays), name=f"gather_wait_{tag}",
        in_specs=[HBM_SPEC] * n + [SEM_SPEC, SEM_SPEC, ANY], out_specs=tuple([HBM_SPEC] * n),
        input_output_aliases={i: i for i in range(n)}, compiler_params=SPLIT_COPY,
    )(*arrays, ssem, rsem, after)
    return list(outs)


def _forward_halves(arrays, items, tag):
    n, m = len(arrays), len(items)

    def body(*refs):
        outs, ssem, rsem = refs[n:2 * n], refs[2 * n], refs[2 * n + 1]
        me = _me()
        sib = _peer(1)
        _sibling_handshake()
        cps = []
        for j, (pos, idx, k) in enumerate(items):
            chip = _chip_of(_peer(CHIP_MASKS[k]))
            cp = pltpu.make_async_remote_copy(
                src_ref=_slot(outs[pos], idx, chip, me[2]), dst_ref=_slot(outs[pos], idx, chip, me[2]),
                send_sem=ssem.at[j], recv_sem=rsem.at[j], device_id=sib, device_id_type=MESH)
            cp.start()
            cps.append(cp)
        for j, (pos, idx, k) in enumerate(items):
            chip = _chip_of(_peer(CHIP_MASKS[k]))
            pltpu.make_async_remote_copy(
                src_ref=_slot(outs[pos], idx, chip, me[2]), dst_ref=_slot(outs[pos], idx, chip, 1 - me[2]),
                send_sem=ssem.at[j], recv_sem=rsem.at[j], device_id=sib, device_id_type=MESH).wait_recv()
        for cp in cps:
            cp.wait_send()

    outs = pl.pallas_call(
        body, out_shape=tuple(jax.ShapeDtypeStruct(a.shape, a.dtype) for a in arrays), name=f"forward_halves_{tag}",
        in_specs=[ANY] * n, out_specs=tuple([ANY] * n), input_output_aliases={i: i for i in range(n)},
        scratch_shapes=[pltpu.SemaphoreType.DMA((m,)), pltpu.SemaphoreType.DMA((m,))], compiler_params=SIBLING_ONLY,
    )(*arrays)
    return list(outs)


def _gather_mod(c_row, w_ada_s, conv_w_s):
    def body(c_ref, wada_ref, cw_s, cw_f, call_ref, mod_ref, wsend, wrecv, lsem, csend, crecv, msend, mrecv):
        me = _me()
        my_chip = _chip_of(me)
        my_dev = my_chip * 2 + me[2]
        sends = []
        for k, mask in enumerate(CHIP_MASKS):
            cp = pltpu.make_async_remote_copy(src_ref=cw_s, dst_ref=cw_f.at[my_chip], send_sem=wsend.at[k], recv_sem=wrecv.at[k],
                                              device_id=_peer(mask), device_id_type=MESH)
            cp.start()
            sends.append(cp)
        local = [pltpu.make_async_copy(cw_s, cw_f.at[my_chip], lsem.at[0])]
        for cp in local:
            cp.start()

        call_ref[my_dev] = c_ref[0]
        csends = []
        for k, mask in enumerate(ALL_MASKS):
            cp = pltpu.make_async_remote_copy(
                src_ref=c_ref.at[0], dst_ref=call_ref.at[my_dev],
                send_sem=csend.at[k], recv_sem=crecv.at[k], device_id=_peer(mask), device_id_type=MESH)
            cp.start()
            csends.append(cp)
        for k, mask in enumerate(ALL_MASKS):
            frm = _peer(mask)
            pltpu.make_async_remote_copy(
                src_ref=c_ref.at[0], dst_ref=call_ref.at[_chip_of(frm) * 2 + frm[2]],
                send_sem=csend.at[k], recv_sem=crecv.at[k], device_id=frm, device_id_type=MESH).wait_recv()
        for cp in csends:
            cp.wait_send()

        c_all = call_ref[...].reshape(N_DEV, D_MODEL).astype(BF16)
        mod_ref[my_chip] = _dot(c_all, wada_ref[...].astype(BF16))
        msends = []
        for k, mask in enumerate(CHIP_MASKS):
            cp = pltpu.make_async_remote_copy(
                src_ref=mod_ref.at[my_chip], dst_ref=mod_ref.at[my_chip],
                send_sem=msend.at[k], recv_sem=mrecv.at[k], device_id=_peer(mask), device_id_type=MESH)
            cp.start()
            msends.append(cp)
        for k, mask in enumerate(CHIP_MASKS):
            frm = _peer(mask)
            pltpu.make_async_remote_copy(
                src_ref=mod_ref.at[my_chip], dst_ref=mod_ref.at[_chip_of(frm)],
                send_sem=msend.at[k], recv_sem=mrecv.at[k], device_id=frm, device_id_type=MESH).wait_recv()
        for cp in msends:
            cp.wait_send()

        for k, mask in enumerate(CHIP_MASKS):
            frm = _peer(mask)
            pltpu.make_async_remote_copy(src_ref=cw_s, dst_ref=cw_f.at[_chip_of(frm)], send_sem=wsend.at[k], recv_sem=wrecv.at[k],
                                         device_id=frm, device_id_type=MESH).wait_recv()
        for cp in sends:
            cp.wait_send()
        for cp in local:
            cp.wait()

    out_shape = (
        jax.ShapeDtypeStruct((N_CHIPS, CONV_W, D_MODEL // N_CHIPS), F32),
        jax.ShapeDtypeStruct((N_DEV, 1, D_MODEL), F32),
        jax.ShapeDtypeStruct((N_CHIPS, N_DEV, SHARD_ADA), F32),
    )
    return pl.pallas_call(
        body, out_shape=out_shape, name="gather_mod",
        in_specs=[VMEM_SPEC, VMEM_SPEC, ANY], out_specs=(ANY, VMEM_SPEC, VMEM_SPEC),
        scratch_shapes=[
            pltpu.SemaphoreType.DMA((3,)), pltpu.SemaphoreType.DMA((3,)), pltpu.SemaphoreType.DMA((1,)),
            pltpu.SemaphoreType.DMA((7,)), pltpu.SemaphoreType.DMA((7,)),
            pltpu.SemaphoreType.DMA((3,)), pltpu.SemaphoreType.DMA((3,)),
        ],
        compiler_params=pltpu.CompilerParams(vmem_limit_bytes=VMEM_LIMIT_V7X),
    )(c_row, w_ada_s, conv_w_s)


def _cast_place(shards, chip_idx, places):
    n = len(shards)

    def body(chip_ref, *refs):
        for s_ref, o_ref in zip(refs[:n], refs[n:]):
            o_ref[...] = s_ref[...].astype(BF16)

    grid_spec = pltpu.PrefetchScalarGridSpec(
        num_scalar_prefetch=1, grid=(1,),
        in_specs=[pl.BlockSpec(s.shape, lambda i, chip_ref, nd=s.ndim: (0,) * nd) for s in shards],
        out_specs=tuple(pl.BlockSpec(block, lambda i, chip_ref, im=im: im(chip_ref[0])) for _, block, im in places))
    return pl.pallas_call(
        body, out_shape=tuple(jax.ShapeDtypeStruct(full, BF16) for full, _, _ in places), grid_spec=grid_spec,
        name="cast_place", compiler_params=_cp("arbitrary"),
    )(chip_idx, *shards)


def _shard_of(ref, kind, chip):
    if kind == "in":
        return ref.at[:, pl.ds(pl.multiple_of(chip * SHARD_IN, 128), SHARD_IN)]
    return ref.at[chip] if kind == "sq" else ref.at[:, chip]


def _land_shape(src, kind):
    if kind == "in":
        return (3, src.shape[0], SHARD_IN)
    return (3,) + src.shape[1:] if kind == "sq" else (3, src.shape[0]) + src.shape[2:]


def _exchange_start(srcs, kinds, tag):
    n = len(srcs)
    lands = [pltpu.with_memory_space_constraint(lax.empty(_land_shape(s, k), s.dtype), pltpu.HBM) for s, k in zip(srcs, kinds)]

    def body(*refs):
        src_refs, land_refs = refs[:n], refs[n:2 * n]
        ssems, rsems = refs[2 * n:3 * n], refs[3 * n:4 * n]
        token = refs[6 * n]
        for i in range(n):
            for k, mask in enumerate(CHIP_MASKS):
                to = _peer(mask)
                pltpu.make_async_remote_copy(
                    src_ref=_shard_of(src_refs[i], kinds[i], _chip_of(to)), dst_ref=land_refs[i].at[k],
                    send_sem=ssems[i], recv_sem=rsems[i], device_id=to, device_id_type=MESH).start()
        token[...] = jnp.zeros_like(token)

    sem = pltpu.SemaphoreType.DMA(())
    out_shape = ((sem,) * (2 * n) + tuple(pltpu.HBM(s.shape, s.dtype) for s in srcs)
                 + tuple(pltpu.HBM(l.shape, l.dtype) for l in lands) + (jax.ShapeDtypeStruct((8, 128), F32),))
    outs = pl.pallas_call(
        body, out_shape=out_shape, name=f"exchange_start_{tag}",
        in_specs=[HBM_SPEC] * (2 * n), out_specs=tuple([SEM_SPEC] * (2 * n) + [HBM_SPEC] * (2 * n) + [VMEM_SPEC]),
        input_output_aliases={i: 2 * n + i for i in range(2 * n)},
        compiler_params=pltpu.CompilerParams(has_side_effects=pltpu.SideEffectType.DATAFLOW_SIDE_EFFECTING),
    )(*[pltpu.with_memory_space_constraint(s, pltpu.HBM) for s in srcs], *lands)
    return outs[:n], outs[n:2 * n], outs[2 * n:3 * n], outs[3 * n:4 * n], outs[4 * n]


def _exchange_wait(ssems, rsems, srcs, lands, after, tag):
    n = len(srcs)

    def body(*refs):
        land_refs = refs[n:2 * n]
        ssem_refs, rsem_refs = refs[2 * n:3 * n], refs[3 * n:4 * n]
        for i in range(n):
            all_three = pltpu.make_async_remote_copy(
                src_ref=land_refs[i], dst_ref=land_refs[i], send_sem=ssem_refs[i], recv_sem=rsem_refs[i],
                device_id=_me(), device_id_type=MESH)
            all_three.wait_send()
            all_three.wait_recv()

    outs = pl.pallas_call(
        body, out_shape=tuple(pltpu.HBM(a.shape, a.dtype) for a in list(srcs) + list(lands)), name=f"exchange_wait_{tag}",
        in_specs=[HBM_SPEC] * (2 * n) + [SEM_SPEC] * (2 * n) + [ANY], out_specs=tuple([HBM_SPEC] * (2 * n)),
        input_output_aliases={i: i for i in range(2 * n)},
        compiler_params=pltpu.CompilerParams(has_side_effects=pltpu.SideEffectType.DATAFLOW_SIDE_EFFECTING),
    )(*srcs, *lands, *ssems, *rsems, after)
    return outs[:n], outs[n:]


def _gather_small(small):
    def body(small_ref, small_all, ssend, srecv):
        me = _me()
        my_dev = _chip_of(me) * 2 + me[2]
        small_all[my_dev] = small_ref[...]
        ssends = []
        for k, mask in enumerate(ALL_MASKS):
            cp = pltpu.make_async_remote_copy(
                src_ref=small_ref, dst_ref=small_all.at[my_dev],
                send_sem=ssend.at[k], recv_sem=srecv.at[k], device_id=_peer(mask), device_id_type=MESH)
            cp.start()
            ssends.append(cp)
        for k, mask in enumerate(ALL_MASKS):
            frm = _peer(mask)
            pltpu.make_async_remote_copy(
                src_ref=small_ref, dst_ref=small_all.at[_chip_of(frm) * 2 + frm[2]],
                send_sem=ssend.at[k], recv_sem=srecv.at[k], device_id=frm, device_id_type=MESH).wait_recv()
        for cp in ssends:
            cp.wait_send()

    return pl.pallas_call(
        body, out_shape=jax.ShapeDtypeStruct((N_DEV, SMALL_ROWS, D_MODEL), F32), name="gather_small",
        in_specs=[VMEM_SPEC], out_specs=VMEM_SPEC,
        scratch_shapes=[pltpu.SemaphoreType.DMA((7,)), pltpu.SemaphoreType.DMA((7,))],
    )(small)


def _half_of(ref, axis, half):
    return ref.at[(slice(None),) * axis + (half,)]


def _swap_halves(parts, axes):
    n = len(parts)

    def body(*refs):
        ins, outs, ssem, rsem = refs[:n], refs[n:2 * n], refs[2 * n], refs[2 * n + 1]
        c = lax.axis_index("c")
        _sibling_handshake()
        cps = [pltpu.make_async_remote_copy(src_ref=_half_of(ins[i], axes[i], 1 - c), dst_ref=outs[i], send_sem=ssem.at[i],
                                            recv_sem=rsem.at[i], device_id=_peer(1), device_id_type=MESH) for i in range(n)]
        for cp in cps:
            cp.start()
        for cp in cps:
            cp.wait()

    shapes = [p.shape[:a] + p.shape[a + 1:] for p, a in zip(parts, axes)]
    return pl.pallas_call(
        body, out_shape=tuple(jax.ShapeDtypeStruct(s, p.dtype) for s, p in zip(shapes, parts)), name="swap_halves",
        in_specs=[ANY] * n, out_specs=tuple([ANY] * n),
        scratch_shapes=[pltpu.SemaphoreType.DMA((n,)), pltpu.SemaphoreType.DMA((n,))], compiler_params=SIBLING_ONLY,
    )(*parts)


def _presum(mines, sibs, c_idx, tag):
    n = len(mines)
    S, _, R, C = mines[0].shape
    tr = min(R, 256)
    tc = SHARD_IN if C % SHARD_IN == 0 else (C // 2 if n > 1 and C % 256 == 0 else C)

    def body(c_ref, *refs):
        for k in range(n):
            total = refs[k][:, 0] + refs[n + k][...]
            refs[2 * n + k][...] = total
            refs[3 * n + k][...] = total.astype(BF16)

    out_spec = pl.BlockSpec((S, tr, tc), lambda i, j, c_ref: (0, i, j))
    grid_spec = pltpu.PrefetchScalarGridSpec(
        num_scalar_prefetch=1, grid=(R // tr, C // tc),
        in_specs=[pl.BlockSpec((S, 1, tr, tc), lambda i, j, c_ref: (0, c_ref[0], i, j))] * n + [out_spec] * n,
        out_specs=(out_spec,) * (2 * n))
    outs = pl.pallas_call(
        body, out_shape=(jax.ShapeDtypeStruct((S, R, C), F32),) * n + (jax.ShapeDtypeStruct((S, R, C), BF16),) * n,
        grid_spec=grid_spec, name=f"presum_{tag}", compiler_params=_cp("parallel", "parallel"),
    )(c_idx, *mines, *sibs)
    return list(outs[:n]), list(outs[n:])


def _assemble_with_sibling(parts, axes):
    n = len(parts)

    def body(*refs):
        outs, ssem, rsem = refs[n:2 * n], refs[2 * n], refs[2 * n + 1]
        c = lax.axis_index("c")
        _sibling_handshake()
        cps = [pltpu.make_async_remote_copy(
            src_ref=_half_of(outs[i], axes[i], c), dst_ref=_half_of(outs[i], axes[i], c), send_sem=ssem.at[i],
            recv_sem=rsem.at[i], device_id=_peer(1), device_id_type=MESH) for i in range(n)]
        for cp in cps:
            cp.start()
        for i in range(n):
            pltpu.make_async_remote_copy(
                src_ref=_half_of(outs[i], axes[i], c), dst_ref=_half_of(outs[i], axes[i], 1 - c), send_sem=ssem.at[i],
                recv_sem=rsem.at[i], device_id=_peer(1), device_id_type=MESH).wait_recv()
        for cp in cps:
            cp.wait_send()

    return pl.pallas_call(
        body, out_shape=tuple(jax.ShapeDtypeStruct(p.shape, p.dtype) for p in parts), name="assemble_with_sibling",
        in_specs=[ANY] * n, out_specs=tuple([ANY] * n), input_output_aliases={i: i for i in range(n)},
        scratch_shapes=[pltpu.SemaphoreType.DMA((n,)), pltpu.SemaphoreType.DMA((n,))], compiler_params=SIBLING_ONLY,
    )(*parts)


def _rope_lane_frequencies():
    inv = np.float32(ROPE_THETA) ** (-(np.arange(0, 2 * ROT_HALF, 2, dtype=np.float32)) / np.float32(2 * ROT_HALF))
    lane = np.arange(128) % HEAD_DIM
    return jnp.asarray(np.where(lane < 2 * ROT_HALF, inv[lane % ROT_HALF], 0.0).astype(np.float32)[None, :])


def _rope_tables(pos, freq):
    ang = pos.astype(F32) * freq
    c, s = jnp.cos(ang), jnp.sin(ang)
    m = lax.broadcasted_iota(jnp.int32, ang.shape, 1) & (HEAD_DIM - 1)
    return (jnp.where(m < 2 * ROT_HALF, c, 1.0), jnp.where(m < ROT_HALF, -s, 0.0),
            jnp.where((m >= ROT_HALF) & (m < 2 * ROT_HALF), s, 0.0))


def _columns(t):
    return [t[:, i:i + 128] for i in range(0, t.shape[-1], 128)]


def _rope(t, c, sa, sb):
    return jnp.concatenate(
        [x * c + pltpu.roll(x, 128 - ROT_HALF, 1) * sa + pltpu.roll(x, ROT_HALF, 1) * sb for x in _columns(t)], axis=1)


def _unrope(d, c, sa, sb):
    return jnp.concatenate(
        [x * c + pltpu.roll(x * sa, ROT_HALF, 1) + pltpu.roll(x * sb, 128 - ROT_HALF, 1) for x in _columns(d)], axis=1)


def _prenorm(x, mod_row, norm_g, pos_col):
    T = x.shape[0]
    tm = min(T, 512)

    def body(x_ref, mod_ref, g_ref, pos_ref, f_ref, h_ref, ht_ref, c_ref, sa_ref, sb_ref):
        xf = x_ref[...]
        shift, scale = mod_ref[:, 0:D_MODEL], mod_ref[:, D_MODEL:2 * D_MODEL]
        h = (xf * _rms(xf)) * g_ref[...] * (1.0 + scale) + shift
        h_ref[...] = h.astype(BF16)
        ht_ref[...] = h.T.astype(BF16)
        c_ref[...], sa_ref[...], sb_ref[...] = _rope_tables(pos_ref[...], f_ref[...])

    tab = jax.ShapeDtypeStruct((T, 128), F32)
    tok = lambda w: pl.BlockSpec((tm, w), lambda i: (i, 0))
    row = lambda w: pl.BlockSpec((1, w), lambda i: (0, 0))
    outs = pl.pallas_call(
        body, out_shape=(jax.ShapeDtypeStruct((T, D_MODEL), BF16), jax.ShapeDtypeStruct((D_MODEL, T), BF16), tab, tab, tab),
        grid=(T // tm,), name="prenorm",
        in_specs=[tok(D_MODEL), row(ADA_W), row(D_MODEL), tok(1), row(128)],
        out_specs=(tok(D_MODEL), pl.BlockSpec((D_MODEL, tm), lambda i: (0, i)), tok(128), tok(128), tok(128)),
        compiler_params=_cp("parallel"),
    )(x, mod_row, norm_g, pos_col, _rope_lane_frequencies())
    return outs[0], outs[1], tuple(outs[2:])


def _in_projection(h, w_in, chip, into, tag):
    T = h.shape[0]
    tm, tn = min(T, 512), SHARD_IN

    def body(chip_ref, h_ref, w_ref, *rest):
        rest[-1][...] = _dot(h_ref[...], w_ref[...])

    in_specs = [pl.BlockSpec((tm, D_MODEL), lambda i, c: (i, 0)),
                pl.BlockSpec((D_MODEL, tn), lambda i, c: (0, c[0]), pipeline_mode=pl.Buffered(1))]
    args = [chip, h, w_in]
    aliases = {}
    if into is not None:
        in_specs.append(ANY)
        args.append(into)
        aliases = {3: 0}
    grid_spec = pltpu.PrefetchScalarGridSpec(num_scalar_prefetch=1, grid=(T // tm,), in_specs=in_specs,
                                             out_specs=pl.BlockSpec((tm, tn), lambda i, c: (i, c[0])))
    return pl.pallas_call(
        body, out_shape=jax.ShapeDtypeStruct((T, IN_W), F32), grid_spec=grid_spec, name=f"in_projection_{tag}",
        input_output_aliases=aliases, compiler_params=_cp("parallel"),
    )(*args)


def _attn_mask(n):
    qi = lax.broadcasted_iota(jnp.int32, (GROUP * BLOCK, BLOCK), 0) & (BLOCK - 1)
    j = lax.broadcasted_iota(jnp.int32, (GROUP * BLOCK, BLOCK), 1)
    own = j <= qi
    return own, jnp.logical_not(own) & (n == 0)


def _fold(x, own):
    return jnp.where(own, x[:, BLOCK:2 * BLOCK], x[:, 0:BLOCK])


def _unfold(xf, own):
    zero = jnp.zeros_like(xf)
    return jnp.concatenate([jnp.where(own, zero, xf), jnp.where(own, xf, zero)], axis=1)


ROW_GROUP_HEAD = (0, 2, 1, 3)


def _sink_col(sink_ref, kh):
    rowg = lax.broadcasted_iota(jnp.int32, (GROUP * BLOCK, 1), 0) // BLOCK
    col = jnp.full((GROUP * BLOCK, 1), sink_ref[0, GROUP * kh + ROW_GROUP_HEAD[0]], F32)
    for g in range(1, GROUP):
        col = jnp.where(rowg == g, sink_ref[0, GROUP * kh + ROW_GROUP_HEAD[g]], col)
    return col


def _low_lanes(shape):
    return lax.broadcasted_iota(jnp.int32, shape, 1) < HEAD_DIM


def _kv_pair_operand(prev, cur, kh):
    c = 128 * (kh // 2)
    col = jnp.concatenate([prev[:, c:c + 128], cur[:, c:c + 128]], axis=0).astype(F32)
    if kh % 2 == 0:
        lo = jnp.where(_low_lanes(col.shape), col, 0.0)
        hi = pltpu.roll(lo, HEAD_DIM, 1)
    else:
        hi = jnp.where(_low_lanes(col.shape), 0.0, col)
        lo = pltpu.roll(hi, HEAD_DIM, 1)
    return jnp.concatenate([lo, hi], axis=0).astype(BF16)


def _pair_rows(x, kh):
    c = 2 * 128 * kh
    return jnp.concatenate([x[:, c:c + 128], x[:, c + 128:c + 256]], axis=0)


def _restack(big):
    return jnp.concatenate([big[:, 0:2 * BLOCK], big[:, 2 * BLOCK:4 * BLOCK]], axis=0)


def _unrestack(stacked):
    return jnp.concatenate([stacked[0:2 * BLOCK], stacked[2 * BLOCK:4 * BLOCK]], axis=1)


def _fold_pair(x2, kh):
    low = _low_lanes((2 * BLOCK, 128))
    mixed = jnp.where(low, x2[0:2 * BLOCK], x2[2 * BLOCK:4 * BLOCK])
    total = mixed + pltpu.roll(mixed, HEAD_DIM, 1)
    return jnp.where(low, total, 0.0) if kh % 2 == 0 else jnp.where(low, 0.0, total)


def _attn_scores(qr, k2, kh):
    q2 = _pair_rows(qr, kh).astype(BF16)
    return q2, _restack(_dot_nt(q2, k2))


def _attn_softmax(s, sink_col, mask):
    own, no_key = mask
    s = jnp.where(no_key, -1e30, _fold(s, own))
    m = jnp.maximum(jnp.max(s, axis=-1, keepdims=True), sink_col)
    p = jnp.exp(s - m)
    p_sink = jnp.exp(sink_col - m)
    denom = jnp.sum(p, axis=-1, keepdims=True) + p_sink
    return p / denom, p_sink / denom


def _attn_forward(proj, tabs, sinks):
    T = proj.shape[0]
    nb = T // BLOCK

    def body(q_ref, kvc_ref, kvp_ref, g0_ref, g1_ref, cc, sac, sbc, cp_, sap, sbp, sink_ref, y_ref, qrb_ref, krb_ref):
        n = pl.program_id(0)
        tc = tcur = (cc[...], sac[...], sbc[...])
        tprev = (cp_[...], sap[...], sbp[...])
        qr = _rope(q_ref[...], *tc) * ATTN_SCALE
        kr_cur = _rope(kvc_ref[:, 0:KV_W], *tcur)
        kr_prev = _rope(kvp_ref[:, 0:KV_W], *tprev)
        qrb_ref[...] = qr.astype(BF16)
        krb_ref[...] = kr_cur.astype(BF16)
        v_cur, v_prev = kvc_ref[:, KV_W:2 * KV_W], kvp_ref[:, KV_W:2 * KV_W]
        mask = _attn_mask(n)
        outs = []
        k2s = [_kv_pair_operand(kr_prev, kr_cur, kh) for kh in range(N_KV)]
        v2s = [_kv_pair_operand(v_prev, v_cur, kh) for kh in range(N_KV)]
        scores = [_attn_scores(qr, k2s[kh], kh) for kh in range(N_KV)]
        for kh in range(N_KV):
            pn, _ = _attn_softmax(scores[kh][1], _sink_col(sink_ref, kh), mask)
            o_big = _dot(_unrestack(_unfold(pn.astype(BF16), mask[0])), v2s[kh])
            outs += [o_big[0:BLOCK], o_big[BLOCK:2 * BLOCK]]
        o = jnp.concatenate(outs, axis=1)
        g = jnp.concatenate([g0_ref[...], g1_ref[...]], axis=1)
        y_ref[...] = (o * (g * _sigmoid(g))).astype(BF16)

    def blk(w, cb):
        return pl.BlockSpec((BLOCK, w), lambda n, cb=cb: (n, cb))

    prev = lambda w, cb: pl.BlockSpec((BLOCK, w), lambda n, cb=cb: (jnp.maximum(n - 1, 0), cb))
    return pl.pallas_call(
        body, grid=(nb,), name="attn_forward",
        out_shape=(jax.ShapeDtypeStruct((T, D_MODEL), BF16), jax.ShapeDtypeStruct((T, D_MODEL), BF16),
                   jax.ShapeDtypeStruct((T, KV_W), BF16)),
        in_specs=[blk(D_MODEL, 0), blk(CB, CB_KV), prev(CB, CB_KV), blk(CB, CB_GA), blk(CB, CB_GA + 1),
                  blk(128, 0), blk(128, 0), blk(128, 0), prev(128, 0), prev(128, 0), prev(128, 0),
                  pl.BlockSpec(memory_space=pltpu.SMEM)],
        out_specs=(blk(D_MODEL, 0), blk(D_MODEL, 0), blk(KV_W, 0)),
        compiler_params=_cp("parallel"),
    )(proj, proj, proj, proj, proj, *tabs, *tabs, sinks)


def _scan_rows8():
    return lax.broadcasted_iota(jnp.int32, (8, D_MODEL), 0)


def _scan_forward(a_ref, b_ref, h_ref, carry, rows):
    row = _scan_rows8()

    def group(i, carry):
        off = pl.multiple_of(i * 8, 8)
        a, b = a_ref[pl.ds(off, 8), :], b_ref[pl.ds(off, 8), :]
        for d in (1, 2, 4):
            ok = row >= d
            b = jnp.where(ok, a * pltpu.roll(b, d, 0) + b, b)
            a = jnp.where(ok, a * pltpu.roll(a, d, 0), a)
        h = a * carry + b
        h_ref[pl.ds(off, 8), :] = h
        return h[7:8, :]

    return lax.fori_loop(0, rows // 8, group, carry)


def _scan_backward(a_ref, g_ref, lam_ref, carry, rows):
    row = _scan_rows8()

    def group(i, carry):
        off = pl.multiple_of((rows // 8 - 1 - i) * 8, 8)
        a, g = a_ref[pl.ds(off, 8), :], g_ref[pl.ds(off, 8), :]
        b = a * g
        for d in (1, 2, 4):
            ok = row < 8 - d
            b = jnp.where(ok, a * pltpu.roll(b, 8 - d, 0) + b, b)
            a = jnp.where(ok, a * pltpu.roll(a, 8 - d, 0), a)
        mu = a * carry + b
        mu_below = jnp.where(row == 7, carry, pltpu.roll(mu, 7, 0))
        lam_ref[pl.ds(off, 8), :] = g + mu_below
        return mu[0:1, :]

    return lax.fori_loop(0, rows // 8, group, carry)


def _conv_taps(xbuf, xr, tail):
    rows = xr.shape[0]
    xbuf[0:8, :] = tail
    xbuf[8:rows + 8, :] = xr
    return [xbuf[pl.ds(8 - (CONV_W - 1 - k), rows), :] for k in range(CONV_W - 1)] + [xr]


def _rnn_gates(xbuf, xr, tail, cw, cb, wa_ref, wx_ref, ba, bx, sp, reset):
    xs = _conv_taps(xbuf, xr, tail)
    xc = xs[0] * cw[0:1, :]
    for k in range(1, CONV_W):
        xc = xc + xs[k] * cw[k:k + 1, :]
    xc = xc + cb
    xcb = xc.astype(BF16)
    za = jnp.concatenate([_dot(xcb[:, RNN_BW * j:RNN_BW * (j + 1)], wa_ref[j]) for j in range(RNN_BLOCKS)], axis=1) + ba
    zx = jnp.concatenate([_dot(xcb[:, RNN_BW * j:RNN_BW * (j + 1)], wx_ref[j]) for j in range(RNN_BLOCKS)], axis=1) + bx
    r, i = _sigmoid(za), _sigmoid(zx)
    neg_log_a = LRU_C * r * sp
    a_raw = jnp.exp(-neg_log_a)
    mult_raw = jnp.sqrt(jnp.tanh(neg_log_a) * (1.0 + a_raw * a_raw))
    a = jnp.where(reset, 0.0, a_raw)
    mult = jnp.where(reset, 1.0, mult_raw)
    return xc, r, i, a, mult


def _rnn_forward(proj, pos_col, conv_w, conv_b, rwa, rwx, ba, bx, lam):
    T = proj.shape[0]
    tr = min(T, 256)

    def body(x0, x1, g0, g1, pos_ref, cw_ref, cb_ref, wa_ref, wx_ref, ba_ref, bx_ref, lam_ref,
             y_ref, h_ref, xc_ref, r_ref, i_ref, a_ref, mult_ref, xbuf, bbuf, tail, carry):
        t = pl.program_id(0)

        @pl.when(t == 0)
        def _():
            tail[...] = jnp.zeros_like(tail)
            carry[...] = jnp.zeros_like(carry)

        xr = jnp.concatenate([x0[...], x1[...]], axis=1)
        sp = _softplus(-lam_ref[...])
        reset = pos_ref[...] == 0
        xc, r, i, a, mult = _rnn_gates(
            xbuf, xr, tail[...], cw_ref[...], cb_ref[...], wa_ref, wx_ref, ba_ref[...], bx_ref[...], sp, reset)
        xc_ref[...] = xc
        r_ref[...] = r
        i_ref[...] = i
        a_ref[...] = a
        mult_ref[...] = mult
        bbuf[...] = mult * (i * xc)
        last = _scan_forward(a_ref, bbuf, h_ref, carry[0:1, :], tr)
        carry[...] = jnp.broadcast_to(last, carry.shape)
        tail[...] = xr[tr - 8:tr, :]
        g = jnp.concatenate([g0[...], g1[...]], axis=1)
        y_ref[...] = (h_ref[...] * (g * _sigmoid(g))).astype(BF16)

    blk = lambda cb: pl.BlockSpec((tr, CB), lambda t, cb=cb: (t, cb))
    row = lambda w: pl.BlockSpec((1, w), lambda t: (0, 0))
    full3 = pl.BlockSpec((RNN_BLOCKS, RNN_BW, RNN_BW), lambda t: (0, 0, 0))
    tok = pl.BlockSpec((tr, D_MODEL), lambda t: (t, 0))
    act = jax.ShapeDtypeStruct((T, D_MODEL), F32)
    return pl.pallas_call(
        body, out_shape=(jax.ShapeDtypeStruct((T, D_MODEL), BF16),) + (act,) * 6,
        grid=(T // tr,), name="rnn_forward",
        in_specs=[blk(CB_XR), blk(CB_XR + 1), blk(CB_GR), blk(CB_GR + 1), pl.BlockSpec((tr, 1), lambda t: (t, 0)),
                  pl.BlockSpec((CONV_W, D_MODEL), lambda t: (0, 0)), row(D_MODEL), full3, full3,
                  row(D_MODEL), row(D_MODEL), row(D_MODEL)],
        out_specs=(tok,) * 7,
        scratch_shapes=[pltpu.VMEM((tr + 8, D_MODEL), F32), pltpu.VMEM((tr, D_MODEL), F32),
                        pltpu.VMEM((8, D_MODEL), F32), pltpu.VMEM((8, D_MODEL), F32)],
        compiler_params=_cp("arbitrary"),
    )(proj, proj, proj, proj, pos_col, conv_w, conv_b, rwa, rwx, ba, bx, lam)


ROW_PARTS = 1


def _merge_and_head(x, target, y_attn, y_rnn, proj, wap, wrp, wo, mod_row, final_g):
    T = x.shape[0]
    tm = min(T, 256)

    def body(x_ref, t_ref, ya_ref, yr_ref, ma0, ma1, mr0, mr1, wap_ref, wrp_ref, wo_ref, mod_ref, fg_ref,
             dx2_ref, mg_ref, do_ref, dpa_ref, dpr_ref, dya_ref, dyr_ref, dc_ref, dfg_ref, dgate_ref, loss_ref):
        i = pl.program_id(0)
        gate = mod_ref[:, 2 * D_MODEL:3 * D_MODEL]
        fg = fg_ref[...]
        parts = [slice(p * (tm // ROW_PARTS), (p + 1) * (tm // ROW_PARTS)) for p in range(ROW_PARTS)]
        each = range(ROW_PARTS)
        pa = [_dot(ya_ref[r, :], wap_ref[...]) for r in parts]
        pr = [_dot(yr_ref[r, :], wrp_ref[...]) for r in parts]
        sa = [_sigmoid(jnp.concatenate([ma0[r, :], ma1[r, :]], axis=1)) for r in parts]
        sr = [_sigmoid(jnp.concatenate([mr0[r, :], mr1[r, :]], axis=1)) for r in parts]
        mb = [(sa[p] * pa[p] + sr[p] * pr[p]).astype(BF16) for p in each]
        o = [_dot(mb[p], wo_ref[...]) for p in each]
        x2 = [x_ref[r, :] + gate * o[p] for p, r in enumerate(parts)]
        r2 = [_rms(v) for v in x2]
        xn2 = [x2[p] * r2[p] for p in each]
        err = [xn2[p] * fg - t_ref[r, :] for p, r in enumerate(parts)]
        dy = [e * (1.0 / D_MODEL) for e in err]
        dxn = [d * fg for d in dy]
        dx2 = [r2[p] * (dxn[p] - xn2[p] * jnp.mean(dxn[p] * xn2[p], axis=-1, keepdims=True)) for p in each]
        dob = [(dx2[p] * gate).astype(BF16) for p in each]
        dmerged = [_dot_nt(d, wo_ref[...]) for d in dob]
        dpa = [(dmerged[p] * sa[p]).astype(BF16) for p in each]
        dpr = [(dmerged[p] * sr[p]).astype(BF16) for p in each]
        dya = [_dot_nt(d, wap_ref[...]) for d in dpa]
        dyr = [_dot_nt(d, wrp_ref[...]) for d in dpr]
        loss_t, dfg_t, dgate_t = 0.0, 0.0, 0.0
        for p, r in enumerate(parts):
            dx2_ref[r, :] = dx2[p]
            mg_ref[r, :] = mb[p]
            do_ref[r, :] = dob[p]
            dpa_ref[r, :] = dpa[p]
            dpr_ref[r, :] = dpr[p]
            dya_ref[r, :] = dya[p]
            dyr_ref[r, :] = dyr[p]
            dc_ref[r, 0:D_MODEL] = (dmerged[p] * pa[p] * sa[p] * (1.0 - sa[p])).astype(BF16)
            dc_ref[r, D_MODEL:2 * D_MODEL] = (dmerged[p] * pr[p] * sr[p] * (1.0 - sr[p])).astype(BF16)
            loss_t = loss_t + 0.5 * jnp.sum(
                jnp.sum(err[p] * err[p], axis=-1, keepdims=True) * (1.0 / D_MODEL), axis=0, keepdims=True)
            dfg_t = dfg_t + jnp.sum(dy[p] * xn2[p], axis=0, keepdims=True)
            dgate_t = dgate_t + jnp.sum(dx2[p] * o[p], axis=0, keepdims=True)

        @pl.when(i == 0)
        def _():
            dfg_ref[...] = jnp.zeros_like(dfg_ref)
            dgate_ref[...] = jnp.zeros_like(dgate_ref)
            loss_ref[...] = jnp.zeros_like(loss_ref)

        dfg_ref[...] += dfg_t
        dgate_ref[...] += dgate_t
        loss_ref[...] += jnp.broadcast_to(loss_t, loss_ref.shape)

    tok = lambda w: pl.BlockSpec((tm, w), lambda i: (i, 0))
    blk = lambda cb: pl.BlockSpec((tm, CB), lambda i, cb=cb: (i, cb))
    wfull = pl.BlockSpec((D_MODEL, D_MODEL), lambda i: (0, 0), pipeline_mode=pl.Buffered(1))
    row = lambda w: pl.BlockSpec((1, w), lambda i: (0, 0))
    out_shape = (
        jax.ShapeDtypeStruct((T, D_MODEL), F32), jax.ShapeDtypeStruct((T, D_MODEL), BF16),
        jax.ShapeDtypeStruct((T, D_MODEL), BF16), jax.ShapeDtypeStruct((T, D_MODEL), BF16),
        jax.ShapeDtypeStruct((T, D_MODEL), BF16), jax.ShapeDtypeStruct((T, D_MODEL), F32),
        jax.ShapeDtypeStruct((T, D_MODEL), F32), jax.ShapeDtypeStruct((T, 2 * D_MODEL), BF16),
        jax.ShapeDtypeStruct((1, D_MODEL), F32), jax.ShapeDtypeStruct((1, D_MODEL), F32),
        jax.ShapeDtypeStruct((1, 128), F32),
    )
    return pl.pallas_call(
        body, out_shape=out_shape, grid=(T // tm,), name="merge_and_head",
        in_specs=[tok(D_MODEL), tok(D_MODEL), tok(D_MODEL), tok(D_MODEL), blk(CB_MA), blk(CB_MA + 1), blk(CB_MR),
                  blk(CB_MR + 1), wfull, wfull, wfull, row(ADA_W), row(D_MODEL)],
        out_specs=(tok(D_MODEL),) * 7 + (tok(2 * D_MODEL), row(D_MODEL), row(D_MODEL), row(128)),
        compiler_params=_cp("arbitrary"),
    )(x, target, y_attn, y_rnn, proj, proj, proj, proj, wap, wrp, wo, mod_row, final_g)


def _attn_backward(proj, qr_b, kr_b, d_y, tabs, sinks):
    T = proj.shape[0]
    nb = T // BLOCK

    def body(qrb_ref, krc_ref, krp_ref, vc_ref, vp_ref, g0_ref, g1_ref, dy_ref, cc, sac, sbc, cp_, sap, sbp, sink_ref,
             dq_ref, dkv_ref, dg_ref, dsink_ref, carry):
        n = pl.program_id(0)

        @pl.when(n == 0)
        def _():
            carry[...] = jnp.zeros_like(carry)
            dsink_ref[...] = jnp.zeros_like(dsink_ref)

        @pl.when(n < nb)
        def _():
            tc = tcur = (cc[...], sac[...], sbc[...])
            tprev = (cp_[...], sap[...], sbp[...])
            qr, kr_cur, kr_prev = qrb_ref[...], krc_ref[...], krp_ref[...]
            v_cur, v_prev = vc_ref[...], vp_ref[...]
            g = jnp.concatenate([g0_ref[...], g1_ref[...]], axis=1)
            sg = _sigmoid(g)
            dy = dy_ref[...]
            d_o = dy * (g * sg)
            mask = _attn_mask(n)
            lane = lax.broadcasted_iota(jnp.int32, (1, 128), 1)
            rowg = lax.broadcasted_iota(jnp.int32, (GROUP * BLOCK, 1), 0) // BLOCK
            o_parts, dq_parts = [], []
            dk_cols, dv_cols = [None, None], [None, None]
            dsink = jnp.zeros((1, 128), F32)
            heads = range(N_KV)
            k2s = [_kv_pair_operand(kr_prev, kr_cur, kh) for kh in heads]
            v2s = [_kv_pair_operand(v_prev, v_cur, kh) for kh in heads]
            scores = [_attn_scores(qr, k2s[kh], kh) for kh in heads]
            do2s = [_pair_rows(d_o, kh).astype(BF16) for kh in heads]
            dpns = [_fold(_restack(_dot_nt(do2s[kh], v2s[kh])), mask[0]) for kh in heads]
            probs = [_attn_softmax(scores[kh][1], _sink_col(sink_ref, kh), mask) for kh in heads]
            p_bigs = [_unrestack(_unfold(probs[kh][0].astype(BF16), mask[0])) for kh in heads]
            o_bigs = [_dot(p_bigs[kh], v2s[kh]) for kh in heads]
            dv2s = [_dot_tn(p_bigs[kh], do2s[kh]) for kh in heads]
            deltas = [jnp.sum(probs[kh][0] * dpns[kh], axis=-1, keepdims=True) for kh in heads]
            ds_bigs = [_unrestack(_unfold((probs[kh][0] * (dpns[kh] - deltas[kh])).astype(BF16), mask[0])) for kh in heads]
            dq2s = [_dot(ds_bigs[kh], k2s[kh]) for kh in heads]
            dk2s = [_dot_tn(ds_bigs[kh], scores[kh][0]) for kh in heads]
            for kh in heads:
                o_parts += [o_bigs[kh][0:BLOCK], o_bigs[kh][BLOCK:2 * BLOCK]]
                dq_parts += [dq2s[kh][0:BLOCK], dq2s[kh][BLOCK:2 * BLOCK]]
                dk_c, dv_c = _fold_pair(dk2s[kh], kh), _fold_pair(dv2s[kh], kh)
                c = kh // 2
                dk_cols[c] = dk_c if dk_cols[c] is None else dk_cols[c] + dk_c
                dv_cols[c] = dv_c if dv_cols[c] is None else dv_cols[c] + dv_c
                ds_rows = probs[kh][1] * deltas[kh]
                for gq in range(GROUP):
                    val = -jnp.sum(jnp.where(rowg == gq, ds_rows, 0.0), axis=0, keepdims=True)
                    dsink = dsink + jnp.where(lane == GROUP * kh + ROW_GROUP_HEAD[gq], val, 0.0)
            o = jnp.concatenate(o_parts, axis=1)
            dg_ref[...] = (dy * o * (sg * (1.0 + g * (1.0 - sg)))).astype(BF16)
            dq_ref[...] = (_unrope(jnp.concatenate(dq_parts, axis=1), *tc) * ATTN_SCALE).astype(BF16)
            dk_all, dv_all = jnp.concatenate(dk_cols, axis=1), jnp.concatenate(dv_cols, axis=1)
            dk_prev = _unrope(dk_all[0:BLOCK], *tprev)
            dk_cur = _unrope(dk_all[BLOCK:2 * BLOCK], *tcur)
            dv_prev, dv_cur = dv_all[0:BLOCK], dv_all[BLOCK:2 * BLOCK]
            dkv_ref[...] = (carry[...] + jnp.concatenate([dk_prev, dv_prev], axis=1)).astype(BF16)
            carry[...] = jnp.concatenate([dk_cur, dv_cur], axis=1)
            dsink_ref[...] += dsink

        @pl.when(n == nb)
        def _():
            dkv_ref[...] = carry[...].astype(BF16)

    cur = lambda w, cb: pl.BlockSpec((BLOCK, w), lambda n, cb=cb: (jnp.minimum(n, nb - 1), cb))
    prev = lambda w, cb: pl.BlockSpec((BLOCK, w), lambda n, cb=cb: (jnp.maximum(jnp.minimum(n, nb - 1) - 1, 0), cb))
    out_shape = (jax.ShapeDtypeStruct((T, D_MODEL), BF16), jax.ShapeDtypeStruct((T, 2 * KV_W), BF16),
                 jax.ShapeDtypeStruct((T, D_MODEL), BF16), jax.ShapeDtypeStruct((1, 128), F32))
    return pl.pallas_call(
        body, out_shape=out_shape, grid=(nb + 1,), name="attn_backward",
        in_specs=[cur(D_MODEL, 0), cur(KV_W, 0), prev(KV_W, 0), cur(KV_W, V_COL_BLOCK), prev(KV_W, V_COL_BLOCK),
                  cur(CB, CB_GA), cur(CB, CB_GA + 1), cur(D_MODEL, 0),
                  cur(128, 0), cur(128, 0), cur(128, 0), prev(128, 0), prev(128, 0), prev(128, 0),
                  pl.BlockSpec(memory_space=pltpu.SMEM)],
        out_specs=(cur(D_MODEL, 0), pl.BlockSpec((BLOCK, 2 * KV_W), lambda n: (jnp.maximum(n - 1, 0), 0)),
                   cur(D_MODEL, 0), pl.BlockSpec((1, 128), lambda n: (0, 0))),
        scratch_shapes=[pltpu.VMEM((BLOCK, 2 * KV_W), F32)],
        compiler_params=_cp("arbitrary"),
    )(qr_b, kr_b, kr_b, proj, proj, proj, proj, d_y, *tabs, *tabs, sinks)


def _rnn_backward(proj, pos_col, h_rnn, saved, d_y, conv_w, rwa, rwx, lam):
    T = proj.shape[0]
    tr = min(T, 256)
    nt = T // tr
    hb = tr // 8

    def body(x0, x1, xh0, xh1, g0, g1, pos_ref, h_ref, hh_ref, xc_ref, r_ref, i_ref, a_ref, mult_ref, dy_ref,
             cw_ref, wa_ref, wx_ref, lam_ref, db_ref, dcw_ref, dcb_ref, dwa_ref, dwx_ref, dba_ref, dbx_ref, dlam_ref,
             xbuf, hbuf, dbuf, gbuf, lbuf, mu_carry, dxc_head):
        step = pl.program_id(0)
        first_tile = step == nt - 1

        @pl.when(step == 0)
        def _():
            mu_carry[...] = jnp.zeros_like(mu_carry)
            dxc_head[...] = jnp.zeros_like(dxc_head)
            for ref in (dcw_ref, dcb_ref, dwa_ref, dwx_ref, dba_ref, dbx_ref, dlam_ref):
                ref[...] = jnp.zeros_like(ref)

        xr = jnp.concatenate([x0[...], x1[...]], axis=1)
        tail = jnp.where(first_tile, 0.0, jnp.concatenate([xh0[...], xh1[...]], axis=1))
        lam_v = lam_ref[...]
        sp = _softplus(-lam_v)
        reset = pos_ref[...] == 0
        cw = cw_ref[...]
        xbuf[0:8, :] = tail
        xbuf[8:tr + 8, :] = xr
        g = jnp.concatenate([g0[...], g1[...]], axis=1)
        sg = _sigmoid(g)
        dy = dy_ref[...]
        h = h_ref[...]
        db_ref[:, D_MODEL:2 * D_MODEL] = (dy * h * (sg * (1.0 + g * (1.0 - sg)))).astype(BF16)
        gbuf[...] = dy * (g * sg)
        top = _scan_backward(a_ref, gbuf, lbuf, mu_carry[0:1, :], tr)
        mu_carry[...] = jnp.broadcast_to(top, mu_carry.shape)
        hbuf[0:8, :] = jnp.where(first_tile, 0.0, hh_ref[...])
        hbuf[8:tr + 8, :] = h
        live = jnp.logical_not(reset)
        dbuf[tr:tr + 8, :] = dxc_head[...]
        for j in range(RNN_BLOCKS):
            sl = slice(RNN_BW * j, RNN_BW * (j + 1))
            lam_t, h_prev = lbuf[:, sl], hbuf[pl.ds(7, tr), sl]
            xc, r, i, a, mult = xc_ref[:, sl], r_ref[:, sl], i_ref[:, sl], a_ref[:, sl], mult_ref[:, sl]
            d_a = jnp.where(live, lam_t * h_prev, 0.0)
            d_mult = jnp.where(live, lam_t * (i * xc), 0.0)
            d_ixc = lam_t * mult
            d_i = d_ixc * xc
            d_log_a = d_a * a - d_mult * (a * a / mult)
            d_za = d_log_a * (-LRU_C * sp[:, sl]) * (r * (1.0 - r))
            d_zx = d_i * (i * (1.0 - i))
            dlam_ref[:, sl] += jnp.sum(d_log_a * r, axis=0, keepdims=True) * (LRU_C * _sigmoid(-lam_v[:, sl]))
            dba_ref[:, sl] += jnp.sum(d_za, axis=0, keepdims=True)
            dbx_ref[:, sl] += jnp.sum(d_zx, axis=0, keepdims=True)
            xcb, dzab, dzxb = xc.astype(BF16), d_za.astype(BF16), d_zx.astype(BF16)
            dwa_ref[j] += _dot_tn(xcb, dzab)
            dwx_ref[j] += _dot_tn(xcb, dzxb)
            d_xc = d_ixc * i + (_dot_nt(dzab, wa_ref[j]) + _dot_nt(dzxb, wx_ref[j]))
            dcb_ref[:, sl] += jnp.sum(d_xc, axis=0, keepdims=True)
            for k in range(CONV_W):
                tap = xr[:, sl] if k == CONV_W - 1 else xbuf[pl.ds(8 - (CONV_W - 1 - k), tr), sl]
                dcw_ref[k:k + 1, sl] += jnp.sum(d_xc * tap, axis=0, keepdims=True)
            dbuf[0:tr, sl] = d_xc
            d_xr = d_xc * cw[CONV_W - 1:CONV_W, sl]
            for k in range(CONV_W - 1):
                d_xr = d_xr + dbuf[pl.ds(CONV_W - 1 - k, tr), sl] * cw[k:k + 1, sl]
            dxc_head[:, sl] = d_xc[0:8, :]
            db_ref[:, sl] = d_xr.astype(BF16)

    rev = lambda s: nt - 1 - s
    blk = lambda cb: pl.BlockSpec((tr, CB), lambda s, cb=cb: (rev(s), cb))
    halo = lambda w, cb: pl.BlockSpec((8, w), lambda s, cb=cb: (jnp.maximum(rev(s) * hb - 1, 0), cb))
    tok = lambda w: pl.BlockSpec((tr, w), lambda s: (rev(s), 0))
    row = lambda w: pl.BlockSpec((1, w), lambda s: (0, 0))
    full3 = pl.BlockSpec((RNN_BLOCKS, RNN_BW, RNN_BW), lambda s: (0, 0, 0))
    cwspec = pl.BlockSpec((CONV_W, D_MODEL), lambda s: (0, 0))
    vec = jax.ShapeDtypeStruct((1, D_MODEL), F32)
    gate_w = jax.ShapeDtypeStruct((RNN_BLOCKS, RNN_BW, RNN_BW), F32)
    out_shape = (jax.ShapeDtypeStruct((T, 2 * D_MODEL), BF16), jax.ShapeDtypeStruct((CONV_W, D_MODEL), F32), vec,
                 gate_w, gate_w, vec, vec, vec)
    big = lambda: pltpu.VMEM((tr, D_MODEL), F32)
    ext = lambda: pltpu.VMEM((tr + 8, D_MODEL), F32)
    return pl.pallas_call(
        body, out_shape=out_shape, grid=(nt,), name="rnn_backward",
        in_specs=[blk(CB_XR), blk(CB_XR + 1), halo(CB, CB_XR), halo(CB, CB_XR + 1), blk(CB_GR), blk(CB_GR + 1),
                  pl.BlockSpec((tr, 1), lambda s: (rev(s), 0)), tok(D_MODEL), halo(D_MODEL, 0)] + [tok(D_MODEL)] * 6
        + [cwspec, full3, full3, row(D_MODEL)],
        out_specs=(tok(2 * D_MODEL), cwspec, row(D_MODEL), full3, full3, row(D_MODEL), row(D_MODEL), row(D_MODEL)),
        scratch_shapes=[ext(), ext(), ext(), big(), big(), pltpu.VMEM((8, D_MODEL), F32), pltpu.VMEM((8, D_MODEL), F32)],
        compiler_params=_cp("arbitrary"),
    )(proj, proj, proj, proj, proj, proj, pos_col, h_rnn, h_rnn, *saved, d_y, conv_w, rwa, rwx, lam)


def _input_backward(pieces, w_in, x, dx2, mod_row, norm_g):
    T = x.shape[0]
    tm = min(T, 512)
    n = len(pieces)

    def body(*refs):
        d_refs = refs[:n]
        w_ref, x_ref, dx2_ref, mod_ref, g_ref, gx_ref, dshift_ref, dscale_ref, dg_ref = refs[n:]
        i = pl.program_id(0)
        dh = None
        for d_ref, (_, start, count) in zip(d_refs, pieces):
            part = _dot_nt(d_ref[...], w_ref[:, start * CB:(start + count) * CB])
            dh = part if dh is None else dh + part

        @pl.when(i == 0)
        def _():
            dshift_ref[...] = jnp.zeros_like(dshift_ref)
            dscale_ref[...] = jnp.zeros_like(dscale_ref)
            dg_ref[...] = jnp.zeros_like(dg_ref)

        xf = x_ref[...]
        r1 = _rms(xf)
        xn = xf * r1
        gn = g_ref[...]
        s1 = 1.0 + mod_ref[:, D_MODEL:2 * D_MODEL]
        dshift_ref[...] += jnp.sum(dh, axis=0, keepdims=True)
        dscale_ref[...] += jnp.sum(dh * (xn * gn), axis=0, keepdims=True)
        dg_ref[...] += jnp.sum(dh * s1 * xn, axis=0, keepdims=True)
        dxn = dh * s1 * gn
        gx_ref[...] = dx2_ref[...] + r1 * (dxn - xn * jnp.mean(dxn * xn, axis=-1, keepdims=True))

    tok = lambda w: pl.BlockSpec((tm, w), lambda i: (i, 0))
    row = lambda w: pl.BlockSpec((1, w), lambda i: (0, 0))
    vec = jax.ShapeDtypeStruct((1, D_MODEL), F32)
    return pl.pallas_call(
        body, out_shape=(jax.ShapeDtypeStruct((T, D_MODEL), F32), vec, vec, vec), grid=(T // tm,), name="input_backward",
        in_specs=[tok(c * CB) for _, _, c in pieces]
        + [pl.BlockSpec((D_MODEL, IN_W), lambda i: (0, 0), pipeline_mode=pl.Buffered(1)), tok(D_MODEL), tok(D_MODEL),
           row(ADA_W), row(D_MODEL)],
        out_specs=(tok(D_MODEL), row(D_MODEL), row(D_MODEL), row(D_MODEL)),
        compiler_params=_cp("arbitrary"),
    )(*[p[0] for p in pieces], w_in, x, dx2, mod_row, norm_g)


def _weight_grad(a, pieces, tag, a_is_transposed=False):
    M, T = a.shape if a_is_transposed else a.shape[::-1]
    n_blocks = sum(count for _, _, count in pieces)
    n = len(pieces)
    contract = _dot if a_is_transposed else _dot_tn

    def body(*refs):
        a_ref, b_refs, o_ref = refs[0], refs[1:1 + n], refs[-1]
        j = pl.program_id(0)
        for b_ref, (_, start, count) in zip(b_refs, pieces):
            @pl.when((j >= start) & (j < start + count))
            def _(b_ref=b_ref):
                o_ref[...] = contract(a_ref[...], b_ref[...])

    def piece_spec(start, count):
        return pl.BlockSpec((T, CB), lambda j: (0, jnp.clip(j - start, 0, count - 1)))

    return pl.pallas_call(
        body, out_shape=jax.ShapeDtypeStruct((M, n_blocks * CB), F32), grid=(n_blocks,), name=f"weight_grad_{tag}",
        in_specs=[pl.BlockSpec(a.shape, lambda j: (0, 0), pipeline_mode=pl.Buffered(1))] + [piece_spec(s, c) for _, s, c in pieces],
        out_specs=pl.BlockSpec((M, CB), lambda j: (0, j)), compiler_params=_cp("arbitrary"),
    )(a, *[p[0] for p in pieces])


def _adamw(w, g, m, v):
    m = ADAM_B1 * m + (1.0 - ADAM_B1) * g
    v = ADAM_B2 * v + (1.0 - ADAM_B2) * (g * g)
    m_hat = m / (1.0 - ADAM_B1 ** ADAM_STEP)
    v_hat = v / (1.0 - ADAM_B2 ** ADAM_STEP)
    delta = -ADAM_LR * (m_hat / (jnp.sqrt(v_hat) + ADAM_EPS) + ADAM_WD * w)
    return delta, m, v


def _sum_landed(kind, owns, lands, where, tag):
    n = len(owns)
    land = lands[0]
    if kind == "in":
        R, C = land.shape[1:]
        tr = 256
        grid = (R // tr,)
        own_spec = pl.BlockSpec((tr, C), lambda i, w: (i, w[0]))
        land_spec = pl.BlockSpec((3, tr, C), lambda i, w: (0, i, 0))
        out_spec = pl.BlockSpec((1, tr, C), lambda i, w: (w[1], i, 0))
        out_shape = (2, R, C)
        pick = lambda ref: ref[...]
    elif kind == "sq":
        R, C = land.shape[1:]
        grid = (1,)
        own_spec = pl.BlockSpec((1, R, C), lambda i, w: (w[0], 0, 0))
        land_spec = pl.BlockSpec((3, R, C), lambda i, w: (0, 0, 0))
        out_spec = pl.BlockSpec((1, R, C), lambda i, w: (w[1], 0, 0))
        out_shape = (2, R, C)
        pick = lambda ref: ref[0]
    else:
        B, R, C = land.shape[1:]
        grid = (1,)
        own_spec = pl.BlockSpec((B, 1, R, C), lambda i, w: (0, w[0], 0, 0))
        land_spec = pl.BlockSpec((3, B, R, C), lambda i, w: (0, 0, 0, 0))
        out_spec = pl.BlockSpec((B, 1, R, C), lambda i, w: (0, w[1], 0, 0))
        out_shape = (B, 2, R, C)
        pick = lambda ref: ref[:, 0]

    def body(w_ref, *refs):
        for k in range(n):
            own_ref, l_ref, o_ref = refs[k], refs[n + k], refs[2 * n + k]
            total = ((pick(own_ref) + l_ref[0].astype(F32)) + l_ref[1].astype(F32)) + l_ref[2].astype(F32)
            if kind == "rg":
                o_ref[:, 0] = total
            else:
                o_ref[0] = total

    grid_spec = pltpu.PrefetchScalarGridSpec(num_scalar_prefetch=1, grid=grid, in_specs=[own_spec] * n + [land_spec] * n,
                                             out_specs=(out_spec,) * n)
    return list(pl.pallas_call(
        body, out_shape=(jax.ShapeDtypeStruct(out_shape, F32),) * n, grid_spec=grid_spec, name=f"sum_landed_{tag}",
        compiler_params=_cp("parallel"),
    )(where, *owns, *lands))


def _adamw_shard(gs, ws, ms, vs, tag):
    n = len(ws)
    R, C = ws[0].shape
    tr = min(R, 256 if n == 1 else 64)

    def body(*refs):
        for k in range(n):
            g = refs[k][...]
            d, nm, nv = _adamw(refs[n + k][...], g, refs[2 * n + k][...], refs[3 * n + k][...])
            out = refs[4 * n + 4 * k:4 * n + 4 * k + 4]
            out[0][...] = g
            out[1][...] = d
            out[2][...] = nm
            out[3][...] = nv

    spec = pl.BlockSpec((tr, C), lambda i: (i, 0))
    sds = jax.ShapeDtypeStruct((R, C), F32)
    outs = pl.pallas_call(
        body, out_shape=(sds,) * (4 * n), grid=(R // tr,), name=f"adamw_{tag}",
        in_specs=[spec] * (4 * n), out_specs=(spec,) * (4 * n), compiler_params=_cp("parallel"),
    )(*gs, *ws, *ms, *vs)
    return [outs[4 * k:4 * k + 4] for k in range(n)]


def _adamw_w_ada(c_t, dmod_cols, w, m, v):
    R, C = w.shape

    def body(ct_ref, dm_ref, w_ref, m_ref, v_ref, g_ref, d_ref, nm_ref, nv_ref):
        g = _dot(ct_ref[...].astype(BF16), dm_ref[...].astype(BF16))
        d, nm, nv = _adamw(w_ref[...], g, m_ref[...], v_ref[...])
        g_ref[...] = g
        d_ref[...] = d
        nm_ref[...] = nm
        nv_ref[...] = nv

    tr = 256
    spec = pl.BlockSpec((tr, C), lambda i: (i, 0))
    sds = jax.ShapeDtypeStruct((R, C), F32)
    return pl.pallas_call(
        body, out_shape=(sds,) * 4, grid=(R // tr,), name="adamw_w_ada",
        in_specs=[pl.BlockSpec((tr, 128), lambda i: (i, 0)), pl.BlockSpec((128, C), lambda i: (0, 0))] + [spec] * 3,
        out_specs=(spec,) * 4, compiler_params=_cp("parallel"),
    )(c_t, dmod_cols, w, m, v)


def _adamw_small(small_all, ws, ms, vs):
    def body(s_ref, w_ref, m_ref, v_ref, g_ref, d_ref, nm_ref, nv_ref):
        g = s_ref[0]
        for b in range(1, N_DEV):
            g = g + s_ref[b]
        d, nm, nv = _adamw(w_ref[...], g, m_ref[...], v_ref[...])
        g_ref[...] = g
        d_ref[...] = d
        nm_ref[...] = nm
        nv_ref[...] = nv

    sds = jax.ShapeDtypeStruct((SMALL_ROWS, D_MODEL), F32)
    return pl.pallas_call(
        body, out_shape=(sds,) * 4, name="adamw_small", in_specs=[VMEM_SPEC] * 4, out_specs=(VMEM_SPEC,) * 4,
        compiler_params=pltpu.CompilerParams(vmem_limit_bytes=VMEM_LIMIT_V7X),
    )(small_all, ws, ms, vs)


ROW_MOD, ROW_NORM_G, ROW_CONV_B, ROW_BA, ROW_BX, ROW_LAM, ROW_FINAL_G, ROW_SINKS, ROW_CONV_W, ROW_LOSS = 0, 3, 4, 5, 6, 7, 8, 9, 10, 14


def _pack_small(b_ada, norm_g, conv_b, ba, bx, lam, final_g, sinks, conv_w_full, loss_row=None):
    lane_pad = lambda a: jnp.pad(a.reshape(1, -1), ((0, 0), (0, D_MODEL - a.size)))
    rows = [b_ada.reshape(3, D_MODEL), norm_g, conv_b, ba, bx, lam, final_g.reshape(1, D_MODEL), lane_pad(sinks), conv_w_full,
            jnp.zeros((1, D_MODEL), F32) if loss_row is None else lane_pad(loss_row),
            jnp.zeros((SMALL_ROWS - ROW_LOSS - 1, D_MODEL), F32)]
    return jnp.concatenate([r.astype(F32) for r in rows], axis=0)


def kernel(x, c, positions, w_ada, b_ada, norm_g, w_in, attn_sinks, conv_w, conv_b, rg_wa, rg_ba, rg_wx, rg_bx, rg_lambda, w_attn_proj, w_rnn_proj, w_out, final_g, loss_target, m_w_ada, m_b_ada, m_norm_g, m_w_in, m_attn_sinks, m_conv_w, m_conv_b, m_rg_wa, m_rg_ba, m_rg_wx, m_rg_bx, m_rg_lambda, m_w_attn_proj, m_w_rnn_proj, m_w_out, m_final_g, v_w_ada, v_b_ada, v_norm_g, v_w_in, v_attn_sinks, v_conv_w, v_conv_b, v_rg_wa, v_rg_ba, v_rg_wx, v_rg_bx, v_rg_lambda, v_w_attn_proj, v_w_rnn_proj, v_w_out, v_final_g):
    T = x.shape[1]
    my_chip = lax.axis_index("x") * 2 + lax.axis_index("y")
    my_dev = my_chip * 2 + lax.axis_index("c")
    x2d, tgt = x[0], loss_target[0]
    pos_col = positions.reshape(T, 1)

    chip_idx = my_chip.reshape(1).astype(jnp.int32)
    c_idx = lax.axis_index("c").reshape(1).astype(jnp.int32)
    sq_place = ((D_MODEL, D_MODEL), (SHARD_ROWS, D_MODEL), lambda chip: (chip, 0))
    rg_place = ((RNN_BLOCKS, RNN_BW, RNN_BW), (RNN_BLOCKS, SHARD_RG, RNN_BW), lambda chip: (0, chip, 0))
    in_place = ((D_MODEL, IN_W), (D_MODEL, SHARD_IN), lambda chip: (0, chip))
    placed = _cast_place([w_in[0], w_attn_proj[0], w_rnn_proj[0], w_out[0], rg_wa[0], rg_wx[0]], chip_idx,
                         [in_place, sq_place, sq_place, sq_place, rg_place, rg_place])
    cw_chips, c_all, mod_chips = _gather_mod(c.reshape(1, 1, D_MODEL), w_ada[0], conv_w[0])
    g_ssems, g_rsems, fulls, g_token = _gather_start([p.reshape(s) for p, s in zip(placed, FULL_SHAPES)], mod_chips)
    conv_w_f = jnp.transpose(cw_chips, (1, 0, 2)).reshape(CONV_W, D_MODEL)
    mod_all = jnp.transpose(mod_chips, (1, 0, 2)).reshape(N_DEV, ADA_W) + b_ada
    mod_row = lax.dynamic_slice_in_dim(mod_all, my_dev, 1, axis=0) + g_token[0:1, 0:1]

    h, h_t, tabs = _prenorm(x2d, mod_row, norm_g, pos_col)
    w_in_v = fulls[0]
    proj = _in_projection(h, w_in_v.reshape(D_MODEL, IN_W), chip_idx, None, "own")
    for k, mask in enumerate(CHIP_MASKS):
        w_in_v = _gather_wait(g_ssems[k], g_rsems[k], [w_in_v], [0], proj, f"w_in_{k}")[0]
        w_in_v = _forward_halves([w_in_v], [(0, 0, k)], f"w_in_{k}")[0]
        from_chip = (chip_idx ^ (mask >> 1)).astype(jnp.int32)
        proj = _in_projection(h, w_in_v.reshape(D_MODEL, IN_W), from_chip, proj, f"from_{k}")
    w_in_f = w_in_v.reshape(D_MODEL, IN_W)
    rest = _gather_wait(g_ssems[3], g_rsems[3], list(fulls[1:]), [1, 2, 3, 4, 5], proj, "rest")
    rest = _forward_halves(rest, [(idx - 1, idx, k) for idx in range(1, N_BIG) for k in range(3)], "rest")
    wap_f, wrp_f, wo_f = (g.reshape(D_MODEL, D_MODEL) for g in rest[0:3])
    rwa_f, rwx_f = (g.reshape(RNN_BLOCKS, RNN_BW, RNN_BW) for g in rest[3:5])
    y_attn, qr_b, kr_b = _attn_forward(proj, tabs, attn_sinks)
    y_rnn, h_rnn, *rnn_saved = _rnn_forward(proj, pos_col, conv_w_f, conv_b, rwa_f, rwx_f, rg_ba, rg_bx, rg_lambda)
    (dx2, merged, d_o, d_pa, d_pr, d_ya, d_yr, d_c, d_final_g, d_gate, loss_vec) = _merge_and_head(
        x2d, tgt, y_attn, y_rnn, proj, wap_f, wrp_f, wo_f, mod_row, final_g.reshape(1, D_MODEL))

    sq = (N_CHIPS, 2, SHARD_ROWS // 2, D_MODEL)
    rg = (RNN_BLOCKS, N_CHIPS, 2, SHARD_RG // 2, RNN_BW)
    rg_flat = (RNN_BLOCKS * N_CHIPS, 2, SHARD_RG // 2, RNN_BW)

    def chip_sum_and_start(views, axes, flat, unflat, tags_, kinds_, group):
        from_sib = _swap_halves(views, axes)
        exact, rounded = [None] * len(views), [None] * len(views)
        for shape in dict.fromkeys(flat):
            ids = [k for k, f in enumerate(flat) if f == shape]
            ex, ro = _presum([views[k].reshape(shape) for k in ids],
                             [from_sib[k].reshape(shape[:1] + shape[2:]) for k in ids], c_idx, tags_[ids[0]])
            for k, e, r in zip(ids, ex, ro):
                exact[k], rounded[k] = e.reshape(unflat[k]), r.reshape(unflat[k])
        return _exchange_start(rounded, kinds_, group), exact

    g_ap = _weight_grad(y_attn, [(d_pa, 0, 2)], "w_attn_proj")
    g_rp = _weight_grad(y_rnn, [(d_pr, 0, 2)], "w_rnn_proj")
    g_o = _weight_grad(merged, [(d_o, 0, 2)], "w_out")
    sq_half = (N_CHIPS, SHARD_ROWS // 2, D_MODEL)
    started1, own1 = chip_sum_and_start([g_ap.reshape(sq), g_rp.reshape(sq), g_o.reshape(sq)], [1, 1, 1], [sq] * 3, [sq_half] * 3,
                                  ["w_attn_proj", "w_rnn_proj", "w_out"], ["sq"] * 3, "proj")
    d_q, d_kv, d_ga, d_sinks = _attn_backward(proj, qr_b, kr_b, d_ya, tabs, attn_sinks + started1[4][0, 0])
    d_b, d_conv_w, d_conv_b, d_rwa, d_rwx, d_ba, d_bx, d_lam = _rnn_backward(
        proj, pos_col, h_rnn, rnn_saved, d_yr, conv_w_f, rwa_f, rwx_f, rg_lambda)
    pieces = [(d_q, CB_Q, 2), (d_kv, CB_KV, 1), (d_ga, CB_GA, 2), (d_b, CB_XR, 4), (d_c, CB_MA, 4)]
    g_in = _weight_grad(h_t, pieces, "w_in", a_is_transposed=True)
    started2, own2 = chip_sum_and_start(
        [g_in.reshape(2, D_MODEL // 2, IN_W), d_rwa.reshape(rg), d_rwx.reshape(rg)], [0, 2, 2],
        [(1, 2, D_MODEL // 2, IN_W), rg_flat, rg_flat],
        [(D_MODEL // 2, IN_W), (RNN_BLOCKS, N_CHIPS, SHARD_RG // 2, RNN_BW), (RNN_BLOCKS, N_CHIPS, SHARD_RG // 2, RNN_BW)],
        ["w_in", "rg_wa", "rg_wx"], ["in", "rg", "rg"], "in")
    grad_x, d_shift, d_scale, d_norm_g = _input_backward(pieces, w_in_f, x2d, dx2, mod_row + started2[4][0, 0], norm_g)

    d_mod = jnp.concatenate([d_shift, d_scale, d_gate], axis=1)
    small = _pack_small(d_mod, d_norm_g, d_conv_b, d_ba, d_bx, d_lam, d_final_g, d_sinks[:, :N_HEADS], d_conv_w, loss_vec)
    small_all = _gather_small(small)
    _, lands1 = _exchange_wait(*started1[:4], grad_x, "proj")
    _, lands2 = _exchange_wait(*started2[:4], grad_x, "in")
    tags = ["w_in", "w_attn_proj", "w_rnn_proj", "w_out", "rg_wa", "rg_wx"]
    chip_sums = [own2[0]] + list(own1) + list(own2[1:])
    lands = [lands2[0]] + list(lands1) + list(lands2[1:])
    where = jnp.concatenate([chip_idx, c_idx])
    kinds = ["in", "sq", "sq", "sq", "rg", "rg"]
    groups = [[0], [1, 2, 3], [4, 5]]
    halves = [None] * 6
    for ids in groups:
        for i, half in zip(ids, _sum_landed(kinds[ids[0]], [chip_sums[i] for i in ids], [lands[i] for i in ids], where,
                                            tags[ids[0]])):
            halves[i] = half
    grads = _assemble_with_sibling(halves, [0, 0, 0, 0, 1, 1])
    shapes2d = [(D_MODEL, SHARD_IN), (SHARD_ROWS, D_MODEL), (SHARD_ROWS, D_MODEL), (SHARD_ROWS, D_MODEL),
                (RNN_BLOCKS * SHARD_RG, RNN_BW), (RNN_BLOCKS * SHARD_RG, RNN_BW)]
    big_w = [w_in, w_attn_proj, w_rnn_proj, w_out, rg_wa, rg_wx]
    big_m = [m_w_in, m_w_attn_proj, m_w_rnn_proj, m_w_out, m_rg_wa, m_rg_wx]
    big_v = [v_w_in, v_w_attn_proj, v_w_rnn_proj, v_w_out, v_rg_wa, v_rg_wx]
    res = {}
    for ids in groups:
        flat2d = lambda arrs: [arrs[i].reshape(shapes2d[i]) for i in ids]
        outs = _adamw_shard(flat2d(grads), flat2d(big_w), flat2d(big_m), flat2d(big_v), tags[ids[0]])
        for i, four in zip(ids, outs):
            res[tags[i]] = [o.reshape(big_w[i].shape) for o in four]

    dmod_all = small_all[:, ROW_MOD:ROW_MOD + 3, :].reshape(N_DEV, ADA_W)
    dmod_cols = lax.dynamic_slice_in_dim(dmod_all, my_chip * SHARD_ADA, SHARD_ADA, axis=1)
    c_t = jnp.pad(jnp.transpose(c_all.reshape(N_DEV, D_MODEL)), ((0, 0), (0, 128 - N_DEV)))
    dmod_cols = jnp.pad(dmod_cols, ((0, 128 - N_DEV), (0, 0)))
    res["w_ada"] = [o.reshape(w_ada.shape) for o in _adamw_w_ada(c_t, dmod_cols, w_ada[0], m_w_ada[0], v_w_ada[0])]

    def full_conv(a):
        return lax.dynamic_update_slice_in_dim(jnp.zeros((CONV_W, D_MODEL), F32), a[0], my_chip * (D_MODEL // N_CHIPS), axis=1)

    packed = [_pack_small(p[0], p[1], p[2], p[3], p[4], p[5], p[6], p[7], full_conv(p[8])) for p in (
        (b_ada, norm_g, conv_b, rg_ba, rg_bx, rg_lambda, final_g, attn_sinks, conv_w),
        (m_b_ada, m_norm_g, m_conv_b, m_rg_ba, m_rg_bx, m_rg_lambda, m_final_g, m_attn_sinks, m_conv_w),
        (v_b_ada, v_norm_g, v_conv_b, v_rg_ba, v_rg_bx, v_rg_lambda, v_final_g, v_attn_sinks, v_conv_w))]
    small_out = _adamw_small(small_all, *packed)

    def unpack(slab):
        cw = lax.dynamic_slice_in_dim(slab[ROW_CONV_W:ROW_CONV_W + CONV_W], my_chip * (D_MODEL // N_CHIPS),
                                      D_MODEL // N_CHIPS, axis=1)
        return {
            "b_ada": slab[ROW_MOD:ROW_MOD + 3].reshape(1, ADA_W), "norm_g": slab[ROW_NORM_G:ROW_NORM_G + 1],
            "conv_b": slab[ROW_CONV_B:ROW_CONV_B + 1], "rg_ba": slab[ROW_BA:ROW_BA + 1], "rg_bx": slab[ROW_BX:ROW_BX + 1],
            "rg_lambda": slab[ROW_LAM:ROW_LAM + 1], "final_g": slab[ROW_FINAL_G], "attn_sinks": slab[ROW_SINKS:ROW_SINKS + 1, :N_HEADS],
            "conv_w": cw[None],
        }

    small_res = [unpack(s) for s in small_out]
    order = ["w_ada", "b_ada", "norm_g", "w_in", "attn_sinks", "conv_w", "conv_b", "rg_wa", "rg_ba", "rg_wx", "rg_bx",
             "rg_lambda", "w_attn_proj", "w_rnn_proj", "w_out", "final_g"]
    loss = small_out[0][ROW_LOSS, 0]
    outs = [loss, grad_x[None]]
    for kind in range(4):
        for name in order:
            outs.append(res[name][kind] if name in res else small_res[kind][name])
    return tuple(outs)
```

```python
import numpy as np
import jax
import jax.numpy as jnp
from jax import lax
from jax.experimental import pallas as pl
from jax.experimental.pallas import tpu as pltpu

F32 = jnp.float32
BF16 = jnp.bfloat16

D_MODEL = 1024
N_HEADS = 16
N_KV = 4
HEAD_DIM = 64
GROUP = N_HEADS // N_KV
BLOCK = 128
KV_W = N_KV * HEAD_DIM
ROT_HALF = 8
ROPE_THETA = 500000.0
ATTN_SCALE = 0.125
RNN_BLOCKS = 4
RNN_BW = 256
CONV_W = 4
LRU_C = 8.0
NORM_EPS = 1e-6
IN_W = 6656
CB = 512
N_CB = IN_W // CB
CB_Q, CB_KV, CB_GA, CB_XR, CB_GR, CB_MA, CB_MR = 0, 2, 3, 5, 7, 9, 11
V_COL_BLOCK = 5
N_CHIPS = 4
N_DEV = 8
SHARD_IN = IN_W // N_CHIPS
SHARD_ROWS = D_MODEL // N_CHIPS
SHARD_RG = RNN_BW // N_CHIPS
ADA_W = 3 * D_MODEL
SHARD_ADA = ADA_W // N_CHIPS
SMALL_ROWS = 16

ADAM_LR = 0.001
ADAM_B1 = 0.9
ADAM_B2 = 0.999
ADAM_EPS = 1e-08
ADAM_WD = 0.01
ADAM_STEP = 10

VMEM_LIMIT_V7X = 52 * 1024 * 1024
MESH = pl.DeviceIdType.MESH
ANY = pl.BlockSpec(memory_space=pl.ANY)
VMEM_SPEC = pl.BlockSpec(memory_space=pltpu.VMEM)


def _cp(*sem):
    return pltpu.CompilerParams(dimension_semantics=sem if sem else None, vmem_limit_bytes=VMEM_LIMIT_V7X)


def _dot(a, b):
    return jnp.dot(a, b, preferred_element_type=F32)


def _dot_nt(a, b):
    return lax.dot_general(a, b, (((1,), (1,)), ((), ())), preferred_element_type=F32)


def _dot_tn(a, b):
    return lax.dot_general(a, b, (((0,), (0,)), ((), ())), preferred_element_type=F32)


def _sigmoid(z):
    return 1.0 / (1.0 + jnp.exp(-z))


def _softplus(z):
    u = jnp.exp(-jnp.abs(z))
    log1p_u = jnp.where(u < 1e-3, u * (1.0 - u * (0.5 - u * (1.0 / 3.0))), jnp.log(1.0 + u))
    return jnp.maximum(z, 0.0) + log1p_u


def _rms(xf):
    return lax.rsqrt(jnp.mean(xf * xf, axis=-1, keepdims=True) + NORM_EPS)


def _me():
    return lax.axis_index("x"), lax.axis_index("y"), lax.axis_index("c")


def _peer(mask):
    x, y, c = _me()
    fx, fy, fc = (mask >> 2) & 1, (mask >> 1) & 1, mask & 1
    return (x ^ fx if fx else x, y ^ fy if fy else y, c ^ fc if fc else c)


def _chip_of(pos):
    return pos[0] * 2 + pos[1]


SIBLING_COLLECTIVE_ID = 0
SIBLING_ONLY = pltpu.CompilerParams(collective_id=SIBLING_COLLECTIVE_ID)


def _sibling_handshake():
    barrier = pltpu.get_barrier_semaphore()
    pl.semaphore_signal(barrier, inc=1, device_id=_peer(1), device_id_type=MESH)
    pl.semaphore_wait(barrier, 1)


CHIP_MASKS = (4, 2, 6)
ALL_MASKS = (1, 2, 3, 4, 5, 6, 7)


HBM_SPEC = pl.BlockSpec(memory_space=pltpu.HBM)
SEM_SPEC = pl.BlockSpec(memory_space=pltpu.SEMAPHORE)
SPLIT_COPY = pltpu.CompilerParams(has_side_effects=pltpu.SideEffectType.DATAFLOW_SIDE_EFFECTING)
N_BIG = 6
FULL_SHAPES = (
    (2, D_MODEL // 2, IN_W),
    (N_CHIPS, 2, SHARD_ROWS // 2, D_MODEL), (N_CHIPS, 2, SHARD_ROWS // 2, D_MODEL), (N_CHIPS, 2, SHARD_ROWS // 2, D_MODEL),
    (RNN_BLOCKS, N_CHIPS, 2, SHARD_RG // 2, RNN_BW), (RNN_BLOCKS, N_CHIPS, 2, SHARD_RG // 2, RNN_BW),
)


def _slot(full, idx, chip, half):
    if idx == 0:
        return full.at[half, :, pl.ds(pl.multiple_of(chip * SHARD_IN, 128), SHARD_IN)]
    return full.at[chip, half] if idx in (1, 2, 3) else full.at[:, chip, half]


def _three_halves(full, idx):
    return full.at[pl.ds(0, 3), 0] if idx in (1, 2, 3) else full.at[:, pl.ds(0, 3), 0]


def _gather_start(fulls, after):
    def body(*refs):
        full_refs = refs[:N_BIG]
        ssems, rsems = refs[N_BIG + 1:N_BIG + 5], refs[N_BIG + 5:N_BIG + 9]
        token = refs[2 * N_BIG + 9]
        me = _me()
        my_chip = _chip_of(me)
        for idx in range(N_BIG):
            for k, mask in enumerate(CHIP_MASKS):
                pair = k if idx == 0 else 3
                mine = _slot(full_refs[idx], idx, my_chip, me[2])
                pltpu.make_async_remote_copy(src_ref=mine, dst_ref=mine, send_sem=ssems[pair], recv_sem=rsems[pair],
                                             device_id=_peer(mask), device_id_type=MESH).start()
        token[...] = jnp.zeros_like(token)

    sem = pltpu.SemaphoreType.DMA(())
    out_shape = (sem,) * 8 + tuple(pltpu.HBM(f.shape, f.dtype) for f in fulls) + (jax.ShapeDtypeStruct((8, 128), F32),)
    outs = pl.pallas_call(
        body, out_shape=out_shape, name="gather_start",
        in_specs=[HBM_SPEC] * N_BIG + [ANY], out_specs=tuple([SEM_SPEC] * 8 + [HBM_SPEC] * N_BIG + [VMEM_SPEC]),
        input_output_aliases={i: 8 + i for i in range(N_BIG)}, compiler_params=SPLIT_COPY,
    )(*[pltpu.with_memory_space_constraint(f, pltpu.HBM) for f in fulls], after)
    return outs[0:4], outs[4:8], outs[8:8 + N_BIG], outs[8 + N_BIG]


def _gather_wait(ssem, rsem, arrays, idxs, after, tag):
    n = len(arrays)

    def body(*refs):
        full_refs, ssem_ref, rsem_ref = refs[:n], refs[n], refs[n + 1]
        me = _me()
        for full, idx in zip(full_refs, idxs):
            region = _slot(full, 0, _chip_of(me), me[2]) if idx == 0 else _three_halves(full, idx)
            arrived = pltpu.make_async_remote_copy(
                src_ref=region, dst_ref=region, send_sem=ssem_ref, recv_sem=rsem_ref, device_id=me, device_id_type=MESH)
            arrived.wait_send()
            arrived.wait_recv()

    outs = pl.pallas_call(
        body, out_shape=tuple(pltpu.HBM(a.shape, a.dtype) for a in arrays), name=f"gather_wait_{tag}",
        in_specs=[HBM_SPEC] * n + [SEM_SPEC, SEM_SPEC, ANY], out_specs=tuple([HBM_SPEC] * n),
        input_output_aliases={i: i for i in range(n)}, compiler_params=SPLIT_COPY,
    )(*arrays, ssem, rsem, after)
    return list(outs)


def _forward_halves(arrays, items, tag):
    n, m = len(arrays), len(items)

    def body(*refs):
        outs, ssem, rsem = refs[n:2 * n], refs[2 * n], refs[2 * n + 1]
        me = _me()
        sib = _peer(1)
        _sibling_handshake()
        cps = []
        for j, (pos, idx, k) in enumerate(items):
            chip = _chip_of(_peer(CHIP_MASKS[k]))
            cp = pltpu.make_async_remote_copy(
                src_ref=_slot(outs[pos], idx, chip, me[2]), dst_ref=_slot(outs[pos], idx, chip, me[2]),
                send_sem=ssem.at[j], recv_sem=rsem.at[j], device_id=sib, device_id_type=MESH)
            cp.start()
            cps.append(cp)
        for j, (pos, idx, k) in enumerate(items):
            chip = _chip_of(_peer(CHIP_MASKS[k]))
            pltpu.make_async_remote_copy(
                src_ref=_slot(outs[pos], idx, chip, me[2]), dst_ref=_slot(outs[pos], idx, chip, 1 - me[2]),
                send_sem=ssem.at[j], recv_sem=rsem.at[j], device_id=sib, device_id_type=MESH).wait_recv()
        for cp in cps:
            cp.wait_send()

    outs = pl.pallas_call(
        body, out_shape=tuple(jax.ShapeDtypeStruct(a.shape, a.dtype) for a in arrays), name=f"forward_halves_{tag}",
        in_specs=[ANY] * n, out_specs=tuple([ANY] * n), input_output_aliases={i: i for i in range(n)},
        scratch_shapes=[pltpu.SemaphoreType.DMA((m,)), pltpu.SemaphoreType.DMA((m,))], compiler_params=SIBLING_ONLY,
    )(*arrays)
    return list(outs)


def _gather_mod(c_row, w_ada_s, conv_w_s):
    def body(c_ref, wada_ref, cw_s, cw_f, call_ref, mod_ref, wsend, wrecv, lsem, csend, crecv, msend, mrecv):
        me = _me()
        my_chip = _chip_of(me)
        my_dev = my_chip * 2 + me[2]
        sends = []
        for k, mask in enumerate(CHIP_MASKS):
            cp = pltpu.make_async_remote_copy(src_ref=cw_s, dst_ref=cw_f.at[my_chip], send_sem=wsend.at[k], recv_sem=wrecv.at[k],
                                              device_id=_peer(mask), device_id_type=MESH)
            cp.start()
            sends.append(cp)
        local = [pltpu.make_async_copy(cw_s, cw_f.at[my_chip], lsem.at[0])]
        for cp in local:
            cp.start()

        call_ref[my_dev] = c_ref[0]
        csends = []
        for k, mask in enumerate(ALL_MASKS):
            cp = pltpu.make_async_remote_copy(
                src_ref=c_ref.at[0], dst_ref=call_ref.at[my_dev],
                send_sem=csend.at[k], recv_sem=crecv.at[k], device_id=_peer(mask), device_id_type=MESH)
            cp.start()
            csends.append(cp)
        for k, mask in enumerate(ALL_MASKS):
            frm = _peer(mask)
            pltpu.make_async_remote_copy(
                src_ref=c_ref.at[0], dst_ref=call_ref.at[_chip_of(frm) * 2 + frm[2]],
                send_sem=csend.at[k], recv_sem=crecv.at[k], device_id=frm, device_id_type=MESH).wait_recv()
        for cp in csends:
            cp.wait_send()

        c_all = call_ref[...].reshape(N_DEV, D_MODEL).astype(BF16)
        mod_ref[my_chip] = _dot(c_all, wada_ref[...].astype(BF16))
        msends = []
        for k, mask in enumerate(CHIP_MASKS):
            cp = pltpu.make_async_remote_copy(
                src_ref=mod_ref.at[my_chip], dst_ref=mod_ref.at[my_chip],
                send_sem=msend.at[k], recv_sem=mrecv.at[k], device_id=_peer(mask), device_id_type=MESH)
            cp.start()
            msends.append(cp)
        for k, mask in enumerate(CHIP_MASKS):
            frm = _peer(mask)
            pltpu.make_async_remote_copy(
                src_ref=mod_ref.at[my_chip], dst_ref=mod_ref.at[_chip_of(frm)],
                send_sem=msend.at[k], recv_sem=mrecv.at[k], device_id=frm, device_id_type=MESH).wait_recv()
        for cp in msends:
            cp.wait_send()

        for k, mask in enumerate(CHIP_MASKS):
            frm = _peer(mask)
            pltpu.make_async_remote_copy(src_ref=cw_s, dst_ref=cw_f.at[_chip_of(frm)], send_sem=wsend.at[k], recv_sem=wrecv.at[k],
                                         device_id=frm, device_id_type=MESH).wait_recv()
        for cp in sends:
            cp.wait_send()
        for cp in local:
            cp.wait()

    out_shape = (
        jax.ShapeDtypeStruct((N_CHIPS, CONV_W, D_MODEL // N_CHIPS), F32),
        jax.ShapeDtypeStruct((N_DEV, 1, D_MODEL), F32),
        jax.ShapeDtypeStruct((N_CHIPS, N_DEV, SHARD_ADA), F32),
    )
    return pl.pallas_call(
        body, out_shape=out_shape, name="gather_mod",
        in_specs=[VMEM_SPEC, VMEM_SPEC, ANY], out_specs=(ANY, VMEM_SPEC, VMEM_SPEC),
        scratch_shapes=[
            pltpu.SemaphoreType.DMA((3,)), pltpu.SemaphoreType.DMA((3,)), pltpu.SemaphoreType.DMA((1,)),
            pltpu.SemaphoreType.DMA((7,)), pltpu.SemaphoreType.DMA((7,)),
            pltpu.SemaphoreType.DMA((3,)), pltpu.SemaphoreType.DMA((3,)),
        ],
        compiler_params=pltpu.CompilerParams(vmem_limit_bytes=VMEM_LIMIT_V7X),
    )(c_row, w_ada_s, conv_w_s)


def _cast_place(shards, chip_idx, places):
    n = len(shards)

    def body(chip_ref, *refs):
        for s_ref, o_ref in zip(refs[:n], refs[n:]):
            o_ref[...] = s_ref[...].astype(BF16)

    grid_spec = pltpu.PrefetchScalarGridSpec(
        num_scalar_prefetch=1, grid=(1,),
        in_specs=[pl.BlockSpec(s.shape, lambda i, chip_ref, nd=s.ndim: (0,) * nd) for s in shards],
        out_specs=tuple(pl.BlockSpec(block, lambda i, chip_ref, im=im: im(chip_ref[0])) for _, block, im in places))
    return pl.pallas_call(
        body, out_shape=tuple(jax.ShapeDtypeStruct(full, BF16) for full, _, _ in places), grid_spec=grid_spec,
        name="cast_place", compiler_params=_cp("arbitrary"),
    )(chip_idx, *shards)


def _shard_of(ref, kind, chip):
    if kind == "in":
        return ref.at[:, pl.ds(pl.multiple_of(chip * SHARD_IN, 128), SHARD_IN)]
    return ref.at[chip] if kind == "sq" else ref.at[:, chip]


def _land_shape(src, kind):
    if kind == "in":
        return (3, src.shape[0], SHARD_IN)
    return (3,) + src.shape[1:] if kind == "sq" else (3, src.shape[0]) + src.shape[2:]


def _exchange_start(srcs, kinds, tag):
    n = len(srcs)
    lands = [pltpu.with_memory_space_constraint(lax.empty(_land_shape(s, k), s.dtype), pltpu.HBM) for s, k in zip(srcs, kinds)]

    def body(*refs):
        src_refs, land_refs = refs[:n], refs[n:2 * n]
        ssems, rsems = refs[2 * n:3 * n], refs[3 * n:4 * n]
        token = refs[6 * n]
        for i in range(n):
            for k, mask in enumerate(CHIP_MASKS):
                to = _peer(mask)
                pltpu.make_async_remote_copy(
                    src_ref=_shard_of(src_refs[i], kinds[i], _chip_of(to)), dst_ref=land_refs[i].at[k],
                    send_sem=ssems[i], recv_sem=rsems[i], device_id=to, device_id_type=MESH).start()
        token[...] = jnp.zeros_like(token)

    sem = pltpu.SemaphoreType.DMA(())
    out_shape = ((sem,) * (2 * n) + tuple(pltpu.HBM(s.shape, s.dtype) for s in srcs)
                 + tuple(pltpu.HBM(l.shape, l.dtype) for l in lands) + (jax.ShapeDtypeStruct((8, 128), F32),))
    outs = pl.pallas_call(
        body, out_shape=out_shape, name=f"exchange_start_{tag}",
        in_specs=[HBM_SPEC] * (2 * n), out_specs=tuple([SEM_SPEC] * (2 * n) + [HBM_SPEC] * (2 * n) + [VMEM_SPEC]),
        input_output_aliases={i: 2 * n + i for i in range(2 * n)},
        compiler_params=pltpu.CompilerParams(has_side_effects=pltpu.SideEffectType.DATAFLOW_SIDE_EFFECTING),
    )(*[pltpu.with_memory_space_constraint(s, pltpu.HBM) for s in srcs], *lands)
    return outs[:n], outs[n:2 * n], outs[2 * n:3 * n], outs[3 * n:4 * n], outs[4 * n]


def _exchange_wait(ssems, rsems, srcs, lands, after, tag):
    n = len(srcs)

    def body(*refs):
        land_refs = refs[n:2 * n]
        ssem_refs, rsem_refs = refs[2 * n:3 * n], refs[3 * n:4 * n]
        for i in range(n):
            all_three = pltpu.make_async_remote_copy(
                src_ref=land_refs[i], dst_ref=land_refs[i], send_sem=ssem_refs[i], recv_sem=rsem_refs[i],
                device_id=_me(), device_id_type=MESH)
            all_three.wait_send()
            all_three.wait_recv()

    outs = pl.pallas_call(
        body, out_shape=tuple(pltpu.HBM(a.shape, a.dtype) for a in list(srcs) + list(lands)), name=f"exchange_wait_{tag}",
        in_specs=[HBM_SPEC] * (2 * n) + [SEM_SPEC] * (2 * n) + [ANY], out_specs=tuple([HBM_SPEC] * (2 * n)),
        input_output_aliases={i: i for i in range(2 * n)},
        compiler_params=pltpu.CompilerParams(has_side_effects=pltpu.SideEffectType.DATAFLOW_SIDE_EFFECTING),
    )(*srcs, *lands, *ssems, *rsems, after)
    return outs[:n], outs[n:]


def _gather_small(small):
    def body(small_ref, small_all, ssend, srecv):
        me = _me()
        my_dev = _chip_of(me) * 2 + me[2]
        small_all[my_dev] = small_ref[...]
        ssends = []
        for k, mask in enumerate(ALL_MASKS):
            cp = pltpu.make_async_remote_copy(
                src_ref=small_ref, dst_ref=small_all.at[my_dev],
                send_sem=ssend.at[k], recv_sem=srecv.at[k], device_id=_peer(mask), device_id_type=MESH)
            cp.start()
            ssends.append(cp)
        for k, mask in enumerate(ALL_MASKS):
            frm = _peer(mask)
            pltpu.make_async_remote_copy(
                src_ref=small_ref, dst_ref=small_all.at[_chip_of(frm) * 2 + frm[2]],
                send_sem=ssend.at[k], recv_sem=srecv.at[k], device_id=frm, device_id_type=MESH).wait_recv()
        for cp in ssends:
            cp.wait_send()

    return pl.pallas_call(
        body, out_shape=jax.ShapeDtypeStruct((N_DEV, SMALL_ROWS, D_MODEL), F32), name="gather_small",
        in_specs=[VMEM_SPEC], out_specs=VMEM_SPEC,
        scratch_shapes=[pltpu.SemaphoreType.DMA((7,)), pltpu.SemaphoreType.DMA((7,))],
    )(small)


def _half_of(ref, axis, half):
    return ref.at[(slice(None),) * axis + (half,)]


def _swap_halves(parts, axes):
    n = len(parts)

    def body(*refs):
        ins, outs, ssem, rsem = refs[:n], refs[n:2 * n], refs[2 * n], refs[2 * n + 1]
        c = lax.axis_index("c")
        _sibling_handshake()
        cps = [pltpu.make_async_remote_copy(src_ref=_half_of(ins[i], axes[i], 1 - c), dst_ref=outs[i], send_sem=ssem.at[i],
                                            recv_sem=rsem.at[i], device_id=_peer(1), device_id_type=MESH) for i in range(n)]
        for cp in cps:
            cp.start()
        for cp in cps:
            cp.wait()

    shapes = [p.shape[:a] + p.shape[a + 1:] for p, a in zip(parts, axes)]
    return pl.pallas_call(
        body, out_shape=tuple(jax.ShapeDtypeStruct(s, p.dtype) for s, p in zip(shapes, parts)), name="swap_halves",
        in_specs=[ANY] * n, out_specs=tuple([ANY] * n),
        scratch_shapes=[pltpu.SemaphoreType.DMA((n,)), pltpu.SemaphoreType.DMA((n,))], compiler_params=SIBLING_ONLY,
    )(*parts)


def _presum(mines, sibs, c_idx, tag):
    n = len(mines)
    S, _, R, C = mines[0].shape
    tr = min(R, 256)
    tc = SHARD_IN if C % SHARD_IN == 0 else (C // 2 if n > 1 and C % 256 == 0 else C)

    def body(c_ref, *refs):
        for k in range(n):
            total = refs[k][:, 0] + refs[n + k][...]
            refs[2 * n + k][...] = total
            refs[3 * n + k][...] = total.astype(BF16)

    out_spec = pl.BlockSpec((S, tr, tc), lambda i, j, c_ref: (0, i, j))
    grid_spec = pltpu.PrefetchScalarGridSpec(
        num_scalar_prefetch=1, grid=(R // tr, C // tc),
        in_specs=[pl.BlockSpec((S, 1, tr, tc), lambda i, j, c_ref: (0, c_ref[0], i, j))] * n + [out_spec] * n,
        out_specs=(out_spec,) * (2 * n))
    outs = pl.pallas_call(
        body, out_shape=(jax.ShapeDtypeStruct((S, R, C), F32),) * n + (jax.ShapeDtypeStruct((S, R, C), BF16),) * n,
        grid_spec=grid_spec, name=f"presum_{tag}", compiler_params=_cp("parallel", "parallel"),
    )(c_idx, *mines, *sibs)
    return list(outs[:n]), list(outs[n:])


def _assemble_with_sibling(parts, axes):
    n = len(parts)

    def body(*refs):
        outs, ssem, rsem = refs[n:2 * n], refs[2 * n], refs[2 * n + 1]
        c = lax.axis_index("c")
        _sibling_handshake()
        cps = [pltpu.make_async_remote_copy(
            src_ref=_half_of(outs[i], axes[i], c), dst_ref=_half_of(outs[i], axes[i], c), send_sem=ssem.at[i],
            recv_sem=rsem.at[i], device_id=_peer(1), device_id_type=MESH) for i in range(n)]
        for cp in cps:
            cp.start()
        for i in range(n):
            pltpu.make_async_remote_copy(
                src_ref=_half_of(outs[i], axes[i], c), dst_ref=_half_of(outs[i], axes[i], 1 - c), send_sem=ssem.at[i],
                recv_sem=rsem.at[i], device_id=_peer(1), device_id_type=MESH).wait_recv()
        for cp in cps:
            cp.wait_send()

    return pl.pallas_call(
        body, out_shape=tuple(jax.ShapeDtypeStruct(p.shape, p.dtype) for p in parts), name="assemble_with_sibling",
        in_specs=[ANY] * n, out_specs=tuple([ANY] * n), input_output_aliases={i: i for i in range(n)},
        scratch_shapes=[pltpu.SemaphoreType.DMA((n,)), pltpu.SemaphoreType.DMA((n,))], compiler_params=SIBLING_ONLY,
    )(*parts)


def _rope_lane_frequencies():
    inv = np.float32(ROPE_THETA) ** (-(np.arange(0, 2 * ROT_HALF, 2, dtype=np.float32)) / np.float32(2 * ROT_HALF))
    lane = np.arange(128) % HEAD_DIM
    return jnp.asarray(np.where(lane < 2 * ROT_HALF, inv[lane % ROT_HALF], 0.0).astype(np.float32)[None, :])


def _rope_tables(pos, freq):
    ang = pos.astype(F32) * freq
    c, s = jnp.cos(ang), jnp.sin(ang)
    m = lax.broadcasted_iota(jnp.int32, ang.shape, 1) & (HEAD_DIM - 1)
    return (jnp.where(m < 2 * ROT_HALF, c, 1.0), jnp.where(m < ROT_HALF, -s, 0.0),
            jnp.where((m >= ROT_HALF) & (m < 2 * ROT_HALF), s, 0.0))


def _columns(t):
    return [t[:, i:i + 128] for i in range(0, t.shape[-1], 128)]


def _rope(t, c, sa, sb):
    return jnp.concatenate(
        [x * c + pltpu.roll(x, 128 - ROT_HALF, 1) * sa + pltpu.roll(x, ROT_HALF, 1) * sb for x in _columns(t)], axis=1)


def _unrope(d, c, sa, sb):
    return jnp.concatenate(
        [x * c + pltpu.roll(x * sa, ROT_HALF, 1) + pltpu.roll(x * sb, 128 - ROT_HALF, 1) for x in _columns(d)], axis=1)


def _prenorm(x, mod_row, norm_g, pos_col):
    T = x.shape[0]
    tm = min(T, 512)

    def body(x_ref, mod_ref, g_ref, pos_ref, f_ref, h_ref, ht_ref, c_ref, sa_ref, sb_ref):
        xf = x_ref[...]
        shift, scale = mod_ref[:, 0:D_MODEL], mod_ref[:, D_MODEL:2 * D_MODEL]
        h = (xf * _rms(xf)) * g_ref[...] * (1.0 + scale) + shift
        h_ref[...] = h.astype(BF16)
        ht_ref[...] = h.T.astype(BF16)
        c_ref[...], sa_ref[...], sb_ref[...] = _rope_tables(pos_ref[...], f_ref[...])

    tab = jax.ShapeDtypeStruct((T, 128), F32)
    tok = lambda w: pl.BlockSpec((tm, w), lambda i: (i, 0))
    row = lambda w: pl.BlockSpec((1, w), lambda i: (0, 0))
    outs = pl.pallas_call(
        body, out_shape=(jax.ShapeDtypeStruct((T, D_MODEL), BF16), jax.ShapeDtypeStruct((D_MODEL, T), BF16), tab, tab, tab),
        grid=(T // tm,), name="prenorm",
        in_specs=[tok(D_MODEL), row(ADA_W), row(D_MODEL), tok(1), row(128)],
        out_specs=(tok(D_MODEL), pl.BlockSpec((D_MODEL, tm), lambda i: (0, i)), tok(128), tok(128), tok(128)),
        compiler_params=_cp("parallel"),
    )(x, mod_row, norm_g, pos_col, _rope_lane_frequencies())
    return outs[0], outs[1], tuple(outs[2:])


def _in_projection(h, w_in, chips, into, tag):
    T = h.shape[0]
    tm, tn = min(T, 512), SHARD_IN
    k = chips.shape[0]

    def body(chip_ref, h_ref, w_ref, *rest):
        rest[-1][...] = _dot(h_ref[...], w_ref[...])

    w_spec = pl.BlockSpec((D_MODEL, tn), lambda s, i, c: (0, c[s]), **({"pipeline_mode": pl.Buffered(1)} if k == 1 else {}))
    in_specs = [pl.BlockSpec((tm, D_MODEL), lambda s, i, c: (i, 0)), w_spec]
    args = [chips, h, w_in]
    aliases = {}
    if into is not None:
        in_specs.append(ANY)
        args.append(into)
        aliases = {3: 0}
    grid_spec = pltpu.PrefetchScalarGridSpec(num_scalar_prefetch=1, grid=(k, T // tm), in_specs=in_specs,
                                             out_specs=pl.BlockSpec((tm, tn), lambda s, i, c: (i, c[s])))
    return pl.pallas_call(
        body, out_shape=jax.ShapeDtypeStruct((T, IN_W), F32), grid_spec=grid_spec, name=f"in_projection_{tag}",
        input_output_aliases=aliases, compiler_params=_cp("parallel", "parallel"),
    )(*args)


def _attn_mask(n):
    qi = lax.broadcasted_iota(jnp.int32, (GROUP * BLOCK, BLOCK), 0) & (BLOCK - 1)
    j = lax.broadcasted_iota(jnp.int32, (GROUP * BLOCK, BLOCK), 1)
    own = j <= qi
    return own, jnp.logical_not(own) & (n == 0)


def _fold(x, own):
    return jnp.where(own, x[:, BLOCK:2 * BLOCK], x[:, 0:BLOCK])


def _unfold(xf, own):
    zero = jnp.zeros_like(xf)
    return jnp.concatenate([jnp.where(own, zero, xf), jnp.where(own, xf, zero)], axis=1)


ROW_GROUP_HEAD = (0, 2, 1, 3)


def _sink_col(sink_ref, kh):
    rowg = lax.broadcasted_iota(jnp.int32, (GROUP * BLOCK, 1), 0) // BLOCK
    col = jnp.full((GROUP * BLOCK, 1), sink_ref[0, GROUP * kh + ROW_GROUP_HEAD[0]], F32)
    for g in range(1, GROUP):
        col = jnp.where(rowg == g, sink_ref[0, GROUP * kh + ROW_GROUP_HEAD[g]], col)
    return col


def _low_lanes(shape):
    return lax.broadcasted_iota(jnp.int32, shape, 1) < HEAD_DIM


def _kv_pair_operand(prev, cur, kh):
    c = 128 * (kh // 2)
    col = jnp.concatenate([prev[:, c:c + 128], cur[:, c:c + 128]], axis=0).astype(F32)
    if kh % 2 == 0:
        lo = jnp.where(_low_lanes(col.shape), col, 0.0)
        hi = pltpu.roll(lo, HEAD_DIM, 1)
    else:
        hi = jnp.where(_low_lanes(col.shape), 0.0, col)
        lo = pltpu.roll(hi, HEAD_DIM, 1)
    return jnp.concatenate([lo, hi], axis=0).astype(BF16)


def _pair_rows(x, kh):
    c = 2 * 128 * kh
    return jnp.concatenate([x[:, c:c + 128], x[:, c + 128:c + 256]], axis=0)


def _restack(big):
    return jnp.concatenate([big[:, 0:2 * BLOCK], big[:, 2 * BLOCK:4 * BLOCK]], axis=0)


def _unrestack(stacked):
    return jnp.concatenate([stacked[0:2 * BLOCK], stacked[2 * BLOCK:4 * BLOCK]], axis=1)


def _fold_pair(x2, kh):
    low = _low_lanes((2 * BLOCK, 128))
    mixed = jnp.where(low, x2[0:2 * BLOCK], x2[2 * BLOCK:4 * BLOCK])
    total = mixed + pltpu.roll(mixed, HEAD_DIM, 1)
    return jnp.where(low, total, 0.0) if kh % 2 == 0 else jnp.where(low, 0.0, total)


def _attn_scores(qr, k2, kh):
    q2 = _pair_rows(qr, kh).astype(BF16)
    return q2, _restack(_dot_nt(q2, k2))


def _attn_softmax(s, sink_col, mask):
    own, no_key = mask
    s = jnp.where(no_key, -1e30, _fold(s, own))
    m = jnp.maximum(jnp.max(s, axis=-1, keepdims=True), sink_col)
    p = jnp.exp(s - m)
    p_sink = jnp.exp(sink_col - m)
    denom = jnp.sum(p, axis=-1, keepdims=True) + p_sink
    return p / denom, p_sink / denom


def _attn_forward(proj, tabs, sinks):
    T = proj.shape[0]
    nb = T // BLOCK

    def body(q_ref, kvc_ref, kvp_ref, g0_ref, g1_ref, cc, sac, sbc, cp_, sap, sbp, sink_ref, y_ref, qrb_ref, krb_ref):
        n = pl.program_id(0)
        tc = tcur = (cc[...], sac[...], sbc[...])
        tprev = (cp_[...], sap[...], sbp[...])
        qr = _rope(q_ref[...], *tc) * ATTN_SCALE
        kr_cur = _rope(kvc_ref[:, 0:KV_W], *tcur)
        kr_prev = _rope(kvp_ref[:, 0:KV_W], *tprev)
        qrb_ref[...] = qr.astype(BF16)
        krb_ref[...] = kr_cur.astype(BF16)
        v_cur, v_prev = kvc_ref[:, KV_W:2 * KV_W], kvp_ref[:, KV_W:2 * KV_W]
        mask = _attn_mask(n)
        outs = []
        k2s = [_kv_pair_operand(kr_prev, kr_cur, kh) for kh in range(N_KV)]
        v2s = [_kv_pair_operand(v_prev, v_cur, kh) for kh in range(N_KV)]
        scores = [_attn_scores(qr, k2s[kh], kh) for kh in range(N_KV)]
        for kh in range(N_KV):
            pn, _ = _attn_softmax(scores[kh][1], _sink_col(sink_ref, kh), mask)
            o_big = _dot(_unrestack(_unfold(pn.astype(BF16), mask[0])), v2s[kh])
            outs += [o_big[0:BLOCK], o_big[BLOCK:2 * BLOCK]]
        o = jnp.concatenate(outs, axis=1)
        g = jnp.concatenate([g0_ref[...], g1_ref[...]], axis=1)
        y_ref[...] = (o * (g * _sigmoid(g))).astype(BF16)

    def blk(w, cb):
        return pl.BlockSpec((BLOCK, w), lambda n, cb=cb: (n, cb))

    prev = lambda w, cb: pl.BlockSpec((BLOCK, w), lambda n, cb=cb: (jnp.maximum(n - 1, 0), cb))
    return pl.pallas_call(
        body, grid=(nb,), name="attn_forward",
        out_shape=(jax.ShapeDtypeStruct((T, D_MODEL), BF16), jax.ShapeDtypeStruct((T, D_MODEL), BF16),
                   jax.ShapeDtypeStruct((T, KV_W), BF16)),
        in_specs=[blk(D_MODEL, 0), blk(CB, CB_KV), prev(CB, CB_KV), blk(CB, CB_GA), blk(CB, CB_GA + 1),
                  blk(128, 0), blk(128, 0), blk(128, 0), prev(128, 0), prev(128, 0), prev(128, 0),
                  pl.BlockSpec(memory_space=pltpu.SMEM)],
        out_specs=(blk(D_MODEL, 0), blk(D_MODEL, 0), blk(KV_W, 0)),
        compiler_params=_cp("parallel"),
    )(proj, proj, proj, proj, proj, *tabs, *tabs, sinks)


def _scan_rows8():
    return lax.broadcasted_iota(jnp.int32, (8, D_MODEL), 0)


def _scan_forward(a_ref, b_ref, h_ref, carry, rows):
    row = _scan_rows8()

    def group(i, carry):
        off = pl.multiple_of(i * 8, 8)
        a, b = a_ref[pl.ds(off, 8), :], b_ref[pl.ds(off, 8), :]
        for d in (1, 2, 4):
            ok = row >= d
            b = jnp.where(ok, a * pltpu.roll(b, d, 0) + b, b)
            a = jnp.where(ok, a * pltpu.roll(a, d, 0), a)
        h = a * carry + b
        h_ref[pl.ds(off, 8), :] = h
        return h[7:8, :]

    return lax.fori_loop(0, rows // 8, group, carry)


def _scan_backward(a_ref, g_ref, lam_ref, carry, rows):
    row = _scan_rows8()

    def group(i, carry):
        off = pl.multiple_of((rows // 8 - 1 - i) * 8, 8)
        a, g = a_ref[pl.ds(off, 8), :], g_ref[pl.ds(off, 8), :]
        b = a * g
        for d in (1, 2, 4):
            ok = row < 8 - d
            b = jnp.where(ok, a * pltpu.roll(b, 8 - d, 0) + b, b)
            a = jnp.where(ok, a * pltpu.roll(a, 8 - d, 0), a)
        mu = a * carry + b
        mu_below = jnp.where(row == 7, carry, pltpu.roll(mu, 7, 0))
        lam_ref[pl.ds(off, 8), :] = g + mu_below
        return mu[0:1, :]

    return lax.fori_loop(0, rows // 8, group, carry)


def _conv_taps(xbuf, xr, tail):
    rows = xr.shape[0]
    xbuf[0:8, :] = tail
    xbuf[8:rows + 8, :] = xr
    return [xbuf[pl.ds(8 - (CONV_W - 1 - k), rows), :] for k in range(CONV_W - 1)] + [xr]


def _rnn_gates(xbuf, xr, tail, cw, cb, wa_ref, wx_ref, ba, bx, sp, reset):
    xs = _conv_taps(xbuf, xr, tail)
    xc = xs[0] * cw[0:1, :]
    for k in range(1, CONV_W):
        xc = xc + xs[k] * cw[k:k + 1, :]
    xc = xc + cb
    xcb = xc.astype(BF16)
    za = jnp.concatenate([_dot(xcb[:, RNN_BW * j:RNN_BW * (j + 1)], wa_ref[j]) for j in range(RNN_BLOCKS)], axis=1) + ba
    zx = jnp.concatenate([_dot(xcb[:, RNN_BW * j:RNN_BW * (j + 1)], wx_ref[j]) for j in range(RNN_BLOCKS)], axis=1) + bx
    r, i = _sigmoid(za), _sigmoid(zx)
    neg_log_a = LRU_C * r * sp
    a_raw = jnp.exp(-neg_log_a)
    mult_raw = jnp.sqrt(jnp.tanh(neg_log_a) * (1.0 + a_raw * a_raw))
    a = jnp.where(reset, 0.0, a_raw)
    mult = jnp.where(reset, 1.0, mult_raw)
    return xc, r, i, a, mult


def _rnn_forward(proj, pos_col, conv_w, conv_b, rwa, rwx, ba, bx, lam):
    T = proj.shape[0]
    tr = min(T, 512)

    def body(x0, x1, g0, g1, pos_ref, cw_ref, cb_ref, wa_ref, wx_ref, ba_ref, bx_ref, lam_ref,
             y_ref, h_ref, xc_ref, r_ref, i_ref, a_ref, mult_ref, xbuf, bbuf, tail, carry):
        t = pl.program_id(0)

        @pl.when(t == 0)
        def _():
            tail[...] = jnp.zeros_like(tail)
            carry[...] = jnp.zeros_like(carry)

        xr = jnp.concatenate([x0[...], x1[...]], axis=1)
        sp = _softplus(-lam_ref[...])
        reset = pos_ref[...] == 0
        xc, r, i, a, mult = _rnn_gates(
            xbuf, xr, tail[...], cw_ref[...], cb_ref[...], wa_ref, wx_ref, ba_ref[...], bx_ref[...], sp, reset)
        xc_ref[...] = xc
        r_ref[...] = r
        i_ref[...] = i
        a_ref[...] = a
        mult_ref[...] = mult
        bbuf[...] = mult * (i * xc)
        last = _scan_forward(a_ref, bbuf, h_ref, carry[0:1, :], tr)
        carry[...] = jnp.broadcast_to(last, carry.shape)
        tail[...] = xr[tr - 8:tr, :]
        g = jnp.concatenate([g0[...], g1[...]], axis=1)
        y_ref[...] = (h_ref[...] * (g * _sigmoid(g))).astype(BF16)

    blk = lambda cb: pl.BlockSpec((tr, CB), lambda t, cb=cb: (t, cb))
    row = lambda w: pl.BlockSpec((1, w), lambda t: (0, 0))
    full3 = pl.BlockSpec((RNN_BLOCKS, RNN_BW, RNN_BW), lambda t: (0, 0, 0))
    tok = pl.BlockSpec((tr, D_MODEL), lambda t: (t, 0))
    act = jax.ShapeDtypeStruct((T, D_MODEL), F32)
    return pl.pallas_call(
        body, out_shape=(jax.ShapeDtypeStruct((T, D_MODEL), BF16),) + (act,) * 6,
        grid=(T // tr,), name="rnn_forward",
        in_specs=[blk(CB_XR), blk(CB_XR + 1), blk(CB_GR), blk(CB_GR + 1), pl.BlockSpec((tr, 1), lambda t: (t, 0)),
                  pl.BlockSpec((CONV_W, D_MODEL), lambda t: (0, 0)), row(D_MODEL), full3, full3,
                  row(D_MODEL), row(D_MODEL), row(D_MODEL)],
        out_specs=(tok,) * 7,
        scratch_shapes=[pltpu.VMEM((tr + 8, D_MODEL), F32), pltpu.VMEM((tr, D_MODEL), F32),
                        pltpu.VMEM((8, D_MODEL), F32), pltpu.VMEM((8, D_MODEL), F32)],
        compiler_params=_cp("arbitrary"),
    )(proj, proj, proj, proj, pos_col, conv_w, conv_b, rwa, rwx, ba, bx, lam)


ROW_PARTS = 1


def _merge_and_head(x, target, y_attn, y_rnn, proj, wap, wrp, wo, mod_row, final_g):
    T = x.shape[0]
    tm = min(T, 256)

    def body(x_ref, t_ref, ya_ref, yr_ref, ma0, ma1, mr0, mr1, wap_ref, wrp_ref, wo_ref, mod_ref, fg_ref,
             dx2_ref, mg_ref, do_ref, dpa_ref, dpr_ref, dya_ref, dyr_ref, dc_ref, dfg_ref, dgate_ref, loss_ref):
        i = pl.program_id(0)
        gate = mod_ref[:, 2 * D_MODEL:3 * D_MODEL]
        fg = fg_ref[...]
        parts = [slice(p * (tm // ROW_PARTS), (p + 1) * (tm // ROW_PARTS)) for p in range(ROW_PARTS)]
        each = range(ROW_PARTS)
        pa = [_dot(ya_ref[r, :], wap_ref[...]) for r in parts]
        pr = [_dot(yr_ref[r, :], wrp_ref[...]) for r in parts]
        sa = [_sigmoid(jnp.concatenate([ma0[r, :], ma1[r, :]], axis=1)) for r in parts]
        sr = [_sigmoid(jnp.concatenate([mr0[r, :], mr1[r, :]], axis=1)) for r in parts]
        mb = [(sa[p] * pa[p] + sr[p] * pr[p]).astype(BF16) for p in each]
        o = [_dot(mb[p], wo_ref[...]) for p in each]
        x2 = [x_ref[r, :] + gate * o[p] for p, r in enumerate(parts)]
        r2 = [_rms(v) for v in x2]
        xn2 = [x2[p] * r2[p] for p in each]
        err = [xn2[p] * fg - t_ref[r, :] for p, r in enumerate(parts)]
        dy = [e * (1.0 / D_MODEL) for e in err]
        dxn = [d * fg for d in dy]
        dx2 = [r2[p] * (dxn[p] - xn2[p] * jnp.mean(dxn[p] * xn2[p], axis=-1, keepdims=True)) for p in each]
        dob = [(dx2[p] * gate).astype(BF16) for p in each]
        dmerged = [_dot_nt(d, wo_ref[...]) for d in dob]
        dpa = [(dmerged[p] * sa[p]).astype(BF16) for p in each]
        dpr = [(dmerged[p] * sr[p]).astype(BF16) for p in each]
        dya = [_dot_nt(d, wap_ref[...]) for d in dpa]
        dyr = [_dot_nt(d, wrp_ref[...]) for d in dpr]
        loss_t, dfg_t, dgate_t = 0.0, 0.0, 0.0
        for p, r in enumerate(parts):
            dx2_ref[r, :] = dx2[p]
            mg_ref[r, :] = mb[p]
            do_ref[r, :] = dob[p]
            dpa_ref[r, :] = dpa[p]
            dpr_ref[r, :] = dpr[p]
            dya_ref[r, :] = dya[p]
            dyr_ref[r, :] = dyr[p]
            dc_ref[r, 0:D_MODEL] = (dmerged[p] * pa[p] * sa[p] * (1.0 - sa[p])).astype(BF16)
            dc_ref[r, D_MODEL:2 * D_MODEL] = (dmerged[p] * pr[p] * sr[p] * (1.0 - sr[p])).astype(BF16)
            loss_t = loss_t + 0.5 * jnp.sum(
                jnp.sum(err[p] * err[p], axis=-1, keepdims=True) * (1.0 / D_MODEL), axis=0, keepdims=True)
            dfg_t = dfg_t + jnp.sum(dy[p] * xn2[p], axis=0, keepdims=True)
            dgate_t = dgate_t + jnp.sum(dx2[p] * o[p], axis=0, keepdims=True)

        @pl.when(i == 0)
        def _():
            dfg_ref[...] = jnp.zeros_like(dfg_ref)
            dgate_ref[...] = jnp.zeros_like(dgate_ref)
            loss_ref[...] = jnp.zeros_like(loss_ref)

        dfg_ref[...] += dfg_t
        dgate_ref[...] += dgate_t
        loss_ref[...] += jnp.broadcast_to(loss_t, loss_ref.shape)

    tok = lambda w: pl.BlockSpec((tm, w), lambda i: (i, 0))
    blk = lambda cb: pl.BlockSpec((tm, CB), lambda i, cb=cb: (i, cb))
    wfull = pl.BlockSpec((D_MODEL, D_MODEL), lambda i: (0, 0), pipeline_mode=pl.Buffered(1))
    row = lambda w: pl.BlockSpec((1, w), lambda i: (0, 0))
    out_shape = (
        jax.ShapeDtypeStruct((T, D_MODEL), F32), jax.ShapeDtypeStruct((T, D_MODEL), BF16),
        jax.ShapeDtypeStruct((T, D_MODEL), BF16), jax.ShapeDtypeStruct((T, D_MODEL), BF16),
        jax.ShapeDtypeStruct((T, D_MODEL), BF16), jax.ShapeDtypeStruct((T, D_MODEL), F32),
        jax.ShapeDtypeStruct((T, D_MODEL), F32), jax.ShapeDtypeStruct((T, 2 * D_MODEL), BF16),
        jax.ShapeDtypeStruct((1, D_MODEL), F32), jax.ShapeDtypeStruct((1, D_MODEL), F32),
        jax.ShapeDtypeStruct((1, 128), F32),
    )
    return pl.pallas_call(
        body, out_shape=out_shape, grid=(T // tm,), name="merge_and_head",
        in_specs=[tok(D_MODEL), tok(D_MODEL), tok(D_MODEL), tok(D_MODEL), blk(CB_MA), blk(CB_MA + 1), blk(CB_MR),
                  blk(CB_MR + 1), wfull, wfull, wfull, row(ADA_W), row(D_MODEL)],
        out_specs=(tok(D_MODEL),) * 7 + (tok(2 * D_MODEL), row(D_MODEL), row(D_MODEL), row(128)),
        compiler_params=_cp("arbitrary"),
    )(x, target, y_attn, y_rnn, proj, proj, proj, proj, wap, wrp, wo, mod_row, final_g)


def _attn_backward(proj, qr_b, kr_b, d_y, tabs, sinks):
    T = proj.shape[0]
    nb = T // BLOCK

    def body(qrb_ref, krc_ref, krp_ref, vc_ref, vp_ref, g0_ref, g1_ref, dy_ref, cc, sac, sbc, cp_, sap, sbp, sink_ref,
             dq_ref, dkv_ref, dg_ref, dsink_ref, carry):
        n = pl.program_id(0)

        @pl.when(n == 0)
        def _():
            carry[...] = jnp.zeros_like(carry)
            dsink_ref[...] = jnp.zeros_like(dsink_ref)

        @pl.when(n < nb)
        def _():
            tc = tcur = (cc[...], sac[...], sbc[...])
            tprev = (cp_[...], sap[...], sbp[...])
            qr, kr_cur, kr_prev = qrb_ref[...], krc_ref[...], krp_ref[...]
            v_cur, v_prev = vc_ref[...], vp_ref[...]
            g = jnp.concatenate([g0_ref[...], g1_ref[...]], axis=1)
            sg = _sigmoid(g)
            dy = dy_ref[...]
            d_o = dy * (g * sg)
            mask = _attn_mask(n)
            lane = lax.broadcasted_iota(jnp.int32, (1, 128), 1)
            rowg = lax.broadcasted_iota(jnp.int32, (GROUP * BLOCK, 1), 0) // BLOCK
            o_parts, dq_parts = [], []
            dk_cols, dv_cols = [None, None], [None, None]
            dsink = jnp.zeros((1, 128), F32)
            heads = range(N_KV)
            k2s = [_kv_pair_operand(kr_prev, kr_cur, kh) for kh in heads]
            v2s = [_kv_pair_operand(v_prev, v_cur, kh) for kh in heads]
            scores = [_attn_scores(qr, k2s[kh], kh) for kh in heads]
            do2s = [_pair_rows(d_o, kh).astype(BF16) for kh in heads]
            dpns = [_fold(_restack(_dot_nt(do2s[kh], v2s[kh])), mask[0]) for kh in heads]
            probs = [_attn_softmax(scores[kh][1], _sink_col(sink_ref, kh), mask) for kh in heads]
            p_bigs = [_unrestack(_unfold(probs[kh][0].astype(BF16), mask[0])) for kh in heads]
            o_bigs = [_dot(p_bigs[kh], v2s[kh]) for kh in heads]
            dv2s = [_dot_tn(p_bigs[kh], do2s[kh]) for kh in heads]
            deltas = [jnp.sum(probs[kh][0] * dpns[kh], axis=-1, keepdims=True) for kh in heads]
            ds_bigs = [_unrestack(_unfold((probs[kh][0] * (dpns[kh] - deltas[kh])).astype(BF16), mask[0])) for kh in heads]
            dq2s = [_dot(ds_bigs[kh], k2s[kh]) for kh in heads]
            dk2s = [_dot_tn(ds_bigs[kh], scores[kh][0]) for kh in heads]
            for kh in heads:
                o_parts += [o_bigs[kh][0:BLOCK], o_bigs[kh][BLOCK:2 * BLOCK]]
                dq_parts += [dq2s[kh][0:BLOCK], dq2s[kh][BLOCK:2 * BLOCK]]
                dk_c, dv_c = _fold_pair(dk2s[kh], kh), _fold_pair(dv2s[kh], kh)
                c = kh // 2
                dk_cols[c] = dk_c if dk_cols[c] is None else dk_cols[c] + dk_c
                dv_cols[c] = dv_c if dv_cols[c] is None else dv_cols[c] + dv_c
                ds_rows = probs[kh][1] * deltas[kh]
                for gq in range(GROUP):
                    val = -jnp.sum(jnp.where(rowg == gq, ds_rows, 0.0), axis=0, keepdims=True)
                    dsink = dsink + jnp.where(lane == GROUP * kh + ROW_GROUP_HEAD[gq], val, 0.0)
            o = jnp.concatenate(o_parts, axis=1)
            dg_ref[...] = (dy * o * (sg * (1.0 + g * (1.0 - sg)))).astype(BF16)
            dq_ref[...] = (_unrope(jnp.concatenate(dq_parts, axis=1), *tc) * ATTN_SCALE).astype(BF16)
            dk_all, dv_all = jnp.concatenate(dk_cols, axis=1), jnp.concatenate(dv_cols, axis=1)
            dk_prev = _unrope(dk_all[0:BLOCK], *tprev)
            dk_cur = _unrope(dk_all[BLOCK:2 * BLOCK], *tcur)
            dv_prev, dv_cur = dv_all[0:BLOCK], dv_all[BLOCK:2 * BLOCK]
            dkv_ref[...] = (carry[...] + jnp.concatenate([dk_prev, dv_prev], axis=1)).astype(BF16)
            carry[...] = jnp.concatenate([dk_cur, dv_cur], axis=1)
            dsink_ref[...] += dsink

        @pl.when(n == nb)
        def _():
            dkv_ref[...] = carry[...].astype(BF16)

    cur = lambda w, cb: pl.BlockSpec((BLOCK, w), lambda n, cb=cb: (jnp.minimum(n, nb - 1), cb))
    prev = lambda w, cb: pl.BlockSpec((BLOCK, w), lambda n, cb=cb: (jnp.maximum(jnp.minimum(n, nb - 1) - 1, 0), cb))
    out_shape = (jax.ShapeDtypeStruct((T, D_MODEL), BF16), jax.ShapeDtypeStruct((T, 2 * KV_W), BF16),
                 jax.ShapeDtypeStruct((T, D_MODEL), BF16), jax.ShapeDtypeStruct((1, 128), F32))
    return pl.pallas_call(
        body, out_shape=out_shape, grid=(nb + 1,), name="attn_backward",
        in_specs=[cur(D_MODEL, 0), cur(KV_W, 0), prev(KV_W, 0), cur(KV_W, V_COL_BLOCK), prev(KV_W, V_COL_BLOCK),
                  cur(CB, CB_GA), cur(CB, CB_GA + 1), cur(D_MODEL, 0),
                  cur(128, 0), cur(128, 0), cur(128, 0), prev(128, 0), prev(128, 0), prev(128, 0),
                  pl.BlockSpec(memory_space=pltpu.SMEM)],
        out_specs=(cur(D_MODEL, 0), pl.BlockSpec((BLOCK, 2 * KV_W), lambda n: (jnp.maximum(n - 1, 0), 0)),
                   cur(D_MODEL, 0), pl.BlockSpec((1, 128), lambda n: (0, 0))),
        scratch_shapes=[pltpu.VMEM((BLOCK, 2 * KV_W), F32)],
        compiler_params=_cp("arbitrary"),
    )(qr_b, kr_b, kr_b, proj, proj, proj, proj, d_y, *tabs, *tabs, sinks)


def _rnn_backward(proj, pos_col, h_rnn, saved, d_y, conv_w, rwa, rwx, lam):
    T = proj.shape[0]
    tr = min(T, 256)
    nt = T // tr
    hb = tr // 8

    def body(x0, x1, xh0, xh1, g0, g1, pos_ref, h_ref, hh_ref, xc_ref, r_ref, i_ref, a_ref, mult_ref, dy_ref,
             cw_ref, wa_ref, wx_ref, lam_ref, db_ref, dcw_ref, dcb_ref, dwa_ref, dwx_ref, dba_ref, dbx_ref, dlam_ref,
             xbuf, hbuf, dbuf, gbuf, lbuf, mu_carry, dxc_head):
        step = pl.program_id(0)
        first_tile = step == nt - 1

        @pl.when(step == 0)
        def _():
            mu_carry[...] = jnp.zeros_like(mu_carry)
            dxc_head[...] = jnp.zeros_like(dxc_head)
            for ref in (dcw_ref, dcb_ref, dwa_ref, dwx_ref, dba_ref, dbx_ref, dlam_ref):
                ref[...] = jnp.zeros_like(ref)

        xr = jnp.concatenate([x0[...], x1[...]], axis=1)
        tail = jnp.where(first_tile, 0.0, jnp.concatenate([xh0[...], xh1[...]], axis=1))
        lam_v = lam_ref[...]
        sp = _softplus(-lam_v)
        reset = pos_ref[...] == 0
        cw = cw_ref[...]
        xbuf[0:8, :] = tail
        xbuf[8:tr + 8, :] = xr
        g = jnp.concatenate([g0[...], g1[...]], axis=1)
        sg = _sigmoid(g)
        dy = dy_ref[...]
        h = h_ref[...]
        db_ref[:, D_MODEL:2 * D_MODEL] = (dy * h * (sg * (1.0 + g * (1.0 - sg)))).astype(BF16)
        gbuf[...] = dy * (g * sg)
        top = _scan_backward(a_ref, gbuf, lbuf, mu_carry[0:1, :], tr)
        mu_carry[...] = jnp.broadcast_to(top, mu_carry.shape)
        hbuf[0:8, :] = jnp.where(first_tile, 0.0, hh_ref[...])
        hbuf[8:tr + 8, :] = h
        live = jnp.logical_not(reset)
        dbuf[tr:tr + 8, :] = dxc_head[...]
        for j in range(RNN_BLOCKS):
            sl = slice(RNN_BW * j, RNN_BW * (j + 1))
            lam_t, h_prev = lbuf[:, sl], hbuf[pl.ds(7, tr), sl]
            xc, r, i, a, mult = xc_ref[:, sl], r_ref[:, sl], i_ref[:, sl], a_ref[:, sl], mult_ref[:, sl]
            d_a = jnp.where(live, lam_t * h_prev, 0.0)
            d_mult = jnp.where(live, lam_t * (i * xc), 0.0)
            d_ixc = lam_t * mult
            d_i = d_ixc * xc
            d_log_a = d_a * a - d_mult * (a * a / mult)
            d_za = d_log_a * (-LRU_C * sp[:, sl]) * (r * (1.0 - r))
            d_zx = d_i * (i * (1.0 - i))
            dlam_ref[:, sl] += jnp.sum(d_log_a * r, axis=0, keepdims=True) * (LRU_C * _sigmoid(-lam_v[:, sl]))
            dba_ref[:, sl] += jnp.sum(d_za, axis=0, keepdims=True)
            dbx_ref[:, sl] += jnp.sum(d_zx, axis=0, keepdims=True)
            xcb, dzab, dzxb = xc.astype(BF16), d_za.astype(BF16), d_zx.astype(BF16)
            dwa_ref[j] += _dot_tn(xcb, dzab)
            dwx_ref[j] += _dot_tn(xcb, dzxb)
            d_xc = d_ixc * i + (_dot_nt(dzab, wa_ref[j]) + _dot_nt(dzxb, wx_ref[j]))
            dcb_ref[:, sl] += jnp.sum(d_xc, axis=0, keepdims=True)
            for k in range(CONV_W):
                tap = xr[:, sl] if k == CONV_W - 1 else xbuf[pl.ds(8 - (CONV_W - 1 - k), tr), sl]
                dcw_ref[k:k + 1, sl] += jnp.sum(d_xc * tap, axis=0, keepdims=True)
            dbuf[0:tr, sl] = d_xc
            d_xr = d_xc * cw[CONV_W - 1:CONV_W, sl]
            for k in range(CONV_W - 1):
                d_xr = d_xr + dbuf[pl.ds(CONV_W - 1 - k, tr), sl] * cw[k:k + 1, sl]
            dxc_head[:, sl] = d_xc[0:8, :]
            db_ref[:, sl] = d_xr.astype(BF16)

    rev = lambda s: nt - 1 - s
    blk = lambda cb: pl.BlockSpec((tr, CB), lambda s, cb=cb: (rev(s), cb))
    halo = lambda w, cb: pl.BlockSpec((8, w), lambda s, cb=cb: (jnp.maximum(rev(s) * hb - 1, 0), cb))
    tok = lambda w: pl.BlockSpec((tr, w), lambda s: (rev(s), 0))
    row = lambda w: pl.BlockSpec((1, w), lambda s: (0, 0))
    full3 = pl.BlockSpec((RNN_BLOCKS, RNN_BW, RNN_BW), lambda s: (0, 0, 0))
    cwspec = pl.BlockSpec((CONV_W, D_MODEL), lambda s: (0, 0))
    vec = jax.ShapeDtypeStruct((1, D_MODEL), F32)
    gate_w = jax.ShapeDtypeStruct((RNN_BLOCKS, RNN_BW, RNN_BW), F32)
    out_shape = (jax.ShapeDtypeStruct((T, 2 * D_MODEL), BF16), jax.ShapeDtypeStruct((CONV_W, D_MODEL), F32), vec,
                 gate_w, gate_w, vec, vec, vec)
    big = lambda: pltpu.VMEM((tr, D_MODEL), F32)
    ext = lambda: pltpu.VMEM((tr + 8, D_MODEL), F32)
    return pl.pallas_call(
        body, out_shape=out_shape, grid=(nt,), name="rnn_backward",
        in_specs=[blk(CB_XR), blk(CB_XR + 1), halo(CB, CB_XR), halo(CB, CB_XR + 1), blk(CB_GR), blk(CB_GR + 1),
                  pl.BlockSpec((tr, 1), lambda s: (rev(s), 0)), tok(D_MODEL), halo(D_MODEL, 0)] + [tok(D_MODEL)] * 6
        + [cwspec, full3, full3, row(D_MODEL)],
        out_specs=(tok(2 * D_MODEL), cwspec, row(D_MODEL), full3, full3, row(D_MODEL), row(D_MODEL), row(D_MODEL)),
        scratch_shapes=[ext(), ext(), ext(), big(), big(), pltpu.VMEM((8, D_MODEL), F32), pltpu.VMEM((8, D_MODEL), F32)],
        compiler_params=_cp("arbitrary"),
    )(proj, proj, proj, proj, proj, proj, pos_col, h_rnn, h_rnn, *saved, d_y, conv_w, rwa, rwx, lam)


def _input_backward(pieces, w_in, x, dx2, mod_row, norm_g):
    T = x.shape[0]
    tm = min(T, 512)
    n = len(pieces)

    def body(*refs):
        d_refs = refs[:n]
        w_ref, x_ref, dx2_ref, mod_ref, g_ref, gx_ref, dshift_ref, dscale_ref, dg_ref = refs[n:]
        i = pl.program_id(0)
        dh = None
        for d_ref, (_, start, count) in zip(d_refs, pieces):
            part = _dot_nt(d_ref[...], w_ref[:, start * CB:(start + count) * CB])
            dh = part if dh is None else dh + part

        @pl.when(i == 0)
        def _():
            dshift_ref[...] = jnp.zeros_like(dshift_ref)
            dscale_ref[...] = jnp.zeros_like(dscale_ref)
            dg_ref[...] = jnp.zeros_like(dg_ref)

        xf = x_ref[...]
        r1 = _rms(xf)
        xn = xf * r1
        gn = g_ref[...]
        s1 = 1.0 + mod_ref[:, D_MODEL:2 * D_MODEL]
        dshift_ref[...] += jnp.sum(dh, axis=0, keepdims=True)
        dscale_ref[...] += jnp.sum(dh * (xn * gn), axis=0, keepdims=True)
        dg_ref[...] += jnp.sum(dh * s1 * xn, axis=0, keepdims=True)
        dxn = dh * s1 * gn
        gx_ref[...] = dx2_ref[...] + r1 * (dxn - xn * jnp.mean(dxn * xn, axis=-1, keepdims=True))

    tok = lambda w: pl.BlockSpec((tm, w), lambda i: (i, 0))
    row = lambda w: pl.BlockSpec((1, w), lambda i: (0, 0))
    vec = jax.ShapeDtypeStruct((1, D_MODEL), F32)
    return pl.pallas_call(
        body, out_shape=(jax.ShapeDtypeStruct((T, D_MODEL), F32), vec, vec, vec), grid=(T // tm,), name="input_backward",
        in_specs=[tok(c * CB) for _, _, c in pieces]
        + [pl.BlockSpec((D_MODEL, IN_W), lambda i: (0, 0), pipeline_mode=pl.Buffered(1)), tok(D_MODEL), tok(D_MODEL),
           row(ADA_W), row(D_MODEL)],
        out_specs=(tok(D_MODEL), row(D_MODEL), row(D_MODEL), row(D_MODEL)),
        compiler_params=_cp("arbitrary"),
    )(*[p[0] for p in pieces], w_in, x, dx2, mod_row, norm_g)


def _weight_grad(a, pieces, tag, a_is_transposed=False):
    M, T = a.shape if a_is_transposed else a.shape[::-1]
    n_blocks = sum(count for _, _, count in pieces)
    n = len(pieces)
    contract = _dot if a_is_transposed else _dot_tn

    def body(*refs):
        a_ref, b_refs, o_ref = refs[0], refs[1:1 + n], refs[-1]
        j = pl.program_id(0)
        for b_ref, (_, start, count) in zip(b_refs, pieces):
            @pl.when((j >= start) & (j < start + count))
            def _(b_ref=b_ref):
                o_ref[...] = contract(a_ref[...], b_ref[...])

    def piece_spec(start, count):
        return pl.BlockSpec((T, CB), lambda j: (0, jnp.clip(j - start, 0, count - 1)))

    return pl.pallas_call(
        body, out_shape=jax.ShapeDtypeStruct((M, n_blocks * CB), F32), grid=(n_blocks,), name=f"weight_grad_{tag}",
        in_specs=[pl.BlockSpec(a.shape, lambda j: (0, 0), pipeline_mode=pl.Buffered(1))] + [piece_spec(s, c) for _, s, c in pieces],
        out_specs=pl.BlockSpec((M, CB), lambda j: (0, j)), compiler_params=_cp("arbitrary"),
    )(a, *[p[0] for p in pieces])


def _adamw(w, g, m, v):
    m = ADAM_B1 * m + (1.0 - ADAM_B1) * g
    v = ADAM_B2 * v + (1.0 - ADAM_B2) * (g * g)
    m_hat = m / (1.0 - ADAM_B1 ** ADAM_STEP)
    v_hat = v / (1.0 - ADAM_B2 ** ADAM_STEP)
    delta = -ADAM_LR * (m_hat / (jnp.sqrt(v_hat) + ADAM_EPS) + ADAM_WD * w)
    return delta, m, v


def _sum_landed(kind, owns, lands, where, tag):
    n = len(owns)
    land = lands[0]
    if kind == "in":
        R, C = land.shape[1:]
        tr = 256
        grid = (R // tr,)
        own_spec = pl.BlockSpec((tr, C), lambda i, w: (i, w[0]))
        land_spec = pl.BlockSpec((3, tr, C), lambda i, w: (0, i, 0))
        out_spec = pl.BlockSpec((1, tr, C), lambda i, w: (w[1], i, 0))
        out_shape = (2, R, C)
        pick = lambda ref: ref[...]
    elif kind == "sq":
        R, C = land.shape[1:]
        grid = (1,)
        own_spec = pl.BlockSpec((1, R, C), lambda i, w: (w[0], 0, 0))
        land_spec = pl.BlockSpec((3, R, C), lambda i, w: (0, 0, 0))
        out_spec = pl.BlockSpec((1, R, C), lambda i, w: (w[1], 0, 0))
        out_shape = (2, R, C)
        pick = lambda ref: ref[0]
    else:
        B, R, C = land.shape[1:]
        grid = (1,)
        own_spec = pl.BlockSpec((B, 1, R, C), lambda i, w: (0, w[0], 0, 0))
        land_spec = pl.BlockSpec((3, B, R, C), lambda i, w: (0, 0, 0, 0))
        out_spec = pl.BlockSpec((B, 1, R, C), lambda i, w: (0, w[1], 0, 0))
        out_shape = (B, 2, R, C)
        pick = lambda ref: ref[:, 0]

    def body(w_ref, *refs):
        for k in range(n):
            own_ref, l_ref, o_ref = refs[k], refs[n + k], refs[2 * n + k]
            total = ((pick(own_ref) + l_ref[0].astype(F32)) + l_ref[1].astype(F32)) + l_ref[2].astype(F32)
            if kind == "rg":
                o_ref[:, 0] = total
            else:
                o_ref[0] = total

    grid_spec = pltpu.PrefetchScalarGridSpec(num_scalar_prefetch=1, grid=grid, in_specs=[own_spec] * n + [land_spec] * n,
                                             out_specs=(out_spec,) * n)
    return list(pl.pallas_call(
        body, out_shape=(jax.ShapeDtypeStruct(out_shape, F32),) * n, grid_spec=grid_spec, name=f"sum_landed_{tag}",
        compiler_params=_cp("parallel"),
    )(where, *owns, *lands))


def _adamw_shard(gs, ws, ms, vs, tag):
    n = len(ws)
    R, C = ws[0].shape
    tr = min(R, 256 if n == 1 else 64)

    def body(*refs):
        for k in range(n):
            g = refs[k][...]
            d, nm, nv = _adamw(refs[n + k][...], g, refs[2 * n + k][...], refs[3 * n + k][...])
            out = refs[4 * n + 4 * k:4 * n + 4 * k + 4]
            out[0][...] = g
            out[1][...] = d
            out[2][...] = nm
            out[3][...] = nv

    spec = pl.BlockSpec((tr, C), lambda i: (i, 0))
    sds = jax.ShapeDtypeStruct((R, C), F32)
    outs = pl.pallas_call(
        body, out_shape=(sds,) * (4 * n), grid=(R // tr,), name=f"adamw_{tag}",
        in_specs=[spec] * (4 * n), out_specs=(spec,) * (4 * n), compiler_params=_cp("parallel"),
    )(*gs, *ws, *ms, *vs)
    return [outs[4 * k:4 * k + 4] for k in range(n)]


def _adamw_w_ada(c_t, dmod_cols, w, m, v):
    R, C = w.shape

    def body(ct_ref, dm_ref, w_ref, m_ref, v_ref, g_ref, d_ref, nm_ref, nv_ref):
        g = _dot(ct_ref[...].astype(BF16), dm_ref[...].astype(BF16))
        d, nm, nv = _adamw(w_ref[...], g, m_ref[...], v_ref[...])
        g_ref[...] = g
        d_ref[...] = d
        nm_ref[...] = nm
        nv_ref[...] = nv

    tr = 256
    spec = pl.BlockSpec((tr, C), lambda i: (i, 0))
    sds = jax.ShapeDtypeStruct((R, C), F32)
    return pl.pallas_call(
        body, out_shape=(sds,) * 4, grid=(R // tr,), name="adamw_w_ada",
        in_specs=[pl.BlockSpec((tr, 128), lambda i: (i, 0)), pl.BlockSpec((128, C), lambda i: (0, 0))] + [spec] * 3,
        out_specs=(spec,) * 4, compiler_params=_cp("parallel"),
    )(c_t, dmod_cols, w, m, v)


def _adamw_small(small_all, ws, ms, vs):
    def body(s_ref, w_ref, m_ref, v_ref, g_ref, d_ref, nm_ref, nv_ref):
        g = s_ref[0]
        for b in range(1, N_DEV):
            g = g + s_ref[b]
        d, nm, nv = _adamw(w_ref[...], g, m_ref[...], v_ref[...])
        g_ref[...] = g
        d_ref[...] = d
        nm_ref[...] = nm
        nv_ref[...] = nv

    sds = jax.ShapeDtypeStruct((SMALL_ROWS, D_MODEL), F32)
    return pl.pallas_call(
        body, out_shape=(sds,) * 4, name="adamw_small", in_specs=[VMEM_SPEC] * 4, out_specs=(VMEM_SPEC,) * 4,
        compiler_params=pltpu.CompilerParams(vmem_limit_bytes=VMEM_LIMIT_V7X),
    )(small_all, ws, ms, vs)


ROW_MOD, ROW_NORM_G, ROW_CONV_B, ROW_BA, ROW_BX, ROW_LAM, ROW_FINAL_G, ROW_SINKS, ROW_CONV_W, ROW_LOSS = 0, 3, 4, 5, 6, 7, 8, 9, 10, 14


def _pack_small(b_ada, norm_g, conv_b, ba, bx, lam, final_g, sinks, conv_w_full, loss_row=None):
    lane_pad = lambda a: jnp.pad(a.reshape(1, -1), ((0, 0), (0, D_MODEL - a.size)))
    rows = [b_ada.reshape(3, D_MODEL), norm_g, conv_b, ba, bx, lam, final_g.reshape(1, D_MODEL), lane_pad(sinks), conv_w_full,
            jnp.zeros((1, D_MODEL), F32) if loss_row is None else lane_pad(loss_row),
            jnp.zeros((SMALL_ROWS - ROW_LOSS - 1, D_MODEL), F32)]
    return jnp.concatenate([r.astype(F32) for r in rows], axis=0)


def kernel(x, c, positions, w_ada, b_ada, norm_g, w_in, attn_sinks, conv_w, conv_b, rg_wa, rg_ba, rg_wx, rg_bx, rg_lambda, w_attn_proj, w_rnn_proj, w_out, final_g, loss_target, m_w_ada, m_b_ada, m_norm_g, m_w_in, m_attn_sinks, m_conv_w, m_conv_b, m_rg_wa, m_rg_ba, m_rg_wx, m_rg_bx, m_rg_lambda, m_w_attn_proj, m_w_rnn_proj, m_w_out, m_final_g, v_w_ada, v_b_ada, v_norm_g, v_w_in, v_attn_sinks, v_conv_w, v_conv_b, v_rg_wa, v_rg_ba, v_rg_wx, v_rg_bx, v_rg_lambda, v_w_attn_proj, v_w_rnn_proj, v_w_out, v_final_g):
    T = x.shape[1]
    my_chip = lax.axis_index("x") * 2 + lax.axis_index("y")
    my_dev = my_chip * 2 + lax.axis_index("c")
    x2d, tgt = x[0], loss_target[0]
    pos_col = positions.reshape(T, 1)

    chip_idx = my_chip.reshape(1).astype(jnp.int32)
    c_idx = lax.axis_index("c").reshape(1).astype(jnp.int32)
    sq_place = ((D_MODEL, D_MODEL), (SHARD_ROWS, D_MODEL), lambda chip: (chip, 0))
    rg_place = ((RNN_BLOCKS, RNN_BW, RNN_BW), (RNN_BLOCKS, SHARD_RG, RNN_BW), lambda chip: (0, chip, 0))
    in_place = ((D_MODEL, IN_W), (D_MODEL, SHARD_IN), lambda chip: (0, chip))
    placed = _cast_place([w_in[0], w_attn_proj[0], w_rnn_proj[0], w_out[0], rg_wa[0], rg_wx[0]], chip_idx,
                         [in_place, sq_place, sq_place, sq_place, rg_place, rg_place])
    cw_chips, c_all, mod_chips = _gather_mod(c.reshape(1, 1, D_MODEL), w_ada[0], conv_w[0])
    g_ssems, g_rsems, fulls, g_token = _gather_start([p.reshape(s) for p, s in zip(placed, FULL_SHAPES)], mod_chips)
    conv_w_f = jnp.transpose(cw_chips, (1, 0, 2)).reshape(CONV_W, D_MODEL)
    mod_all = jnp.transpose(mod_chips, (1, 0, 2)).reshape(N_DEV, ADA_W) + b_ada
    mod_row = lax.dynamic_slice_in_dim(mod_all, my_dev, 1, axis=0) + g_token[0:1, 0:1]

    h, h_t, tabs = _prenorm(x2d, mod_row, norm_g, pos_col)
    w_in_v = fulls[0]
    proj = _in_projection(h, w_in_v.reshape(D_MODEL, IN_W), chip_idx, None, "own")
    for ks, tag in (((0, 1), "neighbours"), ((2,), "diagonal")):
        for k in ks:
            w_in_v = _gather_wait(g_ssems[k], g_rsems[k], [w_in_v], [0], proj, f"w_in_{k}")[0]
        w_in_v = _forward_halves([w_in_v], [(0, 0, k) for k in ks], f"w_in_{tag}")[0]
        from_chips = jnp.concatenate([chip_idx ^ (CHIP_MASKS[k] >> 1) for k in ks]).astype(jnp.int32)
        proj = _in_projection(h, w_in_v.reshape(D_MODEL, IN_W), from_chips, proj, f"from_{tag}")
    w_in_f = w_in_v.reshape(D_MODEL, IN_W)
    rest = _gather_wait(g_ssems[3], g_rsems[3], list(fulls[1:]), [1, 2, 3, 4, 5], proj, "rest")
    rest = _forward_halves(rest, [(idx - 1, idx, k) for idx in range(1, N_BIG) for k in range(3)], "rest")
    wap_f, wrp_f, wo_f = (g.reshape(D_MODEL, D_MODEL) for g in rest[0:3])
    rwa_f, rwx_f = (g.reshape(RNN_BLOCKS, RNN_BW, RNN_BW) for g in rest[3:5])
    y_attn, qr_b, kr_b = _attn_forward(proj, tabs, attn_sinks)
    y_rnn, h_rnn, *rnn_saved = _rnn_forward(proj, pos_col, conv_w_f, conv_b, rwa_f, rwx_f, rg_ba, rg_bx, rg_lambda)
    (dx2, merged, d_o, d_pa, d_pr, d_ya, d_yr, d_c, d_final_g, d_gate, loss_vec) = _merge_and_head(
        x2d, tgt, y_attn, y_rnn, proj, wap_f, wrp_f, wo_f, mod_row, final_g.reshape(1, D_MODEL))

    sq = (N_CHIPS, 2, SHARD_ROWS // 2, D_MODEL)
    rg = (RNN_BLOCKS, N_CHIPS, 2, SHARD_RG // 2, RNN_BW)
    rg_flat = (RNN_BLOCKS * N_CHIPS, 2, SHARD_RG // 2, RNN_BW)

    def chip_sum_and_start(views, axes, flat, unflat, tags_, kinds_, group):
        from_sib = _swap_halves(views, axes)
        exact, rounded = [None] * len(views), [None] * len(views)
        for shape in dict.fromkeys(flat):
            ids = [k for k, f in enumerate(flat) if f == shape]
            ex, ro = _presum([views[k].reshape(shape) for k in ids],
                             [from_sib[k].reshape(shape[:1] + shape[2:]) for k in ids], c_idx, tags_[ids[0]])
            for k, e, r in zip(ids, ex, ro):
                exact[k], rounded[k] = e.reshape(unflat[k]), r.reshape(unflat[k])
        return _exchange_start(rounded, kinds_, group), exact

    g_ap = _weight_grad(y_attn, [(d_pa, 0, 2)], "w_attn_proj")
    g_rp = _weight_grad(y_rnn, [(d_pr, 0, 2)], "w_rnn_proj")
    g_o = _weight_grad(merged, [(d_o, 0, 2)], "w_out")
    sq_half = (N_CHIPS, SHARD_ROWS // 2, D_MODEL)
    started1, own1 = chip_sum_and_start([g_ap.reshape(sq), g_rp.reshape(sq), g_o.reshape(sq)], [1, 1, 1], [sq] * 3, [sq_half] * 3,
                                  ["w_attn_proj", "w_rnn_proj", "w_out"], ["sq"] * 3, "proj")
    d_q, d_kv, d_ga, d_sinks = _attn_backward(proj, qr_b, kr_b, d_ya, tabs, attn_sinks + started1[4][0, 0])
    d_b, d_conv_w, d_conv_b, d_rwa, d_rwx, d_ba, d_bx, d_lam = _rnn_backward(
        proj, pos_col, h_rnn, rnn_saved, d_yr, conv_w_f, rwa_f, rwx_f, rg_lambda)
    pieces = [(d_q, CB_Q, 2), (d_kv, CB_KV, 1), (d_ga, CB_GA, 2), (d_b, CB_XR, 4), (d_c, CB_MA, 4)]
    g_in = _weight_grad(h_t, pieces, "w_in", a_is_transposed=True)
    started2, own2 = chip_sum_and_start(
        [g_in.reshape(2, D_MODEL // 2, IN_W), d_rwa.reshape(rg), d_rwx.reshape(rg)], [0, 2, 2],
        [(1, 2, D_MODEL // 2, IN_W), rg_flat, rg_flat],
        [(D_MODEL // 2, IN_W), (RNN_BLOCKS, N_CHIPS, SHARD_RG // 2, RNN_BW), (RNN_BLOCKS, N_CHIPS, SHARD_RG // 2, RNN_BW)],
        ["w_in", "rg_wa", "rg_wx"], ["in", "rg", "rg"], "in")
    grad_x, d_shift, d_scale, d_norm_g = _input_backward(pieces, w_in_f, x2d, dx2, mod_row + started2[4][0, 0], norm_g)

    d_mod = jnp.concatenate([d_shift, d_scale, d_gate], axis=1)
    small = _pack_small(d_mod, d_norm_g, d_conv_b, d_ba, d_bx, d_lam, d_final_g, d_sinks[:, :N_HEADS], d_conv_w, loss_vec)
    small_all = _gather_small(small)
    _, lands1 = _exchange_wait(*started1[:4], grad_x, "proj")
    _, lands2 = _exchange_wait(*started2[:4], grad_x, "in")
    tags = ["w_in", "w_attn_proj", "w_rnn_proj", "w_out", "rg_wa", "rg_wx"]
    chip_sums = [own2[0]] + list(own1) + list(own2[1:])
    lands = [lands2[0]] + list(lands1) + list(lands2[1:])
    where = jnp.concatenate([chip_idx, c_idx])
    kinds = ["in", "sq", "sq", "sq", "rg", "rg"]
    groups = [[0], [1, 2, 3], [4, 5]]
    halves = [None] * 6
    for ids in groups:
        for i, half in zip(ids, _sum_landed(kinds[ids[0]], [chip_sums[i] for i in ids], [lands[i] for i in ids], where,
                                            tags[ids[0]])):
            halves[i] = half
    grads = _assemble_with_sibling(halves, [0, 0, 0, 0, 1, 1])
    shapes2d = [(D_MODEL, SHARD_IN), (SHARD_ROWS, D_MODEL), (SHARD_ROWS, D_MODEL), (SHARD_ROWS, D_MODEL),
                (RNN_BLOCKS * SHARD_RG, RNN_BW), (RNN_BLOCKS * SHARD_RG, RNN_BW)]
    big_w = [w_in, w_attn_proj, w_rnn_proj, w_out, rg_wa, rg_wx]
    big_m = [m_w_in, m_w_attn_proj, m_w_rnn_proj, m_w_out, m_rg_wa, m_rg_wx]
    big_v = [v_w_in, v_w_attn_proj, v_w_rnn_proj, v_w_out, v_rg_wa, v_rg_wx]
    res = {}
    for ids in groups:
        flat2d = lambda arrs: [arrs[i].reshape(shapes2d[i]) for i in ids]
        outs = _adamw_shard(flat2d(grads), flat2d(big_w), flat2d(big_m), flat2d(big_v), tags[ids[0]])
        for i, four in zip(ids, outs):
            res[tags[i]] = [o.reshape(big_w[i].shape) for o in four]

    dmod_all = small_all[:, ROW_MOD:ROW_MOD + 3, :].reshape(N_DEV, ADA_W)
    dmod_cols = lax.dynamic_slice_in_dim(dmod_all, my_chip * SHARD_ADA, SHARD_ADA, axis=1)
    c_t = jnp.pad(jnp.transpose(c_all.reshape(N_DEV, D_MODEL)), ((0, 0), (0, 128 - N_DEV)))
    dmod_cols = jnp.pad(dmod_cols, ((0, 128 - N_DEV), (0, 0)))
    res["w_ada"] = [o.reshape(w_ada.shape) for o in _adamw_w_ada(c_t, dmod_cols, w_ada[0], m_w_ada[0], v_w_ada[0])]

    def full_conv(a):
        return lax.dynamic_update_slice_in_dim(jnp.zeros((CONV_W, D_MODEL), F32), a[0], my_chip * (D_MODEL // N_CHIPS), axis=1)

    packed = [_pack_small(p[0], p[1], p[2], p[3], p[4], p[5], p[6], p[7], full_conv(p[8])) for p in (
        (b_ada, norm_g, conv_b, rg_ba, rg_bx, rg_lambda, final_g, attn_sinks, conv_w),
        (m_b_ada, m_norm_g, m_conv_b, m_rg_ba, m_rg_bx, m_rg_lambda, m_final_g, m_attn_sinks, m_conv_w),
        (v_b_ada, v_norm_g, v_conv_b, v_rg_ba, v_rg_bx, v_rg_lambda, v_final_g, v_attn_sinks, v_conv_w))]
    small_out = _adamw_small(small_all, *packed)

    def unpack(slab):
        cw = lax.dynamic_slice_in_dim(slab[ROW_CONV_W:ROW_CONV_W + CONV_W], my_chip * (D_MODEL // N_CHIPS),
                                      D_MODEL // N_CHIPS, axis=1)
        return {
            "b_ada": slab[ROW_MOD:ROW_MOD + 3].reshape(1, ADA_W), "norm_g": slab[ROW_NORM_G:ROW_NORM_G + 1],
            "conv_b": slab[ROW_CONV_B:ROW_CONV_B + 1], "rg_ba": slab[ROW_BA:ROW_BA + 1], "rg_bx": slab[ROW_BX:ROW_BX + 1],
            "rg_lambda": slab[ROW_LAM:ROW_LAM + 1], "final_g": slab[ROW_FINAL_G], "attn_sinks": slab[ROW_SINKS:ROW_SINKS + 1, :N_HEADS],
            "conv_w": cw[None],
        }

    small_res = [unpack(s) for s in small_out]
    order = ["w_ada", "b_ada", "norm_g", "w_in", "attn_sinks", "conv_w", "conv_b", "rg_wa", "rg_ba", "rg_wx", "rg_bx",
             "rg_lambda", "w_attn_proj", "w_rnn_proj", "w_out", "final_g"]
    loss = small_out[0][ROW_LOSS, 0]
    outs = [loss, grad_x[None]]
    for kind in range(4):
        for name in order:
            outs.append(res[name][kind] if name in res else small_res[kind][name])
    return tuple(outs)
```

```python
import numpy as np
import jax
import jax.numpy as jnp
from jax import lax
from jax.experimental import pallas as pl
from jax.experimental.pallas import tpu as pltpu

F32 = jnp.float32
BF16 = jnp.bfloat16

D_MODEL = 1024
N_HEADS = 16
N_KV = 4
HEAD_DIM = 64
GROUP = N_HEADS // N_KV
BLOCK = 128
KV_W = N_KV * HEAD_DIM
ROT_HALF = 8
ROPE_THETA = 500000.0
ATTN_SCALE = 0.125
RNN_BLOCKS = 4
RNN_BW = 256
CONV_W = 4
LRU_C = 8.0
NORM_EPS = 1e-6
IN_W = 6656
CB = 512
N_CB = IN_W // CB
CB_Q, CB_KV, CB_GA, CB_XR, CB_GR, CB_MA, CB_MR = 0, 2, 3, 5, 7, 9, 11
V_COL_BLOCK = 5
N_CHIPS = 4
N_DEV = 8
SHARD_IN = IN_W // N_CHIPS
SHARD_ROWS = D_MODEL // N_CHIPS
SHARD_RG = RNN_BW // N_CHIPS
ADA_W = 3 * D_MODEL
SHARD_ADA = ADA_W // N_CHIPS
SMALL_ROWS = 16

ADAM_LR = 0.001
ADAM_B1 = 0.9
ADAM_B2 = 0.999
ADAM_EPS = 1e-08
ADAM_WD = 0.01
ADAM_STEP = 10

VMEM_LIMIT_V7X = 52 * 1024 * 1024
MESH = pl.DeviceIdType.MESH
ANY = pl.BlockSpec(memory_space=pl.ANY)
VMEM_SPEC = pl.BlockSpec(memory_space=pltpu.VMEM)


def _cp(*sem):
    return pltpu.CompilerParams(dimension_semantics=sem if sem else None, vmem_limit_bytes=VMEM_LIMIT_V7X)


def _dot(a, b):
    return jnp.dot(a, b, preferred_element_type=F32)


def _dot_nt(a, b):
    return lax.dot_general(a, b, (((1,), (1,)), ((), ())), preferred_element_type=F32)


def _dot_tn(a, b):
    return lax.dot_general(a, b, (((0,), (0,)), ((), ())), preferred_element_type=F32)


def _sigmoid(z):
    return 1.0 / (1.0 + jnp.exp(-z))


def _softplus(z):
    u = jnp.exp(-jnp.abs(z))
    log1p_u = jnp.where(u < 1e-3, u * (1.0 - u * (0.5 - u * (1.0 / 3.0))), jnp.log(1.0 + u))
    return jnp.maximum(z, 0.0) + log1p_u


def _rms(xf):
    return lax.rsqrt(jnp.mean(xf * xf, axis=-1, keepdims=True) + NORM_EPS)


def _me():
    return lax.axis_index("x"), lax.axis_index("y"), lax.axis_index("c")


def _peer(mask):
    x, y, c = _me()
    fx, fy, fc = (mask >> 2) & 1, (mask >> 1) & 1, mask & 1
    return (x ^ fx if fx else x, y ^ fy if fy else y, c ^ fc if fc else c)


def _chip_of(pos):
    return pos[0] * 2 + pos[1]


SIBLING_COLLECTIVE_ID = 0
SIBLING_ONLY = pltpu.CompilerParams(collective_id=SIBLING_COLLECTIVE_ID)


def _sibling_handshake():
    barrier = pltpu.get_barrier_semaphore()
    pl.semaphore_signal(barrier, inc=1, device_id=_peer(1), device_id_type=MESH)
    pl.semaphore_wait(barrier, 1)


CHIP_MASKS = (4, 2, 6)
ALL_MASKS = (1, 2, 3, 4, 5, 6, 7)


HBM_SPEC = pl.BlockSpec(memory_space=pltpu.HBM)
SEM_SPEC = pl.BlockSpec(memory_space=pltpu.SEMAPHORE)
SPLIT_COPY = pltpu.CompilerParams(has_side_effects=pltpu.SideEffectType.DATAFLOW_SIDE_EFFECTING)
N_BIG = 6
FULL_SHAPES = (
    (2, D_MODEL // 2, IN_W),
    (N_CHIPS, 2, SHARD_ROWS // 2, D_MODEL), (N_CHIPS, 2, SHARD_ROWS // 2, D_MODEL), (N_CHIPS, 2, SHARD_ROWS // 2, D_MODEL),
    (RNN_BLOCKS, N_CHIPS, 2, SHARD_RG // 2, RNN_BW), (RNN_BLOCKS, N_CHIPS, 2, SHARD_RG // 2, RNN_BW),
)


def _slot(full, idx, chip, half):
    if idx == 0:
        return full.at[half, :, pl.ds(pl.multiple_of(chip * SHARD_IN, 128), SHARD_IN)]
    return full.at[chip, half] if idx in (1, 2, 3) else full.at[:, chip, half]


def _three_halves(full, idx):
    return full.at[pl.ds(0, 3), 0] if idx in (1, 2, 3) else full.at[:, pl.ds(0, 3), 0]


def _gather_start(fulls, after):
    def body(*refs):
        full_refs = refs[:N_BIG]
        ssems, rsems = refs[N_BIG + 1:N_BIG + 5], refs[N_BIG + 5:N_BIG + 9]
        token = refs[2 * N_BIG + 9]
        me = _me()
        my_chip = _chip_of(me)
        for idx in range(N_BIG):
            for k, mask in enumerate(CHIP_MASKS):
                pair = k if idx == 0 else 3
                mine = _slot(full_refs[idx], idx, my_chip, me[2])
                pltpu.make_async_remote_copy(src_ref=mine, dst_ref=mine, send_sem=ssems[pair], recv_sem=rsems[pair],
                                             device_id=_peer(mask), device_id_type=MESH).start()
        token[...] = jnp.zeros_like(token)

    sem = pltpu.SemaphoreType.DMA(())
    out_shape = (sem,) * 8 + tuple(pltpu.HBM(f.shape, f.dtype) for f in fulls) + (jax.ShapeDtypeStruct((8, 128), F32),)
    outs = pl.pallas_call(
        body, out_shape=out_shape, name="gather_start",
        in_specs=[HBM_SPEC] * N_BIG + [ANY], out_specs=tuple([SEM_SPEC] * 8 + [HBM_SPEC] * N_BIG + [VMEM_SPEC]),
        input_output_aliases={i: 8 + i for i in range(N_BIG)}, compiler_params=SPLIT_COPY,
    )(*[pltpu.with_memory_space_constraint(f, pltpu.HBM) for f in fulls], after)
    return outs[0:4], outs[4:8], outs[8:8 + N_BIG], outs[8 + N_BIG]


def _gather_wait(ssem, rsem, arrays, idxs, after, tag):
    n = len(arrays)

    def body(*refs):
        full_refs, ssem_ref, rsem_ref = refs[:n], refs[n], refs[n + 1]
        me = _me()
        for full, idx in zip(full_refs, idxs):
            region = _slot(full, 0, _chip_of(me), me[2]) if idx == 0 else _three_halves(full, idx)
            arrived = pltpu.make_async_remote_copy(
                src_ref=region, dst_ref=region, send_sem=ssem_ref, recv_sem=rsem_ref, device_id=me, device_id_type=MESH)
            arrived.wait_send()
            arrived.wait_recv()

    outs = pl.pallas_call(
        body, out_shape=tuple(pltpu.HBM(a.shape, a.dtype) for a in arrays), name=f"gather_wait_{tag}",
        in_specs=[HBM_SPEC] * n + [SEM_SPEC, SEM_SPEC, ANY], out_specs=tuple([HBM_SPEC] * n),
        input_output_aliases={i: i for i in range(n)}, compiler_params=SPLIT_COPY,
    )(*arrays, ssem, rsem, after)
    return list(outs)


def _forward_halves(arrays, items, tag):
    n, m = len(arrays), len(items)

    def body(*refs):
        outs, ssem, rsem = refs[n:2 * n], refs[2 * n], refs[2 * n + 1]
        me = _me()
        sib = _peer(1)
        _sibling_handshake()
        cps = []
        for j, (pos, idx, k) in enumerate(items):
            chip = _chip_of(_peer(CHIP_MASKS[k]))
            cp = pltpu.make_async_remote_copy(
                src_ref=_slot(outs[pos], idx, chip, me[2]), dst_ref=_slot(outs[pos], idx, chip, me[2]),
                send_sem=ssem.at[j], recv_sem=rsem.at[j], device_id=sib, device_id_type=MESH)
            cp.start()
            cps.append(cp)
        for j, (pos, idx, k) in enumerate(items):
            chip = _chip_of(_peer(CHIP_MASKS[k]))
            pltpu.make_async_remote_copy(
                src_ref=_slot(outs[pos], idx, chip, me[2]), dst_ref=_slot(outs[pos], idx, chip, 1 - me[2]),
                send_sem=ssem.at[j], recv_sem=rsem.at[j], device_id=sib, device_id_type=MESH).wait_recv()
        for cp in cps:
            cp.wait_send()

    outs = pl.pallas_call(
        body, out_shape=tuple(jax.ShapeDtypeStruct(a.shape, a.dtype) for a in arrays), name=f"forward_halves_{tag}",
        in_specs=[ANY] * n, out_specs=tuple([ANY] * n), input_output_aliases={i: i for i in range(n)},
        scratch_shapes=[pltpu.SemaphoreType.DMA((m,)), pltpu.SemaphoreType.DMA((m,))], compiler_params=SIBLING_ONLY,
    )(*arrays)
    return list(outs)


def _gather_mod(c_row, w_ada_s, conv_w_s):
    def body(c_ref, wada_ref, cw_s, cw_f, call_ref, mod_ref, wsend, wrecv, lsem, csend, crecv, msend, mrecv):
        me = _me()
        my_chip = _chip_of(me)
        my_dev = my_chip * 2 + me[2]
        sends = []
        for k, mask in enumerate(CHIP_MASKS):
            cp = pltpu.make_async_remote_copy(src_ref=cw_s, dst_ref=cw_f.at[my_chip], send_sem=wsend.at[k], recv_sem=wrecv.at[k],
                                              device_id=_peer(mask), device_id_type=MESH)
            cp.start()
            sends.append(cp)
        local = [pltpu.make_async_copy(cw_s, cw_f.at[my_chip], lsem.at[0])]
        for cp in local:
            cp.start()

        call_ref[my_dev] = c_ref[0]
        csends = []
        for k, mask in enumerate(ALL_MASKS):
            cp = pltpu.make_async_remote_copy(
                src_ref=c_ref.at[0], dst_ref=call_ref.at[my_dev],
                send_sem=csend.at[k], recv_sem=crecv.at[k], device_id=_peer(mask), device_id_type=MESH)
            cp.start()
            csends.append(cp)
        for k, mask in enumerate(ALL_MASKS):
            frm = _peer(mask)
            pltpu.make_async_remote_copy(
                src_ref=c_ref.at[0], dst_ref=call_ref.at[_chip_of(frm) * 2 + frm[2]],
                send_sem=csend.at[k], recv_sem=crecv.at[k], device_id=frm, device_id_type=MESH).wait_recv()
        for cp in csends:
            cp.wait_send()

        c_all = call_ref[...].reshape(N_DEV, D_MODEL).astype(BF16)
        mod_ref[my_chip] = _dot(c_all, wada_ref[...].astype(BF16))
        msends = []
        for k, mask in enumerate(CHIP_MASKS):
            cp = pltpu.make_async_remote_copy(
                src_ref=mod_ref.at[my_chip], dst_ref=mod_ref.at[my_chip],
                send_sem=msend.at[k], recv_sem=mrecv.at[k], device_id=_peer(mask), device_id_type=MESH)
            cp.start()
            msends.append(cp)
        for k, mask in enumerate(CHIP_MASKS):
            frm = _peer(mask)
            pltpu.make_async_remote_copy(
                src_ref=mod_ref.at[my_chip], dst_ref=mod_ref.at[_chip_of(frm)],
                send_sem=msend.at[k], recv_sem=mrecv.at[k], device_id=frm, device_id_type=MESH).wait_recv()
        for cp in msends:
            cp.wait_send()

        for k, mask in enumerate(CHIP_MASKS):
            frm = _peer(mask)
            pltpu.make_async_remote_copy(src_ref=cw_s, dst_ref=cw_f.at[_chip_of(frm)], send_sem=wsend.at[k], recv_sem=wrecv.at[k],
                                         device_id=frm, device_id_type=MESH).wait_recv()
        for cp in sends:
            cp.wait_send()
        for cp in local:
            cp.wait()

    out_shape = (
        jax.ShapeDtypeStruct((N_CHIPS, CONV_W, D_MODEL // N_CHIPS), F32),
        jax.ShapeDtypeStruct((N_DEV, 1, D_MODEL), F32),
        jax.ShapeDtypeStruct((N_CHIPS, N_DEV, SHARD_ADA), F32),
    )
    return pl.pallas_call(
        body, out_shape=out_shape, name="gather_mod",
        in_specs=[VMEM_SPEC, VMEM_SPEC, ANY], out_specs=(ANY, VMEM_SPEC, VMEM_SPEC),
        scratch_shapes=[
            pltpu.SemaphoreType.DMA((3,)), pltpu.SemaphoreType.DMA((3,)), pltpu.SemaphoreType.DMA((1,)),
            pltpu.SemaphoreType.DMA((7,)), pltpu.SemaphoreType.DMA((7,)),
            pltpu.SemaphoreType.DMA((3,)), pltpu.SemaphoreType.DMA((3,)),
        ],
        compiler_params=pltpu.CompilerParams(vmem_limit_bytes=VMEM_LIMIT_V7X),
    )(c_row, w_ada_s, conv_w_s)


def _cast_place(shards, chip_idx, places):
    n = len(shards)

    def body(chip_ref, *refs):
        for s_ref, o_ref in zip(refs[:n], refs[n:]):
            o_ref[...] = s_ref[...].astype(BF16)

    grid_spec = pltpu.PrefetchScalarGridSpec(
        num_scalar_prefetch=1, grid=(1,),
        in_specs=[pl.BlockSpec(s.shape, lambda i, chip_ref, nd=s.ndim: (0,) * nd) for s in shards],
        out_specs=tuple(pl.BlockSpec(block, lambda i, chip_ref, im=im: im(chip_ref[0])) for _, block, im in places))
    return pl.pallas_call(
        body, out_shape=tuple(jax.ShapeDtypeStruct(full, BF16) for full, _, _ in places), grid_spec=grid_spec,
        name="cast_place", compiler_params=_cp("arbitrary"),
    )(chip_idx, *shards)


def _shard_of(ref, kind, chip):
    if kind == "in":
        return ref.at[:, pl.ds(pl.multiple_of(chip * SHARD_IN, 128), SHARD_IN)]
    return ref.at[chip] if kind == "sq" else ref.at[:, chip]


def _land_shape(src, kind):
    if kind == "in":
        return (3, src.shape[0], SHARD_IN)
    return (3,) + src.shape[1:] if kind == "sq" else (3, src.shape[0]) + src.shape[2:]


def _exchange_start(srcs, kinds, tag):
    n = len(srcs)
    lands = [pltpu.with_memory_space_constraint(lax.empty(_land_shape(s, k), s.dtype), pltpu.HBM) for s, k in zip(srcs, kinds)]

    def body(*refs):
        src_refs, land_refs = refs[:n], refs[n:2 * n]
        ssems, rsems = refs[2 * n:3 * n], refs[3 * n:4 * n]
        token = refs[6 * n]
        for i in range(n):
            for k, mask in enumerate(CHIP_MASKS):
                to = _peer(mask)
                pltpu.make_async_remote_copy(
                    src_ref=_shard_of(src_refs[i], kinds[i], _chip_of(to)), dst_ref=land_refs[i].at[k],
                    send_sem=ssems[i], recv_sem=rsems[i], device_id=to, device_id_type=MESH).start()
        token[...] = jnp.zeros_like(token)

    sem = pltpu.SemaphoreType.DMA(())
    out_shape = ((sem,) * (2 * n) + tuple(pltpu.HBM(s.shape, s.dtype) for s in srcs)
                 + tuple(pltpu.HBM(l.shape, l.dtype) for l in lands) + (jax.ShapeDtypeStruct((8, 128), F32),))
    outs = pl.pallas_call(
        body, out_shape=out_shape, name=f"exchange_start_{tag}",
        in_specs=[HBM_SPEC] * (2 * n), out_specs=tuple([SEM_SPEC] * (2 * n) + [HBM_SPEC] * (2 * n) + [VMEM_SPEC]),
        input_output_aliases={i: 2 * n + i for i in range(2 * n)},
        compiler_params=pltpu.CompilerParams(has_side_effects=pltpu.SideEffectType.DATAFLOW_SIDE_EFFECTING),
    )(*[pltpu.with_memory_space_constraint(s, pltpu.HBM) for s in srcs], *lands)
    return outs[:n], outs[n:2 * n], outs[2 * n:3 * n], outs[3 * n:4 * n], outs[4 * n]


def _exchange_wait(ssems, rsems, srcs, lands, after, tag):
    n = len(srcs)

    def body(*refs):
        land_refs = refs[n:2 * n]
        ssem_refs, rsem_refs = refs[2 * n:3 * n], refs[3 * n:4 * n]
        for i in range(n):
            all_three = pltpu.make_async_remote_copy(
                src_ref=land_refs[i], dst_ref=land_refs[i], send_sem=ssem_refs[i], recv_sem=rsem_refs[i],
                device_id=_me(), device_id_type=MESH)
            all_three.wait_send()
            all_three.wait_recv()

    outs = pl.pallas_call(
        body, out_shape=tuple(pltpu.HBM(a.shape, a.dtype) for a in list(srcs) + list(lands)), name=f"exchange_wait_{tag}",
        in_specs=[HBM_SPEC] * (2 * n) + [SEM_SPEC] * (2 * n) + [ANY], out_specs=tuple([HBM_SPEC] * (2 * n)),
        input_output_aliases={i: i for i in range(2 * n)},
        compiler_params=pltpu.CompilerParams(has_side_effects=pltpu.SideEffectType.DATAFLOW_SIDE_EFFECTING),
    )(*srcs, *lands, *ssems, *rsems, after)
    return outs[:n], outs[n:]


def _gather_small(small):
    def body(small_ref, small_all, ssend, srecv):
        me = _me()
        my_dev = _chip_of(me) * 2 + me[2]
        small_all[my_dev] = small_ref[...]
        ssends = []
        for k, mask in enumerate(ALL_MASKS):
            cp = pltpu.make_async_remote_copy(
                src_ref=small_ref, dst_ref=small_all.at[my_dev],
                send_sem=ssend.at[k], recv_sem=srecv.at[k], device_id=_peer(mask), device_id_type=MESH)
            cp.start()
            ssends.append(cp)
        for k, mask in enumerate(ALL_MASKS):
            frm = _peer(mask)
            pltpu.make_async_remote_copy(
                src_ref=small_ref, dst_ref=small_all.at[_chip_of(frm) * 2 + frm[2]],
                send_sem=ssend.at[k], recv_sem=srecv.at[k], device_id=frm, device_id_type=MESH).wait_recv()
        for cp in ssends:
            cp.wait_send()

    return pl.pallas_call(
        body, out_shape=jax.ShapeDtypeStruct((N_DEV, SMALL_ROWS, D_MODEL), F32), name="gather_small",
        in_specs=[VMEM_SPEC], out_specs=VMEM_SPEC,
        scratch_shapes=[pltpu.SemaphoreType.DMA((7,)), pltpu.SemaphoreType.DMA((7,))],
    )(small)


def _half_of(ref, axis, half):
    return ref.at[(slice(None),) * axis + (half,)]


def _swap_halves(parts, axes):
    n = len(parts)

    def body(*refs):
        ins, outs, ssem, rsem = refs[:n], refs[n:2 * n], refs[2 * n], refs[2 * n + 1]
        c = lax.axis_index("c")
        _sibling_handshake()
        cps = [pltpu.make_async_remote_copy(src_ref=_half_of(ins[i], axes[i], 1 - c), dst_ref=outs[i], send_sem=ssem.at[i],
                                            recv_sem=rsem.at[i], device_id=_peer(1), device_id_type=MESH) for i in range(n)]
        for cp in cps:
            cp.start()
        for cp in cps:
            cp.wait()

    shapes = [p.shape[:a] + p.shape[a + 1:] for p, a in zip(parts, axes)]
    return pl.pallas_call(
        body, out_shape=tuple(jax.ShapeDtypeStruct(s, p.dtype) for s, p in zip(shapes, parts)), name="swap_halves",
        in_specs=[ANY] * n, out_specs=tuple([ANY] * n),
        scratch_shapes=[pltpu.SemaphoreType.DMA((n,)), pltpu.SemaphoreType.DMA((n,))], compiler_params=SIBLING_ONLY,
    )(*parts)


def _presum(mines, sibs, c_idx, tag):
    n = len(mines)
    S, _, R, C = mines[0].shape
    tr = min(R, 256)
    tc = SHARD_IN if C % SHARD_IN == 0 else (C // 2 if n > 1 and C % 256 == 0 else C)

    def body(c_ref, *refs):
        for k in range(n):
            total = refs[k][:, 0] + refs[n + k][...]
            refs[2 * n + k][...] = total
            refs[3 * n + k][...] = total.astype(BF16)

    out_spec = pl.BlockSpec((S, tr, tc), lambda i, j, c_ref: (0, i, j))
    grid_spec = pltpu.PrefetchScalarGridSpec(
        num_scalar_prefetch=1, grid=(R // tr, C // tc),
        in_specs=[pl.BlockSpec((S, 1, tr, tc), lambda i, j, c_ref: (0, c_ref[0], i, j))] * n + [out_spec] * n,
        out_specs=(out_spec,) * (2 * n))
    outs = pl.pallas_call(
        body, out_shape=(jax.ShapeDtypeStruct((S, R, C), F32),) * n + (jax.ShapeDtypeStruct((S, R, C), BF16),) * n,
        grid_spec=grid_spec, name=f"presum_{tag}", compiler_params=_cp("parallel", "parallel"),
    )(c_idx, *mines, *sibs)
    return list(outs[:n]), list(outs[n:])


def _assemble_with_sibling(parts, axes):
    n = len(parts)

    def body(*refs):
        outs, ssem, rsem = refs[n:2 * n], refs[2 * n], refs[2 * n + 1]
        c = lax.axis_index("c")
        _sibling_handshake()
        cps = [pltpu.make_async_remote_copy(
            src_ref=_half_of(outs[i], axes[i], c), dst_ref=_half_of(outs[i], axes[i], c), send_sem=ssem.at[i],
            recv_sem=rsem.at[i], device_id=_peer(1), device_id_type=MESH) for i in range(n)]
        for cp in cps:
            cp.start()
        for i in range(n):
            pltpu.make_async_remote_copy(
                src_ref=_half_of(outs[i], axes[i], c), dst_ref=_half_of(outs[i], axes[i], 1 - c), send_sem=ssem.at[i],
                recv_sem=rsem.at[i], device_id=_peer(1), device_id_type=MESH).wait_recv()
        for cp in cps:
            cp.wait_send()

    return pl.pallas_call(
        body, out_shape=tuple(jax.ShapeDtypeStruct(p.shape, p.dtype) for p in parts), name="assemble_with_sibling",
        in_specs=[ANY] * n, out_specs=tuple([ANY] * n), input_output_aliases={i: i for i in range(n)},
        scratch_shapes=[pltpu.SemaphoreType.DMA((n,)), pltpu.SemaphoreType.DMA((n,))], compiler_params=SIBLING_ONLY,
    )(*parts)


def _rope_lane_frequencies():
    inv = np.float32(ROPE_THETA) ** (-(np.arange(0, 2 * ROT_HALF, 2, dtype=np.float32)) / np.float32(2 * ROT_HALF))
    lane = np.arange(128) % HEAD_DIM
    return jnp.asarray(np.where(lane < 2 * ROT_HALF, inv[lane % ROT_HALF], 0.0).astype(np.float32)[None, :])


def _rope_tables(pos, freq):
    ang = pos.astype(F32) * freq
    c, s = jnp.cos(ang), jnp.sin(ang)
    m = lax.broadcasted_iota(jnp.int32, ang.shape, 1) & (HEAD_DIM - 1)
    return (jnp.where(m < 2 * ROT_HALF, c, 1.0), jnp.where(m < ROT_HALF, -s, 0.0),
            jnp.where((m >= ROT_HALF) & (m < 2 * ROT_HALF), s, 0.0))


def _columns(t):
    return [t[:, i:i + 128] for i in range(0, t.shape[-1], 128)]


def _rope(t, c, sa, sb):
    return jnp.concatenate(
        [x * c + pltpu.roll(x, 128 - ROT_HALF, 1) * sa + pltpu.roll(x, ROT_HALF, 1) * sb for x in _columns(t)], axis=1)


def _unrope(d, c, sa, sb):
    return jnp.concatenate(
        [x * c + pltpu.roll(x * sa, ROT_HALF, 1) + pltpu.roll(x * sb, 128 - ROT_HALF, 1) for x in _columns(d)], axis=1)


def _prenorm(x, mod_row, norm_g, pos_col):
    T = x.shape[0]
    tm = min(T, 512)

    def body(x_ref, mod_ref, g_ref, pos_ref, f_ref, h_ref, ht_ref, c_ref, sa_ref, sb_ref):
        xf = x_ref[...]
        shift, scale = mod_ref[:, 0:D_MODEL], mod_ref[:, D_MODEL:2 * D_MODEL]
        h = (xf * _rms(xf)) * g_ref[...] * (1.0 + scale) + shift
        h_ref[...] = h.astype(BF16)
        ht_ref[...] = h.T.astype(BF16)
        c_ref[...], sa_ref[...], sb_ref[...] = _rope_tables(pos_ref[...], f_ref[...])

    tab = jax.ShapeDtypeStruct((T, 128), F32)
    tok = lambda w: pl.BlockSpec((tm, w), lambda i: (i, 0))
    row = lambda w: pl.BlockSpec((1, w), lambda i: (0, 0))
    outs = pl.pallas_call(
        body, out_shape=(jax.ShapeDtypeStruct((T, D_MODEL), BF16), jax.ShapeDtypeStruct((D_MODEL, T), BF16), tab, tab, tab),
        grid=(T // tm,), name="prenorm",
        in_specs=[tok(D_MODEL), row(ADA_W), row(D_MODEL), tok(1), row(128)],
        out_specs=(tok(D_MODEL), pl.BlockSpec((D_MODEL, tm), lambda i: (0, i)), tok(128), tok(128), tok(128)),
        compiler_params=_cp("parallel"),
    )(x, mod_row, norm_g, pos_col, _rope_lane_frequencies())
    return outs[0], outs[1], tuple(outs[2:])


def _in_projection(h, w_in, chips, into, tag):
    T = h.shape[0]
    tm, tn = min(T, 512), SHARD_IN
    k = chips.shape[0]

    def body(chip_ref, h_ref, w_ref, *rest):
        rest[-1][...] = _dot(h_ref[...], w_ref[...])

    w_spec = pl.BlockSpec((D_MODEL, tn), lambda s, i, c: (0, c[s]), **({"pipeline_mode": pl.Buffered(1)} if k == 1 else {}))
    in_specs = [pl.BlockSpec((tm, D_MODEL), lambda s, i, c: (i, 0)), w_spec]
    args = [chips, h, w_in]
    aliases = {}
    if into is not None:
        in_specs.append(ANY)
        args.append(into)
        aliases = {3: 0}
    grid_spec = pltpu.PrefetchScalarGridSpec(num_scalar_prefetch=1, grid=(k, T // tm), in_specs=in_specs,
                                             out_specs=pl.BlockSpec((tm, tn), lambda s, i, c: (i, c[s])))
    return pl.pallas_call(
        body, out_shape=jax.ShapeDtypeStruct((T, IN_W), F32), grid_spec=grid_spec, name=f"in_projection_{tag}",
        input_output_aliases=aliases, compiler_params=_cp("parallel", "parallel"),
    )(*args)


def _attn_mask(n):
    qi = lax.broadcasted_iota(jnp.int32, (GROUP * BLOCK, BLOCK), 0) & (BLOCK - 1)
    j = lax.broadcasted_iota(jnp.int32, (GROUP * BLOCK, BLOCK), 1)
    own = j <= qi
    return own, jnp.logical_not(own) & (n == 0)


def _fold(x, own):
    return jnp.where(own, x[:, BLOCK:2 * BLOCK], x[:, 0:BLOCK])


def _unfold(xf, own):
    zero = jnp.zeros_like(xf)
    return jnp.concatenate([jnp.where(own, zero, xf), jnp.where(own, xf, zero)], axis=1)


ROW_GROUP_HEAD = (0, 2, 1, 3)


def _sink_col(sink_ref, kh):
    rowg = lax.broadcasted_iota(jnp.int32, (GROUP * BLOCK, 1), 0) // BLOCK
    col = jnp.full((GROUP * BLOCK, 1), sink_ref[0, GROUP * kh + ROW_GROUP_HEAD[0]], F32)
    for g in range(1, GROUP):
        col = jnp.where(rowg == g, sink_ref[0, GROUP * kh + ROW_GROUP_HEAD[g]], col)
    return col


def _low_lanes(shape):
    return lax.broadcasted_iota(jnp.int32, shape, 1) < HEAD_DIM


def _kv_pair_operand(prev, cur, kh):
    c = 128 * (kh // 2)
    col = jnp.concatenate([prev[:, c:c + 128], cur[:, c:c + 128]], axis=0).astype(F32)
    if kh % 2 == 0:
        lo = jnp.where(_low_lanes(col.shape), col, 0.0)
        hi = pltpu.roll(lo, HEAD_DIM, 1)
    else:
        hi = jnp.where(_low_lanes(col.shape), 0.0, col)
        lo = pltpu.roll(hi, HEAD_DIM, 1)
    return jnp.concatenate([lo, hi], axis=0).astype(BF16)


def _pair_rows(x, kh):
    c = 2 * 128 * kh
    return jnp.concatenate([x[:, c:c + 128], x[:, c + 128:c + 256]], axis=0)


def _restack(big):
    return jnp.concatenate([big[:, 0:2 * BLOCK], big[:, 2 * BLOCK:4 * BLOCK]], axis=0)


def _unrestack(stacked):
    return jnp.concatenate([stacked[0:2 * BLOCK], stacked[2 * BLOCK:4 * BLOCK]], axis=1)


def _fold_pair(x2, kh):
    low = _low_lanes((2 * BLOCK, 128))
    mixed = jnp.where(low, x2[0:2 * BLOCK], x2[2 * BLOCK:4 * BLOCK])
    total = mixed + pltpu.roll(mixed, HEAD_DIM, 1)
    return jnp.where(low, total, 0.0) if kh % 2 == 0 else jnp.where(low, 0.0, total)


def _attn_scores(qr, k2, kh):
    q2 = _pair_rows(qr, kh).astype(BF16)
    return q2, _restack(_dot_nt(q2, k2))


def _attn_softmax(s, sink_col, mask):
    own, no_key = mask
    s = jnp.where(no_key, -1e30, _fold(s, own))
    m = jnp.maximum(jnp.max(s, axis=-1, keepdims=True), sink_col)
    p = jnp.exp(s - m)
    p_sink = jnp.exp(sink_col - m)
    denom = jnp.sum(p, axis=-1, keepdims=True) + p_sink
    return p / denom, p_sink / denom


def _attn_forward(proj, tabs, sinks):
    T = proj.shape[0]
    nb = T // BLOCK

    def body(q_ref, kvc_ref, kvp_ref, g0_ref, g1_ref, cc, sac, sbc, cp_, sap, sbp, sink_ref, y_ref, qrb_ref, krb_ref):
        n = pl.program_id(0)
        tc = tcur = (cc[...], sac[...], sbc[...])
        tprev = (cp_[...], sap[...], sbp[...])
        qr = _rope(q_ref[...], *tc) * ATTN_SCALE
        kr_cur = _rope(kvc_ref[:, 0:KV_W], *tcur)
        kr_prev = _rope(kvp_ref[:, 0:KV_W], *tprev)
        qrb_ref[...] = qr.astype(BF16)
        krb_ref[...] = kr_cur.astype(BF16)
        v_cur, v_prev = kvc_ref[:, KV_W:2 * KV_W], kvp_ref[:, KV_W:2 * KV_W]
        mask = _attn_mask(n)
        outs = []
        k2s = [_kv_pair_operand(kr_prev, kr_cur, kh) for kh in range(N_KV)]
        v2s = [_kv_pair_operand(v_prev, v_cur, kh) for kh in range(N_KV)]
        scores = [_attn_scores(qr, k2s[kh], kh) for kh in range(N_KV)]
        for kh in range(N_KV):
            pn, _ = _attn_softmax(scores[kh][1], _sink_col(sink_ref, kh), mask)
            o_big = _dot(_unrestack(_unfold(pn.astype(BF16), mask[0])), v2s[kh])
            outs += [o_big[0:BLOCK], o_big[BLOCK:2 * BLOCK]]
        o = jnp.concatenate(outs, axis=1)
        g = jnp.concatenate([g0_ref[...], g1_ref[...]], axis=1)
        y_ref[...] = (o * (g * _sigmoid(g))).astype(BF16)

    def blk(w, cb):
        return pl.BlockSpec((BLOCK, w), lambda n, cb=cb: (n, cb))

    prev = lambda w, cb: pl.BlockSpec((BLOCK, w), lambda n, cb=cb: (jnp.maximum(n - 1, 0), cb))
    return pl.pallas_call(
        body, grid=(nb,), name="attn_forward",
        out_shape=(jax.ShapeDtypeStruct((T, D_MODEL), BF16), jax.ShapeDtypeStruct((T, D_MODEL), BF16),
                   jax.ShapeDtypeStruct((T, KV_W), BF16)),
        in_specs=[blk(D_MODEL, 0), blk(CB, CB_KV), prev(CB, CB_KV), blk(CB, CB_GA), blk(CB, CB_GA + 1),
                  blk(128, 0), blk(128, 0), blk(128, 0), prev(128, 0), prev(128, 0), prev(128, 0),
                  pl.BlockSpec(memory_space=pltpu.SMEM)],
        out_specs=(blk(D_MODEL, 0), blk(D_MODEL, 0), blk(KV_W, 0)),
        compiler_params=_cp("parallel"),
    )(proj, proj, proj, proj, proj, *tabs, *tabs, sinks)


def _scan_rows8():
    return lax.broadcasted_iota(jnp.int32, (8, D_MODEL), 0)


def _scan_forward(a_ref, b_ref, h_ref, carry, rows):
    row = _scan_rows8()

    def group(i, carry):
        off = pl.multiple_of(i * 8, 8)
        a, b = a_ref[pl.ds(off, 8), :], b_ref[pl.ds(off, 8), :]
        for d in (1, 2, 4):
            ok = row >= d
            b = jnp.where(ok, a * pltpu.roll(b, d, 0) + b, b)
            a = jnp.where(ok, a * pltpu.roll(a, d, 0), a)
        h = a * carry + b
        h_ref[pl.ds(off, 8), :] = h
        return h[7:8, :]

    return lax.fori_loop(0, rows // 8, group, carry)


def _scan_backward(a_ref, g_ref, lam_ref, carry, rows):
    row = _scan_rows8()

    def group(i, carry):
        off = pl.multiple_of((rows // 8 - 1 - i) * 8, 8)
        a, g = a_ref[pl.ds(off, 8), :], g_ref[pl.ds(off, 8), :]
        b = a * g
        for d in (1, 2, 4):
            ok = row < 8 - d
            b = jnp.where(ok, a * pltpu.roll(b, 8 - d, 0) + b, b)
            a = jnp.where(ok, a * pltpu.roll(a, 8 - d, 0), a)
        mu = a * carry + b
        mu_below = jnp.where(row == 7, carry, pltpu.roll(mu, 7, 0))
        lam_ref[pl.ds(off, 8), :] = g + mu_below
        return mu[0:1, :]

    return lax.fori_loop(0, rows // 8, group, carry)


def _conv_taps(xbuf, xr, tail):
    rows = xr.shape[0]
    xbuf[0:8, :] = tail
    xbuf[8:rows + 8, :] = xr
    return [xbuf[pl.ds(8 - (CONV_W - 1 - k), rows), :] for k in range(CONV_W - 1)] + [xr]


def _rnn_gates(xbuf, xr, tail, cw, cb, wa_ref, wx_ref, ba, bx, sp, reset):
    xs = _conv_taps(xbuf, xr, tail)
    xc = xs[0] * cw[0:1, :]
    for k in range(1, CONV_W):
        xc = xc + xs[k] * cw[k:k + 1, :]
    xc = xc + cb
    xcb = xc.astype(BF16)
    za = jnp.concatenate([_dot(xcb[:, RNN_BW * j:RNN_BW * (j + 1)], wa_ref[j]) for j in range(RNN_BLOCKS)], axis=1) + ba
    zx = jnp.concatenate([_dot(xcb[:, RNN_BW * j:RNN_BW * (j + 1)], wx_ref[j]) for j in range(RNN_BLOCKS)], axis=1) + bx
    r, i = _sigmoid(za), _sigmoid(zx)
    neg_log_a = LRU_C * r * sp
    a_raw = jnp.exp(-neg_log_a)
    mult_raw = jnp.sqrt(jnp.tanh(neg_log_a) * (1.0 + a_raw * a_raw))
    a = jnp.where(reset, 0.0, a_raw)
    mult = jnp.where(reset, 1.0, mult_raw)
    return xc, r, i, a, mult


def _rnn_forward(proj, pos_col, conv_w, conv_b, rwa, rwx, ba, bx, lam):
    T = proj.shape[0]
    tr = min(T, 256)

    def body(x0, x1, g0, g1, pos_ref, cw_ref, cb_ref, wa_ref, wx_ref, ba_ref, bx_ref, lam_ref,
             y_ref, h_ref, xc_ref, r_ref, i_ref, a_ref, mult_ref, xbuf, bbuf, tail, carry):
        t = pl.program_id(0)

        @pl.when(t == 0)
        def _():
            tail[...] = jnp.zeros_like(tail)
            carry[...] = jnp.zeros_like(carry)

        xr = jnp.concatenate([x0[...], x1[...]], axis=1)
        sp = _softplus(-lam_ref[...])
        reset = pos_ref[...] == 0
        xc, r, i, a, mult = _rnn_gates(
            xbuf, xr, tail[...], cw_ref[...], cb_ref[...], wa_ref, wx_ref, ba_ref[...], bx_ref[...], sp, reset)
        xc_ref[...] = xc
        r_ref[...] = r
        i_ref[...] = i
        a_ref[...] = a
        mult_ref[...] = mult
        bbuf[...] = mult * (i * xc)
        last = _scan_forward(a_ref, bbuf, h_ref, carry[0:1, :], tr)
        carry[...] = jnp.broadcast_to(last, carry.shape)
        tail[...] = xr[tr - 8:tr, :]
        g = jnp.concatenate([g0[...], g1[...]], axis=1)
        y_ref[...] = (h_ref[...] * (g * _sigmoid(g))).astype(BF16)

    blk = lambda cb: pl.BlockSpec((tr, CB), lambda t, cb=cb: (t, cb))
    row = lambda w: pl.BlockSpec((1, w), lambda t: (0, 0))
    full3 = pl.BlockSpec((RNN_BLOCKS, RNN_BW, RNN_BW), lambda t: (0, 0, 0))
    tok = pl.BlockSpec((tr, D_MODEL), lambda t: (t, 0))
    act = jax.ShapeDtypeStruct((T, D_MODEL), F32)
    return pl.pallas_call(
        body, out_shape=(jax.ShapeDtypeStruct((T, D_MODEL), BF16),) + (act,) * 6,
        grid=(T // tr,), name="rnn_forward",
        in_specs=[blk(CB_XR), blk(CB_XR + 1), blk(CB_GR), blk(CB_GR + 1), pl.BlockSpec((tr, 1), lambda t: (t, 0)),
                  pl.BlockSpec((CONV_W, D_MODEL), lambda t: (0, 0)), row(D_MODEL), full3, full3,
                  row(D_MODEL), row(D_MODEL), row(D_MODEL)],
        out_specs=(tok,) * 7,
        scratch_shapes=[pltpu.VMEM((tr + 8, D_MODEL), F32), pltpu.VMEM((tr, D_MODEL), F32),
                        pltpu.VMEM((8, D_MODEL), F32), pltpu.VMEM((8, D_MODEL), F32)],
        compiler_params=_cp("arbitrary"),
    )(proj, proj, proj, proj, pos_col, conv_w, conv_b, rwa, rwx, ba, bx, lam)


ROW_PARTS = 1


def _merge_and_head(x, target, y_attn, y_rnn, proj, wap, wrp, wo, mod_row, final_g):
    T = x.shape[0]
    tm = min(T, 256)

    def body(x_ref, t_ref, ya_ref, yr_ref, ma0, ma1, mr0, mr1, wap_ref, wrp_ref, wo_ref, mod_ref, fg_ref,
             dx2_ref, mg_ref, do_ref, dpa_ref, dpr_ref, dya_ref, dyr_ref, dc_ref, dfg_ref, dgate_ref, loss_ref):
        i = pl.program_id(0)
        gate = mod_ref[:, 2 * D_MODEL:3 * D_MODEL]
        fg = fg_ref[...]
        parts = [slice(p * (tm // ROW_PARTS), (p + 1) * (tm // ROW_PARTS)) for p in range(ROW_PARTS)]
        each = range(ROW_PARTS)
        pa = [_dot(ya_ref[r, :], wap_ref[...]) for r in parts]
        pr = [_dot(yr_ref[r, :], wrp_ref[...]) for r in parts]
        sa = [_sigmoid(jnp.concatenate([ma0[r, :], ma1[r, :]], axis=1)) for r in parts]
        sr = [_sigmoid(jnp.concatenate([mr0[r, :], mr1[r, :]], axis=1)) for r in parts]
        mb = [(sa[p] * pa[p] + sr[p] * pr[p]).astype(BF16) for p in each]
        o = [_dot(mb[p], wo_ref[...]) for p in each]
        x2 = [x_ref[r, :] + gate * o[p] for p, r in enumerate(parts)]
        r2 = [_rms(v) for v in x2]
        xn2 = [x2[p] * r2[p] for p in each]
        err = [xn2[p] * fg - t_ref[r, :] for p, r in enumerate(parts)]
        dy = [e * (1.0 / D_MODEL) for e in err]
        dxn = [d * fg for d in dy]
        dx2 = [r2[p] * (dxn[p] - xn2[p] * jnp.mean(dxn[p] * xn2[p], axis=-1, keepdims=True)) for p in each]
        dob = [(dx2[p] * gate).astype(BF16) for p in each]
        dmerged = [_dot_nt(d, wo_ref[...]) for d in dob]
        dpa = [(dmerged[p] * sa[p]).astype(BF16) for p in each]
        dpr = [(dmerged[p] * sr[p]).astype(BF16) for p in each]
        dya = [_dot_nt(d, wap_ref[...]) for d in dpa]
        dyr = [_dot_nt(d, wrp_ref[...]) for d in dpr]
        loss_t, dfg_t, dgate_t = 0.0, 0.0, 0.0
        for p, r in enumerate(parts):
            dx2_ref[r, :] = dx2[p]
            mg_ref[r, :] = mb[p]
            do_ref[r, :] = dob[p]
            dpa_ref[r, :] = dpa[p]
            dpr_ref[r, :] = dpr[p]
            dya_ref[r, :] = dya[p]
            dyr_ref[r, :] = dyr[p]
            dc_ref[r, 0:D_MODEL] = (dmerged[p] * pa[p] * sa[p] * (1.0 - sa[p])).astype(BF16)
            dc_ref[r, D_MODEL:2 * D_MODEL] = (dmerged[p] * pr[p] * sr[p] * (1.0 - sr[p])).astype(BF16)
            loss_t = loss_t + 0.5 * jnp.sum(
                jnp.sum(err[p] * err[p], axis=-1, keepdims=True) * (1.0 / D_MODEL), axis=0, keepdims=True)
            dfg_t = dfg_t + jnp.sum(dy[p] * xn2[p], axis=0, keepdims=True)
            dgate_t = dgate_t + jnp.sum(dx2[p] * o[p], axis=0, keepdims=True)

        @pl.when(i == 0)
        def _():
            dfg_ref[...] = jnp.zeros_like(dfg_ref)
            dgate_ref[...] = jnp.zeros_like(dgate_ref)
            loss_ref[...] = jnp.zeros_like(loss_ref)

        dfg_ref[...] += dfg_t
        dgate_ref[...] += dgate_t
        loss_ref[...] += jnp.broadcast_to(loss_t, loss_ref.shape)

    tok = lambda w: pl.BlockSpec((tm, w), lambda i: (i, 0))
    blk = lambda cb: pl.BlockSpec((tm, CB), lambda i, cb=cb: (i, cb))
    wfull = pl.BlockSpec((D_MODEL, D_MODEL), lambda i: (0, 0), pipeline_mode=pl.Buffered(1))
    row = lambda w: pl.BlockSpec((1, w), lambda i: (0, 0))
    out_shape = (
        jax.ShapeDtypeStruct((T, D_MODEL), F32), jax.ShapeDtypeStruct((T, D_MODEL), BF16),
        jax.ShapeDtypeStruct((T, D_MODEL), BF16), jax.ShapeDtypeStruct((T, D_MODEL), BF16),
        jax.ShapeDtypeStruct((T, D_MODEL), BF16), jax.ShapeDtypeStruct((T, D_MODEL), F32),
        jax.ShapeDtypeStruct((T, D_MODEL), F32), jax.ShapeDtypeStruct((T, 2 * D_MODEL), BF16),
        jax.ShapeDtypeStruct((1, D_MODEL), F32), jax.ShapeDtypeStruct((1, D_MODEL), F32),
        jax.ShapeDtypeStruct((1, 128), F32),
    )
    return pl.pallas_call(
        body, out_shape=out_shape, grid=(T // tm,), name="merge_and_head",
        in_specs=[tok(D_MODEL), tok(D_MODEL), tok(D_MODEL), tok(D_MODEL), blk(CB_MA), blk(CB_MA + 1), blk(CB_MR),
                  blk(CB_MR + 1), wfull, wfull, wfull, row(ADA_W), row(D_MODEL)],
        out_specs=(tok(D_MODEL),) * 7 + (tok(2 * D_MODEL), row(D_MODEL), row(D_MODEL), row(128)),
        compiler_params=_cp("arbitrary"),
    )(x, target, y_attn, y_rnn, proj, proj, proj, proj, wap, wrp, wo, mod_row, final_g)


def _attn_backward(proj, qr_b, kr_b, d_y, tabs, sinks):
    T = proj.shape[0]
    nb = T // BLOCK

    def body(qrb_ref, krc_ref, krp_ref, vc_ref, vp_ref, g0_ref, g1_ref, dy_ref, cc, sac, sbc, cp_, sap, sbp, sink_ref,
             dq_ref, dkv_ref, dg_ref, dsink_ref, carry):
        n = pl.program_id(0)

        @pl.when(n == 0)
        def _():
            carry[...] = jnp.zeros_like(carry)
            dsink_ref[...] = jnp.zeros_like(dsink_ref)

        @pl.when(n < nb)
        def _():
            tc = tcur = (cc[...], sac[...], sbc[...])
            tprev = (cp_[...], sap[...], sbp[...])
            qr, kr_cur, kr_prev = qrb_ref[...], krc_ref[...], krp_ref[...]
            v_cur, v_prev = vc_ref[...], vp_ref[...]
            g = jnp.concatenate([g0_ref[...], g1_ref[...]], axis=1)
            sg = _sigmoid(g)
            dy = dy_ref[...]
            d_o = dy * (g * sg)
            mask = _attn_mask(n)
            lane = lax.broadcasted_iota(jnp.int32, (1, 128), 1)
            rowg = lax.broadcasted_iota(jnp.int32, (GROUP * BLOCK, 1), 0) // BLOCK
            o_parts, dq_parts = [], []
            dk_cols, dv_cols = [None, None], [None, None]
            dsink = jnp.zeros((1, 128), F32)
            heads = range(N_KV)
            k2s = [_kv_pair_operand(kr_prev, kr_cur, kh) for kh in heads]
            v2s = [_kv_pair_operand(v_prev, v_cur, kh) for kh in heads]
            scores = [_attn_scores(qr, k2s[kh], kh) for kh in heads]
            do2s = [_pair_rows(d_o, kh).astype(BF16) for kh in heads]
            dpns = [_fold(_restack(_dot_nt(do2s[kh], v2s[kh])), mask[0]) for kh in heads]
            probs = [_attn_softmax(scores[kh][1], _sink_col(sink_ref, kh), mask) for kh in heads]
            p_bigs = [_unrestack(_unfold(probs[kh][0].astype(BF16), mask[0])) for kh in heads]
            o_bigs = [_dot(p_bigs[kh], v2s[kh]) for kh in heads]
            dv2s = [_dot_tn(p_bigs[kh], do2s[kh]) for kh in heads]
            deltas = [jnp.sum(probs[kh][0] * dpns[kh], axis=-1, keepdims=True) for kh in heads]
            ds_bigs = [_unrestack(_unfold((probs[kh][0] * (dpns[kh] - deltas[kh])).astype(BF16), mask[0])) for kh in heads]
            dq2s = [_dot(ds_bigs[kh], k2s[kh]) for kh in heads]
            dk2s = [_dot_tn(ds_bigs[kh], scores[kh][0]) for kh in heads]
            for kh in heads:
                o_parts += [o_bigs[kh][0:BLOCK], o_bigs[kh][BLOCK:2 * BLOCK]]
                dq_parts += [dq2s[kh][0:BLOCK], dq2s[kh][BLOCK:2 * BLOCK]]
                dk_c, dv_c = _fold_pair(dk2s[kh], kh), _fold_pair(dv2s[kh], kh)
                c = kh // 2
                dk_cols[c] = dk_c if dk_cols[c] is None else dk_cols[c] + dk_c
                dv_cols[c] = dv_c if dv_cols[c] is None else dv_cols[c] + dv_c
                ds_rows = probs[kh][1] * deltas[kh]
                for gq in range(GROUP):
                    val = -jnp.sum(jnp.where(rowg == gq, ds_rows, 0.0), axis=0, keepdims=True)
                    dsink = dsink + jnp.where(lane == GROUP * kh + ROW_GROUP_HEAD[gq], val, 0.0)
            o = jnp.concatenate(o_parts, axis=1)
            dg_ref[...] = (dy * o * (sg * (1.0 + g * (1.0 - sg)))).astype(BF16)
            dq_ref[...] = (_unrope(jnp.concatenate(dq_parts, axis=1), *tc) * ATTN_SCALE).astype(BF16)
            dk_all, dv_all = jnp.concatenate(dk_cols, axis=1), jnp.concatenate(dv_cols, axis=1)
            dk_prev = _unrope(dk_all[0:BLOCK], *tprev)
            dk_cur = _unrope(dk_all[BLOCK:2 * BLOCK], *tcur)
            dv_prev, dv_cur = dv_all[0:BLOCK], dv_all[BLOCK:2 * BLOCK]
            dkv_ref[...] = (carry[...] + jnp.concatenate([dk_prev, dv_prev], axis=1)).astype(BF16)
            carry[...] = jnp.concatenate([dk_cur, dv_cur], axis=1)
            dsink_ref[...] += dsink

        @pl.when(n == nb)
        def _():
            dkv_ref[...] = carry[...].astype(BF16)

    cur = lambda w, cb: pl.BlockSpec((BLOCK, w), lambda n, cb=cb: (jnp.minimum(n, nb - 1), cb))
    prev = lambda w, cb: pl.BlockSpec((BLOCK, w), lambda n, cb=cb: (jnp.maximum(jnp.minimum(n, nb - 1) - 1, 0), cb))
    out_shape = (jax.ShapeDtypeStruct((T, D_MODEL), BF16), jax.ShapeDtypeStruct((T, 2 * KV_W), BF16),
                 jax.ShapeDtypeStruct((T, D_MODEL), BF16), jax.ShapeDtypeStruct((1, 128), F32))
    return pl.pallas_call(
        body, out_shape=out_shape, grid=(nb + 1,), name="attn_backward",
        in_specs=[cur(D_MODEL, 0), cur(KV_W, 0), prev(KV_W, 0), cur(KV_W, V_COL_BLOCK), prev(KV_W, V_COL_BLOCK),
                  cur(CB, CB_GA), cur(CB, CB_GA + 1), cur(D_MODEL, 0),
                  cur(128, 0), cur(128, 0), cur(128, 0), prev(128, 0), prev(128, 0), prev(128, 0),
                  pl.BlockSpec(memory_space=pltpu.SMEM)],
        out_specs=(cur(D_MODEL, 0), pl.BlockSpec((BLOCK, 2 * KV_W), lambda n: (jnp.maximum(n - 1, 0), 0)),
                   cur(D_MODEL, 0), pl.BlockSpec((1, 128), lambda n: (0, 0))),
        scratch_shapes=[pltpu.VMEM((BLOCK, 2 * KV_W), F32)],
        compiler_params=_cp("arbitrary"),
    )(qr_b, kr_b, kr_b, proj, proj, proj, proj, d_y, *tabs, *tabs, sinks)


def _rnn_backward(proj, pos_col, h_rnn, saved, d_y, conv_w, rwa, rwx, lam):
    T = proj.shape[0]
    tr = min(T, 256)
    nt = T // tr
    hb = tr // 8

    def body(x0, x1, xh0, xh1, g0, g1, pos_ref, h_ref, hh_ref, xc_ref, r_ref, i_ref, a_ref, mult_ref, dy_ref,
             cw_ref, wa_ref, wx_ref, lam_ref, db_ref, dcw_ref, dcb_ref, dwa_ref, dwx_ref, dba_ref, dbx_ref, dlam_ref,
             xbuf, hbuf, dbuf, gbuf, lbuf, mu_carry, dxc_head):
        step = pl.program_id(0)
        first_tile = step == nt - 1

        @pl.when(step == 0)
        def _():
            mu_carry[...] = jnp.zeros_like(mu_carry)
            dxc_head[...] = jnp.zeros_like(dxc_head)
            for ref in (dcw_ref, dcb_ref, dwa_ref, dwx_ref, dba_ref, dbx_ref, dlam_ref):
                ref[...] = jnp.zeros_like(ref)

        xr = jnp.concatenate([x0[...], x1[...]], axis=1)
        tail = jnp.where(first_tile, 0.0, jnp.concatenate([xh0[...], xh1[...]], axis=1))
        lam_v = lam_ref[...]
        sp = _softplus(-lam_v)
        reset = pos_ref[...] == 0
        cw = cw_ref[...]
        xbuf[0:8, :] = tail
        xbuf[8:tr + 8, :] = xr
        g = jnp.concatenate([g0[...], g1[...]], axis=1)
        sg = _sigmoid(g)
        dy = dy_ref[...]
        h = h_ref[...]
        db_ref[:, D_MODEL:2 * D_MODEL] = (dy * h * (sg * (1.0 + g * (1.0 - sg)))).astype(BF16)
        gbuf[...] = dy * (g * sg)
        top = _scan_backward(a_ref, gbuf, lbuf, mu_carry[0:1, :], tr)
        mu_carry[...] = jnp.broadcast_to(top, mu_carry.shape)
        hbuf[0:8, :] = jnp.where(first_tile, 0.0, hh_ref[...])
        hbuf[8:tr + 8, :] = h
        live = jnp.logical_not(reset)
        dbuf[tr:tr + 8, :] = dxc_head[...]
        for j in range(RNN_BLOCKS):
            sl = slice(RNN_BW * j, RNN_BW * (j + 1))
            lam_t, h_prev = lbuf[:, sl], hbuf[pl.ds(7, tr), sl]
            xc, r, i, a, mult = xc_ref[:, sl], r_ref[:, sl], i_ref[:, sl], a_ref[:, sl], mult_ref[:, sl]
            d_a = jnp.where(live, lam_t * h_prev, 0.0)
            d_mult = jnp.where(live, lam_t * (i * xc), 0.0)
            d_ixc = lam_t * mult
            d_i = d_ixc * xc
            d_log_a = d_a * a - d_mult * (a * a / mult)
            d_za = d_log_a * (-LRU_C * sp[:, sl]) * (r * (1.0 - r))
            d_zx = d_i * (i * (1.0 - i))
            dlam_ref[:, sl] += jnp.sum(d_log_a * r, axis=0, keepdims=True) * (LRU_C * _sigmoid(-lam_v[:, sl]))
            dba_ref[:, sl] += jnp.sum(d_za, axis=0, keepdims=True)
            dbx_ref[:, sl] += jnp.sum(d_zx, axis=0, keepdims=True)
            xcb, dzab, dzxb = xc.astype(BF16), d_za.astype(BF16), d_zx.astype(BF16)
            dwa_ref[j] += _dot_tn(xcb, dzab)
            dwx_ref[j] += _dot_tn(xcb, dzxb)
            d_xc = d_ixc * i + (_dot_nt(dzab, wa_ref[j]) + _dot_nt(dzxb, wx_ref[j]))
            dcb_ref[:, sl] += jnp.sum(d_xc, axis=0, keepdims=True)
            for k in range(CONV_W):
                tap = xr[:, sl] if k == CONV_W - 1 else xbuf[pl.ds(8 - (CONV_W - 1 - k), tr), sl]
                dcw_ref[k:k + 1, sl] += jnp.sum(d_xc * tap, axis=0, keepdims=True)
            dbuf[0:tr, sl] = d_xc
            d_xr = d_xc * cw[CONV_W - 1:CONV_W, sl]
            for k in range(CONV_W - 1):
                d_xr = d_xr + dbuf[pl.ds(CONV_W - 1 - k, tr), sl] * cw[k:k + 1, sl]
            dxc_head[:, sl] = d_xc[0:8, :]
            db_ref[:, sl] = d_xr.astype(BF16)

    rev = lambda s: nt - 1 - s
    blk = lambda cb: pl.BlockSpec((tr, CB), lambda s, cb=cb: (rev(s), cb))
    halo = lambda w, cb: pl.BlockSpec((8, w), lambda s, cb=cb: (jnp.maximum(rev(s) * hb - 1, 0), cb))
    tok = lambda w: pl.BlockSpec((tr, w), lambda s: (rev(s), 0))
    row = lambda w: pl.BlockSpec((1, w), lambda s: (0, 0))
    full3 = pl.BlockSpec((RNN_BLOCKS, RNN_BW, RNN_BW), lambda s: (0, 0, 0))
    cwspec = pl.BlockSpec((CONV_W, D_MODEL), lambda s: (0, 0))
    vec = jax.ShapeDtypeStruct((1, D_MODEL), F32)
    gate_w = jax.ShapeDtypeStruct((RNN_BLOCKS, RNN_BW, RNN_BW), F32)
    out_shape = (jax.ShapeDtypeStruct((T, 2 * D_MODEL), BF16), jax.ShapeDtypeStruct((CONV_W, D_MODEL), F32), vec,
                 gate_w, gate_w, vec, vec, vec)
    big = lambda: pltpu.VMEM((tr, D_MODEL), F32)
    ext = lambda: pltpu.VMEM((tr + 8, D_MODEL), F32)
    return pl.pallas_call(
        body, out_shape=out_shape, grid=(nt,), name="rnn_backward",
        in_specs=[blk(CB_XR), blk(CB_XR + 1), halo(CB, CB_XR), halo(CB, CB_XR + 1), blk(CB_GR), blk(CB_GR + 1),
                  pl.BlockSpec((tr, 1), lambda s: (rev(s), 0)), tok(D_MODEL), halo(D_MODEL, 0)] + [tok(D_MODEL)] * 6
        + [cwspec, full3, full3, row(D_MODEL)],
        out_specs=(tok(2 * D_MODEL), cwspec, row(D_MODEL), full3, full3, row(D_MODEL), row(D_MODEL), row(D_MODEL)),
        scratch_shapes=[ext(), ext(), ext(), big(), big(), pltpu.VMEM((8, D_MODEL), F32), pltpu.VMEM((8, D_MODEL), F32)],
        compiler_params=_cp("arbitrary"),
    )(proj, proj, proj, proj, proj, proj, pos_col, h_rnn, h_rnn, *saved, d_y, conv_w, rwa, rwx, lam)


def _input_backward(pieces, w_in, x, dx2, mod_row, norm_g):
    T = x.shape[0]
    tm = min(T, 512)
    n = len(pieces)

    def body(*refs):
        d_refs = refs[:n]
        w_ref, x_ref, dx2_ref, mod_ref, g_ref, gx_ref, dshift_ref, dscale_ref, dg_ref = refs[n:]
        i = pl.program_id(0)
        dh = None
        for d_ref, (_, start, count) in zip(d_refs, pieces):
            part = _dot_nt(d_ref[...], w_ref[:, start * CB:(start + count) * CB])
            dh = part if dh is None else dh + part

        @pl.when(i == 0)
        def _():
            dshift_ref[...] = jnp.zeros_like(dshift_ref)
            dscale_ref[...] = jnp.zeros_like(dscale_ref)
            dg_ref[...] = jnp.zeros_like(dg_ref)

        xf = x_ref[...]
        r1 = _rms(xf)
        xn = xf * r1
        gn = g_ref[...]
        s1 = 1.0 + mod_ref[:, D_MODEL:2 * D_MODEL]
        dshift_ref[...] += jnp.sum(dh, axis=0, keepdims=True)
        dscale_ref[...] += jnp.sum(dh * (xn * gn), axis=0, keepdims=True)
        dg_ref[...] += jnp.sum(dh * s1 * xn, axis=0, keepdims=True)
        dxn = dh * s1 * gn
        gx_ref[...] = dx2_ref[...] + r1 * (dxn - xn * jnp.mean(dxn * xn, axis=-1, keepdims=True))

    tok = lambda w: pl.BlockSpec((tm, w), lambda i: (i, 0))
    row = lambda w: pl.BlockSpec((1, w), lambda i: (0, 0))
    vec = jax.ShapeDtypeStruct((1, D_MODEL), F32)
    return pl.pallas_call(
        body, out_shape=(jax.ShapeDtypeStruct((T, D_MODEL), F32), vec, vec, vec), grid=(T // tm,), name="input_backward",
        in_specs=[tok(c * CB) for _, _, c in pieces]
        + [pl.BlockSpec((D_MODEL, IN_W), lambda i: (0, 0), pipeline_mode=pl.Buffered(1)), tok(D_MODEL), tok(D_MODEL),
           row(ADA_W), row(D_MODEL)],
        out_specs=(tok(D_MODEL), row(D_MODEL), row(D_MODEL), row(D_MODEL)),
        compiler_params=_cp("arbitrary"),
    )(*[p[0] for p in pieces], w_in, x, dx2, mod_row, norm_g)


def _weight_grad(a, pieces, tag, a_is_transposed=False):
    M, T = a.shape if a_is_transposed else a.shape[::-1]
    n_blocks = sum(count for _, _, count in pieces)
    n = len(pieces)
    contract = _dot if a_is_transposed else _dot_tn

    def body(*refs):
        a_ref, b_refs, o_ref = refs[0], refs[1:1 + n], refs[-1]
        j = pl.program_id(0)
        for b_ref, (_, start, count) in zip(b_refs, pieces):
            @pl.when((j >= start) & (j < start + count))
            def _(b_ref=b_ref):
                o_ref[...] = contract(a_ref[...], b_ref[...])

    def piece_spec(start, count):
        return pl.BlockSpec((T, CB), lambda j: (0, jnp.clip(j - start, 0, count - 1)))

    return pl.pallas_call(
        body, out_shape=jax.ShapeDtypeStruct((M, n_blocks * CB), F32), grid=(n_blocks,), name=f"weight_grad_{tag}",
        in_specs=[pl.BlockSpec(a.shape, lambda j: (0, 0), pipeline_mode=pl.Buffered(1))] + [piece_spec(s, c) for _, s, c in pieces],
        out_specs=pl.BlockSpec((M, CB), lambda j: (0, j)), compiler_params=_cp("arbitrary"),
    )(a, *[p[0] for p in pieces])


def _adamw(w, g, m, v):
    m = ADAM_B1 * m + (1.0 - ADAM_B1) * g
    v = ADAM_B2 * v + (1.0 - ADAM_B2) * (g * g)
    m_hat = m / (1.0 - ADAM_B1 ** ADAM_STEP)
    v_hat = v / (1.0 - ADAM_B2 ** ADAM_STEP)
    delta = -ADAM_LR * (m_hat / (jnp.sqrt(v_hat) + ADAM_EPS) + ADAM_WD * w)
    return delta, m, v


def _sum_landed(kind, owns, lands, where, tag):
    n = len(owns)
    land = lands[0]
    if kind == "in":
        R, C = land.shape[1:]
        tr = 256
        grid = (R // tr,)
        own_spec = pl.BlockSpec((tr, C), lambda i, w: (i, w[0]))
        land_spec = pl.BlockSpec((3, tr, C), lambda i, w: (0, i, 0))
        out_spec = pl.BlockSpec((1, tr, C), lambda i, w: (w[1], i, 0))
        out_shape = (2, R, C)
        pick = lambda ref: ref[...]
    elif kind == "sq":
        R, C = land.shape[1:]
        grid = (1,)
        own_spec = pl.BlockSpec((1, R, C), lambda i, w: (w[0], 0, 0))
        land_spec = pl.BlockSpec((3, R, C), lambda i, w: (0, 0, 0))
        out_spec = pl.BlockSpec((1, R, C), lambda i, w: (w[1], 0, 0))
        out_shape = (2, R, C)
        pick = lambda ref: ref[0]
    else:
        B, R, C = land.shape[1:]
        grid = (1,)
        own_spec = pl.BlockSpec((B, 1, R, C), lambda i, w: (0, w[0], 0, 0))
        land_spec = pl.BlockSpec((3, B, R, C), lambda i, w: (0, 0, 0, 0))
        out_spec = pl.BlockSpec((B, 1, R, C), lambda i, w: (0, w[1], 0, 0))
        out_shape = (B, 2, R, C)
        pick = lambda ref: ref[:, 0]

    def body(w_ref, *refs):
        for k in range(n):
            own_ref, l_ref, o_ref = refs[k], refs[n + k], refs[2 * n + k]
            total = ((pick(own_ref) + l_ref[0].astype(F32)) + l_ref[1].astype(F32)) + l_ref[2].astype(F32)
            if kind == "rg":
                o_ref[:, 0] = total
            else:
                o_ref[0] = total

    grid_spec = pltpu.PrefetchScalarGridSpec(num_scalar_prefetch=1, grid=grid, in_specs=[own_spec] * n + [land_spec] * n,
                                             out_specs=(out_spec,) * n)
    return list(pl.pallas_call(
        body, out_shape=(jax.ShapeDtypeStruct(out_shape, F32),) * n, grid_spec=grid_spec, name=f"sum_landed_{tag}",
        compiler_params=_cp("parallel"),
    )(where, *owns, *lands))


def _adamw_shard(gs, ws, ms, vs, tag):
    n = len(ws)
    R, C = ws[0].shape
    tr = min(R, 256 if n == 1 else 64)

    def body(*refs):
        for k in range(n):
            g = refs[k][...]
            d, nm, nv = _adamw(refs[n + k][...], g, refs[2 * n + k][...], refs[3 * n + k][...])
            out = refs[4 * n + 4 * k:4 * n + 4 * k + 4]
            out[0][...] = g
            out[1][...] = d
            out[2][...] = nm
            out[3][...] = nv

    spec = pl.BlockSpec((tr, C), lambda i: (i, 0))
    sds = jax.ShapeDtypeStruct((R, C), F32)
    outs = pl.pallas_call(
        body, out_shape=(sds,) * (4 * n), grid=(R // tr,), name=f"adamw_{tag}",
        in_specs=[spec] * (4 * n), out_specs=(spec,) * (4 * n), compiler_params=_cp("parallel"),
    )(*gs, *ws, *ms, *vs)
    return [outs[4 * k:4 * k + 4] for k in range(n)]


def _adamw_w_ada(c_t, dmod_cols, w, m, v):
    R, C = w.shape

    def body(ct_ref, dm_ref, w_ref, m_ref, v_ref, g_ref, d_ref, nm_ref, nv_ref):
        g = _dot(ct_ref[...].astype(BF16), dm_ref[...].astype(BF16))
        d, nm, nv = _adamw(w_ref[...], g, m_ref[...], v_ref[...])
        g_ref[...] = g
        d_ref[...] = d
        nm_ref[...] = nm
        nv_ref[...] = nv

    tr = 256
    spec = pl.BlockSpec((tr, C), lambda i: (i, 0))
    sds = jax.ShapeDtypeStruct((R, C), F32)
    return pl.pallas_call(
        body, out_shape=(sds,) * 4, grid=(R // tr,), name="adamw_w_ada",
        in_specs=[pl.BlockSpec((tr, 128), lambda i: (i, 0)), pl.BlockSpec((128, C), lambda i: (0, 0))] + [spec] * 3,
        out_specs=(spec,) * 4, compiler_params=_cp("parallel"),
    )(c_t, dmod_cols, w, m, v)


def _adamw_small(small_all, ws, ms, vs):
    def body(s_ref, w_ref, m_ref, v_ref, g_ref, d_ref, nm_ref, nv_ref):
        g = s_ref[0]
        for b in range(1, N_DEV):
            g = g + s_ref[b]
        d, nm, nv = _adamw(w_ref[...], g, m_ref[...], v_ref[...])
        g_ref[...] = g
        d_ref[...] = d
        nm_ref[...] = nm
        nv_ref[...] = nv

    sds = jax.ShapeDtypeStruct((SMALL_ROWS, D_MODEL), F32)
    return pl.pallas_call(
        body, out_shape=(sds,) * 4, name="adamw_small", in_specs=[VMEM_SPEC] * 4, out_specs=(VMEM_SPEC,) * 4,
        compiler_params=pltpu.CompilerParams(vmem_limit_bytes=VMEM_LIMIT_V7X),
    )(small_all, ws, ms, vs)


ROW_MOD, ROW_NORM_G, ROW_CONV_B, ROW_BA, ROW_BX, ROW_LAM, ROW_FINAL_G, ROW_SINKS, ROW_CONV_W, ROW_LOSS = 0, 3, 4, 5, 6, 7, 8, 9, 10, 14


def _pack_small(b_ada, norm_g, conv_b, ba, bx, lam, final_g, sinks, conv_w_full, loss_row=None):
    lane_pad = lambda a: jnp.pad(a.reshape(1, -1), ((0, 0), (0, D_MODEL - a.size)))
    rows = [b_ada.reshape(3, D_MODEL), norm_g, conv_b, ba, bx, lam, final_g.reshape(1, D_MODEL), lane_pad(sinks), conv_w_full,
            jnp.zeros((1, D_MODEL), F32) if loss_row is None else lane_pad(loss_row),
            jnp.zeros((SMALL_ROWS - ROW_LOSS - 1, D_MODEL), F32)]
    return jnp.concatenate([r.astype(F32) for r in rows], axis=0)


def kernel(x, c, positions, w_ada, b_ada, norm_g, w_in, attn_sinks, conv_w, conv_b, rg_wa, rg_ba, rg_wx, rg_bx, rg_lambda, w_attn_proj, w_rnn_proj, w_out, final_g, loss_target, m_w_ada, m_b_ada, m_norm_g, m_w_in, m_attn_sinks, m_conv_w, m_conv_b, m_rg_wa, m_rg_ba, m_rg_wx, m_rg_bx, m_rg_lambda, m_w_attn_proj, m_w_rnn_proj, m_w_out, m_final_g, v_w_ada, v_b_ada, v_norm_g, v_w_in, v_attn_sinks, v_conv_w, v_conv_b, v_rg_wa, v_rg_ba, v_rg_wx, v_rg_bx, v_rg_lambda, v_w_attn_proj, v_w_rnn_proj, v_w_out, v_final_g):
    T = x.shape[1]
    my_chip = lax.axis_index("x") * 2 + lax.axis_index("y")
    my_dev = my_chip * 2 + lax.axis_index("c")
    x2d, tgt = x[0], loss_target[0]
    pos_col = positions.reshape(T, 1)

    chip_idx = my_chip.reshape(1).astype(jnp.int32)
    c_idx = lax.axis_index("c").reshape(1).astype(jnp.int32)
    sq_place = ((D_MODEL, D_MODEL), (SHARD_ROWS, D_MODEL), lambda chip: (chip, 0))
    rg_place = ((RNN_BLOCKS, RNN_BW, RNN_BW), (RNN_BLOCKS, SHARD_RG, RNN_BW), lambda chip: (0, chip, 0))
    in_place = ((D_MODEL, IN_W), (D_MODEL, SHARD_IN), lambda chip: (0, chip))
    placed = _cast_place([w_in[0], w_attn_proj[0], w_rnn_proj[0], w_out[0], rg_wa[0], rg_wx[0]], chip_idx,
                         [in_place, sq_place, sq_place, sq_place, rg_place, rg_place])
    cw_chips, c_all, mod_chips = _gather_mod(c.reshape(1, 1, D_MODEL), w_ada[0], conv_w[0])
    g_ssems, g_rsems, fulls, g_token = _gather_start([p.reshape(s) for p, s in zip(placed, FULL_SHAPES)], mod_chips)
    conv_w_f = jnp.transpose(cw_chips, (1, 0, 2)).reshape(CONV_W, D_MODEL)
    mod_all = jnp.transpose(mod_chips, (1, 0, 2)).reshape(N_DEV, ADA_W) + b_ada
    mod_row = lax.dynamic_slice_in_dim(mod_all, my_dev, 1, axis=0) + g_token[0:1, 0:1]

    h, h_t, tabs = _prenorm(x2d, mod_row, norm_g, pos_col)
    w_in_v = fulls[0]
    proj = _in_projection(h, w_in_v.reshape(D_MODEL, IN_W), chip_idx, None, "own")
    for k, mask in enumerate(CHIP_MASKS):
        w_in_v = _gather_wait(g_ssems[k], g_rsems[k], [w_in_v], [0], proj, f"w_in_{k}")[0]
        w_in_v = _forward_halves([w_in_v], [(0, 0, k)], f"w_in_{k}")[0]
        from_chip = (chip_idx ^ (mask >> 1)).astype(jnp.int32)
        proj = _in_projection(h, w_in_v.reshape(D_MODEL, IN_W), from_chip, proj, f"from_{k}")
    w_in_f = w_in_v.reshape(D_MODEL, IN_W)
    rest = _gather_wait(g_ssems[3], g_rsems[3], list(fulls[1:]), [1, 2, 3, 4, 5], proj, "rest")
    rest = _forward_halves(rest, [(idx - 1, idx, k) for idx in range(1, N_BIG) for k in range(3)], "rest")
    wap_f, wrp_f, wo_f = (g.reshape(D_MODEL, D_MODEL) for g in rest[0:3])
    rwa_f, rwx_f = (g.reshape(RNN_BLOCKS, RNN_BW, RNN_BW) for g in rest[3:5])
    y_attn, qr_b, kr_b = _attn_forward(proj, tabs, attn_sinks)
    y_rnn, h_rnn, *rnn_saved = _rnn_forward(proj, pos_col, conv_w_f, conv_b, rwa_f, rwx_f, rg_ba, rg_bx, rg_lambda)
    (dx2, merged, d_o, d_pa, d_pr, d_ya, d_yr, d_c, d_final_g, d_gate, loss_vec) = _merge_and_head(
        x2d, tgt, y_attn, y_rnn, proj, wap_f, wrp_f, wo_f, mod_row, final_g.reshape(1, D_MODEL))

    sq = (N_CHIPS, 2, SHARD_ROWS // 2, D_MODEL)
    rg = (RNN_BLOCKS, N_CHIPS, 2, SHARD_RG // 2, RNN_BW)
    rg_flat = (RNN_BLOCKS * N_CHIPS, 2, SHARD_RG // 2, RNN_BW)

    def chip_sum_and_start(views, axes, flat, unflat, tags_, kinds_, group):
        from_sib = _swap_halves(views, axes)
        exact, rounded = [None] * len(views), [None] * len(views)
        for shape in dict.fromkeys(flat):
            ids = [k for k, f in enumerate(flat) if f == shape]
            ex, ro = _presum([views[k].reshape(shape) for k in ids],
                             [from_sib[k].reshape(shape[:1] + shape[2:]) for k in ids], c_idx, tags_[ids[0]])
            for k, e, r in zip(ids, ex, ro):
                exact[k], rounded[k] = e.reshape(unflat[k]), r.reshape(unflat[k])
        return _exchange_start(rounded, kinds_, group), exact

    g_ap = _weight_grad(y_attn, [(d_pa, 0, 2)], "w_attn_proj")
    g_rp = _weight_grad(y_rnn, [(d_pr, 0, 2)], "w_rnn_proj")
    g_o = _weight_grad(merged, [(d_o, 0, 2)], "w_out")
    sq_half = (N_CHIPS, SHARD_ROWS // 2, D_MODEL)
    started1, own1 = chip_sum_and_start([g_ap.reshape(sq), g_rp.reshape(sq), g_o.reshape(sq)], [1, 1, 1], [sq] * 3, [sq_half] * 3,
                                  ["w_attn_proj", "w_rnn_proj", "w_out"], ["sq"] * 3, "proj")
    d_q, d_kv, d_ga, d_sinks = _attn_backward(proj, qr_b, kr_b, d_ya, tabs, attn_sinks + started1[4][0, 0])
    d_b, d_conv_w, d_conv_b, d_rwa, d_rwx, d_ba, d_bx, d_lam = _rnn_backward(
        proj, pos_col, h_rnn, rnn_saved, d_yr, conv_w_f, rwa_f, rwx_f, rg_lambda)
    pieces = [(d_q, CB_Q, 2), (d_kv, CB_KV, 1), (d_ga, CB_GA, 2), (d_b, CB_XR, 4), (d_c, CB_MA, 4)]
    g_in = _weight_grad(h_t, pieces, "w_in", a_is_transposed=True)
    started2, own2 = chip_sum_and_start(
        [g_in.reshape(2, D_MODEL // 2, IN_W), d_rwa.reshape(rg), d_rwx.reshape(rg)], [0, 2, 2],
        [(1, 2, D_MODEL // 2, IN_W), rg_flat, rg_flat],
        [(D_MODEL // 2, IN_W), (RNN_BLOCKS, N_CHIPS, SHARD_RG // 2, RNN_BW), (RNN_BLOCKS, N_CHIPS, SHARD_RG // 2, RNN_BW)],
        ["w_in", "rg_wa", "rg_wx"], ["in", "rg", "rg"], "in")
    grad_x, d_shift, d_scale, d_norm_g = _input_backward(pieces, w_in_f, x2d, dx2, mod_row + started2[4][0, 0], norm_g)

    d_mod = jnp.concatenate([d_shift, d_scale, d_gate], axis=1)
    small = _pack_small(d_mod, d_norm_g, d_conv_b, d_ba, d_bx, d_lam, d_final_g, d_sinks[:, :N_HEADS], d_conv_w, loss_vec)
    small_all = _gather_small(small)
    _, lands1 = _exchange_wait(*started1[:4], grad_x, "proj")
    _, lands2 = _exchange_wait(*started2[:4], grad_x, "in")
    tags = ["w_in", "w_attn_proj", "w_rnn_proj", "w_out", "rg_wa", "rg_wx"]
    chip_sums = [own2[0]] + list(own1) + list(own2[1:])
    lands = [lands2[0]] + list(lands1) + list(lands2[1:])
    where = jnp.concatenate([chip_idx, c_idx])
    kinds = ["in", "sq", "sq", "sq", "rg", "rg"]
    groups = [[0], [1, 2, 3], [4, 5]]
    halves = [None] * 6
    for ids in groups:
        for i, half in zip(ids, _sum_landed(kinds[ids[0]], [chip_sums[i] for i in ids], [lands[i] for i in ids], where,
                                            tags[ids[0]])):
            halves[i] = half
    grads = _assemble_with_sibling(halves, [0, 0, 0, 0, 1, 1])
    shapes2d = [(D_MODEL, SHARD_IN), (SHARD_ROWS, D_MODEL), (SHARD_ROWS, D_MODEL), (SHARD_ROWS, D_MODEL),
                (RNN_BLOCKS * SHARD_RG, RNN_BW), (RNN_BLOCKS * SHARD_RG, RNN_BW)]
    big_w = [w_in, w_attn_proj, w_rnn_proj, w_out, rg_wa, rg_wx]
    big_m = [m_w_in, m_w_attn_proj, m_w_rnn_proj, m_w_out, m_rg_wa, m_rg_wx]
    big_v = [v_w_in, v_w_attn_proj, v_w_rnn_proj, v_w_out, v_rg_wa, v_rg_wx]
    res = {}
    for ids in groups:
        flat2d = lambda arrs: [arrs[i].reshape(shapes2d[i]) for i in ids]
        outs = _adamw_shard(flat2d(grads), flat2d(big_w), flat2d(big_m), flat2d(big_v), tags[ids[0]])
        for i, four in zip(ids, outs):
            res[tags[i]] = [o.reshape(big_w[i].shape) for o in four]

    dmod_all = small_all[:, ROW_MOD:ROW_MOD + 3, :].reshape(N_DEV, ADA_W)
    dmod_cols = lax.dynamic_slice_in_dim(dmod_all, my_chip * SHARD_ADA, SHARD_ADA, axis=1)
    c_t = jnp.pad(jnp.transpose(c_all.reshape(N_DEV, D_MODEL)), ((0, 0), (0, 128 - N_DEV)))
    dmod_cols = jnp.pad(dmod_cols, ((0, 128 - N_DEV), (0, 0)))
    res["w_ada"] = [o.reshape(w_ada.shape) for o in _adamw_w_ada(c_t, dmod_cols, w_ada[0], m_w_ada[0], v_w_ada[0])]

    def full_conv(a):
        return lax.dynamic_update_slice_in_dim(jnp.zeros((CONV_W, D_MODEL), F32), a[0], my_chip * (D_MODEL // N_CHIPS), axis=1)

    packed = [_pack_small(p[0], p[1], p[2], p[3], p[4], p[5], p[6], p[7], full_conv(p[8])) for p in (
        (b_ada, norm_g, conv_b, rg_ba, rg_bx, rg_lambda, final_g, attn_sinks, conv_w),
        (m_b_ada, m_norm_g, m_conv_b, m_rg_ba, m_rg_bx, m_rg_lambda, m_final_g, m_attn_sinks, m_conv_w),
        (v_b_ada, v_norm_g, v_conv_b, v_rg_ba, v_rg_bx, v_rg_lambda, v_final_g, v_attn_sinks, v_conv_w))]
    small_out = _adamw_small(small_all, *packed)

    def unpack(slab):
        cw = lax.dynamic_slice_in_dim(slab[ROW_CONV_W:ROW_CONV_W + CONV_W], my_chip * (D_MODEL // N_CHIPS),
                                      D_MODEL // N_CHIPS, axis=1)
        return {
            "b_ada": slab[ROW_MOD:ROW_MOD + 3].reshape(1, ADA_W), "norm_g": slab[ROW_NORM_G:ROW_NORM_G + 1],
            "conv_b": slab[ROW_CONV_B:ROW_CONV_B + 1], "rg_ba": slab[ROW_BA:ROW_BA + 1], "rg_bx": slab[ROW_BX:ROW_BX + 1],
            "rg_lambda": slab[ROW_LAM:ROW_LAM + 1], "final_g": slab[ROW_FINAL_G], "attn_sinks": slab[ROW_SINKS:ROW_SINKS + 1, :N_HEADS],
            "conv_w": cw[None],
        }

    small_res = [unpack(s) for s in small_out]
    order = ["w_ada", "b_ada", "norm_g", "w_in", "attn_sinks", "conv_w", "conv_b", "rg_wa", "rg_ba", "rg_wx", "rg_bx",
             "rg_lambda", "w_attn_proj", "w_rnn_proj", "w_out", "final_g"]
    loss = small_out[0][ROW_LOSS, 0]
    outs = [loss, grad_x[None]]
    for kind in range(4):
        for name in order:
            outs.append(res[name][kind] if name in res else small_res[kind][name])
    return tuple(outs)
```

```python
import numpy as np
import jax
import jax.numpy as jnp
from jax import lax
from jax.experimental import pallas as pl
from jax.experimental.pallas import tpu as pltpu

F32 = jnp.float32
BF16 = jnp.bfloat16

D_MODEL = 1024
N_HEADS = 16
N_KV = 4
HEAD_DIM = 64
GROUP = N_HEADS // N_KV
BLOCK = 128
KV_W = N_KV * HEAD_DIM
ROT_HALF = 8
ROPE_THETA = 500000.0
ATTN_SCALE = 0.125
RNN_BLOCKS = 4
RNN_BW = 256
CONV_W = 4
LRU_C = 8.0
NORM_EPS = 1e-6
IN_W = 6656
CB = 512
N_CB = IN_W // CB
CB_Q, CB_KV, CB_GA, CB_XR, CB_GR, CB_MA, CB_MR = 0, 2, 3, 5, 7, 9, 11
V_COL_BLOCK = 5
N_CHIPS = 4
N_DEV = 8
SHARD_IN = IN_W // N_CHIPS
SHARD_ROWS = D_MODEL // N_CHIPS
SHARD_RG = RNN_BW // N_CHIPS
ADA_W = 3 * D_MODEL
SHARD_ADA = ADA_W // N_CHIPS
SMALL_ROWS = 16

ADAM_LR = 0.001
ADAM_B1 = 0.9
ADAM_B2 = 0.999
ADAM_EPS = 1e-08
ADAM_WD = 0.01
ADAM_STEP = 10

VMEM_LIMIT_V7X = 52 * 1024 * 1024
MESH = pl.DeviceIdType.MESH
ANY = pl.BlockSpec(memory_space=pl.ANY)
VMEM_SPEC = pl.BlockSpec(memory_space=pltpu.VMEM)


def _in_hbm(*arrays):
    return [pltpu.with_memory_space_constraint(a, pltpu.HBM) for a in arrays]


def _cp(*sem):
    return pltpu.CompilerParams(dimension_semantics=sem if sem else None, vmem_limit_bytes=VMEM_LIMIT_V7X)


def _dot(a, b):
    return jnp.dot(a, b, preferred_element_type=F32)


def _dot_nt(a, b):
    return lax.dot_general(a, b, (((1,), (1,)), ((), ())), preferred_element_type=F32)


def _dot_tn(a, b):
    return lax.dot_general(a, b, (((0,), (0,)), ((), ())), preferred_element_type=F32)


def _sigmoid(z):
    return 1.0 / (1.0 + jnp.exp(-z))


def _softplus(z):
    u = jnp.exp(-jnp.abs(z))
    log1p_u = jnp.where(u < 1e-3, u * (1.0 - u * (0.5 - u * (1.0 / 3.0))), jnp.log(1.0 + u))
    return jnp.maximum(z, 0.0) + log1p_u


def _rms(xf):
    return lax.rsqrt(jnp.mean(xf * xf, axis=-1, keepdims=True) + NORM_EPS)


def _me():
    return lax.axis_index("x"), lax.axis_index("y"), lax.axis_index("c")


def _peer(mask):
    x, y, c = _me()
    fx, fy, fc = (mask >> 2) & 1, (mask >> 1) & 1, mask & 1
    return (x ^ fx if fx else x, y ^ fy if fy else y, c ^ fc if fc else c)


def _chip_of(pos):
    return pos[0] * 2 + pos[1]


SIBLING_COLLECTIVE_ID = 0
SIBLING_ONLY = pltpu.CompilerParams(collective_id=SIBLING_COLLECTIVE_ID)


def _sibling_handshake():
    barrier = pltpu.get_barrier_semaphore()
    pl.semaphore_signal(barrier, inc=1, device_id=_peer(1), device_id_type=MESH)
    pl.semaphore_wait(barrier, 1)


CHIP_MASKS = (4, 2, 6)
ALL_MASKS = (1, 2, 3, 4, 5, 6, 7)


HBM_SPEC = pl.BlockSpec(memory_space=pltpu.HBM)
SEM_SPEC = pl.BlockSpec(memory_space=pltpu.SEMAPHORE)
SPLIT_COPY = pltpu.CompilerParams(has_side_effects=pltpu.SideEffectType.DATAFLOW_SIDE_EFFECTING)
N_BIG = 6
FULL_SHAPES = (
    (2, D_MODEL // 2, IN_W),
    (N_CHIPS, 2, SHARD_ROWS // 2, D_MODEL), (N_CHIPS, 2, SHARD_ROWS // 2, D_MODEL), (N_CHIPS, 2, SHARD_ROWS // 2, D_MODEL),
    (RNN_BLOCKS, N_CHIPS, 2, SHARD_RG // 2, RNN_BW), (RNN_BLOCKS, N_CHIPS, 2, SHARD_RG // 2, RNN_BW),
)


def _slot(full, idx, chip, half):
    if idx == 0:
        return full.at[half, :, pl.ds(pl.multiple_of(chip * SHARD_IN, 128), SHARD_IN)]
    return full.at[chip, half] if idx in (1, 2, 3) else full.at[:, chip, half]


def _three_halves(full, idx):
    return full.at[pl.ds(0, 3), 0] if idx in (1, 2, 3) else full.at[:, pl.ds(0, 3), 0]


def _gather_start(fulls, after):
    def body(*refs):
        full_refs = refs[:N_BIG]
        ssems, rsems = refs[N_BIG + 1:N_BIG + 5], refs[N_BIG + 5:N_BIG + 9]
        token = refs[2 * N_BIG + 9]
        me = _me()
        my_chip = _chip_of(me)
        for idx in range(N_BIG):
            for k, mask in enumerate(CHIP_MASKS):
                pair = k if idx == 0 else 3
                mine = _slot(full_refs[idx], idx, my_chip, me[2])
                pltpu.make_async_remote_copy(src_ref=mine, dst_ref=mine, send_sem=ssems[pair], recv_sem=rsems[pair],
                                             device_id=_peer(mask), device_id_type=MESH).start()
        token[...] = jnp.zeros_like(token)

    sem = pltpu.SemaphoreType.DMA(())
    out_shape = (sem,) * 8 + tuple(pltpu.HBM(f.shape, f.dtype) for f in fulls) + (jax.ShapeDtypeStruct((8, 128), F32),)
    outs = pl.pallas_call(
        body, out_shape=out_shape, name="gather_start",
        in_specs=[HBM_SPEC] * N_BIG + [ANY], out_specs=tuple([SEM_SPEC] * 8 + [HBM_SPEC] * N_BIG + [VMEM_SPEC]),
        input_output_aliases={i: 8 + i for i in range(N_BIG)}, compiler_params=SPLIT_COPY,
    )(*[pltpu.with_memory_space_constraint(f, pltpu.HBM) for f in fulls], after)
    return outs[0:4], outs[4:8], outs[8:8 + N_BIG], outs[8 + N_BIG]


def _gather_wait(ssem, rsem, arrays, idxs, after, tag):
    n = len(arrays)

    def body(*refs):
        full_refs, ssem_ref, rsem_ref = refs[:n], refs[n], refs[n + 1]
        me = _me()
        for full, idx in zip(full_refs, idxs):
            region = _slot(full, 0, _chip_of(me), me[2]) if idx == 0 else _three_halves(full, idx)
            arrived = pltpu.make_async_remote_copy(
                src_ref=region, dst_ref=region, send_sem=ssem_ref, recv_sem=rsem_ref, device_id=me, device_id_type=MESH)
            arrived.wait_send()
            arrived.wait_recv()

    outs = pl.pallas_call(
        body, out_shape=tuple(pltpu.HBM(a.shape, a.dtype) for a in arrays), name=f"gather_wait_{tag}",
        in_specs=[HBM_SPEC] * n + [SEM_SPEC, SEM_SPEC, ANY], out_specs=tuple([HBM_SPEC] * n),
        input_output_aliases={i: i for i in range(n)}, compiler_params=SPLIT_COPY,
    )(*arrays, ssem, rsem, after)
    return list(outs)


def _forward_halves(arrays, items, tag):
    n, m = len(arrays), len(items)

    def body(*refs):
        outs, ssem, rsem = refs[n:2 * n], refs[2 * n], refs[2 * n + 1]
        me = _me()
        sib = _peer(1)
        _sibling_handshake()
        cps = []
        for j, (pos, idx, k) in enumerate(items):
            chip = _chip_of(_peer(CHIP_MASKS[k]))
            cp = pltpu.make_async_remote_copy(
                src_ref=_slot(outs[pos], idx, chip, me[2]), dst_ref=_slot(outs[pos], idx, chip, me[2]),
                send_sem=ssem.at[j], recv_sem=rsem.at[j], device_id=sib, device_id_type=MESH)
            cp.start()
            cps.append(cp)
        for j, (pos, idx, k) in enumerate(items):
            chip = _chip_of(_peer(CHIP_MASKS[k]))
            pltpu.make_async_remote_copy(
                src_ref=_slot(outs[pos], idx, chip, me[2]), dst_ref=_slot(outs[pos], idx, chip, 1 - me[2]),
                send_sem=ssem.at[j], recv_sem=rsem.at[j], device_id=sib, device_id_type=MESH).wait_recv()
        for cp in cps:
            cp.wait_send()

    outs = pl.pallas_call(
        body, out_shape=tuple(jax.ShapeDtypeStruct(a.shape, a.dtype) for a in arrays), name=f"forward_halves_{tag}",
        in_specs=[ANY] * n, out_specs=tuple([ANY] * n), input_output_aliases={i: i for i in range(n)},
        scratch_shapes=[pltpu.SemaphoreType.DMA((m,)), pltpu.SemaphoreType.DMA((m,))], compiler_params=SIBLING_ONLY,
    )(*arrays)
    return list(outs)


def _gather_mod(c_row, w_ada_s, conv_w_s):
    def body(c_ref, wada_ref, cw_s, cw_f, call_ref, mod_ref, wsend, wrecv, lsem, csend, crecv, msend, mrecv):
        me = _me()
        my_chip = _chip_of(me)
        my_dev = my_chip * 2 + me[2]
        sends = []
        for k, mask in enumerate(CHIP_MASKS):
            cp = pltpu.make_async_remote_copy(src_ref=cw_s, dst_ref=cw_f.at[my_chip], send_sem=wsend.at[k], recv_sem=wrecv.at[k],
                                              device_id=_peer(mask), device_id_type=MESH)
            cp.start()
            sends.append(cp)
        local = [pltpu.make_async_copy(cw_s, cw_f.at[my_chip], lsem.at[0])]
        for cp in local:
            cp.start()

        call_ref[my_dev] = c_ref[0]
        csends = []
        for k, mask in enumerate(ALL_MASKS):
            cp = pltpu.make_async_remote_copy(
                src_ref=c_ref.at[0], dst_ref=call_ref.at[my_dev],
                send_sem=csend.at[k], recv_sem=crecv.at[k], device_id=_peer(mask), device_id_type=MESH)
            cp.start()
            csends.append(cp)
        for k, mask in enumerate(ALL_MASKS):
            frm = _peer(mask)
            pltpu.make_async_remote_copy(
                src_ref=c_ref.at[0], dst_ref=call_ref.at[_chip_of(frm) * 2 + frm[2]],
                send_sem=csend.at[k], recv_sem=crecv.at[k], device_id=frm, device_id_type=MESH).wait_recv()
        for cp in csends:
            cp.wait_send()

        c_all = call_ref[...].reshape(N_DEV, D_MODEL).astype(BF16)
        mod_ref[my_chip] = _dot(c_all, wada_ref[...].astype(BF16))
        msends = []
        for k, mask in enumerate(CHIP_MASKS):
            cp = pltpu.make_async_remote_copy(
                src_ref=mod_ref.at[my_chip], dst_ref=mod_ref.at[my_chip],
                send_sem=msend.at[k], recv_sem=mrecv.at[k], device_id=_peer(mask), device_id_type=MESH)
            cp.start()
            msends.append(cp)
        for k, mask in enumerate(CHIP_MASKS):
            frm = _peer(mask)
            pltpu.make_async_remote_copy(
                src_ref=mod_ref.at[my_chip], dst_ref=mod_ref.at[_chip_of(frm)],
                send_sem=msend.at[k], recv_sem=mrecv.at[k], device_id=frm, device_id_type=MESH).wait_recv()
        for cp in msends:
            cp.wait_send()

        for k, mask in enumerate(CHIP_MASKS):
            frm = _peer(mask)
            pltpu.make_async_remote_copy(src_ref=cw_s, dst_ref=cw_f.at[_chip_of(frm)], send_sem=wsend.at[k], recv_sem=wrecv.at[k],
                                         device_id=frm, device_id_type=MESH).wait_recv()
        for cp in sends:
            cp.wait_send()
        for cp in local:
            cp.wait()

    out_shape = (
        jax.ShapeDtypeStruct((N_CHIPS, CONV_W, D_MODEL // N_CHIPS), F32),
        jax.ShapeDtypeStruct((N_DEV, 1, D_MODEL), F32),
        jax.ShapeDtypeStruct((N_CHIPS, N_DEV, SHARD_ADA), F32),
    )
    return pl.pallas_call(
        body, out_shape=out_shape, name="gather_mod",
        in_specs=[VMEM_SPEC, VMEM_SPEC, ANY], out_specs=(ANY, VMEM_SPEC, VMEM_SPEC),
        scratch_shapes=[
            pltpu.SemaphoreType.DMA((3,)), pltpu.SemaphoreType.DMA((3,)), pltpu.SemaphoreType.DMA((1,)),
            pltpu.SemaphoreType.DMA((7,)), pltpu.SemaphoreType.DMA((7,)),
            pltpu.SemaphoreType.DMA((3,)), pltpu.SemaphoreType.DMA((3,)),
        ],
        compiler_params=pltpu.CompilerParams(vmem_limit_bytes=VMEM_LIMIT_V7X),
    )(c_row, w_ada_s, conv_w_s)


def _cast_place(shards, chip_idx, places):
    n = len(shards)

    def body(chip_ref, *refs):
        for s_ref, o_ref in zip(refs[:n], refs[n:]):
            o_ref[...] = s_ref[...].astype(BF16)

    grid_spec = pltpu.PrefetchScalarGridSpec(
        num_scalar_prefetch=1, grid=(1,),
        in_specs=[pl.BlockSpec(s.shape, lambda i, chip_ref, nd=s.ndim: (0,) * nd) for s in shards],
        out_specs=tuple(pl.BlockSpec(block, lambda i, chip_ref, im=im: im(chip_ref[0])) for _, block, im in places))
    return pl.pallas_call(
        body, out_shape=tuple(jax.ShapeDtypeStruct(full, BF16) for full, _, _ in places), grid_spec=grid_spec,
        name="cast_place", compiler_params=_cp("arbitrary"),
    )(chip_idx, *_in_hbm(*shards))


def _shard_of(ref, kind, chip):
    if kind == "in":
        return ref.at[:, pl.ds(pl.multiple_of(chip * SHARD_IN, 128), SHARD_IN)]
    return ref.at[chip] if kind == "sq" else ref.at[:, chip]


def _land_shape(src, kind):
    if kind == "in":
        return (3, src.shape[0], SHARD_IN)
    return (3,) + src.shape[1:] if kind == "sq" else (3, src.shape[0]) + src.shape[2:]


def _exchange_start(srcs, kinds, tag):
    n = len(srcs)
    lands = [pltpu.with_memory_space_constraint(lax.empty(_land_shape(s, k), s.dtype), pltpu.HBM) for s, k in zip(srcs, kinds)]

    def body(*refs):
        src_refs, land_refs = refs[:n], refs[n:2 * n]
        ssems, rsems = refs[2 * n:3 * n], refs[3 * n:4 * n]
        token = refs[6 * n]
        for i in range(n):
            for k, mask in enumerate(CHIP_MASKS):
                to = _peer(mask)
                pltpu.make_async_remote_copy(
                    src_ref=_shard_of(src_refs[i], kinds[i], _chip_of(to)), dst_ref=land_refs[i].at[k],
                    send_sem=ssems[i], recv_sem=rsems[i], device_id=to, device_id_type=MESH).start()
        token[...] = jnp.zeros_like(token)

    sem = pltpu.SemaphoreType.DMA(())
    out_shape = ((sem,) * (2 * n) + tuple(pltpu.HBM(s.shape, s.dtype) for s in srcs)
                 + tuple(pltpu.HBM(l.shape, l.dtype) for l in lands) + (jax.ShapeDtypeStruct((8, 128), F32),))
    outs = pl.pallas_call(
        body, out_shape=out_shape, name=f"exchange_start_{tag}",
        in_specs=[HBM_SPEC] * (2 * n), out_specs=tuple([SEM_SPEC] * (2 * n) + [HBM_SPEC] * (2 * n) + [VMEM_SPEC]),
        input_output_aliases={i: 2 * n + i for i in range(2 * n)},
        compiler_params=pltpu.CompilerParams(has_side_effects=pltpu.SideEffectType.DATAFLOW_SIDE_EFFECTING),
    )(*[pltpu.with_memory_space_constraint(s, pltpu.HBM) for s in srcs], *lands)
    return outs[:n], outs[n:2 * n], outs[2 * n:3 * n], outs[3 * n:4 * n], outs[4 * n]


def _exchange_wait(ssems, rsems, srcs, lands, after, tag):
    n = len(srcs)

    def body(*refs):
        land_refs = refs[n:2 * n]
        ssem_refs, rsem_refs = refs[2 * n:3 * n], refs[3 * n:4 * n]
        for i in range(n):
            all_three = pltpu.make_async_remote_copy(
                src_ref=land_refs[i], dst_ref=land_refs[i], send_sem=ssem_refs[i], recv_sem=rsem_refs[i],
                device_id=_me(), device_id_type=MESH)
            all_three.wait_send()
            all_three.wait_recv()

    outs = pl.pallas_call(
        body, out_shape=tuple(pltpu.HBM(a.shape, a.dtype) for a in list(srcs) + list(lands)), name=f"exchange_wait_{tag}",
        in_specs=[HBM_SPEC] * (2 * n) + [SEM_SPEC] * (2 * n) + [ANY], out_specs=tuple([HBM_SPEC] * (2 * n)),
        input_output_aliases={i: i for i in range(2 * n)},
        compiler_params=pltpu.CompilerParams(has_side_effects=pltpu.SideEffectType.DATAFLOW_SIDE_EFFECTING),
    )(*srcs, *lands, *ssems, *rsems, after)
    return outs[:n], outs[n:]


def _gather_small(small):
    def body(small_ref, small_all, ssend, srecv):
        me = _me()
        my_dev = _chip_of(me) * 2 + me[2]
        small_all[my_dev] = small_ref[...]
        ssends = []
        for k, mask in enumerate(ALL_MASKS):
            cp = pltpu.make_async_remote_copy(
                src_ref=small_ref, dst_ref=small_all.at[my_dev],
                send_sem=ssend.at[k], recv_sem=srecv.at[k], device_id=_peer(mask), device_id_type=MESH)
            cp.start()
            ssends.append(cp)
        for k, mask in enumerate(ALL_MASKS):
            frm = _peer(mask)
            pltpu.make_async_remote_copy(
                src_ref=small_ref, dst_ref=small_all.at[_chip_of(frm) * 2 + frm[2]],
                send_sem=ssend.at[k], recv_sem=srecv.at[k], device_id=frm, device_id_type=MESH).wait_recv()
        for cp in ssends:
            cp.wait_send()

    return pl.pallas_call(
        body, out_shape=jax.ShapeDtypeStruct((N_DEV, SMALL_ROWS, D_MODEL), F32), name="gather_small",
        in_specs=[VMEM_SPEC], out_specs=VMEM_SPEC,
        scratch_shapes=[pltpu.SemaphoreType.DMA((7,)), pltpu.SemaphoreType.DMA((7,))],
    )(small)


def _half_of(ref, axis, half):
    return ref.at[(slice(None),) * axis + (half,)]


def _swap_halves(parts, axes):
    n = len(parts)

    def body(*refs):
        ins, outs, ssem, rsem = refs[:n], refs[n:2 * n], refs[2 * n], refs[2 * n + 1]
        c = lax.axis_index("c")
        _sibling_handshake()
        cps = [pltpu.make_async_remote_copy(src_ref=_half_of(ins[i], axes[i], 1 - c), dst_ref=outs[i], send_sem=ssem.at[i],
                                            recv_sem=rsem.at[i], device_id=_peer(1), device_id_type=MESH) for i in range(n)]
        for cp in cps:
            cp.start()
        for cp in cps:
            cp.wait()

    shapes = [p.shape[:a] + p.shape[a + 1:] for p, a in zip(parts, axes)]
    return pl.pallas_call(
        body, out_shape=tuple(jax.ShapeDtypeStruct(s, p.dtype) for s, p in zip(shapes, parts)), name="swap_halves",
        in_specs=[ANY] * n, out_specs=tuple([ANY] * n),
        scratch_shapes=[pltpu.SemaphoreType.DMA((n,)), pltpu.SemaphoreType.DMA((n,))], compiler_params=SIBLING_ONLY,
    )(*parts)


def _presum(mines, sibs, c_idx, tag):
    n = len(mines)
    S, _, R, C = mines[0].shape
    tr = min(R, 256)
    tc = SHARD_IN if C % SHARD_IN == 0 else (C // 2 if n > 1 and C % 256 == 0 else C)

    def body(c_ref, *refs):
        for k in range(n):
            total = refs[k][:, 0] + refs[n + k][...]
            refs[2 * n + k][...] = total
            refs[3 * n + k][...] = total.astype(BF16)

    out_spec = pl.BlockSpec((S, tr, tc), lambda i, j, c_ref: (0, i, j))
    grid_spec = pltpu.PrefetchScalarGridSpec(
        num_scalar_prefetch=1, grid=(R // tr, C // tc),
        in_specs=[pl.BlockSpec((S, 1, tr, tc), lambda i, j, c_ref: (0, c_ref[0], i, j))] * n + [out_spec] * n,
        out_specs=(out_spec,) * (2 * n))
    outs = pl.pallas_call(
        body, out_shape=(jax.ShapeDtypeStruct((S, R, C), F32),) * n + (jax.ShapeDtypeStruct((S, R, C), BF16),) * n,
        grid_spec=grid_spec, name=f"presum_{tag}", compiler_params=_cp("parallel", "parallel"),
    )(c_idx, *mines, *sibs)
    return list(outs[:n]), list(outs[n:])


def _assemble_with_sibling(parts, axes):
    n = len(parts)

    def body(*refs):
        outs, ssem, rsem = refs[n:2 * n], refs[2 * n], refs[2 * n + 1]
        c = lax.axis_index("c")
        _sibling_handshake()
        cps = [pltpu.make_async_remote_copy(
            src_ref=_half_of(outs[i], axes[i], c), dst_ref=_half_of(outs[i], axes[i], c), send_sem=ssem.at[i],
            recv_sem=rsem.at[i], device_id=_peer(1), device_id_type=MESH) for i in range(n)]
        for cp in cps:
            cp.start()
        for i in range(n):
            pltpu.make_async_remote_copy(
                src_ref=_half_of(outs[i], axes[i], c), dst_ref=_half_of(outs[i], axes[i], 1 - c), send_sem=ssem.at[i],
                recv_sem=rsem.at[i], device_id=_peer(1), device_id_type=MESH).wait_recv()
        for cp in cps:
            cp.wait_send()

    return pl.pallas_call(
        body, out_shape=tuple(jax.ShapeDtypeStruct(p.shape, p.dtype) for p in parts), name="assemble_with_sibling",
        in_specs=[ANY] * n, out_specs=tuple([ANY] * n), input_output_aliases={i: i for i in range(n)},
        scratch_shapes=[pltpu.SemaphoreType.DMA((n,)), pltpu.SemaphoreType.DMA((n,))], compiler_params=SIBLING_ONLY,
    )(*parts)


def _rope_lane_frequencies():
    inv = np.float32(ROPE_THETA) ** (-(np.arange(0, 2 * ROT_HALF, 2, dtype=np.float32)) / np.float32(2 * ROT_HALF))
    lane = np.arange(128) % HEAD_DIM
    return jnp.asarray(np.where(lane < 2 * ROT_HALF, inv[lane % ROT_HALF], 0.0).astype(np.float32)[None, :])


def _rope_tables(pos, freq):
    ang = pos.astype(F32) * freq
    c, s = jnp.cos(ang), jnp.sin(ang)
    m = lax.broadcasted_iota(jnp.int32, ang.shape, 1) & (HEAD_DIM - 1)
    return (jnp.where(m < 2 * ROT_HALF, c, 1.0), jnp.where(m < ROT_HALF, -s, 0.0),
            jnp.where((m >= ROT_HALF) & (m < 2 * ROT_HALF), s, 0.0))


def _columns(t):
    return [t[:, i:i + 128] for i in range(0, t.shape[-1], 128)]


def _rope(t, c, sa, sb):
    return jnp.concatenate(
        [x * c + pltpu.roll(x, 128 - ROT_HALF, 1) * sa + pltpu.roll(x, ROT_HALF, 1) * sb for x in _columns(t)], axis=1)


def _unrope(d, c, sa, sb):
    return jnp.concatenate(
        [x * c + pltpu.roll(x * sa, ROT_HALF, 1) + pltpu.roll(x * sb, 128 - ROT_HALF, 1) for x in _columns(d)], axis=1)


def _prenorm(x, mod_row, norm_g, pos_col):
    T = x.shape[0]
    tm = min(T, 512)

    def body(x_ref, mod_ref, g_ref, pos_ref, f_ref, h_ref, ht_ref, c_ref, sa_ref, sb_ref):
        xf = x_ref[...]
        shift, scale = mod_ref[:, 0:D_MODEL], mod_ref[:, D_MODEL:2 * D_MODEL]
        h = (xf * _rms(xf)) * g_ref[...] * (1.0 + scale) + shift
        h_ref[...] = h.astype(BF16)
        ht_ref[...] = h.T.astype(BF16)
        c_ref[...], sa_ref[...], sb_ref[...] = _rope_tables(pos_ref[...], f_ref[...])

    tab = jax.ShapeDtypeStruct((T, 128), F32)
    tok = lambda w: pl.BlockSpec((tm, w), lambda i: (i, 0))
    row = lambda w: pl.BlockSpec((1, w), lambda i: (0, 0))
    outs = pl.pallas_call(
        body, out_shape=(jax.ShapeDtypeStruct((T, D_MODEL), BF16), jax.ShapeDtypeStruct((D_MODEL, T), BF16), tab, tab, tab),
        grid=(T // tm,), name="prenorm",
        in_specs=[tok(D_MODEL), row(ADA_W), row(D_MODEL), tok(1), row(128)],
        out_specs=(tok(D_MODEL), pl.BlockSpec((D_MODEL, tm), lambda i: (0, i)), tok(128), tok(128), tok(128)),
        compiler_params=_cp("parallel"),
    )(x, *_in_hbm(mod_row, norm_g), pos_col, _rope_lane_frequencies())
    return outs[0], outs[1], tuple(outs[2:])


def _in_projection(h, w_in, chips, into, tag):
    T = h.shape[0]
    tm, tn = min(T, 512), SHARD_IN
    k = chips.shape[0]

    def body(chip_ref, h_ref, w_ref, *rest):
        rest[-1][...] = _dot(h_ref[...], w_ref[...])

    w_spec = pl.BlockSpec((D_MODEL, tn), lambda s, i, c: (0, c[s]), **({"pipeline_mode": pl.Buffered(1)} if k == 1 else {}))
    in_specs = [pl.BlockSpec((tm, D_MODEL), lambda s, i, c: (i, 0)), w_spec]
    args = [chips, h, w_in]
    aliases = {}
    if into is not None:
        in_specs.append(ANY)
        args.append(into)
        aliases = {3: 0}
    grid_spec = pltpu.PrefetchScalarGridSpec(num_scalar_prefetch=1, grid=(k, T // tm), in_specs=in_specs,
                                             out_specs=pl.BlockSpec((tm, tn), lambda s, i, c: (i, c[s])))
    return pl.pallas_call(
        body, out_shape=jax.ShapeDtypeStruct((T, IN_W), F32), grid_spec=grid_spec, name=f"in_projection_{tag}",
        input_output_aliases=aliases, compiler_params=_cp("parallel", "parallel"),
    )(*args)


def _attn_mask(n):
    qi = lax.broadcasted_iota(jnp.int32, (GROUP * BLOCK, BLOCK), 0) & (BLOCK - 1)
    j = lax.broadcasted_iota(jnp.int32, (GROUP * BLOCK, BLOCK), 1)
    own = j <= qi
    return own, jnp.logical_not(own) & (n == 0)


def _fold(x, own):
    return jnp.where(own, x[:, BLOCK:2 * BLOCK], x[:, 0:BLOCK])


def _unfold(xf, own):
    zero = jnp.zeros_like(xf)
    return jnp.concatenate([jnp.where(own, zero, xf), jnp.where(own, xf, zero)], axis=1)


ROW_GROUP_HEAD = (0, 2, 1, 3)


def _sink_col(sink_ref, kh):
    rowg = lax.broadcasted_iota(jnp.int32, (GROUP * BLOCK, 1), 0) // BLOCK
    col = jnp.full((GROUP * BLOCK, 1), sink_ref[0, GROUP * kh + ROW_GROUP_HEAD[0]], F32)
    for g in range(1, GROUP):
        col = jnp.where(rowg == g, sink_ref[0, GROUP * kh + ROW_GROUP_HEAD[g]], col)
    return col


def _low_lanes(shape):
    return lax.broadcasted_iota(jnp.int32, shape, 1) < HEAD_DIM


def _kv_pair_operand(prev, cur, kh):
    c = 128 * (kh // 2)
    col = jnp.concatenate([prev[:, c:c + 128], cur[:, c:c + 128]], axis=0).astype(F32)
    if kh % 2 == 0:
        lo = jnp.where(_low_lanes(col.shape), col, 0.0)
        hi = pltpu.roll(lo, HEAD_DIM, 1)
    else:
        hi = jnp.where(_low_lanes(col.shape), 0.0, col)
        lo = pltpu.roll(hi, HEAD_DIM, 1)
    return jnp.concatenate([lo, hi], axis=0).astype(BF16)


def _pair_rows(x, kh):
    c = 2 * 128 * kh
    return jnp.concatenate([x[:, c:c + 128], x[:, c + 128:c + 256]], axis=0)


def _restack(big):
    return jnp.concatenate([big[:, 0:2 * BLOCK], big[:, 2 * BLOCK:4 * BLOCK]], axis=0)


def _unrestack(stacked):
    return jnp.concatenate([stacked[0:2 * BLOCK], stacked[2 * BLOCK:4 * BLOCK]], axis=1)


def _fold_pair(x2, kh):
    low = _low_lanes((2 * BLOCK, 128))
    mixed = jnp.where(low, x2[0:2 * BLOCK], x2[2 * BLOCK:4 * BLOCK])
    total = mixed + pltpu.roll(mixed, HEAD_DIM, 1)
    return jnp.where(low, total, 0.0) if kh % 2 == 0 else jnp.where(low, 0.0, total)


def _attn_scores(qr, k2, kh):
    q2 = _pair_rows(qr, kh).astype(BF16)
    return q2, _restack(_dot_nt(q2, k2))


def _attn_softmax(s, sink_col, mask):
    own, no_key = mask
    s = jnp.where(no_key, -1e30, _fold(s, own))
    m = jnp.maximum(jnp.max(s, axis=-1, keepdims=True), sink_col)
    p = jnp.exp(s - m)
    p_sink = jnp.exp(sink_col - m)
    denom = jnp.sum(p, axis=-1, keepdims=True) + p_sink
    return p / denom, p_sink / denom


def _attn_forward(proj, tabs, sinks):
    T = proj.shape[0]
    nb = T // BLOCK

    def body(q_ref, kvc_ref, kvp_ref, g0_ref, g1_ref, cc, sac, sbc, cp_, sap, sbp, sink_ref, y_ref, qrb_ref, krb_ref):
        n = pl.program_id(0)
        tc = tcur = (cc[...], sac[...], sbc[...])
        tprev = (cp_[...], sap[...], sbp[...])
        qr = _rope(q_ref[...], *tc) * ATTN_SCALE
        kr_cur = _rope(kvc_ref[:, 0:KV_W], *tcur)
        kr_prev = _rope(kvp_ref[:, 0:KV_W], *tprev)
        qrb_ref[...] = qr.astype(BF16)
        krb_ref[...] = kr_cur.astype(BF16)
        v_cur, v_prev = kvc_ref[:, KV_W:2 * KV_W], kvp_ref[:, KV_W:2 * KV_W]
        mask = _attn_mask(n)
        outs = []
        k2s = [_kv_pair_operand(kr_prev, kr_cur, kh) for kh in range(N_KV)]
        v2s = [_kv_pair_operand(v_prev, v_cur, kh) for kh in range(N_KV)]
        scores = [_attn_scores(qr, k2s[kh], kh) for kh in range(N_KV)]
        for kh in range(N_KV):
            pn, _ = _attn_softmax(scores[kh][1], _sink_col(sink_ref, kh), mask)
            o_big = _dot(_unrestack(_unfold(pn.astype(BF16), mask[0])), v2s[kh])
            outs += [o_big[0:BLOCK], o_big[BLOCK:2 * BLOCK]]
        o = jnp.concatenate(outs, axis=1)
        g = jnp.concatenate([g0_ref[...], g1_ref[...]], axis=1)
        y_ref[...] = (o * (g * _sigmoid(g))).astype(BF16)

    def blk(w, cb):
        return pl.BlockSpec((BLOCK, w), lambda n, cb=cb: (n, cb))

    prev = lambda w, cb: pl.BlockSpec((BLOCK, w), lambda n, cb=cb: (jnp.maximum(n - 1, 0), cb))
    return pl.pallas_call(
        body, grid=(nb,), name="attn_forward",
        out_shape=(jax.ShapeDtypeStruct((T, D_MODEL), BF16), jax.ShapeDtypeStruct((T, D_MODEL), BF16),
                   jax.ShapeDtypeStruct((T, KV_W), BF16)),
        in_specs=[blk(D_MODEL, 0), blk(CB, CB_KV), prev(CB, CB_KV), blk(CB, CB_GA), blk(CB, CB_GA + 1),
                  blk(128, 0), blk(128, 0), blk(128, 0), prev(128, 0), prev(128, 0), prev(128, 0),
                  pl.BlockSpec(memory_space=pltpu.SMEM)],
        out_specs=(blk(D_MODEL, 0), blk(D_MODEL, 0), blk(KV_W, 0)),
        compiler_params=_cp("parallel"),
    )(proj, proj, proj, proj, proj, *tabs, *tabs, sinks)


def _scan_rows8():
    return lax.broadcasted_iota(jnp.int32, (8, D_MODEL), 0)


def _scan_forward(a_ref, b_ref, h_ref, carry, rows):
    row = _scan_rows8()

    def group(i, carry):
        off = pl.multiple_of(i * 8, 8)
        a, b = a_ref[pl.ds(off, 8), :], b_ref[pl.ds(off, 8), :]
        for d in (1, 2, 4):
            ok = row >= d
            b = jnp.where(ok, a * pltpu.roll(b, d, 0) + b, b)
            a = jnp.where(ok, a * pltpu.roll(a, d, 0), a)
        h = a * carry + b
        h_ref[pl.ds(off, 8), :] = h
        return h[7:8, :]

    return lax.fori_loop(0, rows // 8, group, carry)


def _scan_backward(a_ref, g_ref, lam_ref, carry, rows):
    row = _scan_rows8()

    def group(i, carry):
        off = pl.multiple_of((rows // 8 - 1 - i) * 8, 8)
        a, g = a_ref[pl.ds(off, 8), :], g_ref[pl.ds(off, 8), :]
        b = a * g
        for d in (1, 2, 4):
            ok = row < 8 - d
            b = jnp.where(ok, a * pltpu.roll(b, 8 - d, 0) + b, b)
            a = jnp.where(ok, a * pltpu.roll(a, 8 - d, 0), a)
        mu = a * carry + b
        mu_below = jnp.where(row == 7, carry, pltpu.roll(mu, 7, 0))
        lam_ref[pl.ds(off, 8), :] = g + mu_below
        return mu[0:1, :]

    return lax.fori_loop(0, rows // 8, group, carry)


def _conv_taps(xbuf, xr, tail):
    rows = xr.shape[0]
    xbuf[0:8, :] = tail
    xbuf[8:rows + 8, :] = xr
    return [xbuf[pl.ds(8 - (CONV_W - 1 - k), rows), :] for k in range(CONV_W - 1)] + [xr]


def _rnn_gates(xbuf, xr, tail, cw, cb, wa_ref, wx_ref, ba, bx, sp, reset):
    xs = _conv_taps(xbuf, xr, tail)
    xc = xs[0] * cw[0:1, :]
    for k in range(1, CONV_W):
        xc = xc + xs[k] * cw[k:k + 1, :]
    xc = xc + cb
    xcb = xc.astype(BF16)
    za = jnp.concatenate([_dot(xcb[:, RNN_BW * j:RNN_BW * (j + 1)], wa_ref[j]) for j in range(RNN_BLOCKS)], axis=1) + ba
    zx = jnp.concatenate([_dot(xcb[:, RNN_BW * j:RNN_BW * (j + 1)], wx_ref[j]) for j in range(RNN_BLOCKS)], axis=1) + bx
    r, i = _sigmoid(za), _sigmoid(zx)
    neg_log_a = LRU_C * r * sp
    a_raw = jnp.exp(-neg_log_a)
    mult_raw = jnp.sqrt(jnp.tanh(neg_log_a) * (1.0 + a_raw * a_raw))
    a = jnp.where(reset, 0.0, a_raw)
    mult = jnp.where(reset, 1.0, mult_raw)
    return xc, r, i, a, mult


def _rnn_forward(proj, pos_col, conv_w, conv_b, rwa, rwx, ba, bx, lam):
    T = proj.shape[0]
    tr = min(T, 256)

    def body(x0, x1, g0, g1, pos_ref, cw_ref, cb_ref, wa_ref, wx_ref, ba_ref, bx_ref, lam_ref,
             y_ref, h_ref, xc_ref, r_ref, i_ref, a_ref, mult_ref, xbuf, bbuf, tail, carry):
        t = pl.program_id(0)

        @pl.when(t == 0)
        def _():
            tail[...] = jnp.zeros_like(tail)
            carry[...] = jnp.zeros_like(carry)

        xr = jnp.concatenate([x0[...], x1[...]], axis=1)
        sp = _softplus(-lam_ref[...])
        reset = pos_ref[...] == 0
        xc, r, i, a, mult = _rnn_gates(
            xbuf, xr, tail[...], cw_ref[...], cb_ref[...], wa_ref, wx_ref, ba_ref[...], bx_ref[...], sp, reset)
        xc_ref[...] = xc
        r_ref[...] = r
        i_ref[...] = i
        a_ref[...] = a
        mult_ref[...] = mult
        bbuf[...] = mult * (i * xc)
        last = _scan_forward(a_ref, bbuf, h_ref, carry[0:1, :], tr)
        carry[...] = jnp.broadcast_to(last, carry.shape)
        tail[...] = xr[tr - 8:tr, :]
        g = jnp.concatenate([g0[...], g1[...]], axis=1)
        y_ref[...] = (h_ref[...] * (g * _sigmoid(g))).astype(BF16)

    blk = lambda cb: pl.BlockSpec((tr, CB), lambda t, cb=cb: (t, cb))
    row = lambda w: pl.BlockSpec((1, w), lambda t: (0, 0))
    full3 = pl.BlockSpec((RNN_BLOCKS, RNN_BW, RNN_BW), lambda t: (0, 0, 0))
    tok = pl.BlockSpec((tr, D_MODEL), lambda t: (t, 0))
    act = jax.ShapeDtypeStruct((T, D_MODEL), F32)
    return pl.pallas_call(
        body, out_shape=(jax.ShapeDtypeStruct((T, D_MODEL), BF16),) + (act,) * 6,
        grid=(T // tr,), name="rnn_forward",
        in_specs=[blk(CB_XR), blk(CB_XR + 1), blk(CB_GR), blk(CB_GR + 1), pl.BlockSpec((tr, 1), lambda t: (t, 0)),
                  pl.BlockSpec((CONV_W, D_MODEL), lambda t: (0, 0)), row(D_MODEL), full3, full3,
                  row(D_MODEL), row(D_MODEL), row(D_MODEL)],
        out_specs=(tok,) * 7,
        scratch_shapes=[pltpu.VMEM((tr + 8, D_MODEL), F32), pltpu.VMEM((tr, D_MODEL), F32),
                        pltpu.VMEM((8, D_MODEL), F32), pltpu.VMEM((8, D_MODEL), F32)],
        compiler_params=_cp("arbitrary"),
    )(proj, proj, proj, proj, pos_col, *_in_hbm(conv_w, conv_b, rwa, rwx, ba, bx, lam))


ROW_PARTS = 1


def _merge_and_head(x, target, y_attn, y_rnn, proj, wap, wrp, wo, mod_row, final_g):
    T = x.shape[0]
    tm = min(T, 256)

    def body(x_ref, t_ref, ya_ref, yr_ref, ma0, ma1, mr0, mr1, wap_ref, wrp_ref, wo_ref, mod_ref, fg_ref,
             dx2_ref, mg_ref, do_ref, dpa_ref, dpr_ref, dya_ref, dyr_ref, dc_ref, dfg_ref, dgate_ref, loss_ref):
        i = pl.program_id(0)
        gate = mod_ref[:, 2 * D_MODEL:3 * D_MODEL]
        fg = fg_ref[...]
        parts = [slice(p * (tm // ROW_PARTS), (p + 1) * (tm // ROW_PARTS)) for p in range(ROW_PARTS)]
        each = range(ROW_PARTS)
        pa = [_dot(ya_ref[r, :], wap_ref[...]) for r in parts]
        pr = [_dot(yr_ref[r, :], wrp_ref[...]) for r in parts]
        sa = [_sigmoid(jnp.concatenate([ma0[r, :], ma1[r, :]], axis=1)) for r in parts]
        sr = [_sigmoid(jnp.concatenate([mr0[r, :], mr1[r, :]], axis=1)) for r in parts]
        mb = [(sa[p] * pa[p] + sr[p] * pr[p]).astype(BF16) for p in each]
        o = [_dot(mb[p], wo_ref[...]) for p in each]
        x2 = [x_ref[r, :] + gate * o[p] for p, r in enumerate(parts)]
        r2 = [_rms(v) for v in x2]
        xn2 = [x2[p] * r2[p] for p in each]
        err = [xn2[p] * fg - t_ref[r, :] for p, r in enumerate(parts)]
        dy = [e * (1.0 / D_MODEL) for e in err]
        dxn = [d * fg for d in dy]
        dx2 = [r2[p] * (dxn[p] - xn2[p] * jnp.mean(dxn[p] * xn2[p], axis=-1, keepdims=True)) for p in each]
        dob = [(dx2[p] * gate).astype(BF16) for p in each]
        dmerged = [_dot_nt(d, wo_ref[...]) for d in dob]
        dpa = [(dmerged[p] * sa[p]).astype(BF16) for p in each]
        dpr = [(dmerged[p] * sr[p]).astype(BF16) for p in each]
        dya = [_dot_nt(d, wap_ref[...]) for d in dpa]
        dyr = [_dot_nt(d, wrp_ref[...]) for d in dpr]
        loss_t, dfg_t, dgate_t = 0.0, 0.0, 0.0
        for p, r in enumerate(parts):
            dx2_ref[r, :] = dx2[p]
            mg_ref[r, :] = mb[p]
            do_ref[r, :] = dob[p]
            dpa_ref[r, :] = dpa[p]
            dpr_ref[r, :] = dpr[p]
            dya_ref[r, :] = dya[p]
            dyr_ref[r, :] = dyr[p]
            dc_ref[r, 0:D_MODEL] = (dmerged[p] * pa[p] * sa[p] * (1.0 - sa[p])).astype(BF16)
            dc_ref[r, D_MODEL:2 * D_MODEL] = (dmerged[p] * pr[p] * sr[p] * (1.0 - sr[p])).astype(BF16)
            loss_t = loss_t + 0.5 * jnp.sum(
                jnp.sum(err[p] * err[p], axis=-1, keepdims=True) * (1.0 / D_MODEL), axis=0, keepdims=True)
            dfg_t = dfg_t + jnp.sum(dy[p] * xn2[p], axis=0, keepdims=True)
            dgate_t = dgate_t + jnp.sum(dx2[p] * o[p], axis=0, keepdims=True)

        @pl.when(i == 0)
        def _():
            dfg_ref[...] = jnp.zeros_like(dfg_ref)
            dgate_ref[...] = jnp.zeros_like(dgate_ref)
            loss_ref[...] = jnp.zeros_like(loss_ref)

        dfg_ref[...] += dfg_t
        dgate_ref[...] += dgate_t
        loss_ref[...] += jnp.broadcast_to(loss_t, loss_ref.shape)

    tok = lambda w: pl.BlockSpec((tm, w), lambda i: (i, 0))
    blk = lambda cb: pl.BlockSpec((tm, CB), lambda i, cb=cb: (i, cb))
    wfull = pl.BlockSpec((D_MODEL, D_MODEL), lambda i: (0, 0), pipeline_mode=pl.Buffered(1))
    row = lambda w: pl.BlockSpec((1, w), lambda i: (0, 0))
    out_shape = (
        jax.ShapeDtypeStruct((T, D_MODEL), F32), jax.ShapeDtypeStruct((T, D_MODEL), BF16),
        jax.ShapeDtypeStruct((T, D_MODEL), BF16), jax.ShapeDtypeStruct((T, D_MODEL), BF16),
        jax.ShapeDtypeStruct((T, D_MODEL), BF16), jax.ShapeDtypeStruct((T, D_MODEL), F32),
        jax.ShapeDtypeStruct((T, D_MODEL), F32), jax.ShapeDtypeStruct((T, 2 * D_MODEL), BF16),
        jax.ShapeDtypeStruct((1, D_MODEL), F32), jax.ShapeDtypeStruct((1, D_MODEL), F32),
        jax.ShapeDtypeStruct((1, 128), F32),
    )
    return pl.pallas_call(
        body, out_shape=out_shape, grid=(T // tm,), name="merge_and_head",
        in_specs=[tok(D_MODEL), tok(D_MODEL), tok(D_MODEL), tok(D_MODEL), blk(CB_MA), blk(CB_MA + 1), blk(CB_MR),
                  blk(CB_MR + 1), wfull, wfull, wfull, row(ADA_W), row(D_MODEL)],
        out_specs=(tok(D_MODEL),) * 7 + (tok(2 * D_MODEL), row(D_MODEL), row(D_MODEL), row(128)),
        compiler_params=_cp("arbitrary"),
    )(x, target, y_attn, y_rnn, proj, proj, proj, proj, wap, wrp, wo, *_in_hbm(mod_row, final_g))


def _attn_backward(proj, qr_b, kr_b, d_y, tabs, sinks):
    T = proj.shape[0]
    nb = T // BLOCK

    def body(qrb_ref, krc_ref, krp_ref, vc_ref, vp_ref, g0_ref, g1_ref, dy_ref, cc, sac, sbc, cp_, sap, sbp, sink_ref,
             dq_ref, dkv_ref, dg_ref, dsink_ref, carry):
        n = pl.program_id(0)

        @pl.when(n == 0)
        def _():
            carry[...] = jnp.zeros_like(carry)
            dsink_ref[...] = jnp.zeros_like(dsink_ref)

        @pl.when(n < nb)
        def _():
            tc = tcur = (cc[...], sac[...], sbc[...])
            tprev = (cp_[...], sap[...], sbp[...])
            qr, kr_cur, kr_prev = qrb_ref[...], krc_ref[...], krp_ref[...]
            v_cur, v_prev = vc_ref[...], vp_ref[...]
            g = jnp.concatenate([g0_ref[...], g1_ref[...]], axis=1)
            sg = _sigmoid(g)
            dy = dy_ref[...]
            d_o = dy * (g * sg)
            mask = _attn_mask(n)
            lane = lax.broadcasted_iota(jnp.int32, (1, 128), 1)
            rowg = lax.broadcasted_iota(jnp.int32, (GROUP * BLOCK, 1), 0) // BLOCK
            o_parts, dq_parts = [], []
            dk_cols, dv_cols = [None, None], [None, None]
            dsink = jnp.zeros((1, 128), F32)
            heads = range(N_KV)
            k2s = [_kv_pair_operand(kr_prev, kr_cur, kh) for kh in heads]
            v2s = [_kv_pair_operand(v_prev, v_cur, kh) for kh in heads]
            scores = [_attn_scores(qr, k2s[kh], kh) for kh in heads]
            do2s = [_pair_rows(d_o, kh).astype(BF16) for kh in heads]
            dpns = [_fold(_restack(_dot_nt(do2s[kh], v2s[kh])), mask[0]) for kh in heads]
            probs = [_attn_softmax(scores[kh][1], _sink_col(sink_ref, kh), mask) for kh in heads]
            p_bigs = [_unrestack(_unfold(probs[kh][0].astype(BF16), mask[0])) for kh in heads]
            o_bigs = [_dot(p_bigs[kh], v2s[kh]) for kh in heads]
            dv2s = [_dot_tn(p_bigs[kh], do2s[kh]) for kh in heads]
            deltas = [jnp.sum(probs[kh][0] * dpns[kh], axis=-1, keepdims=True) for kh in heads]
            ds_bigs = [_unrestack(_unfold((probs[kh][0] * (dpns[kh] - deltas[kh])).astype(BF16), mask[0])) for kh in heads]
            dq2s = [_dot(ds_bigs[kh], k2s[kh]) for kh in heads]
            dk2s = [_dot_tn(ds_bigs[kh], scores[kh][0]) for kh in heads]
            for kh in heads:
                o_parts += [o_bigs[kh][0:BLOCK], o_bigs[kh][BLOCK:2 * BLOCK]]
                dq_parts += [dq2s[kh][0:BLOCK], dq2s[kh][BLOCK:2 * BLOCK]]
                dk_c, dv_c = _fold_pair(dk2s[kh], kh), _fold_pair(dv2s[kh], kh)
                c = kh // 2
                dk_cols[c] = dk_c if dk_cols[c] is None else dk_cols[c] + dk_c
                dv_cols[c] = dv_c if dv_cols[c] is None else dv_cols[c] + dv_c
                ds_rows = probs[kh][1] * deltas[kh]
                for gq in range(GROUP):
                    val = -jnp.sum(jnp.where(rowg == gq, ds_rows, 0.0), axis=0, keepdims=True)
                    dsink = dsink + jnp.where(lane == GROUP * kh + ROW_GROUP_HEAD[gq], val, 0.0)
            o = jnp.concatenate(o_parts, axis=1)
            dg_ref[...] = (dy * o * (sg * (1.0 + g * (1.0 - sg)))).astype(BF16)
            dq_ref[...] = (_unrope(jnp.concatenate(dq_parts, axis=1), *tc) * ATTN_SCALE).astype(BF16)
            dk_all, dv_all = jnp.concatenate(dk_cols, axis=1), jnp.concatenate(dv_cols, axis=1)
            dk_prev = _unrope(dk_all[0:BLOCK], *tprev)
            dk_cur = _unrope(dk_all[BLOCK:2 * BLOCK], *tcur)
            dv_prev, dv_cur = dv_all[0:BLOCK], dv_all[BLOCK:2 * BLOCK]
            dkv_ref[...] = (carry[...] + jnp.concatenate([dk_prev, dv_prev], axis=1)).astype(BF16)
            carry[...] = jnp.concatenate([dk_cur, dv_cur], axis=1)
            dsink_ref[...] += dsink

        @pl.when(n == nb)
        def _():
            dkv_ref[...] = carry[...].astype(BF16)

    cur = lambda w, cb: pl.BlockSpec((BLOCK, w), lambda n, cb=cb: (jnp.minimum(n, nb - 1), cb))
    prev = lambda w, cb: pl.BlockSpec((BLOCK, w), lambda n, cb=cb: (jnp.maximum(jnp.minimum(n, nb - 1) - 1, 0), cb))
    out_shape = (jax.ShapeDtypeStruct((T, D_MODEL), BF16), jax.ShapeDtypeStruct((T, 2 * KV_W), BF16),
                 jax.ShapeDtypeStruct((T, D_MODEL), BF16), jax.ShapeDtypeStruct((1, 128), F32))
    return pl.pallas_call(
        body, out_shape=out_shape, grid=(nb + 1,), name="attn_backward",
        in_specs=[cur(D_MODEL, 0), cur(KV_W, 0), prev(KV_W, 0), cur(KV_W, V_COL_BLOCK), prev(KV_W, V_COL_BLOCK),
                  cur(CB, CB_GA), cur(CB, CB_GA + 1), cur(D_MODEL, 0),
                  cur(128, 0), cur(128, 0), cur(128, 0), prev(128, 0), prev(128, 0), prev(128, 0),
                  pl.BlockSpec(memory_space=pltpu.SMEM)],
        out_specs=(cur(D_MODEL, 0), pl.BlockSpec((BLOCK, 2 * KV_W), lambda n: (jnp.maximum(n - 1, 0), 0)),
                   cur(D_MODEL, 0), pl.BlockSpec((1, 128), lambda n: (0, 0))),
        scratch_shapes=[pltpu.VMEM((BLOCK, 2 * KV_W), F32)],
        compiler_params=_cp("arbitrary"),
    )(qr_b, kr_b, kr_b, proj, proj, proj, proj, d_y, *tabs, *tabs, sinks)


def _rnn_backward(proj, pos_col, h_rnn, saved, d_y, conv_w, rwa, rwx, lam):
    T = proj.shape[0]
    tr = min(T, 256)
    nt = T // tr
    hb = tr // 8

    def body(x0, x1, xh0, xh1, g0, g1, pos_ref, h_ref, hh_ref, xc_ref, r_ref, i_ref, a_ref, mult_ref, dy_ref,
             cw_ref, wa_ref, wx_ref, lam_ref, db_ref, dcw_ref, dcb_ref, dwa_ref, dwx_ref, dba_ref, dbx_ref, dlam_ref,
             xbuf, hbuf, dbuf, gbuf, lbuf, mu_carry, dxc_head):
        step = pl.program_id(0)
        first_tile = step == nt - 1

        @pl.when(step == 0)
        def _():
            mu_carry[...] = jnp.zeros_like(mu_carry)
            dxc_head[...] = jnp.zeros_like(dxc_head)
            for ref in (dcw_ref, dcb_ref, dwa_ref, dwx_ref, dba_ref, dbx_ref, dlam_ref):
                ref[...] = jnp.zeros_like(ref)

        xr = jnp.concatenate([x0[...], x1[...]], axis=1)
        tail = jnp.where(first_tile, 0.0, jnp.concatenate([xh0[...], xh1[...]], axis=1))
        lam_v = lam_ref[...]
        sp = _softplus(-lam_v)
        reset = pos_ref[...] == 0
        cw = cw_ref[...]
        xbuf[0:8, :] = tail
        xbuf[8:tr + 8, :] = xr
        g = jnp.concatenate([g0[...], g1[...]], axis=1)
        sg = _sigmoid(g)
        dy = dy_ref[...]
        h = h_ref[...]
        db_ref[:, D_MODEL:2 * D_MODEL] = (dy * h * (sg * (1.0 + g * (1.0 - sg)))).astype(BF16)
        gbuf[...] = dy * (g * sg)
        top = _scan_backward(a_ref, gbuf, lbuf, mu_carry[0:1, :], tr)
        mu_carry[...] = jnp.broadcast_to(top, mu_carry.shape)
        hbuf[0:8, :] = jnp.where(first_tile, 0.0, hh_ref[...])
        hbuf[8:tr + 8, :] = h
        live = jnp.logical_not(reset)
        dbuf[tr:tr + 8, :] = dxc_head[...]
        for j in range(RNN_BLOCKS):
            sl = slice(RNN_BW * j, RNN_BW * (j + 1))
            lam_t, h_prev = lbuf[:, sl], hbuf[pl.ds(7, tr), sl]
            xc, r, i, a, mult = xc_ref[:, sl], r_ref[:, sl], i_ref[:, sl], a_ref[:, sl], mult_ref[:, sl]
            d_a = jnp.where(live, lam_t * h_prev, 0.0)
            d_mult = jnp.where(live, lam_t * (i * xc), 0.0)
            d_ixc = lam_t * mult
            d_i = d_ixc * xc
            d_log_a = d_a * a - d_mult * (a * a / mult)
            d_za = d_log_a * (-LRU_C * sp[:, sl]) * (r * (1.0 - r))
            d_zx = d_i * (i * (1.0 - i))
            dlam_ref[:, sl] += jnp.sum(d_log_a * r, axis=0, keepdims=True) * (LRU_C * _sigmoid(-lam_v[:, sl]))
            dba_ref[:, sl] += jnp.sum(d_za, axis=0, keepdims=True)
            dbx_ref[:, sl] += jnp.sum(d_zx, axis=0, keepdims=True)
            xcb, dzab, dzxb = xc.astype(BF16), d_za.astype(BF16), d_zx.astype(BF16)
            dwa_ref[j] += _dot_tn(xcb, dzab)
            dwx_ref[j] += _dot_tn(xcb, dzxb)
            d_xc = d_ixc * i + (_dot_nt(dzab, wa_ref[j]) + _dot_nt(dzxb, wx_ref[j]))
            dcb_ref[:, sl] += jnp.sum(d_xc, axis=0, keepdims=True)
            for k in range(CONV_W):
                tap = xr[:, sl] if k == CONV_W - 1 else xbuf[pl.ds(8 - (CONV_W - 1 - k), tr), sl]
                dcw_ref[k:k + 1, sl] += jnp.sum(d_xc * tap, axis=0, keepdims=True)
            dbuf[0:tr, sl] = d_xc
            d_xr = d_xc * cw[CONV_W - 1:CONV_W, sl]
            for k in range(CONV_W - 1):
                d_xr = d_xr + dbuf[pl.ds(CONV_W - 1 - k, tr), sl] * cw[k:k + 1, sl]
            dxc_head[:, sl] = d_xc[0:8, :]
            db_ref[:, sl] = d_xr.astype(BF16)

    rev = lambda s: nt - 1 - s
    blk = lambda cb: pl.BlockSpec((tr, CB), lambda s, cb=cb: (rev(s), cb))
    halo = lambda w, cb: pl.BlockSpec((8, w), lambda s, cb=cb: (jnp.maximum(rev(s) * hb - 1, 0), cb))
    tok = lambda w: pl.BlockSpec((tr, w), lambda s: (rev(s), 0))
    row = lambda w: pl.BlockSpec((1, w), lambda s: (0, 0))
    full3 = pl.BlockSpec((RNN_BLOCKS, RNN_BW, RNN_BW), lambda s: (0, 0, 0))
    cwspec = pl.BlockSpec((CONV_W, D_MODEL), lambda s: (0, 0))
    vec = jax.ShapeDtypeStruct((1, D_MODEL), F32)
    gate_w = jax.ShapeDtypeStruct((RNN_BLOCKS, RNN_BW, RNN_BW), F32)
    out_shape = (jax.ShapeDtypeStruct((T, 2 * D_MODEL), BF16), jax.ShapeDtypeStruct((CONV_W, D_MODEL), F32), vec,
                 gate_w, gate_w, vec, vec, vec)
    big = lambda: pltpu.VMEM((tr, D_MODEL), F32)
    ext = lambda: pltpu.VMEM((tr + 8, D_MODEL), F32)
    return pl.pallas_call(
        body, out_shape=out_shape, grid=(nt,), name="rnn_backward",
        in_specs=[blk(CB_XR), blk(CB_XR + 1), halo(CB, CB_XR), halo(CB, CB_XR + 1), blk(CB_GR), blk(CB_GR + 1),
                  pl.BlockSpec((tr, 1), lambda s: (rev(s), 0)), tok(D_MODEL), halo(D_MODEL, 0)] + [tok(D_MODEL)] * 6
        + [cwspec, full3, full3, row(D_MODEL)],
        out_specs=(tok(2 * D_MODEL), cwspec, row(D_MODEL), full3, full3, row(D_MODEL), row(D_MODEL), row(D_MODEL)),
        scratch_shapes=[ext(), ext(), ext(), big(), big(), pltpu.VMEM((8, D_MODEL), F32), pltpu.VMEM((8, D_MODEL), F32)],
        compiler_params=_cp("arbitrary"),
    )(proj, proj, proj, proj, proj, proj, pos_col, h_rnn, h_rnn, *saved, d_y, *_in_hbm(conv_w, rwa, rwx, lam))


def _input_backward(pieces, w_in, x, dx2, mod_row, norm_g):
    T = x.shape[0]
    tm = min(T, 512)
    n = len(pieces)

    def body(*refs):
        d_refs = refs[:n]
        w_ref, x_ref, dx2_ref, mod_ref, g_ref, gx_ref, dshift_ref, dscale_ref, dg_ref = refs[n:]
        i = pl.program_id(0)
        dh = None
        for d_ref, (_, start, count) in zip(d_refs, pieces):
            part = _dot_nt(d_ref[...], w_ref[:, start * CB:(start + count) * CB])
            dh = part if dh is None else dh + part

        @pl.when(i == 0)
        def _():
            dshift_ref[...] = jnp.zeros_like(dshift_ref)
            dscale_ref[...] = jnp.zeros_like(dscale_ref)
            dg_ref[...] = jnp.zeros_like(dg_ref)

        xf = x_ref[...]
        r1 = _rms(xf)
        xn = xf * r1
        gn = g_ref[...]
        s1 = 1.0 + mod_ref[:, D_MODEL:2 * D_MODEL]
        dshift_ref[...] += jnp.sum(dh, axis=0, keepdims=True)
        dscale_ref[...] += jnp.sum(dh * (xn * gn), axis=0, keepdims=True)
        dg_ref[...] += jnp.sum(dh * s1 * xn, axis=0, keepdims=True)
        dxn = dh * s1 * gn
        gx_ref[...] = dx2_ref[...] + r1 * (dxn - xn * jnp.mean(dxn * xn, axis=-1, keepdims=True))

    tok = lambda w: pl.BlockSpec((tm, w), lambda i: (i, 0))
    row = lambda w: pl.BlockSpec((1, w), lambda i: (0, 0))
    vec = jax.ShapeDtypeStruct((1, D_MODEL), F32)
    return pl.pallas_call(
        body, out_shape=(jax.ShapeDtypeStruct((T, D_MODEL), F32), vec, vec, vec), grid=(T // tm,), name="input_backward",
        in_specs=[tok(c * CB) for _, _, c in pieces]
        + [pl.BlockSpec((D_MODEL, IN_W), lambda i: (0, 0), pipeline_mode=pl.Buffered(1)), tok(D_MODEL), tok(D_MODEL),
           row(ADA_W), row(D_MODEL)],
        out_specs=(tok(D_MODEL), row(D_MODEL), row(D_MODEL), row(D_MODEL)),
        compiler_params=_cp("arbitrary"),
    )(*[p[0] for p in pieces], w_in, x, dx2, *_in_hbm(mod_row, norm_g))


def _weight_grad(a, pieces, tag, a_is_transposed=False):
    M, T = a.shape if a_is_transposed else a.shape[::-1]
    n_blocks = sum(count for _, _, count in pieces)
    n = len(pieces)
    contract = _dot if a_is_transposed else _dot_tn

    def body(*refs):
        a_ref, b_refs, o_ref = refs[0], refs[1:1 + n], refs[-1]
        j = pl.program_id(0)
        for b_ref, (_, start, count) in zip(b_refs, pieces):
            @pl.when((j >= start) & (j < start + count))
            def _(b_ref=b_ref):
                o_ref[...] = contract(a_ref[...], b_ref[...])

    def piece_spec(start, count):
        return pl.BlockSpec((T, CB), lambda j: (0, jnp.clip(j - start, 0, count - 1)))

    return pl.pallas_call(
        body, out_shape=jax.ShapeDtypeStruct((M, n_blocks * CB), F32), grid=(n_blocks,), name=f"weight_grad_{tag}",
        in_specs=[pl.BlockSpec(a.shape, lambda j: (0, 0), pipeline_mode=pl.Buffered(1))] + [piece_spec(s, c) for _, s, c in pieces],
        out_specs=pl.BlockSpec((M, CB), lambda j: (0, j)), compiler_params=_cp("arbitrary"),
    )(a, *[p[0] for p in pieces])


def _adamw(w, g, m, v):
    m = ADAM_B1 * m + (1.0 - ADAM_B1) * g
    v = ADAM_B2 * v + (1.0 - ADAM_B2) * (g * g)
    m_hat = m / (1.0 - ADAM_B1 ** ADAM_STEP)
    v_hat = v / (1.0 - ADAM_B2 ** ADAM_STEP)
    delta = -ADAM_LR * (m_hat / (jnp.sqrt(v_hat) + ADAM_EPS) + ADAM_WD * w)
    return delta, m, v


def _sum_landed(kind, owns, lands, where, tag):
    n = len(owns)
    land = lands[0]
    if kind == "in":
        R, C = land.shape[1:]
        tr = 256
        grid = (R // tr,)
        own_spec = pl.BlockSpec((tr, C), lambda i, w: (i, w[0]))
        land_spec = pl.BlockSpec((3, tr, C), lambda i, w: (0, i, 0))
        out_spec = pl.BlockSpec((1, tr, C), lambda i, w: (w[1], i, 0))
        out_shape = (2, R, C)
        pick = lambda ref: ref[...]
    elif kind == "sq":
        R, C = land.shape[1:]
        grid = (1,)
        own_spec = pl.BlockSpec((1, R, C), lambda i, w: (w[0], 0, 0))
        land_spec = pl.BlockSpec((3, R, C), lambda i, w: (0, 0, 0))
        out_spec = pl.BlockSpec((1, R, C), lambda i, w: (w[1], 0, 0))
        out_shape = (2, R, C)
        pick = lambda ref: ref[0]
    else:
        B, R, C = land.shape[1:]
        grid = (1,)
        own_spec = pl.BlockSpec((B, 1, R, C), lambda i, w: (0, w[0], 0, 0))
        land_spec = pl.BlockSpec((3, B, R, C), lambda i, w: (0, 0, 0, 0))
        out_spec = pl.BlockSpec((B, 1, R, C), lambda i, w: (0, w[1], 0, 0))
        out_shape = (B, 2, R, C)
        pick = lambda ref: ref[:, 0]

    def body(w_ref, *refs):
        for k in range(n):
            own_ref, l_ref, o_ref = refs[k], refs[n + k], refs[2 * n + k]
            total = ((pick(own_ref) + l_ref[0].astype(F32)) + l_ref[1].astype(F32)) + l_ref[2].astype(F32)
            if kind == "rg":
                o_ref[:, 0] = total
            else:
                o_ref[0] = total

    grid_spec = pltpu.PrefetchScalarGridSpec(num_scalar_prefetch=1, grid=grid, in_specs=[own_spec] * n + [land_spec] * n,
                                             out_specs=(out_spec,) * n)
    return list(pl.pallas_call(
        body, out_shape=(jax.ShapeDtypeStruct(out_shape, F32),) * n, grid_spec=grid_spec, name=f"sum_landed_{tag}",
        compiler_params=_cp("parallel"),
    )(where, *owns, *lands))


def _adamw_shard(gs, ws, ms, vs, tag):
    n = len(ws)
    R, C = ws[0].shape
    tr = min(R, 256 if n == 1 else 64)

    def body(*refs):
        for k in range(n):
            g = refs[k][...]
            d, nm, nv = _adamw(refs[n + k][...], g, refs[2 * n + k][...], refs[3 * n + k][...])
            out = refs[4 * n + 4 * k:4 * n + 4 * k + 4]
            out[0][...] = g
            out[1][...] = d
            out[2][...] = nm
            out[3][...] = nv

    spec = pl.BlockSpec((tr, C), lambda i: (i, 0))
    sds = jax.ShapeDtypeStruct((R, C), F32)
    outs = pl.pallas_call(
        body, out_shape=(sds,) * (4 * n), grid=(R // tr,), name=f"adamw_{tag}",
        in_specs=[spec] * (4 * n), out_specs=(spec,) * (4 * n), compiler_params=_cp("parallel"),
    )(*gs, *_in_hbm(*ws, *ms, *vs))
    return [outs[4 * k:4 * k + 4] for k in range(n)]


def _adamw_w_ada(c_t, dmod_cols, w, m, v):
    R, C = w.shape

    def body(ct_ref, dm_ref, w_ref, m_ref, v_ref, g_ref, d_ref, nm_ref, nv_ref):
        g = _dot(ct_ref[...].astype(BF16), dm_ref[...].astype(BF16))
        d, nm, nv = _adamw(w_ref[...], g, m_ref[...], v_ref[...])
        g_ref[...] = g
        d_ref[...] = d
        nm_ref[...] = nm
        nv_ref[...] = nv

    tr = 256
    spec = pl.BlockSpec((tr, C), lambda i: (i, 0))
    sds = jax.ShapeDtypeStruct((R, C), F32)
    return pl.pallas_call(
        body, out_shape=(sds,) * 4, grid=(R // tr,), name="adamw_w_ada",
        in_specs=[pl.BlockSpec((tr, 128), lambda i: (i, 0)), pl.BlockSpec((128, C), lambda i: (0, 0))] + [spec] * 3,
        out_specs=(spec,) * 4, compiler_params=_cp("parallel"),
    )(c_t, dmod_cols, w, m, v)


def _adamw_small(small_all, ws, ms, vs):
    def body(s_ref, w_ref, m_ref, v_ref, g_ref, d_ref, nm_ref, nv_ref):
        g = s_ref[0]
        for b in range(1, N_DEV):
            g = g + s_ref[b]
        d, nm, nv = _adamw(w_ref[...], g, m_ref[...], v_ref[...])
        g_ref[...] = g
        d_ref[...] = d
        nm_ref[...] = nm
        nv_ref[...] = nv

    sds = jax.ShapeDtypeStruct((SMALL_ROWS, D_MODEL), F32)
    return pl.pallas_call(
        body, out_shape=(sds,) * 4, name="adamw_small", in_specs=[VMEM_SPEC] * 4, out_specs=(VMEM_SPEC,) * 4,
        compiler_params=pltpu.CompilerParams(vmem_limit_bytes=VMEM_LIMIT_V7X),
    )(small_all, ws, ms, vs)


ROW_MOD, ROW_NORM_G, ROW_CONV_B, ROW_BA, ROW_BX, ROW_LAM, ROW_FINAL_G, ROW_SINKS, ROW_CONV_W, ROW_LOSS = 0, 3, 4, 5, 6, 7, 8, 9, 10, 14


def _pack_small(b_ada, norm_g, conv_b, ba, bx, lam, final_g, sinks, conv_w_full, loss_row=None):
    lane_pad = lambda a: jnp.pad(a.reshape(1, -1), ((0, 0), (0, D_MODEL - a.size)))
    rows = [b_ada.reshape(3, D_MODEL), norm_g, conv_b, ba, bx, lam, final_g.reshape(1, D_MODEL), lane_pad(sinks), conv_w_full,
            jnp.zeros((1, D_MODEL), F32) if loss_row is None else lane_pad(loss_row),
            jnp.zeros((SMALL_ROWS - ROW_LOSS - 1, D_MODEL), F32)]
    return jnp.concatenate([r.astype(F32) for r in rows], axis=0)


def kernel(x, c, positions, w_ada, b_ada, norm_g, w_in, attn_sinks, conv_w, conv_b, rg_wa, rg_ba, rg_wx, rg_bx, rg_lambda, w_attn_proj, w_rnn_proj, w_out, final_g, loss_target, m_w_ada, m_b_ada, m_norm_g, m_w_in, m_attn_sinks, m_conv_w, m_conv_b, m_rg_wa, m_rg_ba, m_rg_wx, m_rg_bx, m_rg_lambda, m_w_attn_proj, m_w_rnn_proj, m_w_out, m_final_g, v_w_ada, v_b_ada, v_norm_g, v_w_in, v_attn_sinks, v_conv_w, v_conv_b, v_rg_wa, v_rg_ba, v_rg_wx, v_rg_bx, v_rg_lambda, v_w_attn_proj, v_w_rnn_proj, v_w_out, v_final_g):
    T = x.shape[1]
    my_chip = lax.axis_index("x") * 2 + lax.axis_index("y")
    my_dev = my_chip * 2 + lax.axis_index("c")
    x2d, tgt = x[0], loss_target[0]
    pos_col = positions.reshape(T, 1)

    chip_idx = my_chip.reshape(1).astype(jnp.int32)
    c_idx = lax.axis_index("c").reshape(1).astype(jnp.int32)
    sq_place = ((D_MODEL, D_MODEL), (SHARD_ROWS, D_MODEL), lambda chip: (chip, 0))
    rg_place = ((RNN_BLOCKS, RNN_BW, RNN_BW), (RNN_BLOCKS, SHARD_RG, RNN_BW), lambda chip: (0, chip, 0))
    in_place = ((D_MODEL, IN_W), (D_MODEL, SHARD_IN), lambda chip: (0, chip))
    placed = _cast_place([w_in[0], w_attn_proj[0], w_rnn_proj[0], w_out[0], rg_wa[0], rg_wx[0]], chip_idx,
                         [in_place, sq_place, sq_place, sq_place, rg_place, rg_place])
    cw_chips, c_all, mod_chips = _gather_mod(c.reshape(1, 1, D_MODEL), w_ada[0], conv_w[0])
    g_ssems, g_rsems, fulls, g_token = _gather_start([p.reshape(s) for p, s in zip(placed, FULL_SHAPES)], mod_chips)
    conv_w_f = jnp.transpose(cw_chips, (1, 0, 2)).reshape(CONV_W, D_MODEL)
    mod_all = jnp.transpose(mod_chips, (1, 0, 2)).reshape(N_DEV, ADA_W) + b_ada
    mod_row = lax.dynamic_slice_in_dim(mod_all, my_dev, 1, axis=0) + g_token[0:1, 0:1]

    h, h_t, tabs = _prenorm(x2d, mod_row, norm_g, pos_col)
    w_in_v = fulls[0]
    proj = _in_projection(h, w_in_v.reshape(D_MODEL, IN_W), chip_idx, None, "own")
    for k, mask in enumerate(CHIP_MASKS):
        w_in_v = _gather_wait(g_ssems[k], g_rsems[k], [w_in_v], [0], proj, f"w_in_{k}")[0]
        w_in_v = _forward_halves([w_in_v], [(0, 0, k)], f"w_in_{k}")[0]
        from_chip = (chip_idx ^ (mask >> 1)).astype(jnp.int32)
        proj = _in_projection(h, w_in_v.reshape(D_MODEL, IN_W), from_chip, proj, f"from_{k}")
    w_in_f = w_in_v.reshape(D_MODEL, IN_W)
    rest = _gather_wait(g_ssems[3], g_rsems[3], list(fulls[1:]), [1, 2, 3, 4, 5], proj, "rest")
    rest = _forward_halves(rest, [(idx - 1, idx, k) for idx in range(1, N_BIG) for k in range(3)], "rest")
    wap_f, wrp_f, wo_f = (g.reshape(D_MODEL, D_MODEL) for g in rest[0:3])
    rwa_f, rwx_f = (g.reshape(RNN_BLOCKS, RNN_BW, RNN_BW) for g in rest[3:5])
    y_attn, qr_b, kr_b = _attn_forward(proj, tabs, attn_sinks)
    y_rnn, h_rnn, *rnn_saved = _rnn_forward(proj, pos_col, conv_w_f, conv_b, rwa_f, rwx_f, rg_ba, rg_bx, rg_lambda)
    (dx2, merged, d_o, d_pa, d_pr, d_ya, d_yr, d_c, d_final_g, d_gate, loss_vec) = _merge_and_head(
        x2d, tgt, y_attn, y_rnn, proj, wap_f, wrp_f, wo_f, mod_row, final_g.reshape(1, D_MODEL))

    sq = (N_CHIPS, 2, SHARD_ROWS // 2, D_MODEL)
    rg = (RNN_BLOCKS, N_CHIPS, 2, SHARD_RG // 2, RNN_BW)
    rg_flat = (RNN_BLOCKS * N_CHIPS, 2, SHARD_RG // 2, RNN_BW)

    def chip_sum_and_start(views, axes, flat, unflat, tags_, kinds_, group):
        from_sib = _swap_halves(views, axes)
        exact, rounded = [None] * len(views), [None] * len(views)
        for shape in dict.fromkeys(flat):
            ids = [k for k, f in enumerate(flat) if f == shape]
            ex, ro = _presum([views[k].reshape(shape) for k in ids],
                             [from_sib[k].reshape(shape[:1] + shape[2:]) for k in ids], c_idx, tags_[ids[0]])
            for k, e, r in zip(ids, ex, ro):
                exact[k], rounded[k] = e.reshape(unflat[k]), r.reshape(unflat[k])
        return _exchange_start(rounded, kinds_, group), exact

    g_ap = _weight_grad(y_attn, [(d_pa, 0, 2)], "w_attn_proj")
    g_rp = _weight_grad(y_rnn, [(d_pr, 0, 2)], "w_rnn_proj")
    g_o = _weight_grad(merged, [(d_o, 0, 2)], "w_out")
    sq_half = (N_CHIPS, SHARD_ROWS // 2, D_MODEL)
    started1, own1 = chip_sum_and_start([g_ap.reshape(sq), g_rp.reshape(sq), g_o.reshape(sq)], [1, 1, 1], [sq] * 3, [sq_half] * 3,
                                  ["w_attn_proj", "w_rnn_proj", "w_out"], ["sq"] * 3, "proj")
    d_q, d_kv, d_ga, d_sinks = _attn_backward(proj, qr_b, kr_b, d_ya, tabs, attn_sinks + started1[4][0, 0])
    d_b, d_conv_w, d_conv_b, d_rwa, d_rwx, d_ba, d_bx, d_lam = _rnn_backward(
        proj, pos_col, h_rnn, rnn_saved, d_yr, conv_w_f, rwa_f, rwx_f, rg_lambda)
    pieces = [(d_q, CB_Q, 2), (d_kv, CB_KV, 1), (d_ga, CB_GA, 2), (d_b, CB_XR, 4), (d_c, CB_MA, 4)]
    g_in = _weight_grad(h_t, pieces, "w_in", a_is_transposed=True)
    started2, own2 = chip_sum_and_start(
        [g_in.reshape(2, D_MODEL // 2, IN_W), d_rwa.reshape(rg), d_rwx.reshape(rg)], [0, 2, 2],
        [(1, 2, D_MODEL // 2, IN_W), rg_flat, rg_flat],
        [(D_MODEL // 2, IN_W), (RNN_BLOCKS, N_CHIPS, SHARD_RG // 2, RNN_BW), (RNN_BLOCKS, N_CHIPS, SHARD_RG // 2, RNN_BW)],
        ["w_in", "rg_wa", "rg_wx"], ["in", "rg", "rg"], "in")
    grad_x, d_shift, d_scale, d_norm_g = _input_backward(pieces, w_in_f, x2d, dx2, mod_row + started2[4][0, 0], norm_g)

    d_mod = jnp.concatenate([d_shift, d_scale, d_gate], axis=1)
    small = _pack_small(d_mod, d_norm_g, d_conv_b, d_ba, d_bx, d_lam, d_final_g, d_sinks[:, :N_HEADS], d_conv_w, loss_vec)
    small_all = _gather_small(small)
    _, lands1 = _exchange_wait(*started1[:4], grad_x, "proj")
    _, lands2 = _exchange_wait(*started2[:4], grad_x, "in")
    tags = ["w_in", "w_attn_proj", "w_rnn_proj", "w_out", "rg_wa", "rg_wx"]
    chip_sums = [own2[0]] + list(own1) + list(own2[1:])
    lands = [lands2[0]] + list(lands1) + list(lands2[1:])
    where = jnp.concatenate([chip_idx, c_idx])
    kinds = ["in", "sq", "sq", "sq", "rg", "rg"]
    groups = [[0], [1, 2, 3], [4, 5]]
    halves = [None] * 6
    for ids in groups:
        for i, half in zip(ids, _sum_landed(kinds[ids[0]], [chip_sums[i] for i in ids], [lands[i] for i in ids], where,
                                            tags[ids[0]])):
            halves[i] = half
    grads = _assemble_with_sibling(halves, [0, 0, 0, 0, 1, 1])
    shapes2d = [(D_MODEL, SHARD_IN), (SHARD_ROWS, D_MODEL), (SHARD_ROWS, D_MODEL), (SHARD_ROWS, D_MODEL),
                (RNN_BLOCKS * SHARD_RG, RNN_BW), (RNN_BLOCKS * SHARD_RG, RNN_BW)]
    big_w = [w_in, w_attn_proj, w_rnn_proj, w_out, rg_wa, rg_wx]
    big_m = [m_w_in, m_w_attn_proj, m_w_rnn_proj, m_w_out, m_rg_wa, m_rg_wx]
    big_v = [v_w_in, v_w_attn_proj, v_w_rnn_proj, v_w_out, v_rg_wa, v_rg_wx]
    res = {}
    for ids in groups:
        flat2d = lambda arrs: [arrs[i].reshape(shapes2d[i]) for i in ids]
        outs = _adamw_shard(flat2d(grads), flat2d(big_w), flat2d(big_m), flat2d(big_v), tags[ids[0]])
        for i, four in zip(ids, outs):
            res[tags[i]] = [o.reshape(big_w[i].shape) for o in four]

    dmod_all = small_all[:, ROW_MOD:ROW_MOD + 3, :].reshape(N_DEV, ADA_W)
    dmod_cols = lax.dynamic_slice_in_dim(dmod_all, my_chip * SHARD_ADA, SHARD_ADA, axis=1)
    c_t = jnp.pad(jnp.transpose(c_all.reshape(N_DEV, D_MODEL)), ((0, 0), (0, 128 - N_DEV)))
    dmod_cols = jnp.pad(dmod_cols, ((0, 128 - N_DEV), (0, 0)))
    res["w_ada"] = [o.reshape(w_ada.shape) for o in _adamw_w_ada(c_t, dmod_cols, w_ada[0], m_w_ada[0], v_w_ada[0])]

    def full_conv(a):
        return lax.dynamic_update_slice_in_dim(jnp.zeros((CONV_W, D_MODEL), F32), a[0], my_chip * (D_MODEL // N_CHIPS), axis=1)

    packed = [_pack_small(p[0], p[1], p[2], p[3], p[4], p[5], p[6], p[7], full_conv(p[8])) for p in (
        (b_ada, norm_g, conv_b, rg_ba, rg_bx, rg_lambda, final_g, attn_sinks, conv_w),
        (m_b_ada, m_norm_g, m_conv_b, m_rg_ba, m_rg_bx, m_rg_lambda, m_final_g, m_attn_sinks, m_conv_w),
        (v_b_ada, v_norm_g, v_conv_b, v_rg_ba, v_rg_bx, v_rg_lambda, v_final_g, v_attn_sinks, v_conv_w))]
    small_out = _adamw_small(small_all, *packed)

    def unpack(slab):
        cw = lax.dynamic_slice_in_dim(slab[ROW_CONV_W:ROW_CONV_W + CONV_W], my_chip * (D_MODEL // N_CHIPS),
                                      D_MODEL // N_CHIPS, axis=1)
        return {
            "b_ada": slab[ROW_MOD:ROW_MOD + 3].reshape(1, ADA_W), "norm_g": slab[ROW_NORM_G:ROW_NORM_G + 1],
            "conv_b": slab[ROW_CONV_B:ROW_CONV_B + 1], "rg_ba": slab[ROW_BA:ROW_BA + 1], "rg_bx": slab[ROW_BX:ROW_BX + 1],
            "rg_lambda": slab[ROW_LAM:ROW_LAM + 1], "final_g": slab[ROW_FINAL_G], "attn_sinks": slab[ROW_SINKS:ROW_SINKS + 1, :N_HEADS],
            "conv_w": cw[None],
        }

    small_res = [unpack(s) for s in small_out]
    order = ["w_ada", "b_ada", "norm_g", "w_in", "attn_sinks", "conv_w", "conv_b", "rg_wa", "rg_ba", "rg_wx", "rg_bx",
             "rg_lambda", "w_attn_proj", "w_rnn_proj", "w_out", "final_g"]
    loss = small_out[0][ROW_LOSS, 0]
    outs = [loss, grad_x[None]]
    for kind in range(4):
        for name in order:
            outs.append(res[name][kind] if name in res else small_res[kind][name])
    return tuple(outs)
```

```python
import numpy as np
import jax
import jax.numpy as jnp
from jax import lax
from jax.experimental import pallas as pl
from jax.experimental.pallas import tpu as pltpu

F32 = jnp.float32
BF16 = jnp.bfloat16

D_MODEL = 1024
N_HEADS = 16
N_KV = 4
HEAD_DIM = 64
GROUP = N_HEADS // N_KV
BLOCK = 128
KV_W = N_KV * HEAD_DIM
ROT_HALF = 8
ROPE_THETA = 500000.0
ATTN_SCALE = 0.125
RNN_BLOCKS = 4
RNN_BW = 256
CONV_W = 4
LRU_C = 8.0
NORM_EPS = 1e-6
IN_W = 6656
CB = 512
N_CB = IN_W // CB
CB_Q, CB_KV, CB_GA, CB_XR, CB_GR, CB_MA, CB_MR = 0, 2, 3, 5, 7, 9, 11
V_COL_BLOCK = 5
N_CHIPS = 4
N_DEV = 8
SHARD_IN = IN_W // N_CHIPS
SHARD_ROWS = D_MODEL // N_CHIPS
SHARD_RG = RNN_BW // N_CHIPS
ADA_W = 3 * D_MODEL
SHARD_ADA = ADA_W // N_CHIPS
SMALL_ROWS = 16

ADAM_LR = 0.001
ADAM_B1 = 0.9
ADAM_B2 = 0.999
ADAM_EPS = 1e-08
ADAM_WD = 0.01
ADAM_STEP = 10

VMEM_LIMIT_V7X = 52 * 1024 * 1024
MESH = pl.DeviceIdType.MESH
ANY = pl.BlockSpec(memory_space=pl.ANY)
VMEM_SPEC = pl.BlockSpec(memory_space=pltpu.VMEM)


def _in_hbm(*arrays):
    return [pltpu.with_memory_space_constraint(a, pltpu.HBM) for a in arrays]


def _cp(*sem):
    return pltpu.CompilerParams(dimension_semantics=sem if sem else None, vmem_limit_bytes=VMEM_LIMIT_V7X)


def _dot(a, b):
    return jnp.dot(a, b, preferred_element_type=F32)


def _dot_nt(a, b):
    return lax.dot_general(a, b, (((1,), (1,)), ((), ())), preferred_element_type=F32)


def _dot_tn(a, b):
    return lax.dot_general(a, b, (((0,), (0,)), ((), ())), preferred_element_type=F32)


def _sigmoid(z):
    return 1.0 / (1.0 + jnp.exp(-z))


def _softplus(z):
    u = jnp.exp(-jnp.abs(z))
    log1p_u = jnp.where(u < 1e-3, u * (1.0 - u * (0.5 - u * (1.0 / 3.0))), jnp.log(1.0 + u))
    return jnp.maximum(z, 0.0) + log1p_u


def _rms(xf):
    return lax.rsqrt(jnp.mean(xf * xf, axis=-1, keepdims=True) + NORM_EPS)


def _me():
    return lax.axis_index("x"), lax.axis_index("y"), lax.axis_index("c")


def _peer(mask):
    x, y, c = _me()
    fx, fy, fc = (mask >> 2) & 1, (mask >> 1) & 1, mask & 1
    return (x ^ fx if fx else x, y ^ fy if fy else y, c ^ fc if fc else c)


def _chip_of(pos):
    return pos[0] * 2 + pos[1]


SIBLING_COLLECTIVE_ID = 0
SIBLING_ONLY = pltpu.CompilerParams(collective_id=SIBLING_COLLECTIVE_ID)


def _sibling_handshake():
    barrier = pltpu.get_barrier_semaphore()
    pl.semaphore_signal(barrier, inc=1, device_id=_peer(1), device_id_type=MESH)
    pl.semaphore_wait(barrier, 1)


CHIP_MASKS = (4, 2, 6)
ALL_MASKS = (1, 2, 3, 4, 5, 6, 7)


HBM_SPEC = pl.BlockSpec(memory_space=pltpu.HBM)
SEM_SPEC = pl.BlockSpec(memory_space=pltpu.SEMAPHORE)
SPLIT_COPY = pltpu.CompilerParams(has_side_effects=pltpu.SideEffectType.DATAFLOW_SIDE_EFFECTING)
N_BIG = 6
FULL_SHAPES = (
    (2, D_MODEL // 2, IN_W),
    (N_CHIPS, 2, SHARD_ROWS // 2, D_MODEL), (N_CHIPS, 2, SHARD_ROWS // 2, D_MODEL), (N_CHIPS, 2, SHARD_ROWS // 2, D_MODEL),
    (RNN_BLOCKS, N_CHIPS, 2, SHARD_RG // 2, RNN_BW), (RNN_BLOCKS, N_CHIPS, 2, SHARD_RG // 2, RNN_BW),
)


def _slot(full, idx, chip, half):
    if idx == 0:
        return full.at[half, :, pl.ds(pl.multiple_of(chip * SHARD_IN, 128), SHARD_IN)]
    return full.at[chip, half] if idx in (1, 2, 3) else full.at[:, chip, half]


def _three_halves(full, idx):
    return full.at[pl.ds(0, 3), 0] if idx in (1, 2, 3) else full.at[:, pl.ds(0, 3), 0]


def _gather_start(fulls, after):
    def body(*refs):
        full_refs = refs[:N_BIG]
        ssems, rsems = refs[N_BIG + 1:N_BIG + 5], refs[N_BIG + 5:N_BIG + 9]
        token = refs[2 * N_BIG + 9]
        me = _me()
        my_chip = _chip_of(me)
        for idx in range(N_BIG):
            for k, mask in enumerate(CHIP_MASKS):
                pair = k if idx == 0 else 3
                mine = _slot(full_refs[idx], idx, my_chip, me[2])
                pltpu.make_async_remote_copy(src_ref=mine, dst_ref=mine, send_sem=ssems[pair], recv_sem=rsems[pair],
                                             device_id=_peer(mask), device_id_type=MESH).start()
        token[...] = jnp.zeros_like(token)

    sem = pltpu.SemaphoreType.DMA(())
    out_shape = (sem,) * 8 + tuple(pltpu.HBM(f.shape, f.dtype) for f in fulls) + (jax.ShapeDtypeStruct((8, 128), F32),)
    outs = pl.pallas_call(
        body, out_shape=out_shape, name="gather_start",
        in_specs=[HBM_SPEC] * N_BIG + [ANY], out_specs=tuple([SEM_SPEC] * 8 + [HBM_SPEC] * N_BIG + [VMEM_SPEC]),
        input_output_aliases={i: 8 + i for i in range(N_BIG)}, compiler_params=SPLIT_COPY,
    )(*[pltpu.with_memory_space_constraint(f, pltpu.HBM) for f in fulls], after)
    return outs[0:4], outs[4:8], outs[8:8 + N_BIG], outs[8 + N_BIG]


def _gather_wait(ssem, rsem, arrays, idxs, after, tag):
    n = len(arrays)

    def body(*refs):
        full_refs, ssem_ref, rsem_ref = refs[:n], refs[n], refs[n + 1]
        me = _me()
        for full, idx in zip(full_refs, idxs):
            region = _slot(full, 0, _chip_of(me), me[2]) if idx == 0 else _three_halves(full, idx)
            arrived = pltpu.make_async_remote_copy(
                src_ref=region, dst_ref=region, send_sem=ssem_ref, recv_sem=rsem_ref, device_id=me, device_id_type=MESH)
            arrived.wait_send()
            arrived.wait_recv()

    outs = pl.pallas_call(
        body, out_shape=tuple(pltpu.HBM(a.shape, a.dtype) for a in arrays), name=f"gather_wait_{tag}",
        in_specs=[HBM_SPEC] * n + [SEM_SPEC, SEM_SPEC, ANY], out_specs=tuple([HBM_SPEC] * n),
        input_output_aliases={i: i for i in range(n)}, compiler_params=SPLIT_COPY,
    )(*arrays, ssem, rsem, after)
    return list(outs)


def _forward_halves(arrays, items, tag):
    n, m = len(arrays), len(items)

    def body(*refs):
        outs, ssem, rsem = refs[n:2 * n], refs[2 * n], refs[2 * n + 1]
        me = _me()
        sib = _peer(1)
        _sibling_handshake()
        cps = []
        for j, (pos, idx, k) in enumerate(items):
            chip = _chip_of(_peer(CHIP_MASKS[k]))
            cp = pltpu.make_async_remote_copy(
                src_ref=_slot(outs[pos], idx, chip, me[2]), dst_ref=_slot(outs[pos], idx, chip, me[2]),
                send_sem=ssem.at[j], recv_sem=rsem.at[j], device_id=sib, device_id_type=MESH)
            cp.start()
            cps.append(cp)
        for j, (pos, idx, k) in enumerate(items):
            chip = _chip_of(_peer(CHIP_MASKS[k]))
            pltpu.make_async_remote_copy(
                src_ref=_slot(outs[pos], idx, chip, me[2]), dst_ref=_slot(outs[pos], idx, chip, 1 - me[2]),
                send_sem=ssem.at[j], recv_sem=rsem.at[j], device_id=sib, device_id_type=MESH).wait_recv()
        for cp in cps:
            cp.wait_send()

    outs = pl.pallas_call(
        body, out_shape=tuple(jax.ShapeDtypeStruct(a.shape, a.dtype) for a in arrays), name=f"forward_halves_{tag}",
        in_specs=[ANY] * n, out_specs=tuple([ANY] * n), input_output_aliases={i: i for i in range(n)},
        scratch_shapes=[pltpu.SemaphoreType.DMA((m,)), pltpu.SemaphoreType.DMA((m,))], compiler_params=SIBLING_ONLY,
    )(*arrays)
    return list(outs)


def _gather_mod(c_row, w_ada_s, conv_w_s):
    def body(c_ref, wada_ref, cw_s, cw_f, call_ref, mod_ref, wsend, wrecv, lsem, csend, crecv, msend, mrecv):
        me = _me()
        my_chip = _chip_of(me)
        my_dev = my_chip * 2 + me[2]
        sends = []
        for k, mask in enumerate(CHIP_MASKS):
            cp = pltpu.make_async_remote_copy(src_ref=cw_s, dst_ref=cw_f.at[my_chip], send_sem=wsend.at[k], recv_sem=wrecv.at[k],
                                              device_id=_peer(mask), device_id_type=MESH)
            cp.start()
            sends.append(cp)
        local = [pltpu.make_async_copy(cw_s, cw_f.at[my_chip], lsem.at[0])]
        for cp in local:
            cp.start()

        call_ref[my_dev] = c_ref[0]
        csends = []
        for k, mask in enumerate(ALL_MASKS):
            cp = pltpu.make_async_remote_copy(
                src_ref=c_ref.at[0], dst_ref=call_ref.at[my_dev],
                send_sem=csend.at[k], recv_sem=crecv.at[k], device_id=_peer(mask), device_id_type=MESH)
            cp.start()
            csends.append(cp)
        for k, mask in enumerate(ALL_MASKS):
            frm = _peer(mask)
            pltpu.make_async_remote_copy(
                src_ref=c_ref.at[0], dst_ref=call_ref.at[_chip_of(frm) * 2 + frm[2]],
                send_sem=csend.at[k], recv_sem=crecv.at[k], device_id=frm, device_id_type=MESH).wait_recv()
        for cp in csends:
            cp.wait_send()

        c_all = call_ref[...].reshape(N_DEV, D_MODEL).astype(BF16)
        mod_ref[my_chip] = _dot(c_all, wada_ref[...].astype(BF16))
        msends = []
        for k, mask in enumerate(CHIP_MASKS):
            cp = pltpu.make_async_remote_copy(
                src_ref=mod_ref.at[my_chip], dst_ref=mod_ref.at[my_chip],
                send_sem=msend.at[k], recv_sem=mrecv.at[k], device_id=_peer(mask), device_id_type=MESH)
            cp.start()
            msends.append(cp)
        for k, mask in enumerate(CHIP_MASKS):
            frm = _peer(mask)
            pltpu.make_async_remote_copy(
                src_ref=mod_ref.at[my_chip], dst_ref=mod_ref.at[_chip_of(frm)],
                send_sem=msend.at[k], recv_sem=mrecv.at[k], device_id=frm, device_id_type=MESH).wait_recv()
        for cp in msends:
            cp.wait_send()

        for k, mask in enumerate(CHIP_MASKS):
            frm = _peer(mask)
            pltpu.make_async_remote_copy(src_ref=cw_s, dst_ref=cw_f.at[_chip_of(frm)], send_sem=wsend.at[k], recv_sem=wrecv.at[k],
                                         device_id=frm, device_id_type=MESH).wait_recv()
        for cp in sends:
            cp.wait_send()
        for cp in local:
            cp.wait()

    out_shape = (
        jax.ShapeDtypeStruct((N_CHIPS, CONV_W, D_MODEL // N_CHIPS), F32),
        jax.ShapeDtypeStruct((N_DEV, 1, D_MODEL), F32),
        jax.ShapeDtypeStruct((N_CHIPS, N_DEV, SHARD_ADA), F32),
    )
    return pl.pallas_call(
        body, out_shape=out_shape, name="gather_mod",
        in_specs=[VMEM_SPEC, VMEM_SPEC, ANY], out_specs=(ANY, VMEM_SPEC, VMEM_SPEC),
        scratch_shapes=[
            pltpu.SemaphoreType.DMA((3,)), pltpu.SemaphoreType.DMA((3,)), pltpu.SemaphoreType.DMA((1,)),
            pltpu.SemaphoreType.DMA((7,)), pltpu.SemaphoreType.DMA((7,)),
            pltpu.SemaphoreType.DMA((3,)), pltpu.SemaphoreType.DMA((3,)),
        ],
        compiler_params=pltpu.CompilerParams(vmem_limit_bytes=VMEM_LIMIT_V7X),
    )(c_row, w_ada_s, conv_w_s)


def _cast_place(shards, chip_idx, places):
    n = len(shards)

    def body(chip_ref, *refs):
        for s_ref, o_ref in zip(refs[:n], refs[n:]):
            o_ref[...] = s_ref[...].astype(BF16)

    grid_spec = pltpu.PrefetchScalarGridSpec(
        num_scalar_prefetch=1, grid=(1,),
        in_specs=[pl.BlockSpec(s.shape, lambda i, chip_ref, nd=s.ndim: (0,) * nd) for s in shards],
        out_specs=tuple(pl.BlockSpec(block, lambda i, chip_ref, im=im: im(chip_ref[0])) for _, block, im in places))
    return pl.pallas_call(
        body, out_shape=tuple(jax.ShapeDtypeStruct(full, BF16) for full, _, _ in places), grid_spec=grid_spec,
        name="cast_place", compiler_params=_cp("arbitrary"),
    )(chip_idx, *_in_hbm(*shards))


def _shard_of(ref, kind, chip):
    if kind == "in":
        return ref.at[:, pl.ds(pl.multiple_of(chip * SHARD_IN, 128), SHARD_IN)]
    return ref.at[chip] if kind == "sq" else ref.at[:, chip]


def _land_shape(src, kind):
    if kind == "in":
        return (3, src.shape[0], SHARD_IN)
    return (3,) + src.shape[1:] if kind == "sq" else (3, src.shape[0]) + src.shape[2:]


def _exchange_start(srcs, kinds, tag):
    n = len(srcs)
    lands = [pltpu.with_memory_space_constraint(lax.empty(_land_shape(s, k), s.dtype), pltpu.HBM) for s, k in zip(srcs, kinds)]

    def body(*refs):
        src_refs, land_refs = refs[:n], refs[n:2 * n]
        ssems, rsems = refs[2 * n:3 * n], refs[3 * n:4 * n]
        token = refs[6 * n]
        for i in range(n):
            for k, mask in enumerate(CHIP_MASKS):
                to = _peer(mask)
                pltpu.make_async_remote_copy(
                    src_ref=_shard_of(src_refs[i], kinds[i], _chip_of(to)), dst_ref=land_refs[i].at[k],
                    send_sem=ssems[i], recv_sem=rsems[i], device_id=to, device_id_type=MESH).start()
        token[...] = jnp.zeros_like(token)

    sem = pltpu.SemaphoreType.DMA(())
    out_shape = ((sem,) * (2 * n) + tuple(pltpu.HBM(s.shape, s.dtype) for s in srcs)
                 + tuple(pltpu.HBM(l.shape, l.dtype) for l in lands) + (jax.ShapeDtypeStruct((8, 128), F32),))
    outs = pl.pallas_call(
        body, out_shape=out_shape, name=f"exchange_start_{tag}",
        in_specs=[HBM_SPEC] * (2 * n), out_specs=tuple([SEM_SPEC] * (2 * n) + [HBM_SPEC] * (2 * n) + [VMEM_SPEC]),
        input_output_aliases={i: 2 * n + i for i in range(2 * n)},
        compiler_params=pltpu.CompilerParams(has_side_effects=pltpu.SideEffectType.DATAFLOW_SIDE_EFFECTING),
    )(*[pltpu.with_memory_space_constraint(s, pltpu.HBM) for s in srcs], *lands)
    return outs[:n], outs[n:2 * n], outs[2 * n:3 * n], outs[3 * n:4 * n], outs[4 * n]


def _exchange_wait(ssems, rsems, srcs, lands, after, tag):
    n = len(srcs)

    def body(*refs):
        land_refs = refs[n:2 * n]
        ssem_refs, rsem_refs = refs[2 * n:3 * n], refs[3 * n:4 * n]
        for i in range(n):
            all_three = pltpu.make_async_remote_copy(
                src_ref=land_refs[i], dst_ref=land_refs[i], send_sem=ssem_refs[i], recv_sem=rsem_refs[i],
                device_id=_me(), device_id_type=MESH)
            all_three.wait_send()
            all_three.wait_recv()

    outs = pl.pallas_call(
        body, out_shape=tuple(pltpu.HBM(a.shape, a.dtype) for a in list(srcs) + list(lands)), name=f"exchange_wait_{tag}",
        in_specs=[HBM_SPEC] * (2 * n) + [SEM_SPEC] * (2 * n) + [ANY], out_specs=tuple([HBM_SPEC] * (2 * n)),
        input_output_aliases={i: i for i in range(2 * n)},
        compiler_params=pltpu.CompilerParams(has_side_effects=pltpu.SideEffectType.DATAFLOW_SIDE_EFFECTING),
    )(*srcs, *lands, *ssems, *rsems, after)
    return outs[:n], outs[n:]


def _gather_small(small):
    def body(small_ref, small_all, ssend, srecv):
        me = _me()
        my_dev = _chip_of(me) * 2 + me[2]
        small_all[my_dev] = small_ref[...]
        ssends = []
        for k, mask in enumerate(ALL_MASKS):
            cp = pltpu.make_async_remote_copy(
                src_ref=small_ref, dst_ref=small_all.at[my_dev],
                send_sem=ssend.at[k], recv_sem=srecv.at[k], device_id=_peer(mask), device_id_type=MESH)
            cp.start()
            ssends.append(cp)
        for k, mask in enumerate(ALL_MASKS):
            frm = _peer(mask)
            pltpu.make_async_remote_copy(
                src_ref=small_ref, dst_ref=small_all.at[_chip_of(frm) * 2 + frm[2]],
                send_sem=ssend.at[k], recv_sem=srecv.at[k], device_id=frm, device_id_type=MESH).wait_recv()
        for cp in ssends:
            cp.wait_send()

    return pl.pallas_call(
        body, out_shape=jax.ShapeDtypeStruct((N_DEV, SMALL_ROWS, D_MODEL), F32), name="gather_small",
        in_specs=[VMEM_SPEC], out_specs=VMEM_SPEC,
        scratch_shapes=[pltpu.SemaphoreType.DMA((7,)), pltpu.SemaphoreType.DMA((7,))],
    )(small)


def _half_of(ref, axis, half):
    return ref.at[(slice(None),) * axis + (half,)]


def _swap_halves(parts, axes):
    n = len(parts)

    def body(*refs):
        ins, outs, ssem, rsem = refs[:n], refs[n:2 * n], refs[2 * n], refs[2 * n + 1]
        c = lax.axis_index("c")
        _sibling_handshake()
        cps = [pltpu.make_async_remote_copy(src_ref=_half_of(ins[i], axes[i], 1 - c), dst_ref=outs[i], send_sem=ssem.at[i],
                                            recv_sem=rsem.at[i], device_id=_peer(1), device_id_type=MESH) for i in range(n)]
        for cp in cps:
            cp.start()
        for cp in cps:
            cp.wait()

    shapes = [p.shape[:a] + p.shape[a + 1:] for p, a in zip(parts, axes)]
    return pl.pallas_call(
        body, out_shape=tuple(jax.ShapeDtypeStruct(s, p.dtype) for s, p in zip(shapes, parts)), name="swap_halves",
        in_specs=[ANY] * n, out_specs=tuple([ANY] * n),
        scratch_shapes=[pltpu.SemaphoreType.DMA((n,)), pltpu.SemaphoreType.DMA((n,))], compiler_params=SIBLING_ONLY,
    )(*parts)


def _presum(mines, sibs, c_idx, tag):
    n = len(mines)
    S, _, R, C = mines[0].shape
    tr = min(R, 256)
    tc = SHARD_IN if C % SHARD_IN == 0 else (C // 2 if n > 1 and C % 256 == 0 else C)

    def body(c_ref, *refs):
        for k in range(n):
            total = refs[k][:, 0] + refs[n + k][...]
            refs[2 * n + k][...] = total
            refs[3 * n + k][...] = total.astype(BF16)

    out_spec = pl.BlockSpec((S, tr, tc), lambda i, j, c_ref: (0, i, j))
    grid_spec = pltpu.PrefetchScalarGridSpec(
        num_scalar_prefetch=1, grid=(R // tr, C // tc),
        in_specs=[pl.BlockSpec((S, 1, tr, tc), lambda i, j, c_ref: (0, c_ref[0], i, j))] * n + [out_spec] * n,
        out_specs=(out_spec,) * (2 * n))
    outs = pl.pallas_call(
        body, out_shape=(jax.ShapeDtypeStruct((S, R, C), F32),) * n + (jax.ShapeDtypeStruct((S, R, C), BF16),) * n,
        grid_spec=grid_spec, name=f"presum_{tag}", compiler_params=_cp("parallel", "parallel"),
    )(c_idx, *mines, *sibs)
    return list(outs[:n]), list(outs[n:])


def _assemble_with_sibling(parts, axes):
    n = len(parts)

    def body(*refs):
        outs, ssem, rsem = refs[n:2 * n], refs[2 * n], refs[2 * n + 1]
        c = lax.axis_index("c")
        _sibling_handshake()
        cps = [pltpu.make_async_remote_copy(
            src_ref=_half_of(outs[i], axes[i], c), dst_ref=_half_of(outs[i], axes[i], c), send_sem=ssem.at[i],
            recv_sem=rsem.at[i], device_id=_peer(1), device_id_type=MESH) for i in range(n)]
        for cp in cps:
            cp.start()
        for i in range(n):
            pltpu.make_async_remote_copy(
                src_ref=_half_of(outs[i], axes[i], c), dst_ref=_half_of(outs[i], axes[i], 1 - c), send_sem=ssem.at[i],
                recv_sem=rsem.at[i], device_id=_peer(1), device_id_type=MESH).wait_recv()
        for cp in cps:
            cp.wait_send()

    return pl.pallas_call(
        body, out_shape=tuple(jax.ShapeDtypeStruct(p.shape, p.dtype) for p in parts), name="assemble_with_sibling",
        in_specs=[ANY] * n, out_specs=tuple([ANY] * n), input_output_aliases={i: i for i in range(n)},
        scratch_shapes=[pltpu.SemaphoreType.DMA((n,)), pltpu.SemaphoreType.DMA((n,))], compiler_params=SIBLING_ONLY,
    )(*parts)


def _rope_lane_frequencies():
    inv = np.float32(ROPE_THETA) ** (-(np.arange(0, 2 * ROT_HALF, 2, dtype=np.float32)) / np.float32(2 * ROT_HALF))
    lane = np.arange(128) % HEAD_DIM
    return jnp.asarray(np.where(lane < 2 * ROT_HALF, inv[lane % ROT_HALF], 0.0).astype(np.float32)[None, :])


def _rope_tables(pos, freq):
    ang = pos.astype(F32) * freq
    c, s = jnp.cos(ang), jnp.sin(ang)
    m = lax.broadcasted_iota(jnp.int32, ang.shape, 1) & (HEAD_DIM - 1)
    return (jnp.where(m < 2 * ROT_HALF, c, 1.0), jnp.where(m < ROT_HALF, -s, 0.0),
            jnp.where((m >= ROT_HALF) & (m < 2 * ROT_HALF), s, 0.0))


def _columns(t):
    return [t[:, i:i + 128] for i in range(0, t.shape[-1], 128)]


def _rope(t, c, sa, sb):
    return jnp.concatenate(
        [x * c + pltpu.roll(x, 128 - ROT_HALF, 1) * sa + pltpu.roll(x, ROT_HALF, 1) * sb for x in _columns(t)], axis=1)


def _unrope(d, c, sa, sb):
    return jnp.concatenate(
        [x * c + pltpu.roll(x * sa, ROT_HALF, 1) + pltpu.roll(x * sb, 128 - ROT_HALF, 1) for x in _columns(d)], axis=1)


def _prenorm(x, mod_row, norm_g, pos_col):
    T = x.shape[0]
    tm = min(T, 512)

    def body(x_ref, mod_ref, g_ref, pos_ref, f_ref, h_ref, ht_ref, c_ref, sa_ref, sb_ref):
        xf = x_ref[...]
        shift, scale = mod_ref[:, 0:D_MODEL], mod_ref[:, D_MODEL:2 * D_MODEL]
        h = (xf * _rms(xf)) * g_ref[...] * (1.0 + scale) + shift
        h_ref[...] = h.astype(BF16)
        ht_ref[...] = h.T.astype(BF16)
        c_ref[...], sa_ref[...], sb_ref[...] = _rope_tables(pos_ref[...], f_ref[...])

    tab = jax.ShapeDtypeStruct((T, 128), F32)
    tok = lambda w: pl.BlockSpec((tm, w), lambda i: (i, 0))
    row = lambda w: pl.BlockSpec((1, w), lambda i: (0, 0))
    outs = pl.pallas_call(
        body, out_shape=(jax.ShapeDtypeStruct((T, D_MODEL), BF16), jax.ShapeDtypeStruct((D_MODEL, T), BF16), tab, tab, tab),
        grid=(T // tm,), name="prenorm",
        in_specs=[tok(D_MODEL), row(ADA_W), row(D_MODEL), tok(1), row(128)],
        out_specs=(tok(D_MODEL), pl.BlockSpec((D_MODEL, tm), lambda i: (0, i)), tok(128), tok(128), tok(128)),
        compiler_params=_cp("parallel"),
    )(x, *_in_hbm(mod_row, norm_g), pos_col, _rope_lane_frequencies())
    return outs[0], outs[1], tuple(outs[2:])


def _in_projection(h, w_in, chips, into, tag):
    T = h.shape[0]
    tm, tn = min(T, 512), SHARD_IN
    k = chips.shape[0]

    def body(chip_ref, h_ref, w_ref, *rest):
        rest[-1][...] = _dot(h_ref[...], w_ref[...])

    w_spec = pl.BlockSpec((D_MODEL, tn), lambda s, i, c: (0, c[s]), **({"pipeline_mode": pl.Buffered(1)} if k == 1 else {}))
    in_specs = [pl.BlockSpec((tm, D_MODEL), lambda s, i, c: (i, 0)), w_spec]
    args = [chips, h, w_in]
    aliases = {}
    if into is not None:
        in_specs.append(ANY)
        args.append(into)
        aliases = {3: 0}
    grid_spec = pltpu.PrefetchScalarGridSpec(num_scalar_prefetch=1, grid=(k, T // tm), in_specs=in_specs,
                                             out_specs=pl.BlockSpec((tm, tn), lambda s, i, c: (i, c[s])))
    return pl.pallas_call(
        body, out_shape=jax.ShapeDtypeStruct((T, IN_W), F32), grid_spec=grid_spec, name=f"in_projection_{tag}",
        input_output_aliases=aliases, compiler_params=_cp("parallel", "parallel"),
    )(*args)


def _attn_mask(n):
    qi = lax.broadcasted_iota(jnp.int32, (GROUP * BLOCK, BLOCK), 0) & (BLOCK - 1)
    j = lax.broadcasted_iota(jnp.int32, (GROUP * BLOCK, BLOCK), 1)
    own = j <= qi
    return own, jnp.logical_not(own) & (n == 0)


def _fold(x, own):
    return jnp.where(own, x[:, BLOCK:2 * BLOCK], x[:, 0:BLOCK])


def _unfold(xf, own):
    zero = jnp.zeros_like(xf)
    return jnp.concatenate([jnp.where(own, zero, xf), jnp.where(own, xf, zero)], axis=1)


ROW_GROUP_HEAD = (0, 2, 1, 3)


def _sink_col(sink_ref, kh):
    rowg = lax.broadcasted_iota(jnp.int32, (GROUP * BLOCK, 1), 0) // BLOCK
    col = jnp.full((GROUP * BLOCK, 1), sink_ref[0, GROUP * kh + ROW_GROUP_HEAD[0]], F32)
    for g in range(1, GROUP):
        col = jnp.where(rowg == g, sink_ref[0, GROUP * kh + ROW_GROUP_HEAD[g]], col)
    return col


def _low_lanes(shape):
    return lax.broadcasted_iota(jnp.int32, shape, 1) < HEAD_DIM


def _kv_pair_operand(prev, cur, kh):
    c = 128 * (kh // 2)
    col = jnp.concatenate([prev[:, c:c + 128], cur[:, c:c + 128]], axis=0).astype(F32)
    if kh % 2 == 0:
        lo = jnp.where(_low_lanes(col.shape), col, 0.0)
        hi = pltpu.roll(lo, HEAD_DIM, 1)
    else:
        hi = jnp.where(_low_lanes(col.shape), 0.0, col)
        lo = pltpu.roll(hi, HEAD_DIM, 1)
    return jnp.concatenate([lo, hi], axis=0).astype(BF16)


def _pair_rows(x, kh):
    c = 2 * 128 * kh
    return jnp.concatenate([x[:, c:c + 128], x[:, c + 128:c + 256]], axis=0)


def _restack(big):
    return jnp.concatenate([big[:, 0:2 * BLOCK], big[:, 2 * BLOCK:4 * BLOCK]], axis=0)


def _unrestack(stacked):
    return jnp.concatenate([stacked[0:2 * BLOCK], stacked[2 * BLOCK:4 * BLOCK]], axis=1)


def _fold_pair(x2, kh):
    low = _low_lanes((2 * BLOCK, 128))
    mixed = jnp.where(low, x2[0:2 * BLOCK], x2[2 * BLOCK:4 * BLOCK])
    total = mixed + pltpu.roll(mixed, HEAD_DIM, 1)
    return jnp.where(low, total, 0.0) if kh % 2 == 0 else jnp.where(low, 0.0, total)


def _attn_scores(qr, k2, kh):
    q2 = _pair_rows(qr, kh).astype(BF16)
    return q2, _restack(_dot_nt(q2, k2))


def _attn_softmax(s, sink_col, mask):
    own, no_key = mask
    s = jnp.where(no_key, -1e30, _fold(s, own))
    m = jnp.maximum(jnp.max(s, axis=-1, keepdims=True), sink_col)
    p = jnp.exp(s - m)
    p_sink = jnp.exp(sink_col - m)
    denom = jnp.sum(p, axis=-1, keepdims=True) + p_sink
    return p / denom, p_sink / denom


def _attn_forward(proj, tabs, sinks):
    T = proj.shape[0]
    nb = T // BLOCK

    def body(q_ref, kvc_ref, kvp_ref, g0_ref, g1_ref, cc, sac, sbc, cp_, sap, sbp, sink_ref, y_ref, qrb_ref, krb_ref):
        n = pl.program_id(0)
        tc = tcur = (cc[...], sac[...], sbc[...])
        tprev = (cp_[...], sap[...], sbp[...])
        qr = _rope(q_ref[...], *tc) * ATTN_SCALE
        kr_cur = _rope(kvc_ref[:, 0:KV_W], *tcur)
        kr_prev = _rope(kvp_ref[:, 0:KV_W], *tprev)
        qrb_ref[...] = qr.astype(BF16)
        krb_ref[...] = kr_cur.astype(BF16)
        v_cur, v_prev = kvc_ref[:, KV_W:2 * KV_W], kvp_ref[:, KV_W:2 * KV_W]
        mask = _attn_mask(n)
        outs = []
        k2s = [_kv_pair_operand(kr_prev, kr_cur, kh) for kh in range(N_KV)]
        v2s = [_kv_pair_operand(v_prev, v_cur, kh) for kh in range(N_KV)]
        scores = [_attn_scores(qr, k2s[kh], kh) for kh in range(N_KV)]
        for kh in range(N_KV):
            pn, _ = _attn_softmax(scores[kh][1], _sink_col(sink_ref, kh), mask)
            o_big = _dot(_unrestack(_unfold(pn.astype(BF16), mask[0])), v2s[kh])
            outs += [o_big[0:BLOCK], o_big[BLOCK:2 * BLOCK]]
        o = jnp.concatenate(outs, axis=1)
        g = jnp.concatenate([g0_ref[...], g1_ref[...]], axis=1)
        y_ref[...] = (o * (g * _sigmoid(g))).astype(BF16)

    def blk(w, cb):
        return pl.BlockSpec((BLOCK, w), lambda n, cb=cb: (n, cb))

    prev = lambda w, cb: pl.BlockSpec((BLOCK, w), lambda n, cb=cb: (jnp.maximum(n - 1, 0), cb))
    return pl.pallas_call(
        body, grid=(nb,), name="attn_forward",
        out_shape=(jax.ShapeDtypeStruct((T, D_MODEL), BF16), jax.ShapeDtypeStruct((T, D_MODEL), BF16),
                   jax.ShapeDtypeStruct((T, KV_W), BF16)),
        in_specs=[blk(D_MODEL, 0), blk(CB, CB_KV), prev(CB, CB_KV), blk(CB, CB_GA), blk(CB, CB_GA + 1),
                  blk(128, 0), blk(128, 0), blk(128, 0), prev(128, 0), prev(128, 0), prev(128, 0),
                  pl.BlockSpec(memory_space=pltpu.SMEM)],
        out_specs=(blk(D_MODEL, 0), blk(D_MODEL, 0), blk(KV_W, 0)),
        compiler_params=_cp("parallel"),
    )(proj, proj, proj, proj, proj, *tabs, *tabs, sinks)


def _scan_rows8():
    return lax.broadcasted_iota(jnp.int32, (8, D_MODEL), 0)


def _scan_forward(a_ref, b_ref, h_ref, carry, rows):
    row = _scan_rows8()

    def group(i, carry):
        off = pl.multiple_of(i * 8, 8)
        a, b = a_ref[pl.ds(off, 8), :], b_ref[pl.ds(off, 8), :]
        for d in (1, 2, 4):
            ok = row >= d
            b = jnp.where(ok, a * pltpu.roll(b, d, 0) + b, b)
            a = jnp.where(ok, a * pltpu.roll(a, d, 0), a)
        h = a * carry + b
        h_ref[pl.ds(off, 8), :] = h
        return h[7:8, :]

    return lax.fori_loop(0, rows // 8, group, carry)


def _scan_backward(a_ref, g_ref, lam_ref, carry, rows):
    row = _scan_rows8()

    def group(i, carry):
        off = pl.multiple_of((rows // 8 - 1 - i) * 8, 8)
        a, g = a_ref[pl.ds(off, 8), :], g_ref[pl.ds(off, 8), :]
        b = a * g
        for d in (1, 2, 4):
            ok = row < 8 - d
            b = jnp.where(ok, a * pltpu.roll(b, 8 - d, 0) + b, b)
            a = jnp.where(ok, a * pltpu.roll(a, 8 - d, 0), a)
        mu = a * carry + b
        mu_below = jnp.where(row == 7, carry, pltpu.roll(mu, 7, 0))
        lam_ref[pl.ds(off, 8), :] = g + mu_below
        return mu[0:1, :]

    return lax.fori_loop(0, rows // 8, group, carry)


def _conv_taps(xbuf, xr, tail):
    rows = xr.shape[0]
    xbuf[0:8, :] = tail
    xbuf[8:rows + 8, :] = xr
    return [xbuf[pl.ds(8 - (CONV_W - 1 - k), rows), :] for k in range(CONV_W - 1)] + [xr]


def _rnn_gates(xbuf, xr, tail, cw, cb, wa_ref, wx_ref, ba, bx, sp, reset):
    xs = _conv_taps(xbuf, xr, tail)
    xc = xs[0] * cw[0:1, :]
    for k in range(1, CONV_W):
        xc = xc + xs[k] * cw[k:k + 1, :]
    xc = xc + cb
    xcb = xc.astype(BF16)
    za = jnp.concatenate([_dot(xcb[:, RNN_BW * j:RNN_BW * (j + 1)], wa_ref[j]) for j in range(RNN_BLOCKS)], axis=1) + ba
    zx = jnp.concatenate([_dot(xcb[:, RNN_BW * j:RNN_BW * (j + 1)], wx_ref[j]) for j in range(RNN_BLOCKS)], axis=1) + bx
    r, i = _sigmoid(za), _sigmoid(zx)
    neg_log_a = LRU_C * r * sp
    a_raw = jnp.exp(-neg_log_a)
    mult_raw = jnp.sqrt(jnp.tanh(neg_log_a) * (1.0 + a_raw * a_raw))
    a = jnp.where(reset, 0.0, a_raw)
    mult = jnp.where(reset, 1.0, mult_raw)
    return xc, r, i, a, mult


def _rnn_forward(proj, pos_col, conv_w, conv_b, rwa, rwx, ba, bx, lam):
    T = proj.shape[0]
    tr = min(T, 256)

    def body(x0, x1, g0, g1, pos_ref, cw_ref, cb_ref, wa_ref, wx_ref, ba_ref, bx_ref, lam_ref,
             y_ref, h_ref, xc_ref, r_ref, i_ref, a_ref, mult_ref, xbuf, bbuf, tail, carry):
        t = pl.program_id(0)

        @pl.when(t == 0)
        def _():
            tail[...] = jnp.zeros_like(tail)
            carry[...] = jnp.zeros_like(carry)

        xr = jnp.concatenate([x0[...], x1[...]], axis=1)
        sp = _softplus(-lam_ref[...])
        reset = pos_ref[...] == 0
        xc, r, i, a, mult = _rnn_gates(
            xbuf, xr, tail[...], cw_ref[...], cb_ref[...], wa_ref, wx_ref, ba_ref[...], bx_ref[...], sp, reset)
        xc_ref[...] = xc
        r_ref[...] = r
        i_ref[...] = i
        a_ref[...] = a
        mult_ref[...] = mult
        bbuf[...] = mult * (i * xc)
        last = _scan_forward(a_ref, bbuf, h_ref, carry[0:1, :], tr)
        carry[...] = jnp.broadcast_to(last, carry.shape)
        tail[...] = xr[tr - 8:tr, :]
        g = jnp.concatenate([g0[...], g1[...]], axis=1)
        y_ref[...] = (h_ref[...] * (g * _sigmoid(g))).astype(BF16)

    blk = lambda cb: pl.BlockSpec((tr, CB), lambda t, cb=cb: (t, cb))
    row = lambda w: pl.BlockSpec((1, w), lambda t: (0, 0))
    full3 = pl.BlockSpec((RNN_BLOCKS, RNN_BW, RNN_BW), lambda t: (0, 0, 0))
    tok = pl.BlockSpec((tr, D_MODEL), lambda t: (t, 0))
    act = jax.ShapeDtypeStruct((T, D_MODEL), F32)
    return pl.pallas_call(
        body, out_shape=(jax.ShapeDtypeStruct((T, D_MODEL), BF16),) + (act,) * 6,
        grid=(T // tr,), name="rnn_forward",
        in_specs=[blk(CB_XR), blk(CB_XR + 1), blk(CB_GR), blk(CB_GR + 1), pl.BlockSpec((tr, 1), lambda t: (t, 0)),
                  pl.BlockSpec((CONV_W, D_MODEL), lambda t: (0, 0)), row(D_MODEL), full3, full3,
                  row(D_MODEL), row(D_MODEL), row(D_MODEL)],
        out_specs=(tok,) * 7,
        scratch_shapes=[pltpu.VMEM((tr + 8, D_MODEL), F32), pltpu.VMEM((tr, D_MODEL), F32),
                        pltpu.VMEM((8, D_MODEL), F32), pltpu.VMEM((8, D_MODEL), F32)],
        compiler_params=_cp("arbitrary"),
    )(proj, proj, proj, proj, pos_col, *_in_hbm(conv_w, conv_b, rwa, rwx, ba, bx, lam))


def _merge_and_head(x, target, y_attn, y_rnn, proj, wap, wrp, wo, mod_row, final_g):
    T = x.shape[0]
    tm = min(T, 256)

    def body(x_ref, t_ref, ya_ref, yr_ref, ma0, ma1, mr0, mr1, wap_ref, wrp_ref, wo_ref, mod_ref, fg_ref,
             dx2_ref, mg_ref, do_ref, dpa_ref, dpr_ref, dya_ref, dyr_ref, dc_ref, dfg_ref, dgate_ref, loss_ref):
        i = pl.program_id(0)
        gate = mod_ref[:, 2 * D_MODEL:3 * D_MODEL]
        fg = fg_ref[...]
        pa, pr = _dot(ya_ref[...], wap_ref[...]), _dot(yr_ref[...], wrp_ref[...])
        sa = _sigmoid(jnp.concatenate([ma0[...], ma1[...]], axis=1))
        sr = _sigmoid(jnp.concatenate([mr0[...], mr1[...]], axis=1))
        mb = (sa * pa + sr * pr).astype(BF16)
        o = _dot(mb, wo_ref[...])
        x2 = x_ref[...] + gate * o
        r2 = _rms(x2)
        xn2 = x2 * r2
        err = xn2 * fg - t_ref[...]
        loss_t = 0.5 * jnp.sum(jnp.sum(err * err, axis=-1, keepdims=True) * (1.0 / D_MODEL), axis=0, keepdims=True)
        dy = err * (1.0 / D_MODEL)
        dfg_t = jnp.sum(dy * xn2, axis=0, keepdims=True)
        dxn = dy * fg
        dx2 = r2 * (dxn - xn2 * jnp.mean(dxn * xn2, axis=-1, keepdims=True))
        dgate_t = jnp.sum(dx2 * o, axis=0, keepdims=True)
        dob = (dx2 * gate).astype(BF16)
        dmerged = _dot_nt(dob, wo_ref[...])
        dpa, dpr = (dmerged * sa).astype(BF16), (dmerged * sr).astype(BF16)
        dya, dyr = _dot_nt(dpa, wap_ref[...]), _dot_nt(dpr, wrp_ref[...])
        dx2_ref[...] = dx2
        mg_ref[...] = mb
        do_ref[...] = dob
        dpa_ref[...] = dpa
        dpr_ref[...] = dpr
        dya_ref[...] = dya
        dyr_ref[...] = dyr
        dc_ref[:, 0:D_MODEL] = (dmerged * pa * sa * (1.0 - sa)).astype(BF16)
        dc_ref[:, D_MODEL:2 * D_MODEL] = (dmerged * pr * sr * (1.0 - sr)).astype(BF16)

        @pl.when(i == 0)
        def _():
            dfg_ref[...] = jnp.zeros_like(dfg_ref)
            dgate_ref[...] = jnp.zeros_like(dgate_ref)
            loss_ref[...] = jnp.zeros_like(loss_ref)

        dfg_ref[...] += dfg_t
        dgate_ref[...] += dgate_t
        loss_ref[...] += jnp.broadcast_to(loss_t, loss_ref.shape)

    tok = lambda w: pl.BlockSpec((tm, w), lambda i: (i, 0))
    blk = lambda cb: pl.BlockSpec((tm, CB), lambda i, cb=cb: (i, cb))
    wfull = pl.BlockSpec((D_MODEL, D_MODEL), lambda i: (0, 0), pipeline_mode=pl.Buffered(1))
    row = lambda w: pl.BlockSpec((1, w), lambda i: (0, 0))
    out_shape = (
        jax.ShapeDtypeStruct((T, D_MODEL), F32), jax.ShapeDtypeStruct((T, D_MODEL), BF16),
        jax.ShapeDtypeStruct((T, D_MODEL), BF16), jax.ShapeDtypeStruct((T, D_MODEL), BF16),
        jax.ShapeDtypeStruct((T, D_MODEL), BF16), jax.ShapeDtypeStruct((T, D_MODEL), F32),
        jax.ShapeDtypeStruct((T, D_MODEL), F32), jax.ShapeDtypeStruct((T, 2 * D_MODEL), BF16),
        jax.ShapeDtypeStruct((1, D_MODEL), F32), jax.ShapeDtypeStruct((1, D_MODEL), F32),
        jax.ShapeDtypeStruct((1, 128), F32),
    )
    return pl.pallas_call(
        body, out_shape=out_shape, grid=(T // tm,), name="merge_and_head",
        in_specs=[tok(D_MODEL), tok(D_MODEL), tok(D_MODEL), tok(D_MODEL), blk(CB_MA), blk(CB_MA + 1), blk(CB_MR),
                  blk(CB_MR + 1), wfull, wfull, wfull, row(ADA_W), row(D_MODEL)],
        out_specs=(tok(D_MODEL),) * 7 + (tok(2 * D_MODEL), row(D_MODEL), row(D_MODEL), row(128)),
        compiler_params=_cp("arbitrary"),
    )(x, target, y_attn, y_rnn, proj, proj, proj, proj, wap, wrp, wo, *_in_hbm(mod_row, final_g))


def _attn_backward(proj, qr_b, kr_b, d_y, tabs, sinks):
    T = proj.shape[0]
    nb = T // BLOCK

    def body(qrb_ref, krc_ref, krp_ref, vc_ref, vp_ref, g0_ref, g1_ref, dy_ref, cc, sac, sbc, cp_, sap, sbp, sink_ref,
             dq_ref, dkv_ref, dg_ref, dsink_ref, carry):
        n = pl.program_id(0)

        @pl.when(n == 0)
        def _():
            carry[...] = jnp.zeros_like(carry)
            dsink_ref[...] = jnp.zeros_like(dsink_ref)

        @pl.when(n < nb)
        def _():
            tc = tcur = (cc[...], sac[...], sbc[...])
            tprev = (cp_[...], sap[...], sbp[...])
            qr, kr_cur, kr_prev = qrb_ref[...], krc_ref[...], krp_ref[...]
            v_cur, v_prev = vc_ref[...], vp_ref[...]
            g = jnp.concatenate([g0_ref[...], g1_ref[...]], axis=1)
            sg = _sigmoid(g)
            dy = dy_ref[...]
            d_o = dy * (g * sg)
            mask = _attn_mask(n)
            lane = lax.broadcasted_iota(jnp.int32, (1, 128), 1)
            rowg = lax.broadcasted_iota(jnp.int32, (GROUP * BLOCK, 1), 0) // BLOCK
            o_parts, dq_parts = [], []
            dk_cols, dv_cols = [None, None], [None, None]
            dsink = jnp.zeros((1, 128), F32)
            heads = range(N_KV)
            k2s = [_kv_pair_operand(kr_prev, kr_cur, kh) for kh in heads]
            v2s = [_kv_pair_operand(v_prev, v_cur, kh) for kh in heads]
            scores = [_attn_scores(qr, k2s[kh], kh) for kh in heads]
            do2s = [_pair_rows(d_o, kh).astype(BF16) for kh in heads]
            dpns = [_fold(_restack(_dot_nt(do2s[kh], v2s[kh])), mask[0]) for kh in heads]
            probs = [_attn_softmax(scores[kh][1], _sink_col(sink_ref, kh), mask) for kh in heads]
            p_bigs = [_unrestack(_unfold(probs[kh][0].astype(BF16), mask[0])) for kh in heads]
            o_bigs = [_dot(p_bigs[kh], v2s[kh]) for kh in heads]
            dv2s = [_dot_tn(p_bigs[kh], do2s[kh]) for kh in heads]
            deltas = [jnp.sum(probs[kh][0] * dpns[kh], axis=-1, keepdims=True) for kh in heads]
            ds_bigs = [_unrestack(_unfold((probs[kh][0] * (dpns[kh] - deltas[kh])).astype(BF16), mask[0])) for kh in heads]
            dq2s = [_dot(ds_bigs[kh], k2s[kh]) for kh in heads]
            dk2s = [_dot_tn(ds_bigs[kh], scores[kh][0]) for kh in heads]
            for kh in heads:
                o_parts += [o_bigs[kh][0:BLOCK], o_bigs[kh][BLOCK:2 * BLOCK]]
                dq_parts += [dq2s[kh][0:BLOCK], dq2s[kh][BLOCK:2 * BLOCK]]
                dk_c, dv_c = _fold_pair(dk2s[kh], kh), _fold_pair(dv2s[kh], kh)
                c = kh // 2
                dk_cols[c] = dk_c if dk_cols[c] is None else dk_cols[c] + dk_c
                dv_cols[c] = dv_c if dv_cols[c] is None else dv_cols[c] + dv_c
                ds_rows = probs[kh][1] * deltas[kh]
                for gq in range(GROUP):
                    val = -jnp.sum(jnp.where(rowg == gq, ds_rows, 0.0), axis=0, keepdims=True)
                    dsink = dsink + jnp.where(lane == GROUP * kh + ROW_GROUP_HEAD[gq], val, 0.0)
            o = jnp.concatenate(o_parts, axis=1)
            dg_ref[...] = (dy * o * (sg * (1.0 + g * (1.0 - sg)))).astype(BF16)
            dq_ref[...] = (_unrope(jnp.concatenate(dq_parts, axis=1), *tc) * ATTN_SCALE).astype(BF16)
            dk_all, dv_all = jnp.concatenate(dk_cols, axis=1), jnp.concatenate(dv_cols, axis=1)
            dk_prev = _unrope(dk_all[0:BLOCK], *tprev)
            dk_cur = _unrope(dk_all[BLOCK:2 * BLOCK], *tcur)
            dv_prev, dv_cur = dv_all[0:BLOCK], dv_all[BLOCK:2 * BLOCK]
            dkv_ref[...] = (carry[...] + jnp.concatenate([dk_prev, dv_prev], axis=1)).astype(BF16)
            carry[...] = jnp.concatenate([dk_cur, dv_cur], axis=1)
            dsink_ref[...] += dsink

        @pl.when(n == nb)
        def _():
            dkv_ref[...] = carry[...].astype(BF16)

    cur = lambda w, cb: pl.BlockSpec((BLOCK, w), lambda n, cb=cb: (jnp.minimum(n, nb - 1), cb))
    prev = lambda w, cb: pl.BlockSpec((BLOCK, w), lambda n, cb=cb: (jnp.maximum(jnp.minimum(n, nb - 1) - 1, 0), cb))
    out_shape = (jax.ShapeDtypeStruct((T, D_MODEL), BF16), jax.ShapeDtypeStruct((T, 2 * KV_W), BF16),
                 jax.ShapeDtypeStruct((T, D_MODEL), BF16), jax.ShapeDtypeStruct((1, 128), F32))
    return pl.pallas_call(
        body, out_shape=out_shape, grid=(nb + 1,), name="attn_backward",
        in_specs=[cur(D_MODEL, 0), cur(KV_W, 0), prev(KV_W, 0), cur(KV_W, V_COL_BLOCK), prev(KV_W, V_COL_BLOCK),
                  cur(CB, CB_GA), cur(CB, CB_GA + 1), cur(D_MODEL, 0),
                  cur(128, 0), cur(128, 0), cur(128, 0), prev(128, 0), prev(128, 0), prev(128, 0),
                  pl.BlockSpec(memory_space=pltpu.SMEM)],
        out_specs=(cur(D_MODEL, 0), pl.BlockSpec((BLOCK, 2 * KV_W), lambda n: (jnp.maximum(n - 1, 0), 0)),
                   cur(D_MODEL, 0), pl.BlockSpec((1, 128), lambda n: (0, 0))),
        scratch_shapes=[pltpu.VMEM((BLOCK, 2 * KV_W), F32)],
        compiler_params=_cp("arbitrary"),
    )(qr_b, kr_b, kr_b, proj, proj, proj, proj, d_y, *tabs, *tabs, sinks)


def _rnn_backward(proj, pos_col, h_rnn, saved, d_y, conv_w, rwa, rwx, lam):
    T = proj.shape[0]
    tr = min(T, 256)
    nt = T // tr
    hb = tr // 8

    def body(x0, x1, xh0, xh1, g0, g1, pos_ref, h_ref, hh_ref, xc_ref, r_ref, i_ref, a_ref, mult_ref, dy_ref,
             cw_ref, wa_ref, wx_ref, lam_ref, db_ref, dcw_ref, dcb_ref, dwa_ref, dwx_ref, dba_ref, dbx_ref, dlam_ref,
             xbuf, hbuf, dbuf, gbuf, lbuf, mu_carry, dxc_head):
        step = pl.program_id(0)
        first_tile = step == nt - 1

        @pl.when(step == 0)
        def _():
            mu_carry[...] = jnp.zeros_like(mu_carry)
            dxc_head[...] = jnp.zeros_like(dxc_head)
            for ref in (dcw_ref, dcb_ref, dwa_ref, dwx_ref, dba_ref, dbx_ref, dlam_ref):
                ref[...] = jnp.zeros_like(ref)

        xr = jnp.concatenate([x0[...], x1[...]], axis=1)
        tail = jnp.where(first_tile, 0.0, jnp.concatenate([xh0[...], xh1[...]], axis=1))
        lam_v = lam_ref[...]
        sp = _softplus(-lam_v)
        reset = pos_ref[...] == 0
        cw = cw_ref[...]
        xbuf[0:8, :] = tail
        xbuf[8:tr + 8, :] = xr
        g = jnp.concatenate([g0[...], g1[...]], axis=1)
        sg = _sigmoid(g)
        dy = dy_ref[...]
        h = h_ref[...]
        db_ref[:, D_MODEL:2 * D_MODEL] = (dy * h * (sg * (1.0 + g * (1.0 - sg)))).astype(BF16)
        gbuf[...] = dy * (g * sg)
        top = _scan_backward(a_ref, gbuf, lbuf, mu_carry[0:1, :], tr)
        mu_carry[...] = jnp.broadcast_to(top, mu_carry.shape)
        hbuf[0:8, :] = jnp.where(first_tile, 0.0, hh_ref[...])
        hbuf[8:tr + 8, :] = h
        live = jnp.logical_not(reset)
        dbuf[tr:tr + 8, :] = dxc_head[...]
        for j in range(RNN_BLOCKS):
            sl = slice(RNN_BW * j, RNN_BW * (j + 1))
            lam_t, h_prev = lbuf[:, sl], hbuf[pl.ds(7, tr), sl]
            xc, r, i, a, mult = xc_ref[:, sl], r_ref[:, sl], i_ref[:, sl], a_ref[:, sl], mult_ref[:, sl]
            d_a = jnp.where(live, lam_t * h_prev, 0.0)
            d_mult = jnp.where(live, lam_t * (i * xc), 0.0)
            d_ixc = lam_t * mult
            d_i = d_ixc * xc
            d_log_a = d_a * a - d_mult * (a * a / mult)
            d_za = d_log_a * (-LRU_C * sp[:, sl]) * (r * (1.0 - r))
            d_zx = d_i * (i * (1.0 - i))
            dlam_ref[:, sl] += jnp.sum(d_log_a * r, axis=0, keepdims=True) * (LRU_C * _sigmoid(-lam_v[:, sl]))
            dba_ref[:, sl] += jnp.sum(d_za, axis=0, keepdims=True)
            dbx_ref[:, sl] += jnp.sum(d_zx, axis=0, keepdims=True)
            xcb, dzab, dzxb = xc.astype(BF16), d_za.astype(BF16), d_zx.astype(BF16)
            dwa_ref[j] += _dot_tn(xcb, dzab)
            dwx_ref[j] += _dot_tn(xcb, dzxb)
            d_xc = d_ixc * i + (_dot_nt(dzab, wa_ref[j]) + _dot_nt(dzxb, wx_ref[j]))
            dcb_ref[:, sl] += jnp.sum(d_xc, axis=0, keepdims=True)
            for k in range(CONV_W):
                tap = xr[:, sl] if k == CONV_W - 1 else xbuf[pl.ds(8 - (CONV_W - 1 - k), tr), sl]
                dcw_ref[k:k + 1, sl] += jnp.sum(d_xc * tap, axis=0, keepdims=True)
            dbuf[0:tr, sl] = d_xc
            d_xr = d_xc * cw[CONV_W - 1:CONV_W, sl]
            for k in range(CONV_W - 1):
                d_xr = d_xr + dbuf[pl.ds(CONV_W - 1 - k, tr), sl] * cw[k:k + 1, sl]
            dxc_head[:, sl] = d_xc[0:8, :]
            db_ref[:, sl] = d_xr.astype(BF16)

    rev = lambda s: nt - 1 - s
    blk = lambda cb: pl.BlockSpec((tr, CB), lambda s, cb=cb: (rev(s), cb))
    halo = lambda w, cb: pl.BlockSpec((8, w), lambda s, cb=cb: (jnp.maximum(rev(s) * hb - 1, 0), cb))
    tok = lambda w: pl.BlockSpec((tr, w), lambda s: (rev(s), 0))
    row = lambda w: pl.BlockSpec((1, w), lambda s: (0, 0))
    full3 = pl.BlockSpec((RNN_BLOCKS, RNN_BW, RNN_BW), lambda s: (0, 0, 0))
    cwspec = pl.BlockSpec((CONV_W, D_MODEL), lambda s: (0, 0))
    vec = jax.ShapeDtypeStruct((1, D_MODEL), F32)
    gate_w = jax.ShapeDtypeStruct((RNN_BLOCKS, RNN_BW, RNN_BW), F32)
    out_shape = (jax.ShapeDtypeStruct((T, 2 * D_MODEL), BF16), jax.ShapeDtypeStruct((CONV_W, D_MODEL), F32), vec,
                 gate_w, gate_w, vec, vec, vec)
    big = lambda: pltpu.VMEM((tr, D_MODEL), F32)
    ext = lambda: pltpu.VMEM((tr + 8, D_MODEL), F32)
    return pl.pallas_call(
        body, out_shape=out_shape, grid=(nt,), name="rnn_backward",
        in_specs=[blk(CB_XR), blk(CB_XR + 1), halo(CB, CB_XR), halo(CB, CB_XR + 1), blk(CB_GR), blk(CB_GR + 1),
                  pl.BlockSpec((tr, 1), lambda s: (rev(s), 0)), tok(D_MODEL), halo(D_MODEL, 0)] + [tok(D_MODEL)] * 6
        + [cwspec, full3, full3, row(D_MODEL)],
        out_specs=(tok(2 * D_MODEL), cwspec, row(D_MODEL), full3, full3, row(D_MODEL), row(D_MODEL), row(D_MODEL)),
        scratch_shapes=[ext(), ext(), ext(), big(), big(), pltpu.VMEM((8, D_MODEL), F32), pltpu.VMEM((8, D_MODEL), F32)],
        compiler_params=_cp("arbitrary"),
    )(proj, proj, proj, proj, proj, proj, pos_col, h_rnn, h_rnn, *saved, d_y, *_in_hbm(conv_w, rwa, rwx, lam))


def _input_backward(pieces, w_in, x, dx2, mod_row, norm_g):
    T = x.shape[0]
    tm = min(T, 512)
    n = len(pieces)

    def body(*refs):
        d_refs = refs[:n]
        w_ref, x_ref, dx2_ref, mod_ref, g_ref, gx_ref, dshift_ref, dscale_ref, dg_ref = refs[n:]
        i = pl.program_id(0)
        dh = None
        for d_ref, (_, start, count) in zip(d_refs, pieces):
            part = _dot_nt(d_ref[...], w_ref[:, start * CB:(start + count) * CB])
            dh = part if dh is None else dh + part

        @pl.when(i == 0)
        def _():
            dshift_ref[...] = jnp.zeros_like(dshift_ref)
            dscale_ref[...] = jnp.zeros_like(dscale_ref)
            dg_ref[...] = jnp.zeros_like(dg_ref)

        xf = x_ref[...]
        r1 = _rms(xf)
        xn = xf * r1
        gn = g_ref[...]
        s1 = 1.0 + mod_ref[:, D_MODEL:2 * D_MODEL]
        dshift_ref[...] += jnp.sum(dh, axis=0, keepdims=True)
        dscale_ref[...] += jnp.sum(dh * (xn * gn), axis=0, keepdims=True)
        dg_ref[...] += jnp.sum(dh * s1 * xn, axis=0, keepdims=True)
        dxn = dh * s1 * gn
        gx_ref[...] = dx2_ref[...] + r1 * (dxn - xn * jnp.mean(dxn * xn, axis=-1, keepdims=True))

    tok = lambda w: pl.BlockSpec((tm, w), lambda i: (i, 0))
    row = lambda w: pl.BlockSpec((1, w), lambda i: (0, 0))
    vec = jax.ShapeDtypeStruct((1, D_MODEL), F32)
    return pl.pallas_call(
        body, out_shape=(jax.ShapeDtypeStruct((T, D_MODEL), F32), vec, vec, vec), grid=(T // tm,), name="input_backward",
        in_specs=[tok(c * CB) for _, _, c in pieces]
        + [pl.BlockSpec((D_MODEL, IN_W), lambda i: (0, 0), pipeline_mode=pl.Buffered(1)), tok(D_MODEL), tok(D_MODEL),
           row(ADA_W), row(D_MODEL)],
        out_specs=(tok(D_MODEL), row(D_MODEL), row(D_MODEL), row(D_MODEL)),
        compiler_params=_cp("arbitrary"),
    )(*[p[0] for p in pieces], w_in, x, dx2, *_in_hbm(mod_row, norm_g))


def _weight_grad(a, pieces, tag, a_is_transposed=False):
    M, T = a.shape if a_is_transposed else a.shape[::-1]
    n_blocks = sum(count for _, _, count in pieces)
    n = len(pieces)
    contract = _dot if a_is_transposed else _dot_tn

    def body(*refs):
        a_ref, b_refs, o_ref = refs[0], refs[1:1 + n], refs[-1]
        j = pl.program_id(0)
        for b_ref, (_, start, count) in zip(b_refs, pieces):
            @pl.when((j >= start) & (j < start + count))
            def _(b_ref=b_ref):
                o_ref[...] = contract(a_ref[...], b_ref[...])

    def piece_spec(start, count):
        return pl.BlockSpec((T, CB), lambda j: (0, jnp.clip(j - start, 0, count - 1)))

    return pl.pallas_call(
        body, out_shape=jax.ShapeDtypeStruct((M, n_blocks * CB), F32), grid=(n_blocks,), name=f"weight_grad_{tag}",
        in_specs=[pl.BlockSpec(a.shape, lambda j: (0, 0), pipeline_mode=pl.Buffered(1))] + [piece_spec(s, c) for _, s, c in pieces],
        out_specs=pl.BlockSpec((M, CB), lambda j: (0, j)), compiler_params=_cp("arbitrary"),
    )(a, *[p[0] for p in pieces])


def _adamw(w, g, m, v):
    m = ADAM_B1 * m + (1.0 - ADAM_B1) * g
    v = ADAM_B2 * v + (1.0 - ADAM_B2) * (g * g)
    m_hat = m / (1.0 - ADAM_B1 ** ADAM_STEP)
    v_hat = v / (1.0 - ADAM_B2 ** ADAM_STEP)
    delta = -ADAM_LR * (m_hat / (jnp.sqrt(v_hat) + ADAM_EPS) + ADAM_WD * w)
    return delta, m, v


def _sum_landed(kind, owns, lands, where, tag):
    n = len(owns)
    land = lands[0]
    if kind == "in":
        R, C = land.shape[1:]
        tr = 256
        grid = (R // tr,)
        own_spec = pl.BlockSpec((tr, C), lambda i, w: (i, w[0]))
        land_spec = pl.BlockSpec((3, tr, C), lambda i, w: (0, i, 0))
        out_spec = pl.BlockSpec((1, tr, C), lambda i, w: (w[1], i, 0))
        out_shape = (2, R, C)
        pick = lambda ref: ref[...]
    elif kind == "sq":
        R, C = land.shape[1:]
        grid = (1,)
        own_spec = pl.BlockSpec((1, R, C), lambda i, w: (w[0], 0, 0))
        land_spec = pl.BlockSpec((3, R, C), lambda i, w: (0, 0, 0))
        out_spec = pl.BlockSpec((1, R, C), lambda i, w: (w[1], 0, 0))
        out_shape = (2, R, C)
        pick = lambda ref: ref[0]
    else:
        B, R, C = land.shape[1:]
        grid = (1,)
        own_spec = pl.BlockSpec((B, 1, R, C), lambda i, w: (0, w[0], 0, 0))
        land_spec = pl.BlockSpec((3, B, R, C), lambda i, w: (0, 0, 0, 0))
        out_spec = pl.BlockSpec((B, 1, R, C), lambda i, w: (0, w[1], 0, 0))
        out_shape = (B, 2, R, C)
        pick = lambda ref: ref[:, 0]

    def body(w_ref, *refs):
        for k in range(n):
            own_ref, l_ref, o_ref = refs[k], refs[n + k], refs[2 * n + k]
            total = ((pick(own_ref) + l_ref[0].astype(F32)) + l_ref[1].astype(F32)) + l_ref[2].astype(F32)
            if kind == "rg":
                o_ref[:, 0] = total
            else:
                o_ref[0] = total

    grid_spec = pltpu.PrefetchScalarGridSpec(num_scalar_prefetch=1, grid=grid, in_specs=[own_spec] * n + [land_spec] * n,
                                             out_specs=(out_spec,) * n)
    return list(pl.pallas_call(
        body, out_shape=(jax.ShapeDtypeStruct(out_shape, F32),) * n, grid_spec=grid_spec, name=f"sum_landed_{tag}",
        compiler_params=_cp("parallel"),
    )(where, *owns, *lands))


def _adamw_shard(gs, ws, ms, vs, tag):
    n = len(ws)
    R, C = ws[0].shape
    tr = min(R, 256 if n == 1 else 64)

    def body(*refs):
        for k in range(n):
            g = refs[k][...]
            d, nm, nv = _adamw(refs[n + k][...], g, refs[2 * n + k][...], refs[3 * n + k][...])
            out = refs[4 * n + 4 * k:4 * n + 4 * k + 4]
            out[0][...] = g
            out[1][...] = d
            out[2][...] = nm
            out[3][...] = nv

    spec = pl.BlockSpec((tr, C), lambda i: (i, 0))
    sds = jax.ShapeDtypeStruct((R, C), F32)
    outs = pl.pallas_call(
        body, out_shape=(sds,) * (4 * n), grid=(R // tr,), name=f"adamw_{tag}",
        in_specs=[spec] * (4 * n), out_specs=(spec,) * (4 * n), compiler_params=_cp("parallel"),
    )(*gs, *_in_hbm(*ws, *ms, *vs))
    return [outs[4 * k:4 * k + 4] for k in range(n)]


def _adamw_w_ada(c_t, dmod_cols, w, m, v):
    R, C = w.shape

    def body(ct_ref, dm_ref, w_ref, m_ref, v_ref, g_ref, d_ref, nm_ref, nv_ref):
        g = _dot(ct_ref[...].astype(BF16), dm_ref[...].astype(BF16))
        d, nm, nv = _adamw(w_ref[...], g, m_ref[...], v_ref[...])
        g_ref[...] = g
        d_ref[...] = d
        nm_ref[...] = nm
        nv_ref[...] = nv

    tr = 256
    spec = pl.BlockSpec((tr, C), lambda i: (i, 0))
    sds = jax.ShapeDtypeStruct((R, C), F32)
    return pl.pallas_call(
        body, out_shape=(sds,) * 4, grid=(R // tr,), name="adamw_w_ada",
        in_specs=[pl.BlockSpec((tr, 128), lambda i: (i, 0)), pl.BlockSpec((128, C), lambda i: (0, 0))] + [spec] * 3,
        out_specs=(spec,) * 4, compiler_params=_cp("parallel"),
    )(c_t, dmod_cols, w, m, v)


def _adamw_small(small_all, ws, ms, vs):
    def body(s_ref, w_ref, m_ref, v_ref, g_ref, d_ref, nm_ref, nv_ref):
        g = s_ref[0]
        for b in range(1, N_DEV):
            g = g + s_ref[b]
        d, nm, nv = _adamw(w_ref[...], g, m_ref[...], v_ref[...])
        g_ref[...] = g
        d_ref[...] = d
        nm_ref[...] = nm
        nv_ref[...] = nv

    sds = jax.ShapeDtypeStruct((SMALL_ROWS, D_MODEL), F32)
    return pl.pallas_call(
        body, out_shape=(sds,) * 4, name="adamw_small", in_specs=[VMEM_SPEC] * 4, out_specs=(VMEM_SPEC,) * 4,
        compiler_params=pltpu.CompilerParams(vmem_limit_bytes=VMEM_LIMIT_V7X),
    )(small_all, ws, ms, vs)


ROW_MOD, ROW_NORM_G, ROW_CONV_B, ROW_BA, ROW_BX, ROW_LAM, ROW_FINAL_G, ROW_SINKS, ROW_CONV_W, ROW_LOSS = 0, 3, 4, 5, 6, 7, 8, 9, 10, 14


def _pack_small(b_ada, norm_g, conv_b, ba, bx, lam, final_g, sinks, conv_w_full, loss_row=None):
    lane_pad = lambda a: jnp.pad(a.reshape(1, -1), ((0, 0), (0, D_MODEL - a.size)))
    rows = [b_ada.reshape(3, D_MODEL), norm_g, conv_b, ba, bx, lam, final_g.reshape(1, D_MODEL), lane_pad(sinks), conv_w_full,
            jnp.zeros((1, D_MODEL), F32) if loss_row is None else lane_pad(loss_row),
            jnp.zeros((SMALL_ROWS - ROW_LOSS - 1, D_MODEL), F32)]
    return jnp.concatenate([r.astype(F32) for r in rows], axis=0)


def kernel(x, c, positions, w_ada, b_ada, norm_g, w_in, attn_sinks, conv_w, conv_b, rg_wa, rg_ba, rg_wx, rg_bx, rg_lambda, w_attn_proj, w_rnn_proj, w_out, final_g, loss_target, m_w_ada, m_b_ada, m_norm_g, m_w_in, m_attn_sinks, m_conv_w, m_conv_b, m_rg_wa, m_rg_ba, m_rg_wx, m_rg_bx, m_rg_lambda, m_w_attn_proj, m_w_rnn_proj, m_w_out, m_final_g, v_w_ada, v_b_ada, v_norm_g, v_w_in, v_attn_sinks, v_conv_w, v_conv_b, v_rg_wa, v_rg_ba, v_rg_wx, v_rg_bx, v_rg_lambda, v_w_attn_proj, v_w_rnn_proj, v_w_out, v_final_g):
    T = x.shape[1]
    my_chip = lax.axis_index("x") * 2 + lax.axis_index("y")
    my_dev = my_chip * 2 + lax.axis_index("c")
    x2d, tgt = x[0], loss_target[0]
    pos_col = positions.reshape(T, 1)

    chip_idx = my_chip.reshape(1).astype(jnp.int32)
    c_idx = lax.axis_index("c").reshape(1).astype(jnp.int32)
    sq_place = ((D_MODEL, D_MODEL), (SHARD_ROWS, D_MODEL), lambda chip: (chip, 0))
    rg_place = ((RNN_BLOCKS, RNN_BW, RNN_BW), (RNN_BLOCKS, SHARD_RG, RNN_BW), lambda chip: (0, chip, 0))
    in_place = ((D_MODEL, IN_W), (D_MODEL, SHARD_IN), lambda chip: (0, chip))
    placed = _cast_place([w_in[0], w_attn_proj[0], w_rnn_proj[0], w_out[0], rg_wa[0], rg_wx[0]], chip_idx,
                         [in_place, sq_place, sq_place, sq_place, rg_place, rg_place])
    cw_chips, c_all, mod_chips = _gather_mod(c.reshape(1, 1, D_MODEL), w_ada[0], conv_w[0])
    g_ssems, g_rsems, fulls, g_token = _gather_start([p.reshape(s) for p, s in zip(placed, FULL_SHAPES)], mod_chips)
    conv_w_f = jnp.transpose(cw_chips, (1, 0, 2)).reshape(CONV_W, D_MODEL)
    mod_all = jnp.transpose(mod_chips, (1, 0, 2)).reshape(N_DEV, ADA_W) + b_ada
    mod_row = lax.dynamic_slice_in_dim(mod_all, my_dev, 1, axis=0) + g_token[0:1, 0:1]

    h, h_t, tabs = _prenorm(x2d, mod_row, norm_g, pos_col)
    w_in_v = fulls[0]
    proj = _in_projection(h, w_in_v.reshape(D_MODEL, IN_W), chip_idx, None, "own")
    for k, mask in enumerate(CHIP_MASKS):
        w_in_v = _gather_wait(g_ssems[k], g_rsems[k], [w_in_v], [0], proj, f"w_in_{k}")[0]
        w_in_v = _forward_halves([w_in_v], [(0, 0, k)], f"w_in_{k}")[0]
        from_chip = (chip_idx ^ (mask >> 1)).astype(jnp.int32)
        proj = _in_projection(h, w_in_v.reshape(D_MODEL, IN_W), from_chip, proj, f"from_{k}")
    w_in_f = w_in_v.reshape(D_MODEL, IN_W)
    rest = _gather_wait(g_ssems[3], g_rsems[3], list(fulls[1:]), [1, 2, 3, 4, 5], proj, "rest")
    rest = _forward_halves(rest, [(idx - 1, idx, k) for idx in range(1, N_BIG) for k in range(3)], "rest")
    wap_f, wrp_f, wo_f = (g.reshape(D_MODEL, D_MODEL) for g in rest[0:3])
    rwa_f, rwx_f = (g.reshape(RNN_BLOCKS, RNN_BW, RNN_BW) for g in rest[3:5])
    y_attn, qr_b, kr_b = _attn_forward(proj, tabs, attn_sinks)
    y_rnn, h_rnn, *rnn_saved = _rnn_forward(proj, pos_col, conv_w_f, conv_b, rwa_f, rwx_f, rg_ba, rg_bx, rg_lambda)
    (dx2, merged, d_o, d_pa, d_pr, d_ya, d_yr, d_c, d_final_g, d_gate, loss_vec) = _merge_and_head(
        x2d, tgt, y_attn, y_rnn, proj, wap_f, wrp_f, wo_f, mod_row, final_g.reshape(1, D_MODEL))

    sq = (N_CHIPS, 2, SHARD_ROWS // 2, D_MODEL)
    rg = (RNN_BLOCKS, N_CHIPS, 2, SHARD_RG // 2, RNN_BW)
    rg_flat = (RNN_BLOCKS * N_CHIPS, 2, SHARD_RG // 2, RNN_BW)

    def chip_sum_and_start(views, axes, flat, unflat, tags_, kinds_, group):
        from_sib = _swap_halves(views, axes)
        exact, rounded = [None] * len(views), [None] * len(views)
        for shape in dict.fromkeys(flat):
            ids = [k for k, f in enumerate(flat) if f == shape]
            ex, ro = _presum([views[k].reshape(shape) for k in ids],
                             [from_sib[k].reshape(shape[:1] + shape[2:]) for k in ids], c_idx, tags_[ids[0]])
            for k, e, r in zip(ids, ex, ro):
                exact[k], rounded[k] = e.reshape(unflat[k]), r.reshape(unflat[k])
        return _exchange_start(rounded, kinds_, group), exact

    g_ap = _weight_grad(y_attn, [(d_pa, 0, 2)], "w_attn_proj")
    g_rp = _weight_grad(y_rnn, [(d_pr, 0, 2)], "w_rnn_proj")
    g_o = _weight_grad(merged, [(d_o, 0, 2)], "w_out")
    sq_half = (N_CHIPS, SHARD_ROWS // 2, D_MODEL)
    started1, own1 = chip_sum_and_start([g_ap.reshape(sq), g_rp.reshape(sq), g_o.reshape(sq)], [1, 1, 1], [sq] * 3, [sq_half] * 3,
                                  ["w_attn_proj", "w_rnn_proj", "w_out"], ["sq"] * 3, "proj")
    d_q, d_kv, d_ga, d_sinks = _attn_backward(proj, qr_b, kr_b, d_ya, tabs, attn_sinks + started1[4][0, 0])
    d_b, d_conv_w, d_conv_b, d_rwa, d_rwx, d_ba, d_bx, d_lam = _rnn_backward(
        proj, pos_col, h_rnn, rnn_saved, d_yr, conv_w_f, rwa_f, rwx_f, rg_lambda)
    pieces = [(d_q, CB_Q, 2), (d_kv, CB_KV, 1), (d_ga, CB_GA, 2), (d_b, CB_XR, 4), (d_c, CB_MA, 4)]
    g_in = _weight_grad(h_t, pieces, "w_in", a_is_transposed=True)
    started2, own2 = chip_sum_and_start(
        [g_in.reshape(2, D_MODEL // 2, IN_W), d_rwa.reshape(rg), d_rwx.reshape(rg)], [0, 2, 2],
        [(1, 2, D_MODEL // 2, IN_W), rg_flat, rg_flat],
        [(D_MODEL // 2, IN_W), (RNN_BLOCKS, N_CHIPS, SHARD_RG // 2, RNN_BW), (RNN_BLOCKS, N_CHIPS, SHARD_RG // 2, RNN_BW)],
        ["w_in", "rg_wa", "rg_wx"], ["in", "rg", "rg"], "in")
    grad_x, d_shift, d_scale, d_norm_g = _input_backward(pieces, w_in_f, x2d, dx2, mod_row + started2[4][0, 0], norm_g)

    d_mod = jnp.concatenate([d_shift, d_scale, d_gate], axis=1)
    small = _pack_small(d_mod, d_norm_g, d_conv_b, d_ba, d_bx, d_lam, d_final_g, d_sinks[:, :N_HEADS], d_conv_w, loss_vec)
    small_all = _gather_small(small)
    _, lands1 = _exchange_wait(*started1[:4], grad_x, "proj")
    _, lands2 = _exchange_wait(*started2[:4], grad_x, "in")
    tags = ["w_in", "w_attn_proj", "w_rnn_proj", "w_out", "rg_wa", "rg_wx"]
    chip_sums = [own2[0]] + list(own1) + list(own2[1:])
    lands = [lands2[0]] + list(lands1) + list(lands2[1:])
    where = jnp.concatenate([chip_idx, c_idx])
    kinds = ["in", "sq", "sq", "sq", "rg", "rg"]
    groups = [[0], [1, 2, 3], [4, 5]]
    halves = [None] * 6
    for ids in groups:
        for i, half in zip(ids, _sum_landed(kinds[ids[0]], [chip_sums[i] for i in ids], [lands[i] for i in ids], where,
                                            tags[ids[0]])):
            halves[i] = half
    grads = _assemble_with_sibling(halves, [0, 0, 0, 0, 1, 1])
    shapes2d = [(D_MODEL, SHARD_IN), (SHARD_ROWS, D_MODEL), (SHARD_ROWS, D_MODEL), (SHARD_ROWS, D_MODEL),
                (RNN_BLOCKS * SHARD_RG, RNN_BW), (RNN_BLOCKS * SHARD_RG, RNN_BW)]
    big_w = [w_in, w_attn_proj, w_rnn_proj, w_out, rg_wa, rg_wx]
    big_m = [m_w_in, m_w_attn_proj, m_w_rnn_proj, m_w_out, m_rg_wa, m_rg_wx]
    big_v = [v_w_in, v_w_attn_proj, v_w_rnn_proj, v_w_out, v_rg_wa, v_rg_wx]
    res = {}
    for ids in groups:
        flat2d = lambda arrs: [arrs[i].reshape(shapes2d[i]) for i in ids]
        outs = _adamw_shard(flat2d(grads), flat2d(big_w), flat2d(big_m), flat2d(big_v), tags[ids[0]])
        for i, four in zip(ids, outs):
            res[tags[i]] = [o.reshape(big_w[i].shape) for o in four]

    dmod_all = small_all[:, ROW_MOD:ROW_MOD + 3, :].reshape(N_DEV, ADA_W)
    dmod_cols = lax.dynamic_slice_in_dim(dmod_all, my_chip * SHARD_ADA, SHARD_ADA, axis=1)
    c_t = jnp.pad(jnp.transpose(c_all.reshape(N_DEV, D_MODEL)), ((0, 0), (0, 128 - N_DEV)))
    dmod_cols = jnp.pad(dmod_cols, ((0, 128 - N_DEV), (0, 0)))
    res["w_ada"] = [o.reshape(w_ada.shape) for o in _adamw_w_ada(c_t, dmod_cols, w_ada[0], m_w_ada[0], v_w_ada[0])]

    def full_conv(a):
        return lax.dynamic_update_slice_in_dim(jnp.zeros((CONV_W, D_MODEL), F32), a[0], my_chip * (D_MODEL // N_CHIPS), axis=1)

    packed = [_pack_small(p[0], p[1], p[2], p[3], p[4], p[5], p[6], p[7], full_conv(p[8])) for p in (
        (b_ada, norm_g, conv_b, rg_ba, rg_bx, rg_lambda, final_g, attn_sinks, conv_w),
        (m_b_ada, m_norm_g, m_conv_b, m_rg_ba, m_rg_bx, m_rg_lambda, m_final_g, m_attn_sinks, m_conv_w),
        (v_b_ada, v_norm_g, v_conv_b, v_rg_ba, v_rg_bx, v_rg_lambda, v_final_g, v_attn_sinks, v_conv_w))]
    small_out = _adamw_small(small_all, *packed)

    def unpack(slab):
        cw = lax.dynamic_slice_in_dim(slab[ROW_CONV_W:ROW_CONV_W + CONV_W], my_chip * (D_MODEL // N_CHIPS),
                                      D_MODEL // N_CHIPS, axis=1)
        return {
            "b_ada": slab[ROW_MOD:ROW_MOD + 3].reshape(1, ADA_W), "norm_g": slab[ROW_NORM_G:ROW_NORM_G + 1],
            "conv_b": slab[ROW_CONV_B:ROW_CONV_B + 1], "rg_ba": slab[ROW_BA:ROW_BA + 1], "rg_bx": slab[ROW_BX:ROW_BX + 1],
            "rg_lambda": slab[ROW_LAM:ROW_LAM + 1], "final_g": slab[ROW_FINAL_G], "attn_sinks": slab[ROW_SINKS:ROW_SINKS + 1, :N_HEADS],
            "conv_w": cw[None],
        }

    small_res = [unpack(s) for s in small_out]
    order = ["w_ada", "b_ada", "norm_g", "w_in", "attn_sinks", "conv_w", "conv_b", "rg_wa", "rg_ba", "rg_wx", "rg_bx",
             "rg_lambda", "w_attn_proj", "w_rnn_proj", "w_out", "final_g"]
    loss = small_out[0][ROW_LOSS, 0]
    outs = [loss, grad_x[None]]
    for kind in range(4):
        for name in order:
            outs.append(res[name][kind] if name in res else small_res[kind][name])
    return tuple(outs)
```

```python
import numpy as np
import jax
import jax.numpy as jnp
from jax import lax
from jax.experimental import pallas as pl
from jax.experimental.pallas import tpu as pltpu

F32 = jnp.float32
BF16 = jnp.bfloat16

D_MODEL = 1024
N_HEADS = 16
N_KV = 4
HEAD_DIM = 64
GROUP = N_HEADS // N_KV
BLOCK = 128
KV_W = N_KV * HEAD_DIM
ROT_HALF = 8
ROPE_THETA = 500000.0
ATTN_SCALE = 0.125
RNN_BLOCKS = 4
RNN_BW = 256
CONV_W = 4
LRU_C = 8.0
NORM_EPS = 1e-6
IN_W = 6656
CB = 512
N_CB = IN_W // CB
CB_Q, CB_KV, CB_GA, CB_XR, CB_GR, CB_MA, CB_MR = 0, 2, 3, 5, 7, 9, 11
V_COL_BLOCK = 5
N_CHIPS = 4
N_DEV = 8
SHARD_IN = IN_W // N_CHIPS
SHARD_ROWS = D_MODEL // N_CHIPS
SHARD_RG = RNN_BW // N_CHIPS
ADA_W = 3 * D_MODEL
SHARD_ADA = ADA_W // N_CHIPS
SMALL_ROWS = 16

ADAM_LR = 0.001
ADAM_B1 = 0.9
ADAM_B2 = 0.999
ADAM_EPS = 1e-08
ADAM_WD = 0.01
ADAM_STEP = 10

VMEM_LIMIT_V7X = 52 * 1024 * 1024
MESH = pl.DeviceIdType.MESH
ANY = pl.BlockSpec(memory_space=pl.ANY)
VMEM_SPEC = pl.BlockSpec(memory_space=pltpu.VMEM)


def _in_hbm(*arrays):
    return [pltpu.with_memory_space_constraint(a, pltpu.HBM) for a in arrays]


def _cp(*sem):
    return pltpu.CompilerParams(dimension_semantics=sem if sem else None, vmem_limit_bytes=VMEM_LIMIT_V7X)


def _dot(a, b):
    return jnp.dot(a, b, preferred_element_type=F32)


def _dot_nt(a, b):
    return lax.dot_general(a, b, (((1,), (1,)), ((), ())), preferred_element_type=F32)


def _dot_tn(a, b):
    return lax.dot_general(a, b, (((0,), (0,)), ((), ())), preferred_element_type=F32)


def _sigmoid(z):
    return 1.0 / (1.0 + jnp.exp(-z))


def _softplus(z):
    u = jnp.exp(-jnp.abs(z))
    log1p_u = jnp.where(u < 1e-3, u * (1.0 - u * (0.5 - u * (1.0 / 3.0))), jnp.log(1.0 + u))
    return jnp.maximum(z, 0.0) + log1p_u


def _rms(xf):
    return lax.rsqrt(jnp.mean(xf * xf, axis=-1, keepdims=True) + NORM_EPS)


def _me():
    return lax.axis_index("x"), lax.axis_index("y"), lax.axis_index("c")


def _peer(mask):
    x, y, c = _me()
    fx, fy, fc = (mask >> 2) & 1, (mask >> 1) & 1, mask & 1
    return (x ^ fx if fx else x, y ^ fy if fy else y, c ^ fc if fc else c)


def _chip_of(pos):
    return pos[0] * 2 + pos[1]


SIBLING_COLLECTIVE_ID = 0
SIBLING_ONLY = pltpu.CompilerParams(collective_id=SIBLING_COLLECTIVE_ID)


def _sibling_handshake():
    barrier = pltpu.get_barrier_semaphore()
    pl.semaphore_signal(barrier, inc=1, device_id=_peer(1), device_id_type=MESH)
    pl.semaphore_wait(barrier, 1)


CHIP_MASKS = (4, 2, 6)
ALL_MASKS = (1, 2, 3, 4, 5, 6, 7)


HBM_SPEC = pl.BlockSpec(memory_space=pltpu.HBM)
SEM_SPEC = pl.BlockSpec(memory_space=pltpu.SEMAPHORE)
SPLIT_COPY = pltpu.CompilerParams(has_side_effects=pltpu.SideEffectType.DATAFLOW_SIDE_EFFECTING)
N_BIG = 6
FULL_SHAPES = (
    (2, D_MODEL // 2, IN_W),
    (N_CHIPS, 2, SHARD_ROWS // 2, D_MODEL), (N_CHIPS, 2, SHARD_ROWS // 2, D_MODEL), (N_CHIPS, 2, SHARD_ROWS // 2, D_MODEL),
    (RNN_BLOCKS, N_CHIPS, 2, SHARD_RG // 2, RNN_BW), (RNN_BLOCKS, N_CHIPS, 2, SHARD_RG // 2, RNN_BW),
)


def _slot(full, idx, chip, half):
    if idx == 0:
        return full.at[half, :, pl.ds(pl.multiple_of(chip * SHARD_IN, 128), SHARD_IN)]
    return full.at[chip, half] if idx in (1, 2, 3) else full.at[:, chip, half]


def _three_halves(full, idx):
    return full.at[pl.ds(0, 3), 0] if idx in (1, 2, 3) else full.at[:, pl.ds(0, 3), 0]


def _gather_start(fulls, after):
    def body(*refs):
        full_refs = refs[:N_BIG]
        ssems, rsems = refs[N_BIG + 1:N_BIG + 5], refs[N_BIG + 5:N_BIG + 9]
        token = refs[2 * N_BIG + 9]
        me = _me()
        my_chip = _chip_of(me)
        for idx in range(N_BIG):
            for k, mask in enumerate(CHIP_MASKS):
                pair = k if idx == 0 else 3
                mine = _slot(full_refs[idx], idx, my_chip, me[2])
                pltpu.make_async_remote_copy(src_ref=mine, dst_ref=mine, send_sem=ssems[pair], recv_sem=rsems[pair],
                                             device_id=_peer(mask), device_id_type=MESH).start()
        token[...] = jnp.zeros_like(token)

    sem = pltpu.SemaphoreType.DMA(())
    out_shape = (sem,) * 8 + tuple(pltpu.HBM(f.shape, f.dtype) for f in fulls) + (jax.ShapeDtypeStruct((8, 128), F32),)
    outs = pl.pallas_call(
        body, out_shape=out_shape, name="gather_start",
        in_specs=[HBM_SPEC] * N_BIG + [ANY], out_specs=tuple([SEM_SPEC] * 8 + [HBM_SPEC] * N_BIG + [VMEM_SPEC]),
        input_output_aliases={i: 8 + i for i in range(N_BIG)}, compiler_params=SPLIT_COPY,
    )(*[pltpu.with_memory_space_constraint(f, pltpu.HBM) for f in fulls], after)
    return outs[0:4], outs[4:8], outs[8:8 + N_BIG], outs[8 + N_BIG]


def _gather_wait(ssem, rsem, arrays, idxs, after, tag):
    n = len(arrays)

    def body(*refs):
        full_refs, ssem_ref, rsem_ref = refs[:n], refs[n], refs[n + 1]
        me = _me()
        for full, idx in zip(full_refs, idxs):
            region = _slot(full, 0, _chip_of(me), me[2]) if idx == 0 else _three_halves(full, idx)
            arrived = pltpu.make_async_remote_copy(
                src_ref=region, dst_ref=region, send_sem=ssem_ref, recv_sem=rsem_ref, device_id=me, device_id_type=MESH)
            arrived.wait_send()
            arrived.wait_recv()

    outs = pl.pallas_call(
        body, out_shape=tuple(pltpu.HBM(a.shape, a.dtype) for a in arrays), name=f"gather_wait_{tag}",
        in_specs=[HBM_SPEC] * n + [SEM_SPEC, SEM_SPEC, ANY], out_specs=tuple([HBM_SPEC] * n),
        input_output_aliases={i: i for i in range(n)}, compiler_params=SPLIT_COPY,
    )(*arrays, ssem, rsem, after)
    return list(outs)


def _forward_halves(arrays, items, tag):
    n, m = len(arrays), len(items)

    def body(*refs):
        outs, ssem, rsem = refs[n:2 * n], refs[2 * n], refs[2 * n + 1]
        me = _me()
        sib = _peer(1)
        _sibling_handshake()
        cps = []
        for j, (pos, idx, k) in enumerate(items):
            chip = _chip_of(_peer(CHIP_MASKS[k]))
            cp = pltpu.make_async_remote_copy(
                src_ref=_slot(outs[pos], idx, chip, me[2]), dst_ref=_slot(outs[pos], idx, chip, me[2]),
                send_sem=ssem.at[j], recv_sem=rsem.at[j], device_id=sib, device_id_type=MESH)
            cp.start()
            cps.append(cp)
        for j, (pos, idx, k) in enumerate(items):
            chip = _chip_of(_peer(CHIP_MASKS[k]))
            pltpu.make_async_remote_copy(
                src_ref=_slot(outs[pos], idx, chip, me[2]), dst_ref=_slot(outs[pos], idx, chip, 1 - me[2]),
                send_sem=ssem.at[j], recv_sem=rsem.at[j], device_id=sib, device_id_type=MESH).wait_recv()
        for cp in cps:
            cp.wait_send()

    outs = pl.pallas_call(
        body, out_shape=tuple(jax.ShapeDtypeStruct(a.shape, a.dtype) for a in arrays), name=f"forward_halves_{tag}",
        in_specs=[ANY] * n, out_specs=tuple([ANY] * n), input_output_aliases={i: i for i in range(n)},
        scratch_shapes=[pltpu.SemaphoreType.DMA((m,)), pltpu.SemaphoreType.DMA((m,))], compiler_params=SIBLING_ONLY,
    )(*arrays)
    return list(outs)


def _gather_mod(c_row, w_ada_s, conv_w_s):
    def body(c_ref, wada_ref, cw_s, cw_f, call_ref, mod_ref, wsend, wrecv, lsem, csend, crecv, msend, mrecv):
        me = _me()
        my_chip = _chip_of(me)
        my_dev = my_chip * 2 + me[2]
        sends = []
        for k, mask in enumerate(CHIP_MASKS):
            cp = pltpu.make_async_remote_copy(src_ref=cw_s, dst_ref=cw_f.at[my_chip], send_sem=wsend.at[k], recv_sem=wrecv.at[k],
                                              device_id=_peer(mask), device_id_type=MESH)
            cp.start()
            sends.append(cp)
        local = [pltpu.make_async_copy(cw_s, cw_f.at[my_chip], lsem.at[0])]
        for cp in local:
            cp.start()

        call_ref[my_dev] = c_ref[0]
        csends = []
        for k, mask in enumerate(ALL_MASKS):
            cp = pltpu.make_async_remote_copy(
                src_ref=c_ref.at[0], dst_ref=call_ref.at[my_dev],
                send_sem=csend.at[k], recv_sem=crecv.at[k], device_id=_peer(mask), device_id_type=MESH)
            cp.start()
            csends.append(cp)
        for k, mask in enumerate(ALL_MASKS):
            frm = _peer(mask)
            pltpu.make_async_remote_copy(
                src_ref=c_ref.at[0], dst_ref=call_ref.at[_chip_of(frm) * 2 + frm[2]],
                send_sem=csend.at[k], recv_sem=crecv.at[k], device_id=frm, device_id_type=MESH).wait_recv()
        for cp in csends:
            cp.wait_send()

        c_all = call_ref[...].reshape(N_DEV, D_MODEL).astype(BF16)
        mod_ref[my_chip] = _dot(c_all, wada_ref[...].astype(BF16))
        msends = []
        for k, mask in enumerate(CHIP_MASKS):
            cp = pltpu.make_async_remote_copy(
                src_ref=mod_ref.at[my_chip], dst_ref=mod_ref.at[my_chip],
                send_sem=msend.at[k], recv_sem=mrecv.at[k], device_id=_peer(mask), device_id_type=MESH)
            cp.start()
            msends.append(cp)
        for k, mask in enumerate(CHIP_MASKS):
            frm = _peer(mask)
            pltpu.make_async_remote_copy(
                src_ref=mod_ref.at[my_chip], dst_ref=mod_ref.at[_chip_of(frm)],
                send_sem=msend.at[k], recv_sem=mrecv.at[k], device_id=frm, device_id_type=MESH).wait_recv()
        for cp in msends:
            cp.wait_send()

        for k, mask in enumerate(CHIP_MASKS):
            frm = _peer(mask)
            pltpu.make_async_remote_copy(src_ref=cw_s, dst_ref=cw_f.at[_chip_of(frm)], send_sem=wsend.at[k], recv_sem=wrecv.at[k],
                                         device_id=frm, device_id_type=MESH).wait_recv()
        for cp in sends:
            cp.wait_send()
        for cp in local:
            cp.wait()

    out_shape = (
        jax.ShapeDtypeStruct((N_CHIPS, CONV_W, D_MODEL // N_CHIPS), F32),
        jax.ShapeDtypeStruct((N_DEV, 1, D_MODEL), F32),
        jax.ShapeDtypeStruct((N_CHIPS, N_DEV, SHARD_ADA), F32),
    )
    return pl.pallas_call(
        body, out_shape=out_shape, name="gather_mod",
        in_specs=[VMEM_SPEC, VMEM_SPEC, ANY], out_specs=(ANY, VMEM_SPEC, VMEM_SPEC),
        scratch_shapes=[
            pltpu.SemaphoreType.DMA((3,)), pltpu.SemaphoreType.DMA((3,)), pltpu.SemaphoreType.DMA((1,)),
            pltpu.SemaphoreType.DMA((7,)), pltpu.SemaphoreType.DMA((7,)),
            pltpu.SemaphoreType.DMA((3,)), pltpu.SemaphoreType.DMA((3,)),
        ],
        compiler_params=pltpu.CompilerParams(vmem_limit_bytes=VMEM_LIMIT_V7X),
    )(c_row, w_ada_s, conv_w_s)


def _cast_place(shards, chip_idx, places):
    n = len(shards)

    def body(chip_ref, *refs):
        for s_ref, o_ref in zip(refs[:n], refs[n:]):
            o_ref[...] = s_ref[...].astype(BF16)

    grid_spec = pltpu.PrefetchScalarGridSpec(
        num_scalar_prefetch=1, grid=(1,),
        in_specs=[pl.BlockSpec(s.shape, lambda i, chip_ref, nd=s.ndim: (0,) * nd) for s in shards],
        out_specs=tuple(pl.BlockSpec(block, lambda i, chip_ref, im=im: im(chip_ref[0])) for _, block, im in places))
    return pl.pallas_call(
        body, out_shape=tuple(jax.ShapeDtypeStruct(full, BF16) for full, _, _ in places), grid_spec=grid_spec,
        name="cast_place", compiler_params=_cp("arbitrary"),
    )(chip_idx, *_in_hbm(*shards))


def _shard_of(ref, kind, chip):
    if kind == "in":
        return ref.at[:, pl.ds(pl.multiple_of(chip * SHARD_IN, 128), SHARD_IN)]
    return ref.at[chip] if kind == "sq" else ref.at[:, chip]


def _land_shape(src, kind):
    if kind == "in":
        return (3, src.shape[0], SHARD_IN)
    return (3,) + src.shape[1:] if kind == "sq" else (3, src.shape[0]) + src.shape[2:]


def _exchange_start(srcs, kinds, tag):
    n = len(srcs)
    lands = [pltpu.with_memory_space_constraint(lax.empty(_land_shape(s, k), s.dtype), pltpu.HBM) for s, k in zip(srcs, kinds)]

    def body(*refs):
        src_refs, land_refs = refs[:n], refs[n:2 * n]
        ssems, rsems = refs[2 * n:3 * n], refs[3 * n:4 * n]
        token = refs[6 * n]
        for i in range(n):
            for k, mask in enumerate(CHIP_MASKS):
                to = _peer(mask)
                pltpu.make_async_remote_copy(
                    src_ref=_shard_of(src_refs[i], kinds[i], _chip_of(to)), dst_ref=land_refs[i].at[k],
                    send_sem=ssems[i], recv_sem=rsems[i], device_id=to, device_id_type=MESH).start()
        token[...] = jnp.zeros_like(token)

    sem = pltpu.SemaphoreType.DMA(())
    out_shape = ((sem,) * (2 * n) + tuple(pltpu.HBM(s.shape, s.dtype) for s in srcs)
                 + tuple(pltpu.HBM(l.shape, l.dtype) for l in lands) + (jax.ShapeDtypeStruct((8, 128), F32),))
    outs = pl.pallas_call(
        body, out_shape=out_shape, name=f"exchange_start_{tag}",
        in_specs=[HBM_SPEC] * (2 * n), out_specs=tuple([SEM_SPEC] * (2 * n) + [HBM_SPEC] * (2 * n) + [VMEM_SPEC]),
        input_output_aliases={i: 2 * n + i for i in range(2 * n)},
        compiler_params=pltpu.CompilerParams(has_side_effects=pltpu.SideEffectType.DATAFLOW_SIDE_EFFECTING),
    )(*[pltpu.with_memory_space_constraint(s, pltpu.HBM) for s in srcs], *lands)
    return outs[:n], outs[n:2 * n], outs[2 * n:3 * n], outs[3 * n:4 * n], outs[4 * n]


def _exchange_wait(ssems, rsems, srcs, lands, after, tag):
    n = len(srcs)

    def body(*refs):
        land_refs = refs[n:2 * n]
        ssem_refs, rsem_refs = refs[2 * n:3 * n], refs[3 * n:4 * n]
        for i in range(n):
            all_three = pltpu.make_async_remote_copy(
                src_ref=land_refs[i], dst_ref=land_refs[i], send_sem=ssem_refs[i], recv_sem=rsem_refs[i],
                device_id=_me(), device_id_type=MESH)
            all_three.wait_send()
            all_three.wait_recv()

    outs = pl.pallas_call(
        body, out_shape=tuple(pltpu.HBM(a.shape, a.dtype) for a in list(srcs) + list(lands)), name=f"exchange_wait_{tag}",
        in_specs=[HBM_SPEC] * (2 * n) + [SEM_SPEC] * (2 * n) + [ANY], out_specs=tuple([HBM_SPEC] * (2 * n)),
        input_output_aliases={i: i for i in range(2 * n)},
        compiler_params=pltpu.CompilerParams(has_side_effects=pltpu.SideEffectType.DATAFLOW_SIDE_EFFECTING),
    )(*srcs, *lands, *ssems, *rsems, after)
    return outs[:n], outs[n:]


def _gather_small(small):
    def body(small_ref, small_all, ssend, srecv):
        me = _me()
        my_dev = _chip_of(me) * 2 + me[2]
        small_all[my_dev] = small_ref[...]
        ssends = []
        for k, mask in enumerate(ALL_MASKS):
            cp = pltpu.make_async_remote_copy(
                src_ref=small_ref, dst_ref=small_all.at[my_dev],
                send_sem=ssend.at[k], recv_sem=srecv.at[k], device_id=_peer(mask), device_id_type=MESH)
            cp.start()
            ssends.append(cp)
        for k, mask in enumerate(ALL_MASKS):
            frm = _peer(mask)
            pltpu.make_async_remote_copy(
                src_ref=small_ref, dst_ref=small_all.at[_chip_of(frm) * 2 + frm[2]],
                send_sem=ssend.at[k], recv_sem=srecv.at[k], device_id=frm, device_id_type=MESH).wait_recv()
        for cp in ssends:
            cp.wait_send()

    return pl.pallas_call(
        body, out_shape=jax.ShapeDtypeStruct((N_DEV, SMALL_ROWS, D_MODEL), F32), name="gather_small",
        in_specs=[VMEM_SPEC], out_specs=VMEM_SPEC,
        scratch_shapes=[pltpu.SemaphoreType.DMA((7,)), pltpu.SemaphoreType.DMA((7,))],
    )(small)


def _half_of(ref, axis, half):
    return ref.at[(slice(None),) * axis + (half,)]


def _swap_halves(parts, axes):
    n = len(parts)

    def body(*refs):
        ins, outs, ssem, rsem = refs[:n], refs[n:2 * n], refs[2 * n], refs[2 * n + 1]
        c = lax.axis_index("c")
        _sibling_handshake()
        cps = [pltpu.make_async_remote_copy(src_ref=_half_of(ins[i], axes[i], 1 - c), dst_ref=outs[i], send_sem=ssem.at[i],
                                            recv_sem=rsem.at[i], device_id=_peer(1), device_id_type=MESH) for i in range(n)]
        for cp in cps:
            cp.start()
        for cp in cps:
            cp.wait()

    shapes = [p.shape[:a] + p.shape[a + 1:] for p, a in zip(parts, axes)]
    return pl.pallas_call(
        body, out_shape=tuple(jax.ShapeDtypeStruct(s, p.dtype) for s, p in zip(shapes, parts)), name="swap_halves",
        in_specs=[ANY] * n, out_specs=tuple([ANY] * n),
        scratch_shapes=[pltpu.SemaphoreType.DMA((n,)), pltpu.SemaphoreType.DMA((n,))], compiler_params=SIBLING_ONLY,
    )(*parts)


def _presum(mines, sibs, c_idx, tag):
    n = len(mines)
    S, _, R, C = mines[0].shape
    tr = min(R, 256)
    tc = SHARD_IN if C % SHARD_IN == 0 else (C // 2 if n > 1 and C % 256 == 0 else C)

    def body(c_ref, *refs):
        for k in range(n):
            total = refs[k][:, 0] + refs[n + k][...]
            refs[2 * n + k][...] = total
            refs[3 * n + k][...] = total.astype(BF16)

    out_spec = pl.BlockSpec((S, tr, tc), lambda i, j, c_ref: (0, i, j))
    grid_spec = pltpu.PrefetchScalarGridSpec(
        num_scalar_prefetch=1, grid=(R // tr, C // tc),
        in_specs=[pl.BlockSpec((S, 1, tr, tc), lambda i, j, c_ref: (0, c_ref[0], i, j))] * n + [out_spec] * n,
        out_specs=(out_spec,) * (2 * n))
    outs = pl.pallas_call(
        body, out_shape=(jax.ShapeDtypeStruct((S, R, C), F32),) * n + (jax.ShapeDtypeStruct((S, R, C), BF16),) * n,
        grid_spec=grid_spec, name=f"presum_{tag}", compiler_params=_cp("parallel", "parallel"),
    )(c_idx, *mines, *sibs)
    return list(outs[:n]), list(outs[n:])


def _assemble_with_sibling(parts, axes):
    n = len(parts)

    def body(*refs):
        outs, ssem, rsem = refs[n:2 * n], refs[2 * n], refs[2 * n + 1]
        c = lax.axis_index("c")
        _sibling_handshake()
        cps = [pltpu.make_async_remote_copy(
            src_ref=_half_of(outs[i], axes[i], c), dst_ref=_half_of(outs[i], axes[i], c), send_sem=ssem.at[i],
            recv_sem=rsem.at[i], device_id=_peer(1), device_id_type=MESH) for i in range(n)]
        for cp in cps:
            cp.start()
        for i in range(n):
            pltpu.make_async_remote_copy(
                src_ref=_half_of(outs[i], axes[i], c), dst_ref=_half_of(outs[i], axes[i], 1 - c), send_sem=ssem.at[i],
                recv_sem=rsem.at[i], device_id=_peer(1), device_id_type=MESH).wait_recv()
        for cp in cps:
            cp.wait_send()

    return pl.pallas_call(
        body, out_shape=tuple(jax.ShapeDtypeStruct(p.shape, p.dtype) for p in parts), name="assemble_with_sibling",
        in_specs=[ANY] * n, out_specs=tuple([ANY] * n), input_output_aliases={i: i for i in range(n)},
        scratch_shapes=[pltpu.SemaphoreType.DMA((n,)), pltpu.SemaphoreType.DMA((n,))], compiler_params=SIBLING_ONLY,
    )(*parts)


def _rope_lane_frequencies():
    inv = np.float32(ROPE_THETA) ** (-(np.arange(0, 2 * ROT_HALF, 2, dtype=np.float32)) / np.float32(2 * ROT_HALF))
    lane = np.arange(128) % HEAD_DIM
    return jnp.asarray(np.where(lane < 2 * ROT_HALF, inv[lane % ROT_HALF], 0.0).astype(np.float32)[None, :])


def _rope_tables(pos, freq):
    ang = pos.astype(F32) * freq
    c, s = jnp.cos(ang), jnp.sin(ang)
    m = lax.broadcasted_iota(jnp.int32, ang.shape, 1) & (HEAD_DIM - 1)
    return (jnp.where(m < 2 * ROT_HALF, c, 1.0), jnp.where(m < ROT_HALF, -s, 0.0),
            jnp.where((m >= ROT_HALF) & (m < 2 * ROT_HALF), s, 0.0))


def _columns(t):
    return [t[:, i:i + 128] for i in range(0, t.shape[-1], 128)]


def _rope(t, c, sa, sb):
    return jnp.concatenate(
        [x * c + pltpu.roll(x, 128 - ROT_HALF, 1) * sa + pltpu.roll(x, ROT_HALF, 1) * sb for x in _columns(t)], axis=1)


def _unrope(d, c, sa, sb):
    return jnp.concatenate(
        [x * c + pltpu.roll(x * sa, ROT_HALF, 1) + pltpu.roll(x * sb, 128 - ROT_HALF, 1) for x in _columns(d)], axis=1)


def _prenorm(x, mod_row, norm_g, pos_col):
    T = x.shape[0]
    tm = min(T, 512)

    def body(x_ref, mod_ref, g_ref, pos_ref, f_ref, h_ref, ht_ref, c_ref, sa_ref, sb_ref):
        xf = x_ref[...]
        shift, scale = mod_ref[:, 0:D_MODEL], mod_ref[:, D_MODEL:2 * D_MODEL]
        h = (xf * _rms(xf)) * g_ref[...] * (1.0 + scale) + shift
        h_ref[...] = h.astype(BF16)
        ht_ref[...] = h.T.astype(BF16)
        c_ref[...], sa_ref[...], sb_ref[...] = _rope_tables(pos_ref[...], f_ref[...])

    tab = jax.ShapeDtypeStruct((T, 128), F32)
    tok = lambda w: pl.BlockSpec((tm, w), lambda i: (i, 0))
    row = lambda w: pl.BlockSpec((1, w), lambda i: (0, 0))
    outs = pl.pallas_call(
        body, out_shape=(jax.ShapeDtypeStruct((T, D_MODEL), BF16), jax.ShapeDtypeStruct((D_MODEL, T), BF16), tab, tab, tab),
        grid=(T // tm,), name="prenorm",
        in_specs=[tok(D_MODEL), row(ADA_W), row(D_MODEL), tok(1), row(128)],
        out_specs=(tok(D_MODEL), pl.BlockSpec((D_MODEL, tm), lambda i: (0, i)), tok(128), tok(128), tok(128)),
        compiler_params=_cp("parallel"),
    )(x, *_in_hbm(mod_row, norm_g), pos_col, _rope_lane_frequencies())
    return outs[0], outs[1], tuple(outs[2:])


def _in_projection(h, w_in, chips, into, tag):
    T = h.shape[0]
    tm, tn = min(T, 1024), SHARD_IN
    k = chips.shape[0]

    def body(chip_ref, h_ref, w_ref, *rest):
        rest[-1][...] = _dot(h_ref[...], w_ref[...])

    w_spec = pl.BlockSpec((D_MODEL, tn), lambda s, i, c: (0, c[s]), **({"pipeline_mode": pl.Buffered(1)} if k == 1 else {}))
    in_specs = [pl.BlockSpec((tm, D_MODEL), lambda s, i, c: (i, 0)), w_spec]
    args = [chips, h, w_in]
    aliases = {}
    if into is not None:
        in_specs.append(ANY)
        args.append(into)
        aliases = {3: 0}
    grid_spec = pltpu.PrefetchScalarGridSpec(num_scalar_prefetch=1, grid=(k, T // tm), in_specs=in_specs,
                                             out_specs=pl.BlockSpec((tm, tn), lambda s, i, c: (i, c[s])))
    return pl.pallas_call(
        body, out_shape=jax.ShapeDtypeStruct((T, IN_W), F32), grid_spec=grid_spec, name=f"in_projection_{tag}",
        input_output_aliases=aliases, compiler_params=_cp("parallel", "parallel"),
    )(*args)


def _attn_mask(n):
    qi = lax.broadcasted_iota(jnp.int32, (GROUP * BLOCK, BLOCK), 0) & (BLOCK - 1)
    j = lax.broadcasted_iota(jnp.int32, (GROUP * BLOCK, BLOCK), 1)
    own = j <= qi
    return own, jnp.logical_not(own) & (n == 0)


def _fold(x, own):
    return jnp.where(own, x[:, BLOCK:2 * BLOCK], x[:, 0:BLOCK])


def _unfold(xf, own):
    zero = jnp.zeros_like(xf)
    return jnp.concatenate([jnp.where(own, zero, xf), jnp.where(own, xf, zero)], axis=1)


ROW_GROUP_HEAD = (0, 2, 1, 3)


def _sink_col(sink_ref, kh):
    rowg = lax.broadcasted_iota(jnp.int32, (GROUP * BLOCK, 1), 0) // BLOCK
    col = jnp.full((GROUP * BLOCK, 1), sink_ref[0, GROUP * kh + ROW_GROUP_HEAD[0]], F32)
    for g in range(1, GROUP):
        col = jnp.where(rowg == g, sink_ref[0, GROUP * kh + ROW_GROUP_HEAD[g]], col)
    return col


def _low_lanes(shape):
    return lax.broadcasted_iota(jnp.int32, shape, 1) < HEAD_DIM


def _kv_pair_operand(prev, cur, kh):
    c = 128 * (kh // 2)
    col = jnp.concatenate([prev[:, c:c + 128], cur[:, c:c + 128]], axis=0).astype(F32)
    if kh % 2 == 0:
        lo = jnp.where(_low_lanes(col.shape), col, 0.0)
        hi = pltpu.roll(lo, HEAD_DIM, 1)
    else:
        hi = jnp.where(_low_lanes(col.shape), 0.0, col)
        lo = pltpu.roll(hi, HEAD_DIM, 1)
    return jnp.concatenate([lo, hi], axis=0).astype(BF16)


def _pair_rows(x, kh):
    c = 2 * 128 * kh
    return jnp.concatenate([x[:, c:c + 128], x[:, c + 128:c + 256]], axis=0)


def _restack(big):
    return jnp.concatenate([big[:, 0:2 * BLOCK], big[:, 2 * BLOCK:4 * BLOCK]], axis=0)


def _unrestack(stacked):
    return jnp.concatenate([stacked[0:2 * BLOCK], stacked[2 * BLOCK:4 * BLOCK]], axis=1)


def _fold_pair(x2, kh):
    low = _low_lanes((2 * BLOCK, 128))
    mixed = jnp.where(low, x2[0:2 * BLOCK], x2[2 * BLOCK:4 * BLOCK])
    total = mixed + pltpu.roll(mixed, HEAD_DIM, 1)
    return jnp.where(low, total, 0.0) if kh % 2 == 0 else jnp.where(low, 0.0, total)


def _attn_scores(qr, k2, kh):
    q2 = _pair_rows(qr, kh).astype(BF16)
    return q2, _restack(_dot_nt(q2, k2))


def _attn_softmax(s, sink_col, mask):
    own, no_key = mask
    s = jnp.where(no_key, -1e30, _fold(s, own))
    m = jnp.maximum(jnp.max(s, axis=-1, keepdims=True), sink_col)
    p = jnp.exp(s - m)
    p_sink = jnp.exp(sink_col - m)
    denom = jnp.sum(p, axis=-1, keepdims=True) + p_sink
    return p / denom, p_sink / denom


def _attn_forward(proj, tabs, sinks):
    T = proj.shape[0]
    nb = T // BLOCK

    def body(q_ref, kvc_ref, kvp_ref, g0_ref, g1_ref, cc, sac, sbc, cp_, sap, sbp, sink_ref, y_ref, qrb_ref, krb_ref):
        n = pl.program_id(0)
        tc = tcur = (cc[...], sac[...], sbc[...])
        tprev = (cp_[...], sap[...], sbp[...])
        qr = _rope(q_ref[...], *tc) * ATTN_SCALE
        kr_cur = _rope(kvc_ref[:, 0:KV_W], *tcur)
        kr_prev = _rope(kvp_ref[:, 0:KV_W], *tprev)
        qrb_ref[...] = qr.astype(BF16)
        krb_ref[...] = kr_cur.astype(BF16)
        v_cur, v_prev = kvc_ref[:, KV_W:2 * KV_W], kvp_ref[:, KV_W:2 * KV_W]
        mask = _attn_mask(n)
        outs = []
        k2s = [_kv_pair_operand(kr_prev, kr_cur, kh) for kh in range(N_KV)]
        v2s = [_kv_pair_operand(v_prev, v_cur, kh) for kh in range(N_KV)]
        scores = [_attn_scores(qr, k2s[kh], kh) for kh in range(N_KV)]
        for kh in range(N_KV):
            pn, _ = _attn_softmax(scores[kh][1], _sink_col(sink_ref, kh), mask)
            o_big = _dot(_unrestack(_unfold(pn.astype(BF16), mask[0])), v2s[kh])
            outs += [o_big[0:BLOCK], o_big[BLOCK:2 * BLOCK]]
        o = jnp.concatenate(outs, axis=1)
        g = jnp.concatenate([g0_ref[...], g1_ref[...]], axis=1)
        y_ref[...] = (o * (g * _sigmoid(g))).astype(BF16)

    def blk(w, cb):
        return pl.BlockSpec((BLOCK, w), lambda n, cb=cb: (n, cb))

    prev = lambda w, cb: pl.BlockSpec((BLOCK, w), lambda n, cb=cb: (jnp.maximum(n - 1, 0), cb))
    return pl.pallas_call(
        body, grid=(nb,), name="attn_forward",
        out_shape=(jax.ShapeDtypeStruct((T, D_MODEL), BF16), jax.ShapeDtypeStruct((T, D_MODEL), BF16),
                   jax.ShapeDtypeStruct((T, KV_W), BF16)),
        in_specs=[blk(D_MODEL, 0), blk(CB, CB_KV), prev(CB, CB_KV), blk(CB, CB_GA), blk(CB, CB_GA + 1),
                  blk(128, 0), blk(128, 0), blk(128, 0), prev(128, 0), prev(128, 0), prev(128, 0),
                  pl.BlockSpec(memory_space=pltpu.SMEM)],
        out_specs=(blk(D_MODEL, 0), blk(D_MODEL, 0), blk(KV_W, 0)),
        compiler_params=_cp("parallel"),
    )(proj, proj, proj, proj, proj, *tabs, *tabs, sinks)


def _scan_rows8():
    return lax.broadcasted_iota(jnp.int32, (8, D_MODEL), 0)


def _scan_forward(a_ref, b_ref, h_ref, carry, rows):
    row = _scan_rows8()

    def group(i, carry):
        off = pl.multiple_of(i * 8, 8)
        a, b = a_ref[pl.ds(off, 8), :], b_ref[pl.ds(off, 8), :]
        for d in (1, 2, 4):
            ok = row >= d
            b = jnp.where(ok, a * pltpu.roll(b, d, 0) + b, b)
            a = jnp.where(ok, a * pltpu.roll(a, d, 0), a)
        h = a * carry + b
        h_ref[pl.ds(off, 8), :] = h
        return h[7:8, :]

    return lax.fori_loop(0, rows // 8, group, carry)


def _scan_backward(a_ref, g_ref, lam_ref, carry, rows):
    row = _scan_rows8()

    def group(i, carry):
        off = pl.multiple_of((rows // 8 - 1 - i) * 8, 8)
        a, g = a_ref[pl.ds(off, 8), :], g_ref[pl.ds(off, 8), :]
        b = a * g
        for d in (1, 2, 4):
            ok = row < 8 - d
            b = jnp.where(ok, a * pltpu.roll(b, 8 - d, 0) + b, b)
            a = jnp.where(ok, a * pltpu.roll(a, 8 - d, 0), a)
        mu = a * carry + b
        mu_below = jnp.where(row == 7, carry, pltpu.roll(mu, 7, 0))
        lam_ref[pl.ds(off, 8), :] = g + mu_below
        return mu[0:1, :]

    return lax.fori_loop(0, rows // 8, group, carry)


def _conv_taps(xbuf, xr, tail):
    rows = xr.shape[0]
    xbuf[0:8, :] = tail
    xbuf[8:rows + 8, :] = xr
    return [xbuf[pl.ds(8 - (CONV_W - 1 - k), rows), :] for k in range(CONV_W - 1)] + [xr]


def _rnn_gates(xbuf, xr, tail, cw, cb, wa_ref, wx_ref, ba, bx, sp, reset):
    xs = _conv_taps(xbuf, xr, tail)
    xc = xs[0] * cw[0:1, :]
    for k in range(1, CONV_W):
        xc = xc + xs[k] * cw[k:k + 1, :]
    xc = xc + cb
    xcb = xc.astype(BF16)
    za = jnp.concatenate([_dot(xcb[:, RNN_BW * j:RNN_BW * (j + 1)], wa_ref[j]) for j in range(RNN_BLOCKS)], axis=1) + ba
    zx = jnp.concatenate([_dot(xcb[:, RNN_BW * j:RNN_BW * (j + 1)], wx_ref[j]) for j in range(RNN_BLOCKS)], axis=1) + bx
    r, i = _sigmoid(za), _sigmoid(zx)
    neg_log_a = LRU_C * r * sp
    a_raw = jnp.exp(-neg_log_a)
    mult_raw = jnp.sqrt(jnp.tanh(neg_log_a) * (1.0 + a_raw * a_raw))
    a = jnp.where(reset, 0.0, a_raw)
    mult = jnp.where(reset, 1.0, mult_raw)
    return xc, r, i, a, mult


def _rnn_forward(proj, pos_col, conv_w, conv_b, rwa, rwx, ba, bx, lam):
    T = proj.shape[0]
    tr = min(T, 256)

    def body(x0, x1, g0, g1, pos_ref, cw_ref, cb_ref, wa_ref, wx_ref, ba_ref, bx_ref, lam_ref,
             y_ref, h_ref, xc_ref, r_ref, i_ref, a_ref, mult_ref, xbuf, bbuf, tail, carry):
        t = pl.program_id(0)

        @pl.when(t == 0)
        def _():
            tail[...] = jnp.zeros_like(tail)
            carry[...] = jnp.zeros_like(carry)

        xr = jnp.concatenate([x0[...], x1[...]], axis=1)
        sp = _softplus(-lam_ref[...])
        reset = pos_ref[...] == 0
        xc, r, i, a, mult = _rnn_gates(
            xbuf, xr, tail[...], cw_ref[...], cb_ref[...], wa_ref, wx_ref, ba_ref[...], bx_ref[...], sp, reset)
        xc_ref[...] = xc
        r_ref[...] = r
        i_ref[...] = i
        a_ref[...] = a
        mult_ref[...] = mult
        bbuf[...] = mult * (i * xc)
        last = _scan_forward(a_ref, bbuf, h_ref, carry[0:1, :], tr)
        carry[...] = jnp.broadcast_to(last, carry.shape)
        tail[...] = xr[tr - 8:tr, :]
        g = jnp.concatenate([g0[...], g1[...]], axis=1)
        y_ref[...] = (h_ref[...] * (g * _sigmoid(g))).astype(BF16)

    blk = lambda cb: pl.BlockSpec((tr, CB), lambda t, cb=cb: (t, cb))
    row = lambda w: pl.BlockSpec((1, w), lambda t: (0, 0))
    full3 = pl.BlockSpec((RNN_BLOCKS, RNN_BW, RNN_BW), lambda t: (0, 0, 0))
    tok = pl.BlockSpec((tr, D_MODEL), lambda t: (t, 0))
    act = jax.ShapeDtypeStruct((T, D_MODEL), F32)
    return pl.pallas_call(
        body, out_shape=(jax.ShapeDtypeStruct((T, D_MODEL), BF16),) + (act,) * 6,
        grid=(T // tr,), name="rnn_forward",
        in_specs=[blk(CB_XR), blk(CB_XR + 1), blk(CB_GR), blk(CB_GR + 1), pl.BlockSpec((tr, 1), lambda t: (t, 0)),
                  pl.BlockSpec((CONV_W, D_MODEL), lambda t: (0, 0)), row(D_MODEL), full3, full3,
                  row(D_MODEL), row(D_MODEL), row(D_MODEL)],
        out_specs=(tok,) * 7,
        scratch_shapes=[pltpu.VMEM((tr + 8, D_MODEL), F32), pltpu.VMEM((tr, D_MODEL), F32),
                        pltpu.VMEM((8, D_MODEL), F32), pltpu.VMEM((8, D_MODEL), F32)],
        compiler_params=_cp("arbitrary"),
    )(proj, proj, proj, proj, pos_col, *_in_hbm(conv_w, conv_b, rwa, rwx, ba, bx, lam))


def _merge_and_head(x, target, y_attn, y_rnn, proj, wap, wrp, wo, mod_row, final_g):
    T = x.shape[0]
    tm = min(T, 256)

    def body(x_ref, t_ref, ya_ref, yr_ref, ma0, ma1, mr0, mr1, wap_ref, wrp_ref, wo_ref, mod_ref, fg_ref,
             dx2_ref, mg_ref, do_ref, dpa_ref, dpr_ref, dya_ref, dyr_ref, dc_ref, dfg_ref, dgate_ref, loss_ref):
        i = pl.program_id(0)
        gate = mod_ref[:, 2 * D_MODEL:3 * D_MODEL]
        fg = fg_ref[...]
        pa, pr = _dot(ya_ref[...], wap_ref[...]), _dot(yr_ref[...], wrp_ref[...])
        sa = _sigmoid(jnp.concatenate([ma0[...], ma1[...]], axis=1))
        sr = _sigmoid(jnp.concatenate([mr0[...], mr1[...]], axis=1))
        mb = (sa * pa + sr * pr).astype(BF16)
        o = _dot(mb, wo_ref[...])
        x2 = x_ref[...] + gate * o
        r2 = _rms(x2)
        xn2 = x2 * r2
        err = xn2 * fg - t_ref[...]
        loss_t = 0.5 * jnp.sum(jnp.sum(err * err, axis=-1, keepdims=True) * (1.0 / D_MODEL), axis=0, keepdims=True)
        dy = err * (1.0 / D_MODEL)
        dfg_t = jnp.sum(dy * xn2, axis=0, keepdims=True)
        dxn = dy * fg
        dx2 = r2 * (dxn - xn2 * jnp.mean(dxn * xn2, axis=-1, keepdims=True))
        dgate_t = jnp.sum(dx2 * o, axis=0, keepdims=True)
        dob = (dx2 * gate).astype(BF16)
        dmerged = _dot_nt(dob, wo_ref[...])
        dpa, dpr = (dmerged * sa).astype(BF16), (dmerged * sr).astype(BF16)
        dya, dyr = _dot_nt(dpa, wap_ref[...]), _dot_nt(dpr, wrp_ref[...])
        dx2_ref[...] = dx2
        mg_ref[...] = mb
        do_ref[...] = dob
        dpa_ref[...] = dpa
        dpr_ref[...] = dpr
        dya_ref[...] = dya
        dyr_ref[...] = dyr
        dc_ref[:, 0:D_MODEL] = (dmerged * pa * sa * (1.0 - sa)).astype(BF16)
        dc_ref[:, D_MODEL:2 * D_MODEL] = (dmerged * pr * sr * (1.0 - sr)).astype(BF16)

        @pl.when(i == 0)
        def _():
            dfg_ref[...] = jnp.zeros_like(dfg_ref)
            dgate_ref[...] = jnp.zeros_like(dgate_ref)
            loss_ref[...] = jnp.zeros_like(loss_ref)

        dfg_ref[...] += dfg_t
        dgate_ref[...] += dgate_t
        loss_ref[...] += jnp.broadcast_to(loss_t, loss_ref.shape)

    tok = lambda w: pl.BlockSpec((tm, w), lambda i: (i, 0))
    blk = lambda cb: pl.BlockSpec((tm, CB), lambda i, cb=cb: (i, cb))
    wfull = pl.BlockSpec((D_MODEL, D_MODEL), lambda i: (0, 0), pipeline_mode=pl.Buffered(1))
    row = lambda w: pl.BlockSpec((1, w), lambda i: (0, 0))
    out_shape = (
        jax.ShapeDtypeStruct((T, D_MODEL), F32), jax.ShapeDtypeStruct((T, D_MODEL), BF16),
        jax.ShapeDtypeStruct((T, D_MODEL), BF16), jax.ShapeDtypeStruct((T, D_MODEL), BF16),
        jax.ShapeDtypeStruct((T, D_MODEL), BF16), jax.ShapeDtypeStruct((T, D_MODEL), F32),
        jax.ShapeDtypeStruct((T, D_MODEL), F32), jax.ShapeDtypeStruct((T, 2 * D_MODEL), BF16),
        jax.ShapeDtypeStruct((1, D_MODEL), F32), jax.ShapeDtypeStruct((1, D_MODEL), F32),
        jax.ShapeDtypeStruct((1, 128), F32),
    )
    return pl.pallas_call(
        body, out_shape=out_shape, grid=(T // tm,), name="merge_and_head",
        in_specs=[tok(D_MODEL), tok(D_MODEL), tok(D_MODEL), tok(D_MODEL), blk(CB_MA), blk(CB_MA + 1), blk(CB_MR),
                  blk(CB_MR + 1), wfull, wfull, wfull, row(ADA_W), row(D_MODEL)],
        out_specs=(tok(D_MODEL),) * 7 + (tok(2 * D_MODEL), row(D_MODEL), row(D_MODEL), row(128)),
        compiler_params=_cp("arbitrary"),
    )(x, target, y_attn, y_rnn, proj, proj, proj, proj, wap, wrp, wo, *_in_hbm(mod_row, final_g))


def _attn_backward(proj, qr_b, kr_b, d_y, tabs, sinks):
    T = proj.shape[0]
    nb = T // BLOCK

    def body(qrb_ref, krc_ref, krp_ref, vc_ref, vp_ref, g0_ref, g1_ref, dy_ref, cc, sac, sbc, cp_, sap, sbp, sink_ref,
             dq_ref, dkv_ref, dg_ref, dsink_ref, carry):
        n = pl.program_id(0)

        @pl.when(n == 0)
        def _():
            carry[...] = jnp.zeros_like(carry)
            dsink_ref[...] = jnp.zeros_like(dsink_ref)

        @pl.when(n < nb)
        def _():
            tc = tcur = (cc[...], sac[...], sbc[...])
            tprev = (cp_[...], sap[...], sbp[...])
            qr, kr_cur, kr_prev = qrb_ref[...], krc_ref[...], krp_ref[...]
            v_cur, v_prev = vc_ref[...], vp_ref[...]
            g = jnp.concatenate([g0_ref[...], g1_ref[...]], axis=1)
            sg = _sigmoid(g)
            dy = dy_ref[...]
            d_o = dy * (g * sg)
            mask = _attn_mask(n)
            lane = lax.broadcasted_iota(jnp.int32, (1, 128), 1)
            rowg = lax.broadcasted_iota(jnp.int32, (GROUP * BLOCK, 1), 0) // BLOCK
            o_parts, dq_parts = [], []
            dk_cols, dv_cols = [None, None], [None, None]
            dsink = jnp.zeros((1, 128), F32)
            heads = range(N_KV)
            k2s = [_kv_pair_operand(kr_prev, kr_cur, kh) for kh in heads]
            v2s = [_kv_pair_operand(v_prev, v_cur, kh) for kh in heads]
            scores = [_attn_scores(qr, k2s[kh], kh) for kh in heads]
            do2s = [_pair_rows(d_o, kh).astype(BF16) for kh in heads]
            dpns = [_fold(_restack(_dot_nt(do2s[kh], v2s[kh])), mask[0]) for kh in heads]
            probs = [_attn_softmax(scores[kh][1], _sink_col(sink_ref, kh), mask) for kh in heads]
            p_bigs = [_unrestack(_unfold(probs[kh][0].astype(BF16), mask[0])) for kh in heads]
            o_bigs = [_dot(p_bigs[kh], v2s[kh]) for kh in heads]
            dv2s = [_dot_tn(p_bigs[kh], do2s[kh]) for kh in heads]
            deltas = [jnp.sum(probs[kh][0] * dpns[kh], axis=-1, keepdims=True) for kh in heads]
            ds_bigs = [_unrestack(_unfold((probs[kh][0] * (dpns[kh] - deltas[kh])).astype(BF16), mask[0])) for kh in heads]
            dq2s = [_dot(ds_bigs[kh], k2s[kh]) for kh in heads]
            dk2s = [_dot_tn(ds_bigs[kh], scores[kh][0]) for kh in heads]
            for kh in heads:
                o_parts += [o_bigs[kh][0:BLOCK], o_bigs[kh][BLOCK:2 * BLOCK]]
                dq_parts += [dq2s[kh][0:BLOCK], dq2s[kh][BLOCK:2 * BLOCK]]
                dk_c, dv_c = _fold_pair(dk2s[kh], kh), _fold_pair(dv2s[kh], kh)
                c = kh // 2
                dk_cols[c] = dk_c if dk_cols[c] is None else dk_cols[c] + dk_c
                dv_cols[c] = dv_c if dv_cols[c] is None else dv_cols[c] + dv_c
                ds_rows = probs[kh][1] * deltas[kh]
                for gq in range(GROUP):
                    val = -jnp.sum(jnp.where(rowg == gq, ds_rows, 0.0), axis=0, keepdims=True)
                    dsink = dsink + jnp.where(lane == GROUP * kh + ROW_GROUP_HEAD[gq], val, 0.0)
            o = jnp.concatenate(o_parts, axis=1)
            dg_ref[...] = (dy * o * (sg * (1.0 + g * (1.0 - sg)))).astype(BF16)
            dq_ref[...] = (_unrope(jnp.concatenate(dq_parts, axis=1), *tc) * ATTN_SCALE).astype(BF16)
            dk_all, dv_all = jnp.concatenate(dk_cols, axis=1), jnp.concatenate(dv_cols, axis=1)
            dk_prev = _unrope(dk_all[0:BLOCK], *tprev)
            dk_cur = _unrope(dk_all[BLOCK:2 * BLOCK], *tcur)
            dv_prev, dv_cur = dv_all[0:BLOCK], dv_all[BLOCK:2 * BLOCK]
            dkv_ref[...] = (carry[...] + jnp.concatenate([dk_prev, dv_prev], axis=1)).astype(BF16)
            carry[...] = jnp.concatenate([dk_cur, dv_cur], axis=1)
            dsink_ref[...] += dsink

        @pl.when(n == nb)
        def _():
            dkv_ref[...] = carry[...].astype(BF16)

    cur = lambda w, cb: pl.BlockSpec((BLOCK, w), lambda n, cb=cb: (jnp.minimum(n, nb - 1), cb))
    prev = lambda w, cb: pl.BlockSpec((BLOCK, w), lambda n, cb=cb: (jnp.maximum(jnp.minimum(n, nb - 1) - 1, 0), cb))
    out_shape = (jax.ShapeDtypeStruct((T, D_MODEL), BF16), jax.ShapeDtypeStruct((T, 2 * KV_W), BF16),
                 jax.ShapeDtypeStruct((T, D_MODEL), BF16), jax.ShapeDtypeStruct((1, 128), F32))
    return pl.pallas_call(
        body, out_shape=out_shape, grid=(nb + 1,), name="attn_backward",
        in_specs=[cur(D_MODEL, 0), cur(KV_W, 0), prev(KV_W, 0), cur(KV_W, V_COL_BLOCK), prev(KV_W, V_COL_BLOCK),
                  cur(CB, CB_GA), cur(CB, CB_GA + 1), cur(D_MODEL, 0),
                  cur(128, 0), cur(128, 0), cur(128, 0), prev(128, 0), prev(128, 0), prev(128, 0),
                  pl.BlockSpec(memory_space=pltpu.SMEM)],
        out_specs=(cur(D_MODEL, 0), pl.BlockSpec((BLOCK, 2 * KV_W), lambda n: (jnp.maximum(n - 1, 0), 0)),
                   cur(D_MODEL, 0), pl.BlockSpec((1, 128), lambda n: (0, 0))),
        scratch_shapes=[pltpu.VMEM((BLOCK, 2 * KV_W), F32)],
        compiler_params=_cp("arbitrary"),
    )(qr_b, kr_b, kr_b, proj, proj, proj, proj, d_y, *tabs, *tabs, sinks)


def _rnn_backward(proj, pos_col, h_rnn, saved, d_y, conv_w, rwa, rwx, lam):
    T = proj.shape[0]
    tr = min(T, 256)
    nt = T // tr
    hb = tr // 8

    def body(x0, x1, xh0, xh1, g0, g1, pos_ref, h_ref, hh_ref, xc_ref, r_ref, i_ref, a_ref, mult_ref, dy_ref,
             cw_ref, wa_ref, wx_ref, lam_ref, db_ref, dcw_ref, dcb_ref, dwa_ref, dwx_ref, dba_ref, dbx_ref, dlam_ref,
             xbuf, hbuf, dbuf, gbuf, lbuf, mu_carry, dxc_head):
        step = pl.program_id(0)
        first_tile = step == nt - 1

        @pl.when(step == 0)
        def _():
            mu_carry[...] = jnp.zeros_like(mu_carry)
            dxc_head[...] = jnp.zeros_like(dxc_head)
            for ref in (dcw_ref, dcb_ref, dwa_ref, dwx_ref, dba_ref, dbx_ref, dlam_ref):
                ref[...] = jnp.zeros_like(ref)

        xr = jnp.concatenate([x0[...], x1[...]], axis=1)
        tail = jnp.where(first_tile, 0.0, jnp.concatenate([xh0[...], xh1[...]], axis=1))
        lam_v = lam_ref[...]
        sp = _softplus(-lam_v)
        reset = pos_ref[...] == 0
        cw = cw_ref[...]
        xbuf[0:8, :] = tail
        xbuf[8:tr + 8, :] = xr
        g = jnp.concatenate([g0[...], g1[...]], axis=1)
        sg = _sigmoid(g)
        dy = dy_ref[...]
        h = h_ref[...]
        db_ref[:, D_MODEL:2 * D_MODEL] = (dy * h * (sg * (1.0 + g * (1.0 - sg)))).astype(BF16)
        gbuf[...] = dy * (g * sg)
        top = _scan_backward(a_ref, gbuf, lbuf, mu_carry[0:1, :], tr)
        mu_carry[...] = jnp.broadcast_to(top, mu_carry.shape)
        hbuf[0:8, :] = jnp.where(first_tile, 0.0, hh_ref[...])
        hbuf[8:tr + 8, :] = h
        live = jnp.logical_not(reset)
        dbuf[tr:tr + 8, :] = dxc_head[...]
        for j in range(RNN_BLOCKS):
            sl = slice(RNN_BW * j, RNN_BW * (j + 1))
            lam_t, h_prev = lbuf[:, sl], hbuf[pl.ds(7, tr), sl]
            xc, r, i, a, mult = xc_ref[:, sl], r_ref[:, sl], i_ref[:, sl], a_ref[:, sl], mult_ref[:, sl]
            d_a = jnp.where(live, lam_t * h_prev, 0.0)
            d_mult = jnp.where(live, lam_t * (i * xc), 0.0)
            d_ixc = lam_t * mult
            d_i = d_ixc * xc
            d_log_a = d_a * a - d_mult * (a * a / mult)
            d_za = d_log_a * (-LRU_C * sp[:, sl]) * (r * (1.0 - r))
            d_zx = d_i * (i * (1.0 - i))
            dlam_ref[:, sl] += jnp.sum(d_log_a * r, axis=0, keepdims=True) * (LRU_C * _sigmoid(-lam_v[:, sl]))
            dba_ref[:, sl] += jnp.sum(d_za, axis=0, keepdims=True)
            dbx_ref[:, sl] += jnp.sum(d_zx, axis=0, keepdims=True)
            xcb, dzab, dzxb = xc.astype(BF16), d_za.astype(BF16), d_zx.astype(BF16)
            dwa_ref[j] += _dot_tn(xcb, dzab)
            dwx_ref[j] += _dot_tn(xcb, dzxb)
            d_xc = d_ixc * i + (_dot_nt(dzab, wa_ref[j]) + _dot_nt(dzxb, wx_ref[j]))
            dcb_ref[:, sl] += jnp.sum(d_xc, axis=0, keepdims=True)
            for k in range(CONV_W):
                tap = xr[:, sl] if k == CONV_W - 1 else xbuf[pl.ds(8 - (CONV_W - 1 - k), tr), sl]
                dcw_ref[k:k + 1, sl] += jnp.sum(d_xc * tap, axis=0, keepdims=True)
            dbuf[0:tr, sl] = d_xc
            d_xr = d_xc * cw[CONV_W - 1:CONV_W, sl]
            for k in range(CONV_W - 1):
                d_xr = d_xr + dbuf[pl.ds(CONV_W - 1 - k, tr), sl] * cw[k:k + 1, sl]
            dxc_head[:, sl] = d_xc[0:8, :]
            db_ref[:, sl] = d_xr.astype(BF16)

    rev = lambda s: nt - 1 - s
    blk = lambda cb: pl.BlockSpec((tr, CB), lambda s, cb=cb: (rev(s), cb))
    halo = lambda w, cb: pl.BlockSpec((8, w), lambda s, cb=cb: (jnp.maximum(rev(s) * hb - 1, 0), cb))
    tok = lambda w: pl.BlockSpec((tr, w), lambda s: (rev(s), 0))
    row = lambda w: pl.BlockSpec((1, w), lambda s: (0, 0))
    full3 = pl.BlockSpec((RNN_BLOCKS, RNN_BW, RNN_BW), lambda s: (0, 0, 0))
    cwspec = pl.BlockSpec((CONV_W, D_MODEL), lambda s: (0, 0))
    vec = jax.ShapeDtypeStruct((1, D_MODEL), F32)
    gate_w = jax.ShapeDtypeStruct((RNN_BLOCKS, RNN_BW, RNN_BW), F32)
    out_shape = (jax.ShapeDtypeStruct((T, 2 * D_MODEL), BF16), jax.ShapeDtypeStruct((CONV_W, D_MODEL), F32), vec,
                 gate_w, gate_w, vec, vec, vec)
    big = lambda: pltpu.VMEM((tr, D_MODEL), F32)
    ext = lambda: pltpu.VMEM((tr + 8, D_MODEL), F32)
    return pl.pallas_call(
        body, out_shape=out_shape, grid=(nt,), name="rnn_backward",
        in_specs=[blk(CB_XR), blk(CB_XR + 1), halo(CB, CB_XR), halo(CB, CB_XR + 1), blk(CB_GR), blk(CB_GR + 1),
                  pl.BlockSpec((tr, 1), lambda s: (rev(s), 0)), tok(D_MODEL), halo(D_MODEL, 0)] + [tok(D_MODEL)] * 6
        + [cwspec, full3, full3, row(D_MODEL)],
        out_specs=(tok(2 * D_MODEL), cwspec, row(D_MODEL), full3, full3, row(D_MODEL), row(D_MODEL), row(D_MODEL)),
        scratch_shapes=[ext(), ext(), ext(), big(), big(), pltpu.VMEM((8, D_MODEL), F32), pltpu.VMEM((8, D_MODEL), F32)],
        compiler_params=_cp("arbitrary"),
    )(proj, proj, proj, proj, proj, proj, pos_col, h_rnn, h_rnn, *saved, d_y, *_in_hbm(conv_w, rwa, rwx, lam))


def _input_backward(pieces, w_in, x, dx2, mod_row, norm_g):
    T = x.shape[0]
    tm = min(T, 512)
    n = len(pieces)

    def body(*refs):
        d_refs = refs[:n]
        w_ref, x_ref, dx2_ref, mod_ref, g_ref, gx_ref, dshift_ref, dscale_ref, dg_ref = refs[n:]
        i = pl.program_id(0)
        dh = None
        for d_ref, (_, start, count) in zip(d_refs, pieces):
            part = _dot_nt(d_ref[...], w_ref[:, start * CB:(start + count) * CB])
            dh = part if dh is None else dh + part

        @pl.when(i == 0)
        def _():
            dshift_ref[...] = jnp.zeros_like(dshift_ref)
            dscale_ref[...] = jnp.zeros_like(dscale_ref)
            dg_ref[...] = jnp.zeros_like(dg_ref)

        xf = x_ref[...]
        r1 = _rms(xf)
        xn = xf * r1
        gn = g_ref[...]
        s1 = 1.0 + mod_ref[:, D_MODEL:2 * D_MODEL]
        dshift_ref[...] += jnp.sum(dh, axis=0, keepdims=True)
        dscale_ref[...] += jnp.sum(dh * (xn * gn), axis=0, keepdims=True)
        dg_ref[...] += jnp.sum(dh * s1 * xn, axis=0, keepdims=True)
        dxn = dh * s1 * gn
        gx_ref[...] = dx2_ref[...] + r1 * (dxn - xn * jnp.mean(dxn * xn, axis=-1, keepdims=True))

    tok = lambda w: pl.BlockSpec((tm, w), lambda i: (i, 0))
    row = lambda w: pl.BlockSpec((1, w), lambda i: (0, 0))
    vec = jax.ShapeDtypeStruct((1, D_MODEL), F32)
    return pl.pallas_call(
        body, out_shape=(jax.ShapeDtypeStruct((T, D_MODEL), F32), vec, vec, vec), grid=(T // tm,), name="input_backward",
        in_specs=[tok(c * CB) for _, _, c in pieces]
        + [pl.BlockSpec((D_MODEL, IN_W), lambda i: (0, 0), pipeline_mode=pl.Buffered(1)), tok(D_MODEL), tok(D_MODEL),
           row(ADA_W), row(D_MODEL)],
        out_specs=(tok(D_MODEL), row(D_MODEL), row(D_MODEL), row(D_MODEL)),
        compiler_params=_cp("arbitrary"),
    )(*[p[0] for p in pieces], w_in, x, dx2, *_in_hbm(mod_row, norm_g))


def _weight_grad(a, pieces, tag, a_is_transposed=False):
    M, T = a.shape if a_is_transposed else a.shape[::-1]
    n_blocks = sum(count for _, _, count in pieces)
    n = len(pieces)
    contract = _dot if a_is_transposed else _dot_tn

    def body(*refs):
        a_ref, b_refs, o_ref = refs[0], refs[1:1 + n], refs[-1]
        j = pl.program_id(0)
        for b_ref, (_, start, count) in zip(b_refs, pieces):
            @pl.when((j >= start) & (j < start + count))
            def _(b_ref=b_ref):
                o_ref[...] = contract(a_ref[...], b_ref[...])

    def piece_spec(start, count):
        return pl.BlockSpec((T, CB), lambda j: (0, jnp.clip(j - start, 0, count - 1)))

    return pl.pallas_call(
        body, out_shape=jax.ShapeDtypeStruct((M, n_blocks * CB), F32), grid=(n_blocks,), name=f"weight_grad_{tag}",
        in_specs=[pl.BlockSpec(a.shape, lambda j: (0, 0), pipeline_mode=pl.Buffered(1))] + [piece_spec(s, c) for _, s, c in pieces],
        out_specs=pl.BlockSpec((M, CB), lambda j: (0, j)), compiler_params=_cp("arbitrary"),
    )(a, *[p[0] for p in pieces])


def _adamw(w, g, m, v):
    m = ADAM_B1 * m + (1.0 - ADAM_B1) * g
    v = ADAM_B2 * v + (1.0 - ADAM_B2) * (g * g)
    m_hat = m / (1.0 - ADAM_B1 ** ADAM_STEP)
    v_hat = v / (1.0 - ADAM_B2 ** ADAM_STEP)
    delta = -ADAM_LR * (m_hat / (jnp.sqrt(v_hat) + ADAM_EPS) + ADAM_WD * w)
    return delta, m, v


def _sum_landed(kind, owns, lands, where, tag):
    n = len(owns)
    land = lands[0]
    if kind == "in":
        R, C = land.shape[1:]
        tr = 256
        grid = (R // tr,)
        own_spec = pl.BlockSpec((tr, C), lambda i, w: (i, w[0]))
        land_spec = pl.BlockSpec((3, tr, C), lambda i, w: (0, i, 0))
        out_spec = pl.BlockSpec((1, tr, C), lambda i, w: (w[1], i, 0))
        out_shape = (2, R, C)
        pick = lambda ref: ref[...]
    elif kind == "sq":
        R, C = land.shape[1:]
        grid = (1,)
        own_spec = pl.BlockSpec((1, R, C), lambda i, w: (w[0], 0, 0))
        land_spec = pl.BlockSpec((3, R, C), lambda i, w: (0, 0, 0))
        out_spec = pl.BlockSpec((1, R, C), lambda i, w: (w[1], 0, 0))
        out_shape = (2, R, C)
        pick = lambda ref: ref[0]
    else:
        B, R, C = land.shape[1:]
        grid = (1,)
        own_spec = pl.BlockSpec((B, 1, R, C), lambda i, w: (0, w[0], 0, 0))
        land_spec = pl.BlockSpec((3, B, R, C), lambda i, w: (0, 0, 0, 0))
        out_spec = pl.BlockSpec((B, 1, R, C), lambda i, w: (0, w[1], 0, 0))
        out_shape = (B, 2, R, C)
        pick = lambda ref: ref[:, 0]

    def body(w_ref, *refs):
        for k in range(n):
            own_ref, l_ref, o_ref = refs[k], refs[n + k], refs[2 * n + k]
            total = ((pick(own_ref) + l_ref[0].astype(F32)) + l_ref[1].astype(F32)) + l_ref[2].astype(F32)
            if kind == "rg":
                o_ref[:, 0] = total
            else:
                o_ref[0] = total

    grid_spec = pltpu.PrefetchScalarGridSpec(num_scalar_prefetch=1, grid=grid, in_specs=[own_spec] * n + [land_spec] * n,
                                             out_specs=(out_spec,) * n)
    return list(pl.pallas_call(
        body, out_shape=(jax.ShapeDtypeStruct(out_shape, F32),) * n, grid_spec=grid_spec, name=f"sum_landed_{tag}",
        compiler_params=_cp("parallel"),
    )(where, *owns, *lands))


def _adamw_shard(gs, ws, ms, vs, tag):
    n = len(ws)
    R, C = ws[0].shape
    tr = min(R, 256 if n == 1 else 64)

    def body(*refs):
        for k in range(n):
            g = refs[k][...]
            d, nm, nv = _adamw(refs[n + k][...], g, refs[2 * n + k][...], refs[3 * n + k][...])
            out = refs[4 * n + 4 * k:4 * n + 4 * k + 4]
            out[0][...] = g
            out[1][...] = d
            out[2][...] = nm
            out[3][...] = nv

    spec = pl.BlockSpec((tr, C), lambda i: (i, 0))
    sds = jax.ShapeDtypeStruct((R, C), F32)
    outs = pl.pallas_call(
        body, out_shape=(sds,) * (4 * n), grid=(R // tr,), name=f"adamw_{tag}",
        in_specs=[spec] * (4 * n), out_specs=(spec,) * (4 * n), compiler_params=_cp("parallel"),
    )(*gs, *_in_hbm(*ws, *ms, *vs))
    return [outs[4 * k:4 * k + 4] for k in range(n)]


def _adamw_w_ada(c_t, dmod_cols, w, m, v):
    R, C = w.shape

    def body(ct_ref, dm_ref, w_ref, m_ref, v_ref, g_ref, d_ref, nm_ref, nv_ref):
        g = _dot(ct_ref[...].astype(BF16), dm_ref[...].astype(BF16))
        d, nm, nv = _adamw(w_ref[...], g, m_ref[...], v_ref[...])
        g_ref[...] = g
        d_ref[...] = d
        nm_ref[...] = nm
        nv_ref[...] = nv

    tr = 256
    spec = pl.BlockSpec((tr, C), lambda i: (i, 0))
    sds = jax.ShapeDtypeStruct((R, C), F32)
    return pl.pallas_call(
        body, out_shape=(sds,) * 4, grid=(R // tr,), name="adamw_w_ada",
        in_specs=[pl.BlockSpec((tr, 128), lambda i: (i, 0)), pl.BlockSpec((128, C), lambda i: (0, 0))] + [spec] * 3,
        out_specs=(spec,) * 4, compiler_params=_cp("parallel"),
    )(c_t, dmod_cols, w, m, v)


def _adamw_small(small_all, ws, ms, vs):
    def body(s_ref, w_ref, m_ref, v_ref, g_ref, d_ref, nm_ref, nv_ref):
        g = s_ref[0]
        for b in range(1, N_DEV):
            g = g + s_ref[b]
        d, nm, nv = _adamw(w_ref[...], g, m_ref[...], v_ref[...])
        g_ref[...] = g
        d_ref[...] = d
        nm_ref[...] = nm
        nv_ref[...] = nv

    sds = jax.ShapeDtypeStruct((SMALL_ROWS, D_MODEL), F32)
    return pl.pallas_call(
        body, out_shape=(sds,) * 4, name="adamw_small", in_specs=[VMEM_SPEC] * 4, out_specs=(VMEM_SPEC,) * 4,
        compiler_params=pltpu.CompilerParams(vmem_limit_bytes=VMEM_LIMIT_V7X),
    )(small_all, ws, ms, vs)


ROW_MOD, ROW_NORM_G, ROW_CONV_B, ROW_BA, ROW_BX, ROW_LAM, ROW_FINAL_G, ROW_SINKS, ROW_CONV_W, ROW_LOSS = 0, 3, 4, 5, 6, 7, 8, 9, 10, 14


def _pack_small(b_ada, norm_g, conv_b, ba, bx, lam, final_g, sinks, conv_w_full, loss_row=None):
    lane_pad = lambda a: jnp.pad(a.reshape(1, -1), ((0, 0), (0, D_MODEL - a.size)))
    rows = [b_ada.reshape(3, D_MODEL), norm_g, conv_b, ba, bx, lam, final_g.reshape(1, D_MODEL), lane_pad(sinks), conv_w_full,
            jnp.zeros((1, D_MODEL), F32) if loss_row is None else lane_pad(loss_row),
            jnp.zeros((SMALL_ROWS - ROW_LOSS - 1, D_MODEL), F32)]
    return jnp.concatenate([r.astype(F32) for r in rows], axis=0)


def kernel(x, c, positions, w_ada, b_ada, norm_g, w_in, attn_sinks, conv_w, conv_b, rg_wa, rg_ba, rg_wx, rg_bx, rg_lambda, w_attn_proj, w_rnn_proj, w_out, final_g, loss_target, m_w_ada, m_b_ada, m_norm_g, m_w_in, m_attn_sinks, m_conv_w, m_conv_b, m_rg_wa, m_rg_ba, m_rg_wx, m_rg_bx, m_rg_lambda, m_w_attn_proj, m_w_rnn_proj, m_w_out, m_final_g, v_w_ada, v_b_ada, v_norm_g, v_w_in, v_attn_sinks, v_conv_w, v_conv_b, v_rg_wa, v_rg_ba, v_rg_wx, v_rg_bx, v_rg_lambda, v_w_attn_proj, v_w_rnn_proj, v_w_out, v_final_g):
    T = x.shape[1]
    my_chip = lax.axis_index("x") * 2 + lax.axis_index("y")
    my_dev = my_chip * 2 + lax.axis_index("c")
    x2d, tgt = x[0], loss_target[0]
    pos_col = positions.reshape(T, 1)

    chip_idx = my_chip.reshape(1).astype(jnp.int32)
    c_idx = lax.axis_index("c").reshape(1).astype(jnp.int32)
    sq_place = ((D_MODEL, D_MODEL), (SHARD_ROWS, D_MODEL), lambda chip: (chip, 0))
    rg_place = ((RNN_BLOCKS, RNN_BW, RNN_BW), (RNN_BLOCKS, SHARD_RG, RNN_BW), lambda chip: (0, chip, 0))
    in_place = ((D_MODEL, IN_W), (D_MODEL, SHARD_IN), lambda chip: (0, chip))
    placed = _cast_place([w_in[0], w_attn_proj[0], w_rnn_proj[0], w_out[0], rg_wa[0], rg_wx[0]], chip_idx,
                         [in_place, sq_place, sq_place, sq_place, rg_place, rg_place])
    cw_chips, c_all, mod_chips = _gather_mod(c.reshape(1, 1, D_MODEL), w_ada[0], conv_w[0])
    g_ssems, g_rsems, fulls, g_token = _gather_start([p.reshape(s) for p, s in zip(placed, FULL_SHAPES)], mod_chips)
    conv_w_f = jnp.transpose(cw_chips, (1, 0, 2)).reshape(CONV_W, D_MODEL)
    mod_all = jnp.transpose(mod_chips, (1, 0, 2)).reshape(N_DEV, ADA_W) + b_ada
    mod_row = lax.dynamic_slice_in_dim(mod_all, my_dev, 1, axis=0) + g_token[0:1, 0:1]

    h, h_t, tabs = _prenorm(x2d, mod_row, norm_g, pos_col)
    w_in_v = fulls[0]
    proj = _in_projection(h, w_in_v.reshape(D_MODEL, IN_W), chip_idx, None, "own")
    for k, mask in enumerate(CHIP_MASKS):
        w_in_v = _gather_wait(g_ssems[k], g_rsems[k], [w_in_v], [0], proj, f"w_in_{k}")[0]
        w_in_v = _forward_halves([w_in_v], [(0, 0, k)], f"w_in_{k}")[0]
        from_chip = (chip_idx ^ (mask >> 1)).astype(jnp.int32)
        proj = _in_projection(h, w_in_v.reshape(D_MODEL, IN_W), from_chip, proj, f"from_{k}")
    w_in_f = w_in_v.reshape(D_MODEL, IN_W)
    rest = _gather_wait(g_ssems[3], g_rsems[3], list(fulls[1:]), [1, 2, 3, 4, 5], proj, "rest")
    rest = _forward_halves(rest, [(idx - 1, idx, k) for idx in range(1, N_BIG) for k in range(3)], "rest")
    wap_f, wrp_f, wo_f = (g.reshape(D_MODEL, D_MODEL) for g in rest[0:3])
    rwa_f, rwx_f = (g.reshape(RNN_BLOCKS, RNN_BW, RNN_BW) for g in rest[3:5])
    y_attn, qr_b, kr_b = _attn_forward(proj, tabs, attn_sinks)
    y_rnn, h_rnn, *rnn_saved = _rnn_forward(proj, pos_col, conv_w_f, conv_b, rwa_f, rwx_f, rg_ba, rg_bx, rg_lambda)
    (dx2, merged, d_o, d_pa, d_pr, d_ya, d_yr, d_c, d_final_g, d_gate, loss_vec) = _merge_and_head(
        x2d, tgt, y_attn, y_rnn, proj, wap_f, wrp_f, wo_f, mod_row, final_g.reshape(1, D_MODEL))

    sq = (N_CHIPS, 2, SHARD_ROWS // 2, D_MODEL)
    rg = (RNN_BLOCKS, N_CHIPS, 2, SHARD_RG // 2, RNN_BW)
    rg_flat = (RNN_BLOCKS * N_CHIPS, 2, SHARD_RG // 2, RNN_BW)

    def chip_sum_and_start(views, axes, flat, unflat, tags_, kinds_, group):
        from_sib = _swap_halves(views, axes)
        exact, rounded = [None] * len(views), [None] * len(views)
        for shape in dict.fromkeys(flat):
            ids = [k for k, f in enumerate(flat) if f == shape]
            ex, ro = _presum([views[k].reshape(shape) for k in ids],
                             [from_sib[k].reshape(shape[:1] + shape[2:]) for k in ids], c_idx, tags_[ids[0]])
            for k, e, r in zip(ids, ex, ro):
                exact[k], rounded[k] = e.reshape(unflat[k]), r.reshape(unflat[k])
        return _exchange_start(rounded, kinds_, group), exact

    g_ap = _weight_grad(y_attn, [(d_pa, 0, 2)], "w_attn_proj")
    g_rp = _weight_grad(y_rnn, [(d_pr, 0, 2)], "w_rnn_proj")
    g_o = _weight_grad(merged, [(d_o, 0, 2)], "w_out")
    sq_half = (N_CHIPS, SHARD_ROWS // 2, D_MODEL)
    started1, own1 = chip_sum_and_start([g_ap.reshape(sq), g_rp.reshape(sq), g_o.reshape(sq)], [1, 1, 1], [sq] * 3, [sq_half] * 3,
                                  ["w_attn_proj", "w_rnn_proj", "w_out"], ["sq"] * 3, "proj")
    d_q, d_kv, d_ga, d_sinks = _attn_backward(proj, qr_b, kr_b, d_ya, tabs, attn_sinks + started1[4][0, 0])
    d_b, d_conv_w, d_conv_b, d_rwa, d_rwx, d_ba, d_bx, d_lam = _rnn_backward(
        proj, pos_col, h_rnn, rnn_saved, d_yr, conv_w_f, rwa_f, rwx_f, rg_lambda)
    pieces = [(d_q, CB_Q, 2), (d_kv, CB_KV, 1), (d_ga, CB_GA, 2), (d_b, CB_XR, 4), (d_c, CB_MA, 4)]
    g_in = _weight_grad(h_t, pieces, "w_in", a_is_transposed=True)
    started2, own2 = chip_sum_and_start(
        [g_in.reshape(2, D_MODEL // 2, IN_W), d_rwa.reshape(rg), d_rwx.reshape(rg)], [0, 2, 2],
        [(1, 2, D_MODEL // 2, IN_W), rg_flat, rg_flat],
        [(D_MODEL // 2, IN_W), (RNN_BLOCKS, N_CHIPS, SHARD_RG // 2, RNN_BW), (RNN_BLOCKS, N_CHIPS, SHARD_RG // 2, RNN_BW)],
        ["w_in", "rg_wa", "rg_wx"], ["in", "rg", "rg"], "in")
    grad_x, d_shift, d_scale, d_norm_g = _input_backward(pieces, w_in_f, x2d, dx2, mod_row + started2[4][0, 0], norm_g)

    d_mod = jnp.concatenate([d_shift, d_scale, d_gate], axis=1)
    small = _pack_small(d_mod, d_norm_g, d_conv_b, d_ba, d_bx, d_lam, d_final_g, d_sinks[:, :N_HEADS], d_conv_w, loss_vec)
    small_all = _gather_small(small)
    _, lands1 = _exchange_wait(*started1[:4], grad_x, "proj")
    _, lands2 = _exchange_wait(*started2[:4], grad_x, "in")
    tags = ["w_in", "w_attn_proj", "w_rnn_proj", "w_out", "rg_wa", "rg_wx"]
    chip_sums = [own2[0]] + list(own1) + list(own2[1:])
    lands = [lands2[0]] + list(lands1) + list(lands2[1:])
    where = jnp.concatenate([chip_idx, c_idx])
    kinds = ["in", "sq", "sq", "sq", "rg", "rg"]
    groups = [[0], [1, 2, 3], [4, 5]]
    halves = [None] * 6
    for ids in groups:
        for i, half in zip(ids, _sum_landed(kinds[ids[0]], [chip_sums[i] for i in ids], [lands[i] for i in ids], where,
                                            tags[ids[0]])):
            halves[i] = half
    grads = _assemble_with_sibling(halves, [0, 0, 0, 0, 1, 1])
    shapes2d = [(D_MODEL, SHARD_IN), (SHARD_ROWS, D_MODEL), (SHARD_ROWS, D_MODEL), (SHARD_ROWS, D_MODEL),
                (RNN_BLOCKS * SHARD_RG, RNN_BW), (RNN_BLOCKS * SHARD_RG, RNN_BW)]
    big_w = [w_in, w_attn_proj, w_rnn_proj, w_out, rg_wa, rg_wx]
    big_m = [m_w_in, m_w_attn_proj, m_w_rnn_proj, m_w_out, m_rg_wa, m_rg_wx]
    big_v = [v_w_in, v_w_attn_proj, v_w_rnn_proj, v_w_out, v_rg_wa, v_rg_wx]
    res = {}
    for ids in groups:
        flat2d = lambda arrs: [arrs[i].reshape(shapes2d[i]) for i in ids]
        outs = _adamw_shard(flat2d(grads), flat2d(big_w), flat2d(big_m), flat2d(big_v), tags[ids[0]])
        for i, four in zip(ids, outs):
            res[tags[i]] = [o.reshape(big_w[i].shape) for o in four]

    dmod_all = small_all[:, ROW_MOD:ROW_MOD + 3, :].reshape(N_DEV, ADA_W)
    dmod_cols = lax.dynamic_slice_in_dim(dmod_all, my_chip * SHARD_ADA, SHARD_ADA, axis=1)
    c_t = jnp.pad(jnp.transpose(c_all.reshape(N_DEV, D_MODEL)), ((0, 0), (0, 128 - N_DEV)))
    dmod_cols = jnp.pad(dmod_cols, ((0, 128 - N_DEV), (0, 0)))
    res["w_ada"] = [o.reshape(w_ada.shape) for o in _adamw_w_ada(c_t, dmod_cols, w_ada[0], m_w_ada[0], v_w_ada[0])]

    def full_conv(a):
        return lax.dynamic_update_slice_in_dim(jnp.zeros((CONV_W, D_MODEL), F32), a[0], my_chip * (D_MODEL // N_CHIPS), axis=1)

    packed = [_pack_small(p[0], p[1], p[2], p[3], p[4], p[5], p[6], p[7], full_conv(p[8])) for p in (
        (b_ada, norm_g, conv_b, rg_ba, rg_bx, rg_lambda, final_g, attn_sinks, conv_w),
        (m_b_ada, m_norm_g, m_conv_b, m_rg_ba, m_rg_bx, m_rg_lambda, m_final_g, m_attn_sinks, m_conv_w),
        (v_b_ada, v_norm_g, v_conv_b, v_rg_ba, v_rg_bx, v_rg_lambda, v_final_g, v_attn_sinks, v_conv_w))]
    small_out = _adamw_small(small_all, *packed)

    def unpack(slab):
        cw = lax.dynamic_slice_in_dim(slab[ROW_CONV_W:ROW_CONV_W + CONV_W], my_chip * (D_MODEL // N_CHIPS),
                                      D_MODEL // N_CHIPS, axis=1)
        return {
            "b_ada": slab[ROW_MOD:ROW_MOD + 3].reshape(1, ADA_W), "norm_g": slab[ROW_NORM_G:ROW_NORM_G + 1],
            "conv_b": slab[ROW_CONV_B:ROW_CONV_B + 1], "rg_ba": slab[ROW_BA:ROW_BA + 1], "rg_bx": slab[ROW_BX:ROW_BX + 1],
            "rg_lambda": slab[ROW_LAM:ROW_LAM + 1], "final_g": slab[ROW_FINAL_G], "attn_sinks": slab[ROW_SINKS:ROW_SINKS + 1, :N_HEADS],
            "conv_w": cw[None],
        }

    small_res = [unpack(s) for s in small_out]
    order = ["w_ada", "b_ada", "norm_g", "w_in", "attn_sinks", "conv_w", "conv_b", "rg_wa", "rg_ba", "rg_wx", "rg_bx",
             "rg_lambda", "w_attn_proj", "w_rnn_proj", "w_out", "final_g"]
    loss = small_out[0][ROW_LOSS, 0]
    outs = [loss, grad_x[None]]
    for kind in range(4):
        for name in order:
            outs.append(res[name][kind] if name in res else small_res[kind][name])
    return tuple(outs)
```

```python
import numpy as np
import jax
import jax.numpy as jnp
from jax import lax
from jax.experimental import pallas as pl
from jax.experimental.pallas import tpu as pltpu

F32 = jnp.float32
BF16 = jnp.bfloat16

D_MODEL = 1024
N_HEADS = 16
N_KV = 4
HEAD_DIM = 64
GROUP = N_HEADS // N_KV
BLOCK = 128
KV_W = N_KV * HEAD_DIM
ROT_HALF = 8
ROPE_THETA = 500000.0
ATTN_SCALE = 0.125
RNN_BLOCKS = 4
RNN_BW = 256
CONV_W = 4
LRU_C = 8.0
NORM_EPS = 1e-6
IN_W = 6656
CB = 512
N_CB = IN_W // CB
CB_Q, CB_KV, CB_GA, CB_XR, CB_GR, CB_MA, CB_MR = 0, 2, 3, 5, 7, 9, 11
V_COL_BLOCK = 5
N_CHIPS = 4
N_DEV = 8
SHARD_IN = IN_W // N_CHIPS
SHARD_ROWS = D_MODEL // N_CHIPS
SHARD_RG = RNN_BW // N_CHIPS
ADA_W = 3 * D_MODEL
SHARD_ADA = ADA_W // N_CHIPS
SMALL_ROWS = 16

ADAM_LR = 0.001
ADAM_B1 = 0.9
ADAM_B2 = 0.999
ADAM_EPS = 1e-08
ADAM_WD = 0.01
ADAM_STEP = 10

VMEM_LIMIT_V7X = 52 * 1024 * 1024
MESH = pl.DeviceIdType.MESH
ANY = pl.BlockSpec(memory_space=pl.ANY)
VMEM_SPEC = pl.BlockSpec(memory_space=pltpu.VMEM)


def _in_hbm(*arrays):
    return [pltpu.with_memory_space_constraint(a, pltpu.HBM) for a in arrays]


def _cp(*sem):
    return pltpu.CompilerParams(dimension_semantics=sem if sem else None, vmem_limit_bytes=VMEM_LIMIT_V7X)


def _dot(a, b):
    return jnp.dot(a, b, preferred_element_type=F32)


def _dot_nt(a, b):
    return lax.dot_general(a, b, (((1,), (1,)), ((), ())), preferred_element_type=F32)


def _dot_tn(a, b):
    return lax.dot_general(a, b, (((0,), (0,)), ((), ())), preferred_element_type=F32)


def _sigmoid(z):
    return 1.0 / (1.0 + jnp.exp(-z))


def _softplus(z):
    u = jnp.exp(-jnp.abs(z))
    log1p_u = jnp.where(u < 1e-3, u * (1.0 - u * (0.5 - u * (1.0 / 3.0))), jnp.log(1.0 + u))
    return jnp.maximum(z, 0.0) + log1p_u


def _rms(xf):
    return lax.rsqrt(jnp.mean(xf * xf, axis=-1, keepdims=True) + NORM_EPS)


def _me():
    return lax.axis_index("x"), lax.axis_index("y"), lax.axis_index("c")


def _peer(mask):
    x, y, c = _me()
    fx, fy, fc = (mask >> 2) & 1, (mask >> 1) & 1, mask & 1
    return (x ^ fx if fx else x, y ^ fy if fy else y, c ^ fc if fc else c)


def _chip_of(pos):
    return pos[0] * 2 + pos[1]


SIBLING_COLLECTIVE_ID = 0
SIBLING_ONLY = pltpu.CompilerParams(collective_id=SIBLING_COLLECTIVE_ID)


def _sibling_handshake():
    barrier = pltpu.get_barrier_semaphore()
    pl.semaphore_signal(barrier, inc=1, device_id=_peer(1), device_id_type=MESH)
    pl.semaphore_wait(barrier, 1)


CHIP_MASKS = (4, 2, 6)
ALL_MASKS = (1, 2, 3, 4, 5, 6, 7)


HBM_SPEC = pl.BlockSpec(memory_space=pltpu.HBM)
SEM_SPEC = pl.BlockSpec(memory_space=pltpu.SEMAPHORE)
SPLIT_COPY = pltpu.CompilerParams(has_side_effects=pltpu.SideEffectType.DATAFLOW_SIDE_EFFECTING)
N_BIG = 6
FULL_SHAPES = (
    (2, D_MODEL // 2, IN_W),
    (N_CHIPS, 2, SHARD_ROWS // 2, D_MODEL), (N_CHIPS, 2, SHARD_ROWS // 2, D_MODEL), (N_CHIPS, 2, SHARD_ROWS // 2, D_MODEL),
    (RNN_BLOCKS, N_CHIPS, 2, SHARD_RG // 2, RNN_BW), (RNN_BLOCKS, N_CHIPS, 2, SHARD_RG // 2, RNN_BW),
)


def _slot(full, idx, chip, half):
    if idx == 0:
        return full.at[half, :, pl.ds(pl.multiple_of(chip * SHARD_IN, 128), SHARD_IN)]
    return full.at[chip, half] if idx in (1, 2, 3) else full.at[:, chip, half]


def _three_halves(full, idx):
    return full.at[pl.ds(0, 3), 0] if idx in (1, 2, 3) else full.at[:, pl.ds(0, 3), 0]


def _gather_start(fulls, after):
    def body(*refs):
        full_refs = refs[:N_BIG]
        ssems, rsems = refs[N_BIG + 1:N_BIG + 5], refs[N_BIG + 5:N_BIG + 9]
        token = refs[2 * N_BIG + 9]
        me = _me()
        my_chip = _chip_of(me)
        for idx in range(N_BIG):
            for k, mask in enumerate(CHIP_MASKS):
                pair = k if idx == 0 else 3
                mine = _slot(full_refs[idx], idx, my_chip, me[2])
                pltpu.make_async_remote_copy(src_ref=mine, dst_ref=mine, send_sem=ssems[pair], recv_sem=rsems[pair],
                                             device_id=_peer(mask), device_id_type=MESH).start()
        token[...] = jnp.zeros_like(token)

    sem = pltpu.SemaphoreType.DMA(())
    out_shape = (sem,) * 8 + tuple(pltpu.HBM(f.shape, f.dtype) for f in fulls) + (jax.ShapeDtypeStruct((8, 128), F32),)
    outs = pl.pallas_call(
        body, out_shape=out_shape, name="gather_start",
        in_specs=[HBM_SPEC] * N_BIG + [ANY], out_specs=tuple([SEM_SPEC] * 8 + [HBM_SPEC] * N_BIG + [VMEM_SPEC]),
        input_output_aliases={i: 8 + i for i in range(N_BIG)}, compiler_params=SPLIT_COPY,
    )(*[pltpu.with_memory_space_constraint(f, pltpu.HBM) for f in fulls], after)
    return outs[0:4], outs[4:8], outs[8:8 + N_BIG], outs[8 + N_BIG]


def _gather_wait(ssem, rsem, arrays, idxs, after, tag):
    n = len(arrays)

    def body(*refs):
        full_refs, ssem_ref, rsem_ref = refs[:n], refs[n], refs[n + 1]
        me = _me()
        for full, idx in zip(full_refs, idxs):
            region = _slot(full, 0, _chip_of(me), me[2]) if idx == 0 else _three_halves(full, idx)
            arrived = pltpu.make_async_remote_copy(
                src_ref=region, dst_ref=region, send_sem=ssem_ref, recv_sem=rsem_ref, device_id=me, device_id_type=MESH)
            arrived.wait_send()
            arrived.wait_recv()

    outs = pl.pallas_call(
        body, out_shape=tuple(pltpu.HBM(a.shape, a.dtype) for a in arrays), name=f"gather_wait_{tag}",
        in_specs=[HBM_SPEC] * n + [SEM_SPEC, SEM_SPEC, ANY], out_specs=tuple([HBM_SPEC] * n),
        input_output_aliases={i: i for i in range(n)}, compiler_params=SPLIT_COPY,
    )(*arrays, ssem, rsem, after)
    return list(outs)


def _forward_halves(arrays, items, tag):
    n, m = len(arrays), len(items)

    def body(*refs):
        outs, ssem, rsem = refs[n:2 * n], refs[2 * n], refs[2 * n + 1]
        me = _me()
        sib = _peer(1)
        _sibling_handshake()
        cps = []
        for j, (pos, idx, k) in enumerate(items):
            chip = _chip_of(_peer(CHIP_MASKS[k]))
            cp = pltpu.make_async_remote_copy(
                src_ref=_slot(outs[pos], idx, chip, me[2]), dst_ref=_slot(outs[pos], idx, chip, me[2]),
                send_sem=ssem.at[j], recv_sem=rsem.at[j], device_id=sib, device_id_type=MESH)
            cp.start()
            cps.append(cp)
        for j, (pos, idx, k) in enumerate(items):
            chip = _chip_of(_peer(CHIP_MASKS[k]))
            pltpu.make_async_remote_copy(
                src_ref=_slot(outs[pos], idx, chip, me[2]), dst_ref=_slot(outs[pos], idx, chip, 1 - me[2]),
                send_sem=ssem.at[j], recv_sem=rsem.at[j], device_id=sib, device_id_type=MESH).wait_recv()
        for cp in cps:
            cp.wait_send()

    outs = pl.pallas_call(
        body, out_shape=tuple(jax.ShapeDtypeStruct(a.shape, a.dtype) for a in arrays), name=f"forward_halves_{tag}",
        in_specs=[ANY] * n, out_specs=tuple([ANY] * n), input_output_aliases={i: i for i in range(n)},
        scratch_shapes=[pltpu.SemaphoreType.DMA((m,)), pltpu.SemaphoreType.DMA((m,))], compiler_params=SIBLING_ONLY,
    )(*arrays)
    return list(outs)


def _gather_mod(c_row, w_ada_s, conv_w_s):
    def body(c_ref, wada_ref, cw_s, cw_f, call_ref, mod_ref, wsend, wrecv, lsem, csend, crecv, msend, mrecv):
        me = _me()
        my_chip = _chip_of(me)
        my_dev = my_chip * 2 + me[2]
        sends = []
        for k, mask in enumerate(CHIP_MASKS):
            cp = pltpu.make_async_remote_copy(src_ref=cw_s, dst_ref=cw_f.at[my_chip], send_sem=wsend.at[k], recv_sem=wrecv.at[k],
                                              device_id=_peer(mask), device_id_type=MESH)
            cp.start()
            sends.append(cp)
        local = [pltpu.make_async_copy(cw_s, cw_f.at[my_chip], lsem.at[0])]
        for cp in local:
            cp.start()

        call_ref[my_dev] = c_ref[0]
        csends = []
        for k, mask in enumerate(ALL_MASKS):
            cp = pltpu.make_async_remote_copy(
                src_ref=c_ref.at[0], dst_ref=call_ref.at[my_dev],
                send_sem=csend.at[k], recv_sem=crecv.at[k], device_id=_peer(mask), device_id_type=MESH)
            cp.start()
            csends.append(cp)
        for k, mask in enumerate(ALL_MASKS):
            frm = _peer(mask)
            pltpu.make_async_remote_copy(
                src_ref=c_ref.at[0], dst_ref=call_ref.at[_chip_of(frm) * 2 + frm[2]],
                send_sem=csend.at[k], recv_sem=crecv.at[k], device_id=frm, device_id_type=MESH).wait_recv()
        for cp in csends:
            cp.wait_send()

        c_all = call_ref[...].reshape(N_DEV, D_MODEL).astype(BF16)
        mod_ref[my_chip] = _dot(c_all, wada_ref[...].astype(BF16))
        msends = []
        for k, mask in enumerate(CHIP_MASKS):
            cp = pltpu.make_async_remote_copy(
                src_ref=mod_ref.at[my_chip], dst_ref=mod_ref.at[my_chip],
                send_sem=msend.at[k], recv_sem=mrecv.at[k], device_id=_peer(mask), device_id_type=MESH)
            cp.start()
            msends.append(cp)
        for k, mask in enumerate(CHIP_MASKS):
            frm = _peer(mask)
            pltpu.make_async_remote_copy(
                src_ref=mod_ref.at[my_chip], dst_ref=mod_ref.at[_chip_of(frm)],
                send_sem=msend.at[k], recv_sem=mrecv.at[k], device_id=frm, device_id_type=MESH).wait_recv()
        for cp in msends:
            cp.wait_send()

        for k, mask in enumerate(CHIP_MASKS):
            frm = _peer(mask)
            pltpu.make_async_remote_copy(src_ref=cw_s, dst_ref=cw_f.at[_chip_of(frm)], send_sem=wsend.at[k], recv_sem=wrecv.at[k],
                                         device_id=frm, device_id_type=MESH).wait_recv()
        for cp in sends:
            cp.wait_send()
        for cp in local:
            cp.wait()

    out_shape = (
        jax.ShapeDtypeStruct((N_CHIPS, CONV_W, D_MODEL // N_CHIPS), F32),
        jax.ShapeDtypeStruct((N_DEV, 1, D_MODEL), F32),
        jax.ShapeDtypeStruct((N_CHIPS, N_DEV, SHARD_ADA), F32),
    )
    return pl.pallas_call(
        body, out_shape=out_shape, name="gather_mod",
        in_specs=[VMEM_SPEC, VMEM_SPEC, ANY], out_specs=(ANY, VMEM_SPEC, VMEM_SPEC),
        scratch_shapes=[
            pltpu.SemaphoreType.DMA((3,)), pltpu.SemaphoreType.DMA((3,)), pltpu.SemaphoreType.DMA((1,)),
            pltpu.SemaphoreType.DMA((7,)), pltpu.SemaphoreType.DMA((7,)),
            pltpu.SemaphoreType.DMA((3,)), pltpu.SemaphoreType.DMA((3,)),
        ],
        compiler_params=pltpu.CompilerParams(vmem_limit_bytes=VMEM_LIMIT_V7X),
    )(c_row, w_ada_s, conv_w_s)


def _cast_place(shards, chip_idx, places):
    n = len(shards)

    def body(chip_ref, *refs):
        for s_ref, o_ref in zip(refs[:n], refs[n:]):
            o_ref[...] = s_ref[...].astype(BF16)

    grid_spec = pltpu.PrefetchScalarGridSpec(
        num_scalar_prefetch=1, grid=(1,),
        in_specs=[pl.BlockSpec(s.shape, lambda i, chip_ref, nd=s.ndim: (0,) * nd) for s in shards],
        out_specs=tuple(pl.BlockSpec(block, lambda i, chip_ref, im=im: im(chip_ref[0])) for _, block, im in places))
    return pl.pallas_call(
        body, out_shape=tuple(jax.ShapeDtypeStruct(full, BF16) for full, _, _ in places), grid_spec=grid_spec,
        name="cast_place", compiler_params=_cp("arbitrary"),
    )(chip_idx, *_in_hbm(*shards))


def _shard_of(ref, kind, chip):
    if kind == "in":
        return ref.at[:, pl.ds(pl.multiple_of(chip * SHARD_IN, 128), SHARD_IN)]
    return ref.at[chip] if kind == "sq" else ref.at[:, chip]


def _land_shape(src, kind):
    if kind == "in":
        return (3, src.shape[0], SHARD_IN)
    return (3,) + src.shape[1:] if kind == "sq" else (3, src.shape[0]) + src.shape[2:]


def _exchange_start(srcs, kinds, tag):
    n = len(srcs)
    lands = [pltpu.with_memory_space_constraint(lax.empty(_land_shape(s, k), s.dtype), pltpu.HBM) for s, k in zip(srcs, kinds)]

    def body(*refs):
        src_refs, land_refs = refs[:n], refs[n:2 * n]
        ssems, rsems = refs[2 * n:3 * n], refs[3 * n:4 * n]
        token = refs[6 * n]
        for i in range(n):
            for k, mask in enumerate(CHIP_MASKS):
                to = _peer(mask)
                pltpu.make_async_remote_copy(
                    src_ref=_shard_of(src_refs[i], kinds[i], _chip_of(to)), dst_ref=land_refs[i].at[k],
                    send_sem=ssems[i], recv_sem=rsems[i], device_id=to, device_id_type=MESH).start()
        token[...] = jnp.zeros_like(token)

    sem = pltpu.SemaphoreType.DMA(())
    out_shape = ((sem,) * (2 * n) + tuple(pltpu.HBM(s.shape, s.dtype) for s in srcs)
                 + tuple(pltpu.HBM(l.shape, l.dtype) for l in lands) + (jax.ShapeDtypeStruct((8, 128), F32),))
    outs = pl.pallas_call(
        body, out_shape=out_shape, name=f"exchange_start_{tag}",
        in_specs=[HBM_SPEC] * (2 * n), out_specs=tuple([SEM_SPEC] * (2 * n) + [HBM_SPEC] * (2 * n) + [VMEM_SPEC]),
        input_output_aliases={i: 2 * n + i for i in range(2 * n)},
        compiler_params=pltpu.CompilerParams(has_side_effects=pltpu.SideEffectType.DATAFLOW_SIDE_EFFECTING),
    )(*[pltpu.with_memory_space_constraint(s, pltpu.HBM) for s in srcs], *lands)
    return outs[:n], outs[n:2 * n], outs[2 * n:3 * n], outs[3 * n:4 * n], outs[4 * n]


def _exchange_wait(ssems, rsems, srcs, lands, after, tag):
    n = len(srcs)

    def body(*refs):
        land_refs = refs[n:2 * n]
        ssem_refs, rsem_refs = refs[2 * n:3 * n], refs[3 * n:4 * n]
        for i in range(n):
            all_three = pltpu.make_async_remote_copy(
                src_ref=land_refs[i], dst_ref=land_refs[i], send_sem=ssem_refs[i], recv_sem=rsem_refs[i],
                device_id=_me(), device_id_type=MESH)
            all_three.wait_send()
            all_three.wait_recv()

    outs = pl.pallas_call(
        body, out_shape=tuple(pltpu.HBM(a.shape, a.dtype) for a in list(srcs) + list(lands)), name=f"exchange_wait_{tag}",
        in_specs=[HBM_SPEC] * (2 * n) + [SEM_SPEC] * (2 * n) + [ANY], out_specs=tuple([HBM_SPEC] * (2 * n)),
        input_output_aliases={i: i for i in range(2 * n)},
        compiler_params=pltpu.CompilerParams(has_side_effects=pltpu.SideEffectType.DATAFLOW_SIDE_EFFECTING),
    )(*srcs, *lands, *ssems, *rsems, after)
    return outs[:n], outs[n:]


def _gather_small(small):
    def body(small_ref, small_all, ssend, srecv):
        me = _me()
        my_dev = _chip_of(me) * 2 + me[2]
        small_all[my_dev] = small_ref[...]
        ssends = []
        for k, mask in enumerate(ALL_MASKS):
            cp = pltpu.make_async_remote_copy(
                src_ref=small_ref, dst_ref=small_all.at[my_dev],
                send_sem=ssend.at[k], recv_sem=srecv.at[k], device_id=_peer(mask), device_id_type=MESH)
            cp.start()
            ssends.append(cp)
        for k, mask in enumerate(ALL_MASKS):
            frm = _peer(mask)
            pltpu.make_async_remote_copy(
                src_ref=small_ref, dst_ref=small_all.at[_chip_of(frm) * 2 + frm[2]],
                send_sem=ssend.at[k], recv_sem=srecv.at[k], device_id=frm, device_id_type=MESH).wait_recv()
        for cp in ssends:
            cp.wait_send()

    return pl.pallas_call(
        body, out_shape=jax.ShapeDtypeStruct((N_DEV, SMALL_ROWS, D_MODEL), F32), name="gather_small",
        in_specs=[VMEM_SPEC], out_specs=VMEM_SPEC,
        scratch_shapes=[pltpu.SemaphoreType.DMA((7,)), pltpu.SemaphoreType.DMA((7,))],
    )(small)


def _half_of(ref, axis, half):
    return ref.at[(slice(None),) * axis + (half,)]


def _swap_halves(parts, axes):
    n = len(parts)

    def body(*refs):
        ins, outs, ssem, rsem = refs[:n], refs[n:2 * n], refs[2 * n], refs[2 * n + 1]
        c = lax.axis_index("c")
        _sibling_handshake()
        cps = [pltpu.make_async_remote_copy(src_ref=_half_of(ins[i], axes[i], 1 - c), dst_ref=outs[i], send_sem=ssem.at[i],
                                            recv_sem=rsem.at[i], device_id=_peer(1), device_id_type=MESH) for i in range(n)]
        for cp in cps:
            cp.start()
        for cp in cps:
            cp.wait()

    shapes = [p.shape[:a] + p.shape[a + 1:] for p, a in zip(parts, axes)]
    return pl.pallas_call(
        body, out_shape=tuple(jax.ShapeDtypeStruct(s, p.dtype) for s, p in zip(shapes, parts)), name="swap_halves",
        in_specs=[ANY] * n, out_specs=tuple([ANY] * n),
        scratch_shapes=[pltpu.SemaphoreType.DMA((n,)), pltpu.SemaphoreType.DMA((n,))], compiler_params=SIBLING_ONLY,
    )(*parts)


def _presum(mines, sibs, c_idx, tag):
    n = len(mines)
    S, _, R, C = mines[0].shape
    tr = min(R, 256)
    tc = SHARD_IN if C % SHARD_IN == 0 else (C // 2 if n > 1 and C % 256 == 0 else C)

    def body(c_ref, *refs):
        for k in range(n):
            total = refs[k][:, 0] + refs[n + k][...]
            refs[2 * n + k][...] = total
            refs[3 * n + k][...] = total.astype(BF16)

    out_spec = pl.BlockSpec((S, tr, tc), lambda i, j, c_ref: (0, i, j))
    grid_spec = pltpu.PrefetchScalarGridSpec(
        num_scalar_prefetch=1, grid=(R // tr, C // tc),
        in_specs=[pl.BlockSpec((S, 1, tr, tc), lambda i, j, c_ref: (0, c_ref[0], i, j))] * n + [out_spec] * n,
        out_specs=(out_spec,) * (2 * n))
    outs = pl.pallas_call(
        body, out_shape=(jax.ShapeDtypeStruct((S, R, C), F32),) * n + (jax.ShapeDtypeStruct((S, R, C), BF16),) * n,
        grid_spec=grid_spec, name=f"presum_{tag}", compiler_params=_cp("parallel", "parallel"),
    )(c_idx, *mines, *sibs)
    return list(outs[:n]), list(outs[n:])


def _assemble_with_sibling(parts, axes):
    n = len(parts)

    def body(*refs):
        outs, ssem, rsem = refs[n:2 * n], refs[2 * n], refs[2 * n + 1]
        c = lax.axis_index("c")
        _sibling_handshake()
        cps = [pltpu.make_async_remote_copy(
            src_ref=_half_of(outs[i], axes[i], c), dst_ref=_half_of(outs[i], axes[i], c), send_sem=ssem.at[i],
            recv_sem=rsem.at[i], device_id=_peer(1), device_id_type=MESH) for i in range(n)]
        for cp in cps:
            cp.start()
        for i in range(n):
            pltpu.make_async_remote_copy(
                src_ref=_half_of(outs[i], axes[i], c), dst_ref=_half_of(outs[i], axes[i], 1 - c), send_sem=ssem.at[i],
                recv_sem=rsem.at[i], device_id=_peer(1), device_id_type=MESH).wait_recv()
        for cp in cps:
            cp.wait_send()

    return pl.pallas_call(
        body, out_shape=tuple(jax.ShapeDtypeStruct(p.shape, p.dtype) for p in parts), name="assemble_with_sibling",
        in_specs=[ANY] * n, out_specs=tuple([ANY] * n), input_output_aliases={i: i for i in range(n)},
        scratch_shapes=[pltpu.SemaphoreType.DMA((n,)), pltpu.SemaphoreType.DMA((n,))], compiler_params=SIBLING_ONLY,
    )(*parts)


def _rope_lane_frequencies():
    inv = np.float32(ROPE_THETA) ** (-(np.arange(0, 2 * ROT_HALF, 2, dtype=np.float32)) / np.float32(2 * ROT_HALF))
    lane = np.arange(128) % HEAD_DIM
    return jnp.asarray(np.where(lane < 2 * ROT_HALF, inv[lane % ROT_HALF], 0.0).astype(np.float32)[None, :])


def _rope_tables(pos, freq):
    ang = pos.astype(F32) * freq
    c, s = jnp.cos(ang), jnp.sin(ang)
    m = lax.broadcasted_iota(jnp.int32, ang.shape, 1) & (HEAD_DIM - 1)
    return (jnp.where(m < 2 * ROT_HALF, c, 1.0), jnp.where(m < ROT_HALF, -s, 0.0),
            jnp.where((m >= ROT_HALF) & (m < 2 * ROT_HALF), s, 0.0))


def _columns(t):
    return [t[:, i:i + 128] for i in range(0, t.shape[-1], 128)]


def _rope(t, c, sa, sb):
    return jnp.concatenate(
        [x * c + pltpu.roll(x, 128 - ROT_HALF, 1) * sa + pltpu.roll(x, ROT_HALF, 1) * sb for x in _columns(t)], axis=1)


def _unrope(d, c, sa, sb):
    return jnp.concatenate(
        [x * c + pltpu.roll(x * sa, ROT_HALF, 1) + pltpu.roll(x * sb, 128 - ROT_HALF, 1) for x in _columns(d)], axis=1)


def _prenorm(x, mod_row, norm_g, pos_col):
    T = x.shape[0]
    tm = min(T, 512)

    def body(x_ref, mod_ref, g_ref, pos_ref, f_ref, h_ref, ht_ref, c_ref, sa_ref, sb_ref):
        xf = x_ref[...]
        shift, scale = mod_ref[:, 0:D_MODEL], mod_ref[:, D_MODEL:2 * D_MODEL]
        h = (xf * _rms(xf)) * g_ref[...] * (1.0 + scale) + shift
        h_ref[...] = h.astype(BF16)
        ht_ref[...] = h.T.astype(BF16)
        c_ref[...], sa_ref[...], sb_ref[...] = _rope_tables(pos_ref[...], f_ref[...])

    tab = jax.ShapeDtypeStruct((T, 128), F32)
    tok = lambda w: pl.BlockSpec((tm, w), lambda i: (i, 0))
    row = lambda w: pl.BlockSpec((1, w), lambda i: (0, 0))
    outs = pl.pallas_call(
        body, out_shape=(jax.ShapeDtypeStruct((T, D_MODEL), BF16), jax.ShapeDtypeStruct((D_MODEL, T), BF16), tab, tab, tab),
        grid=(T // tm,), name="prenorm",
        in_specs=[tok(D_MODEL), row(ADA_W), row(D_MODEL), tok(1), row(128)],
        out_specs=(tok(D_MODEL), pl.BlockSpec((D_MODEL, tm), lambda i: (0, i)), tok(128), tok(128), tok(128)),
        compiler_params=_cp("parallel"),
    )(x, *_in_hbm(mod_row, norm_g), pos_col, _rope_lane_frequencies())
    return outs[0], outs[1], tuple(outs[2:])


def _in_projection(h, w_in, chips, into, tag):
    T = h.shape[0]
    tm, tn = min(T, 512), SHARD_IN
    k = chips.shape[0]

    def body(chip_ref, h_ref, w_ref, *rest):
        rest[-1][...] = _dot(h_ref[...], w_ref[...])

    w_spec = pl.BlockSpec((D_MODEL, tn), lambda s, i, c: (0, c[s]), **({"pipeline_mode": pl.Buffered(1)} if k == 1 else {}))
    in_specs = [pl.BlockSpec((tm, D_MODEL), lambda s, i, c: (i, 0)), w_spec]
    args = [chips, h, w_in]
    aliases = {}
    if into is not None:
        in_specs.append(ANY)
        args.append(into)
        aliases = {3: 0}
    grid_spec = pltpu.PrefetchScalarGridSpec(num_scalar_prefetch=1, grid=(k, T // tm), in_specs=in_specs,
                                             out_specs=pl.BlockSpec((tm, tn), lambda s, i, c: (i, c[s])))
    return pl.pallas_call(
        body, out_shape=jax.ShapeDtypeStruct((T, IN_W), F32), grid_spec=grid_spec, name=f"in_projection_{tag}",
        input_output_aliases=aliases, compiler_params=_cp("parallel", "parallel"),
    )(*args)


def _attn_mask(n):
    qi = lax.broadcasted_iota(jnp.int32, (GROUP * BLOCK, BLOCK), 0) & (BLOCK - 1)
    j = lax.broadcasted_iota(jnp.int32, (GROUP * BLOCK, BLOCK), 1)
    own = j <= qi
    return own, jnp.logical_not(own) & (n == 0)


def _fold(x, own):
    return jnp.where(own, x[:, BLOCK:2 * BLOCK], x[:, 0:BLOCK])


def _unfold(xf, own):
    zero = jnp.zeros_like(xf)
    return jnp.concatenate([jnp.where(own, zero, xf), jnp.where(own, xf, zero)], axis=1)


ROW_GROUP_HEAD = (0, 2, 1, 3)


def _sink_col(sink_ref, kh):
    rowg = lax.broadcasted_iota(jnp.int32, (GROUP * BLOCK, 1), 0) // BLOCK
    col = jnp.full((GROUP * BLOCK, 1), sink_ref[0, GROUP * kh + ROW_GROUP_HEAD[0]], F32)
    for g in range(1, GROUP):
        col = jnp.where(rowg == g, sink_ref[0, GROUP * kh + ROW_GROUP_HEAD[g]], col)
    return col


def _low_lanes(shape):
    return lax.broadcasted_iota(jnp.int32, shape, 1) < HEAD_DIM


def _kv_pair_operand(prev, cur, kh):
    c = 128 * (kh // 2)
    col = jnp.concatenate([prev[:, c:c + 128], cur[:, c:c + 128]], axis=0).astype(F32)
    if kh % 2 == 0:
        lo = jnp.where(_low_lanes(col.shape), col, 0.0)
        hi = pltpu.roll(lo, HEAD_DIM, 1)
    else:
        hi = jnp.where(_low_lanes(col.shape), 0.0, col)
        lo = pltpu.roll(hi, HEAD_DIM, 1)
    return jnp.concatenate([lo, hi], axis=0).astype(BF16)


def _pair_rows(x, kh):
    c = 2 * 128 * kh
    return jnp.concatenate([x[:, c:c + 128], x[:, c + 128:c + 256]], axis=0)


def _restack(big):
    return jnp.concatenate([big[:, 0:2 * BLOCK], big[:, 2 * BLOCK:4 * BLOCK]], axis=0)


def _unrestack(stacked):
    return jnp.concatenate([stacked[0:2 * BLOCK], stacked[2 * BLOCK:4 * BLOCK]], axis=1)


def _fold_pair(x2, kh):
    low = _low_lanes((2 * BLOCK, 128))
    mixed = jnp.where(low, x2[0:2 * BLOCK], x2[2 * BLOCK:4 * BLOCK])
    total = mixed + pltpu.roll(mixed, HEAD_DIM, 1)
    return jnp.where(low, total, 0.0) if kh % 2 == 0 else jnp.where(low, 0.0, total)


def _attn_scores(qr, k2, kh):
    q2 = _pair_rows(qr, kh).astype(BF16)
    return q2, _restack(_dot_nt(q2, k2))


def _attn_softmax(s, sink_col, mask):
    own, no_key = mask
    s = jnp.where(no_key, -1e30, _fold(s, own))
    m = jnp.maximum(jnp.max(s, axis=-1, keepdims=True), sink_col)
    p = jnp.exp(s - m)
    p_sink = jnp.exp(sink_col - m)
    denom = jnp.sum(p, axis=-1, keepdims=True) + p_sink
    return p / denom, p_sink / denom


def _attn_forward(proj, tabs, sinks):
    T = proj.shape[0]
    nb = T // BLOCK

    def body(q_ref, kvc_ref, kvp_ref, g0_ref, g1_ref, cc, sac, sbc, cp_, sap, sbp, sink_ref, y_ref, qrb_ref, krb_ref, p_ref):
        n = pl.program_id(0)
        tc = tcur = (cc[...], sac[...], sbc[...])
        tprev = (cp_[...], sap[...], sbp[...])
        qr = _rope(q_ref[...], *tc) * ATTN_SCALE
        kr_cur = _rope(kvc_ref[:, 0:KV_W], *tcur)
        kr_prev = _rope(kvp_ref[:, 0:KV_W], *tprev)
        qrb_ref[...] = qr.astype(BF16)
        krb_ref[...] = kr_cur.astype(BF16)
        v_cur, v_prev = kvc_ref[:, KV_W:2 * KV_W], kvp_ref[:, KV_W:2 * KV_W]
        mask = _attn_mask(n)
        outs = []
        k2s = [_kv_pair_operand(kr_prev, kr_cur, kh) for kh in range(N_KV)]
        v2s = [_kv_pair_operand(v_prev, v_cur, kh) for kh in range(N_KV)]
        scores = [_attn_scores(qr, k2s[kh], kh) for kh in range(N_KV)]
        p_parts = []
        for kh in range(N_KV):
            pn, _ = _attn_softmax(scores[kh][1], _sink_col(sink_ref, kh), mask)
            p_parts += [pn[g * BLOCK:(g + 1) * BLOCK] for g in range(GROUP)]
            o_big = _dot(_unrestack(_unfold(pn.astype(BF16), mask[0])), v2s[kh])
            outs += [o_big[0:BLOCK], o_big[BLOCK:2 * BLOCK]]
        p_ref[...] = jnp.concatenate(p_parts, axis=1)
        o = jnp.concatenate(outs, axis=1)
        g = jnp.concatenate([g0_ref[...], g1_ref[...]], axis=1)
        y_ref[...] = (o * (g * _sigmoid(g))).astype(BF16)

    def blk(w, cb):
        return pl.BlockSpec((BLOCK, w), lambda n, cb=cb: (n, cb))

    prev = lambda w, cb: pl.BlockSpec((BLOCK, w), lambda n, cb=cb: (jnp.maximum(n - 1, 0), cb))
    return pl.pallas_call(
        body, grid=(nb,), name="attn_forward",
        out_shape=(jax.ShapeDtypeStruct((T, D_MODEL), BF16), jax.ShapeDtypeStruct((T, D_MODEL), BF16),
                   jax.ShapeDtypeStruct((T, KV_W), BF16), jax.ShapeDtypeStruct((T, N_HEADS * BLOCK), F32)),
        in_specs=[blk(D_MODEL, 0), blk(CB, CB_KV), prev(CB, CB_KV), blk(CB, CB_GA), blk(CB, CB_GA + 1),
                  blk(128, 0), blk(128, 0), blk(128, 0), prev(128, 0), prev(128, 0), prev(128, 0),
                  pl.BlockSpec(memory_space=pltpu.SMEM)],
        out_specs=(blk(D_MODEL, 0), blk(D_MODEL, 0), blk(KV_W, 0), blk(N_HEADS * BLOCK, 0)),
        compiler_params=_cp("parallel"),
    )(proj, proj, proj, proj, proj, *tabs, *tabs, sinks)


def _scan_rows8():
    return lax.broadcasted_iota(jnp.int32, (8, D_MODEL), 0)


def _scan_forward(a_ref, b_ref, h_ref, carry, rows):
    row = _scan_rows8()

    def group(i, carry):
        off = pl.multiple_of(i * 8, 8)
        a, b = a_ref[pl.ds(off, 8), :], b_ref[pl.ds(off, 8), :]
        for d in (1, 2, 4):
            ok = row >= d
            b = jnp.where(ok, a * pltpu.roll(b, d, 0) + b, b)
            a = jnp.where(ok, a * pltpu.roll(a, d, 0), a)
        h = a * carry + b
        h_ref[pl.ds(off, 8), :] = h
        return h[7:8, :]

    return lax.fori_loop(0, rows // 8, group, carry)


def _scan_backward(a_ref, g_ref, lam_ref, carry, rows):
    row = _scan_rows8()

    def group(i, carry):
        off = pl.multiple_of((rows // 8 - 1 - i) * 8, 8)
        a, g = a_ref[pl.ds(off, 8), :], g_ref[pl.ds(off, 8), :]
        b = a * g
        for d in (1, 2, 4):
            ok = row < 8 - d
            b = jnp.where(ok, a * pltpu.roll(b, 8 - d, 0) + b, b)
            a = jnp.where(ok, a * pltpu.roll(a, 8 - d, 0), a)
        mu = a * carry + b
        mu_below = jnp.where(row == 7, carry, pltpu.roll(mu, 7, 0))
        lam_ref[pl.ds(off, 8), :] = g + mu_below
        return mu[0:1, :]

    return lax.fori_loop(0, rows // 8, group, carry)


def _conv_taps(xbuf, xr, tail):
    rows = xr.shape[0]
    xbuf[0:8, :] = tail
    xbuf[8:rows + 8, :] = xr
    return [xbuf[pl.ds(8 - (CONV_W - 1 - k), rows), :] for k in range(CONV_W - 1)] + [xr]


def _rnn_gates(xbuf, xr, tail, cw, cb, wa_ref, wx_ref, ba, bx, sp, reset):
    xs = _conv_taps(xbuf, xr, tail)
    xc = xs[0] * cw[0:1, :]
    for k in range(1, CONV_W):
        xc = xc + xs[k] * cw[k:k + 1, :]
    xc = xc + cb
    xcb = xc.astype(BF16)
    za = jnp.concatenate([_dot(xcb[:, RNN_BW * j:RNN_BW * (j + 1)], wa_ref[j]) for j in range(RNN_BLOCKS)], axis=1) + ba
    zx = jnp.concatenate([_dot(xcb[:, RNN_BW * j:RNN_BW * (j + 1)], wx_ref[j]) for j in range(RNN_BLOCKS)], axis=1) + bx
    r, i = _sigmoid(za), _sigmoid(zx)
    neg_log_a = LRU_C * r * sp
    a_raw = jnp.exp(-neg_log_a)
    mult_raw = jnp.sqrt(jnp.tanh(neg_log_a) * (1.0 + a_raw * a_raw))
    a = jnp.where(reset, 0.0, a_raw)
    mult = jnp.where(reset, 1.0, mult_raw)
    return xc, r, i, a, mult


def _rnn_forward(proj, pos_col, conv_w, conv_b, rwa, rwx, ba, bx, lam):
    T = proj.shape[0]
    tr = min(T, 256)

    def body(x0, x1, g0, g1, pos_ref, cw_ref, cb_ref, wa_ref, wx_ref, ba_ref, bx_ref, lam_ref,
             y_ref, h_ref, xc_ref, r_ref, i_ref, a_ref, mult_ref, xbuf, bbuf, tail, carry):
        t = pl.program_id(0)

        @pl.when(t == 0)
        def _():
            tail[...] = jnp.zeros_like(tail)
            carry[...] = jnp.zeros_like(carry)

        xr = jnp.concatenate([x0[...], x1[...]], axis=1)
        sp = _softplus(-lam_ref[...])
        reset = pos_ref[...] == 0
        xc, r, i, a, mult = _rnn_gates(
            xbuf, xr, tail[...], cw_ref[...], cb_ref[...], wa_ref, wx_ref, ba_ref[...], bx_ref[...], sp, reset)
        xc_ref[...] = xc
        r_ref[...] = r
        i_ref[...] = i
        a_ref[...] = a
        mult_ref[...] = mult
        bbuf[...] = mult * (i * xc)
        last = _scan_forward(a_ref, bbuf, h_ref, carry[0:1, :], tr)
        carry[...] = jnp.broadcast_to(last, carry.shape)
        tail[...] = xr[tr - 8:tr, :]
        g = jnp.concatenate([g0[...], g1[...]], axis=1)
        y_ref[...] = (h_ref[...] * (g * _sigmoid(g))).astype(BF16)

    blk = lambda cb: pl.BlockSpec((tr, CB), lambda t, cb=cb: (t, cb))
    row = lambda w: pl.BlockSpec((1, w), lambda t: (0, 0))
    full3 = pl.BlockSpec((RNN_BLOCKS, RNN_BW, RNN_BW), lambda t: (0, 0, 0))
    tok = pl.BlockSpec((tr, D_MODEL), lambda t: (t, 0))
    act = jax.ShapeDtypeStruct((T, D_MODEL), F32)
    return pl.pallas_call(
        body, out_shape=(jax.ShapeDtypeStruct((T, D_MODEL), BF16),) + (act,) * 6,
        grid=(T // tr,), name="rnn_forward",
        in_specs=[blk(CB_XR), blk(CB_XR + 1), blk(CB_GR), blk(CB_GR + 1), pl.BlockSpec((tr, 1), lambda t: (t, 0)),
                  pl.BlockSpec((CONV_W, D_MODEL), lambda t: (0, 0)), row(D_MODEL), full3, full3,
                  row(D_MODEL), row(D_MODEL), row(D_MODEL)],
        out_specs=(tok,) * 7,
        scratch_shapes=[pltpu.VMEM((tr + 8, D_MODEL), F32), pltpu.VMEM((tr, D_MODEL), F32),
                        pltpu.VMEM((8, D_MODEL), F32), pltpu.VMEM((8, D_MODEL), F32)],
        compiler_params=_cp("arbitrary"),
    )(proj, proj, proj, proj, pos_col, *_in_hbm(conv_w, conv_b, rwa, rwx, ba, bx, lam))


def _merge_and_head(x, target, y_attn, y_rnn, proj, wap, wrp, wo, mod_row, final_g):
    T = x.shape[0]
    tm = min(T, 256)

    def body(x_ref, t_ref, ya_ref, yr_ref, ma0, ma1, mr0, mr1, wap_ref, wrp_ref, wo_ref, mod_ref, fg_ref,
             dx2_ref, mg_ref, do_ref, dpa_ref, dpr_ref, dya_ref, dyr_ref, dc_ref, dfg_ref, dgate_ref, loss_ref):
        i = pl.program_id(0)
        gate = mod_ref[:, 2 * D_MODEL:3 * D_MODEL]
        fg = fg_ref[...]
        pa, pr = _dot(ya_ref[...], wap_ref[...]), _dot(yr_ref[...], wrp_ref[...])
        sa = _sigmoid(jnp.concatenate([ma0[...], ma1[...]], axis=1))
        sr = _sigmoid(jnp.concatenate([mr0[...], mr1[...]], axis=1))
        mb = (sa * pa + sr * pr).astype(BF16)
        o = _dot(mb, wo_ref[...])
        x2 = x_ref[...] + gate * o
        r2 = _rms(x2)
        xn2 = x2 * r2
        err = xn2 * fg - t_ref[...]
        loss_t = 0.5 * jnp.sum(jnp.sum(err * err, axis=-1, keepdims=True) * (1.0 / D_MODEL), axis=0, keepdims=True)
        dy = err * (1.0 / D_MODEL)
        dfg_t = jnp.sum(dy * xn2, axis=0, keepdims=True)
        dxn = dy * fg
        dx2 = r2 * (dxn - xn2 * jnp.mean(dxn * xn2, axis=-1, keepdims=True))
        dgate_t = jnp.sum(dx2 * o, axis=0, keepdims=True)
        dob = (dx2 * gate).astype(BF16)
        dmerged = _dot_nt(dob, wo_ref[...])
        dpa, dpr = (dmerged * sa).astype(BF16), (dmerged * sr).astype(BF16)
        dya, dyr = _dot_nt(dpa, wap_ref[...]), _dot_nt(dpr, wrp_ref[...])
        dx2_ref[...] = dx2
        mg_ref[...] = mb
        do_ref[...] = dob
        dpa_ref[...] = dpa
        dpr_ref[...] = dpr
        dya_ref[...] = dya
        dyr_ref[...] = dyr
        dc_ref[:, 0:D_MODEL] = (dmerged * pa * sa * (1.0 - sa)).astype(BF16)
        dc_ref[:, D_MODEL:2 * D_MODEL] = (dmerged * pr * sr * (1.0 - sr)).astype(BF16)

        @pl.when(i == 0)
        def _():
            dfg_ref[...] = jnp.zeros_like(dfg_ref)
            dgate_ref[...] = jnp.zeros_like(dgate_ref)
            loss_ref[...] = jnp.zeros_like(loss_ref)

        dfg_ref[...] += dfg_t
        dgate_ref[...] += dgate_t
        loss_ref[...] += jnp.broadcast_to(loss_t, loss_ref.shape)

    tok = lambda w: pl.BlockSpec((tm, w), lambda i: (i, 0))
    blk = lambda cb: pl.BlockSpec((tm, CB), lambda i, cb=cb: (i, cb))
    wfull = pl.BlockSpec((D_MODEL, D_MODEL), lambda i: (0, 0), pipeline_mode=pl.Buffered(1))
    row = lambda w: pl.BlockSpec((1, w), lambda i: (0, 0))
    out_shape = (
        jax.ShapeDtypeStruct((T, D_MODEL), F32), jax.ShapeDtypeStruct((T, D_MODEL), BF16),
        jax.ShapeDtypeStruct((T, D_MODEL), BF16), jax.ShapeDtypeStruct((T, D_MODEL), BF16),
        jax.ShapeDtypeStruct((T, D_MODEL), BF16), jax.ShapeDtypeStruct((T, D_MODEL), F32),
        jax.ShapeDtypeStruct((T, D_MODEL), F32), jax.ShapeDtypeStruct((T, 2 * D_MODEL), BF16),
        jax.ShapeDtypeStruct((1, D_MODEL), F32), jax.ShapeDtypeStruct((1, D_MODEL), F32),
        jax.ShapeDtypeStruct((1, 128), F32),
    )
    return pl.pallas_call(
        body, out_shape=out_shape, grid=(T // tm,), name="merge_and_head",
        in_specs=[tok(D_MODEL), tok(D_MODEL), tok(D_MODEL), tok(D_MODEL), blk(CB_MA), blk(CB_MA + 1), blk(CB_MR),
                  blk(CB_MR + 1), wfull, wfull, wfull, row(ADA_W), row(D_MODEL)],
        out_specs=(tok(D_MODEL),) * 7 + (tok(2 * D_MODEL), row(D_MODEL), row(D_MODEL), row(128)),
        compiler_params=_cp("arbitrary"),
    )(x, target, y_attn, y_rnn, proj, proj, proj, proj, wap, wrp, wo, *_in_hbm(mod_row, final_g))


def _attn_backward(proj, qr_b, kr_b, p_all, d_y, tabs, after):
    T = proj.shape[0]
    nb = T // BLOCK

    def body(qrb_ref, krc_ref, krp_ref, vc_ref, vp_ref, g0_ref, g1_ref, dy_ref, p_ref, cc, sac, sbc, cp_, sap, sbp, after_ref,
             dq_ref, dkv_ref, dg_ref, dsink_ref, carry):
        n = pl.program_id(0)

        @pl.when(n == 0)
        def _():
            carry[...] = jnp.zeros_like(carry)
            dsink_ref[...] = jnp.zeros_like(dsink_ref)

        @pl.when(n < nb)
        def _():
            tc = tcur = (cc[...], sac[...], sbc[...])
            tprev = (cp_[...], sap[...], sbp[...])
            qr, kr_cur, kr_prev = qrb_ref[...], krc_ref[...], krp_ref[...]
            v_cur, v_prev = vc_ref[...], vp_ref[...]
            g = jnp.concatenate([g0_ref[...], g1_ref[...]], axis=1)
            sg = _sigmoid(g)
            dy = dy_ref[...]
            d_o = dy * (g * sg)
            mask = _attn_mask(n)
            lane = lax.broadcasted_iota(jnp.int32, (1, 128), 1)
            rowg = lax.broadcasted_iota(jnp.int32, (GROUP * BLOCK, 1), 0) // BLOCK
            o_parts, dq_parts = [], []
            dk_cols, dv_cols = [None, None], [None, None]
            dsink = jnp.zeros((1, 128), F32)
            heads = range(N_KV)
            k2s = [_kv_pair_operand(kr_prev, kr_cur, kh) for kh in heads]
            v2s = [_kv_pair_operand(v_prev, v_cur, kh) for kh in heads]
            q2s = [_pair_rows(qr, kh).astype(BF16) for kh in heads]
            do2s = [_pair_rows(d_o, kh).astype(BF16) for kh in heads]
            dpns = [_fold(_restack(_dot_nt(do2s[kh], v2s[kh])), mask[0]) for kh in heads]
            pns = [jnp.concatenate([p_ref[:, BLOCK * (GROUP * kh + g):BLOCK * (GROUP * kh + g + 1)] for g in range(GROUP)], axis=0)
                   for kh in heads]
            probs = [(pn, 1.0 - jnp.sum(pn, axis=-1, keepdims=True)) for pn in pns]
            p_bigs = [_unrestack(_unfold(probs[kh][0].astype(BF16), mask[0])) for kh in heads]
            o_bigs = [_dot(p_bigs[kh], v2s[kh]) for kh in heads]
            dv2s = [_dot_tn(p_bigs[kh], do2s[kh]) for kh in heads]
            deltas = [jnp.sum(probs[kh][0] * dpns[kh], axis=-1, keepdims=True) for kh in heads]
            ds_bigs = [_unrestack(_unfold((probs[kh][0] * (dpns[kh] - deltas[kh])).astype(BF16), mask[0])) for kh in heads]
            dq2s = [_dot(ds_bigs[kh], k2s[kh]) for kh in heads]
            dk2s = [_dot_tn(ds_bigs[kh], q2s[kh]) for kh in heads]
            for kh in heads:
                o_parts += [o_bigs[kh][0:BLOCK], o_bigs[kh][BLOCK:2 * BLOCK]]
                dq_parts += [dq2s[kh][0:BLOCK], dq2s[kh][BLOCK:2 * BLOCK]]
                dk_c, dv_c = _fold_pair(dk2s[kh], kh), _fold_pair(dv2s[kh], kh)
                c = kh // 2
                dk_cols[c] = dk_c if dk_cols[c] is None else dk_cols[c] + dk_c
                dv_cols[c] = dv_c if dv_cols[c] is None else dv_cols[c] + dv_c
                ds_rows = probs[kh][1] * deltas[kh]
                for gq in range(GROUP):
                    val = -jnp.sum(jnp.where(rowg == gq, ds_rows, 0.0), axis=0, keepdims=True)
                    dsink = dsink + jnp.where(lane == GROUP * kh + ROW_GROUP_HEAD[gq], val, 0.0)
            o = jnp.concatenate(o_parts, axis=1)
            dg_ref[...] = (dy * o * (sg * (1.0 + g * (1.0 - sg)))).astype(BF16)
            dq_ref[...] = (_unrope(jnp.concatenate(dq_parts, axis=1), *tc) * ATTN_SCALE).astype(BF16)
            dk_all, dv_all = jnp.concatenate(dk_cols, axis=1), jnp.concatenate(dv_cols, axis=1)
            dk_prev = _unrope(dk_all[0:BLOCK], *tprev)
            dk_cur = _unrope(dk_all[BLOCK:2 * BLOCK], *tcur)
            dv_prev, dv_cur = dv_all[0:BLOCK], dv_all[BLOCK:2 * BLOCK]
            dkv_ref[...] = (carry[...] + jnp.concatenate([dk_prev, dv_prev], axis=1)).astype(BF16)
            carry[...] = jnp.concatenate([dk_cur, dv_cur], axis=1)
            dsink_ref[...] += dsink

        @pl.when(n == nb)
        def _():
            dkv_ref[...] = carry[...].astype(BF16)

    cur = lambda w, cb: pl.BlockSpec((BLOCK, w), lambda n, cb=cb: (jnp.minimum(n, nb - 1), cb))
    prev = lambda w, cb: pl.BlockSpec((BLOCK, w), lambda n, cb=cb: (jnp.maximum(jnp.minimum(n, nb - 1) - 1, 0), cb))
    out_shape = (jax.ShapeDtypeStruct((T, D_MODEL), BF16), jax.ShapeDtypeStruct((T, 2 * KV_W), BF16),
                 jax.ShapeDtypeStruct((T, D_MODEL), BF16), jax.ShapeDtypeStruct((1, 128), F32))
    return pl.pallas_call(
        body, out_shape=out_shape, grid=(nb + 1,), name="attn_backward",
        in_specs=[cur(D_MODEL, 0), cur(KV_W, 0), prev(KV_W, 0), cur(KV_W, V_COL_BLOCK), prev(KV_W, V_COL_BLOCK),
                  cur(CB, CB_GA), cur(CB, CB_GA + 1), cur(D_MODEL, 0), cur(N_HEADS * BLOCK, 0),
                  cur(128, 0), cur(128, 0), cur(128, 0), prev(128, 0), prev(128, 0), prev(128, 0),
                  pl.BlockSpec(memory_space=pltpu.SMEM)],
        out_specs=(cur(D_MODEL, 0), pl.BlockSpec((BLOCK, 2 * KV_W), lambda n: (jnp.maximum(n - 1, 0), 0)),
                   cur(D_MODEL, 0), pl.BlockSpec((1, 128), lambda n: (0, 0))),
        scratch_shapes=[pltpu.VMEM((BLOCK, 2 * KV_W), F32)],
        compiler_params=_cp("arbitrary"),
    )(qr_b, kr_b, kr_b, proj, proj, proj, proj, d_y, p_all, *tabs, *tabs, after)


def _rnn_backward(proj, pos_col, h_rnn, saved, d_y, conv_w, rwa, rwx, lam):
    T = proj.shape[0]
    tr = min(T, 256)
    nt = T // tr
    hb = tr // 8

    def body(x0, x1, xh0, xh1, g0, g1, pos_ref, h_ref, hh_ref, xc_ref, r_ref, i_ref, a_ref, mult_ref, dy_ref,
             cw_ref, wa_ref, wx_ref, lam_ref, db_ref, dcw_ref, dcb_ref, dwa_ref, dwx_ref, dba_ref, dbx_ref, dlam_ref,
             xbuf, hbuf, dbuf, gbuf, lbuf, mu_carry, dxc_head):
        step = pl.program_id(0)
        first_tile = step == nt - 1

        @pl.when(step == 0)
        def _():
            mu_carry[...] = jnp.zeros_like(mu_carry)
            dxc_head[...] = jnp.zeros_like(dxc_head)
            for ref in (dcw_ref, dcb_ref, dwa_ref, dwx_ref, dba_ref, dbx_ref, dlam_ref):
                ref[...] = jnp.zeros_like(ref)

        xr = jnp.concatenate([x0[...], x1[...]], axis=1)
        tail = jnp.where(first_tile, 0.0, jnp.concatenate([xh0[...], xh1[...]], axis=1))
        lam_v = lam_ref[...]
        sp = _softplus(-lam_v)
        reset = pos_ref[...] == 0
        cw = cw_ref[...]
        xbuf[0:8, :] = tail
        xbuf[8:tr + 8, :] = xr
        g = jnp.concatenate([g0[...], g1[...]], axis=1)
        sg = _sigmoid(g)
        dy = dy_ref[...]
        h = h_ref[...]
        db_ref[:, D_MODEL:2 * D_MODEL] = (dy * h * (sg * (1.0 + g * (1.0 - sg)))).astype(BF16)
        gbuf[...] = dy * (g * sg)
        top = _scan_backward(a_ref, gbuf, lbuf, mu_carry[0:1, :], tr)
        mu_carry[...] = jnp.broadcast_to(top, mu_carry.shape)
        hbuf[0:8, :] = jnp.where(first_tile, 0.0, hh_ref[...])
        hbuf[8:tr + 8, :] = h
        live = jnp.logical_not(reset)
        dbuf[tr:tr + 8, :] = dxc_head[...]
        for j in range(RNN_BLOCKS):
            sl = slice(RNN_BW * j, RNN_BW * (j + 1))
            lam_t, h_prev = lbuf[:, sl], hbuf[pl.ds(7, tr), sl]
            xc, r, i, a, mult = xc_ref[:, sl], r_ref[:, sl], i_ref[:, sl], a_ref[:, sl], mult_ref[:, sl]
            d_a = jnp.where(live, lam_t * h_prev, 0.0)
            d_mult = jnp.where(live, lam_t * (i * xc), 0.0)
            d_ixc = lam_t * mult
            d_i = d_ixc * xc
            d_log_a = d_a * a - d_mult * (a * a / mult)
            d_za = d_log_a * (-LRU_C * sp[:, sl]) * (r * (1.0 - r))
            d_zx = d_i * (i * (1.0 - i))
            dlam_ref[:, sl] += jnp.sum(d_log_a * r, axis=0, keepdims=True) * (LRU_C * _sigmoid(-lam_v[:, sl]))
            dba_ref[:, sl] += jnp.sum(d_za, axis=0, keepdims=True)
            dbx_ref[:, sl] += jnp.sum(d_zx, axis=0, keepdims=True)
            xcb, dzab, dzxb = xc.astype(BF16), d_za.astype(BF16), d_zx.astype(BF16)
            dwa_ref[j] += _dot_tn(xcb, dzab)
            dwx_ref[j] += _dot_tn(xcb, dzxb)
            d_xc = d_ixc * i + (_dot_nt(dzab, wa_ref[j]) + _dot_nt(dzxb, wx_ref[j]))
            dcb_ref[:, sl] += jnp.sum(d_xc, axis=0, keepdims=True)
            for k in range(CONV_W):
                tap = xr[:, sl] if k == CONV_W - 1 else xbuf[pl.ds(8 - (CONV_W - 1 - k), tr), sl]
                dcw_ref[k:k + 1, sl] += jnp.sum(d_xc * tap, axis=0, keepdims=True)
            dbuf[0:tr, sl] = d_xc
            d_xr = d_xc * cw[CONV_W - 1:CONV_W, sl]
            for k in range(CONV_W - 1):
                d_xr = d_xr + dbuf[pl.ds(CONV_W - 1 - k, tr), sl] * cw[k:k + 1, sl]
            dxc_head[:, sl] = d_xc[0:8, :]
            db_ref[:, sl] = d_xr.astype(BF16)

    rev = lambda s: nt - 1 - s
    blk = lambda cb: pl.BlockSpec((tr, CB), lambda s, cb=cb: (rev(s), cb))
    halo = lambda w, cb: pl.BlockSpec((8, w), lambda s, cb=cb: (jnp.maximum(rev(s) * hb - 1, 0), cb))
    tok = lambda w: pl.BlockSpec((tr, w), lambda s: (rev(s), 0))
    row = lambda w: pl.BlockSpec((1, w), lambda s: (0, 0))
    full3 = pl.BlockSpec((RNN_BLOCKS, RNN_BW, RNN_BW), lambda s: (0, 0, 0))
    cwspec = pl.BlockSpec((CONV_W, D_MODEL), lambda s: (0, 0))
    vec = jax.ShapeDtypeStruct((1, D_MODEL), F32)
    gate_w = jax.ShapeDtypeStruct((RNN_BLOCKS, RNN_BW, RNN_BW), F32)
    out_shape = (jax.ShapeDtypeStruct((T, 2 * D_MODEL), BF16), jax.ShapeDtypeStruct((CONV_W, D_MODEL), F32), vec,
                 gate_w, gate_w, vec, vec, vec)
    big = lambda: pltpu.VMEM((tr, D_MODEL), F32)
    ext = lambda: pltpu.VMEM((tr + 8, D_MODEL), F32)
    return pl.pallas_call(
        body, out_shape=out_shape, grid=(nt,), name="rnn_backward",
        in_specs=[blk(CB_XR), blk(CB_XR + 1), halo(CB, CB_XR), halo(CB, CB_XR + 1), blk(CB_GR), blk(CB_GR + 1),
                  pl.BlockSpec((tr, 1), lambda s: (rev(s), 0)), tok(D_MODEL), halo(D_MODEL, 0)] + [tok(D_MODEL)] * 6
        + [cwspec, full3, full3, row(D_MODEL)],
        out_specs=(tok(2 * D_MODEL), cwspec, row(D_MODEL), full3, full3, row(D_MODEL), row(D_MODEL), row(D_MODEL)),
        scratch_shapes=[ext(), ext(), ext(), big(), big(), pltpu.VMEM((8, D_MODEL), F32), pltpu.VMEM((8, D_MODEL), F32)],
        compiler_params=_cp("arbitrary"),
    )(proj, proj, proj, proj, proj, proj, pos_col, h_rnn, h_rnn, *saved, d_y, *_in_hbm(conv_w, rwa, rwx, lam))


def _input_backward(pieces, w_in, x, dx2, mod_row, norm_g):
    T = x.shape[0]
    tm = min(T, 512)
    n = len(pieces)

    def body(*refs):
        d_refs = refs[:n]
        w_ref, x_ref, dx2_ref, mod_ref, g_ref, gx_ref, dshift_ref, dscale_ref, dg_ref = refs[n:]
        i = pl.program_id(0)
        dh = None
        for d_ref, (_, start, count) in zip(d_refs, pieces):
            part = _dot_nt(d_ref[...], w_ref[:, start * CB:(start + count) * CB])
            dh = part if dh is None else dh + part

        @pl.when(i == 0)
        def _():
            dshift_ref[...] = jnp.zeros_like(dshift_ref)
            dscale_ref[...] = jnp.zeros_like(dscale_ref)
            dg_ref[...] = jnp.zeros_like(dg_ref)

        xf = x_ref[...]
        r1 = _rms(xf)
        xn = xf * r1
        gn = g_ref[...]
        s1 = 1.0 + mod_ref[:, D_MODEL:2 * D_MODEL]
        dshift_ref[...] += jnp.sum(dh, axis=0, keepdims=True)
        dscale_ref[...] += jnp.sum(dh * (xn * gn), axis=0, keepdims=True)
        dg_ref[...] += jnp.sum(dh * s1 * xn, axis=0, keepdims=True)
        dxn = dh * s1 * gn
        gx_ref[...] = dx2_ref[...] + r1 * (dxn - xn * jnp.mean(dxn * xn, axis=-1, keepdims=True))

    tok = lambda w: pl.BlockSpec((tm, w), lambda i: (i, 0))
    row = lambda w: pl.BlockSpec((1, w), lambda i: (0, 0))
    vec = jax.ShapeDtypeStruct((1, D_MODEL), F32)
    return pl.pallas_call(
        body, out_shape=(jax.ShapeDtypeStruct((T, D_MODEL), F32), vec, vec, vec), grid=(T // tm,), name="input_backward",
        in_specs=[tok(c * CB) for _, _, c in pieces]
        + [pl.BlockSpec((D_MODEL, IN_W), lambda i: (0, 0), pipeline_mode=pl.Buffered(1)), tok(D_MODEL), tok(D_MODEL),
           row(ADA_W), row(D_MODEL)],
        out_specs=(tok(D_MODEL), row(D_MODEL), row(D_MODEL), row(D_MODEL)),
        compiler_params=_cp("arbitrary"),
    )(*[p[0] for p in pieces], w_in, x, dx2, *_in_hbm(mod_row, norm_g))


def _weight_grad(a, pieces, tag, a_is_transposed=False):
    M, T = a.shape if a_is_transposed else a.shape[::-1]
    n_blocks = sum(count for _, _, count in pieces)
    n = len(pieces)
    contract = _dot if a_is_transposed else _dot_tn

    def body(*refs):
        a_ref, b_refs, o_ref = refs[0], refs[1:1 + n], refs[-1]
        j = pl.program_id(0)
        for b_ref, (_, start, count) in zip(b_refs, pieces):
            @pl.when((j >= start) & (j < start + count))
            def _(b_ref=b_ref):
                o_ref[...] = contract(a_ref[...], b_ref[...])

    def piece_spec(start, count):
        return pl.BlockSpec((T, CB), lambda j: (0, jnp.clip(j - start, 0, count - 1)))

    return pl.pallas_call(
        body, out_shape=jax.ShapeDtypeStruct((M, n_blocks * CB), F32), grid=(n_blocks,), name=f"weight_grad_{tag}",
        in_specs=[pl.BlockSpec(a.shape, lambda j: (0, 0), pipeline_mode=pl.Buffered(1))] + [piece_spec(s, c) for _, s, c in pieces],
        out_specs=pl.BlockSpec((M, CB), lambda j: (0, j)), compiler_params=_cp("arbitrary"),
    )(a, *[p[0] for p in pieces])


def _adamw(w, g, m, v):
    m = ADAM_B1 * m + (1.0 - ADAM_B1) * g
    v = ADAM_B2 * v + (1.0 - ADAM_B2) * (g * g)
    m_hat = m / (1.0 - ADAM_B1 ** ADAM_STEP)
    v_hat = v / (1.0 - ADAM_B2 ** ADAM_STEP)
    delta = -ADAM_LR * (m_hat / (jnp.sqrt(v_hat) + ADAM_EPS) + ADAM_WD * w)
    return delta, m, v


def _sum_landed(kind, owns, lands, where, tag):
    n = len(owns)
    land = lands[0]
    if kind == "in":
        R, C = land.shape[1:]
        tr = 256
        grid = (R // tr,)
        own_spec = pl.BlockSpec((tr, C), lambda i, w: (i, w[0]))
        land_spec = pl.BlockSpec((3, tr, C), lambda i, w: (0, i, 0))
        out_spec = pl.BlockSpec((1, tr, C), lambda i, w: (w[1], i, 0))
        out_shape = (2, R, C)
        pick = lambda ref: ref[...]
    elif kind == "sq":
        R, C = land.shape[1:]
        grid = (1,)
        own_spec = pl.BlockSpec((1, R, C), lambda i, w: (w[0], 0, 0))
        land_spec = pl.BlockSpec((3, R, C), lambda i, w: (0, 0, 0))
        out_spec = pl.BlockSpec((1, R, C), lambda i, w: (w[1], 0, 0))
        out_shape = (2, R, C)
        pick = lambda ref: ref[0]
    else:
        B, R, C = land.shape[1:]
        grid = (1,)
        own_spec = pl.BlockSpec((B, 1, R, C), lambda i, w: (0, w[0], 0, 0))
        land_spec = pl.BlockSpec((3, B, R, C), lambda i, w: (0, 0, 0, 0))
        out_spec = pl.BlockSpec((B, 1, R, C), lambda i, w: (0, w[1], 0, 0))
        out_shape = (B, 2, R, C)
        pick = lambda ref: ref[:, 0]

    def body(w_ref, *refs):
        for k in range(n):
            own_ref, l_ref, o_ref = refs[k], refs[n + k], refs[2 * n + k]
            total = ((pick(own_ref) + l_ref[0].astype(F32)) + l_ref[1].astype(F32)) + l_ref[2].astype(F32)
            if kind == "rg":
                o_ref[:, 0] = total
            else:
                o_ref[0] = total

    grid_spec = pltpu.PrefetchScalarGridSpec(num_scalar_prefetch=1, grid=grid, in_specs=[own_spec] * n + [land_spec] * n,
                                             out_specs=(out_spec,) * n)
    return list(pl.pallas_call(
        body, out_shape=(jax.ShapeDtypeStruct(out_shape, F32),) * n, grid_spec=grid_spec, name=f"sum_landed_{tag}",
        compiler_params=_cp("parallel"),
    )(where, *owns, *lands))


def _adamw_shard(gs, ws, ms, vs, tag):
    n = len(ws)
    R, C = ws[0].shape
    tr = min(R, 256 if n == 1 else 64)

    def body(*refs):
        for k in range(n):
            g = refs[k][...]
            d, nm, nv = _adamw(refs[n + k][...], g, refs[2 * n + k][...], refs[3 * n + k][...])
            out = refs[4 * n + 4 * k:4 * n + 4 * k + 4]
            out[0][...] = g
            out[1][...] = d
            out[2][...] = nm
            out[3][...] = nv

    spec = pl.BlockSpec((tr, C), lambda i: (i, 0))
    sds = jax.ShapeDtypeStruct((R, C), F32)
    outs = pl.pallas_call(
        body, out_shape=(sds,) * (4 * n), grid=(R // tr,), name=f"adamw_{tag}",
        in_specs=[spec] * (4 * n), out_specs=(spec,) * (4 * n), compiler_params=_cp("parallel"),
    )(*gs, *_in_hbm(*ws, *ms, *vs))
    return [outs[4 * k:4 * k + 4] for k in range(n)]


def _adamw_w_ada(c_t, dmod_cols, w, m, v):
    R, C = w.shape

    def body(ct_ref, dm_ref, w_ref, m_ref, v_ref, g_ref, d_ref, nm_ref, nv_ref):
        g = _dot(ct_ref[...].astype(BF16), dm_ref[...].astype(BF16))
        d, nm, nv = _adamw(w_ref[...], g, m_ref[...], v_ref[...])
        g_ref[...] = g
        d_ref[...] = d
        nm_ref[...] = nm
        nv_ref[...] = nv

    tr = 256
    spec = pl.BlockSpec((tr, C), lambda i: (i, 0))
    sds = jax.ShapeDtypeStruct((R, C), F32)
    return pl.pallas_call(
        body, out_shape=(sds,) * 4, grid=(R // tr,), name="adamw_w_ada",
        in_specs=[pl.BlockSpec((tr, 128), lambda i: (i, 0)), pl.BlockSpec((128, C), lambda i: (0, 0))] + [spec] * 3,
        out_specs=(spec,) * 4, compiler_params=_cp("parallel"),
    )(c_t, dmod_cols, w, m, v)


def _adamw_small(small_all, ws, ms, vs):
    def body(s_ref, w_ref, m_ref, v_ref, g_ref, d_ref, nm_ref, nv_ref):
        g = s_ref[0]
        for b in range(1, N_DEV):
            g = g + s_ref[b]
        d, nm, nv = _adamw(w_ref[...], g, m_ref[...], v_ref[...])
        g_ref[...] = g
        d_ref[...] = d
        nm_ref[...] = nm
        nv_ref[...] = nv

    sds = jax.ShapeDtypeStruct((SMALL_ROWS, D_MODEL), F32)
    return pl.pallas_call(
        body, out_shape=(sds,) * 4, name="adamw_small", in_specs=[VMEM_SPEC] * 4, out_specs=(VMEM_SPEC,) * 4,
        compiler_params=pltpu.CompilerParams(vmem_limit_bytes=VMEM_LIMIT_V7X),
    )(small_all, ws, ms, vs)


ROW_MOD, ROW_NORM_G, ROW_CONV_B, ROW_BA, ROW_BX, ROW_LAM, ROW_FINAL_G, ROW_SINKS, ROW_CONV_W, ROW_LOSS = 0, 3, 4, 5, 6, 7, 8, 9, 10, 14


def _pack_small(b_ada, norm_g, conv_b, ba, bx, lam, final_g, sinks, conv_w_full, loss_row=None):
    lane_pad = lambda a: jnp.pad(a.reshape(1, -1), ((0, 0), (0, D_MODEL - a.size)))
    rows = [b_ada.reshape(3, D_MODEL), norm_g, conv_b, ba, bx, lam, final_g.reshape(1, D_MODEL), lane_pad(sinks), conv_w_full,
            jnp.zeros((1, D_MODEL), F32) if loss_row is None else lane_pad(loss_row),
            jnp.zeros((SMALL_ROWS - ROW_LOSS - 1, D_MODEL), F32)]
    return jnp.concatenate([r.astype(F32) for r in rows], axis=0)


def kernel(x, c, positions, w_ada, b_ada, norm_g, w_in, attn_sinks, conv_w, conv_b, rg_wa, rg_ba, rg_wx, rg_bx, rg_lambda, w_attn_proj, w_rnn_proj, w_out, final_g, loss_target, m_w_ada, m_b_ada, m_norm_g, m_w_in, m_attn_sinks, m_conv_w, m_conv_b, m_rg_wa, m_rg_ba, m_rg_wx, m_rg_bx, m_rg_lambda, m_w_attn_proj, m_w_rnn_proj, m_w_out, m_final_g, v_w_ada, v_b_ada, v_norm_g, v_w_in, v_attn_sinks, v_conv_w, v_conv_b, v_rg_wa, v_rg_ba, v_rg_wx, v_rg_bx, v_rg_lambda, v_w_attn_proj, v_w_rnn_proj, v_w_out, v_final_g):
    T = x.shape[1]
    my_chip = lax.axis_index("x") * 2 + lax.axis_index("y")
    my_dev = my_chip * 2 + lax.axis_index("c")
    x2d, tgt = x[0], loss_target[0]
    pos_col = positions.reshape(T, 1)

    chip_idx = my_chip.reshape(1).astype(jnp.int32)
    c_idx = lax.axis_index("c").reshape(1).astype(jnp.int32)
    sq_place = ((D_MODEL, D_MODEL), (SHARD_ROWS, D_MODEL), lambda chip: (chip, 0))
    rg_place = ((RNN_BLOCKS, RNN_BW, RNN_BW), (RNN_BLOCKS, SHARD_RG, RNN_BW), lambda chip: (0, chip, 0))
    in_place = ((D_MODEL, IN_W), (D_MODEL, SHARD_IN), lambda chip: (0, chip))
    placed = _cast_place([w_in[0], w_attn_proj[0], w_rnn_proj[0], w_out[0], rg_wa[0], rg_wx[0]], chip_idx,
                         [in_place, sq_place, sq_place, sq_place, rg_place, rg_place])
    cw_chips, c_all, mod_chips = _gather_mod(c.reshape(1, 1, D_MODEL), w_ada[0], conv_w[0])
    g_ssems, g_rsems, fulls, g_token = _gather_start([p.reshape(s) for p, s in zip(placed, FULL_SHAPES)], mod_chips)
    conv_w_f = jnp.transpose(cw_chips, (1, 0, 2)).reshape(CONV_W, D_MODEL)
    mod_all = jnp.transpose(mod_chips, (1, 0, 2)).reshape(N_DEV, ADA_W) + b_ada
    mod_row = lax.dynamic_slice_in_dim(mod_all, my_dev, 1, axis=0) + g_token[0:1, 0:1]

    h, h_t, tabs = _prenorm(x2d, mod_row, norm_g, pos_col)
    w_in_v = fulls[0]
    proj = _in_projection(h, w_in_v.reshape(D_MODEL, IN_W), chip_idx, None, "own")
    for k, mask in enumerate(CHIP_MASKS):
        w_in_v = _gather_wait(g_ssems[k], g_rsems[k], [w_in_v], [0], proj, f"w_in_{k}")[0]
        w_in_v = _forward_halves([w_in_v], [(0, 0, k)], f"w_in_{k}")[0]
        from_chip = (chip_idx ^ (mask >> 1)).astype(jnp.int32)
        proj = _in_projection(h, w_in_v.reshape(D_MODEL, IN_W), from_chip, proj, f"from_{k}")
    w_in_f = w_in_v.reshape(D_MODEL, IN_W)
    rest = _gather_wait(g_ssems[3], g_rsems[3], list(fulls[1:]), [1, 2, 3, 4, 5], proj, "rest")
    rest = _forward_halves(rest, [(idx - 1, idx, k) for idx in range(1, N_BIG) for k in range(3)], "rest")
    wap_f, wrp_f, wo_f = (g.reshape(D_MODEL, D_MODEL) for g in rest[0:3])
    rwa_f, rwx_f = (g.reshape(RNN_BLOCKS, RNN_BW, RNN_BW) for g in rest[3:5])
    y_attn, qr_b, kr_b, p_all = _attn_forward(proj, tabs, attn_sinks)
    y_rnn, h_rnn, *rnn_saved = _rnn_forward(proj, pos_col, conv_w_f, conv_b, rwa_f, rwx_f, rg_ba, rg_bx, rg_lambda)
    (dx2, merged, d_o, d_pa, d_pr, d_ya, d_yr, d_c, d_final_g, d_gate, loss_vec) = _merge_and_head(
        x2d, tgt, y_attn, y_rnn, proj, wap_f, wrp_f, wo_f, mod_row, final_g.reshape(1, D_MODEL))

    sq = (N_CHIPS, 2, SHARD_ROWS // 2, D_MODEL)
    rg = (RNN_BLOCKS, N_CHIPS, 2, SHARD_RG // 2, RNN_BW)
    rg_flat = (RNN_BLOCKS * N_CHIPS, 2, SHARD_RG // 2, RNN_BW)

    def chip_sum_and_start(views, axes, flat, unflat, tags_, kinds_, group):
        from_sib = _swap_halves(views, axes)
        exact, rounded = [None] * len(views), [None] * len(views)
        for shape in dict.fromkeys(flat):
            ids = [k for k, f in enumerate(flat) if f == shape]
            ex, ro = _presum([views[k].reshape(shape) for k in ids],
                             [from_sib[k].reshape(shape[:1] + shape[2:]) for k in ids], c_idx, tags_[ids[0]])
            for k, e, r in zip(ids, ex, ro):
                exact[k], rounded[k] = e.reshape(unflat[k]), r.reshape(unflat[k])
        return _exchange_start(rounded, kinds_, group), exact

    g_ap = _weight_grad(y_attn, [(d_pa, 0, 2)], "w_attn_proj")
    g_rp = _weight_grad(y_rnn, [(d_pr, 0, 2)], "w_rnn_proj")
    g_o = _weight_grad(merged, [(d_o, 0, 2)], "w_out")
    sq_half = (N_CHIPS, SHARD_ROWS // 2, D_MODEL)
    started1, own1 = chip_sum_and_start([g_ap.reshape(sq), g_rp.reshape(sq), g_o.reshape(sq)], [1, 1, 1], [sq] * 3, [sq_half] * 3,
                                  ["w_attn_proj", "w_rnn_proj", "w_out"], ["sq"] * 3, "proj")
    d_q, d_kv, d_ga, d_sinks = _attn_backward(proj, qr_b, kr_b, p_all, d_ya, tabs, started1[4][0:1, 0:16])
    d_b, d_conv_w, d_conv_b, d_rwa, d_rwx, d_ba, d_bx, d_lam = _rnn_backward(
        proj, pos_col, h_rnn, rnn_saved, d_yr, conv_w_f, rwa_f, rwx_f, rg_lambda)
    pieces = [(d_q, CB_Q, 2), (d_kv, CB_KV, 1), (d_ga, CB_GA, 2), (d_b, CB_XR, 4), (d_c, CB_MA, 4)]
    g_in = _weight_grad(h_t, pieces, "w_in", a_is_transposed=True)
    started2, own2 = chip_sum_and_start(
        [g_in.reshape(2, D_MODEL // 2, IN_W), d_rwa.reshape(rg), d_rwx.reshape(rg)], [0, 2, 2],
        [(1, 2, D_MODEL // 2, IN_W), rg_flat, rg_flat],
        [(D_MODEL // 2, IN_W), (RNN_BLOCKS, N_CHIPS, SHARD_RG // 2, RNN_BW), (RNN_BLOCKS, N_CHIPS, SHARD_RG // 2, RNN_BW)],
        ["w_in", "rg_wa", "rg_wx"], ["in", "rg", "rg"], "in")
    grad_x, d_shift, d_scale, d_norm_g = _input_backward(pieces, w_in_f, x2d, dx2, mod_row + started2[4][0, 0], norm_g)

    d_mod = jnp.concatenate([d_shift, d_scale, d_gate], axis=1)
    small = _pack_small(d_mod, d_norm_g, d_conv_b, d_ba, d_bx, d_lam, d_final_g, d_sinks[:, :N_HEADS], d_conv_w, loss_vec)
    small_all = _gather_small(small)
    _, lands1 = _exchange_wait(*started1[:4], grad_x, "proj")
    _, lands2 = _exchange_wait(*started2[:4], grad_x, "in")
    tags = ["w_in", "w_attn_proj", "w_rnn_proj", "w_out", "rg_wa", "rg_wx"]
    chip_sums = [own2[0]] + list(own1) + list(own2[1:])
    lands = [lands2[0]] + list(lands1) + list(lands2[1:])
    where = jnp.concatenate([chip_idx, c_idx])
    kinds = ["in", "sq", "sq", "sq", "rg", "rg"]
    groups = [[0], [1, 2, 3], [4, 5]]
    halves = [None] * 6
    for ids in groups:
        for i, half in zip(ids, _sum_landed(kinds[ids[0]], [chip_sums[i] for i in ids], [lands[i] for i in ids], where,
                                            tags[ids[0]])):
            halves[i] = half
    grads = _assemble_with_sibling(halves, [0, 0, 0, 0, 1, 1])
    shapes2d = [(D_MODEL, SHARD_IN), (SHARD_ROWS, D_MODEL), (SHARD_ROWS, D_MODEL), (SHARD_ROWS, D_MODEL),
                (RNN_BLOCKS * SHARD_RG, RNN_BW), (RNN_BLOCKS * SHARD_RG, RNN_BW)]
    big_w = [w_in, w_attn_proj, w_rnn_proj, w_out, rg_wa, rg_wx]
    big_m = [m_w_in, m_w_attn_proj, m_w_rnn_proj, m_w_out, m_rg_wa, m_rg_wx]
    big_v = [v_w_in, v_w_attn_proj, v_w_rnn_proj, v_w_out, v_rg_wa, v_rg_wx]
    res = {}
    for ids in groups:
        flat2d = lambda arrs: [arrs[i].reshape(shapes2d[i]) for i in ids]
        outs = _adamw_shard(flat2d(grads), flat2d(big_w), flat2d(big_m), flat2d(big_v), tags[ids[0]])
        for i, four in zip(ids, outs):
            res[tags[i]] = [o.reshape(big_w[i].shape) for o in four]

    dmod_all = small_all[:, ROW_MOD:ROW_MOD + 3, :].reshape(N_DEV, ADA_W)
    dmod_cols = lax.dynamic_slice_in_dim(dmod_all, my_chip * SHARD_ADA, SHARD_ADA, axis=1)
    c_t = jnp.pad(jnp.transpose(c_all.reshape(N_DEV, D_MODEL)), ((0, 0), (0, 128 - N_DEV)))
    dmod_cols = jnp.pad(dmod_cols, ((0, 128 - N_DEV), (0, 0)))
    res["w_ada"] = [o.reshape(w_ada.shape) for o in _adamw_w_ada(c_t, dmod_cols, w_ada[0], m_w_ada[0], v_w_ada[0])]

    def full_conv(a):
        return lax.dynamic_update_slice_in_dim(jnp.zeros((CONV_W, D_MODEL), F32), a[0], my_chip * (D_MODEL // N_CHIPS), axis=1)

    packed = [_pack_small(p[0], p[1], p[2], p[3], p[4], p[5], p[6], p[7], full_conv(p[8])) for p in (
        (b_ada, norm_g, conv_b, rg_ba, rg_bx, rg_lambda, final_g, attn_sinks, conv_w),
        (m_b_ada, m_norm_g, m_conv_b, m_rg_ba, m_rg_bx, m_rg_lambda, m_final_g, m_attn_sinks, m_conv_w),
        (v_b_ada, v_norm_g, v_conv_b, v_rg_ba, v_rg_bx, v_rg_lambda, v_final_g, v_attn_sinks, v_conv_w))]
    small_out = _adamw_small(small_all, *packed)

    def unpack(slab):
        cw = lax.dynamic_slice_in_dim(slab[ROW_CONV_W:ROW_CONV_W + CONV_W], my_chip * (D_MODEL // N_CHIPS),
                                      D_MODEL // N_CHIPS, axis=1)
        return {
            "b_ada": slab[ROW_MOD:ROW_MOD + 3].reshape(1, ADA_W), "norm_g": slab[ROW_NORM_G:ROW_NORM_G + 1],
            "conv_b": slab[ROW_CONV_B:ROW_CONV_B + 1], "rg_ba": slab[ROW_BA:ROW_BA + 1], "rg_bx": slab[ROW_BX:ROW_BX + 1],
            "rg_lambda": slab[ROW_LAM:ROW_LAM + 1], "final_g": slab[ROW_FINAL_G], "attn_sinks": slab[ROW_SINKS:ROW_SINKS + 1, :N_HEADS],
            "conv_w": cw[None],
        }

    small_res = [unpack(s) for s in small_out]
    order = ["w_ada", "b_ada", "norm_g", "w_in", "attn_sinks", "conv_w", "conv_b", "rg_wa", "rg_ba", "rg_wx", "rg_bx",
             "rg_lambda", "w_attn_proj", "w_rnn_proj", "w_out", "final_g"]
    loss = small_out[0][ROW_LOSS, 0]
    outs = [loss, grad_x[None]]
    for kind in range(4):
        for name in order:
            outs.append(res[name][kind] if name in res else small_res[kind][name])
    return tuple(outs)
```

```python
import numpy as np
import jax
import jax.numpy as jnp
from jax import lax
from jax.experimental import pallas as pl
from jax.experimental.pallas import tpu as pltpu

F32 = jnp.float32
BF16 = jnp.bfloat16

D_MODEL = 1024
N_HEADS = 16
N_KV = 4
HEAD_DIM = 64
GROUP = N_HEADS // N_KV
BLOCK = 128
KV_W = N_KV * HEAD_DIM
ROT_HALF = 8
ROPE_THETA = 500000.0
ATTN_SCALE = 0.125
RNN_BLOCKS = 4
RNN_BW = 256
CONV_W = 4
LRU_C = 8.0
NORM_EPS = 1e-6
IN_W = 6656
CB = 512
N_CB = IN_W // CB
CB_Q, CB_KV, CB_GA, CB_XR, CB_GR, CB_MA, CB_MR = 0, 2, 3, 5, 7, 9, 11
V_COL_BLOCK = 5
N_CHIPS = 4
N_DEV = 8
SHARD_IN = IN_W // N_CHIPS
SHARD_ROWS = D_MODEL // N_CHIPS
SHARD_RG = RNN_BW // N_CHIPS
ADA_W = 3 * D_MODEL
SHARD_ADA = ADA_W // N_CHIPS
SMALL_ROWS = 16

ADAM_LR = 0.001
ADAM_B1 = 0.9
ADAM_B2 = 0.999
ADAM_EPS = 1e-08
ADAM_WD = 0.01
ADAM_STEP = 10

VMEM_LIMIT_V7X = 52 * 1024 * 1024
MESH = pl.DeviceIdType.MESH
ANY = pl.BlockSpec(memory_space=pl.ANY)
VMEM_SPEC = pl.BlockSpec(memory_space=pltpu.VMEM)


def _in_hbm(*arrays):
    return [pltpu.with_memory_space_constraint(a, pltpu.HBM) for a in arrays]


def _cp(*sem):
    return pltpu.CompilerParams(dimension_semantics=sem if sem else None, vmem_limit_bytes=VMEM_LIMIT_V7X)


def _dot(a, b):
    return jnp.dot(a, b, preferred_element_type=F32)


def _dot_nt(a, b):
    return lax.dot_general(a, b, (((1,), (1,)), ((), ())), preferred_element_type=F32)


def _dot_tn(a, b):
    return lax.dot_general(a, b, (((0,), (0,)), ((), ())), preferred_element_type=F32)


def _sigmoid(z):
    return 1.0 / (1.0 + jnp.exp(-z))


def _softplus(z):
    u = jnp.exp(-jnp.abs(z))
    log1p_u = jnp.where(u < 1e-3, u * (1.0 - u * (0.5 - u * (1.0 / 3.0))), jnp.log(1.0 + u))
    return jnp.maximum(z, 0.0) + log1p_u


def _rms(xf):
    return lax.rsqrt(jnp.mean(xf * xf, axis=-1, keepdims=True) + NORM_EPS)


def _me():
    return lax.axis_index("x"), lax.axis_index("y"), lax.axis_index("c")


def _peer(mask):
    x, y, c = _me()
    fx, fy, fc = (mask >> 2) & 1, (mask >> 1) & 1, mask & 1
    return (x ^ fx if fx else x, y ^ fy if fy else y, c ^ fc if fc else c)


def _chip_of(pos):
    return pos[0] * 2 + pos[1]


SIBLING_COLLECTIVE_ID = 0
SIBLING_ONLY = pltpu.CompilerParams(collective_id=SIBLING_COLLECTIVE_ID)


def _sibling_handshake():
    barrier = pltpu.get_barrier_semaphore()
    pl.semaphore_signal(barrier, inc=1, device_id=_peer(1), device_id_type=MESH)
    pl.semaphore_wait(barrier, 1)


CHIP_MASKS = (4, 2, 6)
ALL_MASKS = (1, 2, 3, 4, 5, 6, 7)


HBM_SPEC = pl.BlockSpec(memory_space=pltpu.HBM)
SEM_SPEC = pl.BlockSpec(memory_space=pltpu.SEMAPHORE)
SPLIT_COPY = pltpu.CompilerParams(has_side_effects=pltpu.SideEffectType.DATAFLOW_SIDE_EFFECTING)
N_BIG = 6
FULL_SHAPES = (
    (2, D_MODEL // 2, IN_W),
    (N_CHIPS, 2, SHARD_ROWS // 2, D_MODEL), (N_CHIPS, 2, SHARD_ROWS // 2, D_MODEL), (N_CHIPS, 2, SHARD_ROWS // 2, D_MODEL),
    (RNN_BLOCKS, N_CHIPS, 2, SHARD_RG // 2, RNN_BW), (RNN_BLOCKS, N_CHIPS, 2, SHARD_RG // 2, RNN_BW),
)


def _slot(full, idx, chip, half):
    if idx == 0:
        return full.at[half, :, pl.ds(pl.multiple_of(chip * SHARD_IN, 128), SHARD_IN)]
    return full.at[chip, half] if idx in (1, 2, 3) else full.at[:, chip, half]


def _three_halves(full, idx):
    return full.at[pl.ds(0, 3), 0] if idx in (1, 2, 3) else full.at[:, pl.ds(0, 3), 0]


def _gather_start(fulls, after):
    def body(*refs):
        full_refs = refs[:N_BIG]
        ssems, rsems = refs[N_BIG + 1:N_BIG + 5], refs[N_BIG + 5:N_BIG + 9]
        token = refs[2 * N_BIG + 9]
        me = _me()
        my_chip = _chip_of(me)
        for idx in range(N_BIG):
            for k, mask in enumerate(CHIP_MASKS):
                pair = k if idx == 0 else 3
                mine = _slot(full_refs[idx], idx, my_chip, me[2])
                pltpu.make_async_remote_copy(src_ref=mine, dst_ref=mine, send_sem=ssems[pair], recv_sem=rsems[pair],
                                             device_id=_peer(mask), device_id_type=MESH).start()
        token[...] = jnp.zeros_like(token)

    sem = pltpu.SemaphoreType.DMA(())
    out_shape = (sem,) * 8 + tuple(pltpu.HBM(f.shape, f.dtype) for f in fulls) + (jax.ShapeDtypeStruct((8, 128), F32),)
    outs = pl.pallas_call(
        body, out_shape=out_shape, name="gather_start",
        in_specs=[HBM_SPEC] * N_BIG + [ANY], out_specs=tuple([SEM_SPEC] * 8 + [HBM_SPEC] * N_BIG + [VMEM_SPEC]),
        input_output_aliases={i: 8 + i for i in range(N_BIG)}, compiler_params=SPLIT_COPY,
    )(*[pltpu.with_memory_space_constraint(f, pltpu.HBM) for f in fulls], after)
    return outs[0:4], outs[4:8], outs[8:8 + N_BIG], outs[8 + N_BIG]


def _gather_wait(ssem, rsem, arrays, idxs, after, tag):
    n = len(arrays)

    def body(*refs):
        full_refs, ssem_ref, rsem_ref = refs[:n], refs[n], refs[n + 1]
        me = _me()
        for full, idx in zip(full_refs, idxs):
            region = _slot(full, 0, _chip_of(me), me[2]) if idx == 0 else _three_halves(full, idx)
            arrived = pltpu.make_async_remote_copy(
                src_ref=region, dst_ref=region, send_sem=ssem_ref, recv_sem=rsem_ref, device_id=me, device_id_type=MESH)
            arrived.wait_send()
            arrived.wait_recv()

    outs = pl.pallas_call(
        body, out_shape=tuple(pltpu.HBM(a.shape, a.dtype) for a in arrays), name=f"gather_wait_{tag}",
        in_specs=[HBM_SPEC] * n + [SEM_SPEC, SEM_SPEC, ANY], out_specs=tuple([HBM_SPEC] * n),
        input_output_aliases={i: i for i in range(n)}, compiler_params=SPLIT_COPY,
    )(*arrays, ssem, rsem, after)
    return list(outs)


def _forward_halves(arrays, items, tag):
    n, m = len(arrays), len(items)

    def body(*refs):
        outs, ssem, rsem = refs[n:2 * n], refs[2 * n], refs[2 * n + 1]
        me = _me()
        sib = _peer(1)
        _sibling_handshake()
        cps = []
        for j, (pos, idx, k) in enumerate(items):
            chip = _chip_of(_peer(CHIP_MASKS[k]))
            cp = pltpu.make_async_remote_copy(
                src_ref=_slot(outs[pos], idx, chip, me[2]), dst_ref=_slot(outs[pos], idx, chip, me[2]),
                send_sem=ssem.at[j], recv_sem=rsem.at[j], device_id=sib, device_id_type=MESH)
            cp.start()
            cps.append(cp)
        for j, (pos, idx, k) in enumerate(items):
            chip = _chip_of(_peer(CHIP_MASKS[k]))
            pltpu.make_async_remote_copy(
                src_ref=_slot(outs[pos], idx, chip, me[2]), dst_ref=_slot(outs[pos], idx, chip, 1 - me[2]),
                send_sem=ssem.at[j], recv_sem=rsem.at[j], device_id=sib, device_id_type=MESH).wait_recv()
        for cp in cps:
            cp.wait_send()

    outs = pl.pallas_call(
        body, out_shape=tuple(jax.ShapeDtypeStruct(a.shape, a.dtype) for a in arrays), name=f"forward_halves_{tag}",
        in_specs=[ANY] * n, out_specs=tuple([ANY] * n), input_output_aliases={i: i for i in range(n)},
        scratch_shapes=[pltpu.SemaphoreType.DMA((m,)), pltpu.SemaphoreType.DMA((m,))], compiler_params=SIBLING_ONLY,
    )(*arrays)
    return list(outs)


def _gather_mod(c_row, w_ada_s, conv_w_s):
    def body(c_ref, wada_ref, cw_s, cw_f, call_ref, mod_ref, wsend, wrecv, lsem, csend, crecv, msend, mrecv):
        me = _me()
        my_chip = _chip_of(me)
        my_dev = my_chip * 2 + me[2]
        sends = []
        for k, mask in enumerate(CHIP_MASKS):
            cp = pltpu.make_async_remote_copy(src_ref=cw_s, dst_ref=cw_f.at[my_chip], send_sem=wsend.at[k], recv_sem=wrecv.at[k],
                                              device_id=_peer(mask), device_id_type=MESH)
            cp.start()
            sends.append(cp)
        local = [pltpu.make_async_copy(cw_s, cw_f.at[my_chip], lsem.at[0])]
        for cp in local:
            cp.start()

        call_ref[my_dev] = c_ref[0]
        csends = []
        for k, mask in enumerate(ALL_MASKS):
            cp = pltpu.make_async_remote_copy(
                src_ref=c_ref.at[0], dst_ref=call_ref.at[my_dev],
                send_sem=csend.at[k], recv_sem=crecv.at[k], device_id=_peer(mask), device_id_type=MESH)
            cp.start()
            csends.append(cp)
        for k, mask in enumerate(ALL_MASKS):
            frm = _peer(mask)
            pltpu.make_async_remote_copy(
                src_ref=c_ref.at[0], dst_ref=call_ref.at[_chip_of(frm) * 2 + frm[2]],
                send_sem=csend.at[k], recv_sem=crecv.at[k], device_id=frm, device_id_type=MESH).wait_recv()
        for cp in csends:
            cp.wait_send()

        c_all = call_ref[...].reshape(N_DEV, D_MODEL).astype(BF16)
        mod_ref[my_chip] = _dot(c_all, wada_ref[...].astype(BF16))
        msends = []
        for k, mask in enumerate(CHIP_MASKS):
            cp = pltpu.make_async_remote_copy(
                src_ref=mod_ref.at[my_chip], dst_ref=mod_ref.at[my_chip],
                send_sem=msend.at[k], recv_sem=mrecv.at[k], device_id=_peer(mask), device_id_type=MESH)
            cp.start()
            msends.append(cp)
        for k, mask in enumerate(CHIP_MASKS):
            frm = _peer(mask)
            pltpu.make_async_remote_copy(
                src_ref=mod_ref.at[my_chip], dst_ref=mod_ref.at[_chip_of(frm)],
                send_sem=msend.at[k], recv_sem=mrecv.at[k], device_id=frm, device_id_type=MESH).wait_recv()
        for cp in msends:
            cp.wait_send()

        for k, mask in enumerate(CHIP_MASKS):
            frm = _peer(mask)
            pltpu.make_async_remote_copy(src_ref=cw_s, dst_ref=cw_f.at[_chip_of(frm)], send_sem=wsend.at[k], recv_sem=wrecv.at[k],
                                         device_id=frm, device_id_type=MESH).wait_recv()
        for cp in sends:
            cp.wait_send()
        for cp in local:
            cp.wait()

    out_shape = (
        jax.ShapeDtypeStruct((N_CHIPS, CONV_W, D_MODEL // N_CHIPS), F32),
        jax.ShapeDtypeStruct((N_DEV, 1, D_MODEL), F32),
        jax.ShapeDtypeStruct((N_CHIPS, N_DEV, SHARD_ADA), F32),
    )
    return pl.pallas_call(
        body, out_shape=out_shape, name="gather_mod",
        in_specs=[VMEM_SPEC, VMEM_SPEC, ANY], out_specs=(ANY, VMEM_SPEC, VMEM_SPEC),
        scratch_shapes=[
            pltpu.SemaphoreType.DMA((3,)), pltpu.SemaphoreType.DMA((3,)), pltpu.SemaphoreType.DMA((1,)),
            pltpu.SemaphoreType.DMA((7,)), pltpu.SemaphoreType.DMA((7,)),
            pltpu.SemaphoreType.DMA((3,)), pltpu.SemaphoreType.DMA((3,)),
        ],
        compiler_params=pltpu.CompilerParams(vmem_limit_bytes=VMEM_LIMIT_V7X),
    )(c_row, w_ada_s, conv_w_s)


def _cast_place(shards, chip_idx, places):
    n = len(shards)

    def body(chip_ref, *refs):
        for s_ref, o_ref in zip(refs[:n], refs[n:]):
            o_ref[...] = s_ref[...].astype(BF16)

    grid_spec = pltpu.PrefetchScalarGridSpec(
        num_scalar_prefetch=1, grid=(1,),
        in_specs=[pl.BlockSpec(s.shape, lambda i, chip_ref, nd=s.ndim: (0,) * nd) for s in shards],
        out_specs=tuple(pl.BlockSpec(block, lambda i, chip_ref, im=im: im(chip_ref[0])) for _, block, im in places))
    return pl.pallas_call(
        body, out_shape=tuple(jax.ShapeDtypeStruct(full, BF16) for full, _, _ in places), grid_spec=grid_spec,
        name="cast_place", compiler_params=_cp("arbitrary"),
    )(chip_idx, *_in_hbm(*shards))


def _shard_of(ref, kind, chip):
    if kind == "in":
        return ref.at[:, pl.ds(pl.multiple_of(chip * SHARD_IN, 128), SHARD_IN)]
    return ref.at[chip] if kind == "sq" else ref.at[:, chip]


def _land_shape(src, kind):
    if kind == "in":
        return (3, src.shape[0], SHARD_IN)
    return (3,) + src.shape[1:] if kind == "sq" else (3, src.shape[0]) + src.shape[2:]


def _exchange_start(srcs, kinds, tag):
    n = len(srcs)
    lands = [pltpu.with_memory_space_constraint(lax.empty(_land_shape(s, k), s.dtype), pltpu.HBM) for s, k in zip(srcs, kinds)]

    def body(*refs):
        src_refs, land_refs = refs[:n], refs[n:2 * n]
        ssems, rsems = refs[2 * n:3 * n], refs[3 * n:4 * n]
        token = refs[6 * n]
        for i in range(n):
            for k, mask in enumerate(CHIP_MASKS):
                to = _peer(mask)
                pltpu.make_async_remote_copy(
                    src_ref=_shard_of(src_refs[i], kinds[i], _chip_of(to)), dst_ref=land_refs[i].at[k],
                    send_sem=ssems[i], recv_sem=rsems[i], device_id=to, device_id_type=MESH).start()
        token[...] = jnp.zeros_like(token)

    sem = pltpu.SemaphoreType.DMA(())
    out_shape = ((sem,) * (2 * n) + tuple(pltpu.HBM(s.shape, s.dtype) for s in srcs)
                 + tuple(pltpu.HBM(l.shape, l.dtype) for l in lands) + (jax.ShapeDtypeStruct((8, 128), F32),))
    outs = pl.pallas_call(
        body, out_shape=out_shape, name=f"exchange_start_{tag}",
        in_specs=[HBM_SPEC] * (2 * n), out_specs=tuple([SEM_SPEC] * (2 * n) + [HBM_SPEC] * (2 * n) + [VMEM_SPEC]),
        input_output_aliases={i: 2 * n + i for i in range(2 * n)},
        compiler_params=pltpu.CompilerParams(has_side_effects=pltpu.SideEffectType.DATAFLOW_SIDE_EFFECTING),
    )(*[pltpu.with_memory_space_constraint(s, pltpu.HBM) for s in srcs], *lands)
    return outs[:n], outs[n:2 * n], outs[2 * n:3 * n], outs[3 * n:4 * n], outs[4 * n]


def _exchange_wait(ssems, rsems, srcs, lands, after, tag):
    n = len(srcs)

    def body(*refs):
        land_refs = refs[n:2 * n]
        ssem_refs, rsem_refs = refs[2 * n:3 * n], refs[3 * n:4 * n]
        for i in range(n):
            all_three = pltpu.make_async_remote_copy(
                src_ref=land_refs[i], dst_ref=land_refs[i], send_sem=ssem_refs[i], recv_sem=rsem_refs[i],
                device_id=_me(), device_id_type=MESH)
            all_three.wait_send()
            all_three.wait_recv()

    outs = pl.pallas_call(
        body, out_shape=tuple(pltpu.HBM(a.shape, a.dtype) for a in list(srcs) + list(lands)), name=f"exchange_wait_{tag}",
        in_specs=[HBM_SPEC] * (2 * n) + [SEM_SPEC] * (2 * n) + [ANY], out_specs=tuple([HBM_SPEC] * (2 * n)),
        input_output_aliases={i: i for i in range(2 * n)},
        compiler_params=pltpu.CompilerParams(has_side_effects=pltpu.SideEffectType.DATAFLOW_SIDE_EFFECTING),
    )(*srcs, *lands, *ssems, *rsems, after)
    return outs[:n], outs[n:]


def _gather_small(small):
    def body(small_ref, small_all, ssend, srecv):
        me = _me()
        my_dev = _chip_of(me) * 2 + me[2]
        small_all[my_dev] = small_ref[...]
        ssends = []
        for k, mask in enumerate(ALL_MASKS):
            cp = pltpu.make_async_remote_copy(
                src_ref=small_ref, dst_ref=small_all.at[my_dev],
                send_sem=ssend.at[k], recv_sem=srecv.at[k], device_id=_peer(mask), device_id_type=MESH)
            cp.start()
            ssends.append(cp)
        for k, mask in enumerate(ALL_MASKS):
            frm = _peer(mask)
            pltpu.make_async_remote_copy(
                src_ref=small_ref, dst_ref=small_all.at[_chip_of(frm) * 2 + frm[2]],
                send_sem=ssend.at[k], recv_sem=srecv.at[k], device_id=frm, device_id_type=MESH).wait_recv()
        for cp in ssends:
            cp.wait_send()

    return pl.pallas_call(
        body, out_shape=jax.ShapeDtypeStruct((N_DEV, SMALL_ROWS, D_MODEL), F32), name="gather_small",
        in_specs=[VMEM_SPEC], out_specs=VMEM_SPEC,
        scratch_shapes=[pltpu.SemaphoreType.DMA((7,)), pltpu.SemaphoreType.DMA((7,))],
    )(small)


def _half_of(ref, axis, half):
    return ref.at[(slice(None),) * axis + (half,)]


def _swap_halves(parts, axes):
    n = len(parts)

    def body(*refs):
        ins, outs, ssem, rsem = refs[:n], refs[n:2 * n], refs[2 * n], refs[2 * n + 1]
        c = lax.axis_index("c")
        _sibling_handshake()
        cps = [pltpu.make_async_remote_copy(src_ref=_half_of(ins[i], axes[i], 1 - c), dst_ref=outs[i], send_sem=ssem.at[i],
                                            recv_sem=rsem.at[i], device_id=_peer(1), device_id_type=MESH) for i in range(n)]
        for cp in cps:
            cp.start()
        for cp in cps:
            cp.wait()

    shapes = [p.shape[:a] + p.shape[a + 1:] for p, a in zip(parts, axes)]
    return pl.pallas_call(
        body, out_shape=tuple(jax.ShapeDtypeStruct(s, p.dtype) for s, p in zip(shapes, parts)), name="swap_halves",
        in_specs=[ANY] * n, out_specs=tuple([ANY] * n),
        scratch_shapes=[pltpu.SemaphoreType.DMA((n,)), pltpu.SemaphoreType.DMA((n,))], compiler_params=SIBLING_ONLY,
    )(*parts)


def _presum(mines, sibs, c_idx, tag):
    n = len(mines)
    S, _, R, C = mines[0].shape
    tr = min(R, 256)
    tc = SHARD_IN if C % SHARD_IN == 0 else (C // 2 if n > 1 and C % 256 == 0 else C)

    def body(c_ref, *refs):
        for k in range(n):
            total = refs[k][:, 0] + refs[n + k][...]
            refs[2 * n + k][...] = total
            refs[3 * n + k][...] = total.astype(BF16)

    out_spec = pl.BlockSpec((S, tr, tc), lambda i, j, c_ref: (0, i, j))
    grid_spec = pltpu.PrefetchScalarGridSpec(
        num_scalar_prefetch=1, grid=(R // tr, C // tc),
        in_specs=[pl.BlockSpec((S, 1, tr, tc), lambda i, j, c_ref: (0, c_ref[0], i, j))] * n + [out_spec] * n,
        out_specs=(out_spec,) * (2 * n))
    outs = pl.pallas_call(
        body, out_shape=(jax.ShapeDtypeStruct((S, R, C), F32),) * n + (jax.ShapeDtypeStruct((S, R, C), BF16),) * n,
        grid_spec=grid_spec, name=f"presum_{tag}", compiler_params=_cp("parallel", "parallel"),
    )(c_idx, *mines, *sibs)
    return list(outs[:n]), list(outs[n:])


def _assemble_with_sibling(parts, axes):
    n = len(parts)

    def body(*refs):
        outs, ssem, rsem = refs[n:2 * n], refs[2 * n], refs[2 * n + 1]
        c = lax.axis_index("c")
        _sibling_handshake()
        cps = [pltpu.make_async_remote_copy(
            src_ref=_half_of(outs[i], axes[i], c), dst_ref=_half_of(outs[i], axes[i], c), send_sem=ssem.at[i],
            recv_sem=rsem.at[i], device_id=_peer(1), device_id_type=MESH) for i in range(n)]
        for cp in cps:
            cp.start()
        for i in range(n):
            pltpu.make_async_remote_copy(
                src_ref=_half_of(outs[i], axes[i], c), dst_ref=_half_of(outs[i], axes[i], 1 - c), send_sem=ssem.at[i],
                recv_sem=rsem.at[i], device_id=_peer(1), device_id_type=MESH).wait_recv()
        for cp in cps:
            cp.wait_send()

    return pl.pallas_call(
        body, out_shape=tuple(jax.ShapeDtypeStruct(p.shape, p.dtype) for p in parts), name="assemble_with_sibling",
        in_specs=[ANY] * n, out_specs=tuple([ANY] * n), input_output_aliases={i: i for i in range(n)},
        scratch_shapes=[pltpu.SemaphoreType.DMA((n,)), pltpu.SemaphoreType.DMA((n,))], compiler_params=SIBLING_ONLY,
    )(*parts)


def _rope_lane_frequencies():
    inv = np.float32(ROPE_THETA) ** (-(np.arange(0, 2 * ROT_HALF, 2, dtype=np.float32)) / np.float32(2 * ROT_HALF))
    lane = np.arange(128) % HEAD_DIM
    return jnp.asarray(np.where(lane < 2 * ROT_HALF, inv[lane % ROT_HALF], 0.0).astype(np.float32)[None, :])


def _rope_tables(pos, freq):
    ang = pos.astype(F32) * freq
    c, s = jnp.cos(ang), jnp.sin(ang)
    m = lax.broadcasted_iota(jnp.int32, ang.shape, 1) & (HEAD_DIM - 1)
    return (jnp.where(m < 2 * ROT_HALF, c, 1.0), jnp.where(m < ROT_HALF, -s, 0.0),
            jnp.where((m >= ROT_HALF) & (m < 2 * ROT_HALF), s, 0.0))


def _columns(t):
    return [t[:, i:i + 128] for i in range(0, t.shape[-1], 128)]


def _rope(t, c, sa, sb):
    return jnp.concatenate(
        [x * c + pltpu.roll(x, 128 - ROT_HALF, 1) * sa + pltpu.roll(x, ROT_HALF, 1) * sb for x in _columns(t)], axis=1)


def _unrope(d, c, sa, sb):
    return jnp.concatenate(
        [x * c + pltpu.roll(x * sa, ROT_HALF, 1) + pltpu.roll(x * sb, 128 - ROT_HALF, 1) for x in _columns(d)], axis=1)


def _prenorm(x, mod_row, norm_g, pos_col):
    T = x.shape[0]
    tm = min(T, 512)

    def body(x_ref, mod_ref, g_ref, pos_ref, f_ref, h_ref, ht_ref, c_ref, sa_ref, sb_ref):
        xf = x_ref[...]
        shift, scale = mod_ref[:, 0:D_MODEL], mod_ref[:, D_MODEL:2 * D_MODEL]
        h = (xf * _rms(xf)) * g_ref[...] * (1.0 + scale) + shift
        h_ref[...] = h.astype(BF16)
        ht_ref[...] = h.T.astype(BF16)
        c_ref[...], sa_ref[...], sb_ref[...] = _rope_tables(pos_ref[...], f_ref[...])

    tab = jax.ShapeDtypeStruct((T, 128), F32)
    tok = lambda w: pl.BlockSpec((tm, w), lambda i: (i, 0))
    row = lambda w: pl.BlockSpec((1, w), lambda i: (0, 0))
    outs = pl.pallas_call(
        body, out_shape=(jax.ShapeDtypeStruct((T, D_MODEL), BF16), jax.ShapeDtypeStruct((D_MODEL, T), BF16), tab, tab, tab),
        grid=(T // tm,), name="prenorm",
        in_specs=[tok(D_MODEL), row(ADA_W), row(D_MODEL), tok(1), row(128)],
        out_specs=(tok(D_MODEL), pl.BlockSpec((D_MODEL, tm), lambda i: (0, i)), tok(128), tok(128), tok(128)),
        compiler_params=_cp("parallel"),
    )(x, *_in_hbm(mod_row, norm_g), pos_col, _rope_lane_frequencies())
    return outs[0], outs[1], tuple(outs[2:])


def _in_projection(h, w_in, chips, into, tag):
    T = h.shape[0]
    tm, tn = min(T, 512), SHARD_IN
    k = chips.shape[0]

    def body(chip_ref, h_ref, w_ref, *rest):
        rest[-1][...] = _dot(h_ref[...], w_ref[...])

    w_spec = pl.BlockSpec((D_MODEL, tn), lambda s, i, c: (0, c[s]), **({"pipeline_mode": pl.Buffered(1)} if k == 1 else {}))
    in_specs = [pl.BlockSpec((tm, D_MODEL), lambda s, i, c: (i, 0)), w_spec]
    args = [chips, h, w_in]
    aliases = {}
    if into is not None:
        in_specs.append(ANY)
        args.append(into)
        aliases = {3: 0}
    grid_spec = pltpu.PrefetchScalarGridSpec(num_scalar_prefetch=1, grid=(k, T // tm), in_specs=in_specs,
                                             out_specs=pl.BlockSpec((tm, tn), lambda s, i, c: (i, c[s])))
    return pl.pallas_call(
        body, out_shape=jax.ShapeDtypeStruct((T, IN_W), F32), grid_spec=grid_spec, name=f"in_projection_{tag}",
        input_output_aliases=aliases, compiler_params=_cp("parallel", "parallel"),
    )(*args)


def _attn_mask(n):
    qi = lax.broadcasted_iota(jnp.int32, (GROUP * BLOCK, BLOCK), 0) & (BLOCK - 1)
    j = lax.broadcasted_iota(jnp.int32, (GROUP * BLOCK, BLOCK), 1)
    own = j <= qi
    return own, jnp.logical_not(own) & (n == 0)


def _fold(x, own):
    return jnp.where(own, x[:, BLOCK:2 * BLOCK], x[:, 0:BLOCK])


def _unfold(xf, own):
    zero = jnp.zeros_like(xf)
    return jnp.concatenate([jnp.where(own, zero, xf), jnp.where(own, xf, zero)], axis=1)


ROW_GROUP_HEAD = (0, 2, 1, 3)


def _sink_col(sink_ref, kh):
    rowg = lax.broadcasted_iota(jnp.int32, (GROUP * BLOCK, 1), 0) // BLOCK
    col = jnp.full((GROUP * BLOCK, 1), sink_ref[0, GROUP * kh + ROW_GROUP_HEAD[0]], F32)
    for g in range(1, GROUP):
        col = jnp.where(rowg == g, sink_ref[0, GROUP * kh + ROW_GROUP_HEAD[g]], col)
    return col


def _low_lanes(shape):
    return lax.broadcasted_iota(jnp.int32, shape, 1) < HEAD_DIM


def _kv_pair_operand(prev, cur, kh):
    c = 128 * (kh // 2)
    col = jnp.concatenate([prev[:, c:c + 128], cur[:, c:c + 128]], axis=0).astype(F32)
    if kh % 2 == 0:
        lo = jnp.where(_low_lanes(col.shape), col, 0.0)
        hi = pltpu.roll(lo, HEAD_DIM, 1)
    else:
        hi = jnp.where(_low_lanes(col.shape), 0.0, col)
        lo = pltpu.roll(hi, HEAD_DIM, 1)
    return jnp.concatenate([lo, hi], axis=0).astype(BF16)


def _pair_rows(x, kh):
    c = 2 * 128 * kh
    return jnp.concatenate([x[:, c:c + 128], x[:, c + 128:c + 256]], axis=0)


def _restack(big):
    return jnp.concatenate([big[:, 0:2 * BLOCK], big[:, 2 * BLOCK:4 * BLOCK]], axis=0)


def _unrestack(stacked):
    return jnp.concatenate([stacked[0:2 * BLOCK], stacked[2 * BLOCK:4 * BLOCK]], axis=1)


def _fold_pair(x2, kh):
    low = _low_lanes((2 * BLOCK, 128))
    mixed = jnp.where(low, x2[0:2 * BLOCK], x2[2 * BLOCK:4 * BLOCK])
    total = mixed + pltpu.roll(mixed, HEAD_DIM, 1)
    return jnp.where(low, total, 0.0) if kh % 2 == 0 else jnp.where(low, 0.0, total)


def _attn_scores(qr, k2, kh):
    q2 = _pair_rows(qr, kh).astype(BF16)
    return q2, _restack(_dot_nt(q2, k2))


def _attn_softmax(s, sink_col, mask):
    own, no_key = mask
    s = jnp.where(no_key, -1e30, _fold(s, own))
    m = jnp.maximum(jnp.max(s, axis=-1, keepdims=True), sink_col)
    p = jnp.exp(s - m)
    p_sink = jnp.exp(sink_col - m)
    denom = jnp.sum(p, axis=-1, keepdims=True) + p_sink
    return p / denom, p_sink / denom


def _attn_forward(proj, tabs, sinks):
    T = proj.shape[0]
    nb = T // BLOCK

    def body(q_ref, kvc_ref, kvp_ref, g0_ref, g1_ref, cc, sac, sbc, cp_, sap, sbp, sink_ref, y_ref, qrb_ref, krb_ref, p_ref):
        n = pl.program_id(0)
        tc = tcur = (cc[...], sac[...], sbc[...])
        tprev = (cp_[...], sap[...], sbp[...])
        qr = _rope(q_ref[...], *tc) * ATTN_SCALE
        kr_cur = _rope(kvc_ref[:, 0:KV_W], *tcur)
        kr_prev = _rope(kvp_ref[:, 0:KV_W], *tprev)
        qrb_ref[...] = qr.astype(BF16)
        krb_ref[...] = kr_cur.astype(BF16)
        v_cur, v_prev = kvc_ref[:, KV_W:2 * KV_W], kvp_ref[:, KV_W:2 * KV_W]
        mask = _attn_mask(n)
        outs = []
        k2s = [_kv_pair_operand(kr_prev, kr_cur, kh) for kh in range(N_KV)]
        v2s = [_kv_pair_operand(v_prev, v_cur, kh) for kh in range(N_KV)]
        scores = [_attn_scores(qr, k2s[kh], kh) for kh in range(N_KV)]
        p_parts = []
        for kh in range(N_KV):
            pn, _ = _attn_softmax(scores[kh][1], _sink_col(sink_ref, kh), mask)
            p_parts += [pn[g * BLOCK:(g + 1) * BLOCK] for g in range(GROUP)]
            o_big = _dot(_unrestack(_unfold(pn.astype(BF16), mask[0])), v2s[kh])
            outs += [o_big[0:BLOCK], o_big[BLOCK:2 * BLOCK]]
        p_ref[...] = jnp.concatenate(p_parts, axis=1)
        o = jnp.concatenate(outs, axis=1)
        g = jnp.concatenate([g0_ref[...], g1_ref[...]], axis=1)
        y_ref[...] = (o * (g * _sigmoid(g))).astype(BF16)

    def blk(w, cb):
        return pl.BlockSpec((BLOCK, w), lambda n, cb=cb: (n, cb))

    prev = lambda w, cb: pl.BlockSpec((BLOCK, w), lambda n, cb=cb: (jnp.maximum(n - 1, 0), cb))
    return pl.pallas_call(
        body, grid=(nb,), name="attn_forward",
        out_shape=(jax.ShapeDtypeStruct((T, D_MODEL), BF16), jax.ShapeDtypeStruct((T, D_MODEL), BF16),
                   jax.ShapeDtypeStruct((T, KV_W), BF16), jax.ShapeDtypeStruct((T, N_HEADS * BLOCK), F32)),
        in_specs=[blk(D_MODEL, 0), blk(CB, CB_KV), prev(CB, CB_KV), blk(CB, CB_GA), blk(CB, CB_GA + 1),
                  blk(128, 0), blk(128, 0), blk(128, 0), prev(128, 0), prev(128, 0), prev(128, 0),
                  pl.BlockSpec(memory_space=pltpu.SMEM)],
        out_specs=(blk(D_MODEL, 0), blk(D_MODEL, 0), blk(KV_W, 0), blk(N_HEADS * BLOCK, 0)),
        compiler_params=_cp("parallel"),
    )(proj, proj, proj, proj, proj, *tabs, *tabs, sinks)


def _scan_rows8():
    return lax.broadcasted_iota(jnp.int32, (8, D_MODEL), 0)


def _scan_forward(a_ref, b_ref, h_ref, carry, rows):
    row = _scan_rows8()

    def group(i, carry):
        off = pl.multiple_of(i * 8, 8)
        a, b = a_ref[pl.ds(off, 8), :], b_ref[pl.ds(off, 8), :]
        for d in (1, 2, 4):
            ok = row >= d
            b = jnp.where(ok, a * pltpu.roll(b, d, 0) + b, b)
            a = jnp.where(ok, a * pltpu.roll(a, d, 0), a)
        h = a * carry + b
        h_ref[pl.ds(off, 8), :] = h
        return h[7:8, :]

    return lax.fori_loop(0, rows // 8, group, carry)


def _scan_backward(a_ref, g_ref, lam_ref, carry, rows):
    row = _scan_rows8()

    def group(i, carry):
        off = pl.multiple_of((rows // 8 - 1 - i) * 8, 8)
        a, g = a_ref[pl.ds(off, 8), :], g_ref[pl.ds(off, 8), :]
        b = a * g
        for d in (1, 2, 4):
            ok = row < 8 - d
            b = jnp.where(ok, a * pltpu.roll(b, 8 - d, 0) + b, b)
            a = jnp.where(ok, a * pltpu.roll(a, 8 - d, 0), a)
        mu = a * carry + b
        mu_below = jnp.where(row == 7, carry, pltpu.roll(mu, 7, 0))
        lam_ref[pl.ds(off, 8), :] = g + mu_below
        return mu[0:1, :]

    return lax.fori_loop(0, rows // 8, group, carry)


def _conv_taps(xbuf, xr, tail):
    rows = xr.shape[0]
    xbuf[0:8, :] = tail
    xbuf[8:rows + 8, :] = xr
    return [xbuf[pl.ds(8 - (CONV_W - 1 - k), rows), :] for k in range(CONV_W - 1)] + [xr]


def _rnn_gates(xbuf, xr, tail, cw, cb, wa_ref, wx_ref, ba, bx, sp, reset):
    xs = _conv_taps(xbuf, xr, tail)
    xc = xs[0] * cw[0:1, :]
    for k in range(1, CONV_W):
        xc = xc + xs[k] * cw[k:k + 1, :]
    xc = xc + cb
    xcb = xc.astype(BF16)
    za = jnp.concatenate([_dot(xcb[:, RNN_BW * j:RNN_BW * (j + 1)], wa_ref[j]) for j in range(RNN_BLOCKS)], axis=1) + ba
    zx = jnp.concatenate([_dot(xcb[:, RNN_BW * j:RNN_BW * (j + 1)], wx_ref[j]) for j in range(RNN_BLOCKS)], axis=1) + bx
    r, i = _sigmoid(za), _sigmoid(zx)
    neg_log_a = LRU_C * r * sp
    a_raw = jnp.exp(-neg_log_a)
    mult_raw = jnp.sqrt(jnp.tanh(neg_log_a) * (1.0 + a_raw * a_raw))
    a = jnp.where(reset, 0.0, a_raw)
    mult = jnp.where(reset, 1.0, mult_raw)
    return xc, r, i, a, mult


def _rnn_forward(proj, pos_col, conv_w, conv_b, rwa, rwx, ba, bx, lam):
    T = proj.shape[0]
    tr = min(T, 256)

    def body(x0, x1, g0, g1, pos_ref, cw_ref, cb_ref, wa_ref, wx_ref, ba_ref, bx_ref, lam_ref,
             y_ref, h_ref, xc_ref, r_ref, i_ref, a_ref, mult_ref, xbuf, bbuf, tail, carry):
        t = pl.program_id(0)

        @pl.when(t == 0)
        def _():
            tail[...] = jnp.zeros_like(tail)
            carry[...] = jnp.zeros_like(carry)

        xr = jnp.concatenate([x0[...], x1[...]], axis=1)
        sp = _softplus(-lam_ref[...])
        reset = pos_ref[...] == 0
        xc, r, i, a, mult = _rnn_gates(
            xbuf, xr, tail[...], cw_ref[...], cb_ref[...], wa_ref, wx_ref, ba_ref[...], bx_ref[...], sp, reset)
        xc_ref[...] = xc
        r_ref[...] = r
        i_ref[...] = i
        a_ref[...] = a
        mult_ref[...] = mult
        bbuf[...] = mult * (i * xc)
        last = _scan_forward(a_ref, bbuf, h_ref, carry[0:1, :], tr)
        carry[...] = jnp.broadcast_to(last, carry.shape)
        tail[...] = xr[tr - 8:tr, :]
        g = jnp.concatenate([g0[...], g1[...]], axis=1)
        y_ref[...] = (h_ref[...] * (g * _sigmoid(g))).astype(BF16)

    blk = lambda cb: pl.BlockSpec((tr, CB), lambda t, cb=cb: (t, cb))
    row = lambda w: pl.BlockSpec((1, w), lambda t: (0, 0))
    full3 = pl.BlockSpec((RNN_BLOCKS, RNN_BW, RNN_BW), lambda t: (0, 0, 0))
    tok = pl.BlockSpec((tr, D_MODEL), lambda t: (t, 0))
    act = jax.ShapeDtypeStruct((T, D_MODEL), F32)
    return pl.pallas_call(
        body, out_shape=(jax.ShapeDtypeStruct((T, D_MODEL), BF16),) + (act,) * 6,
        grid=(T // tr,), name="rnn_forward",
        in_specs=[blk(CB_XR), blk(CB_XR + 1), blk(CB_GR), blk(CB_GR + 1), pl.BlockSpec((tr, 1), lambda t: (t, 0)),
                  pl.BlockSpec((CONV_W, D_MODEL), lambda t: (0, 0)), row(D_MODEL), full3, full3,
                  row(D_MODEL), row(D_MODEL), row(D_MODEL)],
        out_specs=(tok,) * 7,
        scratch_shapes=[pltpu.VMEM((tr + 8, D_MODEL), F32), pltpu.VMEM((tr, D_MODEL), F32),
                        pltpu.VMEM((8, D_MODEL), F32), pltpu.VMEM((8, D_MODEL), F32)],
        compiler_params=_cp("arbitrary"),
    )(proj, proj, proj, proj, pos_col, *_in_hbm(conv_w, conv_b, rwa, rwx, ba, bx, lam))


def _merge_and_head(x, target, y_attn, y_rnn, proj, wap, wrp, wo, mod_row, final_g):
    T = x.shape[0]
    tm = min(T, 256)

    def body(x_ref, t_ref, ya_ref, yr_ref, ma0, ma1, mr0, mr1, wap_ref, wrp_ref, wo_ref, mod_ref, fg_ref,
             dx2_ref, mg_ref, do_ref, dpa_ref, dpr_ref, dya_ref, dyr_ref, dc_ref, dfg_ref, dgate_ref, loss_ref):
        i = pl.program_id(0)
        gate = mod_ref[:, 2 * D_MODEL:3 * D_MODEL]
        fg = fg_ref[...]
        pa, pr = _dot(ya_ref[...], wap_ref[...]), _dot(yr_ref[...], wrp_ref[...])
        sa = _sigmoid(jnp.concatenate([ma0[...], ma1[...]], axis=1))
        sr = _sigmoid(jnp.concatenate([mr0[...], mr1[...]], axis=1))
        mb = (sa * pa + sr * pr).astype(BF16)
        o = _dot(mb, wo_ref[...])
        x2 = x_ref[...] + gate * o
        r2 = _rms(x2)
        xn2 = x2 * r2
        err = xn2 * fg - t_ref[...]
        loss_t = 0.5 * jnp.sum(jnp.sum(err * err, axis=-1, keepdims=True) * (1.0 / D_MODEL), axis=0, keepdims=True)
        dy = err * (1.0 / D_MODEL)
        dfg_t = jnp.sum(dy * xn2, axis=0, keepdims=True)
        dxn = dy * fg
        dx2 = r2 * (dxn - xn2 * jnp.mean(dxn * xn2, axis=-1, keepdims=True))
        dgate_t = jnp.sum(dx2 * o, axis=0, keepdims=True)
        dob = (dx2 * gate).astype(BF16)
        dmerged = _dot_nt(dob, wo_ref[...])
        dpa, dpr = (dmerged * sa).astype(BF16), (dmerged * sr).astype(BF16)
        dya, dyr = _dot_nt(dpa, wap_ref[...]), _dot_nt(dpr, wrp_ref[...])
        dx2_ref[...] = dx2
        mg_ref[...] = mb
        do_ref[...] = dob
        dpa_ref[...] = dpa
        dpr_ref[...] = dpr
        dya_ref[...] = dya
        dyr_ref[...] = dyr
        dc_ref[:, 0:D_MODEL] = (dmerged * pa * sa * (1.0 - sa)).astype(BF16)
        dc_ref[:, D_MODEL:2 * D_MODEL] = (dmerged * pr * sr * (1.0 - sr)).astype(BF16)

        @pl.when(i == 0)
        def _():
            dfg_ref[...] = jnp.zeros_like(dfg_ref)
            dgate_ref[...] = jnp.zeros_like(dgate_ref)
            loss_ref[...] = jnp.zeros_like(loss_ref)

        dfg_ref[...] += dfg_t
        dgate_ref[...] += dgate_t
        loss_ref[...] += jnp.broadcast_to(loss_t, loss_ref.shape)

    tok = lambda w: pl.BlockSpec((tm, w), lambda i: (i, 0))
    blk = lambda cb: pl.BlockSpec((tm, CB), lambda i, cb=cb: (i, cb))
    wfull = pl.BlockSpec((D_MODEL, D_MODEL), lambda i: (0, 0), pipeline_mode=pl.Buffered(1))
    row = lambda w: pl.BlockSpec((1, w), lambda i: (0, 0))
    out_shape = (
        jax.ShapeDtypeStruct((T, D_MODEL), F32), jax.ShapeDtypeStruct((T, D_MODEL), BF16),
        jax.ShapeDtypeStruct((T, D_MODEL), BF16), jax.ShapeDtypeStruct((T, D_MODEL), BF16),
        jax.ShapeDtypeStruct((T, D_MODEL), BF16), jax.ShapeDtypeStruct((T, D_MODEL), F32),
        jax.ShapeDtypeStruct((T, D_MODEL), F32), jax.ShapeDtypeStruct((T, 2 * D_MODEL), BF16),
        jax.ShapeDtypeStruct((1, D_MODEL), F32), jax.ShapeDtypeStruct((1, D_MODEL), F32),
        jax.ShapeDtypeStruct((1, 128), F32),
    )
    return pl.pallas_call(
        body, out_shape=out_shape, grid=(T // tm,), name="merge_and_head",
        in_specs=[tok(D_MODEL), tok(D_MODEL), tok(D_MODEL), tok(D_MODEL), blk(CB_MA), blk(CB_MA + 1), blk(CB_MR),
                  blk(CB_MR + 1), wfull, wfull, wfull, row(ADA_W), row(D_MODEL)],
        out_specs=(tok(D_MODEL),) * 7 + (tok(2 * D_MODEL), row(D_MODEL), row(D_MODEL), row(128)),
        compiler_params=_cp("arbitrary"),
    )(x, target, y_attn, y_rnn, proj, proj, proj, proj, wap, wrp, wo, *_in_hbm(mod_row, final_g))


def _attn_backward(proj, qr_b, kr_b, p_all, d_y, tabs, after):
    T = proj.shape[0]
    nb = T // BLOCK

    def body(qrb_ref, krc_ref, krp_ref, vc_ref, vp_ref, g0_ref, g1_ref, dy_ref, p_ref, cc, sac, sbc, cp_, sap, sbp, after_ref,
             dq_ref, dkv_ref, dg_ref, dsink_ref, carry):
        n = pl.program_id(0)

        @pl.when(n == 0)
        def _():
            carry[...] = jnp.zeros_like(carry)
            dsink_ref[...] = jnp.zeros_like(dsink_ref)

        @pl.when(n < nb)
        def _():
            tc = tcur = (cc[...], sac[...], sbc[...])
            tprev = (cp_[...], sap[...], sbp[...])
            qr, kr_cur, kr_prev = qrb_ref[...], krc_ref[...], krp_ref[...]
            v_cur, v_prev = vc_ref[...], vp_ref[...]
            g = jnp.concatenate([g0_ref[...], g1_ref[...]], axis=1)
            sg = _sigmoid(g)
            dy = dy_ref[...]
            d_o = dy * (g * sg)
            mask = _attn_mask(n)
            lane = lax.broadcasted_iota(jnp.int32, (1, 128), 1)
            rowg = lax.broadcasted_iota(jnp.int32, (GROUP * BLOCK, 1), 0) // BLOCK
            o_parts, dq_parts = [], []
            dk_cols, dv_cols = [None, None], [None, None]
            dsink = jnp.zeros((1, 128), F32)
            heads = range(N_KV)
            k2s = [_kv_pair_operand(kr_prev, kr_cur, kh) for kh in heads]
            v2s = [_kv_pair_operand(v_prev, v_cur, kh) for kh in heads]
            q2s = [_pair_rows(qr, kh).astype(BF16) for kh in heads]
            do2s = [_pair_rows(d_o, kh).astype(BF16) for kh in heads]
            dpns = [_fold(_restack(_dot_nt(do2s[kh], v2s[kh])), mask[0]) for kh in heads]
            pns = [jnp.concatenate([p_ref[:, BLOCK * (GROUP * kh + g):BLOCK * (GROUP * kh + g + 1)] for g in range(GROUP)], axis=0)
                   for kh in heads]
            probs = [(pn, 1.0 - jnp.sum(pn, axis=-1, keepdims=True)) for pn in pns]
            p_bigs = [_unrestack(_unfold(probs[kh][0].astype(BF16), mask[0])) for kh in heads]
            o_bigs = [_dot(p_bigs[kh], v2s[kh]) for kh in heads]
            dv2s = [_dot_tn(p_bigs[kh], do2s[kh]) for kh in heads]
            deltas = [jnp.sum(probs[kh][0] * dpns[kh], axis=-1, keepdims=True) for kh in heads]
            ds_bigs = [_unrestack(_unfold((probs[kh][0] * (dpns[kh] - deltas[kh])).astype(BF16), mask[0])) for kh in heads]
            dq2s = [_dot(ds_bigs[kh], k2s[kh]) for kh in heads]
            dk2s = [_dot_tn(ds_bigs[kh], q2s[kh]) for kh in heads]
            for kh in heads:
                o_parts += [o_bigs[kh][0:BLOCK], o_bigs[kh][BLOCK:2 * BLOCK]]
                dq_parts += [dq2s[kh][0:BLOCK], dq2s[kh][BLOCK:2 * BLOCK]]
                dk_c, dv_c = _fold_pair(dk2s[kh], kh), _fold_pair(dv2s[kh], kh)
                c = kh // 2
                dk_cols[c] = dk_c if dk_cols[c] is None else dk_cols[c] + dk_c
                dv_cols[c] = dv_c if dv_cols[c] is None else dv_cols[c] + dv_c
                ds_rows = probs[kh][1] * deltas[kh]
                for gq in range(GROUP):
                    val = -jnp.sum(jnp.where(rowg == gq, ds_rows, 0.0), axis=0, keepdims=True)
                    dsink = dsink + jnp.where(lane == GROUP * kh + ROW_GROUP_HEAD[gq], val, 0.0)
            o = jnp.concatenate(o_parts, axis=1)
            dg_ref[...] = (dy * o * (sg * (1.0 + g * (1.0 - sg)))).astype(BF16)
            dq_ref[...] = (_unrope(jnp.concatenate(dq_parts, axis=1), *tc) * ATTN_SCALE).astype(BF16)
            dk_all, dv_all = jnp.concatenate(dk_cols, axis=1), jnp.concatenate(dv_cols, axis=1)
            dk_prev = _unrope(dk_all[0:BLOCK], *tprev)
            dk_cur = _unrope(dk_all[BLOCK:2 * BLOCK], *tcur)
            dv_prev, dv_cur = dv_all[0:BLOCK], dv_all[BLOCK:2 * BLOCK]
            dkv_ref[...] = (carry[...] + jnp.concatenate([dk_prev, dv_prev], axis=1)).astype(BF16)
            carry[...] = jnp.concatenate([dk_cur, dv_cur], axis=1)
            dsink_ref[...] += dsink

        @pl.when(n == nb)
        def _():
            dkv_ref[...] = carry[...].astype(BF16)

    cur = lambda w, cb: pl.BlockSpec((BLOCK, w), lambda n, cb=cb: (jnp.minimum(n, nb - 1), cb))
    prev = lambda w, cb: pl.BlockSpec((BLOCK, w), lambda n, cb=cb: (jnp.maximum(jnp.minimum(n, nb - 1) - 1, 0), cb))
    out_shape = (jax.ShapeDtypeStruct((T, D_MODEL), BF16), jax.ShapeDtypeStruct((T, 2 * KV_W), BF16),
                 jax.ShapeDtypeStruct((T, D_MODEL), BF16), jax.ShapeDtypeStruct((1, 128), F32))
    return pl.pallas_call(
        body, out_shape=out_shape, grid=(nb + 1,), name="attn_backward",
        in_specs=[cur(D_MODEL, 0), cur(KV_W, 0), prev(KV_W, 0), cur(KV_W, V_COL_BLOCK), prev(KV_W, V_COL_BLOCK),
                  cur(CB, CB_GA), cur(CB, CB_GA + 1), cur(D_MODEL, 0), cur(N_HEADS * BLOCK, 0),
                  cur(128, 0), cur(128, 0), cur(128, 0), prev(128, 0), prev(128, 0), prev(128, 0),
                  pl.BlockSpec(memory_space=pltpu.SMEM)],
        out_specs=(cur(D_MODEL, 0), pl.BlockSpec((BLOCK, 2 * KV_W), lambda n: (jnp.maximum(n - 1, 0), 0)),
                   cur(D_MODEL, 0), pl.BlockSpec((1, 128), lambda n: (0, 0))),
        scratch_shapes=[pltpu.VMEM((BLOCK, 2 * KV_W), F32)],
        compiler_params=_cp("arbitrary"),
    )(qr_b, kr_b, kr_b, proj, proj, proj, proj, d_y, p_all, *tabs, *tabs, after)


def _rnn_backward(proj, pos_col, h_rnn, saved, d_y, conv_w, rwa, rwx, lam):
    T = proj.shape[0]
    tr = min(T, 256)
    nt = T // tr
    hb = tr // 8

    def body(x0, x1, xh0, xh1, g0, g1, pos_ref, h_ref, hh_ref, xc_ref, r_ref, i_ref, a_ref, mult_ref, dy_ref,
             cw_ref, wa_ref, wx_ref, lam_ref, db_ref, dcw_ref, dcb_ref, dwa_ref, dwx_ref, dba_ref, dbx_ref, dlam_ref,
             xbuf, hbuf, dbuf, gbuf, lbuf, mu_carry, dxc_head):
        step = pl.program_id(0)
        first_tile = step == nt - 1

        @pl.when(step == 0)
        def _():
            mu_carry[...] = jnp.zeros_like(mu_carry)
            dxc_head[...] = jnp.zeros_like(dxc_head)
            for ref in (dcw_ref, dcb_ref, dwa_ref, dwx_ref, dba_ref, dbx_ref, dlam_ref):
                ref[...] = jnp.zeros_like(ref)

        xr = jnp.concatenate([x0[...], x1[...]], axis=1)
        tail = jnp.where(first_tile, 0.0, jnp.concatenate([xh0[...], xh1[...]], axis=1))
        lam_v = lam_ref[...]
        sp = _softplus(-lam_v)
        reset = pos_ref[...] == 0
        cw = cw_ref[...]
        xbuf[0:8, :] = tail
        xbuf[8:tr + 8, :] = xr
        g = jnp.concatenate([g0[...], g1[...]], axis=1)
        sg = _sigmoid(g)
        dy = dy_ref[...]
        h = h_ref[...]
        db_ref[:, D_MODEL:2 * D_MODEL] = (dy * h * (sg * (1.0 + g * (1.0 - sg)))).astype(BF16)
        gbuf[...] = dy * (g * sg)
        top = _scan_backward(a_ref, gbuf, lbuf, mu_carry[0:1, :], tr)
        mu_carry[...] = jnp.broadcast_to(top, mu_carry.shape)
        hbuf[0:8, :] = jnp.where(first_tile, 0.0, hh_ref[...])
        hbuf[8:tr + 8, :] = h
        live = jnp.logical_not(reset)
        dbuf[tr:tr + 8, :] = dxc_head[...]
        for j in range(RNN_BLOCKS):
            sl = slice(RNN_BW * j, RNN_BW * (j + 1))
            lam_t, h_prev = lbuf[:, sl], hbuf[pl.ds(7, tr), sl]
            xc, r, i, a, mult = xc_ref[:, sl], r_ref[:, sl], i_ref[:, sl], a_ref[:, sl], mult_ref[:, sl]
            d_a = jnp.where(live, lam_t * h_prev, 0.0)
            d_mult = jnp.where(live, lam_t * (i * xc), 0.0)
            d_ixc = lam_t * mult
            d_i = d_ixc * xc
            d_log_a = d_a * a - d_mult * (a * a / mult)
            d_za = d_log_a * (-LRU_C * sp[:, sl]) * (r * (1.0 - r))
            d_zx = d_i * (i * (1.0 - i))
            dlam_ref[:, sl] += jnp.sum(d_log_a * r, axis=0, keepdims=True) * (LRU_C * _sigmoid(-lam_v[:, sl]))
            dba_ref[:, sl] += jnp.sum(d_za, axis=0, keepdims=True)
            dbx_ref[:, sl] += jnp.sum(d_zx, axis=0, keepdims=True)
            xcb, dzab, dzxb = xc.astype(BF16), d_za.astype(BF16), d_zx.astype(BF16)
            dwa_ref[j] += _dot_tn(xcb, dzab)
            dwx_ref[j] += _dot_tn(xcb, dzxb)
            d_xc = d_ixc * i + (_dot_nt(dzab, wa_ref[j]) + _dot_nt(dzxb, wx_ref[j]))
            dcb_ref[:, sl] += jnp.sum(d_xc, axis=0, keepdims=True)
            for k in range(CONV_W):
                tap = xr[:, sl] if k == CONV_W - 1 else xbuf[pl.ds(8 - (CONV_W - 1 - k), tr), sl]
                dcw_ref[k:k + 1, sl] += jnp.sum(d_xc * tap, axis=0, keepdims=True)
            dbuf[0:tr, sl] = d_xc
            d_xr = d_xc * cw[CONV_W - 1:CONV_W, sl]
            for k in range(CONV_W - 1):
                d_xr = d_xr + dbuf[pl.ds(CONV_W - 1 - k, tr), sl] * cw[k:k + 1, sl]
            dxc_head[:, sl] = d_xc[0:8, :]
            db_ref[:, sl] = d_xr.astype(BF16)

    rev = lambda s: nt - 1 - s
    blk = lambda cb: pl.BlockSpec((tr, CB), lambda s, cb=cb: (rev(s), cb))
    halo = lambda w, cb: pl.BlockSpec((8, w), lambda s, cb=cb: (jnp.maximum(rev(s) * hb - 1, 0), cb))
    tok = lambda w: pl.BlockSpec((tr, w), lambda s: (rev(s), 0))
    row = lambda w: pl.BlockSpec((1, w), lambda s: (0, 0))
    full3 = pl.BlockSpec((RNN_BLOCKS, RNN_BW, RNN_BW), lambda s: (0, 0, 0))
    cwspec = pl.BlockSpec((CONV_W, D_MODEL), lambda s: (0, 0))
    vec = jax.ShapeDtypeStruct((1, D_MODEL), F32)
    gate_w = jax.ShapeDtypeStruct((RNN_BLOCKS, RNN_BW, RNN_BW), F32)
    out_shape = (jax.ShapeDtypeStruct((T, 2 * D_MODEL), BF16), jax.ShapeDtypeStruct((CONV_W, D_MODEL), F32), vec,
                 gate_w, gate_w, vec, vec, vec)
    big = lambda: pltpu.VMEM((tr, D_MODEL), F32)
    ext = lambda: pltpu.VMEM((tr + 8, D_MODEL), F32)
    return pl.pallas_call(
        body, out_shape=out_shape, grid=(nt,), name="rnn_backward",
        in_specs=[blk(CB_XR), blk(CB_XR + 1), halo(CB, CB_XR), halo(CB, CB_XR + 1), blk(CB_GR), blk(CB_GR + 1),
                  pl.BlockSpec((tr, 1), lambda s: (rev(s), 0)), tok(D_MODEL), halo(D_MODEL, 0)] + [tok(D_MODEL)] * 6
        + [cwspec, full3, full3, row(D_MODEL)],
        out_specs=(tok(2 * D_MODEL), cwspec, row(D_MODEL), full3, full3, row(D_MODEL), row(D_MODEL), row(D_MODEL)),
        scratch_shapes=[ext(), ext(), ext(), big(), big(), pltpu.VMEM((8, D_MODEL), F32), pltpu.VMEM((8, D_MODEL), F32)],
        compiler_params=_cp("arbitrary"),
    )(proj, proj, proj, proj, proj, proj, pos_col, h_rnn, h_rnn, *saved, d_y, *_in_hbm(conv_w, rwa, rwx, lam))


def _input_backward(pieces, w_in, x, dx2, mod_row, norm_g):
    T = x.shape[0]
    tm = min(T, 512)
    n = len(pieces)

    def body(*refs):
        d_refs = refs[:n]
        w_ref, x_ref, dx2_ref, mod_ref, g_ref, gx_ref, dshift_ref, dscale_ref, dg_ref = refs[n:]
        i = pl.program_id(0)
        dh = None
        for d_ref, (_, start, count) in zip(d_refs, pieces):
            part = _dot_nt(d_ref[...], w_ref[:, start * CB:(start + count) * CB])
            dh = part if dh is None else dh + part

        @pl.when(i == 0)
        def _():
            dshift_ref[...] = jnp.zeros_like(dshift_ref)
            dscale_ref[...] = jnp.zeros_like(dscale_ref)
            dg_ref[...] = jnp.zeros_like(dg_ref)

        xf = x_ref[...]
        r1 = _rms(xf)
        xn = xf * r1
        gn = g_ref[...]
        s1 = 1.0 + mod_ref[:, D_MODEL:2 * D_MODEL]
        dshift_ref[...] += jnp.sum(dh, axis=0, keepdims=True)
        dscale_ref[...] += jnp.sum(dh * (xn * gn), axis=0, keepdims=True)
        dg_ref[...] += jnp.sum(dh * s1 * xn, axis=0, keepdims=True)
        dxn = dh * s1 * gn
        gx_ref[...] = dx2_ref[...] + r1 * (dxn - xn * jnp.mean(dxn * xn, axis=-1, keepdims=True))

    tok = lambda w: pl.BlockSpec((tm, w), lambda i: (i, 0))
    row = lambda w: pl.BlockSpec((1, w), lambda i: (0, 0))
    vec = jax.ShapeDtypeStruct((1, D_MODEL), F32)
    return pl.pallas_call(
        body, out_shape=(jax.ShapeDtypeStruct((T, D_MODEL), F32), vec, vec, vec), grid=(T // tm,), name="input_backward",
        in_specs=[tok(c * CB) for _, _, c in pieces]
        + [pl.BlockSpec((D_MODEL, IN_W), lambda i: (0, 0), pipeline_mode=pl.Buffered(1)), tok(D_MODEL), tok(D_MODEL),
           row(ADA_W), row(D_MODEL)],
        out_specs=(tok(D_MODEL), row(D_MODEL), row(D_MODEL), row(D_MODEL)),
        compiler_params=_cp("arbitrary"),
    )(*[p[0] for p in pieces], w_in, x, dx2, *_in_hbm(mod_row, norm_g))


def _weight_grad(a, pieces, tag, a_is_transposed=False, block_cols=CB):
    CB = block_cols
    M, T = a.shape if a_is_transposed else a.shape[::-1]
    n_blocks = sum(count for _, _, count in pieces)
    n = len(pieces)
    contract = _dot if a_is_transposed else _dot_tn

    def body(*refs):
        a_ref, b_refs, o_ref = refs[0], refs[1:1 + n], refs[-1]
        j = pl.program_id(0)
        for b_ref, (_, start, count) in zip(b_refs, pieces):
            @pl.when((j >= start) & (j < start + count))
            def _(b_ref=b_ref):
                o_ref[...] = contract(a_ref[...], b_ref[...])

    def piece_spec(start, count):
        return pl.BlockSpec((T, CB), lambda j: (0, jnp.clip(j - start, 0, count - 1)))

    return pl.pallas_call(
        body, out_shape=jax.ShapeDtypeStruct((M, n_blocks * CB), F32), grid=(n_blocks,), name=f"weight_grad_{tag}",
        in_specs=[pl.BlockSpec(a.shape, lambda j: (0, 0), pipeline_mode=pl.Buffered(1))] + [piece_spec(s, c) for _, s, c in pieces],
        out_specs=pl.BlockSpec((M, CB), lambda j: (0, j)), compiler_params=_cp("arbitrary"),
    )(a, *[p[0] for p in pieces])


def _adamw(w, g, m, v):
    m = ADAM_B1 * m + (1.0 - ADAM_B1) * g
    v = ADAM_B2 * v + (1.0 - ADAM_B2) * (g * g)
    m_hat = m / (1.0 - ADAM_B1 ** ADAM_STEP)
    v_hat = v / (1.0 - ADAM_B2 ** ADAM_STEP)
    delta = -ADAM_LR * (m_hat / (jnp.sqrt(v_hat) + ADAM_EPS) + ADAM_WD * w)
    return delta, m, v


def _sum_landed(kind, owns, lands, where, tag):
    n = len(owns)
    land = lands[0]
    if kind == "in":
        R, C = land.shape[1:]
        tr = 256
        grid = (R // tr,)
        own_spec = pl.BlockSpec((tr, C), lambda i, w: (i, w[0]))
        land_spec = pl.BlockSpec((3, tr, C), lambda i, w: (0, i, 0))
        out_spec = pl.BlockSpec((1, tr, C), lambda i, w: (w[1], i, 0))
        out_shape = (2, R, C)
        pick = lambda ref: ref[...]
    elif kind == "sq":
        R, C = land.shape[1:]
        grid = (1,)
        own_spec = pl.BlockSpec((1, R, C), lambda i, w: (w[0], 0, 0))
        land_spec = pl.BlockSpec((3, R, C), lambda i, w: (0, 0, 0))
        out_spec = pl.BlockSpec((1, R, C), lambda i, w: (w[1], 0, 0))
        out_shape = (2, R, C)
        pick = lambda ref: ref[0]
    else:
        B, R, C = land.shape[1:]
        grid = (1,)
        own_spec = pl.BlockSpec((B, 1, R, C), lambda i, w: (0, w[0], 0, 0))
        land_spec = pl.BlockSpec((3, B, R, C), lambda i, w: (0, 0, 0, 0))
        out_spec = pl.BlockSpec((B, 1, R, C), lambda i, w: (0, w[1], 0, 0))
        out_shape = (B, 2, R, C)
        pick = lambda ref: ref[:, 0]

    def body(w_ref, *refs):
        for k in range(n):
            own_ref, l_ref, o_ref = refs[k], refs[n + k], refs[2 * n + k]
            total = ((pick(own_ref) + l_ref[0].astype(F32)) + l_ref[1].astype(F32)) + l_ref[2].astype(F32)
            if kind == "rg":
                o_ref[:, 0] = total
            else:
                o_ref[0] = total

    grid_spec = pltpu.PrefetchScalarGridSpec(num_scalar_prefetch=1, grid=grid, in_specs=[own_spec] * n + [land_spec] * n,
                                             out_specs=(out_spec,) * n)
    return list(pl.pallas_call(
        body, out_shape=(jax.ShapeDtypeStruct(out_shape, F32),) * n, grid_spec=grid_spec, name=f"sum_landed_{tag}",
        compiler_params=_cp("parallel"),
    )(where, *owns, *lands))


def _adamw_shard(gs, ws, ms, vs, tag):
    n = len(ws)
    R, C = ws[0].shape
    tr = min(R, 256 if n == 1 else 64)

    def body(*refs):
        for k in range(n):
            g = refs[k][...]
            d, nm, nv = _adamw(refs[n + k][...], g, refs[2 * n + k][...], refs[3 * n + k][...])
            out = refs[4 * n + 4 * k:4 * n + 4 * k + 4]
            out[0][...] = g
            out[1][...] = d
            out[2][...] = nm
            out[3][...] = nv

    spec = pl.BlockSpec((tr, C), lambda i: (i, 0))
    sds = jax.ShapeDtypeStruct((R, C), F32)
    outs = pl.pallas_call(
        body, out_shape=(sds,) * (4 * n), grid=(R // tr,), name=f"adamw_{tag}",
        in_specs=[spec] * (4 * n), out_specs=(spec,) * (4 * n), compiler_params=_cp("parallel"),
    )(*gs, *_in_hbm(*ws, *ms, *vs))
    return [outs[4 * k:4 * k + 4] for k in range(n)]


def _adamw_w_ada(c_t, dmod_cols, w, m, v):
    R, C = w.shape

    def body(ct_ref, dm_ref, w_ref, m_ref, v_ref, g_ref, d_ref, nm_ref, nv_ref):
        g = _dot(ct_ref[...].astype(BF16), dm_ref[...].astype(BF16))
        d, nm, nv = _adamw(w_ref[...], g, m_ref[...], v_ref[...])
        g_ref[...] = g
        d_ref[...] = d
        nm_ref[...] = nm
        nv_ref[...] = nv

    tr = 256
    spec = pl.BlockSpec((tr, C), lambda i: (i, 0))
    sds = jax.ShapeDtypeStruct((R, C), F32)
    return pl.pallas_call(
        body, out_shape=(sds,) * 4, grid=(R // tr,), name="adamw_w_ada",
        in_specs=[pl.BlockSpec((tr, 128), lambda i: (i, 0)), pl.BlockSpec((128, C), lambda i: (0, 0))] + [spec] * 3,
        out_specs=(spec,) * 4, compiler_params=_cp("parallel"),
    )(c_t, dmod_cols, w, m, v)


def _adamw_small(small_all, ws, ms, vs):
    def body(s_ref, w_ref, m_ref, v_ref, g_ref, d_ref, nm_ref, nv_ref):
        g = s_ref[0]
        for b in range(1, N_DEV):
            g = g + s_ref[b]
        d, nm, nv = _adamw(w_ref[...], g, m_ref[...], v_ref[...])
        g_ref[...] = g
        d_ref[...] = d
        nm_ref[...] = nm
        nv_ref[...] = nv

    sds = jax.ShapeDtypeStruct((SMALL_ROWS, D_MODEL), F32)
    return pl.pallas_call(
        body, out_shape=(sds,) * 4, name="adamw_small", in_specs=[VMEM_SPEC] * 4, out_specs=(VMEM_SPEC,) * 4,
        compiler_params=pltpu.CompilerParams(vmem_limit_bytes=VMEM_LIMIT_V7X),
    )(small_all, ws, ms, vs)


ROW_MOD, ROW_NORM_G, ROW_CONV_B, ROW_BA, ROW_BX, ROW_LAM, ROW_FINAL_G, ROW_SINKS, ROW_CONV_W, ROW_LOSS = 0, 3, 4, 5, 6, 7, 8, 9, 10, 14


def _pack_small(b_ada, norm_g, conv_b, ba, bx, lam, final_g, sinks, conv_w_full, loss_row=None):
    lane_pad = lambda a: jnp.pad(a.reshape(1, -1), ((0, 0), (0, D_MODEL - a.size)))
    rows = [b_ada.reshape(3, D_MODEL), norm_g, conv_b, ba, bx, lam, final_g.reshape(1, D_MODEL), lane_pad(sinks), conv_w_full,
            jnp.zeros((1, D_MODEL), F32) if loss_row is None else lane_pad(loss_row),
            jnp.zeros((SMALL_ROWS - ROW_LOSS - 1, D_MODEL), F32)]
    return jnp.concatenate([r.astype(F32) for r in rows], axis=0)


def kernel(x, c, positions, w_ada, b_ada, norm_g, w_in, attn_sinks, conv_w, conv_b, rg_wa, rg_ba, rg_wx, rg_bx, rg_lambda, w_attn_proj, w_rnn_proj, w_out, final_g, loss_target, m_w_ada, m_b_ada, m_norm_g, m_w_in, m_attn_sinks, m_conv_w, m_conv_b, m_rg_wa, m_rg_ba, m_rg_wx, m_rg_bx, m_rg_lambda, m_w_attn_proj, m_w_rnn_proj, m_w_out, m_final_g, v_w_ada, v_b_ada, v_norm_g, v_w_in, v_attn_sinks, v_conv_w, v_conv_b, v_rg_wa, v_rg_ba, v_rg_wx, v_rg_bx, v_rg_lambda, v_w_attn_proj, v_w_rnn_proj, v_w_out, v_final_g):
    T = x.shape[1]
    my_chip = lax.axis_index("x") * 2 + lax.axis_index("y")
    my_dev = my_chip * 2 + lax.axis_index("c")
    x2d, tgt = x[0], loss_target[0]
    pos_col = positions.reshape(T, 1)

    chip_idx = my_chip.reshape(1).astype(jnp.int32)
    c_idx = lax.axis_index("c").reshape(1).astype(jnp.int32)
    sq_place = ((D_MODEL, D_MODEL), (SHARD_ROWS, D_MODEL), lambda chip: (chip, 0))
    rg_place = ((RNN_BLOCKS, RNN_BW, RNN_BW), (RNN_BLOCKS, SHARD_RG, RNN_BW), lambda chip: (0, chip, 0))
    in_place = ((D_MODEL, IN_W), (D_MODEL, SHARD_IN), lambda chip: (0, chip))
    placed = _cast_place([w_in[0], w_attn_proj[0], w_rnn_proj[0], w_out[0], rg_wa[0], rg_wx[0]], chip_idx,
                         [in_place, sq_place, sq_place, sq_place, rg_place, rg_place])
    cw_chips, c_all, mod_chips = _gather_mod(c.reshape(1, 1, D_MODEL), w_ada[0], conv_w[0])
    g_ssems, g_rsems, fulls, g_token = _gather_start([p.reshape(s) for p, s in zip(placed, FULL_SHAPES)], mod_chips)
    conv_w_f = jnp.transpose(cw_chips, (1, 0, 2)).reshape(CONV_W, D_MODEL)
    mod_all = jnp.transpose(mod_chips, (1, 0, 2)).reshape(N_DEV, ADA_W) + b_ada
    mod_row = lax.dynamic_slice_in_dim(mod_all, my_dev, 1, axis=0) + g_token[0:1, 0:1]

    h, h_t, tabs = _prenorm(x2d, mod_row, norm_g, pos_col)
    w_in_v = fulls[0]
    proj = _in_projection(h, w_in_v.reshape(D_MODEL, IN_W), chip_idx, None, "own")
    for k, mask in enumerate(CHIP_MASKS):
        w_in_v = _gather_wait(g_ssems[k], g_rsems[k], [w_in_v], [0], proj, f"w_in_{k}")[0]
        w_in_v = _forward_halves([w_in_v], [(0, 0, k)], f"w_in_{k}")[0]
        from_chip = (chip_idx ^ (mask >> 1)).astype(jnp.int32)
        proj = _in_projection(h, w_in_v.reshape(D_MODEL, IN_W), from_chip, proj, f"from_{k}")
    w_in_f = w_in_v.reshape(D_MODEL, IN_W)
    rest = _gather_wait(g_ssems[3], g_rsems[3], list(fulls[1:]), [1, 2, 3, 4, 5], proj, "rest")
    rest = _forward_halves(rest, [(idx - 1, idx, k) for idx in range(1, N_BIG) for k in range(3)], "rest")
    wap_f, wrp_f, wo_f = (g.reshape(D_MODEL, D_MODEL) for g in rest[0:3])
    rwa_f, rwx_f = (g.reshape(RNN_BLOCKS, RNN_BW, RNN_BW) for g in rest[3:5])
    y_attn, qr_b, kr_b, p_all = _attn_forward(proj, tabs, attn_sinks)
    y_rnn, h_rnn, *rnn_saved = _rnn_forward(proj, pos_col, conv_w_f, conv_b, rwa_f, rwx_f, rg_ba, rg_bx, rg_lambda)
    (dx2, merged, d_o, d_pa, d_pr, d_ya, d_yr, d_c, d_final_g, d_gate, loss_vec) = _merge_and_head(
        x2d, tgt, y_attn, y_rnn, proj, wap_f, wrp_f, wo_f, mod_row, final_g.reshape(1, D_MODEL))

    sq = (N_CHIPS, 2, SHARD_ROWS // 2, D_MODEL)
    rg = (RNN_BLOCKS, N_CHIPS, 2, SHARD_RG // 2, RNN_BW)
    rg_flat = (RNN_BLOCKS * N_CHIPS, 2, SHARD_RG // 2, RNN_BW)

    def chip_sum_and_start(views, axes, flat, unflat, tags_, kinds_, group):
        from_sib = _swap_halves(views, axes)
        exact, rounded = [None] * len(views), [None] * len(views)
        for shape in dict.fromkeys(flat):
            ids = [k for k, f in enumerate(flat) if f == shape]
            ex, ro = _presum([views[k].reshape(shape) for k in ids],
                             [from_sib[k].reshape(shape[:1] + shape[2:]) for k in ids], c_idx, tags_[ids[0]])
            for k, e, r in zip(ids, ex, ro):
                exact[k], rounded[k] = e.reshape(unflat[k]), r.reshape(unflat[k])
        return _exchange_start(rounded, kinds_, group), exact

    g_ap = _weight_grad(y_attn, [(d_pa, 0, 1)], "w_attn_proj", block_cols=D_MODEL)
    g_rp = _weight_grad(y_rnn, [(d_pr, 0, 1)], "w_rnn_proj", block_cols=D_MODEL)
    g_o = _weight_grad(merged, [(d_o, 0, 1)], "w_out", block_cols=D_MODEL)
    sq_half = (N_CHIPS, SHARD_ROWS // 2, D_MODEL)
    started1, own1 = chip_sum_and_start([g_ap.reshape(sq), g_rp.reshape(sq), g_o.reshape(sq)], [1, 1, 1], [sq] * 3, [sq_half] * 3,
                                  ["w_attn_proj", "w_rnn_proj", "w_out"], ["sq"] * 3, "proj")
    d_q, d_kv, d_ga, d_sinks = _attn_backward(proj, qr_b, kr_b, p_all, d_ya, tabs, started1[4][0:1, 0:16])
    d_b, d_conv_w, d_conv_b, d_rwa, d_rwx, d_ba, d_bx, d_lam = _rnn_backward(
        proj, pos_col, h_rnn, rnn_saved, d_yr, conv_w_f, rwa_f, rwx_f, rg_lambda)
    pieces = [(d_q, CB_Q, 2), (d_kv, CB_KV, 1), (d_ga, CB_GA, 2), (d_b, CB_XR, 4), (d_c, CB_MA, 4)]
    g_in = _weight_grad(h_t, pieces, "w_in", a_is_transposed=True)
    started2, own2 = chip_sum_and_start(
        [g_in.reshape(2, D_MODEL // 2, IN_W), d_rwa.reshape(rg), d_rwx.reshape(rg)], [0, 2, 2],
        [(1, 2, D_MODEL // 2, IN_W), rg_flat, rg_flat],
        [(D_MODEL // 2, IN_W), (RNN_BLOCKS, N_CHIPS, SHARD_RG // 2, RNN_BW), (RNN_BLOCKS, N_CHIPS, SHARD_RG // 2, RNN_BW)],
        ["w_in", "rg_wa", "rg_wx"], ["in", "rg", "rg"], "in")
    grad_x, d_shift, d_scale, d_norm_g = _input_backward(pieces, w_in_f, x2d, dx2, mod_row + started2[4][0, 0], norm_g)

    d_mod = jnp.concatenate([d_shift, d_scale, d_gate], axis=1)
    small = _pack_small(d_mod, d_norm_g, d_conv_b, d_ba, d_bx, d_lam, d_final_g, d_sinks[:, :N_HEADS], d_conv_w, loss_vec)
    small_all = _gather_small(small)
    _, lands1 = _exchange_wait(*started1[:4], grad_x, "proj")
    _, lands2 = _exchange_wait(*started2[:4], grad_x, "in")
    tags = ["w_in", "w_attn_proj", "w_rnn_proj", "w_out", "rg_wa", "rg_wx"]
    chip_sums = [own2[0]] + list(own1) + list(own2[1:])
    lands = [lands2[0]] + list(lands1) + list(lands2[1:])
    where = jnp.concatenate([chip_idx, c_idx])
    kinds = ["in", "sq", "sq", "sq", "rg", "rg"]
    groups = [[0], [1, 2, 3], [4, 5]]
    halves = [None] * 6
    for ids in groups:
        for i, half in zip(ids, _sum_landed(kinds[ids[0]], [chip_sums[i] for i in ids], [lands[i] for i in ids], where,
                                            tags[ids[0]])):
            halves[i] = half
    grads = _assemble_with_sibling(halves, [0, 0, 0, 0, 1, 1])
    shapes2d = [(D_MODEL, SHARD_IN), (SHARD_ROWS, D_MODEL), (SHARD_ROWS, D_MODEL), (SHARD_ROWS, D_MODEL),
                (RNN_BLOCKS * SHARD_RG, RNN_BW), (RNN_BLOCKS * SHARD_RG, RNN_BW)]
    big_w = [w_in, w_attn_proj, w_rnn_proj, w_out, rg_wa, rg_wx]
    big_m = [m_w_in, m_w_attn_proj, m_w_rnn_proj, m_w_out, m_rg_wa, m_rg_wx]
    big_v = [v_w_in, v_w_attn_proj, v_w_rnn_proj, v_w_out, v_rg_wa, v_rg_wx]
    res = {}
    for ids in groups:
        flat2d = lambda arrs: [arrs[i].reshape(shapes2d[i]) for i in ids]
        outs = _adamw_shard(flat2d(grads), flat2d(big_w), flat2d(big_m), flat2d(big_v), tags[ids[0]])
        for i, four in zip(ids, outs):
            res[tags[i]] = [o.reshape(big_w[i].shape) for o in four]

    dmod_all = small_all[:, ROW_MOD:ROW_MOD + 3, :].reshape(N_DEV, ADA_W)
    dmod_cols = lax.dynamic_slice_in_dim(dmod_all, my_chip * SHARD_ADA, SHARD_ADA, axis=1)
    c_t = jnp.pad(jnp.transpose(c_all.reshape(N_DEV, D_MODEL)), ((0, 0), (0, 128 - N_DEV)))
    dmod_cols = jnp.pad(dmod_cols, ((0, 128 - N_DEV), (0, 0)))
    res["w_ada"] = [o.reshape(w_ada.shape) for o in _adamw_w_ada(c_t, dmod_cols, w_ada[0], m_w_ada[0], v_w_ada[0])]

    def full_conv(a):
        return lax.dynamic_update_slice_in_dim(jnp.zeros((CONV_W, D_MODEL), F32), a[0], my_chip * (D_MODEL // N_CHIPS), axis=1)

    packed = [_pack_small(p[0], p[1], p[2], p[3], p[4], p[5], p[6], p[7], full_conv(p[8])) for p in (
        (b_ada, norm_g, conv_b, rg_ba, rg_bx, rg_lambda, final_g, attn_sinks, conv_w),
        (m_b_ada, m_norm_g, m_conv_b, m_rg_ba, m_rg_bx, m_rg_lambda, m_final_g, m_attn_sinks, m_conv_w),
        (v_b_ada, v_norm_g, v_conv_b, v_rg_ba, v_rg_bx, v_rg_lambda, v_final_g, v_attn_sinks, v_conv_w))]
    small_out = _adamw_small(small_all, *packed)

    def unpack(slab):
        cw = lax.dynamic_slice_in_dim(slab[ROW_CONV_W:ROW_CONV_W + CONV_W], my_chip * (D_MODEL // N_CHIPS),
                                      D_MODEL // N_CHIPS, axis=1)
        return {
            "b_ada": slab[ROW_MOD:ROW_MOD + 3].reshape(1, ADA_W), "norm_g": slab[ROW_NORM_G:ROW_NORM_G + 1],
            "conv_b": slab[ROW_CONV_B:ROW_CONV_B + 1], "rg_ba": slab[ROW_BA:ROW_BA + 1], "rg_bx": slab[ROW_BX:ROW_BX + 1],
            "rg_lambda": slab[ROW_LAM:ROW_LAM + 1], "final_g": slab[ROW_FINAL_G], "attn_sinks": slab[ROW_SINKS:ROW_SINKS + 1, :N_HEADS],
            "conv_w": cw[None],
        }

    small_res = [unpack(s) for s in small_out]
    order = ["w_ada", "b_ada", "norm_g", "w_in", "attn_sinks", "conv_w", "conv_b", "rg_wa", "rg_ba", "rg_wx", "rg_bx",
             "rg_lambda", "w_attn_proj", "w_rnn_proj", "w_out", "final_g"]
    loss = small_out[0][ROW_LOSS, 0]
    outs = [loss, grad_x[None]]
    for kind in range(4):
        for name in order:
            outs.append(res[name][kind] if name in res else small_res[kind][name])
    return tuple(outs)
```

```python
import numpy as np
import jax
import jax.numpy as jnp
from jax import lax
from jax.experimental import pallas as pl
from jax.experimental.pallas import tpu as pltpu

F32 = jnp.float32
BF16 = jnp.bfloat16

D_MODEL = 1024
N_HEADS = 16
N_KV = 4
HEAD_DIM = 64
GROUP = N_HEADS // N_KV
BLOCK = 128
KV_W = N_KV * HEAD_DIM
ROT_HALF = 8
ROPE_THETA = 500000.0
ATTN_SCALE = 0.125
RNN_BLOCKS = 4
RNN_BW = 256
CONV_W = 4
LRU_C = 8.0
NORM_EPS = 1e-6
IN_W = 6656
CB = 512
N_CB = IN_W // CB
CB_Q, CB_KV, CB_GA, CB_XR, CB_GR, CB_MA, CB_MR = 0, 2, 3, 5, 7, 9, 11
V_COL_BLOCK = 5
N_CHIPS = 4
N_DEV = 8
SHARD_IN = IN_W // N_CHIPS
SHARD_ROWS = D_MODEL // N_CHIPS
SHARD_RG = RNN_BW // N_CHIPS
ADA_W = 3 * D_MODEL
SHARD_ADA = ADA_W // N_CHIPS
SMALL_ROWS = 16

ADAM_LR = 0.001
ADAM_B1 = 0.9
ADAM_B2 = 0.999
ADAM_EPS = 1e-08
ADAM_WD = 0.01
ADAM_STEP = 10

VMEM_LIMIT_V7X = 52 * 1024 * 1024
MESH = pl.DeviceIdType.MESH
ANY = pl.BlockSpec(memory_space=pl.ANY)
VMEM_SPEC = pl.BlockSpec(memory_space=pltpu.VMEM)


def _in_hbm(*arrays):
    return [pltpu.with_memory_space_constraint(a, pltpu.HBM) for a in arrays]


def _cp(*sem):
    return pltpu.CompilerParams(dimension_semantics=sem if sem else None, vmem_limit_bytes=VMEM_LIMIT_V7X)


def _dot(a, b):
    return jnp.dot(a, b, preferred_element_type=F32)


def _dot_nt(a, b):
    return lax.dot_general(a, b, (((1,), (1,)), ((), ())), preferred_element_type=F32)


def _dot_tn(a, b):
    return lax.dot_general(a, b, (((0,), (0,)), ((), ())), preferred_element_type=F32)


def _sigmoid(z):
    return 1.0 / (1.0 + jnp.exp(-z))


def _softplus(z):
    u = jnp.exp(-jnp.abs(z))
    log1p_u = jnp.where(u < 1e-3, u * (1.0 - u * (0.5 - u * (1.0 / 3.0))), jnp.log(1.0 + u))
    return jnp.maximum(z, 0.0) + log1p_u


def _rms(xf):
    return lax.rsqrt(jnp.mean(xf * xf, axis=-1, keepdims=True) + NORM_EPS)


def _me():
    return lax.axis_index("x"), lax.axis_index("y"), lax.axis_index("c")


def _peer(mask):
    x, y, c = _me()
    fx, fy, fc = (mask >> 2) & 1, (mask >> 1) & 1, mask & 1
    return (x ^ fx if fx else x, y ^ fy if fy else y, c ^ fc if fc else c)


def _chip_of(pos):
    return pos[0] * 2 + pos[1]


SIBLING_COLLECTIVE_ID = 0
SIBLING_ONLY = pltpu.CompilerParams(collective_id=SIBLING_COLLECTIVE_ID)


def _sibling_handshake():
    barrier = pltpu.get_barrier_semaphore()
    pl.semaphore_signal(barrier, inc=1, device_id=_peer(1), device_id_type=MESH)
    pl.semaphore_wait(barrier, 1)


CHIP_MASKS = (4, 2, 6)
ALL_MASKS = (1, 2, 3, 4, 5, 6, 7)


HBM_SPEC = pl.BlockSpec(memory_space=pltpu.HBM)
SEM_SPEC = pl.BlockSpec(memory_space=pltpu.SEMAPHORE)
SPLIT_COPY = pltpu.CompilerParams(has_side_effects=pltpu.SideEffectType.DATAFLOW_SIDE_EFFECTING)
N_BIG = 6
FULL_SHAPES = (
    (2, D_MODEL // 2, IN_W),
    (N_CHIPS, 2, SHARD_ROWS // 2, D_MODEL), (N_CHIPS, 2, SHARD_ROWS // 2, D_MODEL), (N_CHIPS, 2, SHARD_ROWS // 2, D_MODEL),
    (RNN_BLOCKS, N_CHIPS, 2, SHARD_RG // 2, RNN_BW), (RNN_BLOCKS, N_CHIPS, 2, SHARD_RG // 2, RNN_BW),
)


def _slot(full, idx, chip, half):
    if idx == 0:
        return full.at[half, :, pl.ds(pl.multiple_of(chip * SHARD_IN, 128), SHARD_IN)]
    return full.at[chip, half] if idx in (1, 2, 3) else full.at[:, chip, half]


def _three_halves(full, idx):
    return full.at[pl.ds(0, 3), 0] if idx in (1, 2, 3) else full.at[:, pl.ds(0, 3), 0]


def _gather_start(fulls, after):
    def body(*refs):
        full_refs = refs[:N_BIG]
        ssems, rsems = refs[N_BIG + 1:N_BIG + 5], refs[N_BIG + 5:N_BIG + 9]
        token = refs[2 * N_BIG + 9]
        me = _me()
        my_chip = _chip_of(me)
        for idx in range(N_BIG):
            for k, mask in enumerate(CHIP_MASKS):
                pair = k if idx == 0 else 3
                mine = _slot(full_refs[idx], idx, my_chip, me[2])
                pltpu.make_async_remote_copy(src_ref=mine, dst_ref=mine, send_sem=ssems[pair], recv_sem=rsems[pair],
                                             device_id=_peer(mask), device_id_type=MESH).start()
        token[...] = jnp.zeros_like(token)

    sem = pltpu.SemaphoreType.DMA(())
    out_shape = (sem,) * 8 + tuple(pltpu.HBM(f.shape, f.dtype) for f in fulls) + (jax.ShapeDtypeStruct((8, 128), F32),)
    outs = pl.pallas_call(
        body, out_shape=out_shape, name="gather_start",
        in_specs=[HBM_SPEC] * N_BIG + [ANY], out_specs=tuple([SEM_SPEC] * 8 + [HBM_SPEC] * N_BIG + [VMEM_SPEC]),
        input_output_aliases={i: 8 + i for i in range(N_BIG)}, compiler_params=SPLIT_COPY,
    )(*[pltpu.with_memory_space_constraint(f, pltpu.HBM) for f in fulls], after)
    return outs[0:4], outs[4:8], outs[8:8 + N_BIG], outs[8 + N_BIG]


def _gather_wait(ssem, rsem, arrays, idxs, after, tag):
    n = len(arrays)

    def body(*refs):
        full_refs, ssem_ref, rsem_ref = refs[:n], refs[n], refs[n + 1]
        me = _me()
        for full, idx in zip(full_refs, idxs):
            region = _slot(full, 0, _chip_of(me), me[2]) if idx == 0 else _three_halves(full, idx)
            arrived = pltpu.make_async_remote_copy(
                src_ref=region, dst_ref=region, send_sem=ssem_ref, recv_sem=rsem_ref, device_id=me, device_id_type=MESH)
            arrived.wait_send()
            arrived.wait_recv()

    outs = pl.pallas_call(
        body, out_shape=tuple(pltpu.HBM(a.shape, a.dtype) for a in arrays), name=f"gather_wait_{tag}",
        in_specs=[HBM_SPEC] * n + [SEM_SPEC, SEM_SPEC, ANY], out_specs=tuple([HBM_SPEC] * n),
        input_output_aliases={i: i for i in range(n)}, compiler_params=SPLIT_COPY,
    )(*arrays, ssem, rsem, after)
    return list(outs)


def _forward_halves(arrays, items, tag):
    n, m = len(arrays), len(items)

    def body(*refs):
        outs, ssem, rsem = refs[n:2 * n], refs[2 * n], refs[2 * n + 1]
        me = _me()
        sib = _peer(1)
        _sibling_handshake()
        cps = []
        for j, (pos, idx, k) in enumerate(items):
            chip = _chip_of(_peer(CHIP_MASKS[k]))
            cp = pltpu.make_async_remote_copy(
                src_ref=_slot(outs[pos], idx, chip, me[2]), dst_ref=_slot(outs[pos], idx, chip, me[2]),
                send_sem=ssem.at[j], recv_sem=rsem.at[j], device_id=sib, device_id_type=MESH)
            cp.start()
            cps.append(cp)
        for j, (pos, idx, k) in enumerate(items):
            chip = _chip_of(_peer(CHIP_MASKS[k]))
            pltpu.make_async_remote_copy(
                src_ref=_slot(outs[pos], idx, chip, me[2]), dst_ref=_slot(outs[pos], idx, chip, 1 - me[2]),
                send_sem=ssem.at[j], recv_sem=rsem.at[j], device_id=sib, device_id_type=MESH).wait_recv()
        for cp in cps:
            cp.wait_send()

    outs = pl.pallas_call(
        body, out_shape=tuple(jax.ShapeDtypeStruct(a.shape, a.dtype) for a in arrays), name=f"forward_halves_{tag}",
        in_specs=[ANY] * n, out_specs=tuple([ANY] * n), input_output_aliases={i: i for i in range(n)},
        scratch_shapes=[pltpu.SemaphoreType.DMA((m,)), pltpu.SemaphoreType.DMA((m,))], compiler_params=SIBLING_ONLY,
    )(*arrays)
    return list(outs)


def _gather_mod(c_row, w_ada_s, conv_w_s):
    def body(c_ref, wada_ref, cw_s, cw_f, call_ref, mod_ref, wsend, wrecv, lsem, csend, crecv, msend, mrecv):
        me = _me()
        my_chip = _chip_of(me)
        my_dev = my_chip * 2 + me[2]
        sends = []
        for k, mask in enumerate(CHIP_MASKS):
            cp = pltpu.make_async_remote_copy(src_ref=cw_s, dst_ref=cw_f.at[my_chip], send_sem=wsend.at[k], recv_sem=wrecv.at[k],
                                              device_id=_peer(mask), device_id_type=MESH)
            cp.start()
            sends.append(cp)
        local = [pltpu.make_async_copy(cw_s, cw_f.at[my_chip], lsem.at[0])]
        for cp in local:
            cp.start()

        call_ref[my_dev] = c_ref[0]
        csends = []
        for k, mask in enumerate(ALL_MASKS):
            cp = pltpu.make_async_remote_copy(
                src_ref=c_ref.at[0], dst_ref=call_ref.at[my_dev],
                send_sem=csend.at[k], recv_sem=crecv.at[k], device_id=_peer(mask), device_id_type=MESH)
            cp.start()
            csends.append(cp)
        for k, mask in enumerate(ALL_MASKS):
            frm = _peer(mask)
            pltpu.make_async_remote_copy(
                src_ref=c_ref.at[0], dst_ref=call_ref.at[_chip_of(frm) * 2 + frm[2]],
                send_sem=csend.at[k], recv_sem=crecv.at[k], device_id=frm, device_id_type=MESH).wait_recv()
        for cp in csends:
            cp.wait_send()

        c_all = call_ref[...].reshape(N_DEV, D_MODEL).astype(BF16)
        mod_ref[my_chip] = _dot(c_all, wada_ref[...].astype(BF16))
        msends = []
        for k, mask in enumerate(CHIP_MASKS):
            cp = pltpu.make_async_remote_copy(
                src_ref=mod_ref.at[my_chip], dst_ref=mod_ref.at[my_chip],
                send_sem=msend.at[k], recv_sem=mrecv.at[k], device_id=_peer(mask), device_id_type=MESH)
            cp.start()
            msends.append(cp)
        for k, mask in enumerate(CHIP_MASKS):
            frm = _peer(mask)
            pltpu.make_async_remote_copy(
                src_ref=mod_ref.at[my_chip], dst_ref=mod_ref.at[_chip_of(frm)],
                send_sem=msend.at[k], recv_sem=mrecv.at[k], device_id=frm, device_id_type=MESH).wait_recv()
        for cp in msends:
            cp.wait_send()

        for k, mask in enumerate(CHIP_MASKS):
            frm = _peer(mask)
            pltpu.make_async_remote_copy(src_ref=cw_s, dst_ref=cw_f.at[_chip_of(frm)], send_sem=wsend.at[k], recv_sem=wrecv.at[k],
                                         device_id=frm, device_id_type=MESH).wait_recv()
        for cp in sends:
            cp.wait_send()
        for cp in local:
            cp.wait()

    out_shape = (
        jax.ShapeDtypeStruct((N_CHIPS, CONV_W, D_MODEL // N_CHIPS), F32),
        jax.ShapeDtypeStruct((N_DEV, 1, D_MODEL), F32),
        jax.ShapeDtypeStruct((N_CHIPS, N_DEV, SHARD_ADA), F32),
    )
    return pl.pallas_call(
        body, out_shape=out_shape, name="gather_mod",
        in_specs=[VMEM_SPEC, VMEM_SPEC, ANY], out_specs=(ANY, VMEM_SPEC, VMEM_SPEC),
        scratch_shapes=[
            pltpu.SemaphoreType.DMA((3,)), pltpu.SemaphoreType.DMA((3,)), pltpu.SemaphoreType.DMA((1,)),
            pltpu.SemaphoreType.DMA((7,)), pltpu.SemaphoreType.DMA((7,)),
            pltpu.SemaphoreType.DMA((3,)), pltpu.SemaphoreType.DMA((3,)),
        ],
        compiler_params=pltpu.CompilerParams(vmem_limit_bytes=VMEM_LIMIT_V7X),
    )(c_row, w_ada_s, conv_w_s)


def _cast_place(shards, chip_idx, places):
    n = len(shards)

    def body(chip_ref, *refs):
        for s_ref, o_ref in zip(refs[:n], refs[n:]):
            o_ref[...] = s_ref[...].astype(BF16)

    grid_spec = pltpu.PrefetchScalarGridSpec(
        num_scalar_prefetch=1, grid=(1,),
        in_specs=[pl.BlockSpec(s.shape, lambda i, chip_ref, nd=s.ndim: (0,) * nd) for s in shards],
        out_specs=tuple(pl.BlockSpec(block, lambda i, chip_ref, im=im: im(chip_ref[0])) for _, block, im in places))
    return pl.pallas_call(
        body, out_shape=tuple(jax.ShapeDtypeStruct(full, BF16) for full, _, _ in places), grid_spec=grid_spec,
        name="cast_place", compiler_params=_cp("arbitrary"),
    )(chip_idx, *_in_hbm(*shards))


def _shard_of(ref, kind, chip):
    if kind == "in":
        return ref.at[:, pl.ds(pl.multiple_of(chip * SHARD_IN, 128), SHARD_IN)]
    return ref.at[chip] if kind == "sq" else ref.at[:, chip]


def _land_shape(src, kind):
    if kind == "in":
        return (3, src.shape[0], SHARD_IN)
    return (3,) + src.shape[1:] if kind == "sq" else (3, src.shape[0]) + src.shape[2:]


def _exchange_start(srcs, kinds, tag):
    n = len(srcs)
    lands = [pltpu.with_memory_space_constraint(lax.empty(_land_shape(s, k), s.dtype), pltpu.HBM) for s, k in zip(srcs, kinds)]

    def body(*refs):
        src_refs, land_refs = refs[:n], refs[n:2 * n]
        ssems, rsems = refs[2 * n:3 * n], refs[3 * n:4 * n]
        token = refs[6 * n]
        for i in range(n):
            for k, mask in enumerate(CHIP_MASKS):
                to = _peer(mask)
                pltpu.make_async_remote_copy(
                    src_ref=_shard_of(src_refs[i], kinds[i], _chip_of(to)), dst_ref=land_refs[i].at[k],
                    send_sem=ssems[i], recv_sem=rsems[i], device_id=to, device_id_type=MESH).start()
        token[...] = jnp.zeros_like(token)

    sem = pltpu.SemaphoreType.DMA(())
    out_shape = ((sem,) * (2 * n) + tuple(pltpu.HBM(s.shape, s.dtype) for s in srcs)
                 + tuple(pltpu.HBM(l.shape, l.dtype) for l in lands) + (jax.ShapeDtypeStruct((8, 128), F32),))
    outs = pl.pallas_call(
        body, out_shape=out_shape, name=f"exchange_start_{tag}",
        in_specs=[HBM_SPEC] * (2 * n), out_specs=tuple([SEM_SPEC] * (2 * n) + [HBM_SPEC] * (2 * n) + [VMEM_SPEC]),
        input_output_aliases={i: 2 * n + i for i in range(2 * n)},
        compiler_params=pltpu.CompilerParams(has_side_effects=pltpu.SideEffectType.DATAFLOW_SIDE_EFFECTING),
    )(*[pltpu.with_memory_space_constraint(s, pltpu.HBM) for s in srcs], *lands)
    return outs[:n], outs[n:2 * n], outs[2 * n:3 * n], outs[3 * n:4 * n], outs[4 * n]


def _exchange_wait(ssems, rsems, srcs, lands, after, tag):
    n = len(srcs)

    def body(*refs):
        land_refs = refs[n:2 * n]
        ssem_refs, rsem_refs = refs[2 * n:3 * n], refs[3 * n:4 * n]
        for i in range(n):
            all_three = pltpu.make_async_remote_copy(
                src_ref=land_refs[i], dst_ref=land_refs[i], send_sem=ssem_refs[i], recv_sem=rsem_refs[i],
                device_id=_me(), device_id_type=MESH)
            all_three.wait_send()
            all_three.wait_recv()

    outs = pl.pallas_call(
        body, out_shape=tuple(pltpu.HBM(a.shape, a.dtype) for a in list(srcs) + list(lands)), name=f"exchange_wait_{tag}",
        in_specs=[HBM_SPEC] * (2 * n) + [SEM_SPEC] * (2 * n) + [ANY], out_specs=tuple([HBM_SPEC] * (2 * n)),
        input_output_aliases={i: i for i in range(2 * n)},
        compiler_params=pltpu.CompilerParams(has_side_effects=pltpu.SideEffectType.DATAFLOW_SIDE_EFFECTING),
    )(*srcs, *lands, *ssems, *rsems, after)
    return outs[:n], outs[n:]


def _gather_small_start(slabs):
    def body(slabs_ref, ssem, rsem, slabs_out, token):
        me = _me()
        mine = slabs_ref.at[_chip_of(me) * 2 + me[2]]
        for mask in ALL_MASKS:
            pltpu.make_async_remote_copy(src_ref=mine, dst_ref=mine, send_sem=ssem, recv_sem=rsem,
                                         device_id=_peer(mask), device_id_type=MESH).start()
        token[...] = jnp.zeros_like(token)

    sem = pltpu.SemaphoreType.DMA(())
    return pl.pallas_call(
        body, out_shape=(sem, sem, pltpu.HBM(slabs.shape, slabs.dtype), jax.ShapeDtypeStruct((8, 128), F32)),
        name="gather_small_start", in_specs=[HBM_SPEC], out_specs=(SEM_SPEC, SEM_SPEC, HBM_SPEC, VMEM_SPEC),
        input_output_aliases={0: 2}, compiler_params=SPLIT_COPY,
    )(pltpu.with_memory_space_constraint(slabs, pltpu.HBM))


def _gather_small_wait(ssem, rsem, slabs, after):
    def body(slabs_ref, ssem_ref, rsem_ref, after_ref, slabs_out):
        seven = slabs_ref.at[pl.ds(0, N_DEV - 1)]
        arrived = pltpu.make_async_remote_copy(
            src_ref=seven, dst_ref=seven, send_sem=ssem_ref, recv_sem=rsem_ref, device_id=_me(), device_id_type=MESH)
        arrived.wait_send()
        arrived.wait_recv()

    return pl.pallas_call(
        body, out_shape=pltpu.HBM(slabs.shape, slabs.dtype), name="gather_small_wait",
        in_specs=[HBM_SPEC, SEM_SPEC, SEM_SPEC, ANY], out_specs=HBM_SPEC, input_output_aliases={0: 0},
        compiler_params=SPLIT_COPY,
    )(slabs, ssem, rsem, after)


def _half_of(ref, axis, half):
    return ref.at[(slice(None),) * axis + (half,)]


def _swap_halves(parts, axes):
    n = len(parts)

    def body(*refs):
        ins, outs, ssem, rsem = refs[:n], refs[n:2 * n], refs[2 * n], refs[2 * n + 1]
        c = lax.axis_index("c")
        _sibling_handshake()
        cps = [pltpu.make_async_remote_copy(src_ref=_half_of(ins[i], axes[i], 1 - c), dst_ref=outs[i], send_sem=ssem.at[i],
                                            recv_sem=rsem.at[i], device_id=_peer(1), device_id_type=MESH) for i in range(n)]
        for cp in cps:
            cp.start()
        for cp in cps:
            cp.wait()

    shapes = [p.shape[:a] + p.shape[a + 1:] for p, a in zip(parts, axes)]
    return pl.pallas_call(
        body, out_shape=tuple(jax.ShapeDtypeStruct(s, p.dtype) for s, p in zip(shapes, parts)), name="swap_halves",
        in_specs=[ANY] * n, out_specs=tuple([ANY] * n),
        scratch_shapes=[pltpu.SemaphoreType.DMA((n,)), pltpu.SemaphoreType.DMA((n,))], compiler_params=SIBLING_ONLY,
    )(*parts)


def _presum(mines, sibs, c_idx, tag):
    n = len(mines)
    S, _, R, C = mines[0].shape
    tr = min(R, 256)
    tc = SHARD_IN if C % SHARD_IN == 0 else (C // 2 if n > 1 and C % 256 == 0 else C)

    def body(c_ref, *refs):
        for k in range(n):
            total = refs[k][:, 0] + refs[n + k][...]
            refs[2 * n + k][...] = total
            refs[3 * n + k][...] = total.astype(BF16)

    out_spec = pl.BlockSpec((S, tr, tc), lambda i, j, c_ref: (0, i, j))
    grid_spec = pltpu.PrefetchScalarGridSpec(
        num_scalar_prefetch=1, grid=(R // tr, C // tc),
        in_specs=[pl.BlockSpec((S, 1, tr, tc), lambda i, j, c_ref: (0, c_ref[0], i, j))] * n + [out_spec] * n,
        out_specs=(out_spec,) * (2 * n))
    outs = pl.pallas_call(
        body, out_shape=(jax.ShapeDtypeStruct((S, R, C), F32),) * n + (jax.ShapeDtypeStruct((S, R, C), BF16),) * n,
        grid_spec=grid_spec, name=f"presum_{tag}", compiler_params=_cp("parallel", "parallel"),
    )(c_idx, *mines, *sibs)
    return list(outs[:n]), list(outs[n:])


def _assemble_with_sibling(parts, axes):
    n = len(parts)

    def body(*refs):
        outs, ssem, rsem = refs[n:2 * n], refs[2 * n], refs[2 * n + 1]
        c = lax.axis_index("c")
        _sibling_handshake()
        cps = [pltpu.make_async_remote_copy(
            src_ref=_half_of(outs[i], axes[i], c), dst_ref=_half_of(outs[i], axes[i], c), send_sem=ssem.at[i],
            recv_sem=rsem.at[i], device_id=_peer(1), device_id_type=MESH) for i in range(n)]
        for cp in cps:
            cp.start()
        for i in range(n):
            pltpu.make_async_remote_copy(
                src_ref=_half_of(outs[i], axes[i], c), dst_ref=_half_of(outs[i], axes[i], 1 - c), send_sem=ssem.at[i],
                recv_sem=rsem.at[i], device_id=_peer(1), device_id_type=MESH).wait_recv()
        for cp in cps:
            cp.wait_send()

    return pl.pallas_call(
        body, out_shape=tuple(jax.ShapeDtypeStruct(p.shape, p.dtype) for p in parts), name="assemble_with_sibling",
        in_specs=[ANY] * n, out_specs=tuple([ANY] * n), input_output_aliases={i: i for i in range(n)},
        scratch_shapes=[pltpu.SemaphoreType.DMA((n,)), pltpu.SemaphoreType.DMA((n,))], compiler_params=SIBLING_ONLY,
    )(*parts)


def _rope_lane_frequencies():
    inv = np.float32(ROPE_THETA) ** (-(np.arange(0, 2 * ROT_HALF, 2, dtype=np.float32)) / np.float32(2 * ROT_HALF))
    lane = np.arange(128) % HEAD_DIM
    return jnp.asarray(np.where(lane < 2 * ROT_HALF, inv[lane % ROT_HALF], 0.0).astype(np.float32)[None, :])


def _rope_tables(pos, freq):
    ang = pos.astype(F32) * freq
    c, s = jnp.cos(ang), jnp.sin(ang)
    m = lax.broadcasted_iota(jnp.int32, ang.shape, 1) & (HEAD_DIM - 1)
    return (jnp.where(m < 2 * ROT_HALF, c, 1.0), jnp.where(m < ROT_HALF, -s, 0.0),
            jnp.where((m >= ROT_HALF) & (m < 2 * ROT_HALF), s, 0.0))


def _columns(t):
    return [t[:, i:i + 128] for i in range(0, t.shape[-1], 128)]


def _rope(t, c, sa, sb):
    return jnp.concatenate(
        [x * c + pltpu.roll(x, 128 - ROT_HALF, 1) * sa + pltpu.roll(x, ROT_HALF, 1) * sb for x in _columns(t)], axis=1)


def _unrope(d, c, sa, sb):
    return jnp.concatenate(
        [x * c + pltpu.roll(x * sa, ROT_HALF, 1) + pltpu.roll(x * sb, 128 - ROT_HALF, 1) for x in _columns(d)], axis=1)


def _prenorm(x, mod_row, norm_g, pos_col):
    T = x.shape[0]
    tm = min(T, 512)

    def body(x_ref, mod_ref, g_ref, pos_ref, f_ref, h_ref, ht_ref, c_ref, sa_ref, sb_ref):
        xf = x_ref[...]
        shift, scale = mod_ref[:, 0:D_MODEL], mod_ref[:, D_MODEL:2 * D_MODEL]
        h = (xf * _rms(xf)) * g_ref[...] * (1.0 + scale) + shift
        h_ref[...] = h.astype(BF16)
        ht_ref[...] = h.T.astype(BF16)
        c_ref[...], sa_ref[...], sb_ref[...] = _rope_tables(pos_ref[...], f_ref[...])

    tab = jax.ShapeDtypeStruct((T, 128), F32)
    tok = lambda w: pl.BlockSpec((tm, w), lambda i: (i, 0))
    row = lambda w: pl.BlockSpec((1, w), lambda i: (0, 0))
    outs = pl.pallas_call(
        body, out_shape=(jax.ShapeDtypeStruct((T, D_MODEL), BF16), jax.ShapeDtypeStruct((D_MODEL, T), BF16), tab, tab, tab),
        grid=(T // tm,), name="prenorm",
        in_specs=[tok(D_MODEL), row(ADA_W), row(D_MODEL), tok(1), row(128)],
        out_specs=(tok(D_MODEL), pl.BlockSpec((D_MODEL, tm), lambda i: (0, i)), tok(128), tok(128), tok(128)),
        compiler_params=_cp("parallel"),
    )(x, *_in_hbm(mod_row, norm_g), pos_col, _rope_lane_frequencies())
    return outs[0], outs[1], tuple(outs[2:])


def _in_projection(h, w_in, chips, into, tag):
    T = h.shape[0]
    tm, tn = min(T, 512), SHARD_IN
    k = chips.shape[0]

    def body(chip_ref, h_ref, w_ref, *rest):
        rest[-1][...] = _dot(h_ref[...], w_ref[...])

    w_spec = pl.BlockSpec((D_MODEL, tn), lambda s, i, c: (0, c[s]), **({"pipeline_mode": pl.Buffered(1)} if k == 1 else {}))
    in_specs = [pl.BlockSpec((tm, D_MODEL), lambda s, i, c: (i, 0)), w_spec]
    args = [chips, h, w_in]
    aliases = {}
    if into is not None:
        in_specs.append(ANY)
        args.append(into)
        aliases = {3: 0}
    grid_spec = pltpu.PrefetchScalarGridSpec(num_scalar_prefetch=1, grid=(k, T // tm), in_specs=in_specs,
                                             out_specs=pl.BlockSpec((tm, tn), lambda s, i, c: (i, c[s])))
    return pl.pallas_call(
        body, out_shape=jax.ShapeDtypeStruct((T, IN_W), F32), grid_spec=grid_spec, name=f"in_projection_{tag}",
        input_output_aliases=aliases, compiler_params=_cp("parallel", "parallel"),
    )(*args)


def _attn_mask(n):
    qi = lax.broadcasted_iota(jnp.int32, (GROUP * BLOCK, BLOCK), 0) & (BLOCK - 1)
    j = lax.broadcasted_iota(jnp.int32, (GROUP * BLOCK, BLOCK), 1)
    own = j <= qi
    return own, jnp.logical_not(own) & (n == 0)


def _fold(x, own):
    return jnp.where(own, x[:, BLOCK:2 * BLOCK], x[:, 0:BLOCK])


def _unfold(xf, own):
    zero = jnp.zeros_like(xf)
    return jnp.concatenate([jnp.where(own, zero, xf), jnp.where(own, xf, zero)], axis=1)


ROW_GROUP_HEAD = (0, 2, 1, 3)


def _sink_col(sink_ref, kh):
    rowg = lax.broadcasted_iota(jnp.int32, (GROUP * BLOCK, 1), 0) // BLOCK
    col = jnp.full((GROUP * BLOCK, 1), sink_ref[0, GROUP * kh + ROW_GROUP_HEAD[0]], F32)
    for g in range(1, GROUP):
        col = jnp.where(rowg == g, sink_ref[0, GROUP * kh + ROW_GROUP_HEAD[g]], col)
    return col


def _low_lanes(shape):
    return lax.broadcasted_iota(jnp.int32, shape, 1) < HEAD_DIM


def _kv_pair_operand(prev, cur, kh):
    c = 128 * (kh // 2)
    col = jnp.concatenate([prev[:, c:c + 128], cur[:, c:c + 128]], axis=0).astype(F32)
    if kh % 2 == 0:
        lo = jnp.where(_low_lanes(col.shape), col, 0.0)
        hi = pltpu.roll(lo, HEAD_DIM, 1)
    else:
        hi = jnp.where(_low_lanes(col.shape), 0.0, col)
        lo = pltpu.roll(hi, HEAD_DIM, 1)
    return jnp.concatenate([lo, hi], axis=0).astype(BF16)


def _pair_rows(x, kh):
    c = 2 * 128 * kh
    return jnp.concatenate([x[:, c:c + 128], x[:, c + 128:c + 256]], axis=0)


def _restack(big):
    return jnp.concatenate([big[:, 0:2 * BLOCK], big[:, 2 * BLOCK:4 * BLOCK]], axis=0)


def _unrestack(stacked):
    return jnp.concatenate([stacked[0:2 * BLOCK], stacked[2 * BLOCK:4 * BLOCK]], axis=1)


def _fold_pair(x2, kh):
    low = _low_lanes((2 * BLOCK, 128))
    mixed = jnp.where(low, x2[0:2 * BLOCK], x2[2 * BLOCK:4 * BLOCK])
    total = mixed + pltpu.roll(mixed, HEAD_DIM, 1)
    return jnp.where(low, total, 0.0) if kh % 2 == 0 else jnp.where(low, 0.0, total)


def _attn_scores(qr, k2, kh):
    q2 = _pair_rows(qr, kh).astype(BF16)
    return q2, _restack(_dot_nt(q2, k2))


def _attn_softmax(s, sink_col, mask):
    own, no_key = mask
    s = jnp.where(no_key, -1e30, _fold(s, own))
    m = jnp.maximum(jnp.max(s, axis=-1, keepdims=True), sink_col)
    p = jnp.exp(s - m)
    p_sink = jnp.exp(sink_col - m)
    denom = jnp.sum(p, axis=-1, keepdims=True) + p_sink
    return p / denom, p_sink / denom


def _attn_forward(proj, tabs, sinks):
    T = proj.shape[0]
    nb = T // BLOCK

    def body(q_ref, kvc_ref, kvp_ref, g0_ref, g1_ref, cc, sac, sbc, cp_, sap, sbp, sink_ref, y_ref, qrb_ref, krb_ref, p_ref):
        n = pl.program_id(0)
        tc = tcur = (cc[...], sac[...], sbc[...])
        tprev = (cp_[...], sap[...], sbp[...])
        qr = _rope(q_ref[...], *tc) * ATTN_SCALE
        kr_cur = _rope(kvc_ref[:, 0:KV_W], *tcur)
        kr_prev = _rope(kvp_ref[:, 0:KV_W], *tprev)
        qrb_ref[...] = qr.astype(BF16)
        krb_ref[...] = kr_cur.astype(BF16)
        v_cur, v_prev = kvc_ref[:, KV_W:2 * KV_W], kvp_ref[:, KV_W:2 * KV_W]
        mask = _attn_mask(n)
        outs = []
        k2s = [_kv_pair_operand(kr_prev, kr_cur, kh) for kh in range(N_KV)]
        v2s = [_kv_pair_operand(v_prev, v_cur, kh) for kh in range(N_KV)]
        scores = [_attn_scores(qr, k2s[kh], kh) for kh in range(N_KV)]
        p_parts = []
        for kh in range(N_KV):
            pn, _ = _attn_softmax(scores[kh][1], _sink_col(sink_ref, kh), mask)
            p_parts += [pn[g * BLOCK:(g + 1) * BLOCK] for g in range(GROUP)]
            o_big = _dot(_unrestack(_unfold(pn.astype(BF16), mask[0])), v2s[kh])
            outs += [o_big[0:BLOCK], o_big[BLOCK:2 * BLOCK]]
        p_ref[...] = jnp.concatenate(p_parts, axis=1)
        o = jnp.concatenate(outs, axis=1)
        g = jnp.concatenate([g0_ref[...], g1_ref[...]], axis=1)
        y_ref[...] = (o * (g * _sigmoid(g))).astype(BF16)

    def blk(w, cb):
        return pl.BlockSpec((BLOCK, w), lambda n, cb=cb: (n, cb))

    prev = lambda w, cb: pl.BlockSpec((BLOCK, w), lambda n, cb=cb: (jnp.maximum(n - 1, 0), cb))
    return pl.pallas_call(
        body, grid=(nb,), name="attn_forward",
        out_shape=(jax.ShapeDtypeStruct((T, D_MODEL), BF16), jax.ShapeDtypeStruct((T, D_MODEL), BF16),
                   jax.ShapeDtypeStruct((T, KV_W), BF16), jax.ShapeDtypeStruct((T, N_HEADS * BLOCK), F32)),
        in_specs=[blk(D_MODEL, 0), blk(CB, CB_KV), prev(CB, CB_KV), blk(CB, CB_GA), blk(CB, CB_GA + 1),
                  blk(128, 0), blk(128, 0), blk(128, 0), prev(128, 0), prev(128, 0), prev(128, 0),
                  pl.BlockSpec(memory_space=pltpu.SMEM)],
        out_specs=(blk(D_MODEL, 0), blk(D_MODEL, 0), blk(KV_W, 0), blk(N_HEADS * BLOCK, 0)),
        compiler_params=_cp("parallel"),
    )(proj, proj, proj, proj, proj, *tabs, *tabs, sinks)


def _scan_rows8():
    return lax.broadcasted_iota(jnp.int32, (8, D_MODEL), 0)


def _scan_forward(a_ref, b_ref, h_ref, carry, rows):
    row = _scan_rows8()

    def group(i, carry):
        off = pl.multiple_of(i * 8, 8)
        a, b = a_ref[pl.ds(off, 8), :], b_ref[pl.ds(off, 8), :]
        for d in (1, 2, 4):
            ok = row >= d
            b = jnp.where(ok, a * pltpu.roll(b, d, 0) + b, b)
            a = jnp.where(ok, a * pltpu.roll(a, d, 0), a)
        h = a * carry + b
        h_ref[pl.ds(off, 8), :] = h
        return h[7:8, :]

    return lax.fori_loop(0, rows // 8, group, carry)


def _scan_backward(a_ref, g_ref, lam_ref, carry, rows):
    row = _scan_rows8()

    def group(i, carry):
        off = pl.multiple_of((rows // 8 - 1 - i) * 8, 8)
        a, g = a_ref[pl.ds(off, 8), :], g_ref[pl.ds(off, 8), :]
        b = a * g
        for d in (1, 2, 4):
            ok = row < 8 - d
            b = jnp.where(ok, a * pltpu.roll(b, 8 - d, 0) + b, b)
            a = jnp.where(ok, a * pltpu.roll(a, 8 - d, 0), a)
        mu = a * carry + b
        mu_below = jnp.where(row == 7, carry, pltpu.roll(mu, 7, 0))
        lam_ref[pl.ds(off, 8), :] = g + mu_below
        return mu[0:1, :]

    return lax.fori_loop(0, rows // 8, group, carry)


def _conv_taps(xbuf, xr, tail):
    rows = xr.shape[0]
    xbuf[0:8, :] = tail
    xbuf[8:rows + 8, :] = xr
    return [xbuf[pl.ds(8 - (CONV_W - 1 - k), rows), :] for k in range(CONV_W - 1)] + [xr]


def _rnn_gates(xbuf, xr, tail, cw, cb, wa_ref, wx_ref, ba, bx, sp, reset):
    xs = _conv_taps(xbuf, xr, tail)
    xc = xs[0] * cw[0:1, :]
    for k in range(1, CONV_W):
        xc = xc + xs[k] * cw[k:k + 1, :]
    xc = xc + cb
    xcb = xc.astype(BF16)
    za = jnp.concatenate([_dot(xcb[:, RNN_BW * j:RNN_BW * (j + 1)], wa_ref[j]) for j in range(RNN_BLOCKS)], axis=1) + ba
    zx = jnp.concatenate([_dot(xcb[:, RNN_BW * j:RNN_BW * (j + 1)], wx_ref[j]) for j in range(RNN_BLOCKS)], axis=1) + bx
    r, i = _sigmoid(za), _sigmoid(zx)
    neg_log_a = LRU_C * r * sp
    a_raw = jnp.exp(-neg_log_a)
    mult_raw = jnp.sqrt(jnp.tanh(neg_log_a) * (1.0 + a_raw * a_raw))
    a = jnp.where(reset, 0.0, a_raw)
    mult = jnp.where(reset, 1.0, mult_raw)
    return xc, r, i, a, mult


def _rnn_forward(proj, pos_col, conv_w, conv_b, rwa, rwx, ba, bx, lam):
    T = proj.shape[0]
    tr = min(T, 256)

    def body(x0, x1, g0, g1, pos_ref, cw_ref, cb_ref, wa_ref, wx_ref, ba_ref, bx_ref, lam_ref,
             y_ref, h_ref, xc_ref, r_ref, i_ref, a_ref, mult_ref, xbuf, bbuf, tail, carry):
        t = pl.program_id(0)

        @pl.when(t == 0)
        def _():
            tail[...] = jnp.zeros_like(tail)
            carry[...] = jnp.zeros_like(carry)

        xr = jnp.concatenate([x0[...], x1[...]], axis=1)
        sp = _softplus(-lam_ref[...])
        reset = pos_ref[...] == 0
        xc, r, i, a, mult = _rnn_gates(
            xbuf, xr, tail[...], cw_ref[...], cb_ref[...], wa_ref, wx_ref, ba_ref[...], bx_ref[...], sp, reset)
        xc_ref[...] = xc
        r_ref[...] = r
        i_ref[...] = i
        a_ref[...] = a
        mult_ref[...] = mult
        bbuf[...] = mult * (i * xc)
        last = _scan_forward(a_ref, bbuf, h_ref, carry[0:1, :], tr)
        carry[...] = jnp.broadcast_to(last, carry.shape)
        tail[...] = xr[tr - 8:tr, :]
        g = jnp.concatenate([g0[...], g1[...]], axis=1)
        y_ref[...] = (h_ref[...] * (g * _sigmoid(g))).astype(BF16)

    blk = lambda cb: pl.BlockSpec((tr, CB), lambda t, cb=cb: (t, cb))
    row = lambda w: pl.BlockSpec((1, w), lambda t: (0, 0))
    full3 = pl.BlockSpec((RNN_BLOCKS, RNN_BW, RNN_BW), lambda t: (0, 0, 0))
    tok = pl.BlockSpec((tr, D_MODEL), lambda t: (t, 0))
    act = jax.ShapeDtypeStruct((T, D_MODEL), F32)
    return pl.pallas_call(
        body, out_shape=(jax.ShapeDtypeStruct((T, D_MODEL), BF16),) + (act,) * 6,
        grid=(T // tr,), name="rnn_forward",
        in_specs=[blk(CB_XR), blk(CB_XR + 1), blk(CB_GR), blk(CB_GR + 1), pl.BlockSpec((tr, 1), lambda t: (t, 0)),
                  pl.BlockSpec((CONV_W, D_MODEL), lambda t: (0, 0)), row(D_MODEL), full3, full3,
                  row(D_MODEL), row(D_MODEL), row(D_MODEL)],
        out_specs=(tok,) * 7,
        scratch_shapes=[pltpu.VMEM((tr + 8, D_MODEL), F32), pltpu.VMEM((tr, D_MODEL), F32),
                        pltpu.VMEM((8, D_MODEL), F32), pltpu.VMEM((8, D_MODEL), F32)],
        compiler_params=_cp("arbitrary"),
    )(proj, proj, proj, proj, pos_col, *_in_hbm(conv_w, conv_b, rwa, rwx, ba, bx, lam))


def _merge_and_head(x, target, y_attn, y_rnn, proj, wap, wrp, wo, mod_row, final_g):
    T = x.shape[0]
    tm = min(T, 256)

    def body(x_ref, t_ref, ya_ref, yr_ref, ma0, ma1, mr0, mr1, wap_ref, wrp_ref, wo_ref, mod_ref, fg_ref,
             dx2_ref, mg_ref, do_ref, dpa_ref, dpr_ref, dya_ref, dyr_ref, dc_ref, dfg_ref, dgate_ref, loss_ref):
        i = pl.program_id(0)
        gate = mod_ref[:, 2 * D_MODEL:3 * D_MODEL]
        fg = fg_ref[...]
        pa, pr = _dot(ya_ref[...], wap_ref[...]), _dot(yr_ref[...], wrp_ref[...])
        sa = _sigmoid(jnp.concatenate([ma0[...], ma1[...]], axis=1))
        sr = _sigmoid(jnp.concatenate([mr0[...], mr1[...]], axis=1))
        mb = (sa * pa + sr * pr).astype(BF16)
        o = _dot(mb, wo_ref[...])
        x2 = x_ref[...] + gate * o
        r2 = _rms(x2)
        xn2 = x2 * r2
        err = xn2 * fg - t_ref[...]
        loss_t = 0.5 * jnp.sum(jnp.sum(err * err, axis=-1, keepdims=True) * (1.0 / D_MODEL), axis=0, keepdims=True)
        dy = err * (1.0 / D_MODEL)
        dfg_t = jnp.sum(dy * xn2, axis=0, keepdims=True)
        dxn = dy * fg
        dx2 = r2 * (dxn - xn2 * jnp.mean(dxn * xn2, axis=-1, keepdims=True))
        dgate_t = jnp.sum(dx2 * o, axis=0, keepdims=True)
        dob = (dx2 * gate).astype(BF16)
        dmerged = _dot_nt(dob, wo_ref[...])
        dpa, dpr = (dmerged * sa).astype(BF16), (dmerged * sr).astype(BF16)
        dya, dyr = _dot_nt(dpa, wap_ref[...]), _dot_nt(dpr, wrp_ref[...])
        dx2_ref[...] = dx2
        mg_ref[...] = mb
        do_ref[...] = dob
        dpa_ref[...] = dpa
        dpr_ref[...] = dpr
        dya_ref[...] = dya
        dyr_ref[...] = dyr
        dc_ref[:, 0:D_MODEL] = (dmerged * pa * sa * (1.0 - sa)).astype(BF16)
        dc_ref[:, D_MODEL:2 * D_MODEL] = (dmerged * pr * sr * (1.0 - sr)).astype(BF16)

        @pl.when(i == 0)
        def _():
            dfg_ref[...] = jnp.zeros_like(dfg_ref)
            dgate_ref[...] = jnp.zeros_like(dgate_ref)
            loss_ref[...] = jnp.zeros_like(loss_ref)

        dfg_ref[...] += dfg_t
        dgate_ref[...] += dgate_t
        loss_ref[...] += jnp.broadcast_to(loss_t, loss_ref.shape)

    tok = lambda w: pl.BlockSpec((tm, w), lambda i: (i, 0))
    blk = lambda cb: pl.BlockSpec((tm, CB), lambda i, cb=cb: (i, cb))
    wfull = pl.BlockSpec((D_MODEL, D_MODEL), lambda i: (0, 0), pipeline_mode=pl.Buffered(1))
    row = lambda w: pl.BlockSpec((1, w), lambda i: (0, 0))
    out_shape = (
        jax.ShapeDtypeStruct((T, D_MODEL), F32), jax.ShapeDtypeStruct((T, D_MODEL), BF16),
        jax.ShapeDtypeStruct((T, D_MODEL), BF16), jax.ShapeDtypeStruct((T, D_MODEL), BF16),
        jax.ShapeDtypeStruct((T, D_MODEL), BF16), jax.ShapeDtypeStruct((T, D_MODEL), F32),
        jax.ShapeDtypeStruct((T, D_MODEL), F32), jax.ShapeDtypeStruct((T, 2 * D_MODEL), BF16),
        jax.ShapeDtypeStruct((1, D_MODEL), F32), jax.ShapeDtypeStruct((1, D_MODEL), F32),
        jax.ShapeDtypeStruct((1, 128), F32),
    )
    return pl.pallas_call(
        body, out_shape=out_shape, grid=(T // tm,), name="merge_and_head",
        in_specs=[tok(D_MODEL), tok(D_MODEL), tok(D_MODEL), tok(D_MODEL), blk(CB_MA), blk(CB_MA + 1), blk(CB_MR),
                  blk(CB_MR + 1), wfull, wfull, wfull, row(ADA_W), row(D_MODEL)],
        out_specs=(tok(D_MODEL),) * 7 + (tok(2 * D_MODEL), row(D_MODEL), row(D_MODEL), row(128)),
        compiler_params=_cp("arbitrary"),
    )(x, target, y_attn, y_rnn, proj, proj, proj, proj, wap, wrp, wo, *_in_hbm(mod_row, final_g))


def _attn_backward(proj, qr_b, kr_b, p_all, d_y, tabs, after):
    T = proj.shape[0]
    nb = T // BLOCK

    def body(qrb_ref, krc_ref, krp_ref, vc_ref, vp_ref, g0_ref, g1_ref, dy_ref, p_ref, cc, sac, sbc, cp_, sap, sbp, after_ref,
             dq_ref, dkv_ref, dg_ref, dsink_ref, carry):
        n = pl.program_id(0)

        @pl.when(n == 0)
        def _():
            carry[...] = jnp.zeros_like(carry)
            dsink_ref[...] = jnp.zeros_like(dsink_ref)

        @pl.when(n < nb)
        def _():
            tc = tcur = (cc[...], sac[...], sbc[...])
            tprev = (cp_[...], sap[...], sbp[...])
            qr, kr_cur, kr_prev = qrb_ref[...], krc_ref[...], krp_ref[...]
            v_cur, v_prev = vc_ref[...], vp_ref[...]
            g = jnp.concatenate([g0_ref[...], g1_ref[...]], axis=1)
            sg = _sigmoid(g)
            dy = dy_ref[...]
            d_o = dy * (g * sg)
            mask = _attn_mask(n)
            lane = lax.broadcasted_iota(jnp.int32, (1, 128), 1)
            rowg = lax.broadcasted_iota(jnp.int32, (GROUP * BLOCK, 1), 0) // BLOCK
            o_parts, dq_parts = [], []
            dk_cols, dv_cols = [None, None], [None, None]
            dsink = jnp.zeros((1, 128), F32)
            heads = range(N_KV)
            k2s = [_kv_pair_operand(kr_prev, kr_cur, kh) for kh in heads]
            v2s = [_kv_pair_operand(v_prev, v_cur, kh) for kh in heads]
            q2s = [_pair_rows(qr, kh).astype(BF16) for kh in heads]
            do2s = [_pair_rows(d_o, kh).astype(BF16) for kh in heads]
            dpns = [_fold(_restack(_dot_nt(do2s[kh], v2s[kh])), mask[0]) for kh in heads]
            pns = [jnp.concatenate([p_ref[:, BLOCK * (GROUP * kh + g):BLOCK * (GROUP * kh + g + 1)] for g in range(GROUP)], axis=0)
                   for kh in heads]
            probs = [(pn, 1.0 - jnp.sum(pn, axis=-1, keepdims=True)) for pn in pns]
            p_bigs = [_unrestack(_unfold(probs[kh][0].astype(BF16), mask[0])) for kh in heads]
            o_bigs = [_dot(p_bigs[kh], v2s[kh]) for kh in heads]
            dv2s = [_dot_tn(p_bigs[kh], do2s[kh]) for kh in heads]
            deltas = [jnp.sum(probs[kh][0] * dpns[kh], axis=-1, keepdims=True) for kh in heads]
            ds_bigs = [_unrestack(_unfold((probs[kh][0] * (dpns[kh] - deltas[kh])).astype(BF16), mask[0])) for kh in heads]
            dq2s = [_dot(ds_bigs[kh], k2s[kh]) for kh in heads]
            dk2s = [_dot_tn(ds_bigs[kh], q2s[kh]) for kh in heads]
            for kh in heads:
                o_parts += [o_bigs[kh][0:BLOCK], o_bigs[kh][BLOCK:2 * BLOCK]]
                dq_parts += [dq2s[kh][0:BLOCK], dq2s[kh][BLOCK:2 * BLOCK]]
                dk_c, dv_c = _fold_pair(dk2s[kh], kh), _fold_pair(dv2s[kh], kh)
                c = kh // 2
                dk_cols[c] = dk_c if dk_cols[c] is None else dk_cols[c] + dk_c
                dv_cols[c] = dv_c if dv_cols[c] is None else dv_cols[c] + dv_c
                ds_rows = probs[kh][1] * deltas[kh]
                for gq in range(GROUP):
                    val = -jnp.sum(jnp.where(rowg == gq, ds_rows, 0.0), axis=0, keepdims=True)
                    dsink = dsink + jnp.where(lane == GROUP * kh + ROW_GROUP_HEAD[gq], val, 0.0)
            o = jnp.concatenate(o_parts, axis=1)
            dg_ref[...] = (dy * o * (sg * (1.0 + g * (1.0 - sg)))).astype(BF16)
            dq_ref[...] = (_unrope(jnp.concatenate(dq_parts, axis=1), *tc) * ATTN_SCALE).astype(BF16)
            dk_all, dv_all = jnp.concatenate(dk_cols, axis=1), jnp.concatenate(dv_cols, axis=1)
            dk_prev = _unrope(dk_all[0:BLOCK], *tprev)
            dk_cur = _unrope(dk_all[BLOCK:2 * BLOCK], *tcur)
            dv_prev, dv_cur = dv_all[0:BLOCK], dv_all[BLOCK:2 * BLOCK]
            dkv_ref[...] = (carry[...] + jnp.concatenate([dk_prev, dv_prev], axis=1)).astype(BF16)
            carry[...] = jnp.concatenate([dk_cur, dv_cur], axis=1)
            dsink_ref[...] += dsink

        @pl.when(n == nb)
        def _():
            dkv_ref[...] = carry[...].astype(BF16)

    cur = lambda w, cb: pl.BlockSpec((BLOCK, w), lambda n, cb=cb: (jnp.minimum(n, nb - 1), cb))
    prev = lambda w, cb: pl.BlockSpec((BLOCK, w), lambda n, cb=cb: (jnp.maximum(jnp.minimum(n, nb - 1) - 1, 0), cb))
    out_shape = (jax.ShapeDtypeStruct((T, D_MODEL), BF16), jax.ShapeDtypeStruct((T, 2 * KV_W), BF16),
                 jax.ShapeDtypeStruct((T, D_MODEL), BF16), jax.ShapeDtypeStruct((1, 128), F32))
    return pl.pallas_call(
        body, out_shape=out_shape, grid=(nb + 1,), name="attn_backward",
        in_specs=[cur(D_MODEL, 0), cur(KV_W, 0), prev(KV_W, 0), cur(KV_W, V_COL_BLOCK), prev(KV_W, V_COL_BLOCK),
                  cur(CB, CB_GA), cur(CB, CB_GA + 1), cur(D_MODEL, 0), cur(N_HEADS * BLOCK, 0),
                  cur(128, 0), cur(128, 0), cur(128, 0), prev(128, 0), prev(128, 0), prev(128, 0),
                  pl.BlockSpec(memory_space=pltpu.SMEM)],
        out_specs=(cur(D_MODEL, 0), pl.BlockSpec((BLOCK, 2 * KV_W), lambda n: (jnp.maximum(n - 1, 0), 0)),
                   cur(D_MODEL, 0), pl.BlockSpec((1, 128), lambda n: (0, 0))),
        scratch_shapes=[pltpu.VMEM((BLOCK, 2 * KV_W), F32)],
        compiler_params=_cp("arbitrary"),
    )(qr_b, kr_b, kr_b, proj, proj, proj, proj, d_y, p_all, *tabs, *tabs, after)


def _rnn_backward(proj, pos_col, h_rnn, saved, d_y, conv_w, rwa, rwx, lam):
    T = proj.shape[0]
    tr = min(T, 256)
    nt = T // tr
    hb = tr // 8

    def body(x0, x1, xh0, xh1, g0, g1, pos_ref, h_ref, hh_ref, xc_ref, r_ref, i_ref, a_ref, mult_ref, dy_ref,
             cw_ref, wa_ref, wx_ref, lam_ref, db_ref, dcw_ref, dcb_ref, dwa_ref, dwx_ref, dba_ref, dbx_ref, dlam_ref,
             xbuf, hbuf, dbuf, gbuf, lbuf, mu_carry, dxc_head):
        step = pl.program_id(0)
        first_tile = step == nt - 1

        @pl.when(step == 0)
        def _():
            mu_carry[...] = jnp.zeros_like(mu_carry)
            dxc_head[...] = jnp.zeros_like(dxc_head)
            for ref in (dcw_ref, dcb_ref, dwa_ref, dwx_ref, dba_ref, dbx_ref, dlam_ref):
                ref[...] = jnp.zeros_like(ref)

        xr = jnp.concatenate([x0[...], x1[...]], axis=1)
        tail = jnp.where(first_tile, 0.0, jnp.concatenate([xh0[...], xh1[...]], axis=1))
        lam_v = lam_ref[...]
        sp = _softplus(-lam_v)
        reset = pos_ref[...] == 0
        cw = cw_ref[...]
        xbuf[0:8, :] = tail
        xbuf[8:tr + 8, :] = xr
        g = jnp.concatenate([g0[...], g1[...]], axis=1)
        sg = _sigmoid(g)
        dy = dy_ref[...]
        h = h_ref[...]
        db_ref[:, D_MODEL:2 * D_MODEL] = (dy * h * (sg * (1.0 + g * (1.0 - sg)))).astype(BF16)
        gbuf[...] = dy * (g * sg)
        top = _scan_backward(a_ref, gbuf, lbuf, mu_carry[0:1, :], tr)
        mu_carry[...] = jnp.broadcast_to(top, mu_carry.shape)
        hbuf[0:8, :] = jnp.where(first_tile, 0.0, hh_ref[...])
        hbuf[8:tr + 8, :] = h
        live = jnp.logical_not(reset)
        dbuf[tr:tr + 8, :] = dxc_head[...]
        for j in range(RNN_BLOCKS):
            sl = slice(RNN_BW * j, RNN_BW * (j + 1))
            lam_t, h_prev = lbuf[:, sl], hbuf[pl.ds(7, tr), sl]
            xc, r, i, a, mult = xc_ref[:, sl], r_ref[:, sl], i_ref[:, sl], a_ref[:, sl], mult_ref[:, sl]
            d_a = jnp.where(live, lam_t * h_prev, 0.0)
            d_mult = jnp.where(live, lam_t * (i * xc), 0.0)
            d_ixc = lam_t * mult
            d_i = d_ixc * xc
            d_log_a = d_a * a - d_mult * (a * a / mult)
            d_za = d_log_a * (-LRU_C * sp[:, sl]) * (r * (1.0 - r))
            d_zx = d_i * (i * (1.0 - i))
            dlam_ref[:, sl] += jnp.sum(d_log_a * r, axis=0, keepdims=True) * (LRU_C * _sigmoid(-lam_v[:, sl]))
            dba_ref[:, sl] += jnp.sum(d_za, axis=0, keepdims=True)
            dbx_ref[:, sl] += jnp.sum(d_zx, axis=0, keepdims=True)
            xcb, dzab, dzxb = xc.astype(BF16), d_za.astype(BF16), d_zx.astype(BF16)
            dwa_ref[j] += _dot_tn(xcb, dzab)
            dwx_ref[j] += _dot_tn(xcb, dzxb)
            d_xc = d_ixc * i + (_dot_nt(dzab, wa_ref[j]) + _dot_nt(dzxb, wx_ref[j]))
            dcb_ref[:, sl] += jnp.sum(d_xc, axis=0, keepdims=True)
            for k in range(CONV_W):
                tap = xr[:, sl] if k == CONV_W - 1 else xbuf[pl.ds(8 - (CONV_W - 1 - k), tr), sl]
                dcw_ref[k:k + 1, sl] += jnp.sum(d_xc * tap, axis=0, keepdims=True)
            dbuf[0:tr, sl] = d_xc
            d_xr = d_xc * cw[CONV_W - 1:CONV_W, sl]
            for k in range(CONV_W - 1):
                d_xr = d_xr + dbuf[pl.ds(CONV_W - 1 - k, tr), sl] * cw[k:k + 1, sl]
            dxc_head[:, sl] = d_xc[0:8, :]
            db_ref[:, sl] = d_xr.astype(BF16)

    rev = lambda s: nt - 1 - s
    blk = lambda cb: pl.BlockSpec((tr, CB), lambda s, cb=cb: (rev(s), cb))
    halo = lambda w, cb: pl.BlockSpec((8, w), lambda s, cb=cb: (jnp.maximum(rev(s) * hb - 1, 0), cb))
    tok = lambda w: pl.BlockSpec((tr, w), lambda s: (rev(s), 0))
    row = lambda w: pl.BlockSpec((1, w), lambda s: (0, 0))
    full3 = pl.BlockSpec((RNN_BLOCKS, RNN_BW, RNN_BW), lambda s: (0, 0, 0))
    cwspec = pl.BlockSpec((CONV_W, D_MODEL), lambda s: (0, 0))
    vec = jax.ShapeDtypeStruct((1, D_MODEL), F32)
    gate_w = jax.ShapeDtypeStruct((RNN_BLOCKS, RNN_BW, RNN_BW), F32)
    out_shape = (jax.ShapeDtypeStruct((T, 2 * D_MODEL), BF16), jax.ShapeDtypeStruct((CONV_W, D_MODEL), F32), vec,
                 gate_w, gate_w, vec, vec, vec)
    big = lambda: pltpu.VMEM((tr, D_MODEL), F32)
    ext = lambda: pltpu.VMEM((tr + 8, D_MODEL), F32)
    return pl.pallas_call(
        body, out_shape=out_shape, grid=(nt,), name="rnn_backward",
        in_specs=[blk(CB_XR), blk(CB_XR + 1), halo(CB, CB_XR), halo(CB, CB_XR + 1), blk(CB_GR), blk(CB_GR + 1),
                  pl.BlockSpec((tr, 1), lambda s: (rev(s), 0)), tok(D_MODEL), halo(D_MODEL, 0)] + [tok(D_MODEL)] * 6
        + [cwspec, full3, full3, row(D_MODEL)],
        out_specs=(tok(2 * D_MODEL), cwspec, row(D_MODEL), full3, full3, row(D_MODEL), row(D_MODEL), row(D_MODEL)),
        scratch_shapes=[ext(), ext(), ext(), big(), big(), pltpu.VMEM((8, D_MODEL), F32), pltpu.VMEM((8, D_MODEL), F32)],
        compiler_params=_cp("arbitrary"),
    )(proj, proj, proj, proj, proj, proj, pos_col, h_rnn, h_rnn, *saved, d_y, *_in_hbm(conv_w, rwa, rwx, lam))


def _input_backward(pieces, w_in, x, dx2, mod_row, norm_g):
    T = x.shape[0]
    tm = min(T, 512)
    n = len(pieces)

    def body(*refs):
        d_refs = refs[:n]
        w_ref, x_ref, dx2_ref, mod_ref, g_ref, gx_ref, dshift_ref, dscale_ref, dg_ref = refs[n:]
        i = pl.program_id(0)
        dh = None
        for d_ref, (_, start, count) in zip(d_refs, pieces):
            part = _dot_nt(d_ref[...], w_ref[:, start * CB:(start + count) * CB])
            dh = part if dh is None else dh + part

        @pl.when(i == 0)
        def _():
            dshift_ref[...] = jnp.zeros_like(dshift_ref)
            dscale_ref[...] = jnp.zeros_like(dscale_ref)
            dg_ref[...] = jnp.zeros_like(dg_ref)

        xf = x_ref[...]
        r1 = _rms(xf)
        xn = xf * r1
        gn = g_ref[...]
        s1 = 1.0 + mod_ref[:, D_MODEL:2 * D_MODEL]
        dshift_ref[...] += jnp.sum(dh, axis=0, keepdims=True)
        dscale_ref[...] += jnp.sum(dh * (xn * gn), axis=0, keepdims=True)
        dg_ref[...] += jnp.sum(dh * s1 * xn, axis=0, keepdims=True)
        dxn = dh * s1 * gn
        gx_ref[...] = dx2_ref[...] + r1 * (dxn - xn * jnp.mean(dxn * xn, axis=-1, keepdims=True))

    tok = lambda w: pl.BlockSpec((tm, w), lambda i: (i, 0))
    row = lambda w: pl.BlockSpec((1, w), lambda i: (0, 0))
    vec = jax.ShapeDtypeStruct((1, D_MODEL), F32)
    return pl.pallas_call(
        body, out_shape=(jax.ShapeDtypeStruct((T, D_MODEL), F32), vec, vec, vec), grid=(T // tm,), name="input_backward",
        in_specs=[tok(c * CB) for _, _, c in pieces]
        + [pl.BlockSpec((D_MODEL, IN_W), lambda i: (0, 0), pipeline_mode=pl.Buffered(1)), tok(D_MODEL), tok(D_MODEL),
           row(ADA_W), row(D_MODEL)],
        out_specs=(tok(D_MODEL), row(D_MODEL), row(D_MODEL), row(D_MODEL)),
        compiler_params=_cp("arbitrary"),
    )(*[p[0] for p in pieces], w_in, x, dx2, *_in_hbm(mod_row, norm_g))


def _weight_grad(a, pieces, tag, a_is_transposed=False):
    M, T = a.shape if a_is_transposed else a.shape[::-1]
    n_blocks = sum(count for _, _, count in pieces)
    n = len(pieces)
    contract = _dot if a_is_transposed else _dot_tn

    def body(*refs):
        a_ref, b_refs, o_ref = refs[0], refs[1:1 + n], refs[-1]
        j = pl.program_id(0)
        for b_ref, (_, start, count) in zip(b_refs, pieces):
            @pl.when((j >= start) & (j < start + count))
            def _(b_ref=b_ref):
                o_ref[...] = contract(a_ref[...], b_ref[...])

    def piece_spec(start, count):
        return pl.BlockSpec((T, CB), lambda j: (0, jnp.clip(j - start, 0, count - 1)))

    return pl.pallas_call(
        body, out_shape=jax.ShapeDtypeStruct((M, n_blocks * CB), F32), grid=(n_blocks,), name=f"weight_grad_{tag}",
        in_specs=[pl.BlockSpec(a.shape, lambda j: (0, 0), pipeline_mode=pl.Buffered(1))] + [piece_spec(s, c) for _, s, c in pieces],
        out_specs=pl.BlockSpec((M, CB), lambda j: (0, j)), compiler_params=_cp("arbitrary"),
    )(a, *[p[0] for p in pieces])


def _adamw(w, g, m, v):
    m = ADAM_B1 * m + (1.0 - ADAM_B1) * g
    v = ADAM_B2 * v + (1.0 - ADAM_B2) * (g * g)
    m_hat = m / (1.0 - ADAM_B1 ** ADAM_STEP)
    v_hat = v / (1.0 - ADAM_B2 ** ADAM_STEP)
    delta = -ADAM_LR * (m_hat / (jnp.sqrt(v_hat) + ADAM_EPS) + ADAM_WD * w)
    return delta, m, v


def _sum_landed(kind, owns, lands, where, tag):
    n = len(owns)
    land = lands[0]
    if kind == "in":
        R, C = land.shape[1:]
        tr = 256
        grid = (R // tr,)
        own_spec = pl.BlockSpec((tr, C), lambda i, w: (i, w[0]))
        land_spec = pl.BlockSpec((3, tr, C), lambda i, w: (0, i, 0))
        out_spec = pl.BlockSpec((1, tr, C), lambda i, w: (w[1], i, 0))
        out_shape = (2, R, C)
        pick = lambda ref: ref[...]
    elif kind == "sq":
        R, C = land.shape[1:]
        grid = (1,)
        own_spec = pl.BlockSpec((1, R, C), lambda i, w: (w[0], 0, 0))
        land_spec = pl.BlockSpec((3, R, C), lambda i, w: (0, 0, 0))
        out_spec = pl.BlockSpec((1, R, C), lambda i, w: (w[1], 0, 0))
        out_shape = (2, R, C)
        pick = lambda ref: ref[0]
    else:
        B, R, C = land.shape[1:]
        grid = (1,)
        own_spec = pl.BlockSpec((B, 1, R, C), lambda i, w: (0, w[0], 0, 0))
        land_spec = pl.BlockSpec((3, B, R, C), lambda i, w: (0, 0, 0, 0))
        out_spec = pl.BlockSpec((B, 1, R, C), lambda i, w: (0, w[1], 0, 0))
        out_shape = (B, 2, R, C)
        pick = lambda ref: ref[:, 0]

    def body(w_ref, *refs):
        for k in range(n):
            own_ref, l_ref, o_ref = refs[k], refs[n + k], refs[2 * n + k]
            total = ((pick(own_ref) + l_ref[0].astype(F32)) + l_ref[1].astype(F32)) + l_ref[2].astype(F32)
            if kind == "rg":
                o_ref[:, 0] = total
            else:
                o_ref[0] = total

    grid_spec = pltpu.PrefetchScalarGridSpec(num_scalar_prefetch=1, grid=grid, in_specs=[own_spec] * n + [land_spec] * n,
                                             out_specs=(out_spec,) * n)
    return list(pl.pallas_call(
        body, out_shape=(jax.ShapeDtypeStruct(out_shape, F32),) * n, grid_spec=grid_spec, name=f"sum_landed_{tag}",
        compiler_params=_cp("parallel"),
    )(where, *owns, *lands))


def _adamw_shard(gs, ws, ms, vs, tag):
    n = len(ws)
    R, C = ws[0].shape
    tr = min(R, 256 if n == 1 else 64)

    def body(*refs):
        for k in range(n):
            g = refs[k][...]
            d, nm, nv = _adamw(refs[n + k][...], g, refs[2 * n + k][...], refs[3 * n + k][...])
            out = refs[4 * n + 4 * k:4 * n + 4 * k + 4]
            out[0][...] = g
            out[1][...] = d
            out[2][...] = nm
            out[3][...] = nv

    spec = pl.BlockSpec((tr, C), lambda i: (i, 0))
    sds = jax.ShapeDtypeStruct((R, C), F32)
    outs = pl.pallas_call(
        body, out_shape=(sds,) * (4 * n), grid=(R // tr,), name=f"adamw_{tag}",
        in_specs=[spec] * (4 * n), out_specs=(spec,) * (4 * n), compiler_params=_cp("parallel"),
    )(*gs, *_in_hbm(*ws, *ms, *vs))
    return [outs[4 * k:4 * k + 4] for k in range(n)]


def _adamw_w_ada(c_t, dmod_cols, w, m, v):
    R, C = w.shape

    def body(ct_ref, dm_ref, w_ref, m_ref, v_ref, g_ref, d_ref, nm_ref, nv_ref):
        g = _dot(ct_ref[...].astype(BF16), dm_ref[...].astype(BF16))
        d, nm, nv = _adamw(w_ref[...], g, m_ref[...], v_ref[...])
        g_ref[...] = g
        d_ref[...] = d
        nm_ref[...] = nm
        nv_ref[...] = nv

    tr = 256
    spec = pl.BlockSpec((tr, C), lambda i: (i, 0))
    sds = jax.ShapeDtypeStruct((R, C), F32)
    return pl.pallas_call(
        body, out_shape=(sds,) * 4, grid=(R // tr,), name="adamw_w_ada",
        in_specs=[pl.BlockSpec((tr, 128), lambda i: (i, 0)), pl.BlockSpec((128, C), lambda i: (0, 0))] + [spec] * 3,
        out_specs=(spec,) * 4, compiler_params=_cp("parallel"),
    )(c_t, dmod_cols, w, m, v)


def _adamw_small(small_all, ws, ms, vs):
    def body(s_ref, w_ref, m_ref, v_ref, g_ref, d_ref, nm_ref, nv_ref):
        g = s_ref[0]
        for b in range(1, N_DEV):
            g = g + s_ref[b]
        d, nm, nv = _adamw(w_ref[...], g, m_ref[...], v_ref[...])
        g_ref[...] = g
        d_ref[...] = d
        nm_ref[...] = nm
        nv_ref[...] = nv

    sds = jax.ShapeDtypeStruct((SMALL_ROWS, D_MODEL), F32)
    return pl.pallas_call(
        body, out_shape=(sds,) * 4, name="adamw_small", in_specs=[VMEM_SPEC] * 4, out_specs=(VMEM_SPEC,) * 4,
        compiler_params=pltpu.CompilerParams(vmem_limit_bytes=VMEM_LIMIT_V7X),
    )(small_all, ws, ms, vs)


ROW_MOD, ROW_NORM_G, ROW_CONV_B, ROW_BA, ROW_BX, ROW_LAM, ROW_FINAL_G, ROW_SINKS, ROW_CONV_W, ROW_LOSS = 0, 3, 4, 5, 6, 7, 8, 9, 10, 14


def _pack_small(b_ada, norm_g, conv_b, ba, bx, lam, final_g, sinks, conv_w_full, loss_row=None):
    lane_pad = lambda a: jnp.pad(a.reshape(1, -1), ((0, 0), (0, D_MODEL - a.size)))
    rows = [b_ada.reshape(3, D_MODEL), norm_g, conv_b, ba, bx, lam, final_g.reshape(1, D_MODEL), lane_pad(sinks), conv_w_full,
            jnp.zeros((1, D_MODEL), F32) if loss_row is None else lane_pad(loss_row),
            jnp.zeros((SMALL_ROWS - ROW_LOSS - 1, D_MODEL), F32)]
    return jnp.concatenate([r.astype(F32) for r in rows], axis=0)


def kernel(x, c, positions, w_ada, b_ada, norm_g, w_in, attn_sinks, conv_w, conv_b, rg_wa, rg_ba, rg_wx, rg_bx, rg_lambda, w_attn_proj, w_rnn_proj, w_out, final_g, loss_target, m_w_ada, m_b_ada, m_norm_g, m_w_in, m_attn_sinks, m_conv_w, m_conv_b, m_rg_wa, m_rg_ba, m_rg_wx, m_rg_bx, m_rg_lambda, m_w_attn_proj, m_w_rnn_proj, m_w_out, m_final_g, v_w_ada, v_b_ada, v_norm_g, v_w_in, v_attn_sinks, v_conv_w, v_conv_b, v_rg_wa, v_rg_ba, v_rg_wx, v_rg_bx, v_rg_lambda, v_w_attn_proj, v_w_rnn_proj, v_w_out, v_final_g):
    T = x.shape[1]
    my_chip = lax.axis_index("x") * 2 + lax.axis_index("y")
    my_dev = my_chip * 2 + lax.axis_index("c")
    x2d, tgt = x[0], loss_target[0]
    pos_col = positions.reshape(T, 1)

    chip_idx = my_chip.reshape(1).astype(jnp.int32)
    c_idx = lax.axis_index("c").reshape(1).astype(jnp.int32)
    sq_place = ((D_MODEL, D_MODEL), (SHARD_ROWS, D_MODEL), lambda chip: (chip, 0))
    rg_place = ((RNN_BLOCKS, RNN_BW, RNN_BW), (RNN_BLOCKS, SHARD_RG, RNN_BW), lambda chip: (0, chip, 0))
    in_place = ((D_MODEL, IN_W), (D_MODEL, SHARD_IN), lambda chip: (0, chip))
    placed = _cast_place([w_in[0], w_attn_proj[0], w_rnn_proj[0], w_out[0], rg_wa[0], rg_wx[0]], chip_idx,
                         [in_place, sq_place, sq_place, sq_place, rg_place, rg_place])
    cw_chips, c_all, mod_chips = _gather_mod(c.reshape(1, 1, D_MODEL), w_ada[0], conv_w[0])
    g_ssems, g_rsems, fulls, g_token = _gather_start([p.reshape(s) for p, s in zip(placed, FULL_SHAPES)], mod_chips)
    conv_w_f = jnp.transpose(cw_chips, (1, 0, 2)).reshape(CONV_W, D_MODEL)
    mod_all = jnp.transpose(mod_chips, (1, 0, 2)).reshape(N_DEV, ADA_W) + b_ada
    mod_row = lax.dynamic_slice_in_dim(mod_all, my_dev, 1, axis=0) + g_token[0:1, 0:1]

    h, h_t, tabs = _prenorm(x2d, mod_row, norm_g, pos_col)
    w_in_v = fulls[0]
    proj = _in_projection(h, w_in_v.reshape(D_MODEL, IN_W), chip_idx, None, "own")
    for k, mask in enumerate(CHIP_MASKS):
        w_in_v = _gather_wait(g_ssems[k], g_rsems[k], [w_in_v], [0], proj, f"w_in_{k}")[0]
        w_in_v = _forward_halves([w_in_v], [(0, 0, k)], f"w_in_{k}")[0]
        from_chip = (chip_idx ^ (mask >> 1)).astype(jnp.int32)
        proj = _in_projection(h, w_in_v.reshape(D_MODEL, IN_W), from_chip, proj, f"from_{k}")
    w_in_f = w_in_v.reshape(D_MODEL, IN_W)
    rest = _gather_wait(g_ssems[3], g_rsems[3], list(fulls[1:]), [1, 2, 3, 4, 5], proj, "rest")
    rest = _forward_halves(rest, [(idx - 1, idx, k) for idx in range(1, N_BIG) for k in range(3)], "rest")
    wap_f, wrp_f, wo_f = (g.reshape(D_MODEL, D_MODEL) for g in rest[0:3])
    rwa_f, rwx_f = (g.reshape(RNN_BLOCKS, RNN_BW, RNN_BW) for g in rest[3:5])
    y_attn, qr_b, kr_b, p_all = _attn_forward(proj, tabs, attn_sinks)
    y_rnn, h_rnn, *rnn_saved = _rnn_forward(proj, pos_col, conv_w_f, conv_b, rwa_f, rwx_f, rg_ba, rg_bx, rg_lambda)
    (dx2, merged, d_o, d_pa, d_pr, d_ya, d_yr, d_c, d_final_g, d_gate, loss_vec) = _merge_and_head(
        x2d, tgt, y_attn, y_rnn, proj, wap_f, wrp_f, wo_f, mod_row, final_g.reshape(1, D_MODEL))

    sq = (N_CHIPS, 2, SHARD_ROWS // 2, D_MODEL)
    rg = (RNN_BLOCKS, N_CHIPS, 2, SHARD_RG // 2, RNN_BW)
    rg_flat = (RNN_BLOCKS * N_CHIPS, 2, SHARD_RG // 2, RNN_BW)

    def chip_sum_and_start(views, axes, flat, unflat, tags_, kinds_, group):
        from_sib = _swap_halves(views, axes)
        exact, rounded = [None] * len(views), [None] * len(views)
        for shape in dict.fromkeys(flat):
            ids = [k for k, f in enumerate(flat) if f == shape]
            ex, ro = _presum([views[k].reshape(shape) for k in ids],
                             [from_sib[k].reshape(shape[:1] + shape[2:]) for k in ids], c_idx, tags_[ids[0]])
            for k, e, r in zip(ids, ex, ro):
                exact[k], rounded[k] = e.reshape(unflat[k]), r.reshape(unflat[k])
        return _exchange_start(rounded, kinds_, group), exact

    g_ap = _weight_grad(y_attn, [(d_pa, 0, 2)], "w_attn_proj")
    g_rp = _weight_grad(y_rnn, [(d_pr, 0, 2)], "w_rnn_proj")
    g_o = _weight_grad(merged, [(d_o, 0, 2)], "w_out")
    sq_half = (N_CHIPS, SHARD_ROWS // 2, D_MODEL)
    started1, own1 = chip_sum_and_start([g_ap.reshape(sq), g_rp.reshape(sq), g_o.reshape(sq)], [1, 1, 1], [sq] * 3, [sq_half] * 3,
                                  ["w_attn_proj", "w_rnn_proj", "w_out"], ["sq"] * 3, "proj")
    d_q, d_kv, d_ga, d_sinks = _attn_backward(proj, qr_b, kr_b, p_all, d_ya, tabs, started1[4][0:1, 0:16])
    d_b, d_conv_w, d_conv_b, d_rwa, d_rwx, d_ba, d_bx, d_lam = _rnn_backward(
        proj, pos_col, h_rnn, rnn_saved, d_yr, conv_w_f, rwa_f, rwx_f, rg_lambda)
    pieces = [(d_q, CB_Q, 2), (d_kv, CB_KV, 1), (d_ga, CB_GA, 2), (d_b, CB_XR, 4), (d_c, CB_MA, 4)]
    g_in = _weight_grad(h_t, pieces, "w_in", a_is_transposed=True)
    started2, own2 = chip_sum_and_start(
        [g_in.reshape(2, D_MODEL // 2, IN_W), d_rwa.reshape(rg), d_rwx.reshape(rg)], [0, 2, 2],
        [(1, 2, D_MODEL // 2, IN_W), rg_flat, rg_flat],
        [(D_MODEL // 2, IN_W), (RNN_BLOCKS, N_CHIPS, SHARD_RG // 2, RNN_BW), (RNN_BLOCKS, N_CHIPS, SHARD_RG // 2, RNN_BW)],
        ["w_in", "rg_wa", "rg_wx"], ["in", "rg", "rg"], "in")
    grad_x, d_shift, d_scale, d_norm_g = _input_backward(pieces, w_in_f, x2d, dx2, mod_row + started2[4][0, 0], norm_g)

    d_mod = jnp.concatenate([d_shift, d_scale, d_gate], axis=1)
    small = _pack_small(d_mod, d_norm_g, d_conv_b, d_ba, d_bx, d_lam, d_final_g, d_sinks[:, :N_HEADS], d_conv_w, loss_vec)
    slabs = lax.dynamic_update_slice(jnp.zeros((N_DEV, SMALL_ROWS, D_MODEL), F32), small[None], (my_dev, 0, 0))
    gs_ssem, gs_rsem, slabs, gs_token = _gather_small_start(slabs)
    _, lands1 = _exchange_wait(*started1[:4], gs_token, "proj")
    _, lands2 = _exchange_wait(*started2[:4], gs_token, "in")
    tags = ["w_in", "w_attn_proj", "w_rnn_proj", "w_out", "rg_wa", "rg_wx"]
    chip_sums = [own2[0]] + list(own1) + list(own2[1:])
    lands = [lands2[0]] + list(lands1) + list(lands2[1:])
    where = jnp.concatenate([chip_idx, c_idx])
    kinds = ["in", "sq", "sq", "sq", "rg", "rg"]
    groups = [[0], [1, 2, 3], [4, 5]]
    halves = [None] * 6
    for ids in groups:
        for i, half in zip(ids, _sum_landed(kinds[ids[0]], [chip_sums[i] for i in ids], [lands[i] for i in ids], where,
                                            tags[ids[0]])):
            halves[i] = half
    grads = _assemble_with_sibling(halves, [0, 0, 0, 0, 1, 1])
    shapes2d = [(D_MODEL, SHARD_IN), (SHARD_ROWS, D_MODEL), (SHARD_ROWS, D_MODEL), (SHARD_ROWS, D_MODEL),
                (RNN_BLOCKS * SHARD_RG, RNN_BW), (RNN_BLOCKS * SHARD_RG, RNN_BW)]
    big_w = [w_in, w_attn_proj, w_rnn_proj, w_out, rg_wa, rg_wx]
    big_m = [m_w_in, m_w_attn_proj, m_w_rnn_proj, m_w_out, m_rg_wa, m_rg_wx]
    big_v = [v_w_in, v_w_attn_proj, v_w_rnn_proj, v_w_out, v_rg_wa, v_rg_wx]
    res = {}
    for ids in groups:
        flat2d = lambda arrs: [arrs[i].reshape(shapes2d[i]) for i in ids]
        outs = _adamw_shard(flat2d(grads), flat2d(big_w), flat2d(big_m), flat2d(big_v), tags[ids[0]])
        for i, four in zip(ids, outs):
            res[tags[i]] = [o.reshape(big_w[i].shape) for o in four]

    small_all = _gather_small_wait(gs_ssem, gs_rsem, slabs, res["w_in"][1])
    dmod_all = small_all[:, ROW_MOD:ROW_MOD + 3, :].reshape(N_DEV, ADA_W)
    dmod_cols = lax.dynamic_slice_in_dim(dmod_all, my_chip * SHARD_ADA, SHARD_ADA, axis=1)
    c_t = jnp.pad(jnp.transpose(c_all.reshape(N_DEV, D_MODEL)), ((0, 0), (0, 128 - N_DEV)))
    dmod_cols = jnp.pad(dmod_cols, ((0, 128 - N_DEV), (0, 0)))
    res["w_ada"] = [o.reshape(w_ada.shape) for o in _adamw_w_ada(c_t, dmod_cols, w_ada[0], m_w_ada[0], v_w_ada[0])]

    def full_conv(a):
        return lax.dynamic_update_slice_in_dim(jnp.zeros((CONV_W, D_MODEL), F32), a[0], my_chip * (D_MODEL // N_CHIPS), axis=1)

    packed = [_pack_small(p[0], p[1], p[2], p[3], p[4], p[5], p[6], p[7], full_conv(p[8])) for p in (
        (b_ada, norm_g, conv_b, rg_ba, rg_bx, rg_lambda, final_g, attn_sinks, conv_w),
        (m_b_ada, m_norm_g, m_conv_b, m_rg_ba, m_rg_bx, m_rg_lambda, m_final_g, m_attn_sinks, m_conv_w),
        (v_b_ada, v_norm_g, v_conv_b, v_rg_ba, v_rg_bx, v_rg_lambda, v_final_g, v_attn_sinks, v_conv_w))]
    small_out = _adamw_small(small_all, *packed)

    def unpack(slab):
        cw = lax.dynamic_slice_in_dim(slab[ROW_CONV_W:ROW_CONV_W + CONV_W], my_chip * (D_MODEL // N_CHIPS),
                                      D_MODEL // N_CHIPS, axis=1)
        return {
            "b_ada": slab[ROW_MOD:ROW_MOD + 3].reshape(1, ADA_W), "norm_g": slab[ROW_NORM_G:ROW_NORM_G + 1],
            "conv_b": slab[ROW_CONV_B:ROW_CONV_B + 1], "rg_ba": slab[ROW_BA:ROW_BA + 1], "rg_bx": slab[ROW_BX:ROW_BX + 1],
            "rg_lambda": slab[ROW_LAM:ROW_LAM + 1], "final_g": slab[ROW_FINAL_G], "attn_sinks": slab[ROW_SINKS:ROW_SINKS + 1, :N_HEADS],
            "conv_w": cw[None],
        }

    small_res = [unpack(s) for s in small_out]
    order = ["w_ada", "b_ada", "norm_g", "w_in", "attn_sinks", "conv_w", "conv_b", "rg_wa", "rg_ba", "rg_wx", "rg_bx",
             "rg_lambda", "w_attn_proj", "w_rnn_proj", "w_out", "final_g"]
    loss = small_out[0][ROW_LOSS, 0]
    outs = [loss, grad_x[None]]
    for kind in range(4):
        for name in order:
            outs.append(res[name][kind] if name in res else small_res[kind][name])
    return tuple(outs)
```

```python
import numpy as np
import jax
import jax.numpy as jnp
from jax import lax
from jax.experimental import pallas as pl
from jax.experimental.pallas import tpu as pltpu

F32 = jnp.float32
BF16 = jnp.bfloat16

D_MODEL = 1024
N_HEADS = 16
N_KV = 4
HEAD_DIM = 64
GROUP = N_HEADS // N_KV
BLOCK = 128
KV_W = N_KV * HEAD_DIM
ROT_HALF = 8
ROPE_THETA = 500000.0
ATTN_SCALE = 0.125
RNN_BLOCKS = 4
RNN_BW = 256
CONV_W = 4
LRU_C = 8.0
NORM_EPS = 1e-6
IN_W = 6656
CB = 512
N_CB = IN_W // CB
CB_Q, CB_KV, CB_GA, CB_XR, CB_GR, CB_MA, CB_MR = 0, 2, 3, 5, 7, 9, 11
V_COL_BLOCK = 5
N_CHIPS = 4
N_DEV = 8
SHARD_IN = IN_W // N_CHIPS
SHARD_ROWS = D_MODEL // N_CHIPS
SHARD_RG = RNN_BW // N_CHIPS
ADA_W = 3 * D_MODEL
SHARD_ADA = ADA_W // N_CHIPS
SMALL_ROWS = 16

ADAM_LR = 0.001
ADAM_B1 = 0.9
ADAM_B2 = 0.999
ADAM_EPS = 1e-08
ADAM_WD = 0.01
ADAM_STEP = 10

VMEM_LIMIT_V7X = 52 * 1024 * 1024
MESH = pl.DeviceIdType.MESH
ANY = pl.BlockSpec(memory_space=pl.ANY)
VMEM_SPEC = pl.BlockSpec(memory_space=pltpu.VMEM)


def _in_hbm(*arrays):
    return [pltpu.with_memory_space_constraint(a, pltpu.HBM) for a in arrays]


def _cp(*sem):
    return pltpu.CompilerParams(dimension_semantics=sem if sem else None, vmem_limit_bytes=VMEM_LIMIT_V7X)


def _dot(a, b):
    return jnp.dot(a, b, preferred_element_type=F32)


def _dot_nt(a, b):
    return lax.dot_general(a, b, (((1,), (1,)), ((), ())), preferred_element_type=F32)


def _dot_tn(a, b):
    return lax.dot_general(a, b, (((0,), (0,)), ((), ())), preferred_element_type=F32)


def _sigmoid(z):
    return 1.0 / (1.0 + jnp.exp(-z))


def _softplus(z):
    u = jnp.exp(-jnp.abs(z))
    log1p_u = jnp.where(u < 1e-3, u * (1.0 - u * (0.5 - u * (1.0 / 3.0))), jnp.log(1.0 + u))
    return jnp.maximum(z, 0.0) + log1p_u


def _rms(xf):
    return lax.rsqrt(jnp.mean(xf * xf, axis=-1, keepdims=True) + NORM_EPS)


def _me():
    return lax.axis_index("x"), lax.axis_index("y"), lax.axis_index("c")


def _peer(mask):
    x, y, c = _me()
    fx, fy, fc = (mask >> 2) & 1, (mask >> 1) & 1, mask & 1
    return (x ^ fx if fx else x, y ^ fy if fy else y, c ^ fc if fc else c)


def _chip_of(pos):
    return pos[0] * 2 + pos[1]


SIBLING_COLLECTIVE_ID = 0
SIBLING_ONLY = pltpu.CompilerParams(collective_id=SIBLING_COLLECTIVE_ID)


def _sibling_handshake():
    barrier = pltpu.get_barrier_semaphore()
    pl.semaphore_signal(barrier, inc=1, device_id=_peer(1), device_id_type=MESH)
    pl.semaphore_wait(barrier, 1)


CHIP_MASKS = (4, 2, 6)
ALL_MASKS = (1, 2, 3, 4, 5, 6, 7)


HBM_SPEC = pl.BlockSpec(memory_space=pltpu.HBM)
SEM_SPEC = pl.BlockSpec(memory_space=pltpu.SEMAPHORE)
SPLIT_COPY = pltpu.CompilerParams(has_side_effects=pltpu.SideEffectType.DATAFLOW_SIDE_EFFECTING)
N_BIG = 6
FULL_SHAPES = (
    (2, D_MODEL // 2, IN_W),
    (N_CHIPS, 2, SHARD_ROWS // 2, D_MODEL), (N_CHIPS, 2, SHARD_ROWS // 2, D_MODEL), (N_CHIPS, 2, SHARD_ROWS // 2, D_MODEL),
    (RNN_BLOCKS, N_CHIPS, 2, SHARD_RG // 2, RNN_BW), (RNN_BLOCKS, N_CHIPS, 2, SHARD_RG // 2, RNN_BW),
)


def _slot(full, idx, chip, half):
    if idx == 0:
        return full.at[half, :, pl.ds(pl.multiple_of(chip * SHARD_IN, 128), SHARD_IN)]
    return full.at[chip, half] if idx in (1, 2, 3) else full.at[:, chip, half]


def _three_halves(full, idx):
    return full.at[pl.ds(0, 3), 0] if idx in (1, 2, 3) else full.at[:, pl.ds(0, 3), 0]


def _gather_start(fulls, after):
    def body(*refs):
        full_refs = refs[:N_BIG]
        ssems, rsems = refs[N_BIG + 1:N_BIG + 5], refs[N_BIG + 5:N_BIG + 9]
        token = refs[2 * N_BIG + 9]
        me = _me()
        my_chip = _chip_of(me)
        for idx in range(N_BIG):
            for k, mask in enumerate(CHIP_MASKS):
                pair = k if idx == 0 else 3
                mine = _slot(full_refs[idx], idx, my_chip, me[2])
                pltpu.make_async_remote_copy(src_ref=mine, dst_ref=mine, send_sem=ssems[pair], recv_sem=rsems[pair],
                                             device_id=_peer(mask), device_id_type=MESH).start()
        token[...] = jnp.zeros_like(token)

    sem = pltpu.SemaphoreType.DMA(())
    out_shape = (sem,) * 8 + tuple(pltpu.HBM(f.shape, f.dtype) for f in fulls) + (jax.ShapeDtypeStruct((8, 128), F32),)
    outs = pl.pallas_call(
        body, out_shape=out_shape, name="gather_start",
        in_specs=[HBM_SPEC] * N_BIG + [ANY], out_specs=tuple([SEM_SPEC] * 8 + [HBM_SPEC] * N_BIG + [VMEM_SPEC]),
        input_output_aliases={i: 8 + i for i in range(N_BIG)}, compiler_params=SPLIT_COPY,
    )(*[pltpu.with_memory_space_constraint(f, pltpu.HBM) for f in fulls], after)
    return outs[0:4], outs[4:8], outs[8:8 + N_BIG], outs[8 + N_BIG]


def _gather_wait(ssem, rsem, arrays, idxs, after, tag):
    n = len(arrays)

    def body(*refs):
        full_refs, ssem_ref, rsem_ref = refs[:n], refs[n], refs[n + 1]
        me = _me()
        for full, idx in zip(full_refs, idxs):
            region = _slot(full, 0, _chip_of(me), me[2]) if idx == 0 else _three_halves(full, idx)
            arrived = pltpu.make_async_remote_copy(
                src_ref=region, dst_ref=region, send_sem=ssem_ref, recv_sem=rsem_ref, device_id=me, device_id_type=MESH)
            arrived.wait_send()
            arrived.wait_recv()

    outs = pl.pallas_call(
        body, out_shape=tuple(pltpu.HBM(a.shape, a.dtype) for a in arrays), name=f"gather_wait_{tag}",
        in_specs=[HBM_SPEC] * n + [SEM_SPEC, SEM_SPEC, ANY], out_specs=tuple([HBM_SPEC] * n),
        input_output_aliases={i: i for i in range(n)}, compiler_params=SPLIT_COPY,
    )(*arrays, ssem, rsem, after)
    return list(outs)


def _forward_halves(arrays, items, tag):
    n, m = len(arrays), len(items)

    def body(*refs):
        outs, ssem, rsem = refs[n:2 * n], refs[2 * n], refs[2 * n + 1]
        me = _me()
        sib = _peer(1)
        _sibling_handshake()
        cps = []
        for j, (pos, idx, k) in enumerate(items):
            chip = _chip_of(_peer(CHIP_MASKS[k]))
            cp = pltpu.make_async_remote_copy(
                src_ref=_slot(outs[pos], idx, chip, me[2]), dst_ref=_slot(outs[pos], idx, chip, me[2]),
                send_sem=ssem.at[j], recv_sem=rsem.at[j], device_id=sib, device_id_type=MESH)
            cp.start()
            cps.append(cp)
        for j, (pos, idx, k) in enumerate(items):
            chip = _chip_of(_peer(CHIP_MASKS[k]))
            pltpu.make_async_remote_copy(
                src_ref=_slot(outs[pos], idx, chip, me[2]), dst_ref=_slot(outs[pos], idx, chip, 1 - me[2]),
                send_sem=ssem.at[j], recv_sem=rsem.at[j], device_id=sib, device_id_type=MESH).wait_recv()
        for cp in cps:
            cp.wait_send()

    outs = pl.pallas_call(
        body, out_shape=tuple(jax.ShapeDtypeStruct(a.shape, a.dtype) for a in arrays), name=f"forward_halves_{tag}",
        in_specs=[ANY] * n, out_specs=tuple([ANY] * n), input_output_aliases={i: i for i in range(n)},
        scratch_shapes=[pltpu.SemaphoreType.DMA((m,)), pltpu.SemaphoreType.DMA((m,))], compiler_params=SIBLING_ONLY,
    )(*arrays)
    return list(outs)


def _gather_mod(c_row, w_ada_s, conv_w_s):
    def body(c_ref, wada_ref, cw_s, cw_f, call_ref, mod_ref, wsend, wrecv, lsem, csend, crecv, msend, mrecv):
        me = _me()
        my_chip = _chip_of(me)
        my_dev = my_chip * 2 + me[2]
        sends = []
        for k, mask in enumerate(CHIP_MASKS):
            cp = pltpu.make_async_remote_copy(src_ref=cw_s, dst_ref=cw_f.at[my_chip], send_sem=wsend.at[k], recv_sem=wrecv.at[k],
                                              device_id=_peer(mask), device_id_type=MESH)
            cp.start()
            sends.append(cp)
        local = [pltpu.make_async_copy(cw_s, cw_f.at[my_chip], lsem.at[0])]
        for cp in local:
            cp.start()

        call_ref[my_dev] = c_ref[0]
        csends = []
        for k, mask in enumerate(ALL_MASKS):
            cp = pltpu.make_async_remote_copy(
                src_ref=c_ref.at[0], dst_ref=call_ref.at[my_dev],
                send_sem=csend.at[k], recv_sem=crecv.at[k], device_id=_peer(mask), device_id_type=MESH)
            cp.start()
            csends.append(cp)
        for k, mask in enumerate(ALL_MASKS):
            frm = _peer(mask)
            pltpu.make_async_remote_copy(
                src_ref=c_ref.at[0], dst_ref=call_ref.at[_chip_of(frm) * 2 + frm[2]],
                send_sem=csend.at[k], recv_sem=crecv.at[k], device_id=frm, device_id_type=MESH).wait_recv()
        for cp in csends:
            cp.wait_send()

        c_all = call_ref[...].reshape(N_DEV, D_MODEL).astype(BF16)
        mod_ref[my_chip] = _dot(c_all, wada_ref[...].astype(BF16))
        msends = []
        for k, mask in enumerate(CHIP_MASKS):
            cp = pltpu.make_async_remote_copy(
                src_ref=mod_ref.at[my_chip], dst_ref=mod_ref.at[my_chip],
                send_sem=msend.at[k], recv_sem=mrecv.at[k], device_id=_peer(mask), device_id_type=MESH)
            cp.start()
            msends.append(cp)
        for k, mask in enumerate(CHIP_MASKS):
            frm = _peer(mask)
            pltpu.make_async_remote_copy(
                src_ref=mod_ref.at[my_chip], dst_ref=mod_ref.at[_chip_of(frm)],
                send_sem=msend.at[k], recv_sem=mrecv.at[k], device_id=frm, device_id_type=MESH).wait_recv()
        for cp in msends:
            cp.wait_send()

        for k, mask in enumerate(CHIP_MASKS):
            frm = _peer(mask)
            pltpu.make_async_remote_copy(src_ref=cw_s, dst_ref=cw_f.at[_chip_of(frm)], send_sem=wsend.at[k], recv_sem=wrecv.at[k],
                                         device_id=frm, device_id_type=MESH).wait_recv()
        for cp in sends:
            cp.wait_send()
        for cp in local:
            cp.wait()

    out_shape = (
        jax.ShapeDtypeStruct((N_CHIPS, CONV_W, D_MODEL // N_CHIPS), F32),
        jax.ShapeDtypeStruct((N_DEV, 1, D_MODEL), F32),
        jax.ShapeDtypeStruct((N_CHIPS, N_DEV, SHARD_ADA), F32),
    )
    return pl.pallas_call(
        body, out_shape=out_shape, name="gather_mod",
        in_specs=[VMEM_SPEC, VMEM_SPEC, ANY], out_specs=(ANY, VMEM_SPEC, VMEM_SPEC),
        scratch_shapes=[
            pltpu.SemaphoreType.DMA((3,)), pltpu.SemaphoreType.DMA((3,)), pltpu.SemaphoreType.DMA((1,)),
            pltpu.SemaphoreType.DMA((7,)), pltpu.SemaphoreType.DMA((7,)),
            pltpu.SemaphoreType.DMA((3,)), pltpu.SemaphoreType.DMA((3,)),
        ],
        compiler_params=pltpu.CompilerParams(vmem_limit_bytes=VMEM_LIMIT_V7X),
    )(c_row, w_ada_s, conv_w_s)


def _cast_place(shards, chip_idx, places):
    n = len(shards)

    def body(chip_ref, *refs):
        for s_ref, o_ref in zip(refs[:n], refs[n:]):
            o_ref[...] = s_ref[...].astype(BF16)

    grid_spec = pltpu.PrefetchScalarGridSpec(
        num_scalar_prefetch=1, grid=(1,),
        in_specs=[pl.BlockSpec(s.shape, lambda i, chip_ref, nd=s.ndim: (0,) * nd) for s in shards],
        out_specs=tuple(pl.BlockSpec(block, lambda i, chip_ref, im=im: im(chip_ref[0])) for _, block, im in places))
    return pl.pallas_call(
        body, out_shape=tuple(jax.ShapeDtypeStruct(full, BF16) for full, _, _ in places), grid_spec=grid_spec,
        name="cast_place", compiler_params=_cp("arbitrary"),
    )(chip_idx, *_in_hbm(*shards))


def _shard_of(ref, kind, chip):
    if kind == "in":
        return ref.at[:, pl.ds(pl.multiple_of(chip * SHARD_IN, 128), SHARD_IN)]
    return ref.at[chip] if kind == "sq" else ref.at[:, chip]


def _land_shape(src, kind):
    if kind == "in":
        return (3, src.shape[0], SHARD_IN)
    return (3,) + src.shape[1:] if kind == "sq" else (3, src.shape[0]) + src.shape[2:]


def _exchange_start(srcs, kinds, tag):
    n = len(srcs)
    lands = [pltpu.with_memory_space_constraint(lax.empty(_land_shape(s, k), s.dtype), pltpu.HBM) for s, k in zip(srcs, kinds)]

    def body(*refs):
        src_refs, land_refs = refs[:n], refs[n:2 * n]
        ssems, rsems = refs[2 * n:3 * n], refs[3 * n:4 * n]
        token = refs[6 * n]
        for i in range(n):
            for k, mask in enumerate(CHIP_MASKS):
                to = _peer(mask)
                pltpu.make_async_remote_copy(
                    src_ref=_shard_of(src_refs[i], kinds[i], _chip_of(to)), dst_ref=land_refs[i].at[k],
                    send_sem=ssems[i], recv_sem=rsems[i], device_id=to, device_id_type=MESH).start()
        token[...] = jnp.zeros_like(token)

    sem = pltpu.SemaphoreType.DMA(())
    out_shape = ((sem,) * (2 * n) + tuple(pltpu.HBM(s.shape, s.dtype) for s in srcs)
                 + tuple(pltpu.HBM(l.shape, l.dtype) for l in lands) + (jax.ShapeDtypeStruct((8, 128), F32),))
    outs = pl.pallas_call(
        body, out_shape=out_shape, name=f"exchange_start_{tag}",
        in_specs=[HBM_SPEC] * (2 * n), out_specs=tuple([SEM_SPEC] * (2 * n) + [HBM_SPEC] * (2 * n) + [VMEM_SPEC]),
        input_output_aliases={i: 2 * n + i for i in range(2 * n)},
        compiler_params=pltpu.CompilerParams(has_side_effects=pltpu.SideEffectType.DATAFLOW_SIDE_EFFECTING),
    )(*[pltpu.with_memory_space_constraint(s, pltpu.HBM) for s in srcs], *lands)
    return outs[:n], outs[n:2 * n], outs[2 * n:3 * n], outs[3 * n:4 * n], outs[4 * n]


def _exchange_wait(ssems, rsems, srcs, lands, after, tag):
    n = len(srcs)

    def body(*refs):
        land_refs = refs[n:2 * n]
        ssem_refs, rsem_refs = refs[2 * n:3 * n], refs[3 * n:4 * n]
        for i in range(n):
            all_three = pltpu.make_async_remote_copy(
                src_ref=land_refs[i], dst_ref=land_refs[i], send_sem=ssem_refs[i], recv_sem=rsem_refs[i],
                device_id=_me(), device_id_type=MESH)
            all_three.wait_send()
            all_three.wait_recv()

    outs = pl.pallas_call(
        body, out_shape=tuple(pltpu.HBM(a.shape, a.dtype) for a in list(srcs) + list(lands)), name=f"exchange_wait_{tag}",
        in_specs=[HBM_SPEC] * (2 * n) + [SEM_SPEC] * (2 * n) + [ANY], out_specs=tuple([HBM_SPEC] * (2 * n)),
        input_output_aliases={i: i for i in range(2 * n)},
        compiler_params=pltpu.CompilerParams(has_side_effects=pltpu.SideEffectType.DATAFLOW_SIDE_EFFECTING),
    )(*srcs, *lands, *ssems, *rsems, after)
    return outs[:n], outs[n:]


def _gather_small_start(slabs):
    def body(slabs_ref, ssem, rsem, slabs_out, token):
        me = _me()
        mine = slabs_ref.at[_chip_of(me) * 2 + me[2]]
        for mask in ALL_MASKS:
            pltpu.make_async_remote_copy(src_ref=mine, dst_ref=mine, send_sem=ssem, recv_sem=rsem,
                                         device_id=_peer(mask), device_id_type=MESH).start()
        token[...] = jnp.zeros_like(token)

    sem = pltpu.SemaphoreType.DMA(())
    return pl.pallas_call(
        body, out_shape=(sem, sem, pltpu.HBM(slabs.shape, slabs.dtype), jax.ShapeDtypeStruct((8, 128), F32)),
        name="gather_small_start", in_specs=[HBM_SPEC], out_specs=(SEM_SPEC, SEM_SPEC, HBM_SPEC, VMEM_SPEC),
        input_output_aliases={0: 2}, compiler_params=SPLIT_COPY,
    )(pltpu.with_memory_space_constraint(slabs, pltpu.HBM))


def _gather_small_wait(ssem, rsem, slabs, after):
    def body(slabs_ref, ssem_ref, rsem_ref, after_ref, slabs_out):
        seven = slabs_ref.at[pl.ds(0, N_DEV - 1)]
        arrived = pltpu.make_async_remote_copy(
            src_ref=seven, dst_ref=seven, send_sem=ssem_ref, recv_sem=rsem_ref, device_id=_me(), device_id_type=MESH)
        arrived.wait_send()
        arrived.wait_recv()

    return pl.pallas_call(
        body, out_shape=pltpu.HBM(slabs.shape, slabs.dtype), name="gather_small_wait",
        in_specs=[HBM_SPEC, SEM_SPEC, SEM_SPEC, ANY], out_specs=HBM_SPEC, input_output_aliases={0: 0},
        compiler_params=SPLIT_COPY,
    )(slabs, ssem, rsem, after)


def _half_of(ref, axis, half):
    return ref.at[(slice(None),) * axis + (half,)]


def _swap_halves(parts, axes):
    n = len(parts)

    def body(*refs):
        ins, outs, ssem, rsem = refs[:n], refs[n:2 * n], refs[2 * n], refs[2 * n + 1]
        c = lax.axis_index("c")
        _sibling_handshake()
        cps = [pltpu.make_async_remote_copy(src_ref=_half_of(ins[i], axes[i], 1 - c), dst_ref=outs[i], send_sem=ssem.at[i],
                                            recv_sem=rsem.at[i], device_id=_peer(1), device_id_type=MESH) for i in range(n)]
        for cp in cps:
            cp.start()
        for cp in cps:
            cp.wait()

    shapes = [p.shape[:a] + p.shape[a + 1:] for p, a in zip(parts, axes)]
    return pl.pallas_call(
        body, out_shape=tuple(jax.ShapeDtypeStruct(s, p.dtype) for s, p in zip(shapes, parts)), name="swap_halves",
        in_specs=[ANY] * n, out_specs=tuple([ANY] * n),
        scratch_shapes=[pltpu.SemaphoreType.DMA((n,)), pltpu.SemaphoreType.DMA((n,))], compiler_params=SIBLING_ONLY,
    )(*parts)


def _swap_halves_start(parts, axes, tag):
    n = len(parts)
    lands = [pltpu.with_memory_space_constraint(lax.empty(p.shape[:a] + p.shape[a + 1:], p.dtype), pltpu.HBM)
             for p, a in zip(parts, axes)]

    def body(*refs):
        ins, land_refs, ssems, rsems, token = refs[:n], refs[n:2 * n], refs[2 * n:3 * n], refs[3 * n:4 * n], refs[6 * n]
        c = lax.axis_index("c")
        for i in range(n):
            pltpu.make_async_remote_copy(src_ref=_half_of(ins[i], axes[i], 1 - c), dst_ref=land_refs[i], send_sem=ssems[i],
                                         recv_sem=rsems[i], device_id=_peer(1), device_id_type=MESH).start()
        token[...] = jnp.zeros_like(token)

    sem = pltpu.SemaphoreType.DMA(())
    out_shape = ((sem,) * (2 * n) + tuple(pltpu.HBM(a.shape, a.dtype) for a in list(parts) + lands)
                 + (jax.ShapeDtypeStruct((8, 128), F32),))
    outs = pl.pallas_call(
        body, out_shape=out_shape, name=f"swap_halves_start_{tag}",
        in_specs=[HBM_SPEC] * (2 * n), out_specs=tuple([SEM_SPEC] * (2 * n) + [HBM_SPEC] * (2 * n) + [VMEM_SPEC]),
        input_output_aliases={i: 2 * n + i for i in range(2 * n)}, compiler_params=SPLIT_COPY,
    )(*[pltpu.with_memory_space_constraint(p, pltpu.HBM) for p in parts], *lands)
    return outs[:n], outs[n:2 * n], outs[2 * n:3 * n], outs[3 * n:4 * n], outs[4 * n]


def _swap_halves_wait(ssems, rsems, parts, lands, after, tag):
    n = len(parts)

    def body(*refs):
        land_refs, ssem_refs, rsem_refs = refs[n:2 * n], refs[2 * n:3 * n], refs[3 * n:4 * n]
        for i in range(n):
            moved = pltpu.make_async_remote_copy(
                src_ref=land_refs[i], dst_ref=land_refs[i], send_sem=ssem_refs[i], recv_sem=rsem_refs[i],
                device_id=_me(), device_id_type=MESH)
            moved.wait_send()
            moved.wait_recv()

    outs = pl.pallas_call(
        body, out_shape=tuple(pltpu.HBM(a.shape, a.dtype) for a in list(parts) + list(lands)), name=f"swap_halves_wait_{tag}",
        in_specs=[HBM_SPEC] * (2 * n) + [SEM_SPEC] * (2 * n) + [ANY], out_specs=tuple([HBM_SPEC] * (2 * n)),
        input_output_aliases={i: i for i in range(2 * n)}, compiler_params=SPLIT_COPY,
    )(*parts, *lands, *ssems, *rsems, after)
    return list(outs[:n]), list(outs[n:])


def _presum(mines, sibs, c_idx, tag):
    n = len(mines)
    S, _, R, C = mines[0].shape
    tr = min(R, 256)
    tc = SHARD_IN if C % SHARD_IN == 0 else (C // 2 if n > 1 and C % 256 == 0 else C)

    def body(c_ref, *refs):
        for k in range(n):
            total = refs[k][:, 0] + refs[n + k][...]
            refs[2 * n + k][...] = total
            refs[3 * n + k][...] = total.astype(BF16)

    out_spec = pl.BlockSpec((S, tr, tc), lambda i, j, c_ref: (0, i, j))
    grid_spec = pltpu.PrefetchScalarGridSpec(
        num_scalar_prefetch=1, grid=(R // tr, C // tc),
        in_specs=[pl.BlockSpec((S, 1, tr, tc), lambda i, j, c_ref: (0, c_ref[0], i, j))] * n + [out_spec] * n,
        out_specs=(out_spec,) * (2 * n))
    outs = pl.pallas_call(
        body, out_shape=(jax.ShapeDtypeStruct((S, R, C), F32),) * n + (jax.ShapeDtypeStruct((S, R, C), BF16),) * n,
        grid_spec=grid_spec, name=f"presum_{tag}", compiler_params=_cp("parallel", "parallel"),
    )(c_idx, *mines, *sibs)
    return list(outs[:n]), list(outs[n:])


def _assemble_with_sibling(parts, axes):
    n = len(parts)

    def body(*refs):
        outs, ssem, rsem = refs[n:2 * n], refs[2 * n], refs[2 * n + 1]
        c = lax.axis_index("c")
        _sibling_handshake()
        cps = [pltpu.make_async_remote_copy(
            src_ref=_half_of(outs[i], axes[i], c), dst_ref=_half_of(outs[i], axes[i], c), send_sem=ssem.at[i],
            recv_sem=rsem.at[i], device_id=_peer(1), device_id_type=MESH) for i in range(n)]
        for cp in cps:
            cp.start()
        for i in range(n):
            pltpu.make_async_remote_copy(
                src_ref=_half_of(outs[i], axes[i], c), dst_ref=_half_of(outs[i], axes[i], 1 - c), send_sem=ssem.at[i],
                recv_sem=rsem.at[i], device_id=_peer(1), device_id_type=MESH).wait_recv()
        for cp in cps:
            cp.wait_send()

    return pl.pallas_call(
        body, out_shape=tuple(jax.ShapeDtypeStruct(p.shape, p.dtype) for p in parts), name="assemble_with_sibling",
        in_specs=[ANY] * n, out_specs=tuple([ANY] * n), input_output_aliases={i: i for i in range(n)},
        scratch_shapes=[pltpu.SemaphoreType.DMA((n,)), pltpu.SemaphoreType.DMA((n,))], compiler_params=SIBLING_ONLY,
    )(*parts)


def _rope_lane_frequencies():
    inv = np.float32(ROPE_THETA) ** (-(np.arange(0, 2 * ROT_HALF, 2, dtype=np.float32)) / np.float32(2 * ROT_HALF))
    lane = np.arange(128) % HEAD_DIM
    return jnp.asarray(np.where(lane < 2 * ROT_HALF, inv[lane % ROT_HALF], 0.0).astype(np.float32)[None, :])


def _rope_tables(pos, freq):
    ang = pos.astype(F32) * freq
    c, s = jnp.cos(ang), jnp.sin(ang)
    m = lax.broadcasted_iota(jnp.int32, ang.shape, 1) & (HEAD_DIM - 1)
    return (jnp.where(m < 2 * ROT_HALF, c, 1.0), jnp.where(m < ROT_HALF, -s, 0.0),
            jnp.where((m >= ROT_HALF) & (m < 2 * ROT_HALF), s, 0.0))


def _columns(t):
    return [t[:, i:i + 128] for i in range(0, t.shape[-1], 128)]


def _rope(t, c, sa, sb):
    return jnp.concatenate(
        [x * c + pltpu.roll(x, 128 - ROT_HALF, 1) * sa + pltpu.roll(x, ROT_HALF, 1) * sb for x in _columns(t)], axis=1)


def _unrope(d, c, sa, sb):
    return jnp.concatenate(
        [x * c + pltpu.roll(x * sa, ROT_HALF, 1) + pltpu.roll(x * sb, 128 - ROT_HALF, 1) for x in _columns(d)], axis=1)


def _prenorm(x, mod_row, norm_g, pos_col):
    T = x.shape[0]
    tm = min(T, 512)

    def body(x_ref, mod_ref, g_ref, pos_ref, f_ref, h_ref, ht_ref, c_ref, sa_ref, sb_ref):
        xf = x_ref[...]
        shift, scale = mod_ref[:, 0:D_MODEL], mod_ref[:, D_MODEL:2 * D_MODEL]
        h = (xf * _rms(xf)) * g_ref[...] * (1.0 + scale) + shift
        h_ref[...] = h.astype(BF16)
        ht_ref[...] = h.T.astype(BF16)
        c_ref[...], sa_ref[...], sb_ref[...] = _rope_tables(pos_ref[...], f_ref[...])

    tab = jax.ShapeDtypeStruct((T, 128), F32)
    tok = lambda w: pl.BlockSpec((tm, w), lambda i: (i, 0))
    row = lambda w: pl.BlockSpec((1, w), lambda i: (0, 0))
    outs = pl.pallas_call(
        body, out_shape=(jax.ShapeDtypeStruct((T, D_MODEL), BF16), jax.ShapeDtypeStruct((D_MODEL, T), BF16), tab, tab, tab),
        grid=(T // tm,), name="prenorm",
        in_specs=[tok(D_MODEL), row(ADA_W), row(D_MODEL), tok(1), row(128)],
        out_specs=(tok(D_MODEL), pl.BlockSpec((D_MODEL, tm), lambda i: (0, i)), tok(128), tok(128), tok(128)),
        compiler_params=_cp("parallel"),
    )(x, *_in_hbm(mod_row, norm_g), pos_col, _rope_lane_frequencies())
    return outs[0], outs[1], tuple(outs[2:])


def _in_projection(h, w_in, chips, into, tag):
    T = h.shape[0]
    tm, tn = min(T, 512), SHARD_IN
    k = chips.shape[0]

    def body(chip_ref, h_ref, w_ref, *rest):
        rest[-1][...] = _dot(h_ref[...], w_ref[...])

    w_spec = pl.BlockSpec((D_MODEL, tn), lambda s, i, c: (0, c[s]), **({"pipeline_mode": pl.Buffered(1)} if k == 1 else {}))
    in_specs = [pl.BlockSpec((tm, D_MODEL), lambda s, i, c: (i, 0)), w_spec]
    args = [chips, h, w_in]
    aliases = {}
    if into is not None:
        in_specs.append(ANY)
        args.append(into)
        aliases = {3: 0}
    grid_spec = pltpu.PrefetchScalarGridSpec(num_scalar_prefetch=1, grid=(k, T // tm), in_specs=in_specs,
                                             out_specs=pl.BlockSpec((tm, tn), lambda s, i, c: (i, c[s])))
    return pl.pallas_call(
        body, out_shape=jax.ShapeDtypeStruct((T, IN_W), F32), grid_spec=grid_spec, name=f"in_projection_{tag}",
        input_output_aliases=aliases, compiler_params=_cp("parallel", "parallel"),
    )(*args)


def _attn_mask(n):
    qi = lax.broadcasted_iota(jnp.int32, (GROUP * BLOCK, BLOCK), 0) & (BLOCK - 1)
    j = lax.broadcasted_iota(jnp.int32, (GROUP * BLOCK, BLOCK), 1)
    own = j <= qi
    return own, jnp.logical_not(own) & (n == 0)


def _fold(x, own):
    return jnp.where(own, x[:, BLOCK:2 * BLOCK], x[:, 0:BLOCK])


def _unfold(xf, own):
    zero = jnp.zeros_like(xf)
    return jnp.concatenate([jnp.where(own, zero, xf), jnp.where(own, xf, zero)], axis=1)


ROW_GROUP_HEAD = (0, 2, 1, 3)


def _sink_col(sink_ref, kh):
    rowg = lax.broadcasted_iota(jnp.int32, (GROUP * BLOCK, 1), 0) // BLOCK
    col = jnp.full((GROUP * BLOCK, 1), sink_ref[0, GROUP * kh + ROW_GROUP_HEAD[0]], F32)
    for g in range(1, GROUP):
        col = jnp.where(rowg == g, sink_ref[0, GROUP * kh + ROW_GROUP_HEAD[g]], col)
    return col


def _low_lanes(shape):
    return lax.broadcasted_iota(jnp.int32, shape, 1) < HEAD_DIM


def _kv_pair_operand(prev, cur, kh):
    c = 128 * (kh // 2)
    col = jnp.concatenate([prev[:, c:c + 128], cur[:, c:c + 128]], axis=0).astype(F32)
    if kh % 2 == 0:
        lo = jnp.where(_low_lanes(col.shape), col, 0.0)
        hi = pltpu.roll(lo, HEAD_DIM, 1)
    else:
        hi = jnp.where(_low_lanes(col.shape), 0.0, col)
        lo = pltpu.roll(hi, HEAD_DIM, 1)
    return jnp.concatenate([lo, hi], axis=0).astype(BF16)


def _pair_rows(x, kh):
    c = 2 * 128 * kh
    return jnp.concatenate([x[:, c:c + 128], x[:, c + 128:c + 256]], axis=0)


def _restack(big):
    return jnp.concatenate([big[:, 0:2 * BLOCK], big[:, 2 * BLOCK:4 * BLOCK]], axis=0)


def _unrestack(stacked):
    return jnp.concatenate([stacked[0:2 * BLOCK], stacked[2 * BLOCK:4 * BLOCK]], axis=1)


def _fold_pair(x2, kh):
    low = _low_lanes((2 * BLOCK, 128))
    mixed = jnp.where(low, x2[0:2 * BLOCK], x2[2 * BLOCK:4 * BLOCK])
    total = mixed + pltpu.roll(mixed, HEAD_DIM, 1)
    return jnp.where(low, total, 0.0) if kh % 2 == 0 else jnp.where(low, 0.0, total)


def _attn_scores(qr, k2, kh):
    q2 = _pair_rows(qr, kh).astype(BF16)
    return q2, _restack(_dot_nt(q2, k2))


def _attn_softmax(s, sink_col, mask):
    own, no_key = mask
    s = jnp.where(no_key, -1e30, _fold(s, own))
    m = jnp.maximum(jnp.max(s, axis=-1, keepdims=True), sink_col)
    p = jnp.exp(s - m)
    p_sink = jnp.exp(sink_col - m)
    denom = jnp.sum(p, axis=-1, keepdims=True) + p_sink
    return p / denom, p_sink / denom


def _attn_forward(proj, tabs, sinks):
    T = proj.shape[0]
    nb = T // BLOCK

    def body(q_ref, kvc_ref, kvp_ref, g0_ref, g1_ref, cc, sac, sbc, cp_, sap, sbp, sink_ref, y_ref, qrb_ref, krb_ref, p_ref):
        n = pl.program_id(0)
        tc = tcur = (cc[...], sac[...], sbc[...])
        tprev = (cp_[...], sap[...], sbp[...])
        qr = _rope(q_ref[...], *tc) * ATTN_SCALE
        kr_cur = _rope(kvc_ref[:, 0:KV_W], *tcur)
        kr_prev = _rope(kvp_ref[:, 0:KV_W], *tprev)
        qrb_ref[...] = qr.astype(BF16)
        krb_ref[...] = kr_cur.astype(BF16)
        v_cur, v_prev = kvc_ref[:, KV_W:2 * KV_W], kvp_ref[:, KV_W:2 * KV_W]
        mask = _attn_mask(n)
        outs = []
        k2s = [_kv_pair_operand(kr_prev, kr_cur, kh) for kh in range(N_KV)]
        v2s = [_kv_pair_operand(v_prev, v_cur, kh) for kh in range(N_KV)]
        scores = [_attn_scores(qr, k2s[kh], kh) for kh in range(N_KV)]
        p_parts = []
        for kh in range(N_KV):
            pn, _ = _attn_softmax(scores[kh][1], _sink_col(sink_ref, kh), mask)
            p_parts += [pn[g * BLOCK:(g + 1) * BLOCK] for g in range(GROUP)]
            o_big = _dot(_unrestack(_unfold(pn.astype(BF16), mask[0])), v2s[kh])
            outs += [o_big[0:BLOCK], o_big[BLOCK:2 * BLOCK]]
        p_ref[...] = jnp.concatenate(p_parts, axis=1)
        o = jnp.concatenate(outs, axis=1)
        g = jnp.concatenate([g0_ref[...], g1_ref[...]], axis=1)
        y_ref[...] = (o * (g * _sigmoid(g))).astype(BF16)

    def blk(w, cb):
        return pl.BlockSpec((BLOCK, w), lambda n, cb=cb: (n, cb))

    prev = lambda w, cb: pl.BlockSpec((BLOCK, w), lambda n, cb=cb: (jnp.maximum(n - 1, 0), cb))
    return pl.pallas_call(
        body, grid=(nb,), name="attn_forward",
        out_shape=(jax.ShapeDtypeStruct((T, D_MODEL), BF16), jax.ShapeDtypeStruct((T, D_MODEL), BF16),
                   jax.ShapeDtypeStruct((T, KV_W), BF16), jax.ShapeDtypeStruct((T, N_HEADS * BLOCK), F32)),
        in_specs=[blk(D_MODEL, 0), blk(CB, CB_KV), prev(CB, CB_KV), blk(CB, CB_GA), blk(CB, CB_GA + 1),
                  blk(128, 0), blk(128, 0), blk(128, 0), prev(128, 0), prev(128, 0), prev(128, 0),
                  pl.BlockSpec(memory_space=pltpu.SMEM)],
        out_specs=(blk(D_MODEL, 0), blk(D_MODEL, 0), blk(KV_W, 0), blk(N_HEADS * BLOCK, 0)),
        compiler_params=_cp("parallel"),
    )(proj, proj, proj, proj, proj, *tabs, *tabs, sinks)


def _scan_rows8():
    return lax.broadcasted_iota(jnp.int32, (8, D_MODEL), 0)


def _scan_forward(a_ref, b_ref, h_ref, carry, rows):
    row = _scan_rows8()

    def group(i, carry):
        off = pl.multiple_of(i * 8, 8)
        a, b = a_ref[pl.ds(off, 8), :], b_ref[pl.ds(off, 8), :]
        for d in (1, 2, 4):
            ok = row >= d
            b = jnp.where(ok, a * pltpu.roll(b, d, 0) + b, b)
            a = jnp.where(ok, a * pltpu.roll(a, d, 0), a)
        h = a * carry + b
        h_ref[pl.ds(off, 8), :] = h
        return h[7:8, :]

    return lax.fori_loop(0, rows // 8, group, carry)


def _scan_backward(a_ref, g_ref, lam_ref, carry, rows):
    row = _scan_rows8()

    def group(i, carry):
        off = pl.multiple_of((rows // 8 - 1 - i) * 8, 8)
        a, g = a_ref[pl.ds(off, 8), :], g_ref[pl.ds(off, 8), :]
        b = a * g
        for d in (1, 2, 4):
            ok = row < 8 - d
            b = jnp.where(ok, a * pltpu.roll(b, 8 - d, 0) + b, b)
            a = jnp.where(ok, a * pltpu.roll(a, 8 - d, 0), a)
        mu = a * carry + b
        mu_below = jnp.where(row == 7, carry, pltpu.roll(mu, 7, 0))
        lam_ref[pl.ds(off, 8), :] = g + mu_below
        return mu[0:1, :]

    return lax.fori_loop(0, rows // 8, group, carry)


def _conv_taps(xbuf, xr, tail):
    rows = xr.shape[0]
    xbuf[0:8, :] = tail
    xbuf[8:rows + 8, :] = xr
    return [xbuf[pl.ds(8 - (CONV_W - 1 - k), rows), :] for k in range(CONV_W - 1)] + [xr]


def _rnn_gates(xbuf, xr, tail, cw, cb, wa_ref, wx_ref, ba, bx, sp, reset):
    xs = _conv_taps(xbuf, xr, tail)
    xc = xs[0] * cw[0:1, :]
    for k in range(1, CONV_W):
        xc = xc + xs[k] * cw[k:k + 1, :]
    xc = xc + cb
    xcb = xc.astype(BF16)
    za = jnp.concatenate([_dot(xcb[:, RNN_BW * j:RNN_BW * (j + 1)], wa_ref[j]) for j in range(RNN_BLOCKS)], axis=1) + ba
    zx = jnp.concatenate([_dot(xcb[:, RNN_BW * j:RNN_BW * (j + 1)], wx_ref[j]) for j in range(RNN_BLOCKS)], axis=1) + bx
    r, i = _sigmoid(za), _sigmoid(zx)
    neg_log_a = LRU_C * r * sp
    a_raw = jnp.exp(-neg_log_a)
    mult_raw = jnp.sqrt(jnp.tanh(neg_log_a) * (1.0 + a_raw * a_raw))
    a = jnp.where(reset, 0.0, a_raw)
    mult = jnp.where(reset, 1.0, mult_raw)
    return xc, r, i, a, mult


def _rnn_forward(proj, pos_col, conv_w, conv_b, rwa, rwx, ba, bx, lam):
    T = proj.shape[0]
    tr = min(T, 256)

    def body(x0, x1, g0, g1, pos_ref, cw_ref, cb_ref, wa_ref, wx_ref, ba_ref, bx_ref, lam_ref,
             y_ref, h_ref, xc_ref, r_ref, i_ref, a_ref, mult_ref, xbuf, bbuf, tail, carry):
        t = pl.program_id(0)

        @pl.when(t == 0)
        def _():
            tail[...] = jnp.zeros_like(tail)
            carry[...] = jnp.zeros_like(carry)

        xr = jnp.concatenate([x0[...], x1[...]], axis=1)
        sp = _softplus(-lam_ref[...])
        reset = pos_ref[...] == 0
        xc, r, i, a, mult = _rnn_gates(
            xbuf, xr, tail[...], cw_ref[...], cb_ref[...], wa_ref, wx_ref, ba_ref[...], bx_ref[...], sp, reset)
        xc_ref[...] = xc
        r_ref[...] = r
        i_ref[...] = i
        a_ref[...] = a
        mult_ref[...] = mult
        bbuf[...] = mult * (i * xc)
        last = _scan_forward(a_ref, bbuf, h_ref, carry[0:1, :], tr)
        carry[...] = jnp.broadcast_to(last, carry.shape)
        tail[...] = xr[tr - 8:tr, :]
        g = jnp.concatenate([g0[...], g1[...]], axis=1)
        y_ref[...] = (h_ref[...] * (g * _sigmoid(g))).astype(BF16)

    blk = lambda cb: pl.BlockSpec((tr, CB), lambda t, cb=cb: (t, cb))
    row = lambda w: pl.BlockSpec((1, w), lambda t: (0, 0))
    full3 = pl.BlockSpec((RNN_BLOCKS, RNN_BW, RNN_BW), lambda t: (0, 0, 0))
    tok = pl.BlockSpec((tr, D_MODEL), lambda t: (t, 0))
    act = jax.ShapeDtypeStruct((T, D_MODEL), F32)
    return pl.pallas_call(
        body, out_shape=(jax.ShapeDtypeStruct((T, D_MODEL), BF16),) + (act,) * 6,
        grid=(T // tr,), name="rnn_forward",
        in_specs=[blk(CB_XR), blk(CB_XR + 1), blk(CB_GR), blk(CB_GR + 1), pl.BlockSpec((tr, 1), lambda t: (t, 0)),
                  pl.BlockSpec((CONV_W, D_MODEL), lambda t: (0, 0)), row(D_MODEL), full3, full3,
                  row(D_MODEL), row(D_MODEL), row(D_MODEL)],
        out_specs=(tok,) * 7,
        scratch_shapes=[pltpu.VMEM((tr + 8, D_MODEL), F32), pltpu.VMEM((tr, D_MODEL), F32),
                        pltpu.VMEM((8, D_MODEL), F32), pltpu.VMEM((8, D_MODEL), F32)],
        compiler_params=_cp("arbitrary"),
    )(proj, proj, proj, proj, pos_col, *_in_hbm(conv_w, conv_b, rwa, rwx, ba, bx, lam))


def _merge_and_head(x, target, y_attn, y_rnn, proj, wap, wrp, wo, mod_row, final_g):
    T = x.shape[0]
    tm = min(T, 256)

    def body(x_ref, t_ref, ya_ref, yr_ref, ma0, ma1, mr0, mr1, wap_ref, wrp_ref, wo_ref, mod_ref, fg_ref,
             dx2_ref, mg_ref, do_ref, dpa_ref, dpr_ref, dya_ref, dyr_ref, dc_ref, dfg_ref, dgate_ref, loss_ref):
        i = pl.program_id(0)
        gate = mod_ref[:, 2 * D_MODEL:3 * D_MODEL]
        fg = fg_ref[...]
        pa, pr = _dot(ya_ref[...], wap_ref[...]), _dot(yr_ref[...], wrp_ref[...])
        sa = _sigmoid(jnp.concatenate([ma0[...], ma1[...]], axis=1))
        sr = _sigmoid(jnp.concatenate([mr0[...], mr1[...]], axis=1))
        mb = (sa * pa + sr * pr).astype(BF16)
        o = _dot(mb, wo_ref[...])
        x2 = x_ref[...] + gate * o
        r2 = _rms(x2)
        xn2 = x2 * r2
        err = xn2 * fg - t_ref[...]
        loss_t = 0.5 * jnp.sum(jnp.sum(err * err, axis=-1, keepdims=True) * (1.0 / D_MODEL), axis=0, keepdims=True)
        dy = err * (1.0 / D_MODEL)
        dfg_t = jnp.sum(dy * xn2, axis=0, keepdims=True)
        dxn = dy * fg
        dx2 = r2 * (dxn - xn2 * jnp.mean(dxn * xn2, axis=-1, keepdims=True))
        dgate_t = jnp.sum(dx2 * o, axis=0, keepdims=True)
        dob = (dx2 * gate).astype(BF16)
        dmerged = _dot_nt(dob, wo_ref[...])
        dpa, dpr = (dmerged * sa).astype(BF16), (dmerged * sr).astype(BF16)
        dya, dyr = _dot_nt(dpa, wap_ref[...]), _dot_nt(dpr, wrp_ref[...])
        dx2_ref[...] = dx2
        mg_ref[...] = mb
        do_ref[...] = dob
        dpa_ref[...] = dpa
        dpr_ref[...] = dpr
        dya_ref[...] = dya
        dyr_ref[...] = dyr
        dc_ref[:, 0:D_MODEL] = (dmerged * pa * sa * (1.0 - sa)).astype(BF16)
        dc_ref[:, D_MODEL:2 * D_MODEL] = (dmerged * pr * sr * (1.0 - sr)).astype(BF16)

        @pl.when(i == 0)
        def _():
            dfg_ref[...] = jnp.zeros_like(dfg_ref)
            dgate_ref[...] = jnp.zeros_like(dgate_ref)
            loss_ref[...] = jnp.zeros_like(loss_ref)

        dfg_ref[...] += dfg_t
        dgate_ref[...] += dgate_t
        loss_ref[...] += jnp.broadcast_to(loss_t, loss_ref.shape)

    tok = lambda w: pl.BlockSpec((tm, w), lambda i: (i, 0))
    blk = lambda cb: pl.BlockSpec((tm, CB), lambda i, cb=cb: (i, cb))
    wfull = pl.BlockSpec((D_MODEL, D_MODEL), lambda i: (0, 0), pipeline_mode=pl.Buffered(1))
    row = lambda w: pl.BlockSpec((1, w), lambda i: (0, 0))
    out_shape = (
        jax.ShapeDtypeStruct((T, D_MODEL), F32), jax.ShapeDtypeStruct((T, D_MODEL), BF16),
        jax.ShapeDtypeStruct((T, D_MODEL), BF16), jax.ShapeDtypeStruct((T, D_MODEL), BF16),
        jax.ShapeDtypeStruct((T, D_MODEL), BF16), jax.ShapeDtypeStruct((T, D_MODEL), F32),
        jax.ShapeDtypeStruct((T, D_MODEL), F32), jax.ShapeDtypeStruct((T, 2 * D_MODEL), BF16),
        jax.ShapeDtypeStruct((1, D_MODEL), F32), jax.ShapeDtypeStruct((1, D_MODEL), F32),
        jax.ShapeDtypeStruct((1, 128), F32),
    )
    return pl.pallas_call(
        body, out_shape=out_shape, grid=(T // tm,), name="merge_and_head",
        in_specs=[tok(D_MODEL), tok(D_MODEL), tok(D_MODEL), tok(D_MODEL), blk(CB_MA), blk(CB_MA + 1), blk(CB_MR),
                  blk(CB_MR + 1), wfull, wfull, wfull, row(ADA_W), row(D_MODEL)],
        out_specs=(tok(D_MODEL),) * 7 + (tok(2 * D_MODEL), row(D_MODEL), row(D_MODEL), row(128)),
        compiler_params=_cp("arbitrary"),
    )(x, target, y_attn, y_rnn, proj, proj, proj, proj, wap, wrp, wo, *_in_hbm(mod_row, final_g))


def _attn_backward(proj, qr_b, kr_b, p_all, d_y, tabs, after):
    T = proj.shape[0]
    nb = T // BLOCK

    def body(qrb_ref, krc_ref, krp_ref, vc_ref, vp_ref, g0_ref, g1_ref, dy_ref, p_ref, cc, sac, sbc, cp_, sap, sbp, after_ref,
             dq_ref, dkv_ref, dg_ref, dsink_ref, carry):
        n = pl.program_id(0)

        @pl.when(n == 0)
        def _():
            carry[...] = jnp.zeros_like(carry)
            dsink_ref[...] = jnp.zeros_like(dsink_ref)

        @pl.when(n < nb)
        def _():
            tc = tcur = (cc[...], sac[...], sbc[...])
            tprev = (cp_[...], sap[...], sbp[...])
            qr, kr_cur, kr_prev = qrb_ref[...], krc_ref[...], krp_ref[...]
            v_cur, v_prev = vc_ref[...], vp_ref[...]
            g = jnp.concatenate([g0_ref[...], g1_ref[...]], axis=1)
            sg = _sigmoid(g)
            dy = dy_ref[...]
            d_o = dy * (g * sg)
            mask = _attn_mask(n)
            lane = lax.broadcasted_iota(jnp.int32, (1, 128), 1)
            rowg = lax.broadcasted_iota(jnp.int32, (GROUP * BLOCK, 1), 0) // BLOCK
            o_parts, dq_parts = [], []
            dk_cols, dv_cols = [None, None], [None, None]
            dsink = jnp.zeros((1, 128), F32)
            heads = range(N_KV)
            k2s = [_kv_pair_operand(kr_prev, kr_cur, kh) for kh in heads]
            v2s = [_kv_pair_operand(v_prev, v_cur, kh) for kh in heads]
            q2s = [_pair_rows(qr, kh).astype(BF16) for kh in heads]
            do2s = [_pair_rows(d_o, kh).astype(BF16) for kh in heads]
            dpns = [_fold(_restack(_dot_nt(do2s[kh], v2s[kh])), mask[0]) for kh in heads]
            pns = [jnp.concatenate([p_ref[:, BLOCK * (GROUP * kh + g):BLOCK * (GROUP * kh + g + 1)] for g in range(GROUP)], axis=0)
                   for kh in heads]
            probs = [(pn, 1.0 - jnp.sum(pn, axis=-1, keepdims=True)) for pn in pns]
            p_bigs = [_unrestack(_unfold(probs[kh][0].astype(BF16), mask[0])) for kh in heads]
            o_bigs = [_dot(p_bigs[kh], v2s[kh]) for kh in heads]
            dv2s = [_dot_tn(p_bigs[kh], do2s[kh]) for kh in heads]
            deltas = [jnp.sum(probs[kh][0] * dpns[kh], axis=-1, keepdims=True) for kh in heads]
            ds_bigs = [_unrestack(_unfold((probs[kh][0] * (dpns[kh] - deltas[kh])).astype(BF16), mask[0])) for kh in heads]
            dq2s = [_dot(ds_bigs[kh], k2s[kh]) for kh in heads]
            dk2s = [_dot_tn(ds_bigs[kh], q2s[kh]) for kh in heads]
            for kh in heads:
                o_parts += [o_bigs[kh][0:BLOCK], o_bigs[kh][BLOCK:2 * BLOCK]]
                dq_parts += [dq2s[kh][0:BLOCK], dq2s[kh][BLOCK:2 * BLOCK]]
                dk_c, dv_c = _fold_pair(dk2s[kh], kh), _fold_pair(dv2s[kh], kh)
                c = kh // 2
                dk_cols[c] = dk_c if dk_cols[c] is None else dk_cols[c] + dk_c
                dv_cols[c] = dv_c if dv_cols[c] is None else dv_cols[c] + dv_c
                ds_rows = probs[kh][1] * deltas[kh]
                for gq in range(GROUP):
                    val = -jnp.sum(jnp.where(rowg == gq, ds_rows, 0.0), axis=0, keepdims=True)
                    dsink = dsink + jnp.where(lane == GROUP * kh + ROW_GROUP_HEAD[gq], val, 0.0)
            o = jnp.concatenate(o_parts, axis=1)
            dg_ref[...] = (dy * o * (sg * (1.0 + g * (1.0 - sg)))).astype(BF16)
            dq_ref[...] = (_unrope(jnp.concatenate(dq_parts, axis=1), *tc) * ATTN_SCALE).astype(BF16)
            dk_all, dv_all = jnp.concatenate(dk_cols, axis=1), jnp.concatenate(dv_cols, axis=1)
            dk_prev = _unrope(dk_all[0:BLOCK], *tprev)
            dk_cur = _unrope(dk_all[BLOCK:2 * BLOCK], *tcur)
            dv_prev, dv_cur = dv_all[0:BLOCK], dv_all[BLOCK:2 * BLOCK]
            dkv_ref[...] = (carry[...] + jnp.concatenate([dk_prev, dv_prev], axis=1)).astype(BF16)
            carry[...] = jnp.concatenate([dk_cur, dv_cur], axis=1)
            dsink_ref[...] += dsink

        @pl.when(n == nb)
        def _():
            dkv_ref[...] = carry[...].astype(BF16)

    cur = lambda w, cb: pl.BlockSpec((BLOCK, w), lambda n, cb=cb: (jnp.minimum(n, nb - 1), cb))
    prev = lambda w, cb: pl.BlockSpec((BLOCK, w), lambda n, cb=cb: (jnp.maximum(jnp.minimum(n, nb - 1) - 1, 0), cb))
    out_shape = (jax.ShapeDtypeStruct((T, D_MODEL), BF16), jax.ShapeDtypeStruct((T, 2 * KV_W), BF16),
                 jax.ShapeDtypeStruct((T, D_MODEL), BF16), jax.ShapeDtypeStruct((1, 128), F32))
    return pl.pallas_call(
        body, out_shape=out_shape, grid=(nb + 1,), name="attn_backward",
        in_specs=[cur(D_MODEL, 0), cur(KV_W, 0), prev(KV_W, 0), cur(KV_W, V_COL_BLOCK), prev(KV_W, V_COL_BLOCK),
                  cur(CB, CB_GA), cur(CB, CB_GA + 1), cur(D_MODEL, 0), cur(N_HEADS * BLOCK, 0),
                  cur(128, 0), cur(128, 0), cur(128, 0), prev(128, 0), prev(128, 0), prev(128, 0),
                  pl.BlockSpec(memory_space=pltpu.SMEM)],
        out_specs=(cur(D_MODEL, 0), pl.BlockSpec((BLOCK, 2 * KV_W), lambda n: (jnp.maximum(n - 1, 0), 0)),
                   cur(D_MODEL, 0), pl.BlockSpec((1, 128), lambda n: (0, 0))),
        scratch_shapes=[pltpu.VMEM((BLOCK, 2 * KV_W), F32)],
        compiler_params=_cp("arbitrary"),
    )(qr_b, kr_b, kr_b, proj, proj, proj, proj, d_y, p_all, *tabs, *tabs, after)


def _rnn_backward(proj, pos_col, h_rnn, saved, d_y, conv_w, rwa, rwx, lam):
    T = proj.shape[0]
    tr = min(T, 256)
    nt = T // tr
    hb = tr // 8

    def body(x0, x1, xh0, xh1, g0, g1, pos_ref, h_ref, hh_ref, xc_ref, r_ref, i_ref, a_ref, mult_ref, dy_ref,
             cw_ref, wa_ref, wx_ref, lam_ref, db_ref, dcw_ref, dcb_ref, dwa_ref, dwx_ref, dba_ref, dbx_ref, dlam_ref,
             xbuf, hbuf, dbuf, gbuf, lbuf, mu_carry, dxc_head):
        step = pl.program_id(0)
        first_tile = step == nt - 1

        @pl.when(step == 0)
        def _():
            mu_carry[...] = jnp.zeros_like(mu_carry)
            dxc_head[...] = jnp.zeros_like(dxc_head)
            for ref in (dcw_ref, dcb_ref, dwa_ref, dwx_ref, dba_ref, dbx_ref, dlam_ref):
                ref[...] = jnp.zeros_like(ref)

        xr = jnp.concatenate([x0[...], x1[...]], axis=1)
        tail = jnp.where(first_tile, 0.0, jnp.concatenate([xh0[...], xh1[...]], axis=1))
        lam_v = lam_ref[...]
        sp = _softplus(-lam_v)
        reset = pos_ref[...] == 0
        cw = cw_ref[...]
        xbuf[0:8, :] = tail
        xbuf[8:tr + 8, :] = xr
        g = jnp.concatenate([g0[...], g1[...]], axis=1)
        sg = _sigmoid(g)
        dy = dy_ref[...]
        h = h_ref[...]
        db_ref[:, D_MODEL:2 * D_MODEL] = (dy * h * (sg * (1.0 + g * (1.0 - sg)))).astype(BF16)
        gbuf[...] = dy * (g * sg)
        top = _scan_backward(a_ref, gbuf, lbuf, mu_carry[0:1, :], tr)
        mu_carry[...] = jnp.broadcast_to(top, mu_carry.shape)
        hbuf[0:8, :] = jnp.where(first_tile, 0.0, hh_ref[...])
        hbuf[8:tr + 8, :] = h
        live = jnp.logical_not(reset)
        dbuf[tr:tr + 8, :] = dxc_head[...]
        for j in range(RNN_BLOCKS):
            sl = slice(RNN_BW * j, RNN_BW * (j + 1))
            lam_t, h_prev = lbuf[:, sl], hbuf[pl.ds(7, tr), sl]
            xc, r, i, a, mult = xc_ref[:, sl], r_ref[:, sl], i_ref[:, sl], a_ref[:, sl], mult_ref[:, sl]
            d_a = jnp.where(live, lam_t * h_prev, 0.0)
            d_mult = jnp.where(live, lam_t * (i * xc), 0.0)
            d_ixc = lam_t * mult
            d_i = d_ixc * xc
            d_log_a = d_a * a - d_mult * (a * a / mult)
            d_za = d_log_a * (-LRU_C * sp[:, sl]) * (r * (1.0 - r))
            d_zx = d_i * (i * (1.0 - i))
            dlam_ref[:, sl] += jnp.sum(d_log_a * r, axis=0, keepdims=True) * (LRU_C * _sigmoid(-lam_v[:, sl]))
            dba_ref[:, sl] += jnp.sum(d_za, axis=0, keepdims=True)
            dbx_ref[:, sl] += jnp.sum(d_zx, axis=0, keepdims=True)
            xcb, dzab, dzxb = xc.astype(BF16), d_za.astype(BF16), d_zx.astype(BF16)
            dwa_ref[j] += _dot_tn(xcb, dzab)
            dwx_ref[j] += _dot_tn(xcb, dzxb)
            d_xc = d_ixc * i + (_dot_nt(dzab, wa_ref[j]) + _dot_nt(dzxb, wx_ref[j]))
            dcb_ref[:, sl] += jnp.sum(d_xc, axis=0, keepdims=True)
            for k in range(CONV_W):
                tap = xr[:, sl] if k == CONV_W - 1 else xbuf[pl.ds(8 - (CONV_W - 1 - k), tr), sl]
                dcw_ref[k:k + 1, sl] += jnp.sum(d_xc * tap, axis=0, keepdims=True)
            dbuf[0:tr, sl] = d_xc
            d_xr = d_xc * cw[CONV_W - 1:CONV_W, sl]
            for k in range(CONV_W - 1):
                d_xr = d_xr + dbuf[pl.ds(CONV_W - 1 - k, tr), sl] * cw[k:k + 1, sl]
            dxc_head[:, sl] = d_xc[0:8, :]
            db_ref[:, sl] = d_xr.astype(BF16)

    rev = lambda s: nt - 1 - s
    blk = lambda cb: pl.BlockSpec((tr, CB), lambda s, cb=cb: (rev(s), cb))
    halo = lambda w, cb: pl.BlockSpec((8, w), lambda s, cb=cb: (jnp.maximum(rev(s) * hb - 1, 0), cb))
    tok = lambda w: pl.BlockSpec((tr, w), lambda s: (rev(s), 0))
    row = lambda w: pl.BlockSpec((1, w), lambda s: (0, 0))
    full3 = pl.BlockSpec((RNN_BLOCKS, RNN_BW, RNN_BW), lambda s: (0, 0, 0))
    cwspec = pl.BlockSpec((CONV_W, D_MODEL), lambda s: (0, 0))
    vec = jax.ShapeDtypeStruct((1, D_MODEL), F32)
    gate_w = jax.ShapeDtypeStruct((RNN_BLOCKS, RNN_BW, RNN_BW), F32)
    out_shape = (jax.ShapeDtypeStruct((T, 2 * D_MODEL), BF16), jax.ShapeDtypeStruct((CONV_W, D_MODEL), F32), vec,
                 gate_w, gate_w, vec, vec, vec)
    big = lambda: pltpu.VMEM((tr, D_MODEL), F32)
    ext = lambda: pltpu.VMEM((tr + 8, D_MODEL), F32)
    return pl.pallas_call(
        body, out_shape=out_shape, grid=(nt,), name="rnn_backward",
        in_specs=[blk(CB_XR), blk(CB_XR + 1), halo(CB, CB_XR), halo(CB, CB_XR + 1), blk(CB_GR), blk(CB_GR + 1),
                  pl.BlockSpec((tr, 1), lambda s: (rev(s), 0)), tok(D_MODEL), halo(D_MODEL, 0)] + [tok(D_MODEL)] * 6
        + [cwspec, full3, full3, row(D_MODEL)],
        out_specs=(tok(2 * D_MODEL), cwspec, row(D_MODEL), full3, full3, row(D_MODEL), row(D_MODEL), row(D_MODEL)),
        scratch_shapes=[ext(), ext(), ext(), big(), big(), pltpu.VMEM((8, D_MODEL), F32), pltpu.VMEM((8, D_MODEL), F32)],
        compiler_params=_cp("arbitrary"),
    )(proj, proj, proj, proj, proj, proj, pos_col, h_rnn, h_rnn, *saved, d_y, *_in_hbm(conv_w, rwa, rwx, lam))


def _input_backward(pieces, w_in, x, dx2, mod_row, norm_g):
    T = x.shape[0]
    tm = min(T, 512)
    n = len(pieces)

    def body(*refs):
        d_refs = refs[:n]
        w_ref, x_ref, dx2_ref, mod_ref, g_ref, gx_ref, dshift_ref, dscale_ref, dg_ref = refs[n:]
        i = pl.program_id(0)
        dh = None
        for d_ref, (_, start, count) in zip(d_refs, pieces):
            part = _dot_nt(d_ref[...], w_ref[:, start * CB:(start + count) * CB])
            dh = part if dh is None else dh + part

        @pl.when(i == 0)
        def _():
            dshift_ref[...] = jnp.zeros_like(dshift_ref)
            dscale_ref[...] = jnp.zeros_like(dscale_ref)
            dg_ref[...] = jnp.zeros_like(dg_ref)

        xf = x_ref[...]
        r1 = _rms(xf)
        xn = xf * r1
        gn = g_ref[...]
        s1 = 1.0 + mod_ref[:, D_MODEL:2 * D_MODEL]
        dshift_ref[...] += jnp.sum(dh, axis=0, keepdims=True)
        dscale_ref[...] += jnp.sum(dh * (xn * gn), axis=0, keepdims=True)
        dg_ref[...] += jnp.sum(dh * s1 * xn, axis=0, keepdims=True)
        dxn = dh * s1 * gn
        gx_ref[...] = dx2_ref[...] + r1 * (dxn - xn * jnp.mean(dxn * xn, axis=-1, keepdims=True))

    tok = lambda w: pl.BlockSpec((tm, w), lambda i: (i, 0))
    row = lambda w: pl.BlockSpec((1, w), lambda i: (0, 0))
    vec = jax.ShapeDtypeStruct((1, D_MODEL), F32)
    return pl.pallas_call(
        body, out_shape=(jax.ShapeDtypeStruct((T, D_MODEL), F32), vec, vec, vec), grid=(T // tm,), name="input_backward",
        in_specs=[tok(c * CB) for _, _, c in pieces]
        + [pl.BlockSpec((D_MODEL, IN_W), lambda i: (0, 0), pipeline_mode=pl.Buffered(1)), tok(D_MODEL), tok(D_MODEL),
           row(ADA_W), row(D_MODEL)],
        out_specs=(tok(D_MODEL), row(D_MODEL), row(D_MODEL), row(D_MODEL)),
        compiler_params=_cp("arbitrary"),
    )(*[p[0] for p in pieces], w_in, x, dx2, *_in_hbm(mod_row, norm_g))


def _weight_grad(a, pieces, tag, a_is_transposed=False):
    M, T = a.shape if a_is_transposed else a.shape[::-1]
    n_blocks = sum(count for _, _, count in pieces)
    n = len(pieces)
    contract = _dot if a_is_transposed else _dot_tn

    def body(*refs):
        a_ref, b_refs, o_ref = refs[0], refs[1:1 + n], refs[-1]
        j = pl.program_id(0)
        for b_ref, (_, start, count) in zip(b_refs, pieces):
            @pl.when((j >= start) & (j < start + count))
            def _(b_ref=b_ref):
                o_ref[...] = contract(a_ref[...], b_ref[...])

    def piece_spec(start, count):
        return pl.BlockSpec((T, CB), lambda j: (0, jnp.clip(j - start, 0, count - 1)))

    return pl.pallas_call(
        body, out_shape=jax.ShapeDtypeStruct((M, n_blocks * CB), F32), grid=(n_blocks,), name=f"weight_grad_{tag}",
        in_specs=[pl.BlockSpec(a.shape, lambda j: (0, 0), pipeline_mode=pl.Buffered(1))] + [piece_spec(s, c) for _, s, c in pieces],
        out_specs=pl.BlockSpec((M, CB), lambda j: (0, j)), compiler_params=_cp("arbitrary"),
    )(a, *[p[0] for p in pieces])


def _adamw(w, g, m, v):
    m = ADAM_B1 * m + (1.0 - ADAM_B1) * g
    v = ADAM_B2 * v + (1.0 - ADAM_B2) * (g * g)
    m_hat = m / (1.0 - ADAM_B1 ** ADAM_STEP)
    v_hat = v / (1.0 - ADAM_B2 ** ADAM_STEP)
    delta = -ADAM_LR * (m_hat / (jnp.sqrt(v_hat) + ADAM_EPS) + ADAM_WD * w)
    return delta, m, v


def _sum_landed(kind, owns, lands, where, tag):
    n = len(owns)
    land = lands[0]
    if kind == "in":
        R, C = land.shape[1:]
        tr = 256
        grid = (R // tr,)
        own_spec = pl.BlockSpec((tr, C), lambda i, w: (i, w[0]))
        land_spec = pl.BlockSpec((3, tr, C), lambda i, w: (0, i, 0))
        out_spec = pl.BlockSpec((1, tr, C), lambda i, w: (w[1], i, 0))
        out_shape = (2, R, C)
        pick = lambda ref: ref[...]
    elif kind == "sq":
        R, C = land.shape[1:]
        grid = (1,)
        own_spec = pl.BlockSpec((1, R, C), lambda i, w: (w[0], 0, 0))
        land_spec = pl.BlockSpec((3, R, C), lambda i, w: (0, 0, 0))
        out_spec = pl.BlockSpec((1, R, C), lambda i, w: (w[1], 0, 0))
        out_shape = (2, R, C)
        pick = lambda ref: ref[0]
    else:
        B, R, C = land.shape[1:]
        grid = (1,)
        own_spec = pl.BlockSpec((B, 1, R, C), lambda i, w: (0, w[0], 0, 0))
        land_spec = pl.BlockSpec((3, B, R, C), lambda i, w: (0, 0, 0, 0))
        out_spec = pl.BlockSpec((B, 1, R, C), lambda i, w: (0, w[1], 0, 0))
        out_shape = (B, 2, R, C)
        pick = lambda ref: ref[:, 0]

    def body(w_ref, *refs):
        for k in range(n):
            own_ref, l_ref, o_ref = refs[k], refs[n + k], refs[2 * n + k]
            total = ((pick(own_ref) + l_ref[0].astype(F32)) + l_ref[1].astype(F32)) + l_ref[2].astype(F32)
            if kind == "rg":
                o_ref[:, 0] = total
            else:
                o_ref[0] = total

    grid_spec = pltpu.PrefetchScalarGridSpec(num_scalar_prefetch=1, grid=grid, in_specs=[own_spec] * n + [land_spec] * n,
                                             out_specs=(out_spec,) * n)
    return list(pl.pallas_call(
        body, out_shape=(jax.ShapeDtypeStruct(out_shape, F32),) * n, grid_spec=grid_spec, name=f"sum_landed_{tag}",
        compiler_params=_cp("parallel"),
    )(where, *owns, *lands))


def _adamw_shard(gs, ws, ms, vs, tag):
    n = len(ws)
    R, C = ws[0].shape
    tr = min(R, 256 if n == 1 else 64)

    def body(*refs):
        for k in range(n):
            g = refs[k][...]
            d, nm, nv = _adamw(refs[n + k][...], g, refs[2 * n + k][...], refs[3 * n + k][...])
            out = refs[4 * n + 4 * k:4 * n + 4 * k + 4]
            out[0][...] = g
            out[1][...] = d
            out[2][...] = nm
            out[3][...] = nv

    spec = pl.BlockSpec((tr, C), lambda i: (i, 0))
    sds = jax.ShapeDtypeStruct((R, C), F32)
    outs = pl.pallas_call(
        body, out_shape=(sds,) * (4 * n), grid=(R // tr,), name=f"adamw_{tag}",
        in_specs=[spec] * (4 * n), out_specs=(spec,) * (4 * n), compiler_params=_cp("parallel"),
    )(*gs, *_in_hbm(*ws, *ms, *vs))
    return [outs[4 * k:4 * k + 4] for k in range(n)]


def _adamw_w_ada(c_t, dmod_cols, w, m, v):
    R, C = w.shape

    def body(ct_ref, dm_ref, w_ref, m_ref, v_ref, g_ref, d_ref, nm_ref, nv_ref):
        g = _dot(ct_ref[...].astype(BF16), dm_ref[...].astype(BF16))
        d, nm, nv = _adamw(w_ref[...], g, m_ref[...], v_ref[...])
        g_ref[...] = g
        d_ref[...] = d
        nm_ref[...] = nm
        nv_ref[...] = nv

    tr = 256
    spec = pl.BlockSpec((tr, C), lambda i: (i, 0))
    sds = jax.ShapeDtypeStruct((R, C), F32)
    return pl.pallas_call(
        body, out_shape=(sds,) * 4, grid=(R // tr,), name="adamw_w_ada",
        in_specs=[pl.BlockSpec((tr, 128), lambda i: (i, 0)), pl.BlockSpec((128, C), lambda i: (0, 0))] + [spec] * 3,
        out_specs=(spec,) * 4, compiler_params=_cp("parallel"),
    )(c_t, dmod_cols, w, m, v)


def _adamw_small(small_all, ws, ms, vs):
    def body(s_ref, w_ref, m_ref, v_ref, g_ref, d_ref, nm_ref, nv_ref):
        g = s_ref[0]
        for b in range(1, N_DEV):
            g = g + s_ref[b]
        d, nm, nv = _adamw(w_ref[...], g, m_ref[...], v_ref[...])
        g_ref[...] = g
        d_ref[...] = d
        nm_ref[...] = nm
        nv_ref[...] = nv

    sds = jax.ShapeDtypeStruct((SMALL_ROWS, D_MODEL), F32)
    return pl.pallas_call(
        body, out_shape=(sds,) * 4, name="adamw_small", in_specs=[VMEM_SPEC] * 4, out_specs=(VMEM_SPEC,) * 4,
        compiler_params=pltpu.CompilerParams(vmem_limit_bytes=VMEM_LIMIT_V7X),
    )(small_all, ws, ms, vs)


ROW_MOD, ROW_NORM_G, ROW_CONV_B, ROW_BA, ROW_BX, ROW_LAM, ROW_FINAL_G, ROW_SINKS, ROW_CONV_W, ROW_LOSS = 0, 3, 4, 5, 6, 7, 8, 9, 10, 14


def _pack_small(b_ada, norm_g, conv_b, ba, bx, lam, final_g, sinks, conv_w_full, loss_row=None):
    lane_pad = lambda a: jnp.pad(a.reshape(1, -1), ((0, 0), (0, D_MODEL - a.size)))
    rows = [b_ada.reshape(3, D_MODEL), norm_g, conv_b, ba, bx, lam, final_g.reshape(1, D_MODEL), lane_pad(sinks), conv_w_full,
            jnp.zeros((1, D_MODEL), F32) if loss_row is None else lane_pad(loss_row),
            jnp.zeros((SMALL_ROWS - ROW_LOSS - 1, D_MODEL), F32)]
    return jnp.concatenate([r.astype(F32) for r in rows], axis=0)


def kernel(x, c, positions, w_ada, b_ada, norm_g, w_in, attn_sinks, conv_w, conv_b, rg_wa, rg_ba, rg_wx, rg_bx, rg_lambda, w_attn_proj, w_rnn_proj, w_out, final_g, loss_target, m_w_ada, m_b_ada, m_norm_g, m_w_in, m_attn_sinks, m_conv_w, m_conv_b, m_rg_wa, m_rg_ba, m_rg_wx, m_rg_bx, m_rg_lambda, m_w_attn_proj, m_w_rnn_proj, m_w_out, m_final_g, v_w_ada, v_b_ada, v_norm_g, v_w_in, v_attn_sinks, v_conv_w, v_conv_b, v_rg_wa, v_rg_ba, v_rg_wx, v_rg_bx, v_rg_lambda, v_w_attn_proj, v_w_rnn_proj, v_w_out, v_final_g):
    T = x.shape[1]
    my_chip = lax.axis_index("x") * 2 + lax.axis_index("y")
    my_dev = my_chip * 2 + lax.axis_index("c")
    x2d, tgt = x[0], loss_target[0]
    pos_col = positions.reshape(T, 1)

    chip_idx = my_chip.reshape(1).astype(jnp.int32)
    c_idx = lax.axis_index("c").reshape(1).astype(jnp.int32)
    sq_place = ((D_MODEL, D_MODEL), (SHARD_ROWS, D_MODEL), lambda chip: (chip, 0))
    rg_place = ((RNN_BLOCKS, RNN_BW, RNN_BW), (RNN_BLOCKS, SHARD_RG, RNN_BW), lambda chip: (0, chip, 0))
    in_place = ((D_MODEL, IN_W), (D_MODEL, SHARD_IN), lambda chip: (0, chip))
    placed = _cast_place([w_in[0], w_attn_proj[0], w_rnn_proj[0], w_out[0], rg_wa[0], rg_wx[0]], chip_idx,
                         [in_place, sq_place, sq_place, sq_place, rg_place, rg_place])
    cw_chips, c_all, mod_chips = _gather_mod(c.reshape(1, 1, D_MODEL), w_ada[0], conv_w[0])
    g_ssems, g_rsems, fulls, g_token = _gather_start([p.reshape(s) for p, s in zip(placed, FULL_SHAPES)], mod_chips)
    conv_w_f = jnp.transpose(cw_chips, (1, 0, 2)).reshape(CONV_W, D_MODEL)
    mod_all = jnp.transpose(mod_chips, (1, 0, 2)).reshape(N_DEV, ADA_W) + b_ada
    mod_row = lax.dynamic_slice_in_dim(mod_all, my_dev, 1, axis=0) + g_token[0:1, 0:1]

    h, h_t, tabs = _prenorm(x2d, mod_row, norm_g, pos_col)
    w_in_v = fulls[0]
    proj = _in_projection(h, w_in_v.reshape(D_MODEL, IN_W), chip_idx, None, "own")
    for k, mask in enumerate(CHIP_MASKS):
        w_in_v = _gather_wait(g_ssems[k], g_rsems[k], [w_in_v], [0], proj, f"w_in_{k}")[0]
        w_in_v = _forward_halves([w_in_v], [(0, 0, k)], f"w_in_{k}")[0]
        from_chip = (chip_idx ^ (mask >> 1)).astype(jnp.int32)
        proj = _in_projection(h, w_in_v.reshape(D_MODEL, IN_W), from_chip, proj, f"from_{k}")
    w_in_f = w_in_v.reshape(D_MODEL, IN_W)
    rest = _gather_wait(g_ssems[3], g_rsems[3], list(fulls[1:]), [1, 2, 3, 4, 5], proj, "rest")
    rest = _forward_halves(rest, [(idx - 1, idx, k) for idx in range(1, N_BIG) for k in range(3)], "rest")
    wap_f, wrp_f, wo_f = (g.reshape(D_MODEL, D_MODEL) for g in rest[0:3])
    rwa_f, rwx_f = (g.reshape(RNN_BLOCKS, RNN_BW, RNN_BW) for g in rest[3:5])
    y_attn, qr_b, kr_b, p_all = _attn_forward(proj, tabs, attn_sinks)
    y_rnn, h_rnn, *rnn_saved = _rnn_forward(proj, pos_col, conv_w_f, conv_b, rwa_f, rwx_f, rg_ba, rg_bx, rg_lambda)
    (dx2, merged, d_o, d_pa, d_pr, d_ya, d_yr, d_c, d_final_g, d_gate, loss_vec) = _merge_and_head(
        x2d, tgt, y_attn, y_rnn, proj, wap_f, wrp_f, wo_f, mod_row, final_g.reshape(1, D_MODEL))

    sq = (N_CHIPS, 2, SHARD_ROWS // 2, D_MODEL)
    rg = (RNN_BLOCKS, N_CHIPS, 2, SHARD_RG // 2, RNN_BW)
    rg_flat = (RNN_BLOCKS * N_CHIPS, 2, SHARD_RG // 2, RNN_BW)

    def chip_sum_and_start(views, axes, flat, unflat, tags_, kinds_, group, from_sib=None):
        if from_sib is None:
            from_sib = _swap_halves(views, axes)
        exact, rounded = [None] * len(views), [None] * len(views)
        for shape in dict.fromkeys(flat):
            ids = [k for k, f in enumerate(flat) if f == shape]
            ex, ro = _presum([views[k].reshape(shape) for k in ids],
                             [from_sib[k].reshape(shape[:1] + shape[2:]) for k in ids], c_idx, tags_[ids[0]])
            for k, e, r in zip(ids, ex, ro):
                exact[k], rounded[k] = e.reshape(unflat[k]), r.reshape(unflat[k])
        return _exchange_start(rounded, kinds_, group), exact

    g_ap = _weight_grad(y_attn, [(d_pa, 0, 2)], "w_attn_proj")
    g_rp = _weight_grad(y_rnn, [(d_pr, 0, 2)], "w_rnn_proj")
    g_o = _weight_grad(merged, [(d_o, 0, 2)], "w_out")
    sq_half = (N_CHIPS, SHARD_ROWS // 2, D_MODEL)
    views1 = [g_ap.reshape(sq), g_rp.reshape(sq), g_o.reshape(sq)]
    sw_ssems, sw_rsems, views1, sib1, sw_token = _swap_halves_start(views1, [1, 1, 1], "proj")
    d_q, d_kv, d_ga, d_sinks = _attn_backward(proj, qr_b, kr_b, p_all, d_ya, tabs, sw_token[0:1, 0:16])
    views1, sib1 = _swap_halves_wait(sw_ssems, sw_rsems, views1, sib1, d_q, "proj")
    started1, own1 = chip_sum_and_start(views1, [1, 1, 1], [sq] * 3, [sq_half] * 3,
                                        ["w_attn_proj", "w_rnn_proj", "w_out"], ["sq"] * 3, "proj", from_sib=sib1)
    d_b, d_conv_w, d_conv_b, d_rwa, d_rwx, d_ba, d_bx, d_lam = _rnn_backward(
        proj, pos_col, h_rnn, rnn_saved, d_yr, conv_w_f, rwa_f, rwx_f, rg_lambda + started1[4][0:1, 0:1])
    pieces = [(d_q, CB_Q, 2), (d_kv, CB_KV, 1), (d_ga, CB_GA, 2), (d_b, CB_XR, 4), (d_c, CB_MA, 4)]
    g_in = _weight_grad(h_t, pieces, "w_in", a_is_transposed=True)
    started2, own2 = chip_sum_and_start(
        [g_in.reshape(2, D_MODEL // 2, IN_W), d_rwa.reshape(rg), d_rwx.reshape(rg)], [0, 2, 2],
        [(1, 2, D_MODEL // 2, IN_W), rg_flat, rg_flat],
        [(D_MODEL // 2, IN_W), (RNN_BLOCKS, N_CHIPS, SHARD_RG // 2, RNN_BW), (RNN_BLOCKS, N_CHIPS, SHARD_RG // 2, RNN_BW)],
        ["w_in", "rg_wa", "rg_wx"], ["in", "rg", "rg"], "in")
    grad_x, d_shift, d_scale, d_norm_g = _input_backward(pieces, w_in_f, x2d, dx2, mod_row + started2[4][0, 0], norm_g)

    d_mod = jnp.concatenate([d_shift, d_scale, d_gate], axis=1)
    small = _pack_small(d_mod, d_norm_g, d_conv_b, d_ba, d_bx, d_lam, d_final_g, d_sinks[:, :N_HEADS], d_conv_w, loss_vec)
    slabs = lax.dynamic_update_slice(jnp.zeros((N_DEV, SMALL_ROWS, D_MODEL), F32), small[None], (my_dev, 0, 0))
    gs_ssem, gs_rsem, slabs, gs_token = _gather_small_start(slabs)
    _, lands1 = _exchange_wait(*started1[:4], gs_token, "proj")
    _, lands2 = _exchange_wait(*started2[:4], gs_token, "in")
    tags = ["w_in", "w_attn_proj", "w_rnn_proj", "w_out", "rg_wa", "rg_wx"]
    chip_sums = [own2[0]] + list(own1) + list(own2[1:])
    lands = [lands2[0]] + list(lands1) + list(lands2[1:])
    where = jnp.concatenate([chip_idx, c_idx])
    kinds = ["in", "sq", "sq", "sq", "rg", "rg"]
    groups = [[0], [1, 2, 3], [4, 5]]
    halves = [None] * 6
    for ids in groups:
        for i, half in zip(ids, _sum_landed(kinds[ids[0]], [chip_sums[i] for i in ids], [lands[i] for i in ids], where,
                                            tags[ids[0]])):
            halves[i] = half
    grads = _assemble_with_sibling(halves, [0, 0, 0, 0, 1, 1])
    shapes2d = [(D_MODEL, SHARD_IN), (SHARD_ROWS, D_MODEL), (SHARD_ROWS, D_MODEL), (SHARD_ROWS, D_MODEL),
                (RNN_BLOCKS * SHARD_RG, RNN_BW), (RNN_BLOCKS * SHARD_RG, RNN_BW)]
    big_w = [w_in, w_attn_proj, w_rnn_proj, w_out, rg_wa, rg_wx]
    big_m = [m_w_in, m_w_attn_proj, m_w_rnn_proj, m_w_out, m_rg_wa, m_rg_wx]
    big_v = [v_w_in, v_w_attn_proj, v_w_rnn_proj, v_w_out, v_rg_wa, v_rg_wx]
    res = {}
    for ids in groups:
        flat2d = lambda arrs: [arrs[i].reshape(shapes2d[i]) for i in ids]
        outs = _adamw_shard(flat2d(grads), flat2d(big_w), flat2d(big_m), flat2d(big_v), tags[ids[0]])
        for i, four in zip(ids, outs):
            res[tags[i]] = [o.reshape(big_w[i].shape) for o in four]

    small_all = _gather_small_wait(gs_ssem, gs_rsem, slabs, res["w_in"][1])
    dmod_all = small_all[:, ROW_MOD:ROW_MOD + 3, :].reshape(N_DEV, ADA_W)
    dmod_cols = lax.dynamic_slice_in_dim(dmod_all, my_chip * SHARD_ADA, SHARD_ADA, axis=1)
    c_t = jnp.pad(jnp.transpose(c_all.reshape(N_DEV, D_MODEL)), ((0, 0), (0, 128 - N_DEV)))
    dmod_cols = jnp.pad(dmod_cols, ((0, 128 - N_DEV), (0, 0)))
    res["w_ada"] = [o.reshape(w_ada.shape) for o in _adamw_w_ada(c_t, dmod_cols, w_ada[0], m_w_ada[0], v_w_ada[0])]

    def full_conv(a):
        return lax.dynamic_update_slice_in_dim(jnp.zeros((CONV_W, D_MODEL), F32), a[0], my_chip * (D_MODEL // N_CHIPS), axis=1)

    packed = [_pack_small(p[0], p[1], p[2], p[3], p[4], p[5], p[6], p[7], full_conv(p[8])) for p in (
        (b_ada, norm_g, conv_b, rg_ba, rg_bx, rg_lambda, final_g, attn_sinks, conv_w),
        (m_b_ada, m_norm_g, m_conv_b, m_rg_ba, m_rg_bx, m_rg_lambda, m_final_g, m_attn_sinks, m_conv_w),
        (v_b_ada, v_norm_g, v_conv_b, v_rg_ba, v_rg_bx, v_rg_lambda, v_final_g, v_attn_sinks, v_conv_w))]
    small_out = _adamw_small(small_all, *packed)

    def unpack(slab):
        cw = lax.dynamic_slice_in_dim(slab[ROW_CONV_W:ROW_CONV_W + CONV_W], my_chip * (D_MODEL // N_CHIPS),
                                      D_MODEL // N_CHIPS, axis=1)
        return {
            "b_ada": slab[ROW_MOD:ROW_MOD + 3].reshape(1, ADA_W), "norm_g": slab[ROW_NORM_G:ROW_NORM_G + 1],
            "conv_b": slab[ROW_CONV_B:ROW_CONV_B + 1], "rg_ba": slab[ROW_BA:ROW_BA + 1], "rg_bx": slab[ROW_BX:ROW_BX + 1],
            "rg_lambda": slab[ROW_LAM:ROW_LAM + 1], "final_g": slab[ROW_FINAL_G], "attn_sinks": slab[ROW_SINKS:ROW_SINKS + 1, :N_HEADS],
            "conv_w": cw[None],
        }

    small_res = [unpack(s) for s in small_out]
    order = ["w_ada", "b_ada", "norm_g", "w_in", "attn_sinks", "conv_w", "conv_b", "rg_wa", "rg_ba", "rg_wx", "rg_bx",
             "rg_lambda", "w_attn_proj", "w_rnn_proj", "w_out", "final_g"]
    loss = small_out[0][ROW_LOSS, 0]
    outs = [loss, grad_x[None]]
    for kind in range(4):
        for name in order:
            outs.append(res[name][kind] if name in res else small_res[kind][name])
    return tuple(outs)
```

```python
import numpy as np
import jax
import jax.numpy as jnp
from jax import lax
from jax.experimental import pallas as pl
from jax.experimental.pallas import tpu as pltpu

F32 = jnp.float32
BF16 = jnp.bfloat16

D_MODEL = 1024
N_HEADS = 16
N_KV = 4
HEAD_DIM = 64
GROUP = N_HEADS // N_KV
BLOCK = 128
KV_W = N_KV * HEAD_DIM
ROT_HALF = 8
ROPE_THETA = 500000.0
ATTN_SCALE = 0.125
RNN_BLOCKS = 4
RNN_BW = 256
CONV_W = 4
LRU_C = 8.0
NORM_EPS = 1e-6
IN_W = 6656
CB = 512
N_CB = IN_W // CB
CB_Q, CB_KV, CB_GA, CB_XR, CB_GR, CB_MA, CB_MR = 0, 2, 3, 5, 7, 9, 11
V_COL_BLOCK = 5
N_CHIPS = 4
N_DEV = 8
SHARD_IN = IN_W // N_CHIPS
SHARD_ROWS = D_MODEL // N_CHIPS
SHARD_RG = RNN_BW // N_CHIPS
ADA_W = 3 * D_MODEL
SHARD_ADA = ADA_W // N_CHIPS
SMALL_ROWS = 16

ADAM_LR = 0.001
ADAM_B1 = 0.9
ADAM_B2 = 0.999
ADAM_EPS = 1e-08
ADAM_WD = 0.01
ADAM_STEP = 10

VMEM_LIMIT_V7X = 52 * 1024 * 1024
MESH = pl.DeviceIdType.MESH
ANY = pl.BlockSpec(memory_space=pl.ANY)
VMEM_SPEC = pl.BlockSpec(memory_space=pltpu.VMEM)


def _in_hbm(*arrays):
    return [pltpu.with_memory_space_constraint(a, pltpu.HBM) for a in arrays]


def _cp(*sem):
    return pltpu.CompilerParams(dimension_semantics=sem if sem else None, vmem_limit_bytes=VMEM_LIMIT_V7X)


def _dot(a, b):
    return jnp.dot(a, b, preferred_element_type=F32)


def _dot_nt(a, b):
    return lax.dot_general(a, b, (((1,), (1,)), ((), ())), preferred_element_type=F32)


def _dot_tn(a, b):
    return lax.dot_general(a, b, (((0,), (0,)), ((), ())), preferred_element_type=F32)


def _sigmoid(z):
    return 1.0 / (1.0 + jnp.exp(-z))


def _softplus(z):
    u = jnp.exp(-jnp.abs(z))
    log1p_u = jnp.where(u < 1e-3, u * (1.0 - u * (0.5 - u * (1.0 / 3.0))), jnp.log(1.0 + u))
    return jnp.maximum(z, 0.0) + log1p_u


def _rms(xf):
    return lax.rsqrt(jnp.mean(xf * xf, axis=-1, keepdims=True) + NORM_EPS)


def _me():
    return lax.axis_index("x"), lax.axis_index("y"), lax.axis_index("c")


def _peer(mask):
    x, y, c = _me()
    fx, fy, fc = (mask >> 2) & 1, (mask >> 1) & 1, mask & 1
    return (x ^ fx if fx else x, y ^ fy if fy else y, c ^ fc if fc else c)


def _chip_of(pos):
    return pos[0] * 2 + pos[1]


SIBLING_COLLECTIVE_ID = 0
SIBLING_ONLY = pltpu.CompilerParams(collective_id=SIBLING_COLLECTIVE_ID)


def _sibling_handshake():
    barrier = pltpu.get_barrier_semaphore()
    pl.semaphore_signal(barrier, inc=1, device_id=_peer(1), device_id_type=MESH)
    pl.semaphore_wait(barrier, 1)


CHIP_MASKS = (4, 2, 6)
ALL_MASKS = (1, 2, 3, 4, 5, 6, 7)


HBM_SPEC = pl.BlockSpec(memory_space=pltpu.HBM)
SEM_SPEC = pl.BlockSpec(memory_space=pltpu.SEMAPHORE)
SPLIT_COPY = pltpu.CompilerParams(has_side_effects=pltpu.SideEffectType.DATAFLOW_SIDE_EFFECTING)
N_BIG = 6
FULL_SHAPES = (
    (2, D_MODEL // 2, IN_W),
    (N_CHIPS, 2, SHARD_ROWS // 2, D_MODEL), (N_CHIPS, 2, SHARD_ROWS // 2, D_MODEL), (N_CHIPS, 2, SHARD_ROWS // 2, D_MODEL),
    (RNN_BLOCKS, N_CHIPS, 2, SHARD_RG // 2, RNN_BW), (RNN_BLOCKS, N_CHIPS, 2, SHARD_RG // 2, RNN_BW),
)


def _slot(full, idx, chip, half):
    if idx == 0:
        return full.at[half, :, pl.ds(pl.multiple_of(chip * SHARD_IN, 128), SHARD_IN)]
    return full.at[chip, half] if idx in (1, 2, 3) else full.at[:, chip, half]


def _three_halves(full, idx):
    return full.at[pl.ds(0, 3), 0] if idx in (1, 2, 3) else full.at[:, pl.ds(0, 3), 0]


def _gather_start(fulls, after):
    def body(*refs):
        full_refs = refs[:N_BIG]
        ssems, rsems = refs[N_BIG + 1:N_BIG + 5], refs[N_BIG + 5:N_BIG + 9]
        token = refs[2 * N_BIG + 9]
        me = _me()
        my_chip = _chip_of(me)
        for idx in range(N_BIG):
            for k, mask in enumerate(CHIP_MASKS):
                pair = k if idx == 0 else 3
                mine = _slot(full_refs[idx], idx, my_chip, me[2])
                pltpu.make_async_remote_copy(src_ref=mine, dst_ref=mine, send_sem=ssems[pair], recv_sem=rsems[pair],
                                             device_id=_peer(mask), device_id_type=MESH).start()
        token[...] = jnp.zeros_like(token)

    sem = pltpu.SemaphoreType.DMA(())
    out_shape = (sem,) * 8 + tuple(pltpu.HBM(f.shape, f.dtype) for f in fulls) + (jax.ShapeDtypeStruct((8, 128), F32),)
    outs = pl.pallas_call(
        body, out_shape=out_shape, name="gather_start",
        in_specs=[HBM_SPEC] * N_BIG + [ANY], out_specs=tuple([SEM_SPEC] * 8 + [HBM_SPEC] * N_BIG + [VMEM_SPEC]),
        input_output_aliases={i: 8 + i for i in range(N_BIG)}, compiler_params=SPLIT_COPY,
    )(*[pltpu.with_memory_space_constraint(f, pltpu.HBM) for f in fulls], after)
    return outs[0:4], outs[4:8], outs[8:8 + N_BIG], outs[8 + N_BIG]


def _gather_wait(ssem, rsem, arrays, idxs, after, tag):
    n = len(arrays)

    def body(*refs):
        full_refs, ssem_ref, rsem_ref = refs[:n], refs[n], refs[n + 1]
        me = _me()
        for full, idx in zip(full_refs, idxs):
            region = _slot(full, 0, _chip_of(me), me[2]) if idx == 0 else _three_halves(full, idx)
            arrived = pltpu.make_async_remote_copy(
                src_ref=region, dst_ref=region, send_sem=ssem_ref, recv_sem=rsem_ref, device_id=me, device_id_type=MESH)
            arrived.wait_send()
            arrived.wait_recv()

    outs = pl.pallas_call(
        body, out_shape=tuple(pltpu.HBM(a.shape, a.dtype) for a in arrays), name=f"gather_wait_{tag}",
        in_specs=[HBM_SPEC] * n + [SEM_SPEC, SEM_SPEC, ANY], out_specs=tuple([HBM_SPEC] * n),
        input_output_aliases={i: i for i in range(n)}, compiler_params=SPLIT_COPY,
    )(*arrays, ssem, rsem, after)
    return list(outs)


def _forward_halves(arrays, items, tag):
    n, m = len(arrays), len(items)

    def body(*refs):
        outs, ssem, rsem = refs[n:2 * n], refs[2 * n], refs[2 * n + 1]
        me = _me()
        sib = _peer(1)
        _sibling_handshake()
        cps = []
        for j, (pos, idx, k) in enumerate(items):
            chip = _chip_of(_peer(CHIP_MASKS[k]))
            cp = pltpu.make_async_remote_copy(
                src_ref=_slot(outs[pos], idx, chip, me[2]), dst_ref=_slot(outs[pos], idx, chip, me[2]),
                send_sem=ssem.at[j], recv_sem=rsem.at[j], device_id=sib, device_id_type=MESH)
            cp.start()
            cps.append(cp)
        for j, (pos, idx, k) in enumerate(items):
            chip = _chip_of(_peer(CHIP_MASKS[k]))
            pltpu.make_async_remote_copy(
                src_ref=_slot(outs[pos], idx, chip, me[2]), dst_ref=_slot(outs[pos], idx, chip, 1 - me[2]),
                send_sem=ssem.at[j], recv_sem=rsem.at[j], device_id=sib, device_id_type=MESH).wait_recv()
        for cp in cps:
            cp.wait_send()

    outs = pl.pallas_call(
        body, out_shape=tuple(jax.ShapeDtypeStruct(a.shape, a.dtype) for a in arrays), name=f"forward_halves_{tag}",
        in_specs=[ANY] * n, out_specs=tuple([ANY] * n), input_output_aliases={i: i for i in range(n)},
        scratch_shapes=[pltpu.SemaphoreType.DMA((m,)), pltpu.SemaphoreType.DMA((m,))], compiler_params=SIBLING_ONLY,
    )(*arrays)
    return list(outs)


def _gather_mod(c_row, w_ada_s, conv_w_s):
    def body(c_ref, wada_ref, cw_s, cw_f, call_ref, mod_ref, wsend, wrecv, lsem, csend, crecv, msend, mrecv):
        me = _me()
        my_chip = _chip_of(me)
        my_dev = my_chip * 2 + me[2]
        sends = []
        for k, mask in enumerate(CHIP_MASKS):
            cp = pltpu.make_async_remote_copy(src_ref=cw_s, dst_ref=cw_f.at[my_chip], send_sem=wsend.at[k], recv_sem=wrecv.at[k],
                                              device_id=_peer(mask), device_id_type=MESH)
            cp.start()
            sends.append(cp)
        local = [pltpu.make_async_copy(cw_s, cw_f.at[my_chip], lsem.at[0])]
        for cp in local:
            cp.start()

        call_ref[my_dev] = c_ref[0]
        csends = []
        for k, mask in enumerate(ALL_MASKS):
            cp = pltpu.make_async_remote_copy(
                src_ref=c_ref.at[0], dst_ref=call_ref.at[my_dev],
                send_sem=csend.at[k], recv_sem=crecv.at[k], device_id=_peer(mask), device_id_type=MESH)
            cp.start()
            csends.append(cp)
        for k, mask in enumerate(ALL_MASKS):
            frm = _peer(mask)
            pltpu.make_async_remote_copy(
                src_ref=c_ref.at[0], dst_ref=call_ref.at[_chip_of(frm) * 2 + frm[2]],
                send_sem=csend.at[k], recv_sem=crecv.at[k], device_id=frm, device_id_type=MESH).wait_recv()
        for cp in csends:
            cp.wait_send()

        c_all = call_ref[...].reshape(N_DEV, D_MODEL).astype(BF16)
        mod_ref[my_chip] = _dot(c_all, wada_ref[...].astype(BF16))
        msends = []
        for k, mask in enumerate(CHIP_MASKS):
            cp = pltpu.make_async_remote_copy(
                src_ref=mod_ref.at[my_chip], dst_ref=mod_ref.at[my_chip],
                send_sem=msend.at[k], recv_sem=mrecv.at[k], device_id=_peer(mask), device_id_type=MESH)
            cp.start()
            msends.append(cp)
        for k, mask in enumerate(CHIP_MASKS):
            frm = _peer(mask)
            pltpu.make_async_remote_copy(
                src_ref=mod_ref.at[my_chip], dst_ref=mod_ref.at[_chip_of(frm)],
                send_sem=msend.at[k], recv_sem=mrecv.at[k], device_id=frm, device_id_type=MESH).wait_recv()
        for cp in msends:
            cp.wait_send()

        for k, mask in enumerate(CHIP_MASKS):
            frm = _peer(mask)
            pltpu.make_async_remote_copy(src_ref=cw_s, dst_ref=cw_f.at[_chip_of(frm)], send_sem=wsend.at[k], recv_sem=wrecv.at[k],
                                         device_id=frm, device_id_type=MESH).wait_recv()
        for cp in sends:
            cp.wait_send()
        for cp in local:
            cp.wait()

    out_shape = (
        jax.ShapeDtypeStruct((N_CHIPS, CONV_W, D_MODEL // N_CHIPS), F32),
        jax.ShapeDtypeStruct((N_DEV, 1, D_MODEL), F32),
        jax.ShapeDtypeStruct((N_CHIPS, N_DEV, SHARD_ADA), F32),
    )
    return pl.pallas_call(
        body, out_shape=out_shape, name="gather_mod",
        in_specs=[VMEM_SPEC, VMEM_SPEC, ANY], out_specs=(ANY, VMEM_SPEC, VMEM_SPEC),
        scratch_shapes=[
            pltpu.SemaphoreType.DMA((3,)), pltpu.SemaphoreType.DMA((3,)), pltpu.SemaphoreType.DMA((1,)),
            pltpu.SemaphoreType.DMA((7,)), pltpu.SemaphoreType.DMA((7,)),
            pltpu.SemaphoreType.DMA((3,)), pltpu.SemaphoreType.DMA((3,)),
        ],
        compiler_params=pltpu.CompilerParams(vmem_limit_bytes=VMEM_LIMIT_V7X),
    )(c_row, w_ada_s, conv_w_s)


def _cast_place(shards, chip_idx, places):
    n = len(shards)

    def body(chip_ref, *refs):
        for s_ref, o_ref in zip(refs[:n], refs[n:]):
            o_ref[...] = s_ref[...].astype(BF16)

    grid_spec = pltpu.PrefetchScalarGridSpec(
        num_scalar_prefetch=1, grid=(1,),
        in_specs=[pl.BlockSpec(s.shape, lambda i, chip_ref, nd=s.ndim: (0,) * nd) for s in shards],
        out_specs=tuple(pl.BlockSpec(block, lambda i, chip_ref, im=im: im(chip_ref[0])) for _, block, im in places))
    return pl.pallas_call(
        body, out_shape=tuple(jax.ShapeDtypeStruct(full, BF16) for full, _, _ in places), grid_spec=grid_spec,
        name="cast_place", compiler_params=_cp("arbitrary"),
    )(chip_idx, *_in_hbm(*shards))


def _shard_of(ref, kind, chip):
    if kind == "in":
        return ref.at[:, pl.ds(pl.multiple_of(chip * SHARD_IN, 128), SHARD_IN)]
    return ref.at[chip] if kind == "sq" else ref.at[:, chip]


def _land_shape(src, kind):
    if kind == "in":
        return (3, src.shape[0], SHARD_IN)
    return (3,) + src.shape[1:] if kind == "sq" else (3, src.shape[0]) + src.shape[2:]


def _exchange_start(srcs, kinds, tag):
    n = len(srcs)
    lands = [pltpu.with_memory_space_constraint(lax.empty(_land_shape(s, k), s.dtype), pltpu.HBM) for s, k in zip(srcs, kinds)]

    def body(*refs):
        src_refs, land_refs = refs[:n], refs[n:2 * n]
        ssems, rsems = refs[2 * n:3 * n], refs[3 * n:4 * n]
        token = refs[6 * n]
        for i in range(n):
            for k, mask in enumerate(CHIP_MASKS):
                to = _peer(mask)
                pltpu.make_async_remote_copy(
                    src_ref=_shard_of(src_refs[i], kinds[i], _chip_of(to)), dst_ref=land_refs[i].at[k],
                    send_sem=ssems[i], recv_sem=rsems[i], device_id=to, device_id_type=MESH).start()
        token[...] = jnp.zeros_like(token)

    sem = pltpu.SemaphoreType.DMA(())
    out_shape = ((sem,) * (2 * n) + tuple(pltpu.HBM(s.shape, s.dtype) for s in srcs)
                 + tuple(pltpu.HBM(l.shape, l.dtype) for l in lands) + (jax.ShapeDtypeStruct((8, 128), F32),))
    outs = pl.pallas_call(
        body, out_shape=out_shape, name=f"exchange_start_{tag}",
        in_specs=[HBM_SPEC] * (2 * n), out_specs=tuple([SEM_SPEC] * (2 * n) + [HBM_SPEC] * (2 * n) + [VMEM_SPEC]),
        input_output_aliases={i: 2 * n + i for i in range(2 * n)},
        compiler_params=pltpu.CompilerParams(has_side_effects=pltpu.SideEffectType.DATAFLOW_SIDE_EFFECTING),
    )(*[pltpu.with_memory_space_constraint(s, pltpu.HBM) for s in srcs], *lands)
    return outs[:n], outs[n:2 * n], outs[2 * n:3 * n], outs[3 * n:4 * n], outs[4 * n]


def _exchange_wait(ssems, rsems, srcs, lands, after, tag):
    n = len(srcs)

    def body(*refs):
        land_refs = refs[n:2 * n]
        ssem_refs, rsem_refs = refs[2 * n:3 * n], refs[3 * n:4 * n]
        for i in range(n):
            all_three = pltpu.make_async_remote_copy(
                src_ref=land_refs[i], dst_ref=land_refs[i], send_sem=ssem_refs[i], recv_sem=rsem_refs[i],
                device_id=_me(), device_id_type=MESH)
            all_three.wait_send()
            all_three.wait_recv()

    outs = pl.pallas_call(
        body, out_shape=tuple(pltpu.HBM(a.shape, a.dtype) for a in list(srcs) + list(lands)), name=f"exchange_wait_{tag}",
        in_specs=[HBM_SPEC] * (2 * n) + [SEM_SPEC] * (2 * n) + [ANY], out_specs=tuple([HBM_SPEC] * (2 * n)),
        input_output_aliases={i: i for i in range(2 * n)},
        compiler_params=pltpu.CompilerParams(has_side_effects=pltpu.SideEffectType.DATAFLOW_SIDE_EFFECTING),
    )(*srcs, *lands, *ssems, *rsems, after)
    return outs[:n], outs[n:]


def _gather_small_start(slabs):
    def body(slabs_ref, ssem, rsem, slabs_out, token):
        me = _me()
        mine = slabs_ref.at[_chip_of(me) * 2 + me[2]]
        for mask in ALL_MASKS:
            pltpu.make_async_remote_copy(src_ref=mine, dst_ref=mine, send_sem=ssem, recv_sem=rsem,
                                         device_id=_peer(mask), device_id_type=MESH).start()
        token[...] = jnp.zeros_like(token)

    sem = pltpu.SemaphoreType.DMA(())
    return pl.pallas_call(
        body, out_shape=(sem, sem, pltpu.HBM(slabs.shape, slabs.dtype), jax.ShapeDtypeStruct((8, 128), F32)),
        name="gather_small_start", in_specs=[HBM_SPEC], out_specs=(SEM_SPEC, SEM_SPEC, HBM_SPEC, VMEM_SPEC),
        input_output_aliases={0: 2}, compiler_params=SPLIT_COPY,
    )(pltpu.with_memory_space_constraint(slabs, pltpu.HBM))


def _gather_small_wait(ssem, rsem, slabs, after):
    def body(slabs_ref, ssem_ref, rsem_ref, after_ref, slabs_out):
        seven = slabs_ref.at[pl.ds(0, N_DEV - 1)]
        arrived = pltpu.make_async_remote_copy(
            src_ref=seven, dst_ref=seven, send_sem=ssem_ref, recv_sem=rsem_ref, device_id=_me(), device_id_type=MESH)
        arrived.wait_send()
        arrived.wait_recv()

    return pl.pallas_call(
        body, out_shape=pltpu.HBM(slabs.shape, slabs.dtype), name="gather_small_wait",
        in_specs=[HBM_SPEC, SEM_SPEC, SEM_SPEC, ANY], out_specs=HBM_SPEC, input_output_aliases={0: 0},
        compiler_params=SPLIT_COPY,
    )(slabs, ssem, rsem, after)


def _half_of(ref, axis, half):
    return ref.at[(slice(None),) * axis + (half,)]


def _swap_halves(parts, axes):
    n = len(parts)

    def body(*refs):
        ins, outs, ssem, rsem = refs[:n], refs[n:2 * n], refs[2 * n], refs[2 * n + 1]
        c = lax.axis_index("c")
        _sibling_handshake()
        cps = [pltpu.make_async_remote_copy(src_ref=_half_of(ins[i], axes[i], 1 - c), dst_ref=outs[i], send_sem=ssem.at[i],
                                            recv_sem=rsem.at[i], device_id=_peer(1), device_id_type=MESH) for i in range(n)]
        for cp in cps:
            cp.start()
        for cp in cps:
            cp.wait()

    shapes = [p.shape[:a] + p.shape[a + 1:] for p, a in zip(parts, axes)]
    return pl.pallas_call(
        body, out_shape=tuple(jax.ShapeDtypeStruct(s, p.dtype) for s, p in zip(shapes, parts)), name="swap_halves",
        in_specs=[ANY] * n, out_specs=tuple([ANY] * n),
        scratch_shapes=[pltpu.SemaphoreType.DMA((n,)), pltpu.SemaphoreType.DMA((n,))], compiler_params=SIBLING_ONLY,
    )(*parts)


def _swap_halves_start(parts, axes, tag):
    n = len(parts)
    lands = [pltpu.with_memory_space_constraint(lax.empty(p.shape[:a] + p.shape[a + 1:], p.dtype), pltpu.HBM)
             for p, a in zip(parts, axes)]

    def body(*refs):
        ins, land_refs, ssems, rsems, token = refs[:n], refs[n:2 * n], refs[2 * n:3 * n], refs[3 * n:4 * n], refs[6 * n]
        c = lax.axis_index("c")
        for i in range(n):
            pltpu.make_async_remote_copy(src_ref=_half_of(ins[i], axes[i], 1 - c), dst_ref=land_refs[i], send_sem=ssems[i],
                                         recv_sem=rsems[i], device_id=_peer(1), device_id_type=MESH).start()
        token[...] = jnp.zeros_like(token)

    sem = pltpu.SemaphoreType.DMA(())
    out_shape = ((sem,) * (2 * n) + tuple(pltpu.HBM(a.shape, a.dtype) for a in list(parts) + lands)
                 + (jax.ShapeDtypeStruct((8, 128), F32),))
    outs = pl.pallas_call(
        body, out_shape=out_shape, name=f"swap_halves_start_{tag}",
        in_specs=[HBM_SPEC] * (2 * n), out_specs=tuple([SEM_SPEC] * (2 * n) + [HBM_SPEC] * (2 * n) + [VMEM_SPEC]),
        input_output_aliases={i: 2 * n + i for i in range(2 * n)}, compiler_params=SPLIT_COPY,
    )(*[pltpu.with_memory_space_constraint(p, pltpu.HBM) for p in parts], *lands)
    return outs[:n], outs[n:2 * n], outs[2 * n:3 * n], outs[3 * n:4 * n], outs[4 * n]


def _swap_halves_wait(ssems, rsems, parts, lands, after, tag):
    n = len(parts)

    def body(*refs):
        land_refs, ssem_refs, rsem_refs = refs[n:2 * n], refs[2 * n:3 * n], refs[3 * n:4 * n]
        for i in range(n):
            moved = pltpu.make_async_remote_copy(
                src_ref=land_refs[i], dst_ref=land_refs[i], send_sem=ssem_refs[i], recv_sem=rsem_refs[i],
                device_id=_me(), device_id_type=MESH)
            moved.wait_send()
            moved.wait_recv()

    outs = pl.pallas_call(
        body, out_shape=tuple(pltpu.HBM(a.shape, a.dtype) for a in list(parts) + list(lands)), name=f"swap_halves_wait_{tag}",
        in_specs=[HBM_SPEC] * (2 * n) + [SEM_SPEC] * (2 * n) + [ANY], out_specs=tuple([HBM_SPEC] * (2 * n)),
        input_output_aliases={i: i for i in range(2 * n)}, compiler_params=SPLIT_COPY,
    )(*parts, *lands, *ssems, *rsems, after)
    return list(outs[:n]), list(outs[n:])


def _presum(mines, sibs, c_idx, tag):
    n = len(mines)
    S, _, R, C = mines[0].shape
    tr = min(R, 256)
    tc = SHARD_IN if C % SHARD_IN == 0 else (C // 2 if n > 1 and C % 256 == 0 else C)

    def body(c_ref, *refs):
        for k in range(n):
            total = refs[k][:, 0] + refs[n + k][...]
            refs[2 * n + k][...] = total
            refs[3 * n + k][...] = total.astype(BF16)

    out_spec = pl.BlockSpec((S, tr, tc), lambda i, j, c_ref: (0, i, j))
    grid_spec = pltpu.PrefetchScalarGridSpec(
        num_scalar_prefetch=1, grid=(R // tr, C // tc),
        in_specs=[pl.BlockSpec((S, 1, tr, tc), lambda i, j, c_ref: (0, c_ref[0], i, j))] * n + [out_spec] * n,
        out_specs=(out_spec,) * (2 * n))
    outs = pl.pallas_call(
        body, out_shape=(jax.ShapeDtypeStruct((S, R, C), F32),) * n + (jax.ShapeDtypeStruct((S, R, C), BF16),) * n,
        grid_spec=grid_spec, name=f"presum_{tag}", compiler_params=_cp("parallel", "parallel"),
    )(c_idx, *mines, *sibs)
    return list(outs[:n]), list(outs[n:])


def _assemble_with_sibling(parts, axes):
    n = len(parts)

    def body(*refs):
        outs, ssem, rsem = refs[n:2 * n], refs[2 * n], refs[2 * n + 1]
        c = lax.axis_index("c")
        _sibling_handshake()
        cps = [pltpu.make_async_remote_copy(
            src_ref=_half_of(outs[i], axes[i], c), dst_ref=_half_of(outs[i], axes[i], c), send_sem=ssem.at[i],
            recv_sem=rsem.at[i], device_id=_peer(1), device_id_type=MESH) for i in range(n)]
        for cp in cps:
            cp.start()
        for i in range(n):
            pltpu.make_async_remote_copy(
                src_ref=_half_of(outs[i], axes[i], c), dst_ref=_half_of(outs[i], axes[i], 1 - c), send_sem=ssem.at[i],
                recv_sem=rsem.at[i], device_id=_peer(1), device_id_type=MESH).wait_recv()
        for cp in cps:
            cp.wait_send()

    return pl.pallas_call(
        body, out_shape=tuple(jax.ShapeDtypeStruct(p.shape, p.dtype) for p in parts), name="assemble_with_sibling",
        in_specs=[ANY] * n, out_specs=tuple([ANY] * n), input_output_aliases={i: i for i in range(n)},
        scratch_shapes=[pltpu.SemaphoreType.DMA((n,)), pltpu.SemaphoreType.DMA((n,))], compiler_params=SIBLING_ONLY,
    )(*parts)


def _assemble_start(parts, axes):
    n = len(parts)

    def body(*refs):
        ssems, rsems, outs, token = refs[n:2 * n], refs[2 * n:3 * n], refs[3 * n:4 * n], refs[4 * n]
        c = lax.axis_index("c")
        for i in range(n):
            mine = _half_of(outs[i], axes[i], c)
            pltpu.make_async_remote_copy(src_ref=mine, dst_ref=mine, send_sem=ssems[i], recv_sem=rsems[i],
                                         device_id=_peer(1), device_id_type=MESH).start()
        token[...] = jnp.zeros_like(token)

    sem = pltpu.SemaphoreType.DMA(())
    out_shape = ((sem,) * (2 * n) + tuple(pltpu.HBM(p.shape, p.dtype) for p in parts)
                 + (jax.ShapeDtypeStruct((8, 128), F32),))
    outs = pl.pallas_call(
        body, out_shape=out_shape, name="assemble_start",
        in_specs=[HBM_SPEC] * n, out_specs=tuple([SEM_SPEC] * (2 * n) + [HBM_SPEC] * n + [VMEM_SPEC]),
        input_output_aliases={i: 2 * n + i for i in range(n)}, compiler_params=SPLIT_COPY,
    )(*[pltpu.with_memory_space_constraint(p, pltpu.HBM) for p in parts])
    return outs[:n], outs[n:2 * n], list(outs[2 * n:3 * n]), outs[3 * n]


def _assemble_wait(ssems, rsems, parts, axes, after):
    n = len(parts)

    def body(*refs):
        ssem_refs, rsem_refs = refs[n:2 * n], refs[2 * n:3 * n]
        for i in range(n):
            half = _half_of(refs[i], axes[i], 0)
            moved = pltpu.make_async_remote_copy(
                src_ref=half, dst_ref=half, send_sem=ssem_refs[i], recv_sem=rsem_refs[i],
                device_id=_me(), device_id_type=MESH)
            moved.wait_send()
            moved.wait_recv()

    outs = pl.pallas_call(
        body, out_shape=tuple(pltpu.HBM(p.shape, p.dtype) for p in parts), name="assemble_wait",
        in_specs=[HBM_SPEC] * n + [SEM_SPEC] * (2 * n) + [ANY], out_specs=tuple([HBM_SPEC] * n),
        input_output_aliases={i: i for i in range(n)}, compiler_params=SPLIT_COPY,
    )(*parts, *ssems, *rsems, after)
    return list(outs)


def _rope_lane_frequencies():
    inv = np.float32(ROPE_THETA) ** (-(np.arange(0, 2 * ROT_HALF, 2, dtype=np.float32)) / np.float32(2 * ROT_HALF))
    lane = np.arange(128) % HEAD_DIM
    return jnp.asarray(np.where(lane < 2 * ROT_HALF, inv[lane % ROT_HALF], 0.0).astype(np.float32)[None, :])


def _rope_tables(pos, freq):
    ang = pos.astype(F32) * freq
    c, s = jnp.cos(ang), jnp.sin(ang)
    m = lax.broadcasted_iota(jnp.int32, ang.shape, 1) & (HEAD_DIM - 1)
    return (jnp.where(m < 2 * ROT_HALF, c, 1.0), jnp.where(m < ROT_HALF, -s, 0.0),
            jnp.where((m >= ROT_HALF) & (m < 2 * ROT_HALF), s, 0.0))


def _columns(t):
    return [t[:, i:i + 128] for i in range(0, t.shape[-1], 128)]


def _rope(t, c, sa, sb):
    return jnp.concatenate(
        [x * c + pltpu.roll(x, 128 - ROT_HALF, 1) * sa + pltpu.roll(x, ROT_HALF, 1) * sb for x in _columns(t)], axis=1)


def _unrope(d, c, sa, sb):
    return jnp.concatenate(
        [x * c + pltpu.roll(x * sa, ROT_HALF, 1) + pltpu.roll(x * sb, 128 - ROT_HALF, 1) for x in _columns(d)], axis=1)


def _prenorm(x, mod_row, norm_g, pos_col):
    T = x.shape[0]
    tm = min(T, 512)

    def body(x_ref, mod_ref, g_ref, pos_ref, f_ref, h_ref, ht_ref, c_ref, sa_ref, sb_ref):
        xf = x_ref[...]
        shift, scale = mod_ref[:, 0:D_MODEL], mod_ref[:, D_MODEL:2 * D_MODEL]
        h = (xf * _rms(xf)) * g_ref[...] * (1.0 + scale) + shift
        h_ref[...] = h.astype(BF16)
        ht_ref[...] = h.T.astype(BF16)
        c_ref[...], sa_ref[...], sb_ref[...] = _rope_tables(pos_ref[...], f_ref[...])

    tab = jax.ShapeDtypeStruct((T, 128), F32)
    tok = lambda w: pl.BlockSpec((tm, w), lambda i: (i, 0))
    row = lambda w: pl.BlockSpec((1, w), lambda i: (0, 0))
    outs = pl.pallas_call(
        body, out_shape=(jax.ShapeDtypeStruct((T, D_MODEL), BF16), jax.ShapeDtypeStruct((D_MODEL, T), BF16), tab, tab, tab),
        grid=(T // tm,), name="prenorm",
        in_specs=[tok(D_MODEL), row(ADA_W), row(D_MODEL), tok(1), row(128)],
        out_specs=(tok(D_MODEL), pl.BlockSpec((D_MODEL, tm), lambda i: (0, i)), tok(128), tok(128), tok(128)),
        compiler_params=_cp("parallel"),
    )(x, *_in_hbm(mod_row, norm_g), pos_col, _rope_lane_frequencies())
    return outs[0], outs[1], tuple(outs[2:])


def _in_projection(h, w_in, chips, into, tag):
    T = h.shape[0]
    tm, tn = min(T, 512), SHARD_IN
    k = chips.shape[0]

    def body(chip_ref, h_ref, w_ref, *rest):
        rest[-1][...] = _dot(h_ref[...], w_ref[...])

    w_spec = pl.BlockSpec((D_MODEL, tn), lambda s, i, c: (0, c[s]), **({"pipeline_mode": pl.Buffered(1)} if k == 1 else {}))
    in_specs = [pl.BlockSpec((tm, D_MODEL), lambda s, i, c: (i, 0)), w_spec]
    args = [chips, h, w_in]
    aliases = {}
    if into is not None:
        in_specs.append(ANY)
        args.append(into)
        aliases = {3: 0}
    grid_spec = pltpu.PrefetchScalarGridSpec(num_scalar_prefetch=1, grid=(k, T // tm), in_specs=in_specs,
                                             out_specs=pl.BlockSpec((tm, tn), lambda s, i, c: (i, c[s])))
    return pl.pallas_call(
        body, out_shape=jax.ShapeDtypeStruct((T, IN_W), F32), grid_spec=grid_spec, name=f"in_projection_{tag}",
        input_output_aliases=aliases, compiler_params=_cp("parallel", "parallel"),
    )(*args)


def _attn_mask(n):
    qi = lax.broadcasted_iota(jnp.int32, (GROUP * BLOCK, BLOCK), 0) & (BLOCK - 1)
    j = lax.broadcasted_iota(jnp.int32, (GROUP * BLOCK, BLOCK), 1)
    own = j <= qi
    return own, jnp.logical_not(own) & (n == 0)


def _fold(x, own):
    return jnp.where(own, x[:, BLOCK:2 * BLOCK], x[:, 0:BLOCK])


def _unfold(xf, own):
    zero = jnp.zeros_like(xf)
    return jnp.concatenate([jnp.where(own, zero, xf), jnp.where(own, xf, zero)], axis=1)


ROW_GROUP_HEAD = (0, 2, 1, 3)


def _sink_col(sink_ref, kh):
    rowg = lax.broadcasted_iota(jnp.int32, (GROUP * BLOCK, 1), 0) // BLOCK
    col = jnp.full((GROUP * BLOCK, 1), sink_ref[0, GROUP * kh + ROW_GROUP_HEAD[0]], F32)
    for g in range(1, GROUP):
        col = jnp.where(rowg == g, sink_ref[0, GROUP * kh + ROW_GROUP_HEAD[g]], col)
    return col


def _low_lanes(shape):
    return lax.broadcasted_iota(jnp.int32, shape, 1) < HEAD_DIM


def _kv_pair_operand(prev, cur, kh):
    c = 128 * (kh // 2)
    col = jnp.concatenate([prev[:, c:c + 128], cur[:, c:c + 128]], axis=0).astype(F32)
    if kh % 2 == 0:
        lo = jnp.where(_low_lanes(col.shape), col, 0.0)
        hi = pltpu.roll(lo, HEAD_DIM, 1)
    else:
        hi = jnp.where(_low_lanes(col.shape), 0.0, col)
        lo = pltpu.roll(hi, HEAD_DIM, 1)
    return jnp.concatenate([lo, hi], axis=0).astype(BF16)


def _pair_rows(x, kh):
    c = 2 * 128 * kh
    return jnp.concatenate([x[:, c:c + 128], x[:, c + 128:c + 256]], axis=0)


def _restack(big):
    return jnp.concatenate([big[:, 0:2 * BLOCK], big[:, 2 * BLOCK:4 * BLOCK]], axis=0)


def _unrestack(stacked):
    return jnp.concatenate([stacked[0:2 * BLOCK], stacked[2 * BLOCK:4 * BLOCK]], axis=1)


def _fold_pair(x2, kh):
    low = _low_lanes((2 * BLOCK, 128))
    mixed = jnp.where(low, x2[0:2 * BLOCK], x2[2 * BLOCK:4 * BLOCK])
    total = mixed + pltpu.roll(mixed, HEAD_DIM, 1)
    return jnp.where(low, total, 0.0) if kh % 2 == 0 else jnp.where(low, 0.0, total)


def _attn_scores(qr, k2, kh):
    q2 = _pair_rows(qr, kh).astype(BF16)
    return q2, _restack(_dot_nt(q2, k2))


def _attn_softmax(s, sink_col, mask):
    own, no_key = mask
    s = jnp.where(no_key, -1e30, _fold(s, own))
    m = jnp.maximum(jnp.max(s, axis=-1, keepdims=True), sink_col)
    p = jnp.exp(s - m)
    p_sink = jnp.exp(sink_col - m)
    denom = jnp.sum(p, axis=-1, keepdims=True) + p_sink
    return p / denom, p_sink / denom


def _attn_forward(proj, tabs, sinks):
    T = proj.shape[0]
    nb = T // BLOCK

    def body(q_ref, kvc_ref, kvp_ref, g0_ref, g1_ref, cc, sac, sbc, cp_, sap, sbp, sink_ref, y_ref, qrb_ref, krb_ref, p_ref):
        n = pl.program_id(0)
        tc = tcur = (cc[...], sac[...], sbc[...])
        tprev = (cp_[...], sap[...], sbp[...])
        qr = _rope(q_ref[...], *tc) * ATTN_SCALE
        kr_cur = _rope(kvc_ref[:, 0:KV_W], *tcur)
        kr_prev = _rope(kvp_ref[:, 0:KV_W], *tprev)
        qrb_ref[...] = qr.astype(BF16)
        krb_ref[...] = kr_cur.astype(BF16)
        v_cur, v_prev = kvc_ref[:, KV_W:2 * KV_W], kvp_ref[:, KV_W:2 * KV_W]
        mask = _attn_mask(n)
        outs = []
        k2s = [_kv_pair_operand(kr_prev, kr_cur, kh) for kh in range(N_KV)]
        v2s = [_kv_pair_operand(v_prev, v_cur, kh) for kh in range(N_KV)]
        scores = [_attn_scores(qr, k2s[kh], kh) for kh in range(N_KV)]
        p_parts = []
        for kh in range(N_KV):
            pn, _ = _attn_softmax(scores[kh][1], _sink_col(sink_ref, kh), mask)
            p_parts += [pn[g * BLOCK:(g + 1) * BLOCK] for g in range(GROUP)]
            o_big = _dot(_unrestack(_unfold(pn.astype(BF16), mask[0])), v2s[kh])
            outs += [o_big[0:BLOCK], o_big[BLOCK:2 * BLOCK]]
        p_ref[...] = jnp.concatenate(p_parts, axis=1)
        o = jnp.concatenate(outs, axis=1)
        g = jnp.concatenate([g0_ref[...], g1_ref[...]], axis=1)
        y_ref[...] = (o * (g * _sigmoid(g))).astype(BF16)

    def blk(w, cb):
        return pl.BlockSpec((BLOCK, w), lambda n, cb=cb: (n, cb))

    prev = lambda w, cb: pl.BlockSpec((BLOCK, w), lambda n, cb=cb: (jnp.maximum(n - 1, 0), cb))
    return pl.pallas_call(
        body, grid=(nb,), name="attn_forward",
        out_shape=(jax.ShapeDtypeStruct((T, D_MODEL), BF16), jax.ShapeDtypeStruct((T, D_MODEL), BF16),
                   jax.ShapeDtypeStruct((T, KV_W), BF16), jax.ShapeDtypeStruct((T, N_HEADS * BLOCK), F32)),
        in_specs=[blk(D_MODEL, 0), blk(CB, CB_KV), prev(CB, CB_KV), blk(CB, CB_GA), blk(CB, CB_GA + 1),
                  blk(128, 0), blk(128, 0), blk(128, 0), prev(128, 0), prev(128, 0), prev(128, 0),
                  pl.BlockSpec(memory_space=pltpu.SMEM)],
        out_specs=(blk(D_MODEL, 0), blk(D_MODEL, 0), blk(KV_W, 0), blk(N_HEADS * BLOCK, 0)),
        compiler_params=_cp("parallel"),
    )(proj, proj, proj, proj, proj, *tabs, *tabs, sinks)


def _scan_rows8():
    return lax.broadcasted_iota(jnp.int32, (8, D_MODEL), 0)


def _scan_forward(a_ref, b_ref, h_ref, carry, rows):
    row = _scan_rows8()

    def group(i, carry):
        off = pl.multiple_of(i * 8, 8)
        a, b = a_ref[pl.ds(off, 8), :], b_ref[pl.ds(off, 8), :]
        for d in (1, 2, 4):
            ok = row >= d
            b = jnp.where(ok, a * pltpu.roll(b, d, 0) + b, b)
            a = jnp.where(ok, a * pltpu.roll(a, d, 0), a)
        h = a * carry + b
        h_ref[pl.ds(off, 8), :] = h
        return h[7:8, :]

    return lax.fori_loop(0, rows // 8, group, carry)


def _scan_backward(a_ref, g_ref, lam_ref, carry, rows):
    row = _scan_rows8()

    def group(i, carry):
        off = pl.multiple_of((rows // 8 - 1 - i) * 8, 8)
        a, g = a_ref[pl.ds(off, 8), :], g_ref[pl.ds(off, 8), :]
        b = a * g
        for d in (1, 2, 4):
            ok = row < 8 - d
            b = jnp.where(ok, a * pltpu.roll(b, 8 - d, 0) + b, b)
            a = jnp.where(ok, a * pltpu.roll(a, 8 - d, 0), a)
        mu = a * carry + b
        mu_below = jnp.where(row == 7, carry, pltpu.roll(mu, 7, 0))
        lam_ref[pl.ds(off, 8), :] = g + mu_below
        return mu[0:1, :]

    return lax.fori_loop(0, rows // 8, group, carry)


def _conv_taps(xbuf, xr, tail):
    rows = xr.shape[0]
    xbuf[0:8, :] = tail
    xbuf[8:rows + 8, :] = xr
    return [xbuf[pl.ds(8 - (CONV_W - 1 - k), rows), :] for k in range(CONV_W - 1)] + [xr]


def _rnn_gates(xbuf, xr, tail, cw, cb, wa_ref, wx_ref, ba, bx, sp, reset):
    xs = _conv_taps(xbuf, xr, tail)
    xc = xs[0] * cw[0:1, :]
    for k in range(1, CONV_W):
        xc = xc + xs[k] * cw[k:k + 1, :]
    xc = xc + cb
    xcb = xc.astype(BF16)
    za = jnp.concatenate([_dot(xcb[:, RNN_BW * j:RNN_BW * (j + 1)], wa_ref[j]) for j in range(RNN_BLOCKS)], axis=1) + ba
    zx = jnp.concatenate([_dot(xcb[:, RNN_BW * j:RNN_BW * (j + 1)], wx_ref[j]) for j in range(RNN_BLOCKS)], axis=1) + bx
    r, i = _sigmoid(za), _sigmoid(zx)
    neg_log_a = LRU_C * r * sp
    a_raw = jnp.exp(-neg_log_a)
    mult_raw = jnp.sqrt(jnp.tanh(neg_log_a) * (1.0 + a_raw * a_raw))
    a = jnp.where(reset, 0.0, a_raw)
    mult = jnp.where(reset, 1.0, mult_raw)
    return xc, r, i, a, mult


def _rnn_forward(proj, pos_col, conv_w, conv_b, rwa, rwx, ba, bx, lam):
    T = proj.shape[0]
    tr = min(T, 256)

    def body(x0, x1, g0, g1, pos_ref, cw_ref, cb_ref, wa_ref, wx_ref, ba_ref, bx_ref, lam_ref,
             y_ref, h_ref, xc_ref, r_ref, i_ref, a_ref, mult_ref, xbuf, bbuf, tail, carry):
        t = pl.program_id(0)

        @pl.when(t == 0)
        def _():
            tail[...] = jnp.zeros_like(tail)
            carry[...] = jnp.zeros_like(carry)

        xr = jnp.concatenate([x0[...], x1[...]], axis=1)
        sp = _softplus(-lam_ref[...])
        reset = pos_ref[...] == 0
        xc, r, i, a, mult = _rnn_gates(
            xbuf, xr, tail[...], cw_ref[...], cb_ref[...], wa_ref, wx_ref, ba_ref[...], bx_ref[...], sp, reset)
        xc_ref[...] = xc
        r_ref[...] = r
        i_ref[...] = i
        a_ref[...] = a
        mult_ref[...] = mult
        bbuf[...] = mult * (i * xc)
        last = _scan_forward(a_ref, bbuf, h_ref, carry[0:1, :], tr)
        carry[...] = jnp.broadcast_to(last, carry.shape)
        tail[...] = xr[tr - 8:tr, :]
        g = jnp.concatenate([g0[...], g1[...]], axis=1)
        y_ref[...] = (h_ref[...] * (g * _sigmoid(g))).astype(BF16)

    blk = lambda cb: pl.BlockSpec((tr, CB), lambda t, cb=cb: (t, cb))
    row = lambda w: pl.BlockSpec((1, w), lambda t: (0, 0))
    full3 = pl.BlockSpec((RNN_BLOCKS, RNN_BW, RNN_BW), lambda t: (0, 0, 0))
    tok = pl.BlockSpec((tr, D_MODEL), lambda t: (t, 0))
    act = jax.ShapeDtypeStruct((T, D_MODEL), F32)
    return pl.pallas_call(
        body, out_shape=(jax.ShapeDtypeStruct((T, D_MODEL), BF16),) + (act,) * 6,
        grid=(T // tr,), name="rnn_forward",
        in_specs=[blk(CB_XR), blk(CB_XR + 1), blk(CB_GR), blk(CB_GR + 1), pl.BlockSpec((tr, 1), lambda t: (t, 0)),
                  pl.BlockSpec((CONV_W, D_MODEL), lambda t: (0, 0)), row(D_MODEL), full3, full3,
                  row(D_MODEL), row(D_MODEL), row(D_MODEL)],
        out_specs=(tok,) * 7,
        scratch_shapes=[pltpu.VMEM((tr + 8, D_MODEL), F32), pltpu.VMEM((tr, D_MODEL), F32),
                        pltpu.VMEM((8, D_MODEL), F32), pltpu.VMEM((8, D_MODEL), F32)],
        compiler_params=_cp("arbitrary"),
    )(proj, proj, proj, proj, pos_col, *_in_hbm(conv_w, conv_b, rwa, rwx, ba, bx, lam))


def _merge_and_head(x, target, y_attn, y_rnn, proj, wap, wrp, wo, mod_row, final_g):
    T = x.shape[0]
    tm = min(T, 256)

    def body(x_ref, t_ref, ya_ref, yr_ref, ma0, ma1, mr0, mr1, wap_ref, wrp_ref, wo_ref, mod_ref, fg_ref,
             dx2_ref, mg_ref, do_ref, dpa_ref, dpr_ref, dya_ref, dyr_ref, dc_ref, dfg_ref, dgate_ref, loss_ref):
        i = pl.program_id(0)
        gate = mod_ref[:, 2 * D_MODEL:3 * D_MODEL]
        fg = fg_ref[...]
        pa, pr = _dot(ya_ref[...], wap_ref[...]), _dot(yr_ref[...], wrp_ref[...])
        sa = _sigmoid(jnp.concatenate([ma0[...], ma1[...]], axis=1))
        sr = _sigmoid(jnp.concatenate([mr0[...], mr1[...]], axis=1))
        mb = (sa * pa + sr * pr).astype(BF16)
        o = _dot(mb, wo_ref[...])
        x2 = x_ref[...] + gate * o
        r2 = _rms(x2)
        xn2 = x2 * r2
        err = xn2 * fg - t_ref[...]
        loss_t = 0.5 * jnp.sum(jnp.sum(err * err, axis=-1, keepdims=True) * (1.0 / D_MODEL), axis=0, keepdims=True)
        dy = err * (1.0 / D_MODEL)
        dfg_t = jnp.sum(dy * xn2, axis=0, keepdims=True)
        dxn = dy * fg
        dx2 = r2 * (dxn - xn2 * jnp.mean(dxn * xn2, axis=-1, keepdims=True))
        dgate_t = jnp.sum(dx2 * o, axis=0, keepdims=True)
        dob = (dx2 * gate).astype(BF16)
        dmerged = _dot_nt(dob, wo_ref[...])
        dpa, dpr = (dmerged * sa).astype(BF16), (dmerged * sr).astype(BF16)
        dya, dyr = _dot_nt(dpa, wap_ref[...]), _dot_nt(dpr, wrp_ref[...])
        dx2_ref[...] = dx2
        mg_ref[...] = mb
        do_ref[...] = dob
        dpa_ref[...] = dpa
        dpr_ref[...] = dpr
        dya_ref[...] = dya
        dyr_ref[...] = dyr
        dc_ref[:, 0:D_MODEL] = (dmerged * pa * sa * (1.0 - sa)).astype(BF16)
        dc_ref[:, D_MODEL:2 * D_MODEL] = (dmerged * pr * sr * (1.0 - sr)).astype(BF16)

        @pl.when(i == 0)
        def _():
            dfg_ref[...] = jnp.zeros_like(dfg_ref)
            dgate_ref[...] = jnp.zeros_like(dgate_ref)
            loss_ref[...] = jnp.zeros_like(loss_ref)

        dfg_ref[...] += dfg_t
        dgate_ref[...] += dgate_t
        loss_ref[...] += jnp.broadcast_to(loss_t, loss_ref.shape)

    tok = lambda w: pl.BlockSpec((tm, w), lambda i: (i, 0))
    blk = lambda cb: pl.BlockSpec((tm, CB), lambda i, cb=cb: (i, cb))
    wfull = pl.BlockSpec((D_MODEL, D_MODEL), lambda i: (0, 0), pipeline_mode=pl.Buffered(1))
    row = lambda w: pl.BlockSpec((1, w), lambda i: (0, 0))
    out_shape = (
        jax.ShapeDtypeStruct((T, D_MODEL), F32), jax.ShapeDtypeStruct((T, D_MODEL), BF16),
        jax.ShapeDtypeStruct((T, D_MODEL), BF16), jax.ShapeDtypeStruct((T, D_MODEL), BF16),
        jax.ShapeDtypeStruct((T, D_MODEL), BF16), jax.ShapeDtypeStruct((T, D_MODEL), F32),
        jax.ShapeDtypeStruct((T, D_MODEL), F32), jax.ShapeDtypeStruct((T, 2 * D_MODEL), BF16),
        jax.ShapeDtypeStruct((1, D_MODEL), F32), jax.ShapeDtypeStruct((1, D_MODEL), F32),
        jax.ShapeDtypeStruct((1, 128), F32),
    )
    return pl.pallas_call(
        body, out_shape=out_shape, grid=(T // tm,), name="merge_and_head",
        in_specs=[tok(D_MODEL), tok(D_MODEL), tok(D_MODEL), tok(D_MODEL), blk(CB_MA), blk(CB_MA + 1), blk(CB_MR),
                  blk(CB_MR + 1), wfull, wfull, wfull, row(ADA_W), row(D_MODEL)],
        out_specs=(tok(D_MODEL),) * 7 + (tok(2 * D_MODEL), row(D_MODEL), row(D_MODEL), row(128)),
        compiler_params=_cp("arbitrary"),
    )(x, target, y_attn, y_rnn, proj, proj, proj, proj, wap, wrp, wo, *_in_hbm(mod_row, final_g))


def _attn_backward(proj, qr_b, kr_b, p_all, d_y, tabs, after):
    T = proj.shape[0]
    nb = T // BLOCK

    def body(qrb_ref, krc_ref, krp_ref, vc_ref, vp_ref, g0_ref, g1_ref, dy_ref, p_ref, cc, sac, sbc, cp_, sap, sbp, after_ref,
             dq_ref, dkv_ref, dg_ref, dsink_ref, carry):
        n = pl.program_id(0)

        @pl.when(n == 0)
        def _():
            carry[...] = jnp.zeros_like(carry)
            dsink_ref[...] = jnp.zeros_like(dsink_ref)

        @pl.when(n < nb)
        def _():
            tc = tcur = (cc[...], sac[...], sbc[...])
            tprev = (cp_[...], sap[...], sbp[...])
            qr, kr_cur, kr_prev = qrb_ref[...], krc_ref[...], krp_ref[...]
            v_cur, v_prev = vc_ref[...], vp_ref[...]
            g = jnp.concatenate([g0_ref[...], g1_ref[...]], axis=1)
            sg = _sigmoid(g)
            dy = dy_ref[...]
            d_o = dy * (g * sg)
            mask = _attn_mask(n)
            lane = lax.broadcasted_iota(jnp.int32, (1, 128), 1)
            rowg = lax.broadcasted_iota(jnp.int32, (GROUP * BLOCK, 1), 0) // BLOCK
            o_parts, dq_parts = [], []
            dk_cols, dv_cols = [None, None], [None, None]
            dsink = jnp.zeros((1, 128), F32)
            heads = range(N_KV)
            k2s = [_kv_pair_operand(kr_prev, kr_cur, kh) for kh in heads]
            v2s = [_kv_pair_operand(v_prev, v_cur, kh) for kh in heads]
            q2s = [_pair_rows(qr, kh).astype(BF16) for kh in heads]
            do2s = [_pair_rows(d_o, kh).astype(BF16) for kh in heads]
            dpns = [_fold(_restack(_dot_nt(do2s[kh], v2s[kh])), mask[0]) for kh in heads]
            pns = [jnp.concatenate([p_ref[:, BLOCK * (GROUP * kh + g):BLOCK * (GROUP * kh + g + 1)] for g in range(GROUP)], axis=0)
                   for kh in heads]
            probs = [(pn, 1.0 - jnp.sum(pn, axis=-1, keepdims=True)) for pn in pns]
            p_bigs = [_unrestack(_unfold(probs[kh][0].astype(BF16), mask[0])) for kh in heads]
            o_bigs = [_dot(p_bigs[kh], v2s[kh]) for kh in heads]
            dv2s = [_dot_tn(p_bigs[kh], do2s[kh]) for kh in heads]
            deltas = [jnp.sum(probs[kh][0] * dpns[kh], axis=-1, keepdims=True) for kh in heads]
            ds_bigs = [_unrestack(_unfold((probs[kh][0] * (dpns[kh] - deltas[kh])).astype(BF16), mask[0])) for kh in heads]
            dq2s = [_dot(ds_bigs[kh], k2s[kh]) for kh in heads]
            dk2s = [_dot_tn(ds_bigs[kh], q2s[kh]) for kh in heads]
            for kh in heads:
                o_parts += [o_bigs[kh][0:BLOCK], o_bigs[kh][BLOCK:2 * BLOCK]]
                dq_parts += [dq2s[kh][0:BLOCK], dq2s[kh][BLOCK:2 * BLOCK]]
                dk_c, dv_c = _fold_pair(dk2s[kh], kh), _fold_pair(dv2s[kh], kh)
                c = kh // 2
                dk_cols[c] = dk_c if dk_cols[c] is None else dk_cols[c] + dk_c
                dv_cols[c] = dv_c if dv_cols[c] is None else dv_cols[c] + dv_c
                ds_rows = probs[kh][1] * deltas[kh]
                for gq in range(GROUP):
                    val = -jnp.sum(jnp.where(rowg == gq, ds_rows, 0.0), axis=0, keepdims=True)
                    dsink = dsink + jnp.where(lane == GROUP * kh + ROW_GROUP_HEAD[gq], val, 0.0)
            o = jnp.concatenate(o_parts, axis=1)
            dg_ref[...] = (dy * o * (sg * (1.0 + g * (1.0 - sg)))).astype(BF16)
            dq_ref[...] = (_unrope(jnp.concatenate(dq_parts, axis=1), *tc) * ATTN_SCALE).astype(BF16)
            dk_all, dv_all = jnp.concatenate(dk_cols, axis=1), jnp.concatenate(dv_cols, axis=1)
            dk_prev = _unrope(dk_all[0:BLOCK], *tprev)
            dk_cur = _unrope(dk_all[BLOCK:2 * BLOCK], *tcur)
            dv_prev, dv_cur = dv_all[0:BLOCK], dv_all[BLOCK:2 * BLOCK]
            dkv_ref[...] = (carry[...] + jnp.concatenate([dk_prev, dv_prev], axis=1)).astype(BF16)
            carry[...] = jnp.concatenate([dk_cur, dv_cur], axis=1)
            dsink_ref[...] += dsink

        @pl.when(n == nb)
        def _():
            dkv_ref[...] = carry[...].astype(BF16)

    cur = lambda w, cb: pl.BlockSpec((BLOCK, w), lambda n, cb=cb: (jnp.minimum(n, nb - 1), cb))
    prev = lambda w, cb: pl.BlockSpec((BLOCK, w), lambda n, cb=cb: (jnp.maximum(jnp.minimum(n, nb - 1) - 1, 0), cb))
    out_shape = (jax.ShapeDtypeStruct((T, D_MODEL), BF16), jax.ShapeDtypeStruct((T, 2 * KV_W), BF16),
                 jax.ShapeDtypeStruct((T, D_MODEL), BF16), jax.ShapeDtypeStruct((1, 128), F32))
    return pl.pallas_call(
        body, out_shape=out_shape, grid=(nb + 1,), name="attn_backward",
        in_specs=[cur(D_MODEL, 0), cur(KV_W, 0), prev(KV_W, 0), cur(KV_W, V_COL_BLOCK), prev(KV_W, V_COL_BLOCK),
                  cur(CB, CB_GA), cur(CB, CB_GA + 1), cur(D_MODEL, 0), cur(N_HEADS * BLOCK, 0),
                  cur(128, 0), cur(128, 0), cur(128, 0), prev(128, 0), prev(128, 0), prev(128, 0),
                  pl.BlockSpec(memory_space=pltpu.SMEM)],
        out_specs=(cur(D_MODEL, 0), pl.BlockSpec((BLOCK, 2 * KV_W), lambda n: (jnp.maximum(n - 1, 0), 0)),
                   cur(D_MODEL, 0), pl.BlockSpec((1, 128), lambda n: (0, 0))),
        scratch_shapes=[pltpu.VMEM((BLOCK, 2 * KV_W), F32)],
        compiler_params=_cp("arbitrary"),
    )(qr_b, kr_b, kr_b, proj, proj, proj, proj, d_y, p_all, *tabs, *tabs, after)


def _rnn_backward(proj, pos_col, h_rnn, saved, d_y, conv_w, rwa, rwx, lam):
    T = proj.shape[0]
    tr = min(T, 256)
    nt = T // tr
    hb = tr // 8

    def body(x0, x1, xh0, xh1, g0, g1, pos_ref, h_ref, hh_ref, xc_ref, r_ref, i_ref, a_ref, mult_ref, dy_ref,
             cw_ref, wa_ref, wx_ref, lam_ref, db_ref, dcw_ref, dcb_ref, dwa_ref, dwx_ref, dba_ref, dbx_ref, dlam_ref,
             xbuf, hbuf, dbuf, gbuf, lbuf, mu_carry, dxc_head):
        step = pl.program_id(0)
        first_tile = step == nt - 1

        @pl.when(step == 0)
        def _():
            mu_carry[...] = jnp.zeros_like(mu_carry)
            dxc_head[...] = jnp.zeros_like(dxc_head)
            for ref in (dcw_ref, dcb_ref, dwa_ref, dwx_ref, dba_ref, dbx_ref, dlam_ref):
                ref[...] = jnp.zeros_like(ref)

        xr = jnp.concatenate([x0[...], x1[...]], axis=1)
        tail = jnp.where(first_tile, 0.0, jnp.concatenate([xh0[...], xh1[...]], axis=1))
        lam_v = lam_ref[...]
        sp = _softplus(-lam_v)
        reset = pos_ref[...] == 0
        cw = cw_ref[...]
        xbuf[0:8, :] = tail
        xbuf[8:tr + 8, :] = xr
        g = jnp.concatenate([g0[...], g1[...]], axis=1)
        sg = _sigmoid(g)
        dy = dy_ref[...]
        h = h_ref[...]
        db_ref[:, D_MODEL:2 * D_MODEL] = (dy * h * (sg * (1.0 + g * (1.0 - sg)))).astype(BF16)
        gbuf[...] = dy * (g * sg)
        top = _scan_backward(a_ref, gbuf, lbuf, mu_carry[0:1, :], tr)
        mu_carry[...] = jnp.broadcast_to(top, mu_carry.shape)
        hbuf[0:8, :] = jnp.where(first_tile, 0.0, hh_ref[...])
        hbuf[8:tr + 8, :] = h
        live = jnp.logical_not(reset)
        dbuf[tr:tr + 8, :] = dxc_head[...]
        for j in range(RNN_BLOCKS):
            sl = slice(RNN_BW * j, RNN_BW * (j + 1))
            lam_t, h_prev = lbuf[:, sl], hbuf[pl.ds(7, tr), sl]
            xc, r, i, a, mult = xc_ref[:, sl], r_ref[:, sl], i_ref[:, sl], a_ref[:, sl], mult_ref[:, sl]
            d_a = jnp.where(live, lam_t * h_prev, 0.0)
            d_mult = jnp.where(live, lam_t * (i * xc), 0.0)
            d_ixc = lam_t * mult
            d_i = d_ixc * xc
            d_log_a = d_a * a - d_mult * (a * a / mult)
            d_za = d_log_a * (-LRU_C * sp[:, sl]) * (r * (1.0 - r))
            d_zx = d_i * (i * (1.0 - i))
            dlam_ref[:, sl] += jnp.sum(d_log_a * r, axis=0, keepdims=True) * (LRU_C * _sigmoid(-lam_v[:, sl]))
            dba_ref[:, sl] += jnp.sum(d_za, axis=0, keepdims=True)
            dbx_ref[:, sl] += jnp.sum(d_zx, axis=0, keepdims=True)
            xcb, dzab, dzxb = xc.astype(BF16), d_za.astype(BF16), d_zx.astype(BF16)
            dwa_ref[j] += _dot_tn(xcb, dzab)
            dwx_ref[j] += _dot_tn(xcb, dzxb)
            d_xc = d_ixc * i + (_dot_nt(dzab, wa_ref[j]) + _dot_nt(dzxb, wx_ref[j]))
            dcb_ref[:, sl] += jnp.sum(d_xc, axis=0, keepdims=True)
            for k in range(CONV_W):
                tap = xr[:, sl] if k == CONV_W - 1 else xbuf[pl.ds(8 - (CONV_W - 1 - k), tr), sl]
                dcw_ref[k:k + 1, sl] += jnp.sum(d_xc * tap, axis=0, keepdims=True)
            dbuf[0:tr, sl] = d_xc
            d_xr = d_xc * cw[CONV_W - 1:CONV_W, sl]
            for k in range(CONV_W - 1):
                d_xr = d_xr + dbuf[pl.ds(CONV_W - 1 - k, tr), sl] * cw[k:k + 1, sl]
            dxc_head[:, sl] = d_xc[0:8, :]
            db_ref[:, sl] = d_xr.astype(BF16)

    rev = lambda s: nt - 1 - s
    blk = lambda cb: pl.BlockSpec((tr, CB), lambda s, cb=cb: (rev(s), cb))
    halo = lambda w, cb: pl.BlockSpec((8, w), lambda s, cb=cb: (jnp.maximum(rev(s) * hb - 1, 0), cb))
    tok = lambda w: pl.BlockSpec((tr, w), lambda s: (rev(s), 0))
    row = lambda w: pl.BlockSpec((1, w), lambda s: (0, 0))
    full3 = pl.BlockSpec((RNN_BLOCKS, RNN_BW, RNN_BW), lambda s: (0, 0, 0))
    cwspec = pl.BlockSpec((CONV_W, D_MODEL), lambda s: (0, 0))
    vec = jax.ShapeDtypeStruct((1, D_MODEL), F32)
    gate_w = jax.ShapeDtypeStruct((RNN_BLOCKS, RNN_BW, RNN_BW), F32)
    out_shape = (jax.ShapeDtypeStruct((T, 2 * D_MODEL), BF16), jax.ShapeDtypeStruct((CONV_W, D_MODEL), F32), vec,
                 gate_w, gate_w, vec, vec, vec)
    big = lambda: pltpu.VMEM((tr, D_MODEL), F32)
    ext = lambda: pltpu.VMEM((tr + 8, D_MODEL), F32)
    return pl.pallas_call(
        body, out_shape=out_shape, grid=(nt,), name="rnn_backward",
        in_specs=[blk(CB_XR), blk(CB_XR + 1), halo(CB, CB_XR), halo(CB, CB_XR + 1), blk(CB_GR), blk(CB_GR + 1),
                  pl.BlockSpec((tr, 1), lambda s: (rev(s), 0)), tok(D_MODEL), halo(D_MODEL, 0)] + [tok(D_MODEL)] * 6
        + [cwspec, full3, full3, row(D_MODEL)],
        out_specs=(tok(2 * D_MODEL), cwspec, row(D_MODEL), full3, full3, row(D_MODEL), row(D_MODEL), row(D_MODEL)),
        scratch_shapes=[ext(), ext(), ext(), big(), big(), pltpu.VMEM((8, D_MODEL), F32), pltpu.VMEM((8, D_MODEL), F32)],
        compiler_params=_cp("arbitrary"),
    )(proj, proj, proj, proj, proj, proj, pos_col, h_rnn, h_rnn, *saved, d_y, *_in_hbm(conv_w, rwa, rwx, lam))


def _input_backward(pieces, w_in, x, dx2, mod_row, norm_g):
    T = x.shape[0]
    tm = min(T, 512)
    n = len(pieces)

    def body(*refs):
        d_refs = refs[:n]
        w_ref, x_ref, dx2_ref, mod_ref, g_ref, gx_ref, dshift_ref, dscale_ref, dg_ref = refs[n:]
        i = pl.program_id(0)
        dh = None
        for d_ref, (_, start, count) in zip(d_refs, pieces):
            part = _dot_nt(d_ref[...], w_ref[:, start * CB:(start + count) * CB])
            dh = part if dh is None else dh + part

        @pl.when(i == 0)
        def _():
            dshift_ref[...] = jnp.zeros_like(dshift_ref)
            dscale_ref[...] = jnp.zeros_like(dscale_ref)
            dg_ref[...] = jnp.zeros_like(dg_ref)

        xf = x_ref[...]
        r1 = _rms(xf)
        xn = xf * r1
        gn = g_ref[...]
        s1 = 1.0 + mod_ref[:, D_MODEL:2 * D_MODEL]
        dshift_ref[...] += jnp.sum(dh, axis=0, keepdims=True)
        dscale_ref[...] += jnp.sum(dh * (xn * gn), axis=0, keepdims=True)
        dg_ref[...] += jnp.sum(dh * s1 * xn, axis=0, keepdims=True)
        dxn = dh * s1 * gn
        gx_ref[...] = dx2_ref[...] + r1 * (dxn - xn * jnp.mean(dxn * xn, axis=-1, keepdims=True))

    tok = lambda w: pl.BlockSpec((tm, w), lambda i: (i, 0))
    row = lambda w: pl.BlockSpec((1, w), lambda i: (0, 0))
    vec = jax.ShapeDtypeStruct((1, D_MODEL), F32)
    return pl.pallas_call(
        body, out_shape=(jax.ShapeDtypeStruct((T, D_MODEL), F32), vec, vec, vec), grid=(T // tm,), name="input_backward",
        in_specs=[tok(c * CB) for _, _, c in pieces]
        + [pl.BlockSpec((D_MODEL, IN_W), lambda i: (0, 0), pipeline_mode=pl.Buffered(1)), tok(D_MODEL), tok(D_MODEL),
           row(ADA_W), row(D_MODEL)],
        out_specs=(tok(D_MODEL), row(D_MODEL), row(D_MODEL), row(D_MODEL)),
        compiler_params=_cp("arbitrary"),
    )(*[p[0] for p in pieces], w_in, x, dx2, *_in_hbm(mod_row, norm_g))


def _weight_grad(a, pieces, tag, a_is_transposed=False):
    M, T = a.shape if a_is_transposed else a.shape[::-1]
    n_blocks = sum(count for _, _, count in pieces)
    n = len(pieces)
    contract = _dot if a_is_transposed else _dot_tn

    def body(*refs):
        a_ref, b_refs, o_ref = refs[0], refs[1:1 + n], refs[-1]
        j = pl.program_id(0)
        for b_ref, (_, start, count) in zip(b_refs, pieces):
            @pl.when((j >= start) & (j < start + count))
            def _(b_ref=b_ref):
                o_ref[...] = contract(a_ref[...], b_ref[...])

    def piece_spec(start, count):
        return pl.BlockSpec((T, CB), lambda j: (0, jnp.clip(j - start, 0, count - 1)))

    return pl.pallas_call(
        body, out_shape=jax.ShapeDtypeStruct((M, n_blocks * CB), F32), grid=(n_blocks,), name=f"weight_grad_{tag}",
        in_specs=[pl.BlockSpec(a.shape, lambda j: (0, 0), pipeline_mode=pl.Buffered(1))] + [piece_spec(s, c) for _, s, c in pieces],
        out_specs=pl.BlockSpec((M, CB), lambda j: (0, j)), compiler_params=_cp("arbitrary"),
    )(a, *[p[0] for p in pieces])


def _adamw(w, g, m, v):
    m = ADAM_B1 * m + (1.0 - ADAM_B1) * g
    v = ADAM_B2 * v + (1.0 - ADAM_B2) * (g * g)
    m_hat = m / (1.0 - ADAM_B1 ** ADAM_STEP)
    v_hat = v / (1.0 - ADAM_B2 ** ADAM_STEP)
    delta = -ADAM_LR * (m_hat / (jnp.sqrt(v_hat) + ADAM_EPS) + ADAM_WD * w)
    return delta, m, v


def _sum_landed(kind, owns, lands, where, tag):
    n = len(owns)
    land = lands[0]
    if kind == "in":
        R, C = land.shape[1:]
        tr = 256
        grid = (R // tr,)
        own_spec = pl.BlockSpec((tr, C), lambda i, w: (i, w[0]))
        land_spec = pl.BlockSpec((3, tr, C), lambda i, w: (0, i, 0))
        out_spec = pl.BlockSpec((1, tr, C), lambda i, w: (w[1], i, 0))
        out_shape = (2, R, C)
        pick = lambda ref: ref[...]
    elif kind == "sq":
        R, C = land.shape[1:]
        grid = (1,)
        own_spec = pl.BlockSpec((1, R, C), lambda i, w: (w[0], 0, 0))
        land_spec = pl.BlockSpec((3, R, C), lambda i, w: (0, 0, 0))
        out_spec = pl.BlockSpec((1, R, C), lambda i, w: (w[1], 0, 0))
        out_shape = (2, R, C)
        pick = lambda ref: ref[0]
    else:
        B, R, C = land.shape[1:]
        grid = (1,)
        own_spec = pl.BlockSpec((B, 1, R, C), lambda i, w: (0, w[0], 0, 0))
        land_spec = pl.BlockSpec((3, B, R, C), lambda i, w: (0, 0, 0, 0))
        out_spec = pl.BlockSpec((B, 1, R, C), lambda i, w: (0, w[1], 0, 0))
        out_shape = (B, 2, R, C)
        pick = lambda ref: ref[:, 0]

    def body(w_ref, *refs):
        for k in range(n):
            own_ref, l_ref, o_ref = refs[k], refs[n + k], refs[2 * n + k]
            total = ((pick(own_ref) + l_ref[0].astype(F32)) + l_ref[1].astype(F32)) + l_ref[2].astype(F32)
            if kind == "rg":
                o_ref[:, 0] = total
            else:
                o_ref[0] = total

    grid_spec = pltpu.PrefetchScalarGridSpec(num_scalar_prefetch=1, grid=grid, in_specs=[own_spec] * n + [land_spec] * n,
                                             out_specs=(out_spec,) * n)
    return list(pl.pallas_call(
        body, out_shape=(jax.ShapeDtypeStruct(out_shape, F32),) * n, grid_spec=grid_spec, name=f"sum_landed_{tag}",
        compiler_params=_cp("parallel"),
    )(where, *owns, *lands))


def _adamw_shard(gs, ws, ms, vs, tag):
    n = len(ws)
    R, C = ws[0].shape
    tr = min(R, 256 if n == 1 else 64)

    def body(*refs):
        for k in range(n):
            g = refs[k][...]
            d, nm, nv = _adamw(refs[n + k][...], g, refs[2 * n + k][...], refs[3 * n + k][...])
            out = refs[4 * n + 4 * k:4 * n + 4 * k + 4]
            out[0][...] = g
            out[1][...] = d
            out[2][...] = nm
            out[3][...] = nv

    spec = pl.BlockSpec((tr, C), lambda i: (i, 0))
    sds = jax.ShapeDtypeStruct((R, C), F32)
    outs = pl.pallas_call(
        body, out_shape=(sds,) * (4 * n), grid=(R // tr,), name=f"adamw_{tag}",
        in_specs=[spec] * (4 * n), out_specs=(spec,) * (4 * n), compiler_params=_cp("parallel"),
    )(*gs, *_in_hbm(*ws, *ms, *vs))
    return [outs[4 * k:4 * k + 4] for k in range(n)]


def _adamw_w_ada(c_t, dmod_cols, w, m, v):
    R, C = w.shape

    def body(ct_ref, dm_ref, w_ref, m_ref, v_ref, g_ref, d_ref, nm_ref, nv_ref):
        g = _dot(ct_ref[...].astype(BF16), dm_ref[...].astype(BF16))
        d, nm, nv = _adamw(w_ref[...], g, m_ref[...], v_ref[...])
        g_ref[...] = g
        d_ref[...] = d
        nm_ref[...] = nm
        nv_ref[...] = nv

    tr = 256
    spec = pl.BlockSpec((tr, C), lambda i: (i, 0))
    sds = jax.ShapeDtypeStruct((R, C), F32)
    return pl.pallas_call(
        body, out_shape=(sds,) * 4, grid=(R // tr,), name="adamw_w_ada",
        in_specs=[pl.BlockSpec((tr, 128), lambda i: (i, 0)), pl.BlockSpec((128, C), lambda i: (0, 0))] + [spec] * 3,
        out_specs=(spec,) * 4, compiler_params=_cp("parallel"),
    )(c_t, dmod_cols, w, m, v)


def _adamw_small(small_all, ws, ms, vs):
    def body(s_ref, w_ref, m_ref, v_ref, g_ref, d_ref, nm_ref, nv_ref):
        g = s_ref[0]
        for b in range(1, N_DEV):
            g = g + s_ref[b]
        d, nm, nv = _adamw(w_ref[...], g, m_ref[...], v_ref[...])
        g_ref[...] = g
        d_ref[...] = d
        nm_ref[...] = nm
        nv_ref[...] = nv

    sds = jax.ShapeDtypeStruct((SMALL_ROWS, D_MODEL), F32)
    return pl.pallas_call(
        body, out_shape=(sds,) * 4, name="adamw_small", in_specs=[VMEM_SPEC] * 4, out_specs=(VMEM_SPEC,) * 4,
        compiler_params=pltpu.CompilerParams(vmem_limit_bytes=VMEM_LIMIT_V7X),
    )(small_all, ws, ms, vs)


ROW_MOD, ROW_NORM_G, ROW_CONV_B, ROW_BA, ROW_BX, ROW_LAM, ROW_FINAL_G, ROW_SINKS, ROW_CONV_W, ROW_LOSS = 0, 3, 4, 5, 6, 7, 8, 9, 10, 14


def _pack_small(b_ada, norm_g, conv_b, ba, bx, lam, final_g, sinks, conv_w_full, loss_row=None):
    lane_pad = lambda a: jnp.pad(a.reshape(1, -1), ((0, 0), (0, D_MODEL - a.size)))
    rows = [b_ada.reshape(3, D_MODEL), norm_g, conv_b, ba, bx, lam, final_g.reshape(1, D_MODEL), lane_pad(sinks), conv_w_full,
            jnp.zeros((1, D_MODEL), F32) if loss_row is None else lane_pad(loss_row),
            jnp.zeros((SMALL_ROWS - ROW_LOSS - 1, D_MODEL), F32)]
    return jnp.concatenate([r.astype(F32) for r in rows], axis=0)


def kernel(x, c, positions, w_ada, b_ada, norm_g, w_in, attn_sinks, conv_w, conv_b, rg_wa, rg_ba, rg_wx, rg_bx, rg_lambda, w_attn_proj, w_rnn_proj, w_out, final_g, loss_target, m_w_ada, m_b_ada, m_norm_g, m_w_in, m_attn_sinks, m_conv_w, m_conv_b, m_rg_wa, m_rg_ba, m_rg_wx, m_rg_bx, m_rg_lambda, m_w_attn_proj, m_w_rnn_proj, m_w_out, m_final_g, v_w_ada, v_b_ada, v_norm_g, v_w_in, v_attn_sinks, v_conv_w, v_conv_b, v_rg_wa, v_rg_ba, v_rg_wx, v_rg_bx, v_rg_lambda, v_w_attn_proj, v_w_rnn_proj, v_w_out, v_final_g):
    T = x.shape[1]
    my_chip = lax.axis_index("x") * 2 + lax.axis_index("y")
    my_dev = my_chip * 2 + lax.axis_index("c")
    x2d, tgt = x[0], loss_target[0]
    pos_col = positions.reshape(T, 1)

    chip_idx = my_chip.reshape(1).astype(jnp.int32)
    c_idx = lax.axis_index("c").reshape(1).astype(jnp.int32)
    sq_place = ((D_MODEL, D_MODEL), (SHARD_ROWS, D_MODEL), lambda chip: (chip, 0))
    rg_place = ((RNN_BLOCKS, RNN_BW, RNN_BW), (RNN_BLOCKS, SHARD_RG, RNN_BW), lambda chip: (0, chip, 0))
    in_place = ((D_MODEL, IN_W), (D_MODEL, SHARD_IN), lambda chip: (0, chip))
    placed = _cast_place([w_in[0], w_attn_proj[0], w_rnn_proj[0], w_out[0], rg_wa[0], rg_wx[0]], chip_idx,
                         [in_place, sq_place, sq_place, sq_place, rg_place, rg_place])
    cw_chips, c_all, mod_chips = _gather_mod(c.reshape(1, 1, D_MODEL), w_ada[0], conv_w[0])
    g_ssems, g_rsems, fulls, g_token = _gather_start([p.reshape(s) for p, s in zip(placed, FULL_SHAPES)], mod_chips)
    conv_w_f = jnp.transpose(cw_chips, (1, 0, 2)).reshape(CONV_W, D_MODEL)
    mod_all = jnp.transpose(mod_chips, (1, 0, 2)).reshape(N_DEV, ADA_W) + b_ada
    mod_row = lax.dynamic_slice_in_dim(mod_all, my_dev, 1, axis=0) + g_token[0:1, 0:1]

    h, h_t, tabs = _prenorm(x2d, mod_row, norm_g, pos_col)
    w_in_v = fulls[0]
    proj = _in_projection(h, w_in_v.reshape(D_MODEL, IN_W), chip_idx, None, "own")
    for k, mask in enumerate(CHIP_MASKS):
        w_in_v = _gather_wait(g_ssems[k], g_rsems[k], [w_in_v], [0], proj, f"w_in_{k}")[0]
        w_in_v = _forward_halves([w_in_v], [(0, 0, k)], f"w_in_{k}")[0]
        from_chip = (chip_idx ^ (mask >> 1)).astype(jnp.int32)
        proj = _in_projection(h, w_in_v.reshape(D_MODEL, IN_W), from_chip, proj, f"from_{k}")
    w_in_f = w_in_v.reshape(D_MODEL, IN_W)
    rest = _gather_wait(g_ssems[3], g_rsems[3], list(fulls[1:]), [1, 2, 3, 4, 5], proj, "rest")
    rest = _forward_halves(rest, [(idx - 1, idx, k) for idx in range(1, N_BIG) for k in range(3)], "rest")
    wap_f, wrp_f, wo_f = (g.reshape(D_MODEL, D_MODEL) for g in rest[0:3])
    rwa_f, rwx_f = (g.reshape(RNN_BLOCKS, RNN_BW, RNN_BW) for g in rest[3:5])
    y_attn, qr_b, kr_b, p_all = _attn_forward(proj, tabs, attn_sinks)
    y_rnn, h_rnn, *rnn_saved = _rnn_forward(proj, pos_col, conv_w_f, conv_b, rwa_f, rwx_f, rg_ba, rg_bx, rg_lambda)
    (dx2, merged, d_o, d_pa, d_pr, d_ya, d_yr, d_c, d_final_g, d_gate, loss_vec) = _merge_and_head(
        x2d, tgt, y_attn, y_rnn, proj, wap_f, wrp_f, wo_f, mod_row, final_g.reshape(1, D_MODEL))

    sq = (N_CHIPS, 2, SHARD_ROWS // 2, D_MODEL)
    rg = (RNN_BLOCKS, N_CHIPS, 2, SHARD_RG // 2, RNN_BW)
    rg_flat = (RNN_BLOCKS * N_CHIPS, 2, SHARD_RG // 2, RNN_BW)

    def chip_sum_and_start(views, axes, flat, unflat, tags_, kinds_, group, from_sib=None):
        if from_sib is None:
            from_sib = _swap_halves(views, axes)
        exact, rounded = [None] * len(views), [None] * len(views)
        for shape in dict.fromkeys(flat):
            ids = [k for k, f in enumerate(flat) if f == shape]
            ex, ro = _presum([views[k].reshape(shape) for k in ids],
                             [from_sib[k].reshape(shape[:1] + shape[2:]) for k in ids], c_idx, tags_[ids[0]])
            for k, e, r in zip(ids, ex, ro):
                exact[k], rounded[k] = e.reshape(unflat[k]), r.reshape(unflat[k])
        return _exchange_start(rounded, kinds_, group), exact

    g_ap = _weight_grad(y_attn, [(d_pa, 0, 2)], "w_attn_proj")
    g_rp = _weight_grad(y_rnn, [(d_pr, 0, 2)], "w_rnn_proj")
    g_o = _weight_grad(merged, [(d_o, 0, 2)], "w_out")
    sq_half = (N_CHIPS, SHARD_ROWS // 2, D_MODEL)
    views1 = [g_ap.reshape(sq), g_rp.reshape(sq), g_o.reshape(sq)]
    sw_ssems, sw_rsems, views1, sib1, sw_token = _swap_halves_start(views1, [1, 1, 1], "proj")
    d_q, d_kv, d_ga, d_sinks = _attn_backward(proj, qr_b, kr_b, p_all, d_ya, tabs, sw_token[0:1, 0:16])
    views1, sib1 = _swap_halves_wait(sw_ssems, sw_rsems, views1, sib1, d_q, "proj")
    started1, own1 = chip_sum_and_start(views1, [1, 1, 1], [sq] * 3, [sq_half] * 3,
                                        ["w_attn_proj", "w_rnn_proj", "w_out"], ["sq"] * 3, "proj", from_sib=sib1)
    d_b, d_conv_w, d_conv_b, d_rwa, d_rwx, d_ba, d_bx, d_lam = _rnn_backward(
        proj, pos_col, h_rnn, rnn_saved, d_yr, conv_w_f, rwa_f, rwx_f, rg_lambda + started1[4][0:1, 0:1])
    pieces = [(d_q, CB_Q, 2), (d_kv, CB_KV, 1), (d_ga, CB_GA, 2), (d_b, CB_XR, 4), (d_c, CB_MA, 4)]
    g_in = _weight_grad(h_t, pieces, "w_in", a_is_transposed=True)
    started2, own2 = chip_sum_and_start(
        [g_in.reshape(2, D_MODEL // 2, IN_W), d_rwa.reshape(rg), d_rwx.reshape(rg)], [0, 2, 2],
        [(1, 2, D_MODEL // 2, IN_W), rg_flat, rg_flat],
        [(D_MODEL // 2, IN_W), (RNN_BLOCKS, N_CHIPS, SHARD_RG // 2, RNN_BW), (RNN_BLOCKS, N_CHIPS, SHARD_RG // 2, RNN_BW)],
        ["w_in", "rg_wa", "rg_wx"], ["in", "rg", "rg"], "in")
    grad_x, d_shift, d_scale, d_norm_g = _input_backward(pieces, w_in_f, x2d, dx2, mod_row + started2[4][0, 0], norm_g)

    d_mod = jnp.concatenate([d_shift, d_scale, d_gate], axis=1)
    small = _pack_small(d_mod, d_norm_g, d_conv_b, d_ba, d_bx, d_lam, d_final_g, d_sinks[:, :N_HEADS], d_conv_w, loss_vec)
    slabs = lax.dynamic_update_slice(jnp.zeros((N_DEV, SMALL_ROWS, D_MODEL), F32), small[None], (my_dev, 0, 0))
    gs_ssem, gs_rsem, slabs, gs_token = _gather_small_start(slabs)
    _, lands1 = _exchange_wait(*started1[:4], gs_token, "proj")
    _, lands2 = _exchange_wait(*started2[:4], gs_token, "in")
    tags = ["w_in", "w_attn_proj", "w_rnn_proj", "w_out", "rg_wa", "rg_wx"]
    chip_sums = [own2[0]] + list(own1) + list(own2[1:])
    lands = [lands2[0]] + list(lands1) + list(lands2[1:])
    where = jnp.concatenate([chip_idx, c_idx])
    kinds = ["in", "sq", "sq", "sq", "rg", "rg"]
    groups = [[0], [1, 2, 3], [4, 5]]
    halves = [None] * 6
    for ids in groups:
        for i, half in zip(ids, _sum_landed(kinds[ids[0]], [chip_sums[i] for i in ids], [lands[i] for i in ids], where,
                                            tags[ids[0]])):
            halves[i] = half
    half_axes = [0, 0, 0, 0, 1, 1]
    asm_ssems, asm_rsems, halves, asm_token = _assemble_start(halves, half_axes)
    res = {}
    small_all = _gather_small_wait(gs_ssem, gs_rsem, slabs, asm_token)
    dmod_all = small_all[:, ROW_MOD:ROW_MOD + 3, :].reshape(N_DEV, ADA_W)
    dmod_cols = lax.dynamic_slice_in_dim(dmod_all, my_chip * SHARD_ADA, SHARD_ADA, axis=1)
    c_t = jnp.pad(jnp.transpose(c_all.reshape(N_DEV, D_MODEL)), ((0, 0), (0, 128 - N_DEV)))
    dmod_cols = jnp.pad(dmod_cols, ((0, 128 - N_DEV), (0, 0)))
    res["w_ada"] = [o.reshape(w_ada.shape) for o in _adamw_w_ada(c_t, dmod_cols, w_ada[0], m_w_ada[0], v_w_ada[0])]

    def full_conv(a):
        return lax.dynamic_update_slice_in_dim(jnp.zeros((CONV_W, D_MODEL), F32), a[0], my_chip * (D_MODEL // N_CHIPS), axis=1)

    packed = [_pack_small(p[0], p[1], p[2], p[3], p[4], p[5], p[6], p[7], full_conv(p[8])) for p in (
        (b_ada, norm_g, conv_b, rg_ba, rg_bx, rg_lambda, final_g, attn_sinks, conv_w),
        (m_b_ada, m_norm_g, m_conv_b, m_rg_ba, m_rg_bx, m_rg_lambda, m_final_g, m_attn_sinks, m_conv_w),
        (v_b_ada, v_norm_g, v_conv_b, v_rg_ba, v_rg_bx, v_rg_lambda, v_final_g, v_attn_sinks, v_conv_w))]
    small_out = _adamw_small(small_all, *packed)

    grads = _assemble_wait(asm_ssems, asm_rsems, halves, half_axes, small_out[0])
    shapes2d = [(D_MODEL, SHARD_IN), (SHARD_ROWS, D_MODEL), (SHARD_ROWS, D_MODEL), (SHARD_ROWS, D_MODEL),
                (RNN_BLOCKS * SHARD_RG, RNN_BW), (RNN_BLOCKS * SHARD_RG, RNN_BW)]
    big_w = [w_in, w_attn_proj, w_rnn_proj, w_out, rg_wa, rg_wx]
    big_m = [m_w_in, m_w_attn_proj, m_w_rnn_proj, m_w_out, m_rg_wa, m_rg_wx]
    big_v = [v_w_in, v_w_attn_proj, v_w_rnn_proj, v_w_out, v_rg_wa, v_rg_wx]
    for ids in groups:
        flat2d = lambda arrs: [arrs[i].reshape(shapes2d[i]) for i in ids]
        outs = _adamw_shard(flat2d(grads), flat2d(big_w), flat2d(big_m), flat2d(big_v), tags[ids[0]])
        for i, four in zip(ids, outs):
            res[tags[i]] = [o.reshape(big_w[i].shape) for o in four]

    def unpack(slab):
        cw = lax.dynamic_slice_in_dim(slab[ROW_CONV_W:ROW_CONV_W + CONV_W], my_chip * (D_MODEL // N_CHIPS),
                                      D_MODEL // N_CHIPS, axis=1)
        return {
            "b_ada": slab[ROW_MOD:ROW_MOD + 3].reshape(1, ADA_W), "norm_g": slab[ROW_NORM_G:ROW_NORM_G + 1],
            "conv_b": slab[ROW_CONV_B:ROW_CONV_B + 1], "rg_ba": slab[ROW_BA:ROW_BA + 1], "rg_bx": slab[ROW_BX:ROW_BX + 1],
            "rg_lambda": slab[ROW_LAM:ROW_LAM + 1], "final_g": slab[ROW_FINAL_G], "attn_sinks": slab[ROW_SINKS:ROW_SINKS + 1, :N_HEADS],
            "conv_w": cw[None],
        }

    small_res = [unpack(s) for s in small_out]
    order = ["w_ada", "b_ada", "norm_g", "w_in", "attn_sinks", "conv_w", "conv_b", "rg_wa", "rg_ba", "rg_wx", "rg_bx",
             "rg_lambda", "w_attn_proj", "w_rnn_proj", "w_out", "final_g"]
    loss = small_out[0][ROW_LOSS, 0]
    outs = [loss, grad_x[None]]
    for kind in range(4):
        for name in order:
            outs.append(res[name][kind] if name in res else small_res[kind][name])
    return tuple(outs)
```

```python
import numpy as np
import jax
import jax.numpy as jnp
from jax import lax
from jax.experimental import pallas as pl
from jax.experimental.pallas import tpu as pltpu

F32 = jnp.float32
BF16 = jnp.bfloat16

D_MODEL = 1024
N_HEADS = 16
N_KV = 4
HEAD_DIM = 64
GROUP = N_HEADS // N_KV
BLOCK = 128
KV_W = N_KV * HEAD_DIM
ROT_HALF = 8
ROPE_THETA = 500000.0
ATTN_SCALE = 0.125
RNN_BLOCKS = 4
RNN_BW = 256
CONV_W = 4
LRU_C = 8.0
NORM_EPS = 1e-6
IN_W = 6656
CB = 512
N_CB = IN_W // CB
CB_Q, CB_KV, CB_GA, CB_XR, CB_GR, CB_MA, CB_MR = 0, 2, 3, 5, 7, 9, 11
V_COL_BLOCK = 5
N_CHIPS = 4
N_DEV = 8
SHARD_IN = IN_W // N_CHIPS
SHARD_ROWS = D_MODEL // N_CHIPS
SHARD_RG = RNN_BW // N_CHIPS
ADA_W = 3 * D_MODEL
SHARD_ADA = ADA_W // N_CHIPS
SMALL_ROWS = 16

ADAM_LR = 0.001
ADAM_B1 = 0.9
ADAM_B2 = 0.999
ADAM_EPS = 1e-08
ADAM_WD = 0.01
ADAM_STEP = 10

VMEM_LIMIT_V7X = 52 * 1024 * 1024
MESH = pl.DeviceIdType.MESH
ANY = pl.BlockSpec(memory_space=pl.ANY)
VMEM_SPEC = pl.BlockSpec(memory_space=pltpu.VMEM)


def _in_hbm(*arrays):
    return [pltpu.with_memory_space_constraint(a, pltpu.HBM) for a in arrays]


def _cp(*sem):
    return pltpu.CompilerParams(dimension_semantics=sem if sem else None, vmem_limit_bytes=VMEM_LIMIT_V7X)


def _dot(a, b):
    return jnp.dot(a, b, preferred_element_type=F32)


def _dot_nt(a, b):
    return lax.dot_general(a, b, (((1,), (1,)), ((), ())), preferred_element_type=F32)


def _dot_tn(a, b):
    return lax.dot_general(a, b, (((0,), (0,)), ((), ())), preferred_element_type=F32)


def _sigmoid(z):
    return 1.0 / (1.0 + jnp.exp(-z))


def _softplus(z):
    u = jnp.exp(-jnp.abs(z))
    log1p_u = jnp.where(u < 1e-3, u * (1.0 - u * (0.5 - u * (1.0 / 3.0))), jnp.log(1.0 + u))
    return jnp.maximum(z, 0.0) + log1p_u


def _rms(xf):
    return lax.rsqrt(jnp.mean(xf * xf, axis=-1, keepdims=True) + NORM_EPS)


def _me():
    return lax.axis_index("x"), lax.axis_index("y"), lax.axis_index("c")


def _peer(mask):
    x, y, c = _me()
    fx, fy, fc = (mask >> 2) & 1, (mask >> 1) & 1, mask & 1
    return (x ^ fx if fx else x, y ^ fy if fy else y, c ^ fc if fc else c)


def _chip_of(pos):
    return pos[0] * 2 + pos[1]


SIBLING_COLLECTIVE_ID = 0
SIBLING_ONLY = pltpu.CompilerParams(collective_id=SIBLING_COLLECTIVE_ID)


def _sibling_handshake():
    barrier = pltpu.get_barrier_semaphore()
    pl.semaphore_signal(barrier, inc=1, device_id=_peer(1), device_id_type=MESH)
    pl.semaphore_wait(barrier, 1)


CHIP_MASKS = (4, 2, 6)
ALL_MASKS = (1, 2, 3, 4, 5, 6, 7)


HBM_SPEC = pl.BlockSpec(memory_space=pltpu.HBM)
SEM_SPEC = pl.BlockSpec(memory_space=pltpu.SEMAPHORE)
SPLIT_COPY = pltpu.CompilerParams(has_side_effects=pltpu.SideEffectType.DATAFLOW_SIDE_EFFECTING)
N_BIG = 6
FULL_SHAPES = (
    (2, D_MODEL // 2, IN_W),
    (N_CHIPS, 2, SHARD_ROWS // 2, D_MODEL), (N_CHIPS, 2, SHARD_ROWS // 2, D_MODEL), (N_CHIPS, 2, SHARD_ROWS // 2, D_MODEL),
    (RNN_BLOCKS, N_CHIPS, 2, SHARD_RG // 2, RNN_BW), (RNN_BLOCKS, N_CHIPS, 2, SHARD_RG // 2, RNN_BW),
)


def _slot(full, idx, chip, half):
    if idx == 0:
        return full.at[half, :, pl.ds(pl.multiple_of(chip * SHARD_IN, 128), SHARD_IN)]
    return full.at[chip, half] if idx in (1, 2, 3) else full.at[:, chip, half]


def _three_halves(full, idx):
    return full.at[pl.ds(0, 3), 0] if idx in (1, 2, 3) else full.at[:, pl.ds(0, 3), 0]


def _gather_start(fulls, after):
    def body(*refs):
        full_refs = refs[:N_BIG]
        ssems, rsems = refs[N_BIG + 1:N_BIG + 5], refs[N_BIG + 5:N_BIG + 9]
        token = refs[2 * N_BIG + 9]
        me = _me()
        my_chip = _chip_of(me)
        for idx in range(N_BIG):
            for k, mask in enumerate(CHIP_MASKS):
                pair = k if idx == 0 else 3
                mine = _slot(full_refs[idx], idx, my_chip, me[2])
                pltpu.make_async_remote_copy(src_ref=mine, dst_ref=mine, send_sem=ssems[pair], recv_sem=rsems[pair],
                                             device_id=_peer(mask), device_id_type=MESH).start()
        token[...] = jnp.zeros_like(token)

    sem = pltpu.SemaphoreType.DMA(())
    out_shape = (sem,) * 8 + tuple(pltpu.HBM(f.shape, f.dtype) for f in fulls) + (jax.ShapeDtypeStruct((8, 128), F32),)
    outs = pl.pallas_call(
        body, out_shape=out_shape, name="gather_start",
        in_specs=[HBM_SPEC] * N_BIG + [ANY], out_specs=tuple([SEM_SPEC] * 8 + [HBM_SPEC] * N_BIG + [VMEM_SPEC]),
        input_output_aliases={i: 8 + i for i in range(N_BIG)}, compiler_params=SPLIT_COPY,
    )(*[pltpu.with_memory_space_constraint(f, pltpu.HBM) for f in fulls], after)
    return outs[0:4], outs[4:8], outs[8:8 + N_BIG], outs[8 + N_BIG]


def _gather_wait(ssem, rsem, arrays, idxs, after, tag):
    n = len(arrays)

    def body(*refs):
        full_refs, ssem_ref, rsem_ref = refs[:n], refs[n], refs[n + 1]
        me = _me()
        for full, idx in zip(full_refs, idxs):
            region = _slot(full, 0, _chip_of(me), me[2]) if idx == 0 else _three_halves(full, idx)
            arrived = pltpu.make_async_remote_copy(
                src_ref=region, dst_ref=region, send_sem=ssem_ref, recv_sem=rsem_ref, device_id=me, device_id_type=MESH)
            arrived.wait_send()
            arrived.wait_recv()

    outs = pl.pallas_call(
        body, out_shape=tuple(pltpu.HBM(a.shape, a.dtype) for a in arrays), name=f"gather_wait_{tag}",
        in_specs=[HBM_SPEC] * n + [SEM_SPEC, SEM_SPEC, ANY], out_specs=tuple([HBM_SPEC] * n),
        input_output_aliases={i: i for i in range(n)}, compiler_params=SPLIT_COPY,
    )(*arrays, ssem, rsem, after)
    return list(outs)


def _forward_halves(arrays, items, tag):
    n, m = len(arrays), len(items)

    def body(*refs):
        outs, ssem, rsem = refs[n:2 * n], refs[2 * n], refs[2 * n + 1]
        me = _me()
        sib = _peer(1)
        _sibling_handshake()
        cps = []
        for j, (pos, idx, k) in enumerate(items):
            chip = _chip_of(_peer(CHIP_MASKS[k]))
            cp = pltpu.make_async_remote_copy(
                src_ref=_slot(outs[pos], idx, chip, me[2]), dst_ref=_slot(outs[pos], idx, chip, me[2]),
                send_sem=ssem.at[j], recv_sem=rsem.at[j], device_id=sib, device_id_type=MESH)
            cp.start()
            cps.append(cp)
        for j, (pos, idx, k) in enumerate(items):
            chip = _chip_of(_peer(CHIP_MASKS[k]))
            pltpu.make_async_remote_copy(
                src_ref=_slot(outs[pos], idx, chip, me[2]), dst_ref=_slot(outs[pos], idx, chip, 1 - me[2]),
                send_sem=ssem.at[j], recv_sem=rsem.at[j], device_id=sib, device_id_type=MESH).wait_recv()
        for cp in cps:
            cp.wait_send()

    outs = pl.pallas_call(
        body, out_shape=tuple(jax.ShapeDtypeStruct(a.shape, a.dtype) for a in arrays), name=f"forward_halves_{tag}",
        in_specs=[ANY] * n, out_specs=tuple([ANY] * n), input_output_aliases={i: i for i in range(n)},
        scratch_shapes=[pltpu.SemaphoreType.DMA((m,)), pltpu.SemaphoreType.DMA((m,))], compiler_params=SIBLING_ONLY,
    )(*arrays)
    return list(outs)


def _gather_mod(c_row, w_ada_s, conv_w_s):
    def body(c_ref, wada_ref, cw_s, cw_f, call_ref, mod_ref, wsend, wrecv, lsem, csend, crecv, msend, mrecv):
        me = _me()
        my_chip = _chip_of(me)
        my_dev = my_chip * 2 + me[2]
        sends = []
        for k, mask in enumerate(CHIP_MASKS):
            cp = pltpu.make_async_remote_copy(src_ref=cw_s, dst_ref=cw_f.at[my_chip], send_sem=wsend.at[k], recv_sem=wrecv.at[k],
                                              device_id=_peer(mask), device_id_type=MESH)
            cp.start()
            sends.append(cp)
        local = [pltpu.make_async_copy(cw_s, cw_f.at[my_chip], lsem.at[0])]
        for cp in local:
            cp.start()

        call_ref[my_dev] = c_ref[0]
        csends = []
        for k, mask in enumerate(ALL_MASKS):
            cp = pltpu.make_async_remote_copy(
                src_ref=c_ref.at[0], dst_ref=call_ref.at[my_dev],
                send_sem=csend.at[k], recv_sem=crecv.at[k], device_id=_peer(mask), device_id_type=MESH)
            cp.start()
            csends.append(cp)
        for k, mask in enumerate(ALL_MASKS):
            frm = _peer(mask)
            pltpu.make_async_remote_copy(
                src_ref=c_ref.at[0], dst_ref=call_ref.at[_chip_of(frm) * 2 + frm[2]],
                send_sem=csend.at[k], recv_sem=crecv.at[k], device_id=frm, device_id_type=MESH).wait_recv()
        for cp in csends:
            cp.wait_send()

        c_all = call_ref[...].reshape(N_DEV, D_MODEL).astype(BF16)
        mod_ref[my_chip] = _dot(c_all, wada_ref[...].astype(BF16))
        msends = []
        for k, mask in enumerate(CHIP_MASKS):
            cp = pltpu.make_async_remote_copy(
                src_ref=mod_ref.at[my_chip], dst_ref=mod_ref.at[my_chip],
                send_sem=msend.at[k], recv_sem=mrecv.at[k], device_id=_peer(mask), device_id_type=MESH)
            cp.start()
            msends.append(cp)
        for k, mask in enumerate(CHIP_MASKS):
            frm = _peer(mask)
            pltpu.make_async_remote_copy(
                src_ref=mod_ref.at[my_chip], dst_ref=mod_ref.at[_chip_of(frm)],
                send_sem=msend.at[k], recv_sem=mrecv.at[k], device_id=frm, device_id_type=MESH).wait_recv()
        for cp in msends:
            cp.wait_send()

        for k, mask in enumerate(CHIP_MASKS):
            frm = _peer(mask)
            pltpu.make_async_remote_copy(src_ref=cw_s, dst_ref=cw_f.at[_chip_of(frm)], send_sem=wsend.at[k], recv_sem=wrecv.at[k],
                                         device_id=frm, device_id_type=MESH).wait_recv()
        for cp in sends:
            cp.wait_send()
        for cp in local:
            cp.wait()

    out_shape = (
        jax.ShapeDtypeStruct((N_CHIPS, CONV_W, D_MODEL // N_CHIPS), F32),
        jax.ShapeDtypeStruct((N_DEV, 1, D_MODEL), F32),
        jax.ShapeDtypeStruct((N_CHIPS, N_DEV, SHARD_ADA), F32),
    )
    return pl.pallas_call(
        body, out_shape=out_shape, name="gather_mod",
        in_specs=[VMEM_SPEC, VMEM_SPEC, ANY], out_specs=(ANY, VMEM_SPEC, VMEM_SPEC),
        scratch_shapes=[
            pltpu.SemaphoreType.DMA((3,)), pltpu.SemaphoreType.DMA((3,)), pltpu.SemaphoreType.DMA((1,)),
            pltpu.SemaphoreType.DMA((7,)), pltpu.SemaphoreType.DMA((7,)),
            pltpu.SemaphoreType.DMA((3,)), pltpu.SemaphoreType.DMA((3,)),
        ],
        compiler_params=pltpu.CompilerParams(vmem_limit_bytes=VMEM_LIMIT_V7X),
    )(c_row, w_ada_s, conv_w_s)


def _cast_place(shards, chip_idx, places):
    n = len(shards)

    def body(chip_ref, *refs):
        for s_ref, o_ref in zip(refs[:n], refs[n:]):
            o_ref[...] = s_ref[...].astype(BF16)

    grid_spec = pltpu.PrefetchScalarGridSpec(
        num_scalar_prefetch=1, grid=(1,),
        in_specs=[pl.BlockSpec(s.shape, lambda i, chip_ref, nd=s.ndim: (0,) * nd) for s in shards],
        out_specs=tuple(pl.BlockSpec(block, lambda i, chip_ref, im=im: im(chip_ref[0])) for _, block, im in places))
    return pl.pallas_call(
        body, out_shape=tuple(jax.ShapeDtypeStruct(full, BF16) for full, _, _ in places), grid_spec=grid_spec,
        name="cast_place", compiler_params=_cp("arbitrary"),
    )(chip_idx, *_in_hbm(*shards))


def _shard_of(ref, kind, chip):
    if kind == "in":
        return ref.at[:, pl.ds(pl.multiple_of(chip * SHARD_IN, 128), SHARD_IN)]
    return ref.at[chip] if kind == "sq" else ref.at[:, chip]


def _land_shape(src, kind):
    if kind == "in":
        return (3, src.shape[0], SHARD_IN)
    return (3,) + src.shape[1:] if kind == "sq" else (3, src.shape[0]) + src.shape[2:]


def _exchange_start(srcs, kinds, tag):
    n = len(srcs)
    lands = [pltpu.with_memory_space_constraint(lax.empty(_land_shape(s, k), s.dtype), pltpu.HBM) for s, k in zip(srcs, kinds)]

    def body(*refs):
        src_refs, land_refs = refs[:n], refs[n:2 * n]
        ssems, rsems = refs[2 * n:3 * n], refs[3 * n:4 * n]
        token = refs[6 * n]
        for i in range(n):
            for k, mask in enumerate(CHIP_MASKS):
                to = _peer(mask)
                pltpu.make_async_remote_copy(
                    src_ref=_shard_of(src_refs[i], kinds[i], _chip_of(to)), dst_ref=land_refs[i].at[k],
                    send_sem=ssems[i], recv_sem=rsems[i], device_id=to, device_id_type=MESH).start()
        token[...] = jnp.zeros_like(token)

    sem = pltpu.SemaphoreType.DMA(())
    out_shape = ((sem,) * (2 * n) + tuple(pltpu.HBM(s.shape, s.dtype) for s in srcs)
                 + tuple(pltpu.HBM(l.shape, l.dtype) for l in lands) + (jax.ShapeDtypeStruct((8, 128), F32),))
    outs = pl.pallas_call(
        body, out_shape=out_shape, name=f"exchange_start_{tag}",
        in_specs=[HBM_SPEC] * (2 * n), out_specs=tuple([SEM_SPEC] * (2 * n) + [HBM_SPEC] * (2 * n) + [VMEM_SPEC]),
        input_output_aliases={i: 2 * n + i for i in range(2 * n)},
        compiler_params=pltpu.CompilerParams(has_side_effects=pltpu.SideEffectType.DATAFLOW_SIDE_EFFECTING),
    )(*[pltpu.with_memory_space_constraint(s, pltpu.HBM) for s in srcs], *lands)
    return outs[:n], outs[n:2 * n], outs[2 * n:3 * n], outs[3 * n:4 * n], outs[4 * n]


def _exchange_wait(ssems, rsems, srcs, lands, after, tag):
    n = len(srcs)

    def body(*refs):
        land_refs = refs[n:2 * n]
        ssem_refs, rsem_refs = refs[2 * n:3 * n], refs[3 * n:4 * n]
        for i in range(n):
            all_three = pltpu.make_async_remote_copy(
                src_ref=land_refs[i], dst_ref=land_refs[i], send_sem=ssem_refs[i], recv_sem=rsem_refs[i],
                device_id=_me(), device_id_type=MESH)
            all_three.wait_send()
            all_three.wait_recv()

    outs = pl.pallas_call(
        body, out_shape=tuple(pltpu.HBM(a.shape, a.dtype) for a in list(srcs) + list(lands)), name=f"exchange_wait_{tag}",
        in_specs=[HBM_SPEC] * (2 * n) + [SEM_SPEC] * (2 * n) + [ANY], out_specs=tuple([HBM_SPEC] * (2 * n)),
        input_output_aliases={i: i for i in range(2 * n)},
        compiler_params=pltpu.CompilerParams(has_side_effects=pltpu.SideEffectType.DATAFLOW_SIDE_EFFECTING),
    )(*srcs, *lands, *ssems, *rsems, after)
    return outs[:n], outs[n:]


def _gather_small_start(slabs):
    def body(slabs_ref, ssem, rsem, slabs_out, token):
        me = _me()
        mine = slabs_ref.at[_chip_of(me) * 2 + me[2]]
        for mask in ALL_MASKS:
            pltpu.make_async_remote_copy(src_ref=mine, dst_ref=mine, send_sem=ssem, recv_sem=rsem,
                                         device_id=_peer(mask), device_id_type=MESH).start()
        token[...] = jnp.zeros_like(token)

    sem = pltpu.SemaphoreType.DMA(())
    return pl.pallas_call(
        body, out_shape=(sem, sem, pltpu.HBM(slabs.shape, slabs.dtype), jax.ShapeDtypeStruct((8, 128), F32)),
        name="gather_small_start", in_specs=[HBM_SPEC], out_specs=(SEM_SPEC, SEM_SPEC, HBM_SPEC, VMEM_SPEC),
        input_output_aliases={0: 2}, compiler_params=SPLIT_COPY,
    )(pltpu.with_memory_space_constraint(slabs, pltpu.HBM))


def _gather_small_wait(ssem, rsem, slabs, after):
    def body(slabs_ref, ssem_ref, rsem_ref, after_ref, slabs_out):
        seven = slabs_ref.at[pl.ds(0, N_DEV - 1)]
        arrived = pltpu.make_async_remote_copy(
            src_ref=seven, dst_ref=seven, send_sem=ssem_ref, recv_sem=rsem_ref, device_id=_me(), device_id_type=MESH)
        arrived.wait_send()
        arrived.wait_recv()

    return pl.pallas_call(
        body, out_shape=pltpu.HBM(slabs.shape, slabs.dtype), name="gather_small_wait",
        in_specs=[HBM_SPEC, SEM_SPEC, SEM_SPEC, ANY], out_specs=HBM_SPEC, input_output_aliases={0: 0},
        compiler_params=SPLIT_COPY,
    )(slabs, ssem, rsem, after)


def _half_of(ref, axis, half):
    return ref.at[(slice(None),) * axis + (half,)]


def _swap_halves(parts, axes):
    n = len(parts)

    def body(*refs):
        ins, outs, ssem, rsem = refs[:n], refs[n:2 * n], refs[2 * n], refs[2 * n + 1]
        c = lax.axis_index("c")
        _sibling_handshake()
        cps = [pltpu.make_async_remote_copy(src_ref=_half_of(ins[i], axes[i], 1 - c), dst_ref=outs[i], send_sem=ssem.at[i],
                                            recv_sem=rsem.at[i], device_id=_peer(1), device_id_type=MESH) for i in range(n)]
        for cp in cps:
            cp.start()
        for cp in cps:
            cp.wait()

    shapes = [p.shape[:a] + p.shape[a + 1:] for p, a in zip(parts, axes)]
    return pl.pallas_call(
        body, out_shape=tuple(jax.ShapeDtypeStruct(s, p.dtype) for s, p in zip(shapes, parts)), name="swap_halves",
        in_specs=[ANY] * n, out_specs=tuple([ANY] * n),
        scratch_shapes=[pltpu.SemaphoreType.DMA((n,)), pltpu.SemaphoreType.DMA((n,))], compiler_params=SIBLING_ONLY,
    )(*parts)


def _swap_halves_start(parts, axes, tag):
    n = len(parts)
    lands = [pltpu.with_memory_space_constraint(lax.empty(p.shape[:a] + p.shape[a + 1:], p.dtype), pltpu.HBM)
             for p, a in zip(parts, axes)]

    def body(*refs):
        ins, land_refs, ssems, rsems, token = refs[:n], refs[n:2 * n], refs[2 * n:3 * n], refs[3 * n:4 * n], refs[6 * n]
        c = lax.axis_index("c")
        for i in range(n):
            pltpu.make_async_remote_copy(src_ref=_half_of(ins[i], axes[i], 1 - c), dst_ref=land_refs[i], send_sem=ssems[i],
                                         recv_sem=rsems[i], device_id=_peer(1), device_id_type=MESH).start()
        token[...] = jnp.zeros_like(token)

    sem = pltpu.SemaphoreType.DMA(())
    out_shape = ((sem,) * (2 * n) + tuple(pltpu.HBM(a.shape, a.dtype) for a in list(parts) + lands)
                 + (jax.ShapeDtypeStruct((8, 128), F32),))
    outs = pl.pallas_call(
        body, out_shape=out_shape, name=f"swap_halves_start_{tag}",
        in_specs=[HBM_SPEC] * (2 * n), out_specs=tuple([SEM_SPEC] * (2 * n) + [HBM_SPEC] * (2 * n) + [VMEM_SPEC]),
        input_output_aliases={i: 2 * n + i for i in range(2 * n)}, compiler_params=SPLIT_COPY,
    )(*[pltpu.with_memory_space_constraint(p, pltpu.HBM) for p in parts], *lands)
    return outs[:n], outs[n:2 * n], outs[2 * n:3 * n], outs[3 * n:4 * n], outs[4 * n]


def _swap_halves_wait(ssems, rsems, parts, lands, after, tag):
    n = len(parts)

    def body(*refs):
        land_refs, ssem_refs, rsem_refs = refs[n:2 * n], refs[2 * n:3 * n], refs[3 * n:4 * n]
        for i in range(n):
            moved = pltpu.make_async_remote_copy(
                src_ref=land_refs[i], dst_ref=land_refs[i], send_sem=ssem_refs[i], recv_sem=rsem_refs[i],
                device_id=_me(), device_id_type=MESH)
            moved.wait_send()
            moved.wait_recv()

    outs = pl.pallas_call(
        body, out_shape=tuple(pltpu.HBM(a.shape, a.dtype) for a in list(parts) + list(lands)), name=f"swap_halves_wait_{tag}",
        in_specs=[HBM_SPEC] * (2 * n) + [SEM_SPEC] * (2 * n) + [ANY], out_specs=tuple([HBM_SPEC] * (2 * n)),
        input_output_aliases={i: i for i in range(2 * n)}, compiler_params=SPLIT_COPY,
    )(*parts, *lands, *ssems, *rsems, after)
    return list(outs[:n]), list(outs[n:])


def _presum(mines, sibs, c_idx, tag):
    n = len(mines)
    S, _, R, C = mines[0].shape
    tr = min(R, 256)
    tc = SHARD_IN if C % SHARD_IN == 0 else (C // 2 if n > 1 and C % 256 == 0 else C)

    def body(c_ref, *refs):
        for k in range(n):
            total = refs[k][:, 0] + refs[n + k][...]
            refs[2 * n + k][...] = total
            refs[3 * n + k][...] = total.astype(BF16)

    out_spec = pl.BlockSpec((S, tr, tc), lambda i, j, c_ref: (0, i, j))
    grid_spec = pltpu.PrefetchScalarGridSpec(
        num_scalar_prefetch=1, grid=(R // tr, C // tc),
        in_specs=[pl.BlockSpec((S, 1, tr, tc), lambda i, j, c_ref: (0, c_ref[0], i, j))] * n + [out_spec] * n,
        out_specs=(out_spec,) * (2 * n))
    outs = pl.pallas_call(
        body, out_shape=(jax.ShapeDtypeStruct((S, R, C), F32),) * n + (jax.ShapeDtypeStruct((S, R, C), BF16),) * n,
        grid_spec=grid_spec, name=f"presum_{tag}", compiler_params=_cp("parallel", "parallel"),
    )(c_idx, *mines, *sibs)
    return list(outs[:n]), list(outs[n:])


def _assemble_with_sibling(parts, axes):
    n = len(parts)

    def body(*refs):
        outs, ssem, rsem = refs[n:2 * n], refs[2 * n], refs[2 * n + 1]
        c = lax.axis_index("c")
        _sibling_handshake()
        cps = [pltpu.make_async_remote_copy(
            src_ref=_half_of(outs[i], axes[i], c), dst_ref=_half_of(outs[i], axes[i], c), send_sem=ssem.at[i],
            recv_sem=rsem.at[i], device_id=_peer(1), device_id_type=MESH) for i in range(n)]
        for cp in cps:
            cp.start()
        for i in range(n):
            pltpu.make_async_remote_copy(
                src_ref=_half_of(outs[i], axes[i], c), dst_ref=_half_of(outs[i], axes[i], 1 - c), send_sem=ssem.at[i],
                recv_sem=rsem.at[i], device_id=_peer(1), device_id_type=MESH).wait_recv()
        for cp in cps:
            cp.wait_send()

    return pl.pallas_call(
        body, out_shape=tuple(jax.ShapeDtypeStruct(p.shape, p.dtype) for p in parts), name="assemble_with_sibling",
        in_specs=[ANY] * n, out_specs=tuple([ANY] * n), input_output_aliases={i: i for i in range(n)},
        scratch_shapes=[pltpu.SemaphoreType.DMA((n,)), pltpu.SemaphoreType.DMA((n,))], compiler_params=SIBLING_ONLY,
    )(*parts)


def _assemble_start(parts, axes):
    n = len(parts)

    def body(*refs):
        ssems, rsems, outs, token = refs[n:2 * n], refs[2 * n:3 * n], refs[3 * n:4 * n], refs[4 * n]
        c = lax.axis_index("c")
        for i in range(n):
            mine = _half_of(outs[i], axes[i], c)
            pltpu.make_async_remote_copy(src_ref=mine, dst_ref=mine, send_sem=ssems[i], recv_sem=rsems[i],
                                         device_id=_peer(1), device_id_type=MESH).start()
        token[...] = jnp.zeros_like(token)

    sem = pltpu.SemaphoreType.DMA(())
    out_shape = ((sem,) * (2 * n) + tuple(pltpu.HBM(p.shape, p.dtype) for p in parts)
                 + (jax.ShapeDtypeStruct((8, 128), F32),))
    outs = pl.pallas_call(
        body, out_shape=out_shape, name="assemble_start",
        in_specs=[HBM_SPEC] * n, out_specs=tuple([SEM_SPEC] * (2 * n) + [HBM_SPEC] * n + [VMEM_SPEC]),
        input_output_aliases={i: 2 * n + i for i in range(n)}, compiler_params=SPLIT_COPY,
    )(*[pltpu.with_memory_space_constraint(p, pltpu.HBM) for p in parts])
    return outs[:n], outs[n:2 * n], list(outs[2 * n:3 * n]), outs[3 * n]


def _assemble_wait(ssems, rsems, parts, axes, after):
    n = len(parts)

    def body(*refs):
        ssem_refs, rsem_refs = refs[n:2 * n], refs[2 * n:3 * n]
        for i in range(n):
            half = _half_of(refs[i], axes[i], 0)
            moved = pltpu.make_async_remote_copy(
                src_ref=half, dst_ref=half, send_sem=ssem_refs[i], recv_sem=rsem_refs[i],
                device_id=_me(), device_id_type=MESH)
            moved.wait_send()
            moved.wait_recv()

    outs = pl.pallas_call(
        body, out_shape=tuple(pltpu.HBM(p.shape, p.dtype) for p in parts), name="assemble_wait",
        in_specs=[HBM_SPEC] * n + [SEM_SPEC] * (2 * n) + [ANY], out_specs=tuple([HBM_SPEC] * n),
        input_output_aliases={i: i for i in range(n)}, compiler_params=SPLIT_COPY,
    )(*parts, *ssems, *rsems, after)
    return list(outs)


def _rope_lane_frequencies():
    inv = np.float32(ROPE_THETA) ** (-(np.arange(0, 2 * ROT_HALF, 2, dtype=np.float32)) / np.float32(2 * ROT_HALF))
    lane = np.arange(128) % HEAD_DIM
    return jnp.asarray(np.where(lane < 2 * ROT_HALF, inv[lane % ROT_HALF], 0.0).astype(np.float32)[None, :])


def _rope_tables(pos, freq):
    ang = pos.astype(F32) * freq
    c, s = jnp.cos(ang), jnp.sin(ang)
    m = lax.broadcasted_iota(jnp.int32, ang.shape, 1) & (HEAD_DIM - 1)
    return (jnp.where(m < 2 * ROT_HALF, c, 1.0), jnp.where(m < ROT_HALF, -s, 0.0),
            jnp.where((m >= ROT_HALF) & (m < 2 * ROT_HALF), s, 0.0))


def _columns(t):
    return [t[:, i:i + 128] for i in range(0, t.shape[-1], 128)]


def _rope(t, c, sa, sb):
    return jnp.concatenate(
        [x * c + pltpu.roll(x, 128 - ROT_HALF, 1) * sa + pltpu.roll(x, ROT_HALF, 1) * sb for x in _columns(t)], axis=1)


def _unrope(d, c, sa, sb):
    return jnp.concatenate(
        [x * c + pltpu.roll(x * sa, ROT_HALF, 1) + pltpu.roll(x * sb, 128 - ROT_HALF, 1) for x in _columns(d)], axis=1)


def _prenorm(x, mod_row, norm_g, pos_col):
    T = x.shape[0]
    tm = min(T, 512)

    def body(x_ref, mod_ref, g_ref, pos_ref, f_ref, h_ref, ht_ref, c_ref, sa_ref, sb_ref):
        xf = x_ref[...]
        shift, scale = mod_ref[:, 0:D_MODEL], mod_ref[:, D_MODEL:2 * D_MODEL]
        h = (xf * _rms(xf)) * g_ref[...] * (1.0 + scale) + shift
        h_ref[...] = h.astype(BF16)
        ht_ref[...] = h.T.astype(BF16)
        c_ref[...], sa_ref[...], sb_ref[...] = _rope_tables(pos_ref[...], f_ref[...])

    tab = jax.ShapeDtypeStruct((T, 128), F32)
    tok = lambda w: pl.BlockSpec((tm, w), lambda i: (i, 0))
    row = lambda w: pl.BlockSpec((1, w), lambda i: (0, 0))
    outs = pl.pallas_call(
        body, out_shape=(jax.ShapeDtypeStruct((T, D_MODEL), BF16), jax.ShapeDtypeStruct((D_MODEL, T), BF16), tab, tab, tab),
        grid=(T // tm,), name="prenorm",
        in_specs=[tok(D_MODEL), row(ADA_W), row(D_MODEL), tok(1), row(128)],
        out_specs=(tok(D_MODEL), pl.BlockSpec((D_MODEL, tm), lambda i: (0, i)), tok(128), tok(128), tok(128)),
        compiler_params=_cp("parallel"),
    )(x, *_in_hbm(mod_row, norm_g), pos_col, _rope_lane_frequencies())
    return outs[0], outs[1], tuple(outs[2:])


def _in_projection(h, w_in, chips, into, tag):
    T = h.shape[0]
    tm, tn = min(T, 512), SHARD_IN
    k = chips.shape[0]

    def body(chip_ref, h_ref, w_ref, *rest):
        rest[-1][...] = _dot(h_ref[...], w_ref[...])

    w_spec = pl.BlockSpec((D_MODEL, tn), lambda s, i, c: (0, c[s]), **({"pipeline_mode": pl.Buffered(1)} if k == 1 else {}))
    in_specs = [pl.BlockSpec((tm, D_MODEL), lambda s, i, c: (i, 0)), w_spec]
    args = [chips, h, w_in]
    aliases = {}
    if into is not None:
        in_specs.append(ANY)
        args.append(into)
        aliases = {3: 0}
    grid_spec = pltpu.PrefetchScalarGridSpec(num_scalar_prefetch=1, grid=(k, T // tm), in_specs=in_specs,
                                             out_specs=pl.BlockSpec((tm, tn), lambda s, i, c: (i, c[s])))
    return pl.pallas_call(
        body, out_shape=jax.ShapeDtypeStruct((T, IN_W), F32), grid_spec=grid_spec, name=f"in_projection_{tag}",
        input_output_aliases=aliases, compiler_params=_cp("parallel", "parallel"),
    )(*args)


def _attn_mask(n):
    qi = lax.broadcasted_iota(jnp.int32, (GROUP * BLOCK, BLOCK), 0) & (BLOCK - 1)
    j = lax.broadcasted_iota(jnp.int32, (GROUP * BLOCK, BLOCK), 1)
    own = j <= qi
    return own, jnp.logical_not(own) & (n == 0)


def _fold(x, own):
    return jnp.where(own, x[:, BLOCK:2 * BLOCK], x[:, 0:BLOCK])


def _unfold(xf, own):
    zero = jnp.zeros_like(xf)
    return jnp.concatenate([jnp.where(own, zero, xf), jnp.where(own, xf, zero)], axis=1)


ROW_GROUP_HEAD = (0, 2, 1, 3)


def _sink_col(sink_ref, kh):
    rowg = lax.broadcasted_iota(jnp.int32, (GROUP * BLOCK, 1), 0) // BLOCK
    col = jnp.full((GROUP * BLOCK, 1), sink_ref[0, GROUP * kh + ROW_GROUP_HEAD[0]], F32)
    for g in range(1, GROUP):
        col = jnp.where(rowg == g, sink_ref[0, GROUP * kh + ROW_GROUP_HEAD[g]], col)
    return col


def _low_lanes(shape):
    return lax.broadcasted_iota(jnp.int32, shape, 1) < HEAD_DIM


def _kv_pair_operand(prev, cur, kh):
    c = 128 * (kh // 2)
    col = jnp.concatenate([prev[:, c:c + 128], cur[:, c:c + 128]], axis=0).astype(F32)
    if kh % 2 == 0:
        lo = jnp.where(_low_lanes(col.shape), col, 0.0)
        hi = pltpu.roll(lo, HEAD_DIM, 1)
    else:
        hi = jnp.where(_low_lanes(col.shape), 0.0, col)
        lo = pltpu.roll(hi, HEAD_DIM, 1)
    return jnp.concatenate([lo, hi], axis=0).astype(BF16)


def _pair_rows(x, kh):
    c = 2 * 128 * kh
    return jnp.concatenate([x[:, c:c + 128], x[:, c + 128:c + 256]], axis=0)


def _restack(big):
    return jnp.concatenate([big[:, 0:2 * BLOCK], big[:, 2 * BLOCK:4 * BLOCK]], axis=0)


def _unrestack(stacked):
    return jnp.concatenate([stacked[0:2 * BLOCK], stacked[2 * BLOCK:4 * BLOCK]], axis=1)


def _fold_pair(x2, kh):
    low = _low_lanes((2 * BLOCK, 128))
    mixed = jnp.where(low, x2[0:2 * BLOCK], x2[2 * BLOCK:4 * BLOCK])
    total = mixed + pltpu.roll(mixed, HEAD_DIM, 1)
    return jnp.where(low, total, 0.0) if kh % 2 == 0 else jnp.where(low, 0.0, total)


def _attn_scores(qr, k2, kh):
    q2 = _pair_rows(qr, kh).astype(BF16)
    return q2, _restack(_dot_nt(q2, k2))


def _attn_softmax(s, sink_col, mask):
    own, no_key = mask
    s = jnp.where(no_key, -1e30, _fold(s, own))
    m = jnp.maximum(jnp.max(s, axis=-1, keepdims=True), sink_col)
    p = jnp.exp(s - m)
    p_sink = jnp.exp(sink_col - m)
    denom = jnp.sum(p, axis=-1, keepdims=True) + p_sink
    return p / denom, p_sink / denom


def _attn_forward(proj, tabs, sinks):
    T = proj.shape[0]
    nb = T // BLOCK

    def body(q_ref, kvc_ref, kvp_ref, g0_ref, g1_ref, cc, sac, sbc, cp_, sap, sbp, sink_ref, y_ref, qrb_ref, krb_ref, p_ref):
        n = pl.program_id(0)
        tc = tcur = (cc[...], sac[...], sbc[...])
        tprev = (cp_[...], sap[...], sbp[...])
        qr = _rope(q_ref[...], *tc) * ATTN_SCALE
        kr_cur = _rope(kvc_ref[:, 0:KV_W], *tcur)
        kr_prev = _rope(kvp_ref[:, 0:KV_W], *tprev)
        qrb_ref[...] = qr.astype(BF16)
        krb_ref[...] = kr_cur.astype(BF16)
        v_cur, v_prev = kvc_ref[:, KV_W:2 * KV_W], kvp_ref[:, KV_W:2 * KV_W]
        mask = _attn_mask(n)
        outs = []
        k2s = [_kv_pair_operand(kr_prev, kr_cur, kh) for kh in range(N_KV)]
        v2s = [_kv_pair_operand(v_prev, v_cur, kh) for kh in range(N_KV)]
        scores = [_attn_scores(qr, k2s[kh], kh) for kh in range(N_KV)]
        p_parts = []
        for kh in range(N_KV):
            pn, _ = _attn_softmax(scores[kh][1], _sink_col(sink_ref, kh), mask)
            p_parts += [pn[g * BLOCK:(g + 1) * BLOCK] for g in range(GROUP)]
            o_big = _dot(_unrestack(_unfold(pn.astype(BF16), mask[0])), v2s[kh])
            outs += [o_big[0:BLOCK], o_big[BLOCK:2 * BLOCK]]
        p_ref[...] = jnp.concatenate(p_parts, axis=1)
        o = jnp.concatenate(outs, axis=1)
        g = jnp.concatenate([g0_ref[...], g1_ref[...]], axis=1)
        y_ref[...] = (o * (g * _sigmoid(g))).astype(BF16)

    def blk(w, cb):
        return pl.BlockSpec((BLOCK, w), lambda n, cb=cb: (n, cb))

    prev = lambda w, cb: pl.BlockSpec((BLOCK, w), lambda n, cb=cb: (jnp.maximum(n - 1, 0), cb))
    return pl.pallas_call(
        body, grid=(nb,), name="attn_forward",
        out_shape=(jax.ShapeDtypeStruct((T, D_MODEL), BF16), jax.ShapeDtypeStruct((T, D_MODEL), BF16),
                   jax.ShapeDtypeStruct((T, KV_W), BF16), jax.ShapeDtypeStruct((T, N_HEADS * BLOCK), F32)),
        in_specs=[blk(D_MODEL, 0), blk(CB, CB_KV), prev(CB, CB_KV), blk(CB, CB_GA), blk(CB, CB_GA + 1),
                  blk(128, 0), blk(128, 0), blk(128, 0), prev(128, 0), prev(128, 0), prev(128, 0),
                  pl.BlockSpec(memory_space=pltpu.SMEM)],
        out_specs=(blk(D_MODEL, 0), blk(D_MODEL, 0), blk(KV_W, 0), blk(N_HEADS * BLOCK, 0)),
        compiler_params=_cp("parallel"),
    )(proj, proj, proj, proj, proj, *tabs, *tabs, sinks)


def _scan_rows8():
    return lax.broadcasted_iota(jnp.int32, (8, D_MODEL), 0)


def _scan_forward(a_ref, b_ref, h_ref, carry, rows):
    row = _scan_rows8()

    def group(i, carry):
        off = pl.multiple_of(i * 8, 8)
        a, b = a_ref[pl.ds(off, 8), :], b_ref[pl.ds(off, 8), :]
        for d in (1, 2, 4):
            ok = row >= d
            b = jnp.where(ok, a * pltpu.roll(b, d, 0) + b, b)
            a = jnp.where(ok, a * pltpu.roll(a, d, 0), a)
        h = a * carry + b
        h_ref[pl.ds(off, 8), :] = h
        return h[7:8, :]

    return lax.fori_loop(0, rows // 8, group, carry)


def _scan_backward(a_ref, g_ref, lam_ref, carry, rows):
    row = _scan_rows8()

    def group(i, carry):
        off = pl.multiple_of((rows // 8 - 1 - i) * 8, 8)
        a, g = a_ref[pl.ds(off, 8), :], g_ref[pl.ds(off, 8), :]
        b = a * g
        for d in (1, 2, 4):
            ok = row < 8 - d
            b = jnp.where(ok, a * pltpu.roll(b, 8 - d, 0) + b, b)
            a = jnp.where(ok, a * pltpu.roll(a, 8 - d, 0), a)
        mu = a * carry + b
        mu_below = jnp.where(row == 7, carry, pltpu.roll(mu, 7, 0))
        lam_ref[pl.ds(off, 8), :] = g + mu_below
        return mu[0:1, :]

    return lax.fori_loop(0, rows // 8, group, carry)


def _conv_taps(xbuf, xr, tail):
    rows = xr.shape[0]
    xbuf[0:8, :] = tail
    xbuf[8:rows + 8, :] = xr
    return [xbuf[pl.ds(8 - (CONV_W - 1 - k), rows), :] for k in range(CONV_W - 1)] + [xr]


def _rnn_gates(xbuf, xr, tail, cw, cb, wa_ref, wx_ref, ba, bx, sp, reset):
    xs = _conv_taps(xbuf, xr, tail)
    xc = xs[0] * cw[0:1, :]
    for k in range(1, CONV_W):
        xc = xc + xs[k] * cw[k:k + 1, :]
    xc = xc + cb
    xcb = xc.astype(BF16)
    za = jnp.concatenate([_dot(xcb[:, RNN_BW * j:RNN_BW * (j + 1)], wa_ref[j]) for j in range(RNN_BLOCKS)], axis=1) + ba
    zx = jnp.concatenate([_dot(xcb[:, RNN_BW * j:RNN_BW * (j + 1)], wx_ref[j]) for j in range(RNN_BLOCKS)], axis=1) + bx
    r, i = _sigmoid(za), _sigmoid(zx)
    neg_log_a = LRU_C * r * sp
    a_raw = jnp.exp(-neg_log_a)
    mult_raw = jnp.sqrt(jnp.tanh(neg_log_a) * (1.0 + a_raw * a_raw))
    a = jnp.where(reset, 0.0, a_raw)
    mult = jnp.where(reset, 1.0, mult_raw)
    return xc, r, i, a, mult


def _rnn_forward(proj, pos_col, conv_w, conv_b, rwa, rwx, ba, bx, lam):
    T = proj.shape[0]
    tr = min(T, 256)

    def body(x0, x1, g0, g1, pos_ref, cw_ref, cb_ref, wa_ref, wx_ref, ba_ref, bx_ref, lam_ref,
             y_ref, h_ref, xc_ref, r_ref, i_ref, a_ref, mult_ref, xbuf, bbuf, tail, carry):
        t = pl.program_id(0)

        @pl.when(t == 0)
        def _():
            tail[...] = jnp.zeros_like(tail)
            carry[...] = jnp.zeros_like(carry)

        xr = jnp.concatenate([x0[...], x1[...]], axis=1)
        sp = _softplus(-lam_ref[...])
        reset = pos_ref[...] == 0
        xc, r, i, a, mult = _rnn_gates(
            xbuf, xr, tail[...], cw_ref[...], cb_ref[...], wa_ref, wx_ref, ba_ref[...], bx_ref[...], sp, reset)
        xc_ref[...] = xc
        r_ref[...] = r
        i_ref[...] = i
        a_ref[...] = a
        mult_ref[...] = mult
        bbuf[...] = mult * (i * xc)
        last = _scan_forward(a_ref, bbuf, h_ref, carry[0:1, :], tr)
        carry[...] = jnp.broadcast_to(last, carry.shape)
        tail[...] = xr[tr - 8:tr, :]
        g = jnp.concatenate([g0[...], g1[...]], axis=1)
        y_ref[...] = (h_ref[...] * (g * _sigmoid(g))).astype(BF16)

    blk = lambda cb: pl.BlockSpec((tr, CB), lambda t, cb=cb: (t, cb))
    row = lambda w: pl.BlockSpec((1, w), lambda t: (0, 0))
    full3 = pl.BlockSpec((RNN_BLOCKS, RNN_BW, RNN_BW), lambda t: (0, 0, 0))
    tok = pl.BlockSpec((tr, D_MODEL), lambda t: (t, 0))
    act = jax.ShapeDtypeStruct((T, D_MODEL), F32)
    return pl.pallas_call(
        body, out_shape=(jax.ShapeDtypeStruct((T, D_MODEL), BF16),) + (act,) * 6,
        grid=(T // tr,), name="rnn_forward",
        in_specs=[blk(CB_XR), blk(CB_XR + 1), blk(CB_GR), blk(CB_GR + 1), pl.BlockSpec((tr, 1), lambda t: (t, 0)),
                  pl.BlockSpec((CONV_W, D_MODEL), lambda t: (0, 0)), row(D_MODEL), full3, full3,
                  row(D_MODEL), row(D_MODEL), row(D_MODEL)],
        out_specs=(tok,) * 7,
        scratch_shapes=[pltpu.VMEM((tr + 8, D_MODEL), F32), pltpu.VMEM((tr, D_MODEL), F32),
                        pltpu.VMEM((8, D_MODEL), F32), pltpu.VMEM((8, D_MODEL), F32)],
        compiler_params=_cp("arbitrary"),
    )(proj, proj, proj, proj, pos_col, *_in_hbm(conv_w, conv_b, rwa, rwx, ba, bx, lam))


def _merge_and_head(x, target, y_attn, y_rnn, proj, wap, wrp, wo, mod_row, final_g):
    T = x.shape[0]
    tm = min(T, 256)

    def body(x_ref, t_ref, ya_ref, yr_ref, ma0, ma1, mr0, mr1, wap_ref, wrp_ref, wo_ref, mod_ref, fg_ref,
             dx2_ref, mg_ref, do_ref, dpa_ref, dpr_ref, dya_ref, dyr_ref, dc_ref, dfg_ref, dgate_ref, loss_ref):
        i = pl.program_id(0)
        gate = mod_ref[:, 2 * D_MODEL:3 * D_MODEL]
        fg = fg_ref[...]
        pa, pr = _dot(ya_ref[...], wap_ref[...]), _dot(yr_ref[...], wrp_ref[...])
        sa = _sigmoid(jnp.concatenate([ma0[...], ma1[...]], axis=1))
        sr = _sigmoid(jnp.concatenate([mr0[...], mr1[...]], axis=1))
        mb = (sa * pa + sr * pr).astype(BF16)
        o = _dot(mb, wo_ref[...])
        x2 = x_ref[...] + gate * o
        r2 = _rms(x2)
        xn2 = x2 * r2
        err = xn2 * fg - t_ref[...]
        loss_t = 0.5 * jnp.sum(jnp.sum(err * err, axis=-1, keepdims=True) * (1.0 / D_MODEL), axis=0, keepdims=True)
        dy = err * (1.0 / D_MODEL)
        dfg_t = jnp.sum(dy * xn2, axis=0, keepdims=True)
        dxn = dy * fg
        dx2 = r2 * (dxn - xn2 * jnp.mean(dxn * xn2, axis=-1, keepdims=True))
        dgate_t = jnp.sum(dx2 * o, axis=0, keepdims=True)
        dob = (dx2 * gate).astype(BF16)
        dmerged = _dot_nt(dob, wo_ref[...])
        dpa, dpr = (dmerged * sa).astype(BF16), (dmerged * sr).astype(BF16)
        dya, dyr = _dot_nt(dpa, wap_ref[...]), _dot_nt(dpr, wrp_ref[...])
        dx2_ref[...] = dx2
        mg_ref[...] = mb
        do_ref[...] = dob
        dpa_ref[...] = dpa
        dpr_ref[...] = dpr
        dya_ref[...] = dya
        dyr_ref[...] = dyr
        dc_ref[:, 0:D_MODEL] = (dmerged * pa * sa * (1.0 - sa)).astype(BF16)
        dc_ref[:, D_MODEL:2 * D_MODEL] = (dmerged * pr * sr * (1.0 - sr)).astype(BF16)

        @pl.when(i == 0)
        def _():
            dfg_ref[...] = jnp.zeros_like(dfg_ref)
            dgate_ref[...] = jnp.zeros_like(dgate_ref)
            loss_ref[...] = jnp.zeros_like(loss_ref)

        dfg_ref[...] += dfg_t
        dgate_ref[...] += dgate_t
        loss_ref[...] += jnp.broadcast_to(loss_t, loss_ref.shape)

    tok = lambda w: pl.BlockSpec((tm, w), lambda i: (i, 0))
    blk = lambda cb: pl.BlockSpec((tm, CB), lambda i, cb=cb: (i, cb))
    wfull = pl.BlockSpec((D_MODEL, D_MODEL), lambda i: (0, 0), pipeline_mode=pl.Buffered(1))
    row = lambda w: pl.BlockSpec((1, w), lambda i: (0, 0))
    out_shape = (
        jax.ShapeDtypeStruct((T, D_MODEL), F32), jax.ShapeDtypeStruct((T, D_MODEL), BF16),
        jax.ShapeDtypeStruct((T, D_MODEL), BF16), jax.ShapeDtypeStruct((T, D_MODEL), BF16),
        jax.ShapeDtypeStruct((T, D_MODEL), BF16), jax.ShapeDtypeStruct((T, D_MODEL), F32),
        jax.ShapeDtypeStruct((T, D_MODEL), F32), jax.ShapeDtypeStruct((T, 2 * D_MODEL), BF16),
        jax.ShapeDtypeStruct((1, D_MODEL), F32), jax.ShapeDtypeStruct((1, D_MODEL), F32),
        jax.ShapeDtypeStruct((1, 128), F32),
    )
    return pl.pallas_call(
        body, out_shape=out_shape, grid=(T // tm,), name="merge_and_head",
        in_specs=[tok(D_MODEL), tok(D_MODEL), tok(D_MODEL), tok(D_MODEL), blk(CB_MA), blk(CB_MA + 1), blk(CB_MR),
                  blk(CB_MR + 1), wfull, wfull, wfull, row(ADA_W), row(D_MODEL)],
        out_specs=(tok(D_MODEL),) * 7 + (tok(2 * D_MODEL), row(D_MODEL), row(D_MODEL), row(128)),
        compiler_params=_cp("arbitrary"),
    )(x, target, y_attn, y_rnn, proj, proj, proj, proj, wap, wrp, wo, *_in_hbm(mod_row, final_g))


def _attn_backward(proj, qr_b, kr_b, p_all, d_y, tabs, after):
    T = proj.shape[0]
    nb = T // BLOCK

    def body(qrb_ref, krc_ref, krp_ref, vc_ref, vp_ref, g0_ref, g1_ref, dy_ref, p_ref, cc, sac, sbc, cp_, sap, sbp, after_ref,
             dq_ref, dkv_ref, dg_ref, dsink_ref, carry):
        n = pl.program_id(0)

        @pl.when(n == 0)
        def _():
            carry[...] = jnp.zeros_like(carry)
            dsink_ref[...] = jnp.zeros_like(dsink_ref)

        @pl.when(n < nb)
        def _():
            tc = tcur = (cc[...], sac[...], sbc[...])
            tprev = (cp_[...], sap[...], sbp[...])
            qr, kr_cur, kr_prev = qrb_ref[...], krc_ref[...], krp_ref[...]
            v_cur, v_prev = vc_ref[...], vp_ref[...]
            g = jnp.concatenate([g0_ref[...], g1_ref[...]], axis=1)
            sg = _sigmoid(g)
            dy = dy_ref[...]
            d_o = dy * (g * sg)
            mask = _attn_mask(n)
            lane = lax.broadcasted_iota(jnp.int32, (1, 128), 1)
            rowg = lax.broadcasted_iota(jnp.int32, (GROUP * BLOCK, 1), 0) // BLOCK
            o_parts, dq_parts = [], []
            dk_cols, dv_cols = [None, None], [None, None]
            dsink = jnp.zeros((1, 128), F32)
            heads = range(N_KV)
            k2s = [_kv_pair_operand(kr_prev, kr_cur, kh) for kh in heads]
            v2s = [_kv_pair_operand(v_prev, v_cur, kh) for kh in heads]
            q2s = [_pair_rows(qr, kh).astype(BF16) for kh in heads]
            do2s = [_pair_rows(d_o, kh).astype(BF16) for kh in heads]
            dpns = [_fold(_restack(_dot_nt(do2s[kh], v2s[kh])), mask[0]) for kh in heads]
            pns = [jnp.concatenate([p_ref[:, BLOCK * (GROUP * kh + g):BLOCK * (GROUP * kh + g + 1)] for g in range(GROUP)], axis=0)
                   for kh in heads]
            probs = [(pn, 1.0 - jnp.sum(pn, axis=-1, keepdims=True)) for pn in pns]
            p_bigs = [_unrestack(_unfold(probs[kh][0].astype(BF16), mask[0])) for kh in heads]
            o_bigs = [_dot(p_bigs[kh], v2s[kh]) for kh in heads]
            dv2s = [_dot_tn(p_bigs[kh], do2s[kh]) for kh in heads]
            deltas = [jnp.sum(probs[kh][0] * dpns[kh], axis=-1, keepdims=True) for kh in heads]
            ds_bigs = [_unrestack(_unfold((probs[kh][0] * (dpns[kh] - deltas[kh])).astype(BF16), mask[0])) for kh in heads]
            dq2s = [_dot(ds_bigs[kh], k2s[kh]) for kh in heads]
            dk2s = [_dot_tn(ds_bigs[kh], q2s[kh]) for kh in heads]
            for kh in heads:
                o_parts += [o_bigs[kh][0:BLOCK], o_bigs[kh][BLOCK:2 * BLOCK]]
                dq_parts += [dq2s[kh][0:BLOCK], dq2s[kh][BLOCK:2 * BLOCK]]
                dk_c, dv_c = _fold_pair(dk2s[kh], kh), _fold_pair(dv2s[kh], kh)
                c = kh // 2
                dk_cols[c] = dk_c if dk_cols[c] is None else dk_cols[c] + dk_c
                dv_cols[c] = dv_c if dv_cols[c] is None else dv_cols[c] + dv_c
                ds_rows = probs[kh][1] * deltas[kh]
                for gq in range(GROUP):
                    val = -jnp.sum(jnp.where(rowg == gq, ds_rows, 0.0), axis=0, keepdims=True)
                    dsink = dsink + jnp.where(lane == GROUP * kh + ROW_GROUP_HEAD[gq], val, 0.0)
            o = jnp.concatenate(o_parts, axis=1)
            dg_ref[...] = (dy * o * (sg * (1.0 + g * (1.0 - sg)))).astype(BF16)
            dq_ref[...] = (_unrope(jnp.concatenate(dq_parts, axis=1), *tc) * ATTN_SCALE).astype(BF16)
            dk_all, dv_all = jnp.concatenate(dk_cols, axis=1), jnp.concatenate(dv_cols, axis=1)
            dk_prev = _unrope(dk_all[0:BLOCK], *tprev)
            dk_cur = _unrope(dk_all[BLOCK:2 * BLOCK], *tcur)
            dv_prev, dv_cur = dv_all[0:BLOCK], dv_all[BLOCK:2 * BLOCK]
            dkv_ref[...] = (carry[...] + jnp.concatenate([dk_prev, dv_prev], axis=1)).astype(BF16)
            carry[...] = jnp.concatenate([dk_cur, dv_cur], axis=1)
            dsink_ref[...] += dsink

        @pl.when(n == nb)
        def _():
            dkv_ref[...] = carry[...].astype(BF16)

    cur = lambda w, cb: pl.BlockSpec((BLOCK, w), lambda n, cb=cb: (jnp.minimum(n, nb - 1), cb))
    prev = lambda w, cb: pl.BlockSpec((BLOCK, w), lambda n, cb=cb: (jnp.maximum(jnp.minimum(n, nb - 1) - 1, 0), cb))
    out_shape = (jax.ShapeDtypeStruct((T, D_MODEL), BF16), jax.ShapeDtypeStruct((T, 2 * KV_W), BF16),
                 jax.ShapeDtypeStruct((T, D_MODEL), BF16), jax.ShapeDtypeStruct((1, 128), F32))
    return pl.pallas_call(
        body, out_shape=out_shape, grid=(nb + 1,), name="attn_backward",
        in_specs=[cur(D_MODEL, 0), cur(KV_W, 0), prev(KV_W, 0), cur(KV_W, V_COL_BLOCK), prev(KV_W, V_COL_BLOCK),
                  cur(CB, CB_GA), cur(CB, CB_GA + 1), cur(D_MODEL, 0), cur(N_HEADS * BLOCK, 0),
                  cur(128, 0), cur(128, 0), cur(128, 0), prev(128, 0), prev(128, 0), prev(128, 0),
                  pl.BlockSpec(memory_space=pltpu.SMEM)],
        out_specs=(cur(D_MODEL, 0), pl.BlockSpec((BLOCK, 2 * KV_W), lambda n: (jnp.maximum(n - 1, 0), 0)),
                   cur(D_MODEL, 0), pl.BlockSpec((1, 128), lambda n: (0, 0))),
        scratch_shapes=[pltpu.VMEM((BLOCK, 2 * KV_W), F32)],
        compiler_params=_cp("arbitrary"),
    )(qr_b, kr_b, kr_b, proj, proj, proj, proj, d_y, p_all, *tabs, *tabs, after)


def _rnn_backward(proj, pos_col, h_rnn, saved, d_y, conv_w, rwa, rwx, lam):
    T = proj.shape[0]
    tr = min(T, 256)
    nt = T // tr
    hb = tr // 8

    def body(x0, x1, xh0, xh1, g0, g1, pos_ref, h_ref, hh_ref, xc_ref, r_ref, i_ref, a_ref, mult_ref, dy_ref,
             cw_ref, wa_ref, wx_ref, lam_ref, db_ref, dcw_ref, dcb_ref, dwa_ref, dwx_ref, dba_ref, dbx_ref, dlam_ref,
             xbuf, hbuf, dbuf, gbuf, lbuf, mu_carry, dxc_head):
        step = pl.program_id(0)
        first_tile = step == nt - 1

        @pl.when(step == 0)
        def _():
            mu_carry[...] = jnp.zeros_like(mu_carry)
            dxc_head[...] = jnp.zeros_like(dxc_head)
            for ref in (dcw_ref, dcb_ref, dwa_ref, dwx_ref, dba_ref, dbx_ref, dlam_ref):
                ref[...] = jnp.zeros_like(ref)

        xr = jnp.concatenate([x0[...], x1[...]], axis=1)
        tail = jnp.where(first_tile, 0.0, jnp.concatenate([xh0[...], xh1[...]], axis=1))
        lam_v = lam_ref[...]
        sp = _softplus(-lam_v)
        reset = pos_ref[...] == 0
        cw = cw_ref[...]
        xbuf[0:8, :] = tail
        xbuf[8:tr + 8, :] = xr
        g = jnp.concatenate([g0[...], g1[...]], axis=1)
        sg = _sigmoid(g)
        dy = dy_ref[...]
        h = h_ref[...]
        db_ref[:, D_MODEL:2 * D_MODEL] = (dy * h * (sg * (1.0 + g * (1.0 - sg)))).astype(BF16)
        gbuf[...] = dy * (g * sg)
        top = _scan_backward(a_ref, gbuf, lbuf, mu_carry[0:1, :], tr)
        mu_carry[...] = jnp.broadcast_to(top, mu_carry.shape)
        hbuf[0:8, :] = jnp.where(first_tile, 0.0, hh_ref[...])
        hbuf[8:tr + 8, :] = h
        live = jnp.logical_not(reset)
        dbuf[tr:tr + 8, :] = dxc_head[...]
        for j in range(RNN_BLOCKS):
            sl = slice(RNN_BW * j, RNN_BW * (j + 1))
            lam_t, h_prev = lbuf[:, sl], hbuf[pl.ds(7, tr), sl]
            xc, r, i, a, mult = xc_ref[:, sl], r_ref[:, sl], i_ref[:, sl], a_ref[:, sl], mult_ref[:, sl]
            d_a = jnp.where(live, lam_t * h_prev, 0.0)
            d_mult = jnp.where(live, lam_t * (i * xc), 0.0)
            d_ixc = lam_t * mult
            d_i = d_ixc * xc
            d_log_a = d_a * a - d_mult * (a * a / mult)
            d_za = d_log_a * (-LRU_C * sp[:, sl]) * (r * (1.0 - r))
            d_zx = d_i * (i * (1.0 - i))
            dlam_ref[:, sl] += jnp.sum(d_log_a * r, axis=0, keepdims=True) * (LRU_C * _sigmoid(-lam_v[:, sl]))
            dba_ref[:, sl] += jnp.sum(d_za, axis=0, keepdims=True)
            dbx_ref[:, sl] += jnp.sum(d_zx, axis=0, keepdims=True)
            xcb, dzab, dzxb = xc.astype(BF16), d_za.astype(BF16), d_zx.astype(BF16)
            dwa_ref[j] += _dot_tn(xcb, dzab)
            dwx_ref[j] += _dot_tn(xcb, dzxb)
            d_xc = d_ixc * i + (_dot_nt(dzab, wa_ref[j]) + _dot_nt(dzxb, wx_ref[j]))
            dcb_ref[:, sl] += jnp.sum(d_xc, axis=0, keepdims=True)
            for k in range(CONV_W):
                tap = xr[:, sl] if k == CONV_W - 1 else xbuf[pl.ds(8 - (CONV_W - 1 - k), tr), sl]
                dcw_ref[k:k + 1, sl] += jnp.sum(d_xc * tap, axis=0, keepdims=True)
            dbuf[0:tr, sl] = d_xc
            d_xr = d_xc * cw[CONV_W - 1:CONV_W, sl]
            for k in range(CONV_W - 1):
                d_xr = d_xr + dbuf[pl.ds(CONV_W - 1 - k, tr), sl] * cw[k:k + 1, sl]
            dxc_head[:, sl] = d_xc[0:8, :]
            db_ref[:, sl] = d_xr.astype(BF16)

    rev = lambda s: nt - 1 - s
    blk = lambda cb: pl.BlockSpec((tr, CB), lambda s, cb=cb: (rev(s), cb))
    halo = lambda w, cb: pl.BlockSpec((8, w), lambda s, cb=cb: (jnp.maximum(rev(s) * hb - 1, 0), cb))
    tok = lambda w: pl.BlockSpec((tr, w), lambda s: (rev(s), 0))
    row = lambda w: pl.BlockSpec((1, w), lambda s: (0, 0))
    full3 = pl.BlockSpec((RNN_BLOCKS, RNN_BW, RNN_BW), lambda s: (0, 0, 0))
    cwspec = pl.BlockSpec((CONV_W, D_MODEL), lambda s: (0, 0))
    vec = jax.ShapeDtypeStruct((1, D_MODEL), F32)
    gate_w = jax.ShapeDtypeStruct((RNN_BLOCKS, RNN_BW, RNN_BW), F32)
    out_shape = (jax.ShapeDtypeStruct((T, 2 * D_MODEL), BF16), jax.ShapeDtypeStruct((CONV_W, D_MODEL), F32), vec,
                 gate_w, gate_w, vec, vec, vec)
    big = lambda: pltpu.VMEM((tr, D_MODEL), F32)
    ext = lambda: pltpu.VMEM((tr + 8, D_MODEL), F32)
    return pl.pallas_call(
        body, out_shape=out_shape, grid=(nt,), name="rnn_backward",
        in_specs=[blk(CB_XR), blk(CB_XR + 1), halo(CB, CB_XR), halo(CB, CB_XR + 1), blk(CB_GR), blk(CB_GR + 1),
                  pl.BlockSpec((tr, 1), lambda s: (rev(s), 0)), tok(D_MODEL), halo(D_MODEL, 0)] + [tok(D_MODEL)] * 6
        + [cwspec, full3, full3, row(D_MODEL)],
        out_specs=(tok(2 * D_MODEL), cwspec, row(D_MODEL), full3, full3, row(D_MODEL), row(D_MODEL), row(D_MODEL)),
        scratch_shapes=[ext(), ext(), ext(), big(), big(), pltpu.VMEM((8, D_MODEL), F32), pltpu.VMEM((8, D_MODEL), F32)],
        compiler_params=_cp("arbitrary"),
    )(proj, proj, proj, proj, proj, proj, pos_col, h_rnn, h_rnn, *saved, d_y, *_in_hbm(conv_w, rwa, rwx, lam))


def _input_backward(pieces, w_in, x, dx2, mod_row, norm_g):
    T = x.shape[0]
    tm = min(T, 512)
    n = len(pieces)

    def body(*refs):
        d_refs = refs[:n]
        w_ref, x_ref, dx2_ref, mod_ref, g_ref, gx_ref, dshift_ref, dscale_ref, dg_ref = refs[n:]
        i = pl.program_id(0)
        dh = None
        for d_ref, (_, start, count) in zip(d_refs, pieces):
            part = _dot_nt(d_ref[...], w_ref[:, start * CB:(start + count) * CB])
            dh = part if dh is None else dh + part

        @pl.when(i == 0)
        def _():
            dshift_ref[...] = jnp.zeros_like(dshift_ref)
            dscale_ref[...] = jnp.zeros_like(dscale_ref)
            dg_ref[...] = jnp.zeros_like(dg_ref)

        xf = x_ref[...]
        r1 = _rms(xf)
        xn = xf * r1
        gn = g_ref[...]
        s1 = 1.0 + mod_ref[:, D_MODEL:2 * D_MODEL]
        dshift_ref[...] += jnp.sum(dh, axis=0, keepdims=True)
        dscale_ref[...] += jnp.sum(dh * (xn * gn), axis=0, keepdims=True)
        dg_ref[...] += jnp.sum(dh * s1 * xn, axis=0, keepdims=True)
        dxn = dh * s1 * gn
        gx_ref[...] = dx2_ref[...] + r1 * (dxn - xn * jnp.mean(dxn * xn, axis=-1, keepdims=True))

    tok = lambda w: pl.BlockSpec((tm, w), lambda i: (i, 0))
    row = lambda w: pl.BlockSpec((1, w), lambda i: (0, 0))
    vec = jax.ShapeDtypeStruct((1, D_MODEL), F32)
    return pl.pallas_call(
        body, out_shape=(jax.ShapeDtypeStruct((T, D_MODEL), F32), vec, vec, vec), grid=(T // tm,), name="input_backward",
        in_specs=[tok(c * CB) for _, _, c in pieces]
        + [pl.BlockSpec((D_MODEL, IN_W), lambda i: (0, 0), pipeline_mode=pl.Buffered(1)), tok(D_MODEL), tok(D_MODEL),
           row(ADA_W), row(D_MODEL)],
        out_specs=(tok(D_MODEL), row(D_MODEL), row(D_MODEL), row(D_MODEL)),
        compiler_params=_cp("arbitrary"),
    )(*[p[0] for p in pieces], w_in, x, dx2, *_in_hbm(mod_row, norm_g))


def _weight_grad(a, pieces, tag, a_is_transposed=False):
    M, T = a.shape if a_is_transposed else a.shape[::-1]
    n_blocks = sum(count for _, _, count in pieces)
    n = len(pieces)
    contract = _dot if a_is_transposed else _dot_tn

    def body(*refs):
        a_ref, b_refs, o_ref = refs[0], refs[1:1 + n], refs[-1]
        j = pl.program_id(0)
        for b_ref, (_, start, count) in zip(b_refs, pieces):
            @pl.when((j >= start) & (j < start + count))
            def _(b_ref=b_ref):
                o_ref[...] = contract(a_ref[...], b_ref[...])

    def piece_spec(start, count):
        return pl.BlockSpec((T, CB), lambda j: (0, jnp.clip(j - start, 0, count - 1)))

    return pl.pallas_call(
        body, out_shape=jax.ShapeDtypeStruct((M, n_blocks * CB), F32), grid=(n_blocks,), name=f"weight_grad_{tag}",
        in_specs=[pl.BlockSpec(a.shape, lambda j: (0, 0), pipeline_mode=pl.Buffered(1))] + [piece_spec(s, c) for _, s, c in pieces],
        out_specs=pl.BlockSpec((M, CB), lambda j: (0, j)), compiler_params=_cp("arbitrary"),
    )(a, *[p[0] for p in pieces])


def _adamw(w, g, m, v):
    m = ADAM_B1 * m + (1.0 - ADAM_B1) * g
    v = ADAM_B2 * v + (1.0 - ADAM_B2) * (g * g)
    m_hat = m / (1.0 - ADAM_B1 ** ADAM_STEP)
    v_hat = v / (1.0 - ADAM_B2 ** ADAM_STEP)
    delta = -ADAM_LR * (m_hat / (jnp.sqrt(v_hat) + ADAM_EPS) + ADAM_WD * w)
    return delta, m, v


def _sum_landed(kind, owns, lands, where, tag):
    n = len(owns)
    land = lands[0]
    if kind == "in":
        R, C = land.shape[1:]
        tr = 256
        grid = (R // tr,)
        own_spec = pl.BlockSpec((tr, C), lambda i, w: (i, w[0]))
        land_spec = pl.BlockSpec((3, tr, C), lambda i, w: (0, i, 0))
        out_spec = pl.BlockSpec((1, tr, C), lambda i, w: (w[1], i, 0))
        out_shape = (2, R, C)
        pick = lambda ref: ref[...]
    elif kind == "sq":
        R, C = land.shape[1:]
        grid = (1,)
        own_spec = pl.BlockSpec((1, R, C), lambda i, w: (w[0], 0, 0))
        land_spec = pl.BlockSpec((3, R, C), lambda i, w: (0, 0, 0))
        out_spec = pl.BlockSpec((1, R, C), lambda i, w: (w[1], 0, 0))
        out_shape = (2, R, C)
        pick = lambda ref: ref[0]
    else:
        B, R, C = land.shape[1:]
        grid = (1,)
        own_spec = pl.BlockSpec((B, 1, R, C), lambda i, w: (0, w[0], 0, 0))
        land_spec = pl.BlockSpec((3, B, R, C), lambda i, w: (0, 0, 0, 0))
        out_spec = pl.BlockSpec((B, 1, R, C), lambda i, w: (0, w[1], 0, 0))
        out_shape = (B, 2, R, C)
        pick = lambda ref: ref[:, 0]

    def body(w_ref, *refs):
        for k in range(n):
            own_ref, l_ref, o_ref = refs[k], refs[n + k], refs[2 * n + k]
            total = ((pick(own_ref) + l_ref[0].astype(F32)) + l_ref[1].astype(F32)) + l_ref[2].astype(F32)
            if kind == "rg":
                o_ref[:, 0] = total
            else:
                o_ref[0] = total

    grid_spec = pltpu.PrefetchScalarGridSpec(num_scalar_prefetch=1, grid=grid, in_specs=[own_spec] * n + [land_spec] * n,
                                             out_specs=(out_spec,) * n)
    return list(pl.pallas_call(
        body, out_shape=(jax.ShapeDtypeStruct(out_shape, F32),) * n, grid_spec=grid_spec, name=f"sum_landed_{tag}",
        compiler_params=_cp("parallel"),
    )(where, *owns, *lands))


def _adamw_shard(gs, ws, ms, vs, tag):
    n = len(ws)
    R, C = ws[0].shape
    tr = min(R, 256 if n == 1 else 64)

    def body(*refs):
        for k in range(n):
            g = refs[k][...]
            d, nm, nv = _adamw(refs[n + k][...], g, refs[2 * n + k][...], refs[3 * n + k][...])
            out = refs[4 * n + 4 * k:4 * n + 4 * k + 4]
            out[0][...] = g
            out[1][...] = d
            out[2][...] = nm
            out[3][...] = nv

    spec = pl.BlockSpec((tr, C), lambda i: (i, 0))
    sds = jax.ShapeDtypeStruct((R, C), F32)
    outs = pl.pallas_call(
        body, out_shape=(sds,) * (4 * n), grid=(R // tr,), name=f"adamw_{tag}",
        in_specs=[spec] * (4 * n), out_specs=(spec,) * (4 * n), compiler_params=_cp("parallel"),
    )(*gs, *_in_hbm(*ws, *ms, *vs))
    return [outs[4 * k:4 * k + 4] for k in range(n)]


def _adamw_w_ada(c_t, dmod_cols, w, m, v):
    R, C = w.shape

    def body(ct_ref, dm_ref, w_ref, m_ref, v_ref, g_ref, d_ref, nm_ref, nv_ref):
        g = _dot(ct_ref[...].astype(BF16), dm_ref[...].astype(BF16))
        d, nm, nv = _adamw(w_ref[...], g, m_ref[...], v_ref[...])
        g_ref[...] = g
        d_ref[...] = d
        nm_ref[...] = nm
        nv_ref[...] = nv

    tr = 256
    spec = pl.BlockSpec((tr, C), lambda i: (i, 0))
    sds = jax.ShapeDtypeStruct((R, C), F32)
    return pl.pallas_call(
        body, out_shape=(sds,) * 4, grid=(R // tr,), name="adamw_w_ada",
        in_specs=[pl.BlockSpec((tr, 128), lambda i: (i, 0)), pl.BlockSpec((128, C), lambda i: (0, 0))] + [spec] * 3,
        out_specs=(spec,) * 4, compiler_params=_cp("parallel"),
    )(*_in_hbm(c_t, dmod_cols), w, m, v)


def _adamw_small(small_all, ws, ms, vs):
    def body(s_ref, w_ref, m_ref, v_ref, g_ref, d_ref, nm_ref, nv_ref):
        g = s_ref[0]
        for b in range(1, N_DEV):
            g = g + s_ref[b]
        d, nm, nv = _adamw(w_ref[...], g, m_ref[...], v_ref[...])
        g_ref[...] = g
        d_ref[...] = d
        nm_ref[...] = nm
        nv_ref[...] = nv

    sds = jax.ShapeDtypeStruct((SMALL_ROWS, D_MODEL), F32)
    spec = pl.BlockSpec((SMALL_ROWS, D_MODEL), lambda i: (0, 0))
    return pl.pallas_call(
        body, out_shape=(sds,) * 4, grid=(1,), name="adamw_small",
        in_specs=[pl.BlockSpec((N_DEV, SMALL_ROWS, D_MODEL), lambda i: (0, 0, 0))] + [spec] * 3, out_specs=(spec,) * 4,
        compiler_params=_cp("arbitrary"),
    )(*_in_hbm(small_all, ws, ms, vs))


ROW_MOD, ROW_NORM_G, ROW_CONV_B, ROW_BA, ROW_BX, ROW_LAM, ROW_FINAL_G, ROW_SINKS, ROW_CONV_W, ROW_LOSS = 0, 3, 4, 5, 6, 7, 8, 9, 10, 14


def _pack_small(b_ada, norm_g, conv_b, ba, bx, lam, final_g, sinks, conv_w_full, loss_row=None):
    lane_pad = lambda a: jnp.pad(a.reshape(1, -1), ((0, 0), (0, D_MODEL - a.size)))
    rows = [b_ada.reshape(3, D_MODEL), norm_g, conv_b, ba, bx, lam, final_g.reshape(1, D_MODEL), lane_pad(sinks), conv_w_full,
            jnp.zeros((1, D_MODEL), F32) if loss_row is None else lane_pad(loss_row),
            jnp.zeros((SMALL_ROWS - ROW_LOSS - 1, D_MODEL), F32)]
    return jnp.concatenate([r.astype(F32) for r in rows], axis=0)


def kernel(x, c, positions, w_ada, b_ada, norm_g, w_in, attn_sinks, conv_w, conv_b, rg_wa, rg_ba, rg_wx, rg_bx, rg_lambda, w_attn_proj, w_rnn_proj, w_out, final_g, loss_target, m_w_ada, m_b_ada, m_norm_g, m_w_in, m_attn_sinks, m_conv_w, m_conv_b, m_rg_wa, m_rg_ba, m_rg_wx, m_rg_bx, m_rg_lambda, m_w_attn_proj, m_w_rnn_proj, m_w_out, m_final_g, v_w_ada, v_b_ada, v_norm_g, v_w_in, v_attn_sinks, v_conv_w, v_conv_b, v_rg_wa, v_rg_ba, v_rg_wx, v_rg_bx, v_rg_lambda, v_w_attn_proj, v_w_rnn_proj, v_w_out, v_final_g):
    T = x.shape[1]
    my_chip = lax.axis_index("x") * 2 + lax.axis_index("y")
    my_dev = my_chip * 2 + lax.axis_index("c")
    x2d, tgt = x[0], loss_target[0]
    pos_col = positions.reshape(T, 1)

    chip_idx = my_chip.reshape(1).astype(jnp.int32)
    c_idx = lax.axis_index("c").reshape(1).astype(jnp.int32)
    sq_place = ((D_MODEL, D_MODEL), (SHARD_ROWS, D_MODEL), lambda chip: (chip, 0))
    rg_place = ((RNN_BLOCKS, RNN_BW, RNN_BW), (RNN_BLOCKS, SHARD_RG, RNN_BW), lambda chip: (0, chip, 0))
    in_place = ((D_MODEL, IN_W), (D_MODEL, SHARD_IN), lambda chip: (0, chip))
    placed = _cast_place([w_in[0], w_attn_proj[0], w_rnn_proj[0], w_out[0], rg_wa[0], rg_wx[0]], chip_idx,
                         [in_place, sq_place, sq_place, sq_place, rg_place, rg_place])
    cw_chips, c_all, mod_chips = _gather_mod(c.reshape(1, 1, D_MODEL), w_ada[0], conv_w[0])
    g_ssems, g_rsems, fulls, g_token = _gather_start([p.reshape(s) for p, s in zip(placed, FULL_SHAPES)], mod_chips)
    conv_w_f = jnp.transpose(cw_chips, (1, 0, 2)).reshape(CONV_W, D_MODEL)
    mod_all = jnp.transpose(mod_chips, (1, 0, 2)).reshape(N_DEV, ADA_W) + b_ada
    mod_row = lax.dynamic_slice_in_dim(mod_all, my_dev, 1, axis=0) + g_token[0:1, 0:1]

    h, h_t, tabs = _prenorm(x2d, mod_row, norm_g, pos_col)
    w_in_v = fulls[0]
    proj = _in_projection(h, w_in_v.reshape(D_MODEL, IN_W), chip_idx, None, "own")
    for k, mask in enumerate(CHIP_MASKS):
        w_in_v = _gather_wait(g_ssems[k], g_rsems[k], [w_in_v], [0], proj, f"w_in_{k}")[0]
        w_in_v = _forward_halves([w_in_v], [(0, 0, k)], f"w_in_{k}")[0]
        from_chip = (chip_idx ^ (mask >> 1)).astype(jnp.int32)
        proj = _in_projection(h, w_in_v.reshape(D_MODEL, IN_W), from_chip, proj, f"from_{k}")
    w_in_f = w_in_v.reshape(D_MODEL, IN_W)
    rest = _gather_wait(g_ssems[3], g_rsems[3], list(fulls[1:]), [1, 2, 3, 4, 5], proj, "rest")
    rest = _forward_halves(rest, [(idx - 1, idx, k) for idx in range(1, N_BIG) for k in range(3)], "rest")
    wap_f, wrp_f, wo_f = (g.reshape(D_MODEL, D_MODEL) for g in rest[0:3])
    rwa_f, rwx_f = (g.reshape(RNN_BLOCKS, RNN_BW, RNN_BW) for g in rest[3:5])
    y_attn, qr_b, kr_b, p_all = _attn_forward(proj, tabs, attn_sinks)
    y_rnn, h_rnn, *rnn_saved = _rnn_forward(proj, pos_col, conv_w_f, conv_b, rwa_f, rwx_f, rg_ba, rg_bx, rg_lambda)
    (dx2, merged, d_o, d_pa, d_pr, d_ya, d_yr, d_c, d_final_g, d_gate, loss_vec) = _merge_and_head(
        x2d, tgt, y_attn, y_rnn, proj, wap_f, wrp_f, wo_f, mod_row, final_g.reshape(1, D_MODEL))

    sq = (N_CHIPS, 2, SHARD_ROWS // 2, D_MODEL)
    rg = (RNN_BLOCKS, N_CHIPS, 2, SHARD_RG // 2, RNN_BW)
    rg_flat = (RNN_BLOCKS * N_CHIPS, 2, SHARD_RG // 2, RNN_BW)

    def chip_sum_and_start(views, axes, flat, unflat, tags_, kinds_, group, from_sib=None):
        if from_sib is None:
            from_sib = _swap_halves(views, axes)
        exact, rounded = [None] * len(views), [None] * len(views)
        for shape in dict.fromkeys(flat):
            ids = [k for k, f in enumerate(flat) if f == shape]
            ex, ro = _presum([views[k].reshape(shape) for k in ids],
                             [from_sib[k].reshape(shape[:1] + shape[2:]) for k in ids], c_idx, tags_[ids[0]])
            for k, e, r in zip(ids, ex, ro):
                exact[k], rounded[k] = e.reshape(unflat[k]), r.reshape(unflat[k])
        return _exchange_start(rounded, kinds_, group), exact

    g_ap = _weight_grad(y_attn, [(d_pa, 0, 2)], "w_attn_proj")
    g_rp = _weight_grad(y_rnn, [(d_pr, 0, 2)], "w_rnn_proj")
    g_o = _weight_grad(merged, [(d_o, 0, 2)], "w_out")
    sq_half = (N_CHIPS, SHARD_ROWS // 2, D_MODEL)
    views1 = [g_ap.reshape(sq), g_rp.reshape(sq), g_o.reshape(sq)]
    sw_ssems, sw_rsems, views1, sib1, sw_token = _swap_halves_start(views1, [1, 1, 1], "proj")
    d_q, d_kv, d_ga, d_sinks = _attn_backward(proj, qr_b, kr_b, p_all, d_ya, tabs, sw_token[0:1, 0:16])
    views1, sib1 = _swap_halves_wait(sw_ssems, sw_rsems, views1, sib1, d_q, "proj")
    started1, own1 = chip_sum_and_start(views1, [1, 1, 1], [sq] * 3, [sq_half] * 3,
                                        ["w_attn_proj", "w_rnn_proj", "w_out"], ["sq"] * 3, "proj", from_sib=sib1)
    d_b, d_conv_w, d_conv_b, d_rwa, d_rwx, d_ba, d_bx, d_lam = _rnn_backward(
        proj, pos_col, h_rnn, rnn_saved, d_yr, conv_w_f, rwa_f, rwx_f, rg_lambda + started1[4][0:1, 0:1])
    pieces = [(d_q, CB_Q, 2), (d_kv, CB_KV, 1), (d_ga, CB_GA, 2), (d_b, CB_XR, 4), (d_c, CB_MA, 4)]
    g_in = _weight_grad(h_t, pieces, "w_in", a_is_transposed=True)
    started2, own2 = chip_sum_and_start(
        [g_in.reshape(2, D_MODEL // 2, IN_W), d_rwa.reshape(rg), d_rwx.reshape(rg)], [0, 2, 2],
        [(1, 2, D_MODEL // 2, IN_W), rg_flat, rg_flat],
        [(D_MODEL // 2, IN_W), (RNN_BLOCKS, N_CHIPS, SHARD_RG // 2, RNN_BW), (RNN_BLOCKS, N_CHIPS, SHARD_RG // 2, RNN_BW)],
        ["w_in", "rg_wa", "rg_wx"], ["in", "rg", "rg"], "in")
    grad_x, d_shift, d_scale, d_norm_g = _input_backward(pieces, w_in_f, x2d, dx2, mod_row + started2[4][0, 0], norm_g)

    d_mod = jnp.concatenate([d_shift, d_scale, d_gate], axis=1)
    small = _pack_small(d_mod, d_norm_g, d_conv_b, d_ba, d_bx, d_lam, d_final_g, d_sinks[:, :N_HEADS], d_conv_w, loss_vec)
    slabs = lax.dynamic_update_slice(jnp.zeros((N_DEV, SMALL_ROWS, D_MODEL), F32), small[None], (my_dev, 0, 0))
    gs_ssem, gs_rsem, slabs, gs_token = _gather_small_start(slabs)
    _, lands1 = _exchange_wait(*started1[:4], gs_token, "proj")
    _, lands2 = _exchange_wait(*started2[:4], gs_token, "in")
    tags = ["w_in", "w_attn_proj", "w_rnn_proj", "w_out", "rg_wa", "rg_wx"]
    chip_sums = [own2[0]] + list(own1) + list(own2[1:])
    lands = [lands2[0]] + list(lands1) + list(lands2[1:])
    where = jnp.concatenate([chip_idx, c_idx])
    kinds = ["in", "sq", "sq", "sq", "rg", "rg"]
    groups = [[0], [1, 2, 3], [4, 5]]
    halves = [None] * 6
    for ids in groups:
        for i, half in zip(ids, _sum_landed(kinds[ids[0]], [chip_sums[i] for i in ids], [lands[i] for i in ids], where,
                                            tags[ids[0]])):
            halves[i] = half
    half_axes = [0, 0, 0, 0, 1, 1]
    asm_ssems, asm_rsems, halves, asm_token = _assemble_start(halves, half_axes)
    res = {}
    small_all = _gather_small_wait(gs_ssem, gs_rsem, slabs, asm_token)
    dmod_all = small_all[:, ROW_MOD:ROW_MOD + 3, :].reshape(N_DEV, ADA_W)
    dmod_cols = lax.dynamic_slice_in_dim(dmod_all, my_chip * SHARD_ADA, SHARD_ADA, axis=1)
    c_t = jnp.pad(jnp.transpose(c_all.reshape(N_DEV, D_MODEL)), ((0, 0), (0, 128 - N_DEV)))
    dmod_cols = jnp.pad(dmod_cols, ((0, 128 - N_DEV), (0, 0)))
    res["w_ada"] = [o.reshape(w_ada.shape) for o in _adamw_w_ada(c_t, dmod_cols, w_ada[0], m_w_ada[0], v_w_ada[0])]

    def full_conv(a):
        return lax.dynamic_update_slice_in_dim(jnp.zeros((CONV_W, D_MODEL), F32), a[0], my_chip * (D_MODEL // N_CHIPS), axis=1)

    packed = [_pack_small(p[0], p[1], p[2], p[3], p[4], p[5], p[6], p[7], full_conv(p[8])) for p in (
        (b_ada, norm_g, conv_b, rg_ba, rg_bx, rg_lambda, final_g, attn_sinks, conv_w),
        (m_b_ada, m_norm_g, m_conv_b, m_rg_ba, m_rg_bx, m_rg_lambda, m_final_g, m_attn_sinks, m_conv_w),
        (v_b_ada, v_norm_g, v_conv_b, v_rg_ba, v_rg_bx, v_rg_lambda, v_final_g, v_attn_sinks, v_conv_w))]
    small_out = _adamw_small(small_all, *packed)

    grads = _assemble_wait(asm_ssems, asm_rsems, halves, half_axes, small_out[0])
    shapes2d = [(D_MODEL, SHARD_IN), (SHARD_ROWS, D_MODEL), (SHARD_ROWS, D_MODEL), (SHARD_ROWS, D_MODEL),
                (RNN_BLOCKS * SHARD_RG, RNN_BW), (RNN_BLOCKS * SHARD_RG, RNN_BW)]
    big_w = [w_in, w_attn_proj, w_rnn_proj, w_out, rg_wa, rg_wx]
    big_m = [m_w_in, m_w_attn_proj, m_w_rnn_proj, m_w_out, m_rg_wa, m_rg_wx]
    big_v = [v_w_in, v_w_attn_proj, v_w_rnn_proj, v_w_out, v_rg_wa, v_rg_wx]
    for ids in groups:
        flat2d = lambda arrs: [arrs[i].reshape(shapes2d[i]) for i in ids]
        outs = _adamw_shard(flat2d(grads), flat2d(big_w), flat2d(big_m), flat2d(big_v), tags[ids[0]])
        for i, four in zip(ids, outs):
            res[tags[i]] = [o.reshape(big_w[i].shape) for o in four]

    def unpack(slab):
        cw = lax.dynamic_slice_in_dim(slab[ROW_CONV_W:ROW_CONV_W + CONV_W], my_chip * (D_MODEL // N_CHIPS),
                                      D_MODEL // N_CHIPS, axis=1)
        return {
            "b_ada": slab[ROW_MOD:ROW_MOD + 3].reshape(1, ADA_W), "norm_g": slab[ROW_NORM_G:ROW_NORM_G + 1],
            "conv_b": slab[ROW_CONV_B:ROW_CONV_B + 1], "rg_ba": slab[ROW_BA:ROW_BA + 1], "rg_bx": slab[ROW_BX:ROW_BX + 1],
            "rg_lambda": slab[ROW_LAM:ROW_LAM + 1], "final_g": slab[ROW_FINAL_G], "attn_sinks": slab[ROW_SINKS:ROW_SINKS + 1, :N_HEADS],
            "conv_w": cw[None],
        }

    small_res = [unpack(s) for s in small_out]
    order = ["w_ada", "b_ada", "norm_g", "w_in", "attn_sinks", "conv_w", "conv_b", "rg_wa", "rg_ba", "rg_wx", "rg_bx",
             "rg_lambda", "w_attn_proj", "w_rnn_proj", "w_out", "final_g"]
    loss = small_out[0][ROW_LOSS, 0]
    outs = [loss, grad_x[None]]
    for kind in range(4):
        for name in order:
            outs.append(res[name][kind] if name in res else small_res[kind][name])
    return tuple(outs)
```

```python
import numpy as np
import jax
import jax.numpy as jnp
from jax import lax
from jax.experimental import pallas as pl
from jax.experimental.pallas import tpu as pltpu

F32 = jnp.float32
BF16 = jnp.bfloat16

D_MODEL = 1024
N_HEADS = 16
N_KV = 4
HEAD_DIM = 64
GROUP = N_HEADS // N_KV
BLOCK = 128
KV_W = N_KV * HEAD_DIM
ROT_HALF = 8
ROPE_THETA = 500000.0
ATTN_SCALE = 0.125
RNN_BLOCKS = 4
RNN_BW = 256
CONV_W = 4
LRU_C = 8.0
NORM_EPS = 1e-6
IN_W = 6656
CB = 512
N_CB = IN_W // CB
CB_Q, CB_KV, CB_GA, CB_XR, CB_GR, CB_MA, CB_MR = 0, 2, 3, 5, 7, 9, 11
V_COL_BLOCK = 5
N_CHIPS = 4
N_DEV = 8
SHARD_IN = IN_W // N_CHIPS
SHARD_ROWS = D_MODEL // N_CHIPS
SHARD_RG = RNN_BW // N_CHIPS
ADA_W = 3 * D_MODEL
SHARD_ADA = ADA_W // N_CHIPS
SMALL_ROWS = 16

ADAM_LR = 0.001
ADAM_B1 = 0.9
ADAM_B2 = 0.999
ADAM_EPS = 1e-08
ADAM_WD = 0.01
ADAM_STEP = 10

VMEM_LIMIT_V7X = 52 * 1024 * 1024
MESH = pl.DeviceIdType.MESH
ANY = pl.BlockSpec(memory_space=pl.ANY)
VMEM_SPEC = pl.BlockSpec(memory_space=pltpu.VMEM)


def _in_hbm(*arrays):
    return [pltpu.with_memory_space_constraint(a, pltpu.HBM) for a in arrays]


def _cp(*sem):
    return pltpu.CompilerParams(dimension_semantics=sem if sem else None, vmem_limit_bytes=VMEM_LIMIT_V7X)


def _dot(a, b):
    return jnp.dot(a, b, preferred_element_type=F32)


def _dot_nt(a, b):
    return lax.dot_general(a, b, (((1,), (1,)), ((), ())), preferred_element_type=F32)


def _dot_tn(a, b):
    return lax.dot_general(a, b, (((0,), (0,)), ((), ())), preferred_element_type=F32)


def _sigmoid(z):
    return 1.0 / (1.0 + jnp.exp(-z))


def _softplus(z):
    u = jnp.exp(-jnp.abs(z))
    log1p_u = jnp.where(u < 1e-3, u * (1.0 - u * (0.5 - u * (1.0 / 3.0))), jnp.log(1.0 + u))
    return jnp.maximum(z, 0.0) + log1p_u


def _rms(xf):
    return lax.rsqrt(jnp.mean(xf * xf, axis=-1, keepdims=True) + NORM_EPS)


def _me():
    return lax.axis_index("x"), lax.axis_index("y"), lax.axis_index("c")


def _peer(mask):
    x, y, c = _me()
    fx, fy, fc = (mask >> 2) & 1, (mask >> 1) & 1, mask & 1
    return (x ^ fx if fx else x, y ^ fy if fy else y, c ^ fc if fc else c)


def _chip_of(pos):
    return pos[0] * 2 + pos[1]


SIBLING_COLLECTIVE_ID = 0
SIBLING_ONLY = pltpu.CompilerParams(collective_id=SIBLING_COLLECTIVE_ID)


def _sibling_handshake():
    barrier = pltpu.get_barrier_semaphore()
    pl.semaphore_signal(barrier, inc=1, device_id=_peer(1), device_id_type=MESH)
    pl.semaphore_wait(barrier, 1)


CHIP_MASKS = (4, 2, 6)
ALL_MASKS = (1, 2, 3, 4, 5, 6, 7)


HBM_SPEC = pl.BlockSpec(memory_space=pltpu.HBM)
SEM_SPEC = pl.BlockSpec(memory_space=pltpu.SEMAPHORE)
SPLIT_COPY = pltpu.CompilerParams(has_side_effects=pltpu.SideEffectType.DATAFLOW_SIDE_EFFECTING)
N_BIG = 6
FULL_SHAPES = (
    (2, D_MODEL // 2, IN_W),
    (N_CHIPS, 2, SHARD_ROWS // 2, D_MODEL), (N_CHIPS, 2, SHARD_ROWS // 2, D_MODEL), (N_CHIPS, 2, SHARD_ROWS // 2, D_MODEL),
    (RNN_BLOCKS, N_CHIPS, 2, SHARD_RG // 2, RNN_BW), (RNN_BLOCKS, N_CHIPS, 2, SHARD_RG // 2, RNN_BW),
)


def _slot(full, idx, chip, half):
    if idx == 0:
        return full.at[half, :, pl.ds(pl.multiple_of(chip * SHARD_IN, 128), SHARD_IN)]
    return full.at[chip, half] if idx in (1, 2, 3) else full.at[:, chip, half]


def _three_halves(full, idx):
    return full.at[pl.ds(0, 3), 0] if idx in (1, 2, 3) else full.at[:, pl.ds(0, 3), 0]


def _gather_start(fulls, after):
    def body(*refs):
        full_refs = refs[:N_BIG]
        ssems, rsems = refs[N_BIG + 1:N_BIG + 5], refs[N_BIG + 5:N_BIG + 9]
        token = refs[2 * N_BIG + 9]
        me = _me()
        my_chip = _chip_of(me)
        for idx in range(N_BIG):
            for k, mask in enumerate(CHIP_MASKS):
                pair = k if idx == 0 else 3
                mine = _slot(full_refs[idx], idx, my_chip, me[2])
                pltpu.make_async_remote_copy(src_ref=mine, dst_ref=mine, send_sem=ssems[pair], recv_sem=rsems[pair],
                                             device_id=_peer(mask), device_id_type=MESH).start()
        token[...] = jnp.zeros_like(token)

    sem = pltpu.SemaphoreType.DMA(())
    out_shape = (sem,) * 8 + tuple(pltpu.HBM(f.shape, f.dtype) for f in fulls) + (jax.ShapeDtypeStruct((8, 128), F32),)
    outs = pl.pallas_call(
        body, out_shape=out_shape, name="gather_start",
        in_specs=[HBM_SPEC] * N_BIG + [ANY], out_specs=tuple([SEM_SPEC] * 8 + [HBM_SPEC] * N_BIG + [VMEM_SPEC]),
        input_output_aliases={i: 8 + i for i in range(N_BIG)}, compiler_params=SPLIT_COPY,
    )(*[pltpu.with_memory_space_constraint(f, pltpu.HBM) for f in fulls], after)
    return outs[0:4], outs[4:8], outs[8:8 + N_BIG], outs[8 + N_BIG]


def _gather_wait(ssem, rsem, arrays, idxs, after, tag):
    n = len(arrays)

    def body(*refs):
        full_refs, ssem_ref, rsem_ref = refs[:n], refs[n], refs[n + 1]
        me = _me()
        for full, idx in zip(full_refs, idxs):
            region = _slot(full, 0, _chip_of(me), me[2]) if idx == 0 else _three_halves(full, idx)
            arrived = pltpu.make_async_remote_copy(
                src_ref=region, dst_ref=region, send_sem=ssem_ref, recv_sem=rsem_ref, device_id=me, device_id_type=MESH)
            arrived.wait_send()
            arrived.wait_recv()

    outs = pl.pallas_call(
        body, out_shape=tuple(pltpu.HBM(a.shape, a.dtype) for a in arrays), name=f"gather_wait_{tag}",
        in_specs=[HBM_SPEC] * n + [SEM_SPEC, SEM_SPEC, ANY], out_specs=tuple([HBM_SPEC] * n),
        input_output_aliases={i: i for i in range(n)}, compiler_params=SPLIT_COPY,
    )(*arrays, ssem, rsem, after)
    return list(outs)


def _forward_halves(arrays, items, tag):
    n, m = len(arrays), len(items)

    def body(*refs):
        outs, ssem, rsem = refs[n:2 * n], refs[2 * n], refs[2 * n + 1]
        me = _me()
        sib = _peer(1)
        _sibling_handshake()
        cps = []
        for j, (pos, idx, k) in enumerate(items):
            chip = _chip_of(_peer(CHIP_MASKS[k]))
            cp = pltpu.make_async_remote_copy(
                src_ref=_slot(outs[pos], idx, chip, me[2]), dst_ref=_slot(outs[pos], idx, chip, me[2]),
                send_sem=ssem.at[j], recv_sem=rsem.at[j], device_id=sib, device_id_type=MESH)
            cp.start()
            cps.append(cp)
        for j, (pos, idx, k) in enumerate(items):
            chip = _chip_of(_peer(CHIP_MASKS[k]))
            pltpu.make_async_remote_copy(
                src_ref=_slot(outs[pos], idx, chip, me[2]), dst_ref=_slot(outs[pos], idx, chip, 1 - me[2]),
                send_sem=ssem.at[j], recv_sem=rsem.at[j], device_id=sib, device_id_type=MESH).wait_recv()
        for cp in cps:
            cp.wait_send()

    outs = pl.pallas_call(
        body, out_shape=tuple(jax.ShapeDtypeStruct(a.shape, a.dtype) for a in arrays), name=f"forward_halves_{tag}",
        in_specs=[ANY] * n, out_specs=tuple([ANY] * n), input_output_aliases={i: i for i in range(n)},
        scratch_shapes=[pltpu.SemaphoreType.DMA((m,)), pltpu.SemaphoreType.DMA((m,))], compiler_params=SIBLING_ONLY,
    )(*arrays)
    return list(outs)


def _forward_rest_start(arrays):
    n = len(arrays)

    def body(*refs):
        ssem, rsem, outs, token = refs[n], refs[n + 1], refs[n + 2:2 * n + 2], refs[2 * n + 2]
        me = _me()
        sib = _peer(1)
        for pos in range(n):
            for mask in CHIP_MASKS:
                mine = _slot(outs[pos], pos + 1, _chip_of(_peer(mask)), me[2])
                pltpu.make_async_remote_copy(src_ref=mine, dst_ref=mine, send_sem=ssem, recv_sem=rsem,
                                             device_id=sib, device_id_type=MESH).start()
        token[...] = jnp.zeros_like(token)

    sem = pltpu.SemaphoreType.DMA(())
    out_shape = (sem, sem) + tuple(pltpu.HBM(a.shape, a.dtype) for a in arrays) + (jax.ShapeDtypeStruct((8, 128), F32),)
    outs = pl.pallas_call(
        body, out_shape=out_shape, name="forward_rest_start",
        in_specs=[HBM_SPEC] * n, out_specs=tuple([SEM_SPEC] * 2 + [HBM_SPEC] * n + [VMEM_SPEC]),
        input_output_aliases={i: 2 + i for i in range(n)}, compiler_params=SPLIT_COPY,
    )(*arrays)
    return outs[0], outs[1], list(outs[2:2 + n]), outs[2 + n]


def _gather_mod(c_row, w_ada_s, conv_w_s):
    def body(c_ref, wada_ref, cw_s, cw_f, call_ref, mod_ref, wsend, wrecv, lsem, csend, crecv, msend, mrecv):
        me = _me()
        my_chip = _chip_of(me)
        my_dev = my_chip * 2 + me[2]
        sends = []
        for k, mask in enumerate(CHIP_MASKS):
            cp = pltpu.make_async_remote_copy(src_ref=cw_s, dst_ref=cw_f.at[my_chip], send_sem=wsend.at[k], recv_sem=wrecv.at[k],
                                              device_id=_peer(mask), device_id_type=MESH)
            cp.start()
            sends.append(cp)
        local = [pltpu.make_async_copy(cw_s, cw_f.at[my_chip], lsem.at[0])]
        for cp in local:
            cp.start()

        call_ref[my_dev] = c_ref[0]
        csends = []
        for k, mask in enumerate(ALL_MASKS):
            cp = pltpu.make_async_remote_copy(
                src_ref=c_ref.at[0], dst_ref=call_ref.at[my_dev],
                send_sem=csend.at[k], recv_sem=crecv.at[k], device_id=_peer(mask), device_id_type=MESH)
            cp.start()
            csends.append(cp)
        for k, mask in enumerate(ALL_MASKS):
            frm = _peer(mask)
            pltpu.make_async_remote_copy(
                src_ref=c_ref.at[0], dst_ref=call_ref.at[_chip_of(frm) * 2 + frm[2]],
                send_sem=csend.at[k], recv_sem=crecv.at[k], device_id=frm, device_id_type=MESH).wait_recv()
        for cp in csends:
            cp.wait_send()

        c_all = call_ref[...].reshape(N_DEV, D_MODEL).astype(BF16)
        mod_ref[my_chip] = _dot(c_all, wada_ref[...].astype(BF16))
        msends = []
        for k, mask in enumerate(CHIP_MASKS):
            cp = pltpu.make_async_remote_copy(
                src_ref=mod_ref.at[my_chip], dst_ref=mod_ref.at[my_chip],
                send_sem=msend.at[k], recv_sem=mrecv.at[k], device_id=_peer(mask), device_id_type=MESH)
            cp.start()
            msends.append(cp)
        for k, mask in enumerate(CHIP_MASKS):
            frm = _peer(mask)
            pltpu.make_async_remote_copy(
                src_ref=mod_ref.at[my_chip], dst_ref=mod_ref.at[_chip_of(frm)],
                send_sem=msend.at[k], recv_sem=mrecv.at[k], device_id=frm, device_id_type=MESH).wait_recv()
        for cp in msends:
            cp.wait_send()

        for k, mask in enumerate(CHIP_MASKS):
            frm = _peer(mask)
            pltpu.make_async_remote_copy(src_ref=cw_s, dst_ref=cw_f.at[_chip_of(frm)], send_sem=wsend.at[k], recv_sem=wrecv.at[k],
                                         device_id=frm, device_id_type=MESH).wait_recv()
        for cp in sends:
            cp.wait_send()
        for cp in local:
            cp.wait()

    out_shape = (
        jax.ShapeDtypeStruct((N_CHIPS, CONV_W, D_MODEL // N_CHIPS), F32),
        jax.ShapeDtypeStruct((N_DEV, 1, D_MODEL), F32),
        jax.ShapeDtypeStruct((N_CHIPS, N_DEV, SHARD_ADA), F32),
    )
    return pl.pallas_call(
        body, out_shape=out_shape, name="gather_mod",
        in_specs=[VMEM_SPEC, VMEM_SPEC, ANY], out_specs=(ANY, VMEM_SPEC, VMEM_SPEC),
        scratch_shapes=[
            pltpu.SemaphoreType.DMA((3,)), pltpu.SemaphoreType.DMA((3,)), pltpu.SemaphoreType.DMA((1,)),
            pltpu.SemaphoreType.DMA((7,)), pltpu.SemaphoreType.DMA((7,)),
            pltpu.SemaphoreType.DMA((3,)), pltpu.SemaphoreType.DMA((3,)),
        ],
        compiler_params=pltpu.CompilerParams(vmem_limit_bytes=VMEM_LIMIT_V7X),
    )(c_row, w_ada_s, conv_w_s)


def _cast_place(shards, chip_idx, places):
    n = len(shards)

    def body(chip_ref, *refs):
        for s_ref, o_ref in zip(refs[:n], refs[n:]):
            o_ref[...] = s_ref[...].astype(BF16)

    grid_spec = pltpu.PrefetchScalarGridSpec(
        num_scalar_prefetch=1, grid=(1,),
        in_specs=[pl.BlockSpec(s.shape, lambda i, chip_ref, nd=s.ndim: (0,) * nd) for s in shards],
        out_specs=tuple(pl.BlockSpec(block, lambda i, chip_ref, im=im: im(chip_ref[0])) for _, block, im in places))
    return pl.pallas_call(
        body, out_shape=tuple(jax.ShapeDtypeStruct(full, BF16) for full, _, _ in places), grid_spec=grid_spec,
        name="cast_place", compiler_params=_cp("arbitrary"),
    )(chip_idx, *_in_hbm(*shards))


def _shard_of(ref, kind, chip):
    if kind == "in":
        return ref.at[:, pl.ds(pl.multiple_of(chip * SHARD_IN, 128), SHARD_IN)]
    return ref.at[chip] if kind == "sq" else ref.at[:, chip]


def _land_shape(src, kind):
    if kind == "in":
        return (3, src.shape[0], SHARD_IN)
    return (3,) + src.shape[1:] if kind == "sq" else (3, src.shape[0]) + src.shape[2:]


def _exchange_start(srcs, kinds, tag):
    n = len(srcs)
    lands = [pltpu.with_memory_space_constraint(lax.empty(_land_shape(s, k), s.dtype), pltpu.HBM) for s, k in zip(srcs, kinds)]

    def body(*refs):
        src_refs, land_refs = refs[:n], refs[n:2 * n]
        ssems, rsems = refs[2 * n:3 * n], refs[3 * n:4 * n]
        token = refs[6 * n]
        for i in range(n):
            for k, mask in enumerate(CHIP_MASKS):
                to = _peer(mask)
                pltpu.make_async_remote_copy(
                    src_ref=_shard_of(src_refs[i], kinds[i], _chip_of(to)), dst_ref=land_refs[i].at[k],
                    send_sem=ssems[i], recv_sem=rsems[i], device_id=to, device_id_type=MESH).start()
        token[...] = jnp.zeros_like(token)

    sem = pltpu.SemaphoreType.DMA(())
    out_shape = ((sem,) * (2 * n) + tuple(pltpu.HBM(s.shape, s.dtype) for s in srcs)
                 + tuple(pltpu.HBM(l.shape, l.dtype) for l in lands) + (jax.ShapeDtypeStruct((8, 128), F32),))
    outs = pl.pallas_call(
        body, out_shape=out_shape, name=f"exchange_start_{tag}",
        in_specs=[HBM_SPEC] * (2 * n), out_specs=tuple([SEM_SPEC] * (2 * n) + [HBM_SPEC] * (2 * n) + [VMEM_SPEC]),
        input_output_aliases={i: 2 * n + i for i in range(2 * n)},
        compiler_params=pltpu.CompilerParams(has_side_effects=pltpu.SideEffectType.DATAFLOW_SIDE_EFFECTING),
    )(*[pltpu.with_memory_space_constraint(s, pltpu.HBM) for s in srcs], *lands)
    return outs[:n], outs[n:2 * n], outs[2 * n:3 * n], outs[3 * n:4 * n], outs[4 * n]


def _exchange_wait(ssems, rsems, srcs, lands, after, tag):
    n = len(srcs)

    def body(*refs):
        land_refs = refs[n:2 * n]
        ssem_refs, rsem_refs = refs[2 * n:3 * n], refs[3 * n:4 * n]
        for i in range(n):
            all_three = pltpu.make_async_remote_copy(
                src_ref=land_refs[i], dst_ref=land_refs[i], send_sem=ssem_refs[i], recv_sem=rsem_refs[i],
                device_id=_me(), device_id_type=MESH)
            all_three.wait_send()
            all_three.wait_recv()

    outs = pl.pallas_call(
        body, out_shape=tuple(pltpu.HBM(a.shape, a.dtype) for a in list(srcs) + list(lands)), name=f"exchange_wait_{tag}",
        in_specs=[HBM_SPEC] * (2 * n) + [SEM_SPEC] * (2 * n) + [ANY], out_specs=tuple([HBM_SPEC] * (2 * n)),
        input_output_aliases={i: i for i in range(2 * n)},
        compiler_params=pltpu.CompilerParams(has_side_effects=pltpu.SideEffectType.DATAFLOW_SIDE_EFFECTING),
    )(*srcs, *lands, *ssems, *rsems, after)
    return outs[:n], outs[n:]


def _gather_small_start(slabs):
    def body(slabs_ref, ssem, rsem, slabs_out, token):
        me = _me()
        mine = slabs_ref.at[_chip_of(me) * 2 + me[2]]
        for mask in ALL_MASKS:
            pltpu.make_async_remote_copy(src_ref=mine, dst_ref=mine, send_sem=ssem, recv_sem=rsem,
                                         device_id=_peer(mask), device_id_type=MESH).start()
        token[...] = jnp.zeros_like(token)

    sem = pltpu.SemaphoreType.DMA(())
    return pl.pallas_call(
        body, out_shape=(sem, sem, pltpu.HBM(slabs.shape, slabs.dtype), jax.ShapeDtypeStruct((8, 128), F32)),
        name="gather_small_start", in_specs=[HBM_SPEC], out_specs=(SEM_SPEC, SEM_SPEC, HBM_SPEC, VMEM_SPEC),
        input_output_aliases={0: 2}, compiler_params=SPLIT_COPY,
    )(pltpu.with_memory_space_constraint(slabs, pltpu.HBM))


def _gather_small_wait(ssem, rsem, slabs, after):
    def body(slabs_ref, ssem_ref, rsem_ref, after_ref, slabs_out):
        seven = slabs_ref.at[pl.ds(0, N_DEV - 1)]
        arrived = pltpu.make_async_remote_copy(
            src_ref=seven, dst_ref=seven, send_sem=ssem_ref, recv_sem=rsem_ref, device_id=_me(), device_id_type=MESH)
        arrived.wait_send()
        arrived.wait_recv()

    return pl.pallas_call(
        body, out_shape=pltpu.HBM(slabs.shape, slabs.dtype), name="gather_small_wait",
        in_specs=[HBM_SPEC, SEM_SPEC, SEM_SPEC, ANY], out_specs=HBM_SPEC, input_output_aliases={0: 0},
        compiler_params=SPLIT_COPY,
    )(slabs, ssem, rsem, after)


def _half_of(ref, axis, half):
    return ref.at[(slice(None),) * axis + (half,)]


def _swap_halves(parts, axes):
    n = len(parts)

    def body(*refs):
        ins, outs, ssem, rsem = refs[:n], refs[n:2 * n], refs[2 * n], refs[2 * n + 1]
        c = lax.axis_index("c")
        _sibling_handshake()
        cps = [pltpu.make_async_remote_copy(src_ref=_half_of(ins[i], axes[i], 1 - c), dst_ref=outs[i], send_sem=ssem.at[i],
                                            recv_sem=rsem.at[i], device_id=_peer(1), device_id_type=MESH) for i in range(n)]
        for cp in cps:
            cp.start()
        for cp in cps:
            cp.wait()

    shapes = [p.shape[:a] + p.shape[a + 1:] for p, a in zip(parts, axes)]
    return pl.pallas_call(
        body, out_shape=tuple(jax.ShapeDtypeStruct(s, p.dtype) for s, p in zip(shapes, parts)), name="swap_halves",
        in_specs=[ANY] * n, out_specs=tuple([ANY] * n),
        scratch_shapes=[pltpu.SemaphoreType.DMA((n,)), pltpu.SemaphoreType.DMA((n,))], compiler_params=SIBLING_ONLY,
    )(*parts)


def _swap_halves_start(parts, axes, tag):
    n = len(parts)
    lands = [pltpu.with_memory_space_constraint(lax.empty(p.shape[:a] + p.shape[a + 1:], p.dtype), pltpu.HBM)
             for p, a in zip(parts, axes)]

    def body(*refs):
        ins, land_refs, ssems, rsems, token = refs[:n], refs[n:2 * n], refs[2 * n:3 * n], refs[3 * n:4 * n], refs[6 * n]
        c = lax.axis_index("c")
        for i in range(n):
            pltpu.make_async_remote_copy(src_ref=_half_of(ins[i], axes[i], 1 - c), dst_ref=land_refs[i], send_sem=ssems[i],
                                         recv_sem=rsems[i], device_id=_peer(1), device_id_type=MESH).start()
        token[...] = jnp.zeros_like(token)

    sem = pltpu.SemaphoreType.DMA(())
    out_shape = ((sem,) * (2 * n) + tuple(pltpu.HBM(a.shape, a.dtype) for a in list(parts) + lands)
                 + (jax.ShapeDtypeStruct((8, 128), F32),))
    outs = pl.pallas_call(
        body, out_shape=out_shape, name=f"swap_halves_start_{tag}",
        in_specs=[HBM_SPEC] * (2 * n), out_specs=tuple([SEM_SPEC] * (2 * n) + [HBM_SPEC] * (2 * n) + [VMEM_SPEC]),
        input_output_aliases={i: 2 * n + i for i in range(2 * n)}, compiler_params=SPLIT_COPY,
    )(*[pltpu.with_memory_space_constraint(p, pltpu.HBM) for p in parts], *lands)
    return outs[:n], outs[n:2 * n], outs[2 * n:3 * n], outs[3 * n:4 * n], outs[4 * n]


def _swap_halves_wait(ssems, rsems, parts, lands, after, tag):
    n = len(parts)

    def body(*refs):
        land_refs, ssem_refs, rsem_refs = refs[n:2 * n], refs[2 * n:3 * n], refs[3 * n:4 * n]
        for i in range(n):
            moved = pltpu.make_async_remote_copy(
                src_ref=land_refs[i], dst_ref=land_refs[i], send_sem=ssem_refs[i], recv_sem=rsem_refs[i],
                device_id=_me(), device_id_type=MESH)
            moved.wait_send()
            moved.wait_recv()

    outs = pl.pallas_call(
        body, out_shape=tuple(pltpu.HBM(a.shape, a.dtype) for a in list(parts) + list(lands)), name=f"swap_halves_wait_{tag}",
        in_specs=[HBM_SPEC] * (2 * n) + [SEM_SPEC] * (2 * n) + [ANY], out_specs=tuple([HBM_SPEC] * (2 * n)),
        input_output_aliases={i: i for i in range(2 * n)}, compiler_params=SPLIT_COPY,
    )(*parts, *lands, *ssems, *rsems, after)
    return list(outs[:n]), list(outs[n:])


def _presum(mines, sibs, c_idx, tag):
    n = len(mines)
    S, _, R, C = mines[0].shape
    tr = min(R, 256)
    tc = SHARD_IN if C % SHARD_IN == 0 else (C // 2 if n > 1 and C % 256 == 0 else C)

    def body(c_ref, *refs):
        for k in range(n):
            total = refs[k][:, 0] + refs[n + k][...]
            refs[2 * n + k][...] = total
            refs[3 * n + k][...] = total.astype(BF16)

    out_spec = pl.BlockSpec((S, tr, tc), lambda i, j, c_ref: (0, i, j))
    grid_spec = pltpu.PrefetchScalarGridSpec(
        num_scalar_prefetch=1, grid=(R // tr, C // tc),
        in_specs=[pl.BlockSpec((S, 1, tr, tc), lambda i, j, c_ref: (0, c_ref[0], i, j))] * n + [out_spec] * n,
        out_specs=(out_spec,) * (2 * n))
    outs = pl.pallas_call(
        body, out_shape=(jax.ShapeDtypeStruct((S, R, C), F32),) * n + (jax.ShapeDtypeStruct((S, R, C), BF16),) * n,
        grid_spec=grid_spec, name=f"presum_{tag}", compiler_params=_cp("parallel", "parallel"),
    )(c_idx, *mines, *sibs)
    return list(outs[:n]), list(outs[n:])


def _assemble_with_sibling(parts, axes):
    n = len(parts)

    def body(*refs):
        outs, ssem, rsem = refs[n:2 * n], refs[2 * n], refs[2 * n + 1]
        c = lax.axis_index("c")
        _sibling_handshake()
        cps = [pltpu.make_async_remote_copy(
            src_ref=_half_of(outs[i], axes[i], c), dst_ref=_half_of(outs[i], axes[i], c), send_sem=ssem.at[i],
            recv_sem=rsem.at[i], device_id=_peer(1), device_id_type=MESH) for i in range(n)]
        for cp in cps:
            cp.start()
        for i in range(n):
            pltpu.make_async_remote_copy(
                src_ref=_half_of(outs[i], axes[i], c), dst_ref=_half_of(outs[i], axes[i], 1 - c), send_sem=ssem.at[i],
                recv_sem=rsem.at[i], device_id=_peer(1), device_id_type=MESH).wait_recv()
        for cp in cps:
            cp.wait_send()

    return pl.pallas_call(
        body, out_shape=tuple(jax.ShapeDtypeStruct(p.shape, p.dtype) for p in parts), name="assemble_with_sibling",
        in_specs=[ANY] * n, out_specs=tuple([ANY] * n), input_output_aliases={i: i for i in range(n)},
        scratch_shapes=[pltpu.SemaphoreType.DMA((n,)), pltpu.SemaphoreType.DMA((n,))], compiler_params=SIBLING_ONLY,
    )(*parts)


def _assemble_start(parts, axes):
    n = len(parts)

    def body(*refs):
        ssems, rsems, outs, token = refs[n:2 * n], refs[2 * n:3 * n], refs[3 * n:4 * n], refs[4 * n]
        c = lax.axis_index("c")
        for i in range(n):
            mine = _half_of(outs[i], axes[i], c)
            pltpu.make_async_remote_copy(src_ref=mine, dst_ref=mine, send_sem=ssems[i], recv_sem=rsems[i],
                                         device_id=_peer(1), device_id_type=MESH).start()
        token[...] = jnp.zeros_like(token)

    sem = pltpu.SemaphoreType.DMA(())
    out_shape = ((sem,) * (2 * n) + tuple(pltpu.HBM(p.shape, p.dtype) for p in parts)
                 + (jax.ShapeDtypeStruct((8, 128), F32),))
    outs = pl.pallas_call(
        body, out_shape=out_shape, name="assemble_start",
        in_specs=[HBM_SPEC] * n, out_specs=tuple([SEM_SPEC] * (2 * n) + [HBM_SPEC] * n + [VMEM_SPEC]),
        input_output_aliases={i: 2 * n + i for i in range(n)}, compiler_params=SPLIT_COPY,
    )(*[pltpu.with_memory_space_constraint(p, pltpu.HBM) for p in parts])
    return outs[:n], outs[n:2 * n], list(outs[2 * n:3 * n]), outs[3 * n]


def _assemble_wait(ssems, rsems, parts, axes, after):
    n = len(parts)

    def body(*refs):
        ssem_refs, rsem_refs = refs[n:2 * n], refs[2 * n:3 * n]
        for i in range(n):
            half = _half_of(refs[i], axes[i], 0)
            moved = pltpu.make_async_remote_copy(
                src_ref=half, dst_ref=half, send_sem=ssem_refs[i], recv_sem=rsem_refs[i],
                device_id=_me(), device_id_type=MESH)
            moved.wait_send()
            moved.wait_recv()

    outs = pl.pallas_call(
        body, out_shape=tuple(pltpu.HBM(p.shape, p.dtype) for p in parts), name="assemble_wait",
        in_specs=[HBM_SPEC] * n + [SEM_SPEC] * (2 * n) + [ANY], out_specs=tuple([HBM_SPEC] * n),
        input_output_aliases={i: i for i in range(n)}, compiler_params=SPLIT_COPY,
    )(*parts, *ssems, *rsems, after)
    return list(outs)


def _rope_lane_frequencies():
    inv = np.float32(ROPE_THETA) ** (-(np.arange(0, 2 * ROT_HALF, 2, dtype=np.float32)) / np.float32(2 * ROT_HALF))
    lane = np.arange(128) % HEAD_DIM
    return jnp.asarray(np.where(lane < 2 * ROT_HALF, inv[lane % ROT_HALF], 0.0).astype(np.float32)[None, :])


def _rope_tables(pos, freq):
    ang = pos.astype(F32) * freq
    c, s = jnp.cos(ang), jnp.sin(ang)
    m = lax.broadcasted_iota(jnp.int32, ang.shape, 1) & (HEAD_DIM - 1)
    return (jnp.where(m < 2 * ROT_HALF, c, 1.0), jnp.where(m < ROT_HALF, -s, 0.0),
            jnp.where((m >= ROT_HALF) & (m < 2 * ROT_HALF), s, 0.0))


def _columns(t):
    return [t[:, i:i + 128] for i in range(0, t.shape[-1], 128)]


def _rope(t, c, sa, sb):
    return jnp.concatenate(
        [x * c + pltpu.roll(x, 128 - ROT_HALF, 1) * sa + pltpu.roll(x, ROT_HALF, 1) * sb for x in _columns(t)], axis=1)


def _unrope(d, c, sa, sb):
    return jnp.concatenate(
        [x * c + pltpu.roll(x * sa, ROT_HALF, 1) + pltpu.roll(x * sb, 128 - ROT_HALF, 1) for x in _columns(d)], axis=1)


def _prenorm(x, mod_row, norm_g, pos_col):
    T = x.shape[0]
    tm = min(T, 512)

    def body(x_ref, mod_ref, g_ref, pos_ref, f_ref, h_ref, ht_ref, c_ref, sa_ref, sb_ref):
        xf = x_ref[...]
        shift, scale = mod_ref[:, 0:D_MODEL], mod_ref[:, D_MODEL:2 * D_MODEL]
        h = (xf * _rms(xf)) * g_ref[...] * (1.0 + scale) + shift
        h_ref[...] = h.astype(BF16)
        ht_ref[...] = h.T.astype(BF16)
        c_ref[...], sa_ref[...], sb_ref[...] = _rope_tables(pos_ref[...], f_ref[...])

    tab = jax.ShapeDtypeStruct((T, 128), F32)
    tok = lambda w: pl.BlockSpec((tm, w), lambda i: (i, 0))
    row = lambda w: pl.BlockSpec((1, w), lambda i: (0, 0))
    outs = pl.pallas_call(
        body, out_shape=(jax.ShapeDtypeStruct((T, D_MODEL), BF16), jax.ShapeDtypeStruct((D_MODEL, T), BF16), tab, tab, tab),
        grid=(T // tm,), name="prenorm",
        in_specs=[tok(D_MODEL), row(ADA_W), row(D_MODEL), tok(1), row(128)],
        out_specs=(tok(D_MODEL), pl.BlockSpec((D_MODEL, tm), lambda i: (0, i)), tok(128), tok(128), tok(128)),
        compiler_params=_cp("parallel"),
    )(x, *_in_hbm(mod_row, norm_g), pos_col, _rope_lane_frequencies())
    return outs[0], outs[1], tuple(outs[2:])


def _in_projection(h, w_in, chips, into, tag):
    T = h.shape[0]
    tm, tn = min(T, 512), SHARD_IN
    k = chips.shape[0]

    def body(chip_ref, h_ref, w_ref, *rest):
        rest[-1][...] = _dot(h_ref[...], w_ref[...])

    w_spec = pl.BlockSpec((D_MODEL, tn), lambda s, i, c: (0, c[s]), **({"pipeline_mode": pl.Buffered(1)} if k == 1 else {}))
    in_specs = [pl.BlockSpec((tm, D_MODEL), lambda s, i, c: (i, 0)), w_spec]
    args = [chips, h, w_in]
    aliases = {}
    if into is not None:
        in_specs.append(ANY)
        args.append(into)
        aliases = {3: 0}
    grid_spec = pltpu.PrefetchScalarGridSpec(num_scalar_prefetch=1, grid=(k, T // tm), in_specs=in_specs,
                                             out_specs=pl.BlockSpec((tm, tn), lambda s, i, c: (i, c[s])))
    return pl.pallas_call(
        body, out_shape=jax.ShapeDtypeStruct((T, IN_W), F32), grid_spec=grid_spec, name=f"in_projection_{tag}",
        input_output_aliases=aliases, compiler_params=_cp("parallel", "parallel"),
    )(*args)


def _attn_mask(n):
    qi = lax.broadcasted_iota(jnp.int32, (GROUP * BLOCK, BLOCK), 0) & (BLOCK - 1)
    j = lax.broadcasted_iota(jnp.int32, (GROUP * BLOCK, BLOCK), 1)
    own = j <= qi
    return own, jnp.logical_not(own) & (n == 0)


def _fold(x, own):
    return jnp.where(own, x[:, BLOCK:2 * BLOCK], x[:, 0:BLOCK])


def _unfold(xf, own):
    zero = jnp.zeros_like(xf)
    return jnp.concatenate([jnp.where(own, zero, xf), jnp.where(own, xf, zero)], axis=1)


ROW_GROUP_HEAD = (0, 2, 1, 3)


def _sink_col(sink_ref, kh):
    rowg = lax.broadcasted_iota(jnp.int32, (GROUP * BLOCK, 1), 0) // BLOCK
    col = jnp.full((GROUP * BLOCK, 1), sink_ref[0, GROUP * kh + ROW_GROUP_HEAD[0]], F32)
    for g in range(1, GROUP):
        col = jnp.where(rowg == g, sink_ref[0, GROUP * kh + ROW_GROUP_HEAD[g]], col)
    return col


def _low_lanes(shape):
    return lax.broadcasted_iota(jnp.int32, shape, 1) < HEAD_DIM


def _kv_pair_operand(prev, cur, kh):
    c = 128 * (kh // 2)
    col = jnp.concatenate([prev[:, c:c + 128], cur[:, c:c + 128]], axis=0).astype(F32)
    if kh % 2 == 0:
        lo = jnp.where(_low_lanes(col.shape), col, 0.0)
        hi = pltpu.roll(lo, HEAD_DIM, 1)
    else:
        hi = jnp.where(_low_lanes(col.shape), 0.0, col)
        lo = pltpu.roll(hi, HEAD_DIM, 1)
    return jnp.concatenate([lo, hi], axis=0).astype(BF16)


def _pair_rows(x, kh):
    c = 2 * 128 * kh
    return jnp.concatenate([x[:, c:c + 128], x[:, c + 128:c + 256]], axis=0)


def _restack(big):
    return jnp.concatenate([big[:, 0:2 * BLOCK], big[:, 2 * BLOCK:4 * BLOCK]], axis=0)


def _unrestack(stacked):
    return jnp.concatenate([stacked[0:2 * BLOCK], stacked[2 * BLOCK:4 * BLOCK]], axis=1)


def _fold_pair(x2, kh):
    low = _low_lanes((2 * BLOCK, 128))
    mixed = jnp.where(low, x2[0:2 * BLOCK], x2[2 * BLOCK:4 * BLOCK])
    total = mixed + pltpu.roll(mixed, HEAD_DIM, 1)
    return jnp.where(low, total, 0.0) if kh % 2 == 0 else jnp.where(low, 0.0, total)


def _attn_scores(qr, k2, kh):
    q2 = _pair_rows(qr, kh).astype(BF16)
    return q2, _restack(_dot_nt(q2, k2))


def _attn_softmax(s, sink_col, mask):
    own, no_key = mask
    s = jnp.where(no_key, -1e30, _fold(s, own))
    m = jnp.maximum(jnp.max(s, axis=-1, keepdims=True), sink_col)
    p = jnp.exp(s - m)
    p_sink = jnp.exp(sink_col - m)
    denom = jnp.sum(p, axis=-1, keepdims=True) + p_sink
    return p / denom, p_sink / denom


def _attn_forward(proj, tabs, sinks):
    T = proj.shape[0]
    nb = T // BLOCK

    def body(q_ref, kvc_ref, kvp_ref, g0_ref, g1_ref, cc, sac, sbc, cp_, sap, sbp, sink_ref, y_ref, qrb_ref, krb_ref, p_ref):
        n = pl.program_id(0)
        tc = tcur = (cc[...], sac[...], sbc[...])
        tprev = (cp_[...], sap[...], sbp[...])
        qr = _rope(q_ref[...], *tc) * ATTN_SCALE
        kr_cur = _rope(kvc_ref[:, 0:KV_W], *tcur)
        kr_prev = _rope(kvp_ref[:, 0:KV_W], *tprev)
        qrb_ref[...] = qr.astype(BF16)
        krb_ref[...] = kr_cur.astype(BF16)
        v_cur, v_prev = kvc_ref[:, KV_W:2 * KV_W], kvp_ref[:, KV_W:2 * KV_W]
        mask = _attn_mask(n)
        outs = []
        k2s = [_kv_pair_operand(kr_prev, kr_cur, kh) for kh in range(N_KV)]
        v2s = [_kv_pair_operand(v_prev, v_cur, kh) for kh in range(N_KV)]
        scores = [_attn_scores(qr, k2s[kh], kh) for kh in range(N_KV)]
        p_parts = []
        for kh in range(N_KV):
            pn, _ = _attn_softmax(scores[kh][1], _sink_col(sink_ref, kh), mask)
            p_parts += [pn[g * BLOCK:(g + 1) * BLOCK] for g in range(GROUP)]
            o_big = _dot(_unrestack(_unfold(pn.astype(BF16), mask[0])), v2s[kh])
            outs += [o_big[0:BLOCK], o_big[BLOCK:2 * BLOCK]]
        p_ref[...] = jnp.concatenate(p_parts, axis=1)
        o = jnp.concatenate(outs, axis=1)
        g = jnp.concatenate([g0_ref[...], g1_ref[...]], axis=1)
        y_ref[...] = (o * (g * _sigmoid(g))).astype(BF16)

    def blk(w, cb):
        return pl.BlockSpec((BLOCK, w), lambda n, cb=cb: (n, cb))

    prev = lambda w, cb: pl.BlockSpec((BLOCK, w), lambda n, cb=cb: (jnp.maximum(n - 1, 0), cb))
    return pl.pallas_call(
        body, grid=(nb,), name="attn_forward",
        out_shape=(jax.ShapeDtypeStruct((T, D_MODEL), BF16), jax.ShapeDtypeStruct((T, D_MODEL), BF16),
                   jax.ShapeDtypeStruct((T, KV_W), BF16), jax.ShapeDtypeStruct((T, N_HEADS * BLOCK), F32)),
        in_specs=[blk(D_MODEL, 0), blk(CB, CB_KV), prev(CB, CB_KV), blk(CB, CB_GA), blk(CB, CB_GA + 1),
                  blk(128, 0), blk(128, 0), blk(128, 0), prev(128, 0), prev(128, 0), prev(128, 0),
                  pl.BlockSpec(memory_space=pltpu.SMEM)],
        out_specs=(blk(D_MODEL, 0), blk(D_MODEL, 0), blk(KV_W, 0), blk(N_HEADS * BLOCK, 0)),
        compiler_params=_cp("parallel"),
    )(proj, proj, proj, proj, proj, *tabs, *tabs, sinks)


def _scan_rows8():
    return lax.broadcasted_iota(jnp.int32, (8, D_MODEL), 0)


def _scan_forward(a_ref, b_ref, h_ref, carry, rows):
    row = _scan_rows8()

    def group(i, carry):
        off = pl.multiple_of(i * 8, 8)
        a, b = a_ref[pl.ds(off, 8), :], b_ref[pl.ds(off, 8), :]
        for d in (1, 2, 4):
            ok = row >= d
            b = jnp.where(ok, a * pltpu.roll(b, d, 0) + b, b)
            a = jnp.where(ok, a * pltpu.roll(a, d, 0), a)
        h = a * carry + b
        h_ref[pl.ds(off, 8), :] = h
        return h[7:8, :]

    return lax.fori_loop(0, rows // 8, group, carry)


def _scan_backward(a_ref, g_ref, lam_ref, carry, rows):
    row = _scan_rows8()

    def group(i, carry):
        off = pl.multiple_of((rows // 8 - 1 - i) * 8, 8)
        a, g = a_ref[pl.ds(off, 8), :], g_ref[pl.ds(off, 8), :]
        b = a * g
        for d in (1, 2, 4):
            ok = row < 8 - d
            b = jnp.where(ok, a * pltpu.roll(b, 8 - d, 0) + b, b)
            a = jnp.where(ok, a * pltpu.roll(a, 8 - d, 0), a)
        mu = a * carry + b
        mu_below = jnp.where(row == 7, carry, pltpu.roll(mu, 7, 0))
        lam_ref[pl.ds(off, 8), :] = g + mu_below
        return mu[0:1, :]

    return lax.fori_loop(0, rows // 8, group, carry)


def _conv_taps(xbuf, xr, tail):
    rows = xr.shape[0]
    xbuf[0:8, :] = tail
    xbuf[8:rows + 8, :] = xr
    return [xbuf[pl.ds(8 - (CONV_W - 1 - k), rows), :] for k in range(CONV_W - 1)] + [xr]


def _rnn_gates(xbuf, xr, tail, cw, cb, wa_ref, wx_ref, ba, bx, sp, reset):
    xs = _conv_taps(xbuf, xr, tail)
    xc = xs[0] * cw[0:1, :]
    for k in range(1, CONV_W):
        xc = xc + xs[k] * cw[k:k + 1, :]
    xc = xc + cb
    xcb = xc.astype(BF16)
    za = jnp.concatenate([_dot(xcb[:, RNN_BW * j:RNN_BW * (j + 1)], wa_ref[j]) for j in range(RNN_BLOCKS)], axis=1) + ba
    zx = jnp.concatenate([_dot(xcb[:, RNN_BW * j:RNN_BW * (j + 1)], wx_ref[j]) for j in range(RNN_BLOCKS)], axis=1) + bx
    r, i = _sigmoid(za), _sigmoid(zx)
    neg_log_a = LRU_C * r * sp
    a_raw = jnp.exp(-neg_log_a)
    mult_raw = jnp.sqrt(jnp.tanh(neg_log_a) * (1.0 + a_raw * a_raw))
    a = jnp.where(reset, 0.0, a_raw)
    mult = jnp.where(reset, 1.0, mult_raw)
    return xc, r, i, a, mult


def _rnn_forward(proj, pos_col, conv_w, conv_b, rwa, rwx, ba, bx, lam):
    T = proj.shape[0]
    tr = min(T, 256)

    def body(x0, x1, g0, g1, pos_ref, cw_ref, cb_ref, wa_ref, wx_ref, ba_ref, bx_ref, lam_ref,
             y_ref, h_ref, xc_ref, r_ref, i_ref, a_ref, mult_ref, xbuf, bbuf, tail, carry):
        t = pl.program_id(0)

        @pl.when(t == 0)
        def _():
            tail[...] = jnp.zeros_like(tail)
            carry[...] = jnp.zeros_like(carry)

        xr = jnp.concatenate([x0[...], x1[...]], axis=1)
        sp = _softplus(-lam_ref[...])
        reset = pos_ref[...] == 0
        xc, r, i, a, mult = _rnn_gates(
            xbuf, xr, tail[...], cw_ref[...], cb_ref[...], wa_ref, wx_ref, ba_ref[...], bx_ref[...], sp, reset)
        xc_ref[...] = xc
        r_ref[...] = r
        i_ref[...] = i
        a_ref[...] = a
        mult_ref[...] = mult
        bbuf[...] = mult * (i * xc)
        last = _scan_forward(a_ref, bbuf, h_ref, carry[0:1, :], tr)
        carry[...] = jnp.broadcast_to(last, carry.shape)
        tail[...] = xr[tr - 8:tr, :]
        g = jnp.concatenate([g0[...], g1[...]], axis=1)
        y_ref[...] = (h_ref[...] * (g * _sigmoid(g))).astype(BF16)

    blk = lambda cb: pl.BlockSpec((tr, CB), lambda t, cb=cb: (t, cb))
    row = lambda w: pl.BlockSpec((1, w), lambda t: (0, 0))
    full3 = pl.BlockSpec((RNN_BLOCKS, RNN_BW, RNN_BW), lambda t: (0, 0, 0))
    tok = pl.BlockSpec((tr, D_MODEL), lambda t: (t, 0))
    act = jax.ShapeDtypeStruct((T, D_MODEL), F32)
    return pl.pallas_call(
        body, out_shape=(jax.ShapeDtypeStruct((T, D_MODEL), BF16),) + (act,) * 6,
        grid=(T // tr,), name="rnn_forward",
        in_specs=[blk(CB_XR), blk(CB_XR + 1), blk(CB_GR), blk(CB_GR + 1), pl.BlockSpec((tr, 1), lambda t: (t, 0)),
                  pl.BlockSpec((CONV_W, D_MODEL), lambda t: (0, 0)), row(D_MODEL), full3, full3,
                  row(D_MODEL), row(D_MODEL), row(D_MODEL)],
        out_specs=(tok,) * 7,
        scratch_shapes=[pltpu.VMEM((tr + 8, D_MODEL), F32), pltpu.VMEM((tr, D_MODEL), F32),
                        pltpu.VMEM((8, D_MODEL), F32), pltpu.VMEM((8, D_MODEL), F32)],
        compiler_params=_cp("arbitrary"),
    )(proj, proj, proj, proj, pos_col, *_in_hbm(conv_w, conv_b, rwa, rwx, ba, bx, lam))


def _merge_and_head(x, target, y_attn, y_rnn, proj, wap, wrp, wo, mod_row, final_g):
    T = x.shape[0]
    tm = min(T, 256)

    def body(x_ref, t_ref, ya_ref, yr_ref, ma0, ma1, mr0, mr1, wap_ref, wrp_ref, wo_ref, mod_ref, fg_ref,
             dx2_ref, mg_ref, do_ref, dpa_ref, dpr_ref, dya_ref, dyr_ref, dc_ref, dfg_ref, dgate_ref, loss_ref):
        i = pl.program_id(0)
        gate = mod_ref[:, 2 * D_MODEL:3 * D_MODEL]
        fg = fg_ref[...]
        pa, pr = _dot(ya_ref[...], wap_ref[...]), _dot(yr_ref[...], wrp_ref[...])
        sa = _sigmoid(jnp.concatenate([ma0[...], ma1[...]], axis=1))
        sr = _sigmoid(jnp.concatenate([mr0[...], mr1[...]], axis=1))
        mb = (sa * pa + sr * pr).astype(BF16)
        o = _dot(mb, wo_ref[...])
        x2 = x_ref[...] + gate * o
        r2 = _rms(x2)
        xn2 = x2 * r2
        err = xn2 * fg - t_ref[...]
        loss_t = 0.5 * jnp.sum(jnp.sum(err * err, axis=-1, keepdims=True) * (1.0 / D_MODEL), axis=0, keepdims=True)
        dy = err * (1.0 / D_MODEL)
        dfg_t = jnp.sum(dy * xn2, axis=0, keepdims=True)
        dxn = dy * fg
        dx2 = r2 * (dxn - xn2 * jnp.mean(dxn * xn2, axis=-1, keepdims=True))
        dgate_t = jnp.sum(dx2 * o, axis=0, keepdims=True)
        dob = (dx2 * gate).astype(BF16)
        dmerged = _dot_nt(dob, wo_ref[...])
        dpa, dpr = (dmerged * sa).astype(BF16), (dmerged * sr).astype(BF16)
        dya, dyr = _dot_nt(dpa, wap_ref[...]), _dot_nt(dpr, wrp_ref[...])
        dx2_ref[...] = dx2
        mg_ref[...] = mb
        do_ref[...] = dob
        dpa_ref[...] = dpa
        dpr_ref[...] = dpr
        dya_ref[...] = dya
        dyr_ref[...] = dyr
        dc_ref[:, 0:D_MODEL] = (dmerged * pa * sa * (1.0 - sa)).astype(BF16)
        dc_ref[:, D_MODEL:2 * D_MODEL] = (dmerged * pr * sr * (1.0 - sr)).astype(BF16)

        @pl.when(i == 0)
        def _():
            dfg_ref[...] = jnp.zeros_like(dfg_ref)
            dgate_ref[...] = jnp.zeros_like(dgate_ref)
            loss_ref[...] = jnp.zeros_like(loss_ref)

        dfg_ref[...] += dfg_t
        dgate_ref[...] += dgate_t
        loss_ref[...] += jnp.broadcast_to(loss_t, loss_ref.shape)

    tok = lambda w: pl.BlockSpec((tm, w), lambda i: (i, 0))
    blk = lambda cb: pl.BlockSpec((tm, CB), lambda i, cb=cb: (i, cb))
    wfull = pl.BlockSpec((D_MODEL, D_MODEL), lambda i: (0, 0), pipeline_mode=pl.Buffered(1))
    row = lambda w: pl.BlockSpec((1, w), lambda i: (0, 0))
    out_shape = (
        jax.ShapeDtypeStruct((T, D_MODEL), F32), jax.ShapeDtypeStruct((T, D_MODEL), BF16),
        jax.ShapeDtypeStruct((T, D_MODEL), BF16), jax.ShapeDtypeStruct((T, D_MODEL), BF16),
        jax.ShapeDtypeStruct((T, D_MODEL), BF16), jax.ShapeDtypeStruct((T, D_MODEL), F32),
        jax.ShapeDtypeStruct((T, D_MODEL), F32), jax.ShapeDtypeStruct((T, 2 * D_MODEL), BF16),
        jax.ShapeDtypeStruct((1, D_MODEL), F32), jax.ShapeDtypeStruct((1, D_MODEL), F32),
        jax.ShapeDtypeStruct((1, 128), F32),
    )
    return pl.pallas_call(
        body, out_shape=out_shape, grid=(T // tm,), name="merge_and_head",
        in_specs=[tok(D_MODEL), tok(D_MODEL), tok(D_MODEL), tok(D_MODEL), blk(CB_MA), blk(CB_MA + 1), blk(CB_MR),
                  blk(CB_MR + 1), wfull, wfull, wfull, row(ADA_W), row(D_MODEL)],
        out_specs=(tok(D_MODEL),) * 7 + (tok(2 * D_MODEL), row(D_MODEL), row(D_MODEL), row(128)),
        compiler_params=_cp("arbitrary"),
    )(x, target, y_attn, y_rnn, proj, proj, proj, proj, wap, wrp, wo, *_in_hbm(mod_row, final_g))


def _attn_backward(proj, qr_b, kr_b, p_all, d_y, tabs, after):
    T = proj.shape[0]
    nb = T // BLOCK

    def body(qrb_ref, krc_ref, krp_ref, vc_ref, vp_ref, g0_ref, g1_ref, dy_ref, p_ref, cc, sac, sbc, cp_, sap, sbp, after_ref,
             dq_ref, dkv_ref, dg_ref, dsink_ref, carry):
        n = pl.program_id(0)

        @pl.when(n == 0)
        def _():
            carry[...] = jnp.zeros_like(carry)
            dsink_ref[...] = jnp.zeros_like(dsink_ref)

        @pl.when(n < nb)
        def _():
            tc = tcur = (cc[...], sac[...], sbc[...])
            tprev = (cp_[...], sap[...], sbp[...])
            qr, kr_cur, kr_prev = qrb_ref[...], krc_ref[...], krp_ref[...]
            v_cur, v_prev = vc_ref[...], vp_ref[...]
            g = jnp.concatenate([g0_ref[...], g1_ref[...]], axis=1)
            sg = _sigmoid(g)
            dy = dy_ref[...]
            d_o = dy * (g * sg)
            mask = _attn_mask(n)
            lane = lax.broadcasted_iota(jnp.int32, (1, 128), 1)
            rowg = lax.broadcasted_iota(jnp.int32, (GROUP * BLOCK, 1), 0) // BLOCK
            o_parts, dq_parts = [], []
            dk_cols, dv_cols = [None, None], [None, None]
            dsink = jnp.zeros((1, 128), F32)
            heads = range(N_KV)
            k2s = [_kv_pair_operand(kr_prev, kr_cur, kh) for kh in heads]
            v2s = [_kv_pair_operand(v_prev, v_cur, kh) for kh in heads]
            q2s = [_pair_rows(qr, kh).astype(BF16) for kh in heads]
            do2s = [_pair_rows(d_o, kh).astype(BF16) for kh in heads]
            dpns = [_fold(_restack(_dot_nt(do2s[kh], v2s[kh])), mask[0]) for kh in heads]
            pns = [jnp.concatenate([p_ref[:, BLOCK * (GROUP * kh + g):BLOCK * (GROUP * kh + g + 1)] for g in range(GROUP)], axis=0)
                   for kh in heads]
            probs = [(pn, 1.0 - jnp.sum(pn, axis=-1, keepdims=True)) for pn in pns]
            p_bigs = [_unrestack(_unfold(probs[kh][0].astype(BF16), mask[0])) for kh in heads]
            o_bigs = [_dot(p_bigs[kh], v2s[kh]) for kh in heads]
            dv2s = [_dot_tn(p_bigs[kh], do2s[kh]) for kh in heads]
            deltas = [jnp.sum(probs[kh][0] * dpns[kh], axis=-1, keepdims=True) for kh in heads]
            ds_bigs = [_unrestack(_unfold((probs[kh][0] * (dpns[kh] - deltas[kh])).astype(BF16), mask[0])) for kh in heads]
            dq2s = [_dot(ds_bigs[kh], k2s[kh]) for kh in heads]
            dk2s = [_dot_tn(ds_bigs[kh], q2s[kh]) for kh in heads]
            for kh in heads:
                o_parts += [o_bigs[kh][0:BLOCK], o_bigs[kh][BLOCK:2 * BLOCK]]
                dq_parts += [dq2s[kh][0:BLOCK], dq2s[kh][BLOCK:2 * BLOCK]]
                dk_c, dv_c = _fold_pair(dk2s[kh], kh), _fold_pair(dv2s[kh], kh)
                c = kh // 2
                dk_cols[c] = dk_c if dk_cols[c] is None else dk_cols[c] + dk_c
                dv_cols[c] = dv_c if dv_cols[c] is None else dv_cols[c] + dv_c
                ds_rows = probs[kh][1] * deltas[kh]
                for gq in range(GROUP):
                    val = -jnp.sum(jnp.where(rowg == gq, ds_rows, 0.0), axis=0, keepdims=True)
                    dsink = dsink + jnp.where(lane == GROUP * kh + ROW_GROUP_HEAD[gq], val, 0.0)
            o = jnp.concatenate(o_parts, axis=1)
            dg_ref[...] = (dy * o * (sg * (1.0 + g * (1.0 - sg)))).astype(BF16)
            dq_ref[...] = (_unrope(jnp.concatenate(dq_parts, axis=1), *tc) * ATTN_SCALE).astype(BF16)
            dk_all, dv_all = jnp.concatenate(dk_cols, axis=1), jnp.concatenate(dv_cols, axis=1)
            dk_prev = _unrope(dk_all[0:BLOCK], *tprev)
            dk_cur = _unrope(dk_all[BLOCK:2 * BLOCK], *tcur)
            dv_prev, dv_cur = dv_all[0:BLOCK], dv_all[BLOCK:2 * BLOCK]
            dkv_ref[...] = (carry[...] + jnp.concatenate([dk_prev, dv_prev], axis=1)).astype(BF16)
            carry[...] = jnp.concatenate([dk_cur, dv_cur], axis=1)
            dsink_ref[...] += dsink

        @pl.when(n == nb)
        def _():
            dkv_ref[...] = carry[...].astype(BF16)

    cur = lambda w, cb: pl.BlockSpec((BLOCK, w), lambda n, cb=cb: (jnp.minimum(n, nb - 1), cb))
    prev = lambda w, cb: pl.BlockSpec((BLOCK, w), lambda n, cb=cb: (jnp.maximum(jnp.minimum(n, nb - 1) - 1, 0), cb))
    out_shape = (jax.ShapeDtypeStruct((T, D_MODEL), BF16), jax.ShapeDtypeStruct((T, 2 * KV_W), BF16),
                 jax.ShapeDtypeStruct((T, D_MODEL), BF16), jax.ShapeDtypeStruct((1, 128), F32))
    return pl.pallas_call(
        body, out_shape=out_shape, grid=(nb + 1,), name="attn_backward",
        in_specs=[cur(D_MODEL, 0), cur(KV_W, 0), prev(KV_W, 0), cur(KV_W, V_COL_BLOCK), prev(KV_W, V_COL_BLOCK),
                  cur(CB, CB_GA), cur(CB, CB_GA + 1), cur(D_MODEL, 0), cur(N_HEADS * BLOCK, 0),
                  cur(128, 0), cur(128, 0), cur(128, 0), prev(128, 0), prev(128, 0), prev(128, 0),
                  pl.BlockSpec(memory_space=pltpu.SMEM)],
        out_specs=(cur(D_MODEL, 0), pl.BlockSpec((BLOCK, 2 * KV_W), lambda n: (jnp.maximum(n - 1, 0), 0)),
                   cur(D_MODEL, 0), pl.BlockSpec((1, 128), lambda n: (0, 0))),
        scratch_shapes=[pltpu.VMEM((BLOCK, 2 * KV_W), F32)],
        compiler_params=_cp("arbitrary"),
    )(qr_b, kr_b, kr_b, proj, proj, proj, proj, d_y, p_all, *tabs, *tabs, after)


def _rnn_backward(proj, pos_col, h_rnn, saved, d_y, conv_w, rwa, rwx, lam):
    T = proj.shape[0]
    tr = min(T, 256)
    nt = T // tr
    hb = tr // 8

    def body(x0, x1, xh0, xh1, g0, g1, pos_ref, h_ref, hh_ref, xc_ref, r_ref, i_ref, a_ref, mult_ref, dy_ref,
             cw_ref, wa_ref, wx_ref, lam_ref, db_ref, dcw_ref, dcb_ref, dwa_ref, dwx_ref, dba_ref, dbx_ref, dlam_ref,
             xbuf, hbuf, dbuf, gbuf, lbuf, mu_carry, dxc_head):
        step = pl.program_id(0)
        first_tile = step == nt - 1

        @pl.when(step == 0)
        def _():
            mu_carry[...] = jnp.zeros_like(mu_carry)
            dxc_head[...] = jnp.zeros_like(dxc_head)
            for ref in (dcw_ref, dcb_ref, dwa_ref, dwx_ref, dba_ref, dbx_ref, dlam_ref):
                ref[...] = jnp.zeros_like(ref)

        xr = jnp.concatenate([x0[...], x1[...]], axis=1)
        tail = jnp.where(first_tile, 0.0, jnp.concatenate([xh0[...], xh1[...]], axis=1))
        lam_v = lam_ref[...]
        sp = _softplus(-lam_v)
        reset = pos_ref[...] == 0
        cw = cw_ref[...]
        xbuf[0:8, :] = tail
        xbuf[8:tr + 8, :] = xr
        g = jnp.concatenate([g0[...], g1[...]], axis=1)
        sg = _sigmoid(g)
        dy = dy_ref[...]
        h = h_ref[...]
        db_ref[:, D_MODEL:2 * D_MODEL] = (dy * h * (sg * (1.0 + g * (1.0 - sg)))).astype(BF16)
        gbuf[...] = dy * (g * sg)
        top = _scan_backward(a_ref, gbuf, lbuf, mu_carry[0:1, :], tr)
        mu_carry[...] = jnp.broadcast_to(top, mu_carry.shape)
        hbuf[0:8, :] = jnp.where(first_tile, 0.0, hh_ref[...])
        hbuf[8:tr + 8, :] = h
        live = jnp.logical_not(reset)
        dbuf[tr:tr + 8, :] = dxc_head[...]
        for j in range(RNN_BLOCKS):
            sl = slice(RNN_BW * j, RNN_BW * (j + 1))
            lam_t, h_prev = lbuf[:, sl], hbuf[pl.ds(7, tr), sl]
            xc, r, i, a, mult = xc_ref[:, sl], r_ref[:, sl], i_ref[:, sl], a_ref[:, sl], mult_ref[:, sl]
            d_a = jnp.where(live, lam_t * h_prev, 0.0)
            d_mult = jnp.where(live, lam_t * (i * xc), 0.0)
            d_ixc = lam_t * mult
            d_i = d_ixc * xc
            d_log_a = d_a * a - d_mult * (a * a / mult)
            d_za = d_log_a * (-LRU_C * sp[:, sl]) * (r * (1.0 - r))
            d_zx = d_i * (i * (1.0 - i))
            dlam_ref[:, sl] += jnp.sum(d_log_a * r, axis=0, keepdims=True) * (LRU_C * _sigmoid(-lam_v[:, sl]))
            dba_ref[:, sl] += jnp.sum(d_za, axis=0, keepdims=True)
            dbx_ref[:, sl] += jnp.sum(d_zx, axis=0, keepdims=True)
            xcb, dzab, dzxb = xc.astype(BF16), d_za.astype(BF16), d_zx.astype(BF16)
            dwa_ref[j] += _dot_tn(xcb, dzab)
            dwx_ref[j] += _dot_tn(xcb, dzxb)
            d_xc = d_ixc * i + (_dot_nt(dzab, wa_ref[j]) + _dot_nt(dzxb, wx_ref[j]))
            dcb_ref[:, sl] += jnp.sum(d_xc, axis=0, keepdims=True)
            for k in range(CONV_W):
                tap = xr[:, sl] if k == CONV_W - 1 else xbuf[pl.ds(8 - (CONV_W - 1 - k), tr), sl]
                dcw_ref[k:k + 1, sl] += jnp.sum(d_xc * tap, axis=0, keepdims=True)
            dbuf[0:tr, sl] = d_xc
            d_xr = d_xc * cw[CONV_W - 1:CONV_W, sl]
            for k in range(CONV_W - 1):
                d_xr = d_xr + dbuf[pl.ds(CONV_W - 1 - k, tr), sl] * cw[k:k + 1, sl]
            dxc_head[:, sl] = d_xc[0:8, :]
            db_ref[:, sl] = d_xr.astype(BF16)

    rev = lambda s: nt - 1 - s
    blk = lambda cb: pl.BlockSpec((tr, CB), lambda s, cb=cb: (rev(s), cb))
    halo = lambda w, cb: pl.BlockSpec((8, w), lambda s, cb=cb: (jnp.maximum(rev(s) * hb - 1, 0), cb))
    tok = lambda w: pl.BlockSpec((tr, w), lambda s: (rev(s), 0))
    row = lambda w: pl.BlockSpec((1, w), lambda s: (0, 0))
    full3 = pl.BlockSpec((RNN_BLOCKS, RNN_BW, RNN_BW), lambda s: (0, 0, 0))
    cwspec = pl.BlockSpec((CONV_W, D_MODEL), lambda s: (0, 0))
    vec = jax.ShapeDtypeStruct((1, D_MODEL), F32)
    gate_w = jax.ShapeDtypeStruct((RNN_BLOCKS, RNN_BW, RNN_BW), F32)
    out_shape = (jax.ShapeDtypeStruct((T, 2 * D_MODEL), BF16), jax.ShapeDtypeStruct((CONV_W, D_MODEL), F32), vec,
                 gate_w, gate_w, vec, vec, vec)
    big = lambda: pltpu.VMEM((tr, D_MODEL), F32)
    ext = lambda: pltpu.VMEM((tr + 8, D_MODEL), F32)
    return pl.pallas_call(
        body, out_shape=out_shape, grid=(nt,), name="rnn_backward",
        in_specs=[blk(CB_XR), blk(CB_XR + 1), halo(CB, CB_XR), halo(CB, CB_XR + 1), blk(CB_GR), blk(CB_GR + 1),
                  pl.BlockSpec((tr, 1), lambda s: (rev(s), 0)), tok(D_MODEL), halo(D_MODEL, 0)] + [tok(D_MODEL)] * 6
        + [cwspec, full3, full3, row(D_MODEL)],
        out_specs=(tok(2 * D_MODEL), cwspec, row(D_MODEL), full3, full3, row(D_MODEL), row(D_MODEL), row(D_MODEL)),
        scratch_shapes=[ext(), ext(), ext(), big(), big(), pltpu.VMEM((8, D_MODEL), F32), pltpu.VMEM((8, D_MODEL), F32)],
        compiler_params=_cp("arbitrary"),
    )(proj, proj, proj, proj, proj, proj, pos_col, h_rnn, h_rnn, *saved, d_y, *_in_hbm(conv_w, rwa, rwx, lam))


def _input_backward(pieces, w_in, x, dx2, mod_row, norm_g):
    T = x.shape[0]
    tm = min(T, 512)
    n = len(pieces)

    def body(*refs):
        d_refs = refs[:n]
        w_ref, x_ref, dx2_ref, mod_ref, g_ref, gx_ref, dshift_ref, dscale_ref, dg_ref = refs[n:]
        i = pl.program_id(0)
        dh = None
        for d_ref, (_, start, count) in zip(d_refs, pieces):
            part = _dot_nt(d_ref[...], w_ref[:, start * CB:(start + count) * CB])
            dh = part if dh is None else dh + part

        @pl.when(i == 0)
        def _():
            dshift_ref[...] = jnp.zeros_like(dshift_ref)
            dscale_ref[...] = jnp.zeros_like(dscale_ref)
            dg_ref[...] = jnp.zeros_like(dg_ref)

        xf = x_ref[...]
        r1 = _rms(xf)
        xn = xf * r1
        gn = g_ref[...]
        s1 = 1.0 + mod_ref[:, D_MODEL:2 * D_MODEL]
        dshift_ref[...] += jnp.sum(dh, axis=0, keepdims=True)
        dscale_ref[...] += jnp.sum(dh * (xn * gn), axis=0, keepdims=True)
        dg_ref[...] += jnp.sum(dh * s1 * xn, axis=0, keepdims=True)
        dxn = dh * s1 * gn
        gx_ref[...] = dx2_ref[...] + r1 * (dxn - xn * jnp.mean(dxn * xn, axis=-1, keepdims=True))

    tok = lambda w: pl.BlockSpec((tm, w), lambda i: (i, 0))
    row = lambda w: pl.BlockSpec((1, w), lambda i: (0, 0))
    vec = jax.ShapeDtypeStruct((1, D_MODEL), F32)
    return pl.pallas_call(
        body, out_shape=(jax.ShapeDtypeStruct((T, D_MODEL), F32), vec, vec, vec), grid=(T // tm,), name="input_backward",
        in_specs=[tok(c * CB) for _, _, c in pieces]
        + [pl.BlockSpec((D_MODEL, IN_W), lambda i: (0, 0), pipeline_mode=pl.Buffered(1)), tok(D_MODEL), tok(D_MODEL),
           row(ADA_W), row(D_MODEL)],
        out_specs=(tok(D_MODEL), row(D_MODEL), row(D_MODEL), row(D_MODEL)),
        compiler_params=_cp("arbitrary"),
    )(*[p[0] for p in pieces], w_in, x, dx2, *_in_hbm(mod_row, norm_g))


def _weight_grad(a, pieces, tag, a_is_transposed=False):
    M, T = a.shape if a_is_transposed else a.shape[::-1]
    n_blocks = sum(count for _, _, count in pieces)
    n = len(pieces)
    contract = _dot if a_is_transposed else _dot_tn

    def body(*refs):
        a_ref, b_refs, o_ref = refs[0], refs[1:1 + n], refs[-1]
        j = pl.program_id(0)
        for b_ref, (_, start, count) in zip(b_refs, pieces):
            @pl.when((j >= start) & (j < start + count))
            def _(b_ref=b_ref):
                o_ref[...] = contract(a_ref[...], b_ref[...])

    def piece_spec(start, count):
        return pl.BlockSpec((T, CB), lambda j: (0, jnp.clip(j - start, 0, count - 1)))

    return pl.pallas_call(
        body, out_shape=jax.ShapeDtypeStruct((M, n_blocks * CB), F32), grid=(n_blocks,), name=f"weight_grad_{tag}",
        in_specs=[pl.BlockSpec(a.shape, lambda j: (0, 0), pipeline_mode=pl.Buffered(1))] + [piece_spec(s, c) for _, s, c in pieces],
        out_specs=pl.BlockSpec((M, CB), lambda j: (0, j)), compiler_params=_cp("arbitrary"),
    )(a, *[p[0] for p in pieces])


def _adamw(w, g, m, v):
    m = ADAM_B1 * m + (1.0 - ADAM_B1) * g
    v = ADAM_B2 * v + (1.0 - ADAM_B2) * (g * g)
    m_hat = m / (1.0 - ADAM_B1 ** ADAM_STEP)
    v_hat = v / (1.0 - ADAM_B2 ** ADAM_STEP)
    delta = -ADAM_LR * (m_hat / (jnp.sqrt(v_hat) + ADAM_EPS) + ADAM_WD * w)
    return delta, m, v


def _sum_landed(kind, owns, lands, where, tag):
    n = len(owns)
    land = lands[0]
    if kind == "in":
        R, C = land.shape[1:]
        tr = 256
        grid = (R // tr,)
        own_spec = pl.BlockSpec((tr, C), lambda i, w: (i, w[0]))
        land_spec = pl.BlockSpec((3, tr, C), lambda i, w: (0, i, 0))
        out_spec = pl.BlockSpec((1, tr, C), lambda i, w: (w[1], i, 0))
        out_shape = (2, R, C)
        pick = lambda ref: ref[...]
    elif kind == "sq":
        R, C = land.shape[1:]
        grid = (1,)
        own_spec = pl.BlockSpec((1, R, C), lambda i, w: (w[0], 0, 0))
        land_spec = pl.BlockSpec((3, R, C), lambda i, w: (0, 0, 0))
        out_spec = pl.BlockSpec((1, R, C), lambda i, w: (w[1], 0, 0))
        out_shape = (2, R, C)
        pick = lambda ref: ref[0]
    else:
        B, R, C = land.shape[1:]
        grid = (1,)
        own_spec = pl.BlockSpec((B, 1, R, C), lambda i, w: (0, w[0], 0, 0))
        land_spec = pl.BlockSpec((3, B, R, C), lambda i, w: (0, 0, 0, 0))
        out_spec = pl.BlockSpec((B, 1, R, C), lambda i, w: (0, w[1], 0, 0))
        out_shape = (B, 2, R, C)
        pick = lambda ref: ref[:, 0]

    def body(w_ref, *refs):
        for k in range(n):
            own_ref, l_ref, o_ref = refs[k], refs[n + k], refs[2 * n + k]
            total = ((pick(own_ref) + l_ref[0].astype(F32)) + l_ref[1].astype(F32)) + l_ref[2].astype(F32)
            if kind == "rg":
                o_ref[:, 0] = total
            else:
                o_ref[0] = total

    grid_spec = pltpu.PrefetchScalarGridSpec(num_scalar_prefetch=1, grid=grid, in_specs=[own_spec] * n + [land_spec] * n,
                                             out_specs=(out_spec,) * n)
    return list(pl.pallas_call(
        body, out_shape=(jax.ShapeDtypeStruct(out_shape, F32),) * n, grid_spec=grid_spec, name=f"sum_landed_{tag}",
        compiler_params=_cp("parallel"),
    )(where, *owns, *lands))


def _adamw_shard(gs, ws, ms, vs, tag):
    n = len(ws)
    R, C = ws[0].shape
    tr = min(R, 256 if n == 1 else 64)

    def body(*refs):
        for k in range(n):
            g = refs[k][...]
            d, nm, nv = _adamw(refs[n + k][...], g, refs[2 * n + k][...], refs[3 * n + k][...])
            out = refs[4 * n + 4 * k:4 * n + 4 * k + 4]
            out[0][...] = g
            out[1][...] = d
            out[2][...] = nm
            out[3][...] = nv

    spec = pl.BlockSpec((tr, C), lambda i: (i, 0))
    sds = jax.ShapeDtypeStruct((R, C), F32)
    outs = pl.pallas_call(
        body, out_shape=(sds,) * (4 * n), grid=(R // tr,), name=f"adamw_{tag}",
        in_specs=[spec] * (4 * n), out_specs=(spec,) * (4 * n), compiler_params=_cp("parallel"),
    )(*gs, *_in_hbm(*ws, *ms, *vs))
    return [outs[4 * k:4 * k + 4] for k in range(n)]


def _adamw_w_ada(c_t, dmod_cols, w, m, v):
    R, C = w.shape

    def body(ct_ref, dm_ref, w_ref, m_ref, v_ref, g_ref, d_ref, nm_ref, nv_ref):
        g = _dot(ct_ref[...].astype(BF16), dm_ref[...].astype(BF16))
        d, nm, nv = _adamw(w_ref[...], g, m_ref[...], v_ref[...])
        g_ref[...] = g
        d_ref[...] = d
        nm_ref[...] = nm
        nv_ref[...] = nv

    tr = 256
    spec = pl.BlockSpec((tr, C), lambda i: (i, 0))
    sds = jax.ShapeDtypeStruct((R, C), F32)
    return pl.pallas_call(
        body, out_shape=(sds,) * 4, grid=(R // tr,), name="adamw_w_ada",
        in_specs=[pl.BlockSpec((tr, 128), lambda i: (i, 0)), pl.BlockSpec((128, C), lambda i: (0, 0))] + [spec] * 3,
        out_specs=(spec,) * 4, compiler_params=_cp("parallel"),
    )(c_t, dmod_cols, w, m, v)


def _adamw_small(small_all, ws, ms, vs):
    def body(s_ref, w_ref, m_ref, v_ref, g_ref, d_ref, nm_ref, nv_ref):
        g = s_ref[0]
        for b in range(1, N_DEV):
            g = g + s_ref[b]
        d, nm, nv = _adamw(w_ref[...], g, m_ref[...], v_ref[...])
        g_ref[...] = g
        d_ref[...] = d
        nm_ref[...] = nm
        nv_ref[...] = nv

    sds = jax.ShapeDtypeStruct((SMALL_ROWS, D_MODEL), F32)
    return pl.pallas_call(
        body, out_shape=(sds,) * 4, name="adamw_small", in_specs=[VMEM_SPEC] * 4, out_specs=(VMEM_SPEC,) * 4,
        compiler_params=pltpu.CompilerParams(vmem_limit_bytes=VMEM_LIMIT_V7X),
    )(small_all, ws, ms, vs)


ROW_MOD, ROW_NORM_G, ROW_CONV_B, ROW_BA, ROW_BX, ROW_LAM, ROW_FINAL_G, ROW_SINKS, ROW_CONV_W, ROW_LOSS = 0, 3, 4, 5, 6, 7, 8, 9, 10, 14


def _pack_small(b_ada, norm_g, conv_b, ba, bx, lam, final_g, sinks, conv_w_full, loss_row=None):
    lane_pad = lambda a: jnp.pad(a.reshape(1, -1), ((0, 0), (0, D_MODEL - a.size)))
    rows = [b_ada.reshape(3, D_MODEL), norm_g, conv_b, ba, bx, lam, final_g.reshape(1, D_MODEL), lane_pad(sinks), conv_w_full,
            jnp.zeros((1, D_MODEL), F32) if loss_row is None else lane_pad(loss_row),
            jnp.zeros((SMALL_ROWS - ROW_LOSS - 1, D_MODEL), F32)]
    return jnp.concatenate([r.astype(F32) for r in rows], axis=0)


def kernel(x, c, positions, w_ada, b_ada, norm_g, w_in, attn_sinks, conv_w, conv_b, rg_wa, rg_ba, rg_wx, rg_bx, rg_lambda, w_attn_proj, w_rnn_proj, w_out, final_g, loss_target, m_w_ada, m_b_ada, m_norm_g, m_w_in, m_attn_sinks, m_conv_w, m_conv_b, m_rg_wa, m_rg_ba, m_rg_wx, m_rg_bx, m_rg_lambda, m_w_attn_proj, m_w_rnn_proj, m_w_out, m_final_g, v_w_ada, v_b_ada, v_norm_g, v_w_in, v_attn_sinks, v_conv_w, v_conv_b, v_rg_wa, v_rg_ba, v_rg_wx, v_rg_bx, v_rg_lambda, v_w_attn_proj, v_w_rnn_proj, v_w_out, v_final_g):
    T = x.shape[1]
    my_chip = lax.axis_index("x") * 2 + lax.axis_index("y")
    my_dev = my_chip * 2 + lax.axis_index("c")
    x2d, tgt = x[0], loss_target[0]
    pos_col = positions.reshape(T, 1)

    chip_idx = my_chip.reshape(1).astype(jnp.int32)
    c_idx = lax.axis_index("c").reshape(1).astype(jnp.int32)
    sq_place = ((D_MODEL, D_MODEL), (SHARD_ROWS, D_MODEL), lambda chip: (chip, 0))
    rg_place = ((RNN_BLOCKS, RNN_BW, RNN_BW), (RNN_BLOCKS, SHARD_RG, RNN_BW), lambda chip: (0, chip, 0))
    in_place = ((D_MODEL, IN_W), (D_MODEL, SHARD_IN), lambda chip: (0, chip))
    placed = _cast_place([w_in[0], w_attn_proj[0], w_rnn_proj[0], w_out[0], rg_wa[0], rg_wx[0]], chip_idx,
                         [in_place, sq_place, sq_place, sq_place, rg_place, rg_place])
    cw_chips, c_all, mod_chips = _gather_mod(c.reshape(1, 1, D_MODEL), w_ada[0], conv_w[0])
    g_ssems, g_rsems, fulls, g_token = _gather_start([p.reshape(s) for p, s in zip(placed, FULL_SHAPES)], mod_chips)
    conv_w_f = jnp.transpose(cw_chips, (1, 0, 2)).reshape(CONV_W, D_MODEL)
    mod_all = jnp.transpose(mod_chips, (1, 0, 2)).reshape(N_DEV, ADA_W) + b_ada
    mod_row = lax.dynamic_slice_in_dim(mod_all, my_dev, 1, axis=0) + g_token[0:1, 0:1]

    h, h_t, tabs = _prenorm(x2d, mod_row, norm_g, pos_col)
    w_in_v = fulls[0]
    proj = _in_projection(h, w_in_v.reshape(D_MODEL, IN_W), chip_idx, None, "own")
    for k, mask in enumerate(CHIP_MASKS):
        w_in_v = _gather_wait(g_ssems[k], g_rsems[k], [w_in_v], [0], proj, f"w_in_{k}")[0]
        w_in_v = _forward_halves([w_in_v], [(0, 0, k)], f"w_in_{k}")[0]
        from_chip = (chip_idx ^ (mask >> 1)).astype(jnp.int32)
        proj = _in_projection(h, w_in_v.reshape(D_MODEL, IN_W), from_chip, proj, f"from_{k}")
    w_in_f = w_in_v.reshape(D_MODEL, IN_W)
    rest = _gather_wait(g_ssems[3], g_rsems[3], list(fulls[1:]), [1, 2, 3, 4, 5], proj, "rest")
    fr_ssem, fr_rsem, rest, fr_token = _forward_rest_start(rest)
    y_attn, qr_b, kr_b, p_all = _attn_forward(proj, tabs, attn_sinks + fr_token[0, 0])
    rest = _gather_wait(fr_ssem, fr_rsem, rest, [1, 2, 3, 4, 5], y_attn, "rest_forwarded")
    wap_f, wrp_f, wo_f = (g.reshape(D_MODEL, D_MODEL) for g in rest[0:3])
    rwa_f, rwx_f = (g.reshape(RNN_BLOCKS, RNN_BW, RNN_BW) for g in rest[3:5])
    y_rnn, h_rnn, *rnn_saved = _rnn_forward(proj, pos_col, conv_w_f, conv_b, rwa_f, rwx_f, rg_ba, rg_bx, rg_lambda)
    (dx2, merged, d_o, d_pa, d_pr, d_ya, d_yr, d_c, d_final_g, d_gate, loss_vec) = _merge_and_head(
        x2d, tgt, y_attn, y_rnn, proj, wap_f, wrp_f, wo_f, mod_row, final_g.reshape(1, D_MODEL))

    sq = (N_CHIPS, 2, SHARD_ROWS // 2, D_MODEL)
    rg = (RNN_BLOCKS, N_CHIPS, 2, SHARD_RG // 2, RNN_BW)
    rg_flat = (RNN_BLOCKS * N_CHIPS, 2, SHARD_RG // 2, RNN_BW)

    def chip_sum_and_start(views, axes, flat, unflat, tags_, kinds_, group, from_sib=None):
        if from_sib is None:
            from_sib = _swap_halves(views, axes)
        exact, rounded = [None] * len(views), [None] * len(views)
        for shape in dict.fromkeys(flat):
            ids = [k for k, f in enumerate(flat) if f == shape]
            ex, ro = _presum([views[k].reshape(shape) for k in ids],
                             [from_sib[k].reshape(shape[:1] + shape[2:]) for k in ids], c_idx, tags_[ids[0]])
            for k, e, r in zip(ids, ex, ro):
                exact[k], rounded[k] = e.reshape(unflat[k]), r.reshape(unflat[k])
        return _exchange_start(rounded, kinds_, group), exact

    g_ap = _weight_grad(y_attn, [(d_pa, 0, 2)], "w_attn_proj")
    g_rp = _weight_grad(y_rnn, [(d_pr, 0, 2)], "w_rnn_proj")
    g_o = _weight_grad(merged, [(d_o, 0, 2)], "w_out")
    sq_half = (N_CHIPS, SHARD_ROWS // 2, D_MODEL)
    views1 = [g_ap.reshape(sq), g_rp.reshape(sq), g_o.reshape(sq)]
    sw_ssems, sw_rsems, views1, sib1, sw_token = _swap_halves_start(views1, [1, 1, 1], "proj")
    d_q, d_kv, d_ga, d_sinks = _attn_backward(proj, qr_b, kr_b, p_all, d_ya, tabs, sw_token[0:1, 0:16])
    views1, sib1 = _swap_halves_wait(sw_ssems, sw_rsems, views1, sib1, d_q, "proj")
    started1, own1 = chip_sum_and_start(views1, [1, 1, 1], [sq] * 3, [sq_half] * 3,
                                        ["w_attn_proj", "w_rnn_proj", "w_out"], ["sq"] * 3, "proj", from_sib=sib1)
    d_b, d_conv_w, d_conv_b, d_rwa, d_rwx, d_ba, d_bx, d_lam = _rnn_backward(
        proj, pos_col, h_rnn, rnn_saved, d_yr, conv_w_f, rwa_f, rwx_f, rg_lambda + started1[4][0:1, 0:1])
    pieces = [(d_q, CB_Q, 2), (d_kv, CB_KV, 1), (d_ga, CB_GA, 2), (d_b, CB_XR, 4), (d_c, CB_MA, 4)]
    g_in = _weight_grad(h_t, pieces, "w_in", a_is_transposed=True)
    started2, own2 = chip_sum_and_start(
        [g_in.reshape(2, D_MODEL // 2, IN_W), d_rwa.reshape(rg), d_rwx.reshape(rg)], [0, 2, 2],
        [(1, 2, D_MODEL // 2, IN_W), rg_flat, rg_flat],
        [(D_MODEL // 2, IN_W), (RNN_BLOCKS, N_CHIPS, SHARD_RG // 2, RNN_BW), (RNN_BLOCKS, N_CHIPS, SHARD_RG // 2, RNN_BW)],
        ["w_in", "rg_wa", "rg_wx"], ["in", "rg", "rg"], "in")
    grad_x, d_shift, d_scale, d_norm_g = _input_backward(pieces, w_in_f, x2d, dx2, mod_row + started2[4][0, 0], norm_g)

    d_mod = jnp.concatenate([d_shift, d_scale, d_gate], axis=1)
    small = _pack_small(d_mod, d_norm_g, d_conv_b, d_ba, d_bx, d_lam, d_final_g, d_sinks[:, :N_HEADS], d_conv_w, loss_vec)
    slabs = lax.dynamic_update_slice(jnp.zeros((N_DEV, SMALL_ROWS, D_MODEL), F32), small[None], (my_dev, 0, 0))
    gs_ssem, gs_rsem, slabs, gs_token = _gather_small_start(slabs)
    _, lands1 = _exchange_wait(*started1[:4], gs_token, "proj")
    _, lands2 = _exchange_wait(*started2[:4], gs_token, "in")
    tags = ["w_in", "w_attn_proj", "w_rnn_proj", "w_out", "rg_wa", "rg_wx"]
    chip_sums = [own2[0]] + list(own1) + list(own2[1:])
    lands = [lands2[0]] + list(lands1) + list(lands2[1:])
    where = jnp.concatenate([chip_idx, c_idx])
    kinds = ["in", "sq", "sq", "sq", "rg", "rg"]
    groups = [[0], [1, 2, 3], [4, 5]]
    halves = [None] * 6
    for ids in groups:
        for i, half in zip(ids, _sum_landed(kinds[ids[0]], [chip_sums[i] for i in ids], [lands[i] for i in ids], where,
                                            tags[ids[0]])):
            halves[i] = half
    half_axes = [0, 0, 0, 0, 1, 1]
    asm_ssems, asm_rsems, halves, asm_token = _assemble_start(halves, half_axes)
    res = {}
    small_all = _gather_small_wait(gs_ssem, gs_rsem, slabs, asm_token)
    dmod_all = small_all[:, ROW_MOD:ROW_MOD + 3, :].reshape(N_DEV, ADA_W)
    dmod_cols = lax.dynamic_slice_in_dim(dmod_all, my_chip * SHARD_ADA, SHARD_ADA, axis=1)
    c_t = jnp.pad(jnp.transpose(c_all.reshape(N_DEV, D_MODEL)), ((0, 0), (0, 128 - N_DEV)))
    dmod_cols = jnp.pad(dmod_cols, ((0, 128 - N_DEV), (0, 0)))
    res["w_ada"] = [o.reshape(w_ada.shape) for o in _adamw_w_ada(c_t, dmod_cols, w_ada[0], m_w_ada[0], v_w_ada[0])]

    def full_conv(a):
        return lax.dynamic_update_slice_in_dim(jnp.zeros((CONV_W, D_MODEL), F32), a[0], my_chip * (D_MODEL // N_CHIPS), axis=1)

    packed = [_pack_small(p[0], p[1], p[2], p[3], p[4], p[5], p[6], p[7], full_conv(p[8])) for p in (
        (b_ada, norm_g, conv_b, rg_ba, rg_bx, rg_lambda, final_g, attn_sinks, conv_w),
        (m_b_ada, m_norm_g, m_conv_b, m_rg_ba, m_rg_bx, m_rg_lambda, m_final_g, m_attn_sinks, m_conv_w),
        (v_b_ada, v_norm_g, v_conv_b, v_rg_ba, v_rg_bx, v_rg_lambda, v_final_g, v_attn_sinks, v_conv_w))]
    small_out = _adamw_small(small_all, *packed)

    grads = _assemble_wait(asm_ssems, asm_rsems, halves, half_axes, small_out[0])
    shapes2d = [(D_MODEL, SHARD_IN), (SHARD_ROWS, D_MODEL), (SHARD_ROWS, D_MODEL), (SHARD_ROWS, D_MODEL),
                (RNN_BLOCKS * SHARD_RG, RNN_BW), (RNN_BLOCKS * SHARD_RG, RNN_BW)]
    big_w = [w_in, w_attn_proj, w_rnn_proj, w_out, rg_wa, rg_wx]
    big_m = [m_w_in, m_w_attn_proj, m_w_rnn_proj, m_w_out, m_rg_wa, m_rg_wx]
    big_v = [v_w_in, v_w_attn_proj, v_w_rnn_proj, v_w_out, v_rg_wa, v_rg_wx]
    for ids in groups:
        flat2d = lambda arrs: [arrs[i].reshape(shapes2d[i]) for i in ids]
        outs = _adamw_shard(flat2d(grads), flat2d(big_w), flat2d(big_m), flat2d(big_v), tags[ids[0]])
        for i, four in zip(ids, outs):
            res[tags[i]] = [o.reshape(big_w[i].shape) for o in four]

    def unpack(slab):
        cw = lax.dynamic_slice_in_dim(slab[ROW_CONV_W:ROW_CONV_W + CONV_W], my_chip * (D_MODEL // N_CHIPS),
                                      D_MODEL // N_CHIPS, axis=1)
        return {
            "b_ada": slab[ROW_MOD:ROW_MOD + 3].reshape(1, ADA_W), "norm_g": slab[ROW_NORM_G:ROW_NORM_G + 1],
            "conv_b": slab[ROW_CONV_B:ROW_CONV_B + 1], "rg_ba": slab[ROW_BA:ROW_BA + 1], "rg_bx": slab[ROW_BX:ROW_BX + 1],
            "rg_lambda": slab[ROW_LAM:ROW_LAM + 1], "final_g": slab[ROW_FINAL_G], "attn_sinks": slab[ROW_SINKS:ROW_SINKS + 1, :N_HEADS],
            "conv_w": cw[None],
        }

    small_res = [unpack(s) for s in small_out]
    order = ["w_ada", "b_ada", "norm_g", "w_in", "attn_sinks", "conv_w", "conv_b", "rg_wa", "rg_ba", "rg_wx", "rg_bx",
             "rg_lambda", "w_attn_proj", "w_rnn_proj", "w_out", "final_g"]
    loss = small_out[0][ROW_LOSS, 0]
    outs = [loss, grad_x[None]]
    for kind in range(4):
        for name in order:
            outs.append(res[name][kind] if name in res else small_res[kind][name])
    return tuple(outs)
```

```python
import numpy as np
import jax
import jax.numpy as jnp
from jax import lax
from jax.experimental import pallas as pl
from jax.experimental.pallas import tpu as pltpu

F32 = jnp.float32
BF16 = jnp.bfloat16

D_MODEL = 1024
N_HEADS = 16
N_KV = 4
HEAD_DIM = 64
GROUP = N_HEADS // N_KV
BLOCK = 128
KV_W = N_KV * HEAD_DIM
ROT_HALF = 8
ROPE_THETA = 500000.0
ATTN_SCALE = 0.125
RNN_BLOCKS = 4
RNN_BW = 256
CONV_W = 4
LRU_C = 8.0
NORM_EPS = 1e-6
IN_W = 6656
CB = 512
N_CB = IN_W // CB
CB_Q, CB_KV, CB_GA, CB_XR, CB_GR, CB_MA, CB_MR = 0, 2, 3, 5, 7, 9, 11
V_COL_BLOCK = 5
N_CHIPS = 4
N_DEV = 8
SHARD_IN = IN_W // N_CHIPS
SHARD_ROWS = D_MODEL // N_CHIPS
SHARD_RG = RNN_BW // N_CHIPS
ADA_W = 3 * D_MODEL
SHARD_ADA = ADA_W // N_CHIPS
SMALL_ROWS = 16

ADAM_LR = 0.001
ADAM_B1 = 0.9
ADAM_B2 = 0.999
ADAM_EPS = 1e-08
ADAM_WD = 0.01
ADAM_STEP = 10

VMEM_LIMIT_V7X = 52 * 1024 * 1024
MESH = pl.DeviceIdType.MESH
ANY = pl.BlockSpec(memory_space=pl.ANY)
VMEM_SPEC = pl.BlockSpec(memory_space=pltpu.VMEM)


def _in_hbm(*arrays):
    return [pltpu.with_memory_space_constraint(a, pltpu.HBM) for a in arrays]


def _cp(*sem):
    return pltpu.CompilerParams(dimension_semantics=sem if sem else None, vmem_limit_bytes=VMEM_LIMIT_V7X)


def _dot(a, b):
    return jnp.dot(a, b, preferred_element_type=F32)


def _dot_nt(a, b):
    return lax.dot_general(a, b, (((1,), (1,)), ((), ())), preferred_element_type=F32)


def _dot_tn(a, b):
    return lax.dot_general(a, b, (((0,), (0,)), ((), ())), preferred_element_type=F32)


def _sigmoid(z):
    return 1.0 / (1.0 + jnp.exp(-z))


def _softplus(z):
    u = jnp.exp(-jnp.abs(z))
    log1p_u = jnp.where(u < 1e-3, u * (1.0 - u * (0.5 - u * (1.0 / 3.0))), jnp.log(1.0 + u))
    return jnp.maximum(z, 0.0) + log1p_u


def _rms(xf):
    return lax.rsqrt(jnp.mean(xf * xf, axis=-1, keepdims=True) + NORM_EPS)


def _me():
    return lax.axis_index("x"), lax.axis_index("y"), lax.axis_index("c")


def _peer(mask):
    x, y, c = _me()
    fx, fy, fc = (mask >> 2) & 1, (mask >> 1) & 1, mask & 1
    return (x ^ fx if fx else x, y ^ fy if fy else y, c ^ fc if fc else c)


def _chip_of(pos):
    return pos[0] * 2 + pos[1]


SIBLING_COLLECTIVE_ID = 0
SIBLING_ONLY = pltpu.CompilerParams(collective_id=SIBLING_COLLECTIVE_ID)


def _sibling_handshake():
    barrier = pltpu.get_barrier_semaphore()
    pl.semaphore_signal(barrier, inc=1, device_id=_peer(1), device_id_type=MESH)
    pl.semaphore_wait(barrier, 1)


CHIP_MASKS = (4, 2, 6)
ALL_MASKS = (1, 2, 3, 4, 5, 6, 7)


HBM_SPEC = pl.BlockSpec(memory_space=pltpu.HBM)
SEM_SPEC = pl.BlockSpec(memory_space=pltpu.SEMAPHORE)
SPLIT_COPY = pltpu.CompilerParams(has_side_effects=pltpu.SideEffectType.DATAFLOW_SIDE_EFFECTING)
N_BIG = 6
FULL_SHAPES = (
    (2, D_MODEL // 2, IN_W),
    (N_CHIPS, 2, SHARD_ROWS // 2, D_MODEL), (N_CHIPS, 2, SHARD_ROWS // 2, D_MODEL), (N_CHIPS, 2, SHARD_ROWS // 2, D_MODEL),
    (RNN_BLOCKS, N_CHIPS, 2, SHARD_RG // 2, RNN_BW), (RNN_BLOCKS, N_CHIPS, 2, SHARD_RG // 2, RNN_BW),
)


def _slot(full, idx, chip, half):
    if idx == 0:
        return full.at[half, :, pl.ds(pl.multiple_of(chip * SHARD_IN, 128), SHARD_IN)]
    return full.at[chip, half] if idx in (1, 2, 3) else full.at[:, chip, half]


def _three_halves(full, idx):
    return full.at[pl.ds(0, 3), 0] if idx in (1, 2, 3) else full.at[:, pl.ds(0, 3), 0]


def _gather_start(fulls, after):
    def body(*refs):
        full_refs = refs[:N_BIG]
        ssems, rsems = refs[N_BIG + 1:N_BIG + 5], refs[N_BIG + 5:N_BIG + 9]
        token = refs[2 * N_BIG + 9]
        me = _me()
        my_chip = _chip_of(me)
        for idx in range(N_BIG):
            for k, mask in enumerate(CHIP_MASKS):
                pair = k if idx == 0 else 3
                mine = _slot(full_refs[idx], idx, my_chip, me[2])
                pltpu.make_async_remote_copy(src_ref=mine, dst_ref=mine, send_sem=ssems[pair], recv_sem=rsems[pair],
                                             device_id=_peer(mask), device_id_type=MESH).start()
        token[...] = jnp.zeros_like(token)

    sem = pltpu.SemaphoreType.DMA(())
    out_shape = (sem,) * 8 + tuple(pltpu.HBM(f.shape, f.dtype) for f in fulls) + (jax.ShapeDtypeStruct((8, 128), F32),)
    outs = pl.pallas_call(
        body, out_shape=out_shape, name="gather_start",
        in_specs=[HBM_SPEC] * N_BIG + [ANY], out_specs=tuple([SEM_SPEC] * 8 + [HBM_SPEC] * N_BIG + [VMEM_SPEC]),
        input_output_aliases={i: 8 + i for i in range(N_BIG)}, compiler_params=SPLIT_COPY,
    )(*[pltpu.with_memory_space_constraint(f, pltpu.HBM) for f in fulls], after)
    return outs[0:4], outs[4:8], outs[8:8 + N_BIG], outs[8 + N_BIG]


def _gather_wait(ssem, rsem, arrays, idxs, after, tag):
    n = len(arrays)

    def body(*refs):
        full_refs, ssem_ref, rsem_ref = refs[:n], refs[n], refs[n + 1]
        me = _me()
        for full, idx in zip(full_refs, idxs):
            region = _slot(full, 0, _chip_of(me), me[2]) if idx == 0 else _three_halves(full, idx)
            arrived = pltpu.make_async_remote_copy(
                src_ref=region, dst_ref=region, send_sem=ssem_ref, recv_sem=rsem_ref, device_id=me, device_id_type=MESH)
            arrived.wait_send()
            arrived.wait_recv()

    outs = pl.pallas_call(
        body, out_shape=tuple(pltpu.HBM(a.shape, a.dtype) for a in arrays), name=f"gather_wait_{tag}",
        in_specs=[HBM_SPEC] * n + [SEM_SPEC, SEM_SPEC, ANY], out_specs=tuple([HBM_SPEC] * n),
        input_output_aliases={i: i for i in range(n)}, compiler_params=SPLIT_COPY,
    )(*arrays, ssem, rsem, after)
    return list(outs)


def _forward_halves(arrays, items, tag):
    n, m = len(arrays), len(items)

    def body(*refs):
        outs, ssem, rsem = refs[n:2 * n], refs[2 * n], refs[2 * n + 1]
        me = _me()
        sib = _peer(1)
        _sibling_handshake()
        cps = []
        for j, (pos, idx, k) in enumerate(items):
            chip = _chip_of(_peer(CHIP_MASKS[k]))
            cp = pltpu.make_async_remote_copy(
                src_ref=_slot(outs[pos], idx, chip, me[2]), dst_ref=_slot(outs[pos], idx, chip, me[2]),
                send_sem=ssem.at[j], recv_sem=rsem.at[j], device_id=sib, device_id_type=MESH)
            cp.start()
            cps.append(cp)
        for j, (pos, idx, k) in enumerate(items):
            chip = _chip_of(_peer(CHIP_MASKS[k]))
            pltpu.make_async_remote_copy(
                src_ref=_slot(outs[pos], idx, chip, me[2]), dst_ref=_slot(outs[pos], idx, chip, 1 - me[2]),
                send_sem=ssem.at[j], recv_sem=rsem.at[j], device_id=sib, device_id_type=MESH).wait_recv()
        for cp in cps:
            cp.wait_send()

    outs = pl.pallas_call(
        body, out_shape=tuple(jax.ShapeDtypeStruct(a.shape, a.dtype) for a in arrays), name=f"forward_halves_{tag}",
        in_specs=[ANY] * n, out_specs=tuple([ANY] * n), input_output_aliases={i: i for i in range(n)},
        scratch_shapes=[pltpu.SemaphoreType.DMA((m,)), pltpu.SemaphoreType.DMA((m,))], compiler_params=SIBLING_ONLY,
    )(*arrays)
    return list(outs)


def _forward_rest_start(arrays):
    n = len(arrays)

    def body(*refs):
        ssem, rsem, outs, token = refs[n], refs[n + 1], refs[n + 2:2 * n + 2], refs[2 * n + 2]
        me = _me()
        sib = _peer(1)
        for pos in range(n):
            for mask in CHIP_MASKS:
                mine = _slot(outs[pos], pos + 1, _chip_of(_peer(mask)), me[2])
                pltpu.make_async_remote_copy(src_ref=mine, dst_ref=mine, send_sem=ssem, recv_sem=rsem,
                                             device_id=sib, device_id_type=MESH).start()
        token[...] = jnp.zeros_like(token)

    sem = pltpu.SemaphoreType.DMA(())
    out_shape = (sem, sem) + tuple(pltpu.HBM(a.shape, a.dtype) for a in arrays) + (jax.ShapeDtypeStruct((8, 128), F32),)
    outs = pl.pallas_call(
        body, out_shape=out_shape, name="forward_rest_start",
        in_specs=[HBM_SPEC] * n, out_specs=tuple([SEM_SPEC] * 2 + [HBM_SPEC] * n + [VMEM_SPEC]),
        input_output_aliases={i: 2 + i for i in range(n)}, compiler_params=SPLIT_COPY,
    )(*arrays)
    return outs[0], outs[1], list(outs[2:2 + n]), outs[2 + n]


def _gather_mod(c_row, w_ada_s, conv_w_s):
    def body(c_ref, wada_ref, cw_s, cw_f, call_ref, mod_ref, wsend, wrecv, lsem, csend, crecv, msend, mrecv):
        me = _me()
        my_chip = _chip_of(me)
        my_dev = my_chip * 2 + me[2]
        sends = []
        for k, mask in enumerate(CHIP_MASKS):
            cp = pltpu.make_async_remote_copy(src_ref=cw_s, dst_ref=cw_f.at[my_chip], send_sem=wsend.at[k], recv_sem=wrecv.at[k],
                                              device_id=_peer(mask), device_id_type=MESH)
            cp.start()
            sends.append(cp)
        local = [pltpu.make_async_copy(cw_s, cw_f.at[my_chip], lsem.at[0])]
        for cp in local:
            cp.start()

        call_ref[my_dev] = c_ref[0]
        csends = []
        for k, mask in enumerate(ALL_MASKS):
            cp = pltpu.make_async_remote_copy(
                src_ref=c_ref.at[0], dst_ref=call_ref.at[my_dev],
                send_sem=csend.at[k], recv_sem=crecv.at[k], device_id=_peer(mask), device_id_type=MESH)
            cp.start()
            csends.append(cp)
        for k, mask in enumerate(ALL_MASKS):
            frm = _peer(mask)
            pltpu.make_async_remote_copy(
                src_ref=c_ref.at[0], dst_ref=call_ref.at[_chip_of(frm) * 2 + frm[2]],
                send_sem=csend.at[k], recv_sem=crecv.at[k], device_id=frm, device_id_type=MESH).wait_recv()
        for cp in csends:
            cp.wait_send()

        c_all = call_ref[...].reshape(N_DEV, D_MODEL).astype(BF16)
        mod_ref[my_chip] = _dot(c_all, wada_ref[...].astype(BF16))
        msends = []
        for k, mask in enumerate(CHIP_MASKS):
            cp = pltpu.make_async_remote_copy(
                src_ref=mod_ref.at[my_chip], dst_ref=mod_ref.at[my_chip],
                send_sem=msend.at[k], recv_sem=mrecv.at[k], device_id=_peer(mask), device_id_type=MESH)
            cp.start()
            msends.append(cp)
        for k, mask in enumerate(CHIP_MASKS):
            frm = _peer(mask)
            pltpu.make_async_remote_copy(
                src_ref=mod_ref.at[my_chip], dst_ref=mod_ref.at[_chip_of(frm)],
                send_sem=msend.at[k], recv_sem=mrecv.at[k], device_id=frm, device_id_type=MESH).wait_recv()
        for cp in msends:
            cp.wait_send()

        for k, mask in enumerate(CHIP_MASKS):
            frm = _peer(mask)
            pltpu.make_async_remote_copy(src_ref=cw_s, dst_ref=cw_f.at[_chip_of(frm)], send_sem=wsend.at[k], recv_sem=wrecv.at[k],
                                         device_id=frm, device_id_type=MESH).wait_recv()
        for cp in sends:
            cp.wait_send()
        for cp in local:
            cp.wait()

    out_shape = (
        jax.ShapeDtypeStruct((N_CHIPS, CONV_W, D_MODEL // N_CHIPS), F32),
        jax.ShapeDtypeStruct((N_DEV, 1, D_MODEL), F32),
        jax.ShapeDtypeStruct((N_CHIPS, N_DEV, SHARD_ADA), F32),
    )
    return pl.pallas_call(
        body, out_shape=out_shape, name="gather_mod",
        in_specs=[VMEM_SPEC, VMEM_SPEC, ANY], out_specs=(ANY, VMEM_SPEC, VMEM_SPEC),
        scratch_shapes=[
            pltpu.SemaphoreType.DMA((3,)), pltpu.SemaphoreType.DMA((3,)), pltpu.SemaphoreType.DMA((1,)),
            pltpu.SemaphoreType.DMA((7,)), pltpu.SemaphoreType.DMA((7,)),
            pltpu.SemaphoreType.DMA((3,)), pltpu.SemaphoreType.DMA((3,)),
        ],
        compiler_params=pltpu.CompilerParams(vmem_limit_bytes=VMEM_LIMIT_V7X),
    )(c_row, w_ada_s, conv_w_s)


def _cast_place(shards, chip_idx, places):
    n = len(shards)

    def body(chip_ref, *refs):
        for s_ref, o_ref in zip(refs[:n], refs[n:]):
            o_ref[...] = s_ref[...].astype(BF16)

    grid_spec = pltpu.PrefetchScalarGridSpec(
        num_scalar_prefetch=1, grid=(1,),
        in_specs=[pl.BlockSpec(s.shape, lambda i, chip_ref, nd=s.ndim: (0,) * nd) for s in shards],
        out_specs=tuple(pl.BlockSpec(block, lambda i, chip_ref, im=im: im(chip_ref[0])) for _, block, im in places))
    return pl.pallas_call(
        body, out_shape=tuple(jax.ShapeDtypeStruct(full, BF16) for full, _, _ in places), grid_spec=grid_spec,
        name="cast_place", compiler_params=_cp("arbitrary"),
    )(chip_idx, *_in_hbm(*shards))


def _shard_of(ref, kind, chip):
    if kind == "in":
        return ref.at[:, pl.ds(pl.multiple_of(chip * SHARD_IN, 128), SHARD_IN)]
    return ref.at[chip] if kind == "sq" else ref.at[:, chip]


def _land_shape(src, kind):
    if kind == "in":
        return (3, src.shape[0], SHARD_IN)
    return (3,) + src.shape[1:] if kind == "sq" else (3, src.shape[0]) + src.shape[2:]


def _exchange_start(srcs, kinds, tag):
    n = len(srcs)
    lands = [pltpu.with_memory_space_constraint(lax.empty(_land_shape(s, k), s.dtype), pltpu.HBM) for s, k in zip(srcs, kinds)]

    def body(*refs):
        src_refs, land_refs = refs[:n], refs[n:2 * n]
        ssems, rsems = refs[2 * n:3 * n], refs[3 * n:4 * n]
        token = refs[6 * n]
        for i in range(n):
            for k, mask in enumerate(CHIP_MASKS):
                to = _peer(mask)
                pltpu.make_async_remote_copy(
                    src_ref=_shard_of(src_refs[i], kinds[i], _chip_of(to)), dst_ref=land_refs[i].at[k],
                    send_sem=ssems[i], recv_sem=rsems[i], device_id=to, device_id_type=MESH).start()
        token[...] = jnp.zeros_like(token)

    sem = pltpu.SemaphoreType.DMA(())
    out_shape = ((sem,) * (2 * n) + tuple(pltpu.HBM(s.shape, s.dtype) for s in srcs)
                 + tuple(pltpu.HBM(l.shape, l.dtype) for l in lands) + (jax.ShapeDtypeStruct((8, 128), F32),))
    outs = pl.pallas_call(
        body, out_shape=out_shape, name=f"exchange_start_{tag}",
        in_specs=[HBM_SPEC] * (2 * n), out_specs=tuple([SEM_SPEC] * (2 * n) + [HBM_SPEC] * (2 * n) + [VMEM_SPEC]),
        input_output_aliases={i: 2 * n + i for i in range(2 * n)},
        compiler_params=pltpu.CompilerParams(has_side_effects=pltpu.SideEffectType.DATAFLOW_SIDE_EFFECTING),
    )(*[pltpu.with_memory_space_constraint(s, pltpu.HBM) for s in srcs], *lands)
    return outs[:n], outs[n:2 * n], outs[2 * n:3 * n], outs[3 * n:4 * n], outs[4 * n]


def _exchange_wait(ssems, rsems, srcs, lands, after, tag):
    n = len(srcs)

    def body(*refs):
        land_refs = refs[n:2 * n]
        ssem_refs, rsem_refs = refs[2 * n:3 * n], refs[3 * n:4 * n]
        for i in range(n):
            all_three = pltpu.make_async_remote_copy(
                src_ref=land_refs[i], dst_ref=land_refs[i], send_sem=ssem_refs[i], recv_sem=rsem_refs[i],
                device_id=_me(), device_id_type=MESH)
            all_three.wait_send()
            all_three.wait_recv()

    outs = pl.pallas_call(
        body, out_shape=tuple(pltpu.HBM(a.shape, a.dtype) for a in list(srcs) + list(lands)), name=f"exchange_wait_{tag}",
        in_specs=[HBM_SPEC] * (2 * n) + [SEM_SPEC] * (2 * n) + [ANY], out_specs=tuple([HBM_SPEC] * (2 * n)),
        input_output_aliases={i: i for i in range(2 * n)},
        compiler_params=pltpu.CompilerParams(has_side_effects=pltpu.SideEffectType.DATAFLOW_SIDE_EFFECTING),
    )(*srcs, *lands, *ssems, *rsems, after)
    return outs[:n], outs[n:]


def _gather_small_start(slabs):
    def body(slabs_ref, ssem, rsem, slabs_out, token):
        me = _me()
        mine = slabs_ref.at[_chip_of(me) * 2 + me[2]]
        for mask in ALL_MASKS:
            pltpu.make_async_remote_copy(src_ref=mine, dst_ref=mine, send_sem=ssem, recv_sem=rsem,
                                         device_id=_peer(mask), device_id_type=MESH).start()
        token[...] = jnp.zeros_like(token)

    sem = pltpu.SemaphoreType.DMA(())
    return pl.pallas_call(
        body, out_shape=(sem, sem, pltpu.HBM(slabs.shape, slabs.dtype), jax.ShapeDtypeStruct((8, 128), F32)),
        name="gather_small_start", in_specs=[HBM_SPEC], out_specs=(SEM_SPEC, SEM_SPEC, HBM_SPEC, VMEM_SPEC),
        input_output_aliases={0: 2}, compiler_params=SPLIT_COPY,
    )(pltpu.with_memory_space_constraint(slabs, pltpu.HBM))


def _gather_small_wait(ssem, rsem, slabs, after):
    def body(slabs_ref, ssem_ref, rsem_ref, after_ref, slabs_out):
        seven = slabs_ref.at[pl.ds(0, N_DEV - 1)]
        arrived = pltpu.make_async_remote_copy(
            src_ref=seven, dst_ref=seven, send_sem=ssem_ref, recv_sem=rsem_ref, device_id=_me(), device_id_type=MESH)
        arrived.wait_send()
        arrived.wait_recv()

    return pl.pallas_call(
        body, out_shape=pltpu.HBM(slabs.shape, slabs.dtype), name="gather_small_wait",
        in_specs=[HBM_SPEC, SEM_SPEC, SEM_SPEC, ANY], out_specs=HBM_SPEC, input_output_aliases={0: 0},
        compiler_params=SPLIT_COPY,
    )(slabs, ssem, rsem, after)


def _half_of(ref, axis, half):
    return ref.at[(slice(None),) * axis + (half,)]


def _swap_halves(parts, axes):
    n = len(parts)

    def body(*refs):
        ins, outs, ssem, rsem = refs[:n], refs[n:2 * n], refs[2 * n], refs[2 * n + 1]
        c = lax.axis_index("c")
        _sibling_handshake()
        cps = [pltpu.make_async_remote_copy(src_ref=_half_of(ins[i], axes[i], 1 - c), dst_ref=outs[i], send_sem=ssem.at[i],
                                            recv_sem=rsem.at[i], device_id=_peer(1), device_id_type=MESH) for i in range(n)]
        for cp in cps:
            cp.start()
        for cp in cps:
            cp.wait()

    shapes = [p.shape[:a] + p.shape[a + 1:] for p, a in zip(parts, axes)]
    return pl.pallas_call(
        body, out_shape=tuple(jax.ShapeDtypeStruct(s, p.dtype) for s, p in zip(shapes, parts)), name="swap_halves",
        in_specs=[ANY] * n, out_specs=tuple([ANY] * n),
        scratch_shapes=[pltpu.SemaphoreType.DMA((n,)), pltpu.SemaphoreType.DMA((n,))], compiler_params=SIBLING_ONLY,
    )(*parts)


def _swap_halves_start(parts, axes, tag):
    n = len(parts)
    lands = [pltpu.with_memory_space_constraint(lax.empty(p.shape[:a] + p.shape[a + 1:], p.dtype), pltpu.HBM)
             for p, a in zip(parts, axes)]

    def body(*refs):
        ins, land_refs, ssems, rsems, token = refs[:n], refs[n:2 * n], refs[2 * n:3 * n], refs[3 * n:4 * n], refs[6 * n]
        c = lax.axis_index("c")
        for i in range(n):
            pltpu.make_async_remote_copy(src_ref=_half_of(ins[i], axes[i], 1 - c), dst_ref=land_refs[i], send_sem=ssems[i],
                                         recv_sem=rsems[i], device_id=_peer(1), device_id_type=MESH).start()
        token[...] = jnp.zeros_like(token)

    sem = pltpu.SemaphoreType.DMA(())
    out_shape = ((sem,) * (2 * n) + tuple(pltpu.HBM(a.shape, a.dtype) for a in list(parts) + lands)
                 + (jax.ShapeDtypeStruct((8, 128), F32),))
    outs = pl.pallas_call(
        body, out_shape=out_shape, name=f"swap_halves_start_{tag}",
        in_specs=[HBM_SPEC] * (2 * n), out_specs=tuple([SEM_SPEC] * (2 * n) + [HBM_SPEC] * (2 * n) + [VMEM_SPEC]),
        input_output_aliases={i: 2 * n + i for i in range(2 * n)}, compiler_params=SPLIT_COPY,
    )(*[pltpu.with_memory_space_constraint(p, pltpu.HBM) for p in parts], *lands)
    return outs[:n], outs[n:2 * n], outs[2 * n:3 * n], outs[3 * n:4 * n], outs[4 * n]


def _swap_halves_wait(ssems, rsems, parts, lands, after, tag):
    n = len(parts)

    def body(*refs):
        land_refs, ssem_refs, rsem_refs = refs[n:2 * n], refs[2 * n:3 * n], refs[3 * n:4 * n]
        for i in range(n):
            moved = pltpu.make_async_remote_copy(
                src_ref=land_refs[i], dst_ref=land_refs[i], send_sem=ssem_refs[i], recv_sem=rsem_refs[i],
                device_id=_me(), device_id_type=MESH)
            moved.wait_send()
            moved.wait_recv()

    outs = pl.pallas_call(
        body, out_shape=tuple(pltpu.HBM(a.shape, a.dtype) for a in list(parts) + list(lands)), name=f"swap_halves_wait_{tag}",
        in_specs=[HBM_SPEC] * (2 * n) + [SEM_SPEC] * (2 * n) + [ANY], out_specs=tuple([HBM_SPEC] * (2 * n)),
        input_output_aliases={i: i for i in range(2 * n)}, compiler_params=SPLIT_COPY,
    )(*parts, *lands, *ssems, *rsems, after)
    return list(outs[:n]), list(outs[n:])


def _presum(mines, sibs, c_idx, tag, where=None):
    n = len(mines)
    S, _, R, C = mines[0].shape
    tr = min(R, 256)
    tc = SHARD_IN if C % SHARD_IN == 0 else (C // 2 if n > 1 and C % 256 == 0 else C)
    own_only = where is not None
    c_at = 1 if own_only else 0

    def body(c_ref, *refs):
        for k in range(n):
            total = refs[k][:, 0] + refs[n + k][...]
            if own_only:
                @pl.when(pl.program_id(1) == c_ref[0])
                def _():
                    refs[2 * n + k][...] = total
            else:
                refs[2 * n + k][...] = total
            refs[3 * n + k][...] = total.astype(BF16)

    out_spec = pl.BlockSpec((S, tr, tc), lambda i, j, c_ref: (0, i, j))
    exact_spec = pl.BlockSpec((S, tr, tc), lambda i, j, c_ref: (0, i, 0)) if own_only else out_spec
    grid_spec = pltpu.PrefetchScalarGridSpec(
        num_scalar_prefetch=1, grid=(R // tr, C // tc),
        in_specs=[pl.BlockSpec((S, 1, tr, tc), lambda i, j, c_ref: (0, c_ref[c_at], i, j))] * n + [out_spec] * n,
        out_specs=(exact_spec,) * n + (out_spec,) * n)
    outs = pl.pallas_call(
        body, out_shape=((jax.ShapeDtypeStruct((S, R, tc if own_only else C), F32),) * n
                         + (jax.ShapeDtypeStruct((S, R, C), BF16),) * n),
        grid_spec=grid_spec, name=f"presum_{tag}", compiler_params=_cp("parallel", "arbitrary" if own_only else "parallel"),
    )(where if own_only else c_idx, *mines, *sibs)
    return list(outs[:n]), list(outs[n:])


def _assemble_with_sibling(parts, axes):
    n = len(parts)

    def body(*refs):
        outs, ssem, rsem = refs[n:2 * n], refs[2 * n], refs[2 * n + 1]
        c = lax.axis_index("c")
        _sibling_handshake()
        cps = [pltpu.make_async_remote_copy(
            src_ref=_half_of(outs[i], axes[i], c), dst_ref=_half_of(outs[i], axes[i], c), send_sem=ssem.at[i],
            recv_sem=rsem.at[i], device_id=_peer(1), device_id_type=MESH) for i in range(n)]
        for cp in cps:
            cp.start()
        for i in range(n):
            pltpu.make_async_remote_copy(
                src_ref=_half_of(outs[i], axes[i], c), dst_ref=_half_of(outs[i], axes[i], 1 - c), send_sem=ssem.at[i],
                recv_sem=rsem.at[i], device_id=_peer(1), device_id_type=MESH).wait_recv()
        for cp in cps:
            cp.wait_send()

    return pl.pallas_call(
        body, out_shape=tuple(jax.ShapeDtypeStruct(p.shape, p.dtype) for p in parts), name="assemble_with_sibling",
        in_specs=[ANY] * n, out_specs=tuple([ANY] * n), input_output_aliases={i: i for i in range(n)},
        scratch_shapes=[pltpu.SemaphoreType.DMA((n,)), pltpu.SemaphoreType.DMA((n,))], compiler_params=SIBLING_ONLY,
    )(*parts)


def _assemble_start(parts, axes):
    n = len(parts)

    def body(*refs):
        ssems, rsems, outs, token = refs[n:2 * n], refs[2 * n:3 * n], refs[3 * n:4 * n], refs[4 * n]
        c = lax.axis_index("c")
        for i in range(n):
            mine = _half_of(outs[i], axes[i], c)
            pltpu.make_async_remote_copy(src_ref=mine, dst_ref=mine, send_sem=ssems[i], recv_sem=rsems[i],
                                         device_id=_peer(1), device_id_type=MESH).start()
        token[...] = jnp.zeros_like(token)

    sem = pltpu.SemaphoreType.DMA(())
    out_shape = ((sem,) * (2 * n) + tuple(pltpu.HBM(p.shape, p.dtype) for p in parts)
                 + (jax.ShapeDtypeStruct((8, 128), F32),))
    outs = pl.pallas_call(
        body, out_shape=out_shape, name="assemble_start",
        in_specs=[HBM_SPEC] * n, out_specs=tuple([SEM_SPEC] * (2 * n) + [HBM_SPEC] * n + [VMEM_SPEC]),
        input_output_aliases={i: 2 * n + i for i in range(n)}, compiler_params=SPLIT_COPY,
    )(*[pltpu.with_memory_space_constraint(p, pltpu.HBM) for p in parts])
    return outs[:n], outs[n:2 * n], list(outs[2 * n:3 * n]), outs[3 * n]


def _assemble_wait(ssems, rsems, parts, axes, after):
    n = len(parts)

    def body(*refs):
        ssem_refs, rsem_refs = refs[n:2 * n], refs[2 * n:3 * n]
        for i in range(n):
            half = _half_of(refs[i], axes[i], 0)
            moved = pltpu.make_async_remote_copy(
                src_ref=half, dst_ref=half, send_sem=ssem_refs[i], recv_sem=rsem_refs[i],
                device_id=_me(), device_id_type=MESH)
            moved.wait_send()
            moved.wait_recv()

    outs = pl.pallas_call(
        body, out_shape=tuple(pltpu.HBM(p.shape, p.dtype) for p in parts), name="assemble_wait",
        in_specs=[HBM_SPEC] * n + [SEM_SPEC] * (2 * n) + [ANY], out_specs=tuple([HBM_SPEC] * n),
        input_output_aliases={i: i for i in range(n)}, compiler_params=SPLIT_COPY,
    )(*parts, *ssems, *rsems, after)
    return list(outs)


def _rope_lane_frequencies():
    inv = np.float32(ROPE_THETA) ** (-(np.arange(0, 2 * ROT_HALF, 2, dtype=np.float32)) / np.float32(2 * ROT_HALF))
    lane = np.arange(128) % HEAD_DIM
    return jnp.asarray(np.where(lane < 2 * ROT_HALF, inv[lane % ROT_HALF], 0.0).astype(np.float32)[None, :])


def _rope_tables(pos, freq):
    ang = pos.astype(F32) * freq
    c, s = jnp.cos(ang), jnp.sin(ang)
    m = lax.broadcasted_iota(jnp.int32, ang.shape, 1) & (HEAD_DIM - 1)
    return (jnp.where(m < 2 * ROT_HALF, c, 1.0), jnp.where(m < ROT_HALF, -s, 0.0),
            jnp.where((m >= ROT_HALF) & (m < 2 * ROT_HALF), s, 0.0))


def _columns(t):
    return [t[:, i:i + 128] for i in range(0, t.shape[-1], 128)]


def _rope(t, c, sa, sb):
    return jnp.concatenate(
        [x * c + pltpu.roll(x, 128 - ROT_HALF, 1) * sa + pltpu.roll(x, ROT_HALF, 1) * sb for x in _columns(t)], axis=1)


def _unrope(d, c, sa, sb):
    return jnp.concatenate(
        [x * c + pltpu.roll(x * sa, ROT_HALF, 1) + pltpu.roll(x * sb, 128 - ROT_HALF, 1) for x in _columns(d)], axis=1)


def _prenorm(x, mod_row, norm_g, pos_col):
    T = x.shape[0]
    tm = min(T, 512)

    def body(x_ref, mod_ref, g_ref, pos_ref, f_ref, h_ref, ht_ref, c_ref, sa_ref, sb_ref):
        xf = x_ref[...]
        shift, scale = mod_ref[:, 0:D_MODEL], mod_ref[:, D_MODEL:2 * D_MODEL]
        h = (xf * _rms(xf)) * g_ref[...] * (1.0 + scale) + shift
        h_ref[...] = h.astype(BF16)
        ht_ref[...] = h.T.astype(BF16)
        c_ref[...], sa_ref[...], sb_ref[...] = _rope_tables(pos_ref[...], f_ref[...])

    tab = jax.ShapeDtypeStruct((T, 128), F32)
    tok = lambda w: pl.BlockSpec((tm, w), lambda i: (i, 0))
    row = lambda w: pl.BlockSpec((1, w), lambda i: (0, 0))
    outs = pl.pallas_call(
        body, out_shape=(jax.ShapeDtypeStruct((T, D_MODEL), BF16), jax.ShapeDtypeStruct((D_MODEL, T), BF16), tab, tab, tab),
        grid=(T // tm,), name="prenorm",
        in_specs=[tok(D_MODEL), row(ADA_W), row(D_MODEL), tok(1), row(128)],
        out_specs=(tok(D_MODEL), pl.BlockSpec((D_MODEL, tm), lambda i: (0, i)), tok(128), tok(128), tok(128)),
        compiler_params=_cp("parallel"),
    )(x, *_in_hbm(mod_row, norm_g), pos_col, _rope_lane_frequencies())
    return outs[0], outs[1], tuple(outs[2:])


def _in_projection(h, w_in, chips, into, tag):
    T = h.shape[0]
    tm, tn = min(T, 512), SHARD_IN
    k = chips.shape[0]

    def body(chip_ref, h_ref, w_ref, *rest):
        rest[-1][...] = _dot(h_ref[...], w_ref[...])

    w_spec = pl.BlockSpec((D_MODEL, tn), lambda s, i, c: (0, c[s]), **({"pipeline_mode": pl.Buffered(1)} if k == 1 else {}))
    in_specs = [pl.BlockSpec((tm, D_MODEL), lambda s, i, c: (i, 0)), w_spec]
    args = [chips, h, w_in]
    aliases = {}
    if into is not None:
        in_specs.append(ANY)
        args.append(into)
        aliases = {3: 0}
    grid_spec = pltpu.PrefetchScalarGridSpec(num_scalar_prefetch=1, grid=(k, T // tm), in_specs=in_specs,
                                             out_specs=pl.BlockSpec((tm, tn), lambda s, i, c: (i, c[s])))
    return pl.pallas_call(
        body, out_shape=jax.ShapeDtypeStruct((T, IN_W), F32), grid_spec=grid_spec, name=f"in_projection_{tag}",
        input_output_aliases=aliases, compiler_params=_cp("parallel", "parallel"),
    )(*args)


def _attn_mask(n):
    qi = lax.broadcasted_iota(jnp.int32, (GROUP * BLOCK, BLOCK), 0) & (BLOCK - 1)
    j = lax.broadcasted_iota(jnp.int32, (GROUP * BLOCK, BLOCK), 1)
    own = j <= qi
    return own, jnp.logical_not(own) & (n == 0)


def _fold(x, own):
    return jnp.where(own, x[:, BLOCK:2 * BLOCK], x[:, 0:BLOCK])


def _unfold(xf, own):
    zero = jnp.zeros_like(xf)
    return jnp.concatenate([jnp.where(own, zero, xf), jnp.where(own, xf, zero)], axis=1)


ROW_GROUP_HEAD = (0, 2, 1, 3)


def _sink_col(sink_ref, kh):
    rowg = lax.broadcasted_iota(jnp.int32, (GROUP * BLOCK, 1), 0) // BLOCK
    col = jnp.full((GROUP * BLOCK, 1), sink_ref[0, GROUP * kh + ROW_GROUP_HEAD[0]], F32)
    for g in range(1, GROUP):
        col = jnp.where(rowg == g, sink_ref[0, GROUP * kh + ROW_GROUP_HEAD[g]], col)
    return col


def _low_lanes(shape):
    return lax.broadcasted_iota(jnp.int32, shape, 1) < HEAD_DIM


def _kv_pair_operand(prev, cur, kh):
    c = 128 * (kh // 2)
    col = jnp.concatenate([prev[:, c:c + 128], cur[:, c:c + 128]], axis=0).astype(F32)
    if kh % 2 == 0:
        lo = jnp.where(_low_lanes(col.shape), col, 0.0)
        hi = pltpu.roll(lo, HEAD_DIM, 1)
    else:
        hi = jnp.where(_low_lanes(col.shape), 0.0, col)
        lo = pltpu.roll(hi, HEAD_DIM, 1)
    return jnp.concatenate([lo, hi], axis=0).astype(BF16)


def _pair_rows(x, kh):
    c = 2 * 128 * kh
    return jnp.concatenate([x[:, c:c + 128], x[:, c + 128:c + 256]], axis=0)


def _restack(big):
    return jnp.concatenate([big[:, 0:2 * BLOCK], big[:, 2 * BLOCK:4 * BLOCK]], axis=0)


def _unrestack(stacked):
    return jnp.concatenate([stacked[0:2 * BLOCK], stacked[2 * BLOCK:4 * BLOCK]], axis=1)


def _fold_pair(x2, kh):
    low = _low_lanes((2 * BLOCK, 128))
    mixed = jnp.where(low, x2[0:2 * BLOCK], x2[2 * BLOCK:4 * BLOCK])
    total = mixed + pltpu.roll(mixed, HEAD_DIM, 1)
    return jnp.where(low, total, 0.0) if kh % 2 == 0 else jnp.where(low, 0.0, total)


def _attn_scores(qr, k2, kh):
    q2 = _pair_rows(qr, kh).astype(BF16)
    return q2, _restack(_dot_nt(q2, k2))


def _attn_softmax(s, sink_col, mask):
    own, no_key = mask
    s = jnp.where(no_key, -1e30, _fold(s, own))
    m = jnp.maximum(jnp.max(s, axis=-1, keepdims=True), sink_col)
    p = jnp.exp(s - m)
    p_sink = jnp.exp(sink_col - m)
    denom = jnp.sum(p, axis=-1, keepdims=True) + p_sink
    return p / denom, p_sink / denom


def _attn_forward(proj, tabs, sinks):
    T = proj.shape[0]
    nb = T // BLOCK

    def body(q_ref, kvc_ref, kvp_ref, g0_ref, g1_ref, cc, sac, sbc, cp_, sap, sbp, sink_ref, y_ref, qrb_ref, krb_ref, p_ref):
        n = pl.program_id(0)
        tc = tcur = (cc[...], sac[...], sbc[...])
        tprev = (cp_[...], sap[...], sbp[...])
        qr = _rope(q_ref[...], *tc) * ATTN_SCALE
        kr_cur = _rope(kvc_ref[:, 0:KV_W], *tcur)
        kr_prev = _rope(kvp_ref[:, 0:KV_W], *tprev)
        qrb_ref[...] = qr.astype(BF16)
        krb_ref[...] = kr_cur.astype(BF16)
        v_cur, v_prev = kvc_ref[:, KV_W:2 * KV_W], kvp_ref[:, KV_W:2 * KV_W]
        mask = _attn_mask(n)
        outs = []
        k2s = [_kv_pair_operand(kr_prev, kr_cur, kh) for kh in range(N_KV)]
        v2s = [_kv_pair_operand(v_prev, v_cur, kh) for kh in range(N_KV)]
        scores = [_attn_scores(qr, k2s[kh], kh) for kh in range(N_KV)]
        p_parts = []
        for kh in range(N_KV):
            pn, _ = _attn_softmax(scores[kh][1], _sink_col(sink_ref, kh), mask)
            p_parts += [pn[g * BLOCK:(g + 1) * BLOCK] for g in range(GROUP)]
            o_big = _dot(_unrestack(_unfold(pn.astype(BF16), mask[0])), v2s[kh])
            outs += [o_big[0:BLOCK], o_big[BLOCK:2 * BLOCK]]
        p_ref[...] = jnp.concatenate(p_parts, axis=1)
        o = jnp.concatenate(outs, axis=1)
        g = jnp.concatenate([g0_ref[...], g1_ref[...]], axis=1)
        y_ref[...] = (o * (g * _sigmoid(g))).astype(BF16)

    def blk(w, cb):
        return pl.BlockSpec((BLOCK, w), lambda n, cb=cb: (n, cb))

    prev = lambda w, cb: pl.BlockSpec((BLOCK, w), lambda n, cb=cb: (jnp.maximum(n - 1, 0), cb))
    return pl.pallas_call(
        body, grid=(nb,), name="attn_forward",
        out_shape=(jax.ShapeDtypeStruct((T, D_MODEL), BF16), jax.ShapeDtypeStruct((T, D_MODEL), BF16),
                   jax.ShapeDtypeStruct((T, KV_W), BF16), jax.ShapeDtypeStruct((T, N_HEADS * BLOCK), F32)),
        in_specs=[blk(D_MODEL, 0), blk(CB, CB_KV), prev(CB, CB_KV), blk(CB, CB_GA), blk(CB, CB_GA + 1),
                  blk(128, 0), blk(128, 0), blk(128, 0), prev(128, 0), prev(128, 0), prev(128, 0),
                  pl.BlockSpec(memory_space=pltpu.SMEM)],
        out_specs=(blk(D_MODEL, 0), blk(D_MODEL, 0), blk(KV_W, 0), blk(N_HEADS * BLOCK, 0)),
        compiler_params=_cp("parallel"),
    )(proj, proj, proj, proj, proj, *tabs, *tabs, sinks)


def _scan_rows8():
    return lax.broadcasted_iota(jnp.int32, (8, D_MODEL), 0)


def _scan_forward(a_ref, b_ref, h_ref, carry, rows):
    row = _scan_rows8()

    def group(i, carry):
        off = pl.multiple_of(i * 8, 8)
        a, b = a_ref[pl.ds(off, 8), :], b_ref[pl.ds(off, 8), :]
        for d in (1, 2, 4):
            ok = row >= d
            b = jnp.where(ok, a * pltpu.roll(b, d, 0) + b, b)
            a = jnp.where(ok, a * pltpu.roll(a, d, 0), a)
        h = a * carry + b
        h_ref[pl.ds(off, 8), :] = h
        return h[7:8, :]

    return lax.fori_loop(0, rows // 8, group, carry)


def _scan_backward(a_ref, g_ref, lam_ref, carry, rows):
    row = _scan_rows8()

    def group(i, carry):
        off = pl.multiple_of((rows // 8 - 1 - i) * 8, 8)
        a, g = a_ref[pl.ds(off, 8), :], g_ref[pl.ds(off, 8), :]
        b = a * g
        for d in (1, 2, 4):
            ok = row < 8 - d
            b = jnp.where(ok, a * pltpu.roll(b, 8 - d, 0) + b, b)
            a = jnp.where(ok, a * pltpu.roll(a, 8 - d, 0), a)
        mu = a * carry + b
        mu_below = jnp.where(row == 7, carry, pltpu.roll(mu, 7, 0))
        lam_ref[pl.ds(off, 8), :] = g + mu_below
        return mu[0:1, :]

    return lax.fori_loop(0, rows // 8, group, carry)


def _conv_taps(xbuf, xr, tail):
    rows = xr.shape[0]
    xbuf[0:8, :] = tail
    xbuf[8:rows + 8, :] = xr
    return [xbuf[pl.ds(8 - (CONV_W - 1 - k), rows), :] for k in range(CONV_W - 1)] + [xr]


def _rnn_gates(xbuf, xr, tail, cw, cb, wa_ref, wx_ref, ba, bx, sp, reset):
    xs = _conv_taps(xbuf, xr, tail)
    xc = xs[0] * cw[0:1, :]
    for k in range(1, CONV_W):
        xc = xc + xs[k] * cw[k:k + 1, :]
    xc = xc + cb
    xcb = xc.astype(BF16)
    za = jnp.concatenate([_dot(xcb[:, RNN_BW * j:RNN_BW * (j + 1)], wa_ref[j]) for j in range(RNN_BLOCKS)], axis=1) + ba
    zx = jnp.concatenate([_dot(xcb[:, RNN_BW * j:RNN_BW * (j + 1)], wx_ref[j]) for j in range(RNN_BLOCKS)], axis=1) + bx
    r, i = _sigmoid(za), _sigmoid(zx)
    neg_log_a = LRU_C * r * sp
    a_raw = jnp.exp(-neg_log_a)
    mult_raw = jnp.sqrt(jnp.tanh(neg_log_a) * (1.0 + a_raw * a_raw))
    a = jnp.where(reset, 0.0, a_raw)
    mult = jnp.where(reset, 1.0, mult_raw)
    return xc, r, i, a, mult


def _rnn_forward(proj, pos_col, conv_w, conv_b, rwa, rwx, ba, bx, lam):
    T = proj.shape[0]
    tr = min(T, 256)

    def body(x0, x1, g0, g1, pos_ref, cw_ref, cb_ref, wa_ref, wx_ref, ba_ref, bx_ref, lam_ref,
             y_ref, h_ref, xc_ref, r_ref, i_ref, a_ref, mult_ref, xbuf, bbuf, tail, carry):
        t = pl.program_id(0)

        @pl.when(t == 0)
        def _():
            tail[...] = jnp.zeros_like(tail)
            carry[...] = jnp.zeros_like(carry)

        xr = jnp.concatenate([x0[...], x1[...]], axis=1)
        sp = _softplus(-lam_ref[...])
        reset = pos_ref[...] == 0
        xc, r, i, a, mult = _rnn_gates(
            xbuf, xr, tail[...], cw_ref[...], cb_ref[...], wa_ref, wx_ref, ba_ref[...], bx_ref[...], sp, reset)
        xc_ref[...] = xc
        r_ref[...] = r
        i_ref[...] = i
        a_ref[...] = a
        mult_ref[...] = mult
        bbuf[...] = mult * (i * xc)
        last = _scan_forward(a_ref, bbuf, h_ref, carry[0:1, :], tr)
        carry[...] = jnp.broadcast_to(last, carry.shape)
        tail[...] = xr[tr - 8:tr, :]
        g = jnp.concatenate([g0[...], g1[...]], axis=1)
        y_ref[...] = (h_ref[...] * (g * _sigmoid(g))).astype(BF16)

    blk = lambda cb: pl.BlockSpec((tr, CB), lambda t, cb=cb: (t, cb))
    row = lambda w: pl.BlockSpec((1, w), lambda t: (0, 0))
    full3 = pl.BlockSpec((RNN_BLOCKS, RNN_BW, RNN_BW), lambda t: (0, 0, 0))
    tok = pl.BlockSpec((tr, D_MODEL), lambda t: (t, 0))
    act = jax.ShapeDtypeStruct((T, D_MODEL), F32)
    return pl.pallas_call(
        body, out_shape=(jax.ShapeDtypeStruct((T, D_MODEL), BF16),) + (act,) * 6,
        grid=(T // tr,), name="rnn_forward",
        in_specs=[blk(CB_XR), blk(CB_XR + 1), blk(CB_GR), blk(CB_GR + 1), pl.BlockSpec((tr, 1), lambda t: (t, 0)),
                  pl.BlockSpec((CONV_W, D_MODEL), lambda t: (0, 0)), row(D_MODEL), full3, full3,
                  row(D_MODEL), row(D_MODEL), row(D_MODEL)],
        out_specs=(tok,) * 7,
        scratch_shapes=[pltpu.VMEM((tr + 8, D_MODEL), F32), pltpu.VMEM((tr, D_MODEL), F32),
                        pltpu.VMEM((8, D_MODEL), F32), pltpu.VMEM((8, D_MODEL), F32)],
        compiler_params=_cp("arbitrary"),
    )(proj, proj, proj, proj, pos_col, *_in_hbm(conv_w, conv_b, rwa, rwx, ba, bx, lam))


def _merge_and_head(x, target, y_attn, y_rnn, proj, wap, wrp, wo, mod_row, final_g):
    T = x.shape[0]
    tm = min(T, 256)

    def body(x_ref, t_ref, ya_ref, yr_ref, ma0, ma1, mr0, mr1, wap_ref, wrp_ref, wo_ref, mod_ref, fg_ref,
             dx2_ref, mg_ref, do_ref, dpa_ref, dpr_ref, dya_ref, dyr_ref, dc_ref, dfg_ref, dgate_ref, loss_ref):
        i = pl.program_id(0)
        gate = mod_ref[:, 2 * D_MODEL:3 * D_MODEL]
        fg = fg_ref[...]
        pa, pr = _dot(ya_ref[...], wap_ref[...]), _dot(yr_ref[...], wrp_ref[...])
        sa = _sigmoid(jnp.concatenate([ma0[...], ma1[...]], axis=1))
        sr = _sigmoid(jnp.concatenate([mr0[...], mr1[...]], axis=1))
        mb = (sa * pa + sr * pr).astype(BF16)
        o = _dot(mb, wo_ref[...])
        x2 = x_ref[...] + gate * o
        r2 = _rms(x2)
        xn2 = x2 * r2
        err = xn2 * fg - t_ref[...]
        loss_t = 0.5 * jnp.sum(jnp.sum(err * err, axis=-1, keepdims=True) * (1.0 / D_MODEL), axis=0, keepdims=True)
        dy = err * (1.0 / D_MODEL)
        dfg_t = jnp.sum(dy * xn2, axis=0, keepdims=True)
        dxn = dy * fg
        dx2 = r2 * (dxn - xn2 * jnp.mean(dxn * xn2, axis=-1, keepdims=True))
        dgate_t = jnp.sum(dx2 * o, axis=0, keepdims=True)
        dob = (dx2 * gate).astype(BF16)
        dmerged = _dot_nt(dob, wo_ref[...])
        dpa, dpr = (dmerged * sa).astype(BF16), (dmerged * sr).astype(BF16)
        dya, dyr = _dot_nt(dpa, wap_ref[...]), _dot_nt(dpr, wrp_ref[...])
        dx2_ref[...] = dx2
        mg_ref[...] = mb
        do_ref[...] = dob
        dpa_ref[...] = dpa
        dpr_ref[...] = dpr
        dya_ref[...] = dya
        dyr_ref[...] = dyr
        dc_ref[:, 0:D_MODEL] = (dmerged * pa * sa * (1.0 - sa)).astype(BF16)
        dc_ref[:, D_MODEL:2 * D_MODEL] = (dmerged * pr * sr * (1.0 - sr)).astype(BF16)

        @pl.when(i == 0)
        def _():
            dfg_ref[...] = jnp.zeros_like(dfg_ref)
            dgate_ref[...] = jnp.zeros_like(dgate_ref)
            loss_ref[...] = jnp.zeros_like(loss_ref)

        dfg_ref[...] += dfg_t
        dgate_ref[...] += dgate_t
        loss_ref[...] += jnp.broadcast_to(loss_t, loss_ref.shape)

    tok = lambda w: pl.BlockSpec((tm, w), lambda i: (i, 0))
    blk = lambda cb: pl.BlockSpec((tm, CB), lambda i, cb=cb: (i, cb))
    wfull = pl.BlockSpec((D_MODEL, D_MODEL), lambda i: (0, 0), pipeline_mode=pl.Buffered(1))
    row = lambda w: pl.BlockSpec((1, w), lambda i: (0, 0))
    out_shape = (
        jax.ShapeDtypeStruct((T, D_MODEL), F32), jax.ShapeDtypeStruct((T, D_MODEL), BF16),
        jax.ShapeDtypeStruct((T, D_MODEL), BF16), jax.ShapeDtypeStruct((T, D_MODEL), BF16),
        jax.ShapeDtypeStruct((T, D_MODEL), BF16), jax.ShapeDtypeStruct((T, D_MODEL), F32),
        jax.ShapeDtypeStruct((T, D_MODEL), F32), jax.ShapeDtypeStruct((T, 2 * D_MODEL), BF16),
        jax.ShapeDtypeStruct((1, D_MODEL), F32), jax.ShapeDtypeStruct((1, D_MODEL), F32),
        jax.ShapeDtypeStruct((1, 128), F32),
    )
    return pl.pallas_call(
        body, out_shape=out_shape, grid=(T // tm,), name="merge_and_head",
        in_specs=[tok(D_MODEL), tok(D_MODEL), tok(D_MODEL), tok(D_MODEL), blk(CB_MA), blk(CB_MA + 1), blk(CB_MR),
                  blk(CB_MR + 1), wfull, wfull, wfull, row(ADA_W), row(D_MODEL)],
        out_specs=(tok(D_MODEL),) * 7 + (tok(2 * D_MODEL), row(D_MODEL), row(D_MODEL), row(128)),
        compiler_params=_cp("arbitrary"),
    )(x, target, y_attn, y_rnn, proj, proj, proj, proj, wap, wrp, wo, *_in_hbm(mod_row, final_g))


def _attn_backward(proj, qr_b, kr_b, p_all, d_y, tabs, after):
    T = proj.shape[0]
    nb = T // BLOCK

    def body(qrb_ref, krc_ref, krp_ref, vc_ref, vp_ref, g0_ref, g1_ref, dy_ref, p_ref, cc, sac, sbc, cp_, sap, sbp, after_ref,
             dq_ref, dkv_ref, dg_ref, dsink_ref, carry):
        n = pl.program_id(0)

        @pl.when(n == 0)
        def _():
            carry[...] = jnp.zeros_like(carry)
            dsink_ref[...] = jnp.zeros_like(dsink_ref)

        @pl.when(n < nb)
        def _():
            tc = tcur = (cc[...], sac[...], sbc[...])
            tprev = (cp_[...], sap[...], sbp[...])
            qr, kr_cur, kr_prev = qrb_ref[...], krc_ref[...], krp_ref[...]
            v_cur, v_prev = vc_ref[...], vp_ref[...]
            g = jnp.concatenate([g0_ref[...], g1_ref[...]], axis=1)
            sg = _sigmoid(g)
            dy = dy_ref[...]
            d_o = dy * (g * sg)
            mask = _attn_mask(n)
            lane = lax.broadcasted_iota(jnp.int32, (1, 128), 1)
            rowg = lax.broadcasted_iota(jnp.int32, (GROUP * BLOCK, 1), 0) // BLOCK
            o_parts, dq_parts = [], []
            dk_cols, dv_cols = [None, None], [None, None]
            dsink = jnp.zeros((1, 128), F32)
            heads = range(N_KV)
            k2s = [_kv_pair_operand(kr_prev, kr_cur, kh) for kh in heads]
            v2s = [_kv_pair_operand(v_prev, v_cur, kh) for kh in heads]
            q2s = [_pair_rows(qr, kh).astype(BF16) for kh in heads]
            do2s = [_pair_rows(d_o, kh).astype(BF16) for kh in heads]
            dpns = [_fold(_restack(_dot_nt(do2s[kh], v2s[kh])), mask[0]) for kh in heads]
            pns = [jnp.concatenate([p_ref[:, BLOCK * (GROUP * kh + g):BLOCK * (GROUP * kh + g + 1)] for g in range(GROUP)], axis=0)
                   for kh in heads]
            probs = [(pn, 1.0 - jnp.sum(pn, axis=-1, keepdims=True)) for pn in pns]
            p_bigs = [_unrestack(_unfold(probs[kh][0].astype(BF16), mask[0])) for kh in heads]
            o_bigs = [_dot(p_bigs[kh], v2s[kh]) for kh in heads]
            dv2s = [_dot_tn(p_bigs[kh], do2s[kh]) for kh in heads]
            deltas = [jnp.sum(probs[kh][0] * dpns[kh], axis=-1, keepdims=True) for kh in heads]
            ds_bigs = [_unrestack(_unfold((probs[kh][0] * (dpns[kh] - deltas[kh])).astype(BF16), mask[0])) for kh in heads]
            dq2s = [_dot(ds_bigs[kh], k2s[kh]) for kh in heads]
            dk2s = [_dot_tn(ds_bigs[kh], q2s[kh]) for kh in heads]
            for kh in heads:
                o_parts += [o_bigs[kh][0:BLOCK], o_bigs[kh][BLOCK:2 * BLOCK]]
                dq_parts += [dq2s[kh][0:BLOCK], dq2s[kh][BLOCK:2 * BLOCK]]
                dk_c, dv_c = _fold_pair(dk2s[kh], kh), _fold_pair(dv2s[kh], kh)
                c = kh // 2
                dk_cols[c] = dk_c if dk_cols[c] is None else dk_cols[c] + dk_c
                dv_cols[c] = dv_c if dv_cols[c] is None else dv_cols[c] + dv_c
                ds_rows = probs[kh][1] * deltas[kh]
                for gq in range(GROUP):
                    val = -jnp.sum(jnp.where(rowg == gq, ds_rows, 0.0), axis=0, keepdims=True)
                    dsink = dsink + jnp.where(lane == GROUP * kh + ROW_GROUP_HEAD[gq], val, 0.0)
            o = jnp.concatenate(o_parts, axis=1)
            dg_ref[...] = (dy * o * (sg * (1.0 + g * (1.0 - sg)))).astype(BF16)
            dq_ref[...] = (_unrope(jnp.concatenate(dq_parts, axis=1), *tc) * ATTN_SCALE).astype(BF16)
            dk_all, dv_all = jnp.concatenate(dk_cols, axis=1), jnp.concatenate(dv_cols, axis=1)
            dk_prev = _unrope(dk_all[0:BLOCK], *tprev)
            dk_cur = _unrope(dk_all[BLOCK:2 * BLOCK], *tcur)
            dv_prev, dv_cur = dv_all[0:BLOCK], dv_all[BLOCK:2 * BLOCK]
            dkv_ref[...] = (carry[...] + jnp.concatenate([dk_prev, dv_prev], axis=1)).astype(BF16)
            carry[...] = jnp.concatenate([dk_cur, dv_cur], axis=1)
            dsink_ref[...] += dsink

        @pl.when(n == nb)
        def _():
            dkv_ref[...] = carry[...].astype(BF16)

    cur = lambda w, cb: pl.BlockSpec((BLOCK, w), lambda n, cb=cb: (jnp.minimum(n, nb - 1), cb))
    prev = lambda w, cb: pl.BlockSpec((BLOCK, w), lambda n, cb=cb: (jnp.maximum(jnp.minimum(n, nb - 1) - 1, 0), cb))
    out_shape = (jax.ShapeDtypeStruct((T, D_MODEL), BF16), jax.ShapeDtypeStruct((T, 2 * KV_W), BF16),
                 jax.ShapeDtypeStruct((T, D_MODEL), BF16), jax.ShapeDtypeStruct((1, 128), F32))
    return pl.pallas_call(
        body, out_shape=out_shape, grid=(nb + 1,), name="attn_backward",
        in_specs=[cur(D_MODEL, 0), cur(KV_W, 0), prev(KV_W, 0), cur(KV_W, V_COL_BLOCK), prev(KV_W, V_COL_BLOCK),
                  cur(CB, CB_GA), cur(CB, CB_GA + 1), cur(D_MODEL, 0), cur(N_HEADS * BLOCK, 0),
                  cur(128, 0), cur(128, 0), cur(128, 0), prev(128, 0), prev(128, 0), prev(128, 0),
                  pl.BlockSpec(memory_space=pltpu.SMEM)],
        out_specs=(cur(D_MODEL, 0), pl.BlockSpec((BLOCK, 2 * KV_W), lambda n: (jnp.maximum(n - 1, 0), 0)),
                   cur(D_MODEL, 0), pl.BlockSpec((1, 128), lambda n: (0, 0))),
        scratch_shapes=[pltpu.VMEM((BLOCK, 2 * KV_W), F32)],
        compiler_params=_cp("arbitrary"),
    )(qr_b, kr_b, kr_b, proj, proj, proj, proj, d_y, p_all, *tabs, *tabs, after)


def _rnn_backward(proj, pos_col, h_rnn, saved, d_y, conv_w, rwa, rwx, lam):
    T = proj.shape[0]
    tr = min(T, 256)
    nt = T // tr
    hb = tr // 8

    def body(x0, x1, xh0, xh1, g0, g1, pos_ref, h_ref, hh_ref, xc_ref, r_ref, i_ref, a_ref, mult_ref, dy_ref,
             cw_ref, wa_ref, wx_ref, lam_ref, db_ref, dcw_ref, dcb_ref, dwa_ref, dwx_ref, dba_ref, dbx_ref, dlam_ref,
             xbuf, hbuf, dbuf, gbuf, lbuf, mu_carry, dxc_head):
        step = pl.program_id(0)
        first_tile = step == nt - 1

        @pl.when(step == 0)
        def _():
            mu_carry[...] = jnp.zeros_like(mu_carry)
            dxc_head[...] = jnp.zeros_like(dxc_head)
            for ref in (dcw_ref, dcb_ref, dwa_ref, dwx_ref, dba_ref, dbx_ref, dlam_ref):
                ref[...] = jnp.zeros_like(ref)

        xr = jnp.concatenate([x0[...], x1[...]], axis=1)
        tail = jnp.where(first_tile, 0.0, jnp.concatenate([xh0[...], xh1[...]], axis=1))
        lam_v = lam_ref[...]
        sp = _softplus(-lam_v)
        reset = pos_ref[...] == 0
        cw = cw_ref[...]
        xbuf[0:8, :] = tail
        xbuf[8:tr + 8, :] = xr
        g = jnp.concatenate([g0[...], g1[...]], axis=1)
        sg = _sigmoid(g)
        dy = dy_ref[...]
        h = h_ref[...]
        db_ref[:, D_MODEL:2 * D_MODEL] = (dy * h * (sg * (1.0 + g * (1.0 - sg)))).astype(BF16)
        gbuf[...] = dy * (g * sg)
        top = _scan_backward(a_ref, gbuf, lbuf, mu_carry[0:1, :], tr)
        mu_carry[...] = jnp.broadcast_to(top, mu_carry.shape)
        hbuf[0:8, :] = jnp.where(first_tile, 0.0, hh_ref[...])
        hbuf[8:tr + 8, :] = h
        live = jnp.logical_not(reset)
        dbuf[tr:tr + 8, :] = dxc_head[...]
        for j in range(RNN_BLOCKS):
            sl = slice(RNN_BW * j, RNN_BW * (j + 1))
            lam_t, h_prev = lbuf[:, sl], hbuf[pl.ds(7, tr), sl]
            xc, r, i, a, mult = xc_ref[:, sl], r_ref[:, sl], i_ref[:, sl], a_ref[:, sl], mult_ref[:, sl]
            d_a = jnp.where(live, lam_t * h_prev, 0.0)
            d_mult = jnp.where(live, lam_t * (i * xc), 0.0)
            d_ixc = lam_t * mult
            d_i = d_ixc * xc
            d_log_a = d_a * a - d_mult * (a * a / mult)
            d_za = d_log_a * (-LRU_C * sp[:, sl]) * (r * (1.0 - r))
            d_zx = d_i * (i * (1.0 - i))
            dlam_ref[:, sl] += jnp.sum(d_log_a * r, axis=0, keepdims=True) * (LRU_C * _sigmoid(-lam_v[:, sl]))
            dba_ref[:, sl] += jnp.sum(d_za, axis=0, keepdims=True)
            dbx_ref[:, sl] += jnp.sum(d_zx, axis=0, keepdims=True)
            xcb, dzab, dzxb = xc.astype(BF16), d_za.astype(BF16), d_zx.astype(BF16)
            dwa_ref[j] += _dot_tn(xcb, dzab)
            dwx_ref[j] += _dot_tn(xcb, dzxb)
            d_xc = d_ixc * i + (_dot_nt(dzab, wa_ref[j]) + _dot_nt(dzxb, wx_ref[j]))
            dcb_ref[:, sl] += jnp.sum(d_xc, axis=0, keepdims=True)
            for k in range(CONV_W):
                tap = xr[:, sl] if k == CONV_W - 1 else xbuf[pl.ds(8 - (CONV_W - 1 - k), tr), sl]
                dcw_ref[k:k + 1, sl] += jnp.sum(d_xc * tap, axis=0, keepdims=True)
            dbuf[0:tr, sl] = d_xc
            d_xr = d_xc * cw[CONV_W - 1:CONV_W, sl]
            for k in range(CONV_W - 1):
                d_xr = d_xr + dbuf[pl.ds(CONV_W - 1 - k, tr), sl] * cw[k:k + 1, sl]
            dxc_head[:, sl] = d_xc[0:8, :]
            db_ref[:, sl] = d_xr.astype(BF16)

    rev = lambda s: nt - 1 - s
    blk = lambda cb: pl.BlockSpec((tr, CB), lambda s, cb=cb: (rev(s), cb))
    halo = lambda w, cb: pl.BlockSpec((8, w), lambda s, cb=cb: (jnp.maximum(rev(s) * hb - 1, 0), cb))
    tok = lambda w: pl.BlockSpec((tr, w), lambda s: (rev(s), 0))
    row = lambda w: pl.BlockSpec((1, w), lambda s: (0, 0))
    full3 = pl.BlockSpec((RNN_BLOCKS, RNN_BW, RNN_BW), lambda s: (0, 0, 0))
    cwspec = pl.BlockSpec((CONV_W, D_MODEL), lambda s: (0, 0))
    vec = jax.ShapeDtypeStruct((1, D_MODEL), F32)
    gate_w = jax.ShapeDtypeStruct((RNN_BLOCKS, RNN_BW, RNN_BW), F32)
    out_shape = (jax.ShapeDtypeStruct((T, 2 * D_MODEL), BF16), jax.ShapeDtypeStruct((CONV_W, D_MODEL), F32), vec,
                 gate_w, gate_w, vec, vec, vec)
    big = lambda: pltpu.VMEM((tr, D_MODEL), F32)
    ext = lambda: pltpu.VMEM((tr + 8, D_MODEL), F32)
    return pl.pallas_call(
        body, out_shape=out_shape, grid=(nt,), name="rnn_backward",
        in_specs=[blk(CB_XR), blk(CB_XR + 1), halo(CB, CB_XR), halo(CB, CB_XR + 1), blk(CB_GR), blk(CB_GR + 1),
                  pl.BlockSpec((tr, 1), lambda s: (rev(s), 0)), tok(D_MODEL), halo(D_MODEL, 0)] + [tok(D_MODEL)] * 6
        + [cwspec, full3, full3, row(D_MODEL)],
        out_specs=(tok(2 * D_MODEL), cwspec, row(D_MODEL), full3, full3, row(D_MODEL), row(D_MODEL), row(D_MODEL)),
        scratch_shapes=[ext(), ext(), ext(), big(), big(), pltpu.VMEM((8, D_MODEL), F32), pltpu.VMEM((8, D_MODEL), F32)],
        compiler_params=_cp("arbitrary"),
    )(proj, proj, proj, proj, proj, proj, pos_col, h_rnn, h_rnn, *saved, d_y, *_in_hbm(conv_w, rwa, rwx, lam))


def _input_backward(pieces, w_in, x, dx2, mod_row, norm_g):
    T = x.shape[0]
    tm = min(T, 512)
    n = len(pieces)

    def body(*refs):
        d_refs = refs[:n]
        w_ref, x_ref, dx2_ref, mod_ref, g_ref, gx_ref, dshift_ref, dscale_ref, dg_ref = refs[n:]
        i = pl.program_id(0)
        dh = None
        for d_ref, (_, start, count) in zip(d_refs, pieces):
            part = _dot_nt(d_ref[...], w_ref[:, start * CB:(start + count) * CB])
            dh = part if dh is None else dh + part

        @pl.when(i == 0)
        def _():
            dshift_ref[...] = jnp.zeros_like(dshift_ref)
            dscale_ref[...] = jnp.zeros_like(dscale_ref)
            dg_ref[...] = jnp.zeros_like(dg_ref)

        xf = x_ref[...]
        r1 = _rms(xf)
        xn = xf * r1
        gn = g_ref[...]
        s1 = 1.0 + mod_ref[:, D_MODEL:2 * D_MODEL]
        dshift_ref[...] += jnp.sum(dh, axis=0, keepdims=True)
        dscale_ref[...] += jnp.sum(dh * (xn * gn), axis=0, keepdims=True)
        dg_ref[...] += jnp.sum(dh * s1 * xn, axis=0, keepdims=True)
        dxn = dh * s1 * gn
        gx_ref[...] = dx2_ref[...] + r1 * (dxn - xn * jnp.mean(dxn * xn, axis=-1, keepdims=True))

    tok = lambda w: pl.BlockSpec((tm, w), lambda i: (i, 0))
    row = lambda w: pl.BlockSpec((1, w), lambda i: (0, 0))
    vec = jax.ShapeDtypeStruct((1, D_MODEL), F32)
    return pl.pallas_call(
        body, out_shape=(jax.ShapeDtypeStruct((T, D_MODEL), F32), vec, vec, vec), grid=(T // tm,), name="input_backward",
        in_specs=[tok(c * CB) for _, _, c in pieces]
        + [pl.BlockSpec((D_MODEL, IN_W), lambda i: (0, 0), pipeline_mode=pl.Buffered(1)), tok(D_MODEL), tok(D_MODEL),
           row(ADA_W), row(D_MODEL)],
        out_specs=(tok(D_MODEL), row(D_MODEL), row(D_MODEL), row(D_MODEL)),
        compiler_params=_cp("arbitrary"),
    )(*[p[0] for p in pieces], w_in, x, dx2, *_in_hbm(mod_row, norm_g))


def _weight_grad(a, pieces, tag, a_is_transposed=False):
    M, T = a.shape if a_is_transposed else a.shape[::-1]
    n_blocks = sum(count for _, _, count in pieces)
    n = len(pieces)
    contract = _dot if a_is_transposed else _dot_tn

    def body(*refs):
        a_ref, b_refs, o_ref = refs[0], refs[1:1 + n], refs[-1]
        j = pl.program_id(0)
        for b_ref, (_, start, count) in zip(b_refs, pieces):
            @pl.when((j >= start) & (j < start + count))
            def _(b_ref=b_ref):
                o_ref[...] = contract(a_ref[...], b_ref[...])

    def piece_spec(start, count):
        return pl.BlockSpec((T, CB), lambda j: (0, jnp.clip(j - start, 0, count - 1)))

    return pl.pallas_call(
        body, out_shape=jax.ShapeDtypeStruct((M, n_blocks * CB), F32), grid=(n_blocks,), name=f"weight_grad_{tag}",
        in_specs=[pl.BlockSpec(a.shape, lambda j: (0, 0), pipeline_mode=pl.Buffered(1))] + [piece_spec(s, c) for _, s, c in pieces],
        out_specs=pl.BlockSpec((M, CB), lambda j: (0, j)), compiler_params=_cp("arbitrary"),
    )(a, *[p[0] for p in pieces])


def _adamw(w, g, m, v):
    m = ADAM_B1 * m + (1.0 - ADAM_B1) * g
    v = ADAM_B2 * v + (1.0 - ADAM_B2) * (g * g)
    m_hat = m / (1.0 - ADAM_B1 ** ADAM_STEP)
    v_hat = v / (1.0 - ADAM_B2 ** ADAM_STEP)
    delta = -ADAM_LR * (m_hat / (jnp.sqrt(v_hat) + ADAM_EPS) + ADAM_WD * w)
    return delta, m, v


def _sum_landed(kind, owns, lands, where, tag):
    n = len(owns)
    land = lands[0]
    if kind == "in":
        R, C = land.shape[1:]
        tr = 256
        grid = (R // tr,)
        own_spec = pl.BlockSpec((tr, C), lambda i, w: (i, 0))
        land_spec = pl.BlockSpec((3, tr, C), lambda i, w: (0, i, 0))
        out_spec = pl.BlockSpec((1, tr, C), lambda i, w: (w[1], i, 0))
        out_shape = (2, R, C)
        pick = lambda ref: ref[...]
    elif kind == "sq":
        R, C = land.shape[1:]
        grid = (1,)
        own_spec = pl.BlockSpec((1, R, C), lambda i, w: (w[0], 0, 0))
        land_spec = pl.BlockSpec((3, R, C), lambda i, w: (0, 0, 0))
        out_spec = pl.BlockSpec((1, R, C), lambda i, w: (w[1], 0, 0))
        out_shape = (2, R, C)
        pick = lambda ref: ref[0]
    else:
        B, R, C = land.shape[1:]
        grid = (1,)
        own_spec = pl.BlockSpec((B, 1, R, C), lambda i, w: (0, w[0], 0, 0))
        land_spec = pl.BlockSpec((3, B, R, C), lambda i, w: (0, 0, 0, 0))
        out_spec = pl.BlockSpec((B, 1, R, C), lambda i, w: (0, w[1], 0, 0))
        out_shape = (B, 2, R, C)
        pick = lambda ref: ref[:, 0]

    def body(w_ref, *refs):
        for k in range(n):
            own_ref, l_ref, o_ref = refs[k], refs[n + k], refs[2 * n + k]
            total = ((pick(own_ref) + l_ref[0].astype(F32)) + l_ref[1].astype(F32)) + l_ref[2].astype(F32)
            if kind == "rg":
                o_ref[:, 0] = total
            else:
                o_ref[0] = total

    grid_spec = pltpu.PrefetchScalarGridSpec(num_scalar_prefetch=1, grid=grid, in_specs=[own_spec] * n + [land_spec] * n,
                                             out_specs=(out_spec,) * n)
    return list(pl.pallas_call(
        body, out_shape=(jax.ShapeDtypeStruct(out_shape, F32),) * n, grid_spec=grid_spec, name=f"sum_landed_{tag}",
        compiler_params=_cp("parallel"),
    )(where, *owns, *lands))


def _adamw_shard(gs, ws, ms, vs, tag):
    n = len(ws)
    R, C = ws[0].shape
    tr = min(R, 256 if n == 1 else 64)

    def body(*refs):
        for k in range(n):
            g = refs[k][...]
            d, nm, nv = _adamw(refs[n + k][...], g, refs[2 * n + k][...], refs[3 * n + k][...])
            out = refs[4 * n + 4 * k:4 * n + 4 * k + 4]
            out[0][...] = g
            out[1][...] = d
            out[2][...] = nm
            out[3][...] = nv

    spec = pl.BlockSpec((tr, C), lambda i: (i, 0))
    sds = jax.ShapeDtypeStruct((R, C), F32)
    outs = pl.pallas_call(
        body, out_shape=(sds,) * (4 * n), grid=(R // tr,), name=f"adamw_{tag}",
        in_specs=[spec] * (4 * n), out_specs=(spec,) * (4 * n), compiler_params=_cp("parallel"),
    )(*gs, *_in_hbm(*ws, *ms, *vs))
    return [outs[4 * k:4 * k + 4] for k in range(n)]


def _adamw_w_ada(c_t, dmod_cols, w, m, v):
    R, C = w.shape

    def body(ct_ref, dm_ref, w_ref, m_ref, v_ref, g_ref, d_ref, nm_ref, nv_ref):
        g = _dot(ct_ref[...].astype(BF16), dm_ref[...].astype(BF16))
        d, nm, nv = _adamw(w_ref[...], g, m_ref[...], v_ref[...])
        g_ref[...] = g
        d_ref[...] = d
        nm_ref[...] = nm
        nv_ref[...] = nv

    tr = 256
    spec = pl.BlockSpec((tr, C), lambda i: (i, 0))
    sds = jax.ShapeDtypeStruct((R, C), F32)
    return pl.pallas_call(
        body, out_shape=(sds,) * 4, grid=(R // tr,), name="adamw_w_ada",
        in_specs=[pl.BlockSpec((tr, 128), lambda i: (i, 0)), pl.BlockSpec((128, C), lambda i: (0, 0))] + [spec] * 3,
        out_specs=(spec,) * 4, compiler_params=_cp("parallel"),
    )(c_t, dmod_cols, w, m, v)


def _adamw_small(small_all, ws, ms, vs):
    def body(s_ref, w_ref, m_ref, v_ref, g_ref, d_ref, nm_ref, nv_ref):
        g = s_ref[0]
        for b in range(1, N_DEV):
            g = g + s_ref[b]
        d, nm, nv = _adamw(w_ref[...], g, m_ref[...], v_ref[...])
        g_ref[...] = g
        d_ref[...] = d
        nm_ref[...] = nm
        nv_ref[...] = nv

    sds = jax.ShapeDtypeStruct((SMALL_ROWS, D_MODEL), F32)
    return pl.pallas_call(
        body, out_shape=(sds,) * 4, name="adamw_small", in_specs=[VMEM_SPEC] * 4, out_specs=(VMEM_SPEC,) * 4,
        compiler_params=pltpu.CompilerParams(vmem_limit_bytes=VMEM_LIMIT_V7X),
    )(small_all, ws, ms, vs)


ROW_MOD, ROW_NORM_G, ROW_CONV_B, ROW_BA, ROW_BX, ROW_LAM, ROW_FINAL_G, ROW_SINKS, ROW_CONV_W, ROW_LOSS = 0, 3, 4, 5, 6, 7, 8, 9, 10, 14


def _pack_small(b_ada, norm_g, conv_b, ba, bx, lam, final_g, sinks, conv_w_full, loss_row=None):
    lane_pad = lambda a: jnp.pad(a.reshape(1, -1), ((0, 0), (0, D_MODEL - a.size)))
    rows = [b_ada.reshape(3, D_MODEL), norm_g, conv_b, ba, bx, lam, final_g.reshape(1, D_MODEL), lane_pad(sinks), conv_w_full,
            jnp.zeros((1, D_MODEL), F32) if loss_row is None else lane_pad(loss_row),
            jnp.zeros((SMALL_ROWS - ROW_LOSS - 1, D_MODEL), F32)]
    return jnp.concatenate([r.astype(F32) for r in rows], axis=0)


def kernel(x, c, positions, w_ada, b_ada, norm_g, w_in, attn_sinks, conv_w, conv_b, rg_wa, rg_ba, rg_wx, rg_bx, rg_lambda, w_attn_proj, w_rnn_proj, w_out, final_g, loss_target, m_w_ada, m_b_ada, m_norm_g, m_w_in, m_attn_sinks, m_conv_w, m_conv_b, m_rg_wa, m_rg_ba, m_rg_wx, m_rg_bx, m_rg_lambda, m_w_attn_proj, m_w_rnn_proj, m_w_out, m_final_g, v_w_ada, v_b_ada, v_norm_g, v_w_in, v_attn_sinks, v_conv_w, v_conv_b, v_rg_wa, v_rg_ba, v_rg_wx, v_rg_bx, v_rg_lambda, v_w_attn_proj, v_w_rnn_proj, v_w_out, v_final_g):
    T = x.shape[1]
    my_chip = lax.axis_index("x") * 2 + lax.axis_index("y")
    my_dev = my_chip * 2 + lax.axis_index("c")
    x2d, tgt = x[0], loss_target[0]
    pos_col = positions.reshape(T, 1)

    chip_idx = my_chip.reshape(1).astype(jnp.int32)
    c_idx = lax.axis_index("c").reshape(1).astype(jnp.int32)
    sq_place = ((D_MODEL, D_MODEL), (SHARD_ROWS, D_MODEL), lambda chip: (chip, 0))
    rg_place = ((RNN_BLOCKS, RNN_BW, RNN_BW), (RNN_BLOCKS, SHARD_RG, RNN_BW), lambda chip: (0, chip, 0))
    in_place = ((D_MODEL, IN_W), (D_MODEL, SHARD_IN), lambda chip: (0, chip))
    placed = _cast_place([w_in[0], w_attn_proj[0], w_rnn_proj[0], w_out[0], rg_wa[0], rg_wx[0]], chip_idx,
                         [in_place, sq_place, sq_place, sq_place, rg_place, rg_place])
    cw_chips, c_all, mod_chips = _gather_mod(c.reshape(1, 1, D_MODEL), w_ada[0], conv_w[0])
    g_ssems, g_rsems, fulls, g_token = _gather_start([p.reshape(s) for p, s in zip(placed, FULL_SHAPES)], mod_chips)
    conv_w_f = jnp.transpose(cw_chips, (1, 0, 2)).reshape(CONV_W, D_MODEL)
    mod_all = jnp.transpose(mod_chips, (1, 0, 2)).reshape(N_DEV, ADA_W) + b_ada
    mod_row = lax.dynamic_slice_in_dim(mod_all, my_dev, 1, axis=0) + g_token[0:1, 0:1]

    h, h_t, tabs = _prenorm(x2d, mod_row, norm_g, pos_col)
    w_in_v = fulls[0]
    proj = _in_projection(h, w_in_v.reshape(D_MODEL, IN_W), chip_idx, None, "own")
    for k, mask in enumerate(CHIP_MASKS):
        w_in_v = _gather_wait(g_ssems[k], g_rsems[k], [w_in_v], [0], proj, f"w_in_{k}")[0]
        w_in_v = _forward_halves([w_in_v], [(0, 0, k)], f"w_in_{k}")[0]
        from_chip = (chip_idx ^ (mask >> 1)).astype(jnp.int32)
        proj = _in_projection(h, w_in_v.reshape(D_MODEL, IN_W), from_chip, proj, f"from_{k}")
    w_in_f = w_in_v.reshape(D_MODEL, IN_W)
    rest = _gather_wait(g_ssems[3], g_rsems[3], list(fulls[1:]), [1, 2, 3, 4, 5], proj, "rest")
    fr_ssem, fr_rsem, rest, fr_token = _forward_rest_start(rest)
    y_attn, qr_b, kr_b, p_all = _attn_forward(proj, tabs, attn_sinks + fr_token[0, 0])
    rest = _gather_wait(fr_ssem, fr_rsem, rest, [1, 2, 3, 4, 5], y_attn, "rest_forwarded")
    wap_f, wrp_f, wo_f = (g.reshape(D_MODEL, D_MODEL) for g in rest[0:3])
    rwa_f, rwx_f = (g.reshape(RNN_BLOCKS, RNN_BW, RNN_BW) for g in rest[3:5])
    y_rnn, h_rnn, *rnn_saved = _rnn_forward(proj, pos_col, conv_w_f, conv_b, rwa_f, rwx_f, rg_ba, rg_bx, rg_lambda)
    (dx2, merged, d_o, d_pa, d_pr, d_ya, d_yr, d_c, d_final_g, d_gate, loss_vec) = _merge_and_head(
        x2d, tgt, y_attn, y_rnn, proj, wap_f, wrp_f, wo_f, mod_row, final_g.reshape(1, D_MODEL))

    sq = (N_CHIPS, 2, SHARD_ROWS // 2, D_MODEL)
    rg = (RNN_BLOCKS, N_CHIPS, 2, SHARD_RG // 2, RNN_BW)
    rg_flat = (RNN_BLOCKS * N_CHIPS, 2, SHARD_RG // 2, RNN_BW)

    where = jnp.concatenate([chip_idx, c_idx])

    def chip_sum_and_start(views, axes, flat, unflat, tags_, kinds_, group, from_sib=None):
        if from_sib is None:
            from_sib = _swap_halves(views, axes)
        exact, rounded = [None] * len(views), [None] * len(views)
        for shape in dict.fromkeys(flat):
            ids = [k for k, f in enumerate(flat) if f == shape]
            own_only = kinds_[ids[0]] == "in"
            ex, ro = _presum([views[k].reshape(shape) for k in ids],
                             [from_sib[k].reshape(shape[:1] + shape[2:]) for k in ids], c_idx, tags_[ids[0]],
                             where if own_only else None)
            for k, e, r in zip(ids, ex, ro):
                exact[k], rounded[k] = e.reshape(e.shape[1:] if own_only else unflat[k]), r.reshape(unflat[k])
        return _exchange_start(rounded, kinds_, group), exact

    g_ap = _weight_grad(y_attn, [(d_pa, 0, 2)], "w_attn_proj")
    g_rp = _weight_grad(y_rnn, [(d_pr, 0, 2)], "w_rnn_proj")
    g_o = _weight_grad(merged, [(d_o, 0, 2)], "w_out")
    sq_half = (N_CHIPS, SHARD_ROWS // 2, D_MODEL)
    views1 = [g_ap.reshape(sq), g_rp.reshape(sq), g_o.reshape(sq)]
    sw_ssems, sw_rsems, views1, sib1, sw_token = _swap_halves_start(views1, [1, 1, 1], "proj")
    d_q, d_kv, d_ga, d_sinks = _attn_backward(proj, qr_b, kr_b, p_all, d_ya, tabs, sw_token[0:1, 0:16])
    views1, sib1 = _swap_halves_wait(sw_ssems, sw_rsems, views1, sib1, d_q, "proj")
    started1, own1 = chip_sum_and_start(views1, [1, 1, 1], [sq] * 3, [sq_half] * 3,
                                        ["w_attn_proj", "w_rnn_proj", "w_out"], ["sq"] * 3, "proj", from_sib=sib1)
    d_b, d_conv_w, d_conv_b, d_rwa, d_rwx, d_ba, d_bx, d_lam = _rnn_backward(
        proj, pos_col, h_rnn, rnn_saved, d_yr, conv_w_f, rwa_f, rwx_f, rg_lambda + started1[4][0:1, 0:1])
    pieces = [(d_q, CB_Q, 2), (d_kv, CB_KV, 1), (d_ga, CB_GA, 2), (d_b, CB_XR, 4), (d_c, CB_MA, 4)]
    g_in = _weight_grad(h_t, pieces, "w_in", a_is_transposed=True)
    started2, own2 = chip_sum_and_start(
        [g_in.reshape(2, D_MODEL // 2, IN_W), d_rwa.reshape(rg), d_rwx.reshape(rg)], [0, 2, 2],
        [(1, 2, D_MODEL // 2, IN_W), rg_flat, rg_flat],
        [(D_MODEL // 2, IN_W), (RNN_BLOCKS, N_CHIPS, SHARD_RG // 2, RNN_BW), (RNN_BLOCKS, N_CHIPS, SHARD_RG // 2, RNN_BW)],
        ["w_in", "rg_wa", "rg_wx"], ["in", "rg", "rg"], "in")
    grad_x, d_shift, d_scale, d_norm_g = _input_backward(pieces, w_in_f, x2d, dx2, mod_row + started2[4][0, 0], norm_g)

    d_mod = jnp.concatenate([d_shift, d_scale, d_gate], axis=1)
    small = _pack_small(d_mod, d_norm_g, d_conv_b, d_ba, d_bx, d_lam, d_final_g, d_sinks[:, :N_HEADS], d_conv_w, loss_vec)
    slabs = lax.dynamic_update_slice(jnp.zeros((N_DEV, SMALL_ROWS, D_MODEL), F32), small[None], (my_dev, 0, 0))
    gs_ssem, gs_rsem, slabs, gs_token = _gather_small_start(slabs)
    _, lands1 = _exchange_wait(*started1[:4], gs_token, "proj")
    _, lands2 = _exchange_wait(*started2[:4], gs_token, "in")
    tags = ["w_in", "w_attn_proj", "w_rnn_proj", "w_out", "rg_wa", "rg_wx"]
    chip_sums = [own2[0]] + list(own1) + list(own2[1:])
    lands = [lands2[0]] + list(lands1) + list(lands2[1:])
    kinds = ["in", "sq", "sq", "sq", "rg", "rg"]
    groups = [[0], [1, 2, 3], [4, 5]]
    halves = [None] * 6
    for ids in groups:
        for i, half in zip(ids, _sum_landed(kinds[ids[0]], [chip_sums[i] for i in ids], [lands[i] for i in ids], where,
                                            tags[ids[0]])):
            halves[i] = half
    half_axes = [0, 0, 0, 0, 1, 1]
    asm_ssems, asm_rsems, halves, asm_token = _assemble_start(halves, half_axes)
    res = {}
    small_all = _gather_small_wait(gs_ssem, gs_rsem, slabs, asm_token)
    dmod_all = small_all[:, ROW_MOD:ROW_MOD + 3, :].reshape(N_DEV, ADA_W)
    dmod_cols = lax.dynamic_slice_in_dim(dmod_all, my_chip * SHARD_ADA, SHARD_ADA, axis=1)
    c_t = jnp.pad(jnp.transpose(c_all.reshape(N_DEV, D_MODEL)), ((0, 0), (0, 128 - N_DEV)))
    dmod_cols = jnp.pad(dmod_cols, ((0, 128 - N_DEV), (0, 0)))
    res["w_ada"] = [o.reshape(w_ada.shape) for o in _adamw_w_ada(c_t, dmod_cols, w_ada[0], m_w_ada[0], v_w_ada[0])]

    def full_conv(a):
        return lax.dynamic_update_slice_in_dim(jnp.zeros((CONV_W, D_MODEL), F32), a[0], my_chip * (D_MODEL // N_CHIPS), axis=1)

    packed = [_pack_small(p[0], p[1], p[2], p[3], p[4], p[5], p[6], p[7], full_conv(p[8])) for p in (
        (b_ada, norm_g, conv_b, rg_ba, rg_bx, rg_lambda, final_g, attn_sinks, conv_w),
        (m_b_ada, m_norm_g, m_conv_b, m_rg_ba, m_rg_bx, m_rg_lambda, m_final_g, m_attn_sinks, m_conv_w),
        (v_b_ada, v_norm_g, v_conv_b, v_rg_ba, v_rg_bx, v_rg_lambda, v_final_g, v_attn_sinks, v_conv_w))]
    small_out = _adamw_small(small_all, *packed)

    grads = _assemble_wait(asm_ssems, asm_rsems, halves, half_axes, small_out[0])
    shapes2d = [(D_MODEL, SHARD_IN), (SHARD_ROWS, D_MODEL), (SHARD_ROWS, D_MODEL), (SHARD_ROWS, D_MODEL),
                (RNN_BLOCKS * SHARD_RG, RNN_BW), (RNN_BLOCKS * SHARD_RG, RNN_BW)]
    big_w = [w_in, w_attn_proj, w_rnn_proj, w_out, rg_wa, rg_wx]
    big_m = [m_w_in, m_w_attn_proj, m_w_rnn_proj, m_w_out, m_rg_wa, m_rg_wx]
    big_v = [v_w_in, v_w_attn_proj, v_w_rnn_proj, v_w_out, v_rg_wa, v_rg_wx]
    for ids in groups:
        flat2d = lambda arrs: [arrs[i].reshape(shapes2d[i]) for i in ids]
        outs = _adamw_shard(flat2d(grads), flat2d(big_w), flat2d(big_m), flat2d(big_v), tags[ids[0]])
        for i, four in zip(ids, outs):
            res[tags[i]] = [o.reshape(big_w[i].shape) for o in four]

    def unpack(slab):
        cw = lax.dynamic_slice_in_dim(slab[ROW_CONV_W:ROW_CONV_W + CONV_W], my_chip * (D_MODEL // N_CHIPS),
                                      D_MODEL // N_CHIPS, axis=1)
        return {
            "b_ada": slab[ROW_MOD:ROW_MOD + 3].reshape(1, ADA_W), "norm_g": slab[ROW_NORM_G:ROW_NORM_G + 1],
            "conv_b": slab[ROW_CONV_B:ROW_CONV_B + 1], "rg_ba": slab[ROW_BA:ROW_BA + 1], "rg_bx": slab[ROW_BX:ROW_BX + 1],
            "rg_lambda": slab[ROW_LAM:ROW_LAM + 1], "final_g": slab[ROW_FINAL_G], "attn_sinks": slab[ROW_SINKS:ROW_SINKS + 1, :N_HEADS],
            "conv_w": cw[None],
        }

    small_res = [unpack(s) for s in small_out]
    order = ["w_ada", "b_ada", "norm_g", "w_in", "attn_sinks", "conv_w", "conv_b", "rg_wa", "rg_ba", "rg_wx", "rg_bx",
             "rg_lambda", "w_attn_proj", "w_rnn_proj", "w_out", "final_g"]
    loss = small_out[0][ROW_LOSS, 0]
    outs = [loss, grad_x[None]]
    for kind in range(4):
        for name in order:
            outs.append(res[name][kind] if name in res else small_res[kind][name])
    return tuple(outs)
```
